```python
import math
import jax, jax.numpy as jnp
from jax import lax
import numpy as np

D_MODEL = 1024
BATCH = 8
SEQ = 4096
DEPTH = 1

POOL_WINDOWS = (2, 4, 8, 16)
POOL_GROUPS = len(POOL_WINDOWS)
POOL_GROUP_DIM = D_MODEL // 8
POOL_WIDTH = POOL_GROUPS * POOL_GROUP_DIM
DN_HEAD_DIM = 128
DN_HEADS = D_MODEL // 128
DN_WIDTH = DN_HEADS * DN_HEAD_DIM
CONV_WIDTH = 4
CHUNK = 64
FFN_HIDDEN = ((8 * D_MODEL // 3) + 127) // 128 * 128
N_SUBLAYERS = 3
RMS_EPS = 1e-6
L2_EPS = 1e-6
MIX_IN_SIZES = (POOL_WIDTH, DN_WIDTH, DN_WIDTH, DN_WIDTH, DN_WIDTH, DN_HEADS, DN_HEADS, D_MODEL, D_MODEL)
MIX_IN_WIDTH = int(sum(MIX_IN_SIZES))
MIX_IN_SPLITS = tuple(int(s) for s in np.cumsum(MIX_IN_SIZES)[:-1])

kernel_name = "hybrid_pool_deltanet_macaron_adaln"


def rms_norm(x, g):
    xf = x.astype(jnp.float32)
    xf = xf * lax.rsqrt(jnp.mean(xf * xf, axis=-1, keepdims=True) + RMS_EPS)
    return (xf * g.astype(jnp.float32)).astype(x.dtype)


def l2_normalize(x):
    return x * lax.rsqrt(jnp.sum(x * x, axis=-1, keepdims=True) + L2_EPS)


def modulate(h, shift, scale):
    return h * (1.0 + scale[:, None, :]) + shift[:, None, :]


def swiglu(h, w_in, w_out):
    gate, up = jnp.split(h @ w_in, 2, axis=-1)
    return (jax.nn.silu(gate) * up) @ w_out


def causal_depthwise_conv(x, w):
    C = x.shape[-1]
    return lax.conv_general_dilated(
        x, w[:, None, :].astype(x.dtype), window_strides=(1,),
        padding=[(CONV_WIDTH - 1, 0)],
        dimension_numbers=('NWC', 'WIO', 'NWC'), feature_group_count=C)


def multiscale_pool(xp):
    T = xp.shape[1]
    xf = xp.astype(jnp.float32)
    cs = jnp.cumsum(xf, axis=1)
    pos = jnp.arange(1, T + 1, dtype=jnp.float32)[:, None]
    outs = []
    for gi, w in enumerate(POOL_WINDOWS):
        sl = slice(gi * POOL_GROUP_DIM, (gi + 1) * POOL_GROUP_DIM)
        c_g = cs[..., sl]
        lagged = jnp.pad(c_g, ((0, 0), (w, 0), (0, 0)))[:, :T]
        mean = (c_g - lagged) / jnp.minimum(pos, float(w))
        outs.append(mean - xf[..., sl])
    return jnp.stack(outs, axis=2).astype(xp.dtype)


def gated_delta_rule_chunked(q, k, v, g, beta):
    B, T, H, K = q.shape
    V = v.shape[-1]
    N = T // CHUNK

    def to_chunks(a):
        a = a.reshape((B, N, CHUNK, H) + a.shape[3:])
        return jnp.moveaxis(a, (1, 3), (0, 2))

    qc, kc, vc, bc = to_chunks(q), to_chunks(k), to_chunks(v), to_chunks(beta)
    gc = jnp.cumsum(to_chunks(g), axis=-1)
    causal = jnp.tril(jnp.ones((CHUNK, CHUNK), dtype=bool))
    strict = jnp.tril(jnp.ones((CHUNK, CHUNK), dtype=bool), -1)
    decay = jnp.exp(jnp.where(causal, gc[..., :, None] - gc[..., None, :], -jnp.inf))
    kk = jnp.einsum('nbhik,nbhjk->nbhij', kc, kc)
    a_mat = jnp.where(strict, bc[..., :, None] * kk * decay, 0.0)
    eye = jnp.eye(CHUNK, dtype=q.dtype)
    rhs = jnp.concatenate([bc[..., None] * vc, (bc * jnp.exp(gc))[..., None] * kc], axis=-1)
    sol = lax.linalg.triangular_solve(a_mat + eye, rhs, left_side=True, lower=True,
                                      unit_diagonal=True)
    u, w = sol[..., :V], sol[..., V:]
    qk = jnp.where(causal, jnp.einsum('nbhik,nbhjk->nbhij', qc, kc) * decay, 0.0)
    q_dec = qc * jnp.exp(gc)[..., None]
    g_last = gc[..., -1]
    k_dec = kc * jnp.exp(g_last[..., None] - gc)[..., None]

    def step(S, xs):
        q_i, qk_i, u_i, w_i, k_i, gl_i = xs
        v_new = u_i - jnp.einsum('bhck,bhkv->bhcv', w_i, S)
        o_i = jnp.einsum('bhck,bhkv->bhcv', q_i, S) + jnp.einsum('bhij,bhjv->bhiv', qk_i, v_new)
        S = S * jnp.exp(gl_i)[..., None, None] + jnp.einsum('bhck,bhcv->bhkv', k_i, v_new)
        return S, o_i

    S0 = jnp.zeros((B, H, K, V), dtype=q.dtype)
    _, o = lax.scan(step, S0, (q_dec, qk, u, w, k_dec, g_last))
    return jnp.moveaxis(o, (0, 2), (1, 3)).reshape(B, T, H, V)


def token_mixer(h, mix_w_in, conv_w, a_log, dt_bias, dn_norm_g, pool_w, pool_scale,
                pool_proj, dn_proj, mix_w_out):
    B, T, _ = h.shape
    proj = h @ mix_w_in
    xp, q, k, v, z, b_raw, a_raw, g_pool, g_dn = jnp.split(proj, MIX_IN_SPLITS, axis=-1)

    pooled = multiscale_pool(xp)
    ya = jnp.einsum('btgc,gcd->btgd', pooled, pool_w).reshape(B, T, POOL_WIDTH) * pool_scale
    ya = ya @ pool_proj

    qkv = jax.nn.silu(causal_depthwise_conv(jnp.concatenate([q, k, v], axis=-1), conv_w))
    qkv = qkv.astype(jnp.float32).reshape(B, T, 3, DN_HEADS, DN_HEAD_DIM)
    qh = l2_normalize(qkv[:, :, 0]) * (DN_HEAD_DIM ** -0.5)
    kh = l2_normalize(qkv[:, :, 1])
    vh = qkv[:, :, 2]
    beta = jax.nn.sigmoid(b_raw.astype(jnp.float32))
    g = -jnp.exp(a_log.astype(jnp.float32)) * jax.nn.softplus(
        a_raw.astype(jnp.float32) + dt_bias.astype(jnp.float32))
    o = gated_delta_rule_chunked(qh, kh, vh, g, beta)
    o = rms_norm(o, dn_norm_g).astype(h.dtype).reshape(B, T, DN_WIDTH) * jax.nn.silu(z)
    yb = o @ dn_proj

    merged = jax.nn.sigmoid(g_pool) * ya + jax.nn.sigmoid(g_dn) * yb
    return merged @ mix_w_out


def _fwd_setup_inputs(seed: int = 0) -> dict:
    key = jax.random.key(seed)
    ks = jax.random.split(key, 20)
    D, L, F = D_MODEL, DEPTH, FFN_HIDDEN

    def nrm(k, shape, fan_in):
        return jax.random.normal(k, shape, jnp.float32) * fan_in ** -0.5

    x = jax.random.normal(ks[0], (BATCH, SEQ, D), jnp.float32)
    c = jax.random.normal(ks[1], (BATCH, D), jnp.float32)
    ada_w = 0.5 * nrm(ks[2], (L, D, N_SUBLAYERS * 3 * D), D)
    ada_b = 0.01 * jax.random.normal(ks[3], (L, N_SUBLAYERS * 3 * D), jnp.float32)
    norm_g = 1.0 + 0.05 * jax.random.normal(ks[4], (L, N_SUBLAYERS, D), jnp.float32)
    ffn1_w_in = nrm(ks[5], (L, D, 2 * F), D)
    ffn1_w_out = nrm(ks[6], (L, F, D), F)
    ffn2_w_in = nrm(ks[7], (L, D, 2 * F), D)
    ffn2_w_out = nrm(ks[8], (L, F, D), F)
    mix_w_in = nrm(ks[9], (L, D, MIX_IN_WIDTH), D)
    conv_w = nrm(ks[10], (L, CONV_WIDTH, 3 * DN_WIDTH), CONV_WIDTH)
    a_log = jnp.log(jax.random.uniform(ks[11], (L, DN_HEADS), jnp.float32, minval=1.0, maxval=16.0))
    dt = jnp.exp(jax.random.uniform(ks[12], (L, DN_HEADS), jnp.float32,
                                    minval=math.log(1e-3), maxval=math.log(1e-1)))
    dt_bias = dt + jnp.log(-jnp.expm1(-dt))
    dn_norm_g = 1.0 + 0.05 * jax.random.normal(ks[13], (L, DN_HEAD_DIM), jnp.float32)
    pool_w = nrm(ks[14], (L, POOL_GROUPS, POOL_GROUP_DIM, POOL_GROUP_DIM), POOL_GROUP_DIM)
    pool_scale = 1.0 + 0.1 * jax.random.normal(ks[15], (L, POOL_WIDTH), jnp.float32)
    pool_proj = nrm(ks[16], (L, POOL_WIDTH, D), POOL_WIDTH)
    dn_proj = nrm(ks[17], (L, DN_WIDTH, D), DN_WIDTH)
    mix_w_out = nrm(ks[18], (L, D, D), D)
    final_g = 1.0 + 0.05 * jax.random.normal(ks[19], (D,), jnp.float32)
    return {"x": x, "c": c, "ada_w": ada_w, "ada_b": ada_b, "norm_g": norm_g,
            "ffn1_w_in": ffn1_w_in, "ffn1_w_out": ffn1_w_out,
            "ffn2_w_in": ffn2_w_in, "ffn2_w_out": ffn2_w_out,
            "mix_w_in": mix_w_in, "conv_w": conv_w, "a_log": a_log, "dt_bias": dt_bias,
            "dn_norm_g": dn_norm_g, "pool_w": pool_w, "pool_scale": pool_scale,
            "pool_proj": pool_proj, "dn_proj": dn_proj, "mix_w_out": mix_w_out,
            "final_g": final_g}


def _fwd_reference(x, c, ada_w, ada_b, norm_g, ffn1_w_in, ffn1_w_out, ffn2_w_in, ffn2_w_out,
              mix_w_in, conv_w, a_log, dt_bias, dn_norm_g, pool_w, pool_scale, pool_proj,
              dn_proj, mix_w_out, final_g):
    B = x.shape[0]
    for l in range(DEPTH):
        mod = (jax.nn.silu(c) @ ada_w[l] + ada_b[l]).reshape(B, N_SUBLAYERS, 3, D_MODEL)
        shift, scale, gate = mod[:, :, 0], mod[:, :, 1], mod[:, :, 2]

        h = modulate(rms_norm(x, norm_g[l, 0]), shift[:, 0], scale[:, 0])
        x = x + 0.5 * gate[:, 0, None, :] * swiglu(h, ffn1_w_in[l], ffn1_w_out[l])

        h = modulate(rms_norm(x, norm_g[l, 1]), shift[:, 1], scale[:, 1])
        x = x + gate[:, 1, None, :] * token_mixer(
            h, mix_w_in[l], conv_w[l], a_log[l], dt_bias[l], dn_norm_g[l], pool_w[l],
            pool_scale[l], pool_proj[l], dn_proj[l], mix_w_out[l])

        h = modulate(rms_norm(x, norm_g[l, 2]), shift[:, 2], scale[:, 2])
        x = x + 0.5 * gate[:, 2, None, :] * swiglu(h, ffn2_w_in[l], ffn2_w_out[l])
    return rms_norm(x, final_g)


import jax as _jax
import jax.numpy as _jnp

TWIN_FORMAT = 'train_step'
FWD_PARAMS = ['x', 'c', 'ada_w', 'ada_b', 'norm_g', 'ffn1_w_in', 'ffn1_w_out', 'ffn2_w_in', 'ffn2_w_out', 'mix_w_in', 'conv_w', 'a_log', 'dt_bias', 'dn_norm_g', 'pool_w', 'pool_scale', 'pool_proj', 'dn_proj', 'mix_w_out', 'final_g']
TWIN_WEIGHTS = ['ada_w', 'ada_b', 'norm_g', 'ffn1_w_in', 'ffn1_w_out', 'ffn2_w_in', 'ffn2_w_out', 'mix_w_in', 'conv_w', 'a_log', 'dt_bias', 'dn_norm_g', 'pool_w', 'pool_scale', 'pool_proj', 'dn_proj', 'mix_w_out', 'final_g']
TWIN_DIFF_INPUT = 'x'
TWIN_INPUTS = ['x', 'c', 'ada_w', 'ada_b', 'norm_g', 'ffn1_w_in', 'ffn1_w_out', 'ffn2_w_in', 'ffn2_w_out', 'mix_w_in', 'conv_w', 'a_log', 'dt_bias', 'dn_norm_g', 'pool_w', 'pool_scale', 'pool_proj', 'dn_proj', 'mix_w_out', 'final_g', 'loss_target', 'm_ada_w', 'm_ada_b', 'm_norm_g', 'm_ffn1_w_in', 'm_ffn1_w_out', 'm_ffn2_w_in', 'm_ffn2_w_out', 'm_mix_w_in', 'm_conv_w', 'm_a_log', 'm_dt_bias', 'm_dn_norm_g', 'm_pool_w', 'm_pool_scale', 'm_pool_proj', 'm_dn_proj', 'm_mix_w_out', 'm_final_g', 'v_ada_w', 'v_ada_b', 'v_norm_g', 'v_ffn1_w_in', 'v_ffn1_w_out', 'v_ffn2_w_in', 'v_ffn2_w_out', 'v_mix_w_in', 'v_conv_w', 'v_a_log', 'v_dt_bias', 'v_dn_norm_g', 'v_pool_w', 'v_pool_scale', 'v_pool_proj', 'v_dn_proj', 'v_mix_w_out', 'v_final_g']
TWIN_OUTPUTS = ['loss', 'grad_x', 'grad_ada_w', 'grad_ada_b', 'grad_norm_g', 'grad_ffn1_w_in', 'grad_ffn1_w_out', 'grad_ffn2_w_in', 'grad_ffn2_w_out', 'grad_mix_w_in', 'grad_conv_w', 'grad_a_log', 'grad_dt_bias', 'grad_dn_norm_g', 'grad_pool_w', 'grad_pool_scale', 'grad_pool_proj', 'grad_dn_proj', 'grad_mix_w_out', 'grad_final_g', 'delta_ada_w', 'delta_ada_b', 'delta_norm_g', 'delta_ffn1_w_in', 'delta_ffn1_w_out', 'delta_ffn2_w_in', 'delta_ffn2_w_out', 'delta_mix_w_in', 'delta_conv_w', 'delta_a_log', 'delta_dt_bias', 'delta_dn_norm_g', 'delta_pool_w', 'delta_pool_scale', 'delta_pool_proj', 'delta_dn_proj', 'delta_mix_w_out', 'delta_final_g', 'new_m_ada_w', 'new_m_ada_b', 'new_m_norm_g', 'new_m_ffn1_w_in', 'new_m_ffn1_w_out', 'new_m_ffn2_w_in', 'new_m_ffn2_w_out', 'new_m_mix_w_in', 'new_m_conv_w', 'new_m_a_log', 'new_m_dt_bias', 'new_m_dn_norm_g', 'new_m_pool_w', 'new_m_pool_scale', 'new_m_pool_proj', 'new_m_dn_proj', 'new_m_mix_w_out', 'new_m_final_g', 'new_v_ada_w', 'new_v_ada_b', 'new_v_norm_g', 'new_v_ffn1_w_in', 'new_v_ffn1_w_out', 'new_v_ffn2_w_in', 'new_v_ffn2_w_out', 'new_v_mix_w_in', 'new_v_conv_w', 'new_v_a_log', 'new_v_dt_bias', 'new_v_dn_norm_g', 'new_v_pool_w', 'new_v_pool_scale', 'new_v_pool_proj', 'new_v_dn_proj', 'new_v_mix_w_out', 'new_v_final_g']
TWIN_LEAF_KINDS = {'loss': 'loss', 'grad_x': 'grad_x', 'grad_ada_w': 'grad_w', 'grad_ada_b': 'grad_w', 'grad_norm_g': 'grad_w', 'grad_ffn1_w_in': 'grad_w', 'grad_ffn1_w_out': 'grad_w', 'grad_ffn2_w_in': 'grad_w', 'grad_ffn2_w_out': 'grad_w', 'grad_mix_w_in': 'grad_w', 'grad_conv_w': 'grad_w', 'grad_a_log': 'grad_w', 'grad_dt_bias': 'grad_w', 'grad_dn_norm_g': 'grad_w', 'grad_pool_w': 'grad_w', 'grad_pool_scale': 'grad_w', 'grad_pool_proj': 'grad_w', 'grad_dn_proj': 'grad_w', 'grad_mix_w_out': 'grad_w', 'grad_final_g': 'grad_w', 'delta_ada_w': 'delta_w', 'delta_ada_b': 'delta_w', 'delta_norm_g': 'delta_w', 'delta_ffn1_w_in': 'delta_w', 'delta_ffn1_w_out': 'delta_w', 'delta_ffn2_w_in': 'delta_w', 'delta_ffn2_w_out': 'delta_w', 'delta_mix_w_in': 'delta_w', 'delta_conv_w': 'delta_w', 'delta_a_log': 'delta_w', 'delta_dt_bias': 'delta_w', 'delta_dn_norm_g': 'delta_w', 'delta_pool_w': 'delta_w', 'delta_pool_scale': 'delta_w', 'delta_pool_proj': 'delta_w', 'delta_dn_proj': 'delta_w', 'delta_mix_w_out': 'delta_w', 'delta_final_g': 'delta_w', 'new_m_ada_w': 'new_m', 'new_m_ada_b': 'new_m', 'new_m_norm_g': 'new_m', 'new_m_ffn1_w_in': 'new_m', 'new_m_ffn1_w_out': 'new_m', 'new_m_ffn2_w_in': 'new_m', 'new_m_ffn2_w_out': 'new_m', 'new_m_mix_w_in': 'new_m', 'new_m_conv_w': 'new_m', 'new_m_a_log': 'new_m', 'new_m_dt_bias': 'new_m', 'new_m_dn_norm_g': 'new_m', 'new_m_pool_w': 'new_m', 'new_m_pool_scale': 'new_m', 'new_m_pool_proj': 'new_m', 'new_m_dn_proj': 'new_m', 'new_m_mix_w_out': 'new_m', 'new_m_final_g': 'new_m', 'new_v_ada_w': 'new_v', 'new_v_ada_b': 'new_v', 'new_v_norm_g': 'new_v', 'new_v_ffn1_w_in': 'new_v', 'new_v_ffn1_w_out': 'new_v', 'new_v_ffn2_w_in': 'new_v', 'new_v_ffn2_w_out': 'new_v', 'new_v_mix_w_in': 'new_v', 'new_v_conv_w': 'new_v', 'new_v_a_log': 'new_v', 'new_v_dt_bias': 'new_v', 'new_v_dn_norm_g': 'new_v', 'new_v_pool_w': 'new_v', 'new_v_pool_scale': 'new_v', 'new_v_pool_proj': 'new_v', 'new_v_dn_proj': 'new_v', 'new_v_mix_w_out': 'new_v', 'new_v_final_g': 'new_v'}


def _forward(args):
    return _fwd_reference(*[args[k] for k in FWD_PARAMS])


def _output_shape():
    def fwd():
        inp = _fwd_setup_inputs(0)
        return _fwd_reference(*[inp[k] for k in FWD_PARAMS])
    out = _jax.eval_shape(fwd)
    return out.shape, out.dtype

N_MICROBATCH = 1
ADAM_LR = 0.001
ADAM_B1 = 0.9
ADAM_B2 = 0.999
ADAM_EPS = 1e-08
ADAM_WD = 0.01
ADAM_STEP = 10
PER_EXAMPLE_BATCH_AXIS = {'x': 0, 'c': 0, 'loss_target': 0}
SHARED_INPUTS = []
_WEIGHT_DTYPES = {'ada_w': _jnp.float32, 'ada_b': _jnp.float32, 'norm_g': _jnp.float32, 'ffn1_w_in': _jnp.float32, 'ffn1_w_out': _jnp.float32, 'ffn2_w_in': _jnp.float32, 'ffn2_w_out': _jnp.float32, 'mix_w_in': _jnp.float32, 'conv_w': _jnp.float32, 'a_log': _jnp.float32, 'dt_bias': _jnp.float32, 'dn_norm_g': _jnp.float32, 'pool_w': _jnp.float32, 'pool_scale': _jnp.float32, 'pool_proj': _jnp.float32, 'dn_proj': _jnp.float32, 'mix_w_out': _jnp.float32, 'final_g': _jnp.float32}
MOMENT_SCALE = {'ada_w': 3.722196e-02, 'ada_b': 7.005611e-02, 'norm_g': 3.243965e-02, 'ffn1_w_in': 1.161742e-02, 'ffn1_w_out': 1.900084e-02, 'ffn2_w_in': 1.141111e-02, 'ffn2_w_out': 1.858867e-02, 'mix_w_in': 1.708948e-02, 'conv_w': 1.435237e-02, 'a_log': 8.994771e-02, 'dt_bias': 8.787254e-02, 'dn_norm_g': 6.301129e-02, 'pool_w': 3.881827e-02, 'pool_scale': 3.715563e-02, 'pool_proj': 2.738474e-02, 'dn_proj': 1.931722e-02, 'mix_w_out': 3.337082e-02, 'final_g': 3.201878e+01}


def _to_microbatches(a, axis):
    t = _jnp.moveaxis(a, axis, 0)
    t = t.reshape((N_MICROBATCH, t.shape[0] // N_MICROBATCH) + t.shape[1:])
    return _jnp.moveaxis(t, 1, axis + 1)


def setup_inputs(seed: int = 0) -> dict:
    inp = _fwd_setup_inputs(seed)
    key = _jax.random.fold_in(_jax.random.key(seed), 7919)
    shape, _ = _output_shape()
    out = dict(inp)
    out["loss_target"] = _jax.random.normal(_jax.random.fold_in(key, 0), shape, _jnp.float32)
    for i, name in enumerate(TWIN_WEIGHTS):
        w = inp[name].astype(_jnp.float32)
        if MOMENT_SCALE is None:
            s = _jnp.sqrt(_jnp.mean(_jnp.square(w)) + 1e-30)
        else:
            s = MOMENT_SCALE[name]
        km, kv = _jax.random.split(_jax.random.fold_in(key, i + 1))
        out[name] = w
        out["m_" + name] = s * _jax.random.normal(km, w.shape, _jnp.float32)
        out["v_" + name] = (s * s) * _jax.random.uniform(kv, w.shape, _jnp.float32, 0.5, 1.5)
    if N_MICROBATCH > 1:
        for name, axis in PER_EXAMPLE_BATCH_AXIS.items():
            out[name] = _to_microbatches(out[name], axis)
    return {'x': out['x'], 'c': out['c'], 'ada_w': out['ada_w'], 'ada_b': out['ada_b'], 'norm_g': out['norm_g'], 'ffn1_w_in': out['ffn1_w_in'], 'ffn1_w_out': out['ffn1_w_out'], 'ffn2_w_in': out['ffn2_w_in'], 'ffn2_w_out': out['ffn2_w_out'], 'mix_w_in': out['mix_w_in'], 'conv_w': out['conv_w'], 'a_log': out['a_log'], 'dt_bias': out['dt_bias'], 'dn_norm_g': out['dn_norm_g'], 'pool_w': out['pool_w'], 'pool_scale': out['pool_scale'], 'pool_proj': out['pool_proj'], 'dn_proj': out['dn_proj'], 'mix_w_out': out['mix_w_out'], 'final_g': out['final_g'], 'loss_target': out['loss_target'], 'm_ada_w': out['m_ada_w'], 'm_ada_b': out['m_ada_b'], 'm_norm_g': out['m_norm_g'], 'm_ffn1_w_in': out['m_ffn1_w_in'], 'm_ffn1_w_out': out['m_ffn1_w_out'], 'm_ffn2_w_in': out['m_ffn2_w_in'], 'm_ffn2_w_out': out['m_ffn2_w_out'], 'm_mix_w_in': out['m_mix_w_in'], 'm_conv_w': out['m_conv_w'], 'm_a_log': out['m_a_log'], 'm_dt_bias': out['m_dt_bias'], 'm_dn_norm_g': out['m_dn_norm_g'], 'm_pool_w': out['m_pool_w'], 'm_pool_scale': out['m_pool_scale'], 'm_pool_proj': out['m_pool_proj'], 'm_dn_proj': out['m_dn_proj'], 'm_mix_w_out': out['m_mix_w_out'], 'm_final_g': out['m_final_g'], 'v_ada_w': out['v_ada_w'], 'v_ada_b': out['v_ada_b'], 'v_norm_g': out['v_norm_g'], 'v_ffn1_w_in': out['v_ffn1_w_in'], 'v_ffn1_w_out': out['v_ffn1_w_out'], 'v_ffn2_w_in': out['v_ffn2_w_in'], 'v_ffn2_w_out': out['v_ffn2_w_out'], 'v_mix_w_in': out['v_mix_w_in'], 'v_conv_w': out['v_conv_w'], 'v_a_log': out['v_a_log'], 'v_dt_bias': out['v_dt_bias'], 'v_dn_norm_g': out['v_dn_norm_g'], 'v_pool_w': out['v_pool_w'], 'v_pool_scale': out['v_pool_scale'], 'v_pool_proj': out['v_pool_proj'], 'v_dn_proj': out['v_dn_proj'], 'v_mix_w_out': out['v_mix_w_out'], 'v_final_g': out['v_final_g']}


def _loss(weights, diff, rest, loss_target):
    with _jax.named_scope("forward"):
        args = {**rest, TWIN_DIFF_INPUT: diff, **{k: w.astype(_WEIGHT_DTYPES[k]) for k, w in weights.items()}}
        y = _forward(args)
    with _jax.named_scope("loss_head"):
        err = _jnp.square(y.astype(_jnp.float32) - loss_target)
        return 0.5 * _jnp.sum(_jnp.mean(err, axis=-1)) if err.ndim else 0.5 * err


def _adamw(w, g, m, v):
    m = ADAM_B1 * m + (1.0 - ADAM_B1) * g
    v = ADAM_B2 * v + (1.0 - ADAM_B2) * _jnp.square(g)
    m_hat = m / (1.0 - ADAM_B1 ** ADAM_STEP)
    v_hat = v / (1.0 - ADAM_B2 ** ADAM_STEP)
    delta = -ADAM_LR * (m_hat / (_jnp.sqrt(v_hat) + ADAM_EPS) + ADAM_WD * w)
    return delta, m, v


def reference(x, c, ada_w, ada_b, norm_g, ffn1_w_in, ffn1_w_out, ffn2_w_in, ffn2_w_out, mix_w_in, conv_w, a_log, dt_bias, dn_norm_g, pool_w, pool_scale, pool_proj, dn_proj, mix_w_out, final_g, loss_target, m_ada_w, m_ada_b, m_norm_g, m_ffn1_w_in, m_ffn1_w_out, m_ffn2_w_in, m_ffn2_w_out, m_mix_w_in, m_conv_w, m_a_log, m_dt_bias, m_dn_norm_g, m_pool_w, m_pool_scale, m_pool_proj, m_dn_proj, m_mix_w_out, m_final_g, v_ada_w, v_ada_b, v_norm_g, v_ffn1_w_in, v_ffn1_w_out, v_ffn2_w_in, v_ffn2_w_out, v_mix_w_in, v_conv_w, v_a_log, v_dt_bias, v_dn_norm_g, v_pool_w, v_pool_scale, v_pool_proj, v_dn_proj, v_mix_w_out, v_final_g):
    given = dict(x=x, c=c, ada_w=ada_w, ada_b=ada_b, norm_g=norm_g, ffn1_w_in=ffn1_w_in, ffn1_w_out=ffn1_w_out, ffn2_w_in=ffn2_w_in, ffn2_w_out=ffn2_w_out, mix_w_in=mix_w_in, conv_w=conv_w, a_log=a_log, dt_bias=dt_bias, dn_norm_g=dn_norm_g, pool_w=pool_w, pool_scale=pool_scale, pool_proj=pool_proj, dn_proj=dn_proj, mix_w_out=mix_w_out, final_g=final_g, loss_target=loss_target, m_ada_w=m_ada_w, m_ada_b=m_ada_b, m_norm_g=m_norm_g, m_ffn1_w_in=m_ffn1_w_in, m_ffn1_w_out=m_ffn1_w_out, m_ffn2_w_in=m_ffn2_w_in, m_ffn2_w_out=m_ffn2_w_out, m_mix_w_in=m_mix_w_in, m_conv_w=m_conv_w, m_a_log=m_a_log, m_dt_bias=m_dt_bias, m_dn_norm_g=m_dn_norm_g, m_pool_w=m_pool_w, m_pool_scale=m_pool_scale, m_pool_proj=m_pool_proj, m_dn_proj=m_dn_proj, m_mix_w_out=m_mix_w_out, m_final_g=m_final_g, v_ada_w=v_ada_w, v_ada_b=v_ada_b, v_norm_g=v_norm_g, v_ffn1_w_in=v_ffn1_w_in, v_ffn1_w_out=v_ffn1_w_out, v_ffn2_w_in=v_ffn2_w_in, v_ffn2_w_out=v_ffn2_w_out, v_mix_w_in=v_mix_w_in, v_conv_w=v_conv_w, v_a_log=v_a_log, v_dt_bias=v_dt_bias, v_dn_norm_g=v_dn_norm_g, v_pool_w=v_pool_w, v_pool_scale=v_pool_scale, v_pool_proj=v_pool_proj, v_dn_proj=v_dn_proj, v_mix_w_out=v_mix_w_out, v_final_g=v_final_g)
    weights = {n: given[n] for n in TWIN_WEIGHTS}
    shared = {n: given[n] for n in SHARED_INPUTS}
    per_example = {n: given[n] for n in ['x', 'c']}
    grad_fn = _jax.value_and_grad(_loss, argnums=(0, 1))

    def one_microbatch(ex, loss_target):
        ex = dict(ex)
        diff = ex.pop(TWIN_DIFF_INPUT)
        return grad_fn(weights, diff, {**shared, **ex}, loss_target)

    if N_MICROBATCH == 1:
        loss, (grad_w, grad_x) = one_microbatch(per_example, given["loss_target"])
    else:
        def body(carry, xs):
            loss_sum, grad_sum = carry
            l_k, (gw_k, gx_k) = one_microbatch(xs[0], xs[1])
            with _jax.named_scope("update"):
                return (loss_sum + l_k, _jax.tree.map(_jnp.add, grad_sum, gw_k)), gx_k

        init = (_jnp.zeros((), _jnp.float32), _jax.tree.map(_jnp.zeros_like, weights))
        (loss, grad_w), grad_x = _jax.lax.scan(body, init, (per_example, given["loss_target"]))
    with _jax.named_scope("update"):
        delta_w, new_m, new_v = {}, {}, {}
        for n in TWIN_WEIGHTS:
            delta_w[n], new_m[n], new_v[n] = _adamw(weights[n], grad_w[n], given["m_" + n], given["v_" + n])
    return (loss, grad_x, *[grad_w[n] for n in TWIN_WEIGHTS], *[delta_w[n] for n in TWIN_WEIGHTS],
            *[new_m[n] for n in TWIN_WEIGHTS], *[new_v[n] for n in TWIN_WEIGHTS])
```

```python
import functools

import jax
import jax.numpy as jnp
from jax import lax
from jax.experimental import pallas as pl
from jax.experimental.pallas import tpu as pltpu

F32 = jnp.float32
BF16 = jnp.bfloat16
SDS = jax.ShapeDtypeStruct
HI = lax.Precision.HIGHEST

D = 1024
FH = 2816
NH = 8
HD = 128
CH = 64
NDEV = 8
PW = 512
PG = 128
RMS_EPS = 1e-6
L2_EPS = 1e-6
TR = 256
HALO = 16
VMEM_LIMIT = 56 * 1024 * 1024

MIXP = 6912
OFF_Q, OFF_K, OFF_V, OFF_Z, OFF_GP, OFF_GD, OFF_XP, OFF_BA = 0, 1024, 2048, 3072, 4096, 5120, 6144, 6656
MIX_RAW = 6672

ADAM_LR = 0.001
ADAM_B1 = 0.9
ADAM_B2 = 0.999
ADAM_EPS = 1e-08
ADAM_WD = 0.01
ADAM_STEP = 10

NN = (((1,), (0,)), ((), ()))
NT = (((1,), (1,)), ((), ()))
TN = (((0,), (0,)), ((), ()))


def _dg(a, b, dims, prec=None):
    return lax.dot_general(a, b, dims, precision=prec, preferred_element_type=F32)


def _make_dots(prec):
    @jax.custom_vjp
    def nn(a, b):
        return _dg(a, b, NN, prec)

    @jax.custom_vjp
    def nt(a, b):
        return _dg(a, b, NT, prec)

    @jax.custom_vjp
    def tn(a, b):
        return _dg(a, b, TN, prec)

    nn.defvjp(lambda a, b: (nn(a, b), (a, b)), lambda r, d: (nt(d, r[1]), tn(r[0], d)))
    nt.defvjp(lambda a, b: (nt(a, b), (a, b)), lambda r, d: (nn(d, r[1]), tn(d, r[0])))
    tn.defvjp(lambda a, b: (tn(a, b), (a, b)), lambda r, d: (nt(r[1], d), nn(r[0], d)))
    return nn, nt, tn


_nn, _nt, _tn = _make_dots(None)


def _params(sem):
    return pltpu.CompilerParams(dimension_semantics=sem, vmem_limit_bytes=VMEM_LIMIT)


def _sigmoid(x):
    return 1.0 / (1.0 + jnp.exp(-x))


def _silu(x):
    return x * _sigmoid(x)


def _dsilu(x):
    s = _sigmoid(x)
    return s * (1.0 + x * (1.0 - s))


def _pick(n, cands):
    for c in cands:
        if n % c == 0:
            return c
    raise ValueError(f"no tile for {n}")


def _iota(shape, dim):
    return lax.broadcasted_iota(jnp.int32, shape, dim)


def _matmul(a, b, *, ta=False, tb=False, out_dtype, name):
    if ta:
        k_dim, m_dim = a.shape
    else:
        m_dim, k_dim = a.shape
    n_dim = b.shape[0] if tb else b.shape[1]
    tm = _pick(m_dim, (512, 256, 128))
    tn = _pick(n_dim, (512, 768, 256, 128))
    tk = k_dim if (k_dim <= 2816 and not ta) else _pick(k_dim, (1024, 512, 256))
    nk = k_dim // tk
    dims = ((((0,) if ta else (1,)), ((1,) if tb else (0,))), ((), ()))

    def body(a_ref, b_ref, o_ref, acc_ref):
        k = pl.program_id(2)

        @pl.when(k == 0)
        def _():
            acc_ref[...] = jnp.zeros_like(acc_ref)

        acc_ref[...] += lax.dot_general(a_ref[...].astype(BF16), b_ref[...].astype(BF16), dims,
                                        preferred_element_type=F32)

        @pl.when(k == nk - 1)
        def _():
            o_ref[...] = acc_ref[...].astype(o_ref.dtype)

    a_spec = (pl.BlockSpec((tk, tm), lambda i, j, k: (k, i)) if ta
              else pl.BlockSpec((tm, tk), lambda i, j, k: (i, k)))
    b_spec = (pl.BlockSpec((tn, tk), lambda i, j, k: (j, k)) if tb
              else pl.BlockSpec((tk, tn), lambda i, j, k: (k, j)))
    return pl.pallas_call(
        body, grid=(m_dim // tm, n_dim // tn, nk),
        in_specs=[a_spec, b_spec],
        out_specs=pl.BlockSpec((tm, tn), lambda i, j, k: (i, j)),
        out_shape=SDS((m_dim, n_dim), out_dtype),
        scratch_shapes=[pltpu.VMEM((tm, tn), F32)],
        compiler_params=_params(("parallel", "parallel", "arbitrary")),
        name=name,
    )(a, b)


def _row(width, col=0):
    return pl.BlockSpec((TR, width), lambda i: (i, col))


def _vec(width):
    return pl.BlockSpec((1, width), lambda i: (0, 0))


def _norm_mod_fwd(x, g, shift, scale, name):
    t = x.shape[0]

    def body(x_ref, g_ref, sh_ref, sc_ref, o_ref):
        xv = x_ref[...]
        r = lax.rsqrt(jnp.mean(xv * xv, axis=-1, keepdims=True) + RMS_EPS)
        o_ref[...] = (((xv * r) * g_ref[...]) * (1.0 + sc_ref[...]) + sh_ref[...]).astype(o_ref.dtype)

    return pl.pallas_call(
        body, grid=(t // TR,), in_specs=[_row(D), _vec(D), _vec(D), _vec(D)], out_specs=_row(D),
        out_shape=SDS((t, D), BF16), compiler_params=_params(("parallel",)), name=name,
    )(x, g, shift, scale)


def _norm_mod_bwd(x, g, scale, dh, dx_in, name):
    t = x.shape[0]

    def body(x_ref, g_ref, sc_ref, dh_ref, dxi_ref, dx_ref, dsh_ref, dsc_ref, dg_ref):
        @pl.when(pl.program_id(0) == 0)
        def _():
            dsh_ref[...] = jnp.zeros_like(dsh_ref)
            dsc_ref[...] = jnp.zeros_like(dsc_ref)
            dg_ref[...] = jnp.zeros_like(dg_ref)

        xv = x_ref[...]
        gv = g_ref[...]
        dh = dh_ref[...]
        r = lax.rsqrt(jnp.mean(xv * xv, axis=-1, keepdims=True) + RMS_EPS)
        n = xv * r
        dsh_ref[...] += jnp.sum(dh, axis=0, keepdims=True)
        dsc_ref[...] += jnp.sum(dh * (n * gv), axis=0, keepdims=True)
        tt = dh * (1.0 + sc_ref[...])
        dg_ref[...] += jnp.sum(tt * n, axis=0, keepdims=True)
        dn = tt * gv
        dx_ref[...] = dxi_ref[...] + r * (dn - n * jnp.mean(dn * n, axis=-1, keepdims=True))

    return pl.pallas_call(
        body, grid=(t // TR,), in_specs=[_row(D), _vec(D), _vec(D), _row(D), _row(D)],
        out_specs=[_row(D), _vec(D), _vec(D), _vec(D)],
        out_shape=[SDS((t, D), F32), SDS((1, D), F32), SDS((1, D), F32), SDS((1, D), F32)],
        compiler_params=_params(("arbitrary",)), name=name,
    )(x, g, scale, dh, dx_in)


def _swiglu_fwd(u, name):
    t = u.shape[0]

    def body(g_ref, u_ref, o_ref):
        o_ref[...] = (_silu(g_ref[...]) * u_ref[...]).astype(o_ref.dtype)

    return pl.pallas_call(
        body, grid=(t // TR,), in_specs=[_row(FH, 0), _row(FH, 1)], out_specs=_row(FH),
        out_shape=SDS((t, FH), BF16), compiler_params=_params(("parallel",)), name=name,
    )(u, u)


def _swiglu_bwd(u, da, name):
    t = u.shape[0]

    def body(g_ref, u_ref, da_ref, o_ref):
        gv = g_ref[...]
        dav = da_ref[...]
        o_ref[:, :FH] = (dav * u_ref[...] * _dsilu(gv)).astype(o_ref.dtype)
        o_ref[:, FH:] = (dav * _silu(gv)).astype(o_ref.dtype)

    return pl.pallas_call(
        body, grid=(t // TR,), in_specs=[_row(FH, 0), _row(FH, 1), _row(FH)], out_specs=_row(2 * FH),
        out_shape=SDS((t, 2 * FH), BF16), compiler_params=_params(("parallel",)), name=name,
    )(u, u, da)


def _resid_fwd(x, y, gate, coef, name):
    t = x.shape[0]

    def body(x_ref, y_ref, g_ref, o_ref):
        o_ref[...] = x_ref[...] + (coef * g_ref[...]) * y_ref[...]

    return pl.pallas_call(
        body, grid=(t // TR,), in_specs=[_row(D), _row(D), _vec(D)], out_specs=_row(D),
        out_shape=SDS((t, D), F32), compiler_params=_params(("parallel",)), name=name,
    )(x, y, gate)


def _resid_bwd(dx, y, gate, coef, name):
    t = dx.shape[0]

    def body(dx_ref, y_ref, g_ref, dy_ref, dg_ref):
        @pl.when(pl.program_id(0) == 0)
        def _():
            dg_ref[...] = jnp.zeros_like(dg_ref)

        dxv = dx_ref[...]
        dy_ref[...] = ((coef * g_ref[...]) * dxv).astype(dy_ref.dtype)
        dg_ref[...] += jnp.sum((coef * dxv) * y_ref[...], axis=0, keepdims=True)

    return pl.pallas_call(
        body, grid=(t // TR,), in_specs=[_row(D), _row(D), _vec(D)], out_specs=[_row(D), _vec(D)],
        out_shape=[SDS((t, D), BF16), SDS((1, D), F32)],
        compiler_params=_params(("arbitrary",)), name=name,
    )(dx, y, gate)


def _final_loss(x, fg, target, name):
    t = x.shape[0]
    nt = t // TR

    def body(x_ref, g_ref, t_ref, loss_ref, dx_ref, dg_ref, acc_ref):
        i = pl.program_id(0)

        @pl.when(i == 0)
        def _():
            acc_ref[...] = jnp.zeros_like(acc_ref)
            dg_ref[...] = jnp.zeros_like(dg_ref)

        xv = x_ref[...]
        gv = g_ref[...]
        r = lax.rsqrt(jnp.mean(xv * xv, axis=-1, keepdims=True) + RMS_EPS)
        n = xv * r
        err = n * gv - t_ref[...]
        acc_ref[...] += jnp.sum(err * err, axis=0, keepdims=True)
        dy = err * (1.0 / D)
        dg_ref[...] += jnp.sum(dy * n, axis=0, keepdims=True)
        dn = dy * gv
        dx_ref[...] = r * (dn - n * jnp.mean(dn * n, axis=-1, keepdims=True))

        @pl.when(i == nt - 1)
        def _():
            tot = jnp.sum(acc_ref[...], axis=1, keepdims=True) * (0.5 / D)
            loss_ref[...] = jnp.broadcast_to(tot, loss_ref.shape)

    return pl.pallas_call(
        body, grid=(nt,), in_specs=[_row(D), _vec(D), _row(D)],
        out_specs=[_vec(128), _row(D), _vec(D)],
        out_shape=[SDS((1, 128), F32), SDS((t, D), F32), SDS((1, D), F32)],
        scratch_shapes=[pltpu.VMEM((1, D), F32)],
        compiler_params=_params(("arbitrary",)), name=name,
    )(x, fg, target)


def _halo_prev(width, col):
    per = TR // HALO
    return pl.BlockSpec((HALO, width), lambda i: (jnp.maximum(i * per - 1, 0), col))


def _halo_next(width, col, nt):
    per = TR // HALO
    return pl.BlockSpec((HALO, width), lambda i: (jnp.minimum((i + 1) * per, nt * per - 1), col))


def _pool_windows(ext, tile_index):
    rows = _iota((TR, PG), 0) + tile_index * TR + 1
    pooled, counts = [], []
    for gi in range(4):
        w = 2 << gi
        e = ext[:, gi * PG:(gi + 1) * PG]
        s = e
        step = 1
        while step < w:
            s = s + pltpu.roll(s, step, 0)
            step *= 2
        cnt = jnp.minimum(rows, w).astype(F32)
        pooled.append(s[HALO:] / cnt - e[HALO:])
        counts.append(cnt)
    return pooled, counts


def _pool_fwd(proj, pool_w, pool_scale, pool_proj, name):
    t = proj.shape[0]
    xcol = OFF_XP // PW

    def body(x_ref, h_ref, pw_ref, ps_ref, pp_ref, o_ref):
        i = pl.program_id(0)
        halo = jnp.where(i > 0, h_ref[...], 0.0)
        ext = jnp.concatenate([halo, x_ref[...]], axis=0)
        pooled, _ = _pool_windows(ext, i)
        mixed = [_dg(pooled[g].astype(BF16), pw_ref[g].astype(BF16), NN) for g in range(4)]
        ypre = jnp.concatenate(mixed, axis=1) * ps_ref[...]
        o_ref[...] = _dg(ypre.astype(BF16), pp_ref[...], NN)

    return pl.pallas_call(
        body, grid=(t // TR,),
        in_specs=[_row(PW, xcol), _halo_prev(PW, xcol),
                  pl.BlockSpec((4, PG, PG), lambda i: (0, 0, 0)), _vec(PW),
                  pl.BlockSpec((PW, D), lambda i: (0, 0))],
        out_specs=_row(D), out_shape=SDS((t, D), F32),
        compiler_params=_params(("parallel",)), name=name,
    )(proj, proj, pool_w, pool_scale, pool_proj)


def _pool_bwd_local(proj, pool_w, pool_scale, pool_proj, dya, name):
    t = proj.shape[0]
    xcol = OFF_XP // PW

    def body(x_ref, h_ref, pw_ref, ps_ref, pp_ref, dya_ref, dwin_ref, dpl_ref, dpw_ref, dps_ref, dpp_ref):
        i = pl.program_id(0)

        @pl.when(i == 0)
        def _():
            dpw_ref[...] = jnp.zeros_like(dpw_ref)
            dps_ref[...] = jnp.zeros_like(dps_ref)
            dpp_ref[...] = jnp.zeros_like(dpp_ref)

        halo = jnp.where(i > 0, h_ref[...], 0.0)
        ext = jnp.concatenate([halo, x_ref[...]], axis=0)
        pooled, counts = _pool_windows(ext, i)
        mixed = jnp.concatenate(
            [_dg(pooled[g].astype(BF16), pw_ref[g].astype(BF16), NN) for g in range(4)], axis=1)
        ps = ps_ref[...]
        ypre = mixed * ps
        dyab = dya_ref[...].astype(BF16)
        dypre = _dg(dyab, pp_ref[...], NT)
        dpp_ref[...] += _dg(ypre.astype(BF16), dyab, TN)
        dps_ref[...] += jnp.sum(dypre * mixed, axis=0, keepdims=True)
        dmixed = dypre * ps
        for g in range(4):
            dm = dmixed[:, g * PG:(g + 1) * PG].astype(BF16)
            dpw_ref[g] += _dg(pooled[g].astype(BF16), dm, TN)
            dpooled = _dg(dm, pw_ref[g].astype(BF16), NT)
            dwin_ref[:, g * PG:(g + 1) * PG] = dpooled / counts[g]
            dpl_ref[:, g * PG:(g + 1) * PG] = dpooled

    return pl.pallas_call(
        body, grid=(t // TR,),
        in_specs=[_row(PW, xcol), _halo_prev(PW, xcol),
                  pl.BlockSpec((4, PG, PG), lambda i: (0, 0, 0)), _vec(PW),
                  pl.BlockSpec((PW, D), lambda i: (0, 0)), _row(D)],
        out_specs=[_row(PW), _row(PW), pl.BlockSpec((4, PG, PG), lambda i: (0, 0, 0)), _vec(PW),
                   pl.BlockSpec((PW, D), lambda i: (0, 0))],
        out_shape=[SDS((t, PW), F32), SDS((t, PW), F32), SDS((4, PG, PG), F32), SDS((1, PW), F32),
                   SDS((PW, D), F32)],
        compiler_params=_params(("arbitrary",)), name=name,
    )(proj, proj, pool_w, pool_scale, pool_proj, dya)


def _pool_bwd_window(dwin, dpl, name):
    t = dwin.shape[0]
    nt = t // TR
    ext_rows = TR + HALO

    def body(dw_ref, h_ref, dp_ref, o_ref):
        i = pl.program_id(0)
        halo = jnp.where(i < nt - 1, h_ref[...], 0.0)
        ext = jnp.concatenate([dw_ref[...], halo], axis=0)
        for gi in range(4):
            w = 2 << gi
            s = ext[:, gi * PG:(gi + 1) * PG]
            step = 1
            while step < w:
                s = s + pltpu.roll(s, ext_rows - step, 0)
                step *= 2
            o_ref[:, gi * PG:(gi + 1) * PG] = (s[:TR] - dp_ref[:, gi * PG:(gi + 1) * PG]).astype(o_ref.dtype)

    return pl.pallas_call(
        body, grid=(nt,), in_specs=[_row(PW), _halo_next(PW, 0, nt), _row(PW)], out_specs=_row(PW),
        out_shape=SDS((t, PW), BF16), compiler_params=_params(("parallel",)), name=name,
    )(dwin, dwin, dpl)


def _conv_group(ext, cw_ref, cols):
    acc = cw_ref[3:4, cols] * ext
    for j in range(3):
        acc = acc + cw_ref[j:j + 1, cols] * pltpu.roll(ext, 3 - j, 0)
    return acc[HALO:]


def _gate_terms(raw, al, dt):
    beta = _sigmoid(raw)
    xg = raw + dt
    sp = jnp.maximum(xg, 0.0) + jnp.log(1.0 + jnp.exp(-jnp.abs(xg)))
    g = -jnp.exp(al) * sp
    return beta, g, _sigmoid(xg)


def _dn_pre_fwd(proj, conv_w, al_row, dt_row, name):
    t = proj.shape[0]

    def body(x_ref, h_ref, cw_ref, ba_ref, al_ref, dt_ref, q_ref, k_ref, v_ref, bg_ref):
        i = pl.program_id(0)
        keep = i > 0
        for grp in range(24):
            cols = slice(grp * HD, (grp + 1) * HD)
            ext = jnp.concatenate([jnp.where(keep, h_ref[:, cols], 0.0), x_ref[:, cols]], axis=0)
            s = _silu(_conv_group(ext, cw_ref, cols))
            seg, head = divmod(grp, NH)
            hc = slice(head * HD, (head + 1) * HD)
            if seg == 0:
                q_ref[:, hc] = s * lax.rsqrt(jnp.sum(s * s, axis=-1, keepdims=True) + L2_EPS) * (HD ** -0.5)
            elif seg == 1:
                k_ref[:, hc] = s * lax.rsqrt(jnp.sum(s * s, axis=-1, keepdims=True) + L2_EPS)
            else:
                v_ref[:, hc] = s
        lane = _iota((TR, 128), 1)
        rowc = _iota((TR, 128), 0) % CH
        beta, g, _ = _gate_terms(ba_ref[...], al_ref[...], dt_ref[...])
        step = 1
        while step < CH:
            g = g + jnp.where(rowc >= step, pltpu.roll(g, step, 0), 0.0)
            step *= 2
        bg_ref[...] = jnp.where(lane < NH, beta, jnp.where(lane < 2 * NH, g, 0.0))

    return pl.pallas_call(
        body, grid=(t // TR,),
        in_specs=[_row(3 * D, 0), _halo_prev(3 * D, 0), pl.BlockSpec((4, 3 * D), lambda i: (0, 0)),
                  _row(128, OFF_BA // 128), _vec(128), _vec(128)],
        out_specs=[_row(D), _row(D), _row(D), _row(128)],
        out_shape=[SDS((t, D), F32), SDS((t, D), F32), SDS((t, D), F32), SDS((t, 128), F32)],
        compiler_params=_params(("parallel",)), name=name,
    )(proj, proj, conv_w, proj, al_row, dt_row)


def _dn_pre_bwd_act(proj, conv_w, al_row, dt_row, dq, dk, dv, dbg, name):
    t = proj.shape[0]

    def body(x_ref, h_ref, cw_ref, ba_ref, al_ref, dt_ref, dq_ref, dk_ref, dv_ref, dbg_ref,
             dc_ref, draw_ref, dal_ref, ddt_ref):
        i = pl.program_id(0)

        @pl.when(i == 0)
        def _():
            dal_ref[...] = jnp.zeros_like(dal_ref)
            ddt_ref[...] = jnp.zeros_like(ddt_ref)

        keep = i > 0
        for grp in range(24):
            cols = slice(grp * HD, (grp + 1) * HD)
            ext = jnp.concatenate([jnp.where(keep, h_ref[:, cols], 0.0), x_ref[:, cols]], axis=0)
            cv = _conv_group(ext, cw_ref, cols)
            seg, head = divmod(grp, NH)
            hc = slice(head * HD, (head + 1) * HD)
            if seg == 2:
                ds = dv_ref[:, hc]
            else:
                s = _silu(cv)
                r = lax.rsqrt(jnp.sum(s * s, axis=-1, keepdims=True) + L2_EPS)
                dy = dq_ref[:, hc] if seg == 0 else dk_ref[:, hc]
                c = (HD ** -0.5) if seg == 0 else 1.0
                ds = (c * r) * (dy - s * ((r * r) * jnp.sum(dy * s, axis=-1, keepdims=True)))
            dc_ref[:, cols] = ds * _dsilu(cv)
        lane = _iota((TR, 128), 1)
        rowc = _iota((TR, 128), 0) % CH
        isb = lane < NH
        isg = jnp.logical_and(lane >= NH, lane < 2 * NH)
        beta, g, sg = _gate_terms(ba_ref[...], al_ref[...], dt_ref[...])
        dbgv = dbg_ref[...]
        dg = dbgv
        step = 1
        while step < CH:
            dg = dg + jnp.where(rowc < CH - step, pltpu.roll(dg, TR - step, 0), 0.0)
            step *= 2
        da_raw = dg * (-jnp.exp(al_ref[...])) * sg
        draw_ref[...] = jnp.where(isb, dbgv * beta * (1.0 - beta), jnp.where(isg, da_raw, 0.0)).astype(draw_ref.dtype)
        dal_ref[...] += jnp.sum(jnp.where(isg, dg * g, 0.0), axis=0, keepdims=True)
        ddt_ref[...] += jnp.sum(jnp.where(isg, da_raw, 0.0), axis=0, keepdims=True)

    return pl.pallas_call(
        body, grid=(t // TR,),
        in_specs=[_row(3 * D, 0), _halo_prev(3 * D, 0), pl.BlockSpec((4, 3 * D), lambda i: (0, 0)),
                  _row(128, OFF_BA // 128), _vec(128), _vec(128), _row(D), _row(D), _row(D), _row(128)],
        out_specs=[_row(3 * D), _row(128), _vec(128), _vec(128)],
        out_shape=[SDS((t, 3 * D), F32), SDS((t, 128), BF16), SDS((1, 128), F32), SDS((1, 128), F32)],
        compiler_params=_params(("arbitrary",)), name=name,
    )(proj, proj, conv_w, proj, al_row, dt_row, dq, dk, dv, dbg)


def _dn_pre_bwd_conv(proj, conv_w, dconv, name):
    t = proj.shape[0]
    nt = t // TR
    ext_rows = TR + HALO

    def body(x_ref, h_ref, cw_ref, dc_ref, dn_ref, dx_ref, dcw_ref):
        i = pl.program_id(0)

        @pl.when(i == 0)
        def _():
            dcw_ref[...] = jnp.zeros_like(dcw_ref)

        keep_prev = i > 0
        keep_next = i < nt - 1
        for grp in range(24):
            cols = slice(grp * HD, (grp + 1) * HD)
            dct = dc_ref[:, cols]
            dext = jnp.concatenate([dct, jnp.where(keep_next, dn_ref[:, cols], 0.0)], axis=0)
            acc = cw_ref[3:4, cols] * dext
            for j in range(3):
                acc = acc + cw_ref[j:j + 1, cols] * pltpu.roll(dext, ext_rows - (3 - j), 0)
            dx_ref[:, cols] = acc[:TR].astype(dx_ref.dtype)
            xext = jnp.concatenate([jnp.where(keep_prev, h_ref[:, cols], 0.0), x_ref[:, cols]], axis=0)
            for j in range(4):
                xs = xext if j == 3 else pltpu.roll(xext, 3 - j, 0)
                dcw_ref[j:j + 1, cols] += jnp.sum(xs[HALO:] * dct, axis=0, keepdims=True)

    return pl.pallas_call(
        body, grid=(nt,),
        in_specs=[_row(3 * D, 0), _halo_prev(3 * D, 0), pl.BlockSpec((4, 3 * D), lambda i: (0, 0)),
                  _row(3 * D), _halo_next(3 * D, 0, nt)],
        out_specs=[_row(3 * D), pl.BlockSpec((4, 3 * D), lambda i: (0, 0))],
        out_shape=[SDS((t, 3 * D), BF16), SDS((4, 3 * D), F32)],
        compiler_params=_params(("arbitrary",)), name=name,
    )(proj, proj, conv_w, dconv, dconv)


def _dn_post_fwd(o, proj, gn, name):
    t = o.shape[0]

    def body(o_ref, z_ref, g_ref, out_ref):
        gv = g_ref[...]
        for h in range(NH):
            hc = slice(h * HD, (h + 1) * HD)
            ov = o_ref[:, hc]
            r = lax.rsqrt(jnp.mean(ov * ov, axis=-1, keepdims=True) + RMS_EPS)
            out_ref[:, hc] = (((ov * r) * gv) * _silu(z_ref[:, hc])).astype(out_ref.dtype)

    return pl.pallas_call(
        body, grid=(t // TR,), in_specs=[_row(D), _row(D, OFF_Z // D), _vec(HD)], out_specs=_row(D),
        out_shape=SDS((t, D), BF16), compiler_params=_params(("parallel",)), name=name,
    )(o, proj, gn)


def _dn_post_bwd(o, proj, gn, dob, name):
    t = o.shape[0]

    def body(o_ref, z_ref, g_ref, d_ref, do_ref, dz_ref, dg_ref):
        @pl.when(pl.program_id(0) == 0)
        def _():
            dg_ref[...] = jnp.zeros_like(dg_ref)

        gv = g_ref[...]
        acc = jnp.zeros((1, HD), F32)
        for h in range(NH):
            hc = slice(h * HD, (h + 1) * HD)
            ov = o_ref[:, hc]
            zv = z_ref[:, hc]
            dv = d_ref[:, hc]
            r = lax.rsqrt(jnp.mean(ov * ov, axis=-1, keepdims=True) + RMS_EPS)
            n = ov * r
            dz_ref[:, hc] = (dv * (n * gv) * _dsilu(zv)).astype(dz_ref.dtype)
            dng = dv * _silu(zv)
            acc = acc + jnp.sum(dng * n, axis=0, keepdims=True)
            dn = dng * gv
            do_ref[:, hc] = r * (dn - n * jnp.mean(dn * n, axis=-1, keepdims=True))
        dg_ref[...] += acc

    return pl.pallas_call(
        body, grid=(t // TR,), in_specs=[_row(D), _row(D, OFF_Z // D), _vec(HD), _row(D)],
        out_specs=[_row(D), _row(D), _vec(HD)],
        out_shape=[SDS((t, D), F32), SDS((t, D), BF16), SDS((1, HD), F32)],
        compiler_params=_params(("arbitrary",)), name=name,
    )(o, proj, gn, dob)


def _merge_fwd(ya, yb, proj, name):
    t = ya.shape[0]

    def body(a_ref, b_ref, gp_ref, gd_ref, o_ref):
        o_ref[...] = (_sigmoid(gp_ref[...]) * a_ref[...] + _sigmoid(gd_ref[...]) * b_ref[...]).astype(o_ref.dtype)

    return pl.pallas_call(
        body, grid=(t // TR,), in_specs=[_row(D), _row(D), _row(D, OFF_GP // D), _row(D, OFF_GD // D)],
        out_specs=_row(D), out_shape=SDS((t, D), BF16),
        compiler_params=_params(("parallel",)), name=name,
    )(ya, yb, proj, proj)


def _merge_bwd(dm, ya, yb, proj, name):
    t = ya.shape[0]

    def body(d_ref, a_ref, b_ref, gp_ref, gd_ref, da_ref, db_ref, dgp_ref, dgd_ref):
        dv = d_ref[...]
        sp = _sigmoid(gp_ref[...])
        sd = _sigmoid(gd_ref[...])
        da_ref[...] = dv * sp
        db_ref[...] = (dv * sd).astype(db_ref.dtype)
        dgp_ref[...] = (dv * a_ref[...] * sp * (1.0 - sp)).astype(dgp_ref.dtype)
        dgd_ref[...] = (dv * b_ref[...] * sd * (1.0 - sd)).astype(dgd_ref.dtype)

    return pl.pallas_call(
        body, grid=(t // TR,),
        in_specs=[_row(D), _row(D), _row(D), _row(D, OFF_GP // D), _row(D, OFF_GD // D)],
        out_specs=[_row(D)] * 4,
        out_shape=[SDS((t, D), F32), SDS((t, D), BF16), SDS((t, D), BF16), SDS((t, D), BF16)],
        compiler_params=_params(("parallel",)), name=name,
    )(dm, ya, yb, proj, proj)


def _neumann_inverse(a):
    ri = _iota((CH, CH), 0)
    ci = _iota((CH, CH), 1)
    n = -a
    p = jnp.where(ri == ci, 1.0, 0.0).astype(F32) + n
    x = n
    for _ in range(5):
        x = _dg(x, x, NN, HI)
        p = p + _dg(p, x, NN, HI)
    return p


@jax.custom_vjp
def _unit_lower_solve(a, rhs):
    return _dg(_neumann_inverse(a), rhs, NN, HI)


def _unit_lower_solve_fwd(a, rhs):
    inv = _neumann_inverse(a)
    sol = _dg(inv, rhs, NN, HI)
    return sol, (inv, sol)


def _unit_lower_solve_bwd(res, d):
    inv, sol = res
    drhs = _dg(inv, d, TN, HI)
    return -_dg(drhs, sol, NT, HI), drhs


_unit_lower_solve.defvjp(_unit_lower_solve_fwd, _unit_lower_solve_bwd)


def _chunk_local(q, k, v, g128, g64, gl128, b128, b64):
    ri = _iota((CH, CH), 0)
    ci = _iota((CH, CH), 1)
    causal = ri >= ci
    strict = ri > ci
    gj = _dg(jnp.ones((CH, CH), F32), jnp.where(ri == ci, g64, 0.0), NN, HI)
    decay = jnp.where(causal, jnp.exp(jnp.where(causal, g64 - gj, 0.0)), 0.0)
    a = jnp.where(strict, b64 * _nt(k, k) * decay, 0.0)
    eg = jnp.exp(g128)
    sol = _unit_lower_solve(a, jnp.concatenate([b128 * v, (b128 * eg) * k], axis=1))
    qk = jnp.where(causal, _nt(q, k) * decay, 0.0)
    return sol[:, :HD], sol[:, HD:], qk, q * eg, k * jnp.exp(gl128 - g128), jnp.exp(gl128)


def _head_gates(bgv, h):
    lane = _iota((CH, 128), 1)
    row = _iota((CH, 128), 0)
    bcol = jnp.sum(jnp.where(lane == h, bgv, 0.0), axis=1, keepdims=True)
    gcol = jnp.sum(jnp.where(lane == NH + h, bgv, 0.0), axis=1, keepdims=True)
    g128 = jnp.broadcast_to(gcol, (CH, 128))
    gl128 = jnp.broadcast_to(jnp.sum(jnp.where(row == CH - 1, g128, 0.0), axis=0, keepdims=True), (CH, 128))
    return (g128, jnp.broadcast_to(gcol, (CH, CH)), gl128,
            jnp.broadcast_to(bcol, (CH, 128)), jnp.broadcast_to(bcol, (CH, CH)))


def _chunk_specs():
    row = pl.BlockSpec((CH, D), lambda i: (i, 0))
    small = pl.BlockSpec((CH, 128), lambda i: (i, 0))
    qk = pl.BlockSpec((NH, CH, CH), lambda i: (i, 0, 0))
    eg = pl.BlockSpec((1, NH, 128), lambda i: (i, 0, 0))
    return row, small, qk, eg


def _dn_local_fwd(q, k, v, bg, name):
    t = q.shape[0]
    n = t // CH

    def body(q_ref, k_ref, v_ref, bg_ref, u_ref, w_ref, qk_ref, qd_ref, kd_ref, eg_ref):
        bgv = bg_ref[...]
        for h in range(NH):
            hc = slice(h * HD, (h + 1) * HD)
            u, w, qk, qd, kd, egl = _chunk_local(q_ref[:, hc], k_ref[:, hc], v_ref[:, hc], *_head_gates(bgv, h))
            u_ref[:, hc] = u
            w_ref[:, hc] = w.astype(w_ref.dtype)
            qd_ref[:, hc] = qd.astype(qd_ref.dtype)
            kd_ref[:, hc] = kd.astype(kd_ref.dtype)
            qk_ref[h] = qk.astype(qk_ref.dtype)
            eg_ref[0, h:h + 1, :] = egl[0:1, :]

    row, small, qkb, egb = _chunk_specs()
    return pl.pallas_call(
        body, grid=(n,), in_specs=[row, row, row, small], out_specs=[row, row, qkb, row, row, egb],
        out_shape=[SDS((t, D), F32), SDS((t, D), BF16), SDS((n * NH, CH, CH), BF16), SDS((t, D), BF16),
                   SDS((t, D), BF16), SDS((n, NH, 128), F32)],
        compiler_params=_params(("parallel",)), name=name,
    )(q, k, v, bg)


def _dn_local_bwd(q, k, v, bg, du, dw, dqk, dqd, dkd, deg, name):
    t = q.shape[0]
    n = t // CH

    def body(q_ref, k_ref, v_ref, bg_ref, du_ref, dw_ref, dqk_ref, dqd_ref, dkd_ref, deg_ref,
             dq_ref, dk_ref, dv_ref, dbg_ref):
        bgv = bg_ref[...]
        lane = _iota((CH, 128), 1)
        row = _iota((CH, 128), 0)
        first = jnp.where(row == 0, 1.0, 0.0)
        acc = jnp.zeros((CH, 128), F32)
        for h in range(NH):
            hc = slice(h * HD, (h + 1) * HD)
            _, vjp = jax.vjp(_chunk_local, q_ref[:, hc], k_ref[:, hc], v_ref[:, hc], *_head_gates(bgv, h))
            cts = (du_ref[:, hc], dw_ref[:, hc], dqk_ref[h], dqd_ref[:, hc], dkd_ref[:, hc],
                   jnp.broadcast_to(deg_ref[0, h:h + 1, :], (CH, 128)) * first)
            dq, dk, dv, dg128, dg64, dgl, db128, db64 = vjp(cts)
            dq_ref[:, hc] = dq
            dk_ref[:, hc] = dk
            dv_ref[:, hc] = dv
            dg = jnp.sum(dg128, axis=1, keepdims=True) + jnp.sum(dg64, axis=1, keepdims=True)
            tot = jnp.sum(jnp.sum(dgl, axis=0, keepdims=True), axis=1, keepdims=True)
            dg = dg + jnp.where(row[:, 0:1] == CH - 1, tot, 0.0)
            db = jnp.sum(db128, axis=1, keepdims=True) + jnp.sum(db64, axis=1, keepdims=True)
            acc = acc + jnp.where(lane == h, db, 0.0) + jnp.where(lane == NH + h, dg, 0.0)
        dbg_ref[...] = acc

    row, small, qkb, egb = _chunk_specs()
    return pl.pallas_call(
        body, grid=(n,), in_specs=[row, row, row, small, row, row, qkb, row, row, egb],
        out_specs=[row, row, row, small],
        out_shape=[SDS((t, D), F32)] * 3 + [SDS((t, 128), F32)],
        compiler_params=_params(("parallel",)), name=name,
    )(q, k, v, bg, du, dw, dqk, dqd, dkd, deg)


def _state_step(s, u, w, qk, qd, kd, egl):
    v_new = u - _nn(w, s)
    o = _nn(qd, s) + _nn(qk, v_new)
    return s * egl + _tn(kd, v_new), o


def _dn_scan_fwd(u, w, qk, qd, kd, eg, name):
    t = u.shape[0]
    n = t // CH

    def body(u_ref, w_ref, qk_ref, qd_ref, kd_ref, eg_ref, o_ref, save_ref, s_ref):
        @pl.when(pl.program_id(0) == 0)
        def _():
            s_ref[...] = jnp.zeros_like(s_ref)

        for h in range(NH):
            hc = slice(h * HD, (h + 1) * HD)
            s = s_ref[h]
            save_ref[0, h] = s
            s_new, o = _state_step(s, u_ref[:, hc], w_ref[:, hc].astype(F32), qk_ref[h].astype(F32),
                                   qd_ref[:, hc].astype(F32), kd_ref[:, hc].astype(F32), eg_ref[0, h:h + 1, :])
            o_ref[:, hc] = o
            s_ref[h] = s_new

    row, _, qkb, egb = _chunk_specs()
    return pl.pallas_call(
        body, grid=(n,), in_specs=[row, row, qkb, row, row, egb],
        out_specs=[row, pl.BlockSpec((1, NH, HD, HD), lambda i: (i, 0, 0, 0))],
        out_shape=[SDS((t, D), F32), SDS((n, NH, HD, HD), F32)],
        scratch_shapes=[pltpu.VMEM((NH, HD, HD), F32)],
        compiler_params=_params(("arbitrary",)), name=name,
    )(u, w, qk, qd, kd, eg)


def _dn_scan_bwd(u, w, qk, qd, kd, eg, saved, do, name):
    t = u.shape[0]
    n = t // CH

    def body(u_ref, w_ref, qk_ref, qd_ref, kd_ref, eg_ref, sv_ref, do_ref,
             du_ref, dw_ref, dqk_ref, dqd_ref, dkd_ref, deg_ref, ds_ref):
        @pl.when(pl.program_id(0) == 0)
        def _():
            ds_ref[...] = jnp.zeros_like(ds_ref)

        for h in range(NH):
            hc = slice(h * HD, (h + 1) * HD)
            _, vjp = jax.vjp(_state_step, sv_ref[0, h], u_ref[:, hc], w_ref[:, hc].astype(F32),
                             qk_ref[h].astype(F32), qd_ref[:, hc].astype(F32), kd_ref[:, hc].astype(F32),
                             eg_ref[0, h:h + 1, :])
            ds, du, dw, dqk, dqd, dkd, deg = vjp((ds_ref[h], do_ref[:, hc]))
            ds_ref[h] = ds
            du_ref[:, hc] = du
            dw_ref[:, hc] = dw
            dqk_ref[h] = dqk
            dqd_ref[:, hc] = dqd
            dkd_ref[:, hc] = dkd
            deg_ref[0, h:h + 1, :] = deg

    rev = lambda i: (n - 1 - i, 0)
    rev3 = lambda i: (n - 1 - i, 0, 0)
    row = pl.BlockSpec((CH, D), rev)
    qkb = pl.BlockSpec((NH, CH, CH), rev3)
    egb = pl.BlockSpec((1, NH, 128), rev3)
    return pl.pallas_call(
        body, grid=(n,),
        in_specs=[row, row, qkb, row, row, egb,
                  pl.BlockSpec((1, NH, HD, HD), lambda i: (n - 1 - i, 0, 0, 0)), row],
        out_specs=[row, row, qkb, row, row, egb],
        out_shape=[SDS((t, D), F32), SDS((t, D), F32), SDS((n * NH, CH, CH), F32), SDS((t, D), F32),
                   SDS((t, D), F32), SDS((n, NH, 128), F32)],
        scratch_shapes=[pltpu.VMEM((NH, HD, HD), F32)],
        compiler_params=_params(("arbitrary",)), name=name,
    )(u, w, qk, qd, kd, eg, saved, do)


def _ada_fwd(c_all, ada_w, ada_b, name):
    ncol = ada_w.shape[1]

    def body(c_ref, w_ref, b_ref, o_ref):
        o_ref[...] = _dg(_silu(c_ref[...]), w_ref[...], NN, HI) + b_ref[...]

    return pl.pallas_call(body, out_shape=SDS((NDEV, ncol), F32),
                          compiler_params=pltpu.CompilerParams(vmem_limit_bytes=VMEM_LIMIT), name=name,
                          )(c_all, ada_w, ada_b)


def _ada_bwd(c_all_t, dmod, name):
    ncol = dmod.shape[1]

    def body(c_ref, d_ref, o_ref):
        sc = _silu(c_ref[...])
        acc = sc[:, 0:1] * d_ref[0:1, :]
        for b in range(1, NDEV):
            acc = acc + sc[:, b:b + 1] * d_ref[b:b + 1, :]
        o_ref[...] = acc

    return pl.pallas_call(body, out_shape=SDS((D, ncol), F32),
                          compiler_params=pltpu.CompilerParams(vmem_limit_bytes=VMEM_LIMIT), name=name,
                          )(c_all_t, dmod)


def _sum_devices(parts, out_dtype, name):
    _, r, c = parts.shape
    tr = TR if r % TR == 0 else r

    def body(p_ref, o_ref):
        acc = p_ref[0].astype(F32)
        for i in range(1, NDEV):
            acc = acc + p_ref[i].astype(F32)
        o_ref[...] = acc.astype(o_ref.dtype)

    return pl.pallas_call(
        body, grid=(r // tr,), in_specs=[pl.BlockSpec((NDEV, tr, c), lambda i: (0, i, 0))],
        out_specs=pl.BlockSpec((tr, c), lambda i: (i, 0)), out_shape=SDS((r, c), out_dtype),
        compiler_params=_params(("parallel",)), name=name,
    )(parts)


def _adamw(w, g, m, v, name):
    r, c = w.shape
    tr = _pick(r, (256, 128, 88, 8)) if r % 8 == 0 else r
    bc1 = 1.0 - ADAM_B1 ** ADAM_STEP
    bc2 = 1.0 - ADAM_B2 ** ADAM_STEP

    def body(w_ref, g_ref, m_ref, v_ref, d_ref, nm_ref, nv_ref):
        gv = g_ref[...]
        m_new = ADAM_B1 * m_ref[...] + (1.0 - ADAM_B1) * gv
        v_new = ADAM_B2 * v_ref[...] + (1.0 - ADAM_B2) * (gv * gv)
        nm_ref[...] = m_new
        nv_ref[...] = v_new
        d_ref[...] = -ADAM_LR * ((m_new / bc1) / (jnp.sqrt(v_new / bc2) + ADAM_EPS) + ADAM_WD * w_ref[...])

    spec = pl.BlockSpec((tr, c), lambda i: (i, 0))
    return pl.pallas_call(
        body, grid=(r // tr,), in_specs=[spec] * 4, out_specs=[spec] * 3,
        out_shape=[SDS((r, c), F32)] * 3, compiler_params=_params(("parallel",)), name=name,
    )(w, g, m, v)


ANY = pl.BlockSpec(memory_space=pl.ANY)
MESH = pl.DeviceIdType.MESH


def _all_gather(x, name):
    r, c_dim = x.shape

    def body(x_ref, out_ref, send_sems, recv_sems, local_sem):
        mx, my, mc = lax.axis_index("x"), lax.axis_index("y"), lax.axis_index("c")
        me, sibling = (mx, my, mc), (mx, my, 1 - mc)
        chips = [(1 - mx, my), (mx, 1 - my), (1 - mx, 1 - my)]

        def rows(px, py, pc):
            return out_ref.at[4 * px + 2 * py + pc]

        def copy(k, block, to, src=None):
            return pltpu.make_async_remote_copy(
                src_ref=rows(*block) if src is None else src, dst_ref=rows(*block),
                send_sem=send_sems.at[k], recv_sem=recv_sems.at[k], device_id=to, device_id_type=MESH)

        mine = pltpu.make_async_copy(x_ref, rows(*me), local_sem)
        mine.start()
        first = [copy(0, me, sibling, src=x_ref)]
        first += [copy(1 + j, me, (*chip, mc), src=x_ref) for j, chip in enumerate(chips)]
        for cp in first:
            cp.start()
        passed = [copy(4 + j, (*chip, mc), sibling) for j, chip in enumerate(chips)]
        for j, chip in enumerate(chips):
            copy(1 + j, (*chip, mc), me).wait_recv()
            passed[j].start()
        copy(0, sibling, me).wait_recv()
        for j, chip in enumerate(chips):
            copy(4 + j, (*chip, 1 - mc), me).wait_recv()
        for cp in first + passed:
            cp.wait_send()
        mine.wait()

    return pl.pallas_call(
        body, out_shape=SDS((NDEV, r, c_dim), x.dtype), in_specs=[ANY], out_specs=ANY,
        scratch_shapes=[pltpu.SemaphoreType.DMA((7,)), pltpu.SemaphoreType.DMA((7,)), pltpu.SemaphoreType.DMA],
        name=name,
    )(x)


def _exchange_blocks(parts, name):
    _, r, c_dim = parts.shape

    def body(p_ref, out_ref, send_sems, recv_sems, local_sem):
        mx, my, mc = lax.axis_index("x"), lax.axis_index("y"), lax.axis_index("c")
        me = 4 * mx + 2 * my + mc
        mine = pltpu.make_async_copy(p_ref.at[me], out_ref.at[me], local_sem)
        mine.start()
        copies = []
        for k in range(1, NDEV):
            px = 1 - mx if k & 4 else mx
            py = 1 - my if k & 2 else my
            pc = 1 - mc if k & 1 else mc
            copies.append(pltpu.make_async_remote_copy(
                src_ref=p_ref.at[4 * px + 2 * py + pc], dst_ref=out_ref.at[me],
                send_sem=send_sems.at[k - 1], recv_sem=recv_sems.at[k - 1],
                device_id=(px, py, pc), device_id_type=MESH))
        for cp in copies:
            cp.start()
        for cp in copies:
            cp.wait_recv()
        for cp in copies:
            cp.wait_send()
        mine.wait()

    return pl.pallas_call(
        body, out_shape=SDS((NDEV, r, c_dim), parts.dtype), in_specs=[ANY], out_specs=ANY,
        scratch_shapes=[pltpu.SemaphoreType.DMA((7,)), pltpu.SemaphoreType.DMA((7,)), pltpu.SemaphoreType.DMA],
        name=name,
    )(parts)


def _cols_from_blocks(blocks, rows):
    w = blocks.shape[1] * blocks.shape[2] // rows
    return blocks.reshape(NDEV, rows, w).transpose(1, 0, 2).reshape(rows, NDEV * w)


def _cols_to_blocks(full):
    rows, total = full.shape
    w = total // NDEV
    return full.reshape(rows, NDEV, w).transpose(1, 0, 2).reshape(NDEV, rows * w // D, D)


def _mix_pad(w):
    rows = w.shape[0]
    xp, q, k, v, z, b, a, gp, gd = jnp.split(w, (512, 1536, 2560, 3584, 4608, 4616, 4624, 5648), axis=1)
    pad = jnp.zeros((rows, MIXP - OFF_BA - 16), w.dtype)
    return jnp.concatenate([q, k, v, z, gp, gd, xp, b, a, pad], axis=1)


def _mix_unpad(w):
    q, k, v, z, gp, gd, xp, b, a = (w[:, OFF_Q:OFF_K], w[:, OFF_K:OFF_V], w[:, OFF_V:OFF_Z], w[:, OFF_Z:OFF_GP],
                                    w[:, OFF_GP:OFF_GD], w[:, OFF_GD:OFF_XP], w[:, OFF_XP:OFF_BA],
                                    w[:, OFF_BA:OFF_BA + 8], w[:, OFF_BA + 8:OFF_BA + 16])
    return jnp.concatenate([xp, q, k, v, z, b, a, gp, gd], axis=1)


def _lane_row(vec8):
    return jnp.zeros((1, 128), F32).at[0, NH:2 * NH].set(vec8)


def _ffn_fwd(x, g, shift, scale, gate, w_in, w_out, tag):
    h = _norm_mod_fwd(x, g, shift, scale, f"{tag}_norm")
    u = _matmul(h, w_in, out_dtype=F32, name=f"{tag}_up")
    a = _swiglu_fwd(u, f"{tag}_act")
    y = _matmul(a, w_out, out_dtype=F32, name=f"{tag}_down")
    return _resid_fwd(x, y, gate, 0.5, f"{tag}_res"), (h, u, a, y)


def _ffn_bwd(dx_out, x, g, scale, gate, w_in, w_out, saved, tag):
    h, u, a, y = saved
    dy, dgate = _resid_bwd(dx_out, y, gate, 0.5, f"{tag}_res_bwd")
    da = _matmul(dy, w_out, tb=True, out_dtype=F32, name=f"{tag}_down_dx")
    dw_out = _matmul(a, dy, ta=True, out_dtype=BF16, name=f"{tag}_down_dw")
    du = _swiglu_bwd(u, da, f"{tag}_act_bwd")
    dh = _matmul(du, w_in, tb=True, out_dtype=F32, name=f"{tag}_up_dx")
    dw_in = _matmul(h, du, ta=True, out_dtype=BF16, name=f"{tag}_up_dw")
    dx, dshift, dscale, dg = _norm_mod_bwd(x, g, scale, dh, dx_out, f"{tag}_norm_bwd")
    return dx, (dshift, dscale, dgate), dg, dw_in, dw_out


def kernel(x, c, ada_w, ada_b, norm_g, ffn1_w_in, ffn1_w_out, ffn2_w_in, ffn2_w_out, mix_w_in, conv_w, a_log, dt_bias, dn_norm_g, pool_w, pool_scale, pool_proj, dn_proj, mix_w_out, final_g, loss_target, m_ada_w, m_ada_b, m_norm_g, m_ffn1_w_in, m_ffn1_w_out, m_ffn2_w_in, m_ffn2_w_out, m_mix_w_in, m_conv_w, m_a_log, m_dt_bias, m_dn_norm_g, m_pool_w, m_pool_scale, m_pool_proj, m_dn_proj, m_mix_w_out, m_final_g, v_ada_w, v_ada_b, v_norm_g, v_ffn1_w_in, v_ffn1_w_out, v_ffn2_w_in, v_ffn2_w_out, v_mix_w_in, v_conv_w, v_a_log, v_dt_bias, v_dn_norm_g, v_pool_w, v_pool_scale, v_pool_proj, v_dn_proj, v_mix_w_out, v_final_g):
    me = 4 * lax.axis_index("x") + 2 * lax.axis_index("y") + lax.axis_index("c")
    x0 = x[0]
    target = loss_target[0]
    t = x0.shape[0]

    big = [ffn1_w_in[0], ffn1_w_out[0], ffn2_w_in[0], ffn2_w_out[0], mix_w_in[0], pool_proj[0], dn_proj[0],
           mix_w_out[0]]
    sizes = [w.size // D for w in big]
    offs = [0]
    for s in sizes:
        offs.append(offs[-1] + s)
    blob_pad = -offs[-1] % TR
    blob = jnp.concatenate([w.astype(BF16).reshape(-1, D) for w in big] + [jnp.zeros((blob_pad, D), BF16)], axis=0)
    gathered = _all_gather(blob, "gather_weights")
    seg = [gathered[:, offs[i]:offs[i + 1], :] for i in range(len(big))]
    w_in1 = _cols_from_blocks(seg[0], D)
    w_out1 = seg[1].reshape(FH, D)
    w_in2 = _cols_from_blocks(seg[2], D)
    w_out2 = seg[3].reshape(FH, D)
    w_mix = _mix_pad(_cols_from_blocks(seg[4], D))
    w_pp = _cols_from_blocks(seg[5], PW)
    w_dn = seg[6].reshape(D, D)
    w_mo = seg[7].reshape(D, D)

    small = jnp.concatenate([c.reshape(8, 128), conv_w[0].reshape(12, 128), norm_g[0].reshape(3, 128),
                             jnp.zeros((1, 128), F32)], axis=0)
    small_all = _all_gather(small, "gather_small")
    c_all = small_all[:, 0:8, :].reshape(NDEV, D)
    conv_full = small_all[:, 8:20, :].reshape(NDEV, 4, 384).transpose(1, 0, 2).reshape(4, 3 * D)
    norm_full = small_all[:, 20:23, :].reshape(NDEV, 3, 128).transpose(1, 0, 2).reshape(3, D)

    ncol = ada_w.shape[2]
    ada_b_mine = lax.dynamic_slice(ada_b, (0, me * ncol), (1, ncol))
    mod_cols = _ada_fwd(c_all, ada_w[0], ada_b_mine, "ada_fwd")
    mod_all = _all_gather(mod_cols, "gather_mod")
    mod = lax.dynamic_index_in_dim(mod_all, me, axis=1, keepdims=False).reshape(9, D)
    shift = [mod[3 * s:3 * s + 1] for s in range(3)]
    scale = [mod[3 * s + 1:3 * s + 2] for s in range(3)]
    gate = [mod[3 * s + 2:3 * s + 3] for s in range(3)]
    ng = [norm_full[s:s + 1] for s in range(3)]
    fg = final_g.reshape(1, D)
    al_row = _lane_row(a_log[0])
    dt_row = _lane_row(dt_bias[0])
    gn = dn_norm_g
    pw = pool_w[0]
    ps = pool_scale

    x1, saved1 = _ffn_fwd(x0, ng[0], shift[0], scale[0], gate[0], w_in1, w_out1, "ffn1")

    h1 = _norm_mod_fwd(x1, ng[1], shift[1], scale[1], "mix_norm")
    proj = _matmul(h1, w_mix, out_dtype=F32, name="mix_in")
    ya = _pool_fwd(proj, pw, ps, w_pp, "pool_fwd")
    qh, kh, vh, bg = _dn_pre_fwd(proj, conv_full, al_row, dt_row, "dn_pre")
    u, w, qk, qd, kd, eg = _dn_local_fwd(qh, kh, vh, bg, "dn_local")
    o, s_saved = _dn_scan_fwd(u, w, qk, qd, kd, eg, "dn_scan")
    ob = _dn_post_fwd(o, proj, gn, "dn_post")
    yb = _matmul(ob, w_dn, out_dtype=F32, name="dn_out")
    merged = _merge_fwd(ya, yb, proj, "merge")
    mix_y = _matmul(merged, w_mo, out_dtype=F32, name="mix_out")
    x2 = _resid_fwd(x1, mix_y, gate[1], 1.0, "mix_res")

    x3, saved2 = _ffn_fwd(x2, ng[2], shift[2], scale[2], gate[2], w_in2, w_out2, "ffn2")
    loss_row, dx3, dfg = _final_loss(x3, fg, target, "loss")

    dx2, dmod2, dng2, dw_in2, dw_out2 = _ffn_bwd(dx3, x2, ng[2], scale[2], gate[2], w_in2, w_out2, saved2, "ffn2")

    dmy, dgate1 = _resid_bwd(dx2, mix_y, gate[1], 1.0, "mix_res_bwd")
    dmerged = _matmul(dmy, w_mo, tb=True, out_dtype=F32, name="mix_out_dx")
    dw_mo = _matmul(merged, dmy, ta=True, out_dtype=BF16, name="mix_out_dw")
    dya, dyb, dgp, dgd = _merge_bwd(dmerged, ya, yb, proj, "merge_bwd")
    dob = _matmul(dyb, w_dn, tb=True, out_dtype=F32, name="dn_out_dx")
    dw_dn = _matmul(ob, dyb, ta=True, out_dtype=BF16, name="dn_out_dw")
    do, dz, dgn = _dn_post_bwd(o, proj, gn, dob, "dn_post_bwd")
    du, dw, dqk, dqd, dkd, deg = _dn_scan_bwd(u, w, qk, qd, kd, eg, s_saved, do, "dn_scan_bwd")
    dqh, dkh, dvh, dbg = _dn_local_bwd(qh, kh, vh, bg, du, dw, dqk, dqd, dkd, deg, "dn_local_bwd")
    dconv, draw, dal, ddt = _dn_pre_bwd_act(proj, conv_full, al_row, dt_row, dqh, dkh, dvh, dbg, "dn_pre_bwd_act")
    dqkv, dcw = _dn_pre_bwd_conv(proj, conv_full, dconv, "dn_pre_bwd_conv")
    dwin, dpl, dpw, dps, dpp = _pool_bwd_local(proj, pw, ps, w_pp, dya, "pool_bwd_local")
    dxp = _pool_bwd_window(dwin, dpl, "pool_bwd_window")
    dproj = jnp.concatenate([dqkv, dz, dgp, dgd, dxp, draw, jnp.zeros((t, MIXP - OFF_BA - 128), BF16)], axis=1)
    dh1 = _matmul(dproj, w_mix, tb=True, out_dtype=F32, name="mix_in_dx")
    dw_mix = _matmul(h1, dproj, ta=True, out_dtype=BF16, name="mix_in_dw")
    dx1, dsh1, dsc1, dng1 = _norm_mod_bwd(x1, ng[1], scale[1], dh1, dx2, "mix_norm_bwd")

    dx0, dmod0, dng0, dw_in1, dw_out1 = _ffn_bwd(dx1, x0, ng[0], scale[0], gate[0], w_in1, w_out1, saved1, "ffn1")

    parts = jnp.concatenate([
        _cols_to_blocks(dw_in1), dw_out1.reshape(NDEV, -1, D), _cols_to_blocks(dw_in2),
        dw_out2.reshape(NDEV, -1, D), _cols_to_blocks(_mix_unpad(dw_mix)), _cols_to_blocks(dpp.astype(BF16)),
        dw_dn.reshape(NDEV, -1, D), dw_mo.reshape(NDEV, -1, D), jnp.zeros((NDEV, blob_pad, D), BF16)], axis=1)
    received = _exchange_blocks(parts, "scatter_grads")
    gsum = _sum_devices(received, F32, "sum_grads")
    gshard = [gsum[offs[i]:offs[i + 1]].reshape(big[i].shape) for i in range(len(big))]

    dmod = jnp.concatenate([*dmod0, dsh1, dsc1, dgate1, *dmod2], axis=1).reshape(-1)
    flat = jnp.concatenate([
        dmod, dal[0, NH:2 * NH], ddt[0, NH:2 * NH], dgn.reshape(-1), dps.reshape(-1), dfg.reshape(-1),
        dpw.reshape(-1), jnp.concatenate([dng0, dng1, dng2], axis=0).reshape(-1), dcw.reshape(-1)])
    nflat = 90 * D
    flat = jnp.concatenate([flat, jnp.zeros((nflat - flat.shape[0],), F32)]).reshape(90, D)
    flat_all = _all_gather(flat, "gather_small_grads")
    tot = _sum_devices(flat_all, F32, "sum_small_grads").reshape(-1)
    dmod_all = flat_all.reshape(NDEV, nflat)[:, :9 * D]
    dmod_cols = lax.dynamic_slice(dmod_all, (0, me * ncol), (NDEV, ncol))
    g_ada_w = _ada_bwd(c_all.T, dmod_cols, "ada_bwd")

    p = 0
    pieces = {}
    for nm, size in (("ada_b", 9 * D), ("a_log", NH), ("dt_bias", NH), ("dn_norm_g", HD), ("pool_scale", PW),
                     ("final_g", D), ("pool_w", 4 * PG * PG), ("norm_g", 3 * D), ("conv_w", 12 * D)):
        pieces[nm] = tot[p:p + size]
        p += size
    g_norm = lax.dynamic_slice(pieces["norm_g"].reshape(3, D), (0, me * 128), (3, 128))
    g_conv = lax.dynamic_slice(pieces["conv_w"].reshape(4, 3 * D), (0, me * 384), (4, 384))

    grads = {
        "ada_w": g_ada_w.reshape(ada_w.shape), "ada_b": pieces["ada_b"].reshape(ada_b.shape),
        "norm_g": g_norm.reshape(norm_g.shape),
        "ffn1_w_in": gshard[0].reshape(ffn1_w_in.shape), "ffn1_w_out": gshard[1].reshape(ffn1_w_out.shape),
        "ffn2_w_in": gshard[2].reshape(ffn2_w_in.shape), "ffn2_w_out": gshard[3].reshape(ffn2_w_out.shape),
        "mix_w_in": gshard[4].reshape(mix_w_in.shape), "conv_w": g_conv.reshape(conv_w.shape),
        "a_log": pieces["a_log"].reshape(a_log.shape), "dt_bias": pieces["dt_bias"].reshape(dt_bias.shape),
        "dn_norm_g": pieces["dn_norm_g"].reshape(dn_norm_g.shape), "pool_w": pieces["pool_w"].reshape(pool_w.shape),
        "pool_scale": pieces["pool_scale"].reshape(pool_scale.shape),
        "pool_proj": gshard[5].reshape(pool_proj.shape), "dn_proj": gshard[6].reshape(dn_proj.shape),
        "mix_w_out": gshard[7].reshape(mix_w_out.shape), "final_g": pieces["final_g"].reshape(final_g.shape),
    }
    weights = {"ada_w": ada_w, "ada_b": ada_b, "norm_g": norm_g, "ffn1_w_in": ffn1_w_in, "ffn1_w_out": ffn1_w_out,
               "ffn2_w_in": ffn2_w_in, "ffn2_w_out": ffn2_w_out, "mix_w_in": mix_w_in, "conv_w": conv_w,
               "a_log": a_log, "dt_bias": dt_bias, "dn_norm_g": dn_norm_g, "pool_w": pool_w,
               "pool_scale": pool_scale, "pool_proj": pool_proj, "dn_proj": dn_proj, "mix_w_out": mix_w_out,
               "final_g": final_g}
    m_in = {"ada_w": m_ada_w, "ada_b": m_ada_b, "norm_g": m_norm_g, "ffn1_w_in": m_ffn1_w_in,
            "ffn1_w_out": m_ffn1_w_out, "ffn2_w_in": m_ffn2_w_in, "ffn2_w_out": m_ffn2_w_out,
            "mix_w_in": m_mix_w_in, "conv_w": m_conv_w, "a_log": m_a_log, "dt_bias": m_dt_bias,
            "dn_norm_g": m_dn_norm_g, "pool_w": m_pool_w, "pool_scale": m_pool_scale, "pool_proj": m_pool_proj,
            "dn_proj": m_dn_proj, "mix_w_out": m_mix_w_out, "final_g": m_final_g}
    v_in = {"ada_w": v_ada_w, "ada_b": v_ada_b, "norm_g": v_norm_g, "ffn1_w_in": v_ffn1_w_in,
            "ffn1_w_out": v_ffn1_w_out, "ffn2_w_in": v_ffn2_w_in, "ffn2_w_out": v_ffn2_w_out,
            "mix_w_in": v_mix_w_in, "conv_w": v_conv_w, "a_log": v_a_log, "dt_bias": v_dt_bias,
            "dn_norm_g": v_dn_norm_g, "pool_w": v_pool_w, "pool_scale": v_pool_scale, "pool_proj": v_pool_proj,
            "dn_proj": v_dn_proj, "mix_w_out": v_mix_w_out, "final_g": v_final_g}

    names = list(weights)
    large = ("ada_w", "ffn1_w_in", "ffn1_w_out", "ffn2_w_in", "ffn2_w_out", "mix_w_in", "pool_proj", "dn_proj",
             "mix_w_out")
    delta, new_m, new_v = {}, {}, {}
    for nm in large:
        shp = weights[nm].shape
        two_d = (shp[-2], shp[-1])
        d_, m_, v_ = _adamw(weights[nm].reshape(two_d), grads[nm].reshape(two_d), m_in[nm].reshape(two_d),
                            v_in[nm].reshape(two_d), f"adamw_{nm}")
        delta[nm], new_m[nm], new_v[nm] = d_.reshape(shp), m_.reshape(shp), v_.reshape(shp)
    rest = [nm for nm in names if nm not in large]
    total = sum(weights[nm].size for nm in rest)
    padded = -(-total // D) * D

    def pack(tree, fill):
        flat_ = jnp.concatenate([tree[nm].reshape(-1) for nm in rest])
        return jnp.concatenate([flat_, jnp.full((padded - total,), fill, F32)]).reshape(-1, D)

    d_, m_, v_ = _adamw(pack(weights, 0.0), pack(grads, 0.0), pack(m_in, 0.0), pack(v_in, 1.0), "adamw_small")
    p = 0
    for nm in rest:
        size = weights[nm].size
        shp = weights[nm].shape
        delta[nm] = d_.reshape(-1)[p:p + size].reshape(shp)
        new_m[nm] = m_.reshape(-1)[p:p + size].reshape(shp)
        new_v[nm] = v_.reshape(-1)[p:p + size].reshape(shp)
        p += size

    loss = lax.psum(loss_row[0, 0], ("x", "y", "c"))
    grad_x = dx0.reshape(x.shape)
    return (loss, grad_x, *[grads[nm] for nm in names], *[delta[nm] for nm in names],
            *[new_m[nm] for nm in names], *[new_v[nm] for nm in names])
```

```python
import functools

import jax
import jax.numpy as jnp
from jax import lax
from jax.experimental import pallas as pl
from jax.experimental.pallas import tpu as pltpu

F32 = jnp.float32
BF16 = jnp.bfloat16
SDS = jax.ShapeDtypeStruct
HI = lax.Precision.HIGHEST

D = 1024
FH = 2816
NH = 8
HD = 128
CH = 64
NDEV = 8
PW = 512
PG = 128
RMS_EPS = 1e-6
L2_EPS = 1e-6
TR = 256
HALO = 16
VMEM_LIMIT = 56 * 1024 * 1024

MIXP = 6912
OFF_Q, OFF_K, OFF_V, OFF_Z, OFF_GP, OFF_GD, OFF_XP, OFF_BA = 0, 1024, 2048, 3072, 4096, 5120, 6144, 6656
MIX_RAW = 6672

ADAM_LR = 0.001
ADAM_B1 = 0.9
ADAM_B2 = 0.999
ADAM_EPS = 1e-08
ADAM_WD = 0.01
ADAM_STEP = 10

NN = (((1,), (0,)), ((), ()))
NT = (((1,), (1,)), ((), ()))
TN = (((0,), (0,)), ((), ()))


def _dg(a, b, dims, prec=None):
    return lax.dot_general(a, b, dims, precision=prec, preferred_element_type=F32)


def _make_dots(prec):
    @jax.custom_vjp
    def nn(a, b):
        return _dg(a, b, NN, prec)

    @jax.custom_vjp
    def nt(a, b):
        return _dg(a, b, NT, prec)

    @jax.custom_vjp
    def tn(a, b):
        return _dg(a, b, TN, prec)

    nn.defvjp(lambda a, b: (nn(a, b), (a, b)), lambda r, d: (nt(d, r[1]), tn(r[0], d)))
    nt.defvjp(lambda a, b: (nt(a, b), (a, b)), lambda r, d: (nn(d, r[1]), tn(d, r[0])))
    tn.defvjp(lambda a, b: (tn(a, b), (a, b)), lambda r, d: (nt(r[1], d), nn(r[0], d)))
    return nn, nt, tn


_nn, _nt, _tn = _make_dots(None)


def _params(sem):
    return pltpu.CompilerParams(dimension_semantics=sem, vmem_limit_bytes=VMEM_LIMIT)


def _sigmoid(x):
    return 1.0 / (1.0 + jnp.exp(-x))


def _silu(x):
    return x * _sigmoid(x)


def _dsilu(x):
    s = _sigmoid(x)
    return s * (1.0 + x * (1.0 - s))


def _pick(n, cands):
    for c in cands:
        if n % c == 0:
            return c
    raise ValueError(f"no tile for {n}")


def _iota(shape, dim):
    return lax.broadcasted_iota(jnp.int32, shape, dim)


def _matmul(a, b, *, ta=False, tb=False, out_dtype, name):
    if ta:
        k_dim, m_dim = a.shape
    else:
        m_dim, k_dim = a.shape
    n_dim = b.shape[0] if tb else b.shape[1]
    tm = _pick(m_dim, (512, 256, 128))
    tn = _pick(n_dim, (512, 768, 256, 128))
    tk = k_dim if (k_dim <= 2816 and not ta) else _pick(k_dim, (2816, 2304, 1024, 512, 256))
    nk = k_dim // tk
    dims = ((((0,) if ta else (1,)), ((1,) if tb else (0,))), ((), ()))

    def body(a_ref, b_ref, o_ref, acc_ref):
        k = pl.program_id(2)

        @pl.when(k == 0)
        def _():
            acc_ref[...] = jnp.zeros_like(acc_ref)

        acc_ref[...] += lax.dot_general(a_ref[...].astype(BF16), b_ref[...].astype(BF16), dims,
                                        preferred_element_type=F32)

        @pl.when(k == nk - 1)
        def _():
            o_ref[...] = acc_ref[...].astype(o_ref.dtype)

    a_spec = (pl.BlockSpec((tk, tm), lambda i, j, k: (k, i)) if ta
              else pl.BlockSpec((tm, tk), lambda i, j, k: (i, k)))
    b_spec = (pl.BlockSpec((tn, tk), lambda i, j, k: (j, k)) if tb
              else pl.BlockSpec((tk, tn), lambda i, j, k: (k, j)))
    return pl.pallas_call(
        body, grid=(m_dim // tm, n_dim // tn, nk),
        in_specs=[a_spec, b_spec],
        out_specs=pl.BlockSpec((tm, tn), lambda i, j, k: (i, j)),
        out_shape=SDS((m_dim, n_dim), out_dtype),
        scratch_shapes=[pltpu.VMEM((tm, tn), F32)],
        compiler_params=_params(("parallel", "parallel", "arbitrary")),
        name=name,
    )(a, b)


def _row(width, col=0):
    return pl.BlockSpec((TR, width), lambda i: (i, col))


def _vec(width):
    return pl.BlockSpec((1, width), lambda i: (0, 0))


def _norm_mod_fwd(x, g, shift, scale, name):
    t = x.shape[0]

    def body(x_ref, g_ref, sh_ref, sc_ref, o_ref):
        xv = x_ref[...]
        r = lax.rsqrt(jnp.mean(xv * xv, axis=-1, keepdims=True) + RMS_EPS)
        o_ref[...] = (((xv * r) * g_ref[...]) * (1.0 + sc_ref[...]) + sh_ref[...]).astype(o_ref.dtype)

    return pl.pallas_call(
        body, grid=(t // TR,), in_specs=[_row(D), _vec(D), _vec(D), _vec(D)], out_specs=_row(D),
        out_shape=SDS((t, D), BF16), compiler_params=_params(("parallel",)), name=name,
    )(x, g, shift, scale)


def _norm_mod_bwd(x, g, scale, dh, dx_in, name):
    t = x.shape[0]

    def body(x_ref, g_ref, sc_ref, dh_ref, dxi_ref, dx_ref, dsh_ref, dsc_ref, dg_ref):
        @pl.when(pl.program_id(0) == 0)
        def _():
            dsh_ref[...] = jnp.zeros_like(dsh_ref)
            dsc_ref[...] = jnp.zeros_like(dsc_ref)
            dg_ref[...] = jnp.zeros_like(dg_ref)

        xv = x_ref[...]
        gv = g_ref[...]
        dh = dh_ref[...]
        r = lax.rsqrt(jnp.mean(xv * xv, axis=-1, keepdims=True) + RMS_EPS)
        n = xv * r
        dsh_ref[...] += jnp.sum(dh, axis=0, keepdims=True)
        dsc_ref[...] += jnp.sum(dh * (n * gv), axis=0, keepdims=True)
        tt = dh * (1.0 + sc_ref[...])
        dg_ref[...] += jnp.sum(tt * n, axis=0, keepdims=True)
        dn = tt * gv
        dx_ref[...] = dxi_ref[...] + r * (dn - n * jnp.mean(dn * n, axis=-1, keepdims=True))

    return pl.pallas_call(
        body, grid=(t // TR,), in_specs=[_row(D), _vec(D), _vec(D), _row(D), _row(D)],
        out_specs=[_row(D), _vec(D), _vec(D), _vec(D)],
        out_shape=[SDS((t, D), F32), SDS((1, D), F32), SDS((1, D), F32), SDS((1, D), F32)],
        compiler_params=_params(("arbitrary",)), name=name,
    )(x, g, scale, dh, dx_in)


def _swiglu_fwd(u, name):
    t = u.shape[0]

    def body(g_ref, u_ref, o_ref):
        o_ref[...] = (_silu(g_ref[...]) * u_ref[...]).astype(o_ref.dtype)

    return pl.pallas_call(
        body, grid=(t // TR,), in_specs=[_row(FH, 0), _row(FH, 1)], out_specs=_row(FH),
        out_shape=SDS((t, FH), BF16), compiler_params=_params(("parallel",)), name=name,
    )(u, u)


def _swiglu_bwd(u, da, name):
    t = u.shape[0]

    def body(g_ref, u_ref, da_ref, o_ref):
        gv = g_ref[...]
        dav = da_ref[...]
        o_ref[:, :FH] = (dav * u_ref[...] * _dsilu(gv)).astype(o_ref.dtype)
        o_ref[:, FH:] = (dav * _silu(gv)).astype(o_ref.dtype)

    return pl.pallas_call(
        body, grid=(t // TR,), in_specs=[_row(FH, 0), _row(FH, 1), _row(FH)], out_specs=_row(2 * FH),
        out_shape=SDS((t, 2 * FH), BF16), compiler_params=_params(("parallel",)), name=name,
    )(u, u, da)


def _resid_fwd(x, y, gate, coef, name):
    t = x.shape[0]

    def body(x_ref, y_ref, g_ref, o_ref):
        o_ref[...] = x_ref[...] + (coef * g_ref[...]) * y_ref[...]

    return pl.pallas_call(
        body, grid=(t // TR,), in_specs=[_row(D), _row(D), _vec(D)], out_specs=_row(D),
        out_shape=SDS((t, D), F32), compiler_params=_params(("parallel",)), name=name,
    )(x, y, gate)


def _resid_bwd(dx, y, gate, coef, name):
    t = dx.shape[0]

    def body(dx_ref, y_ref, g_ref, dy_ref, dg_ref):
        @pl.when(pl.program_id(0) == 0)
        def _():
            dg_ref[...] = jnp.zeros_like(dg_ref)

        dxv = dx_ref[...]
        dy_ref[...] = ((coef * g_ref[...]) * dxv).astype(dy_ref.dtype)
        dg_ref[...] += jnp.sum((coef * dxv) * y_ref[...], axis=0, keepdims=True)

    return pl.pallas_call(
        body, grid=(t // TR,), in_specs=[_row(D), _row(D), _vec(D)], out_specs=[_row(D), _vec(D)],
        out_shape=[SDS((t, D), BF16), SDS((1, D), F32)],
        compiler_params=_params(("arbitrary",)), name=name,
    )(dx, y, gate)


def _final_loss(x, fg, target, name):
    t = x.shape[0]
    nt = t // TR

    def body(x_ref, g_ref, t_ref, loss_ref, dx_ref, dg_ref, acc_ref):
        i = pl.program_id(0)

        @pl.when(i == 0)
        def _():
            acc_ref[...] = jnp.zeros_like(acc_ref)
            dg_ref[...] = jnp.zeros_like(dg_ref)

        xv = x_ref[...]
        gv = g_ref[...]
        r = lax.rsqrt(jnp.mean(xv * xv, axis=-1, keepdims=True) + RMS_EPS)
        n = xv * r
        err = n * gv - t_ref[...]
        acc_ref[...] += jnp.sum(err * err, axis=0, keepdims=True)
        dy = err * (1.0 / D)
        dg_ref[...] += jnp.sum(dy * n, axis=0, keepdims=True)
        dn = dy * gv
        dx_ref[...] = r * (dn - n * jnp.mean(dn * n, axis=-1, keepdims=True))

        @pl.when(i == nt - 1)
        def _():
            tot = jnp.sum(acc_ref[...], axis=1, keepdims=True) * (0.5 / D)
            loss_ref[...] = jnp.broadcast_to(tot, loss_ref.shape)

    return pl.pallas_call(
        body, grid=(nt,), in_specs=[_row(D), _vec(D), _row(D)],
        out_specs=[_vec(128), _row(D), _vec(D)],
        out_shape=[SDS((1, 128), F32), SDS((t, D), F32), SDS((1, D), F32)],
        scratch_shapes=[pltpu.VMEM((1, D), F32)],
        compiler_params=_params(("arbitrary",)), name=name,
    )(x, fg, target)


def _halo_prev(width, col):
    per = TR // HALO
    return pl.BlockSpec((HALO, width), lambda i: (jnp.maximum(i * per - 1, 0), col))


def _halo_next(width, col, nt):
    per = TR // HALO
    return pl.BlockSpec((HALO, width), lambda i: (jnp.minimum((i + 1) * per, nt * per - 1), col))


def _pool_windows(ext, tile_index):
    rows = _iota((TR, PG), 0) + tile_index * TR + 1
    pooled, counts = [], []
    for gi in range(4):
        w = 2 << gi
        e = ext[:, gi * PG:(gi + 1) * PG]
        s = e
        step = 1
        while step < w:
            s = s + pltpu.roll(s, step, 0)
            step *= 2
        cnt = jnp.minimum(rows, w).astype(F32)
        pooled.append(s[HALO:] / cnt - e[HALO:])
        counts.append(cnt)
    return pooled, counts


def _pool_fwd(proj, pool_w, pool_scale, pool_proj, name):
    t = proj.shape[0]
    xcol = OFF_XP // PW

    def body(x_ref, h_ref, pw_ref, ps_ref, pp_ref, o_ref):
        i = pl.program_id(0)
        halo = jnp.where(i > 0, h_ref[...], 0.0)
        ext = jnp.concatenate([halo, x_ref[...]], axis=0)
        pooled, _ = _pool_windows(ext, i)
        mixed = [_dg(pooled[g].astype(BF16), pw_ref[g].astype(BF16), NN) for g in range(4)]
        ypre = jnp.concatenate(mixed, axis=1) * ps_ref[...]
        o_ref[...] = _dg(ypre.astype(BF16), pp_ref[...], NN)

    return pl.pallas_call(
        body, grid=(t // TR,),
        in_specs=[_row(PW, xcol), _halo_prev(PW, xcol),
                  pl.BlockSpec((4, PG, PG), lambda i: (0, 0, 0)), _vec(PW),
                  pl.BlockSpec((PW, D), lambda i: (0, 0))],
        out_specs=_row(D), out_shape=SDS((t, D), F32),
        compiler_params=_params(("parallel",)), name=name,
    )(proj, proj, pool_w, pool_scale, pool_proj)


def _pool_bwd_local(proj, pool_w, pool_scale, pool_proj, dya, name):
    t = proj.shape[0]
    xcol = OFF_XP // PW

    def body(x_ref, h_ref, pw_ref, ps_ref, pp_ref, dya_ref, dwin_ref, dpl_ref, dpw_ref, dps_ref, dpp_ref):
        i = pl.program_id(0)

        @pl.when(i == 0)
        def _():
            dpw_ref[...] = jnp.zeros_like(dpw_ref)
            dps_ref[...] = jnp.zeros_like(dps_ref)
            dpp_ref[...] = jnp.zeros_like(dpp_ref)

        halo = jnp.where(i > 0, h_ref[...], 0.0)
        ext = jnp.concatenate([halo, x_ref[...]], axis=0)
        pooled, counts = _pool_windows(ext, i)
        mixed = jnp.concatenate(
            [_dg(pooled[g].astype(BF16), pw_ref[g].astype(BF16), NN) for g in range(4)], axis=1)
        ps = ps_ref[...]
        ypre = mixed * ps
        dyab = dya_ref[...].astype(BF16)
        dypre = _dg(dyab, pp_ref[...], NT)
        dpp_ref[...] += _dg(ypre.astype(BF16), dyab, TN)
        dps_ref[...] += jnp.sum(dypre * mixed, axis=0, keepdims=True)
        dmixed = dypre * ps
        for g in range(4):
            dm = dmixed[:, g * PG:(g + 1) * PG].astype(BF16)
            dpw_ref[g] += _dg(pooled[g].astype(BF16), dm, TN)
            dpooled = _dg(dm, pw_ref[g].astype(BF16), NT)
            dwin_ref[:, g * PG:(g + 1) * PG] = dpooled / counts[g]
            dpl_ref[:, g * PG:(g + 1) * PG] = dpooled

    return pl.pallas_call(
        body, grid=(t // TR,),
        in_specs=[_row(PW, xcol), _halo_prev(PW, xcol),
                  pl.BlockSpec((4, PG, PG), lambda i: (0, 0, 0)), _vec(PW),
                  pl.BlockSpec((PW, D), lambda i: (0, 0)), _row(D)],
        out_specs=[_row(PW), _row(PW), pl.BlockSpec((4, PG, PG), lambda i: (0, 0, 0)), _vec(PW),
                   pl.BlockSpec((PW, D), lambda i: (0, 0))],
        out_shape=[SDS((t, PW), F32), SDS((t, PW), F32), SDS((4, PG, PG), F32), SDS((1, PW), F32),
                   SDS((PW, D), F32)],
        compiler_params=_params(("arbitrary",)), name=name,
    )(proj, proj, pool_w, pool_scale, pool_proj, dya)


def _pool_bwd_window(dwin, dpl, name):
    t = dwin.shape[0]
    nt = t // TR
    ext_rows = TR + HALO

    def body(dw_ref, h_ref, dp_ref, o_ref):
        i = pl.program_id(0)
        halo = jnp.where(i < nt - 1, h_ref[...], 0.0)
        ext = jnp.concatenate([dw_ref[...], halo], axis=0)
        for gi in range(4):
            w = 2 << gi
            s = ext[:, gi * PG:(gi + 1) * PG]
            step = 1
            while step < w:
                s = s + pltpu.roll(s, ext_rows - step, 0)
                step *= 2
            o_ref[:, gi * PG:(gi + 1) * PG] = (s[:TR] - dp_ref[:, gi * PG:(gi + 1) * PG]).astype(o_ref.dtype)

    return pl.pallas_call(
        body, grid=(nt,), in_specs=[_row(PW), _halo_next(PW, 0, nt), _row(PW)], out_specs=_row(PW),
        out_shape=SDS((t, PW), BF16), compiler_params=_params(("parallel",)), name=name,
    )(dwin, dwin, dpl)


def _conv_group(ext, cw_ref, cols):
    acc = cw_ref[3:4, cols] * ext
    for j in range(3):
        acc = acc + cw_ref[j:j + 1, cols] * pltpu.roll(ext, 3 - j, 0)
    return acc[HALO:]


def _gate_terms(raw, al, dt):
    beta = _sigmoid(raw)
    xg = raw + dt
    sp = jnp.maximum(xg, 0.0) + jnp.log(1.0 + jnp.exp(-jnp.abs(xg)))
    g = -jnp.exp(al) * sp
    return beta, g, _sigmoid(xg)


def _dn_pre_fwd(proj, conv_w, al_row, dt_row, name):
    t = proj.shape[0]

    def body(x_ref, h_ref, cw_ref, ba_ref, al_ref, dt_ref, q_ref, k_ref, v_ref, bg_ref):
        i = pl.program_id(0)
        keep = i > 0
        for grp in range(24):
            cols = slice(grp * HD, (grp + 1) * HD)
            ext = jnp.concatenate([jnp.where(keep, h_ref[:, cols], 0.0), x_ref[:, cols]], axis=0)
            s = _silu(_conv_group(ext, cw_ref, cols))
            seg, head = divmod(grp, NH)
            hc = slice(head * HD, (head + 1) * HD)
            if seg == 0:
                q_ref[:, hc] = s * lax.rsqrt(jnp.sum(s * s, axis=-1, keepdims=True) + L2_EPS) * (HD ** -0.5)
            elif seg == 1:
                k_ref[:, hc] = s * lax.rsqrt(jnp.sum(s * s, axis=-1, keepdims=True) + L2_EPS)
            else:
                v_ref[:, hc] = s
        lane = _iota((TR, 128), 1)
        rowc = _iota((TR, 128), 0) % CH
        beta, g, _ = _gate_terms(ba_ref[...], al_ref[...], dt_ref[...])
        step = 1
        while step < CH:
            g = g + jnp.where(rowc >= step, pltpu.roll(g, step, 0), 0.0)
            step *= 2
        bg_ref[...] = jnp.where(lane < NH, beta, jnp.where(lane < 2 * NH, g, 0.0))

    return pl.pallas_call(
        body, grid=(t // TR,),
        in_specs=[_row(3 * D, 0), _halo_prev(3 * D, 0), pl.BlockSpec((4, 3 * D), lambda i: (0, 0)),
                  _row(128, OFF_BA // 128), _vec(128), _vec(128)],
        out_specs=[_row(D), _row(D), _row(D), _row(128)],
        out_shape=[SDS((t, D), F32), SDS((t, D), F32), SDS((t, D), F32), SDS((t, 128), F32)],
        compiler_params=_params(("parallel",)), name=name,
    )(proj, proj, conv_w, proj, al_row, dt_row)


def _dn_pre_bwd_act(proj, conv_w, al_row, dt_row, dq, dk, dv, dbg, name):
    t = proj.shape[0]

    def body(x_ref, h_ref, cw_ref, ba_ref, al_ref, dt_ref, dq_ref, dk_ref, dv_ref, dbg_ref,
             dc_ref, draw_ref, dal_ref, ddt_ref):
        i = pl.program_id(0)

        @pl.when(i == 0)
        def _():
            dal_ref[...] = jnp.zeros_like(dal_ref)
            ddt_ref[...] = jnp.zeros_like(ddt_ref)

        keep = i > 0
        for grp in range(24):
            cols = slice(grp * HD, (grp + 1) * HD)
            ext = jnp.concatenate([jnp.where(keep, h_ref[:, cols], 0.0), x_ref[:, cols]], axis=0)
            cv = _conv_group(ext, cw_ref, cols)
            seg, head = divmod(grp, NH)
            hc = slice(head * HD, (head + 1) * HD)
            if seg == 2:
                ds = dv_ref[:, hc]
            else:
                s = _silu(cv)
                r = lax.rsqrt(jnp.sum(s * s, axis=-1, keepdims=True) + L2_EPS)
                dy = dq_ref[:, hc] if seg == 0 else dk_ref[:, hc]
                c = (HD ** -0.5) if seg == 0 else 1.0
                ds = (c * r) * (dy - s * ((r * r) * jnp.sum(dy * s, axis=-1, keepdims=True)))
            dc_ref[:, cols] = ds * _dsilu(cv)
        lane = _iota((TR, 128), 1)
        rowc = _iota((TR, 128), 0) % CH
        isb = lane < NH
        isg = jnp.logical_and(lane >= NH, lane < 2 * NH)
        beta, g, sg = _gate_terms(ba_ref[...], al_ref[...], dt_ref[...])
        dbgv = dbg_ref[...]
        dg = dbgv
        step = 1
        while step < CH:
            dg = dg + jnp.where(rowc < CH - step, pltpu.roll(dg, TR - step, 0), 0.0)
            step *= 2
        da_raw = dg * (-jnp.exp(al_ref[...])) * sg
        draw_ref[...] = jnp.where(isb, dbgv * beta * (1.0 - beta), jnp.where(isg, da_raw, 0.0)).astype(draw_ref.dtype)
        dal_ref[...] += jnp.sum(jnp.where(isg, dg * g, 0.0), axis=0, keepdims=True)
        ddt_ref[...] += jnp.sum(jnp.where(isg, da_raw, 0.0), axis=0, keepdims=True)

    return pl.pallas_call(
        body, grid=(t // TR,),
        in_specs=[_row(3 * D, 0), _halo_prev(3 * D, 0), pl.BlockSpec((4, 3 * D), lambda i: (0, 0)),
                  _row(128, OFF_BA // 128), _vec(128), _vec(128), _row(D), _row(D), _row(D), _row(128)],
        out_specs=[_row(3 * D), _row(128), _vec(128), _vec(128)],
        out_shape=[SDS((t, 3 * D), F32), SDS((t, 128), BF16), SDS((1, 128), F32), SDS((1, 128), F32)],
        compiler_params=_params(("arbitrary",)), name=name,
    )(proj, proj, conv_w, proj, al_row, dt_row, dq, dk, dv, dbg)


def _dn_pre_bwd_conv(proj, conv_w, dconv, name):
    t = proj.shape[0]
    nt = t // TR
    ext_rows = TR + HALO

    def body(x_ref, h_ref, cw_ref, dc_ref, dn_ref, dx_ref, dcw_ref):
        i = pl.program_id(0)

        @pl.when(i == 0)
        def _():
            dcw_ref[...] = jnp.zeros_like(dcw_ref)

        keep_prev = i > 0
        keep_next = i < nt - 1
        for grp in range(24):
            cols = slice(grp * HD, (grp + 1) * HD)
            dct = dc_ref[:, cols]
            dext = jnp.concatenate([dct, jnp.where(keep_next, dn_ref[:, cols], 0.0)], axis=0)
            acc = cw_ref[3:4, cols] * dext
            for j in range(3):
                acc = acc + cw_ref[j:j + 1, cols] * pltpu.roll(dext, ext_rows - (3 - j), 0)
            dx_ref[:, cols] = acc[:TR].astype(dx_ref.dtype)
            xext = jnp.concatenate([jnp.where(keep_prev, h_ref[:, cols], 0.0), x_ref[:, cols]], axis=0)
            for j in range(4):
                xs = xext if j == 3 else pltpu.roll(xext, 3 - j, 0)
                dcw_ref[j:j + 1, cols] += jnp.sum(xs[HALO:] * dct, axis=0, keepdims=True)

    return pl.pallas_call(
        body, grid=(nt,),
        in_specs=[_row(3 * D, 0), _halo_prev(3 * D, 0), pl.BlockSpec((4, 3 * D), lambda i: (0, 0)),
                  _row(3 * D), _halo_next(3 * D, 0, nt)],
        out_specs=[_row(3 * D), pl.BlockSpec((4, 3 * D), lambda i: (0, 0))],
        out_shape=[SDS((t, 3 * D), BF16), SDS((4, 3 * D), F32)],
        compiler_params=_params(("arbitrary",)), name=name,
    )(proj, proj, conv_w, dconv, dconv)


def _dn_post_fwd(o, proj, gn, name):
    t = o.shape[0]

    def body(o_ref, z_ref, g_ref, out_ref):
        gv = g_ref[...]
        for h in range(NH):
            hc = slice(h * HD, (h + 1) * HD)
            ov = o_ref[:, hc]
            r = lax.rsqrt(jnp.mean(ov * ov, axis=-1, keepdims=True) + RMS_EPS)
            out_ref[:, hc] = (((ov * r) * gv) * _silu(z_ref[:, hc])).astype(out_ref.dtype)

    return pl.pallas_call(
        body, grid=(t // TR,), in_specs=[_row(D), _row(D, OFF_Z // D), _vec(HD)], out_specs=_row(D),
        out_shape=SDS((t, D), BF16), compiler_params=_params(("parallel",)), name=name,
    )(o, proj, gn)


def _dn_post_bwd(o, proj, gn, dob, name):
    t = o.shape[0]

    def body(o_ref, z_ref, g_ref, d_ref, do_ref, dz_ref, dg_ref):
        @pl.when(pl.program_id(0) == 0)
        def _():
            dg_ref[...] = jnp.zeros_like(dg_ref)

        gv = g_ref[...]
        acc = jnp.zeros((1, HD), F32)
        for h in range(NH):
            hc = slice(h * HD, (h + 1) * HD)
            ov = o_ref[:, hc]
            zv = z_ref[:, hc]
            dv = d_ref[:, hc]
            r = lax.rsqrt(jnp.mean(ov * ov, axis=-1, keepdims=True) + RMS_EPS)
            n = ov * r
            dz_ref[:, hc] = (dv * (n * gv) * _dsilu(zv)).astype(dz_ref.dtype)
            dng = dv * _silu(zv)
            acc = acc + jnp.sum(dng * n, axis=0, keepdims=True)
            dn = dng * gv
            do_ref[:, hc] = r * (dn - n * jnp.mean(dn * n, axis=-1, keepdims=True))
        dg_ref[...] += acc

    return pl.pallas_call(
        body, grid=(t // TR,), in_specs=[_row(D), _row(D, OFF_Z // D), _vec(HD), _row(D)],
        out_specs=[_row(D), _row(D), _vec(HD)],
        out_shape=[SDS((t, D), F32), SDS((t, D), BF16), SDS((1, HD), F32)],
        compiler_params=_params(("arbitrary",)), name=name,
    )(o, proj, gn, dob)


def _merge_fwd(ya, yb, proj, name):
    t = ya.shape[0]

    def body(a_ref, b_ref, gp_ref, gd_ref, o_ref):
        o_ref[...] = (_sigmoid(gp_ref[...]) * a_ref[...] + _sigmoid(gd_ref[...]) * b_ref[...]).astype(o_ref.dtype)

    return pl.pallas_call(
        body, grid=(t // TR,), in_specs=[_row(D), _row(D), _row(D, OFF_GP // D), _row(D, OFF_GD // D)],
        out_specs=_row(D), out_shape=SDS((t, D), BF16),
        compiler_params=_params(("parallel",)), name=name,
    )(ya, yb, proj, proj)


def _merge_bwd(dm, ya, yb, proj, name):
    t = ya.shape[0]

    def body(d_ref, a_ref, b_ref, gp_ref, gd_ref, da_ref, db_ref, dgp_ref, dgd_ref):
        dv = d_ref[...]
        sp = _sigmoid(gp_ref[...])
        sd = _sigmoid(gd_ref[...])
        da_ref[...] = dv * sp
        db_ref[...] = (dv * sd).astype(db_ref.dtype)
        dgp_ref[...] = (dv * a_ref[...] * sp * (1.0 - sp)).astype(dgp_ref.dtype)
        dgd_ref[...] = (dv * b_ref[...] * sd * (1.0 - sd)).astype(dgd_ref.dtype)

    return pl.pallas_call(
        body, grid=(t // TR,),
        in_specs=[_row(D), _row(D), _row(D), _row(D, OFF_GP // D), _row(D, OFF_GD // D)],
        out_specs=[_row(D)] * 4,
        out_shape=[SDS((t, D), F32), SDS((t, D), BF16), SDS((t, D), BF16), SDS((t, D), BF16)],
        compiler_params=_params(("parallel",)), name=name,
    )(dm, ya, yb, proj, proj)


def _split2(x):
    hi = x.astype(BF16)
    return hi, (x - hi.astype(F32)).astype(BF16)


def _dot3(a, b, dims):
    ah, al = _split2(a)
    bh, bl = _split2(b)
    return _dg(ah, bh, dims) + (_dg(ah, bl, dims) + _dg(al, bh, dims))


def _neumann_inverse(a):
    ri = _iota((CH, CH), 0)
    ci = _iota((CH, CH), 1)
    n = -a
    p = jnp.where(ri == ci, 1.0, 0.0).astype(F32) + n
    x = n
    for _ in range(5):
        x = _dot3(x, x, NN)
        p = p + _dot3(p, x, NN)
    return p


def _solve_with(inv):
    @jax.custom_vjp
    def solve(a, rhs):
        return _dot3(inv, rhs, NN)

    def fwd(a, rhs):
        sol = _dot3(inv, rhs, NN)
        return sol, sol

    def bwd(sol, d):
        drhs = _dot3(inv, d, TN)
        return -_dot3(drhs, sol, NT), drhs

    solve.defvjp(fwd, bwd)
    return solve


@jax.custom_vjp
def _rows_to_lanes(g64):
    ri = _iota((CH, CH), 0)
    ci = _iota((CH, CH), 1)
    diag = jnp.where(ri == ci, g64, 0.0)
    ones = jnp.ones((CH, CH), BF16)
    hi = diag.astype(BF16)
    rem = diag - hi.astype(F32)
    mid = rem.astype(BF16)
    lo = (rem - mid.astype(F32)).astype(BF16)
    return _dg(ones, hi, NN) + (_dg(ones, mid, NN) + _dg(ones, lo, NN))


def _rows_to_lanes_bwd(_, d):
    ri = _iota((CH, CH), 0)
    ci = _iota((CH, CH), 1)
    return (jnp.where(ri == ci, jnp.broadcast_to(jnp.sum(d, axis=0, keepdims=True), (CH, CH)), 0.0),)


_rows_to_lanes.defvjp(lambda g64: (_rows_to_lanes(g64), None), _rows_to_lanes_bwd)


def _chunk_local(solve, q, k, v, g128, g64, gl128, b128, b64):
    ri = _iota((CH, CH), 0)
    ci = _iota((CH, CH), 1)
    causal = ri >= ci
    strict = ri > ci
    decay = jnp.where(causal, jnp.exp(jnp.where(causal, g64 - _rows_to_lanes(g64), 0.0)), 0.0)
    a = jnp.where(strict, b64 * _nt(k, k) * decay, 0.0)
    eg = jnp.exp(g128)
    sol = solve(a, jnp.concatenate([b128 * v, (b128 * eg) * k], axis=1))
    qk = jnp.where(causal, _nt(q, k) * decay, 0.0)
    return sol[:, :HD], sol[:, HD:], qk, q * eg, k * jnp.exp(gl128 - g128), jnp.exp(gl128)


def _head_gates(bgv, h):
    lane = _iota((CH, 128), 1)
    row = _iota((CH, 128), 0)
    bcol = jnp.sum(jnp.where(lane == h, bgv, 0.0), axis=1, keepdims=True)
    gcol = jnp.sum(jnp.where(lane == NH + h, bgv, 0.0), axis=1, keepdims=True)
    g128 = jnp.broadcast_to(gcol, (CH, 128))
    gl128 = jnp.broadcast_to(jnp.sum(jnp.where(row == CH - 1, g128, 0.0), axis=0, keepdims=True), (CH, 128))
    return (g128, jnp.broadcast_to(gcol, (CH, CH)), gl128,
            jnp.broadcast_to(bcol, (CH, 128)), jnp.broadcast_to(bcol, (CH, CH)))


def _chunk_specs():
    row = pl.BlockSpec((CH, D), lambda i: (i, 0))
    small = pl.BlockSpec((CH, 128), lambda i: (i, 0))
    qk = pl.BlockSpec((NH, CH, CH), lambda i: (i, 0, 0))
    eg = pl.BlockSpec((1, NH, 128), lambda i: (i, 0, 0))
    return row, small, qk, eg


def _dn_local_fwd(q, k, v, bg, name):
    t = q.shape[0]
    n = t // CH

    def body(q_ref, k_ref, v_ref, bg_ref, u_ref, w_ref, qk_ref, qd_ref, kd_ref, eg_ref, inv_ref):
        bgv = bg_ref[...]
        for h in range(NH):
            hc = slice(h * HD, (h + 1) * HD)

            def solve(a, rhs, h=h):
                inv = _neumann_inverse(a)
                inv_ref[h] = inv
                return _dot3(inv, rhs, NN)

            u, w, qk, qd, kd, egl = _chunk_local(solve, q_ref[:, hc], k_ref[:, hc], v_ref[:, hc],
                                                 *_head_gates(bgv, h))
            u_ref[:, hc] = u
            w_ref[:, hc] = w.astype(w_ref.dtype)
            qd_ref[:, hc] = qd.astype(qd_ref.dtype)
            kd_ref[:, hc] = kd.astype(kd_ref.dtype)
            qk_ref[h] = qk.astype(qk_ref.dtype)
            eg_ref[0, h:h + 1, :] = egl[0:1, :]

    row, small, qkb, egb = _chunk_specs()
    return pl.pallas_call(
        body, grid=(n,), in_specs=[row, row, row, small], out_specs=[row, row, qkb, row, row, egb, qkb],
        out_shape=[SDS((t, D), F32), SDS((t, D), BF16), SDS((n * NH, CH, CH), BF16), SDS((t, D), BF16),
                   SDS((t, D), BF16), SDS((n, NH, 128), F32), SDS((n * NH, CH, CH), F32)],
        compiler_params=_params(("parallel",)), name=name,
    )(q, k, v, bg)


def _dn_local_bwd(q, k, v, bg, inv, du, dw, dqk, dqd, dkd, deg, name):
    t = q.shape[0]
    n = t // CH

    def body(q_ref, k_ref, v_ref, bg_ref, inv_ref, du_ref, dw_ref, dqk_ref, dqd_ref, dkd_ref, deg_ref,
             dq_ref, dk_ref, dv_ref, dbg_ref):
        bgv = bg_ref[...]
        lane = _iota((CH, 128), 1)
        row = _iota((CH, 128), 0)
        first = jnp.where(row == 0, 1.0, 0.0)
        acc = jnp.zeros((CH, 128), F32)
        for h in range(NH):
            hc = slice(h * HD, (h + 1) * HD)
            _, vjp = jax.vjp(functools.partial(_chunk_local, _solve_with(inv_ref[h])),
                             q_ref[:, hc], k_ref[:, hc], v_ref[:, hc], *_head_gates(bgv, h))
            cts = (du_ref[:, hc], dw_ref[:, hc], dqk_ref[h], dqd_ref[:, hc], dkd_ref[:, hc],
                   jnp.broadcast_to(deg_ref[0, h:h + 1, :], (CH, 128)) * first)
            dq, dk, dv, dg128, dg64, dgl, db128, db64 = vjp(cts)
            dq_ref[:, hc] = dq
            dk_ref[:, hc] = dk
            dv_ref[:, hc] = dv
            dg = jnp.sum(dg128, axis=1, keepdims=True) + jnp.sum(dg64, axis=1, keepdims=True)
            tot = jnp.sum(jnp.sum(dgl, axis=0, keepdims=True), axis=1, keepdims=True)
            dg = dg + jnp.where(row[:, 0:1] == CH - 1, tot, 0.0)
            db = jnp.sum(db128, axis=1, keepdims=True) + jnp.sum(db64, axis=1, keepdims=True)
            acc = acc + jnp.where(lane == h, db, 0.0) + jnp.where(lane == NH + h, dg, 0.0)
        dbg_ref[...] = acc

    row, small, qkb, egb = _chunk_specs()
    return pl.pallas_call(
        body, grid=(n,), in_specs=[row, row, row, small, qkb, row, row, qkb, row, row, egb],
        out_specs=[row, row, row, small],
        out_shape=[SDS((t, D), F32)] * 3 + [SDS((t, 128), F32)],
        compiler_params=_params(("parallel",)), name=name,
    )(q, k, v, bg, inv, du, dw, dqk, dqd, dkd, deg)


def _state_step(s, u, w, qk, qd, kd, egl):
    v_new = u - _nn(w, s)
    o = _nn(qd, s) + _nn(qk, v_new)
    return s * egl + _tn(kd, v_new), o


def _dn_scan_fwd(u, w, qk, qd, kd, eg, name):
    t = u.shape[0]
    n = t // CH

    def body(u_ref, w_ref, qk_ref, qd_ref, kd_ref, eg_ref, o_ref, save_ref, s_ref):
        @pl.when(pl.program_id(0) == 0)
        def _():
            s_ref[...] = jnp.zeros_like(s_ref)

        for h in range(NH):
            hc = slice(h * HD, (h + 1) * HD)
            s = s_ref[h]
            save_ref[0, h] = s
            s_new, o = _state_step(s, u_ref[:, hc], w_ref[:, hc].astype(F32), qk_ref[h].astype(F32),
                                   qd_ref[:, hc].astype(F32), kd_ref[:, hc].astype(F32), eg_ref[0, h:h + 1, :])
            o_ref[:, hc] = o
            s_ref[h] = s_new

    row, _, qkb, egb = _chunk_specs()
    return pl.pallas_call(
        body, grid=(n,), in_specs=[row, row, qkb, row, row, egb],
        out_specs=[row, pl.BlockSpec((1, NH, HD, HD), lambda i: (i, 0, 0, 0))],
        out_shape=[SDS((t, D), F32), SDS((n, NH, HD, HD), F32)],
        scratch_shapes=[pltpu.VMEM((NH, HD, HD), F32)],
        compiler_params=_params(("arbitrary",)), name=name,
    )(u, w, qk, qd, kd, eg)


def _dn_scan_bwd(u, w, qk, qd, kd, eg, saved, do, name):
    t = u.shape[0]
    n = t // CH

    def body(u_ref, w_ref, qk_ref, qd_ref, kd_ref, eg_ref, sv_ref, do_ref,
             du_ref, dw_ref, dqk_ref, dqd_ref, dkd_ref, deg_ref, ds_ref):
        @pl.when(pl.program_id(0) == 0)
        def _():
            ds_ref[...] = jnp.zeros_like(ds_ref)

        for h in range(NH):
            hc = slice(h * HD, (h + 1) * HD)
            _, vjp = jax.vjp(_state_step, sv_ref[0, h], u_ref[:, hc], w_ref[:, hc].astype(F32),
                             qk_ref[h].astype(F32), qd_ref[:, hc].astype(F32), kd_ref[:, hc].astype(F32),
                             eg_ref[0, h:h + 1, :])
            ds, du, dw, dqk, dqd, dkd, deg = vjp((ds_ref[h], do_ref[:, hc]))
            ds_ref[h] = ds
            du_ref[:, hc] = du
            dw_ref[:, hc] = dw
            dqk_ref[h] = dqk
            dqd_ref[:, hc] = dqd
            dkd_ref[:, hc] = dkd
            deg_ref[0, h:h + 1, :] = deg

    rev = lambda i: (n - 1 - i, 0)
    rev3 = lambda i: (n - 1 - i, 0, 0)
    row = pl.BlockSpec((CH, D), rev)
    qkb = pl.BlockSpec((NH, CH, CH), rev3)
    egb = pl.BlockSpec((1, NH, 128), rev3)
    return pl.pallas_call(
        body, grid=(n,),
        in_specs=[row, row, qkb, row, row, egb,
                  pl.BlockSpec((1, NH, HD, HD), lambda i: (n - 1 - i, 0, 0, 0)), row],
        out_specs=[row, row, qkb, row, row, egb],
        out_shape=[SDS((t, D), F32), SDS((t, D), F32), SDS((n * NH, CH, CH), F32), SDS((t, D), F32),
                   SDS((t, D), F32), SDS((n, NH, 128), F32)],
        scratch_shapes=[pltpu.VMEM((NH, HD, HD), F32)],
        compiler_params=_params(("arbitrary",)), name=name,
    )(u, w, qk, qd, kd, eg, saved, do)


def _ada_fwd(c_all, ada_w, ada_b, name):
    ncol = ada_w.shape[1]

    def body(c_ref, w_ref, b_ref, o_ref):
        o_ref[...] = _dg(_silu(c_ref[...]), w_ref[...], NN, HI) + b_ref[...]

    return pl.pallas_call(body, out_shape=SDS((NDEV, ncol), F32),
                          compiler_params=pltpu.CompilerParams(vmem_limit_bytes=VMEM_LIMIT), name=name,
                          )(c_all, ada_w, ada_b)


def _ada_bwd(c_all_t, dmod, name):
    ncol = dmod.shape[1]

    def body(c_ref, d_ref, o_ref):
        sc = _silu(c_ref[...])
        acc = sc[:, 0:1] * d_ref[0:1, :]
        for b in range(1, NDEV):
            acc = acc + sc[:, b:b + 1] * d_ref[b:b + 1, :]
        o_ref[...] = acc

    return pl.pallas_call(body, out_shape=SDS((D, ncol), F32),
                          compiler_params=pltpu.CompilerParams(vmem_limit_bytes=VMEM_LIMIT), name=name,
                          )(c_all_t, dmod)


def _sum_devices(parts, out_dtype, name):
    _, r, c = parts.shape
    tr = TR if r % TR == 0 else r

    def body(p_ref, o_ref):
        acc = p_ref[0].astype(F32)
        for i in range(1, NDEV):
            acc = acc + p_ref[i].astype(F32)
        o_ref[...] = acc.astype(o_ref.dtype)

    return pl.pallas_call(
        body, grid=(r // tr,), in_specs=[pl.BlockSpec((NDEV, tr, c), lambda i: (0, i, 0))],
        out_specs=pl.BlockSpec((tr, c), lambda i: (i, 0)), out_shape=SDS((r, c), out_dtype),
        compiler_params=_params(("parallel",)), name=name,
    )(parts)


def _adamw(w, g, m, v, name):
    r, c = w.shape
    tr = _pick(r, (256, 128, 88, 8)) if r % 8 == 0 else r
    bc1 = 1.0 - ADAM_B1 ** ADAM_STEP
    bc2 = 1.0 - ADAM_B2 ** ADAM_STEP

    def body(w_ref, g_ref, m_ref, v_ref, d_ref, nm_ref, nv_ref):
        gv = g_ref[...]
        m_new = ADAM_B1 * m_ref[...] + (1.0 - ADAM_B1) * gv
        v_new = ADAM_B2 * v_ref[...] + (1.0 - ADAM_B2) * (gv * gv)
        nm_ref[...] = m_new
        nv_ref[...] = v_new
        d_ref[...] = -ADAM_LR * ((m_new / bc1) / (jnp.sqrt(v_new / bc2) + ADAM_EPS) + ADAM_WD * w_ref[...])

    spec = pl.BlockSpec((tr, c), lambda i: (i, 0))
    return pl.pallas_call(
        body, grid=(r // tr,), in_specs=[spec] * 4, out_specs=[spec] * 3,
        out_shape=[SDS((r, c), F32)] * 3, compiler_params=_params(("parallel",)), name=name,
    )(w, g, m, v)


ANY = pl.BlockSpec(memory_space=pl.ANY)
MESH = pl.DeviceIdType.MESH


def _all_gather(x, name):
    r, c_dim = x.shape

    def body(x_ref, out_ref, send_sems, recv_sems, local_sem):
        mx, my, mc = lax.axis_index("x"), lax.axis_index("y"), lax.axis_index("c")
        me, sibling = (mx, my, mc), (mx, my, 1 - mc)
        chips = [(1 - mx, my), (mx, 1 - my), (1 - mx, 1 - my)]

        def rows(px, py, pc):
            return out_ref.at[4 * px + 2 * py + pc]

        def copy(k, block, to, src=None):
            return pltpu.make_async_remote_copy(
                src_ref=rows(*block) if src is None else src, dst_ref=rows(*block),
                send_sem=send_sems.at[k], recv_sem=recv_sems.at[k], device_id=to, device_id_type=MESH)

        mine = pltpu.make_async_copy(x_ref, rows(*me), local_sem)
        mine.start()
        first = [copy(0, me, sibling, src=x_ref)]
        first += [copy(1 + j, me, (*chip, mc), src=x_ref) for j, chip in enumerate(chips)]
        for cp in first:
            cp.start()
        passed = [copy(4 + j, (*chip, mc), sibling) for j, chip in enumerate(chips)]
        for j, chip in enumerate(chips):
            copy(1 + j, (*chip, mc), me).wait_recv()
            passed[j].start()
        copy(0, sibling, me).wait_recv()
        for j, chip in enumerate(chips):
            copy(4 + j, (*chip, 1 - mc), me).wait_recv()
        for cp in first + passed:
            cp.wait_send()
        mine.wait()

    return pl.pallas_call(
        body, out_shape=SDS((NDEV, r, c_dim), x.dtype), in_specs=[ANY], out_specs=ANY,
        scratch_shapes=[pltpu.SemaphoreType.DMA((7,)), pltpu.SemaphoreType.DMA((7,)), pltpu.SemaphoreType.DMA],
        name=name,
    )(x)


def _exchange_blocks(parts, name):
    _, r, c_dim = parts.shape

    def body(p_ref, out_ref, send_sems, recv_sems, local_sem):
        mx, my, mc = lax.axis_index("x"), lax.axis_index("y"), lax.axis_index("c")
        me = 4 * mx + 2 * my + mc
        mine = pltpu.make_async_copy(p_ref.at[me], out_ref.at[me], local_sem)
        mine.start()
        copies = []
        for k in range(1, NDEV):
            px = 1 - mx if k & 4 else mx
            py = 1 - my if k & 2 else my
            pc = 1 - mc if k & 1 else mc
            copies.append(pltpu.make_async_remote_copy(
                src_ref=p_ref.at[4 * px + 2 * py + pc], dst_ref=out_ref.at[me],
                send_sem=send_sems.at[k - 1], recv_sem=recv_sems.at[k - 1],
                device_id=(px, py, pc), device_id_type=MESH))
        for cp in copies:
            cp.start()
        for cp in copies:
            cp.wait_recv()
        for cp in copies:
            cp.wait_send()
        mine.wait()

    return pl.pallas_call(
        body, out_shape=SDS((NDEV, r, c_dim), parts.dtype), in_specs=[ANY], out_specs=ANY,
        scratch_shapes=[pltpu.SemaphoreType.DMA((7,)), pltpu.SemaphoreType.DMA((7,)), pltpu.SemaphoreType.DMA],
        name=name,
    )(parts)


def _cols_from_blocks(blocks, rows):
    w = blocks.shape[1] * blocks.shape[2] // rows
    return blocks.reshape(NDEV, rows, w).transpose(1, 0, 2).reshape(rows, NDEV * w)


def _cols_to_blocks(full):
    rows, total = full.shape
    w = total // NDEV
    return full.reshape(rows, NDEV, w).transpose(1, 0, 2).reshape(NDEV, rows * w // D, D)


def _mix_pad(w):
    rows = w.shape[0]
    xp, q, k, v, z, b, a, gp, gd = jnp.split(w, (512, 1536, 2560, 3584, 4608, 4616, 4624, 5648), axis=1)
    pad = jnp.zeros((rows, MIXP - OFF_BA - 16), w.dtype)
    return jnp.concatenate([q, k, v, z, gp, gd, xp, b, a, pad], axis=1)


def _mix_unpad(w):
    q, k, v, z, gp, gd, xp, b, a = (w[:, OFF_Q:OFF_K], w[:, OFF_K:OFF_V], w[:, OFF_V:OFF_Z], w[:, OFF_Z:OFF_GP],
                                    w[:, OFF_GP:OFF_GD], w[:, OFF_GD:OFF_XP], w[:, OFF_XP:OFF_BA],
                                    w[:, OFF_BA:OFF_BA + 8], w[:, OFF_BA + 8:OFF_BA + 16])
    return jnp.concatenate([xp, q, k, v, z, b, a, gp, gd], axis=1)


def _lane_row(vec8):
    return jnp.zeros((1, 128), F32).at[0, NH:2 * NH].set(vec8)


def _ffn_fwd(x, g, shift, scale, gate, w_in, w_out, tag):
    h = _norm_mod_fwd(x, g, shift, scale, f"{tag}_norm")
    u = _matmul(h, w_in, out_dtype=F32, name=f"{tag}_up")
    a = _swiglu_fwd(u, f"{tag}_act")
    y = _matmul(a, w_out, out_dtype=F32, name=f"{tag}_down")
    return _resid_fwd(x, y, gate, 0.5, f"{tag}_res"), (h, u, a, y)


def _ffn_bwd(dx_out, x, g, scale, gate, w_in, w_out, saved, tag):
    h, u, a, y = saved
    dy, dgate = _resid_bwd(dx_out, y, gate, 0.5, f"{tag}_res_bwd")
    da = _matmul(dy, w_out, tb=True, out_dtype=F32, name=f"{tag}_down_dx")
    dw_out = _matmul(a, dy, ta=True, out_dtype=BF16, name=f"{tag}_down_dw")
    du = _swiglu_bwd(u, da, f"{tag}_act_bwd")
    dh = _matmul(du, w_in, tb=True, out_dtype=F32, name=f"{tag}_up_dx")
    dw_in = _matmul(h, du, ta=True, out_dtype=BF16, name=f"{tag}_up_dw")
    dx, dshift, dscale, dg = _norm_mod_bwd(x, g, scale, dh, dx_out, f"{tag}_norm_bwd")
    return dx, (dshift, dscale, dgate), dg, dw_in, dw_out


def kernel(x, c, ada_w, ada_b, norm_g, ffn1_w_in, ffn1_w_out, ffn2_w_in, ffn2_w_out, mix_w_in, conv_w, a_log, dt_bias, dn_norm_g, pool_w, pool_scale, pool_proj, dn_proj, mix_w_out, final_g, loss_target, m_ada_w, m_ada_b, m_norm_g, m_ffn1_w_in, m_ffn1_w_out, m_ffn2_w_in, m_ffn2_w_out, m_mix_w_in, m_conv_w, m_a_log, m_dt_bias, m_dn_norm_g, m_pool_w, m_pool_scale, m_pool_proj, m_dn_proj, m_mix_w_out, m_final_g, v_ada_w, v_ada_b, v_norm_g, v_ffn1_w_in, v_ffn1_w_out, v_ffn2_w_in, v_ffn2_w_out, v_mix_w_in, v_conv_w, v_a_log, v_dt_bias, v_dn_norm_g, v_pool_w, v_pool_scale, v_pool_proj, v_dn_proj, v_mix_w_out, v_final_g):
    me = 4 * lax.axis_index("x") + 2 * lax.axis_index("y") + lax.axis_index("c")
    x0 = x[0]
    target = loss_target[0]
    t = x0.shape[0]

    big = [ffn1_w_in[0], ffn1_w_out[0], ffn2_w_in[0], ffn2_w_out[0], mix_w_in[0], pool_proj[0], dn_proj[0],
           mix_w_out[0]]
    sizes = [w.size // D for w in big]
    offs = [0]
    for s in sizes:
        offs.append(offs[-1] + s)
    blob_pad = -offs[-1] % TR
    blob = jnp.concatenate([w.astype(BF16).reshape(-1, D) for w in big] + [jnp.zeros((blob_pad, D), BF16)], axis=0)
    gathered = _all_gather(blob, "gather_weights")
    seg = [gathered[:, offs[i]:offs[i + 1], :] for i in range(len(big))]
    w_in1 = _cols_from_blocks(seg[0], D)
    w_out1 = seg[1].reshape(FH, D)
    w_in2 = _cols_from_blocks(seg[2], D)
    w_out2 = seg[3].reshape(FH, D)
    w_mix = _mix_pad(_cols_from_blocks(seg[4], D))
    w_pp = _cols_from_blocks(seg[5], PW)
    w_dn = seg[6].reshape(D, D)
    w_mo = seg[7].reshape(D, D)

    small = jnp.concatenate([c.reshape(8, 128), conv_w[0].reshape(12, 128), norm_g[0].reshape(3, 128),
                             jnp.zeros((1, 128), F32)], axis=0)
    small_all = _all_gather(small, "gather_small")
    c_all = small_all[:, 0:8, :].reshape(NDEV, D)
    conv_full = small_all[:, 8:20, :].reshape(NDEV, 4, 384).transpose(1, 0, 2).reshape(4, 3 * D)
    norm_full = small_all[:, 20:23, :].reshape(NDEV, 3, 128).transpose(1, 0, 2).reshape(3, D)

    ncol = ada_w.shape[2]
    ada_b_mine = lax.dynamic_slice(ada_b, (0, me * ncol), (1, ncol))
    mod_cols = _ada_fwd(c_all, ada_w[0], ada_b_mine, "ada_fwd")
    mod_all = _all_gather(mod_cols, "gather_mod")
    mod = lax.dynamic_index_in_dim(mod_all, me, axis=1, keepdims=False).reshape(9, D)
    shift = [mod[3 * s:3 * s + 1] for s in range(3)]
    scale = [mod[3 * s + 1:3 * s + 2] for s in range(3)]
    gate = [mod[3 * s + 2:3 * s + 3] for s in range(3)]
    ng = [norm_full[s:s + 1] for s in range(3)]
    fg = final_g.reshape(1, D)
    al_row = _lane_row(a_log[0])
    dt_row = _lane_row(dt_bias[0])
    gn = dn_norm_g
    pw = pool_w[0]
    ps = pool_scale

    x1, saved1 = _ffn_fwd(x0, ng[0], shift[0], scale[0], gate[0], w_in1, w_out1, "ffn1")

    h1 = _norm_mod_fwd(x1, ng[1], shift[1], scale[1], "mix_norm")
    proj = _matmul(h1, w_mix, out_dtype=F32, name="mix_in")
    ya = _pool_fwd(proj, pw, ps, w_pp, "pool_fwd")
    qh, kh, vh, bg = _dn_pre_fwd(proj, conv_full, al_row, dt_row, "dn_pre")
    u, w, qk, qd, kd, eg, inv = _dn_local_fwd(qh, kh, vh, bg, "dn_local")
    o, s_saved = _dn_scan_fwd(u, w, qk, qd, kd, eg, "dn_scan")
    ob = _dn_post_fwd(o, proj, gn, "dn_post")
    yb = _matmul(ob, w_dn, out_dtype=F32, name="dn_out")
    merged = _merge_fwd(ya, yb, proj, "merge")
    mix_y = _matmul(merged, w_mo, out_dtype=F32, name="mix_out")
    x2 = _resid_fwd(x1, mix_y, gate[1], 1.0, "mix_res")

    x3, saved2 = _ffn_fwd(x2, ng[2], shift[2], scale[2], gate[2], w_in2, w_out2, "ffn2")
    loss_row, dx3, dfg = _final_loss(x3, fg, target, "loss")

    dx2, dmod2, dng2, dw_in2, dw_out2 = _ffn_bwd(dx3, x2, ng[2], scale[2], gate[2], w_in2, w_out2, saved2, "ffn2")

    dmy, dgate1 = _resid_bwd(dx2, mix_y, gate[1], 1.0, "mix_res_bwd")
    dmerged = _matmul(dmy, w_mo, tb=True, out_dtype=F32, name="mix_out_dx")
    dw_mo = _matmul(merged, dmy, ta=True, out_dtype=BF16, name="mix_out_dw")
    dya, dyb, dgp, dgd = _merge_bwd(dmerged, ya, yb, proj, "merge_bwd")
    dob = _matmul(dyb, w_dn, tb=True, out_dtype=F32, name="dn_out_dx")
    dw_dn = _matmul(ob, dyb, ta=True, out_dtype=BF16, name="dn_out_dw")
    do, dz, dgn = _dn_post_bwd(o, proj, gn, dob, "dn_post_bwd")
    du, dw, dqk, dqd, dkd, deg = _dn_scan_bwd(u, w, qk, qd, kd, eg, s_saved, do, "dn_scan_bwd")
    dqh, dkh, dvh, dbg = _dn_local_bwd(qh, kh, vh, bg, inv, du, dw, dqk, dqd, dkd, deg, "dn_local_bwd")
    dconv, draw, dal, ddt = _dn_pre_bwd_act(proj, conv_full, al_row, dt_row, dqh, dkh, dvh, dbg, "dn_pre_bwd_act")
    dqkv, dcw = _dn_pre_bwd_conv(proj, conv_full, dconv, "dn_pre_bwd_conv")
    dwin, dpl, dpw, dps, dpp = _pool_bwd_local(proj, pw, ps, w_pp, dya, "pool_bwd_local")
    dxp = _pool_bwd_window(dwin, dpl, "pool_bwd_window")
    dproj = jnp.concatenate([dqkv, dz, dgp, dgd, dxp, draw, jnp.zeros((t, MIXP - OFF_BA - 128), BF16)], axis=1)
    dh1 = _matmul(dproj, w_mix, tb=True, out_dtype=F32, name="mix_in_dx")
    dw_mix = _matmul(h1, dproj, ta=True, out_dtype=BF16, name="mix_in_dw")
    dx1, dsh1, dsc1, dng1 = _norm_mod_bwd(x1, ng[1], scale[1], dh1, dx2, "mix_norm_bwd")

    dx0, dmod0, dng0, dw_in1, dw_out1 = _ffn_bwd(dx1, x0, ng[0], scale[0], gate[0], w_in1, w_out1, saved1, "ffn1")

    parts = jnp.concatenate([
        _cols_to_blocks(dw_in1), dw_out1.reshape(NDEV, -1, D), _cols_to_blocks(dw_in2),
        dw_out2.reshape(NDEV, -1, D), _cols_to_blocks(_mix_unpad(dw_mix)), _cols_to_blocks(dpp.astype(BF16)),
        dw_dn.reshape(NDEV, -1, D), dw_mo.reshape(NDEV, -1, D), jnp.zeros((NDEV, blob_pad, D), BF16)], axis=1)
    received = _exchange_blocks(parts, "scatter_grads")
    gsum = _sum_devices(received, F32, "sum_grads")
    gshard = [gsum[offs[i]:offs[i + 1]].reshape(big[i].shape) for i in range(len(big))]

    dmod = jnp.concatenate([*dmod0, dsh1, dsc1, dgate1, *dmod2], axis=1).reshape(-1)
    flat = jnp.concatenate([
        dmod, dal[0, NH:2 * NH], ddt[0, NH:2 * NH], dgn.reshape(-1), dps.reshape(-1), dfg.reshape(-1),
        dpw.reshape(-1), jnp.concatenate([dng0, dng1, dng2], axis=0).reshape(-1), dcw.reshape(-1)])
    nflat = 90 * D
    flat = jnp.concatenate([flat, jnp.zeros((nflat - flat.shape[0],), F32)]).reshape(90, D)
    flat_all = _all_gather(flat, "gather_small_grads")
    tot = _sum_devices(flat_all, F32, "sum_small_grads").reshape(-1)
    dmod_all = flat_all.reshape(NDEV, nflat)[:, :9 * D]
    dmod_cols = lax.dynamic_slice(dmod_all, (0, me * ncol), (NDEV, ncol))
    g_ada_w = _ada_bwd(c_all.T, dmod_cols, "ada_bwd")

    p = 0
    pieces = {}
    for nm, size in (("ada_b", 9 * D), ("a_log", NH), ("dt_bias", NH), ("dn_norm_g", HD), ("pool_scale", PW),
                     ("final_g", D), ("pool_w", 4 * PG * PG), ("norm_g", 3 * D), ("conv_w", 12 * D)):
        pieces[nm] = tot[p:p + size]
        p += size
    g_norm = lax.dynamic_slice(pieces["norm_g"].reshape(3, D), (0, me * 128), (3, 128))
    g_conv = lax.dynamic_slice(pieces["conv_w"].reshape(4, 3 * D), (0, me * 384), (4, 384))

    grads = {
        "ada_w": g_ada_w.reshape(ada_w.shape), "ada_b": pieces["ada_b"].reshape(ada_b.shape),
        "norm_g": g_norm.reshape(norm_g.shape),
        "ffn1_w_in": gshard[0].reshape(ffn1_w_in.shape), "ffn1_w_out": gshard[1].reshape(ffn1_w_out.shape),
        "ffn2_w_in": gshard[2].reshape(ffn2_w_in.shape), "ffn2_w_out": gshard[3].reshape(ffn2_w_out.shape),
        "mix_w_in": gshard[4].reshape(mix_w_in.shape), "conv_w": g_conv.reshape(conv_w.shape),
        "a_log": pieces["a_log"].reshape(a_log.shape), "dt_bias": pieces["dt_bias"].reshape(dt_bias.shape),
        "dn_norm_g": pieces["dn_norm_g"].reshape(dn_norm_g.shape), "pool_w": pieces["pool_w"].reshape(pool_w.shape),
        "pool_scale": pieces["pool_scale"].reshape(pool_scale.shape),
        "pool_proj": gshard[5].reshape(pool_proj.shape), "dn_proj": gshard[6].reshape(dn_proj.shape),
        "mix_w_out": gshard[7].reshape(mix_w_out.shape), "final_g": pieces["final_g"].reshape(final_g.shape),
    }
    weights = {"ada_w": ada_w, "ada_b": ada_b, "norm_g": norm_g, "ffn1_w_in": ffn1_w_in, "ffn1_w_out": ffn1_w_out,
               "ffn2_w_in": ffn2_w_in, "ffn2_w_out": ffn2_w_out, "mix_w_in": mix_w_in, "conv_w": conv_w,
               "a_log": a_log, "dt_bias": dt_bias, "dn_norm_g": dn_norm_g, "pool_w": pool_w,
               "pool_scale": pool_scale, "pool_proj": pool_proj, "dn_proj": dn_proj, "mix_w_out": mix_w_out,
               "final_g": final_g}
    m_in = {"ada_w": m_ada_w, "ada_b": m_ada_b, "norm_g": m_norm_g, "ffn1_w_in": m_ffn1_w_in,
            "ffn1_w_out": m_ffn1_w_out, "ffn2_w_in": m_ffn2_w_in, "ffn2_w_out": m_ffn2_w_out,
            "mix_w_in": m_mix_w_in, "conv_w": m_conv_w, "a_log": m_a_log, "dt_bias": m_dt_bias,
            "dn_norm_g": m_dn_norm_g, "pool_w": m_pool_w, "pool_scale": m_pool_scale, "pool_proj": m_pool_proj,
            "dn_proj": m_dn_proj, "mix_w_out": m_mix_w_out, "final_g": m_final_g}
    v_in = {"ada_w": v_ada_w, "ada_b": v_ada_b, "norm_g": v_norm_g, "ffn1_w_in": v_ffn1_w_in,
            "ffn1_w_out": v_ffn1_w_out, "ffn2_w_in": v_ffn2_w_in, "ffn2_w_out": v_ffn2_w_out,
            "mix_w_in": v_mix_w_in, "conv_w": v_conv_w, "a_log": v_a_log, "dt_bias": v_dt_bias,
            "dn_norm_g": v_dn_norm_g, "pool_w": v_pool_w, "pool_scale": v_pool_scale, "pool_proj": v_pool_proj,
            "dn_proj": v_dn_proj, "mix_w_out": v_mix_w_out, "final_g": v_final_g}

    names = list(weights)
    large = ("ada_w", "ffn1_w_in", "ffn1_w_out", "ffn2_w_in", "ffn2_w_out", "mix_w_in", "pool_proj", "dn_proj",
             "mix_w_out")
    delta, new_m, new_v = {}, {}, {}
    for nm in large:
        shp = weights[nm].shape
        two_d = (shp[-2], shp[-1])
        d_, m_, v_ = _adamw(weights[nm].reshape(two_d), grads[nm].reshape(two_d), m_in[nm].reshape(two_d),
                            v_in[nm].reshape(two_d), f"adamw_{nm}")
        delta[nm], new_m[nm], new_v[nm] = d_.reshape(shp), m_.reshape(shp), v_.reshape(shp)
    rest = [nm for nm in names if nm not in large]
    total = sum(weights[nm].size for nm in rest)
    padded = -(-total // D) * D

    def pack(tree, fill):
        flat_ = jnp.concatenate([tree[nm].reshape(-1) for nm in rest])
        return jnp.concatenate([flat_, jnp.full((padded - total,), fill, F32)]).reshape(-1, D)

    d_, m_, v_ = _adamw(pack(weights, 0.0), pack(grads, 0.0), pack(m_in, 0.0), pack(v_in, 1.0), "adamw_small")
    p = 0
    for nm in rest:
        size = weights[nm].size
        shp = weights[nm].shape
        delta[nm] = d_.reshape(-1)[p:p + size].reshape(shp)
        new_m[nm] = m_.reshape(-1)[p:p + size].reshape(shp)
        new_v[nm] = v_.reshape(-1)[p:p + size].reshape(shp)
        p += size

    loss = lax.psum(loss_row[0, 0], ("x", "y", "c"))
    grad_x = dx0.reshape(x.shape)
    return (loss, grad_x, *[grads[nm] for nm in names], *[delta[nm] for nm in names],
            *[new_m[nm] for nm in names], *[new_v[nm] for nm in names])
```

```python
import functools

import jax
import jax.numpy as jnp
from jax import lax
from jax.experimental import pallas as pl
from jax.experimental.pallas import tpu as pltpu

F32 = jnp.float32
BF16 = jnp.bfloat16
SDS = jax.ShapeDtypeStruct
HI = lax.Precision.HIGHEST

D = 1024
FH = 2816
NH = 8
HD = 128
CH = 64
NDEV = 8
PW = 512
PG = 128
RMS_EPS = 1e-6
L2_EPS = 1e-6
TR = 256
HALO = 16
VMEM_LIMIT = 56 * 1024 * 1024

MIXP = 6912
OFF_Q, OFF_K, OFF_V, OFF_Z, OFF_GP, OFF_GD, OFF_XP, OFF_BA = 0, 1024, 2048, 3072, 4096, 5120, 6144, 6656
MIX_RAW = 6672

ADAM_LR = 0.001
ADAM_B1 = 0.9
ADAM_B2 = 0.999
ADAM_EPS = 1e-08
ADAM_WD = 0.01
ADAM_STEP = 10

NN = (((1,), (0,)), ((), ()))
NT = (((1,), (1,)), ((), ()))
TN = (((0,), (0,)), ((), ()))


def _dg(a, b, dims, prec=None):
    return lax.dot_general(a, b, dims, precision=prec, preferred_element_type=F32)


def _make_dots(prec):
    @jax.custom_vjp
    def nn(a, b):
        return _dg(a, b, NN, prec)

    @jax.custom_vjp
    def nt(a, b):
        return _dg(a, b, NT, prec)

    @jax.custom_vjp
    def tn(a, b):
        return _dg(a, b, TN, prec)

    nn.defvjp(lambda a, b: (nn(a, b), (a, b)), lambda r, d: (nt(d, r[1]), tn(r[0], d)))
    nt.defvjp(lambda a, b: (nt(a, b), (a, b)), lambda r, d: (nn(d, r[1]), tn(d, r[0])))
    tn.defvjp(lambda a, b: (tn(a, b), (a, b)), lambda r, d: (nt(r[1], d), nn(r[0], d)))
    return nn, nt, tn


_nn, _nt, _tn = _make_dots(None)


def _params(sem):
    return pltpu.CompilerParams(dimension_semantics=sem, vmem_limit_bytes=VMEM_LIMIT)


def _sigmoid(x):
    return 1.0 / (1.0 + jnp.exp(-x))


def _silu(x):
    return x * _sigmoid(x)


def _dsilu(x):
    s = _sigmoid(x)
    return s * (1.0 + x * (1.0 - s))


def _pick(n, cands):
    for c in cands:
        if n % c == 0:
            return c
    raise ValueError(f"no tile for {n}")


def _iota(shape, dim):
    return lax.broadcasted_iota(jnp.int32, shape, dim)


def _matmul(a, b, *, ta=False, tb=False, out_dtype, name):
    if ta:
        k_dim, m_dim = a.shape
    else:
        m_dim, k_dim = a.shape
    n_dim = b.shape[0] if tb else b.shape[1]
    tm = _pick(m_dim, (512, 256, 128))
    tn = _pick(n_dim, (512, 768, 256, 128))
    tk = k_dim if (k_dim <= 2816 and not ta) else _pick(k_dim, (2816, 2304, 1024, 512, 256))
    nk = k_dim // tk
    dims = ((((0,) if ta else (1,)), ((1,) if tb else (0,))), ((), ()))

    def body(a_ref, b_ref, o_ref, acc_ref):
        k = pl.program_id(2)

        @pl.when(k == 0)
        def _():
            acc_ref[...] = jnp.zeros_like(acc_ref)

        acc_ref[...] += lax.dot_general(a_ref[...].astype(BF16), b_ref[...].astype(BF16), dims,
                                        preferred_element_type=F32)

        @pl.when(k == nk - 1)
        def _():
            o_ref[...] = acc_ref[...].astype(o_ref.dtype)

    a_spec = (pl.BlockSpec((tk, tm), lambda i, j, k: (k, i)) if ta
              else pl.BlockSpec((tm, tk), lambda i, j, k: (i, k)))
    b_spec = (pl.BlockSpec((tn, tk), lambda i, j, k: (j, k)) if tb
              else pl.BlockSpec((tk, tn), lambda i, j, k: (k, j)))
    return pl.pallas_call(
        body, grid=(m_dim // tm, n_dim // tn, nk),
        in_specs=[a_spec, b_spec],
        out_specs=pl.BlockSpec((tm, tn), lambda i, j, k: (i, j)),
        out_shape=SDS((m_dim, n_dim), out_dtype),
        scratch_shapes=[pltpu.VMEM((tm, tn), F32)],
        compiler_params=_params(("parallel", "parallel", "arbitrary")),
        name=name,
    )(a, b)


def _row(width, col=0):
    return pl.BlockSpec((TR, width), lambda i: (i, col))


def _vec(width):
    return pl.BlockSpec((1, width), lambda i: (0, 0))


def _norm_mod_fwd(x, g, shift, scale, name):
    t = x.shape[0]

    def body(x_ref, g_ref, sh_ref, sc_ref, o_ref):
        xv = x_ref[...]
        r = lax.rsqrt(jnp.mean(xv * xv, axis=-1, keepdims=True) + RMS_EPS)
        o_ref[...] = (((xv * r) * g_ref[...]) * (1.0 + sc_ref[...]) + sh_ref[...]).astype(o_ref.dtype)

    return pl.pallas_call(
        body, grid=(t // TR,), in_specs=[_row(D), _vec(D), _vec(D), _vec(D)], out_specs=_row(D),
        out_shape=SDS((t, D), BF16), compiler_params=_params(("parallel",)), name=name,
    )(x, g, shift, scale)


def _norm_mod_bwd(x, g, scale, dh, dx_in, name):
    t = x.shape[0]

    def body(x_ref, g_ref, sc_ref, dh_ref, dxi_ref, dx_ref, dsh_ref, dsc_ref, dg_ref):
        @pl.when(pl.program_id(0) == 0)
        def _():
            dsh_ref[...] = jnp.zeros_like(dsh_ref)
            dsc_ref[...] = jnp.zeros_like(dsc_ref)
            dg_ref[...] = jnp.zeros_like(dg_ref)

        xv = x_ref[...]
        gv = g_ref[...]
        dh = dh_ref[...]
        r = lax.rsqrt(jnp.mean(xv * xv, axis=-1, keepdims=True) + RMS_EPS)
        n = xv * r
        dsh_ref[...] += jnp.sum(dh, axis=0, keepdims=True)
        dsc_ref[...] += jnp.sum(dh * (n * gv), axis=0, keepdims=True)
        tt = dh * (1.0 + sc_ref[...])
        dg_ref[...] += jnp.sum(tt * n, axis=0, keepdims=True)
        dn = tt * gv
        dx_ref[...] = dxi_ref[...] + r * (dn - n * jnp.mean(dn * n, axis=-1, keepdims=True))

    return pl.pallas_call(
        body, grid=(t // TR,), in_specs=[_row(D), _vec(D), _vec(D), _row(D), _row(D)],
        out_specs=[_row(D), _vec(D), _vec(D), _vec(D)],
        out_shape=[SDS((t, D), F32), SDS((1, D), F32), SDS((1, D), F32), SDS((1, D), F32)],
        compiler_params=_params(("arbitrary",)), name=name,
    )(x, g, scale, dh, dx_in)


def _swiglu_fwd(u, name):
    t = u.shape[0]

    def body(g_ref, u_ref, o_ref):
        o_ref[...] = (_silu(g_ref[...]) * u_ref[...]).astype(o_ref.dtype)

    return pl.pallas_call(
        body, grid=(t // TR,), in_specs=[_row(FH, 0), _row(FH, 1)], out_specs=_row(FH),
        out_shape=SDS((t, FH), BF16), compiler_params=_params(("parallel",)), name=name,
    )(u, u)


def _swiglu_bwd(u, da, name):
    t = u.shape[0]

    def body(g_ref, u_ref, da_ref, o_ref):
        gv = g_ref[...]
        dav = da_ref[...]
        o_ref[:, :FH] = (dav * u_ref[...] * _dsilu(gv)).astype(o_ref.dtype)
        o_ref[:, FH:] = (dav * _silu(gv)).astype(o_ref.dtype)

    return pl.pallas_call(
        body, grid=(t // TR,), in_specs=[_row(FH, 0), _row(FH, 1), _row(FH)], out_specs=_row(2 * FH),
        out_shape=SDS((t, 2 * FH), BF16), compiler_params=_params(("parallel",)), name=name,
    )(u, u, da)


def _resid_fwd(x, y, gate, coef, name):
    t = x.shape[0]

    def body(x_ref, y_ref, g_ref, o_ref):
        o_ref[...] = x_ref[...] + (coef * g_ref[...]) * y_ref[...]

    return pl.pallas_call(
        body, grid=(t // TR,), in_specs=[_row(D), _row(D), _vec(D)], out_specs=_row(D),
        out_shape=SDS((t, D), F32), compiler_params=_params(("parallel",)), name=name,
    )(x, y, gate)


def _resid_bwd(dx, y, gate, coef, name):
    t = dx.shape[0]

    def body(dx_ref, y_ref, g_ref, dy_ref, dg_ref):
        @pl.when(pl.program_id(0) == 0)
        def _():
            dg_ref[...] = jnp.zeros_like(dg_ref)

        dxv = dx_ref[...]
        dy_ref[...] = ((coef * g_ref[...]) * dxv).astype(dy_ref.dtype)
        dg_ref[...] += jnp.sum((coef * dxv) * y_ref[...], axis=0, keepdims=True)

    return pl.pallas_call(
        body, grid=(t // TR,), in_specs=[_row(D), _row(D), _vec(D)], out_specs=[_row(D), _vec(D)],
        out_shape=[SDS((t, D), BF16), SDS((1, D), F32)],
        compiler_params=_params(("arbitrary",)), name=name,
    )(dx, y, gate)


def _final_loss(x, fg, target, name):
    t = x.shape[0]
    nt = t // TR

    def body(x_ref, g_ref, t_ref, loss_ref, dx_ref, dg_ref, acc_ref):
        i = pl.program_id(0)

        @pl.when(i == 0)
        def _():
            acc_ref[...] = jnp.zeros_like(acc_ref)
            dg_ref[...] = jnp.zeros_like(dg_ref)

        xv = x_ref[...]
        gv = g_ref[...]
        r = lax.rsqrt(jnp.mean(xv * xv, axis=-1, keepdims=True) + RMS_EPS)
        n = xv * r
        err = n * gv - t_ref[...]
        acc_ref[...] += jnp.sum(err * err, axis=0, keepdims=True)
        dy = err * (1.0 / D)
        dg_ref[...] += jnp.sum(dy * n, axis=0, keepdims=True)
        dn = dy * gv
        dx_ref[...] = r * (dn - n * jnp.mean(dn * n, axis=-1, keepdims=True))

        @pl.when(i == nt - 1)
        def _():
            tot = jnp.sum(acc_ref[...], axis=1, keepdims=True) * (0.5 / D)
            loss_ref[...] = jnp.broadcast_to(tot, loss_ref.shape)

    return pl.pallas_call(
        body, grid=(nt,), in_specs=[_row(D), _vec(D), _row(D)],
        out_specs=[_vec(128), _row(D), _vec(D)],
        out_shape=[SDS((1, 128), F32), SDS((t, D), F32), SDS((1, D), F32)],
        scratch_shapes=[pltpu.VMEM((1, D), F32)],
        compiler_params=_params(("arbitrary",)), name=name,
    )(x, fg, target)


def _halo_prev(width, col):
    per = TR // HALO
    return pl.BlockSpec((HALO, width), lambda i: (jnp.maximum(i * per - 1, 0), col))


def _halo_next(width, col, nt):
    per = TR // HALO
    return pl.BlockSpec((HALO, width), lambda i: (jnp.minimum((i + 1) * per, nt * per - 1), col))


def _pool_windows(ext, tile_index):
    rows = _iota((TR, PG), 0) + tile_index * TR + 1
    pooled, counts = [], []
    for gi in range(4):
        w = 2 << gi
        e = ext[:, gi * PG:(gi + 1) * PG]
        s = e
        step = 1
        while step < w:
            s = s + pltpu.roll(s, step, 0)
            step *= 2
        cnt = jnp.minimum(rows, w).astype(F32)
        pooled.append(s[HALO:] / cnt - e[HALO:])
        counts.append(cnt)
    return pooled, counts


def _pool_fwd(proj, pool_w, pool_scale, pool_proj, name):
    t = proj.shape[0]
    xcol = OFF_XP // PW

    def body(x_ref, h_ref, pw_ref, ps_ref, pp_ref, o_ref):
        i = pl.program_id(0)
        halo = jnp.where(i > 0, h_ref[...], 0.0)
        ext = jnp.concatenate([halo, x_ref[...]], axis=0)
        pooled, _ = _pool_windows(ext, i)
        mixed = [_dg(pooled[g].astype(BF16), pw_ref[g].astype(BF16), NN) for g in range(4)]
        ypre = jnp.concatenate(mixed, axis=1) * ps_ref[...]
        o_ref[...] = _dg(ypre.astype(BF16), pp_ref[...], NN)

    return pl.pallas_call(
        body, grid=(t // TR,),
        in_specs=[_row(PW, xcol), _halo_prev(PW, xcol),
                  pl.BlockSpec((4, PG, PG), lambda i: (0, 0, 0)), _vec(PW),
                  pl.BlockSpec((PW, D), lambda i: (0, 0))],
        out_specs=_row(D), out_shape=SDS((t, D), F32),
        compiler_params=_params(("parallel",)), name=name,
    )(proj, proj, pool_w, pool_scale, pool_proj)


def _pool_bwd_local(proj, pool_w, pool_scale, pool_proj, dya, name):
    t = proj.shape[0]
    xcol = OFF_XP // PW

    def body(x_ref, h_ref, pw_ref, ps_ref, pp_ref, dya_ref, dwin_ref, dpl_ref, dpw_ref, dps_ref, dpp_ref):
        i = pl.program_id(0)

        @pl.when(i == 0)
        def _():
            dpw_ref[...] = jnp.zeros_like(dpw_ref)
            dps_ref[...] = jnp.zeros_like(dps_ref)
            dpp_ref[...] = jnp.zeros_like(dpp_ref)

        halo = jnp.where(i > 0, h_ref[...], 0.0)
        ext = jnp.concatenate([halo, x_ref[...]], axis=0)
        pooled, counts = _pool_windows(ext, i)
        mixed = jnp.concatenate(
            [_dg(pooled[g].astype(BF16), pw_ref[g].astype(BF16), NN) for g in range(4)], axis=1)
        ps = ps_ref[...]
        ypre = mixed * ps
        dyab = dya_ref[...].astype(BF16)
        dypre = _dg(dyab, pp_ref[...], NT)
        dpp_ref[...] += _dg(ypre.astype(BF16), dyab, TN)
        dps_ref[...] += jnp.sum(dypre * mixed, axis=0, keepdims=True)
        dmixed = dypre * ps
        for g in range(4):
            dm = dmixed[:, g * PG:(g + 1) * PG].astype(BF16)
            dpw_ref[g] += _dg(pooled[g].astype(BF16), dm, TN)
            dpooled = _dg(dm, pw_ref[g].astype(BF16), NT)
            dwin_ref[:, g * PG:(g + 1) * PG] = dpooled / counts[g]
            dpl_ref[:, g * PG:(g + 1) * PG] = dpooled

    return pl.pallas_call(
        body, grid=(t // TR,),
        in_specs=[_row(PW, xcol), _halo_prev(PW, xcol),
                  pl.BlockSpec((4, PG, PG), lambda i: (0, 0, 0)), _vec(PW),
                  pl.BlockSpec((PW, D), lambda i: (0, 0)), _row(D)],
        out_specs=[_row(PW), _row(PW), pl.BlockSpec((4, PG, PG), lambda i: (0, 0, 0)), _vec(PW),
                   pl.BlockSpec((PW, D), lambda i: (0, 0))],
        out_shape=[SDS((t, PW), F32), SDS((t, PW), F32), SDS((4, PG, PG), F32), SDS((1, PW), F32),
                   SDS((PW, D), F32)],
        compiler_params=_params(("arbitrary",)), name=name,
    )(proj, proj, pool_w, pool_scale, pool_proj, dya)


def _pool_bwd_window(dwin, dpl, name):
    t = dwin.shape[0]
    nt = t // TR
    ext_rows = TR + HALO

    def body(dw_ref, h_ref, dp_ref, o_ref):
        i = pl.program_id(0)
        halo = jnp.where(i < nt - 1, h_ref[...], 0.0)
        ext = jnp.concatenate([dw_ref[...], halo], axis=0)
        for gi in range(4):
            w = 2 << gi
            s = ext[:, gi * PG:(gi + 1) * PG]
            step = 1
            while step < w:
                s = s + pltpu.roll(s, ext_rows - step, 0)
                step *= 2
            o_ref[:, gi * PG:(gi + 1) * PG] = (s[:TR] - dp_ref[:, gi * PG:(gi + 1) * PG]).astype(o_ref.dtype)

    return pl.pallas_call(
        body, grid=(nt,), in_specs=[_row(PW), _halo_next(PW, 0, nt), _row(PW)], out_specs=_row(PW),
        out_shape=SDS((t, PW), BF16), compiler_params=_params(("parallel",)), name=name,
    )(dwin, dwin, dpl)


def _conv_group(ext, cw_ref, cols):
    acc = cw_ref[3:4, cols] * ext
    for j in range(3):
        acc = acc + cw_ref[j:j + 1, cols] * pltpu.roll(ext, 3 - j, 0)
    return acc[HALO:]


def _gate_terms(raw, al, dt):
    beta = _sigmoid(raw)
    xg = raw + dt
    sp = jnp.maximum(xg, 0.0) + jnp.log(1.0 + jnp.exp(-jnp.abs(xg)))
    g = -jnp.exp(al) * sp
    return beta, g, _sigmoid(xg)


def _dn_pre_fwd(proj, conv_w, al_row, dt_row, name):
    t = proj.shape[0]

    def body(x_ref, h_ref, cw_ref, ba_ref, al_ref, dt_ref, q_ref, k_ref, v_ref, bg_ref):
        i = pl.program_id(0)
        keep = i > 0
        for grp in range(24):
            cols = slice(grp * HD, (grp + 1) * HD)
            ext = jnp.concatenate([jnp.where(keep, h_ref[:, cols], 0.0), x_ref[:, cols]], axis=0)
            s = _silu(_conv_group(ext, cw_ref, cols))
            seg, head = divmod(grp, NH)
            hc = slice(head * HD, (head + 1) * HD)
            if seg == 0:
                q_ref[:, hc] = s * lax.rsqrt(jnp.sum(s * s, axis=-1, keepdims=True) + L2_EPS) * (HD ** -0.5)
            elif seg == 1:
                k_ref[:, hc] = s * lax.rsqrt(jnp.sum(s * s, axis=-1, keepdims=True) + L2_EPS)
            else:
                v_ref[:, hc] = s
        lane = _iota((TR, 128), 1)
        rowc = _iota((TR, 128), 0) % CH
        beta, g, _ = _gate_terms(ba_ref[...], al_ref[...], dt_ref[...])
        step = 1
        while step < CH:
            g = g + jnp.where(rowc >= step, pltpu.roll(g, step, 0), 0.0)
            step *= 2
        bg_ref[...] = jnp.where(lane < NH, beta, jnp.where(lane < 2 * NH, g, 0.0))

    return pl.pallas_call(
        body, grid=(t // TR,),
        in_specs=[_row(3 * D, 0), _halo_prev(3 * D, 0), pl.BlockSpec((4, 3 * D), lambda i: (0, 0)),
                  _row(128, OFF_BA // 128), _vec(128), _vec(128)],
        out_specs=[_row(D), _row(D), _row(D), _row(128)],
        out_shape=[SDS((t, D), F32), SDS((t, D), F32), SDS((t, D), F32), SDS((t, 128), F32)],
        compiler_params=_params(("parallel",)), name=name,
    )(proj, proj, conv_w, proj, al_row, dt_row)


def _dn_pre_bwd_act(proj, conv_w, al_row, dt_row, dq, dk, dv, dbg, name):
    t = proj.shape[0]

    def body(x_ref, h_ref, cw_ref, ba_ref, al_ref, dt_ref, dq_ref, dk_ref, dv_ref, dbg_ref,
             dc_ref, draw_ref, dal_ref, ddt_ref):
        i = pl.program_id(0)

        @pl.when(i == 0)
        def _():
            dal_ref[...] = jnp.zeros_like(dal_ref)
            ddt_ref[...] = jnp.zeros_like(ddt_ref)

        keep = i > 0
        for grp in range(24):
            cols = slice(grp * HD, (grp + 1) * HD)
            ext = jnp.concatenate([jnp.where(keep, h_ref[:, cols], 0.0), x_ref[:, cols]], axis=0)
            cv = _conv_group(ext, cw_ref, cols)
            seg, head = divmod(grp, NH)
            hc = slice(head * HD, (head + 1) * HD)
            if seg == 2:
                ds = dv_ref[:, hc]
            else:
                s = _silu(cv)
                r = lax.rsqrt(jnp.sum(s * s, axis=-1, keepdims=True) + L2_EPS)
                dy = dq_ref[:, hc] if seg == 0 else dk_ref[:, hc]
                c = (HD ** -0.5) if seg == 0 else 1.0
                ds = (c * r) * (dy - s * ((r * r) * jnp.sum(dy * s, axis=-1, keepdims=True)))
            dc_ref[:, cols] = ds * _dsilu(cv)
        lane = _iota((TR, 128), 1)
        rowc = _iota((TR, 128), 0) % CH
        isb = lane < NH
        isg = jnp.logical_and(lane >= NH, lane < 2 * NH)
        beta, g, sg = _gate_terms(ba_ref[...], al_ref[...], dt_ref[...])
        dbgv = dbg_ref[...]
        dg = dbgv
        step = 1
        while step < CH:
            dg = dg + jnp.where(rowc < CH - step, pltpu.roll(dg, TR - step, 0), 0.0)
            step *= 2
        da_raw = dg * (-jnp.exp(al_ref[...])) * sg
        draw_ref[...] = jnp.where(isb, dbgv * beta * (1.0 - beta), jnp.where(isg, da_raw, 0.0)).astype(draw_ref.dtype)
        dal_ref[...] += jnp.sum(jnp.where(isg, dg * g, 0.0), axis=0, keepdims=True)
        ddt_ref[...] += jnp.sum(jnp.where(isg, da_raw, 0.0), axis=0, keepdims=True)

    return pl.pallas_call(
        body, grid=(t // TR,),
        in_specs=[_row(3 * D, 0), _halo_prev(3 * D, 0), pl.BlockSpec((4, 3 * D), lambda i: (0, 0)),
                  _row(128, OFF_BA // 128), _vec(128), _vec(128), _row(D), _row(D), _row(D), _row(128)],
        out_specs=[_row(3 * D), _row(128), _vec(128), _vec(128)],
        out_shape=[SDS((t, 3 * D), F32), SDS((t, 128), BF16), SDS((1, 128), F32), SDS((1, 128), F32)],
        compiler_params=_params(("arbitrary",)), name=name,
    )(proj, proj, conv_w, proj, al_row, dt_row, dq, dk, dv, dbg)


def _dn_pre_bwd_conv(proj, conv_w, dconv, name):
    t = proj.shape[0]
    nt = t // TR
    ext_rows = TR + HALO

    def body(x_ref, h_ref, cw_ref, dc_ref, dn_ref, dx_ref, dcw_ref):
        i = pl.program_id(0)

        @pl.when(i == 0)
        def _():
            dcw_ref[...] = jnp.zeros_like(dcw_ref)

        keep_prev = i > 0
        keep_next = i < nt - 1
        for grp in range(24):
            cols = slice(grp * HD, (grp + 1) * HD)
            dct = dc_ref[:, cols]
            dext = jnp.concatenate([dct, jnp.where(keep_next, dn_ref[:, cols], 0.0)], axis=0)
            acc = cw_ref[3:4, cols] * dext
            for j in range(3):
                acc = acc + cw_ref[j:j + 1, cols] * pltpu.roll(dext, ext_rows - (3 - j), 0)
            dx_ref[:, cols] = acc[:TR].astype(dx_ref.dtype)
            xext = jnp.concatenate([jnp.where(keep_prev, h_ref[:, cols], 0.0), x_ref[:, cols]], axis=0)
            for j in range(4):
                xs = xext if j == 3 else pltpu.roll(xext, 3 - j, 0)
                dcw_ref[j:j + 1, cols] += jnp.sum(xs[HALO:] * dct, axis=0, keepdims=True)

    return pl.pallas_call(
        body, grid=(nt,),
        in_specs=[_row(3 * D, 0), _halo_prev(3 * D, 0), pl.BlockSpec((4, 3 * D), lambda i: (0, 0)),
                  _row(3 * D), _halo_next(3 * D, 0, nt)],
        out_specs=[_row(3 * D), pl.BlockSpec((4, 3 * D), lambda i: (0, 0))],
        out_shape=[SDS((t, 3 * D), BF16), SDS((4, 3 * D), F32)],
        compiler_params=_params(("arbitrary",)), name=name,
    )(proj, proj, conv_w, dconv, dconv)


def _dn_post_fwd(o, proj, gn, name):
    t = o.shape[0]

    def body(o_ref, z_ref, g_ref, out_ref):
        gv = g_ref[...]
        for h in range(NH):
            hc = slice(h * HD, (h + 1) * HD)
            ov = o_ref[:, hc]
            r = lax.rsqrt(jnp.mean(ov * ov, axis=-1, keepdims=True) + RMS_EPS)
            out_ref[:, hc] = (((ov * r) * gv) * _silu(z_ref[:, hc])).astype(out_ref.dtype)

    return pl.pallas_call(
        body, grid=(t // TR,), in_specs=[_row(D), _row(D, OFF_Z // D), _vec(HD)], out_specs=_row(D),
        out_shape=SDS((t, D), BF16), compiler_params=_params(("parallel",)), name=name,
    )(o, proj, gn)


def _dn_post_bwd(o, proj, gn, dob, name):
    t = o.shape[0]

    def body(o_ref, z_ref, g_ref, d_ref, do_ref, dz_ref, dg_ref):
        @pl.when(pl.program_id(0) == 0)
        def _():
            dg_ref[...] = jnp.zeros_like(dg_ref)

        gv = g_ref[...]
        acc = jnp.zeros((1, HD), F32)
        for h in range(NH):
            hc = slice(h * HD, (h + 1) * HD)
            ov = o_ref[:, hc]
            zv = z_ref[:, hc]
            dv = d_ref[:, hc]
            r = lax.rsqrt(jnp.mean(ov * ov, axis=-1, keepdims=True) + RMS_EPS)
            n = ov * r
            dz_ref[:, hc] = (dv * (n * gv) * _dsilu(zv)).astype(dz_ref.dtype)
            dng = dv * _silu(zv)
            acc = acc + jnp.sum(dng * n, axis=0, keepdims=True)
            dn = dng * gv
            do_ref[:, hc] = r * (dn - n * jnp.mean(dn * n, axis=-1, keepdims=True))
        dg_ref[...] += acc

    return pl.pallas_call(
        body, grid=(t // TR,), in_specs=[_row(D), _row(D, OFF_Z // D), _vec(HD), _row(D)],
        out_specs=[_row(D), _row(D), _vec(HD)],
        out_shape=[SDS((t, D), F32), SDS((t, D), BF16), SDS((1, HD), F32)],
        compiler_params=_params(("arbitrary",)), name=name,
    )(o, proj, gn, dob)


def _merge_fwd(ya, yb, proj, name):
    t = ya.shape[0]

    def body(a_ref, b_ref, gp_ref, gd_ref, o_ref):
        o_ref[...] = (_sigmoid(gp_ref[...]) * a_ref[...] + _sigmoid(gd_ref[...]) * b_ref[...]).astype(o_ref.dtype)

    return pl.pallas_call(
        body, grid=(t // TR,), in_specs=[_row(D), _row(D), _row(D, OFF_GP // D), _row(D, OFF_GD // D)],
        out_specs=_row(D), out_shape=SDS((t, D), BF16),
        compiler_params=_params(("parallel",)), name=name,
    )(ya, yb, proj, proj)


def _merge_bwd(dm, ya, yb, proj, name):
    t = ya.shape[0]

    def body(d_ref, a_ref, b_ref, gp_ref, gd_ref, da_ref, db_ref, dgp_ref, dgd_ref):
        dv = d_ref[...]
        sp = _sigmoid(gp_ref[...])
        sd = _sigmoid(gd_ref[...])
        da_ref[...] = dv * sp
        db_ref[...] = (dv * sd).astype(db_ref.dtype)
        dgp_ref[...] = (dv * a_ref[...] * sp * (1.0 - sp)).astype(dgp_ref.dtype)
        dgd_ref[...] = (dv * b_ref[...] * sd * (1.0 - sd)).astype(dgd_ref.dtype)

    return pl.pallas_call(
        body, grid=(t // TR,),
        in_specs=[_row(D), _row(D), _row(D), _row(D, OFF_GP // D), _row(D, OFF_GD // D)],
        out_specs=[_row(D)] * 4,
        out_shape=[SDS((t, D), F32), SDS((t, D), BF16), SDS((t, D), BF16), SDS((t, D), BF16)],
        compiler_params=_params(("parallel",)), name=name,
    )(dm, ya, yb, proj, proj)


def _split2(x):
    hi = x.astype(BF16)
    return hi, (x - hi.astype(F32)).astype(BF16)


def _dot3(a, b, dims):
    ah, al = _split2(a)
    bh, bl = _split2(b)
    return _dg(ah, bh, dims) + (_dg(ah, bl, dims) + _dg(al, bh, dims))


def _neumann_inverses(mats):
    ri = _iota((CH, CH), 0)
    ci = _iota((CH, CH), 1)
    eye = jnp.where(ri == ci, 1.0, 0.0).astype(F32)
    xs = [-a for a in mats]
    ps = [eye + x for x in xs]
    for _ in range(5):
        xs = [_dot3(x, x, NN) for x in xs]
        ps = [p + _dot3(p, x, NN) for p, x in zip(ps, xs)]
    return ps


def _solve_with(inv):
    @jax.custom_vjp
    def solve(a, rhs):
        return _dot3(inv, rhs, NN)

    def fwd(a, rhs):
        sol = _dot3(inv, rhs, NN)
        return sol, sol

    def bwd(sol, d):
        drhs = _dot3(inv, d, TN)
        return -_dot3(drhs, sol, NT), drhs

    solve.defvjp(fwd, bwd)
    return solve


@jax.custom_vjp
def _rows_to_lanes(g64):
    ri = _iota((CH, CH), 0)
    ci = _iota((CH, CH), 1)
    diag = jnp.where(ri == ci, g64, 0.0)
    ones = jnp.ones((CH, CH), BF16)
    hi = diag.astype(BF16)
    rem = diag - hi.astype(F32)
    mid = rem.astype(BF16)
    lo = (rem - mid.astype(F32)).astype(BF16)
    return _dg(ones, hi, NN) + (_dg(ones, mid, NN) + _dg(ones, lo, NN))


def _rows_to_lanes_bwd(_, d):
    ri = _iota((CH, CH), 0)
    ci = _iota((CH, CH), 1)
    return (jnp.where(ri == ci, jnp.broadcast_to(jnp.sum(d, axis=0, keepdims=True), (CH, CH)), 0.0),)


_rows_to_lanes.defvjp(lambda g64: (_rows_to_lanes(g64), None), _rows_to_lanes_bwd)


def _chunk_local(solve_all, q, k, v, g128, g64, gl128, b128, b64):
    ri = _iota((CH, CH), 0)
    ci = _iota((CH, CH), 1)
    causal = ri >= ci
    strict = ri > ci
    gj = [_rows_to_lanes(g) for g in g64]
    decay = [jnp.where(causal, jnp.exp(jnp.where(causal, g - t, 0.0)), 0.0) for g, t in zip(g64, gj)]
    kk = [_nt(x, x) for x in k]
    a = [jnp.where(strict, b * m * dc, 0.0) for b, m, dc in zip(b64, kk, decay)]
    eg = [jnp.exp(g) for g in g128]
    rhs = [jnp.concatenate([b * x, (b * e) * y], axis=1) for b, x, e, y in zip(b128, v, eg, k)]
    sol = solve_all(a, rhs)
    qk = [jnp.where(causal, _nt(x, y) * dc, 0.0) for x, y, dc in zip(q, k, decay)]
    return ([s[:, :HD] for s in sol], [s[:, HD:] for s in sol], qk, [x * e for x, e in zip(q, eg)],
            [x * jnp.exp(gl - g) for x, gl, g in zip(k, gl128, g128)], [jnp.exp(gl) for gl in gl128])


def _all_head_gates(bgv):
    return tuple(list(z) for z in zip(*[_head_gates(bgv, h) for h in range(NH)]))


def _head_gates(bgv, h):
    lane = _iota((CH, 128), 1)
    row = _iota((CH, 128), 0)
    bcol = jnp.sum(jnp.where(lane == h, bgv, 0.0), axis=1, keepdims=True)
    gcol = jnp.sum(jnp.where(lane == NH + h, bgv, 0.0), axis=1, keepdims=True)
    g128 = jnp.broadcast_to(gcol, (CH, 128))
    gl128 = jnp.broadcast_to(jnp.sum(jnp.where(row == CH - 1, g128, 0.0), axis=0, keepdims=True), (CH, 128))
    return (g128, jnp.broadcast_to(gcol, (CH, CH)), gl128,
            jnp.broadcast_to(bcol, (CH, 128)), jnp.broadcast_to(bcol, (CH, CH)))


def _chunk_specs():
    row = pl.BlockSpec((CH, D), lambda i: (i, 0))
    small = pl.BlockSpec((CH, 128), lambda i: (i, 0))
    qk = pl.BlockSpec((NH, CH, CH), lambda i: (i, 0, 0))
    eg = pl.BlockSpec((1, NH, 128), lambda i: (i, 0, 0))
    return row, small, qk, eg


def _dn_local_fwd(q, k, v, bg, name):
    t = q.shape[0]
    n = t // CH

    def body(q_ref, k_ref, v_ref, bg_ref, u_ref, w_ref, qk_ref, qd_ref, kd_ref, eg_ref, inv_ref):
        cols = [slice(h * HD, (h + 1) * HD) for h in range(NH)]

        def solve_all(mats, rhs):
            invs = _neumann_inverses(mats)
            for h in range(NH):
                inv_ref[h] = invs[h]
            return [_dot3(m, r, NN) for m, r in zip(invs, rhs)]

        u, w, qk, qd, kd, egl = _chunk_local(
            solve_all, [q_ref[:, c] for c in cols], [k_ref[:, c] for c in cols], [v_ref[:, c] for c in cols],
            *_all_head_gates(bg_ref[...]))
        for h, hc in enumerate(cols):
            u_ref[:, hc] = u[h]
            w_ref[:, hc] = w[h].astype(w_ref.dtype)
            qd_ref[:, hc] = qd[h].astype(qd_ref.dtype)
            kd_ref[:, hc] = kd[h].astype(kd_ref.dtype)
            qk_ref[h] = qk[h].astype(qk_ref.dtype)
            eg_ref[0, h:h + 1, :] = egl[h][0:1, :]

    row, small, qkb, egb = _chunk_specs()
    return pl.pallas_call(
        body, grid=(n,), in_specs=[row, row, row, small], out_specs=[row, row, qkb, row, row, egb, qkb],
        out_shape=[SDS((t, D), F32), SDS((t, D), BF16), SDS((n * NH, CH, CH), BF16), SDS((t, D), BF16),
                   SDS((t, D), BF16), SDS((n, NH, 128), F32), SDS((n * NH, CH, CH), F32)],
        compiler_params=_params(("parallel",)), name=name,
    )(q, k, v, bg)


def _dn_local_bwd(q, k, v, bg, inv, du, dw, dqk, dqd, dkd, deg, name):
    t = q.shape[0]
    n = t // CH

    def body(q_ref, k_ref, v_ref, bg_ref, inv_ref, du_ref, dw_ref, dqk_ref, dqd_ref, dkd_ref, deg_ref,
             dq_ref, dk_ref, dv_ref, dbg_ref):
        bgv = bg_ref[...]
        lane = _iota((CH, 128), 1)
        row = _iota((CH, 128), 0)
        first = jnp.where(row == 0, 1.0, 0.0)
        acc = jnp.zeros((CH, 128), F32)
        cols = [slice(h * HD, (h + 1) * HD) for h in range(NH)]
        solves = [_solve_with(inv_ref[h]) for h in range(NH)]

        def solve_all(mats, rhs):
            return [f(m, r) for f, m, r in zip(solves, mats, rhs)]

        _, vjp = jax.vjp(functools.partial(_chunk_local, solve_all),
                         [q_ref[:, c] for c in cols], [k_ref[:, c] for c in cols], [v_ref[:, c] for c in cols],
                         *_all_head_gates(bgv))
        cts = ([du_ref[:, c] for c in cols], [dw_ref[:, c] for c in cols], [dqk_ref[h] for h in range(NH)],
               [dqd_ref[:, c] for c in cols], [dkd_ref[:, c] for c in cols],
               [jnp.broadcast_to(deg_ref[0, h:h + 1, :], (CH, 128)) * first for h in range(NH)])
        dq, dk, dv, dg128, dg64, dgl, db128, db64 = vjp(cts)
        for h, hc in enumerate(cols):
            dq_ref[:, hc] = dq[h]
            dk_ref[:, hc] = dk[h]
            dv_ref[:, hc] = dv[h]
            dg = jnp.sum(dg128[h], axis=1, keepdims=True) + jnp.sum(dg64[h], axis=1, keepdims=True)
            tot = jnp.sum(jnp.sum(dgl[h], axis=0, keepdims=True), axis=1, keepdims=True)
            dg = dg + jnp.where(row[:, 0:1] == CH - 1, tot, 0.0)
            db = jnp.sum(db128[h], axis=1, keepdims=True) + jnp.sum(db64[h], axis=1, keepdims=True)
            acc = acc + jnp.where(lane == h, db, 0.0) + jnp.where(lane == NH + h, dg, 0.0)
        dbg_ref[...] = acc

    row, small, qkb, egb = _chunk_specs()
    return pl.pallas_call(
        body, grid=(n,), in_specs=[row, row, row, small, qkb, row, row, qkb, row, row, egb],
        out_specs=[row, row, row, small],
        out_shape=[SDS((t, D), F32)] * 3 + [SDS((t, 128), F32)],
        compiler_params=_params(("parallel",)), name=name,
    )(q, k, v, bg, inv, du, dw, dqk, dqd, dkd, deg)


def _state_step(s, u, w, qk, qd, kd, egl):
    ws = [_nn(a, b) for a, b in zip(w, s)]
    v_new = [a - b for a, b in zip(u, ws)]
    qs = [_nn(a, b) for a, b in zip(qd, s)]
    intra = [_nn(a, b) for a, b in zip(qk, v_new)]
    upd = [_tn(a, b) for a, b in zip(kd, v_new)]
    return [a * e + b for a, e, b in zip(s, egl, upd)], [a + b for a, b in zip(qs, intra)]


def _dn_scan_fwd(u, w, qk, qd, kd, eg, name):
    t = u.shape[0]
    n = t // CH

    def body(u_ref, w_ref, qk_ref, qd_ref, kd_ref, eg_ref, o_ref, save_ref, s_ref):
        @pl.when(pl.program_id(0) == 0)
        def _():
            s_ref[...] = jnp.zeros_like(s_ref)

        cols = [slice(h * HD, (h + 1) * HD) for h in range(NH)]
        s = [s_ref[h] for h in range(NH)]
        for h in range(NH):
            save_ref[0, h] = s[h]
        s_new, o = _state_step(
            s, [u_ref[:, c] for c in cols], [w_ref[:, c].astype(F32) for c in cols],
            [qk_ref[h].astype(F32) for h in range(NH)], [qd_ref[:, c].astype(F32) for c in cols],
            [kd_ref[:, c].astype(F32) for c in cols], [eg_ref[0, h:h + 1, :] for h in range(NH)])
        for h, hc in enumerate(cols):
            o_ref[:, hc] = o[h]
            s_ref[h] = s_new[h]

    row, _, qkb, egb = _chunk_specs()
    return pl.pallas_call(
        body, grid=(n,), in_specs=[row, row, qkb, row, row, egb],
        out_specs=[row, pl.BlockSpec((1, NH, HD, HD), lambda i: (i, 0, 0, 0))],
        out_shape=[SDS((t, D), F32), SDS((n, NH, HD, HD), F32)],
        scratch_shapes=[pltpu.VMEM((NH, HD, HD), F32)],
        compiler_params=_params(("arbitrary",)), name=name,
    )(u, w, qk, qd, kd, eg)


def _dn_scan_bwd(u, w, qk, qd, kd, eg, saved, do, name):
    t = u.shape[0]
    n = t // CH

    def body(u_ref, w_ref, qk_ref, qd_ref, kd_ref, eg_ref, sv_ref, do_ref,
             du_ref, dw_ref, dqk_ref, dqd_ref, dkd_ref, deg_ref, ds_ref):
        @pl.when(pl.program_id(0) == 0)
        def _():
            ds_ref[...] = jnp.zeros_like(ds_ref)

        cols = [slice(h * HD, (h + 1) * HD) for h in range(NH)]
        _, vjp = jax.vjp(
            _state_step, [sv_ref[0, h] for h in range(NH)], [u_ref[:, c] for c in cols],
            [w_ref[:, c].astype(F32) for c in cols], [qk_ref[h].astype(F32) for h in range(NH)],
            [qd_ref[:, c].astype(F32) for c in cols], [kd_ref[:, c].astype(F32) for c in cols],
            [eg_ref[0, h:h + 1, :] for h in range(NH)])
        ds, du, dw, dqk, dqd, dkd, deg = vjp(([ds_ref[h] for h in range(NH)], [do_ref[:, c] for c in cols]))
        for h, hc in enumerate(cols):
            ds_ref[h] = ds[h]
            du_ref[:, hc] = du[h]
            dw_ref[:, hc] = dw[h]
            dqk_ref[h] = dqk[h]
            dqd_ref[:, hc] = dqd[h]
            dkd_ref[:, hc] = dkd[h]
            deg_ref[0, h:h + 1, :] = deg[h]

    rev = lambda i: (n - 1 - i, 0)
    rev3 = lambda i: (n - 1 - i, 0, 0)
    row = pl.BlockSpec((CH, D), rev)
    qkb = pl.BlockSpec((NH, CH, CH), rev3)
    egb = pl.BlockSpec((1, NH, 128), rev3)
    return pl.pallas_call(
        body, grid=(n,),
        in_specs=[row, row, qkb, row, row, egb,
                  pl.BlockSpec((1, NH, HD, HD), lambda i: (n - 1 - i, 0, 0, 0)), row],
        out_specs=[row, row, qkb, row, row, egb],
        out_shape=[SDS((t, D), F32), SDS((t, D), F32), SDS((n * NH, CH, CH), F32), SDS((t, D), F32),
                   SDS((t, D), F32), SDS((n, NH, 128), F32)],
        scratch_shapes=[pltpu.VMEM((NH, HD, HD), F32)],
        compiler_params=_params(("arbitrary",)), name=name,
    )(u, w, qk, qd, kd, eg, saved, do)


def _ada_fwd(c_all, ada_w, ada_b, name):
    ncol = ada_w.shape[1]

    def body(c_ref, w_ref, b_ref, o_ref):
        o_ref[...] = _dg(_silu(c_ref[...]), w_ref[...], NN, HI) + b_ref[...]

    return pl.pallas_call(body, out_shape=SDS((NDEV, ncol), F32),
                          compiler_params=pltpu.CompilerParams(vmem_limit_bytes=VMEM_LIMIT), name=name,
                          )(c_all, ada_w, ada_b)


def _ada_bwd(c_all_t, dmod, name):
    ncol = dmod.shape[1]

    def body(c_ref, d_ref, o_ref):
        sc = _silu(c_ref[...])
        acc = sc[:, 0:1] * d_ref[0:1, :]
        for b in range(1, NDEV):
            acc = acc + sc[:, b:b + 1] * d_ref[b:b + 1, :]
        o_ref[...] = acc

    return pl.pallas_call(body, out_shape=SDS((D, ncol), F32),
                          compiler_params=pltpu.CompilerParams(vmem_limit_bytes=VMEM_LIMIT), name=name,
                          )(c_all_t, dmod)


def _sum_devices(parts, out_dtype, name):
    _, r, c = parts.shape
    tr = TR if r % TR == 0 else r

    def body(p_ref, o_ref):
        acc = p_ref[0].astype(F32)
        for i in range(1, NDEV):
            acc = acc + p_ref[i].astype(F32)
        o_ref[...] = acc.astype(o_ref.dtype)

    return pl.pallas_call(
        body, grid=(r // tr,), in_specs=[pl.BlockSpec((NDEV, tr, c), lambda i: (0, i, 0))],
        out_specs=pl.BlockSpec((tr, c), lambda i: (i, 0)), out_shape=SDS((r, c), out_dtype),
        compiler_params=_params(("parallel",)), name=name,
    )(parts)


def _adamw(w, g, m, v, name):
    r, c = w.shape
    tr = _pick(r, (256, 128, 88, 8)) if r % 8 == 0 else r
    bc1 = 1.0 - ADAM_B1 ** ADAM_STEP
    bc2 = 1.0 - ADAM_B2 ** ADAM_STEP

    def body(w_ref, g_ref, m_ref, v_ref, d_ref, nm_ref, nv_ref):
        gv = g_ref[...]
        m_new = ADAM_B1 * m_ref[...] + (1.0 - ADAM_B1) * gv
        v_new = ADAM_B2 * v_ref[...] + (1.0 - ADAM_B2) * (gv * gv)
        nm_ref[...] = m_new
        nv_ref[...] = v_new
        d_ref[...] = -ADAM_LR * ((m_new / bc1) / (jnp.sqrt(v_new / bc2) + ADAM_EPS) + ADAM_WD * w_ref[...])

    spec = pl.BlockSpec((tr, c), lambda i: (i, 0))
    return pl.pallas_call(
        body, grid=(r // tr,), in_specs=[spec] * 4, out_specs=[spec] * 3,
        out_shape=[SDS((r, c), F32)] * 3, compiler_params=_params(("parallel",)), name=name,
    )(w, g, m, v)


ANY = pl.BlockSpec(memory_space=pl.ANY)
MESH = pl.DeviceIdType.MESH


def _all_gather(x, name):
    r, c_dim = x.shape

    def body(x_ref, out_ref, send_sems, recv_sems, local_sem):
        mx, my, mc = lax.axis_index("x"), lax.axis_index("y"), lax.axis_index("c")
        me, sibling = (mx, my, mc), (mx, my, 1 - mc)
        chips = [(1 - mx, my), (mx, 1 - my), (1 - mx, 1 - my)]

        def rows(px, py, pc):
            return out_ref.at[4 * px + 2 * py + pc]

        def copy(k, block, to, src=None):
            return pltpu.make_async_remote_copy(
                src_ref=rows(*block) if src is None else src, dst_ref=rows(*block),
                send_sem=send_sems.at[k], recv_sem=recv_sems.at[k], device_id=to, device_id_type=MESH)

        mine = pltpu.make_async_copy(x_ref, rows(*me), local_sem)
        mine.start()
        first = [copy(0, me, sibling, src=x_ref)]
        first += [copy(1 + j, me, (*chip, mc), src=x_ref) for j, chip in enumerate(chips)]
        for cp in first:
            cp.start()
        passed = [copy(4 + j, (*chip, mc), sibling) for j, chip in enumerate(chips)]
        for j, chip in enumerate(chips):
            copy(1 + j, (*chip, mc), me).wait_recv()
            passed[j].start()
        copy(0, sibling, me).wait_recv()
        for j, chip in enumerate(chips):
            copy(4 + j, (*chip, 1 - mc), me).wait_recv()
        for cp in first + passed:
            cp.wait_send()
        mine.wait()

    return pl.pallas_call(
        body, out_shape=SDS((NDEV, r, c_dim), x.dtype), in_specs=[ANY], out_specs=ANY,
        scratch_shapes=[pltpu.SemaphoreType.DMA((7,)), pltpu.SemaphoreType.DMA((7,)), pltpu.SemaphoreType.DMA],
        name=name,
    )(x)


def _exchange_blocks(parts, name):
    _, r, c_dim = parts.shape

    def body(p_ref, out_ref, send_sems, recv_sems, local_sem):
        mx, my, mc = lax.axis_index("x"), lax.axis_index("y"), lax.axis_index("c")
        me = 4 * mx + 2 * my + mc
        mine = pltpu.make_async_copy(p_ref.at[me], out_ref.at[me], local_sem)
        mine.start()
        copies = []
        for k in range(1, NDEV):
            px = 1 - mx if k & 4 else mx
            py = 1 - my if k & 2 else my
            pc = 1 - mc if k & 1 else mc
            copies.append(pltpu.make_async_remote_copy(
                src_ref=p_ref.at[4 * px + 2 * py + pc], dst_ref=out_ref.at[me],
                send_sem=send_sems.at[k - 1], recv_sem=recv_sems.at[k - 1],
                device_id=(px, py, pc), device_id_type=MESH))
        for cp in copies:
            cp.start()
        for cp in copies:
            cp.wait_recv()
        for cp in copies:
            cp.wait_send()
        mine.wait()

    return pl.pallas_call(
        body, out_shape=SDS((NDEV, r, c_dim), parts.dtype), in_specs=[ANY], out_specs=ANY,
        scratch_shapes=[pltpu.SemaphoreType.DMA((7,)), pltpu.SemaphoreType.DMA((7,)), pltpu.SemaphoreType.DMA],
        name=name,
    )(parts)


def _cols_from_blocks(blocks, rows):
    w = blocks.shape[1] * blocks.shape[2] // rows
    return blocks.reshape(NDEV, rows, w).transpose(1, 0, 2).reshape(rows, NDEV * w)


def _cols_to_blocks(full):
    rows, total = full.shape
    w = total // NDEV
    return full.reshape(rows, NDEV, w).transpose(1, 0, 2).reshape(NDEV, rows * w // D, D)


def _mix_pad(w):
    rows = w.shape[0]
    xp, q, k, v, z, b, a, gp, gd = jnp.split(w, (512, 1536, 2560, 3584, 4608, 4616, 4624, 5648), axis=1)
    pad = jnp.zeros((rows, MIXP - OFF_BA - 16), w.dtype)
    return jnp.concatenate([q, k, v, z, gp, gd, xp, b, a, pad], axis=1)


def _mix_unpad(w):
    q, k, v, z, gp, gd, xp, b, a = (w[:, OFF_Q:OFF_K], w[:, OFF_K:OFF_V], w[:, OFF_V:OFF_Z], w[:, OFF_Z:OFF_GP],
                                    w[:, OFF_GP:OFF_GD], w[:, OFF_GD:OFF_XP], w[:, OFF_XP:OFF_BA],
                                    w[:, OFF_BA:OFF_BA + 8], w[:, OFF_BA + 8:OFF_BA + 16])
    return jnp.concatenate([xp, q, k, v, z, b, a, gp, gd], axis=1)


def _lane_row(vec8):
    return jnp.zeros((1, 128), F32).at[0, NH:2 * NH].set(vec8)


def _ffn_fwd(x, g, shift, scale, gate, w_in, w_out, tag):
    h = _norm_mod_fwd(x, g, shift, scale, f"{tag}_norm")
    u = _matmul(h, w_in, out_dtype=F32, name=f"{tag}_up")
    a = _swiglu_fwd(u, f"{tag}_act")
    y = _matmul(a, w_out, out_dtype=F32, name=f"{tag}_down")
    return _resid_fwd(x, y, gate, 0.5, f"{tag}_res"), (h, u, a, y)


def _ffn_bwd(dx_out, x, g, scale, gate, w_in, w_out, saved, tag):
    h, u, a, y = saved
    dy, dgate = _resid_bwd(dx_out, y, gate, 0.5, f"{tag}_res_bwd")
    da = _matmul(dy, w_out, tb=True, out_dtype=F32, name=f"{tag}_down_dx")
    dw_out = _matmul(a, dy, ta=True, out_dtype=BF16, name=f"{tag}_down_dw")
    du = _swiglu_bwd(u, da, f"{tag}_act_bwd")
    dh = _matmul(du, w_in, tb=True, out_dtype=F32, name=f"{tag}_up_dx")
    dw_in = _matmul(h, du, ta=True, out_dtype=BF16, name=f"{tag}_up_dw")
    dx, dshift, dscale, dg = _norm_mod_bwd(x, g, scale, dh, dx_out, f"{tag}_norm_bwd")
    return dx, (dshift, dscale, dgate), dg, dw_in, dw_out


def kernel(x, c, ada_w, ada_b, norm_g, ffn1_w_in, ffn1_w_out, ffn2_w_in, ffn2_w_out, mix_w_in, conv_w, a_log, dt_bias, dn_norm_g, pool_w, pool_scale, pool_proj, dn_proj, mix_w_out, final_g, loss_target, m_ada_w, m_ada_b, m_norm_g, m_ffn1_w_in, m_ffn1_w_out, m_ffn2_w_in, m_ffn2_w_out, m_mix_w_in, m_conv_w, m_a_log, m_dt_bias, m_dn_norm_g, m_pool_w, m_pool_scale, m_pool_proj, m_dn_proj, m_mix_w_out, m_final_g, v_ada_w, v_ada_b, v_norm_g, v_ffn1_w_in, v_ffn1_w_out, v_ffn2_w_in, v_ffn2_w_out, v_mix_w_in, v_conv_w, v_a_log, v_dt_bias, v_dn_norm_g, v_pool_w, v_pool_scale, v_pool_proj, v_dn_proj, v_mix_w_out, v_final_g):
    me = 4 * lax.axis_index("x") + 2 * lax.axis_index("y") + lax.axis_index("c")
    x0 = x[0]
    target = loss_target[0]
    t = x0.shape[0]

    big = [ffn1_w_in[0], ffn1_w_out[0], ffn2_w_in[0], ffn2_w_out[0], mix_w_in[0], pool_proj[0], dn_proj[0],
           mix_w_out[0]]
    sizes = [w.size // D for w in big]
    offs = [0]
    for s in sizes:
        offs.append(offs[-1] + s)
    blob_pad = -offs[-1] % TR
    blob = jnp.concatenate([w.astype(BF16).reshape(-1, D) for w in big] + [jnp.zeros((blob_pad, D), BF16)], axis=0)
    gathered = _all_gather(blob, "gather_weights")
    seg = [gathered[:, offs[i]:offs[i + 1], :] for i in range(len(big))]
    w_in1 = _cols_from_blocks(seg[0], D)
    w_out1 = seg[1].reshape(FH, D)
    w_in2 = _cols_from_blocks(seg[2], D)
    w_out2 = seg[3].reshape(FH, D)
    w_mix = _mix_pad(_cols_from_blocks(seg[4], D))
    w_pp = _cols_from_blocks(seg[5], PW)
    w_dn = seg[6].reshape(D, D)
    w_mo = seg[7].reshape(D, D)

    small = jnp.concatenate([c.reshape(8, 128), conv_w[0].reshape(12, 128), norm_g[0].reshape(3, 128),
                             jnp.zeros((1, 128), F32)], axis=0)
    small_all = _all_gather(small, "gather_small")
    c_all = small_all[:, 0:8, :].reshape(NDEV, D)
    conv_full = small_all[:, 8:20, :].reshape(NDEV, 4, 384).transpose(1, 0, 2).reshape(4, 3 * D)
    norm_full = small_all[:, 20:23, :].reshape(NDEV, 3, 128).transpose(1, 0, 2).reshape(3, D)

    ncol = ada_w.shape[2]
    ada_b_mine = lax.dynamic_slice(ada_b, (0, me * ncol), (1, ncol))
    mod_cols = _ada_fwd(c_all, ada_w[0], ada_b_mine, "ada_fwd")
    mod_all = _all_gather(mod_cols, "gather_mod")
    mod = lax.dynamic_index_in_dim(mod_all, me, axis=1, keepdims=False).reshape(9, D)
    shift = [mod[3 * s:3 * s + 1] for s in range(3)]
    scale = [mod[3 * s + 1:3 * s + 2] for s in range(3)]
    gate = [mod[3 * s + 2:3 * s + 3] for s in range(3)]
    ng = [norm_full[s:s + 1] for s in range(3)]
    fg = final_g.reshape(1, D)
    al_row = _lane_row(a_log[0])
    dt_row = _lane_row(dt_bias[0])
    gn = dn_norm_g
    pw = pool_w[0]
    ps = pool_scale

    x1, saved1 = _ffn_fwd(x0, ng[0], shift[0], scale[0], gate[0], w_in1, w_out1, "ffn1")

    h1 = _norm_mod_fwd(x1, ng[1], shift[1], scale[1], "mix_norm")
    proj = _matmul(h1, w_mix, out_dtype=F32, name="mix_in")
    ya = _pool_fwd(proj, pw, ps, w_pp, "pool_fwd")
    qh, kh, vh, bg = _dn_pre_fwd(proj, conv_full, al_row, dt_row, "dn_pre")
    u, w, qk, qd, kd, eg, inv = _dn_local_fwd(qh, kh, vh, bg, "dn_local")
    o, s_saved = _dn_scan_fwd(u, w, qk, qd, kd, eg, "dn_scan")
    ob = _dn_post_fwd(o, proj, gn, "dn_post")
    yb = _matmul(ob, w_dn, out_dtype=F32, name="dn_out")
    merged = _merge_fwd(ya, yb, proj, "merge")
    mix_y = _matmul(merged, w_mo, out_dtype=F32, name="mix_out")
    x2 = _resid_fwd(x1, mix_y, gate[1], 1.0, "mix_res")

    x3, saved2 = _ffn_fwd(x2, ng[2], shift[2], scale[2], gate[2], w_in2, w_out2, "ffn2")
    loss_row, dx3, dfg = _final_loss(x3, fg, target, "loss")

    dx2, dmod2, dng2, dw_in2, dw_out2 = _ffn_bwd(dx3, x2, ng[2], scale[2], gate[2], w_in2, w_out2, saved2, "ffn2")

    dmy, dgate1 = _resid_bwd(dx2, mix_y, gate[1], 1.0, "mix_res_bwd")
    dmerged = _matmul(dmy, w_mo, tb=True, out_dtype=F32, name="mix_out_dx")
    dw_mo = _matmul(merged, dmy, ta=True, out_dtype=BF16, name="mix_out_dw")
    dya, dyb, dgp, dgd = _merge_bwd(dmerged, ya, yb, proj, "merge_bwd")
    dob = _matmul(dyb, w_dn, tb=True, out_dtype=F32, name="dn_out_dx")
    dw_dn = _matmul(ob, dyb, ta=True, out_dtype=BF16, name="dn_out_dw")
    do, dz, dgn = _dn_post_bwd(o, proj, gn, dob, "dn_post_bwd")
    du, dw, dqk, dqd, dkd, deg = _dn_scan_bwd(u, w, qk, qd, kd, eg, s_saved, do, "dn_scan_bwd")
    dqh, dkh, dvh, dbg = _dn_local_bwd(qh, kh, vh, bg, inv, du, dw, dqk, dqd, dkd, deg, "dn_local_bwd")
    dconv, draw, dal, ddt = _dn_pre_bwd_act(proj, conv_full, al_row, dt_row, dqh, dkh, dvh, dbg, "dn_pre_bwd_act")
    dqkv, dcw = _dn_pre_bwd_conv(proj, conv_full, dconv, "dn_pre_bwd_conv")
    dwin, dpl, dpw, dps, dpp = _pool_bwd_local(proj, pw, ps, w_pp, dya, "pool_bwd_local")
    dxp = _pool_bwd_window(dwin, dpl, "pool_bwd_window")
    dproj = jnp.concatenate([dqkv, dz, dgp, dgd, dxp, draw, jnp.zeros((t, MIXP - OFF_BA - 128), BF16)], axis=1)
    dh1 = _matmul(dproj, w_mix, tb=True, out_dtype=F32, name="mix_in_dx")
    dw_mix = _matmul(h1, dproj, ta=True, out_dtype=BF16, name="mix_in_dw")
    dx1, dsh1, dsc1, dng1 = _norm_mod_bwd(x1, ng[1], scale[1], dh1, dx2, "mix_norm_bwd")

    dx0, dmod0, dng0, dw_in1, dw_out1 = _ffn_bwd(dx1, x0, ng[0], scale[0], gate[0], w_in1, w_out1, saved1, "ffn1")

    parts = jnp.concatenate([
        _cols_to_blocks(dw_in1), dw_out1.reshape(NDEV, -1, D), _cols_to_blocks(dw_in2),
        dw_out2.reshape(NDEV, -1, D), _cols_to_blocks(_mix_unpad(dw_mix)), _cols_to_blocks(dpp.astype(BF16)),
        dw_dn.reshape(NDEV, -1, D), dw_mo.reshape(NDEV, -1, D), jnp.zeros((NDEV, blob_pad, D), BF16)], axis=1)
    received = _exchange_blocks(parts, "scatter_grads")
    gsum = _sum_devices(received, F32, "sum_grads")
    gshard = [gsum[offs[i]:offs[i + 1]].reshape(big[i].shape) for i in range(len(big))]

    dmod = jnp.concatenate([*dmod0, dsh1, dsc1, dgate1, *dmod2], axis=1).reshape(-1)
    flat = jnp.concatenate([
        dmod, dal[0, NH:2 * NH], ddt[0, NH:2 * NH], dgn.reshape(-1), dps.reshape(-1), dfg.reshape(-1),
        dpw.reshape(-1), jnp.concatenate([dng0, dng1, dng2], axis=0).reshape(-1), dcw.reshape(-1)])
    nflat = 90 * D
    flat = jnp.concatenate([flat, jnp.zeros((nflat - flat.shape[0],), F32)]).reshape(90, D)
    flat_all = _all_gather(flat, "gather_small_grads")
    tot = _sum_devices(flat_all, F32, "sum_small_grads").reshape(-1)
    dmod_all = flat_all.reshape(NDEV, nflat)[:, :9 * D]
    dmod_cols = lax.dynamic_slice(dmod_all, (0, me * ncol), (NDEV, ncol))
    g_ada_w = _ada_bwd(c_all.T, dmod_cols, "ada_bwd")

    p = 0
    pieces = {}
    for nm, size in (("ada_b", 9 * D), ("a_log", NH), ("dt_bias", NH), ("dn_norm_g", HD), ("pool_scale", PW),
                     ("final_g", D), ("pool_w", 4 * PG * PG), ("norm_g", 3 * D), ("conv_w", 12 * D)):
        pieces[nm] = tot[p:p + size]
        p += size
    g_norm = lax.dynamic_slice(pieces["norm_g"].reshape(3, D), (0, me * 128), (3, 128))
    g_conv = lax.dynamic_slice(pieces["conv_w"].reshape(4, 3 * D), (0, me * 384), (4, 384))

    grads = {
        "ada_w": g_ada_w.reshape(ada_w.shape), "ada_b": pieces["ada_b"].reshape(ada_b.shape),
        "norm_g": g_norm.reshape(norm_g.shape),
        "ffn1_w_in": gshard[0].reshape(ffn1_w_in.shape), "ffn1_w_out": gshard[1].reshape(ffn1_w_out.shape),
        "ffn2_w_in": gshard[2].reshape(ffn2_w_in.shape), "ffn2_w_out": gshard[3].reshape(ffn2_w_out.shape),
        "mix_w_in": gshard[4].reshape(mix_w_in.shape), "conv_w": g_conv.reshape(conv_w.shape),
        "a_log": pieces["a_log"].reshape(a_log.shape), "dt_bias": pieces["dt_bias"].reshape(dt_bias.shape),
        "dn_norm_g": pieces["dn_norm_g"].reshape(dn_norm_g.shape), "pool_w": pieces["pool_w"].reshape(pool_w.shape),
        "pool_scale": pieces["pool_scale"].reshape(pool_scale.shape),
        "pool_proj": gshard[5].reshape(pool_proj.shape), "dn_proj": gshard[6].reshape(dn_proj.shape),
        "mix_w_out": gshard[7].reshape(mix_w_out.shape), "final_g": pieces["final_g"].reshape(final_g.shape),
    }
    weights = {"ada_w": ada_w, "ada_b": ada_b, "norm_g": norm_g, "ffn1_w_in": ffn1_w_in, "ffn1_w_out": ffn1_w_out,
               "ffn2_w_in": ffn2_w_in, "ffn2_w_out": ffn2_w_out, "mix_w_in": mix_w_in, "conv_w": conv_w,
               "a_log": a_log, "dt_bias": dt_bias, "dn_norm_g": dn_norm_g, "pool_w": pool_w,
               "pool_scale": pool_scale, "pool_proj": pool_proj, "dn_proj": dn_proj, "mix_w_out": mix_w_out,
               "final_g": final_g}
    m_in = {"ada_w": m_ada_w, "ada_b": m_ada_b, "norm_g": m_norm_g, "ffn1_w_in": m_ffn1_w_in,
            "ffn1_w_out": m_ffn1_w_out, "ffn2_w_in": m_ffn2_w_in, "ffn2_w_out": m_ffn2_w_out,
            "mix_w_in": m_mix_w_in, "conv_w": m_conv_w, "a_log": m_a_log, "dt_bias": m_dt_bias,
            "dn_norm_g": m_dn_norm_g, "pool_w": m_pool_w, "pool_scale": m_pool_scale, "pool_proj": m_pool_proj,
            "dn_proj": m_dn_proj, "mix_w_out": m_mix_w_out, "final_g": m_final_g}
    v_in = {"ada_w": v_ada_w, "ada_b": v_ada_b, "norm_g": v_norm_g, "ffn1_w_in": v_ffn1_w_in,
            "ffn1_w_out": v_ffn1_w_out, "ffn2_w_in": v_ffn2_w_in, "ffn2_w_out": v_ffn2_w_out,
            "mix_w_in": v_mix_w_in, "conv_w": v_conv_w, "a_log": v_a_log, "dt_bias": v_dt_bias,
            "dn_norm_g": v_dn_norm_g, "pool_w": v_pool_w, "pool_scale": v_pool_scale, "pool_proj": v_pool_proj,
            "dn_proj": v_dn_proj, "mix_w_out": v_mix_w_out, "final_g": v_final_g}

    names = list(weights)
    large = ("ada_w", "ffn1_w_in", "ffn1_w_out", "ffn2_w_in", "ffn2_w_out", "mix_w_in", "pool_proj", "dn_proj",
             "mix_w_out")
    delta, new_m, new_v = {}, {}, {}
    for nm in large:
        shp = weights[nm].shape
        two_d = (shp[-2], shp[-1])
        d_, m_, v_ = _adamw(weights[nm].reshape(two_d), grads[nm].reshape(two_d), m_in[nm].reshape(two_d),
                            v_in[nm].reshape(two_d), f"adamw_{nm}")
        delta[nm], new_m[nm], new_v[nm] = d_.reshape(shp), m_.reshape(shp), v_.reshape(shp)
    rest = [nm for nm in names if nm not in large]
    total = sum(weights[nm].size for nm in rest)
    padded = -(-total // D) * D

    def pack(tree, fill):
        flat_ = jnp.concatenate([tree[nm].reshape(-1) for nm in rest])
        return jnp.concatenate([flat_, jnp.full((padded - total,), fill, F32)]).reshape(-1, D)

    d_, m_, v_ = _adamw(pack(weights, 0.0), pack(grads, 0.0), pack(m_in, 0.0), pack(v_in, 1.0), "adamw_small")
    p = 0
    for nm in rest:
        size = weights[nm].size
        shp = weights[nm].shape
        delta[nm] = d_.reshape(-1)[p:p + size].reshape(shp)
        new_m[nm] = m_.reshape(-1)[p:p + size].reshape(shp)
        new_v[nm] = v_.reshape(-1)[p:p + size].reshape(shp)
        p += size

    loss = lax.psum(loss_row[0, 0], ("x", "y", "c"))
    grad_x = dx0.reshape(x.shape)
    return (loss, grad_x, *[grads[nm] for nm in names], *[delta[nm] for nm in names],
            *[new_m[nm] for nm in names], *[new_v[nm] for nm in names])
```

```python
import functools

import jax
import jax.numpy as jnp
from jax import lax
from jax.experimental import pallas as pl
from jax.experimental.pallas import tpu as pltpu

F32 = jnp.float32
BF16 = jnp.bfloat16
SDS = jax.ShapeDtypeStruct
HI = lax.Precision.HIGHEST

D = 1024
FH = 2816
FB = 704
NH = 8
HD = 128
CH = 64
NDEV = 8
PW = 512
PG = 128
RMS_EPS = 1e-6
L2_EPS = 1e-6
TR = 256
HALO = 16
VMEM_LIMIT = 56 * 1024 * 1024

MIXP = 6912
OFF_Q, OFF_K, OFF_V, OFF_Z, OFF_GP, OFF_GD, OFF_XP, OFF_BA = 0, 1024, 2048, 3072, 4096, 5120, 6144, 6656
MIX_RAW = 6672

ADAM_LR = 0.001
ADAM_B1 = 0.9
ADAM_B2 = 0.999
ADAM_EPS = 1e-08
ADAM_WD = 0.01
ADAM_STEP = 10

NN = (((1,), (0,)), ((), ()))
NT = (((1,), (1,)), ((), ()))
TN = (((0,), (0,)), ((), ()))


def _dg(a, b, dims, prec=None):
    return lax.dot_general(a, b, dims, precision=prec, preferred_element_type=F32)


def _make_dots(prec):
    @jax.custom_vjp
    def nn(a, b):
        return _dg(a, b, NN, prec)

    @jax.custom_vjp
    def nt(a, b):
        return _dg(a, b, NT, prec)

    @jax.custom_vjp
    def tn(a, b):
        return _dg(a, b, TN, prec)

    nn.defvjp(lambda a, b: (nn(a, b), (a, b)), lambda r, d: (nt(d, r[1]), tn(r[0], d)))
    nt.defvjp(lambda a, b: (nt(a, b), (a, b)), lambda r, d: (nn(d, r[1]), tn(d, r[0])))
    tn.defvjp(lambda a, b: (tn(a, b), (a, b)), lambda r, d: (nt(r[1], d), nn(r[0], d)))
    return nn, nt, tn


_nn, _nt, _tn = _make_dots(None)


def _params(sem):
    return pltpu.CompilerParams(dimension_semantics=sem, vmem_limit_bytes=VMEM_LIMIT)


def _sigmoid(x):
    return 1.0 / (1.0 + jnp.exp(-x))


def _silu(x):
    return x * _sigmoid(x)


def _dsilu(x):
    s = _sigmoid(x)
    return s * (1.0 + x * (1.0 - s))


def _pick(n, cands):
    for c in cands:
        if n % c == 0:
            return c
    raise ValueError(f"no tile for {n}")


def _iota(shape, dim):
    return lax.broadcasted_iota(jnp.int32, shape, dim)


def _matmul(a, b, *, ta=False, tb=False, a_blk=False, b_blk=False, o_blk=False, tm=None, tn=None, tk=None,
            out_dtype, name):
    if a_blk:
        nb, r, cb = a.shape
        if ta:
            k_dim, m_dim, tm = r, nb * cb, cb
        else:
            m_dim, k_dim, tk = r, nb * cb, cb
    else:
        k_dim, m_dim = a.shape if ta else a.shape[::-1]
    if b_blk:
        nb, r, cb = b.shape
        if tb:
            n_dim, tk = r, cb
            assert nb * cb == k_dim
        else:
            n_dim, tn = nb * cb, cb
            assert r == k_dim
    else:
        n_dim = b.shape[0] if tb else b.shape[1]
    tm = tm or _pick(m_dim, (512, 256, 128))
    tn = tn or _pick(n_dim, (512, 768, 256, 128))
    tk = tk or (k_dim if (k_dim <= 2816 and not ta) else _pick(k_dim, (2816, 2304, 1024, 512, 256)))
    nk = k_dim // tk
    dims = ((((0,) if ta else (1,)), ((1,) if tb else (0,))), ((), ()))

    def body(a_ref, b_ref, o_ref, acc_ref):
        k = pl.program_id(2)

        @pl.when(k == 0)
        def _():
            acc_ref[...] = jnp.zeros_like(acc_ref)

        acc_ref[...] += lax.dot_general(a_ref[...].astype(BF16), b_ref[...].astype(BF16), dims,
                                        preferred_element_type=F32)

        @pl.when(k == nk - 1)
        def _():
            o_ref[...] = acc_ref[...].astype(o_ref.dtype)

    if a_blk:
        a_spec = (pl.BlockSpec((None, tk, tm), lambda i, j, k: (i, k, 0)) if ta
                  else pl.BlockSpec((None, tm, tk), lambda i, j, k: (k, i, 0)))
    else:
        a_spec = (pl.BlockSpec((tk, tm), lambda i, j, k: (k, i)) if ta
                  else pl.BlockSpec((tm, tk), lambda i, j, k: (i, k)))
    if b_blk:
        b_spec = (pl.BlockSpec((None, tn, tk), lambda i, j, k: (k, j, 0)) if tb
                  else pl.BlockSpec((None, tk, tn), lambda i, j, k: (j, k, 0)))
    else:
        b_spec = (pl.BlockSpec((tn, tk), lambda i, j, k: (j, k)) if tb
                  else pl.BlockSpec((tk, tn), lambda i, j, k: (k, j)))
    if o_blk:
        o_spec = pl.BlockSpec((None, tm, tn), lambda i, j, k: (j, i, 0))
        o_shape = SDS((n_dim // tn, m_dim, tn), out_dtype)
    else:
        o_spec = pl.BlockSpec((tm, tn), lambda i, j, k: (i, j))
        o_shape = SDS((m_dim, n_dim), out_dtype)
    return pl.pallas_call(
        body, grid=(m_dim // tm, n_dim // tn, nk),
        in_specs=[a_spec, b_spec],
        out_specs=o_spec,
        out_shape=o_shape,
        scratch_shapes=[pltpu.VMEM((tm, tn), F32)],
        compiler_params=_params(("parallel", "parallel", "arbitrary")),
        name=name,
    )(a, b)


def _row(width, col=0):
    return pl.BlockSpec((TR, width), lambda i: (i, col))


def _vec(width):
    return pl.BlockSpec((1, width), lambda i: (0, 0))


def _norm_mod_fwd(x, g, shift, scale, name):
    t = x.shape[0]

    def body(x_ref, g_ref, sh_ref, sc_ref, o_ref):
        xv = x_ref[...]
        r = lax.rsqrt(jnp.mean(xv * xv, axis=-1, keepdims=True) + RMS_EPS)
        o_ref[...] = (((xv * r) * g_ref[...]) * (1.0 + sc_ref[...]) + sh_ref[...]).astype(o_ref.dtype)

    return pl.pallas_call(
        body, grid=(t // TR,), in_specs=[_row(D), _vec(D), _vec(D), _vec(D)], out_specs=_row(D),
        out_shape=SDS((t, D), BF16), compiler_params=_params(("parallel",)), name=name,
    )(x, g, shift, scale)


def _norm_mod_bwd(x, g, scale, dh, dx_in, name):
    t = x.shape[0]

    def body(x_ref, g_ref, sc_ref, dh_ref, dxi_ref, dx_ref, dsh_ref, dsc_ref, dg_ref):
        @pl.when(pl.program_id(0) == 0)
        def _():
            dsh_ref[...] = jnp.zeros_like(dsh_ref)
            dsc_ref[...] = jnp.zeros_like(dsc_ref)
            dg_ref[...] = jnp.zeros_like(dg_ref)

        xv = x_ref[...]
        gv = g_ref[...]
        dh = dh_ref[...]
        r = lax.rsqrt(jnp.mean(xv * xv, axis=-1, keepdims=True) + RMS_EPS)
        n = xv * r
        dsh_ref[...] += jnp.sum(dh, axis=0, keepdims=True)
        dsc_ref[...] += jnp.sum(dh * (n * gv), axis=0, keepdims=True)
        tt = dh * (1.0 + sc_ref[...])
        dg_ref[...] += jnp.sum(tt * n, axis=0, keepdims=True)
        dn = tt * gv
        dx_ref[...] = dxi_ref[...] + r * (dn - n * jnp.mean(dn * n, axis=-1, keepdims=True))

    return pl.pallas_call(
        body, grid=(t // TR,), in_specs=[_row(D), _vec(D), _vec(D), _row(D), _row(D)],
        out_specs=[_row(D), _vec(D), _vec(D), _vec(D)],
        out_shape=[SDS((t, D), F32), SDS((1, D), F32), SDS((1, D), F32), SDS((1, D), F32)],
        compiler_params=_params(("arbitrary",)), name=name,
    )(x, g, scale, dh, dx_in)


def _swiglu_specs():
    pair = pl.BlockSpec((2, None, TR, FB), lambda i, j: (0, j, i, 0))
    one = pl.BlockSpec((None, TR, FB), lambda i, j: (j, i, 0))
    return pair, one


def _swiglu_fwd(u, name):
    t = u.shape[2]

    def body(u_ref, o_ref):
        o_ref[...] = (_silu(u_ref[0]) * u_ref[1]).astype(o_ref.dtype)

    pair, one = _swiglu_specs()
    return pl.pallas_call(
        body, grid=(t // TR, NDEV // 2), in_specs=[pair], out_specs=one,
        out_shape=SDS((NDEV // 2, t, FB), BF16), compiler_params=_params(("parallel", "parallel")), name=name,
    )(u)


def _swiglu_bwd(u, da, name):
    t = u.shape[2]

    def body(u_ref, da_ref, o_ref):
        gv = u_ref[0]
        dav = da_ref[...]
        o_ref[0] = (dav * u_ref[1] * _dsilu(gv)).astype(o_ref.dtype)
        o_ref[1] = (dav * _silu(gv)).astype(o_ref.dtype)

    pair, one = _swiglu_specs()
    return pl.pallas_call(
        body, grid=(t // TR, NDEV // 2), in_specs=[pair, one], out_specs=pair,
        out_shape=SDS((2, NDEV // 2, t, FB), BF16), compiler_params=_params(("parallel", "parallel")), name=name,
    )(u, da)


def _resid_fwd(x, y, gate, coef, name):
    t = x.shape[0]

    def body(x_ref, y_ref, g_ref, o_ref):
        o_ref[...] = x_ref[...] + (coef * g_ref[...]) * y_ref[...]

    return pl.pallas_call(
        body, grid=(t // TR,), in_specs=[_row(D), _row(D), _vec(D)], out_specs=_row(D),
        out_shape=SDS((t, D), F32), compiler_params=_params(("parallel",)), name=name,
    )(x, y, gate)


def _resid_bwd(dx, y, gate, coef, name):
    t = dx.shape[0]

    def body(dx_ref, y_ref, g_ref, dy_ref, dg_ref):
        @pl.when(pl.program_id(0) == 0)
        def _():
            dg_ref[...] = jnp.zeros_like(dg_ref)

        dxv = dx_ref[...]
        dy_ref[...] = ((coef * g_ref[...]) * dxv).astype(dy_ref.dtype)
        dg_ref[...] += jnp.sum((coef * dxv) * y_ref[...], axis=0, keepdims=True)

    return pl.pallas_call(
        body, grid=(t // TR,), in_specs=[_row(D), _row(D), _vec(D)], out_specs=[_row(D), _vec(D)],
        out_shape=[SDS((t, D), BF16), SDS((1, D), F32)],
        compiler_params=_params(("arbitrary",)), name=name,
    )(dx, y, gate)


def _final_loss(x, fg, target, name):
    t = x.shape[0]
    nt = t // TR

    def body(x_ref, g_ref, t_ref, loss_ref, dx_ref, dg_ref, acc_ref):
        i = pl.program_id(0)

        @pl.when(i == 0)
        def _():
            acc_ref[...] = jnp.zeros_like(acc_ref)
            dg_ref[...] = jnp.zeros_like(dg_ref)

        xv = x_ref[...]
        gv = g_ref[...]
        r = lax.rsqrt(jnp.mean(xv * xv, axis=-1, keepdims=True) + RMS_EPS)
        n = xv * r
        err = n * gv - t_ref[...]
        acc_ref[...] += jnp.sum(err * err, axis=0, keepdims=True)
        dy = err * (1.0 / D)
        dg_ref[...] += jnp.sum(dy * n, axis=0, keepdims=True)
        dn = dy * gv
        dx_ref[...] = r * (dn - n * jnp.mean(dn * n, axis=-1, keepdims=True))

        @pl.when(i == nt - 1)
        def _():
            tot = jnp.sum(acc_ref[...], axis=1, keepdims=True) * (0.5 / D)
            loss_ref[...] = jnp.broadcast_to(tot, loss_ref.shape)

    return pl.pallas_call(
        body, grid=(nt,), in_specs=[_row(D), _vec(D), _row(D)],
        out_specs=[_vec(128), _row(D), _vec(D)],
        out_shape=[SDS((1, 128), F32), SDS((t, D), F32), SDS((1, D), F32)],
        scratch_shapes=[pltpu.VMEM((1, D), F32)],
        compiler_params=_params(("arbitrary",)), name=name,
    )(x, fg, target)


def _halo_prev(width, col):
    per = TR // HALO
    return pl.BlockSpec((HALO, width), lambda i: (jnp.maximum(i * per - 1, 0), col))


def _halo_next(width, col, nt):
    per = TR // HALO
    return pl.BlockSpec((HALO, width), lambda i: (jnp.minimum((i + 1) * per, nt * per - 1), col))


def _pool_windows(ext, tile_index):
    rows = _iota((TR, PG), 0) + tile_index * TR + 1
    pooled, counts = [], []
    for gi in range(4):
        w = 2 << gi
        e = ext[:, gi * PG:(gi + 1) * PG]
        s = e
        step = 1
        while step < w:
            s = s + pltpu.roll(s, step, 0)
            step *= 2
        cnt = jnp.minimum(rows, w).astype(F32)
        pooled.append(s[HALO:] / cnt - e[HALO:])
        counts.append(cnt)
    return pooled, counts


def _pool_fwd(proj, pool_w, pool_scale, pool_proj, name):
    t = proj.shape[0]
    xcol = OFF_XP // PW

    def body(x_ref, h_ref, pw_ref, ps_ref, pp_ref, o_ref):
        i = pl.program_id(0)
        halo = jnp.where(i > 0, h_ref[...], 0.0)
        ext = jnp.concatenate([halo, x_ref[...]], axis=0)
        pooled, _ = _pool_windows(ext, i)
        mixed = [_dg(pooled[g].astype(BF16), pw_ref[g].astype(BF16), NN) for g in range(4)]
        ypre = jnp.concatenate(mixed, axis=1) * ps_ref[...]
        o_ref[...] = _dg(ypre.astype(BF16), pp_ref[...], NN)

    return pl.pallas_call(
        body, grid=(t // TR,),
        in_specs=[_row(PW, xcol), _halo_prev(PW, xcol),
                  pl.BlockSpec((4, PG, PG), lambda i: (0, 0, 0)), _vec(PW),
                  pl.BlockSpec((PW, D), lambda i: (0, 0))],
        out_specs=_row(D), out_shape=SDS((t, D), F32),
        compiler_params=_params(("parallel",)), name=name,
    )(proj, proj, pool_w, pool_scale, pool_proj)


def _pool_bwd_local(proj, pool_w, pool_scale, pool_proj, dya, name):
    t = proj.shape[0]
    xcol = OFF_XP // PW

    def body(x_ref, h_ref, pw_ref, ps_ref, pp_ref, dya_ref, dwin_ref, dpl_ref, dpw_ref, dps_ref, dpp_ref):
        i = pl.program_id(0)

        @pl.when(i == 0)
        def _():
            dpw_ref[...] = jnp.zeros_like(dpw_ref)
            dps_ref[...] = jnp.zeros_like(dps_ref)
            dpp_ref[...] = jnp.zeros_like(dpp_ref)

        halo = jnp.where(i > 0, h_ref[...], 0.0)
        ext = jnp.concatenate([halo, x_ref[...]], axis=0)
        pooled, counts = _pool_windows(ext, i)
        mixed = jnp.concatenate(
            [_dg(pooled[g].astype(BF16), pw_ref[g].astype(BF16), NN) for g in range(4)], axis=1)
        ps = ps_ref[...]
        ypre = mixed * ps
        dyab = dya_ref[...].astype(BF16)
        dypre = _dg(dyab, pp_ref[...], NT)
        dpp_ref[...] += _dg(ypre.astype(BF16), dyab, TN)
        dps_ref[...] += jnp.sum(dypre * mixed, axis=0, keepdims=True)
        dmixed = dypre * ps
        for g in range(4):
            dm = dmixed[:, g * PG:(g + 1) * PG].astype(BF16)
            dpw_ref[g] += _dg(pooled[g].astype(BF16), dm, TN)
            dpooled = _dg(dm, pw_ref[g].astype(BF16), NT)
            dwin_ref[:, g * PG:(g + 1) * PG] = dpooled / counts[g]
            dpl_ref[:, g * PG:(g + 1) * PG] = dpooled

    return pl.pallas_call(
        body, grid=(t // TR,),
        in_specs=[_row(PW, xcol), _halo_prev(PW, xcol),
                  pl.BlockSpec((4, PG, PG), lambda i: (0, 0, 0)), _vec(PW),
                  pl.BlockSpec((PW, D), lambda i: (0, 0)), _row(D)],
        out_specs=[_row(PW), _row(PW), pl.BlockSpec((4, PG, PG), lambda i: (0, 0, 0)), _vec(PW),
                   pl.BlockSpec((PW, D), lambda i: (0, 0))],
        out_shape=[SDS((t, PW), F32), SDS((t, PW), F32), SDS((4, PG, PG), F32), SDS((1, PW), F32),
                   SDS((PW, D), F32)],
        compiler_params=_params(("arbitrary",)), name=name,
    )(proj, proj, pool_w, pool_scale, pool_proj, dya)


def _pool_bwd_window(dwin, dpl, name):
    t = dwin.shape[0]
    nt = t // TR
    ext_rows = TR + HALO

    def body(dw_ref, h_ref, dp_ref, o_ref):
        i = pl.program_id(0)
        halo = jnp.where(i < nt - 1, h_ref[...], 0.0)
        ext = jnp.concatenate([dw_ref[...], halo], axis=0)
        for gi in range(4):
            w = 2 << gi
            s = ext[:, gi * PG:(gi + 1) * PG]
            step = 1
            while step < w:
                s = s + pltpu.roll(s, ext_rows - step, 0)
                step *= 2
            o_ref[:, gi * PG:(gi + 1) * PG] = (s[:TR] - dp_ref[:, gi * PG:(gi + 1) * PG]).astype(o_ref.dtype)

    return pl.pallas_call(
        body, grid=(nt,), in_specs=[_row(PW), _halo_next(PW, 0, nt), _row(PW)], out_specs=_row(PW),
        out_shape=SDS((t, PW), BF16), compiler_params=_params(("parallel",)), name=name,
    )(dwin, dwin, dpl)


def _conv_group(ext, cw_ref, cols):
    acc = cw_ref[3:4, cols] * ext
    for j in range(3):
        acc = acc + cw_ref[j:j + 1, cols] * pltpu.roll(ext, 3 - j, 0)
    return acc[HALO:]


def _gate_terms(raw, al, dt):
    beta = _sigmoid(raw)
    xg = raw + dt
    sp = jnp.maximum(xg, 0.0) + jnp.log(1.0 + jnp.exp(-jnp.abs(xg)))
    g = -jnp.exp(al) * sp
    return beta, g, _sigmoid(xg)


def _dn_pre_fwd(proj, conv_w, al_row, dt_row, name):
    t = proj.shape[0]

    def body(x_ref, h_ref, cw_ref, ba_ref, al_ref, dt_ref, q_ref, k_ref, v_ref, bg_ref):
        i = pl.program_id(0)
        keep = i > 0
        for grp in range(24):
            cols = slice(grp * HD, (grp + 1) * HD)
            ext = jnp.concatenate([jnp.where(keep, h_ref[:, cols], 0.0), x_ref[:, cols]], axis=0)
            s = _silu(_conv_group(ext, cw_ref, cols))
            seg, head = divmod(grp, NH)
            hc = slice(head * HD, (head + 1) * HD)
            if seg == 0:
                q_ref[:, hc] = s * lax.rsqrt(jnp.sum(s * s, axis=-1, keepdims=True) + L2_EPS) * (HD ** -0.5)
            elif seg == 1:
                k_ref[:, hc] = s * lax.rsqrt(jnp.sum(s * s, axis=-1, keepdims=True) + L2_EPS)
            else:
                v_ref[:, hc] = s
        lane = _iota((TR, 128), 1)
        rowc = _iota((TR, 128), 0) % CH
        beta, g, _ = _gate_terms(ba_ref[...], al_ref[...], dt_ref[...])
        step = 1
        while step < CH:
            g = g + jnp.where(rowc >= step, pltpu.roll(g, step, 0), 0.0)
            step *= 2
        bg_ref[...] = jnp.where(lane < NH, beta, jnp.where(lane < 2 * NH, g, 0.0))

    return pl.pallas_call(
        body, grid=(t // TR,),
        in_specs=[_row(3 * D, 0), _halo_prev(3 * D, 0), pl.BlockSpec((4, 3 * D), lambda i: (0, 0)),
                  _row(128, OFF_BA // 128), _vec(128), _vec(128)],
        out_specs=[_row(D), _row(D), _row(D), _row(128)],
        out_shape=[SDS((t, D), F32), SDS((t, D), F32), SDS((t, D), F32), SDS((t, 128), F32)],
        compiler_params=_params(("parallel",)), name=name,
    )(proj, proj, conv_w, proj, al_row, dt_row)


def _dn_pre_bwd_act(proj, conv_w, al_row, dt_row, dq, dk, dv, dbg, name):
    t = proj.shape[0]

    def body(x_ref, h_ref, cw_ref, ba_ref, al_ref, dt_ref, dq_ref, dk_ref, dv_ref, dbg_ref,
             dc_ref, draw_ref, dal_ref, ddt_ref):
        i = pl.program_id(0)

        @pl.when(i == 0)
        def _():
            dal_ref[...] = jnp.zeros_like(dal_ref)
            ddt_ref[...] = jnp.zeros_like(ddt_ref)

        keep = i > 0
        for grp in range(24):
            cols = slice(grp * HD, (grp + 1) * HD)
            ext = jnp.concatenate([jnp.where(keep, h_ref[:, cols], 0.0), x_ref[:, cols]], axis=0)
            cv = _conv_group(ext, cw_ref, cols)
            seg, head = divmod(grp, NH)
            hc = slice(head * HD, (head + 1) * HD)
            if seg == 2:
                ds = dv_ref[:, hc]
            else:
                s = _silu(cv)
                r = lax.rsqrt(jnp.sum(s * s, axis=-1, keepdims=True) + L2_EPS)
                dy = dq_ref[:, hc] if seg == 0 else dk_ref[:, hc]
                c = (HD ** -0.5) if seg == 0 else 1.0
                ds = (c * r) * (dy - s * ((r * r) * jnp.sum(dy * s, axis=-1, keepdims=True)))
            dc_ref[:, cols] = ds * _dsilu(cv)
        lane = _iota((TR, 128), 1)
        rowc = _iota((TR, 128), 0) % CH
        isb = lane < NH
        isg = jnp.logical_and(lane >= NH, lane < 2 * NH)
        beta, g, sg = _gate_terms(ba_ref[...], al_ref[...], dt_ref[...])
        dbgv = dbg_ref[...]
        dg = dbgv
        step = 1
        while step < CH:
            dg = dg + jnp.where(rowc < CH - step, pltpu.roll(dg, TR - step, 0), 0.0)
            step *= 2
        da_raw = dg * (-jnp.exp(al_ref[...])) * sg
        draw_ref[...] = jnp.where(isb, dbgv * beta * (1.0 - beta), jnp.where(isg, da_raw, 0.0)).astype(draw_ref.dtype)
        dal_ref[...] += jnp.sum(jnp.where(isg, dg * g, 0.0), axis=0, keepdims=True)
        ddt_ref[...] += jnp.sum(jnp.where(isg, da_raw, 0.0), axis=0, keepdims=True)

    return pl.pallas_call(
        body, grid=(t // TR,),
        in_specs=[_row(3 * D, 0), _halo_prev(3 * D, 0), pl.BlockSpec((4, 3 * D), lambda i: (0, 0)),
                  _row(128, OFF_BA // 128), _vec(128), _vec(128), _row(D), _row(D), _row(D), _row(128)],
        out_specs=[_row(3 * D), _row(128), _vec(128), _vec(128)],
        out_shape=[SDS((t, 3 * D), F32), SDS((t, 128), BF16), SDS((1, 128), F32), SDS((1, 128), F32)],
        compiler_params=_params(("arbitrary",)), name=name,
    )(proj, proj, conv_w, proj, al_row, dt_row, dq, dk, dv, dbg)


def _dn_pre_bwd_conv(proj, conv_w, dconv, name):
    t = proj.shape[0]
    nt = t // TR
    ext_rows = TR + HALO

    def body(x_ref, h_ref, cw_ref, dc_ref, dn_ref, dx_ref, dcw_ref):
        i = pl.program_id(0)

        @pl.when(i == 0)
        def _():
            dcw_ref[...] = jnp.zeros_like(dcw_ref)

        keep_prev = i > 0
        keep_next = i < nt - 1
        for grp in range(24):
            cols = slice(grp * HD, (grp + 1) * HD)
            dct = dc_ref[:, cols]
            dext = jnp.concatenate([dct, jnp.where(keep_next, dn_ref[:, cols], 0.0)], axis=0)
            acc = cw_ref[3:4, cols] * dext
            for j in range(3):
                acc = acc + cw_ref[j:j + 1, cols] * pltpu.roll(dext, ext_rows - (3 - j), 0)
            dx_ref[:, cols] = acc[:TR].astype(dx_ref.dtype)
            xext = jnp.concatenate([jnp.where(keep_prev, h_ref[:, cols], 0.0), x_ref[:, cols]], axis=0)
            for j in range(4):
                xs = xext if j == 3 else pltpu.roll(xext, 3 - j, 0)
                dcw_ref[j:j + 1, cols] += jnp.sum(xs[HALO:] * dct, axis=0, keepdims=True)

    return pl.pallas_call(
        body, grid=(nt,),
        in_specs=[_row(3 * D, 0), _halo_prev(3 * D, 0), pl.BlockSpec((4, 3 * D), lambda i: (0, 0)),
                  _row(3 * D), _halo_next(3 * D, 0, nt)],
        out_specs=[_row(3 * D), pl.BlockSpec((4, 3 * D), lambda i: (0, 0))],
        out_shape=[SDS((t, 3 * D), BF16), SDS((4, 3 * D), F32)],
        compiler_params=_params(("arbitrary",)), name=name,
    )(proj, proj, conv_w, dconv, dconv)


def _dn_post_fwd(o, proj, gn, name):
    t = o.shape[0]

    def body(o_ref, z_ref, g_ref, out_ref):
        gv = g_ref[...]
        for h in range(NH):
            hc = slice(h * HD, (h + 1) * HD)
            ov = o_ref[:, hc]
            r = lax.rsqrt(jnp.mean(ov * ov, axis=-1, keepdims=True) + RMS_EPS)
            out_ref[:, hc] = (((ov * r) * gv) * _silu(z_ref[:, hc])).astype(out_ref.dtype)

    return pl.pallas_call(
        body, grid=(t // TR,), in_specs=[_row(D), _row(D, OFF_Z // D), _vec(HD)], out_specs=_row(D),
        out_shape=SDS((t, D), BF16), compiler_params=_params(("parallel",)), name=name,
    )(o, proj, gn)


def _dn_post_bwd(o, proj, gn, dob, name):
    t = o.shape[0]

    def body(o_ref, z_ref, g_ref, d_ref, do_ref, dz_ref, dg_ref):
        @pl.when(pl.program_id(0) == 0)
        def _():
            dg_ref[...] = jnp.zeros_like(dg_ref)

        gv = g_ref[...]
        acc = jnp.zeros((1, HD), F32)
        for h in range(NH):
            hc = slice(h * HD, (h + 1) * HD)
            ov = o_ref[:, hc]
            zv = z_ref[:, hc]
            dv = d_ref[:, hc]
            r = lax.rsqrt(jnp.mean(ov * ov, axis=-1, keepdims=True) + RMS_EPS)
            n = ov * r
            dz_ref[:, hc] = (dv * (n * gv) * _dsilu(zv)).astype(dz_ref.dtype)
            dng = dv * _silu(zv)
            acc = acc + jnp.sum(dng * n, axis=0, keepdims=True)
            dn = dng * gv
            do_ref[:, hc] = r * (dn - n * jnp.mean(dn * n, axis=-1, keepdims=True))
        dg_ref[...] += acc

    return pl.pallas_call(
        body, grid=(t // TR,), in_specs=[_row(D), _row(D, OFF_Z // D), _vec(HD), _row(D)],
        out_specs=[_row(D), _row(D), _vec(HD)],
        out_shape=[SDS((t, D), F32), SDS((t, D), BF16), SDS((1, HD), F32)],
        compiler_params=_params(("arbitrary",)), name=name,
    )(o, proj, gn, dob)


def _merge_fwd(ya, yb, proj, name):
    t = ya.shape[0]

    def body(a_ref, b_ref, gp_ref, gd_ref, o_ref):
        o_ref[...] = (_sigmoid(gp_ref[...]) * a_ref[...] + _sigmoid(gd_ref[...]) * b_ref[...]).astype(o_ref.dtype)

    return pl.pallas_call(
        body, grid=(t // TR,), in_specs=[_row(D), _row(D), _row(D, OFF_GP // D), _row(D, OFF_GD // D)],
        out_specs=_row(D), out_shape=SDS((t, D), BF16),
        compiler_params=_params(("parallel",)), name=name,
    )(ya, yb, proj, proj)


def _merge_bwd(dm, ya, yb, proj, name):
    t = ya.shape[0]

    def body(d_ref, a_ref, b_ref, gp_ref, gd_ref, da_ref, db_ref, dgp_ref, dgd_ref):
        dv = d_ref[...]
        sp = _sigmoid(gp_ref[...])
        sd = _sigmoid(gd_ref[...])
        da_ref[...] = dv * sp
        db_ref[...] = (dv * sd).astype(db_ref.dtype)
        dgp_ref[...] = (dv * a_ref[...] * sp * (1.0 - sp)).astype(dgp_ref.dtype)
        dgd_ref[...] = (dv * b_ref[...] * sd * (1.0 - sd)).astype(dgd_ref.dtype)

    return pl.pallas_call(
        body, grid=(t // TR,),
        in_specs=[_row(D), _row(D), _row(D), _row(D, OFF_GP // D), _row(D, OFF_GD // D)],
        out_specs=[_row(D)] * 4,
        out_shape=[SDS((t, D), F32), SDS((t, D), BF16), SDS((t, D), BF16), SDS((t, D), BF16)],
        compiler_params=_params(("parallel",)), name=name,
    )(dm, ya, yb, proj, proj)


def _split2(x):
    hi = x.astype(BF16)
    return hi, (x - hi.astype(F32)).astype(BF16)


def _dot3(a, b, dims):
    ah, al = _split2(a)
    bh, bl = _split2(b)
    return _dg(ah, bh, dims) + (_dg(ah, bl, dims) + _dg(al, bh, dims))


def _neumann_inverses(mats):
    ri = _iota((CH, CH), 0)
    ci = _iota((CH, CH), 1)
    eye = jnp.where(ri == ci, 1.0, 0.0).astype(F32)
    xs = [-a for a in mats]
    ps = [eye + x for x in xs]
    for _ in range(5):
        xs = [_dot3(x, x, NN) for x in xs]
        ps = [p + _dot3(p, x, NN) for p, x in zip(ps, xs)]
    return ps


def _solve_with(inv):
    @jax.custom_vjp
    def solve(a, rhs):
        return _dot3(inv, rhs, NN)

    def fwd(a, rhs):
        sol = _dot3(inv, rhs, NN)
        return sol, sol

    def bwd(sol, d):
        drhs = _dot3(inv, d, TN)
        return -_dot3(drhs, sol, NT), drhs

    solve.defvjp(fwd, bwd)
    return solve


@jax.custom_vjp
def _rows_to_lanes(g64):
    ri = _iota((CH, CH), 0)
    ci = _iota((CH, CH), 1)
    diag = jnp.where(ri == ci, g64, 0.0)
    ones = jnp.ones((CH, CH), BF16)
    hi = diag.astype(BF16)
    rem = diag - hi.astype(F32)
    mid = rem.astype(BF16)
    lo = (rem - mid.astype(F32)).astype(BF16)
    return _dg(ones, hi, NN) + (_dg(ones, mid, NN) + _dg(ones, lo, NN))


def _rows_to_lanes_bwd(_, d):
    ri = _iota((CH, CH), 0)
    ci = _iota((CH, CH), 1)
    return (jnp.where(ri == ci, jnp.broadcast_to(jnp.sum(d, axis=0, keepdims=True), (CH, CH)), 0.0),)


_rows_to_lanes.defvjp(lambda g64: (_rows_to_lanes(g64), None), _rows_to_lanes_bwd)


def _chunk_local(solve_all, q, k, v, g128, g64, gl128, b128, b64):
    ri = _iota((CH, CH), 0)
    ci = _iota((CH, CH), 1)
    causal = ri >= ci
    strict = ri > ci
    gj = [_rows_to_lanes(g) for g in g64]
    decay = [jnp.where(causal, jnp.exp(jnp.where(causal, g - t, 0.0)), 0.0) for g, t in zip(g64, gj)]
    kk = [_nt(x, x) for x in k]
    a = [jnp.where(strict, b * m * dc, 0.0) for b, m, dc in zip(b64, kk, decay)]
    eg = [jnp.exp(g) for g in g128]
    rhs = [jnp.concatenate([b * x, (b * e) * y], axis=1) for b, x, e, y in zip(b128, v, eg, k)]
    sol = solve_all(a, rhs)
    qk = [jnp.where(causal, _nt(x, y) * dc, 0.0) for x, y, dc in zip(q, k, decay)]
    return ([s[:, :HD] for s in sol], [s[:, HD:] for s in sol], qk, [x * e for x, e in zip(q, eg)],
            [x * jnp.exp(gl - g) for x, gl, g in zip(k, gl128, g128)], [jnp.exp(gl) for gl in gl128])


def _all_head_gates(bgv):
    return tuple(list(z) for z in zip(*[_head_gates(bgv, h) for h in range(NH)]))


def _head_gates(bgv, h):
    lane = _iota((CH, 128), 1)
    row = _iota((CH, 128), 0)
    bcol = jnp.sum(jnp.where(lane == h, bgv, 0.0), axis=1, keepdims=True)
    gcol = jnp.sum(jnp.where(lane == NH + h, bgv, 0.0), axis=1, keepdims=True)
    g128 = jnp.broadcast_to(gcol, (CH, 128))
    gl128 = jnp.broadcast_to(jnp.sum(jnp.where(row == CH - 1, g128, 0.0), axis=0, keepdims=True), (CH, 128))
    return (g128, jnp.broadcast_to(gcol, (CH, CH)), gl128,
            jnp.broadcast_to(bcol, (CH, 128)), jnp.broadcast_to(bcol, (CH, CH)))


def _chunk_specs():
    row = pl.BlockSpec((CH, D), lambda i: (i, 0))
    small = pl.BlockSpec((CH, 128), lambda i: (i, 0))
    qk = pl.BlockSpec((NH, CH, CH), lambda i: (i, 0, 0))
    eg = pl.BlockSpec((1, NH, 128), lambda i: (i, 0, 0))
    return row, small, qk, eg


def _dn_local_fwd(q, k, v, bg, name):
    t = q.shape[0]
    n = t // CH

    def body(q_ref, k_ref, v_ref, bg_ref, u_ref, w_ref, qk_ref, qd_ref, kd_ref, eg_ref, inv_ref):
        cols = [slice(h * HD, (h + 1) * HD) for h in range(NH)]

        def solve_all(mats, rhs):
            invs = _neumann_inverses(mats)
            for h in range(NH):
                inv_ref[h] = invs[h]
            return [_dot3(m, r, NN) for m, r in zip(invs, rhs)]

        u, w, qk, qd, kd, egl = _chunk_local(
            solve_all, [q_ref[:, c] for c in cols], [k_ref[:, c] for c in cols], [v_ref[:, c] for c in cols],
            *_all_head_gates(bg_ref[...]))
        for h, hc in enumerate(cols):
            u_ref[:, hc] = u[h]
            w_ref[:, hc] = w[h].astype(w_ref.dtype)
            qd_ref[:, hc] = qd[h].astype(qd_ref.dtype)
            kd_ref[:, hc] = kd[h].astype(kd_ref.dtype)
            qk_ref[h] = qk[h].astype(qk_ref.dtype)
            eg_ref[0, h:h + 1, :] = egl[h][0:1, :]

    row, small, qkb, egb = _chunk_specs()
    return pl.pallas_call(
        body, grid=(n,), in_specs=[row, row, row, small], out_specs=[row, row, qkb, row, row, egb, qkb],
        out_shape=[SDS((t, D), F32), SDS((t, D), BF16), SDS((n * NH, CH, CH), BF16), SDS((t, D), BF16),
                   SDS((t, D), BF16), SDS((n, NH, 128), F32), SDS((n * NH, CH, CH), F32)],
        compiler_params=_params(("parallel",)), name=name,
    )(q, k, v, bg)


def _dn_local_bwd(q, k, v, bg, inv, du, dw, dqk, dqd, dkd, deg, name):
    t = q.shape[0]
    n = t // CH

    def body(q_ref, k_ref, v_ref, bg_ref, inv_ref, du_ref, dw_ref, dqk_ref, dqd_ref, dkd_ref, deg_ref,
             dq_ref, dk_ref, dv_ref, dbg_ref):
        bgv = bg_ref[...]
        lane = _iota((CH, 128), 1)
        row = _iota((CH, 128), 0)
        first = jnp.where(row == 0, 1.0, 0.0)
        acc = jnp.zeros((CH, 128), F32)
        cols = [slice(h * HD, (h + 1) * HD) for h in range(NH)]
        solves = [_solve_with(inv_ref[h]) for h in range(NH)]

        def solve_all(mats, rhs):
            return [f(m, r) for f, m, r in zip(solves, mats, rhs)]

        _, vjp = jax.vjp(functools.partial(_chunk_local, solve_all),
                         [q_ref[:, c] for c in cols], [k_ref[:, c] for c in cols], [v_ref[:, c] for c in cols],
                         *_all_head_gates(bgv))
        cts = ([du_ref[:, c] for c in cols], [dw_ref[:, c] for c in cols], [dqk_ref[h] for h in range(NH)],
               [dqd_ref[:, c] for c in cols], [dkd_ref[:, c] for c in cols],
               [jnp.broadcast_to(deg_ref[0, h:h + 1, :], (CH, 128)) * first for h in range(NH)])
        dq, dk, dv, dg128, dg64, dgl, db128, db64 = vjp(cts)
        for h, hc in enumerate(cols):
            dq_ref[:, hc] = dq[h]
            dk_ref[:, hc] = dk[h]
            dv_ref[:, hc] = dv[h]
            dg = jnp.sum(dg128[h], axis=1, keepdims=True) + jnp.sum(dg64[h], axis=1, keepdims=True)
            tot = jnp.sum(jnp.sum(dgl[h], axis=0, keepdims=True), axis=1, keepdims=True)
            dg = dg + jnp.where(row[:, 0:1] == CH - 1, tot, 0.0)
            db = jnp.sum(db128[h], axis=1, keepdims=True) + jnp.sum(db64[h], axis=1, keepdims=True)
            acc = acc + jnp.where(lane == h, db, 0.0) + jnp.where(lane == NH + h, dg, 0.0)
        dbg_ref[...] = acc

    row, small, qkb, egb = _chunk_specs()
    return pl.pallas_call(
        body, grid=(n,), in_specs=[row, row, row, small, qkb, row, row, qkb, row, row, egb],
        out_specs=[row, row, row, small],
        out_shape=[SDS((t, D), F32)] * 3 + [SDS((t, 128), F32)],
        compiler_params=_params(("parallel",)), name=name,
    )(q, k, v, bg, inv, du, dw, dqk, dqd, dkd, deg)


def _state_step(s, u, w, qk, qd, kd, egl):
    ws = [_nn(a, b) for a, b in zip(w, s)]
    v_new = [a - b for a, b in zip(u, ws)]
    qs = [_nn(a, b) for a, b in zip(qd, s)]
    intra = [_nn(a, b) for a, b in zip(qk, v_new)]
    upd = [_tn(a, b) for a, b in zip(kd, v_new)]
    return [a * e + b for a, e, b in zip(s, egl, upd)], [a + b for a, b in zip(qs, intra)]


def _dn_scan_fwd(u, w, qk, qd, kd, eg, name):
    t = u.shape[0]
    n = t // CH

    def body(u_ref, w_ref, qk_ref, qd_ref, kd_ref, eg_ref, o_ref, save_ref, s_ref):
        @pl.when(pl.program_id(0) == 0)
        def _():
            s_ref[...] = jnp.zeros_like(s_ref)

        cols = [slice(h * HD, (h + 1) * HD) for h in range(NH)]
        s = [s_ref[h] for h in range(NH)]
        for h in range(NH):
            save_ref[0, h] = s[h]
        s_new, o = _state_step(
            s, [u_ref[:, c] for c in cols], [w_ref[:, c].astype(F32) for c in cols],
            [qk_ref[h].astype(F32) for h in range(NH)], [qd_ref[:, c].astype(F32) for c in cols],
            [kd_ref[:, c].astype(F32) for c in cols], [eg_ref[0, h:h + 1, :] for h in range(NH)])
        for h, hc in enumerate(cols):
            o_ref[:, hc] = o[h]
            s_ref[h] = s_new[h]

    row, _, qkb, egb = _chunk_specs()
    return pl.pallas_call(
        body, grid=(n,), in_specs=[row, row, qkb, row, row, egb],
        out_specs=[row, pl.BlockSpec((1, NH, HD, HD), lambda i: (i, 0, 0, 0))],
        out_shape=[SDS((t, D), F32), SDS((n, NH, HD, HD), F32)],
        scratch_shapes=[pltpu.VMEM((NH, HD, HD), F32)],
        compiler_params=_params(("arbitrary",)), name=name,
    )(u, w, qk, qd, kd, eg)


def _dn_scan_bwd(u, w, qk, qd, kd, eg, saved, do, name):
    t = u.shape[0]
    n = t // CH

    def body(u_ref, w_ref, qk_ref, qd_ref, kd_ref, eg_ref, sv_ref, do_ref,
             du_ref, dw_ref, dqk_ref, dqd_ref, dkd_ref, deg_ref, ds_ref):
        @pl.when(pl.program_id(0) == 0)
        def _():
            ds_ref[...] = jnp.zeros_like(ds_ref)

        cols = [slice(h * HD, (h + 1) * HD) for h in range(NH)]
        _, vjp = jax.vjp(
            _state_step, [sv_ref[0, h] for h in range(NH)], [u_ref[:, c] for c in cols],
            [w_ref[:, c].astype(F32) for c in cols], [qk_ref[h].astype(F32) for h in range(NH)],
            [qd_ref[:, c].astype(F32) for c in cols], [kd_ref[:, c].astype(F32) for c in cols],
            [eg_ref[0, h:h + 1, :] for h in range(NH)])
        ds, du, dw, dqk, dqd, dkd, deg = vjp(([ds_ref[h] for h in range(NH)], [do_ref[:, c] for c in cols]))
        for h, hc in enumerate(cols):
            ds_ref[h] = ds[h]
            du_ref[:, hc] = du[h]
            dw_ref[:, hc] = dw[h]
            dqk_ref[h] = dqk[h]
            dqd_ref[:, hc] = dqd[h]
            dkd_ref[:, hc] = dkd[h]
            deg_ref[0, h:h + 1, :] = deg[h]

    rev = lambda i: (n - 1 - i, 0)
    rev3 = lambda i: (n - 1 - i, 0, 0)
    row = pl.BlockSpec((CH, D), rev)
    qkb = pl.BlockSpec((NH, CH, CH), rev3)
    egb = pl.BlockSpec((1, NH, 128), rev3)
    return pl.pallas_call(
        body, grid=(n,),
        in_specs=[row, row, qkb, row, row, egb,
                  pl.BlockSpec((1, NH, HD, HD), lambda i: (n - 1 - i, 0, 0, 0)), row],
        out_specs=[row, row, qkb, row, row, egb],
        out_shape=[SDS((t, D), F32), SDS((t, D), F32), SDS((n * NH, CH, CH), F32), SDS((t, D), F32),
                   SDS((t, D), F32), SDS((n, NH, 128), F32)],
        scratch_shapes=[pltpu.VMEM((NH, HD, HD), F32)],
        compiler_params=_params(("arbitrary",)), name=name,
    )(u, w, qk, qd, kd, eg, saved, do)


def _ada_fwd(c_all, ada_w, ada_b, name):
    ncol = ada_w.shape[1]

    def body(c_ref, w_ref, b_ref, o_ref):
        o_ref[...] = _dg(_silu(c_ref[...]), w_ref[...], NN, HI) + b_ref[...]

    return pl.pallas_call(body, out_shape=SDS((NDEV, ncol), F32),
                          compiler_params=pltpu.CompilerParams(vmem_limit_bytes=VMEM_LIMIT), name=name,
                          )(c_all, ada_w, ada_b)


def _ada_bwd(c_all_t, dmod, name):
    ncol = dmod.shape[1]

    def body(c_ref, d_ref, o_ref):
        sc = _silu(c_ref[...])
        acc = sc[:, 0:1] * d_ref[0:1, :]
        for b in range(1, NDEV):
            acc = acc + sc[:, b:b + 1] * d_ref[b:b + 1, :]
        o_ref[...] = acc

    return pl.pallas_call(body, out_shape=SDS((D, ncol), F32),
                          compiler_params=pltpu.CompilerParams(vmem_limit_bytes=VMEM_LIMIT), name=name,
                          )(c_all_t, dmod)


def _sum_devices(parts, out_dtype, name):
    _, r, c = parts.shape
    tr = TR if r % TR == 0 else r

    def body(p_ref, o_ref):
        acc = p_ref[0].astype(F32)
        for i in range(1, NDEV):
            acc = acc + p_ref[i].astype(F32)
        o_ref[...] = acc.astype(o_ref.dtype)

    return pl.pallas_call(
        body, grid=(r // tr,), in_specs=[pl.BlockSpec((NDEV, tr, c), lambda i: (0, i, 0))],
        out_specs=pl.BlockSpec((tr, c), lambda i: (i, 0)), out_shape=SDS((r, c), out_dtype),
        compiler_params=_params(("parallel",)), name=name,
    )(parts)


def _adamw(w, g, m, v, name):
    r, c = w.shape
    tr = _pick(r, (256, 128, 88, 8)) if r % 8 == 0 else r
    bc1 = 1.0 - ADAM_B1 ** ADAM_STEP
    bc2 = 1.0 - ADAM_B2 ** ADAM_STEP

    def body(w_ref, g_ref, m_ref, v_ref, d_ref, nm_ref, nv_ref):
        gv = g_ref[...]
        m_new = ADAM_B1 * m_ref[...] + (1.0 - ADAM_B1) * gv
        v_new = ADAM_B2 * v_ref[...] + (1.0 - ADAM_B2) * (gv * gv)
        nm_ref[...] = m_new
        nv_ref[...] = v_new
        d_ref[...] = -ADAM_LR * ((m_new / bc1) / (jnp.sqrt(v_new / bc2) + ADAM_EPS) + ADAM_WD * w_ref[...])

    spec = pl.BlockSpec((tr, c), lambda i: (i, 0))
    return pl.pallas_call(
        body, grid=(r // tr,), in_specs=[spec] * 4, out_specs=[spec] * 3,
        out_shape=[SDS((r, c), F32)] * 3, compiler_params=_params(("parallel",)), name=name,
    )(w, g, m, v)


ANY = pl.BlockSpec(memory_space=pl.ANY)
MESH = pl.DeviceIdType.MESH


def _all_gather(xs, name):
    n = len(xs)

    def body(*refs):
        x_refs, out_refs = refs[:n], refs[n:2 * n]
        send_sems, recv_sems, local_sems = refs[2 * n:]
        mx, my, mc = lax.axis_index("x"), lax.axis_index("y"), lax.axis_index("c")
        me, sibling = (mx, my, mc), (mx, my, 1 - mc)
        chips = [(1 - mx, my), (mx, 1 - my), (1 - mx, 1 - my)]

        def rows(a, px, py, pc):
            return out_refs[a].at[4 * px + 2 * py + pc]

        def copy(a, k, block, to, src=None):
            return pltpu.make_async_remote_copy(
                src_ref=rows(a, *block) if src is None else src, dst_ref=rows(a, *block),
                send_sem=send_sems.at[a, k], recv_sem=recv_sems.at[a, k], device_id=to, device_id_type=MESH)

        mine = [pltpu.make_async_copy(x_refs[a], rows(a, *me), local_sems.at[a]) for a in range(n)]
        for cp in mine:
            cp.start()
        first = []
        for a in range(n):
            first.append(copy(a, 0, me, sibling, src=x_refs[a]))
            first += [copy(a, 1 + j, me, (*chip, mc), src=x_refs[a]) for j, chip in enumerate(chips)]
        for cp in first:
            cp.start()
        passed = []
        for a in range(n):
            for j, chip in enumerate(chips):
                copy(a, 1 + j, (*chip, mc), me).wait_recv()
                passed.append(copy(a, 4 + j, (*chip, mc), sibling))
                passed[-1].start()
        for a in range(n):
            copy(a, 0, sibling, me).wait_recv()
            for j, chip in enumerate(chips):
                copy(a, 4 + j, (*chip, 1 - mc), me).wait_recv()
        for cp in first + passed:
            cp.wait_send()
        for cp in mine:
            cp.wait()

    return pl.pallas_call(
        body, out_shape=[SDS((NDEV,) + x.shape, x.dtype) for x in xs], in_specs=[ANY] * n, out_specs=[ANY] * n,
        scratch_shapes=[pltpu.SemaphoreType.DMA((n, 7)), pltpu.SemaphoreType.DMA((n, 7)),
                        pltpu.SemaphoreType.DMA((n,))],
        name=name,
    )(*xs)


def _exchange_blocks(parts, name):
    n = len(parts)

    def body(*refs):
        p_refs, out_refs = refs[:n], refs[n:2 * n]
        send_sems, recv_sems, local_sems = refs[2 * n:]
        mx, my, mc = lax.axis_index("x"), lax.axis_index("y"), lax.axis_index("c")
        me = 4 * mx + 2 * my + mc
        mine = [pltpu.make_async_copy(p_refs[a].at[me], out_refs[a].at[me], local_sems.at[a]) for a in range(n)]
        for cp in mine:
            cp.start()
        copies = []
        for a in range(n):
            for k in range(1, NDEV):
                px = 1 - mx if k & 4 else mx
                py = 1 - my if k & 2 else my
                pc = 1 - mc if k & 1 else mc
                copies.append(pltpu.make_async_remote_copy(
                    src_ref=p_refs[a].at[4 * px + 2 * py + pc], dst_ref=out_refs[a].at[me],
                    send_sem=send_sems.at[a, k - 1], recv_sem=recv_sems.at[a, k - 1],
                    device_id=(px, py, pc), device_id_type=MESH))
        for cp in copies:
            cp.start()
        for cp in copies:
            cp.wait_recv()
        for cp in copies:
            cp.wait_send()
        for cp in mine:
            cp.wait()

    return pl.pallas_call(
        body, out_shape=[SDS(p.shape, p.dtype) for p in parts], in_specs=[ANY] * n, out_specs=[ANY] * n,
        scratch_shapes=[pltpu.SemaphoreType.DMA((n, 7)), pltpu.SemaphoreType.DMA((n, 7)),
                        pltpu.SemaphoreType.DMA((n,))],
        name=name,
    )(*parts)


def _cols_from_blocks(blocks):
    _, rows, w = blocks.shape
    return blocks.transpose(1, 0, 2).reshape(rows, NDEV * w)


def _cols_to_blocks(full):
    rows, total = full.shape
    return full.reshape(rows, NDEV, total // NDEV).transpose(1, 0, 2)


def _mix_pad(w):
    rows = w.shape[0]
    xp, q, k, v, z, b, a, gp, gd = jnp.split(w, (512, 1536, 2560, 3584, 4608, 4616, 4624, 5648), axis=1)
    pad = jnp.zeros((rows, MIXP - OFF_BA - 16), w.dtype)
    return jnp.concatenate([q, k, v, z, gp, gd, xp, b, a, pad], axis=1)


def _mix_unpad(w):
    q, k, v, z, gp, gd, xp, b, a = (w[:, OFF_Q:OFF_K], w[:, OFF_K:OFF_V], w[:, OFF_V:OFF_Z], w[:, OFF_Z:OFF_GP],
                                    w[:, OFF_GP:OFF_GD], w[:, OFF_GD:OFF_XP], w[:, OFF_XP:OFF_BA],
                                    w[:, OFF_BA:OFF_BA + 8], w[:, OFF_BA + 8:OFF_BA + 16])
    return jnp.concatenate([xp, q, k, v, z, b, a, gp, gd], axis=1)


def _lane_row(vec8):
    return jnp.zeros((1, 128), F32).at[0, NH:2 * NH].set(vec8)


def _ffn_fwd(x, g, shift, scale, gate, w_in, w_out, tag):
    t = x.shape[0]
    h = _norm_mod_fwd(x, g, shift, scale, f"{tag}_norm")
    u = _matmul(h, w_in, b_blk=True, o_blk=True, out_dtype=F32, name=f"{tag}_up").reshape(2, NDEV // 2, t, FB)
    a = _swiglu_fwd(u, f"{tag}_act")
    y = _matmul(a, w_out, a_blk=True, out_dtype=F32, name=f"{tag}_down")
    return _resid_fwd(x, y, gate, 0.5, f"{tag}_res"), (h, u, a, y)


def _ffn_bwd(dx_out, x, g, scale, gate, w_in, w_out, saved, tag):
    h, u, a, y = saved
    t = x.shape[0]
    dy, dgate = _resid_bwd(dx_out, y, gate, 0.5, f"{tag}_res_bwd")
    da = _matmul(dy, w_out, tb=True, tn=FB, o_blk=True, out_dtype=F32, name=f"{tag}_down_dx")
    dw_out = _matmul(a, dy, ta=True, a_blk=True, out_dtype=BF16, name=f"{tag}_down_dw")
    du = _swiglu_bwd(u, da, f"{tag}_act_bwd").reshape(NDEV, t, FB)
    dh = _matmul(du, w_in, tb=True, a_blk=True, b_blk=True, out_dtype=F32, name=f"{tag}_up_dx")
    dw_in = _matmul(h, du, ta=True, b_blk=True, o_blk=True, out_dtype=BF16, name=f"{tag}_up_dw")
    dx, dshift, dscale, dg = _norm_mod_bwd(x, g, scale, dh, dx_out, f"{tag}_norm_bwd")
    return dx, (dshift, dscale, dgate), dg, dw_in, dw_out.reshape(NDEV, FH // NDEV, D)


def kernel(x, c, ada_w, ada_b, norm_g, ffn1_w_in, ffn1_w_out, ffn2_w_in, ffn2_w_out, mix_w_in, conv_w, a_log, dt_bias, dn_norm_g, pool_w, pool_scale, pool_proj, dn_proj, mix_w_out, final_g, loss_target, m_ada_w, m_ada_b, m_norm_g, m_ffn1_w_in, m_ffn1_w_out, m_ffn2_w_in, m_ffn2_w_out, m_mix_w_in, m_conv_w, m_a_log, m_dt_bias, m_dn_norm_g, m_pool_w, m_pool_scale, m_pool_proj, m_dn_proj, m_mix_w_out, m_final_g, v_ada_w, v_ada_b, v_norm_g, v_ffn1_w_in, v_ffn1_w_out, v_ffn2_w_in, v_ffn2_w_out, v_mix_w_in, v_conv_w, v_a_log, v_dt_bias, v_dn_norm_g, v_pool_w, v_pool_scale, v_pool_proj, v_dn_proj, v_mix_w_out, v_final_g):
    me = 4 * lax.axis_index("x") + 2 * lax.axis_index("y") + lax.axis_index("c")
    x0 = x[0]
    target = loss_target[0]
    t = x0.shape[0]

    big = [ffn1_w_in[0], ffn1_w_out[0], ffn2_w_in[0], ffn2_w_out[0], mix_w_in[0], pool_proj[0], dn_proj[0],
           mix_w_out[0]]
    seg = _all_gather([w.astype(BF16) for w in big], "gather_weights")
    w_in1 = seg[0]
    w_out1 = seg[1].reshape(FH, D)
    w_in2 = seg[2]
    w_out2 = seg[3].reshape(FH, D)
    w_mix = _mix_pad(_cols_from_blocks(seg[4]))
    w_pp = _cols_from_blocks(seg[5])
    w_dn = seg[6].reshape(D, D)
    w_mo = seg[7].reshape(D, D)

    small = jnp.concatenate([c.reshape(8, 128), conv_w[0].reshape(12, 128), norm_g[0].reshape(3, 128),
                             jnp.zeros((1, 128), F32)], axis=0)
    small_all, = _all_gather([small], "gather_small")
    c_all = small_all[:, 0:8, :].reshape(NDEV, D)
    conv_full = small_all[:, 8:20, :].reshape(NDEV, 4, 384).transpose(1, 0, 2).reshape(4, 3 * D)
    norm_full = small_all[:, 20:23, :].reshape(NDEV, 3, 128).transpose(1, 0, 2).reshape(3, D)

    ncol = ada_w.shape[2]
    ada_b_mine = lax.dynamic_slice(ada_b, (0, me * ncol), (1, ncol))
    mod_cols = _ada_fwd(c_all, ada_w[0], ada_b_mine, "ada_fwd")
    mod_all, = _all_gather([mod_cols], "gather_mod")
    mod = lax.dynamic_index_in_dim(mod_all, me, axis=1, keepdims=False).reshape(9, D)
    shift = [mod[3 * s:3 * s + 1] for s in range(3)]
    scale = [mod[3 * s + 1:3 * s + 2] for s in range(3)]
    gate = [mod[3 * s + 2:3 * s + 3] for s in range(3)]
    ng = [norm_full[s:s + 1] for s in range(3)]
    fg = final_g.reshape(1, D)
    al_row = _lane_row(a_log[0])
    dt_row = _lane_row(dt_bias[0])
    gn = dn_norm_g
    pw = pool_w[0]
    ps = pool_scale

    x1, saved1 = _ffn_fwd(x0, ng[0], shift[0], scale[0], gate[0], w_in1, w_out1, "ffn1")

    h1 = _norm_mod_fwd(x1, ng[1], shift[1], scale[1], "mix_norm")
    proj = _matmul(h1, w_mix, out_dtype=F32, name="mix_in")
    ya = _pool_fwd(proj, pw, ps, w_pp, "pool_fwd")
    qh, kh, vh, bg = _dn_pre_fwd(proj, conv_full, al_row, dt_row, "dn_pre")
    u, w, qk, qd, kd, eg, inv = _dn_local_fwd(qh, kh, vh, bg, "dn_local")
    o, s_saved = _dn_scan_fwd(u, w, qk, qd, kd, eg, "dn_scan")
    ob = _dn_post_fwd(o, proj, gn, "dn_post")
    yb = _matmul(ob, w_dn, out_dtype=F32, name="dn_out")
    merged = _merge_fwd(ya, yb, proj, "merge")
    mix_y = _matmul(merged, w_mo, out_dtype=F32, name="mix_out")
    x2 = _resid_fwd(x1, mix_y, gate[1], 1.0, "mix_res")

    x3, saved2 = _ffn_fwd(x2, ng[2], shift[2], scale[2], gate[2], w_in2, w_out2, "ffn2")
    loss_row, dx3, dfg = _final_loss(x3, fg, target, "loss")

    dx2, dmod2, dng2, dw_in2, dw_out2 = _ffn_bwd(dx3, x2, ng[2], scale[2], gate[2], w_in2, w_out2, saved2, "ffn2")

    dmy, dgate1 = _resid_bwd(dx2, mix_y, gate[1], 1.0, "mix_res_bwd")
    dmerged = _matmul(dmy, w_mo, tb=True, out_dtype=F32, name="mix_out_dx")
    dw_mo = _matmul(merged, dmy, ta=True, out_dtype=BF16, name="mix_out_dw")
    dya, dyb, dgp, dgd = _merge_bwd(dmerged, ya, yb, proj, "merge_bwd")
    dob = _matmul(dyb, w_dn, tb=True, out_dtype=F32, name="dn_out_dx")
    dw_dn = _matmul(ob, dyb, ta=True, out_dtype=BF16, name="dn_out_dw")
    do, dz, dgn = _dn_post_bwd(o, proj, gn, dob, "dn_post_bwd")
    du, dw, dqk, dqd, dkd, deg = _dn_scan_bwd(u, w, qk, qd, kd, eg, s_saved, do, "dn_scan_bwd")
    dqh, dkh, dvh, dbg = _dn_local_bwd(qh, kh, vh, bg, inv, du, dw, dqk, dqd, dkd, deg, "dn_local_bwd")
    dconv, draw, dal, ddt = _dn_pre_bwd_act(proj, conv_full, al_row, dt_row, dqh, dkh, dvh, dbg, "dn_pre_bwd_act")
    dqkv, dcw = _dn_pre_bwd_conv(proj, conv_full, dconv, "dn_pre_bwd_conv")
    dwin, dpl, dpw, dps, dpp = _pool_bwd_local(proj, pw, ps, w_pp, dya, "pool_bwd_local")
    dxp = _pool_bwd_window(dwin, dpl, "pool_bwd_window")
    dproj = jnp.concatenate([dqkv, dz, dgp, dgd, dxp, draw, jnp.zeros((t, MIXP - OFF_BA - 128), BF16)], axis=1)
    dh1 = _matmul(dproj, w_mix, tb=True, out_dtype=F32, name="mix_in_dx")
    dw_mix = _matmul(h1, dproj, ta=True, out_dtype=BF16, name="mix_in_dw")
    dx1, dsh1, dsc1, dng1 = _norm_mod_bwd(x1, ng[1], scale[1], dh1, dx2, "mix_norm_bwd")

    dx0, dmod0, dng0, dw_in1, dw_out1 = _ffn_bwd(dx1, x0, ng[0], scale[0], gate[0], w_in1, w_out1, saved1, "ffn1")

    parts = [dw_in1, dw_out1, dw_in2, dw_out2, _cols_to_blocks(_mix_unpad(dw_mix)),
             _cols_to_blocks(dpp.astype(BF16)), dw_dn.reshape(NDEV, -1, D), dw_mo.reshape(NDEV, -1, D)]
    received = _exchange_blocks(parts, "scatter_grads")
    gshard = [_sum_devices(r, F32, f"sum_grads_{i}") for i, r in enumerate(received)]

    dmod = jnp.concatenate([*dmod0, dsh1, dsc1, dgate1, *dmod2], axis=1).reshape(-1)
    flat = jnp.concatenate([
        dmod, dal[0, NH:2 * NH], ddt[0, NH:2 * NH], dgn.reshape(-1), dps.reshape(-1), dfg.reshape(-1),
        dpw.reshape(-1), jnp.concatenate([dng0, dng1, dng2], axis=0).reshape(-1), dcw.reshape(-1)])
    nflat = 90 * D
    flat = jnp.concatenate([flat, jnp.zeros((nflat - flat.shape[0],), F32)]).reshape(90, D)
    flat_all, = _all_gather([flat], "gather_small_grads")
    tot = _sum_devices(flat_all, F32, "sum_small_grads").reshape(-1)
    dmod_all = flat_all.reshape(NDEV, nflat)[:, :9 * D]
    dmod_cols = lax.dynamic_slice(dmod_all, (0, me * ncol), (NDEV, ncol))
    g_ada_w = _ada_bwd(c_all.T, dmod_cols, "ada_bwd")

    p = 0
    pieces = {}
    for nm, size in (("ada_b", 9 * D), ("a_log", NH), ("dt_bias", NH), ("dn_norm_g", HD), ("pool_scale", PW),
                     ("final_g", D), ("pool_w", 4 * PG * PG), ("norm_g", 3 * D), ("conv_w", 12 * D)):
        pieces[nm] = tot[p:p + size]
        p += size
    g_norm = lax.dynamic_slice(pieces["norm_g"].reshape(3, D), (0, me * 128), (3, 128))
    g_conv = lax.dynamic_slice(pieces["conv_w"].reshape(4, 3 * D), (0, me * 384), (4, 384))

    grads = {
        "ada_w": g_ada_w.reshape(ada_w.shape), "ada_b": pieces["ada_b"].reshape(ada_b.shape),
        "norm_g": g_norm.reshape(norm_g.shape),
        "ffn1_w_in": gshard[0].reshape(ffn1_w_in.shape), "ffn1_w_out": gshard[1].reshape(ffn1_w_out.shape),
        "ffn2_w_in": gshard[2].reshape(ffn2_w_in.shape), "ffn2_w_out": gshard[3].reshape(ffn2_w_out.shape),
        "mix_w_in": gshard[4].reshape(mix_w_in.shape), "conv_w": g_conv.reshape(conv_w.shape),
        "a_log": pieces["a_log"].reshape(a_log.shape), "dt_bias": pieces["dt_bias"].reshape(dt_bias.shape),
        "dn_norm_g": pieces["dn_norm_g"].reshape(dn_norm_g.shape), "pool_w": pieces["pool_w"].reshape(pool_w.shape),
        "pool_scale": pieces["pool_scale"].reshape(pool_scale.shape),
        "pool_proj": gshard[5].reshape(pool_proj.shape), "dn_proj": gshard[6].reshape(dn_proj.shape),
        "mix_w_out": gshard[7].reshape(mix_w_out.shape), "final_g": pieces["final_g"].reshape(final_g.shape),
    }
    weights = {"ada_w": ada_w, "ada_b": ada_b, "norm_g": norm_g, "ffn1_w_in": ffn1_w_in, "ffn1_w_out": ffn1_w_out,
               "ffn2_w_in": ffn2_w_in, "ffn2_w_out": ffn2_w_out, "mix_w_in": mix_w_in, "conv_w": conv_w,
               "a_log": a_log, "dt_bias": dt_bias, "dn_norm_g": dn_norm_g, "pool_w": pool_w,
               "pool_scale": pool_scale, "pool_proj": pool_proj, "dn_proj": dn_proj, "mix_w_out": mix_w_out,
               "final_g": final_g}
    m_in = {"ada_w": m_ada_w, "ada_b": m_ada_b, "norm_g": m_norm_g, "ffn1_w_in": m_ffn1_w_in,
            "ffn1_w_out": m_ffn1_w_out, "ffn2_w_in": m_ffn2_w_in, "ffn2_w_out": m_ffn2_w_out,
            "mix_w_in": m_mix_w_in, "conv_w": m_conv_w, "a_log": m_a_log, "dt_bias": m_dt_bias,
            "dn_norm_g": m_dn_norm_g, "pool_w": m_pool_w, "pool_scale": m_pool_scale, "pool_proj": m_pool_proj,
            "dn_proj": m_dn_proj, "mix_w_out": m_mix_w_out, "final_g": m_final_g}
    v_in = {"ada_w": v_ada_w, "ada_b": v_ada_b, "norm_g": v_norm_g, "ffn1_w_in": v_ffn1_w_in,
            "ffn1_w_out": v_ffn1_w_out, "ffn2_w_in": v_ffn2_w_in, "ffn2_w_out": v_ffn2_w_out,
            "mix_w_in": v_mix_w_in, "conv_w": v_conv_w, "a_log": v_a_log, "dt_bias": v_dt_bias,
            "dn_norm_g": v_dn_norm_g, "pool_w": v_pool_w, "pool_scale": v_pool_scale, "pool_proj": v_pool_proj,
            "dn_proj": v_dn_proj, "mix_w_out": v_mix_w_out, "final_g": v_final_g}

    names = list(weights)
    large = ("ada_w", "ffn1_w_in", "ffn1_w_out", "ffn2_w_in", "ffn2_w_out", "mix_w_in", "pool_proj", "dn_proj",
             "mix_w_out")
    delta, new_m, new_v = {}, {}, {}
    for nm in large:
        shp = weights[nm].shape
        two_d = (shp[-2], shp[-1])
        d_, m_, v_ = _adamw(weights[nm].reshape(two_d), grads[nm].reshape(two_d), m_in[nm].reshape(two_d),
                            v_in[nm].reshape(two_d), f"adamw_{nm}")
        delta[nm], new_m[nm], new_v[nm] = d_.reshape(shp), m_.reshape(shp), v_.reshape(shp)
    rest = [nm for nm in names if nm not in large]
    total = sum(weights[nm].size for nm in rest)
    padded = -(-total // D) * D

    def pack(tree, fill):
        flat_ = jnp.concatenate([tree[nm].reshape(-1) for nm in rest])
        return jnp.concatenate([flat_, jnp.full((padded - total,), fill, F32)]).reshape(-1, D)

    d_, m_, v_ = _adamw(pack(weights, 0.0), pack(grads, 0.0), pack(m_in, 0.0), pack(v_in, 1.0), "adamw_small")
    p = 0
    for nm in rest:
        size = weights[nm].size
        shp = weights[nm].shape
        delta[nm] = d_.reshape(-1)[p:p + size].reshape(shp)
        new_m[nm] = m_.reshape(-1)[p:p + size].reshape(shp)
        new_v[nm] = v_.reshape(-1)[p:p + size].reshape(shp)
        p += size

    loss = lax.psum(loss_row[0, 0], ("x", "y", "c"))
    grad_x = dx0.reshape(x.shape)
    return (loss, grad_x, *[grads[nm] for nm in names], *[delta[nm] for nm in names],
            *[new_m[nm] for nm in names], *[new_v[nm] for nm in names])
```

```python
import functools

import jax
import jax.numpy as jnp
from jax import lax
from jax.experimental import pallas as pl
from jax.experimental.pallas import tpu as pltpu

F32 = jnp.float32
BF16 = jnp.bfloat16
SDS = jax.ShapeDtypeStruct
HI = lax.Precision.HIGHEST

D = 1024
FH = 2816
FB = 704
NH = 8
HD = 128
CH = 64
NDEV = 8
PW = 512
PG = 128
RMS_EPS = 1e-6
L2_EPS = 1e-6
TR = 256
HALO = 16
VMEM_LIMIT = 56 * 1024 * 1024

MIXP = 6912
OFF_Q, OFF_K, OFF_V, OFF_Z, OFF_GP, OFF_GD, OFF_XP, OFF_BA = 0, 1024, 2048, 3072, 4096, 5120, 6144, 6656
MIX_RAW = 6672

ADAM_LR = 0.001
ADAM_B1 = 0.9
ADAM_B2 = 0.999
ADAM_EPS = 1e-08
ADAM_WD = 0.01
ADAM_STEP = 10

NN = (((1,), (0,)), ((), ()))
NT = (((1,), (1,)), ((), ()))
TN = (((0,), (0,)), ((), ()))


def _dg(a, b, dims, prec=None):
    return lax.dot_general(a, b, dims, precision=prec, preferred_element_type=F32)


def _make_dots(prec):
    @jax.custom_vjp
    def nn(a, b):
        return _dg(a, b, NN, prec)

    @jax.custom_vjp
    def nt(a, b):
        return _dg(a, b, NT, prec)

    @jax.custom_vjp
    def tn(a, b):
        return _dg(a, b, TN, prec)

    nn.defvjp(lambda a, b: (nn(a, b), (a, b)), lambda r, d: (nt(d, r[1]), tn(r[0], d)))
    nt.defvjp(lambda a, b: (nt(a, b), (a, b)), lambda r, d: (nn(d, r[1]), tn(d, r[0])))
    tn.defvjp(lambda a, b: (tn(a, b), (a, b)), lambda r, d: (nt(r[1], d), nn(r[0], d)))
    return nn, nt, tn


_nn, _nt, _tn = _make_dots(None)


def _params(sem):
    return pltpu.CompilerParams(dimension_semantics=sem, vmem_limit_bytes=VMEM_LIMIT)


def _sigmoid(x):
    return 1.0 / (1.0 + jnp.exp(-x))


def _silu(x):
    return x * _sigmoid(x)


def _dsilu(x):
    s = _sigmoid(x)
    return s * (1.0 + x * (1.0 - s))


def _pick(n, cands):
    for c in cands:
        if n % c == 0:
            return c
    raise ValueError(f"no tile for {n}")


def _iota(shape, dim):
    return lax.broadcasted_iota(jnp.int32, shape, dim)


def _matmul(a, b, *, ta=False, tb=False, a_blk=False, b_blk=False, o_blk=False, tm=None, tn=None, tk=None,
            out_dtype, name):
    if a_blk:
        nb, r, cb = a.shape
        if ta:
            k_dim, m_dim, tm = r, nb * cb, cb
        else:
            m_dim, k_dim, tk = r, nb * cb, cb
    else:
        k_dim, m_dim = a.shape if ta else a.shape[::-1]
    if b_blk:
        nb, r, cb = b.shape
        if tb:
            n_dim, tk = r, cb
            assert nb * cb == k_dim
        else:
            n_dim, tn = nb * cb, cb
            assert r == k_dim
    else:
        n_dim = b.shape[0] if tb else b.shape[1]
    tm = tm or _pick(m_dim, (512, 256, 128))
    tn = tn or _pick(n_dim, (512, 768, 256, 128))
    tk = tk or (k_dim if (k_dim <= 2816 and not ta) else _pick(k_dim, (2816, 2304, 1024, 512, 256)))
    nk = k_dim // tk
    dims = ((((0,) if ta else (1,)), ((1,) if tb else (0,))), ((), ()))

    def body(a_ref, b_ref, o_ref, acc_ref):
        k = pl.program_id(2)

        @pl.when(k == 0)
        def _():
            acc_ref[...] = jnp.zeros_like(acc_ref)

        acc_ref[...] += lax.dot_general(a_ref[...].astype(BF16), b_ref[...].astype(BF16), dims,
                                        preferred_element_type=F32)

        @pl.when(k == nk - 1)
        def _():
            o_ref[...] = acc_ref[...].astype(o_ref.dtype)

    if a_blk:
        a_spec = (pl.BlockSpec((None, tk, tm), lambda i, j, k: (i, k, 0)) if ta
                  else pl.BlockSpec((None, tm, tk), lambda i, j, k: (k, i, 0)))
    else:
        a_spec = (pl.BlockSpec((tk, tm), lambda i, j, k: (k, i)) if ta
                  else pl.BlockSpec((tm, tk), lambda i, j, k: (i, k)))
    if b_blk:
        b_spec = (pl.BlockSpec((None, tn, tk), lambda i, j, k: (k, j, 0)) if tb
                  else pl.BlockSpec((None, tk, tn), lambda i, j, k: (j, k, 0)))
    else:
        b_spec = (pl.BlockSpec((tn, tk), lambda i, j, k: (j, k)) if tb
                  else pl.BlockSpec((tk, tn), lambda i, j, k: (k, j)))
    if o_blk:
        o_spec = pl.BlockSpec((None, tm, tn), lambda i, j, k: (j, i, 0))
        o_shape = SDS((n_dim // tn, m_dim, tn), out_dtype)
    else:
        o_spec = pl.BlockSpec((tm, tn), lambda i, j, k: (i, j))
        o_shape = SDS((m_dim, n_dim), out_dtype)
    return pl.pallas_call(
        body, grid=(m_dim // tm, n_dim // tn, nk),
        in_specs=[a_spec, b_spec],
        out_specs=o_spec,
        out_shape=o_shape,
        scratch_shapes=[pltpu.VMEM((tm, tn), F32)],
        compiler_params=_params(("parallel", "parallel", "arbitrary")),
        name=name,
    )(a, b)


def _row(width, col=0):
    return pl.BlockSpec((TR, width), lambda i: (i, col))


def _vec(width):
    return pl.BlockSpec((1, width), lambda i: (0, 0))


def _norm_mod_fwd(x, g, shift, scale, name):
    t = x.shape[0]

    def body(x_ref, g_ref, sh_ref, sc_ref, o_ref):
        xv = x_ref[...]
        r = lax.rsqrt(jnp.mean(xv * xv, axis=-1, keepdims=True) + RMS_EPS)
        o_ref[...] = (((xv * r) * g_ref[...]) * (1.0 + sc_ref[...]) + sh_ref[...]).astype(o_ref.dtype)

    return pl.pallas_call(
        body, grid=(t // TR,), in_specs=[_row(D), _vec(D), _vec(D), _vec(D)], out_specs=_row(D),
        out_shape=SDS((t, D), BF16), compiler_params=_params(("parallel",)), name=name,
    )(x, g, shift, scale)


def _norm_mod_bwd(x, g, scale, dh, dx_in, name):
    t = x.shape[0]

    def body(x_ref, g_ref, sc_ref, dh_ref, dxi_ref, dx_ref, dsh_ref, dsc_ref, dg_ref):
        @pl.when(pl.program_id(0) == 0)
        def _():
            dsh_ref[...] = jnp.zeros_like(dsh_ref)
            dsc_ref[...] = jnp.zeros_like(dsc_ref)
            dg_ref[...] = jnp.zeros_like(dg_ref)

        xv = x_ref[...]
        gv = g_ref[...]
        dh = dh_ref[...]
        r = lax.rsqrt(jnp.mean(xv * xv, axis=-1, keepdims=True) + RMS_EPS)
        n = xv * r
        dsh_ref[...] += jnp.sum(dh, axis=0, keepdims=True)
        dsc_ref[...] += jnp.sum(dh * (n * gv), axis=0, keepdims=True)
        tt = dh * (1.0 + sc_ref[...])
        dg_ref[...] += jnp.sum(tt * n, axis=0, keepdims=True)
        dn = tt * gv
        dx_ref[...] = dxi_ref[...] + r * (dn - n * jnp.mean(dn * n, axis=-1, keepdims=True))

    return pl.pallas_call(
        body, grid=(t // TR,), in_specs=[_row(D), _vec(D), _vec(D), _row(D), _row(D)],
        out_specs=[_row(D), _vec(D), _vec(D), _vec(D)],
        out_shape=[SDS((t, D), F32), SDS((1, D), F32), SDS((1, D), F32), SDS((1, D), F32)],
        compiler_params=_params(("arbitrary",)), name=name,
    )(x, g, scale, dh, dx_in)


def _swiglu_specs():
    pair = pl.BlockSpec((2, None, TR, FB), lambda i, j: (0, j, i, 0))
    one = pl.BlockSpec((None, TR, FB), lambda i, j: (j, i, 0))
    return pair, one


def _swiglu_fwd(u, name):
    t = u.shape[2]

    def body(u_ref, o_ref):
        o_ref[...] = (_silu(u_ref[0]) * u_ref[1]).astype(o_ref.dtype)

    pair, one = _swiglu_specs()
    return pl.pallas_call(
        body, grid=(t // TR, NDEV // 2), in_specs=[pair], out_specs=one,
        out_shape=SDS((NDEV // 2, t, FB), BF16), compiler_params=_params(("parallel", "parallel")), name=name,
    )(u)


def _swiglu_bwd(u, da, name):
    t = u.shape[2]

    def body(u_ref, da_ref, o_ref):
        gv = u_ref[0]
        dav = da_ref[...]
        o_ref[0] = (dav * u_ref[1] * _dsilu(gv)).astype(o_ref.dtype)
        o_ref[1] = (dav * _silu(gv)).astype(o_ref.dtype)

    pair, one = _swiglu_specs()
    return pl.pallas_call(
        body, grid=(t // TR, NDEV // 2), in_specs=[pair, one], out_specs=pair,
        out_shape=SDS((2, NDEV // 2, t, FB), BF16), compiler_params=_params(("parallel", "parallel")), name=name,
    )(u, da)


def _resid_fwd(x, y, gate, coef, name):
    t = x.shape[0]

    def body(x_ref, y_ref, g_ref, o_ref):
        o_ref[...] = x_ref[...] + (coef * g_ref[...]) * y_ref[...]

    return pl.pallas_call(
        body, grid=(t // TR,), in_specs=[_row(D), _row(D), _vec(D)], out_specs=_row(D),
        out_shape=SDS((t, D), F32), compiler_params=_params(("parallel",)), name=name,
    )(x, y, gate)


def _resid_bwd(dx, y, gate, coef, name):
    t = dx.shape[0]

    def body(dx_ref, y_ref, g_ref, dy_ref, dg_ref):
        @pl.when(pl.program_id(0) == 0)
        def _():
            dg_ref[...] = jnp.zeros_like(dg_ref)

        dxv = dx_ref[...]
        dy_ref[...] = ((coef * g_ref[...]) * dxv).astype(dy_ref.dtype)
        dg_ref[...] += jnp.sum((coef * dxv) * y_ref[...], axis=0, keepdims=True)

    return pl.pallas_call(
        body, grid=(t // TR,), in_specs=[_row(D), _row(D), _vec(D)], out_specs=[_row(D), _vec(D)],
        out_shape=[SDS((t, D), BF16), SDS((1, D), F32)],
        compiler_params=_params(("arbitrary",)), name=name,
    )(dx, y, gate)


def _final_loss(x, fg, target, name):
    t = x.shape[0]
    nt = t // TR

    def body(x_ref, g_ref, t_ref, loss_ref, dx_ref, dg_ref, acc_ref):
        i = pl.program_id(0)

        @pl.when(i == 0)
        def _():
            acc_ref[...] = jnp.zeros_like(acc_ref)
            dg_ref[...] = jnp.zeros_like(dg_ref)

        xv = x_ref[...]
        gv = g_ref[...]
        r = lax.rsqrt(jnp.mean(xv * xv, axis=-1, keepdims=True) + RMS_EPS)
        n = xv * r
        err = n * gv - t_ref[...]
        acc_ref[...] += jnp.sum(err * err, axis=0, keepdims=True)
        dy = err * (1.0 / D)
        dg_ref[...] += jnp.sum(dy * n, axis=0, keepdims=True)
        dn = dy * gv
        dx_ref[...] = r * (dn - n * jnp.mean(dn * n, axis=-1, keepdims=True))

        @pl.when(i == nt - 1)
        def _():
            tot = jnp.sum(acc_ref[...], axis=1, keepdims=True) * (0.5 / D)
            loss_ref[...] = jnp.broadcast_to(tot, loss_ref.shape)

    return pl.pallas_call(
        body, grid=(nt,), in_specs=[_row(D), _vec(D), _row(D)],
        out_specs=[_vec(128), _row(D), _vec(D)],
        out_shape=[SDS((1, 128), F32), SDS((t, D), F32), SDS((1, D), F32)],
        scratch_shapes=[pltpu.VMEM((1, D), F32)],
        compiler_params=_params(("arbitrary",)), name=name,
    )(x, fg, target)


def _halo_prev(width, col):
    per = TR // HALO
    return pl.BlockSpec((HALO, width), lambda i: (jnp.maximum(i * per - 1, 0), col))


def _halo_next(width, col, nt):
    per = TR // HALO
    return pl.BlockSpec((HALO, width), lambda i: (jnp.minimum((i + 1) * per, nt * per - 1), col))


def _pool_windows(ext, tile_index):
    rows = _iota((TR, PG), 0) + tile_index * TR + 1
    pooled, counts = [], []
    for gi in range(4):
        w = 2 << gi
        e = ext[:, gi * PG:(gi + 1) * PG]
        s = e
        step = 1
        while step < w:
            s = s + pltpu.roll(s, step, 0)
            step *= 2
        cnt = jnp.minimum(rows, w).astype(F32)
        pooled.append(s[HALO:] / cnt - e[HALO:])
        counts.append(cnt)
    return pooled, counts


def _pool_fwd(proj, pool_w, pool_scale, pool_proj, name):
    t = proj.shape[0]
    xcol = OFF_XP // PW

    def body(x_ref, h_ref, pw_ref, ps_ref, pp_ref, o_ref):
        i = pl.program_id(0)
        halo = jnp.where(i > 0, h_ref[...], 0.0)
        ext = jnp.concatenate([halo, x_ref[...]], axis=0)
        pooled, _ = _pool_windows(ext, i)
        mixed = [_dg(pooled[g].astype(BF16), pw_ref[g].astype(BF16), NN) for g in range(4)]
        ypre = jnp.concatenate(mixed, axis=1) * ps_ref[...]
        o_ref[...] = _dg(ypre.astype(BF16), pp_ref[...], NN)

    return pl.pallas_call(
        body, grid=(t // TR,),
        in_specs=[_row(PW, xcol), _halo_prev(PW, xcol),
                  pl.BlockSpec((4, PG, PG), lambda i: (0, 0, 0)), _vec(PW),
                  pl.BlockSpec((PW, D), lambda i: (0, 0))],
        out_specs=_row(D), out_shape=SDS((t, D), F32),
        compiler_params=_params(("parallel",)), name=name,
    )(proj, proj, pool_w, pool_scale, pool_proj)


def _pool_bwd_local(proj, pool_w, pool_scale, pool_proj, dya, name):
    t = proj.shape[0]
    xcol = OFF_XP // PW

    def body(x_ref, h_ref, pw_ref, ps_ref, pp_ref, dya_ref, dwin_ref, dpl_ref, dpw_ref, dps_ref, dpp_ref):
        i = pl.program_id(0)

        @pl.when(i == 0)
        def _():
            dpw_ref[...] = jnp.zeros_like(dpw_ref)
            dps_ref[...] = jnp.zeros_like(dps_ref)
            dpp_ref[...] = jnp.zeros_like(dpp_ref)

        halo = jnp.where(i > 0, h_ref[...], 0.0)
        ext = jnp.concatenate([halo, x_ref[...]], axis=0)
        pooled, counts = _pool_windows(ext, i)
        mixed = jnp.concatenate(
            [_dg(pooled[g].astype(BF16), pw_ref[g].astype(BF16), NN) for g in range(4)], axis=1)
        ps = ps_ref[...]
        ypre = mixed * ps
        dyab = dya_ref[...].astype(BF16)
        dypre = _dg(dyab, pp_ref[...], NT)
        dpp_ref[...] += _dg(ypre.astype(BF16), dyab, TN)
        dps_ref[...] += jnp.sum(dypre * mixed, axis=0, keepdims=True)
        dmixed = dypre * ps
        for g in range(4):
            dm = dmixed[:, g * PG:(g + 1) * PG].astype(BF16)
            dpw_ref[g] += _dg(pooled[g].astype(BF16), dm, TN)
            dpooled = _dg(dm, pw_ref[g].astype(BF16), NT)
            dwin_ref[:, g * PG:(g + 1) * PG] = dpooled / counts[g]
            dpl_ref[:, g * PG:(g + 1) * PG] = dpooled

    return pl.pallas_call(
        body, grid=(t // TR,),
        in_specs=[_row(PW, xcol), _halo_prev(PW, xcol),
                  pl.BlockSpec((4, PG, PG), lambda i: (0, 0, 0)), _vec(PW),
                  pl.BlockSpec((PW, D), lambda i: (0, 0)), _row(D)],
        out_specs=[_row(PW), _row(PW), pl.BlockSpec((4, PG, PG), lambda i: (0, 0, 0)), _vec(PW),
                   pl.BlockSpec((PW, D), lambda i: (0, 0))],
        out_shape=[SDS((t, PW), F32), SDS((t, PW), F32), SDS((4, PG, PG), F32), SDS((1, PW), F32),
                   SDS((PW, D), F32)],
        compiler_params=_params(("arbitrary",)), name=name,
    )(proj, proj, pool_w, pool_scale, pool_proj, dya)


def _pool_bwd_window(dwin, dpl, name):
    t = dwin.shape[0]
    nt = t // TR
    ext_rows = TR + HALO

    def body(dw_ref, h_ref, dp_ref, o_ref):
        i = pl.program_id(0)
        halo = jnp.where(i < nt - 1, h_ref[...], 0.0)
        ext = jnp.concatenate([dw_ref[...], halo], axis=0)
        for gi in range(4):
            w = 2 << gi
            s = ext[:, gi * PG:(gi + 1) * PG]
            step = 1
            while step < w:
                s = s + pltpu.roll(s, ext_rows - step, 0)
                step *= 2
            o_ref[:, gi * PG:(gi + 1) * PG] = (s[:TR] - dp_ref[:, gi * PG:(gi + 1) * PG]).astype(o_ref.dtype)

    return pl.pallas_call(
        body, grid=(nt,), in_specs=[_row(PW), _halo_next(PW, 0, nt), _row(PW)], out_specs=_row(PW),
        out_shape=SDS((t, PW), BF16), compiler_params=_params(("parallel",)), name=name,
    )(dwin, dwin, dpl)


def _conv_group(ext, cw_ref, cols):
    acc = cw_ref[3:4, cols] * ext
    for j in range(3):
        acc = acc + cw_ref[j:j + 1, cols] * pltpu.roll(ext, 3 - j, 0)
    return acc[HALO:]


def _gate_terms(raw, al, dt):
    beta = _sigmoid(raw)
    xg = raw + dt
    sp = jnp.maximum(xg, 0.0) + jnp.log(1.0 + jnp.exp(-jnp.abs(xg)))
    g = -jnp.exp(al) * sp
    return beta, g, _sigmoid(xg)


def _dn_pre_fwd(proj, conv_w, al_row, dt_row, name):
    t = proj.shape[0]

    def body(x_ref, h_ref, cw_ref, ba_ref, al_ref, dt_ref, q_ref, k_ref, v_ref, bg_ref):
        i = pl.program_id(0)
        keep = i > 0
        for grp in range(24):
            cols = slice(grp * HD, (grp + 1) * HD)
            ext = jnp.concatenate([jnp.where(keep, h_ref[:, cols], 0.0), x_ref[:, cols]], axis=0)
            s = _silu(_conv_group(ext, cw_ref, cols))
            seg, head = divmod(grp, NH)
            hc = slice(head * HD, (head + 1) * HD)
            if seg == 0:
                q_ref[:, hc] = s * lax.rsqrt(jnp.sum(s * s, axis=-1, keepdims=True) + L2_EPS) * (HD ** -0.5)
            elif seg == 1:
                k_ref[:, hc] = s * lax.rsqrt(jnp.sum(s * s, axis=-1, keepdims=True) + L2_EPS)
            else:
                v_ref[:, hc] = s
        lane = _iota((TR, 128), 1)
        rowc = _iota((TR, 128), 0) % CH
        beta, g, _ = _gate_terms(ba_ref[...], al_ref[...], dt_ref[...])
        step = 1
        while step < CH:
            g = g + jnp.where(rowc >= step, pltpu.roll(g, step, 0), 0.0)
            step *= 2
        bg_ref[...] = jnp.where(lane < NH, beta, jnp.where(lane < 2 * NH, g, 0.0))

    return pl.pallas_call(
        body, grid=(t // TR,),
        in_specs=[_row(3 * D, 0), _halo_prev(3 * D, 0), pl.BlockSpec((4, 3 * D), lambda i: (0, 0)),
                  _row(128, OFF_BA // 128), _vec(128), _vec(128)],
        out_specs=[_row(D), _row(D), _row(D), _row(128)],
        out_shape=[SDS((t, D), F32), SDS((t, D), F32), SDS((t, D), F32), SDS((t, 128), F32)],
        compiler_params=_params(("parallel",)), name=name,
    )(proj, proj, conv_w, proj, al_row, dt_row)


def _dn_pre_bwd_act(proj, conv_w, al_row, dt_row, dq, dk, dv, dbg, name):
    t = proj.shape[0]

    def body(x_ref, h_ref, cw_ref, ba_ref, al_ref, dt_ref, dq_ref, dk_ref, dv_ref, dbg_ref,
             dc_ref, draw_ref, dal_ref, ddt_ref):
        i = pl.program_id(0)

        @pl.when(i == 0)
        def _():
            dal_ref[...] = jnp.zeros_like(dal_ref)
            ddt_ref[...] = jnp.zeros_like(ddt_ref)

        keep = i > 0
        for grp in range(24):
            cols = slice(grp * HD, (grp + 1) * HD)
            ext = jnp.concatenate([jnp.where(keep, h_ref[:, cols], 0.0), x_ref[:, cols]], axis=0)
            cv = _conv_group(ext, cw_ref, cols)
            seg, head = divmod(grp, NH)
            hc = slice(head * HD, (head + 1) * HD)
            if seg == 2:
                ds = dv_ref[:, hc]
            else:
                s = _silu(cv)
                r = lax.rsqrt(jnp.sum(s * s, axis=-1, keepdims=True) + L2_EPS)
                dy = dq_ref[:, hc] if seg == 0 else dk_ref[:, hc]
                c = (HD ** -0.5) if seg == 0 else 1.0
                ds = (c * r) * (dy - s * ((r * r) * jnp.sum(dy * s, axis=-1, keepdims=True)))
            dc_ref[:, cols] = ds * _dsilu(cv)
        lane = _iota((TR, 128), 1)
        rowc = _iota((TR, 128), 0) % CH
        isb = lane < NH
        isg = jnp.logical_and(lane >= NH, lane < 2 * NH)
        beta, g, sg = _gate_terms(ba_ref[...], al_ref[...], dt_ref[...])
        dbgv = dbg_ref[...]
        dg = dbgv
        step = 1
        while step < CH:
            dg = dg + jnp.where(rowc < CH - step, pltpu.roll(dg, TR - step, 0), 0.0)
            step *= 2
        da_raw = dg * (-jnp.exp(al_ref[...])) * sg
        draw_ref[...] = jnp.where(isb, dbgv * beta * (1.0 - beta), jnp.where(isg, da_raw, 0.0)).astype(draw_ref.dtype)
        dal_ref[...] += jnp.sum(jnp.where(isg, dg * g, 0.0), axis=0, keepdims=True)
        ddt_ref[...] += jnp.sum(jnp.where(isg, da_raw, 0.0), axis=0, keepdims=True)

    return pl.pallas_call(
        body, grid=(t // TR,),
        in_specs=[_row(3 * D, 0), _halo_prev(3 * D, 0), pl.BlockSpec((4, 3 * D), lambda i: (0, 0)),
                  _row(128, OFF_BA // 128), _vec(128), _vec(128), _row(D), _row(D), _row(D), _row(128)],
        out_specs=[_row(3 * D), _row(128), _vec(128), _vec(128)],
        out_shape=[SDS((t, 3 * D), F32), SDS((t, 128), BF16), SDS((1, 128), F32), SDS((1, 128), F32)],
        compiler_params=_params(("arbitrary",)), name=name,
    )(proj, proj, conv_w, proj, al_row, dt_row, dq, dk, dv, dbg)


def _dn_pre_bwd_conv(proj, conv_w, dconv, name):
    t = proj.shape[0]
    nt = t // TR
    ext_rows = TR + HALO

    def body(x_ref, h_ref, cw_ref, dc_ref, dn_ref, dx_ref, dcw_ref):
        i = pl.program_id(0)

        @pl.when(i == 0)
        def _():
            dcw_ref[...] = jnp.zeros_like(dcw_ref)

        keep_prev = i > 0
        keep_next = i < nt - 1
        for grp in range(24):
            cols = slice(grp * HD, (grp + 1) * HD)
            dct = dc_ref[:, cols]
            dext = jnp.concatenate([dct, jnp.where(keep_next, dn_ref[:, cols], 0.0)], axis=0)
            acc = cw_ref[3:4, cols] * dext
            for j in range(3):
                acc = acc + cw_ref[j:j + 1, cols] * pltpu.roll(dext, ext_rows - (3 - j), 0)
            dx_ref[:, cols] = acc[:TR].astype(dx_ref.dtype)
            xext = jnp.concatenate([jnp.where(keep_prev, h_ref[:, cols], 0.0), x_ref[:, cols]], axis=0)
            for j in range(4):
                xs = xext if j == 3 else pltpu.roll(xext, 3 - j, 0)
                dcw_ref[j:j + 1, cols] += jnp.sum(xs[HALO:] * dct, axis=0, keepdims=True)

    return pl.pallas_call(
        body, grid=(nt,),
        in_specs=[_row(3 * D, 0), _halo_prev(3 * D, 0), pl.BlockSpec((4, 3 * D), lambda i: (0, 0)),
                  _row(3 * D), _halo_next(3 * D, 0, nt)],
        out_specs=[_row(3 * D), pl.BlockSpec((4, 3 * D), lambda i: (0, 0))],
        out_shape=[SDS((t, 3 * D), BF16), SDS((4, 3 * D), F32)],
        compiler_params=_params(("arbitrary",)), name=name,
    )(proj, proj, conv_w, dconv, dconv)


def _dn_post_fwd(o, proj, gn, name):
    t = o.shape[0]

    def body(o_ref, z_ref, g_ref, out_ref):
        gv = g_ref[...]
        for h in range(NH):
            hc = slice(h * HD, (h + 1) * HD)
            ov = o_ref[:, hc]
            r = lax.rsqrt(jnp.mean(ov * ov, axis=-1, keepdims=True) + RMS_EPS)
            out_ref[:, hc] = (((ov * r) * gv) * _silu(z_ref[:, hc])).astype(out_ref.dtype)

    return pl.pallas_call(
        body, grid=(t // TR,), in_specs=[_row(D), _row(D, OFF_Z // D), _vec(HD)], out_specs=_row(D),
        out_shape=SDS((t, D), BF16), compiler_params=_params(("parallel",)), name=name,
    )(o, proj, gn)


def _dn_post_bwd(o, proj, gn, dob, name):
    t = o.shape[0]

    def body(o_ref, z_ref, g_ref, d_ref, do_ref, dz_ref, dg_ref):
        @pl.when(pl.program_id(0) == 0)
        def _():
            dg_ref[...] = jnp.zeros_like(dg_ref)

        gv = g_ref[...]
        acc = jnp.zeros((1, HD), F32)
        for h in range(NH):
            hc = slice(h * HD, (h + 1) * HD)
            ov = o_ref[:, hc]
            zv = z_ref[:, hc]
            dv = d_ref[:, hc]
            r = lax.rsqrt(jnp.mean(ov * ov, axis=-1, keepdims=True) + RMS_EPS)
            n = ov * r
            dz_ref[:, hc] = (dv * (n * gv) * _dsilu(zv)).astype(dz_ref.dtype)
            dng = dv * _silu(zv)
            acc = acc + jnp.sum(dng * n, axis=0, keepdims=True)
            dn = dng * gv
            do_ref[:, hc] = r * (dn - n * jnp.mean(dn * n, axis=-1, keepdims=True))
        dg_ref[...] += acc

    return pl.pallas_call(
        body, grid=(t // TR,), in_specs=[_row(D), _row(D, OFF_Z // D), _vec(HD), _row(D)],
        out_specs=[_row(D), _row(D), _vec(HD)],
        out_shape=[SDS((t, D), F32), SDS((t, D), BF16), SDS((1, HD), F32)],
        compiler_params=_params(("arbitrary",)), name=name,
    )(o, proj, gn, dob)


def _merge_fwd(ya, yb, proj, name):
    t = ya.shape[0]

    def body(a_ref, b_ref, gp_ref, gd_ref, o_ref):
        o_ref[...] = (_sigmoid(gp_ref[...]) * a_ref[...] + _sigmoid(gd_ref[...]) * b_ref[...]).astype(o_ref.dtype)

    return pl.pallas_call(
        body, grid=(t // TR,), in_specs=[_row(D), _row(D), _row(D, OFF_GP // D), _row(D, OFF_GD // D)],
        out_specs=_row(D), out_shape=SDS((t, D), BF16),
        compiler_params=_params(("parallel",)), name=name,
    )(ya, yb, proj, proj)


def _merge_bwd(dm, ya, yb, proj, name):
    t = ya.shape[0]

    def body(d_ref, a_ref, b_ref, gp_ref, gd_ref, da_ref, db_ref, dgp_ref, dgd_ref):
        dv = d_ref[...]
        sp = _sigmoid(gp_ref[...])
        sd = _sigmoid(gd_ref[...])
        da_ref[...] = dv * sp
        db_ref[...] = (dv * sd).astype(db_ref.dtype)
        dgp_ref[...] = (dv * a_ref[...] * sp * (1.0 - sp)).astype(dgp_ref.dtype)
        dgd_ref[...] = (dv * b_ref[...] * sd * (1.0 - sd)).astype(dgd_ref.dtype)

    return pl.pallas_call(
        body, grid=(t // TR,),
        in_specs=[_row(D), _row(D), _row(D), _row(D, OFF_GP // D), _row(D, OFF_GD // D)],
        out_specs=[_row(D)] * 4,
        out_shape=[SDS((t, D), F32), SDS((t, D), BF16), SDS((t, D), BF16), SDS((t, D), BF16)],
        compiler_params=_params(("parallel",)), name=name,
    )(dm, ya, yb, proj, proj)


def _split2(x):
    hi = x.astype(BF16)
    return hi, (x - hi.astype(F32)).astype(BF16)


def _dot3(a, b, dims):
    ah, al = _split2(a)
    bh, bl = _split2(b)
    return _dg(ah, bh, dims) + (_dg(ah, bl, dims) + _dg(al, bh, dims))


def _neumann_inverses(mats):
    ri = _iota((CH, CH), 0)
    ci = _iota((CH, CH), 1)
    eye = jnp.where(ri == ci, 1.0, 0.0).astype(F32)
    xs = [-a for a in mats]
    ps = [eye + x for x in xs]
    for _ in range(5):
        xs = [_dot3(x, x, NN) for x in xs]
        ps = [p + _dot3(p, x, NN) for p, x in zip(ps, xs)]
    return ps


def _solve_with(inv):
    @jax.custom_vjp
    def solve(a, rhs):
        return _dot3(inv, rhs, NN)

    def fwd(a, rhs):
        sol = _dot3(inv, rhs, NN)
        return sol, sol

    def bwd(sol, d):
        drhs = _dot3(inv, d, TN)
        return -_dot3(drhs, sol, NT), drhs

    solve.defvjp(fwd, bwd)
    return solve


@jax.custom_vjp
def _rows_to_lanes(g64):
    ri = _iota((CH, CH), 0)
    ci = _iota((CH, CH), 1)
    diag = jnp.where(ri == ci, g64, 0.0)
    ones = jnp.ones((CH, CH), BF16)
    hi = diag.astype(BF16)
    rem = diag - hi.astype(F32)
    mid = rem.astype(BF16)
    lo = (rem - mid.astype(F32)).astype(BF16)
    return _dg(ones, hi, NN) + (_dg(ones, mid, NN) + _dg(ones, lo, NN))


def _rows_to_lanes_bwd(_, d):
    ri = _iota((CH, CH), 0)
    ci = _iota((CH, CH), 1)
    return (jnp.where(ri == ci, jnp.broadcast_to(jnp.sum(d, axis=0, keepdims=True), (CH, CH)), 0.0),)


_rows_to_lanes.defvjp(lambda g64: (_rows_to_lanes(g64), None), _rows_to_lanes_bwd)


def _chunk_local(solve_all, q, k, v, g128, g64, gl128, b128, b64):
    ri = _iota((CH, CH), 0)
    ci = _iota((CH, CH), 1)
    causal = ri >= ci
    strict = ri > ci
    gj = [_rows_to_lanes(g) for g in g64]
    decay = [jnp.where(causal, jnp.exp(jnp.where(causal, g - t, 0.0)), 0.0) for g, t in zip(g64, gj)]
    kk = [_nt(x, x) for x in k]
    a = [jnp.where(strict, b * m * dc, 0.0) for b, m, dc in zip(b64, kk, decay)]
    eg = [jnp.exp(g) for g in g128]
    rhs = [jnp.concatenate([b * x, (b * e) * y], axis=1) for b, x, e, y in zip(b128, v, eg, k)]
    sol = solve_all(a, rhs)
    qk = [jnp.where(causal, _nt(x, y) * dc, 0.0) for x, y, dc in zip(q, k, decay)]
    return ([s[:, :HD] for s in sol], [s[:, HD:] for s in sol], qk, [x * e for x, e in zip(q, eg)],
            [x * jnp.exp(gl - g) for x, gl, g in zip(k, gl128, g128)], [jnp.exp(gl) for gl in gl128])


def _all_head_gates(bgv):
    return tuple(list(z) for z in zip(*[_head_gates(bgv, h) for h in range(NH)]))


def _head_gates(bgv, h):
    lane = _iota((CH, 128), 1)
    row = _iota((CH, 128), 0)
    bcol = jnp.sum(jnp.where(lane == h, bgv, 0.0), axis=1, keepdims=True)
    gcol = jnp.sum(jnp.where(lane == NH + h, bgv, 0.0), axis=1, keepdims=True)
    g128 = jnp.broadcast_to(gcol, (CH, 128))
    gl128 = jnp.broadcast_to(jnp.sum(jnp.where(row == CH - 1, g128, 0.0), axis=0, keepdims=True), (CH, 128))
    return (g128, jnp.broadcast_to(gcol, (CH, CH)), gl128,
            jnp.broadcast_to(bcol, (CH, 128)), jnp.broadcast_to(bcol, (CH, CH)))


def _chunk_specs():
    row = pl.BlockSpec((CH, D), lambda i: (i, 0))
    small = pl.BlockSpec((CH, 128), lambda i: (i, 0))
    qk = pl.BlockSpec((NH, CH, CH), lambda i: (i, 0, 0))
    eg = pl.BlockSpec((1, NH, 128), lambda i: (i, 0, 0))
    return row, small, qk, eg


def _dn_local_fwd(q, k, v, bg, name):
    t = q.shape[0]
    n = t // CH

    def body(q_ref, k_ref, v_ref, bg_ref, u_ref, w_ref, qk_ref, qd_ref, kd_ref, eg_ref, inv_ref):
        cols = [slice(h * HD, (h + 1) * HD) for h in range(NH)]

        def solve_all(mats, rhs):
            invs = _neumann_inverses(mats)
            for h in range(NH):
                inv_ref[h] = invs[h]
            return [_dot3(m, r, NN) for m, r in zip(invs, rhs)]

        u, w, qk, qd, kd, egl = _chunk_local(
            solve_all, [q_ref[:, c] for c in cols], [k_ref[:, c] for c in cols], [v_ref[:, c] for c in cols],
            *_all_head_gates(bg_ref[...]))
        for h, hc in enumerate(cols):
            u_ref[:, hc] = u[h]
            w_ref[:, hc] = w[h].astype(w_ref.dtype)
            qd_ref[:, hc] = qd[h].astype(qd_ref.dtype)
            kd_ref[:, hc] = kd[h].astype(kd_ref.dtype)
            qk_ref[h] = qk[h].astype(qk_ref.dtype)
            eg_ref[0, h:h + 1, :] = egl[h][0:1, :]

    row, small, qkb, egb = _chunk_specs()
    return pl.pallas_call(
        body, grid=(n,), in_specs=[row, row, row, small], out_specs=[row, row, qkb, row, row, egb, qkb],
        out_shape=[SDS((t, D), F32), SDS((t, D), BF16), SDS((n * NH, CH, CH), BF16), SDS((t, D), BF16),
                   SDS((t, D), BF16), SDS((n, NH, 128), F32), SDS((n * NH, CH, CH), F32)],
        compiler_params=_params(("parallel",)), name=name,
    )(q, k, v, bg)


def _dn_local_bwd(q, k, v, bg, inv, du, dw, dqk, dqd, dkd, deg, name):
    t = q.shape[0]
    n = t // CH

    def body(q_ref, k_ref, v_ref, bg_ref, inv_ref, du_ref, dw_ref, dqk_ref, dqd_ref, dkd_ref, deg_ref,
             dq_ref, dk_ref, dv_ref, dbg_ref):
        bgv = bg_ref[...]
        lane = _iota((CH, 128), 1)
        row = _iota((CH, 128), 0)
        first = jnp.where(row == 0, 1.0, 0.0)
        acc = jnp.zeros((CH, 128), F32)
        cols = [slice(h * HD, (h + 1) * HD) for h in range(NH)]
        solves = [_solve_with(inv_ref[h]) for h in range(NH)]

        def solve_all(mats, rhs):
            return [f(m, r) for f, m, r in zip(solves, mats, rhs)]

        _, vjp = jax.vjp(functools.partial(_chunk_local, solve_all),
                         [q_ref[:, c] for c in cols], [k_ref[:, c] for c in cols], [v_ref[:, c] for c in cols],
                         *_all_head_gates(bgv))
        cts = ([du_ref[:, c] for c in cols], [dw_ref[:, c] for c in cols], [dqk_ref[h] for h in range(NH)],
               [dqd_ref[:, c] for c in cols], [dkd_ref[:, c] for c in cols],
               [jnp.broadcast_to(deg_ref[0, h:h + 1, :], (CH, 128)) * first for h in range(NH)])
        dq, dk, dv, dg128, dg64, dgl, db128, db64 = vjp(cts)
        for h, hc in enumerate(cols):
            dq_ref[:, hc] = dq[h]
            dk_ref[:, hc] = dk[h]
            dv_ref[:, hc] = dv[h]
            dg = jnp.sum(dg128[h], axis=1, keepdims=True) + jnp.sum(dg64[h], axis=1, keepdims=True)
            tot = jnp.sum(jnp.sum(dgl[h], axis=0, keepdims=True), axis=1, keepdims=True)
            dg = dg + jnp.where(row[:, 0:1] == CH - 1, tot, 0.0)
            db = jnp.sum(db128[h], axis=1, keepdims=True) + jnp.sum(db64[h], axis=1, keepdims=True)
            acc = acc + jnp.where(lane == h, db, 0.0) + jnp.where(lane == NH + h, dg, 0.0)
        dbg_ref[...] = acc

    row, small, qkb, egb = _chunk_specs()
    return pl.pallas_call(
        body, grid=(n,), in_specs=[row, row, row, small, qkb, row, row, qkb, row, row, egb],
        out_specs=[row, row, row, small],
        out_shape=[SDS((t, D), F32)] * 3 + [SDS((t, 128), F32)],
        compiler_params=_params(("parallel",)), name=name,
    )(q, k, v, bg, inv, du, dw, dqk, dqd, dkd, deg)


def _state_step(s, u, w, qk, qd, kd, egl):
    ws = [_nn(a, b) for a, b in zip(w, s)]
    v_new = [a - b for a, b in zip(u, ws)]
    qs = [_nn(a, b) for a, b in zip(qd, s)]
    intra = [_nn(a, b) for a, b in zip(qk, v_new)]
    upd = [_tn(a, b) for a, b in zip(kd, v_new)]
    return [a * e + b for a, e, b in zip(s, egl, upd)], [a + b for a, b in zip(qs, intra)]


def _dn_scan_fwd(u, w, qk, qd, kd, eg, name):
    t = u.shape[0]
    n = t // CH

    def body(u_ref, w_ref, qk_ref, qd_ref, kd_ref, eg_ref, o_ref, save_ref, s_ref):
        @pl.when(pl.program_id(0) == 0)
        def _():
            s_ref[...] = jnp.zeros_like(s_ref)

        cols = [slice(h * HD, (h + 1) * HD) for h in range(NH)]
        s = [s_ref[h] for h in range(NH)]
        for h in range(NH):
            save_ref[0, h] = s[h]
        s_new, o = _state_step(
            s, [u_ref[:, c] for c in cols], [w_ref[:, c].astype(F32) for c in cols],
            [qk_ref[h].astype(F32) for h in range(NH)], [qd_ref[:, c].astype(F32) for c in cols],
            [kd_ref[:, c].astype(F32) for c in cols], [eg_ref[0, h:h + 1, :] for h in range(NH)])
        for h, hc in enumerate(cols):
            o_ref[:, hc] = o[h]
            s_ref[h] = s_new[h]

    row, _, qkb, egb = _chunk_specs()
    return pl.pallas_call(
        body, grid=(n,), in_specs=[row, row, qkb, row, row, egb],
        out_specs=[row, pl.BlockSpec((1, NH, HD, HD), lambda i: (i, 0, 0, 0))],
        out_shape=[SDS((t, D), F32), SDS((n, NH, HD, HD), F32)],
        scratch_shapes=[pltpu.VMEM((NH, HD, HD), F32)],
        compiler_params=_params(("arbitrary",)), name=name,
    )(u, w, qk, qd, kd, eg)


def _dn_scan_bwd(u, w, qk, qd, kd, eg, saved, do, name):
    t = u.shape[0]
    n = t // CH

    def body(u_ref, w_ref, qk_ref, qd_ref, kd_ref, eg_ref, sv_ref, do_ref,
             du_ref, dw_ref, dqk_ref, dqd_ref, dkd_ref, deg_ref, ds_ref):
        @pl.when(pl.program_id(0) == 0)
        def _():
            ds_ref[...] = jnp.zeros_like(ds_ref)

        cols = [slice(h * HD, (h + 1) * HD) for h in range(NH)]
        _, vjp = jax.vjp(
            _state_step, [sv_ref[0, h] for h in range(NH)], [u_ref[:, c] for c in cols],
            [w_ref[:, c].astype(F32) for c in cols], [qk_ref[h].astype(F32) for h in range(NH)],
            [qd_ref[:, c].astype(F32) for c in cols], [kd_ref[:, c].astype(F32) for c in cols],
            [eg_ref[0, h:h + 1, :] for h in range(NH)])
        ds, du, dw, dqk, dqd, dkd, deg = vjp(([ds_ref[h] for h in range(NH)], [do_ref[:, c] for c in cols]))
        for h, hc in enumerate(cols):
            ds_ref[h] = ds[h]
            du_ref[:, hc] = du[h]
            dw_ref[:, hc] = dw[h]
            dqk_ref[h] = dqk[h]
            dqd_ref[:, hc] = dqd[h]
            dkd_ref[:, hc] = dkd[h]
            deg_ref[0, h:h + 1, :] = deg[h]

    rev = lambda i: (n - 1 - i, 0)
    rev3 = lambda i: (n - 1 - i, 0, 0)
    row = pl.BlockSpec((CH, D), rev)
    qkb = pl.BlockSpec((NH, CH, CH), rev3)
    egb = pl.BlockSpec((1, NH, 128), rev3)
    return pl.pallas_call(
        body, grid=(n,),
        in_specs=[row, row, qkb, row, row, egb,
                  pl.BlockSpec((1, NH, HD, HD), lambda i: (n - 1 - i, 0, 0, 0)), row],
        out_specs=[row, row, qkb, row, row, egb],
        out_shape=[SDS((t, D), F32), SDS((t, D), F32), SDS((n * NH, CH, CH), F32), SDS((t, D), F32),
                   SDS((t, D), F32), SDS((n, NH, 128), F32)],
        scratch_shapes=[pltpu.VMEM((NH, HD, HD), F32)],
        compiler_params=_params(("arbitrary",)), name=name,
    )(u, w, qk, qd, kd, eg, saved, do)


def _ada_fwd(c_all, ada_w, ada_b, name):
    ncol = ada_w.shape[1]

    def body(c_ref, w_ref, b_ref, o_ref):
        o_ref[...] = _dg(_silu(c_ref[...]), w_ref[...], NN, HI) + b_ref[...]

    return pl.pallas_call(body, out_shape=SDS((NDEV, ncol), F32),
                          compiler_params=pltpu.CompilerParams(vmem_limit_bytes=VMEM_LIMIT), name=name,
                          )(c_all, ada_w, ada_b)


def _ada_bwd(c_all_t, dmod, name):
    ncol = dmod.shape[1]

    def body(c_ref, d_ref, o_ref):
        sc = _silu(c_ref[...])
        acc = sc[:, 0:1] * d_ref[0:1, :]
        for b in range(1, NDEV):
            acc = acc + sc[:, b:b + 1] * d_ref[b:b + 1, :]
        o_ref[...] = acc

    return pl.pallas_call(body, out_shape=SDS((D, ncol), F32),
                          compiler_params=pltpu.CompilerParams(vmem_limit_bytes=VMEM_LIMIT), name=name,
                          )(c_all_t, dmod)


def _sum_devices(parts, out_dtype, name):
    _, r, c = parts.shape
    tr = TR if r % TR == 0 else r

    def body(p_ref, o_ref):
        acc = p_ref[0].astype(F32)
        for i in range(1, NDEV):
            acc = acc + p_ref[i].astype(F32)
        o_ref[...] = acc.astype(o_ref.dtype)

    return pl.pallas_call(
        body, grid=(r // tr,), in_specs=[pl.BlockSpec((NDEV, tr, c), lambda i: (0, i, 0))],
        out_specs=pl.BlockSpec((tr, c), lambda i: (i, 0)), out_shape=SDS((r, c), out_dtype),
        compiler_params=_params(("parallel",)), name=name,
    )(parts)


def _adamw(w, g, m, v, name):
    r, c = w.shape
    tr = _pick(r, (256, 128, 88, 8)) if r % 8 == 0 else r
    bc1 = 1.0 - ADAM_B1 ** ADAM_STEP
    bc2 = 1.0 - ADAM_B2 ** ADAM_STEP

    def body(w_ref, g_ref, m_ref, v_ref, d_ref, nm_ref, nv_ref):
        gv = g_ref[...]
        m_new = ADAM_B1 * m_ref[...] + (1.0 - ADAM_B1) * gv
        v_new = ADAM_B2 * v_ref[...] + (1.0 - ADAM_B2) * (gv * gv)
        nm_ref[...] = m_new
        nv_ref[...] = v_new
        d_ref[...] = -ADAM_LR * ((m_new / bc1) / (jnp.sqrt(v_new / bc2) + ADAM_EPS) + ADAM_WD * w_ref[...])

    spec = pl.BlockSpec((tr, c), lambda i: (i, 0))
    return pl.pallas_call(
        body, grid=(r // tr,), in_specs=[spec] * 4, out_specs=[spec] * 3,
        out_shape=[SDS((r, c), F32)] * 3, compiler_params=_params(("parallel",)), name=name,
    )(w, g, m, v)


ANY = pl.BlockSpec(memory_space=pl.ANY)
MESH = pl.DeviceIdType.MESH


def _all_gather(xs, name):
    n = len(xs)

    def body(*refs):
        x_refs, out_refs = refs[:n], refs[n:2 * n]
        send_sems, recv_sems, local_sems = refs[2 * n:]
        mx, my, mc = lax.axis_index("x"), lax.axis_index("y"), lax.axis_index("c")
        me, sibling = (mx, my, mc), (mx, my, 1 - mc)
        chips = [(1 - mx, my), (mx, 1 - my), (1 - mx, 1 - my)]

        def rows(a, px, py, pc):
            return out_refs[a].at[4 * px + 2 * py + pc]

        def copy(a, k, block, to, src=None):
            return pltpu.make_async_remote_copy(
                src_ref=rows(a, *block) if src is None else src, dst_ref=rows(a, *block),
                send_sem=send_sems.at[a, k], recv_sem=recv_sems.at[a, k], device_id=to, device_id_type=MESH)

        mine = [pltpu.make_async_copy(x_refs[a], rows(a, *me), local_sems.at[a]) for a in range(n)]
        for cp in mine:
            cp.start()
        first = []
        for a in range(n):
            first.append(copy(a, 0, me, sibling, src=x_refs[a]))
            first += [copy(a, 1 + j, me, (*chip, mc), src=x_refs[a]) for j, chip in enumerate(chips)]
        for cp in first:
            cp.start()
        passed = []
        for a in range(n):
            for j, chip in enumerate(chips):
                copy(a, 1 + j, (*chip, mc), me).wait_recv()
                passed.append(copy(a, 4 + j, (*chip, mc), sibling))
                passed[-1].start()
        for a in range(n):
            copy(a, 0, sibling, me).wait_recv()
            for j, chip in enumerate(chips):
                copy(a, 4 + j, (*chip, 1 - mc), me).wait_recv()
        for cp in first + passed:
            cp.wait_send()
        for cp in mine:
            cp.wait()

    return pl.pallas_call(
        body, out_shape=[SDS((NDEV,) + x.shape, x.dtype) for x in xs], in_specs=[ANY] * n, out_specs=[ANY] * n,
        scratch_shapes=[pltpu.SemaphoreType.DMA((n, 7)), pltpu.SemaphoreType.DMA((n, 7)),
                        pltpu.SemaphoreType.DMA((n,))],
        name=name,
    )(*xs)


def _exchange_blocks(parts, name):
    n = len(parts)

    def body(*refs):
        p_refs, out_refs = refs[:n], refs[n:2 * n]
        send_sems, recv_sems, local_sems = refs[2 * n:]
        mx, my, mc = lax.axis_index("x"), lax.axis_index("y"), lax.axis_index("c")
        me = 4 * mx + 2 * my + mc
        mine = [pltpu.make_async_copy(p_refs[a].at[me], out_refs[a].at[me], local_sems.at[a]) for a in range(n)]
        for cp in mine:
            cp.start()
        copies = []
        for a in range(n):
            for k in range(1, NDEV):
                px = 1 - mx if k & 4 else mx
                py = 1 - my if k & 2 else my
                pc = 1 - mc if k & 1 else mc
                copies.append(pltpu.make_async_remote_copy(
                    src_ref=p_refs[a].at[4 * px + 2 * py + pc], dst_ref=out_refs[a].at[me],
                    send_sem=send_sems.at[a, k - 1], recv_sem=recv_sems.at[a, k - 1],
                    device_id=(px, py, pc), device_id_type=MESH))
        for cp in copies:
            cp.start()
        for cp in copies:
            cp.wait_recv()
        for cp in copies:
            cp.wait_send()
        for cp in mine:
            cp.wait()

    return pl.pallas_call(
        body, out_shape=[SDS(p.shape, p.dtype) for p in parts], in_specs=[ANY] * n, out_specs=[ANY] * n,
        scratch_shapes=[pltpu.SemaphoreType.DMA((n, 7)), pltpu.SemaphoreType.DMA((n, 7)),
                        pltpu.SemaphoreType.DMA((n,))],
        name=name,
    )(*parts)


HBM = pl.BlockSpec(memory_space=pltpu.HBM)
SEM = pl.BlockSpec(memory_space=pltpu.SEMAPHORE)
EFFECT = pltpu.SideEffectType.DATAFLOW_SIDE_EFFECTING


def _peers():
    mx, my, mc = lax.axis_index("x"), lax.axis_index("y"), lax.axis_index("c")
    out = []
    for k in range(1, NDEV):
        out.append((1 - mx if k & 4 else mx, 1 - my if k & 2 else my, 1 - mc if k & 1 else mc))
    return 4 * mx + 2 * my + mc, out


def _push_start(srcs, sliced, name):
    n = len(srcs)
    me_idx = 4 * lax.axis_index("x") + 2 * lax.axis_index("y") + lax.axis_index("c")
    lands = []
    for s in srcs:
        blk = lax.dynamic_index_in_dim(s, me_idx, 0, keepdims=True) if sliced else s[None]
        shape = s.shape if sliced else (NDEV,) + s.shape
        lands.append(lax.dynamic_update_slice(lax.empty(shape, s.dtype), blk, (me_idx,) + (0,) * (len(shape) - 1)))

    def body(*refs):
        src_refs, land_refs = refs[:n], refs[n:2 * n]
        send_sems, recv_sems = refs[2 * n:3 * n], refs[3 * n:4 * n]
        token = refs[-1]
        me, peers = _peers()
        for a in range(n):
            for k, (px, py, pc) in enumerate(peers):
                src = src_refs[a].at[4 * px + 2 * py + pc] if sliced else src_refs[a]
                pltpu.make_async_remote_copy(
                    src_ref=src, dst_ref=land_refs[a].at[me], send_sem=send_sems[a].at[k],
                    recv_sem=recv_sems[a].at[k], device_id=(px, py, pc), device_id_type=MESH).start()
        token[...] = jnp.zeros_like(token)

    outs = pl.pallas_call(
        body, name=name,
        out_shape=([pltpu.SemaphoreType.DMA((NDEV - 1,))] * (2 * n)
                   + [pltpu.HBM(s.shape, s.dtype) for s in srcs] + [pltpu.HBM(l.shape, l.dtype) for l in lands]
                   + [SDS((8, 128), F32)]),
        in_specs=[HBM] * (2 * n),
        out_specs=[SEM] * (2 * n) + [HBM] * (2 * n) + [pl.BlockSpec(memory_space=pltpu.VMEM)],
        input_output_aliases={i: 2 * n + i for i in range(2 * n)},
        compiler_params=pltpu.CompilerParams(has_side_effects=EFFECT),
    )(*[pltpu.with_memory_space_constraint(s, pltpu.HBM) for s in srcs],
      *[pltpu.with_memory_space_constraint(l, pltpu.HBM) for l in lands])
    sends, recvs = outs[:n], outs[n:2 * n]
    src_thru, land_thru = outs[2 * n:3 * n], outs[3 * n:4 * n]
    return [(sends[a], recvs[a], src_thru[a], land_thru[a]) for a in range(n)], outs[-1]


def _push_wait(started, sliced, after, name):
    n = len(started)

    def body(*refs):
        src_refs, land_refs = refs[:n], refs[n:2 * n]
        send_sems, recv_sems = refs[2 * n:3 * n], refs[3 * n:4 * n]
        me, peers = _peers()
        for a in range(n):
            for k, (px, py, pc) in enumerate(peers):
                src = src_refs[a].at[4 * px + 2 * py + pc] if sliced else src_refs[a]
                cp = pltpu.make_async_remote_copy(
                    src_ref=src, dst_ref=land_refs[a].at[me], send_sem=send_sems[a].at[k],
                    recv_sem=recv_sems[a].at[k], device_id=(px, py, pc), device_id_type=MESH)
                cp.wait_send()
                cp.wait_recv()

    srcs = [s[2] for s in started]
    lands = [s[3] for s in started]
    outs = pl.pallas_call(
        body, name=name,
        out_shape=[pltpu.HBM(s.shape, s.dtype) for s in srcs] + [pltpu.HBM(l.shape, l.dtype) for l in lands],
        in_specs=[HBM] * (2 * n) + [SEM] * (2 * n) + [pl.BlockSpec(memory_space=pl.ANY)],
        out_specs=[HBM] * (2 * n),
        input_output_aliases={i: i for i in range(2 * n)},
        compiler_params=pltpu.CompilerParams(has_side_effects=EFFECT),
    )(*srcs, *lands, *[s[0] for s in started], *[s[1] for s in started], after)
    return outs[n:]


def _cols_from_blocks(blocks):
    _, rows, w = blocks.shape
    return blocks.transpose(1, 0, 2).reshape(rows, NDEV * w)


def _cols_to_blocks(full):
    rows, total = full.shape
    return full.reshape(rows, NDEV, total // NDEV).transpose(1, 0, 2)


def _mix_pad(w):
    rows = w.shape[0]
    xp, q, k, v, z, b, a, gp, gd = jnp.split(w, (512, 1536, 2560, 3584, 4608, 4616, 4624, 5648), axis=1)
    pad = jnp.zeros((rows, MIXP - OFF_BA - 16), w.dtype)
    return jnp.concatenate([q, k, v, z, gp, gd, xp, b, a, pad], axis=1)


def _mix_unpad(w):
    q, k, v, z, gp, gd, xp, b, a = (w[:, OFF_Q:OFF_K], w[:, OFF_K:OFF_V], w[:, OFF_V:OFF_Z], w[:, OFF_Z:OFF_GP],
                                    w[:, OFF_GP:OFF_GD], w[:, OFF_GD:OFF_XP], w[:, OFF_XP:OFF_BA],
                                    w[:, OFF_BA:OFF_BA + 8], w[:, OFF_BA + 8:OFF_BA + 16])
    return jnp.concatenate([xp, q, k, v, z, b, a, gp, gd], axis=1)


def _lane_row(vec8):
    return jnp.zeros((1, 128), F32).at[0, NH:2 * NH].set(vec8)


def _tie(x, token):
    return lax.optimization_barrier((x, token))[0]


def _ffn_fwd(x, g, shift, scale, gate, started, tag):
    t = x.shape[0]
    h = _norm_mod_fwd(x, g, shift, scale, f"{tag}_norm")
    w_in, = _push_wait(started[:1], False, h, f"{tag}_gather_wait_in")
    u = _matmul(h, w_in, b_blk=True, o_blk=True, out_dtype=F32, name=f"{tag}_up").reshape(2, NDEV // 2, t, FB)
    a = _swiglu_fwd(u, f"{tag}_act")
    w_out, = _push_wait(started[1:], False, a, f"{tag}_gather_wait_out")
    w_out = w_out.reshape(FH, D)
    y = _matmul(a, w_out, a_blk=True, out_dtype=F32, name=f"{tag}_down")
    return _resid_fwd(x, y, gate, 0.5, f"{tag}_res"), (h, u, a, y), w_in, w_out


def _ffn_bwd(dx_out, x, g, scale, gate, w_in, w_out, saved, tag):
    h, u, a, y = saved
    t = x.shape[0]
    dy, dgate = _resid_bwd(dx_out, y, gate, 0.5, f"{tag}_res_bwd")
    da = _matmul(dy, w_out, tb=True, tn=FB, o_blk=True, out_dtype=F32, name=f"{tag}_down_dx")
    dw_out = _matmul(a, dy, ta=True, a_blk=True, out_dtype=BF16, name=f"{tag}_down_dw")
    du = _swiglu_bwd(u, da, f"{tag}_act_bwd").reshape(NDEV, t, FB)
    dw_in = _matmul(h, du, ta=True, b_blk=True, o_blk=True, out_dtype=BF16, name=f"{tag}_up_dw")
    started, token = _push_start([dw_in, dw_out.reshape(NDEV, FH // NDEV, D)], True, f"{tag}_grad_start")
    dh = _matmul(_tie(du, token), w_in, tb=True, a_blk=True, b_blk=True, out_dtype=F32, name=f"{tag}_up_dx")
    dx, dshift, dscale, dg = _norm_mod_bwd(x, g, scale, dh, dx_out, f"{tag}_norm_bwd")
    return dx, (dshift, dscale, dgate), dg, started


def kernel(x, c, ada_w, ada_b, norm_g, ffn1_w_in, ffn1_w_out, ffn2_w_in, ffn2_w_out, mix_w_in, conv_w, a_log, dt_bias, dn_norm_g, pool_w, pool_scale, pool_proj, dn_proj, mix_w_out, final_g, loss_target, m_ada_w, m_ada_b, m_norm_g, m_ffn1_w_in, m_ffn1_w_out, m_ffn2_w_in, m_ffn2_w_out, m_mix_w_in, m_conv_w, m_a_log, m_dt_bias, m_dn_norm_g, m_pool_w, m_pool_scale, m_pool_proj, m_dn_proj, m_mix_w_out, m_final_g, v_ada_w, v_ada_b, v_norm_g, v_ffn1_w_in, v_ffn1_w_out, v_ffn2_w_in, v_ffn2_w_out, v_mix_w_in, v_conv_w, v_a_log, v_dt_bias, v_dn_norm_g, v_pool_w, v_pool_scale, v_pool_proj, v_dn_proj, v_mix_w_out, v_final_g):
    me = 4 * lax.axis_index("x") + 2 * lax.axis_index("y") + lax.axis_index("c")
    x0 = x[0]
    target = loss_target[0]
    t = x0.shape[0]

    big = [ffn1_w_in[0], ffn1_w_out[0], ffn2_w_in[0], ffn2_w_out[0], mix_w_in[0], pool_proj[0], dn_proj[0],
           mix_w_out[0]]
    order = [0, 1, 4, 5, 6, 7, 2, 3]
    started, token = _push_start([big[i].astype(BF16) for i in order], False, "gather_start")
    started = {i: s for i, s in zip(order, started)}

    small = jnp.concatenate([c.reshape(8, 128), conv_w[0].reshape(12, 128), norm_g[0].reshape(3, 128),
                             jnp.zeros((1, 128), F32)], axis=0)
    small = _tie(small, token)
    small_all, = _all_gather([small], "gather_small")
    c_all = small_all[:, 0:8, :].reshape(NDEV, D)
    conv_full = small_all[:, 8:20, :].reshape(NDEV, 4, 384).transpose(1, 0, 2).reshape(4, 3 * D)
    norm_full = small_all[:, 20:23, :].reshape(NDEV, 3, 128).transpose(1, 0, 2).reshape(3, D)

    ncol = ada_w.shape[2]
    ada_b_mine = lax.dynamic_slice(ada_b, (0, me * ncol), (1, ncol))
    mod_cols = _ada_fwd(c_all, ada_w[0], ada_b_mine, "ada_fwd")
    mod_all, = _all_gather([mod_cols], "gather_mod")
    mod = lax.dynamic_index_in_dim(mod_all, me, axis=1, keepdims=False).reshape(9, D)
    shift = [mod[3 * s:3 * s + 1] for s in range(3)]
    scale = [mod[3 * s + 1:3 * s + 2] for s in range(3)]
    gate = [mod[3 * s + 2:3 * s + 3] for s in range(3)]
    ng = [norm_full[s:s + 1] for s in range(3)]
    fg = final_g.reshape(1, D)
    al_row = _lane_row(a_log[0])
    dt_row = _lane_row(dt_bias[0])
    gn = dn_norm_g
    pw = pool_w[0]
    ps = pool_scale

    x1, saved1, w_in1, w_out1 = _ffn_fwd(x0, ng[0], shift[0], scale[0], gate[0], [started[0], started[1]], "ffn1")

    h1 = _norm_mod_fwd(x1, ng[1], shift[1], scale[1], "mix_norm")
    seg = _push_wait([started[i] for i in (4, 5, 6, 7)], False, h1, "mix_gather_wait")
    w_mix = _mix_pad(_cols_from_blocks(seg[0]))
    w_pp = _cols_from_blocks(seg[1])
    w_dn = seg[2].reshape(D, D)
    w_mo = seg[3].reshape(D, D)
    proj = _matmul(h1, w_mix, out_dtype=F32, name="mix_in")
    ya = _pool_fwd(proj, pw, ps, w_pp, "pool_fwd")
    qh, kh, vh, bg = _dn_pre_fwd(proj, conv_full, al_row, dt_row, "dn_pre")
    u, w, qk, qd, kd, eg, inv = _dn_local_fwd(qh, kh, vh, bg, "dn_local")
    o, s_saved = _dn_scan_fwd(u, w, qk, qd, kd, eg, "dn_scan")
    ob = _dn_post_fwd(o, proj, gn, "dn_post")
    yb = _matmul(ob, w_dn, out_dtype=F32, name="dn_out")
    merged = _merge_fwd(ya, yb, proj, "merge")
    mix_y = _matmul(merged, w_mo, out_dtype=F32, name="mix_out")
    x2 = _resid_fwd(x1, mix_y, gate[1], 1.0, "mix_res")

    x3, saved2, w_in2, w_out2 = _ffn_fwd(x2, ng[2], shift[2], scale[2], gate[2], [started[2], started[3]], "ffn2")
    loss_row, dx3, dfg = _final_loss(x3, fg, target, "loss")

    dx2, dmod2, dng2, sent2 = _ffn_bwd(dx3, x2, ng[2], scale[2], gate[2], w_in2, w_out2, saved2, "ffn2")

    dmy, dgate1 = _resid_bwd(dx2, mix_y, gate[1], 1.0, "mix_res_bwd")
    dmerged = _matmul(dmy, w_mo, tb=True, out_dtype=F32, name="mix_out_dx")
    dw_mo = _matmul(merged, dmy, ta=True, out_dtype=BF16, name="mix_out_dw")
    dya, dyb, dgp, dgd = _merge_bwd(dmerged, ya, yb, proj, "merge_bwd")
    dob = _matmul(dyb, w_dn, tb=True, out_dtype=F32, name="dn_out_dx")
    dw_dn = _matmul(ob, dyb, ta=True, out_dtype=BF16, name="dn_out_dw")
    do, dz, dgn = _dn_post_bwd(o, proj, gn, dob, "dn_post_bwd")
    du, dw, dqk, dqd, dkd, deg = _dn_scan_bwd(u, w, qk, qd, kd, eg, s_saved, do, "dn_scan_bwd")
    dqh, dkh, dvh, dbg = _dn_local_bwd(qh, kh, vh, bg, inv, du, dw, dqk, dqd, dkd, deg, "dn_local_bwd")
    dconv, draw, dal, ddt = _dn_pre_bwd_act(proj, conv_full, al_row, dt_row, dqh, dkh, dvh, dbg, "dn_pre_bwd_act")
    dqkv, dcw = _dn_pre_bwd_conv(proj, conv_full, dconv, "dn_pre_bwd_conv")
    dwin, dpl, dpw, dps, dpp = _pool_bwd_local(proj, pw, ps, w_pp, dya, "pool_bwd_local")
    dxp = _pool_bwd_window(dwin, dpl, "pool_bwd_window")
    dproj = jnp.concatenate([dqkv, dz, dgp, dgd, dxp, draw, jnp.zeros((t, MIXP - OFF_BA - 128), BF16)], axis=1)
    dw_mix = _matmul(h1, dproj, ta=True, out_dtype=BF16, name="mix_in_dw")
    sent1, token = _push_start(
        [_cols_to_blocks(_mix_unpad(dw_mix)), _cols_to_blocks(dpp.astype(BF16)), dw_dn.reshape(NDEV, -1, D),
         dw_mo.reshape(NDEV, -1, D)], True, "mix_grad_start")
    dh1 = _matmul(_tie(dproj, token), w_mix, tb=True, out_dtype=F32, name="mix_in_dx")
    dx1, dsh1, dsc1, dng1 = _norm_mod_bwd(x1, ng[1], scale[1], dh1, dx2, "mix_norm_bwd")

    dx0, dmod0, dng0, sent0 = _ffn_bwd(dx1, x0, ng[0], scale[0], gate[0], w_in1, w_out1, saved1, "ffn1")

    dmod = jnp.concatenate([*dmod0, dsh1, dsc1, dgate1, *dmod2], axis=1).reshape(-1)
    flat = jnp.concatenate([
        dmod, dal[0, NH:2 * NH], ddt[0, NH:2 * NH], dgn.reshape(-1), dps.reshape(-1), dfg.reshape(-1),
        dpw.reshape(-1), jnp.concatenate([dng0, dng1, dng2], axis=0).reshape(-1), dcw.reshape(-1)])
    nflat = 90 * D
    flat = jnp.concatenate([flat, jnp.zeros((nflat - flat.shape[0],), F32)]).reshape(90, D)
    flat_all, = _all_gather([flat], "gather_small_grads")
    tot = _sum_devices(flat_all, F32, "sum_small_grads").reshape(-1)
    dmod_all = flat_all.reshape(NDEV, nflat)[:, :9 * D]
    dmod_cols = lax.dynamic_slice(dmod_all, (0, me * ncol), (NDEV, ncol))
    g_ada_w = _ada_bwd(c_all.T, dmod_cols, "ada_bwd")

    p = 0
    pieces = {}
    for nm, size in (("ada_b", 9 * D), ("a_log", NH), ("dt_bias", NH), ("dn_norm_g", HD), ("pool_scale", PW),
                     ("final_g", D), ("pool_w", 4 * PG * PG), ("norm_g", 3 * D), ("conv_w", 12 * D)):
        pieces[nm] = tot[p:p + size]
        p += size
    g_norm = lax.dynamic_slice(pieces["norm_g"].reshape(3, D), (0, me * 128), (3, 128))
    g_conv = lax.dynamic_slice(pieces["conv_w"].reshape(4, 3 * D), (0, me * 384), (4, 384))

    grads = {
        "ada_w": g_ada_w.reshape(ada_w.shape), "ada_b": pieces["ada_b"].reshape(ada_b.shape),
        "norm_g": g_norm.reshape(norm_g.shape), "conv_w": g_conv.reshape(conv_w.shape),
        "a_log": pieces["a_log"].reshape(a_log.shape), "dt_bias": pieces["dt_bias"].reshape(dt_bias.shape),
        "dn_norm_g": pieces["dn_norm_g"].reshape(dn_norm_g.shape), "pool_w": pieces["pool_w"].reshape(pool_w.shape),
        "pool_scale": pieces["pool_scale"].reshape(pool_scale.shape),
        "final_g": pieces["final_g"].reshape(final_g.shape),
    }
    weights = {"ada_w": ada_w, "ada_b": ada_b, "norm_g": norm_g, "ffn1_w_in": ffn1_w_in, "ffn1_w_out": ffn1_w_out,
               "ffn2_w_in": ffn2_w_in, "ffn2_w_out": ffn2_w_out, "mix_w_in": mix_w_in, "conv_w": conv_w,
               "a_log": a_log, "dt_bias": dt_bias, "dn_norm_g": dn_norm_g, "pool_w": pool_w,
               "pool_scale": pool_scale, "pool_proj": pool_proj, "dn_proj": dn_proj, "mix_w_out": mix_w_out,
               "final_g": final_g}
    m_in = {"ada_w": m_ada_w, "ada_b": m_ada_b, "norm_g": m_norm_g, "ffn1_w_in": m_ffn1_w_in,
            "ffn1_w_out": m_ffn1_w_out, "ffn2_w_in": m_ffn2_w_in, "ffn2_w_out": m_ffn2_w_out,
            "mix_w_in": m_mix_w_in, "conv_w": m_conv_w, "a_log": m_a_log, "dt_bias": m_dt_bias,
            "dn_norm_g": m_dn_norm_g, "pool_w": m_pool_w, "pool_scale": m_pool_scale, "pool_proj": m_pool_proj,
            "dn_proj": m_dn_proj, "mix_w_out": m_mix_w_out, "final_g": m_final_g}
    v_in = {"ada_w": v_ada_w, "ada_b": v_ada_b, "norm_g": v_norm_g, "ffn1_w_in": v_ffn1_w_in,
            "ffn1_w_out": v_ffn1_w_out, "ffn2_w_in": v_ffn2_w_in, "ffn2_w_out": v_ffn2_w_out,
            "mix_w_in": v_mix_w_in, "conv_w": v_conv_w, "a_log": v_a_log, "dt_bias": v_dt_bias,
            "dn_norm_g": v_dn_norm_g, "pool_w": v_pool_w, "pool_scale": v_pool_scale, "pool_proj": v_pool_proj,
            "dn_proj": v_dn_proj, "mix_w_out": v_mix_w_out, "final_g": v_final_g}

    names = list(weights)
    large = ("ada_w", "ffn1_w_in", "ffn1_w_out", "ffn2_w_in", "ffn2_w_out", "mix_w_in", "pool_proj", "dn_proj",
             "mix_w_out")
    delta, new_m, new_v = {}, {}, {}

    def update(nm):
        shp = weights[nm].shape
        two_d = (shp[-2], shp[-1])
        d_, m_, v_ = _adamw(weights[nm].reshape(two_d), grads[nm].reshape(two_d), m_in[nm].reshape(two_d),
                            v_in[nm].reshape(two_d), f"adamw_{nm}")
        delta[nm], new_m[nm], new_v[nm] = d_.reshape(shp), m_.reshape(shp), v_.reshape(shp)
        return d_

    def reduce(sent, group, after, tag):
        for nm, r in zip(group, _push_wait(sent, True, after, f"{tag}_grad_wait")):
            grads[nm] = _sum_devices(r, F32, f"sum_grads_{nm}").reshape(weights[nm].shape)

    done = update("ada_w")
    reduce(sent2, ("ffn2_w_in", "ffn2_w_out"), done, "ffn2")
    update("ffn2_w_in")
    done = update("ffn2_w_out")
    reduce(sent1, ("mix_w_in", "pool_proj", "dn_proj", "mix_w_out"), done, "mix")
    for nm in ("mix_w_in", "pool_proj", "dn_proj", "mix_w_out"):
        done = update(nm)
    reduce(sent0, ("ffn1_w_in", "ffn1_w_out"), done, "ffn1")
    update("ffn1_w_in")
    update("ffn1_w_out")
    rest = [nm for nm in names if nm not in large]
    total = sum(weights[nm].size for nm in rest)
    padded = -(-total // D) * D

    def pack(tree, fill):
        flat_ = jnp.concatenate([tree[nm].reshape(-1) for nm in rest])
        return jnp.concatenate([flat_, jnp.full((padded - total,), fill, F32)]).reshape(-1, D)

    d_, m_, v_ = _adamw(pack(weights, 0.0), pack(grads, 0.0), pack(m_in, 0.0), pack(v_in, 1.0), "adamw_small")
    p = 0
    for nm in rest:
        size = weights[nm].size
        shp = weights[nm].shape
        delta[nm] = d_.reshape(-1)[p:p + size].reshape(shp)
        new_m[nm] = m_.reshape(-1)[p:p + size].reshape(shp)
        new_v[nm] = v_.reshape(-1)[p:p + size].reshape(shp)
        p += size

    loss = lax.psum(loss_row[0, 0], ("x", "y", "c"))
    grad_x = dx0.reshape(x.shape)
    return (loss, grad_x, *[grads[nm] for nm in names], *[delta[nm] for nm in names],
            *[new_m[nm] for nm in names], *[new_v[nm] for nm in names])
```

```python
import functools

import jax
import jax.numpy as jnp
from jax import lax
from jax.experimental import pallas as pl
from jax.experimental.pallas import tpu as pltpu

F32 = jnp.float32
BF16 = jnp.bfloat16
SDS = jax.ShapeDtypeStruct
HI = lax.Precision.HIGHEST

D = 1024
FH = 2816
FB = 704
NH = 8
HD = 128
CH = 64
NDEV = 8
PW = 512
PG = 128
RMS_EPS = 1e-6
L2_EPS = 1e-6
TR = 256
HALO = 16
VMEM_LIMIT = 56 * 1024 * 1024

MIXP = 6912
OFF_Q, OFF_K, OFF_V, OFF_Z, OFF_GP, OFF_GD, OFF_XP, OFF_BA = 0, 1024, 2048, 3072, 4096, 5120, 6144, 6656
MIX_RAW = 6672

ADAM_LR = 0.001
ADAM_B1 = 0.9
ADAM_B2 = 0.999
ADAM_EPS = 1e-08
ADAM_WD = 0.01
ADAM_STEP = 10

NN = (((1,), (0,)), ((), ()))
NT = (((1,), (1,)), ((), ()))
TN = (((0,), (0,)), ((), ()))


def _dg(a, b, dims, prec=None):
    return lax.dot_general(a, b, dims, precision=prec, preferred_element_type=F32)


def _make_dots(prec):
    @jax.custom_vjp
    def nn(a, b):
        return _dg(a, b, NN, prec)

    @jax.custom_vjp
    def nt(a, b):
        return _dg(a, b, NT, prec)

    @jax.custom_vjp
    def tn(a, b):
        return _dg(a, b, TN, prec)

    nn.defvjp(lambda a, b: (nn(a, b), (a, b)), lambda r, d: (nt(d, r[1]), tn(r[0], d)))
    nt.defvjp(lambda a, b: (nt(a, b), (a, b)), lambda r, d: (nn(d, r[1]), tn(d, r[0])))
    tn.defvjp(lambda a, b: (tn(a, b), (a, b)), lambda r, d: (nt(r[1], d), nn(r[0], d)))
    return nn, nt, tn


_nn, _nt, _tn = _make_dots(None)


def _params(sem):
    return pltpu.CompilerParams(dimension_semantics=sem, vmem_limit_bytes=VMEM_LIMIT)


def _sigmoid(x):
    return 1.0 / (1.0 + jnp.exp(-x))


def _silu(x):
    return x * _sigmoid(x)


def _dsilu(x):
    s = _sigmoid(x)
    return s * (1.0 + x * (1.0 - s))


def _pick(n, cands):
    for c in cands:
        if n % c == 0:
            return c
    raise ValueError(f"no tile for {n}")


def _iota(shape, dim):
    return lax.broadcasted_iota(jnp.int32, shape, dim)


def _matmul(a, b, *, ta=False, tb=False, a_blk=False, b_blk=False, o_blk=False, tm=None, tn=None, tk=None,
            out_dtype, name, after=None):
    if a_blk:
        nb, r, cb = a.shape
        if ta:
            k_dim, m_dim, tm = r, nb * cb, cb
        else:
            m_dim, k_dim, tk = r, nb * cb, cb
    else:
        k_dim, m_dim = a.shape if ta else a.shape[::-1]
    if b_blk:
        nb, r, cb = b.shape
        if tb:
            n_dim, tk = r, cb
            assert nb * cb == k_dim
        else:
            n_dim, tn = nb * cb, cb
            assert r == k_dim
    else:
        n_dim = b.shape[0] if tb else b.shape[1]
    tm = tm or _pick(m_dim, (1024, 512, 256, 128))
    tn = tn or _pick(n_dim, (1024, 768, 512, 256, 128))
    tk = tk or (k_dim if (k_dim <= 2816 and not ta) else _pick(k_dim, (2816, 2304, 1024, 512, 256)))
    nk = k_dim // tk
    dims = ((((0,) if ta else (1,)), ((1,) if tb else (0,))), ((), ()))

    def body(a_ref, b_ref, *rest):
        o_ref, acc_ref = rest[-2:]
        k = pl.program_id(2)

        @pl.when(k == 0)
        def _():
            acc_ref[...] = jnp.zeros_like(acc_ref)

        acc_ref[...] += lax.dot_general(a_ref[...].astype(BF16), b_ref[...].astype(BF16), dims,
                                        preferred_element_type=F32)

        @pl.when(k == nk - 1)
        def _():
            o_ref[...] = acc_ref[...].astype(o_ref.dtype)

    if a_blk:
        a_spec = (pl.BlockSpec((None, tk, tm), lambda i, j, k: (i, k, 0)) if ta
                  else pl.BlockSpec((None, tm, tk), lambda i, j, k: (k, i, 0)))
    else:
        a_spec = (pl.BlockSpec((tk, tm), lambda i, j, k: (k, i)) if ta
                  else pl.BlockSpec((tm, tk), lambda i, j, k: (i, k)))
    if b_blk:
        b_spec = (pl.BlockSpec((None, tn, tk), lambda i, j, k: (k, j, 0)) if tb
                  else pl.BlockSpec((None, tk, tn), lambda i, j, k: (j, k, 0)))
    else:
        b_spec = (pl.BlockSpec((tn, tk), lambda i, j, k: (j, k)) if tb
                  else pl.BlockSpec((tk, tn), lambda i, j, k: (k, j)))
    if o_blk:
        o_spec = pl.BlockSpec((None, tm, tn), lambda i, j, k: (j, i, 0))
        o_shape = SDS((n_dim // tn, m_dim, tn), out_dtype)
    else:
        o_spec = pl.BlockSpec((tm, tn), lambda i, j, k: (i, j))
        o_shape = SDS((m_dim, n_dim), out_dtype)
    return pl.pallas_call(
        body, grid=(m_dim // tm, n_dim // tn, nk),
        in_specs=[a_spec, b_spec] + ([] if after is None else [pl.BlockSpec(memory_space=pl.ANY)]),
        out_specs=o_spec,
        out_shape=o_shape,
        scratch_shapes=[pltpu.VMEM((tm, tn), F32)],
        compiler_params=_params(("parallel", "parallel", "arbitrary")),
        name=name,
    )(a, b, *([] if after is None else [after]))


def _row(width, col=0):
    return pl.BlockSpec((TR, width), lambda i: (i, col))


def _vec(width):
    return pl.BlockSpec((1, width), lambda i: (0, 0))


def _norm_mod_fwd(x, g, shift, scale, name):
    t = x.shape[0]

    def body(x_ref, g_ref, sh_ref, sc_ref, o_ref):
        xv = x_ref[...]
        r = lax.rsqrt(jnp.mean(xv * xv, axis=-1, keepdims=True) + RMS_EPS)
        o_ref[...] = (((xv * r) * g_ref[...]) * (1.0 + sc_ref[...]) + sh_ref[...]).astype(o_ref.dtype)

    return pl.pallas_call(
        body, grid=(t // TR,), in_specs=[_row(D), _vec(D), _vec(D), _vec(D)], out_specs=_row(D),
        out_shape=SDS((t, D), BF16), compiler_params=_params(("parallel",)), name=name,
    )(x, g, shift, scale)


def _norm_mod_bwd(x, g, scale, dh, dx_in, name):
    t = x.shape[0]

    def body(x_ref, g_ref, sc_ref, dh_ref, dxi_ref, dx_ref, dsh_ref, dsc_ref, dg_ref):
        @pl.when(pl.program_id(0) == 0)
        def _():
            dsh_ref[...] = jnp.zeros_like(dsh_ref)
            dsc_ref[...] = jnp.zeros_like(dsc_ref)
            dg_ref[...] = jnp.zeros_like(dg_ref)

        xv = x_ref[...]
        gv = g_ref[...]
        dh = dh_ref[...]
        r = lax.rsqrt(jnp.mean(xv * xv, axis=-1, keepdims=True) + RMS_EPS)
        n = xv * r
        dsh_ref[...] += jnp.sum(dh, axis=0, keepdims=True)
        dsc_ref[...] += jnp.sum(dh * (n * gv), axis=0, keepdims=True)
        tt = dh * (1.0 + sc_ref[...])
        dg_ref[...] += jnp.sum(tt * n, axis=0, keepdims=True)
        dn = tt * gv
        dx_ref[...] = dxi_ref[...] + r * (dn - n * jnp.mean(dn * n, axis=-1, keepdims=True))

    return pl.pallas_call(
        body, grid=(t // TR,), in_specs=[_row(D), _vec(D), _vec(D), _row(D), _row(D)],
        out_specs=[_row(D), _vec(D), _vec(D), _vec(D)],
        out_shape=[SDS((t, D), F32), SDS((1, D), F32), SDS((1, D), F32), SDS((1, D), F32)],
        compiler_params=_params(("arbitrary",)), name=name,
    )(x, g, scale, dh, dx_in)


def _swiglu_specs():
    pair = pl.BlockSpec((2, None, TR, FB), lambda i, j: (0, j, i, 0))
    one = pl.BlockSpec((None, TR, FB), lambda i, j: (j, i, 0))
    return pair, one


def _swiglu_fwd(u, name):
    t = u.shape[2]

    def body(u_ref, o_ref):
        o_ref[...] = (_silu(u_ref[0]) * u_ref[1]).astype(o_ref.dtype)

    pair, one = _swiglu_specs()
    return pl.pallas_call(
        body, grid=(t // TR, NDEV // 2), in_specs=[pair], out_specs=one,
        out_shape=SDS((NDEV // 2, t, FB), BF16), compiler_params=_params(("parallel", "parallel")), name=name,
    )(u)


def _swiglu_bwd(u, da, name):
    t = u.shape[2]

    def body(u_ref, da_ref, o_ref):
        gv = u_ref[0]
        dav = da_ref[...]
        o_ref[0] = (dav * u_ref[1] * _dsilu(gv)).astype(o_ref.dtype)
        o_ref[1] = (dav * _silu(gv)).astype(o_ref.dtype)

    pair, one = _swiglu_specs()
    return pl.pallas_call(
        body, grid=(t // TR, NDEV // 2), in_specs=[pair, one], out_specs=pair,
        out_shape=SDS((2, NDEV // 2, t, FB), BF16), compiler_params=_params(("parallel", "parallel")), name=name,
    )(u, da)


def _resid_fwd(x, y, gate, coef, name):
    t = x.shape[0]

    def body(x_ref, y_ref, g_ref, o_ref):
        o_ref[...] = x_ref[...] + (coef * g_ref[...]) * y_ref[...]

    return pl.pallas_call(
        body, grid=(t // TR,), in_specs=[_row(D), _row(D), _vec(D)], out_specs=_row(D),
        out_shape=SDS((t, D), F32), compiler_params=_params(("parallel",)), name=name,
    )(x, y, gate)


def _resid_bwd(dx, y, gate, coef, name):
    t = dx.shape[0]

    def body(dx_ref, y_ref, g_ref, dy_ref, dg_ref):
        @pl.when(pl.program_id(0) == 0)
        def _():
            dg_ref[...] = jnp.zeros_like(dg_ref)

        dxv = dx_ref[...]
        dy_ref[...] = ((coef * g_ref[...]) * dxv).astype(dy_ref.dtype)
        dg_ref[...] += jnp.sum((coef * dxv) * y_ref[...], axis=0, keepdims=True)

    return pl.pallas_call(
        body, grid=(t // TR,), in_specs=[_row(D), _row(D), _vec(D)], out_specs=[_row(D), _vec(D)],
        out_shape=[SDS((t, D), BF16), SDS((1, D), F32)],
        compiler_params=_params(("arbitrary",)), name=name,
    )(dx, y, gate)


def _final_loss(x, fg, target, name):
    t = x.shape[0]
    nt = t // TR

    def body(x_ref, g_ref, t_ref, loss_ref, dx_ref, dg_ref, acc_ref):
        i = pl.program_id(0)

        @pl.when(i == 0)
        def _():
            acc_ref[...] = jnp.zeros_like(acc_ref)
            dg_ref[...] = jnp.zeros_like(dg_ref)

        xv = x_ref[...]
        gv = g_ref[...]
        r = lax.rsqrt(jnp.mean(xv * xv, axis=-1, keepdims=True) + RMS_EPS)
        n = xv * r
        err = n * gv - t_ref[...]
        acc_ref[...] += jnp.sum(err * err, axis=0, keepdims=True)
        dy = err * (1.0 / D)
        dg_ref[...] += jnp.sum(dy * n, axis=0, keepdims=True)
        dn = dy * gv
        dx_ref[...] = r * (dn - n * jnp.mean(dn * n, axis=-1, keepdims=True))

        @pl.when(i == nt - 1)
        def _():
            tot = jnp.sum(acc_ref[...], axis=1, keepdims=True) * (0.5 / D)
            loss_ref[...] = jnp.broadcast_to(tot, loss_ref.shape)

    return pl.pallas_call(
        body, grid=(nt,), in_specs=[_row(D), _vec(D), _row(D)],
        out_specs=[_vec(128), _row(D), _vec(D)],
        out_shape=[SDS((1, 128), F32), SDS((t, D), F32), SDS((1, D), F32)],
        scratch_shapes=[pltpu.VMEM((1, D), F32)],
        compiler_params=_params(("arbitrary",)), name=name,
    )(x, fg, target)


def _halo_prev(width, col):
    per = TR // HALO
    return pl.BlockSpec((HALO, width), lambda i: (jnp.maximum(i * per - 1, 0), col))


def _halo_next(width, col, nt):
    per = TR // HALO
    return pl.BlockSpec((HALO, width), lambda i: (jnp.minimum((i + 1) * per, nt * per - 1), col))


def _pool_windows(ext, tile_index):
    rows = _iota((TR, PG), 0) + tile_index * TR + 1
    pooled, counts = [], []
    for gi in range(4):
        w = 2 << gi
        e = ext[:, gi * PG:(gi + 1) * PG]
        s = e
        step = 1
        while step < w:
            s = s + pltpu.roll(s, step, 0)
            step *= 2
        cnt = jnp.minimum(rows, w).astype(F32)
        pooled.append(s[HALO:] / cnt - e[HALO:])
        counts.append(cnt)
    return pooled, counts


def _pool_fwd(proj, pool_w, pool_scale, pool_proj, name):
    t = proj.shape[0]
    xcol = OFF_XP // PW

    def body(x_ref, h_ref, pw_ref, ps_ref, pp_ref, o_ref):
        i = pl.program_id(0)
        halo = jnp.where(i > 0, h_ref[...], 0.0)
        ext = jnp.concatenate([halo, x_ref[...]], axis=0)
        pooled, _ = _pool_windows(ext, i)
        mixed = [_dg(pooled[g].astype(BF16), pw_ref[g].astype(BF16), NN) for g in range(4)]
        ypre = jnp.concatenate(mixed, axis=1) * ps_ref[...]
        o_ref[...] = _dg(ypre.astype(BF16), pp_ref[...], NN)

    return pl.pallas_call(
        body, grid=(t // TR,),
        in_specs=[_row(PW, xcol), _halo_prev(PW, xcol),
                  pl.BlockSpec((4, PG, PG), lambda i: (0, 0, 0)), _vec(PW),
                  pl.BlockSpec((PW, D), lambda i: (0, 0))],
        out_specs=_row(D), out_shape=SDS((t, D), F32),
        compiler_params=_params(("parallel",)), name=name,
    )(proj, proj, pool_w, pool_scale, pool_proj)


def _pool_bwd_local(proj, pool_w, pool_scale, pool_proj, dya, name):
    t = proj.shape[0]
    xcol = OFF_XP // PW

    def body(x_ref, h_ref, pw_ref, ps_ref, pp_ref, dya_ref, dwin_ref, dpl_ref, dpw_ref, dps_ref, dpp_ref):
        i = pl.program_id(0)

        @pl.when(i == 0)
        def _():
            dpw_ref[...] = jnp.zeros_like(dpw_ref)
            dps_ref[...] = jnp.zeros_like(dps_ref)
            dpp_ref[...] = jnp.zeros_like(dpp_ref)

        halo = jnp.where(i > 0, h_ref[...], 0.0)
        ext = jnp.concatenate([halo, x_ref[...]], axis=0)
        pooled, counts = _pool_windows(ext, i)
        mixed = jnp.concatenate(
            [_dg(pooled[g].astype(BF16), pw_ref[g].astype(BF16), NN) for g in range(4)], axis=1)
        ps = ps_ref[...]
        ypre = mixed * ps
        dyab = dya_ref[...].astype(BF16)
        dypre = _dg(dyab, pp_ref[...], NT)
        dpp_ref[...] += _dg(ypre.astype(BF16), dyab, TN)
        dps_ref[...] += jnp.sum(dypre * mixed, axis=0, keepdims=True)
        dmixed = dypre * ps
        for g in range(4):
            dm = dmixed[:, g * PG:(g + 1) * PG].astype(BF16)
            dpw_ref[g] += _dg(pooled[g].astype(BF16), dm, TN)
            dpooled = _dg(dm, pw_ref[g].astype(BF16), NT)
            dwin_ref[:, g * PG:(g + 1) * PG] = dpooled / counts[g]
            dpl_ref[:, g * PG:(g + 1) * PG] = dpooled

    return pl.pallas_call(
        body, grid=(t // TR,),
        in_specs=[_row(PW, xcol), _halo_prev(PW, xcol),
                  pl.BlockSpec((4, PG, PG), lambda i: (0, 0, 0)), _vec(PW),
                  pl.BlockSpec((PW, D), lambda i: (0, 0)), _row(D)],
        out_specs=[_row(PW), _row(PW), pl.BlockSpec((4, PG, PG), lambda i: (0, 0, 0)), _vec(PW),
                   pl.BlockSpec((PW, D), lambda i: (0, 0))],
        out_shape=[SDS((t, PW), F32), SDS((t, PW), F32), SDS((4, PG, PG), F32), SDS((1, PW), F32),
                   SDS((PW, D), F32)],
        compiler_params=_params(("arbitrary",)), name=name,
    )(proj, proj, pool_w, pool_scale, pool_proj, dya)


def _pool_bwd_window(dwin, dpl, name):
    t = dwin.shape[0]
    nt = t // TR
    ext_rows = TR + HALO

    def body(dw_ref, h_ref, dp_ref, o_ref):
        i = pl.program_id(0)
        halo = jnp.where(i < nt - 1, h_ref[...], 0.0)
        ext = jnp.concatenate([dw_ref[...], halo], axis=0)
        for gi in range(4):
            w = 2 << gi
            s = ext[:, gi * PG:(gi + 1) * PG]
            step = 1
            while step < w:
                s = s + pltpu.roll(s, ext_rows - step, 0)
                step *= 2
            o_ref[:, gi * PG:(gi + 1) * PG] = (s[:TR] - dp_ref[:, gi * PG:(gi + 1) * PG]).astype(o_ref.dtype)

    return pl.pallas_call(
        body, grid=(nt,), in_specs=[_row(PW), _halo_next(PW, 0, nt), _row(PW)], out_specs=_row(PW),
        out_shape=SDS((t, PW), BF16), compiler_params=_params(("parallel",)), name=name,
    )(dwin, dwin, dpl)


def _conv_group(ext, cw_ref, cols):
    acc = cw_ref[3:4, cols] * ext
    for j in range(3):
        acc = acc + cw_ref[j:j + 1, cols] * pltpu.roll(ext, 3 - j, 0)
    return acc[HALO:]


def _gate_terms(raw, al, dt):
    beta = _sigmoid(raw)
    xg = raw + dt
    sp = jnp.maximum(xg, 0.0) + jnp.log(1.0 + jnp.exp(-jnp.abs(xg)))
    g = -jnp.exp(al) * sp
    return beta, g, _sigmoid(xg)


def _dn_pre_fwd(proj, conv_w, al_row, dt_row, name):
    t = proj.shape[0]

    def body(x_ref, h_ref, cw_ref, ba_ref, al_ref, dt_ref, q_ref, k_ref, v_ref, bg_ref):
        i = pl.program_id(0)
        keep = i > 0
        for grp in range(24):
            cols = slice(grp * HD, (grp + 1) * HD)
            ext = jnp.concatenate([jnp.where(keep, h_ref[:, cols], 0.0), x_ref[:, cols]], axis=0)
            s = _silu(_conv_group(ext, cw_ref, cols))
            seg, head = divmod(grp, NH)
            hc = slice(head * HD, (head + 1) * HD)
            if seg == 0:
                q_ref[:, hc] = s * lax.rsqrt(jnp.sum(s * s, axis=-1, keepdims=True) + L2_EPS) * (HD ** -0.5)
            elif seg == 1:
                k_ref[:, hc] = s * lax.rsqrt(jnp.sum(s * s, axis=-1, keepdims=True) + L2_EPS)
            else:
                v_ref[:, hc] = s
        lane = _iota((TR, 128), 1)
        rowc = _iota((TR, 128), 0) % CH
        beta, g, _ = _gate_terms(ba_ref[...], al_ref[...], dt_ref[...])
        step = 1
        while step < CH:
            g = g + jnp.where(rowc >= step, pltpu.roll(g, step, 0), 0.0)
            step *= 2
        bg_ref[...] = jnp.where(lane < NH, beta, jnp.where(lane < 2 * NH, g, 0.0))

    return pl.pallas_call(
        body, grid=(t // TR,),
        in_specs=[_row(3 * D, 0), _halo_prev(3 * D, 0), pl.BlockSpec((4, 3 * D), lambda i: (0, 0)),
                  _row(128, OFF_BA // 128), _vec(128), _vec(128)],
        out_specs=[_row(D), _row(D), _row(D), _row(128)],
        out_shape=[SDS((t, D), F32), SDS((t, D), F32), SDS((t, D), F32), SDS((t, 128), F32)],
        compiler_params=_params(("parallel",)), name=name,
    )(proj, proj, conv_w, proj, al_row, dt_row)


def _dn_pre_bwd_act(proj, conv_w, al_row, dt_row, dq, dk, dv, dbg, name):
    t = proj.shape[0]

    def body(x_ref, h_ref, cw_ref, ba_ref, al_ref, dt_ref, dq_ref, dk_ref, dv_ref, dbg_ref,
             dc_ref, draw_ref, dal_ref, ddt_ref):
        i = pl.program_id(0)

        @pl.when(i == 0)
        def _():
            dal_ref[...] = jnp.zeros_like(dal_ref)
            ddt_ref[...] = jnp.zeros_like(ddt_ref)

        keep = i > 0
        for grp in range(24):
            cols = slice(grp * HD, (grp + 1) * HD)
            ext = jnp.concatenate([jnp.where(keep, h_ref[:, cols], 0.0), x_ref[:, cols]], axis=0)
            cv = _conv_group(ext, cw_ref, cols)
            seg, head = divmod(grp, NH)
            hc = slice(head * HD, (head + 1) * HD)
            if seg == 2:
                ds = dv_ref[:, hc]
            else:
                s = _silu(cv)
                r = lax.rsqrt(jnp.sum(s * s, axis=-1, keepdims=True) + L2_EPS)
                dy = dq_ref[:, hc] if seg == 0 else dk_ref[:, hc]
                c = (HD ** -0.5) if seg == 0 else 1.0
                ds = (c * r) * (dy - s * ((r * r) * jnp.sum(dy * s, axis=-1, keepdims=True)))
            dc_ref[:, cols] = ds * _dsilu(cv)
        lane = _iota((TR, 128), 1)
        rowc = _iota((TR, 128), 0) % CH
        isb = lane < NH
        isg = jnp.logical_and(lane >= NH, lane < 2 * NH)
        beta, g, sg = _gate_terms(ba_ref[...], al_ref[...], dt_ref[...])
        dbgv = dbg_ref[...]
        dg = dbgv
        step = 1
        while step < CH:
            dg = dg + jnp.where(rowc < CH - step, pltpu.roll(dg, TR - step, 0), 0.0)
            step *= 2
        da_raw = dg * (-jnp.exp(al_ref[...])) * sg
        draw_ref[...] = jnp.where(isb, dbgv * beta * (1.0 - beta), jnp.where(isg, da_raw, 0.0)).astype(draw_ref.dtype)
        dal_ref[...] += jnp.sum(jnp.where(isg, dg * g, 0.0), axis=0, keepdims=True)
        ddt_ref[...] += jnp.sum(jnp.where(isg, da_raw, 0.0), axis=0, keepdims=True)

    return pl.pallas_call(
        body, grid=(t // TR,),
        in_specs=[_row(3 * D, 0), _halo_prev(3 * D, 0), pl.BlockSpec((4, 3 * D), lambda i: (0, 0)),
                  _row(128, OFF_BA // 128), _vec(128), _vec(128), _row(D), _row(D), _row(D), _row(128)],
        out_specs=[_row(3 * D), _row(128), _vec(128), _vec(128)],
        out_shape=[SDS((t, 3 * D), F32), SDS((t, 128), BF16), SDS((1, 128), F32), SDS((1, 128), F32)],
        compiler_params=_params(("arbitrary",)), name=name,
    )(proj, proj, conv_w, proj, al_row, dt_row, dq, dk, dv, dbg)


def _dn_pre_bwd_conv(proj, conv_w, dconv, name):
    t = proj.shape[0]
    nt = t // TR
    ext_rows = TR + HALO

    def body(x_ref, h_ref, cw_ref, dc_ref, dn_ref, dx_ref, dcw_ref):
        i = pl.program_id(0)

        @pl.when(i == 0)
        def _():
            dcw_ref[...] = jnp.zeros_like(dcw_ref)

        keep_prev = i > 0
        keep_next = i < nt - 1
        for grp in range(24):
            cols = slice(grp * HD, (grp + 1) * HD)
            dct = dc_ref[:, cols]
            dext = jnp.concatenate([dct, jnp.where(keep_next, dn_ref[:, cols], 0.0)], axis=0)
            acc = cw_ref[3:4, cols] * dext
            for j in range(3):
                acc = acc + cw_ref[j:j + 1, cols] * pltpu.roll(dext, ext_rows - (3 - j), 0)
            dx_ref[:, cols] = acc[:TR].astype(dx_ref.dtype)
            xext = jnp.concatenate([jnp.where(keep_prev, h_ref[:, cols], 0.0), x_ref[:, cols]], axis=0)
            for j in range(4):
                xs = xext if j == 3 else pltpu.roll(xext, 3 - j, 0)
                dcw_ref[j:j + 1, cols] += jnp.sum(xs[HALO:] * dct, axis=0, keepdims=True)

    return pl.pallas_call(
        body, grid=(nt,),
        in_specs=[_row(3 * D, 0), _halo_prev(3 * D, 0), pl.BlockSpec((4, 3 * D), lambda i: (0, 0)),
                  _row(3 * D), _halo_next(3 * D, 0, nt)],
        out_specs=[_row(3 * D), pl.BlockSpec((4, 3 * D), lambda i: (0, 0))],
        out_shape=[SDS((t, 3 * D), BF16), SDS((4, 3 * D), F32)],
        compiler_params=_params(("arbitrary",)), name=name,
    )(proj, proj, conv_w, dconv, dconv)


def _dn_post_fwd(o, proj, gn, name):
    t = o.shape[0]

    def body(o_ref, z_ref, g_ref, out_ref):
        gv = g_ref[...]
        for h in range(NH):
            hc = slice(h * HD, (h + 1) * HD)
            ov = o_ref[:, hc]
            r = lax.rsqrt(jnp.mean(ov * ov, axis=-1, keepdims=True) + RMS_EPS)
            out_ref[:, hc] = (((ov * r) * gv) * _silu(z_ref[:, hc])).astype(out_ref.dtype)

    return pl.pallas_call(
        body, grid=(t // TR,), in_specs=[_row(D), _row(D, OFF_Z // D), _vec(HD)], out_specs=_row(D),
        out_shape=SDS((t, D), BF16), compiler_params=_params(("parallel",)), name=name,
    )(o, proj, gn)


def _dn_post_bwd(o, proj, gn, dob, name):
    t = o.shape[0]

    def body(o_ref, z_ref, g_ref, d_ref, do_ref, dz_ref, dg_ref):
        @pl.when(pl.program_id(0) == 0)
        def _():
            dg_ref[...] = jnp.zeros_like(dg_ref)

        gv = g_ref[...]
        acc = jnp.zeros((1, HD), F32)
        for h in range(NH):
            hc = slice(h * HD, (h + 1) * HD)
            ov = o_ref[:, hc]
            zv = z_ref[:, hc]
            dv = d_ref[:, hc]
            r = lax.rsqrt(jnp.mean(ov * ov, axis=-1, keepdims=True) + RMS_EPS)
            n = ov * r
            dz_ref[:, hc] = (dv * (n * gv) * _dsilu(zv)).astype(dz_ref.dtype)
            dng = dv * _silu(zv)
            acc = acc + jnp.sum(dng * n, axis=0, keepdims=True)
            dn = dng * gv
            do_ref[:, hc] = r * (dn - n * jnp.mean(dn * n, axis=-1, keepdims=True))
        dg_ref[...] += acc

    return pl.pallas_call(
        body, grid=(t // TR,), in_specs=[_row(D), _row(D, OFF_Z // D), _vec(HD), _row(D)],
        out_specs=[_row(D), _row(D), _vec(HD)],
        out_shape=[SDS((t, D), F32), SDS((t, D), BF16), SDS((1, HD), F32)],
        compiler_params=_params(("arbitrary",)), name=name,
    )(o, proj, gn, dob)


def _merge_fwd(ya, yb, proj, name):
    t = ya.shape[0]

    def body(a_ref, b_ref, gp_ref, gd_ref, o_ref):
        o_ref[...] = (_sigmoid(gp_ref[...]) * a_ref[...] + _sigmoid(gd_ref[...]) * b_ref[...]).astype(o_ref.dtype)

    return pl.pallas_call(
        body, grid=(t // TR,), in_specs=[_row(D), _row(D), _row(D, OFF_GP // D), _row(D, OFF_GD // D)],
        out_specs=_row(D), out_shape=SDS((t, D), BF16),
        compiler_params=_params(("parallel",)), name=name,
    )(ya, yb, proj, proj)


def _merge_bwd(dm, ya, yb, proj, name):
    t = ya.shape[0]

    def body(d_ref, a_ref, b_ref, gp_ref, gd_ref, da_ref, db_ref, dgp_ref, dgd_ref):
        dv = d_ref[...]
        sp = _sigmoid(gp_ref[...])
        sd = _sigmoid(gd_ref[...])
        da_ref[...] = dv * sp
        db_ref[...] = (dv * sd).astype(db_ref.dtype)
        dgp_ref[...] = (dv * a_ref[...] * sp * (1.0 - sp)).astype(dgp_ref.dtype)
        dgd_ref[...] = (dv * b_ref[...] * sd * (1.0 - sd)).astype(dgd_ref.dtype)

    return pl.pallas_call(
        body, grid=(t // TR,),
        in_specs=[_row(D), _row(D), _row(D), _row(D, OFF_GP // D), _row(D, OFF_GD // D)],
        out_specs=[_row(D)] * 4,
        out_shape=[SDS((t, D), F32), SDS((t, D), BF16), SDS((t, D), BF16), SDS((t, D), BF16)],
        compiler_params=_params(("parallel",)), name=name,
    )(dm, ya, yb, proj, proj)


def _split2(x):
    hi = x.astype(BF16)
    return hi, (x - hi.astype(F32)).astype(BF16)


def _dot3(a, b, dims):
    ah, al = _split2(a)
    bh, bl = _split2(b)
    return _dg(ah, bh, dims) + (_dg(ah, bl, dims) + _dg(al, bh, dims))


def _neumann_inverses(mats):
    ri = _iota((CH, CH), 0)
    ci = _iota((CH, CH), 1)
    eye = jnp.where(ri == ci, 1.0, 0.0).astype(F32)
    xs = [-a for a in mats]
    ps = [eye + x for x in xs]
    for _ in range(5):
        xs = [_dot3(x, x, NN) for x in xs]
        ps = [p + _dot3(p, x, NN) for p, x in zip(ps, xs)]
    return ps


def _solve_with(inv):
    @jax.custom_vjp
    def solve(a, rhs):
        return _dot3(inv, rhs, NN)

    def fwd(a, rhs):
        sol = _dot3(inv, rhs, NN)
        return sol, sol

    def bwd(sol, d):
        drhs = _dot3(inv, d, TN)
        return -_dot3(drhs, sol, NT), drhs

    solve.defvjp(fwd, bwd)
    return solve


@jax.custom_vjp
def _rows_to_lanes(g64):
    ri = _iota((CH, CH), 0)
    ci = _iota((CH, CH), 1)
    diag = jnp.where(ri == ci, g64, 0.0)
    ones = jnp.ones((CH, CH), BF16)
    hi = diag.astype(BF16)
    rem = diag - hi.astype(F32)
    mid = rem.astype(BF16)
    lo = (rem - mid.astype(F32)).astype(BF16)
    return _dg(ones, hi, NN) + (_dg(ones, mid, NN) + _dg(ones, lo, NN))


def _rows_to_lanes_bwd(_, d):
    ri = _iota((CH, CH), 0)
    ci = _iota((CH, CH), 1)
    return (jnp.where(ri == ci, jnp.broadcast_to(jnp.sum(d, axis=0, keepdims=True), (CH, CH)), 0.0),)


_rows_to_lanes.defvjp(lambda g64: (_rows_to_lanes(g64), None), _rows_to_lanes_bwd)


def _chunk_local(solve_all, q, k, v, g128, g64, gl128, b128, b64):
    ri = _iota((CH, CH), 0)
    ci = _iota((CH, CH), 1)
    causal = ri >= ci
    strict = ri > ci
    gj = [_rows_to_lanes(g) for g in g64]
    decay = [jnp.where(causal, jnp.exp(jnp.where(causal, g - t, 0.0)), 0.0) for g, t in zip(g64, gj)]
    kk = [_nt(x, x) for x in k]
    a = [jnp.where(strict, b * m * dc, 0.0) for b, m, dc in zip(b64, kk, decay)]
    eg = [jnp.exp(g) for g in g128]
    rhs = [jnp.concatenate([b * x, (b * e) * y], axis=1) for b, x, e, y in zip(b128, v, eg, k)]
    sol = solve_all(a, rhs)
    qk = [jnp.where(causal, _nt(x, y) * dc, 0.0) for x, y, dc in zip(q, k, decay)]
    return ([s[:, :HD] for s in sol], [s[:, HD:] for s in sol], qk, [x * e for x, e in zip(q, eg)],
            [x * jnp.exp(gl - g) for x, gl, g in zip(k, gl128, g128)], [jnp.exp(gl) for gl in gl128])


def _all_head_gates(bgv):
    return tuple(list(z) for z in zip(*[_head_gates(bgv, h) for h in range(NH)]))


def _head_gates(bgv, h):
    lane = _iota((CH, 128), 1)
    row = _iota((CH, 128), 0)
    bcol = jnp.sum(jnp.where(lane == h, bgv, 0.0), axis=1, keepdims=True)
    gcol = jnp.sum(jnp.where(lane == NH + h, bgv, 0.0), axis=1, keepdims=True)
    g128 = jnp.broadcast_to(gcol, (CH, 128))
    gl128 = jnp.broadcast_to(jnp.sum(jnp.where(row == CH - 1, g128, 0.0), axis=0, keepdims=True), (CH, 128))
    return (g128, jnp.broadcast_to(gcol, (CH, CH)), gl128,
            jnp.broadcast_to(bcol, (CH, 128)), jnp.broadcast_to(bcol, (CH, CH)))


def _chunk_specs():
    row = pl.BlockSpec((CH, D), lambda i: (i, 0))
    small = pl.BlockSpec((CH, 128), lambda i: (i, 0))
    qk = pl.BlockSpec((NH, CH, CH), lambda i: (i, 0, 0))
    eg = pl.BlockSpec((1, NH, 128), lambda i: (i, 0, 0))
    return row, small, qk, eg


def _dn_local_fwd(q, k, v, bg, name):
    t = q.shape[0]
    n = t // CH

    def body(q_ref, k_ref, v_ref, bg_ref, u_ref, w_ref, qk_ref, qd_ref, kd_ref, eg_ref, inv_ref):
        cols = [slice(h * HD, (h + 1) * HD) for h in range(NH)]

        def solve_all(mats, rhs):
            invs = _neumann_inverses(mats)
            for h in range(NH):
                inv_ref[h] = invs[h]
            return [_dot3(m, r, NN) for m, r in zip(invs, rhs)]

        u, w, qk, qd, kd, egl = _chunk_local(
            solve_all, [q_ref[:, c] for c in cols], [k_ref[:, c] for c in cols], [v_ref[:, c] for c in cols],
            *_all_head_gates(bg_ref[...]))
        for h, hc in enumerate(cols):
            u_ref[:, hc] = u[h]
            w_ref[:, hc] = w[h].astype(w_ref.dtype)
            qd_ref[:, hc] = qd[h].astype(qd_ref.dtype)
            kd_ref[:, hc] = kd[h].astype(kd_ref.dtype)
            qk_ref[h] = qk[h].astype(qk_ref.dtype)
            eg_ref[0, h:h + 1, :] = egl[h][0:1, :]

    row, small, qkb, egb = _chunk_specs()
    return pl.pallas_call(
        body, grid=(n,), in_specs=[row, row, row, small], out_specs=[row, row, qkb, row, row, egb, qkb],
        out_shape=[SDS((t, D), F32), SDS((t, D), BF16), SDS((n * NH, CH, CH), BF16), SDS((t, D), BF16),
                   SDS((t, D), BF16), SDS((n, NH, 128), F32), SDS((n * NH, CH, CH), F32)],
        compiler_params=_params(("parallel",)), name=name,
    )(q, k, v, bg)


def _dn_local_bwd(q, k, v, bg, inv, du, dw, dqk, dqd, dkd, deg, name):
    t = q.shape[0]
    n = t // CH

    def body(q_ref, k_ref, v_ref, bg_ref, inv_ref, du_ref, dw_ref, dqk_ref, dqd_ref, dkd_ref, deg_ref,
             dq_ref, dk_ref, dv_ref, dbg_ref):
        bgv = bg_ref[...]
        lane = _iota((CH, 128), 1)
        row = _iota((CH, 128), 0)
        first = jnp.where(row == 0, 1.0, 0.0)
        acc = jnp.zeros((CH, 128), F32)
        cols = [slice(h * HD, (h + 1) * HD) for h in range(NH)]
        solves = [_solve_with(inv_ref[h]) for h in range(NH)]

        def solve_all(mats, rhs):
            return [f(m, r) for f, m, r in zip(solves, mats, rhs)]

        _, vjp = jax.vjp(functools.partial(_chunk_local, solve_all),
                         [q_ref[:, c] for c in cols], [k_ref[:, c] for c in cols], [v_ref[:, c] for c in cols],
                         *_all_head_gates(bgv))
        cts = ([du_ref[:, c] for c in cols], [dw_ref[:, c] for c in cols], [dqk_ref[h] for h in range(NH)],
               [dqd_ref[:, c] for c in cols], [dkd_ref[:, c] for c in cols],
               [jnp.broadcast_to(deg_ref[0, h:h + 1, :], (CH, 128)) * first for h in range(NH)])
        dq, dk, dv, dg128, dg64, dgl, db128, db64 = vjp(cts)
        for h, hc in enumerate(cols):
            dq_ref[:, hc] = dq[h]
            dk_ref[:, hc] = dk[h]
            dv_ref[:, hc] = dv[h]
            dg = jnp.sum(dg128[h], axis=1, keepdims=True) + jnp.sum(dg64[h], axis=1, keepdims=True)
            tot = jnp.sum(jnp.sum(dgl[h], axis=0, keepdims=True), axis=1, keepdims=True)
            dg = dg + jnp.where(row[:, 0:1] == CH - 1, tot, 0.0)
            db = jnp.sum(db128[h], axis=1, keepdims=True) + jnp.sum(db64[h], axis=1, keepdims=True)
            acc = acc + jnp.where(lane == h, db, 0.0) + jnp.where(lane == NH + h, dg, 0.0)
        dbg_ref[...] = acc

    row, small, qkb, egb = _chunk_specs()
    return pl.pallas_call(
        body, grid=(n,), in_specs=[row, row, row, small, qkb, row, row, qkb, row, row, egb],
        out_specs=[row, row, row, small],
        out_shape=[SDS((t, D), F32)] * 3 + [SDS((t, 128), F32)],
        compiler_params=_params(("parallel",)), name=name,
    )(q, k, v, bg, inv, du, dw, dqk, dqd, dkd, deg)


def _state_step(s, u, w, qk, qd, kd, egl):
    ws = [_nn(a, b) for a, b in zip(w, s)]
    v_new = [a - b for a, b in zip(u, ws)]
    qs = [_nn(a, b) for a, b in zip(qd, s)]
    intra = [_nn(a, b) for a, b in zip(qk, v_new)]
    upd = [_tn(a, b) for a, b in zip(kd, v_new)]
    return [a * e + b for a, e, b in zip(s, egl, upd)], [a + b for a, b in zip(qs, intra)]


def _dn_scan_fwd(u, w, qk, qd, kd, eg, name):
    t = u.shape[0]
    n = t // CH

    def body(u_ref, w_ref, qk_ref, qd_ref, kd_ref, eg_ref, o_ref, save_ref, s_ref):
        @pl.when(pl.program_id(0) == 0)
        def _():
            s_ref[...] = jnp.zeros_like(s_ref)

        cols = [slice(h * HD, (h + 1) * HD) for h in range(NH)]
        s = [s_ref[h] for h in range(NH)]
        for h in range(NH):
            save_ref[0, h] = s[h]
        s_new, o = _state_step(
            s, [u_ref[:, c] for c in cols], [w_ref[:, c].astype(F32) for c in cols],
            [qk_ref[h].astype(F32) for h in range(NH)], [qd_ref[:, c].astype(F32) for c in cols],
            [kd_ref[:, c].astype(F32) for c in cols], [eg_ref[0, h:h + 1, :] for h in range(NH)])
        for h, hc in enumerate(cols):
            o_ref[:, hc] = o[h]
            s_ref[h] = s_new[h]

    row, _, qkb, egb = _chunk_specs()
    return pl.pallas_call(
        body, grid=(n,), in_specs=[row, row, qkb, row, row, egb],
        out_specs=[row, pl.BlockSpec((1, NH, HD, HD), lambda i: (i, 0, 0, 0))],
        out_shape=[SDS((t, D), F32), SDS((n, NH, HD, HD), F32)],
        scratch_shapes=[pltpu.VMEM((NH, HD, HD), F32)],
        compiler_params=_params(("arbitrary",)), name=name,
    )(u, w, qk, qd, kd, eg)


def _dn_scan_bwd(u, w, qk, qd, kd, eg, saved, do, name):
    t = u.shape[0]
    n = t // CH

    def body(u_ref, w_ref, qk_ref, qd_ref, kd_ref, eg_ref, sv_ref, do_ref,
             du_ref, dw_ref, dqk_ref, dqd_ref, dkd_ref, deg_ref, ds_ref):
        @pl.when(pl.program_id(0) == 0)
        def _():
            ds_ref[...] = jnp.zeros_like(ds_ref)

        cols = [slice(h * HD, (h + 1) * HD) for h in range(NH)]
        _, vjp = jax.vjp(
            _state_step, [sv_ref[0, h] for h in range(NH)], [u_ref[:, c] for c in cols],
            [w_ref[:, c].astype(F32) for c in cols], [qk_ref[h].astype(F32) for h in range(NH)],
            [qd_ref[:, c].astype(F32) for c in cols], [kd_ref[:, c].astype(F32) for c in cols],
            [eg_ref[0, h:h + 1, :] for h in range(NH)])
        ds, du, dw, dqk, dqd, dkd, deg = vjp(([ds_ref[h] for h in range(NH)], [do_ref[:, c] for c in cols]))
        for h, hc in enumerate(cols):
            ds_ref[h] = ds[h]
            du_ref[:, hc] = du[h]
            dw_ref[:, hc] = dw[h]
            dqk_ref[h] = dqk[h]
            dqd_ref[:, hc] = dqd[h]
            dkd_ref[:, hc] = dkd[h]
            deg_ref[0, h:h + 1, :] = deg[h]

    rev = lambda i: (n - 1 - i, 0)
    rev3 = lambda i: (n - 1 - i, 0, 0)
    row = pl.BlockSpec((CH, D), rev)
    qkb = pl.BlockSpec((NH, CH, CH), rev3)
    egb = pl.BlockSpec((1, NH, 128), rev3)
    return pl.pallas_call(
        body, grid=(n,),
        in_specs=[row, row, qkb, row, row, egb,
                  pl.BlockSpec((1, NH, HD, HD), lambda i: (n - 1 - i, 0, 0, 0)), row],
        out_specs=[row, row, qkb, row, row, egb],
        out_shape=[SDS((t, D), F32), SDS((t, D), F32), SDS((n * NH, CH, CH), F32), SDS((t, D), F32),
                   SDS((t, D), F32), SDS((n, NH, 128), F32)],
        scratch_shapes=[pltpu.VMEM((NH, HD, HD), F32)],
        compiler_params=_params(("arbitrary",)), name=name,
    )(u, w, qk, qd, kd, eg, saved, do)


def _ada_fwd(c_all, ada_w, ada_b, name):
    ncol = ada_w.shape[1]

    def body(c_ref, w_ref, b_ref, o_ref):
        o_ref[...] = _dg(_silu(c_ref[...]), w_ref[...], NN, HI) + b_ref[...]

    return pl.pallas_call(body, out_shape=SDS((NDEV, ncol), F32),
                          compiler_params=pltpu.CompilerParams(vmem_limit_bytes=VMEM_LIMIT), name=name,
                          )(c_all, ada_w, ada_b)


def _ada_bwd(c_all_t, dmod, name):
    ncol = dmod.shape[1]

    def body(c_ref, d_ref, o_ref):
        sc = _silu(c_ref[...])
        acc = sc[:, 0:1] * d_ref[0:1, :]
        for b in range(1, NDEV):
            acc = acc + sc[:, b:b + 1] * d_ref[b:b + 1, :]
        o_ref[...] = acc

    return pl.pallas_call(body, out_shape=SDS((D, ncol), F32),
                          compiler_params=pltpu.CompilerParams(vmem_limit_bytes=VMEM_LIMIT), name=name,
                          )(c_all_t, dmod)


def _sum_devices(parts, out_dtype, name):
    _, r, c = parts.shape
    tr = TR if r % TR == 0 else r

    def body(p_ref, o_ref):
        acc = p_ref[0].astype(F32)
        for i in range(1, NDEV):
            acc = acc + p_ref[i].astype(F32)
        o_ref[...] = acc.astype(o_ref.dtype)

    return pl.pallas_call(
        body, grid=(r // tr,), in_specs=[pl.BlockSpec((NDEV, tr, c), lambda i: (0, i, 0))],
        out_specs=pl.BlockSpec((tr, c), lambda i: (i, 0)), out_shape=SDS((r, c), out_dtype),
        compiler_params=_params(("parallel",)), name=name,
    )(parts)


def _adamw(w, g, m, v, name):
    r, c = w.shape
    tr = _pick(r, (256, 128, 88, 8)) if r % 8 == 0 else r
    bc1 = 1.0 - ADAM_B1 ** ADAM_STEP
    bc2 = 1.0 - ADAM_B2 ** ADAM_STEP

    def body(w_ref, g_ref, m_ref, v_ref, d_ref, nm_ref, nv_ref):
        gv = g_ref[...]
        m_new = ADAM_B1 * m_ref[...] + (1.0 - ADAM_B1) * gv
        v_new = ADAM_B2 * v_ref[...] + (1.0 - ADAM_B2) * (gv * gv)
        nm_ref[...] = m_new
        nv_ref[...] = v_new
        d_ref[...] = -ADAM_LR * ((m_new / bc1) / (jnp.sqrt(v_new / bc2) + ADAM_EPS) + ADAM_WD * w_ref[...])

    spec = pl.BlockSpec((tr, c), lambda i: (i, 0))
    return pl.pallas_call(
        body, grid=(r // tr,), in_specs=[spec] * 4, out_specs=[spec] * 3,
        out_shape=[SDS((r, c), F32)] * 3, compiler_params=_params(("parallel",)), name=name,
    )(w, g, m, v)


ANY = pl.BlockSpec(memory_space=pl.ANY)
MESH = pl.DeviceIdType.MESH


def _all_gather(xs, name, after=None):
    n = len(xs)
    extra = [] if after is None else [after]

    def body(*refs):
        x_refs, out_refs = refs[:n], refs[n + len(extra):2 * n + len(extra)]
        send_sems, recv_sems, local_sems = refs[-3:]
        mx, my, mc = lax.axis_index("x"), lax.axis_index("y"), lax.axis_index("c")
        me, sibling = (mx, my, mc), (mx, my, 1 - mc)
        chips = [(1 - mx, my), (mx, 1 - my), (1 - mx, 1 - my)]

        def rows(a, px, py, pc):
            return out_refs[a].at[4 * px + 2 * py + pc]

        def copy(a, k, block, to, src=None):
            return pltpu.make_async_remote_copy(
                src_ref=rows(a, *block) if src is None else src, dst_ref=rows(a, *block),
                send_sem=send_sems.at[a, k], recv_sem=recv_sems.at[a, k], device_id=to, device_id_type=MESH)

        mine = [pltpu.make_async_copy(x_refs[a], rows(a, *me), local_sems.at[a]) for a in range(n)]
        for cp in mine:
            cp.start()
        first = []
        for a in range(n):
            first.append(copy(a, 0, me, sibling, src=x_refs[a]))
            first += [copy(a, 1 + j, me, (*chip, mc), src=x_refs[a]) for j, chip in enumerate(chips)]
        for cp in first:
            cp.start()
        passed = []
        for a in range(n):
            for j, chip in enumerate(chips):
                copy(a, 1 + j, (*chip, mc), me).wait_recv()
                passed.append(copy(a, 4 + j, (*chip, mc), sibling))
                passed[-1].start()
        for a in range(n):
            copy(a, 0, sibling, me).wait_recv()
            for j, chip in enumerate(chips):
                copy(a, 4 + j, (*chip, 1 - mc), me).wait_recv()
        for cp in first + passed:
            cp.wait_send()
        for cp in mine:
            cp.wait()

    return pl.pallas_call(
        body, out_shape=[SDS((NDEV,) + x.shape, x.dtype) for x in xs], in_specs=[ANY] * (n + len(extra)),
        out_specs=[ANY] * n,
        scratch_shapes=[pltpu.SemaphoreType.DMA((n, 7)), pltpu.SemaphoreType.DMA((n, 7)),
                        pltpu.SemaphoreType.DMA((n,))],
        name=name,
    )(*xs, *extra)


HBM = pl.BlockSpec(memory_space=pltpu.HBM)
SEM = pl.BlockSpec(memory_space=pltpu.SEMAPHORE)
EFFECT = pltpu.SideEffectType.DATAFLOW_SIDE_EFFECTING


def _peers():
    mx, my, mc = lax.axis_index("x"), lax.axis_index("y"), lax.axis_index("c")
    out = []
    for k in range(1, NDEV):
        out.append((1 - mx if k & 4 else mx, 1 - my if k & 2 else my, 1 - mc if k & 1 else mc))
    return 4 * mx + 2 * my + mc, out


def _push_start(srcs, sliced, name):
    n = len(srcs)
    me_idx = 4 * lax.axis_index("x") + 2 * lax.axis_index("y") + lax.axis_index("c")
    lands = []
    for s in srcs:
        blk = lax.dynamic_index_in_dim(s, me_idx, 0, keepdims=True) if sliced else s[None]
        shape = s.shape if sliced else (NDEV,) + s.shape
        lands.append(lax.dynamic_update_slice(lax.empty(shape, s.dtype), blk, (me_idx,) + (0,) * (len(shape) - 1)))

    def body(*refs):
        src_refs, land_refs = refs[:n], refs[n:2 * n]
        send_sems, recv_sems = refs[2 * n:3 * n], refs[3 * n:4 * n]
        token = refs[-1]
        me, peers = _peers()
        for a in range(n):
            for k, (px, py, pc) in enumerate(peers):
                src = src_refs[a].at[4 * px + 2 * py + pc] if sliced else src_refs[a]
                pltpu.make_async_remote_copy(
                    src_ref=src, dst_ref=land_refs[a].at[me], send_sem=send_sems[a].at[k],
                    recv_sem=recv_sems[a].at[k], device_id=(px, py, pc), device_id_type=MESH).start()
        token[...] = jnp.zeros_like(token)

    outs = pl.pallas_call(
        body, name=name,
        out_shape=([pltpu.SemaphoreType.DMA((NDEV - 1,))] * (2 * n)
                   + [pltpu.HBM(s.shape, s.dtype) for s in srcs] + [pltpu.HBM(l.shape, l.dtype) for l in lands]
                   + [SDS((8, 128), F32)]),
        in_specs=[HBM] * (2 * n),
        out_specs=[SEM] * (2 * n) + [HBM] * (2 * n) + [pl.BlockSpec(memory_space=pltpu.VMEM)],
        input_output_aliases={i: 2 * n + i for i in range(2 * n)},
        compiler_params=pltpu.CompilerParams(has_side_effects=EFFECT),
    )(*[pltpu.with_memory_space_constraint(s, pltpu.HBM) for s in srcs],
      *[pltpu.with_memory_space_constraint(l, pltpu.HBM) for l in lands])
    sends, recvs = outs[:n], outs[n:2 * n]
    src_thru, land_thru = outs[2 * n:3 * n], outs[3 * n:4 * n]
    return [(sends[a], recvs[a], src_thru[a], land_thru[a]) for a in range(n)], outs[-1]


def _push_wait(started, sliced, after, name):
    n = len(started)

    def body(*refs):
        src_refs, land_refs = refs[:n], refs[n:2 * n]
        send_sems, recv_sems = refs[2 * n:3 * n], refs[3 * n:4 * n]
        me, peers = _peers()
        for a in range(n):
            for k, (px, py, pc) in enumerate(peers):
                src = src_refs[a].at[4 * px + 2 * py + pc] if sliced else src_refs[a]
                cp = pltpu.make_async_remote_copy(
                    src_ref=src, dst_ref=land_refs[a].at[me], send_sem=send_sems[a].at[k],
                    recv_sem=recv_sems[a].at[k], device_id=(px, py, pc), device_id_type=MESH)
                cp.wait_send()
                cp.wait_recv()

    srcs = [s[2] for s in started]
    lands = [s[3] for s in started]
    outs = pl.pallas_call(
        body, name=name,
        out_shape=[pltpu.HBM(s.shape, s.dtype) for s in srcs] + [pltpu.HBM(l.shape, l.dtype) for l in lands],
        in_specs=[HBM] * (2 * n) + [SEM] * (2 * n) + [pl.BlockSpec(memory_space=pl.ANY)],
        out_specs=[HBM] * (2 * n),
        input_output_aliases={i: i for i in range(2 * n)},
        compiler_params=pltpu.CompilerParams(has_side_effects=EFFECT),
    )(*srcs, *lands, *[s[0] for s in started], *[s[1] for s in started], after)
    return outs[n:]


def _cols_from_blocks(blocks):
    _, rows, w = blocks.shape
    return blocks.transpose(1, 0, 2).reshape(rows, NDEV * w)


def _cols_to_blocks(full):
    rows, total = full.shape
    return full.reshape(rows, NDEV, total // NDEV).transpose(1, 0, 2)


def _mix_pad(w):
    rows = w.shape[0]
    xp, q, k, v, z, b, a, gp, gd = jnp.split(w, (512, 1536, 2560, 3584, 4608, 4616, 4624, 5648), axis=1)
    pad = jnp.zeros((rows, MIXP - OFF_BA - 16), w.dtype)
    return jnp.concatenate([q, k, v, z, gp, gd, xp, b, a, pad], axis=1)


def _mix_unpad(w):
    q, k, v, z, gp, gd, xp, b, a = (w[:, OFF_Q:OFF_K], w[:, OFF_K:OFF_V], w[:, OFF_V:OFF_Z], w[:, OFF_Z:OFF_GP],
                                    w[:, OFF_GP:OFF_GD], w[:, OFF_GD:OFF_XP], w[:, OFF_XP:OFF_BA],
                                    w[:, OFF_BA:OFF_BA + 8], w[:, OFF_BA + 8:OFF_BA + 16])
    return jnp.concatenate([xp, q, k, v, z, b, a, gp, gd], axis=1)


def _lane_row(vec8):
    return jnp.zeros((1, 128), F32).at[0, NH:2 * NH].set(vec8)


def _ffn_fwd(x, g, shift, scale, gate, started, tag):
    t = x.shape[0]
    h = _norm_mod_fwd(x, g, shift, scale, f"{tag}_norm")
    w_in, = _push_wait(started[:1], False, h, f"{tag}_gather_wait_in")
    u = _matmul(h, w_in, b_blk=True, o_blk=True, out_dtype=F32, name=f"{tag}_up").reshape(2, NDEV // 2, t, FB)
    a = _swiglu_fwd(u, f"{tag}_act")
    w_out, = _push_wait(started[1:], False, a, f"{tag}_gather_wait_out")
    w_out = w_out.reshape(FH, D)
    y = _matmul(a, w_out, a_blk=True, out_dtype=F32, name=f"{tag}_down")
    return _resid_fwd(x, y, gate, 0.5, f"{tag}_res"), (h, u, a, y), w_in, w_out


def _ffn_bwd(dx_out, x, g, scale, gate, w_in, w_out, saved, tag):
    h, u, a, y = saved
    t = x.shape[0]
    dy, dgate = _resid_bwd(dx_out, y, gate, 0.5, f"{tag}_res_bwd")
    da = _matmul(dy, w_out, tb=True, tn=FB, o_blk=True, out_dtype=F32, name=f"{tag}_down_dx")
    dw_out = _matmul(a, dy, ta=True, a_blk=True, out_dtype=BF16, name=f"{tag}_down_dw")
    du = _swiglu_bwd(u, da, f"{tag}_act_bwd").reshape(NDEV, t, FB)
    dw_in = _matmul(h, du, ta=True, b_blk=True, o_blk=True, out_dtype=BF16, name=f"{tag}_up_dw")
    started, token = _push_start([dw_in, dw_out.reshape(NDEV, FH // NDEV, D)], True, f"{tag}_grad_start")
    dh = _matmul(du, w_in, tb=True, a_blk=True, b_blk=True, out_dtype=F32, name=f"{tag}_up_dx", after=token)
    dx, dshift, dscale, dg = _norm_mod_bwd(x, g, scale, dh, dx_out, f"{tag}_norm_bwd")
    return dx, (dshift, dscale, dgate), dg, started


def kernel(x, c, ada_w, ada_b, norm_g, ffn1_w_in, ffn1_w_out, ffn2_w_in, ffn2_w_out, mix_w_in, conv_w, a_log, dt_bias, dn_norm_g, pool_w, pool_scale, pool_proj, dn_proj, mix_w_out, final_g, loss_target, m_ada_w, m_ada_b, m_norm_g, m_ffn1_w_in, m_ffn1_w_out, m_ffn2_w_in, m_ffn2_w_out, m_mix_w_in, m_conv_w, m_a_log, m_dt_bias, m_dn_norm_g, m_pool_w, m_pool_scale, m_pool_proj, m_dn_proj, m_mix_w_out, m_final_g, v_ada_w, v_ada_b, v_norm_g, v_ffn1_w_in, v_ffn1_w_out, v_ffn2_w_in, v_ffn2_w_out, v_mix_w_in, v_conv_w, v_a_log, v_dt_bias, v_dn_norm_g, v_pool_w, v_pool_scale, v_pool_proj, v_dn_proj, v_mix_w_out, v_final_g):
    me = 4 * lax.axis_index("x") + 2 * lax.axis_index("y") + lax.axis_index("c")
    x0 = x[0]
    target = loss_target[0]
    t = x0.shape[0]

    big = [ffn1_w_in[0], ffn1_w_out[0], ffn2_w_in[0], ffn2_w_out[0], mix_w_in[0], pool_proj[0], dn_proj[0],
           mix_w_out[0]]
    order = [0, 1, 4, 5, 6, 7, 2, 3]
    started, token = _push_start([big[i].astype(BF16) for i in order], False, "gather_start")
    started = {i: s for i, s in zip(order, started)}

    small = jnp.concatenate([c.reshape(8, 128), conv_w[0].reshape(12, 128), norm_g[0].reshape(3, 128),
                             jnp.zeros((1, 128), F32)], axis=0)
    small_all, = _all_gather([small], "gather_small", after=token)
    c_all = small_all[:, 0:8, :].reshape(NDEV, D)
    conv_full = small_all[:, 8:20, :].reshape(NDEV, 4, 384).transpose(1, 0, 2).reshape(4, 3 * D)
    norm_full = small_all[:, 20:23, :].reshape(NDEV, 3, 128).transpose(1, 0, 2).reshape(3, D)

    ncol = ada_w.shape[2]
    ada_b_mine = lax.dynamic_slice(ada_b, (0, me * ncol), (1, ncol))
    mod_cols = _ada_fwd(c_all, ada_w[0], ada_b_mine, "ada_fwd")
    mod_all, = _all_gather([mod_cols], "gather_mod")
    mod = lax.dynamic_index_in_dim(mod_all, me, axis=1, keepdims=False).reshape(9, D)
    shift = [mod[3 * s:3 * s + 1] for s in range(3)]
    scale = [mod[3 * s + 1:3 * s + 2] for s in range(3)]
    gate = [mod[3 * s + 2:3 * s + 3] for s in range(3)]
    ng = [norm_full[s:s + 1] for s in range(3)]
    fg = final_g.reshape(1, D)
    al_row = _lane_row(a_log[0])
    dt_row = _lane_row(dt_bias[0])
    gn = dn_norm_g
    pw = pool_w[0]
    ps = pool_scale

    x1, saved1, w_in1, w_out1 = _ffn_fwd(x0, ng[0], shift[0], scale[0], gate[0], [started[0], started[1]], "ffn1")

    h1 = _norm_mod_fwd(x1, ng[1], shift[1], scale[1], "mix_norm")
    seg = _push_wait([started[i] for i in (4, 5, 6, 7)], False, h1, "mix_gather_wait")
    w_mix = _mix_pad(_cols_from_blocks(seg[0]))
    w_pp = _cols_from_blocks(seg[1])
    w_dn = seg[2].reshape(D, D)
    w_mo = seg[3].reshape(D, D)
    proj = _matmul(h1, w_mix, out_dtype=F32, name="mix_in")
    ya = _pool_fwd(proj, pw, ps, w_pp, "pool_fwd")
    qh, kh, vh, bg = _dn_pre_fwd(proj, conv_full, al_row, dt_row, "dn_pre")
    u, w, qk, qd, kd, eg, inv = _dn_local_fwd(qh, kh, vh, bg, "dn_local")
    o, s_saved = _dn_scan_fwd(u, w, qk, qd, kd, eg, "dn_scan")
    ob = _dn_post_fwd(o, proj, gn, "dn_post")
    yb = _matmul(ob, w_dn, out_dtype=F32, name="dn_out")
    merged = _merge_fwd(ya, yb, proj, "merge")
    mix_y = _matmul(merged, w_mo, out_dtype=F32, name="mix_out")
    x2 = _resid_fwd(x1, mix_y, gate[1], 1.0, "mix_res")

    x3, saved2, w_in2, w_out2 = _ffn_fwd(x2, ng[2], shift[2], scale[2], gate[2], [started[2], started[3]], "ffn2")
    loss_row, dx3, dfg = _final_loss(x3, fg, target, "loss")

    dx2, dmod2, dng2, sent2 = _ffn_bwd(dx3, x2, ng[2], scale[2], gate[2], w_in2, w_out2, saved2, "ffn2")

    dmy, dgate1 = _resid_bwd(dx2, mix_y, gate[1], 1.0, "mix_res_bwd")
    dmerged = _matmul(dmy, w_mo, tb=True, out_dtype=F32, name="mix_out_dx")
    dw_mo = _matmul(merged, dmy, ta=True, out_dtype=BF16, name="mix_out_dw")
    dya, dyb, dgp, dgd = _merge_bwd(dmerged, ya, yb, proj, "merge_bwd")
    dob = _matmul(dyb, w_dn, tb=True, out_dtype=F32, name="dn_out_dx")
    dw_dn = _matmul(ob, dyb, ta=True, out_dtype=BF16, name="dn_out_dw")
    do, dz, dgn = _dn_post_bwd(o, proj, gn, dob, "dn_post_bwd")
    du, dw, dqk, dqd, dkd, deg = _dn_scan_bwd(u, w, qk, qd, kd, eg, s_saved, do, "dn_scan_bwd")
    dqh, dkh, dvh, dbg = _dn_local_bwd(qh, kh, vh, bg, inv, du, dw, dqk, dqd, dkd, deg, "dn_local_bwd")
    dconv, draw, dal, ddt = _dn_pre_bwd_act(proj, conv_full, al_row, dt_row, dqh, dkh, dvh, dbg, "dn_pre_bwd_act")
    dqkv, dcw = _dn_pre_bwd_conv(proj, conv_full, dconv, "dn_pre_bwd_conv")
    dwin, dpl, dpw, dps, dpp = _pool_bwd_local(proj, pw, ps, w_pp, dya, "pool_bwd_local")
    dxp = _pool_bwd_window(dwin, dpl, "pool_bwd_window")
    dproj = jnp.concatenate([dqkv, dz, dgp, dgd, dxp, draw, jnp.zeros((t, MIXP - OFF_BA - 128), BF16)], axis=1)
    dw_mix = _matmul(h1, dproj, ta=True, out_dtype=BF16, name="mix_in_dw")
    sent1, token = _push_start(
        [_cols_to_blocks(_mix_unpad(dw_mix)), _cols_to_blocks(dpp.astype(BF16)), dw_dn.reshape(NDEV, -1, D),
         dw_mo.reshape(NDEV, -1, D)], True, "mix_grad_start")
    dh1 = _matmul(dproj, w_mix, tb=True, out_dtype=F32, name="mix_in_dx", after=token)
    dx1, dsh1, dsc1, dng1 = _norm_mod_bwd(x1, ng[1], scale[1], dh1, dx2, "mix_norm_bwd")

    dx0, dmod0, dng0, sent0 = _ffn_bwd(dx1, x0, ng[0], scale[0], gate[0], w_in1, w_out1, saved1, "ffn1")

    dmod = jnp.concatenate([*dmod0, dsh1, dsc1, dgate1, *dmod2], axis=1).reshape(-1)
    flat = jnp.concatenate([
        dmod, dal[0, NH:2 * NH], ddt[0, NH:2 * NH], dgn.reshape(-1), dps.reshape(-1), dfg.reshape(-1),
        dpw.reshape(-1), jnp.concatenate([dng0, dng1, dng2], axis=0).reshape(-1), dcw.reshape(-1)])
    nflat = 90 * D
    flat = jnp.concatenate([flat, jnp.zeros((nflat - flat.shape[0],), F32)]).reshape(90, D)
    flat_all, = _all_gather([flat], "gather_small_grads")
    tot = _sum_devices(flat_all, F32, "sum_small_grads").reshape(-1)
    dmod_all = flat_all.reshape(NDEV, nflat)[:, :9 * D]
    dmod_cols = lax.dynamic_slice(dmod_all, (0, me * ncol), (NDEV, ncol))
    g_ada_w = _ada_bwd(c_all.T, dmod_cols, "ada_bwd")

    p = 0
    pieces = {}
    for nm, size in (("ada_b", 9 * D), ("a_log", NH), ("dt_bias", NH), ("dn_norm_g", HD), ("pool_scale", PW),
                     ("final_g", D), ("pool_w", 4 * PG * PG), ("norm_g", 3 * D), ("conv_w", 12 * D)):
        pieces[nm] = tot[p:p + size]
        p += size
    g_norm = lax.dynamic_slice(pieces["norm_g"].reshape(3, D), (0, me * 128), (3, 128))
    g_conv = lax.dynamic_slice(pieces["conv_w"].reshape(4, 3 * D), (0, me * 384), (4, 384))

    grads = {
        "ada_w": g_ada_w.reshape(ada_w.shape), "ada_b": pieces["ada_b"].reshape(ada_b.shape),
        "norm_g": g_norm.reshape(norm_g.shape), "conv_w": g_conv.reshape(conv_w.shape),
        "a_log": pieces["a_log"].reshape(a_log.shape), "dt_bias": pieces["dt_bias"].reshape(dt_bias.shape),
        "dn_norm_g": pieces["dn_norm_g"].reshape(dn_norm_g.shape), "pool_w": pieces["pool_w"].reshape(pool_w.shape),
        "pool_scale": pieces["pool_scale"].reshape(pool_scale.shape),
        "final_g": pieces["final_g"].reshape(final_g.shape),
    }
    weights = {"ada_w": ada_w, "ada_b": ada_b, "norm_g": norm_g, "ffn1_w_in": ffn1_w_in, "ffn1_w_out": ffn1_w_out,
               "ffn2_w_in": ffn2_w_in, "ffn2_w_out": ffn2_w_out, "mix_w_in": mix_w_in, "conv_w": conv_w,
               "a_log": a_log, "dt_bias": dt_bias, "dn_norm_g": dn_norm_g, "pool_w": pool_w,
               "pool_scale": pool_scale, "pool_proj": pool_proj, "dn_proj": dn_proj, "mix_w_out": mix_w_out,
               "final_g": final_g}
    m_in = {"ada_w": m_ada_w, "ada_b": m_ada_b, "norm_g": m_norm_g, "ffn1_w_in": m_ffn1_w_in,
            "ffn1_w_out": m_ffn1_w_out, "ffn2_w_in": m_ffn2_w_in, "ffn2_w_out": m_ffn2_w_out,
            "mix_w_in": m_mix_w_in, "conv_w": m_conv_w, "a_log": m_a_log, "dt_bias": m_dt_bias,
            "dn_norm_g": m_dn_norm_g, "pool_w": m_pool_w, "pool_scale": m_pool_scale, "pool_proj": m_pool_proj,
            "dn_proj": m_dn_proj, "mix_w_out": m_mix_w_out, "final_g": m_final_g}
    v_in = {"ada_w": v_ada_w, "ada_b": v_ada_b, "norm_g": v_norm_g, "ffn1_w_in": v_ffn1_w_in,
            "ffn1_w_out": v_ffn1_w_out, "ffn2_w_in": v_ffn2_w_in, "ffn2_w_out": v_ffn2_w_out,
            "mix_w_in": v_mix_w_in, "conv_w": v_conv_w, "a_log": v_a_log, "dt_bias": v_dt_bias,
            "dn_norm_g": v_dn_norm_g, "pool_w": v_pool_w, "pool_scale": v_pool_scale, "pool_proj": v_pool_proj,
            "dn_proj": v_dn_proj, "mix_w_out": v_mix_w_out, "final_g": v_final_g}

    names = list(weights)
    large = ("ada_w", "ffn1_w_in", "ffn1_w_out", "ffn2_w_in", "ffn2_w_out", "mix_w_in", "pool_proj", "dn_proj",
             "mix_w_out")
    delta, new_m, new_v = {}, {}, {}

    def update(nm):
        shp = weights[nm].shape
        two_d = (shp[-2], shp[-1])
        d_, m_, v_ = _adamw(weights[nm].reshape(two_d), grads[nm].reshape(two_d), m_in[nm].reshape(two_d),
                            v_in[nm].reshape(two_d), f"adamw_{nm}")
        delta[nm], new_m[nm], new_v[nm] = d_.reshape(shp), m_.reshape(shp), v_.reshape(shp)
        return d_

    def reduce(sent, group, after, tag):
        for nm, r in zip(group, _push_wait(sent, True, after, f"{tag}_grad_wait")):
            grads[nm] = _sum_devices(r, F32, f"sum_grads_{nm}").reshape(weights[nm].shape)

    done = update("ada_w")
    reduce(sent2, ("ffn2_w_in", "ffn2_w_out"), done, "ffn2")
    update("ffn2_w_in")
    done = update("ffn2_w_out")
    reduce(sent1, ("mix_w_in", "pool_proj", "dn_proj", "mix_w_out"), done, "mix")
    for nm in ("mix_w_in", "pool_proj", "dn_proj", "mix_w_out"):
        done = update(nm)
    reduce(sent0, ("ffn1_w_in", "ffn1_w_out"), done, "ffn1")
    update("ffn1_w_in")
    update("ffn1_w_out")
    rest = [nm for nm in names if nm not in large]
    total = sum(weights[nm].size for nm in rest)
    padded = -(-total // D) * D

    def pack(tree, fill):
        flat_ = jnp.concatenate([tree[nm].reshape(-1) for nm in rest])
        return jnp.concatenate([flat_, jnp.full((padded - total,), fill, F32)]).reshape(-1, D)

    d_, m_, v_ = _adamw(pack(weights, 0.0), pack(grads, 0.0), pack(m_in, 0.0), pack(v_in, 1.0), "adamw_small")
    p = 0
    for nm in rest:
        size = weights[nm].size
        shp = weights[nm].shape
        delta[nm] = d_.reshape(-1)[p:p + size].reshape(shp)
        new_m[nm] = m_.reshape(-1)[p:p + size].reshape(shp)
        new_v[nm] = v_.reshape(-1)[p:p + size].reshape(shp)
        p += size

    loss = lax.psum(loss_row[0, 0], ("x", "y", "c"))
    grad_x = dx0.reshape(x.shape)
    return (loss, grad_x, *[grads[nm] for nm in names], *[delta[nm] for nm in names],
            *[new_m[nm] for nm in names], *[new_v[nm] for nm in names])
```

```python
import functools

import jax
import jax.numpy as jnp
from jax import lax
from jax.experimental import pallas as pl
from jax.experimental.pallas import tpu as pltpu

F32 = jnp.float32
BF16 = jnp.bfloat16
SDS = jax.ShapeDtypeStruct
HI = lax.Precision.HIGHEST

D = 1024
FH = 2816
FB = 704
NH = 8
HD = 128
CH = 64
NDEV = 8
PW = 512
PG = 128
RMS_EPS = 1e-6
L2_EPS = 1e-6
TR = 256
HALO = 16
VMEM_LIMIT = 56 * 1024 * 1024

MIXP = 6912
OFF_Q, OFF_K, OFF_V, OFF_Z, OFF_GP, OFF_GD, OFF_XP, OFF_BA = 0, 1024, 2048, 3072, 4096, 5120, 6144, 6656
MIX_RAW = 6672

ADAM_LR = 0.001
ADAM_B1 = 0.9
ADAM_B2 = 0.999
ADAM_EPS = 1e-08
ADAM_WD = 0.01
ADAM_STEP = 10

NN = (((1,), (0,)), ((), ()))
NT = (((1,), (1,)), ((), ()))
TN = (((0,), (0,)), ((), ()))


def _dg(a, b, dims, prec=None):
    return lax.dot_general(a, b, dims, precision=prec, preferred_element_type=F32)


def _make_dots(prec):
    @jax.custom_vjp
    def nn(a, b):
        return _dg(a, b, NN, prec)

    @jax.custom_vjp
    def nt(a, b):
        return _dg(a, b, NT, prec)

    @jax.custom_vjp
    def tn(a, b):
        return _dg(a, b, TN, prec)

    nn.defvjp(lambda a, b: (nn(a, b), (a, b)), lambda r, d: (nt(d, r[1]), tn(r[0], d)))
    nt.defvjp(lambda a, b: (nt(a, b), (a, b)), lambda r, d: (nn(d, r[1]), tn(d, r[0])))
    tn.defvjp(lambda a, b: (tn(a, b), (a, b)), lambda r, d: (nt(r[1], d), nn(r[0], d)))
    return nn, nt, tn


_nn, _nt, _tn = _make_dots(None)


def _params(sem):
    return pltpu.CompilerParams(dimension_semantics=sem, vmem_limit_bytes=VMEM_LIMIT)


def _sigmoid(x):
    return 1.0 / (1.0 + jnp.exp(-x))


def _silu(x):
    return x * _sigmoid(x)


def _dsilu(x):
    s = _sigmoid(x)
    return s * (1.0 + x * (1.0 - s))


def _pick(n, cands):
    for c in cands:
        if n % c == 0:
            return c
    raise ValueError(f"no tile for {n}")


def _iota(shape, dim):
    return lax.broadcasted_iota(jnp.int32, shape, dim)


def _matmul(a, b, *, ta=False, tb=False, a_blk=False, b_blk=False, o_blk=False, tm=None, tn=None, tk=None,
            out_dtype, name, after=None):
    if a_blk:
        nb, r, cb = a.shape
        if ta:
            k_dim, m_dim, tm = r, nb * cb, cb
        else:
            m_dim, k_dim, tk = r, nb * cb, cb
    else:
        k_dim, m_dim = a.shape if ta else a.shape[::-1]
    if b_blk:
        nb, r, cb = b.shape
        if tb:
            n_dim, tk = r, cb
            assert nb * cb == k_dim
        else:
            n_dim, tn = nb * cb, cb
            assert r == k_dim
    else:
        n_dim = b.shape[0] if tb else b.shape[1]
    tm = tm or _pick(m_dim, (1024, 512, 256, 128))
    tn = tn or _pick(n_dim, (1024, 768, 512, 256, 128))
    tk = tk or (k_dim if (k_dim <= 2816 and not ta) else _pick(k_dim, (2816, 2304, 1024, 512, 256)))
    nk = k_dim // tk
    dims = ((((0,) if ta else (1,)), ((1,) if tb else (0,))), ((), ()))

    def body(a_ref, b_ref, *rest):
        o_ref, acc_ref = rest[-2:]
        k = pl.program_id(2)

        @pl.when(k == 0)
        def _():
            acc_ref[...] = jnp.zeros_like(acc_ref)

        acc_ref[...] += lax.dot_general(a_ref[...].astype(BF16), b_ref[...].astype(BF16), dims,
                                        preferred_element_type=F32)

        @pl.when(k == nk - 1)
        def _():
            o_ref[...] = acc_ref[...].astype(o_ref.dtype)

    if a_blk:
        a_spec = (pl.BlockSpec((None, tk, tm), lambda i, j, k: (i, k, 0)) if ta
                  else pl.BlockSpec((None, tm, tk), lambda i, j, k: (k, i, 0)))
    else:
        a_spec = (pl.BlockSpec((tk, tm), lambda i, j, k: (k, i)) if ta
                  else pl.BlockSpec((tm, tk), lambda i, j, k: (i, k)))
    if b_blk:
        b_spec = (pl.BlockSpec((None, tn, tk), lambda i, j, k: (k, j, 0)) if tb
                  else pl.BlockSpec((None, tk, tn), lambda i, j, k: (j, k, 0)))
    else:
        b_spec = (pl.BlockSpec((tn, tk), lambda i, j, k: (j, k)) if tb
                  else pl.BlockSpec((tk, tn), lambda i, j, k: (k, j)))
    if o_blk:
        o_spec = pl.BlockSpec((None, tm, tn), lambda i, j, k: (j, i, 0))
        o_shape = SDS((n_dim // tn, m_dim, tn), out_dtype)
    else:
        o_spec = pl.BlockSpec((tm, tn), lambda i, j, k: (i, j))
        o_shape = SDS((m_dim, n_dim), out_dtype)
    return pl.pallas_call(
        body, grid=(m_dim // tm, n_dim // tn, nk),
        in_specs=[a_spec, b_spec] + ([] if after is None else [pl.BlockSpec(memory_space=pl.ANY)]),
        out_specs=o_spec,
        out_shape=o_shape,
        scratch_shapes=[pltpu.VMEM((tm, tn), F32)],
        compiler_params=_params(("parallel", "parallel", "arbitrary")),
        name=name,
    )(a, b, *([] if after is None else [after]))


def _row(width, col=0):
    return pl.BlockSpec((TR, width), lambda i: (i, col))


def _vec(width):
    return pl.BlockSpec((1, width), lambda i: (0, 0))


def _norm_mod_fwd(x, g, shift, scale, name):
    t = x.shape[0]

    def body(x_ref, g_ref, sh_ref, sc_ref, o_ref):
        xv = x_ref[...]
        r = lax.rsqrt(jnp.mean(xv * xv, axis=-1, keepdims=True) + RMS_EPS)
        o_ref[...] = (((xv * r) * g_ref[...]) * (1.0 + sc_ref[...]) + sh_ref[...]).astype(o_ref.dtype)

    return pl.pallas_call(
        body, grid=(t // TR,), in_specs=[_row(D), _vec(D), _vec(D), _vec(D)], out_specs=_row(D),
        out_shape=SDS((t, D), BF16), compiler_params=_params(("parallel",)), name=name,
    )(x, g, shift, scale)


def _norm_mod_bwd(x, g, scale, dh, dx_in, name):
    t = x.shape[0]

    def body(x_ref, g_ref, sc_ref, dh_ref, dxi_ref, dx_ref, dsh_ref, dsc_ref, dg_ref):
        @pl.when(pl.program_id(0) == 0)
        def _():
            dsh_ref[...] = jnp.zeros_like(dsh_ref)
            dsc_ref[...] = jnp.zeros_like(dsc_ref)
            dg_ref[...] = jnp.zeros_like(dg_ref)

        xv = x_ref[...]
        gv = g_ref[...]
        dh = dh_ref[...]
        r = lax.rsqrt(jnp.mean(xv * xv, axis=-1, keepdims=True) + RMS_EPS)
        n = xv * r
        dsh_ref[...] += jnp.sum(dh, axis=0, keepdims=True)
        dsc_ref[...] += jnp.sum(dh * (n * gv), axis=0, keepdims=True)
        tt = dh * (1.0 + sc_ref[...])
        dg_ref[...] += jnp.sum(tt * n, axis=0, keepdims=True)
        dn = tt * gv
        dx_ref[...] = dxi_ref[...] + r * (dn - n * jnp.mean(dn * n, axis=-1, keepdims=True))

    return pl.pallas_call(
        body, grid=(t // TR,), in_specs=[_row(D), _vec(D), _vec(D), _row(D), _row(D)],
        out_specs=[_row(D), _vec(D), _vec(D), _vec(D)],
        out_shape=[SDS((t, D), F32), SDS((1, D), F32), SDS((1, D), F32), SDS((1, D), F32)],
        compiler_params=_params(("arbitrary",)), name=name,
    )(x, g, scale, dh, dx_in)


def _swiglu_specs():
    pair = pl.BlockSpec((2, None, TR, FB), lambda i, j: (0, j, i, 0))
    one = pl.BlockSpec((None, TR, FB), lambda i, j: (j, i, 0))
    return pair, one


def _swiglu_fwd(u, name):
    t = u.shape[2]

    def body(u_ref, o_ref):
        o_ref[...] = (_silu(u_ref[0]) * u_ref[1]).astype(o_ref.dtype)

    pair, one = _swiglu_specs()
    return pl.pallas_call(
        body, grid=(t // TR, NDEV // 2), in_specs=[pair], out_specs=one,
        out_shape=SDS((NDEV // 2, t, FB), BF16), compiler_params=_params(("parallel", "parallel")), name=name,
    )(u)


def _swiglu_bwd(u, da, name):
    t = u.shape[2]

    def body(u_ref, da_ref, o_ref):
        gv = u_ref[0]
        dav = da_ref[...]
        o_ref[0] = (dav * u_ref[1] * _dsilu(gv)).astype(o_ref.dtype)
        o_ref[1] = (dav * _silu(gv)).astype(o_ref.dtype)

    pair, one = _swiglu_specs()
    return pl.pallas_call(
        body, grid=(t // TR, NDEV // 2), in_specs=[pair, one], out_specs=pair,
        out_shape=SDS((2, NDEV // 2, t, FB), BF16), compiler_params=_params(("parallel", "parallel")), name=name,
    )(u, da)


def _resid_fwd(x, y, gate, coef, name):
    t = x.shape[0]

    def body(x_ref, y_ref, g_ref, o_ref):
        o_ref[...] = x_ref[...] + (coef * g_ref[...]) * y_ref[...]

    return pl.pallas_call(
        body, grid=(t // TR,), in_specs=[_row(D), _row(D), _vec(D)], out_specs=_row(D),
        out_shape=SDS((t, D), F32), compiler_params=_params(("parallel",)), name=name,
    )(x, y, gate)


def _resid_bwd(dx, y, gate, coef, name):
    t = dx.shape[0]

    def body(dx_ref, y_ref, g_ref, dy_ref, dg_ref):
        @pl.when(pl.program_id(0) == 0)
        def _():
            dg_ref[...] = jnp.zeros_like(dg_ref)

        dxv = dx_ref[...]
        dy_ref[...] = ((coef * g_ref[...]) * dxv).astype(dy_ref.dtype)
        dg_ref[...] += jnp.sum((coef * dxv) * y_ref[...], axis=0, keepdims=True)

    return pl.pallas_call(
        body, grid=(t // TR,), in_specs=[_row(D), _row(D), _vec(D)], out_specs=[_row(D), _vec(D)],
        out_shape=[SDS((t, D), BF16), SDS((1, D), F32)],
        compiler_params=_params(("arbitrary",)), name=name,
    )(dx, y, gate)


def _final_loss(x, fg, target, name):
    t = x.shape[0]
    nt = t // TR

    def body(x_ref, g_ref, t_ref, loss_ref, dx_ref, dg_ref, acc_ref):
        i = pl.program_id(0)

        @pl.when(i == 0)
        def _():
            acc_ref[...] = jnp.zeros_like(acc_ref)
            dg_ref[...] = jnp.zeros_like(dg_ref)

        xv = x_ref[...]
        gv = g_ref[...]
        r = lax.rsqrt(jnp.mean(xv * xv, axis=-1, keepdims=True) + RMS_EPS)
        n = xv * r
        err = n * gv - t_ref[...]
        acc_ref[...] += jnp.sum(err * err, axis=0, keepdims=True)
        dy = err * (1.0 / D)
        dg_ref[...] += jnp.sum(dy * n, axis=0, keepdims=True)
        dn = dy * gv
        dx_ref[...] = r * (dn - n * jnp.mean(dn * n, axis=-1, keepdims=True))

        @pl.when(i == nt - 1)
        def _():
            tot = jnp.sum(acc_ref[...], axis=1, keepdims=True) * (0.5 / D)
            loss_ref[...] = jnp.broadcast_to(tot, loss_ref.shape)

    return pl.pallas_call(
        body, grid=(nt,), in_specs=[_row(D), _vec(D), _row(D)],
        out_specs=[_vec(128), _row(D), _vec(D)],
        out_shape=[SDS((1, 128), F32), SDS((t, D), F32), SDS((1, D), F32)],
        scratch_shapes=[pltpu.VMEM((1, D), F32)],
        compiler_params=_params(("arbitrary",)), name=name,
    )(x, fg, target)


def _halo_prev(width, col):
    per = TR // HALO
    return pl.BlockSpec((HALO, width), lambda i: (jnp.maximum(i * per - 1, 0), col))


def _halo_next(width, col, nt):
    per = TR // HALO
    return pl.BlockSpec((HALO, width), lambda i: (jnp.minimum((i + 1) * per, nt * per - 1), col))


def _pool_windows(ext, tile_index):
    rows = _iota((TR, PG), 0) + tile_index * TR + 1
    pooled, counts = [], []
    for gi in range(4):
        w = 2 << gi
        e = ext[:, gi * PG:(gi + 1) * PG]
        s = e
        step = 1
        while step < w:
            s = s + pltpu.roll(s, step, 0)
            step *= 2
        cnt = jnp.minimum(rows, w).astype(F32)
        pooled.append(s[HALO:] / cnt - e[HALO:])
        counts.append(cnt)
    return pooled, counts


def _pool_fwd(proj, pool_w, pool_scale, pool_proj, name):
    t = proj.shape[0]
    xcol = OFF_XP // PW

    def body(x_ref, h_ref, pw_ref, ps_ref, pp_ref, o_ref):
        i = pl.program_id(0)
        halo = jnp.where(i > 0, h_ref[...], 0.0)
        ext = jnp.concatenate([halo, x_ref[...]], axis=0)
        pooled, _ = _pool_windows(ext, i)
        mixed = [_dg(pooled[g].astype(BF16), pw_ref[g].astype(BF16), NN) for g in range(4)]
        ypre = jnp.concatenate(mixed, axis=1) * ps_ref[...]
        o_ref[...] = _dg(ypre.astype(BF16), pp_ref[...], NN)

    return pl.pallas_call(
        body, grid=(t // TR,),
        in_specs=[_row(PW, xcol), _halo_prev(PW, xcol),
                  pl.BlockSpec((4, PG, PG), lambda i: (0, 0, 0)), _vec(PW),
                  pl.BlockSpec((PW, D), lambda i: (0, 0))],
        out_specs=_row(D), out_shape=SDS((t, D), F32),
        compiler_params=_params(("parallel",)), name=name,
    )(proj, proj, pool_w, pool_scale, pool_proj)


def _pool_bwd_local(proj, pool_w, pool_scale, pool_proj, dya, name):
    t = proj.shape[0]
    xcol = OFF_XP // PW

    def body(x_ref, h_ref, pw_ref, ps_ref, pp_ref, dya_ref, dwin_ref, dpl_ref, dpw_ref, dps_ref, dpp_ref):
        i = pl.program_id(0)

        @pl.when(i == 0)
        def _():
            dpw_ref[...] = jnp.zeros_like(dpw_ref)
            dps_ref[...] = jnp.zeros_like(dps_ref)
            dpp_ref[...] = jnp.zeros_like(dpp_ref)

        halo = jnp.where(i > 0, h_ref[...], 0.0)
        ext = jnp.concatenate([halo, x_ref[...]], axis=0)
        pooled, counts = _pool_windows(ext, i)
        mixed = jnp.concatenate(
            [_dg(pooled[g].astype(BF16), pw_ref[g].astype(BF16), NN) for g in range(4)], axis=1)
        ps = ps_ref[...]
        ypre = mixed * ps
        dyab = dya_ref[...].astype(BF16)
        dypre = _dg(dyab, pp_ref[...], NT)
        dpp_ref[...] += _dg(ypre.astype(BF16), dyab, TN)
        dps_ref[...] += jnp.sum(dypre * mixed, axis=0, keepdims=True)
        dmixed = dypre * ps
        for g in range(4):
            dm = dmixed[:, g * PG:(g + 1) * PG].astype(BF16)
            dpw_ref[g] += _dg(pooled[g].astype(BF16), dm, TN)
            dpooled = _dg(dm, pw_ref[g].astype(BF16), NT)
            dwin_ref[:, g * PG:(g + 1) * PG] = dpooled / counts[g]
            dpl_ref[:, g * PG:(g + 1) * PG] = dpooled

    return pl.pallas_call(
        body, grid=(t // TR,),
        in_specs=[_row(PW, xcol), _halo_prev(PW, xcol),
                  pl.BlockSpec((4, PG, PG), lambda i: (0, 0, 0)), _vec(PW),
                  pl.BlockSpec((PW, D), lambda i: (0, 0)), _row(D)],
        out_specs=[_row(PW), _row(PW), pl.BlockSpec((4, PG, PG), lambda i: (0, 0, 0)), _vec(PW),
                   pl.BlockSpec((PW, D), lambda i: (0, 0))],
        out_shape=[SDS((t, PW), F32), SDS((t, PW), F32), SDS((4, PG, PG), F32), SDS((1, PW), F32),
                   SDS((PW, D), F32)],
        compiler_params=_params(("arbitrary",)), name=name,
    )(proj, proj, pool_w, pool_scale, pool_proj, dya)


def _pool_bwd_window(dwin, dpl, name):
    t = dwin.shape[0]
    nt = t // TR
    ext_rows = TR + HALO

    def body(dw_ref, h_ref, dp_ref, o_ref):
        i = pl.program_id(0)
        halo = jnp.where(i < nt - 1, h_ref[...], 0.0)
        ext = jnp.concatenate([dw_ref[...], halo], axis=0)
        for gi in range(4):
            w = 2 << gi
            s = ext[:, gi * PG:(gi + 1) * PG]
            step = 1
            while step < w:
                s = s + pltpu.roll(s, ext_rows - step, 0)
                step *= 2
            o_ref[:, gi * PG:(gi + 1) * PG] = (s[:TR] - dp_ref[:, gi * PG:(gi + 1) * PG]).astype(o_ref.dtype)

    return pl.pallas_call(
        body, grid=(nt,), in_specs=[_row(PW), _halo_next(PW, 0, nt), _row(PW)], out_specs=_row(PW),
        out_shape=SDS((t, PW), BF16), compiler_params=_params(("parallel",)), name=name,
    )(dwin, dwin, dpl)


def _conv_group(ext, cw_ref, cols):
    acc = cw_ref[3:4, cols] * ext
    for j in range(3):
        acc = acc + cw_ref[j:j + 1, cols] * pltpu.roll(ext, 3 - j, 0)
    return acc[HALO:]


def _gate_terms(raw, al, dt):
    beta = _sigmoid(raw)
    xg = raw + dt
    sp = jnp.maximum(xg, 0.0) + jnp.log(1.0 + jnp.exp(-jnp.abs(xg)))
    g = -jnp.exp(al) * sp
    return beta, g, _sigmoid(xg)


def _dn_pre_fwd(proj, conv_w, al_row, dt_row, name):
    t = proj.shape[0]

    def body(x_ref, h_ref, cw_ref, ba_ref, al_ref, dt_ref, q_ref, k_ref, v_ref, bg_ref):
        i = pl.program_id(0)
        keep = i > 0
        for grp in range(24):
            cols = slice(grp * HD, (grp + 1) * HD)
            ext = jnp.concatenate([jnp.where(keep, h_ref[:, cols], 0.0), x_ref[:, cols]], axis=0)
            s = _silu(_conv_group(ext, cw_ref, cols))
            seg, head = divmod(grp, NH)
            hc = slice(head * HD, (head + 1) * HD)
            if seg == 0:
                q_ref[:, hc] = s * lax.rsqrt(jnp.sum(s * s, axis=-1, keepdims=True) + L2_EPS) * (HD ** -0.5)
            elif seg == 1:
                k_ref[:, hc] = s * lax.rsqrt(jnp.sum(s * s, axis=-1, keepdims=True) + L2_EPS)
            else:
                v_ref[:, hc] = s
        lane = _iota((TR, 128), 1)
        rowc = _iota((TR, 128), 0) % CH
        beta, g, _ = _gate_terms(ba_ref[...], al_ref[...], dt_ref[...])
        step = 1
        while step < CH:
            g = g + jnp.where(rowc >= step, pltpu.roll(g, step, 0), 0.0)
            step *= 2
        bg_ref[...] = jnp.where(lane < NH, beta, jnp.where(lane < 2 * NH, g, 0.0))

    return pl.pallas_call(
        body, grid=(t // TR,),
        in_specs=[_row(3 * D, 0), _halo_prev(3 * D, 0), pl.BlockSpec((4, 3 * D), lambda i: (0, 0)),
                  _row(128, OFF_BA // 128), _vec(128), _vec(128)],
        out_specs=[_row(D), _row(D), _row(D), _row(128)],
        out_shape=[SDS((t, D), F32), SDS((t, D), F32), SDS((t, D), F32), SDS((t, 128), F32)],
        compiler_params=_params(("parallel",)), name=name,
    )(proj, proj, conv_w, proj, al_row, dt_row)


def _dn_pre_bwd_act(proj, conv_w, al_row, dt_row, dq, dk, dv, dbg, name):
    t = proj.shape[0]

    def body(x_ref, h_ref, cw_ref, ba_ref, al_ref, dt_ref, dq_ref, dk_ref, dv_ref, dbg_ref,
             dc_ref, draw_ref, dal_ref, ddt_ref):
        i = pl.program_id(0)

        @pl.when(i == 0)
        def _():
            dal_ref[...] = jnp.zeros_like(dal_ref)
            ddt_ref[...] = jnp.zeros_like(ddt_ref)

        keep = i > 0
        for grp in range(24):
            cols = slice(grp * HD, (grp + 1) * HD)
            ext = jnp.concatenate([jnp.where(keep, h_ref[:, cols], 0.0), x_ref[:, cols]], axis=0)
            cv = _conv_group(ext, cw_ref, cols)
            seg, head = divmod(grp, NH)
            hc = slice(head * HD, (head + 1) * HD)
            if seg == 2:
                ds = dv_ref[:, hc]
            else:
                s = _silu(cv)
                r = lax.rsqrt(jnp.sum(s * s, axis=-1, keepdims=True) + L2_EPS)
                dy = dq_ref[:, hc] if seg == 0 else dk_ref[:, hc]
                c = (HD ** -0.5) if seg == 0 else 1.0
                ds = (c * r) * (dy - s * ((r * r) * jnp.sum(dy * s, axis=-1, keepdims=True)))
            dc_ref[:, cols] = ds * _dsilu(cv)
        lane = _iota((TR, 128), 1)
        rowc = _iota((TR, 128), 0) % CH
        isb = lane < NH
        isg = jnp.logical_and(lane >= NH, lane < 2 * NH)
        beta, g, sg = _gate_terms(ba_ref[...], al_ref[...], dt_ref[...])
        dbgv = dbg_ref[...]
        dg = dbgv
        step = 1
        while step < CH:
            dg = dg + jnp.where(rowc < CH - step, pltpu.roll(dg, TR - step, 0), 0.0)
            step *= 2
        da_raw = dg * (-jnp.exp(al_ref[...])) * sg
        draw_ref[...] = jnp.where(isb, dbgv * beta * (1.0 - beta), jnp.where(isg, da_raw, 0.0)).astype(draw_ref.dtype)
        dal_ref[...] += jnp.sum(jnp.where(isg, dg * g, 0.0), axis=0, keepdims=True)
        ddt_ref[...] += jnp.sum(jnp.where(isg, da_raw, 0.0), axis=0, keepdims=True)

    return pl.pallas_call(
        body, grid=(t // TR,),
        in_specs=[_row(3 * D, 0), _halo_prev(3 * D, 0), pl.BlockSpec((4, 3 * D), lambda i: (0, 0)),
                  _row(128, OFF_BA // 128), _vec(128), _vec(128), _row(D), _row(D), _row(D), _row(128)],
        out_specs=[_row(3 * D), _row(128), _vec(128), _vec(128)],
        out_shape=[SDS((t, 3 * D), F32), SDS((t, 128), BF16), SDS((1, 128), F32), SDS((1, 128), F32)],
        compiler_params=_params(("arbitrary",)), name=name,
    )(proj, proj, conv_w, proj, al_row, dt_row, dq, dk, dv, dbg)


def _dn_pre_bwd_conv(proj, conv_w, dconv, name):
    t = proj.shape[0]
    nt = t // TR
    ext_rows = TR + HALO

    def body(x_ref, h_ref, cw_ref, dc_ref, dn_ref, dx_ref, dcw_ref):
        i = pl.program_id(0)

        @pl.when(i == 0)
        def _():
            dcw_ref[...] = jnp.zeros_like(dcw_ref)

        keep_prev = i > 0
        keep_next = i < nt - 1
        for grp in range(24):
            cols = slice(grp * HD, (grp + 1) * HD)
            dct = dc_ref[:, cols]
            dext = jnp.concatenate([dct, jnp.where(keep_next, dn_ref[:, cols], 0.0)], axis=0)
            acc = cw_ref[3:4, cols] * dext
            for j in range(3):
                acc = acc + cw_ref[j:j + 1, cols] * pltpu.roll(dext, ext_rows - (3 - j), 0)
            dx_ref[:, cols] = acc[:TR].astype(dx_ref.dtype)
            xext = jnp.concatenate([jnp.where(keep_prev, h_ref[:, cols], 0.0), x_ref[:, cols]], axis=0)
            for j in range(4):
                xs = xext if j == 3 else pltpu.roll(xext, 3 - j, 0)
                dcw_ref[j:j + 1, cols] += jnp.sum(xs[HALO:] * dct, axis=0, keepdims=True)

    return pl.pallas_call(
        body, grid=(nt,),
        in_specs=[_row(3 * D, 0), _halo_prev(3 * D, 0), pl.BlockSpec((4, 3 * D), lambda i: (0, 0)),
                  _row(3 * D), _halo_next(3 * D, 0, nt)],
        out_specs=[_row(3 * D), pl.BlockSpec((4, 3 * D), lambda i: (0, 0))],
        out_shape=[SDS((t, 3 * D), BF16), SDS((4, 3 * D), F32)],
        compiler_params=_params(("arbitrary",)), name=name,
    )(proj, proj, conv_w, dconv, dconv)


def _dn_post_fwd(o, proj, gn, name):
    t = o.shape[0]

    def body(o_ref, z_ref, g_ref, out_ref):
        gv = g_ref[...]
        for h in range(NH):
            hc = slice(h * HD, (h + 1) * HD)
            ov = o_ref[:, hc]
            r = lax.rsqrt(jnp.mean(ov * ov, axis=-1, keepdims=True) + RMS_EPS)
            out_ref[:, hc] = (((ov * r) * gv) * _silu(z_ref[:, hc])).astype(out_ref.dtype)

    return pl.pallas_call(
        body, grid=(t // TR,), in_specs=[_row(D), _row(D, OFF_Z // D), _vec(HD)], out_specs=_row(D),
        out_shape=SDS((t, D), BF16), compiler_params=_params(("parallel",)), name=name,
    )(o, proj, gn)


def _dn_post_bwd(o, proj, gn, dob, name):
    t = o.shape[0]

    def body(o_ref, z_ref, g_ref, d_ref, do_ref, dz_ref, dg_ref):
        @pl.when(pl.program_id(0) == 0)
        def _():
            dg_ref[...] = jnp.zeros_like(dg_ref)

        gv = g_ref[...]
        acc = jnp.zeros((1, HD), F32)
        for h in range(NH):
            hc = slice(h * HD, (h + 1) * HD)
            ov = o_ref[:, hc]
            zv = z_ref[:, hc]
            dv = d_ref[:, hc]
            r = lax.rsqrt(jnp.mean(ov * ov, axis=-1, keepdims=True) + RMS_EPS)
            n = ov * r
            dz_ref[:, hc] = (dv * (n * gv) * _dsilu(zv)).astype(dz_ref.dtype)
            dng = dv * _silu(zv)
            acc = acc + jnp.sum(dng * n, axis=0, keepdims=True)
            dn = dng * gv
            do_ref[:, hc] = r * (dn - n * jnp.mean(dn * n, axis=-1, keepdims=True))
        dg_ref[...] += acc

    return pl.pallas_call(
        body, grid=(t // TR,), in_specs=[_row(D), _row(D, OFF_Z // D), _vec(HD), _row(D)],
        out_specs=[_row(D), _row(D), _vec(HD)],
        out_shape=[SDS((t, D), F32), SDS((t, D), BF16), SDS((1, HD), F32)],
        compiler_params=_params(("arbitrary",)), name=name,
    )(o, proj, gn, dob)


def _merge_fwd(ya, yb, proj, name):
    t = ya.shape[0]

    def body(a_ref, b_ref, gp_ref, gd_ref, o_ref):
        o_ref[...] = (_sigmoid(gp_ref[...]) * a_ref[...] + _sigmoid(gd_ref[...]) * b_ref[...]).astype(o_ref.dtype)

    return pl.pallas_call(
        body, grid=(t // TR,), in_specs=[_row(D), _row(D), _row(D, OFF_GP // D), _row(D, OFF_GD // D)],
        out_specs=_row(D), out_shape=SDS((t, D), BF16),
        compiler_params=_params(("parallel",)), name=name,
    )(ya, yb, proj, proj)


def _merge_bwd(dm, ya, yb, proj, name):
    t = ya.shape[0]

    def body(d_ref, a_ref, b_ref, gp_ref, gd_ref, da_ref, db_ref, dgp_ref, dgd_ref):
        dv = d_ref[...]
        sp = _sigmoid(gp_ref[...])
        sd = _sigmoid(gd_ref[...])
        da_ref[...] = dv * sp
        db_ref[...] = (dv * sd).astype(db_ref.dtype)
        dgp_ref[...] = (dv * a_ref[...] * sp * (1.0 - sp)).astype(dgp_ref.dtype)
        dgd_ref[...] = (dv * b_ref[...] * sd * (1.0 - sd)).astype(dgd_ref.dtype)

    return pl.pallas_call(
        body, grid=(t // TR,),
        in_specs=[_row(D), _row(D), _row(D), _row(D, OFF_GP // D), _row(D, OFF_GD // D)],
        out_specs=[_row(D)] * 4,
        out_shape=[SDS((t, D), F32), SDS((t, D), BF16), SDS((t, D), BF16), SDS((t, D), BF16)],
        compiler_params=_params(("parallel",)), name=name,
    )(dm, ya, yb, proj, proj)


def _split2(x):
    hi = x.astype(BF16)
    return hi, (x - hi.astype(F32)).astype(BF16)


def _dot3(a, b, dims):
    ah, al = _split2(a)
    bh, bl = _split2(b)
    return _dg(ah, bh, dims) + (_dg(ah, bl, dims) + _dg(al, bh, dims))


def _neumann_inverses(mats):
    ri = _iota((CH, CH), 0)
    ci = _iota((CH, CH), 1)
    eye = jnp.where(ri == ci, 1.0, 0.0).astype(F32)
    xs = [-a for a in mats]
    ps = [eye + x for x in xs]
    for _ in range(5):
        xs = [_dot3(x, x, NN) for x in xs]
        ps = [p + _dot3(p, x, NN) for p, x in zip(ps, xs)]
    return ps


def _solve_with(inv):
    @jax.custom_vjp
    def solve(a, rhs):
        return _dot3(inv, rhs, NN)

    def fwd(a, rhs):
        sol = _dot3(inv, rhs, NN)
        return sol, sol

    def bwd(sol, d):
        drhs = _dot3(inv, d, TN)
        return -_dot3(drhs, sol, NT), drhs

    solve.defvjp(fwd, bwd)
    return solve


@jax.custom_vjp
def _rows_to_lanes(g64):
    ri = _iota((CH, CH), 0)
    ci = _iota((CH, CH), 1)
    diag = jnp.where(ri == ci, g64, 0.0)
    ones = jnp.ones((CH, CH), BF16)
    hi = diag.astype(BF16)
    rem = diag - hi.astype(F32)
    mid = rem.astype(BF16)
    lo = (rem - mid.astype(F32)).astype(BF16)
    return _dg(ones, hi, NN) + (_dg(ones, mid, NN) + _dg(ones, lo, NN))


def _rows_to_lanes_bwd(_, d):
    ri = _iota((CH, CH), 0)
    ci = _iota((CH, CH), 1)
    return (jnp.where(ri == ci, jnp.broadcast_to(jnp.sum(d, axis=0, keepdims=True), (CH, CH)), 0.0),)


_rows_to_lanes.defvjp(lambda g64: (_rows_to_lanes(g64), None), _rows_to_lanes_bwd)


def _chunk_local(solve_all, q, k, v, g128, g64, gl128, b128, b64):
    ri = _iota((CH, CH), 0)
    ci = _iota((CH, CH), 1)
    causal = ri >= ci
    strict = ri > ci
    gj = [_rows_to_lanes(g) for g in g64]
    decay = [jnp.where(causal, jnp.exp(jnp.where(causal, g - t, 0.0)), 0.0) for g, t in zip(g64, gj)]
    kk = [_nt(x, x) for x in k]
    a = [jnp.where(strict, b * m * dc, 0.0) for b, m, dc in zip(b64, kk, decay)]
    eg = [jnp.exp(g) for g in g128]
    rhs = [jnp.concatenate([b * x, (b * e) * y], axis=1) for b, x, e, y in zip(b128, v, eg, k)]
    sol = solve_all(a, rhs)
    qk = [jnp.where(causal, _nt(x, y) * dc, 0.0) for x, y, dc in zip(q, k, decay)]
    return ([s[:, :HD] for s in sol], [s[:, HD:] for s in sol], qk, [x * e for x, e in zip(q, eg)],
            [x * jnp.exp(gl - g) for x, gl, g in zip(k, gl128, g128)], [jnp.exp(gl) for gl in gl128])


def _all_head_gates(bgv):
    return tuple(list(z) for z in zip(*[_head_gates(bgv, h) for h in range(NH)]))


def _head_gates(bgv, h):
    lane = _iota((CH, 128), 1)
    row = _iota((CH, 128), 0)
    bcol = jnp.sum(jnp.where(lane == h, bgv, 0.0), axis=1, keepdims=True)
    gcol = jnp.sum(jnp.where(lane == NH + h, bgv, 0.0), axis=1, keepdims=True)
    g128 = jnp.broadcast_to(gcol, (CH, 128))
    gl128 = jnp.broadcast_to(jnp.sum(jnp.where(row == CH - 1, g128, 0.0), axis=0, keepdims=True), (CH, 128))
    return (g128, jnp.broadcast_to(gcol, (CH, CH)), gl128,
            jnp.broadcast_to(bcol, (CH, 128)), jnp.broadcast_to(bcol, (CH, CH)))


def _chunk_specs():
    row = pl.BlockSpec((CH, D), lambda i: (i, 0))
    small = pl.BlockSpec((CH, 128), lambda i: (i, 0))
    qk = pl.BlockSpec((NH, CH, CH), lambda i: (i, 0, 0))
    eg = pl.BlockSpec((1, NH, 128), lambda i: (i, 0, 0))
    return row, small, qk, eg


def _dn_local_fwd(q, k, v, bg, name):
    t = q.shape[0]
    n = t // CH

    def body(q_ref, k_ref, v_ref, bg_ref, u_ref, w_ref, qk_ref, qd_ref, kd_ref, eg_ref, inv_ref):
        cols = [slice(h * HD, (h + 1) * HD) for h in range(NH)]

        def solve_all(mats, rhs):
            invs = _neumann_inverses(mats)
            for h in range(NH):
                inv_ref[h] = invs[h]
            return [_dot3(m, r, NN) for m, r in zip(invs, rhs)]

        u, w, qk, qd, kd, egl = _chunk_local(
            solve_all, [q_ref[:, c] for c in cols], [k_ref[:, c] for c in cols], [v_ref[:, c] for c in cols],
            *_all_head_gates(bg_ref[...]))
        for h, hc in enumerate(cols):
            u_ref[:, hc] = u[h]
            w_ref[:, hc] = w[h].astype(w_ref.dtype)
            qd_ref[:, hc] = qd[h].astype(qd_ref.dtype)
            kd_ref[:, hc] = kd[h].astype(kd_ref.dtype)
            qk_ref[h] = qk[h].astype(qk_ref.dtype)
            eg_ref[0, h:h + 1, :] = egl[h][0:1, :]

    row, small, qkb, egb = _chunk_specs()
    return pl.pallas_call(
        body, grid=(n,), in_specs=[row, row, row, small], out_specs=[row, row, qkb, row, row, egb, qkb],
        out_shape=[SDS((t, D), F32), SDS((t, D), BF16), SDS((n * NH, CH, CH), BF16), SDS((t, D), BF16),
                   SDS((t, D), BF16), SDS((n, NH, 128), F32), SDS((n * NH, CH, CH), F32)],
        compiler_params=_params(("parallel",)), name=name,
    )(q, k, v, bg)


def _dn_local_bwd(q, k, v, bg, inv, du, dw, dqk, dqd, dkd, deg, name):
    t = q.shape[0]
    n = t // CH

    def body(q_ref, k_ref, v_ref, bg_ref, inv_ref, du_ref, dw_ref, dqk_ref, dqd_ref, dkd_ref, deg_ref,
             dq_ref, dk_ref, dv_ref, dbg_ref):
        bgv = bg_ref[...]
        lane = _iota((CH, 128), 1)
        row = _iota((CH, 128), 0)
        first = jnp.where(row == 0, 1.0, 0.0)
        acc = jnp.zeros((CH, 128), F32)
        cols = [slice(h * HD, (h + 1) * HD) for h in range(NH)]
        solves = [_solve_with(inv_ref[h]) for h in range(NH)]

        def solve_all(mats, rhs):
            return [f(m, r) for f, m, r in zip(solves, mats, rhs)]

        _, vjp = jax.vjp(functools.partial(_chunk_local, solve_all),
                         [q_ref[:, c] for c in cols], [k_ref[:, c] for c in cols], [v_ref[:, c] for c in cols],
                         *_all_head_gates(bgv))
        cts = ([du_ref[:, c] for c in cols], [dw_ref[:, c] for c in cols], [dqk_ref[h] for h in range(NH)],
               [dqd_ref[:, c] for c in cols], [dkd_ref[:, c] for c in cols],
               [jnp.broadcast_to(deg_ref[0, h:h + 1, :], (CH, 128)) * first for h in range(NH)])
        dq, dk, dv, dg128, dg64, dgl, db128, db64 = vjp(cts)
        for h, hc in enumerate(cols):
            dq_ref[:, hc] = dq[h]
            dk_ref[:, hc] = dk[h]
            dv_ref[:, hc] = dv[h]
            dg = jnp.sum(dg128[h], axis=1, keepdims=True) + jnp.sum(dg64[h], axis=1, keepdims=True)
            tot = jnp.sum(jnp.sum(dgl[h], axis=0, keepdims=True), axis=1, keepdims=True)
            dg = dg + jnp.where(row[:, 0:1] == CH - 1, tot, 0.0)
            db = jnp.sum(db128[h], axis=1, keepdims=True) + jnp.sum(db64[h], axis=1, keepdims=True)
            acc = acc + jnp.where(lane == h, db, 0.0) + jnp.where(lane == NH + h, dg, 0.0)
        dbg_ref[...] = acc

    row, small, qkb, egb = _chunk_specs()
    return pl.pallas_call(
        body, grid=(n,), in_specs=[row, row, row, small, qkb, row, row, qkb, row, row, egb],
        out_specs=[row, row, row, small],
        out_shape=[SDS((t, D), F32)] * 3 + [SDS((t, 128), F32)],
        compiler_params=_params(("parallel",)), name=name,
    )(q, k, v, bg, inv, du, dw, dqk, dqd, dkd, deg)


def _state_step(s, u, w, qk, qd, kd, egl):
    ws = [_nn(a, b) for a, b in zip(w, s)]
    v_new = [a - b for a, b in zip(u, ws)]
    qs = [_nn(a, b) for a, b in zip(qd, s)]
    intra = [_nn(a, b) for a, b in zip(qk, v_new)]
    upd = [_tn(a, b) for a, b in zip(kd, v_new)]
    return [a * e + b for a, e, b in zip(s, egl, upd)], [a + b for a, b in zip(qs, intra)]


def _dn_scan_fwd(u, w, qk, qd, kd, eg, name):
    t = u.shape[0]
    n = t // CH

    def body(u_ref, w_ref, qk_ref, qd_ref, kd_ref, eg_ref, o_ref, save_ref, s_ref):
        @pl.when(pl.program_id(0) == 0)
        def _():
            s_ref[...] = jnp.zeros_like(s_ref)

        cols = [slice(h * HD, (h + 1) * HD) for h in range(NH)]
        s = [s_ref[h] for h in range(NH)]
        for h in range(NH):
            save_ref[0, h] = s[h]
        s_new, o = _state_step(
            s, [u_ref[:, c] for c in cols], [w_ref[:, c].astype(F32) for c in cols],
            [qk_ref[h].astype(F32) for h in range(NH)], [qd_ref[:, c].astype(F32) for c in cols],
            [kd_ref[:, c].astype(F32) for c in cols], [eg_ref[0, h:h + 1, :] for h in range(NH)])
        for h, hc in enumerate(cols):
            o_ref[:, hc] = o[h]
            s_ref[h] = s_new[h]

    row, _, qkb, egb = _chunk_specs()
    return pl.pallas_call(
        body, grid=(n,), in_specs=[row, row, qkb, row, row, egb],
        out_specs=[row, pl.BlockSpec((1, NH, HD, HD), lambda i: (i, 0, 0, 0))],
        out_shape=[SDS((t, D), F32), SDS((n, NH, HD, HD), F32)],
        scratch_shapes=[pltpu.VMEM((NH, HD, HD), F32)],
        compiler_params=_params(("arbitrary",)), name=name,
    )(u, w, qk, qd, kd, eg)


def _dn_scan_bwd(u, w, qk, qd, kd, eg, saved, do, name):
    t = u.shape[0]
    n = t // CH

    def body(u_ref, w_ref, qk_ref, qd_ref, kd_ref, eg_ref, sv_ref, do_ref,
             du_ref, dw_ref, dqk_ref, dqd_ref, dkd_ref, deg_ref, ds_ref):
        @pl.when(pl.program_id(0) == 0)
        def _():
            ds_ref[...] = jnp.zeros_like(ds_ref)

        cols = [slice(h * HD, (h + 1) * HD) for h in range(NH)]
        _, vjp = jax.vjp(
            _state_step, [sv_ref[0, h] for h in range(NH)], [u_ref[:, c] for c in cols],
            [w_ref[:, c].astype(F32) for c in cols], [qk_ref[h].astype(F32) for h in range(NH)],
            [qd_ref[:, c].astype(F32) for c in cols], [kd_ref[:, c].astype(F32) for c in cols],
            [eg_ref[0, h:h + 1, :] for h in range(NH)])
        ds, du, dw, dqk, dqd, dkd, deg = vjp(([ds_ref[h] for h in range(NH)], [do_ref[:, c] for c in cols]))
        for h, hc in enumerate(cols):
            ds_ref[h] = ds[h]
            du_ref[:, hc] = du[h]
            dw_ref[:, hc] = dw[h]
            dqk_ref[h] = dqk[h]
            dqd_ref[:, hc] = dqd[h]
            dkd_ref[:, hc] = dkd[h]
            deg_ref[0, h:h + 1, :] = deg[h]

    rev = lambda i: (n - 1 - i, 0)
    rev3 = lambda i: (n - 1 - i, 0, 0)
    row = pl.BlockSpec((CH, D), rev)
    qkb = pl.BlockSpec((NH, CH, CH), rev3)
    egb = pl.BlockSpec((1, NH, 128), rev3)
    return pl.pallas_call(
        body, grid=(n,),
        in_specs=[row, row, qkb, row, row, egb,
                  pl.BlockSpec((1, NH, HD, HD), lambda i: (n - 1 - i, 0, 0, 0)), row],
        out_specs=[row, row, qkb, row, row, egb],
        out_shape=[SDS((t, D), F32), SDS((t, D), F32), SDS((n * NH, CH, CH), F32), SDS((t, D), F32),
                   SDS((t, D), F32), SDS((n, NH, 128), F32)],
        scratch_shapes=[pltpu.VMEM((NH, HD, HD), F32)],
        compiler_params=_params(("arbitrary",)), name=name,
    )(u, w, qk, qd, kd, eg, saved, do)


def _ada_fwd(c_all, ada_w, ada_b, name):
    ncol = ada_w.shape[1]

    def body(c_ref, w_ref, b_ref, o_ref):
        o_ref[...] = _dg(_silu(c_ref[...]), w_ref[...], NN, HI) + b_ref[...]

    return pl.pallas_call(body, out_shape=SDS((NDEV, ncol), F32),
                          compiler_params=pltpu.CompilerParams(vmem_limit_bytes=VMEM_LIMIT), name=name,
                          )(c_all, ada_w, ada_b)


def _ada_bwd(c_all_t, dmod, name):
    ncol = dmod.shape[1]

    def body(c_ref, d_ref, o_ref):
        sc = _silu(c_ref[...])
        acc = sc[:, 0:1] * d_ref[0:1, :]
        for b in range(1, NDEV):
            acc = acc + sc[:, b:b + 1] * d_ref[b:b + 1, :]
        o_ref[...] = acc

    return pl.pallas_call(body, out_shape=SDS((D, ncol), F32),
                          compiler_params=pltpu.CompilerParams(vmem_limit_bytes=VMEM_LIMIT), name=name,
                          )(c_all_t, dmod)


def _sum_devices(parts, out_dtype, name):
    _, r, c = parts.shape
    tr = TR if r % TR == 0 else r

    def body(p_ref, o_ref):
        acc = p_ref[0].astype(F32)
        for i in range(1, NDEV):
            acc = acc + p_ref[i].astype(F32)
        o_ref[...] = acc.astype(o_ref.dtype)

    return pl.pallas_call(
        body, grid=(r // tr,), in_specs=[pl.BlockSpec((NDEV, tr, c), lambda i: (0, i, 0))],
        out_specs=pl.BlockSpec((tr, c), lambda i: (i, 0)), out_shape=SDS((r, c), out_dtype),
        compiler_params=_params(("parallel",)), name=name,
    )(parts)


def _adamw(w, g, m, v, name):
    r, c = w.shape
    tr = _pick(r, (256, 128, 88, 8)) if r % 8 == 0 else r
    bc1 = 1.0 - ADAM_B1 ** ADAM_STEP
    bc2 = 1.0 - ADAM_B2 ** ADAM_STEP

    def body(w_ref, g_ref, m_ref, v_ref, d_ref, nm_ref, nv_ref):
        gv = g_ref[...]
        m_new = ADAM_B1 * m_ref[...] + (1.0 - ADAM_B1) * gv
        v_new = ADAM_B2 * v_ref[...] + (1.0 - ADAM_B2) * (gv * gv)
        nm_ref[...] = m_new
        nv_ref[...] = v_new
        d_ref[...] = -ADAM_LR * ((m_new / bc1) / (jnp.sqrt(v_new / bc2) + ADAM_EPS) + ADAM_WD * w_ref[...])

    spec = pl.BlockSpec((tr, c), lambda i: (i, 0))
    return pl.pallas_call(
        body, grid=(r // tr,), in_specs=[spec] * 4, out_specs=[spec] * 3,
        out_shape=[SDS((r, c), F32)] * 3, compiler_params=_params(("parallel",)), name=name,
    )(w, g, m, v)


ANY = pl.BlockSpec(memory_space=pl.ANY)
MESH = pl.DeviceIdType.MESH


def _all_gather(xs, name, after=None):
    n = len(xs)
    extra = [] if after is None else [after]

    def body(*refs):
        x_refs, out_refs = refs[:n], refs[n + len(extra):2 * n + len(extra)]
        send_sems, recv_sems, local_sems = refs[-3:]
        mx, my, mc = lax.axis_index("x"), lax.axis_index("y"), lax.axis_index("c")
        me, sibling = (mx, my, mc), (mx, my, 1 - mc)
        chips = [(1 - mx, my), (mx, 1 - my), (1 - mx, 1 - my)]

        def rows(a, px, py, pc):
            return out_refs[a].at[4 * px + 2 * py + pc]

        def copy(a, k, block, to, src=None):
            return pltpu.make_async_remote_copy(
                src_ref=rows(a, *block) if src is None else src, dst_ref=rows(a, *block),
                send_sem=send_sems.at[a, k], recv_sem=recv_sems.at[a, k], device_id=to, device_id_type=MESH)

        mine = [pltpu.make_async_copy(x_refs[a], rows(a, *me), local_sems.at[a]) for a in range(n)]
        for cp in mine:
            cp.start()
        first = []
        for a in range(n):
            first.append(copy(a, 0, me, sibling, src=x_refs[a]))
            first += [copy(a, 1 + j, me, (*chip, mc), src=x_refs[a]) for j, chip in enumerate(chips)]
        for cp in first:
            cp.start()
        passed = []
        for a in range(n):
            for j, chip in enumerate(chips):
                copy(a, 1 + j, (*chip, mc), me).wait_recv()
                passed.append(copy(a, 4 + j, (*chip, mc), sibling))
                passed[-1].start()
        for a in range(n):
            copy(a, 0, sibling, me).wait_recv()
            for j, chip in enumerate(chips):
                copy(a, 4 + j, (*chip, 1 - mc), me).wait_recv()
        for cp in first + passed:
            cp.wait_send()
        for cp in mine:
            cp.wait()

    return pl.pallas_call(
        body, out_shape=[SDS((NDEV,) + x.shape, x.dtype) for x in xs], in_specs=[ANY] * (n + len(extra)),
        out_specs=[ANY] * n,
        scratch_shapes=[pltpu.SemaphoreType.DMA((n, 7)), pltpu.SemaphoreType.DMA((n, 7)),
                        pltpu.SemaphoreType.DMA((n,))],
        name=name,
    )(*xs, *extra)


HBM = pl.BlockSpec(memory_space=pltpu.HBM)
SEM = pl.BlockSpec(memory_space=pltpu.SEMAPHORE)
EFFECT = pltpu.SideEffectType.DATAFLOW_SIDE_EFFECTING


def _peers():
    mx, my, mc = lax.axis_index("x"), lax.axis_index("y"), lax.axis_index("c")
    out = []
    for k in range(1, NDEV):
        out.append((1 - mx if k & 4 else mx, 1 - my if k & 2 else my, 1 - mc if k & 1 else mc))
    return 4 * mx + 2 * my + mc, out


def _push_start(srcs, sliced, name, after=None):
    n = len(srcs)
    extra = [] if after is None else [after]
    me_idx = 4 * lax.axis_index("x") + 2 * lax.axis_index("y") + lax.axis_index("c")
    lands = []
    for s in srcs:
        blk = lax.dynamic_index_in_dim(s, me_idx, 0, keepdims=True) if sliced else s[None]
        shape = s.shape if sliced else (NDEV,) + s.shape
        lands.append(lax.dynamic_update_slice(lax.empty(shape, s.dtype), blk, (me_idx,) + (0,) * (len(shape) - 1)))

    def body(*refs):
        src_refs, land_refs = refs[:n], refs[n:2 * n]
        outs = refs[2 * n + len(extra):]
        send_sems, recv_sems = outs[:n], outs[n:2 * n]
        token = refs[-1]
        me, peers = _peers()
        for a in range(n):
            for k, (px, py, pc) in enumerate(peers):
                src = src_refs[a].at[4 * px + 2 * py + pc] if sliced else src_refs[a]
                pltpu.make_async_remote_copy(
                    src_ref=src, dst_ref=land_refs[a].at[me], send_sem=send_sems[a].at[k],
                    recv_sem=recv_sems[a].at[k], device_id=(px, py, pc), device_id_type=MESH).start()
        token[...] = jnp.zeros_like(token)

    outs = pl.pallas_call(
        body, name=name,
        out_shape=([pltpu.SemaphoreType.DMA((NDEV - 1,))] * (2 * n)
                   + [pltpu.HBM(s.shape, s.dtype) for s in srcs] + [pltpu.HBM(l.shape, l.dtype) for l in lands]
                   + [SDS((8, 128), F32)]),
        in_specs=[HBM] * (2 * n) + [pl.BlockSpec(memory_space=pl.ANY)] * len(extra),
        out_specs=[SEM] * (2 * n) + [HBM] * (2 * n) + [pl.BlockSpec(memory_space=pltpu.VMEM)],
        input_output_aliases={i: 2 * n + i for i in range(2 * n)},
        compiler_params=pltpu.CompilerParams(has_side_effects=EFFECT),
    )(*[pltpu.with_memory_space_constraint(s, pltpu.HBM) for s in srcs],
      *[pltpu.with_memory_space_constraint(l, pltpu.HBM) for l in lands], *extra)
    sends, recvs = outs[:n], outs[n:2 * n]
    src_thru, land_thru = outs[2 * n:3 * n], outs[3 * n:4 * n]
    return [(sends[a], recvs[a], src_thru[a], land_thru[a]) for a in range(n)], outs[-1]


def _push_wait(started, sliced, after, name):
    n = len(started)

    def body(*refs):
        src_refs, land_refs = refs[:n], refs[n:2 * n]
        send_sems, recv_sems = refs[2 * n:3 * n], refs[3 * n:4 * n]
        me, peers = _peers()
        for a in range(n):
            for k, (px, py, pc) in enumerate(peers):
                src = src_refs[a].at[4 * px + 2 * py + pc] if sliced else src_refs[a]
                cp = pltpu.make_async_remote_copy(
                    src_ref=src, dst_ref=land_refs[a].at[me], send_sem=send_sems[a].at[k],
                    recv_sem=recv_sems[a].at[k], device_id=(px, py, pc), device_id_type=MESH)
                cp.wait_send()
                cp.wait_recv()

    srcs = [s[2] for s in started]
    lands = [s[3] for s in started]
    outs = pl.pallas_call(
        body, name=name,
        out_shape=[pltpu.HBM(s.shape, s.dtype) for s in srcs] + [pltpu.HBM(l.shape, l.dtype) for l in lands],
        in_specs=[HBM] * (2 * n) + [SEM] * (2 * n) + [pl.BlockSpec(memory_space=pl.ANY)],
        out_specs=[HBM] * (2 * n),
        input_output_aliases={i: i for i in range(2 * n)},
        compiler_params=pltpu.CompilerParams(has_side_effects=EFFECT),
    )(*srcs, *lands, *[s[0] for s in started], *[s[1] for s in started], after)
    return outs[n:]


def _cols_from_blocks(blocks):
    _, rows, w = blocks.shape
    return blocks.transpose(1, 0, 2).reshape(rows, NDEV * w)


def _cols_to_blocks(full):
    rows, total = full.shape
    return full.reshape(rows, NDEV, total // NDEV).transpose(1, 0, 2)


def _mix_pad(w):
    rows = w.shape[0]
    xp, q, k, v, z, b, a, gp, gd = jnp.split(w, (512, 1536, 2560, 3584, 4608, 4616, 4624, 5648), axis=1)
    pad = jnp.zeros((rows, MIXP - OFF_BA - 16), w.dtype)
    return jnp.concatenate([q, k, v, z, gp, gd, xp, b, a, pad], axis=1)


def _mix_unpad(w):
    q, k, v, z, gp, gd, xp, b, a = (w[:, OFF_Q:OFF_K], w[:, OFF_K:OFF_V], w[:, OFF_V:OFF_Z], w[:, OFF_Z:OFF_GP],
                                    w[:, OFF_GP:OFF_GD], w[:, OFF_GD:OFF_XP], w[:, OFF_XP:OFF_BA],
                                    w[:, OFF_BA:OFF_BA + 8], w[:, OFF_BA + 8:OFF_BA + 16])
    return jnp.concatenate([xp, q, k, v, z, b, a, gp, gd], axis=1)


def _lane_row(vec8):
    return jnp.zeros((1, 128), F32).at[0, NH:2 * NH].set(vec8)


def _ffn_fwd(x, g, shift, scale, gate, w_in, w_out, tag, token=None):
    t = x.shape[0]
    h = _norm_mod_fwd(x, g, shift, scale, f"{tag}_norm")
    if isinstance(w_in, tuple):
        w_in, = _push_wait([w_in], False, h, f"{tag}_gather_wait_in")
    u = _matmul(h, w_in, b_blk=True, o_blk=True, out_dtype=F32, name=f"{tag}_up",
                after=token).reshape(2, NDEV // 2, t, FB)
    a = _swiglu_fwd(u, f"{tag}_act")
    w_out, = _push_wait([w_out], False, a, f"{tag}_gather_wait_out")
    w_out = w_out.reshape(FH, D)
    y = _matmul(a, w_out, a_blk=True, out_dtype=F32, name=f"{tag}_down")
    return _resid_fwd(x, y, gate, 0.5, f"{tag}_res"), (h, u, a, y), w_in, w_out


def _ffn_bwd(dx_out, x, g, scale, gate, w_in, w_out, saved, tag):
    h, u, a, y = saved
    t = x.shape[0]
    dy, dgate = _resid_bwd(dx_out, y, gate, 0.5, f"{tag}_res_bwd")
    dw_out = _matmul(a, dy, ta=True, a_blk=True, out_dtype=BF16, name=f"{tag}_down_dw")
    sent_out, token = _push_start([dw_out.reshape(NDEV, FH // NDEV, D)], True, f"{tag}_grad_start_out")
    da = _matmul(dy, w_out, tb=True, tn=FB, o_blk=True, out_dtype=F32, name=f"{tag}_down_dx", after=token)
    du = _swiglu_bwd(u, da, f"{tag}_act_bwd").reshape(NDEV, t, FB)
    dw_in = _matmul(h, du, ta=True, b_blk=True, o_blk=True, out_dtype=BF16, name=f"{tag}_up_dw")
    sent_in, token = _push_start([dw_in], True, f"{tag}_grad_start_in")
    dh = _matmul(du, w_in, tb=True, a_blk=True, b_blk=True, out_dtype=F32, name=f"{tag}_up_dx", after=token)
    dx, dshift, dscale, dg = _norm_mod_bwd(x, g, scale, dh, dx_out, f"{tag}_norm_bwd")
    return dx, (dshift, dscale, dgate), dg, sent_in + sent_out


def kernel(x, c, ada_w, ada_b, norm_g, ffn1_w_in, ffn1_w_out, ffn2_w_in, ffn2_w_out, mix_w_in, conv_w, a_log, dt_bias, dn_norm_g, pool_w, pool_scale, pool_proj, dn_proj, mix_w_out, final_g, loss_target, m_ada_w, m_ada_b, m_norm_g, m_ffn1_w_in, m_ffn1_w_out, m_ffn2_w_in, m_ffn2_w_out, m_mix_w_in, m_conv_w, m_a_log, m_dt_bias, m_dn_norm_g, m_pool_w, m_pool_scale, m_pool_proj, m_dn_proj, m_mix_w_out, m_final_g, v_ada_w, v_ada_b, v_norm_g, v_ffn1_w_in, v_ffn1_w_out, v_ffn2_w_in, v_ffn2_w_out, v_mix_w_in, v_conv_w, v_a_log, v_dt_bias, v_dn_norm_g, v_pool_w, v_pool_scale, v_pool_proj, v_dn_proj, v_mix_w_out, v_final_g):
    me = 4 * lax.axis_index("x") + 2 * lax.axis_index("y") + lax.axis_index("c")
    x0 = x[0]
    target = loss_target[0]
    t = x0.shape[0]

    big = [ffn1_w_in[0], ffn1_w_out[0], ffn2_w_in[0], ffn2_w_out[0], mix_w_in[0], pool_proj[0], dn_proj[0],
           mix_w_out[0]]
    small = jnp.concatenate([c.reshape(8, 128), conv_w[0].reshape(12, 128), norm_g[0].reshape(3, 128),
                             jnp.zeros((1, 128), F32)], axis=0)
    small_all, = _all_gather([small], "gather_small")
    c_all = small_all[:, 0:8, :].reshape(NDEV, D)
    conv_full = small_all[:, 8:20, :].reshape(NDEV, 4, 384).transpose(1, 0, 2).reshape(4, 3 * D)
    norm_full = small_all[:, 20:23, :].reshape(NDEV, 3, 128).transpose(1, 0, 2).reshape(3, D)

    ncol = ada_w.shape[2]
    ada_b_mine = lax.dynamic_slice(ada_b, (0, me * ncol), (1, ncol))
    mod_cols = _ada_fwd(c_all, ada_w[0], ada_b_mine, "ada_fwd")
    mod_all, w_in1 = _all_gather([mod_cols, big[0].astype(BF16)], "gather_mod_first_weight")
    order = [1, 4, 5, 6, 7, 2, 3]
    started, token = _push_start([big[i].astype(BF16) for i in order], False, "gather_start", after=mod_all)
    started = {i: s for i, s in zip(order, started)}
    mod = lax.dynamic_index_in_dim(mod_all, me, axis=1, keepdims=False).reshape(9, D)
    shift = [mod[3 * s:3 * s + 1] for s in range(3)]
    scale = [mod[3 * s + 1:3 * s + 2] for s in range(3)]
    gate = [mod[3 * s + 2:3 * s + 3] for s in range(3)]
    ng = [norm_full[s:s + 1] for s in range(3)]
    fg = final_g.reshape(1, D)
    al_row = _lane_row(a_log[0])
    dt_row = _lane_row(dt_bias[0])
    gn = dn_norm_g
    pw = pool_w[0]
    ps = pool_scale

    x1, saved1, w_in1, w_out1 = _ffn_fwd(x0, ng[0], shift[0], scale[0], gate[0], w_in1, started[1], "ffn1", token)

    h1 = _norm_mod_fwd(x1, ng[1], shift[1], scale[1], "mix_norm")
    seg = _push_wait([started[i] for i in (4, 5, 6, 7)], False, h1, "mix_gather_wait")
    w_mix = _mix_pad(_cols_from_blocks(seg[0]))
    w_pp = _cols_from_blocks(seg[1])
    w_dn = seg[2].reshape(D, D)
    w_mo = seg[3].reshape(D, D)
    proj = _matmul(h1, w_mix, out_dtype=F32, name="mix_in")
    ya = _pool_fwd(proj, pw, ps, w_pp, "pool_fwd")
    qh, kh, vh, bg = _dn_pre_fwd(proj, conv_full, al_row, dt_row, "dn_pre")
    u, w, qk, qd, kd, eg, inv = _dn_local_fwd(qh, kh, vh, bg, "dn_local")
    o, s_saved = _dn_scan_fwd(u, w, qk, qd, kd, eg, "dn_scan")
    ob = _dn_post_fwd(o, proj, gn, "dn_post")
    yb = _matmul(ob, w_dn, out_dtype=F32, name="dn_out")
    merged = _merge_fwd(ya, yb, proj, "merge")
    mix_y = _matmul(merged, w_mo, out_dtype=F32, name="mix_out")
    x2 = _resid_fwd(x1, mix_y, gate[1], 1.0, "mix_res")

    x3, saved2, w_in2, w_out2 = _ffn_fwd(x2, ng[2], shift[2], scale[2], gate[2], started[2], started[3], "ffn2")
    loss_row, dx3, dfg = _final_loss(x3, fg, target, "loss")

    dx2, dmod2, dng2, sent2 = _ffn_bwd(dx3, x2, ng[2], scale[2], gate[2], w_in2, w_out2, saved2, "ffn2")

    dmy, dgate1 = _resid_bwd(dx2, mix_y, gate[1], 1.0, "mix_res_bwd")
    dmerged = _matmul(dmy, w_mo, tb=True, out_dtype=F32, name="mix_out_dx")
    dw_mo = _matmul(merged, dmy, ta=True, out_dtype=BF16, name="mix_out_dw")
    dya, dyb, dgp, dgd = _merge_bwd(dmerged, ya, yb, proj, "merge_bwd")
    dob = _matmul(dyb, w_dn, tb=True, out_dtype=F32, name="dn_out_dx")
    dw_dn = _matmul(ob, dyb, ta=True, out_dtype=BF16, name="dn_out_dw")
    do, dz, dgn = _dn_post_bwd(o, proj, gn, dob, "dn_post_bwd")
    du, dw, dqk, dqd, dkd, deg = _dn_scan_bwd(u, w, qk, qd, kd, eg, s_saved, do, "dn_scan_bwd")
    dqh, dkh, dvh, dbg = _dn_local_bwd(qh, kh, vh, bg, inv, du, dw, dqk, dqd, dkd, deg, "dn_local_bwd")
    dconv, draw, dal, ddt = _dn_pre_bwd_act(proj, conv_full, al_row, dt_row, dqh, dkh, dvh, dbg, "dn_pre_bwd_act")
    dqkv, dcw = _dn_pre_bwd_conv(proj, conv_full, dconv, "dn_pre_bwd_conv")
    dwin, dpl, dpw, dps, dpp = _pool_bwd_local(proj, pw, ps, w_pp, dya, "pool_bwd_local")
    dxp = _pool_bwd_window(dwin, dpl, "pool_bwd_window")
    dproj = jnp.concatenate([dqkv, dz, dgp, dgd, dxp, draw, jnp.zeros((t, MIXP - OFF_BA - 128), BF16)], axis=1)
    dw_mix = _matmul(h1, dproj, ta=True, out_dtype=BF16, name="mix_in_dw")
    sent1, token = _push_start(
        [_cols_to_blocks(_mix_unpad(dw_mix)), _cols_to_blocks(dpp.astype(BF16)), dw_dn.reshape(NDEV, -1, D),
         dw_mo.reshape(NDEV, -1, D)], True, "mix_grad_start")
    dh1 = _matmul(dproj, w_mix, tb=True, out_dtype=F32, name="mix_in_dx", after=token)
    dx1, dsh1, dsc1, dng1 = _norm_mod_bwd(x1, ng[1], scale[1], dh1, dx2, "mix_norm_bwd")

    dx0, dmod0, dng0, sent0 = _ffn_bwd(dx1, x0, ng[0], scale[0], gate[0], w_in1, w_out1, saved1, "ffn1")

    dmod = jnp.concatenate([*dmod0, dsh1, dsc1, dgate1, *dmod2], axis=1).reshape(-1)
    flat = jnp.concatenate([
        dmod, dal[0, NH:2 * NH], ddt[0, NH:2 * NH], dgn.reshape(-1), dps.reshape(-1), dfg.reshape(-1),
        dpw.reshape(-1), jnp.concatenate([dng0, dng1, dng2], axis=0).reshape(-1), dcw.reshape(-1)])
    nflat = 90 * D
    flat = jnp.concatenate([flat, jnp.zeros((nflat - flat.shape[0],), F32)]).reshape(90, D)
    flat_all, = _all_gather([flat], "gather_small_grads")
    tot = _sum_devices(flat_all, F32, "sum_small_grads").reshape(-1)
    dmod_all = flat_all.reshape(NDEV, nflat)[:, :9 * D]
    dmod_cols = lax.dynamic_slice(dmod_all, (0, me * ncol), (NDEV, ncol))
    g_ada_w = _ada_bwd(c_all.T, dmod_cols, "ada_bwd")

    p = 0
    pieces = {}
    for nm, size in (("ada_b", 9 * D), ("a_log", NH), ("dt_bias", NH), ("dn_norm_g", HD), ("pool_scale", PW),
                     ("final_g", D), ("pool_w", 4 * PG * PG), ("norm_g", 3 * D), ("conv_w", 12 * D)):
        pieces[nm] = tot[p:p + size]
        p += size
    g_norm = lax.dynamic_slice(pieces["norm_g"].reshape(3, D), (0, me * 128), (3, 128))
    g_conv = lax.dynamic_slice(pieces["conv_w"].reshape(4, 3 * D), (0, me * 384), (4, 384))

    grads = {
        "ada_w": g_ada_w.reshape(ada_w.shape), "ada_b": pieces["ada_b"].reshape(ada_b.shape),
        "norm_g": g_norm.reshape(norm_g.shape), "conv_w": g_conv.reshape(conv_w.shape),
        "a_log": pieces["a_log"].reshape(a_log.shape), "dt_bias": pieces["dt_bias"].reshape(dt_bias.shape),
        "dn_norm_g": pieces["dn_norm_g"].reshape(dn_norm_g.shape), "pool_w": pieces["pool_w"].reshape(pool_w.shape),
        "pool_scale": pieces["pool_scale"].reshape(pool_scale.shape),
        "final_g": pieces["final_g"].reshape(final_g.shape),
    }
    weights = {"ada_w": ada_w, "ada_b": ada_b, "norm_g": norm_g, "ffn1_w_in": ffn1_w_in, "ffn1_w_out": ffn1_w_out,
               "ffn2_w_in": ffn2_w_in, "ffn2_w_out": ffn2_w_out, "mix_w_in": mix_w_in, "conv_w": conv_w,
               "a_log": a_log, "dt_bias": dt_bias, "dn_norm_g": dn_norm_g, "pool_w": pool_w,
               "pool_scale": pool_scale, "pool_proj": pool_proj, "dn_proj": dn_proj, "mix_w_out": mix_w_out,
               "final_g": final_g}
    m_in = {"ada_w": m_ada_w, "ada_b": m_ada_b, "norm_g": m_norm_g, "ffn1_w_in": m_ffn1_w_in,
            "ffn1_w_out": m_ffn1_w_out, "ffn2_w_in": m_ffn2_w_in, "ffn2_w_out": m_ffn2_w_out,
            "mix_w_in": m_mix_w_in, "conv_w": m_conv_w, "a_log": m_a_log, "dt_bias": m_dt_bias,
            "dn_norm_g": m_dn_norm_g, "pool_w": m_pool_w, "pool_scale": m_pool_scale, "pool_proj": m_pool_proj,
            "dn_proj": m_dn_proj, "mix_w_out": m_mix_w_out, "final_g": m_final_g}
    v_in = {"ada_w": v_ada_w, "ada_b": v_ada_b, "norm_g": v_norm_g, "ffn1_w_in": v_ffn1_w_in,
            "ffn1_w_out": v_ffn1_w_out, "ffn2_w_in": v_ffn2_w_in, "ffn2_w_out": v_ffn2_w_out,
            "mix_w_in": v_mix_w_in, "conv_w": v_conv_w, "a_log": v_a_log, "dt_bias": v_dt_bias,
            "dn_norm_g": v_dn_norm_g, "pool_w": v_pool_w, "pool_scale": v_pool_scale, "pool_proj": v_pool_proj,
            "dn_proj": v_dn_proj, "mix_w_out": v_mix_w_out, "final_g": v_final_g}

    names = list(weights)
    large = ("ada_w", "ffn1_w_in", "ffn1_w_out", "ffn2_w_in", "ffn2_w_out", "mix_w_in", "pool_proj", "dn_proj",
             "mix_w_out")
    delta, new_m, new_v = {}, {}, {}

    def update(nm):
        shp = weights[nm].shape
        two_d = (shp[-2], shp[-1])
        d_, m_, v_ = _adamw(weights[nm].reshape(two_d), grads[nm].reshape(two_d), m_in[nm].reshape(two_d),
                            v_in[nm].reshape(two_d), f"adamw_{nm}")
        delta[nm], new_m[nm], new_v[nm] = d_.reshape(shp), m_.reshape(shp), v_.reshape(shp)
        return d_

    def reduce(sent, group, after, tag):
        for nm, r in zip(group, _push_wait(sent, True, after, f"{tag}_grad_wait")):
            grads[nm] = _sum_devices(r, F32, f"sum_grads_{nm}").reshape(weights[nm].shape)

    done = update("ada_w")
    reduce(sent2, ("ffn2_w_in", "ffn2_w_out"), done, "ffn2")
    update("ffn2_w_in")
    done = update("ffn2_w_out")
    reduce(sent1, ("mix_w_in", "pool_proj", "dn_proj", "mix_w_out"), done, "mix")
    for nm in ("mix_w_in", "pool_proj", "dn_proj", "mix_w_out"):
        done = update(nm)
    reduce(sent0, ("ffn1_w_in", "ffn1_w_out"), done, "ffn1")
    update("ffn1_w_in")
    update("ffn1_w_out")
    rest = [nm for nm in names if nm not in large]
    total = sum(weights[nm].size for nm in rest)
    padded = -(-total // D) * D

    def pack(tree, fill):
        flat_ = jnp.concatenate([tree[nm].reshape(-1) for nm in rest])
        return jnp.concatenate([flat_, jnp.full((padded - total,), fill, F32)]).reshape(-1, D)

    d_, m_, v_ = _adamw(pack(weights, 0.0), pack(grads, 0.0), pack(m_in, 0.0), pack(v_in, 1.0), "adamw_small")
    p = 0
    for nm in rest:
        size = weights[nm].size
        shp = weights[nm].shape
        delta[nm] = d_.reshape(-1)[p:p + size].reshape(shp)
        new_m[nm] = m_.reshape(-1)[p:p + size].reshape(shp)
        new_v[nm] = v_.reshape(-1)[p:p + size].reshape(shp)
        p += size

    loss = lax.psum(loss_row[0, 0], ("x", "y", "c"))
    grad_x = dx0.reshape(x.shape)
    return (loss, grad_x, *[grads[nm] for nm in names], *[delta[nm] for nm in names],
            *[new_m[nm] for nm in names], *[new_v[nm] for nm in names])
```

```python
import functools

import jax
import jax.numpy as jnp
from jax import lax
from jax.experimental import pallas as pl
from jax.experimental.pallas import tpu as pltpu

F32 = jnp.float32
BF16 = jnp.bfloat16
SDS = jax.ShapeDtypeStruct
HI = lax.Precision.HIGHEST

D = 1024
FH = 2816
FB = 704
NH = 8
HD = 128
CH = 64
NDEV = 8
PW = 512
PG = 128
RMS_EPS = 1e-6
L2_EPS = 1e-6
TR = 256
HALO = 16
VMEM_LIMIT = 56 * 1024 * 1024

MIXP = 6912
OFF_Q, OFF_K, OFF_V, OFF_Z, OFF_GP, OFF_GD, OFF_XP, OFF_BA = 0, 1024, 2048, 3072, 4096, 5120, 6144, 6656
MIX_RAW = 6672

ADAM_LR = 0.001
ADAM_B1 = 0.9
ADAM_B2 = 0.999
ADAM_EPS = 1e-08
ADAM_WD = 0.01
ADAM_STEP = 10

NN = (((1,), (0,)), ((), ()))
NT = (((1,), (1,)), ((), ()))
TN = (((0,), (0,)), ((), ()))


def _dg(a, b, dims, prec=None):
    return lax.dot_general(a, b, dims, precision=prec, preferred_element_type=F32)


def _make_dots(prec):
    @jax.custom_vjp
    def nn(a, b):
        return _dg(a, b, NN, prec)

    @jax.custom_vjp
    def nt(a, b):
        return _dg(a, b, NT, prec)

    @jax.custom_vjp
    def tn(a, b):
        return _dg(a, b, TN, prec)

    nn.defvjp(lambda a, b: (nn(a, b), (a, b)), lambda r, d: (nt(d, r[1]), tn(r[0], d)))
    nt.defvjp(lambda a, b: (nt(a, b), (a, b)), lambda r, d: (nn(d, r[1]), tn(d, r[0])))
    tn.defvjp(lambda a, b: (tn(a, b), (a, b)), lambda r, d: (nt(r[1], d), nn(r[0], d)))
    return nn, nt, tn


_nn, _nt, _tn = _make_dots(None)


def _params(sem):
    return pltpu.CompilerParams(dimension_semantics=sem, vmem_limit_bytes=VMEM_LIMIT)


def _sigmoid(x):
    return 1.0 / (1.0 + jnp.exp(-x))


def _silu(x):
    return x * _sigmoid(x)


def _dsilu(x):
    s = _sigmoid(x)
    return s * (1.0 + x * (1.0 - s))


def _pick(n, cands):
    for c in cands:
        if n % c == 0:
            return c
    raise ValueError(f"no tile for {n}")


def _iota(shape, dim):
    return lax.broadcasted_iota(jnp.int32, shape, dim)


def _matmul(a, b, *, ta=False, tb=False, a_blk=False, b_blk=False, o_blk=False, tm=None, tn=None, tk=None,
            out_dtype, name, after=None):
    if a_blk:
        nb, r, cb = a.shape
        if ta:
            k_dim, m_dim, tm = r, nb * cb, cb
        else:
            m_dim, k_dim, tk = r, nb * cb, cb
    else:
        k_dim, m_dim = a.shape if ta else a.shape[::-1]
    if b_blk:
        nb, r, cb = b.shape
        if tb:
            n_dim, tk = r, cb
            assert nb * cb == k_dim
        else:
            n_dim, tn = nb * cb, cb
            assert r == k_dim
    else:
        n_dim = b.shape[0] if tb else b.shape[1]
    tm = tm or _pick(m_dim, (1024, 512, 256, 128))
    tn = tn or _pick(n_dim, (1024, 768, 512, 256, 128))
    tk = tk or (k_dim if (k_dim <= 2816 and not ta) else _pick(k_dim, (2816, 2304, 1024, 512, 256)))
    nk = k_dim // tk
    dims = ((((0,) if ta else (1,)), ((1,) if tb else (0,))), ((), ()))

    def body(a_ref, b_ref, *rest):
        o_ref, acc_ref = rest[-2:]
        k = pl.program_id(2)

        @pl.when(k == 0)
        def _():
            acc_ref[...] = jnp.zeros_like(acc_ref)

        acc_ref[...] += lax.dot_general(a_ref[...].astype(BF16), b_ref[...].astype(BF16), dims,
                                        preferred_element_type=F32)

        @pl.when(k == nk - 1)
        def _():
            o_ref[...] = acc_ref[...].astype(o_ref.dtype)

    if a_blk:
        a_spec = (pl.BlockSpec((None, tk, tm), lambda i, j, k: (i, k, 0)) if ta
                  else pl.BlockSpec((None, tm, tk), lambda i, j, k: (k, i, 0)))
    else:
        a_spec = (pl.BlockSpec((tk, tm), lambda i, j, k: (k, i)) if ta
                  else pl.BlockSpec((tm, tk), lambda i, j, k: (i, k)))
    if b_blk:
        b_spec = (pl.BlockSpec((None, tn, tk), lambda i, j, k: (k, j, 0)) if tb
                  else pl.BlockSpec((None, tk, tn), lambda i, j, k: (j, k, 0)))
    else:
        b_spec = (pl.BlockSpec((tn, tk), lambda i, j, k: (j, k)) if tb
                  else pl.BlockSpec((tk, tn), lambda i, j, k: (k, j)))
    if o_blk:
        o_spec = pl.BlockSpec((None, tm, tn), lambda i, j, k: (j, i, 0))
        o_shape = SDS((n_dim // tn, m_dim, tn), out_dtype)
    else:
        o_spec = pl.BlockSpec((tm, tn), lambda i, j, k: (i, j))
        o_shape = SDS((m_dim, n_dim), out_dtype)
    return pl.pallas_call(
        body, grid=(m_dim // tm, n_dim // tn, nk),
        in_specs=[a_spec, b_spec] + ([] if after is None else [pl.BlockSpec(memory_space=pl.ANY)]),
        out_specs=o_spec,
        out_shape=o_shape,
        scratch_shapes=[pltpu.VMEM((tm, tn), F32)],
        compiler_params=_params(("parallel", "parallel", "arbitrary")),
        name=name,
    )(a, b, *([] if after is None else [after]))


def _row(width, col=0):
    return pl.BlockSpec((TR, width), lambda i: (i, col))


def _vec(width):
    return pl.BlockSpec((1, width), lambda i: (0, 0))


def _norm_mod_fwd(x, g, shift, scale, name):
    t = x.shape[0]

    def body(x_ref, g_ref, sh_ref, sc_ref, o_ref):
        xv = x_ref[...]
        r = lax.rsqrt(jnp.mean(xv * xv, axis=-1, keepdims=True) + RMS_EPS)
        o_ref[...] = (((xv * r) * g_ref[...]) * (1.0 + sc_ref[...]) + sh_ref[...]).astype(o_ref.dtype)

    return pl.pallas_call(
        body, grid=(t // TR,), in_specs=[_row(D), _vec(D), _vec(D), _vec(D)], out_specs=_row(D),
        out_shape=SDS((t, D), BF16), compiler_params=_params(("parallel",)), name=name,
    )(x, g, shift, scale)


def _norm_mod_bwd(x, g, scale, dh, dx_in, name):
    t = x.shape[0]

    def body(x_ref, g_ref, sc_ref, dh_ref, dxi_ref, dx_ref, dsh_ref, dsc_ref, dg_ref):
        @pl.when(pl.program_id(0) == 0)
        def _():
            dsh_ref[...] = jnp.zeros_like(dsh_ref)
            dsc_ref[...] = jnp.zeros_like(dsc_ref)
            dg_ref[...] = jnp.zeros_like(dg_ref)

        xv = x_ref[...]
        gv = g_ref[...]
        dh = dh_ref[...]
        r = lax.rsqrt(jnp.mean(xv * xv, axis=-1, keepdims=True) + RMS_EPS)
        n = xv * r
        dsh_ref[...] += jnp.sum(dh, axis=0, keepdims=True)
        dsc_ref[...] += jnp.sum(dh * (n * gv), axis=0, keepdims=True)
        tt = dh * (1.0 + sc_ref[...])
        dg_ref[...] += jnp.sum(tt * n, axis=0, keepdims=True)
        dn = tt * gv
        dx_ref[...] = dxi_ref[...] + r * (dn - n * jnp.mean(dn * n, axis=-1, keepdims=True))

    return pl.pallas_call(
        body, grid=(t // TR,), in_specs=[_row(D), _vec(D), _vec(D), _row(D), _row(D)],
        out_specs=[_row(D), _vec(D), _vec(D), _vec(D)],
        out_shape=[SDS((t, D), F32), SDS((1, D), F32), SDS((1, D), F32), SDS((1, D), F32)],
        compiler_params=_params(("arbitrary",)), name=name,
    )(x, g, scale, dh, dx_in)


def _swiglu_up(h, w_in, name, after=None):
    t = h.shape[0]
    tm = _pick(t, (1024, 512, 256))
    half = NDEV // 2
    extra = [] if after is None else [after]

    def body(h_ref, wg_ref, wu_ref, *rest):
        u_ref, a_ref = rest[-2:]
        hv = h_ref[...]
        gate = _dg(hv, wg_ref[...], NN)
        up = _dg(hv, wu_ref[...], NN)
        u_ref[0] = gate.astype(u_ref.dtype)
        u_ref[1] = up.astype(u_ref.dtype)
        a_ref[...] = (_silu(gate) * up).astype(a_ref.dtype)

    return pl.pallas_call(
        body, grid=(t // tm, half),
        in_specs=[pl.BlockSpec((tm, D), lambda i, j: (i, 0)),
                  pl.BlockSpec((None, D, FB), lambda i, j: (j, 0, 0)),
                  pl.BlockSpec((None, D, FB), lambda i, j: (j + half, 0, 0))]
        + [pl.BlockSpec(memory_space=pl.ANY)] * len(extra),
        out_specs=[pl.BlockSpec((2, None, tm, FB), lambda i, j: (0, j, i, 0)),
                   pl.BlockSpec((None, tm, FB), lambda i, j: (j, i, 0))],
        out_shape=[SDS((2, half, t, FB), BF16), SDS((half, t, FB), BF16)],
        compiler_params=_params(("parallel", "parallel")), name=name,
    )(h, w_in, w_in, *extra)


def _swiglu_down_bwd(dy, w_out, u, name, after=None):
    t = dy.shape[0]
    tm = _pick(t, (1024, 512, 256))
    half = NDEV // 2
    extra = [] if after is None else [after]
    pair = pl.BlockSpec((2, None, tm, FB), lambda i, j: (0, j, i, 0))

    def body(dy_ref, w_ref, u_ref, *rest):
        o_ref = rest[-1]
        da = _dg(dy_ref[...], w_ref[...], NT)
        gate = u_ref[0].astype(F32)
        o_ref[0] = (da * u_ref[1].astype(F32) * _dsilu(gate)).astype(o_ref.dtype)
        o_ref[1] = (da * _silu(gate)).astype(o_ref.dtype)

    return pl.pallas_call(
        body, grid=(t // tm, half),
        in_specs=[pl.BlockSpec((tm, D), lambda i, j: (i, 0)), pl.BlockSpec((FB, D), lambda i, j: (j, 0)), pair]
        + [pl.BlockSpec(memory_space=pl.ANY)] * len(extra),
        out_specs=pair, out_shape=SDS((2, half, t, FB), BF16),
        compiler_params=_params(("parallel", "parallel")), name=name,
    )(dy, w_out, u, *extra)


def _resid_fwd(x, y, gate, coef, name):
    t = x.shape[0]

    def body(x_ref, y_ref, g_ref, o_ref):
        o_ref[...] = x_ref[...] + (coef * g_ref[...]) * y_ref[...]

    return pl.pallas_call(
        body, grid=(t // TR,), in_specs=[_row(D), _row(D), _vec(D)], out_specs=_row(D),
        out_shape=SDS((t, D), F32), compiler_params=_params(("parallel",)), name=name,
    )(x, y, gate)


def _resid_bwd(dx, y, gate, coef, name):
    t = dx.shape[0]

    def body(dx_ref, y_ref, g_ref, dy_ref, dg_ref):
        @pl.when(pl.program_id(0) == 0)
        def _():
            dg_ref[...] = jnp.zeros_like(dg_ref)

        dxv = dx_ref[...]
        dy_ref[...] = ((coef * g_ref[...]) * dxv).astype(dy_ref.dtype)
        dg_ref[...] += jnp.sum((coef * dxv) * y_ref[...], axis=0, keepdims=True)

    return pl.pallas_call(
        body, grid=(t // TR,), in_specs=[_row(D), _row(D), _vec(D)], out_specs=[_row(D), _vec(D)],
        out_shape=[SDS((t, D), BF16), SDS((1, D), F32)],
        compiler_params=_params(("arbitrary",)), name=name,
    )(dx, y, gate)


def _final_loss(x, fg, target, name):
    t = x.shape[0]
    nt = t // TR

    def body(x_ref, g_ref, t_ref, loss_ref, dx_ref, dg_ref, acc_ref):
        i = pl.program_id(0)

        @pl.when(i == 0)
        def _():
            acc_ref[...] = jnp.zeros_like(acc_ref)
            dg_ref[...] = jnp.zeros_like(dg_ref)

        xv = x_ref[...]
        gv = g_ref[...]
        r = lax.rsqrt(jnp.mean(xv * xv, axis=-1, keepdims=True) + RMS_EPS)
        n = xv * r
        err = n * gv - t_ref[...]
        acc_ref[...] += jnp.sum(err * err, axis=0, keepdims=True)
        dy = err * (1.0 / D)
        dg_ref[...] += jnp.sum(dy * n, axis=0, keepdims=True)
        dn = dy * gv
        dx_ref[...] = r * (dn - n * jnp.mean(dn * n, axis=-1, keepdims=True))

        @pl.when(i == nt - 1)
        def _():
            tot = jnp.sum(acc_ref[...], axis=1, keepdims=True) * (0.5 / D)
            loss_ref[...] = jnp.broadcast_to(tot, loss_ref.shape)

    return pl.pallas_call(
        body, grid=(nt,), in_specs=[_row(D), _vec(D), _row(D)],
        out_specs=[_vec(128), _row(D), _vec(D)],
        out_shape=[SDS((1, 128), F32), SDS((t, D), F32), SDS((1, D), F32)],
        scratch_shapes=[pltpu.VMEM((1, D), F32)],
        compiler_params=_params(("arbitrary",)), name=name,
    )(x, fg, target)


def _halo_prev(width, col):
    per = TR // HALO
    return pl.BlockSpec((HALO, width), lambda i: (jnp.maximum(i * per - 1, 0), col))


def _halo_next(width, col, nt):
    per = TR // HALO
    return pl.BlockSpec((HALO, width), lambda i: (jnp.minimum((i + 1) * per, nt * per - 1), col))


def _pool_windows(ext, tile_index):
    rows = _iota((TR, PG), 0) + tile_index * TR + 1
    pooled, counts = [], []
    for gi in range(4):
        w = 2 << gi
        e = ext[:, gi * PG:(gi + 1) * PG]
        s = e
        step = 1
        while step < w:
            s = s + pltpu.roll(s, step, 0)
            step *= 2
        cnt = jnp.minimum(rows, w).astype(F32)
        pooled.append(s[HALO:] / cnt - e[HALO:])
        counts.append(cnt)
    return pooled, counts


def _pool_fwd(proj, pool_w, pool_scale, pool_proj, name):
    t = proj.shape[0]
    xcol = OFF_XP // PW

    def body(x_ref, h_ref, pw_ref, ps_ref, pp_ref, o_ref):
        i = pl.program_id(0)
        halo = jnp.where(i > 0, h_ref[...], 0.0)
        ext = jnp.concatenate([halo, x_ref[...]], axis=0)
        pooled, _ = _pool_windows(ext, i)
        mixed = [_dg(pooled[g].astype(BF16), pw_ref[g].astype(BF16), NN) for g in range(4)]
        ypre = jnp.concatenate(mixed, axis=1) * ps_ref[...]
        o_ref[...] = _dg(ypre.astype(BF16), pp_ref[...], NN)

    return pl.pallas_call(
        body, grid=(t // TR,),
        in_specs=[_row(PW, xcol), _halo_prev(PW, xcol),
                  pl.BlockSpec((4, PG, PG), lambda i: (0, 0, 0)), _vec(PW),
                  pl.BlockSpec((PW, D), lambda i: (0, 0))],
        out_specs=_row(D), out_shape=SDS((t, D), F32),
        compiler_params=_params(("parallel",)), name=name,
    )(proj, proj, pool_w, pool_scale, pool_proj)


def _pool_bwd_local(proj, pool_w, pool_scale, pool_proj, dya, name):
    t = proj.shape[0]
    xcol = OFF_XP // PW

    def body(x_ref, h_ref, pw_ref, ps_ref, pp_ref, dya_ref, dwin_ref, dpl_ref, dpw_ref, dps_ref, dpp_ref):
        i = pl.program_id(0)

        @pl.when(i == 0)
        def _():
            dpw_ref[...] = jnp.zeros_like(dpw_ref)
            dps_ref[...] = jnp.zeros_like(dps_ref)
            dpp_ref[...] = jnp.zeros_like(dpp_ref)

        halo = jnp.where(i > 0, h_ref[...], 0.0)
        ext = jnp.concatenate([halo, x_ref[...]], axis=0)
        pooled, counts = _pool_windows(ext, i)
        mixed = jnp.concatenate(
            [_dg(pooled[g].astype(BF16), pw_ref[g].astype(BF16), NN) for g in range(4)], axis=1)
        ps = ps_ref[...]
        ypre = mixed * ps
        dyab = dya_ref[...].astype(BF16)
        dypre = _dg(dyab, pp_ref[...], NT)
        dpp_ref[...] += _dg(ypre.astype(BF16), dyab, TN)
        dps_ref[...] += jnp.sum(dypre * mixed, axis=0, keepdims=True)
        dmixed = dypre * ps
        for g in range(4):
            dm = dmixed[:, g * PG:(g + 1) * PG].astype(BF16)
            dpw_ref[g] += _dg(pooled[g].astype(BF16), dm, TN)
            dpooled = _dg(dm, pw_ref[g].astype(BF16), NT)
            dwin_ref[:, g * PG:(g + 1) * PG] = dpooled / counts[g]
            dpl_ref[:, g * PG:(g + 1) * PG] = dpooled

    return pl.pallas_call(
        body, grid=(t // TR,),
        in_specs=[_row(PW, xcol), _halo_prev(PW, xcol),
                  pl.BlockSpec((4, PG, PG), lambda i: (0, 0, 0)), _vec(PW),
                  pl.BlockSpec((PW, D), lambda i: (0, 0)), _row(D)],
        out_specs=[_row(PW), _row(PW), pl.BlockSpec((4, PG, PG), lambda i: (0, 0, 0)), _vec(PW),
                   pl.BlockSpec((PW, D), lambda i: (0, 0))],
        out_shape=[SDS((t, PW), F32), SDS((t, PW), F32), SDS((4, PG, PG), F32), SDS((1, PW), F32),
                   SDS((PW, D), F32)],
        compiler_params=_params(("arbitrary",)), name=name,
    )(proj, proj, pool_w, pool_scale, pool_proj, dya)


def _pool_bwd_window(dwin, dpl, name):
    t = dwin.shape[0]
    nt = t // TR
    ext_rows = TR + HALO

    def body(dw_ref, h_ref, dp_ref, o_ref):
        i = pl.program_id(0)
        halo = jnp.where(i < nt - 1, h_ref[...], 0.0)
        ext = jnp.concatenate([dw_ref[...], halo], axis=0)
        for gi in range(4):
            w = 2 << gi
            s = ext[:, gi * PG:(gi + 1) * PG]
            step = 1
            while step < w:
                s = s + pltpu.roll(s, ext_rows - step, 0)
                step *= 2
            o_ref[:, gi * PG:(gi + 1) * PG] = (s[:TR] - dp_ref[:, gi * PG:(gi + 1) * PG]).astype(o_ref.dtype)

    return pl.pallas_call(
        body, grid=(nt,), in_specs=[_row(PW), _halo_next(PW, 0, nt), _row(PW)], out_specs=_row(PW),
        out_shape=SDS((t, PW), BF16), compiler_params=_params(("parallel",)), name=name,
    )(dwin, dwin, dpl)


def _conv_group(ext, cw_ref, cols):
    acc = cw_ref[3:4, cols] * ext
    for j in range(3):
        acc = acc + cw_ref[j:j + 1, cols] * pltpu.roll(ext, 3 - j, 0)
    return acc[HALO:]


def _gate_terms(raw, al, dt):
    beta = _sigmoid(raw)
    xg = raw + dt
    sp = jnp.maximum(xg, 0.0) + jnp.log(1.0 + jnp.exp(-jnp.abs(xg)))
    g = -jnp.exp(al) * sp
    return beta, g, _sigmoid(xg)


def _dn_pre_fwd(proj, conv_w, al_row, dt_row, name):
    t = proj.shape[0]

    def body(x_ref, h_ref, cw_ref, ba_ref, al_ref, dt_ref, q_ref, k_ref, v_ref, bg_ref):
        i = pl.program_id(0)
        keep = i > 0
        for grp in range(24):
            cols = slice(grp * HD, (grp + 1) * HD)
            ext = jnp.concatenate([jnp.where(keep, h_ref[:, cols], 0.0), x_ref[:, cols]], axis=0)
            s = _silu(_conv_group(ext, cw_ref, cols))
            seg, head = divmod(grp, NH)
            hc = slice(head * HD, (head + 1) * HD)
            if seg == 0:
                q_ref[:, hc] = s * lax.rsqrt(jnp.sum(s * s, axis=-1, keepdims=True) + L2_EPS) * (HD ** -0.5)
            elif seg == 1:
                k_ref[:, hc] = s * lax.rsqrt(jnp.sum(s * s, axis=-1, keepdims=True) + L2_EPS)
            else:
                v_ref[:, hc] = s
        lane = _iota((TR, 128), 1)
        rowc = _iota((TR, 128), 0) % CH
        beta, g, _ = _gate_terms(ba_ref[...], al_ref[...], dt_ref[...])
        step = 1
        while step < CH:
            g = g + jnp.where(rowc >= step, pltpu.roll(g, step, 0), 0.0)
            step *= 2
        bg_ref[...] = jnp.where(lane < NH, beta, jnp.where(lane < 2 * NH, g, 0.0))

    return pl.pallas_call(
        body, grid=(t // TR,),
        in_specs=[_row(3 * D, 0), _halo_prev(3 * D, 0), pl.BlockSpec((4, 3 * D), lambda i: (0, 0)),
                  _row(128, OFF_BA // 128), _vec(128), _vec(128)],
        out_specs=[_row(D), _row(D), _row(D), _row(128)],
        out_shape=[SDS((t, D), F32), SDS((t, D), F32), SDS((t, D), F32), SDS((t, 128), F32)],
        compiler_params=_params(("parallel",)), name=name,
    )(proj, proj, conv_w, proj, al_row, dt_row)


def _dn_pre_bwd_act(proj, conv_w, al_row, dt_row, dq, dk, dv, dbg, name):
    t = proj.shape[0]

    def body(x_ref, h_ref, cw_ref, ba_ref, al_ref, dt_ref, dq_ref, dk_ref, dv_ref, dbg_ref,
             dc_ref, draw_ref, dal_ref, ddt_ref):
        i = pl.program_id(0)

        @pl.when(i == 0)
        def _():
            dal_ref[...] = jnp.zeros_like(dal_ref)
            ddt_ref[...] = jnp.zeros_like(ddt_ref)

        keep = i > 0
        for grp in range(24):
            cols = slice(grp * HD, (grp + 1) * HD)
            ext = jnp.concatenate([jnp.where(keep, h_ref[:, cols], 0.0), x_ref[:, cols]], axis=0)
            cv = _conv_group(ext, cw_ref, cols)
            seg, head = divmod(grp, NH)
            hc = slice(head * HD, (head + 1) * HD)
            if seg == 2:
                ds = dv_ref[:, hc]
            else:
                s = _silu(cv)
                r = lax.rsqrt(jnp.sum(s * s, axis=-1, keepdims=True) + L2_EPS)
                dy = dq_ref[:, hc] if seg == 0 else dk_ref[:, hc]
                c = (HD ** -0.5) if seg == 0 else 1.0
                ds = (c * r) * (dy - s * ((r * r) * jnp.sum(dy * s, axis=-1, keepdims=True)))
            dc_ref[:, cols] = ds * _dsilu(cv)
        lane = _iota((TR, 128), 1)
        rowc = _iota((TR, 128), 0) % CH
        isb = lane < NH
        isg = jnp.logical_and(lane >= NH, lane < 2 * NH)
        beta, g, sg = _gate_terms(ba_ref[...], al_ref[...], dt_ref[...])
        dbgv = dbg_ref[...]
        dg = dbgv
        step = 1
        while step < CH:
            dg = dg + jnp.where(rowc < CH - step, pltpu.roll(dg, TR - step, 0), 0.0)
            step *= 2
        da_raw = dg * (-jnp.exp(al_ref[...])) * sg
        draw_ref[...] = jnp.where(isb, dbgv * beta * (1.0 - beta), jnp.where(isg, da_raw, 0.0)).astype(draw_ref.dtype)
        dal_ref[...] += jnp.sum(jnp.where(isg, dg * g, 0.0), axis=0, keepdims=True)
        ddt_ref[...] += jnp.sum(jnp.where(isg, da_raw, 0.0), axis=0, keepdims=True)

    return pl.pallas_call(
        body, grid=(t // TR,),
        in_specs=[_row(3 * D, 0), _halo_prev(3 * D, 0), pl.BlockSpec((4, 3 * D), lambda i: (0, 0)),
                  _row(128, OFF_BA // 128), _vec(128), _vec(128), _row(D), _row(D), _row(D), _row(128)],
        out_specs=[_row(3 * D), _row(128), _vec(128), _vec(128)],
        out_shape=[SDS((t, 3 * D), F32), SDS((t, 128), BF16), SDS((1, 128), F32), SDS((1, 128), F32)],
        compiler_params=_params(("arbitrary",)), name=name,
    )(proj, proj, conv_w, proj, al_row, dt_row, dq, dk, dv, dbg)


def _dn_pre_bwd_conv(proj, conv_w, dconv, name):
    t = proj.shape[0]
    nt = t // TR
    ext_rows = TR + HALO

    def body(x_ref, h_ref, cw_ref, dc_ref, dn_ref, dx_ref, dcw_ref):
        i = pl.program_id(0)

        @pl.when(i == 0)
        def _():
            dcw_ref[...] = jnp.zeros_like(dcw_ref)

        keep_prev = i > 0
        keep_next = i < nt - 1
        for grp in range(24):
            cols = slice(grp * HD, (grp + 1) * HD)
            dct = dc_ref[:, cols]
            dext = jnp.concatenate([dct, jnp.where(keep_next, dn_ref[:, cols], 0.0)], axis=0)
            acc = cw_ref[3:4, cols] * dext
            for j in range(3):
                acc = acc + cw_ref[j:j + 1, cols] * pltpu.roll(dext, ext_rows - (3 - j), 0)
            dx_ref[:, cols] = acc[:TR].astype(dx_ref.dtype)
            xext = jnp.concatenate([jnp.where(keep_prev, h_ref[:, cols], 0.0), x_ref[:, cols]], axis=0)
            for j in range(4):
                xs = xext if j == 3 else pltpu.roll(xext, 3 - j, 0)
                dcw_ref[j:j + 1, cols] += jnp.sum(xs[HALO:] * dct, axis=0, keepdims=True)

    return pl.pallas_call(
        body, grid=(nt,),
        in_specs=[_row(3 * D, 0), _halo_prev(3 * D, 0), pl.BlockSpec((4, 3 * D), lambda i: (0, 0)),
                  _row(3 * D), _halo_next(3 * D, 0, nt)],
        out_specs=[_row(3 * D), pl.BlockSpec((4, 3 * D), lambda i: (0, 0))],
        out_shape=[SDS((t, 3 * D), BF16), SDS((4, 3 * D), F32)],
        compiler_params=_params(("arbitrary",)), name=name,
    )(proj, proj, conv_w, dconv, dconv)


def _dn_post_fwd(o, proj, gn, name):
    t = o.shape[0]

    def body(o_ref, z_ref, g_ref, out_ref):
        gv = g_ref[...]
        for h in range(NH):
            hc = slice(h * HD, (h + 1) * HD)
            ov = o_ref[:, hc]
            r = lax.rsqrt(jnp.mean(ov * ov, axis=-1, keepdims=True) + RMS_EPS)
            out_ref[:, hc] = (((ov * r) * gv) * _silu(z_ref[:, hc])).astype(out_ref.dtype)

    return pl.pallas_call(
        body, grid=(t // TR,), in_specs=[_row(D), _row(D, OFF_Z // D), _vec(HD)], out_specs=_row(D),
        out_shape=SDS((t, D), BF16), compiler_params=_params(("parallel",)), name=name,
    )(o, proj, gn)


def _dn_post_bwd(o, proj, gn, dob, name):
    t = o.shape[0]

    def body(o_ref, z_ref, g_ref, d_ref, do_ref, dz_ref, dg_ref):
        @pl.when(pl.program_id(0) == 0)
        def _():
            dg_ref[...] = jnp.zeros_like(dg_ref)

        gv = g_ref[...]
        acc = jnp.zeros((1, HD), F32)
        for h in range(NH):
            hc = slice(h * HD, (h + 1) * HD)
            ov = o_ref[:, hc]
            zv = z_ref[:, hc]
            dv = d_ref[:, hc]
            r = lax.rsqrt(jnp.mean(ov * ov, axis=-1, keepdims=True) + RMS_EPS)
            n = ov * r
            dz_ref[:, hc] = (dv * (n * gv) * _dsilu(zv)).astype(dz_ref.dtype)
            dng = dv * _silu(zv)
            acc = acc + jnp.sum(dng * n, axis=0, keepdims=True)
            dn = dng * gv
            do_ref[:, hc] = r * (dn - n * jnp.mean(dn * n, axis=-1, keepdims=True))
        dg_ref[...] += acc

    return pl.pallas_call(
        body, grid=(t // TR,), in_specs=[_row(D), _row(D, OFF_Z // D), _vec(HD), _row(D)],
        out_specs=[_row(D), _row(D), _vec(HD)],
        out_shape=[SDS((t, D), F32), SDS((t, D), BF16), SDS((1, HD), F32)],
        compiler_params=_params(("arbitrary",)), name=name,
    )(o, proj, gn, dob)


def _merge_fwd(ya, yb, proj, name):
    t = ya.shape[0]

    def body(a_ref, b_ref, gp_ref, gd_ref, o_ref):
        o_ref[...] = (_sigmoid(gp_ref[...]) * a_ref[...] + _sigmoid(gd_ref[...]) * b_ref[...]).astype(o_ref.dtype)

    return pl.pallas_call(
        body, grid=(t // TR,), in_specs=[_row(D), _row(D), _row(D, OFF_GP // D), _row(D, OFF_GD // D)],
        out_specs=_row(D), out_shape=SDS((t, D), BF16),
        compiler_params=_params(("parallel",)), name=name,
    )(ya, yb, proj, proj)


def _merge_bwd(dm, ya, yb, proj, name):
    t = ya.shape[0]

    def body(d_ref, a_ref, b_ref, gp_ref, gd_ref, da_ref, db_ref, dgp_ref, dgd_ref):
        dv = d_ref[...]
        sp = _sigmoid(gp_ref[...])
        sd = _sigmoid(gd_ref[...])
        da_ref[...] = dv * sp
        db_ref[...] = (dv * sd).astype(db_ref.dtype)
        dgp_ref[...] = (dv * a_ref[...] * sp * (1.0 - sp)).astype(dgp_ref.dtype)
        dgd_ref[...] = (dv * b_ref[...] * sd * (1.0 - sd)).astype(dgd_ref.dtype)

    return pl.pallas_call(
        body, grid=(t // TR,),
        in_specs=[_row(D), _row(D), _row(D), _row(D, OFF_GP // D), _row(D, OFF_GD // D)],
        out_specs=[_row(D)] * 4,
        out_shape=[SDS((t, D), F32), SDS((t, D), BF16), SDS((t, D), BF16), SDS((t, D), BF16)],
        compiler_params=_params(("parallel",)), name=name,
    )(dm, ya, yb, proj, proj)


def _split2(x):
    hi = x.astype(BF16)
    return hi, (x - hi.astype(F32)).astype(BF16)


def _dot3(a, b, dims):
    ah, al = _split2(a)
    bh, bl = _split2(b)
    return _dg(ah, bh, dims) + (_dg(ah, bl, dims) + _dg(al, bh, dims))


def _neumann_inverses(mats):
    ri = _iota((CH, CH), 0)
    ci = _iota((CH, CH), 1)
    eye = jnp.where(ri == ci, 1.0, 0.0).astype(F32)
    xs = [-a for a in mats]
    ps = [eye + x for x in xs]
    for _ in range(5):
        xs = [_dot3(x, x, NN) for x in xs]
        ps = [p + _dot3(p, x, NN) for p, x in zip(ps, xs)]
    return ps


def _solve_with(inv):
    @jax.custom_vjp
    def solve(a, rhs):
        return _dot3(inv, rhs, NN)

    def fwd(a, rhs):
        sol = _dot3(inv, rhs, NN)
        return sol, sol

    def bwd(sol, d):
        drhs = _dot3(inv, d, TN)
        return -_dot3(drhs, sol, NT), drhs

    solve.defvjp(fwd, bwd)
    return solve


@jax.custom_vjp
def _rows_to_lanes(g64):
    ri = _iota((CH, CH), 0)
    ci = _iota((CH, CH), 1)
    diag = jnp.where(ri == ci, g64, 0.0)
    ones = jnp.ones((CH, CH), BF16)
    hi = diag.astype(BF16)
    rem = diag - hi.astype(F32)
    mid = rem.astype(BF16)
    lo = (rem - mid.astype(F32)).astype(BF16)
    return _dg(ones, hi, NN) + (_dg(ones, mid, NN) + _dg(ones, lo, NN))


def _rows_to_lanes_bwd(_, d):
    ri = _iota((CH, CH), 0)
    ci = _iota((CH, CH), 1)
    return (jnp.where(ri == ci, jnp.broadcast_to(jnp.sum(d, axis=0, keepdims=True), (CH, CH)), 0.0),)


_rows_to_lanes.defvjp(lambda g64: (_rows_to_lanes(g64), None), _rows_to_lanes_bwd)


def _chunk_local(solve_all, q, k, v, g128, g64, gl128, b128, b64):
    ri = _iota((CH, CH), 0)
    ci = _iota((CH, CH), 1)
    causal = ri >= ci
    strict = ri > ci
    gj = [_rows_to_lanes(g) for g in g64]
    decay = [jnp.where(causal, jnp.exp(jnp.where(causal, g - t, 0.0)), 0.0) for g, t in zip(g64, gj)]
    kk = [_nt(x, x) for x in k]
    a = [jnp.where(strict, b * m * dc, 0.0) for b, m, dc in zip(b64, kk, decay)]
    eg = [jnp.exp(g) for g in g128]
    rhs = [jnp.concatenate([b * x, (b * e) * y], axis=1) for b, x, e, y in zip(b128, v, eg, k)]
    sol = solve_all(a, rhs)
    qk = [jnp.where(causal, _nt(x, y) * dc, 0.0) for x, y, dc in zip(q, k, decay)]
    return ([s[:, :HD] for s in sol], [s[:, HD:] for s in sol], qk, [x * e for x, e in zip(q, eg)],
            [x * jnp.exp(gl - g) for x, gl, g in zip(k, gl128, g128)], [jnp.exp(gl) for gl in gl128])


def _all_head_gates(bgv):
    return tuple(list(z) for z in zip(*[_head_gates(bgv, h) for h in range(NH)]))


def _head_gates(bgv, h):
    lane = _iota((CH, 128), 1)
    row = _iota((CH, 128), 0)
    bcol = jnp.sum(jnp.where(lane == h, bgv, 0.0), axis=1, keepdims=True)
    gcol = jnp.sum(jnp.where(lane == NH + h, bgv, 0.0), axis=1, keepdims=True)
    g128 = jnp.broadcast_to(gcol, (CH, 128))
    gl128 = jnp.broadcast_to(jnp.sum(jnp.where(row == CH - 1, g128, 0.0), axis=0, keepdims=True), (CH, 128))
    return (g128, jnp.broadcast_to(gcol, (CH, CH)), gl128,
            jnp.broadcast_to(bcol, (CH, 128)), jnp.broadcast_to(bcol, (CH, CH)))


def _chunk_specs():
    row = pl.BlockSpec((CH, D), lambda i: (i, 0))
    small = pl.BlockSpec((CH, 128), lambda i: (i, 0))
    qk = pl.BlockSpec((NH, CH, CH), lambda i: (i, 0, 0))
    eg = pl.BlockSpec((1, NH, 128), lambda i: (i, 0, 0))
    return row, small, qk, eg


def _dn_local_fwd(q, k, v, bg, name):
    t = q.shape[0]
    n = t // CH

    def body(q_ref, k_ref, v_ref, bg_ref, u_ref, w_ref, qk_ref, qd_ref, kd_ref, eg_ref, inv_ref):
        cols = [slice(h * HD, (h + 1) * HD) for h in range(NH)]

        def solve_all(mats, rhs):
            invs = _neumann_inverses(mats)
            for h in range(NH):
                inv_ref[h] = invs[h]
            return [_dot3(m, r, NN) for m, r in zip(invs, rhs)]

        u, w, qk, qd, kd, egl = _chunk_local(
            solve_all, [q_ref[:, c] for c in cols], [k_ref[:, c] for c in cols], [v_ref[:, c] for c in cols],
            *_all_head_gates(bg_ref[...]))
        for h, hc in enumerate(cols):
            u_ref[:, hc] = u[h]
            w_ref[:, hc] = w[h].astype(w_ref.dtype)
            qd_ref[:, hc] = qd[h].astype(qd_ref.dtype)
            kd_ref[:, hc] = kd[h].astype(kd_ref.dtype)
            qk_ref[h] = qk[h].astype(qk_ref.dtype)
            eg_ref[0, h:h + 1, :] = egl[h][0:1, :]

    row, small, qkb, egb = _chunk_specs()
    return pl.pallas_call(
        body, grid=(n,), in_specs=[row, row, row, small], out_specs=[row, row, qkb, row, row, egb, qkb],
        out_shape=[SDS((t, D), F32), SDS((t, D), BF16), SDS((n * NH, CH, CH), BF16), SDS((t, D), BF16),
                   SDS((t, D), BF16), SDS((n, NH, 128), F32), SDS((n * NH, CH, CH), F32)],
        compiler_params=_params(("parallel",)), name=name,
    )(q, k, v, bg)


def _dn_local_bwd(q, k, v, bg, inv, du, dw, dqk, dqd, dkd, deg, name):
    t = q.shape[0]
    n = t // CH

    def body(q_ref, k_ref, v_ref, bg_ref, inv_ref, du_ref, dw_ref, dqk_ref, dqd_ref, dkd_ref, deg_ref,
             dq_ref, dk_ref, dv_ref, dbg_ref):
        bgv = bg_ref[...]
        lane = _iota((CH, 128), 1)
        row = _iota((CH, 128), 0)
        first = jnp.where(row == 0, 1.0, 0.0)
        acc = jnp.zeros((CH, 128), F32)
        cols = [slice(h * HD, (h + 1) * HD) for h in range(NH)]
        solves = [_solve_with(inv_ref[h]) for h in range(NH)]

        def solve_all(mats, rhs):
            return [f(m, r) for f, m, r in zip(solves, mats, rhs)]

        _, vjp = jax.vjp(functools.partial(_chunk_local, solve_all),
                         [q_ref[:, c] for c in cols], [k_ref[:, c] for c in cols], [v_ref[:, c] for c in cols],
                         *_all_head_gates(bgv))
        cts = ([du_ref[:, c] for c in cols], [dw_ref[:, c] for c in cols], [dqk_ref[h] for h in range(NH)],
               [dqd_ref[:, c] for c in cols], [dkd_ref[:, c] for c in cols],
               [jnp.broadcast_to(deg_ref[0, h:h + 1, :], (CH, 128)) * first for h in range(NH)])
        dq, dk, dv, dg128, dg64, dgl, db128, db64 = vjp(cts)
        for h, hc in enumerate(cols):
            dq_ref[:, hc] = dq[h]
            dk_ref[:, hc] = dk[h]
            dv_ref[:, hc] = dv[h]
            dg = jnp.sum(dg128[h], axis=1, keepdims=True) + jnp.sum(dg64[h], axis=1, keepdims=True)
            tot = jnp.sum(jnp.sum(dgl[h], axis=0, keepdims=True), axis=1, keepdims=True)
            dg = dg + jnp.where(row[:, 0:1] == CH - 1, tot, 0.0)
            db = jnp.sum(db128[h], axis=1, keepdims=True) + jnp.sum(db64[h], axis=1, keepdims=True)
            acc = acc + jnp.where(lane == h, db, 0.0) + jnp.where(lane == NH + h, dg, 0.0)
        dbg_ref[...] = acc

    row, small, qkb, egb = _chunk_specs()
    return pl.pallas_call(
        body, grid=(n,), in_specs=[row, row, row, small, qkb, row, row, qkb, row, row, egb],
        out_specs=[row, row, row, small],
        out_shape=[SDS((t, D), F32)] * 3 + [SDS((t, 128), F32)],
        compiler_params=_params(("parallel",)), name=name,
    )(q, k, v, bg, inv, du, dw, dqk, dqd, dkd, deg)


def _state_step(s, u, w, qk, qd, kd, egl):
    ws = [_nn(a, b) for a, b in zip(w, s)]
    v_new = [a - b for a, b in zip(u, ws)]
    qs = [_nn(a, b) for a, b in zip(qd, s)]
    intra = [_nn(a, b) for a, b in zip(qk, v_new)]
    upd = [_tn(a, b) for a, b in zip(kd, v_new)]
    return [a * e + b for a, e, b in zip(s, egl, upd)], [a + b for a, b in zip(qs, intra)]


def _dn_scan_fwd(u, w, qk, qd, kd, eg, name):
    t = u.shape[0]
    n = t // CH

    def body(u_ref, w_ref, qk_ref, qd_ref, kd_ref, eg_ref, o_ref, save_ref, s_ref):
        @pl.when(pl.program_id(0) == 0)
        def _():
            s_ref[...] = jnp.zeros_like(s_ref)

        cols = [slice(h * HD, (h + 1) * HD) for h in range(NH)]
        s = [s_ref[h] for h in range(NH)]
        for h in range(NH):
            save_ref[0, h] = s[h]
        s_new, o = _state_step(
            s, [u_ref[:, c] for c in cols], [w_ref[:, c].astype(F32) for c in cols],
            [qk_ref[h].astype(F32) for h in range(NH)], [qd_ref[:, c].astype(F32) for c in cols],
            [kd_ref[:, c].astype(F32) for c in cols], [eg_ref[0, h:h + 1, :] for h in range(NH)])
        for h, hc in enumerate(cols):
            o_ref[:, hc] = o[h]
            s_ref[h] = s_new[h]

    row, _, qkb, egb = _chunk_specs()
    return pl.pallas_call(
        body, grid=(n,), in_specs=[row, row, qkb, row, row, egb],
        out_specs=[row, pl.BlockSpec((1, NH, HD, HD), lambda i: (i, 0, 0, 0))],
        out_shape=[SDS((t, D), F32), SDS((n, NH, HD, HD), F32)],
        scratch_shapes=[pltpu.VMEM((NH, HD, HD), F32)],
        compiler_params=_params(("arbitrary",)), name=name,
    )(u, w, qk, qd, kd, eg)


def _dn_scan_bwd(u, w, qk, qd, kd, eg, saved, do, name):
    t = u.shape[0]
    n = t // CH

    def body(u_ref, w_ref, qk_ref, qd_ref, kd_ref, eg_ref, sv_ref, do_ref,
             du_ref, dw_ref, dqk_ref, dqd_ref, dkd_ref, deg_ref, ds_ref):
        @pl.when(pl.program_id(0) == 0)
        def _():
            ds_ref[...] = jnp.zeros_like(ds_ref)

        cols = [slice(h * HD, (h + 1) * HD) for h in range(NH)]
        _, vjp = jax.vjp(
            _state_step, [sv_ref[0, h] for h in range(NH)], [u_ref[:, c] for c in cols],
            [w_ref[:, c].astype(F32) for c in cols], [qk_ref[h].astype(F32) for h in range(NH)],
            [qd_ref[:, c].astype(F32) for c in cols], [kd_ref[:, c].astype(F32) for c in cols],
            [eg_ref[0, h:h + 1, :] for h in range(NH)])
        ds, du, dw, dqk, dqd, dkd, deg = vjp(([ds_ref[h] for h in range(NH)], [do_ref[:, c] for c in cols]))
        for h, hc in enumerate(cols):
            ds_ref[h] = ds[h]
            du_ref[:, hc] = du[h]
            dw_ref[:, hc] = dw[h]
            dqk_ref[h] = dqk[h]
            dqd_ref[:, hc] = dqd[h]
            dkd_ref[:, hc] = dkd[h]
            deg_ref[0, h:h + 1, :] = deg[h]

    rev = lambda i: (n - 1 - i, 0)
    rev3 = lambda i: (n - 1 - i, 0, 0)
    row = pl.BlockSpec((CH, D), rev)
    qkb = pl.BlockSpec((NH, CH, CH), rev3)
    egb = pl.BlockSpec((1, NH, 128), rev3)
    return pl.pallas_call(
        body, grid=(n,),
        in_specs=[row, row, qkb, row, row, egb,
                  pl.BlockSpec((1, NH, HD, HD), lambda i: (n - 1 - i, 0, 0, 0)), row],
        out_specs=[row, row, qkb, row, row, egb],
        out_shape=[SDS((t, D), F32), SDS((t, D), F32), SDS((n * NH, CH, CH), F32), SDS((t, D), F32),
                   SDS((t, D), F32), SDS((n, NH, 128), F32)],
        scratch_shapes=[pltpu.VMEM((NH, HD, HD), F32)],
        compiler_params=_params(("arbitrary",)), name=name,
    )(u, w, qk, qd, kd, eg, saved, do)


def _ada_fwd(c_all, ada_w, ada_b, name):
    ncol = ada_w.shape[1]

    def body(c_ref, w_ref, b_ref, o_ref):
        o_ref[...] = _dg(_silu(c_ref[...]), w_ref[...], NN, HI) + b_ref[...]

    return pl.pallas_call(body, out_shape=SDS((NDEV, ncol), F32),
                          compiler_params=pltpu.CompilerParams(vmem_limit_bytes=VMEM_LIMIT), name=name,
                          )(c_all, ada_w, ada_b)


def _ada_bwd(c_all_t, dmod, name):
    ncol = dmod.shape[1]

    def body(c_ref, d_ref, o_ref):
        sc = _silu(c_ref[...])
        acc = sc[:, 0:1] * d_ref[0:1, :]
        for b in range(1, NDEV):
            acc = acc + sc[:, b:b + 1] * d_ref[b:b + 1, :]
        o_ref[...] = acc

    return pl.pallas_call(body, out_shape=SDS((D, ncol), F32),
                          compiler_params=pltpu.CompilerParams(vmem_limit_bytes=VMEM_LIMIT), name=name,
                          )(c_all_t, dmod)


def _sum_devices(parts, out_dtype, name):
    _, r, c = parts.shape
    tr = TR if r % TR == 0 else r

    def body(p_ref, o_ref):
        acc = p_ref[0].astype(F32)
        for i in range(1, NDEV):
            acc = acc + p_ref[i].astype(F32)
        o_ref[...] = acc.astype(o_ref.dtype)

    return pl.pallas_call(
        body, grid=(r // tr,), in_specs=[pl.BlockSpec((NDEV, tr, c), lambda i: (0, i, 0))],
        out_specs=pl.BlockSpec((tr, c), lambda i: (i, 0)), out_shape=SDS((r, c), out_dtype),
        compiler_params=_params(("parallel",)), name=name,
    )(parts)


def _adamw(w, g, m, v, name):
    r, c = w.shape
    tr = _pick(r, (256, 128, 88, 8)) if r % 8 == 0 else r
    bc1 = 1.0 - ADAM_B1 ** ADAM_STEP
    bc2 = 1.0 - ADAM_B2 ** ADAM_STEP

    def body(w_ref, g_ref, m_ref, v_ref, d_ref, nm_ref, nv_ref):
        gv = g_ref[...]
        m_new = ADAM_B1 * m_ref[...] + (1.0 - ADAM_B1) * gv
        v_new = ADAM_B2 * v_ref[...] + (1.0 - ADAM_B2) * (gv * gv)
        nm_ref[...] = m_new
        nv_ref[...] = v_new
        d_ref[...] = -ADAM_LR * ((m_new / bc1) / (jnp.sqrt(v_new / bc2) + ADAM_EPS) + ADAM_WD * w_ref[...])

    spec = pl.BlockSpec((tr, c), lambda i: (i, 0))
    return pl.pallas_call(
        body, grid=(r // tr,), in_specs=[spec] * 4, out_specs=[spec] * 3,
        out_shape=[SDS((r, c), F32)] * 3, compiler_params=_params(("parallel",)), name=name,
    )(w, g, m, v)


ANY = pl.BlockSpec(memory_space=pl.ANY)
MESH = pl.DeviceIdType.MESH


def _all_gather(xs, name, after=None):
    n = len(xs)
    extra = [] if after is None else [after]

    def body(*refs):
        x_refs, out_refs = refs[:n], refs[n + len(extra):2 * n + len(extra)]
        send_sems, recv_sems, local_sems = refs[-3:]
        mx, my, mc = lax.axis_index("x"), lax.axis_index("y"), lax.axis_index("c")
        me, sibling = (mx, my, mc), (mx, my, 1 - mc)
        chips = [(1 - mx, my), (mx, 1 - my), (1 - mx, 1 - my)]

        def rows(a, px, py, pc):
            return out_refs[a].at[4 * px + 2 * py + pc]

        def copy(a, k, block, to, src=None):
            return pltpu.make_async_remote_copy(
                src_ref=rows(a, *block) if src is None else src, dst_ref=rows(a, *block),
                send_sem=send_sems.at[a, k], recv_sem=recv_sems.at[a, k], device_id=to, device_id_type=MESH)

        mine = [pltpu.make_async_copy(x_refs[a], rows(a, *me), local_sems.at[a]) for a in range(n)]
        for cp in mine:
            cp.start()
        first = []
        for a in range(n):
            first.append(copy(a, 0, me, sibling, src=x_refs[a]))
            first += [copy(a, 1 + j, me, (*chip, mc), src=x_refs[a]) for j, chip in enumerate(chips)]
        for cp in first:
            cp.start()
        passed = []
        for a in range(n):
            for j, chip in enumerate(chips):
                copy(a, 1 + j, (*chip, mc), me).wait_recv()
                passed.append(copy(a, 4 + j, (*chip, mc), sibling))
                passed[-1].start()
        for a in range(n):
            copy(a, 0, sibling, me).wait_recv()
            for j, chip in enumerate(chips):
                copy(a, 4 + j, (*chip, 1 - mc), me).wait_recv()
        for cp in first + passed:
            cp.wait_send()
        for cp in mine:
            cp.wait()

    return pl.pallas_call(
        body, out_shape=[SDS((NDEV,) + x.shape, x.dtype) for x in xs], in_specs=[ANY] * (n + len(extra)),
        out_specs=[ANY] * n,
        scratch_shapes=[pltpu.SemaphoreType.DMA((n, 7)), pltpu.SemaphoreType.DMA((n, 7)),
                        pltpu.SemaphoreType.DMA((n,))],
        name=name,
    )(*xs, *extra)


HBM = pl.BlockSpec(memory_space=pltpu.HBM)
SEM = pl.BlockSpec(memory_space=pltpu.SEMAPHORE)
EFFECT = pltpu.SideEffectType.DATAFLOW_SIDE_EFFECTING


def _peers():
    mx, my, mc = lax.axis_index("x"), lax.axis_index("y"), lax.axis_index("c")
    out = []
    for k in range(1, NDEV):
        out.append((1 - mx if k & 4 else mx, 1 - my if k & 2 else my, 1 - mc if k & 1 else mc))
    return 4 * mx + 2 * my + mc, out


def _push_start(srcs, sliced, name, after=None):
    n = len(srcs)
    extra = [] if after is None else [after]
    me_idx = 4 * lax.axis_index("x") + 2 * lax.axis_index("y") + lax.axis_index("c")
    lands = []
    for s in srcs:
        blk = lax.dynamic_index_in_dim(s, me_idx, 0, keepdims=True) if sliced else s[None]
        shape = s.shape if sliced else (NDEV,) + s.shape
        lands.append(lax.dynamic_update_slice(lax.empty(shape, s.dtype), blk, (me_idx,) + (0,) * (len(shape) - 1)))

    def body(*refs):
        src_refs, land_refs = refs[:n], refs[n:2 * n]
        outs = refs[2 * n + len(extra):]
        send_sems, recv_sems = outs[:n], outs[n:2 * n]
        token = refs[-1]
        me, peers = _peers()
        for a in range(n):
            for k, (px, py, pc) in enumerate(peers):
                src = src_refs[a].at[4 * px + 2 * py + pc] if sliced else src_refs[a]
                pltpu.make_async_remote_copy(
                    src_ref=src, dst_ref=land_refs[a].at[me], send_sem=send_sems[a].at[k],
                    recv_sem=recv_sems[a].at[k], device_id=(px, py, pc), device_id_type=MESH).start()
        token[...] = jnp.zeros_like(token)

    outs = pl.pallas_call(
        body, name=name,
        out_shape=([pltpu.SemaphoreType.DMA((NDEV - 1,))] * (2 * n)
                   + [pltpu.HBM(s.shape, s.dtype) for s in srcs] + [pltpu.HBM(l.shape, l.dtype) for l in lands]
                   + [SDS((8, 128), F32)]),
        in_specs=[HBM] * (2 * n) + [pl.BlockSpec(memory_space=pl.ANY)] * len(extra),
        out_specs=[SEM] * (2 * n) + [HBM] * (2 * n) + [pl.BlockSpec(memory_space=pltpu.VMEM)],
        input_output_aliases={i: 2 * n + i for i in range(2 * n)},
        compiler_params=pltpu.CompilerParams(has_side_effects=EFFECT),
    )(*[pltpu.with_memory_space_constraint(s, pltpu.HBM) for s in srcs],
      *[pltpu.with_memory_space_constraint(l, pltpu.HBM) for l in lands], *extra)
    sends, recvs = outs[:n], outs[n:2 * n]
    src_thru, land_thru = outs[2 * n:3 * n], outs[3 * n:4 * n]
    return [(sends[a], recvs[a], src_thru[a], land_thru[a]) for a in range(n)], outs[-1]


def _push_wait(started, sliced, after, name):
    n = len(started)

    def body(*refs):
        src_refs, land_refs = refs[:n], refs[n:2 * n]
        send_sems, recv_sems = refs[2 * n:3 * n], refs[3 * n:4 * n]
        me, peers = _peers()
        for a in range(n):
            for k, (px, py, pc) in enumerate(peers):
                src = src_refs[a].at[4 * px + 2 * py + pc] if sliced else src_refs[a]
                cp = pltpu.make_async_remote_copy(
                    src_ref=src, dst_ref=land_refs[a].at[me], send_sem=send_sems[a].at[k],
                    recv_sem=recv_sems[a].at[k], device_id=(px, py, pc), device_id_type=MESH)
                cp.wait_send()
                cp.wait_recv()

    srcs = [s[2] for s in started]
    lands = [s[3] for s in started]
    outs = pl.pallas_call(
        body, name=name,
        out_shape=[pltpu.HBM(s.shape, s.dtype) for s in srcs] + [pltpu.HBM(l.shape, l.dtype) for l in lands],
        in_specs=[HBM] * (2 * n) + [SEM] * (2 * n) + [pl.BlockSpec(memory_space=pl.ANY)],
        out_specs=[HBM] * (2 * n),
        input_output_aliases={i: i for i in range(2 * n)},
        compiler_params=pltpu.CompilerParams(has_side_effects=EFFECT),
    )(*srcs, *lands, *[s[0] for s in started], *[s[1] for s in started], after)
    return outs[n:]


def _cols_from_blocks(blocks):
    _, rows, w = blocks.shape
    return blocks.transpose(1, 0, 2).reshape(rows, NDEV * w)


def _cols_to_blocks(full):
    rows, total = full.shape
    return full.reshape(rows, NDEV, total // NDEV).transpose(1, 0, 2)


def _mix_pad(w):
    rows = w.shape[0]
    xp, q, k, v, z, b, a, gp, gd = jnp.split(w, (512, 1536, 2560, 3584, 4608, 4616, 4624, 5648), axis=1)
    pad = jnp.zeros((rows, MIXP - OFF_BA - 16), w.dtype)
    return jnp.concatenate([q, k, v, z, gp, gd, xp, b, a, pad], axis=1)


def _mix_unpad(w):
    q, k, v, z, gp, gd, xp, b, a = (w[:, OFF_Q:OFF_K], w[:, OFF_K:OFF_V], w[:, OFF_V:OFF_Z], w[:, OFF_Z:OFF_GP],
                                    w[:, OFF_GP:OFF_GD], w[:, OFF_GD:OFF_XP], w[:, OFF_XP:OFF_BA],
                                    w[:, OFF_BA:OFF_BA + 8], w[:, OFF_BA + 8:OFF_BA + 16])
    return jnp.concatenate([xp, q, k, v, z, b, a, gp, gd], axis=1)


def _lane_row(vec8):
    return jnp.zeros((1, 128), F32).at[0, NH:2 * NH].set(vec8)


def _ffn_fwd(x, g, shift, scale, gate, w_in, w_out, tag, token=None):
    t = x.shape[0]
    h = _norm_mod_fwd(x, g, shift, scale, f"{tag}_norm")
    if isinstance(w_in, tuple):
        w_in, = _push_wait([w_in], False, h, f"{tag}_gather_wait_in")
    u, a = _swiglu_up(h, w_in, f"{tag}_up", after=token)
    w_out, = _push_wait([w_out], False, a, f"{tag}_gather_wait_out")
    w_out = w_out.reshape(FH, D)
    y = _matmul(a, w_out, a_blk=True, out_dtype=F32, name=f"{tag}_down")
    return _resid_fwd(x, y, gate, 0.5, f"{tag}_res"), (h, u, a, y), w_in, w_out


def _ffn_bwd(dx_out, x, g, scale, gate, w_in, w_out, saved, tag):
    h, u, a, y = saved
    t = x.shape[0]
    dy, dgate = _resid_bwd(dx_out, y, gate, 0.5, f"{tag}_res_bwd")
    dw_out = _matmul(a, dy, ta=True, a_blk=True, out_dtype=BF16, name=f"{tag}_down_dw")
    sent_out, token = _push_start([dw_out.reshape(NDEV, FH // NDEV, D)], True, f"{tag}_grad_start_out")
    du = _swiglu_down_bwd(dy, w_out, u, f"{tag}_down_dx", after=token).reshape(NDEV, t, FB)
    dw_in = _matmul(h, du, ta=True, b_blk=True, o_blk=True, out_dtype=BF16, name=f"{tag}_up_dw")
    sent_in, token = _push_start([dw_in], True, f"{tag}_grad_start_in")
    dh = _matmul(du, w_in, tb=True, a_blk=True, b_blk=True, out_dtype=F32, name=f"{tag}_up_dx", after=token)
    dx, dshift, dscale, dg = _norm_mod_bwd(x, g, scale, dh, dx_out, f"{tag}_norm_bwd")
    return dx, (dshift, dscale, dgate), dg, sent_in + sent_out


def kernel(x, c, ada_w, ada_b, norm_g, ffn1_w_in, ffn1_w_out, ffn2_w_in, ffn2_w_out, mix_w_in, conv_w, a_log, dt_bias, dn_norm_g, pool_w, pool_scale, pool_proj, dn_proj, mix_w_out, final_g, loss_target, m_ada_w, m_ada_b, m_norm_g, m_ffn1_w_in, m_ffn1_w_out, m_ffn2_w_in, m_ffn2_w_out, m_mix_w_in, m_conv_w, m_a_log, m_dt_bias, m_dn_norm_g, m_pool_w, m_pool_scale, m_pool_proj, m_dn_proj, m_mix_w_out, m_final_g, v_ada_w, v_ada_b, v_norm_g, v_ffn1_w_in, v_ffn1_w_out, v_ffn2_w_in, v_ffn2_w_out, v_mix_w_in, v_conv_w, v_a_log, v_dt_bias, v_dn_norm_g, v_pool_w, v_pool_scale, v_pool_proj, v_dn_proj, v_mix_w_out, v_final_g):
    me = 4 * lax.axis_index("x") + 2 * lax.axis_index("y") + lax.axis_index("c")
    x0 = x[0]
    target = loss_target[0]
    t = x0.shape[0]

    big = [ffn1_w_in[0], ffn1_w_out[0], ffn2_w_in[0], ffn2_w_out[0], mix_w_in[0], pool_proj[0], dn_proj[0],
           mix_w_out[0]]
    small = jnp.concatenate([c.reshape(8, 128), conv_w[0].reshape(12, 128), norm_g[0].reshape(3, 128),
                             jnp.zeros((1, 128), F32)], axis=0)
    small_all, = _all_gather([small], "gather_small")
    c_all = small_all[:, 0:8, :].reshape(NDEV, D)
    conv_full = small_all[:, 8:20, :].reshape(NDEV, 4, 384).transpose(1, 0, 2).reshape(4, 3 * D)
    norm_full = small_all[:, 20:23, :].reshape(NDEV, 3, 128).transpose(1, 0, 2).reshape(3, D)

    ncol = ada_w.shape[2]
    ada_b_mine = lax.dynamic_slice(ada_b, (0, me * ncol), (1, ncol))
    mod_cols = _ada_fwd(c_all, ada_w[0], ada_b_mine, "ada_fwd")
    mod_all, w_in1 = _all_gather([mod_cols, big[0].astype(BF16)], "gather_mod_first_weight")
    order = [1, 4, 5, 6, 7, 2, 3]
    started, token = _push_start([big[i].astype(BF16) for i in order], False, "gather_start", after=mod_all)
    started = {i: s for i, s in zip(order, started)}
    mod = lax.dynamic_index_in_dim(mod_all, me, axis=1, keepdims=False).reshape(9, D)
    shift = [mod[3 * s:3 * s + 1] for s in range(3)]
    scale = [mod[3 * s + 1:3 * s + 2] for s in range(3)]
    gate = [mod[3 * s + 2:3 * s + 3] for s in range(3)]
    ng = [norm_full[s:s + 1] for s in range(3)]
    fg = final_g.reshape(1, D)
    al_row = _lane_row(a_log[0])
    dt_row = _lane_row(dt_bias[0])
    gn = dn_norm_g
    pw = pool_w[0]
    ps = pool_scale

    x1, saved1, w_in1, w_out1 = _ffn_fwd(x0, ng[0], shift[0], scale[0], gate[0], w_in1, started[1], "ffn1", token)

    h1 = _norm_mod_fwd(x1, ng[1], shift[1], scale[1], "mix_norm")
    seg = _push_wait([started[i] for i in (4, 5, 6, 7)], False, h1, "mix_gather_wait")
    w_mix = _mix_pad(_cols_from_blocks(seg[0]))
    w_pp = _cols_from_blocks(seg[1])
    w_dn = seg[2].reshape(D, D)
    w_mo = seg[3].reshape(D, D)
    proj = _matmul(h1, w_mix, out_dtype=F32, name="mix_in")
    ya = _pool_fwd(proj, pw, ps, w_pp, "pool_fwd")
    qh, kh, vh, bg = _dn_pre_fwd(proj, conv_full, al_row, dt_row, "dn_pre")
    u, w, qk, qd, kd, eg, inv = _dn_local_fwd(qh, kh, vh, bg, "dn_local")
    o, s_saved = _dn_scan_fwd(u, w, qk, qd, kd, eg, "dn_scan")
    ob = _dn_post_fwd(o, proj, gn, "dn_post")
    yb = _matmul(ob, w_dn, out_dtype=F32, name="dn_out")
    merged = _merge_fwd(ya, yb, proj, "merge")
    mix_y = _matmul(merged, w_mo, out_dtype=F32, name="mix_out")
    x2 = _resid_fwd(x1, mix_y, gate[1], 1.0, "mix_res")

    x3, saved2, w_in2, w_out2 = _ffn_fwd(x2, ng[2], shift[2], scale[2], gate[2], started[2], started[3], "ffn2")
    loss_row, dx3, dfg = _final_loss(x3, fg, target, "loss")

    dx2, dmod2, dng2, sent2 = _ffn_bwd(dx3, x2, ng[2], scale[2], gate[2], w_in2, w_out2, saved2, "ffn2")

    dmy, dgate1 = _resid_bwd(dx2, mix_y, gate[1], 1.0, "mix_res_bwd")
    dmerged = _matmul(dmy, w_mo, tb=True, out_dtype=F32, name="mix_out_dx")
    dw_mo = _matmul(merged, dmy, ta=True, out_dtype=BF16, name="mix_out_dw")
    dya, dyb, dgp, dgd = _merge_bwd(dmerged, ya, yb, proj, "merge_bwd")
    dob = _matmul(dyb, w_dn, tb=True, out_dtype=F32, name="dn_out_dx")
    dw_dn = _matmul(ob, dyb, ta=True, out_dtype=BF16, name="dn_out_dw")
    do, dz, dgn = _dn_post_bwd(o, proj, gn, dob, "dn_post_bwd")
    du, dw, dqk, dqd, dkd, deg = _dn_scan_bwd(u, w, qk, qd, kd, eg, s_saved, do, "dn_scan_bwd")
    dqh, dkh, dvh, dbg = _dn_local_bwd(qh, kh, vh, bg, inv, du, dw, dqk, dqd, dkd, deg, "dn_local_bwd")
    dconv, draw, dal, ddt = _dn_pre_bwd_act(proj, conv_full, al_row, dt_row, dqh, dkh, dvh, dbg, "dn_pre_bwd_act")
    dqkv, dcw = _dn_pre_bwd_conv(proj, conv_full, dconv, "dn_pre_bwd_conv")
    dwin, dpl, dpw, dps, dpp = _pool_bwd_local(proj, pw, ps, w_pp, dya, "pool_bwd_local")
    dxp = _pool_bwd_window(dwin, dpl, "pool_bwd_window")
    dproj = jnp.concatenate([dqkv, dz, dgp, dgd, dxp, draw, jnp.zeros((t, MIXP - OFF_BA - 128), BF16)], axis=1)
    dw_mix = _matmul(h1, dproj, ta=True, out_dtype=BF16, name="mix_in_dw")
    sent1, token = _push_start(
        [_cols_to_blocks(_mix_unpad(dw_mix)), _cols_to_blocks(dpp.astype(BF16)), dw_dn.reshape(NDEV, -1, D),
         dw_mo.reshape(NDEV, -1, D)], True, "mix_grad_start")
    dh1 = _matmul(dproj, w_mix, tb=True, out_dtype=F32, name="mix_in_dx", after=token)
    dx1, dsh1, dsc1, dng1 = _norm_mod_bwd(x1, ng[1], scale[1], dh1, dx2, "mix_norm_bwd")

    dx0, dmod0, dng0, sent0 = _ffn_bwd(dx1, x0, ng[0], scale[0], gate[0], w_in1, w_out1, saved1, "ffn1")

    dmod = jnp.concatenate([*dmod0, dsh1, dsc1, dgate1, *dmod2], axis=1).reshape(-1)
    flat = jnp.concatenate([
        dmod, dal[0, NH:2 * NH], ddt[0, NH:2 * NH], dgn.reshape(-1), dps.reshape(-1), dfg.reshape(-1),
        dpw.reshape(-1), jnp.concatenate([dng0, dng1, dng2], axis=0).reshape(-1), dcw.reshape(-1)])
    nflat = 90 * D
    flat = jnp.concatenate([flat, jnp.zeros((nflat - flat.shape[0],), F32)]).reshape(90, D)
    flat_all, = _all_gather([flat], "gather_small_grads")
    tot = _sum_devices(flat_all, F32, "sum_small_grads").reshape(-1)
    dmod_all = flat_all.reshape(NDEV, nflat)[:, :9 * D]
    dmod_cols = lax.dynamic_slice(dmod_all, (0, me * ncol), (NDEV, ncol))
    g_ada_w = _ada_bwd(c_all.T, dmod_cols, "ada_bwd")

    p = 0
    pieces = {}
    for nm, size in (("ada_b", 9 * D), ("a_log", NH), ("dt_bias", NH), ("dn_norm_g", HD), ("pool_scale", PW),
                     ("final_g", D), ("pool_w", 4 * PG * PG), ("norm_g", 3 * D), ("conv_w", 12 * D)):
        pieces[nm] = tot[p:p + size]
        p += size
    g_norm = lax.dynamic_slice(pieces["norm_g"].reshape(3, D), (0, me * 128), (3, 128))
    g_conv = lax.dynamic_slice(pieces["conv_w"].reshape(4, 3 * D), (0, me * 384), (4, 384))

    grads = {
        "ada_w": g_ada_w.reshape(ada_w.shape), "ada_b": pieces["ada_b"].reshape(ada_b.shape),
        "norm_g": g_norm.reshape(norm_g.shape), "conv_w": g_conv.reshape(conv_w.shape),
        "a_log": pieces["a_log"].reshape(a_log.shape), "dt_bias": pieces["dt_bias"].reshape(dt_bias.shape),
        "dn_norm_g": pieces["dn_norm_g"].reshape(dn_norm_g.shape), "pool_w": pieces["pool_w"].reshape(pool_w.shape),
        "pool_scale": pieces["pool_scale"].reshape(pool_scale.shape),
        "final_g": pieces["final_g"].reshape(final_g.shape),
    }
    weights = {"ada_w": ada_w, "ada_b": ada_b, "norm_g": norm_g, "ffn1_w_in": ffn1_w_in, "ffn1_w_out": ffn1_w_out,
               "ffn2_w_in": ffn2_w_in, "ffn2_w_out": ffn2_w_out, "mix_w_in": mix_w_in, "conv_w": conv_w,
               "a_log": a_log, "dt_bias": dt_bias, "dn_norm_g": dn_norm_g, "pool_w": pool_w,
               "pool_scale": pool_scale, "pool_proj": pool_proj, "dn_proj": dn_proj, "mix_w_out": mix_w_out,
               "final_g": final_g}
    m_in = {"ada_w": m_ada_w, "ada_b": m_ada_b, "norm_g": m_norm_g, "ffn1_w_in": m_ffn1_w_in,
            "ffn1_w_out": m_ffn1_w_out, "ffn2_w_in": m_ffn2_w_in, "ffn2_w_out": m_ffn2_w_out,
            "mix_w_in": m_mix_w_in, "conv_w": m_conv_w, "a_log": m_a_log, "dt_bias": m_dt_bias,
            "dn_norm_g": m_dn_norm_g, "pool_w": m_pool_w, "pool_scale": m_pool_scale, "pool_proj": m_pool_proj,
            "dn_proj": m_dn_proj, "mix_w_out": m_mix_w_out, "final_g": m_final_g}
    v_in = {"ada_w": v_ada_w, "ada_b": v_ada_b, "norm_g": v_norm_g, "ffn1_w_in": v_ffn1_w_in,
            "ffn1_w_out": v_ffn1_w_out, "ffn2_w_in": v_ffn2_w_in, "ffn2_w_out": v_ffn2_w_out,
            "mix_w_in": v_mix_w_in, "conv_w": v_conv_w, "a_log": v_a_log, "dt_bias": v_dt_bias,
            "dn_norm_g": v_dn_norm_g, "pool_w": v_pool_w, "pool_scale": v_pool_scale, "pool_proj": v_pool_proj,
            "dn_proj": v_dn_proj, "mix_w_out": v_mix_w_out, "final_g": v_final_g}

    names = list(weights)
    large = ("ada_w", "ffn1_w_in", "ffn1_w_out", "ffn2_w_in", "ffn2_w_out", "mix_w_in", "pool_proj", "dn_proj",
             "mix_w_out")
    delta, new_m, new_v = {}, {}, {}

    def update(nm):
        shp = weights[nm].shape
        two_d = (shp[-2], shp[-1])
        d_, m_, v_ = _adamw(weights[nm].reshape(two_d), grads[nm].reshape(two_d), m_in[nm].reshape(two_d),
                            v_in[nm].reshape(two_d), f"adamw_{nm}")
        delta[nm], new_m[nm], new_v[nm] = d_.reshape(shp), m_.reshape(shp), v_.reshape(shp)
        return d_

    def reduce(sent, group, after, tag):
        for nm, r in zip(group, _push_wait(sent, True, after, f"{tag}_grad_wait")):
            grads[nm] = _sum_devices(r, F32, f"sum_grads_{nm}").reshape(weights[nm].shape)

    done = update("ada_w")
    reduce(sent2, ("ffn2_w_in", "ffn2_w_out"), done, "ffn2")
    update("ffn2_w_in")
    done = update("ffn2_w_out")
    reduce(sent1, ("mix_w_in", "pool_proj", "dn_proj", "mix_w_out"), done, "mix")
    for nm in ("mix_w_in", "pool_proj", "dn_proj", "mix_w_out"):
        done = update(nm)
    reduce(sent0, ("ffn1_w_in", "ffn1_w_out"), done, "ffn1")
    update("ffn1_w_in")
    update("ffn1_w_out")
    rest = [nm for nm in names if nm not in large]
    total = sum(weights[nm].size for nm in rest)
    padded = -(-total // D) * D

    def pack(tree, fill):
        flat_ = jnp.concatenate([tree[nm].reshape(-1) for nm in rest])
        return jnp.concatenate([flat_, jnp.full((padded - total,), fill, F32)]).reshape(-1, D)

    d_, m_, v_ = _adamw(pack(weights, 0.0), pack(grads, 0.0), pack(m_in, 0.0), pack(v_in, 1.0), "adamw_small")
    p = 0
    for nm in rest:
        size = weights[nm].size
        shp = weights[nm].shape
        delta[nm] = d_.reshape(-1)[p:p + size].reshape(shp)
        new_m[nm] = m_.reshape(-1)[p:p + size].reshape(shp)
        new_v[nm] = v_.reshape(-1)[p:p + size].reshape(shp)
        p += size

    loss = lax.psum(loss_row[0, 0], ("x", "y", "c"))
    grad_x = dx0.reshape(x.shape)
    return (loss, grad_x, *[grads[nm] for nm in names], *[delta[nm] for nm in names],
            *[new_m[nm] for nm in names], *[new_v[nm] for nm in names])
```

```python
import functools

import jax
import jax.numpy as jnp
from jax import lax
from jax.experimental import pallas as pl
from jax.experimental.pallas import tpu as pltpu

F32 = jnp.float32
BF16 = jnp.bfloat16
SDS = jax.ShapeDtypeStruct
HI = lax.Precision.HIGHEST

D = 1024
FH = 2816
FB = 704
NH = 8
HD = 128
CH = 64
NDEV = 8
PW = 512
PG = 128
RMS_EPS = 1e-6
L2_EPS = 1e-6
TR = 512
HALO = 16
VMEM_LIMIT = 56 * 1024 * 1024

MIXP = 6912
OFF_Q, OFF_K, OFF_V, OFF_Z, OFF_GP, OFF_GD, OFF_XP, OFF_BA = 0, 1024, 2048, 3072, 4096, 5120, 6144, 6656
MIX_RAW = 6672

ADAM_LR = 0.001
ADAM_B1 = 0.9
ADAM_B2 = 0.999
ADAM_EPS = 1e-08
ADAM_WD = 0.01
ADAM_STEP = 10

NN = (((1,), (0,)), ((), ()))
NT = (((1,), (1,)), ((), ()))
TN = (((0,), (0,)), ((), ()))


def _dg(a, b, dims, prec=None):
    return lax.dot_general(a, b, dims, precision=prec, preferred_element_type=F32)


def _make_dots(prec):
    @jax.custom_vjp
    def nn(a, b):
        return _dg(a, b, NN, prec)

    @jax.custom_vjp
    def nt(a, b):
        return _dg(a, b, NT, prec)

    @jax.custom_vjp
    def tn(a, b):
        return _dg(a, b, TN, prec)

    nn.defvjp(lambda a, b: (nn(a, b), (a, b)), lambda r, d: (nt(d, r[1]), tn(r[0], d)))
    nt.defvjp(lambda a, b: (nt(a, b), (a, b)), lambda r, d: (nn(d, r[1]), tn(d, r[0])))
    tn.defvjp(lambda a, b: (tn(a, b), (a, b)), lambda r, d: (nt(r[1], d), nn(r[0], d)))
    return nn, nt, tn


_nn, _nt, _tn = _make_dots(None)


def _params(sem):
    return pltpu.CompilerParams(dimension_semantics=sem, vmem_limit_bytes=VMEM_LIMIT)


def _sigmoid(x):
    return 1.0 / (1.0 + jnp.exp(-x))


def _silu(x):
    return x * _sigmoid(x)


def _dsilu(x):
    s = _sigmoid(x)
    return s * (1.0 + x * (1.0 - s))


def _pick(n, cands):
    for c in cands:
        if n % c == 0:
            return c
    raise ValueError(f"no tile for {n}")


def _iota(shape, dim):
    return lax.broadcasted_iota(jnp.int32, shape, dim)


def _matmul(a, b, *, ta=False, tb=False, a_blk=False, b_blk=False, o_blk=False, tm=None, tn=None, tk=None,
            out_dtype, name, after=None):
    if a_blk:
        nb, r, cb = a.shape
        if ta:
            k_dim, m_dim, tm = r, nb * cb, cb
        else:
            m_dim, k_dim, tk = r, nb * cb, cb
    else:
        k_dim, m_dim = a.shape if ta else a.shape[::-1]
    if b_blk:
        nb, r, cb = b.shape
        if tb:
            n_dim, tk = r, cb
            assert nb * cb == k_dim
        else:
            n_dim, tn = nb * cb, cb
            assert r == k_dim
    else:
        n_dim = b.shape[0] if tb else b.shape[1]
    tm = tm or _pick(m_dim, (1024, 768, 512, 256, 128))
    tn = tn or _pick(n_dim, (1024, 768, 512, 256, 128))
    tk = tk or (k_dim if (k_dim <= 2816 and not ta) else _pick(k_dim, (2816, 2304, 1024, 512, 256)))
    nk = k_dim // tk
    dims = ((((0,) if ta else (1,)), ((1,) if tb else (0,))), ((), ()))

    def body(a_ref, b_ref, *rest):
        o_ref, acc_ref = rest[-2:]
        k = pl.program_id(2)

        @pl.when(k == 0)
        def _():
            acc_ref[...] = jnp.zeros_like(acc_ref)

        acc_ref[...] += lax.dot_general(a_ref[...].astype(BF16), b_ref[...].astype(BF16), dims,
                                        preferred_element_type=F32)

        @pl.when(k == nk - 1)
        def _():
            o_ref[...] = acc_ref[...].astype(o_ref.dtype)

    if a_blk:
        a_spec = (pl.BlockSpec((None, tk, tm), lambda i, j, k: (i, k, 0)) if ta
                  else pl.BlockSpec((None, tm, tk), lambda i, j, k: (k, i, 0)))
    else:
        a_spec = (pl.BlockSpec((tk, tm), lambda i, j, k: (k, i)) if ta
                  else pl.BlockSpec((tm, tk), lambda i, j, k: (i, k)))
    if b_blk:
        b_spec = (pl.BlockSpec((None, tn, tk), lambda i, j, k: (k, j, 0)) if tb
                  else pl.BlockSpec((None, tk, tn), lambda i, j, k: (j, k, 0)))
    else:
        b_spec = (pl.BlockSpec((tn, tk), lambda i, j, k: (j, k)) if tb
                  else pl.BlockSpec((tk, tn), lambda i, j, k: (k, j)))
    if o_blk:
        o_spec = pl.BlockSpec((None, tm, tn), lambda i, j, k: (j, i, 0))
        o_shape = SDS((n_dim // tn, m_dim, tn), out_dtype)
    else:
        o_spec = pl.BlockSpec((tm, tn), lambda i, j, k: (i, j))
        o_shape = SDS((m_dim, n_dim), out_dtype)
    return pl.pallas_call(
        body, grid=(m_dim // tm, n_dim // tn, nk),
        in_specs=[a_spec, b_spec] + ([] if after is None else [pl.BlockSpec(memory_space=pl.ANY)]),
        out_specs=o_spec,
        out_shape=o_shape,
        scratch_shapes=[pltpu.VMEM((tm, tn), F32)],
        compiler_params=_params(("parallel", "parallel", "arbitrary")),
        name=name,
    )(a, b, *([] if after is None else [after]))


def _row(width, col=0):
    return pl.BlockSpec((TR, width), lambda i: (i, col))


def _vec(width):
    return pl.BlockSpec((1, width), lambda i: (0, 0))


def _norm_mod_fwd(x, g, shift, scale, name):
    t = x.shape[0]

    def body(x_ref, g_ref, sh_ref, sc_ref, o_ref):
        xv = x_ref[...]
        r = lax.rsqrt(jnp.mean(xv * xv, axis=-1, keepdims=True) + RMS_EPS)
        o_ref[...] = (((xv * r) * g_ref[...]) * (1.0 + sc_ref[...]) + sh_ref[...]).astype(o_ref.dtype)

    return pl.pallas_call(
        body, grid=(t // TR,), in_specs=[_row(D), _vec(D), _vec(D), _vec(D)], out_specs=_row(D),
        out_shape=SDS((t, D), BF16), compiler_params=_params(("parallel",)), name=name,
    )(x, g, shift, scale)


def _norm_mod_bwd(x, g, scale, dh, dx_in, name):
    t = x.shape[0]

    def body(x_ref, g_ref, sc_ref, dh_ref, dxi_ref, dx_ref, dsh_ref, dsc_ref, dg_ref):
        @pl.when(pl.program_id(0) == 0)
        def _():
            dsh_ref[...] = jnp.zeros_like(dsh_ref)
            dsc_ref[...] = jnp.zeros_like(dsc_ref)
            dg_ref[...] = jnp.zeros_like(dg_ref)

        xv = x_ref[...]
        gv = g_ref[...]
        dh = dh_ref[...]
        r = lax.rsqrt(jnp.mean(xv * xv, axis=-1, keepdims=True) + RMS_EPS)
        n = xv * r
        dsh_ref[...] += jnp.sum(dh, axis=0, keepdims=True)
        dsc_ref[...] += jnp.sum(dh * (n * gv), axis=0, keepdims=True)
        tt = dh * (1.0 + sc_ref[...])
        dg_ref[...] += jnp.sum(tt * n, axis=0, keepdims=True)
        dn = tt * gv
        dx_ref[...] = dxi_ref[...] + r * (dn - n * jnp.mean(dn * n, axis=-1, keepdims=True))

    return pl.pallas_call(
        body, grid=(t // TR,), in_specs=[_row(D), _vec(D), _vec(D), _row(D), _row(D)],
        out_specs=[_row(D), _vec(D), _vec(D), _vec(D)],
        out_shape=[SDS((t, D), F32), SDS((1, D), F32), SDS((1, D), F32), SDS((1, D), F32)],
        compiler_params=_params(("arbitrary",)), name=name,
    )(x, g, scale, dh, dx_in)


def _swiglu_up(h, w_in, name, after=None):
    t = h.shape[0]
    tm = _pick(t, (1024, 512, 256))
    half = NDEV // 2
    extra = [] if after is None else [after]

    def body(h_ref, wg_ref, wu_ref, *rest):
        u_ref, a_ref = rest[-2:]
        hv = h_ref[...]
        gate = _dg(hv, wg_ref[...], NN)
        up = _dg(hv, wu_ref[...], NN)
        u_ref[0] = gate.astype(u_ref.dtype)
        u_ref[1] = up.astype(u_ref.dtype)
        a_ref[...] = (_silu(gate) * up).astype(a_ref.dtype)

    return pl.pallas_call(
        body, grid=(t // tm, half),
        in_specs=[pl.BlockSpec((tm, D), lambda i, j: (i, 0)),
                  pl.BlockSpec((None, D, FB), lambda i, j: (j, 0, 0)),
                  pl.BlockSpec((None, D, FB), lambda i, j: (j + half, 0, 0))]
        + [pl.BlockSpec(memory_space=pl.ANY)] * len(extra),
        out_specs=[pl.BlockSpec((2, None, tm, FB), lambda i, j: (0, j, i, 0)),
                   pl.BlockSpec((None, tm, FB), lambda i, j: (j, i, 0))],
        out_shape=[SDS((2, half, t, FB), BF16), SDS((half, t, FB), BF16)],
        compiler_params=_params(("parallel", "parallel")), name=name,
    )(h, w_in, w_in, *extra)


def _swiglu_down_bwd(dy, w_out, u, name, after=None):
    t = dy.shape[0]
    tm = _pick(t, (1024, 512, 256))
    half = NDEV // 2
    extra = [] if after is None else [after]
    pair = pl.BlockSpec((2, None, tm, FB), lambda i, j: (0, j, i, 0))

    def body(dy_ref, w_ref, u_ref, *rest):
        o_ref = rest[-1]
        da = _dg(dy_ref[...], w_ref[...], NT)
        gate = u_ref[0].astype(F32)
        o_ref[0] = (da * u_ref[1].astype(F32) * _dsilu(gate)).astype(o_ref.dtype)
        o_ref[1] = (da * _silu(gate)).astype(o_ref.dtype)

    return pl.pallas_call(
        body, grid=(t // tm, half),
        in_specs=[pl.BlockSpec((tm, D), lambda i, j: (i, 0)), pl.BlockSpec((FB, D), lambda i, j: (j, 0)), pair]
        + [pl.BlockSpec(memory_space=pl.ANY)] * len(extra),
        out_specs=pair, out_shape=SDS((2, half, t, FB), BF16),
        compiler_params=_params(("parallel", "parallel")), name=name,
    )(dy, w_out, u, *extra)


def _resid_fwd(x, y, gate, coef, name):
    t = x.shape[0]

    def body(x_ref, y_ref, g_ref, o_ref):
        o_ref[...] = x_ref[...] + (coef * g_ref[...]) * y_ref[...]

    return pl.pallas_call(
        body, grid=(t // TR,), in_specs=[_row(D), _row(D), _vec(D)], out_specs=_row(D),
        out_shape=SDS((t, D), F32), compiler_params=_params(("parallel",)), name=name,
    )(x, y, gate)


def _resid_bwd(dx, y, gate, coef, name):
    t = dx.shape[0]

    def body(dx_ref, y_ref, g_ref, dy_ref, dg_ref):
        @pl.when(pl.program_id(0) == 0)
        def _():
            dg_ref[...] = jnp.zeros_like(dg_ref)

        dxv = dx_ref[...]
        dy_ref[...] = ((coef * g_ref[...]) * dxv).astype(dy_ref.dtype)
        dg_ref[...] += jnp.sum((coef * dxv) * y_ref[...], axis=0, keepdims=True)

    return pl.pallas_call(
        body, grid=(t // TR,), in_specs=[_row(D), _row(D), _vec(D)], out_specs=[_row(D), _vec(D)],
        out_shape=[SDS((t, D), BF16), SDS((1, D), F32)],
        compiler_params=_params(("arbitrary",)), name=name,
    )(dx, y, gate)


def _final_loss(x, fg, target, name):
    t = x.shape[0]
    nt = t // TR

    def body(x_ref, g_ref, t_ref, loss_ref, dx_ref, dg_ref, acc_ref):
        i = pl.program_id(0)

        @pl.when(i == 0)
        def _():
            acc_ref[...] = jnp.zeros_like(acc_ref)
            dg_ref[...] = jnp.zeros_like(dg_ref)

        xv = x_ref[...]
        gv = g_ref[...]
        r = lax.rsqrt(jnp.mean(xv * xv, axis=-1, keepdims=True) + RMS_EPS)
        n = xv * r
        err = n * gv - t_ref[...]
        acc_ref[...] += jnp.sum(err * err, axis=0, keepdims=True)
        dy = err * (1.0 / D)
        dg_ref[...] += jnp.sum(dy * n, axis=0, keepdims=True)
        dn = dy * gv
        dx_ref[...] = r * (dn - n * jnp.mean(dn * n, axis=-1, keepdims=True))

        @pl.when(i == nt - 1)
        def _():
            tot = jnp.sum(acc_ref[...], axis=1, keepdims=True) * (0.5 / D)
            loss_ref[...] = jnp.broadcast_to(tot, loss_ref.shape)

    return pl.pallas_call(
        body, grid=(nt,), in_specs=[_row(D), _vec(D), _row(D)],
        out_specs=[_vec(128), _row(D), _vec(D)],
        out_shape=[SDS((1, 128), F32), SDS((t, D), F32), SDS((1, D), F32)],
        scratch_shapes=[pltpu.VMEM((1, D), F32)],
        compiler_params=_params(("arbitrary",)), name=name,
    )(x, fg, target)


def _halo_prev(width, col):
    per = TR // HALO
    return pl.BlockSpec((HALO, width), lambda i: (jnp.maximum(i * per - 1, 0), col))


def _halo_next(width, col, nt):
    per = TR // HALO
    return pl.BlockSpec((HALO, width), lambda i: (jnp.minimum((i + 1) * per, nt * per - 1), col))


def _pool_windows(ext, tile_index):
    rows = _iota((TR, PG), 0) + tile_index * TR + 1
    pooled, counts = [], []
    for gi in range(4):
        w = 2 << gi
        e = ext[:, gi * PG:(gi + 1) * PG]
        s = e
        step = 1
        while step < w:
            s = s + pltpu.roll(s, step, 0)
            step *= 2
        cnt = jnp.minimum(rows, w).astype(F32)
        pooled.append(s[HALO:] / cnt - e[HALO:])
        counts.append(cnt)
    return pooled, counts


def _pool_fwd(proj, pool_w, pool_scale, pool_proj, name):
    t = proj.shape[0]
    xcol = OFF_XP // PW

    def body(x_ref, h_ref, pw_ref, ps_ref, pp_ref, o_ref):
        i = pl.program_id(0)
        halo = jnp.where(i > 0, h_ref[...], 0.0)
        ext = jnp.concatenate([halo, x_ref[...]], axis=0)
        pooled, _ = _pool_windows(ext, i)
        mixed = [_dg(pooled[g].astype(BF16), pw_ref[g].astype(BF16), NN) for g in range(4)]
        ypre = jnp.concatenate(mixed, axis=1) * ps_ref[...]
        o_ref[...] = _dg(ypre.astype(BF16), pp_ref[...], NN)

    return pl.pallas_call(
        body, grid=(t // TR,),
        in_specs=[_row(PW, xcol), _halo_prev(PW, xcol),
                  pl.BlockSpec((4, PG, PG), lambda i: (0, 0, 0)), _vec(PW),
                  pl.BlockSpec((PW, D), lambda i: (0, 0))],
        out_specs=_row(D), out_shape=SDS((t, D), F32),
        compiler_params=_params(("parallel",)), name=name,
    )(proj, proj, pool_w, pool_scale, pool_proj)


def _pool_bwd_local(proj, pool_w, pool_scale, pool_proj, dya, name):
    t = proj.shape[0]
    xcol = OFF_XP // PW

    def body(x_ref, h_ref, pw_ref, ps_ref, pp_ref, dya_ref, dwin_ref, dpl_ref, dpw_ref, dps_ref, dpp_ref):
        i = pl.program_id(0)

        @pl.when(i == 0)
        def _():
            dpw_ref[...] = jnp.zeros_like(dpw_ref)
            dps_ref[...] = jnp.zeros_like(dps_ref)
            dpp_ref[...] = jnp.zeros_like(dpp_ref)

        halo = jnp.where(i > 0, h_ref[...], 0.0)
        ext = jnp.concatenate([halo, x_ref[...]], axis=0)
        pooled, counts = _pool_windows(ext, i)
        mixed = jnp.concatenate(
            [_dg(pooled[g].astype(BF16), pw_ref[g].astype(BF16), NN) for g in range(4)], axis=1)
        ps = ps_ref[...]
        ypre = mixed * ps
        dyab = dya_ref[...].astype(BF16)
        dypre = _dg(dyab, pp_ref[...], NT)
        dpp_ref[...] += _dg(ypre.astype(BF16), dyab, TN)
        dps_ref[...] += jnp.sum(dypre * mixed, axis=0, keepdims=True)
        dmixed = dypre * ps
        for g in range(4):
            dm = dmixed[:, g * PG:(g + 1) * PG].astype(BF16)
            dpw_ref[g] += _dg(pooled[g].astype(BF16), dm, TN)
            dpooled = _dg(dm, pw_ref[g].astype(BF16), NT)
            dwin_ref[:, g * PG:(g + 1) * PG] = dpooled / counts[g]
            dpl_ref[:, g * PG:(g + 1) * PG] = dpooled

    return pl.pallas_call(
        body, grid=(t // TR,),
        in_specs=[_row(PW, xcol), _halo_prev(PW, xcol),
                  pl.BlockSpec((4, PG, PG), lambda i: (0, 0, 0)), _vec(PW),
                  pl.BlockSpec((PW, D), lambda i: (0, 0)), _row(D)],
        out_specs=[_row(PW), _row(PW), pl.BlockSpec((4, PG, PG), lambda i: (0, 0, 0)), _vec(PW),
                   pl.BlockSpec((PW, D), lambda i: (0, 0))],
        out_shape=[SDS((t, PW), F32), SDS((t, PW), F32), SDS((4, PG, PG), F32), SDS((1, PW), F32),
                   SDS((PW, D), F32)],
        compiler_params=_params(("arbitrary",)), name=name,
    )(proj, proj, pool_w, pool_scale, pool_proj, dya)


def _pool_bwd_window(dwin, dpl, name):
    t = dwin.shape[0]
    nt = t // TR
    ext_rows = TR + HALO

    def body(dw_ref, h_ref, dp_ref, o_ref):
        i = pl.program_id(0)
        halo = jnp.where(i < nt - 1, h_ref[...], 0.0)
        ext = jnp.concatenate([dw_ref[...], halo], axis=0)
        for gi in range(4):
            w = 2 << gi
            s = ext[:, gi * PG:(gi + 1) * PG]
            step = 1
            while step < w:
                s = s + pltpu.roll(s, ext_rows - step, 0)
                step *= 2
            o_ref[:, gi * PG:(gi + 1) * PG] = (s[:TR] - dp_ref[:, gi * PG:(gi + 1) * PG]).astype(o_ref.dtype)

    return pl.pallas_call(
        body, grid=(nt,), in_specs=[_row(PW), _halo_next(PW, 0, nt), _row(PW)], out_specs=_row(PW),
        out_shape=SDS((t, PW), BF16), compiler_params=_params(("parallel",)), name=name,
    )(dwin, dwin, dpl)


def _conv_group(ext, cw_ref, cols):
    acc = cw_ref[3:4, cols] * ext
    for j in range(3):
        acc = acc + cw_ref[j:j + 1, cols] * pltpu.roll(ext, 3 - j, 0)
    return acc[HALO:]


def _gate_terms(raw, al, dt):
    beta = _sigmoid(raw)
    xg = raw + dt
    sp = jnp.maximum(xg, 0.0) + jnp.log(1.0 + jnp.exp(-jnp.abs(xg)))
    g = -jnp.exp(al) * sp
    return beta, g, _sigmoid(xg)


def _dn_pre_fwd(proj, conv_w, al_row, dt_row, name):
    t = proj.shape[0]

    def body(x_ref, h_ref, cw_ref, ba_ref, al_ref, dt_ref, q_ref, k_ref, v_ref, bg_ref):
        i = pl.program_id(0)
        keep = i > 0
        for grp in range(24):
            cols = slice(grp * HD, (grp + 1) * HD)
            ext = jnp.concatenate([jnp.where(keep, h_ref[:, cols], 0.0), x_ref[:, cols]], axis=0)
            s = _silu(_conv_group(ext, cw_ref, cols))
            seg, head = divmod(grp, NH)
            hc = slice(head * HD, (head + 1) * HD)
            if seg == 0:
                q_ref[:, hc] = s * lax.rsqrt(jnp.sum(s * s, axis=-1, keepdims=True) + L2_EPS) * (HD ** -0.5)
            elif seg == 1:
                k_ref[:, hc] = s * lax.rsqrt(jnp.sum(s * s, axis=-1, keepdims=True) + L2_EPS)
            else:
                v_ref[:, hc] = s
        lane = _iota((TR, 128), 1)
        rowc = _iota((TR, 128), 0) % CH
        beta, g, _ = _gate_terms(ba_ref[...], al_ref[...], dt_ref[...])
        step = 1
        while step < CH:
            g = g + jnp.where(rowc >= step, pltpu.roll(g, step, 0), 0.0)
            step *= 2
        bg_ref[...] = jnp.where(lane < NH, beta, jnp.where(lane < 2 * NH, g, 0.0))

    return pl.pallas_call(
        body, grid=(t // TR,),
        in_specs=[_row(3 * D, 0), _halo_prev(3 * D, 0), pl.BlockSpec((4, 3 * D), lambda i: (0, 0)),
                  _row(128, OFF_BA // 128), _vec(128), _vec(128)],
        out_specs=[_row(D), _row(D), _row(D), _row(128)],
        out_shape=[SDS((t, D), F32), SDS((t, D), F32), SDS((t, D), F32), SDS((t, 128), F32)],
        compiler_params=_params(("parallel",)), name=name,
    )(proj, proj, conv_w, proj, al_row, dt_row)


def _dn_pre_bwd_act(proj, conv_w, al_row, dt_row, dq, dk, dv, dbg, name):
    t = proj.shape[0]

    def body(x_ref, h_ref, cw_ref, ba_ref, al_ref, dt_ref, dq_ref, dk_ref, dv_ref, dbg_ref,
             dc_ref, draw_ref, dal_ref, ddt_ref):
        i = pl.program_id(0)

        @pl.when(i == 0)
        def _():
            dal_ref[...] = jnp.zeros_like(dal_ref)
            ddt_ref[...] = jnp.zeros_like(ddt_ref)

        keep = i > 0
        for grp in range(24):
            cols = slice(grp * HD, (grp + 1) * HD)
            ext = jnp.concatenate([jnp.where(keep, h_ref[:, cols], 0.0), x_ref[:, cols]], axis=0)
            cv = _conv_group(ext, cw_ref, cols)
            seg, head = divmod(grp, NH)
            hc = slice(head * HD, (head + 1) * HD)
            if seg == 2:
                ds = dv_ref[:, hc]
            else:
                s = _silu(cv)
                r = lax.rsqrt(jnp.sum(s * s, axis=-1, keepdims=True) + L2_EPS)
                dy = dq_ref[:, hc] if seg == 0 else dk_ref[:, hc]
                c = (HD ** -0.5) if seg == 0 else 1.0
                ds = (c * r) * (dy - s * ((r * r) * jnp.sum(dy * s, axis=-1, keepdims=True)))
            dc_ref[:, cols] = ds * _dsilu(cv)
        lane = _iota((TR, 128), 1)
        rowc = _iota((TR, 128), 0) % CH
        isb = lane < NH
        isg = jnp.logical_and(lane >= NH, lane < 2 * NH)
        beta, g, sg = _gate_terms(ba_ref[...], al_ref[...], dt_ref[...])
        dbgv = dbg_ref[...]
        dg = dbgv
        step = 1
        while step < CH:
            dg = dg + jnp.where(rowc < CH - step, pltpu.roll(dg, TR - step, 0), 0.0)
            step *= 2
        da_raw = dg * (-jnp.exp(al_ref[...])) * sg
        draw_ref[...] = jnp.where(isb, dbgv * beta * (1.0 - beta), jnp.where(isg, da_raw, 0.0)).astype(draw_ref.dtype)
        dal_ref[...] += jnp.sum(jnp.where(isg, dg * g, 0.0), axis=0, keepdims=True)
        ddt_ref[...] += jnp.sum(jnp.where(isg, da_raw, 0.0), axis=0, keepdims=True)

    return pl.pallas_call(
        body, grid=(t // TR,),
        in_specs=[_row(3 * D, 0), _halo_prev(3 * D, 0), pl.BlockSpec((4, 3 * D), lambda i: (0, 0)),
                  _row(128, OFF_BA // 128), _vec(128), _vec(128), _row(D), _row(D), _row(D), _row(128)],
        out_specs=[_row(3 * D), _row(128), _vec(128), _vec(128)],
        out_shape=[SDS((t, 3 * D), F32), SDS((t, 128), BF16), SDS((1, 128), F32), SDS((1, 128), F32)],
        compiler_params=_params(("arbitrary",)), name=name,
    )(proj, proj, conv_w, proj, al_row, dt_row, dq, dk, dv, dbg)


def _dn_pre_bwd_conv(proj, conv_w, dconv, name):
    t = proj.shape[0]
    nt = t // TR
    ext_rows = TR + HALO

    def body(x_ref, h_ref, cw_ref, dc_ref, dn_ref, dx_ref, dcw_ref):
        i = pl.program_id(0)

        @pl.when(i == 0)
        def _():
            dcw_ref[...] = jnp.zeros_like(dcw_ref)

        keep_prev = i > 0
        keep_next = i < nt - 1
        for grp in range(24):
            cols = slice(grp * HD, (grp + 1) * HD)
            dct = dc_ref[:, cols]
            dext = jnp.concatenate([dct, jnp.where(keep_next, dn_ref[:, cols], 0.0)], axis=0)
            acc = cw_ref[3:4, cols] * dext
            for j in range(3):
                acc = acc + cw_ref[j:j + 1, cols] * pltpu.roll(dext, ext_rows - (3 - j), 0)
            dx_ref[:, cols] = acc[:TR].astype(dx_ref.dtype)
            xext = jnp.concatenate([jnp.where(keep_prev, h_ref[:, cols], 0.0), x_ref[:, cols]], axis=0)
            for j in range(4):
                xs = xext if j == 3 else pltpu.roll(xext, 3 - j, 0)
                dcw_ref[j:j + 1, cols] += jnp.sum(xs[HALO:] * dct, axis=0, keepdims=True)

    return pl.pallas_call(
        body, grid=(nt,),
        in_specs=[_row(3 * D, 0), _halo_prev(3 * D, 0), pl.BlockSpec((4, 3 * D), lambda i: (0, 0)),
                  _row(3 * D), _halo_next(3 * D, 0, nt)],
        out_specs=[_row(3 * D), pl.BlockSpec((4, 3 * D), lambda i: (0, 0))],
        out_shape=[SDS((t, 3 * D), BF16), SDS((4, 3 * D), F32)],
        compiler_params=_params(("arbitrary",)), name=name,
    )(proj, proj, conv_w, dconv, dconv)


def _dn_post_fwd(o, proj, gn, name):
    t = o.shape[0]

    def body(o_ref, z_ref, g_ref, out_ref):
        gv = g_ref[...]
        for h in range(NH):
            hc = slice(h * HD, (h + 1) * HD)
            ov = o_ref[:, hc]
            r = lax.rsqrt(jnp.mean(ov * ov, axis=-1, keepdims=True) + RMS_EPS)
            out_ref[:, hc] = (((ov * r) * gv) * _silu(z_ref[:, hc])).astype(out_ref.dtype)

    return pl.pallas_call(
        body, grid=(t // TR,), in_specs=[_row(D), _row(D, OFF_Z // D), _vec(HD)], out_specs=_row(D),
        out_shape=SDS((t, D), BF16), compiler_params=_params(("parallel",)), name=name,
    )(o, proj, gn)


def _dn_post_bwd(o, proj, gn, dob, name):
    t = o.shape[0]

    def body(o_ref, z_ref, g_ref, d_ref, do_ref, dz_ref, dg_ref):
        @pl.when(pl.program_id(0) == 0)
        def _():
            dg_ref[...] = jnp.zeros_like(dg_ref)

        gv = g_ref[...]
        acc = jnp.zeros((1, HD), F32)
        for h in range(NH):
            hc = slice(h * HD, (h + 1) * HD)
            ov = o_ref[:, hc]
            zv = z_ref[:, hc]
            dv = d_ref[:, hc]
            r = lax.rsqrt(jnp.mean(ov * ov, axis=-1, keepdims=True) + RMS_EPS)
            n = ov * r
            dz_ref[:, hc] = (dv * (n * gv) * _dsilu(zv)).astype(dz_ref.dtype)
            dng = dv * _silu(zv)
            acc = acc + jnp.sum(dng * n, axis=0, keepdims=True)
            dn = dng * gv
            do_ref[:, hc] = r * (dn - n * jnp.mean(dn * n, axis=-1, keepdims=True))
        dg_ref[...] += acc

    return pl.pallas_call(
        body, grid=(t // TR,), in_specs=[_row(D), _row(D, OFF_Z // D), _vec(HD), _row(D)],
        out_specs=[_row(D), _row(D), _vec(HD)],
        out_shape=[SDS((t, D), F32), SDS((t, D), BF16), SDS((1, HD), F32)],
        compiler_params=_params(("arbitrary",)), name=name,
    )(o, proj, gn, dob)


def _merge_fwd(ya, yb, proj, name):
    t = ya.shape[0]

    def body(a_ref, b_ref, gp_ref, gd_ref, o_ref):
        o_ref[...] = (_sigmoid(gp_ref[...]) * a_ref[...] + _sigmoid(gd_ref[...]) * b_ref[...]).astype(o_ref.dtype)

    return pl.pallas_call(
        body, grid=(t // TR,), in_specs=[_row(D), _row(D), _row(D, OFF_GP // D), _row(D, OFF_GD // D)],
        out_specs=_row(D), out_shape=SDS((t, D), BF16),
        compiler_params=_params(("parallel",)), name=name,
    )(ya, yb, proj, proj)


def _merge_bwd(dm, ya, yb, proj, name):
    t = ya.shape[0]

    def body(d_ref, a_ref, b_ref, gp_ref, gd_ref, da_ref, db_ref, dgp_ref, dgd_ref):
        dv = d_ref[...]
        sp = _sigmoid(gp_ref[...])
        sd = _sigmoid(gd_ref[...])
        da_ref[...] = dv * sp
        db_ref[...] = (dv * sd).astype(db_ref.dtype)
        dgp_ref[...] = (dv * a_ref[...] * sp * (1.0 - sp)).astype(dgp_ref.dtype)
        dgd_ref[...] = (dv * b_ref[...] * sd * (1.0 - sd)).astype(dgd_ref.dtype)

    return pl.pallas_call(
        body, grid=(t // TR,),
        in_specs=[_row(D), _row(D), _row(D), _row(D, OFF_GP // D), _row(D, OFF_GD // D)],
        out_specs=[_row(D)] * 4,
        out_shape=[SDS((t, D), F32), SDS((t, D), BF16), SDS((t, D), BF16), SDS((t, D), BF16)],
        compiler_params=_params(("parallel",)), name=name,
    )(dm, ya, yb, proj, proj)


def _split2(x):
    hi = x.astype(BF16)
    return hi, (x - hi.astype(F32)).astype(BF16)


def _dot3(a, b, dims):
    ah, al = _split2(a)
    bh, bl = _split2(b)
    return _dg(ah, bh, dims) + (_dg(ah, bl, dims) + _dg(al, bh, dims))


def _neumann_inverses(mats):
    ri = _iota((CH, CH), 0)
    ci = _iota((CH, CH), 1)
    eye = jnp.where(ri == ci, 1.0, 0.0).astype(F32)
    xs = [-a for a in mats]
    ps = [eye + x for x in xs]
    for _ in range(5):
        xs = [_dot3(x, x, NN) for x in xs]
        ps = [p + _dot3(p, x, NN) for p, x in zip(ps, xs)]
    return ps


def _solve_with(inv):
    @jax.custom_vjp
    def solve(a, rhs):
        return _dot3(inv, rhs, NN)

    def fwd(a, rhs):
        sol = _dot3(inv, rhs, NN)
        return sol, sol

    def bwd(sol, d):
        drhs = _dot3(inv, d, TN)
        return -_dot3(drhs, sol, NT), drhs

    solve.defvjp(fwd, bwd)
    return solve


@jax.custom_vjp
def _rows_to_lanes(g64):
    ri = _iota((CH, CH), 0)
    ci = _iota((CH, CH), 1)
    diag = jnp.where(ri == ci, g64, 0.0)
    ones = jnp.ones((CH, CH), BF16)
    hi = diag.astype(BF16)
    rem = diag - hi.astype(F32)
    mid = rem.astype(BF16)
    lo = (rem - mid.astype(F32)).astype(BF16)
    return _dg(ones, hi, NN) + (_dg(ones, mid, NN) + _dg(ones, lo, NN))


def _rows_to_lanes_bwd(_, d):
    ri = _iota((CH, CH), 0)
    ci = _iota((CH, CH), 1)
    return (jnp.where(ri == ci, jnp.broadcast_to(jnp.sum(d, axis=0, keepdims=True), (CH, CH)), 0.0),)


_rows_to_lanes.defvjp(lambda g64: (_rows_to_lanes(g64), None), _rows_to_lanes_bwd)


def _chunk_local(solve_all, q, k, v, g128, g64, gl128, b128, b64):
    ri = _iota((CH, CH), 0)
    ci = _iota((CH, CH), 1)
    causal = ri >= ci
    strict = ri > ci
    gj = [_rows_to_lanes(g) for g in g64]
    decay = [jnp.where(causal, jnp.exp(jnp.where(causal, g - t, 0.0)), 0.0) for g, t in zip(g64, gj)]
    kk = [_nt(x, x) for x in k]
    a = [jnp.where(strict, b * m * dc, 0.0) for b, m, dc in zip(b64, kk, decay)]
    eg = [jnp.exp(g) for g in g128]
    rhs = [jnp.concatenate([b * x, (b * e) * y], axis=1) for b, x, e, y in zip(b128, v, eg, k)]
    sol = solve_all(a, rhs)
    qk = [jnp.where(causal, _nt(x, y) * dc, 0.0) for x, y, dc in zip(q, k, decay)]
    return ([s[:, :HD] for s in sol], [s[:, HD:] for s in sol], qk, [x * e for x, e in zip(q, eg)],
            [x * jnp.exp(gl - g) for x, gl, g in zip(k, gl128, g128)], [jnp.exp(gl) for gl in gl128])


def _all_head_gates(bgv):
    return tuple(list(z) for z in zip(*[_head_gates(bgv, h) for h in range(NH)]))


def _head_gates(bgv, h):
    lane = _iota((CH, 128), 1)
    row = _iota((CH, 128), 0)
    bcol = jnp.sum(jnp.where(lane == h, bgv, 0.0), axis=1, keepdims=True)
    gcol = jnp.sum(jnp.where(lane == NH + h, bgv, 0.0), axis=1, keepdims=True)
    g128 = jnp.broadcast_to(gcol, (CH, 128))
    gl128 = jnp.broadcast_to(jnp.sum(jnp.where(row == CH - 1, g128, 0.0), axis=0, keepdims=True), (CH, 128))
    return (g128, jnp.broadcast_to(gcol, (CH, CH)), gl128,
            jnp.broadcast_to(bcol, (CH, 128)), jnp.broadcast_to(bcol, (CH, CH)))


def _chunk_specs():
    row = pl.BlockSpec((CH, D), lambda i: (i, 0))
    small = pl.BlockSpec((CH, 128), lambda i: (i, 0))
    qk = pl.BlockSpec((NH, CH, CH), lambda i: (i, 0, 0))
    eg = pl.BlockSpec((1, NH, 128), lambda i: (i, 0, 0))
    return row, small, qk, eg


def _dn_local_fwd(q, k, v, bg, name):
    t = q.shape[0]
    n = t // CH

    def body(q_ref, k_ref, v_ref, bg_ref, u_ref, w_ref, qk_ref, qd_ref, kd_ref, eg_ref, inv_ref):
        cols = [slice(h * HD, (h + 1) * HD) for h in range(NH)]

        def solve_all(mats, rhs):
            invs = _neumann_inverses(mats)
            for h in range(NH):
                inv_ref[h] = invs[h]
            return [_dot3(m, r, NN) for m, r in zip(invs, rhs)]

        u, w, qk, qd, kd, egl = _chunk_local(
            solve_all, [q_ref[:, c] for c in cols], [k_ref[:, c] for c in cols], [v_ref[:, c] for c in cols],
            *_all_head_gates(bg_ref[...]))
        for h, hc in enumerate(cols):
            u_ref[:, hc] = u[h]
            w_ref[:, hc] = w[h].astype(w_ref.dtype)
            qd_ref[:, hc] = qd[h].astype(qd_ref.dtype)
            kd_ref[:, hc] = kd[h].astype(kd_ref.dtype)
            qk_ref[h] = qk[h].astype(qk_ref.dtype)
            eg_ref[0, h:h + 1, :] = egl[h][0:1, :]

    row, small, qkb, egb = _chunk_specs()
    return pl.pallas_call(
        body, grid=(n,), in_specs=[row, row, row, small], out_specs=[row, row, qkb, row, row, egb, qkb],
        out_shape=[SDS((t, D), F32), SDS((t, D), BF16), SDS((n * NH, CH, CH), BF16), SDS((t, D), BF16),
                   SDS((t, D), BF16), SDS((n, NH, 128), F32), SDS((n * NH, CH, CH), F32)],
        compiler_params=_params(("parallel",)), name=name,
    )(q, k, v, bg)


def _dn_local_bwd(q, k, v, bg, inv, du, dw, dqk, dqd, dkd, deg, name):
    t = q.shape[0]
    n = t // CH

    def body(q_ref, k_ref, v_ref, bg_ref, inv_ref, du_ref, dw_ref, dqk_ref, dqd_ref, dkd_ref, deg_ref,
             dq_ref, dk_ref, dv_ref, dbg_ref):
        bgv = bg_ref[...]
        lane = _iota((CH, 128), 1)
        row = _iota((CH, 128), 0)
        first = jnp.where(row == 0, 1.0, 0.0)
        acc = jnp.zeros((CH, 128), F32)
        cols = [slice(h * HD, (h + 1) * HD) for h in range(NH)]
        solves = [_solve_with(inv_ref[h]) for h in range(NH)]

        def solve_all(mats, rhs):
            return [f(m, r) for f, m, r in zip(solves, mats, rhs)]

        _, vjp = jax.vjp(functools.partial(_chunk_local, solve_all),
                         [q_ref[:, c] for c in cols], [k_ref[:, c] for c in cols], [v_ref[:, c] for c in cols],
                         *_all_head_gates(bgv))
        cts = ([du_ref[:, c] for c in cols], [dw_ref[:, c] for c in cols], [dqk_ref[h] for h in range(NH)],
               [dqd_ref[:, c] for c in cols], [dkd_ref[:, c] for c in cols],
               [jnp.broadcast_to(deg_ref[0, h:h + 1, :], (CH, 128)) * first for h in range(NH)])
        dq, dk, dv, dg128, dg64, dgl, db128, db64 = vjp(cts)
        for h, hc in enumerate(cols):
            dq_ref[:, hc] = dq[h]
            dk_ref[:, hc] = dk[h]
            dv_ref[:, hc] = dv[h]
            dg = jnp.sum(dg128[h], axis=1, keepdims=True) + jnp.sum(dg64[h], axis=1, keepdims=True)
            tot = jnp.sum(jnp.sum(dgl[h], axis=0, keepdims=True), axis=1, keepdims=True)
            dg = dg + jnp.where(row[:, 0:1] == CH - 1, tot, 0.0)
            db = jnp.sum(db128[h], axis=1, keepdims=True) + jnp.sum(db64[h], axis=1, keepdims=True)
            acc = acc + jnp.where(lane == h, db, 0.0) + jnp.where(lane == NH + h, dg, 0.0)
        dbg_ref[...] = acc

    row, small, qkb, egb = _chunk_specs()
    return pl.pallas_call(
        body, grid=(n,), in_specs=[row, row, row, small, qkb, row, row, qkb, row, row, egb],
        out_specs=[row, row, row, small],
        out_shape=[SDS((t, D), F32)] * 3 + [SDS((t, 128), F32)],
        compiler_params=_params(("parallel",)), name=name,
    )(q, k, v, bg, inv, du, dw, dqk, dqd, dkd, deg)


def _state_step(s, u, w, qk, qd, kd, egl):
    ws = [_nn(a, b) for a, b in zip(w, s)]
    v_new = [a - b for a, b in zip(u, ws)]
    qs = [_nn(a, b) for a, b in zip(qd, s)]
    intra = [_nn(a, b) for a, b in zip(qk, v_new)]
    upd = [_tn(a, b) for a, b in zip(kd, v_new)]
    return [a * e + b for a, e, b in zip(s, egl, upd)], [a + b for a, b in zip(qs, intra)]


def _dn_scan_fwd(u, w, qk, qd, kd, eg, name):
    t = u.shape[0]
    n = t // CH

    def body(u_ref, w_ref, qk_ref, qd_ref, kd_ref, eg_ref, o_ref, save_ref, s_ref):
        @pl.when(pl.program_id(0) == 0)
        def _():
            s_ref[...] = jnp.zeros_like(s_ref)

        cols = [slice(h * HD, (h + 1) * HD) for h in range(NH)]
        s = [s_ref[h] for h in range(NH)]
        for h in range(NH):
            save_ref[0, h] = s[h]
        s_new, o = _state_step(
            s, [u_ref[:, c] for c in cols], [w_ref[:, c].astype(F32) for c in cols],
            [qk_ref[h].astype(F32) for h in range(NH)], [qd_ref[:, c].astype(F32) for c in cols],
            [kd_ref[:, c].astype(F32) for c in cols], [eg_ref[0, h:h + 1, :] for h in range(NH)])
        for h, hc in enumerate(cols):
            o_ref[:, hc] = o[h]
            s_ref[h] = s_new[h]

    row, _, qkb, egb = _chunk_specs()
    return pl.pallas_call(
        body, grid=(n,), in_specs=[row, row, qkb, row, row, egb],
        out_specs=[row, pl.BlockSpec((1, NH, HD, HD), lambda i: (i, 0, 0, 0))],
        out_shape=[SDS((t, D), F32), SDS((n, NH, HD, HD), F32)],
        scratch_shapes=[pltpu.VMEM((NH, HD, HD), F32)],
        compiler_params=_params(("arbitrary",)), name=name,
    )(u, w, qk, qd, kd, eg)


def _dn_scan_bwd(u, w, qk, qd, kd, eg, saved, do, name):
    t = u.shape[0]
    n = t // CH

    def body(u_ref, w_ref, qk_ref, qd_ref, kd_ref, eg_ref, sv_ref, do_ref,
             du_ref, dw_ref, dqk_ref, dqd_ref, dkd_ref, deg_ref, ds_ref):
        @pl.when(pl.program_id(0) == 0)
        def _():
            ds_ref[...] = jnp.zeros_like(ds_ref)

        cols = [slice(h * HD, (h + 1) * HD) for h in range(NH)]
        _, vjp = jax.vjp(
            _state_step, [sv_ref[0, h] for h in range(NH)], [u_ref[:, c] for c in cols],
            [w_ref[:, c].astype(F32) for c in cols], [qk_ref[h].astype(F32) for h in range(NH)],
            [qd_ref[:, c].astype(F32) for c in cols], [kd_ref[:, c].astype(F32) for c in cols],
            [eg_ref[0, h:h + 1, :] for h in range(NH)])
        ds, du, dw, dqk, dqd, dkd, deg = vjp(([ds_ref[h] for h in range(NH)], [do_ref[:, c] for c in cols]))
        for h, hc in enumerate(cols):
            ds_ref[h] = ds[h]
            du_ref[:, hc] = du[h]
            dw_ref[:, hc] = dw[h]
            dqk_ref[h] = dqk[h]
            dqd_ref[:, hc] = dqd[h]
            dkd_ref[:, hc] = dkd[h]
            deg_ref[0, h:h + 1, :] = deg[h]

    rev = lambda i: (n - 1 - i, 0)
    rev3 = lambda i: (n - 1 - i, 0, 0)
    row = pl.BlockSpec((CH, D), rev)
    qkb = pl.BlockSpec((NH, CH, CH), rev3)
    egb = pl.BlockSpec((1, NH, 128), rev3)
    return pl.pallas_call(
        body, grid=(n,),
        in_specs=[row, row, qkb, row, row, egb,
                  pl.BlockSpec((1, NH, HD, HD), lambda i: (n - 1 - i, 0, 0, 0)), row],
        out_specs=[row, row, qkb, row, row, egb],
        out_shape=[SDS((t, D), F32), SDS((t, D), F32), SDS((n * NH, CH, CH), F32), SDS((t, D), F32),
                   SDS((t, D), F32), SDS((n, NH, 128), F32)],
        scratch_shapes=[pltpu.VMEM((NH, HD, HD), F32)],
        compiler_params=_params(("arbitrary",)), name=name,
    )(u, w, qk, qd, kd, eg, saved, do)


def _ada_fwd(c_all, ada_w, ada_b, name):
    ncol = ada_w.shape[1]

    def body(c_ref, w_ref, b_ref, o_ref):
        o_ref[...] = _dg(_silu(c_ref[...]), w_ref[...], NN, HI) + b_ref[...]

    return pl.pallas_call(body, out_shape=SDS((NDEV, ncol), F32),
                          compiler_params=pltpu.CompilerParams(vmem_limit_bytes=VMEM_LIMIT), name=name,
                          )(c_all, ada_w, ada_b)


def _ada_bwd(c_all_t, dmod, name):
    ncol = dmod.shape[1]

    def body(c_ref, d_ref, o_ref):
        sc = _silu(c_ref[...])
        acc = sc[:, 0:1] * d_ref[0:1, :]
        for b in range(1, NDEV):
            acc = acc + sc[:, b:b + 1] * d_ref[b:b + 1, :]
        o_ref[...] = acc

    return pl.pallas_call(body, out_shape=SDS((D, ncol), F32),
                          compiler_params=pltpu.CompilerParams(vmem_limit_bytes=VMEM_LIMIT), name=name,
                          )(c_all_t, dmod)


def _sum_devices(parts, out_dtype, name):
    _, r, c = parts.shape
    tr = TR if r % TR == 0 else r

    def body(p_ref, o_ref):
        acc = p_ref[0].astype(F32)
        for i in range(1, NDEV):
            acc = acc + p_ref[i].astype(F32)
        o_ref[...] = acc.astype(o_ref.dtype)

    return pl.pallas_call(
        body, grid=(r // tr,), in_specs=[pl.BlockSpec((NDEV, tr, c), lambda i: (0, i, 0))],
        out_specs=pl.BlockSpec((tr, c), lambda i: (i, 0)), out_shape=SDS((r, c), out_dtype),
        compiler_params=_params(("parallel",)), name=name,
    )(parts)


def _adamw(w, g, m, v, name):
    r, c = w.shape
    tr = _pick(r, (256, 128, 88, 8)) if r % 8 == 0 else r
    bc1 = 1.0 - ADAM_B1 ** ADAM_STEP
    bc2 = 1.0 - ADAM_B2 ** ADAM_STEP

    def body(w_ref, g_ref, m_ref, v_ref, d_ref, nm_ref, nv_ref):
        gv = g_ref[...]
        m_new = ADAM_B1 * m_ref[...] + (1.0 - ADAM_B1) * gv
        v_new = ADAM_B2 * v_ref[...] + (1.0 - ADAM_B2) * (gv * gv)
        nm_ref[...] = m_new
        nv_ref[...] = v_new
        d_ref[...] = -ADAM_LR * ((m_new / bc1) / (jnp.sqrt(v_new / bc2) + ADAM_EPS) + ADAM_WD * w_ref[...])

    spec = pl.BlockSpec((tr, c), lambda i: (i, 0))
    return pl.pallas_call(
        body, grid=(r // tr,), in_specs=[spec] * 4, out_specs=[spec] * 3,
        out_shape=[SDS((r, c), F32)] * 3, compiler_params=_params(("parallel",)), name=name,
    )(w, g, m, v)


ANY = pl.BlockSpec(memory_space=pl.ANY)
MESH = pl.DeviceIdType.MESH


def _all_gather(xs, name, after=None):
    n = len(xs)
    extra = [] if after is None else [after]

    def body(*refs):
        x_refs, out_refs = refs[:n], refs[n + len(extra):2 * n + len(extra)]
        send_sems, recv_sems, local_sems = refs[-3:]
        mx, my, mc = lax.axis_index("x"), lax.axis_index("y"), lax.axis_index("c")
        me, sibling = (mx, my, mc), (mx, my, 1 - mc)
        chips = [(1 - mx, my), (mx, 1 - my), (1 - mx, 1 - my)]

        def rows(a, px, py, pc):
            return out_refs[a].at[4 * px + 2 * py + pc]

        def copy(a, k, block, to, src=None):
            return pltpu.make_async_remote_copy(
                src_ref=rows(a, *block) if src is None else src, dst_ref=rows(a, *block),
                send_sem=send_sems.at[a, k], recv_sem=recv_sems.at[a, k], device_id=to, device_id_type=MESH)

        mine = [pltpu.make_async_copy(x_refs[a], rows(a, *me), local_sems.at[a]) for a in range(n)]
        for cp in mine:
            cp.start()
        first = []
        for a in range(n):
            first.append(copy(a, 0, me, sibling, src=x_refs[a]))
            first += [copy(a, 1 + j, me, (*chip, mc), src=x_refs[a]) for j, chip in enumerate(chips)]
        for cp in first:
            cp.start()
        passed = []
        for a in range(n):
            for j, chip in enumerate(chips):
                copy(a, 1 + j, (*chip, mc), me).wait_recv()
                passed.append(copy(a, 4 + j, (*chip, mc), sibling))
                passed[-1].start()
        for a in range(n):
            copy(a, 0, sibling, me).wait_recv()
            for j, chip in enumerate(chips):
                copy(a, 4 + j, (*chip, 1 - mc), me).wait_recv()
        for cp in first + passed:
            cp.wait_send()
        for cp in mine:
            cp.wait()

    return pl.pallas_call(
        body, out_shape=[SDS((NDEV,) + x.shape, x.dtype) for x in xs], in_specs=[ANY] * (n + len(extra)),
        out_specs=[ANY] * n,
        scratch_shapes=[pltpu.SemaphoreType.DMA((n, 7)), pltpu.SemaphoreType.DMA((n, 7)),
                        pltpu.SemaphoreType.DMA((n,))],
        name=name,
    )(*xs, *extra)


HBM = pl.BlockSpec(memory_space=pltpu.HBM)
SEM = pl.BlockSpec(memory_space=pltpu.SEMAPHORE)
EFFECT = pltpu.SideEffectType.DATAFLOW_SIDE_EFFECTING


def _peers():
    mx, my, mc = lax.axis_index("x"), lax.axis_index("y"), lax.axis_index("c")
    out = []
    for k in range(1, NDEV):
        out.append((1 - mx if k & 4 else mx, 1 - my if k & 2 else my, 1 - mc if k & 1 else mc))
    return 4 * mx + 2 * my + mc, out


def _push_start(srcs, sliced, name, after=None):
    n = len(srcs)
    extra = [] if after is None else [after]
    me_idx = 4 * lax.axis_index("x") + 2 * lax.axis_index("y") + lax.axis_index("c")
    lands = []
    for s in srcs:
        blk = lax.dynamic_index_in_dim(s, me_idx, 0, keepdims=True) if sliced else s[None]
        shape = s.shape if sliced else (NDEV,) + s.shape
        lands.append(lax.dynamic_update_slice(lax.empty(shape, s.dtype), blk, (me_idx,) + (0,) * (len(shape) - 1)))

    def body(*refs):
        src_refs, land_refs = refs[:n], refs[n:2 * n]
        outs = refs[2 * n + len(extra):]
        send_sems, recv_sems = outs[:n], outs[n:2 * n]
        token = refs[-1]
        me, peers = _peers()
        for a in range(n):
            for k, (px, py, pc) in enumerate(peers):
                src = src_refs[a].at[4 * px + 2 * py + pc] if sliced else src_refs[a]
                pltpu.make_async_remote_copy(
                    src_ref=src, dst_ref=land_refs[a].at[me], send_sem=send_sems[a].at[k],
                    recv_sem=recv_sems[a].at[k], device_id=(px, py, pc), device_id_type=MESH).start()
        token[...] = jnp.zeros_like(token)

    outs = pl.pallas_call(
        body, name=name,
        out_shape=([pltpu.SemaphoreType.DMA((NDEV - 1,))] * (2 * n)
                   + [pltpu.HBM(s.shape, s.dtype) for s in srcs] + [pltpu.HBM(l.shape, l.dtype) for l in lands]
                   + [SDS((8, 128), F32)]),
        in_specs=[HBM] * (2 * n) + [pl.BlockSpec(memory_space=pl.ANY)] * len(extra),
        out_specs=[SEM] * (2 * n) + [HBM] * (2 * n) + [pl.BlockSpec(memory_space=pltpu.VMEM)],
        input_output_aliases={i: 2 * n + i for i in range(2 * n)},
        compiler_params=pltpu.CompilerParams(has_side_effects=EFFECT),
    )(*[pltpu.with_memory_space_constraint(s, pltpu.HBM) for s in srcs],
      *[pltpu.with_memory_space_constraint(l, pltpu.HBM) for l in lands], *extra)
    sends, recvs = outs[:n], outs[n:2 * n]
    src_thru, land_thru = outs[2 * n:3 * n], outs[3 * n:4 * n]
    return [(sends[a], recvs[a], src_thru[a], land_thru[a]) for a in range(n)], outs[-1]


def _push_wait(started, sliced, after, name):
    n = len(started)

    def body(*refs):
        src_refs, land_refs = refs[:n], refs[n:2 * n]
        send_sems, recv_sems = refs[2 * n:3 * n], refs[3 * n:4 * n]
        me, peers = _peers()
        for a in range(n):
            for k, (px, py, pc) in enumerate(peers):
                src = src_refs[a].at[4 * px + 2 * py + pc] if sliced else src_refs[a]
                cp = pltpu.make_async_remote_copy(
                    src_ref=src, dst_ref=land_refs[a].at[me], send_sem=send_sems[a].at[k],
                    recv_sem=recv_sems[a].at[k], device_id=(px, py, pc), device_id_type=MESH)
                cp.wait_send()
                cp.wait_recv()

    srcs = [s[2] for s in started]
    lands = [s[3] for s in started]
    outs = pl.pallas_call(
        body, name=name,
        out_shape=[pltpu.HBM(s.shape, s.dtype) for s in srcs] + [pltpu.HBM(l.shape, l.dtype) for l in lands],
        in_specs=[HBM] * (2 * n) + [SEM] * (2 * n) + [pl.BlockSpec(memory_space=pl.ANY)],
        out_specs=[HBM] * (2 * n),
        input_output_aliases={i: i for i in range(2 * n)},
        compiler_params=pltpu.CompilerParams(has_side_effects=EFFECT),
    )(*srcs, *lands, *[s[0] for s in started], *[s[1] for s in started], after)
    return outs[n:]


def _cols_from_blocks(blocks):
    _, rows, w = blocks.shape
    return blocks.transpose(1, 0, 2).reshape(rows, NDEV * w)


def _cols_to_blocks(full):
    rows, total = full.shape
    return full.reshape(rows, NDEV, total // NDEV).transpose(1, 0, 2)


def _mix_pad(wt):
    xp, q, k, v, z, ba, gp, gd = jnp.split(wt, (512, 1536, 2560, 3584, 4608, 4624, 5648), axis=0)
    pad = jnp.zeros((MIXP - OFF_BA - 16, wt.shape[1]), wt.dtype)
    return jnp.concatenate([q, k, v, z, gp, gd, xp, ba, pad], axis=0)


def _mix_unpad(wt):
    q, k, v, z, gp, gd, xp, ba = (wt[OFF_Q:OFF_K], wt[OFF_K:OFF_V], wt[OFF_V:OFF_Z], wt[OFF_Z:OFF_GP],
                                  wt[OFF_GP:OFF_GD], wt[OFF_GD:OFF_XP], wt[OFF_XP:OFF_BA], wt[OFF_BA:OFF_BA + 16])
    return jnp.concatenate([xp, q, k, v, z, ba, gp, gd], axis=0)


def _lane_row(vec8):
    return jnp.zeros((1, 128), F32).at[0, NH:2 * NH].set(vec8)


def _ffn_fwd(x, g, shift, scale, gate, w_in, w_out, tag, token=None):
    t = x.shape[0]
    h = _norm_mod_fwd(x, g, shift, scale, f"{tag}_norm")
    if isinstance(w_in, tuple):
        w_in, = _push_wait([w_in], False, h, f"{tag}_gather_wait_in")
    u, a = _swiglu_up(h, w_in, f"{tag}_up", after=token)
    w_out, = _push_wait([w_out], False, a, f"{tag}_gather_wait_out")
    w_out = w_out.reshape(FH, D)
    y = _matmul(a, w_out, a_blk=True, out_dtype=F32, name=f"{tag}_down")
    return _resid_fwd(x, y, gate, 0.5, f"{tag}_res"), (h, u, a, y), w_in, w_out


def _ffn_bwd(dx_out, x, g, scale, gate, w_in, w_out, saved, tag):
    h, u, a, y = saved
    t = x.shape[0]
    dy, dgate = _resid_bwd(dx_out, y, gate, 0.5, f"{tag}_res_bwd")
    dw_out = _matmul(a, dy, ta=True, a_blk=True, out_dtype=BF16, name=f"{tag}_down_dw")
    sent_out, token = _push_start([dw_out.reshape(NDEV, FH // NDEV, D)], True, f"{tag}_grad_start_out")
    du = _swiglu_down_bwd(dy, w_out, u, f"{tag}_down_dx", after=token).reshape(NDEV, t, FB)
    dw_in = _matmul(h, du, ta=True, b_blk=True, o_blk=True, out_dtype=BF16, name=f"{tag}_up_dw")
    sent_in, token = _push_start([dw_in], True, f"{tag}_grad_start_in")
    dh = _matmul(du, w_in, tb=True, a_blk=True, b_blk=True, out_dtype=F32, name=f"{tag}_up_dx", after=token)
    dx, dshift, dscale, dg = _norm_mod_bwd(x, g, scale, dh, dx_out, f"{tag}_norm_bwd")
    return dx, (dshift, dscale, dgate), dg, sent_in + sent_out


def kernel(x, c, ada_w, ada_b, norm_g, ffn1_w_in, ffn1_w_out, ffn2_w_in, ffn2_w_out, mix_w_in, conv_w, a_log, dt_bias, dn_norm_g, pool_w, pool_scale, pool_proj, dn_proj, mix_w_out, final_g, loss_target, m_ada_w, m_ada_b, m_norm_g, m_ffn1_w_in, m_ffn1_w_out, m_ffn2_w_in, m_ffn2_w_out, m_mix_w_in, m_conv_w, m_a_log, m_dt_bias, m_dn_norm_g, m_pool_w, m_pool_scale, m_pool_proj, m_dn_proj, m_mix_w_out, m_final_g, v_ada_w, v_ada_b, v_norm_g, v_ffn1_w_in, v_ffn1_w_out, v_ffn2_w_in, v_ffn2_w_out, v_mix_w_in, v_conv_w, v_a_log, v_dt_bias, v_dn_norm_g, v_pool_w, v_pool_scale, v_pool_proj, v_dn_proj, v_mix_w_out, v_final_g):
    me = 4 * lax.axis_index("x") + 2 * lax.axis_index("y") + lax.axis_index("c")
    x0 = x[0]
    target = loss_target[0]
    t = x0.shape[0]

    big = [ffn1_w_in[0], ffn1_w_out[0], ffn2_w_in[0], ffn2_w_out[0], mix_w_in[0], pool_proj[0], dn_proj[0],
           mix_w_out[0]]
    small = jnp.concatenate([c.reshape(8, 128), conv_w[0].reshape(12, 128), norm_g[0].reshape(3, 128),
                             jnp.zeros((1, 128), F32)], axis=0)
    small_all, = _all_gather([small], "gather_small")
    c_all = small_all[:, 0:8, :].reshape(NDEV, D)
    conv_full = small_all[:, 8:20, :].reshape(NDEV, 4, 384).transpose(1, 0, 2).reshape(4, 3 * D)
    norm_full = small_all[:, 20:23, :].reshape(NDEV, 3, 128).transpose(1, 0, 2).reshape(3, D)

    ncol = ada_w.shape[2]
    ada_b_mine = lax.dynamic_slice(ada_b, (0, me * ncol), (1, ncol))
    mod_cols = _ada_fwd(c_all, ada_w[0], ada_b_mine, "ada_fwd")
    mod_all, w_in1 = _all_gather([mod_cols, big[0].astype(BF16)], "gather_mod_first_weight")
    order = [1, 4, 5, 6, 7, 2, 3]
    payload = {i: (big[i].T if i == 4 else big[i]).astype(BF16) for i in order}
    started, token = _push_start([payload[i] for i in order], False, "gather_start", after=mod_all)
    started = {i: s for i, s in zip(order, started)}
    mod = lax.dynamic_index_in_dim(mod_all, me, axis=1, keepdims=False).reshape(9, D)
    shift = [mod[3 * s:3 * s + 1] for s in range(3)]
    scale = [mod[3 * s + 1:3 * s + 2] for s in range(3)]
    gate = [mod[3 * s + 2:3 * s + 3] for s in range(3)]
    ng = [norm_full[s:s + 1] for s in range(3)]
    fg = final_g.reshape(1, D)
    al_row = _lane_row(a_log[0])
    dt_row = _lane_row(dt_bias[0])
    gn = dn_norm_g
    pw = pool_w[0]
    ps = pool_scale

    x1, saved1, w_in1, w_out1 = _ffn_fwd(x0, ng[0], shift[0], scale[0], gate[0], w_in1, started[1], "ffn1", token)

    h1 = _norm_mod_fwd(x1, ng[1], shift[1], scale[1], "mix_norm")
    seg, = _push_wait([started[4]], False, h1, "mix_gather_wait")
    w_mix = _mix_pad(seg.reshape(MIX_RAW, D))
    proj = _matmul(h1, w_mix, tb=True, out_dtype=F32, name="mix_in")
    qh, kh, vh, bg = _dn_pre_fwd(proj, conv_full, al_row, dt_row, "dn_pre")
    seg = _push_wait([started[i] for i in (5, 6, 7)], False, qh, "mix_gather_wait_rest")
    w_pp = _cols_from_blocks(seg[0])
    w_dn = seg[1].reshape(D, D)
    w_mo = seg[2].reshape(D, D)
    ya = _pool_fwd(proj, pw, ps, w_pp, "pool_fwd")
    u, w, qk, qd, kd, eg, inv = _dn_local_fwd(qh, kh, vh, bg, "dn_local")
    o, s_saved = _dn_scan_fwd(u, w, qk, qd, kd, eg, "dn_scan")
    ob = _dn_post_fwd(o, proj, gn, "dn_post")
    yb = _matmul(ob, w_dn, out_dtype=F32, name="dn_out")
    merged = _merge_fwd(ya, yb, proj, "merge")
    mix_y = _matmul(merged, w_mo, out_dtype=F32, name="mix_out")
    x2 = _resid_fwd(x1, mix_y, gate[1], 1.0, "mix_res")

    x3, saved2, w_in2, w_out2 = _ffn_fwd(x2, ng[2], shift[2], scale[2], gate[2], started[2], started[3], "ffn2")
    loss_row, dx3, dfg = _final_loss(x3, fg, target, "loss")

    dx2, dmod2, dng2, sent2 = _ffn_bwd(dx3, x2, ng[2], scale[2], gate[2], w_in2, w_out2, saved2, "ffn2")

    dmy, dgate1 = _resid_bwd(dx2, mix_y, gate[1], 1.0, "mix_res_bwd")
    dmerged = _matmul(dmy, w_mo, tb=True, out_dtype=F32, name="mix_out_dx")
    dw_mo = _matmul(merged, dmy, ta=True, out_dtype=BF16, name="mix_out_dw")
    dya, dyb, dgp, dgd = _merge_bwd(dmerged, ya, yb, proj, "merge_bwd")
    dob = _matmul(dyb, w_dn, tb=True, out_dtype=F32, name="dn_out_dx")
    dw_dn = _matmul(ob, dyb, ta=True, out_dtype=BF16, name="dn_out_dw")
    do, dz, dgn = _dn_post_bwd(o, proj, gn, dob, "dn_post_bwd")
    du, dw, dqk, dqd, dkd, deg = _dn_scan_bwd(u, w, qk, qd, kd, eg, s_saved, do, "dn_scan_bwd")
    dqh, dkh, dvh, dbg = _dn_local_bwd(qh, kh, vh, bg, inv, du, dw, dqk, dqd, dkd, deg, "dn_local_bwd")
    dconv, draw, dal, ddt = _dn_pre_bwd_act(proj, conv_full, al_row, dt_row, dqh, dkh, dvh, dbg, "dn_pre_bwd_act")
    dqkv, dcw = _dn_pre_bwd_conv(proj, conv_full, dconv, "dn_pre_bwd_conv")
    dwin, dpl, dpw, dps, dpp = _pool_bwd_local(proj, pw, ps, w_pp, dya, "pool_bwd_local")
    dxp = _pool_bwd_window(dwin, dpl, "pool_bwd_window")
    dproj = jnp.concatenate([dqkv, dz, dgp, dgd, dxp, draw, jnp.zeros((t, MIXP - OFF_BA - 128), BF16)], axis=1)
    dw_mix = _matmul(dproj, h1, ta=True, out_dtype=BF16, name="mix_in_dw")
    sent1, token = _push_start(
        [_mix_unpad(dw_mix).reshape(NDEV, MIX_RAW // NDEV, D), _cols_to_blocks(dpp.astype(BF16)),
         dw_dn.reshape(NDEV, -1, D), dw_mo.reshape(NDEV, -1, D)], True, "mix_grad_start")
    dh1 = _matmul(dproj, w_mix, out_dtype=F32, name="mix_in_dx", after=token)
    dx1, dsh1, dsc1, dng1 = _norm_mod_bwd(x1, ng[1], scale[1], dh1, dx2, "mix_norm_bwd")

    dx0, dmod0, dng0, sent0 = _ffn_bwd(dx1, x0, ng[0], scale[0], gate[0], w_in1, w_out1, saved1, "ffn1")

    dmod = jnp.concatenate([*dmod0, dsh1, dsc1, dgate1, *dmod2], axis=1).reshape(-1)
    flat = jnp.concatenate([
        dmod, dal[0, NH:2 * NH], ddt[0, NH:2 * NH], dgn.reshape(-1), dps.reshape(-1), dfg.reshape(-1),
        dpw.reshape(-1), jnp.concatenate([dng0, dng1, dng2], axis=0).reshape(-1), dcw.reshape(-1)])
    nflat = 90 * D
    flat = jnp.concatenate([flat, jnp.zeros((nflat - flat.shape[0],), F32)]).reshape(90, D)
    flat_all, = _all_gather([flat], "gather_small_grads")
    tot = _sum_devices(flat_all, F32, "sum_small_grads").reshape(-1)
    dmod_all = flat_all.reshape(NDEV, nflat)[:, :9 * D]
    dmod_cols = lax.dynamic_slice(dmod_all, (0, me * ncol), (NDEV, ncol))
    g_ada_w = _ada_bwd(c_all.T, dmod_cols, "ada_bwd")

    p = 0
    pieces = {}
    for nm, size in (("ada_b", 9 * D), ("a_log", NH), ("dt_bias", NH), ("dn_norm_g", HD), ("pool_scale", PW),
                     ("final_g", D), ("pool_w", 4 * PG * PG), ("norm_g", 3 * D), ("conv_w", 12 * D)):
        pieces[nm] = tot[p:p + size]
        p += size
    g_norm = lax.dynamic_slice(pieces["norm_g"].reshape(3, D), (0, me * 128), (3, 128))
    g_conv = lax.dynamic_slice(pieces["conv_w"].reshape(4, 3 * D), (0, me * 384), (4, 384))

    grads = {
        "ada_w": g_ada_w.reshape(ada_w.shape), "ada_b": pieces["ada_b"].reshape(ada_b.shape),
        "norm_g": g_norm.reshape(norm_g.shape), "conv_w": g_conv.reshape(conv_w.shape),
        "a_log": pieces["a_log"].reshape(a_log.shape), "dt_bias": pieces["dt_bias"].reshape(dt_bias.shape),
        "dn_norm_g": pieces["dn_norm_g"].reshape(dn_norm_g.shape), "pool_w": pieces["pool_w"].reshape(pool_w.shape),
        "pool_scale": pieces["pool_scale"].reshape(pool_scale.shape),
        "final_g": pieces["final_g"].reshape(final_g.shape),
    }
    weights = {"ada_w": ada_w, "ada_b": ada_b, "norm_g": norm_g, "ffn1_w_in": ffn1_w_in, "ffn1_w_out": ffn1_w_out,
               "ffn2_w_in": ffn2_w_in, "ffn2_w_out": ffn2_w_out, "mix_w_in": mix_w_in, "conv_w": conv_w,
               "a_log": a_log, "dt_bias": dt_bias, "dn_norm_g": dn_norm_g, "pool_w": pool_w,
               "pool_scale": pool_scale, "pool_proj": pool_proj, "dn_proj": dn_proj, "mix_w_out": mix_w_out,
               "final_g": final_g}
    m_in = {"ada_w": m_ada_w, "ada_b": m_ada_b, "norm_g": m_norm_g, "ffn1_w_in": m_ffn1_w_in,
            "ffn1_w_out": m_ffn1_w_out, "ffn2_w_in": m_ffn2_w_in, "ffn2_w_out": m_ffn2_w_out,
            "mix_w_in": m_mix_w_in, "conv_w": m_conv_w, "a_log": m_a_log, "dt_bias": m_dt_bias,
            "dn_norm_g": m_dn_norm_g, "pool_w": m_pool_w, "pool_scale": m_pool_scale, "pool_proj": m_pool_proj,
            "dn_proj": m_dn_proj, "mix_w_out": m_mix_w_out, "final_g": m_final_g}
    v_in = {"ada_w": v_ada_w, "ada_b": v_ada_b, "norm_g": v_norm_g, "ffn1_w_in": v_ffn1_w_in,
            "ffn1_w_out": v_ffn1_w_out, "ffn2_w_in": v_ffn2_w_in, "ffn2_w_out": v_ffn2_w_out,
            "mix_w_in": v_mix_w_in, "conv_w": v_conv_w, "a_log": v_a_log, "dt_bias": v_dt_bias,
            "dn_norm_g": v_dn_norm_g, "pool_w": v_pool_w, "pool_scale": v_pool_scale, "pool_proj": v_pool_proj,
            "dn_proj": v_dn_proj, "mix_w_out": v_mix_w_out, "final_g": v_final_g}

    names = list(weights)
    large = ("ada_w", "ffn1_w_in", "ffn1_w_out", "ffn2_w_in", "ffn2_w_out", "mix_w_in", "pool_proj", "dn_proj",
             "mix_w_out")
    delta, new_m, new_v = {}, {}, {}

    def update(nm):
        shp = weights[nm].shape
        two_d = (shp[-2], shp[-1])
        d_, m_, v_ = _adamw(weights[nm].reshape(two_d), grads[nm].reshape(two_d), m_in[nm].reshape(two_d),
                            v_in[nm].reshape(two_d), f"adamw_{nm}")
        delta[nm], new_m[nm], new_v[nm] = d_.reshape(shp), m_.reshape(shp), v_.reshape(shp)
        return d_

    def reduce(sent, group, after, tag):
        for nm, r in zip(group, _push_wait(sent, True, after, f"{tag}_grad_wait")):
            total = _sum_devices(r, F32, f"sum_grads_{nm}")
            grads[nm] = (total.T if nm == "mix_w_in" else total).reshape(weights[nm].shape)

    done = update("ada_w")
    reduce(sent2, ("ffn2_w_in", "ffn2_w_out"), done, "ffn2")
    update("ffn2_w_in")
    done = update("ffn2_w_out")
    reduce(sent1, ("mix_w_in", "pool_proj", "dn_proj", "mix_w_out"), done, "mix")
    for nm in ("mix_w_in", "pool_proj", "dn_proj", "mix_w_out"):
        done = update(nm)
    reduce(sent0, ("ffn1_w_in", "ffn1_w_out"), done, "ffn1")
    update("ffn1_w_in")
    update("ffn1_w_out")
    rest = [nm for nm in names if nm not in large]
    total = sum(weights[nm].size for nm in rest)
    padded = -(-total // D) * D

    def pack(tree, fill):
        flat_ = jnp.concatenate([tree[nm].reshape(-1) for nm in rest])
        return jnp.concatenate([flat_, jnp.full((padded - total,), fill, F32)]).reshape(-1, D)

    d_, m_, v_ = _adamw(pack(weights, 0.0), pack(grads, 0.0), pack(m_in, 0.0), pack(v_in, 1.0), "adamw_small")
    p = 0
    for nm in rest:
        size = weights[nm].size
        shp = weights[nm].shape
        delta[nm] = d_.reshape(-1)[p:p + size].reshape(shp)
        new_m[nm] = m_.reshape(-1)[p:p + size].reshape(shp)
        new_v[nm] = v_.reshape(-1)[p:p + size].reshape(shp)
        p += size

    loss = lax.psum(loss_row[0, 0], ("x", "y", "c"))
    grad_x = dx0.reshape(x.shape)
    return (loss, grad_x, *[grads[nm] for nm in names], *[delta[nm] for nm in names],
            *[new_m[nm] for nm in names], *[new_v[nm] for nm in names])
```

```python
import functools

import jax
import jax.numpy as jnp
from jax import lax
from jax.experimental import pallas as pl
from jax.experimental.pallas import tpu as pltpu

F32 = jnp.float32
BF16 = jnp.bfloat16
SDS = jax.ShapeDtypeStruct
HI = lax.Precision.HIGHEST

D = 1024
FH = 2816
FB = 704
NH = 8
HD = 128
CH = 64
NDEV = 8
PW = 512
PG = 128
RMS_EPS = 1e-6
L2_EPS = 1e-6
TR = 512
HALO = 16
VMEM_LIMIT = 56 * 1024 * 1024

MIXP = 6912
OFF_Q, OFF_K, OFF_V, OFF_Z, OFF_GP, OFF_GD, OFF_XP, OFF_BA = 0, 1024, 2048, 3072, 4096, 5120, 6144, 6656
MIX_RAW = 6672

ADAM_LR = 0.001
ADAM_B1 = 0.9
ADAM_B2 = 0.999
ADAM_EPS = 1e-08
ADAM_WD = 0.01
ADAM_STEP = 10

NN = (((1,), (0,)), ((), ()))
NT = (((1,), (1,)), ((), ()))
TN = (((0,), (0,)), ((), ()))


def _dg(a, b, dims, prec=None):
    return lax.dot_general(a, b, dims, precision=prec, preferred_element_type=F32)


def _make_dots(prec):
    @jax.custom_vjp
    def nn(a, b):
        return _dg(a, b, NN, prec)

    @jax.custom_vjp
    def nt(a, b):
        return _dg(a, b, NT, prec)

    @jax.custom_vjp
    def tn(a, b):
        return _dg(a, b, TN, prec)

    nn.defvjp(lambda a, b: (nn(a, b), (a, b)), lambda r, d: (nt(d, r[1]), tn(r[0], d)))
    nt.defvjp(lambda a, b: (nt(a, b), (a, b)), lambda r, d: (nn(d, r[1]), tn(d, r[0])))
    tn.defvjp(lambda a, b: (tn(a, b), (a, b)), lambda r, d: (nt(r[1], d), nn(r[0], d)))
    return nn, nt, tn


_nn, _nt, _tn = _make_dots(None)


def _params(sem):
    return pltpu.CompilerParams(dimension_semantics=sem, vmem_limit_bytes=VMEM_LIMIT)


def _sigmoid(x):
    return 1.0 / (1.0 + jnp.exp(-x))


def _silu(x):
    return x * _sigmoid(x)


def _dsilu(x):
    s = _sigmoid(x)
    return s * (1.0 + x * (1.0 - s))


def _pick(n, cands):
    for c in cands:
        if n % c == 0:
            return c
    raise ValueError(f"no tile for {n}")


def _iota(shape, dim):
    return lax.broadcasted_iota(jnp.int32, shape, dim)


def _matmul(a, b, *, ta=False, tb=False, a_blk=False, b_blk=False, o_blk=False, tm=None, tn=None, tk=None,
            out_dtype, name, after=None):
    if a_blk:
        nb, r, cb = a.shape
        if ta:
            k_dim, m_dim, tm = r, nb * cb, cb
        else:
            m_dim, k_dim, tk = r, nb * cb, cb
    else:
        k_dim, m_dim = a.shape if ta else a.shape[::-1]
    if b_blk:
        nb, r, cb = b.shape
        if tb:
            n_dim, tk = r, cb
            assert nb * cb == k_dim
        else:
            n_dim, tn = nb * cb, cb
            assert r == k_dim
    else:
        n_dim = b.shape[0] if tb else b.shape[1]
    tm = tm or _pick(m_dim, (1024, 768, 512, 256, 128))
    tn = tn or _pick(n_dim, (1024, 768, 512, 256, 128))
    tk = tk or (k_dim if (k_dim <= 2816 and not ta) else _pick(k_dim, (2816, 2304, 1024, 512, 256)))
    nk = k_dim // tk
    dims = ((((0,) if ta else (1,)), ((1,) if tb else (0,))), ((), ()))

    def body(a_ref, b_ref, *rest):
        o_ref, acc_ref = rest[-2:]
        k = pl.program_id(2)

        @pl.when(k == 0)
        def _():
            acc_ref[...] = jnp.zeros_like(acc_ref)

        acc_ref[...] += lax.dot_general(a_ref[...].astype(BF16), b_ref[...].astype(BF16), dims,
                                        preferred_element_type=F32)

        @pl.when(k == nk - 1)
        def _():
            o_ref[...] = acc_ref[...].astype(o_ref.dtype)

    if a_blk:
        a_spec = (pl.BlockSpec((None, tk, tm), lambda i, j, k: (i, k, 0)) if ta
                  else pl.BlockSpec((None, tm, tk), lambda i, j, k: (k, i, 0)))
    else:
        a_spec = (pl.BlockSpec((tk, tm), lambda i, j, k: (k, i)) if ta
                  else pl.BlockSpec((tm, tk), lambda i, j, k: (i, k)))
    if b_blk:
        b_spec = (pl.BlockSpec((None, tn, tk), lambda i, j, k: (k, j, 0)) if tb
                  else pl.BlockSpec((None, tk, tn), lambda i, j, k: (j, k, 0)))
    else:
        b_spec = (pl.BlockSpec((tn, tk), lambda i, j, k: (j, k)) if tb
                  else pl.BlockSpec((tk, tn), lambda i, j, k: (k, j)))
    if o_blk:
        o_spec = pl.BlockSpec((None, tm, tn), lambda i, j, k: (j, i, 0))
        o_shape = SDS((n_dim // tn, m_dim, tn), out_dtype)
    else:
        o_spec = pl.BlockSpec((tm, tn), lambda i, j, k: (i, j))
        o_shape = SDS((m_dim, n_dim), out_dtype)
    return pl.pallas_call(
        body, grid=(m_dim // tm, n_dim // tn, nk),
        in_specs=[a_spec, b_spec] + ([] if after is None else [pl.BlockSpec(memory_space=pl.ANY)]),
        out_specs=o_spec,
        out_shape=o_shape,
        scratch_shapes=[pltpu.VMEM((tm, tn), F32)],
        compiler_params=_params(("parallel", "parallel", "arbitrary")),
        name=name,
    )(a, b, *([] if after is None else [after]))


def _row(width, col=0):
    return pl.BlockSpec((TR, width), lambda i: (i, col))


def _vec(width):
    return pl.BlockSpec((1, width), lambda i: (0, 0))


def _norm_mod_fwd(x, g, shift, scale, name):
    t = x.shape[0]

    def body(x_ref, g_ref, sh_ref, sc_ref, o_ref):
        xv = x_ref[...]
        r = lax.rsqrt(jnp.mean(xv * xv, axis=-1, keepdims=True) + RMS_EPS)
        o_ref[...] = (((xv * r) * g_ref[...]) * (1.0 + sc_ref[...]) + sh_ref[...]).astype(o_ref.dtype)

    return pl.pallas_call(
        body, grid=(t // TR,), in_specs=[_row(D), _vec(D), _vec(D), _vec(D)], out_specs=_row(D),
        out_shape=SDS((t, D), BF16), compiler_params=_params(("parallel",)), name=name,
    )(x, g, shift, scale)


def _norm_mod_bwd(x, g, scale, dh, dx_in, name):
    t = x.shape[0]

    def body(x_ref, g_ref, sc_ref, dh_ref, dxi_ref, dx_ref, dsh_ref, dsc_ref, dg_ref):
        @pl.when(pl.program_id(0) == 0)
        def _():
            dsh_ref[...] = jnp.zeros_like(dsh_ref)
            dsc_ref[...] = jnp.zeros_like(dsc_ref)
            dg_ref[...] = jnp.zeros_like(dg_ref)

        xv = x_ref[...]
        gv = g_ref[...]
        dh = dh_ref[...]
        r = lax.rsqrt(jnp.mean(xv * xv, axis=-1, keepdims=True) + RMS_EPS)
        n = xv * r
        dsh_ref[...] += jnp.sum(dh, axis=0, keepdims=True)
        dsc_ref[...] += jnp.sum(dh * (n * gv), axis=0, keepdims=True)
        tt = dh * (1.0 + sc_ref[...])
        dg_ref[...] += jnp.sum(tt * n, axis=0, keepdims=True)
        dn = tt * gv
        dx_ref[...] = dxi_ref[...] + r * (dn - n * jnp.mean(dn * n, axis=-1, keepdims=True))

    return pl.pallas_call(
        body, grid=(t // TR,), in_specs=[_row(D), _vec(D), _vec(D), _row(D), _row(D)],
        out_specs=[_row(D), _vec(D), _vec(D), _vec(D)],
        out_shape=[SDS((t, D), F32), SDS((1, D), F32), SDS((1, D), F32), SDS((1, D), F32)],
        compiler_params=_params(("arbitrary",)), name=name,
    )(x, g, scale, dh, dx_in)


def _swiglu_up(h, w_in, name, after=None):
    t = h.shape[0]
    tm = _pick(t, (1024, 512, 256))
    half = NDEV // 2
    extra = [] if after is None else [after]

    def body(h_ref, wg_ref, wu_ref, *rest):
        u_ref, a_ref = rest[-2:]
        hv = h_ref[...]
        gate = _dg(hv, wg_ref[...], NT)
        up = _dg(hv, wu_ref[...], NT)
        u_ref[0] = gate.astype(u_ref.dtype)
        u_ref[1] = up.astype(u_ref.dtype)
        a_ref[...] = (_silu(gate) * up).astype(a_ref.dtype)

    return pl.pallas_call(
        body, grid=(t // tm, half),
        in_specs=[pl.BlockSpec((tm, D), lambda i, j: (i, 0)),
                  pl.BlockSpec((FB, D), lambda i, j: (j, 0)),
                  pl.BlockSpec((FB, D), lambda i, j: (j + half, 0))]
        + [pl.BlockSpec(memory_space=pl.ANY)] * len(extra),
        out_specs=[pl.BlockSpec((2, None, tm, FB), lambda i, j: (0, j, i, 0)),
                   pl.BlockSpec((None, tm, FB), lambda i, j: (j, i, 0))],
        out_shape=[SDS((2, half, t, FB), BF16), SDS((half, t, FB), BF16)],
        compiler_params=_params(("parallel", "parallel")), name=name,
    )(h, w_in, w_in, *extra)


def _swiglu_down_bwd(dy, w_out, u, name, after=None):
    t = dy.shape[0]
    tm = _pick(t, (1024, 512, 256))
    half = NDEV // 2
    extra = [] if after is None else [after]
    pair = pl.BlockSpec((2, None, tm, FB), lambda i, j: (0, j, i, 0))

    def body(dy_ref, w_ref, u_ref, *rest):
        o_ref = rest[-1]
        da = _dg(dy_ref[...], w_ref[...], NT)
        gate = u_ref[0].astype(F32)
        o_ref[0] = (da * u_ref[1].astype(F32) * _dsilu(gate)).astype(o_ref.dtype)
        o_ref[1] = (da * _silu(gate)).astype(o_ref.dtype)

    return pl.pallas_call(
        body, grid=(t // tm, half),
        in_specs=[pl.BlockSpec((tm, D), lambda i, j: (i, 0)), pl.BlockSpec((FB, D), lambda i, j: (j, 0)), pair]
        + [pl.BlockSpec(memory_space=pl.ANY)] * len(extra),
        out_specs=pair, out_shape=SDS((2, half, t, FB), BF16),
        compiler_params=_params(("parallel", "parallel")), name=name,
    )(dy, w_out, u, *extra)


def _resid_fwd(x, y, gate, coef, name):
    t = x.shape[0]

    def body(x_ref, y_ref, g_ref, o_ref):
        o_ref[...] = x_ref[...] + (coef * g_ref[...]) * y_ref[...]

    return pl.pallas_call(
        body, grid=(t // TR,), in_specs=[_row(D), _row(D), _vec(D)], out_specs=_row(D),
        out_shape=SDS((t, D), F32), compiler_params=_params(("parallel",)), name=name,
    )(x, y, gate)


def _resid_bwd(dx, y, gate, coef, name):
    t = dx.shape[0]

    def body(dx_ref, y_ref, g_ref, dy_ref, dg_ref):
        @pl.when(pl.program_id(0) == 0)
        def _():
            dg_ref[...] = jnp.zeros_like(dg_ref)

        dxv = dx_ref[...]
        dy_ref[...] = ((coef * g_ref[...]) * dxv).astype(dy_ref.dtype)
        dg_ref[...] += jnp.sum((coef * dxv) * y_ref[...], axis=0, keepdims=True)

    return pl.pallas_call(
        body, grid=(t // TR,), in_specs=[_row(D), _row(D), _vec(D)], out_specs=[_row(D), _vec(D)],
        out_shape=[SDS((t, D), BF16), SDS((1, D), F32)],
        compiler_params=_params(("arbitrary",)), name=name,
    )(dx, y, gate)


def _final_loss(x, fg, target, name):
    t = x.shape[0]
    nt = t // TR

    def body(x_ref, g_ref, t_ref, loss_ref, dx_ref, dg_ref, acc_ref):
        i = pl.program_id(0)

        @pl.when(i == 0)
        def _():
            acc_ref[...] = jnp.zeros_like(acc_ref)
            dg_ref[...] = jnp.zeros_like(dg_ref)

        xv = x_ref[...]
        gv = g_ref[...]
        r = lax.rsqrt(jnp.mean(xv * xv, axis=-1, keepdims=True) + RMS_EPS)
        n = xv * r
        err = n * gv - t_ref[...]
        acc_ref[...] += jnp.sum(err * err, axis=0, keepdims=True)
        dy = err * (1.0 / D)
        dg_ref[...] += jnp.sum(dy * n, axis=0, keepdims=True)
        dn = dy * gv
        dx_ref[...] = r * (dn - n * jnp.mean(dn * n, axis=-1, keepdims=True))

        @pl.when(i == nt - 1)
        def _():
            tot = jnp.sum(acc_ref[...], axis=1, keepdims=True) * (0.5 / D)
            loss_ref[...] = jnp.broadcast_to(tot, loss_ref.shape)

    return pl.pallas_call(
        body, grid=(nt,), in_specs=[_row(D), _vec(D), _row(D)],
        out_specs=[_vec(128), _row(D), _vec(D)],
        out_shape=[SDS((1, 128), F32), SDS((t, D), F32), SDS((1, D), F32)],
        scratch_shapes=[pltpu.VMEM((1, D), F32)],
        compiler_params=_params(("arbitrary",)), name=name,
    )(x, fg, target)


def _halo_prev(width, col):
    per = TR // HALO
    return pl.BlockSpec((HALO, width), lambda i: (jnp.maximum(i * per - 1, 0), col))


def _halo_next(width, col, nt):
    per = TR // HALO
    return pl.BlockSpec((HALO, width), lambda i: (jnp.minimum((i + 1) * per, nt * per - 1), col))


def _pool_windows(ext, tile_index):
    rows = _iota((TR, PG), 0) + tile_index * TR + 1
    pooled, counts = [], []
    for gi in range(4):
        w = 2 << gi
        e = ext[:, gi * PG:(gi + 1) * PG]
        s = e
        step = 1
        while step < w:
            s = s + pltpu.roll(s, step, 0)
            step *= 2
        cnt = jnp.minimum(rows, w).astype(F32)
        pooled.append(s[HALO:] / cnt - e[HALO:])
        counts.append(cnt)
    return pooled, counts


def _pool_fwd(proj, pool_w, pool_scale, pool_proj, name):
    t = proj.shape[0]
    xcol = OFF_XP // PW

    def body(x_ref, h_ref, pw_ref, ps_ref, pp_ref, o_ref):
        i = pl.program_id(0)
        halo = jnp.where(i > 0, h_ref[...], 0.0)
        ext = jnp.concatenate([halo, x_ref[...]], axis=0)
        pooled, _ = _pool_windows(ext, i)
        mixed = [_dg(pooled[g].astype(BF16), pw_ref[g].astype(BF16), NN) for g in range(4)]
        ypre = jnp.concatenate(mixed, axis=1) * ps_ref[...]
        o_ref[...] = _dg(ypre.astype(BF16), pp_ref[...], NN)

    return pl.pallas_call(
        body, grid=(t // TR,),
        in_specs=[_row(PW, xcol), _halo_prev(PW, xcol),
                  pl.BlockSpec((4, PG, PG), lambda i: (0, 0, 0)), _vec(PW),
                  pl.BlockSpec((PW, D), lambda i: (0, 0))],
        out_specs=_row(D), out_shape=SDS((t, D), F32),
        compiler_params=_params(("parallel",)), name=name,
    )(proj, proj, pool_w, pool_scale, pool_proj)


def _pool_bwd_local(proj, pool_w, pool_scale, pool_proj, dya, name):
    t = proj.shape[0]
    xcol = OFF_XP // PW

    def body(x_ref, h_ref, pw_ref, ps_ref, pp_ref, dya_ref, dwin_ref, dpl_ref, dpw_ref, dps_ref, dpp_ref):
        i = pl.program_id(0)

        @pl.when(i == 0)
        def _():
            dpw_ref[...] = jnp.zeros_like(dpw_ref)
            dps_ref[...] = jnp.zeros_like(dps_ref)
            dpp_ref[...] = jnp.zeros_like(dpp_ref)

        halo = jnp.where(i > 0, h_ref[...], 0.0)
        ext = jnp.concatenate([halo, x_ref[...]], axis=0)
        pooled, counts = _pool_windows(ext, i)
        mixed = jnp.concatenate(
            [_dg(pooled[g].astype(BF16), pw_ref[g].astype(BF16), NN) for g in range(4)], axis=1)
        ps = ps_ref[...]
        ypre = mixed * ps
        dyab = dya_ref[...].astype(BF16)
        dypre = _dg(dyab, pp_ref[...], NT)
        dpp_ref[...] += _dg(ypre.astype(BF16), dyab, TN)
        dps_ref[...] += jnp.sum(dypre * mixed, axis=0, keepdims=True)
        dmixed = dypre * ps
        for g in range(4):
            dm = dmixed[:, g * PG:(g + 1) * PG].astype(BF16)
            dpw_ref[g] += _dg(pooled[g].astype(BF16), dm, TN)
            dpooled = _dg(dm, pw_ref[g].astype(BF16), NT)
            dwin_ref[:, g * PG:(g + 1) * PG] = dpooled / counts[g]
            dpl_ref[:, g * PG:(g + 1) * PG] = dpooled

    return pl.pallas_call(
        body, grid=(t // TR,),
        in_specs=[_row(PW, xcol), _halo_prev(PW, xcol),
                  pl.BlockSpec((4, PG, PG), lambda i: (0, 0, 0)), _vec(PW),
                  pl.BlockSpec((PW, D), lambda i: (0, 0)), _row(D)],
        out_specs=[_row(PW), _row(PW), pl.BlockSpec((4, PG, PG), lambda i: (0, 0, 0)), _vec(PW),
                   pl.BlockSpec((PW, D), lambda i: (0, 0))],
        out_shape=[SDS((t, PW), F32), SDS((t, PW), F32), SDS((4, PG, PG), F32), SDS((1, PW), F32),
                   SDS((PW, D), F32)],
        compiler_params=_params(("arbitrary",)), name=name,
    )(proj, proj, pool_w, pool_scale, pool_proj, dya)


def _pool_bwd_window(dwin, dpl, name):
    t = dwin.shape[0]
    nt = t // TR
    ext_rows = TR + HALO

    def body(dw_ref, h_ref, dp_ref, o_ref):
        i = pl.program_id(0)
        halo = jnp.where(i < nt - 1, h_ref[...], 0.0)
        ext = jnp.concatenate([dw_ref[...], halo], axis=0)
        for gi in range(4):
            w = 2 << gi
            s = ext[:, gi * PG:(gi + 1) * PG]
            step = 1
            while step < w:
                s = s + pltpu.roll(s, ext_rows - step, 0)
                step *= 2
            o_ref[:, gi * PG:(gi + 1) * PG] = (s[:TR] - dp_ref[:, gi * PG:(gi + 1) * PG]).astype(o_ref.dtype)

    return pl.pallas_call(
        body, grid=(nt,), in_specs=[_row(PW), _halo_next(PW, 0, nt), _row(PW)], out_specs=_row(PW),
        out_shape=SDS((t, PW), BF16), compiler_params=_params(("parallel",)), name=name,
    )(dwin, dwin, dpl)


def _conv_group(ext, cw_ref, cols):
    acc = cw_ref[3:4, cols] * ext
    for j in range(3):
        acc = acc + cw_ref[j:j + 1, cols] * pltpu.roll(ext, 3 - j, 0)
    return acc[HALO:]


def _gate_terms(raw, al, dt):
    beta = _sigmoid(raw)
    xg = raw + dt
    sp = jnp.maximum(xg, 0.0) + jnp.log(1.0 + jnp.exp(-jnp.abs(xg)))
    g = -jnp.exp(al) * sp
    return beta, g, _sigmoid(xg)


def _dn_pre_fwd(proj, conv_w, al_row, dt_row, name):
    t = proj.shape[0]

    def body(x_ref, h_ref, cw_ref, ba_ref, al_ref, dt_ref, q_ref, k_ref, v_ref, bg_ref):
        i = pl.program_id(0)
        keep = i > 0
        for grp in range(24):
            cols = slice(grp * HD, (grp + 1) * HD)
            ext = jnp.concatenate([jnp.where(keep, h_ref[:, cols], 0.0), x_ref[:, cols]], axis=0)
            s = _silu(_conv_group(ext, cw_ref, cols))
            seg, head = divmod(grp, NH)
            hc = slice(head * HD, (head + 1) * HD)
            if seg == 0:
                q_ref[:, hc] = s * lax.rsqrt(jnp.sum(s * s, axis=-1, keepdims=True) + L2_EPS) * (HD ** -0.5)
            elif seg == 1:
                k_ref[:, hc] = s * lax.rsqrt(jnp.sum(s * s, axis=-1, keepdims=True) + L2_EPS)
            else:
                v_ref[:, hc] = s
        lane = _iota((TR, 128), 1)
        rowc = _iota((TR, 128), 0) % CH
        beta, g, _ = _gate_terms(ba_ref[...], al_ref[...], dt_ref[...])
        step = 1
        while step < CH:
            g = g + jnp.where(rowc >= step, pltpu.roll(g, step, 0), 0.0)
            step *= 2
        bg_ref[...] = jnp.where(lane < NH, beta, jnp.where(lane < 2 * NH, g, 0.0))

    return pl.pallas_call(
        body, grid=(t // TR,),
        in_specs=[_row(3 * D, 0), _halo_prev(3 * D, 0), pl.BlockSpec((4, 3 * D), lambda i: (0, 0)),
                  _row(128, OFF_BA // 128), _vec(128), _vec(128)],
        out_specs=[_row(D), _row(D), _row(D), _row(128)],
        out_shape=[SDS((t, D), F32), SDS((t, D), F32), SDS((t, D), F32), SDS((t, 128), F32)],
        compiler_params=_params(("parallel",)), name=name,
    )(proj, proj, conv_w, proj, al_row, dt_row)


def _dn_pre_bwd_act(proj, conv_w, al_row, dt_row, dq, dk, dv, dbg, name):
    t = proj.shape[0]

    def body(x_ref, h_ref, cw_ref, ba_ref, al_ref, dt_ref, dq_ref, dk_ref, dv_ref, dbg_ref,
             dc_ref, draw_ref, dal_ref, ddt_ref):
        i = pl.program_id(0)

        @pl.when(i == 0)
        def _():
            dal_ref[...] = jnp.zeros_like(dal_ref)
            ddt_ref[...] = jnp.zeros_like(ddt_ref)

        keep = i > 0
        for grp in range(24):
            cols = slice(grp * HD, (grp + 1) * HD)
            ext = jnp.concatenate([jnp.where(keep, h_ref[:, cols], 0.0), x_ref[:, cols]], axis=0)
            cv = _conv_group(ext, cw_ref, cols)
            seg, head = divmod(grp, NH)
            hc = slice(head * HD, (head + 1) * HD)
            if seg == 2:
                ds = dv_ref[:, hc]
            else:
                s = _silu(cv)
                r = lax.rsqrt(jnp.sum(s * s, axis=-1, keepdims=True) + L2_EPS)
                dy = dq_ref[:, hc] if seg == 0 else dk_ref[:, hc]
                c = (HD ** -0.5) if seg == 0 else 1.0
                ds = (c * r) * (dy - s * ((r * r) * jnp.sum(dy * s, axis=-1, keepdims=True)))
            dc_ref[:, cols] = ds * _dsilu(cv)
        lane = _iota((TR, 128), 1)
        rowc = _iota((TR, 128), 0) % CH
        isb = lane < NH
        isg = jnp.logical_and(lane >= NH, lane < 2 * NH)
        beta, g, sg = _gate_terms(ba_ref[...], al_ref[...], dt_ref[...])
        dbgv = dbg_ref[...]
        dg = dbgv
        step = 1
        while step < CH:
            dg = dg + jnp.where(rowc < CH - step, pltpu.roll(dg, TR - step, 0), 0.0)
            step *= 2
        da_raw = dg * (-jnp.exp(al_ref[...])) * sg
        draw_ref[...] = jnp.where(isb, dbgv * beta * (1.0 - beta), jnp.where(isg, da_raw, 0.0)).astype(draw_ref.dtype)
        dal_ref[...] += jnp.sum(jnp.where(isg, dg * g, 0.0), axis=0, keepdims=True)
        ddt_ref[...] += jnp.sum(jnp.where(isg, da_raw, 0.0), axis=0, keepdims=True)

    return pl.pallas_call(
        body, grid=(t // TR,),
        in_specs=[_row(3 * D, 0), _halo_prev(3 * D, 0), pl.BlockSpec((4, 3 * D), lambda i: (0, 0)),
                  _row(128, OFF_BA // 128), _vec(128), _vec(128), _row(D), _row(D), _row(D), _row(128)],
        out_specs=[_row(3 * D), _row(128), _vec(128), _vec(128)],
        out_shape=[SDS((t, 3 * D), F32), SDS((t, 128), BF16), SDS((1, 128), F32), SDS((1, 128), F32)],
        compiler_params=_params(("arbitrary",)), name=name,
    )(proj, proj, conv_w, proj, al_row, dt_row, dq, dk, dv, dbg)


def _dn_pre_bwd_conv(proj, conv_w, dconv, name):
    t = proj.shape[0]
    nt = t // TR
    ext_rows = TR + HALO

    def body(x_ref, h_ref, cw_ref, dc_ref, dn_ref, dx_ref, dcw_ref):
        i = pl.program_id(0)

        @pl.when(i == 0)
        def _():
            dcw_ref[...] = jnp.zeros_like(dcw_ref)

        keep_prev = i > 0
        keep_next = i < nt - 1
        for grp in range(24):
            cols = slice(grp * HD, (grp + 1) * HD)
            dct = dc_ref[:, cols]
            dext = jnp.concatenate([dct, jnp.where(keep_next, dn_ref[:, cols], 0.0)], axis=0)
            acc = cw_ref[3:4, cols] * dext
            for j in range(3):
                acc = acc + cw_ref[j:j + 1, cols] * pltpu.roll(dext, ext_rows - (3 - j), 0)
            dx_ref[:, cols] = acc[:TR].astype(dx_ref.dtype)
            xext = jnp.concatenate([jnp.where(keep_prev, h_ref[:, cols], 0.0), x_ref[:, cols]], axis=0)
            for j in range(4):
                xs = xext if j == 3 else pltpu.roll(xext, 3 - j, 0)
                dcw_ref[j:j + 1, cols] += jnp.sum(xs[HALO:] * dct, axis=0, keepdims=True)

    return pl.pallas_call(
        body, grid=(nt,),
        in_specs=[_row(3 * D, 0), _halo_prev(3 * D, 0), pl.BlockSpec((4, 3 * D), lambda i: (0, 0)),
                  _row(3 * D), _halo_next(3 * D, 0, nt)],
        out_specs=[_row(3 * D), pl.BlockSpec((4, 3 * D), lambda i: (0, 0))],
        out_shape=[SDS((t, 3 * D), BF16), SDS((4, 3 * D), F32)],
        compiler_params=_params(("arbitrary",)), name=name,
    )(proj, proj, conv_w, dconv, dconv)


def _dn_post_fwd(o, proj, gn, name):
    t = o.shape[0]

    def body(o_ref, z_ref, g_ref, out_ref):
        gv = g_ref[...]
        for h in range(NH):
            hc = slice(h * HD, (h + 1) * HD)
            ov = o_ref[:, hc]
            r = lax.rsqrt(jnp.mean(ov * ov, axis=-1, keepdims=True) + RMS_EPS)
            out_ref[:, hc] = (((ov * r) * gv) * _silu(z_ref[:, hc])).astype(out_ref.dtype)

    return pl.pallas_call(
        body, grid=(t // TR,), in_specs=[_row(D), _row(D, OFF_Z // D), _vec(HD)], out_specs=_row(D),
        out_shape=SDS((t, D), BF16), compiler_params=_params(("parallel",)), name=name,
    )(o, proj, gn)


def _dn_post_bwd(o, proj, gn, dob, name):
    t = o.shape[0]

    def body(o_ref, z_ref, g_ref, d_ref, do_ref, dz_ref, dg_ref):
        @pl.when(pl.program_id(0) == 0)
        def _():
            dg_ref[...] = jnp.zeros_like(dg_ref)

        gv = g_ref[...]
        acc = jnp.zeros((1, HD), F32)
        for h in range(NH):
            hc = slice(h * HD, (h + 1) * HD)
            ov = o_ref[:, hc]
            zv = z_ref[:, hc]
            dv = d_ref[:, hc]
            r = lax.rsqrt(jnp.mean(ov * ov, axis=-1, keepdims=True) + RMS_EPS)
            n = ov * r
            dz_ref[:, hc] = (dv * (n * gv) * _dsilu(zv)).astype(dz_ref.dtype)
            dng = dv * _silu(zv)
            acc = acc + jnp.sum(dng * n, axis=0, keepdims=True)
            dn = dng * gv
            do_ref[:, hc] = r * (dn - n * jnp.mean(dn * n, axis=-1, keepdims=True))
        dg_ref[...] += acc

    return pl.pallas_call(
        body, grid=(t // TR,), in_specs=[_row(D), _row(D, OFF_Z // D), _vec(HD), _row(D)],
        out_specs=[_row(D), _row(D), _vec(HD)],
        out_shape=[SDS((t, D), F32), SDS((t, D), BF16), SDS((1, HD), F32)],
        compiler_params=_params(("arbitrary",)), name=name,
    )(o, proj, gn, dob)


def _merge_fwd(ya, yb, proj, name):
    t = ya.shape[0]

    def body(a_ref, b_ref, gp_ref, gd_ref, o_ref):
        o_ref[...] = (_sigmoid(gp_ref[...]) * a_ref[...] + _sigmoid(gd_ref[...]) * b_ref[...]).astype(o_ref.dtype)

    return pl.pallas_call(
        body, grid=(t // TR,), in_specs=[_row(D), _row(D), _row(D, OFF_GP // D), _row(D, OFF_GD // D)],
        out_specs=_row(D), out_shape=SDS((t, D), BF16),
        compiler_params=_params(("parallel",)), name=name,
    )(ya, yb, proj, proj)


def _merge_bwd(dm, ya, yb, proj, name):
    t = ya.shape[0]

    def body(d_ref, a_ref, b_ref, gp_ref, gd_ref, da_ref, db_ref, dgp_ref, dgd_ref):
        dv = d_ref[...]
        sp = _sigmoid(gp_ref[...])
        sd = _sigmoid(gd_ref[...])
        da_ref[...] = dv * sp
        db_ref[...] = (dv * sd).astype(db_ref.dtype)
        dgp_ref[...] = (dv * a_ref[...] * sp * (1.0 - sp)).astype(dgp_ref.dtype)
        dgd_ref[...] = (dv * b_ref[...] * sd * (1.0 - sd)).astype(dgd_ref.dtype)

    return pl.pallas_call(
        body, grid=(t // TR,),
        in_specs=[_row(D), _row(D), _row(D), _row(D, OFF_GP // D), _row(D, OFF_GD // D)],
        out_specs=[_row(D)] * 4,
        out_shape=[SDS((t, D), F32), SDS((t, D), BF16), SDS((t, D), BF16), SDS((t, D), BF16)],
        compiler_params=_params(("parallel",)), name=name,
    )(dm, ya, yb, proj, proj)


def _split2(x):
    hi = x.astype(BF16)
    return hi, (x - hi.astype(F32)).astype(BF16)


def _dot3(a, b, dims):
    ah, al = _split2(a)
    bh, bl = _split2(b)
    return _dg(ah, bh, dims) + (_dg(ah, bl, dims) + _dg(al, bh, dims))


def _neumann_inverses(mats):
    ri = _iota((CH, CH), 0)
    ci = _iota((CH, CH), 1)
    eye = jnp.where(ri == ci, 1.0, 0.0).astype(F32)
    xs = [-a for a in mats]
    ps = [eye + x for x in xs]
    for _ in range(5):
        xs = [_dot3(x, x, NN) for x in xs]
        ps = [p + _dot3(p, x, NN) for p, x in zip(ps, xs)]
    return ps


def _solve_with(inv):
    @jax.custom_vjp
    def solve(a, rhs):
        return _dot3(inv, rhs, NN)

    def fwd(a, rhs):
        sol = _dot3(inv, rhs, NN)
        return sol, sol

    def bwd(sol, d):
        drhs = _dot3(inv, d, TN)
        return -_dot3(drhs, sol, NT), drhs

    solve.defvjp(fwd, bwd)
    return solve


@jax.custom_vjp
def _rows_to_lanes(g64):
    ri = _iota((CH, CH), 0)
    ci = _iota((CH, CH), 1)
    diag = jnp.where(ri == ci, g64, 0.0)
    ones = jnp.ones((CH, CH), BF16)
    hi = diag.astype(BF16)
    rem = diag - hi.astype(F32)
    mid = rem.astype(BF16)
    lo = (rem - mid.astype(F32)).astype(BF16)
    return _dg(ones, hi, NN) + (_dg(ones, mid, NN) + _dg(ones, lo, NN))


def _rows_to_lanes_bwd(_, d):
    ri = _iota((CH, CH), 0)
    ci = _iota((CH, CH), 1)
    return (jnp.where(ri == ci, jnp.broadcast_to(jnp.sum(d, axis=0, keepdims=True), (CH, CH)), 0.0),)


_rows_to_lanes.defvjp(lambda g64: (_rows_to_lanes(g64), None), _rows_to_lanes_bwd)


def _chunk_local(solve_all, q, k, v, g128, g64, gl128, b128, b64):
    ri = _iota((CH, CH), 0)
    ci = _iota((CH, CH), 1)
    causal = ri >= ci
    strict = ri > ci
    gj = [_rows_to_lanes(g) for g in g64]
    decay = [jnp.where(causal, jnp.exp(jnp.where(causal, g - t, 0.0)), 0.0) for g, t in zip(g64, gj)]
    kk = [_nt(x, x) for x in k]
    a = [jnp.where(strict, b * m * dc, 0.0) for b, m, dc in zip(b64, kk, decay)]
    eg = [jnp.exp(g) for g in g128]
    rhs = [jnp.concatenate([b * x, (b * e) * y], axis=1) for b, x, e, y in zip(b128, v, eg, k)]
    sol = solve_all(a, rhs)
    qk = [jnp.where(causal, _nt(x, y) * dc, 0.0) for x, y, dc in zip(q, k, decay)]
    return ([s[:, :HD] for s in sol], [s[:, HD:] for s in sol], qk, [x * e for x, e in zip(q, eg)],
            [x * jnp.exp(gl - g) for x, gl, g in zip(k, gl128, g128)], [jnp.exp(gl) for gl in gl128])


def _all_head_gates(bgv):
    return tuple(list(z) for z in zip(*[_head_gates(bgv, h) for h in range(NH)]))


def _head_gates(bgv, h):
    lane = _iota((CH, 128), 1)
    row = _iota((CH, 128), 0)
    bcol = jnp.sum(jnp.where(lane == h, bgv, 0.0), axis=1, keepdims=True)
    gcol = jnp.sum(jnp.where(lane == NH + h, bgv, 0.0), axis=1, keepdims=True)
    g128 = jnp.broadcast_to(gcol, (CH, 128))
    gl128 = jnp.broadcast_to(jnp.sum(jnp.where(row == CH - 1, g128, 0.0), axis=0, keepdims=True), (CH, 128))
    return (g128, jnp.broadcast_to(gcol, (CH, CH)), gl128,
            jnp.broadcast_to(bcol, (CH, 128)), jnp.broadcast_to(bcol, (CH, CH)))


def _chunk_specs():
    row = pl.BlockSpec((CH, D), lambda i: (i, 0))
    small = pl.BlockSpec((CH, 128), lambda i: (i, 0))
    qk = pl.BlockSpec((NH, CH, CH), lambda i: (i, 0, 0))
    eg = pl.BlockSpec((1, NH, 128), lambda i: (i, 0, 0))
    return row, small, qk, eg


def _dn_local_fwd(q, k, v, bg, name):
    t = q.shape[0]
    n = t // CH

    def body(q_ref, k_ref, v_ref, bg_ref, u_ref, w_ref, qk_ref, qd_ref, kd_ref, eg_ref, inv_ref):
        cols = [slice(h * HD, (h + 1) * HD) for h in range(NH)]

        def solve_all(mats, rhs):
            invs = _neumann_inverses(mats)
            for h in range(NH):
                inv_ref[h] = invs[h]
            return [_dot3(m, r, NN) for m, r in zip(invs, rhs)]

        u, w, qk, qd, kd, egl = _chunk_local(
            solve_all, [q_ref[:, c] for c in cols], [k_ref[:, c] for c in cols], [v_ref[:, c] for c in cols],
            *_all_head_gates(bg_ref[...]))
        for h, hc in enumerate(cols):
            u_ref[:, hc] = u[h]
            w_ref[:, hc] = w[h].astype(w_ref.dtype)
            qd_ref[:, hc] = qd[h].astype(qd_ref.dtype)
            kd_ref[:, hc] = kd[h].astype(kd_ref.dtype)
            qk_ref[h] = qk[h].astype(qk_ref.dtype)
            eg_ref[0, h:h + 1, :] = egl[h][0:1, :]

    row, small, qkb, egb = _chunk_specs()
    return pl.pallas_call(
        body, grid=(n,), in_specs=[row, row, row, small], out_specs=[row, row, qkb, row, row, egb, qkb],
        out_shape=[SDS((t, D), F32), SDS((t, D), BF16), SDS((n * NH, CH, CH), BF16), SDS((t, D), BF16),
                   SDS((t, D), BF16), SDS((n, NH, 128), F32), SDS((n * NH, CH, CH), F32)],
        compiler_params=_params(("parallel",)), name=name,
    )(q, k, v, bg)


def _dn_local_bwd(q, k, v, bg, inv, du, dw, dqk, dqd, dkd, deg, name):
    t = q.shape[0]
    n = t // CH

    def body(q_ref, k_ref, v_ref, bg_ref, inv_ref, du_ref, dw_ref, dqk_ref, dqd_ref, dkd_ref, deg_ref,
             dq_ref, dk_ref, dv_ref, dbg_ref):
        bgv = bg_ref[...]
        lane = _iota((CH, 128), 1)
        row = _iota((CH, 128), 0)
        first = jnp.where(row == 0, 1.0, 0.0)
        acc = jnp.zeros((CH, 128), F32)
        cols = [slice(h * HD, (h + 1) * HD) for h in range(NH)]
        solves = [_solve_with(inv_ref[h]) for h in range(NH)]

        def solve_all(mats, rhs):
            return [f(m, r) for f, m, r in zip(solves, mats, rhs)]

        _, vjp = jax.vjp(functools.partial(_chunk_local, solve_all),
                         [q_ref[:, c] for c in cols], [k_ref[:, c] for c in cols], [v_ref[:, c] for c in cols],
                         *_all_head_gates(bgv))
        cts = ([du_ref[:, c] for c in cols], [dw_ref[:, c] for c in cols], [dqk_ref[h] for h in range(NH)],
               [dqd_ref[:, c] for c in cols], [dkd_ref[:, c] for c in cols],
               [jnp.broadcast_to(deg_ref[0, h:h + 1, :], (CH, 128)) * first for h in range(NH)])
        dq, dk, dv, dg128, dg64, dgl, db128, db64 = vjp(cts)
        for h, hc in enumerate(cols):
            dq_ref[:, hc] = dq[h]
            dk_ref[:, hc] = dk[h]
            dv_ref[:, hc] = dv[h]
            dg = jnp.sum(dg128[h], axis=1, keepdims=True) + jnp.sum(dg64[h], axis=1, keepdims=True)
            tot = jnp.sum(jnp.sum(dgl[h], axis=0, keepdims=True), axis=1, keepdims=True)
            dg = dg + jnp.where(row[:, 0:1] == CH - 1, tot, 0.0)
            db = jnp.sum(db128[h], axis=1, keepdims=True) + jnp.sum(db64[h], axis=1, keepdims=True)
            acc = acc + jnp.where(lane == h, db, 0.0) + jnp.where(lane == NH + h, dg, 0.0)
        dbg_ref[...] = acc

    row, small, qkb, egb = _chunk_specs()
    return pl.pallas_call(
        body, grid=(n,), in_specs=[row, row, row, small, qkb, row, row, qkb, row, row, egb],
        out_specs=[row, row, row, small],
        out_shape=[SDS((t, D), F32)] * 3 + [SDS((t, 128), F32)],
        compiler_params=_params(("parallel",)), name=name,
    )(q, k, v, bg, inv, du, dw, dqk, dqd, dkd, deg)


def _state_step(s, u, w, qk, qd, kd, egl):
    ws = [_nn(a, b) for a, b in zip(w, s)]
    v_new = [a - b for a, b in zip(u, ws)]
    qs = [_nn(a, b) for a, b in zip(qd, s)]
    intra = [_nn(a, b) for a, b in zip(qk, v_new)]
    upd = [_tn(a, b) for a, b in zip(kd, v_new)]
    return [a * e + b for a, e, b in zip(s, egl, upd)], [a + b for a, b in zip(qs, intra)]


def _dn_scan_fwd(u, w, qk, qd, kd, eg, name):
    t = u.shape[0]
    n = t // CH

    def body(u_ref, w_ref, qk_ref, qd_ref, kd_ref, eg_ref, o_ref, save_ref, s_ref):
        @pl.when(pl.program_id(0) == 0)
        def _():
            s_ref[...] = jnp.zeros_like(s_ref)

        cols = [slice(h * HD, (h + 1) * HD) for h in range(NH)]
        s = [s_ref[h] for h in range(NH)]
        for h in range(NH):
            save_ref[0, h] = s[h]
        s_new, o = _state_step(
            s, [u_ref[:, c] for c in cols], [w_ref[:, c].astype(F32) for c in cols],
            [qk_ref[h].astype(F32) for h in range(NH)], [qd_ref[:, c].astype(F32) for c in cols],
            [kd_ref[:, c].astype(F32) for c in cols], [eg_ref[0, h:h + 1, :] for h in range(NH)])
        for h, hc in enumerate(cols):
            o_ref[:, hc] = o[h]
            s_ref[h] = s_new[h]

    row, _, qkb, egb = _chunk_specs()
    return pl.pallas_call(
        body, grid=(n,), in_specs=[row, row, qkb, row, row, egb],
        out_specs=[row, pl.BlockSpec((1, NH, HD, HD), lambda i: (i, 0, 0, 0))],
        out_shape=[SDS((t, D), F32), SDS((n, NH, HD, HD), F32)],
        scratch_shapes=[pltpu.VMEM((NH, HD, HD), F32)],
        compiler_params=_params(("arbitrary",)), name=name,
    )(u, w, qk, qd, kd, eg)


def _dn_scan_bwd(u, w, qk, qd, kd, eg, saved, do, name):
    t = u.shape[0]
    n = t // CH

    def body(u_ref, w_ref, qk_ref, qd_ref, kd_ref, eg_ref, sv_ref, do_ref,
             du_ref, dw_ref, dqk_ref, dqd_ref, dkd_ref, deg_ref, ds_ref):
        @pl.when(pl.program_id(0) == 0)
        def _():
            ds_ref[...] = jnp.zeros_like(ds_ref)

        cols = [slice(h * HD, (h + 1) * HD) for h in range(NH)]
        _, vjp = jax.vjp(
            _state_step, [sv_ref[0, h] for h in range(NH)], [u_ref[:, c] for c in cols],
            [w_ref[:, c].astype(F32) for c in cols], [qk_ref[h].astype(F32) for h in range(NH)],
            [qd_ref[:, c].astype(F32) for c in cols], [kd_ref[:, c].astype(F32) for c in cols],
            [eg_ref[0, h:h + 1, :] for h in range(NH)])
        ds, du, dw, dqk, dqd, dkd, deg = vjp(([ds_ref[h] for h in range(NH)], [do_ref[:, c] for c in cols]))
        for h, hc in enumerate(cols):
            ds_ref[h] = ds[h]
            du_ref[:, hc] = du[h]
            dw_ref[:, hc] = dw[h]
            dqk_ref[h] = dqk[h]
            dqd_ref[:, hc] = dqd[h]
            dkd_ref[:, hc] = dkd[h]
            deg_ref[0, h:h + 1, :] = deg[h]

    rev = lambda i: (n - 1 - i, 0)
    rev3 = lambda i: (n - 1 - i, 0, 0)
    row = pl.BlockSpec((CH, D), rev)
    qkb = pl.BlockSpec((NH, CH, CH), rev3)
    egb = pl.BlockSpec((1, NH, 128), rev3)
    return pl.pallas_call(
        body, grid=(n,),
        in_specs=[row, row, qkb, row, row, egb,
                  pl.BlockSpec((1, NH, HD, HD), lambda i: (n - 1 - i, 0, 0, 0)), row],
        out_specs=[row, row, qkb, row, row, egb],
        out_shape=[SDS((t, D), F32), SDS((t, D), F32), SDS((n * NH, CH, CH), F32), SDS((t, D), F32),
                   SDS((t, D), F32), SDS((n, NH, 128), F32)],
        scratch_shapes=[pltpu.VMEM((NH, HD, HD), F32)],
        compiler_params=_params(("arbitrary",)), name=name,
    )(u, w, qk, qd, kd, eg, saved, do)


def _ada_fwd(c_all, ada_w, ada_b, name):
    ncol = ada_w.shape[1]

    def body(c_ref, w_ref, b_ref, o_ref):
        o_ref[...] = _dg(_silu(c_ref[...]), w_ref[...], NN, HI) + b_ref[...]

    return pl.pallas_call(body, out_shape=SDS((NDEV, ncol), F32),
                          compiler_params=pltpu.CompilerParams(vmem_limit_bytes=VMEM_LIMIT), name=name,
                          )(c_all, ada_w, ada_b)


def _ada_bwd(c_all_t, dmod, name):
    ncol = dmod.shape[1]

    def body(c_ref, d_ref, o_ref):
        sc = _silu(c_ref[...])
        acc = sc[:, 0:1] * d_ref[0:1, :]
        for b in range(1, NDEV):
            acc = acc + sc[:, b:b + 1] * d_ref[b:b + 1, :]
        o_ref[...] = acc

    return pl.pallas_call(body, out_shape=SDS((D, ncol), F32),
                          compiler_params=pltpu.CompilerParams(vmem_limit_bytes=VMEM_LIMIT), name=name,
                          )(c_all_t, dmod)


def _sum_devices(parts, out_dtype, name):
    _, r, c = parts.shape
    tr = TR if r % TR == 0 else r

    def body(p_ref, o_ref):
        acc = p_ref[0].astype(F32)
        for i in range(1, NDEV):
            acc = acc + p_ref[i].astype(F32)
        o_ref[...] = acc.astype(o_ref.dtype)

    return pl.pallas_call(
        body, grid=(r // tr,), in_specs=[pl.BlockSpec((NDEV, tr, c), lambda i: (0, i, 0))],
        out_specs=pl.BlockSpec((tr, c), lambda i: (i, 0)), out_shape=SDS((r, c), out_dtype),
        compiler_params=_params(("parallel",)), name=name,
    )(parts)


def _adamw(w, g, m, v, name):
    r, c = w.shape
    if r % 8 == 0:
        tr, tc = _pick(r, (256, 352, 128, 8)), c
    else:
        tr, tc = r, (256 if c % 256 == 0 else c)
    bc1 = 1.0 - ADAM_B1 ** ADAM_STEP
    bc2 = 1.0 - ADAM_B2 ** ADAM_STEP

    def body(w_ref, g_ref, m_ref, v_ref, d_ref, nm_ref, nv_ref):
        gv = g_ref[...]
        m_new = ADAM_B1 * m_ref[...] + (1.0 - ADAM_B1) * gv
        v_new = ADAM_B2 * v_ref[...] + (1.0 - ADAM_B2) * (gv * gv)
        nm_ref[...] = m_new
        nv_ref[...] = v_new
        d_ref[...] = -ADAM_LR * ((m_new / bc1) / (jnp.sqrt(v_new / bc2) + ADAM_EPS) + ADAM_WD * w_ref[...])

    spec = pl.BlockSpec((tr, tc), lambda i, j: (i, j))
    return pl.pallas_call(
        body, grid=(r // tr, c // tc), in_specs=[spec] * 4, out_specs=[spec] * 3,
        out_shape=[SDS((r, c), F32)] * 3, compiler_params=_params(("parallel", "parallel")), name=name,
    )(w, g, m, v)


ANY = pl.BlockSpec(memory_space=pl.ANY)
MESH = pl.DeviceIdType.MESH


def _all_gather(xs, name, after=None):
    n = len(xs)
    extra = [] if after is None else [after]

    def body(*refs):
        x_refs, out_refs = refs[:n], refs[n + len(extra):2 * n + len(extra)]
        send_sems, recv_sems, local_sems = refs[-3:]
        mx, my, mc = lax.axis_index("x"), lax.axis_index("y"), lax.axis_index("c")
        me, sibling = (mx, my, mc), (mx, my, 1 - mc)
        chips = [(1 - mx, my), (mx, 1 - my), (1 - mx, 1 - my)]

        def rows(a, px, py, pc):
            return out_refs[a].at[4 * px + 2 * py + pc]

        def copy(a, k, block, to, src=None):
            return pltpu.make_async_remote_copy(
                src_ref=rows(a, *block) if src is None else src, dst_ref=rows(a, *block),
                send_sem=send_sems.at[a, k], recv_sem=recv_sems.at[a, k], device_id=to, device_id_type=MESH)

        mine = [pltpu.make_async_copy(x_refs[a], rows(a, *me), local_sems.at[a]) for a in range(n)]
        for cp in mine:
            cp.start()
        first = []
        for a in range(n):
            first.append(copy(a, 0, me, sibling, src=x_refs[a]))
            first += [copy(a, 1 + j, me, (*chip, mc), src=x_refs[a]) for j, chip in enumerate(chips)]
        for cp in first:
            cp.start()
        passed = []
        for a in range(n):
            for j, chip in enumerate(chips):
                copy(a, 1 + j, (*chip, mc), me).wait_recv()
                passed.append(copy(a, 4 + j, (*chip, mc), sibling))
                passed[-1].start()
        for a in range(n):
            copy(a, 0, sibling, me).wait_recv()
            for j, chip in enumerate(chips):
                copy(a, 4 + j, (*chip, 1 - mc), me).wait_recv()
        for cp in first + passed:
            cp.wait_send()
        for cp in mine:
            cp.wait()

    return pl.pallas_call(
        body, out_shape=[SDS((NDEV,) + x.shape, x.dtype) for x in xs], in_specs=[ANY] * (n + len(extra)),
        out_specs=[ANY] * n,
        scratch_shapes=[pltpu.SemaphoreType.DMA((n, 7)), pltpu.SemaphoreType.DMA((n, 7)),
                        pltpu.SemaphoreType.DMA((n,))],
        name=name,
    )(*xs, *extra)


HBM = pl.BlockSpec(memory_space=pltpu.HBM)
SEM = pl.BlockSpec(memory_space=pltpu.SEMAPHORE)
EFFECT = pltpu.SideEffectType.DATAFLOW_SIDE_EFFECTING


def _peers():
    mx, my, mc = lax.axis_index("x"), lax.axis_index("y"), lax.axis_index("c")
    out = []
    for k in range(1, NDEV):
        out.append((1 - mx if k & 4 else mx, 1 - my if k & 2 else my, 1 - mc if k & 1 else mc))
    return 4 * mx + 2 * my + mc, out


def _push_start(srcs, sliced, name, after=None):
    n = len(srcs)
    extra = [] if after is None else [after]
    me_idx = 4 * lax.axis_index("x") + 2 * lax.axis_index("y") + lax.axis_index("c")
    lands = []
    for s in srcs:
        blk = lax.dynamic_index_in_dim(s, me_idx, 0, keepdims=True) if sliced else s[None]
        shape = s.shape if sliced else (NDEV,) + s.shape
        lands.append(lax.dynamic_update_slice(lax.empty(shape, s.dtype), blk, (me_idx,) + (0,) * (len(shape) - 1)))

    def body(*refs):
        src_refs, land_refs = refs[:n], refs[n:2 * n]
        outs = refs[2 * n + len(extra):]
        send_sems, recv_sems = outs[:n], outs[n:2 * n]
        token = refs[-1]
        me, peers = _peers()
        for a in range(n):
            for k, (px, py, pc) in enumerate(peers):
                src = src_refs[a].at[4 * px + 2 * py + pc] if sliced else src_refs[a]
                pltpu.make_async_remote_copy(
                    src_ref=src, dst_ref=land_refs[a].at[me], send_sem=send_sems[a].at[k],
                    recv_sem=recv_sems[a].at[k], device_id=(px, py, pc), device_id_type=MESH).start()
        token[...] = jnp.zeros_like(token)

    outs = pl.pallas_call(
        body, name=name,
        out_shape=([pltpu.SemaphoreType.DMA((NDEV - 1,))] * (2 * n)
                   + [pltpu.HBM(s.shape, s.dtype) for s in srcs] + [pltpu.HBM(l.shape, l.dtype) for l in lands]
                   + [SDS((8, 128), F32)]),
        in_specs=[HBM] * (2 * n) + [pl.BlockSpec(memory_space=pl.ANY)] * len(extra),
        out_specs=[SEM] * (2 * n) + [HBM] * (2 * n) + [pl.BlockSpec(memory_space=pltpu.VMEM)],
        input_output_aliases={i: 2 * n + i for i in range(2 * n)},
        compiler_params=pltpu.CompilerParams(has_side_effects=EFFECT),
    )(*[pltpu.with_memory_space_constraint(s, pltpu.HBM) for s in srcs],
      *[pltpu.with_memory_space_constraint(l, pltpu.HBM) for l in lands], *extra)
    sends, recvs = outs[:n], outs[n:2 * n]
    src_thru, land_thru = outs[2 * n:3 * n], outs[3 * n:4 * n]
    return [(sends[a], recvs[a], src_thru[a], land_thru[a]) for a in range(n)], outs[-1]


def _push_wait(started, sliced, after, name):
    n = len(started)

    def body(*refs):
        src_refs, land_refs = refs[:n], refs[n:2 * n]
        send_sems, recv_sems = refs[2 * n:3 * n], refs[3 * n:4 * n]
        me, peers = _peers()
        for a in range(n):
            for k, (px, py, pc) in enumerate(peers):
                src = src_refs[a].at[4 * px + 2 * py + pc] if sliced else src_refs[a]
                cp = pltpu.make_async_remote_copy(
                    src_ref=src, dst_ref=land_refs[a].at[me], send_sem=send_sems[a].at[k],
                    recv_sem=recv_sems[a].at[k], device_id=(px, py, pc), device_id_type=MESH)
                cp.wait_send()
                cp.wait_recv()

    srcs = [s[2] for s in started]
    lands = [s[3] for s in started]
    outs = pl.pallas_call(
        body, name=name,
        out_shape=[pltpu.HBM(s.shape, s.dtype) for s in srcs] + [pltpu.HBM(l.shape, l.dtype) for l in lands],
        in_specs=[HBM] * (2 * n) + [SEM] * (2 * n) + [pl.BlockSpec(memory_space=pl.ANY)],
        out_specs=[HBM] * (2 * n),
        input_output_aliases={i: i for i in range(2 * n)},
        compiler_params=pltpu.CompilerParams(has_side_effects=EFFECT),
    )(*srcs, *lands, *[s[0] for s in started], *[s[1] for s in started], after)
    return outs[n:]


def _cols_from_blocks(blocks):
    _, rows, w = blocks.shape
    return blocks.transpose(1, 0, 2).reshape(rows, NDEV * w)


def _cols_to_blocks(full):
    rows, total = full.shape
    return full.reshape(rows, NDEV, total // NDEV).transpose(1, 0, 2)


def _mix_pad(wt):
    xp, q, k, v, z, ba, gp, gd = jnp.split(wt, (512, 1536, 2560, 3584, 4608, 4624, 5648), axis=0)
    pad = jnp.zeros((MIXP - OFF_BA - 16, wt.shape[1]), wt.dtype)
    return jnp.concatenate([q, k, v, z, gp, gd, xp, ba, pad], axis=0)


def _mix_unpad(wt):
    q, k, v, z, gp, gd, xp, ba = (wt[OFF_Q:OFF_K], wt[OFF_K:OFF_V], wt[OFF_V:OFF_Z], wt[OFF_Z:OFF_GP],
                                  wt[OFF_GP:OFF_GD], wt[OFF_GD:OFF_XP], wt[OFF_XP:OFF_BA], wt[OFF_BA:OFF_BA + 16])
    return jnp.concatenate([xp, q, k, v, z, ba, gp, gd], axis=0)


def _lane_row(vec8):
    return jnp.zeros((1, 128), F32).at[0, NH:2 * NH].set(vec8)


def _ffn_fwd(x, g, shift, scale, gate, w_in, w_out, tag, token=None, start_more=None):
    h = _norm_mod_fwd(x, g, shift, scale, f"{tag}_norm")
    if isinstance(w_in, tuple):
        w_in, = _push_wait([w_in], False, h, f"{tag}_gather_wait_in")
    w_in = w_in.reshape(2 * FH, D)
    u, a = _swiglu_up(h, w_in, f"{tag}_up", after=token)
    w_out, = _push_wait([w_out], False, a, f"{tag}_gather_wait_out")
    w_out = w_out.reshape(FH, D)
    y = _matmul(a, w_out, a_blk=True, out_dtype=F32, name=f"{tag}_down",
                after=None if start_more is None else start_more(h))
    return _resid_fwd(x, y, gate, 0.5, f"{tag}_res"), (h, u, a, y), w_in, w_out


def _ffn_bwd(dx_out, x, g, scale, gate, w_in, w_out, saved, tag):
    h, u, a, y = saved
    t = x.shape[0]
    dy, dgate = _resid_bwd(dx_out, y, gate, 0.5, f"{tag}_res_bwd")
    dw_out = _matmul(a, dy, ta=True, a_blk=True, out_dtype=BF16, name=f"{tag}_down_dw")
    sent_out, token = _push_start([dw_out.reshape(NDEV, FH // NDEV, D)], True, f"{tag}_grad_start_out")
    du = _swiglu_down_bwd(dy, w_out, u, f"{tag}_down_dx", after=token).reshape(NDEV, t, FB)
    dw_in = _matmul(du, h, ta=True, a_blk=True, out_dtype=BF16, name=f"{tag}_up_dw")
    sent_in, token = _push_start([dw_in.reshape(NDEV, FB, D)], True, f"{tag}_grad_start_in")
    dh = _matmul(du, w_in, a_blk=True, out_dtype=F32, name=f"{tag}_up_dx", after=token)
    dx, dshift, dscale, dg = _norm_mod_bwd(x, g, scale, dh, dx_out, f"{tag}_norm_bwd")
    return dx, (dshift, dscale, dgate), dg, sent_in + sent_out


def kernel(x, c, ada_w, ada_b, norm_g, ffn1_w_in, ffn1_w_out, ffn2_w_in, ffn2_w_out, mix_w_in, conv_w, a_log, dt_bias, dn_norm_g, pool_w, pool_scale, pool_proj, dn_proj, mix_w_out, final_g, loss_target, m_ada_w, m_ada_b, m_norm_g, m_ffn1_w_in, m_ffn1_w_out, m_ffn2_w_in, m_ffn2_w_out, m_mix_w_in, m_conv_w, m_a_log, m_dt_bias, m_dn_norm_g, m_pool_w, m_pool_scale, m_pool_proj, m_dn_proj, m_mix_w_out, m_final_g, v_ada_w, v_ada_b, v_norm_g, v_ffn1_w_in, v_ffn1_w_out, v_ffn2_w_in, v_ffn2_w_out, v_mix_w_in, v_conv_w, v_a_log, v_dt_bias, v_dn_norm_g, v_pool_w, v_pool_scale, v_pool_proj, v_dn_proj, v_mix_w_out, v_final_g):
    me = 4 * lax.axis_index("x") + 2 * lax.axis_index("y") + lax.axis_index("c")
    x0 = x[0]
    target = loss_target[0]
    t = x0.shape[0]

    big = [ffn1_w_in[0], ffn1_w_out[0], ffn2_w_in[0], ffn2_w_out[0], mix_w_in[0], pool_proj[0], dn_proj[0],
           mix_w_out[0]]
    small = jnp.concatenate([c.reshape(8, 128), conv_w[0].reshape(12, 128), norm_g[0].reshape(3, 128),
                             jnp.zeros((1, 128), F32)], axis=0)
    small_all, = _all_gather([small], "gather_small")
    c_all = small_all[:, 0:8, :].reshape(NDEV, D)
    conv_full = small_all[:, 8:20, :].reshape(NDEV, 4, 384).transpose(1, 0, 2).reshape(4, 3 * D)
    norm_full = small_all[:, 20:23, :].reshape(NDEV, 3, 128).transpose(1, 0, 2).reshape(3, D)

    ncol = ada_w.shape[2]
    ada_b_mine = lax.dynamic_slice(ada_b, (0, me * ncol), (1, ncol))
    mod_cols = _ada_fwd(c_all, ada_w[0], ada_b_mine, "ada_fwd")
    transposed = (0, 2, 4)
    payload = [(w.T if i in transposed else w).astype(BF16) for i, w in enumerate(big)]
    mod_all, w_in1 = _all_gather([mod_cols, payload[0]], "gather_mod_first_weight")
    started, token = _push_start([payload[1], payload[4]], False, "gather_start", after=mod_all)
    started = {1: started[0], 4: started[1]}

    def start_rest(h):
        more, token = _push_start([payload[i] for i in (5, 6, 7, 2, 3)], False, "gather_start_rest", after=h)
        started.update(zip((5, 6, 7, 2, 3), more))
        return token

    mod = lax.dynamic_index_in_dim(mod_all, me, axis=1, keepdims=False).reshape(9, D)
    shift = [mod[3 * s:3 * s + 1] for s in range(3)]
    scale = [mod[3 * s + 1:3 * s + 2] for s in range(3)]
    gate = [mod[3 * s + 2:3 * s + 3] for s in range(3)]
    ng = [norm_full[s:s + 1] for s in range(3)]
    fg = final_g.reshape(1, D)
    al_row = _lane_row(a_log[0])
    dt_row = _lane_row(dt_bias[0])
    gn = dn_norm_g
    pw = pool_w[0]
    ps = pool_scale

    x1, saved1, w_in1, w_out1 = _ffn_fwd(x0, ng[0], shift[0], scale[0], gate[0], w_in1, started[1], "ffn1", token,
                                         start_rest)

    h1 = _norm_mod_fwd(x1, ng[1], shift[1], scale[1], "mix_norm")
    seg, = _push_wait([started[4]], False, h1, "mix_gather_wait")
    w_mix = _mix_pad(seg.reshape(MIX_RAW, D))
    proj = _matmul(h1, w_mix, tb=True, out_dtype=F32, name="mix_in")
    qh, kh, vh, bg = _dn_pre_fwd(proj, conv_full, al_row, dt_row, "dn_pre")
    seg = _push_wait([started[i] for i in (5, 6, 7)], False, qh, "mix_gather_wait_rest")
    w_pp = _cols_from_blocks(seg[0])
    w_dn = seg[1].reshape(D, D)
    w_mo = seg[2].reshape(D, D)
    ya = _pool_fwd(proj, pw, ps, w_pp, "pool_fwd")
    u, w, qk, qd, kd, eg, inv = _dn_local_fwd(qh, kh, vh, bg, "dn_local")
    o, s_saved = _dn_scan_fwd(u, w, qk, qd, kd, eg, "dn_scan")
    ob = _dn_post_fwd(o, proj, gn, "dn_post")
    yb = _matmul(ob, w_dn, out_dtype=F32, name="dn_out")
    merged = _merge_fwd(ya, yb, proj, "merge")
    mix_y = _matmul(merged, w_mo, out_dtype=F32, name="mix_out")
    x2 = _resid_fwd(x1, mix_y, gate[1], 1.0, "mix_res")

    x3, saved2, w_in2, w_out2 = _ffn_fwd(x2, ng[2], shift[2], scale[2], gate[2], started[2], started[3], "ffn2")
    loss_row, dx3, dfg = _final_loss(x3, fg, target, "loss")

    dx2, dmod2, dng2, sent2 = _ffn_bwd(dx3, x2, ng[2], scale[2], gate[2], w_in2, w_out2, saved2, "ffn2")

    dmy, dgate1 = _resid_bwd(dx2, mix_y, gate[1], 1.0, "mix_res_bwd")
    dmerged = _matmul(dmy, w_mo, tb=True, out_dtype=F32, name="mix_out_dx")
    dw_mo = _matmul(merged, dmy, ta=True, out_dtype=BF16, name="mix_out_dw")
    dya, dyb, dgp, dgd = _merge_bwd(dmerged, ya, yb, proj, "merge_bwd")
    dob = _matmul(dyb, w_dn, tb=True, out_dtype=F32, name="dn_out_dx")
    dw_dn = _matmul(ob, dyb, ta=True, out_dtype=BF16, name="dn_out_dw")
    do, dz, dgn = _dn_post_bwd(o, proj, gn, dob, "dn_post_bwd")
    du, dw, dqk, dqd, dkd, deg = _dn_scan_bwd(u, w, qk, qd, kd, eg, s_saved, do, "dn_scan_bwd")
    dqh, dkh, dvh, dbg = _dn_local_bwd(qh, kh, vh, bg, inv, du, dw, dqk, dqd, dkd, deg, "dn_local_bwd")
    dconv, draw, dal, ddt = _dn_pre_bwd_act(proj, conv_full, al_row, dt_row, dqh, dkh, dvh, dbg, "dn_pre_bwd_act")
    dqkv, dcw = _dn_pre_bwd_conv(proj, conv_full, dconv, "dn_pre_bwd_conv")
    dwin, dpl, dpw, dps, dpp = _pool_bwd_local(proj, pw, ps, w_pp, dya, "pool_bwd_local")
    dxp = _pool_bwd_window(dwin, dpl, "pool_bwd_window")
    dproj = jnp.concatenate([dqkv, dz, dgp, dgd, dxp, draw, jnp.zeros((t, MIXP - OFF_BA - 128), BF16)], axis=1)
    dw_mix = _matmul(dproj, h1, ta=True, out_dtype=BF16, name="mix_in_dw")
    sent1, token = _push_start(
        [_mix_unpad(dw_mix).reshape(NDEV, MIX_RAW // NDEV, D), _cols_to_blocks(dpp.astype(BF16)),
         dw_dn.reshape(NDEV, -1, D), dw_mo.reshape(NDEV, -1, D)], True, "mix_grad_start")
    dh1 = _matmul(dproj, w_mix, out_dtype=F32, name="mix_in_dx", after=token)
    dx1, dsh1, dsc1, dng1 = _norm_mod_bwd(x1, ng[1], scale[1], dh1, dx2, "mix_norm_bwd")

    dx0, dmod0, dng0, sent0 = _ffn_bwd(dx1, x0, ng[0], scale[0], gate[0], w_in1, w_out1, saved1, "ffn1")

    dmod = jnp.concatenate([*dmod0, dsh1, dsc1, dgate1, *dmod2], axis=1).reshape(-1)
    flat = jnp.concatenate([
        dmod, dal[0, NH:2 * NH], ddt[0, NH:2 * NH], dgn.reshape(-1), dps.reshape(-1), dfg.reshape(-1),
        dpw.reshape(-1), jnp.concatenate([dng0, dng1, dng2], axis=0).reshape(-1), dcw.reshape(-1)])
    nflat = 90 * D
    flat = jnp.concatenate([flat, jnp.zeros((nflat - flat.shape[0],), F32)]).reshape(90, D)
    flat_all, = _all_gather([flat], "gather_small_grads")
    tot = _sum_devices(flat_all, F32, "sum_small_grads").reshape(-1)
    dmod_all = flat_all.reshape(NDEV, nflat)[:, :9 * D]
    dmod_cols = lax.dynamic_slice(dmod_all, (0, me * ncol), (NDEV, ncol))
    g_ada_w = _ada_bwd(c_all.T, dmod_cols, "ada_bwd")

    p = 0
    pieces = {}
    for nm, size in (("ada_b", 9 * D), ("a_log", NH), ("dt_bias", NH), ("dn_norm_g", HD), ("pool_scale", PW),
                     ("final_g", D), ("pool_w", 4 * PG * PG), ("norm_g", 3 * D), ("conv_w", 12 * D)):
        pieces[nm] = tot[p:p + size]
        p += size
    g_norm = lax.dynamic_slice(pieces["norm_g"].reshape(3, D), (0, me * 128), (3, 128))
    g_conv = lax.dynamic_slice(pieces["conv_w"].reshape(4, 3 * D), (0, me * 384), (4, 384))

    grads = {
        "ada_w": g_ada_w.reshape(ada_w.shape), "ada_b": pieces["ada_b"].reshape(ada_b.shape),
        "norm_g": g_norm.reshape(norm_g.shape), "conv_w": g_conv.reshape(conv_w.shape),
        "a_log": pieces["a_log"].reshape(a_log.shape), "dt_bias": pieces["dt_bias"].reshape(dt_bias.shape),
        "dn_norm_g": pieces["dn_norm_g"].reshape(dn_norm_g.shape), "pool_w": pieces["pool_w"].reshape(pool_w.shape),
        "pool_scale": pieces["pool_scale"].reshape(pool_scale.shape),
        "final_g": pieces["final_g"].reshape(final_g.shape),
    }
    weights = {"ada_w": ada_w, "ada_b": ada_b, "norm_g": norm_g, "ffn1_w_in": ffn1_w_in, "ffn1_w_out": ffn1_w_out,
               "ffn2_w_in": ffn2_w_in, "ffn2_w_out": ffn2_w_out, "mix_w_in": mix_w_in, "conv_w": conv_w,
               "a_log": a_log, "dt_bias": dt_bias, "dn_norm_g": dn_norm_g, "pool_w": pool_w,
               "pool_scale": pool_scale, "pool_proj": pool_proj, "dn_proj": dn_proj, "mix_w_out": mix_w_out,
               "final_g": final_g}
    m_in = {"ada_w": m_ada_w, "ada_b": m_ada_b, "norm_g": m_norm_g, "ffn1_w_in": m_ffn1_w_in,
            "ffn1_w_out": m_ffn1_w_out, "ffn2_w_in": m_ffn2_w_in, "ffn2_w_out": m_ffn2_w_out,
            "mix_w_in": m_mix_w_in, "conv_w": m_conv_w, "a_log": m_a_log, "dt_bias": m_dt_bias,
            "dn_norm_g": m_dn_norm_g, "pool_w": m_pool_w, "pool_scale": m_pool_scale, "pool_proj": m_pool_proj,
            "dn_proj": m_dn_proj, "mix_w_out": m_mix_w_out, "final_g": m_final_g}
    v_in = {"ada_w": v_ada_w, "ada_b": v_ada_b, "norm_g": v_norm_g, "ffn1_w_in": v_ffn1_w_in,
            "ffn1_w_out": v_ffn1_w_out, "ffn2_w_in": v_ffn2_w_in, "ffn2_w_out": v_ffn2_w_out,
            "mix_w_in": v_mix_w_in, "conv_w": v_conv_w, "a_log": v_a_log, "dt_bias": v_dt_bias,
            "dn_norm_g": v_dn_norm_g, "pool_w": v_pool_w, "pool_scale": v_pool_scale, "pool_proj": v_pool_proj,
            "dn_proj": v_dn_proj, "mix_w_out": v_mix_w_out, "final_g": v_final_g}

    names = list(weights)
    large = ("ada_w", "ffn1_w_in", "ffn1_w_out", "ffn2_w_in", "ffn2_w_out", "mix_w_in", "pool_proj", "dn_proj",
             "mix_w_out")
    delta, new_m, new_v = {}, {}, {}

    flipped = ("ffn1_w_in", "ffn2_w_in", "mix_w_in")

    def update(nm):
        shp = weights[nm].shape
        two_d = (shp[-2], shp[-1])
        view = (lambda a: a.reshape(two_d).T) if nm in flipped else (lambda a: a.reshape(two_d))
        back = (lambda a: a.T.reshape(shp)) if nm in flipped else (lambda a: a.reshape(shp))
        g2 = grads_t[nm] if nm in flipped else grads[nm].reshape(two_d)
        d_, m_, v_ = _adamw(view(weights[nm]), g2, view(m_in[nm]), view(v_in[nm]), f"adamw_{nm}")
        delta[nm], new_m[nm], new_v[nm] = back(d_), back(m_), back(v_)
        return d_

    grads_t = {}

    def reduce(sent, group, after, tag):
        for nm, r in zip(group, _push_wait(sent, True, after, f"{tag}_grad_wait")):
            total = _sum_devices(r, F32, f"sum_grads_{nm}")
            if nm in flipped:
                grads_t[nm] = total
                grads[nm] = total.T.reshape(weights[nm].shape)
            else:
                grads[nm] = total.reshape(weights[nm].shape)

    done = update("ada_w")
    reduce(sent2, ("ffn2_w_in", "ffn2_w_out"), done, "ffn2")
    update("ffn2_w_in")
    done = update("ffn2_w_out")
    reduce(sent1, ("mix_w_in", "pool_proj", "dn_proj", "mix_w_out"), done, "mix")
    for nm in ("mix_w_in", "pool_proj", "dn_proj", "mix_w_out"):
        done = update(nm)
    reduce(sent0, ("ffn1_w_in", "ffn1_w_out"), done, "ffn1")
    update("ffn1_w_in")
    update("ffn1_w_out")
    rest = [nm for nm in names if nm not in large]
    total = sum(weights[nm].size for nm in rest)
    padded = -(-total // D) * D

    def pack(tree, fill):
        flat_ = jnp.concatenate([tree[nm].reshape(-1) for nm in rest])
        return jnp.concatenate([flat_, jnp.full((padded - total,), fill, F32)]).reshape(-1, D)

    d_, m_, v_ = _adamw(pack(weights, 0.0), pack(grads, 0.0), pack(m_in, 0.0), pack(v_in, 1.0), "adamw_small")
    p = 0
    for nm in rest:
        size = weights[nm].size
        shp = weights[nm].shape
        delta[nm] = d_.reshape(-1)[p:p + size].reshape(shp)
        new_m[nm] = m_.reshape(-1)[p:p + size].reshape(shp)
        new_v[nm] = v_.reshape(-1)[p:p + size].reshape(shp)
        p += size

    loss = lax.psum(loss_row[0, 0], ("x", "y", "c"))
    grad_x = dx0.reshape(x.shape)
    return (loss, grad_x, *[grads[nm] for nm in names], *[delta[nm] for nm in names],
            *[new_m[nm] for nm in names], *[new_v[nm] for nm in names])
```

```python
import functools

import jax
import jax.numpy as jnp
from jax import lax
from jax.experimental import pallas as pl
from jax.experimental.pallas import tpu as pltpu

F32 = jnp.float32
BF16 = jnp.bfloat16
SDS = jax.ShapeDtypeStruct
HI = lax.Precision.HIGHEST

D = 1024
FH = 2816
FB = 704
NH = 8
HD = 128
CH = 64
SCAN_CHUNKS = 2
NDEV = 8
PW = 512
PG = 128
RMS_EPS = 1e-6
L2_EPS = 1e-6
TR = 512
HALO = 16
VMEM_LIMIT = 56 * 1024 * 1024

MIXP = 6912
OFF_Q, OFF_K, OFF_V, OFF_Z, OFF_GP, OFF_GD, OFF_XP, OFF_BA = 0, 1024, 2048, 3072, 4096, 5120, 6144, 6656
MIX_RAW = 6672

ADAM_LR = 0.001
ADAM_B1 = 0.9
ADAM_B2 = 0.999
ADAM_EPS = 1e-08
ADAM_WD = 0.01
ADAM_STEP = 10

NN = (((1,), (0,)), ((), ()))
NT = (((1,), (1,)), ((), ()))
TN = (((0,), (0,)), ((), ()))


def _dg(a, b, dims, prec=None):
    return lax.dot_general(a, b, dims, precision=prec, preferred_element_type=F32)


def _make_dots(prec):
    @jax.custom_vjp
    def nn(a, b):
        return _dg(a, b, NN, prec)

    @jax.custom_vjp
    def nt(a, b):
        return _dg(a, b, NT, prec)

    @jax.custom_vjp
    def tn(a, b):
        return _dg(a, b, TN, prec)

    nn.defvjp(lambda a, b: (nn(a, b), (a, b)), lambda r, d: (nt(d, r[1]), tn(r[0], d)))
    nt.defvjp(lambda a, b: (nt(a, b), (a, b)), lambda r, d: (nn(d, r[1]), tn(d, r[0])))
    tn.defvjp(lambda a, b: (tn(a, b), (a, b)), lambda r, d: (nt(r[1], d), nn(r[0], d)))
    return nn, nt, tn


_nn, _nt, _tn = _make_dots(None)


def _params(sem):
    return pltpu.CompilerParams(dimension_semantics=sem, vmem_limit_bytes=VMEM_LIMIT)


def _sigmoid(x):
    return 1.0 / (1.0 + jnp.exp(-x))


def _silu(x):
    return x * _sigmoid(x)


def _dsilu(x):
    s = _sigmoid(x)
    return s * (1.0 + x * (1.0 - s))


def _pick(n, cands):
    for c in cands:
        if n % c == 0:
            return c
    raise ValueError(f"no tile for {n}")


def _iota(shape, dim):
    return lax.broadcasted_iota(jnp.int32, shape, dim)


def _matmul(a, b, *, ta=False, tb=False, a_blk=False, b_blk=False, o_blk=False, tm=None, tn=None, tk=None,
            out_dtype, name, after=None):
    if a_blk:
        nb, r, cb = a.shape
        if ta:
            k_dim, m_dim, tm = r, nb * cb, cb
        else:
            m_dim, k_dim, tk = r, nb * cb, cb
    else:
        k_dim, m_dim = a.shape if ta else a.shape[::-1]
    if b_blk:
        nb, r, cb = b.shape
        if tb:
            n_dim, tk = r, cb
            assert nb * cb == k_dim
        else:
            n_dim, tn = nb * cb, cb
            assert r == k_dim
    else:
        n_dim = b.shape[0] if tb else b.shape[1]
    tm = tm or _pick(m_dim, (1024, 768, 512, 256, 128))
    tn = tn or _pick(n_dim, (1024, 768, 512, 256, 128))
    tk = tk or (k_dim if (k_dim <= 2816 and not ta) else _pick(k_dim, (2816, 2304, 1024, 512, 256)))
    nk = k_dim // tk
    dims = ((((0,) if ta else (1,)), ((1,) if tb else (0,))), ((), ()))

    def body(a_ref, b_ref, *rest):
        o_ref, acc_ref = rest[-2:]
        k = pl.program_id(2)

        @pl.when(k == 0)
        def _():
            acc_ref[...] = jnp.zeros_like(acc_ref)

        acc_ref[...] += lax.dot_general(a_ref[...].astype(BF16), b_ref[...].astype(BF16), dims,
                                        preferred_element_type=F32)

        @pl.when(k == nk - 1)
        def _():
            o_ref[...] = acc_ref[...].astype(o_ref.dtype)

    if a_blk:
        a_spec = (pl.BlockSpec((None, tk, tm), lambda i, j, k: (i, k, 0)) if ta
                  else pl.BlockSpec((None, tm, tk), lambda i, j, k: (k, i, 0)))
    else:
        a_spec = (pl.BlockSpec((tk, tm), lambda i, j, k: (k, i)) if ta
                  else pl.BlockSpec((tm, tk), lambda i, j, k: (i, k)))
    if b_blk:
        b_spec = (pl.BlockSpec((None, tn, tk), lambda i, j, k: (k, j, 0)) if tb
                  else pl.BlockSpec((None, tk, tn), lambda i, j, k: (j, k, 0)))
    else:
        b_spec = (pl.BlockSpec((tn, tk), lambda i, j, k: (j, k)) if tb
                  else pl.BlockSpec((tk, tn), lambda i, j, k: (k, j)))
    if o_blk:
        o_spec = pl.BlockSpec((None, tm, tn), lambda i, j, k: (j, i, 0))
        o_shape = SDS((n_dim // tn, m_dim, tn), out_dtype)
    else:
        o_spec = pl.BlockSpec((tm, tn), lambda i, j, k: (i, j))
        o_shape = SDS((m_dim, n_dim), out_dtype)
    return pl.pallas_call(
        body, grid=(m_dim // tm, n_dim // tn, nk),
        in_specs=[a_spec, b_spec] + ([] if after is None else [pl.BlockSpec(memory_space=pl.ANY)]),
        out_specs=o_spec,
        out_shape=o_shape,
        scratch_shapes=[pltpu.VMEM((tm, tn), F32)],
        compiler_params=_params(("parallel", "parallel", "arbitrary")),
        name=name,
    )(a, b, *([] if after is None else [after]))


def _row(width, col=0):
    return pl.BlockSpec((TR, width), lambda i: (i, col))


def _vec(width):
    return pl.BlockSpec((1, width), lambda i: (0, 0))


def _norm_mod_fwd(x, g, shift, scale, name):
    t = x.shape[0]

    def body(x_ref, g_ref, sh_ref, sc_ref, o_ref):
        xv = x_ref[...]
        r = lax.rsqrt(jnp.mean(xv * xv, axis=-1, keepdims=True) + RMS_EPS)
        o_ref[...] = (((xv * r) * g_ref[...]) * (1.0 + sc_ref[...]) + sh_ref[...]).astype(o_ref.dtype)

    return pl.pallas_call(
        body, grid=(t // TR,), in_specs=[_row(D), _vec(D), _vec(D), _vec(D)], out_specs=_row(D),
        out_shape=SDS((t, D), BF16), compiler_params=_params(("parallel",)), name=name,
    )(x, g, shift, scale)


def _norm_mod_bwd(x, g, scale, dh, dx_in, name):
    t = x.shape[0]

    def body(x_ref, g_ref, sc_ref, dh_ref, dxi_ref, dx_ref, dsh_ref, dsc_ref, dg_ref):
        @pl.when(pl.program_id(0) == 0)
        def _():
            dsh_ref[...] = jnp.zeros_like(dsh_ref)
            dsc_ref[...] = jnp.zeros_like(dsc_ref)
            dg_ref[...] = jnp.zeros_like(dg_ref)

        xv = x_ref[...]
        gv = g_ref[...]
        dh = dh_ref[...]
        r = lax.rsqrt(jnp.mean(xv * xv, axis=-1, keepdims=True) + RMS_EPS)
        n = xv * r
        dsh_ref[...] += jnp.sum(dh, axis=0, keepdims=True)
        dsc_ref[...] += jnp.sum(dh * (n * gv), axis=0, keepdims=True)
        tt = dh * (1.0 + sc_ref[...])
        dg_ref[...] += jnp.sum(tt * n, axis=0, keepdims=True)
        dn = tt * gv
        dx_ref[...] = dxi_ref[...] + r * (dn - n * jnp.mean(dn * n, axis=-1, keepdims=True))

    return pl.pallas_call(
        body, grid=(t // TR,), in_specs=[_row(D), _vec(D), _vec(D), _row(D), _row(D)],
        out_specs=[_row(D), _vec(D), _vec(D), _vec(D)],
        out_shape=[SDS((t, D), F32), SDS((1, D), F32), SDS((1, D), F32), SDS((1, D), F32)],
        compiler_params=_params(("arbitrary",)), name=name,
    )(x, g, scale, dh, dx_in)


def _swiglu_up(h, w_in, name, after=None):
    t = h.shape[0]
    tm = _pick(t, (1024, 512, 256))
    half = NDEV // 2
    extra = [] if after is None else [after]

    def body(h_ref, wg_ref, wu_ref, *rest):
        u_ref, a_ref = rest[-2:]
        hv = h_ref[...]
        gate = _dg(hv, wg_ref[...], NT)
        up = _dg(hv, wu_ref[...], NT)
        u_ref[0] = gate.astype(u_ref.dtype)
        u_ref[1] = up.astype(u_ref.dtype)
        a_ref[...] = (_silu(gate) * up).astype(a_ref.dtype)

    return pl.pallas_call(
        body, grid=(t // tm, half),
        in_specs=[pl.BlockSpec((tm, D), lambda i, j: (i, 0)),
                  pl.BlockSpec((FB, D), lambda i, j: (j, 0)),
                  pl.BlockSpec((FB, D), lambda i, j: (j + half, 0))]
        + [pl.BlockSpec(memory_space=pl.ANY)] * len(extra),
        out_specs=[pl.BlockSpec((2, None, tm, FB), lambda i, j: (0, j, i, 0)),
                   pl.BlockSpec((None, tm, FB), lambda i, j: (j, i, 0))],
        out_shape=[SDS((2, half, t, FB), BF16), SDS((half, t, FB), BF16)],
        compiler_params=_params(("parallel", "parallel")), name=name,
    )(h, w_in, w_in, *extra)


def _swiglu_down_bwd(dy, w_out, u, name, after=None):
    t = dy.shape[0]
    tm = _pick(t, (1024, 512, 256))
    half = NDEV // 2
    extra = [] if after is None else [after]
    pair = pl.BlockSpec((2, None, tm, FB), lambda i, j: (0, j, i, 0))

    def body(dy_ref, w_ref, u_ref, *rest):
        o_ref = rest[-1]
        da = _dg(dy_ref[...], w_ref[...], NT)
        gate = u_ref[0].astype(F32)
        o_ref[0] = (da * u_ref[1].astype(F32) * _dsilu(gate)).astype(o_ref.dtype)
        o_ref[1] = (da * _silu(gate)).astype(o_ref.dtype)

    return pl.pallas_call(
        body, grid=(t // tm, half),
        in_specs=[pl.BlockSpec((tm, D), lambda i, j: (i, 0)), pl.BlockSpec((FB, D), lambda i, j: (j, 0)), pair]
        + [pl.BlockSpec(memory_space=pl.ANY)] * len(extra),
        out_specs=pair, out_shape=SDS((2, half, t, FB), BF16),
        compiler_params=_params(("parallel", "parallel")), name=name,
    )(dy, w_out, u, *extra)


def _resid_fwd(x, y, gate, coef, name):
    t = x.shape[0]

    def body(x_ref, y_ref, g_ref, o_ref):
        o_ref[...] = x_ref[...] + (coef * g_ref[...]) * y_ref[...]

    return pl.pallas_call(
        body, grid=(t // TR,), in_specs=[_row(D), _row(D), _vec(D)], out_specs=_row(D),
        out_shape=SDS((t, D), F32), compiler_params=_params(("parallel",)), name=name,
    )(x, y, gate)


def _resid_bwd(dx, y, gate, coef, name):
    t = dx.shape[0]

    def body(dx_ref, y_ref, g_ref, dy_ref, dg_ref):
        @pl.when(pl.program_id(0) == 0)
        def _():
            dg_ref[...] = jnp.zeros_like(dg_ref)

        dxv = dx_ref[...]
        dy_ref[...] = ((coef * g_ref[...]) * dxv).astype(dy_ref.dtype)
        dg_ref[...] += jnp.sum((coef * dxv) * y_ref[...], axis=0, keepdims=True)

    return pl.pallas_call(
        body, grid=(t // TR,), in_specs=[_row(D), _row(D), _vec(D)], out_specs=[_row(D), _vec(D)],
        out_shape=[SDS((t, D), BF16), SDS((1, D), F32)],
        compiler_params=_params(("arbitrary",)), name=name,
    )(dx, y, gate)


def _final_loss(x, fg, target, name):
    t = x.shape[0]
    nt = t // TR

    def body(x_ref, g_ref, t_ref, loss_ref, dx_ref, dg_ref, acc_ref):
        i = pl.program_id(0)

        @pl.when(i == 0)
        def _():
            acc_ref[...] = jnp.zeros_like(acc_ref)
            dg_ref[...] = jnp.zeros_like(dg_ref)

        xv = x_ref[...]
        gv = g_ref[...]
        r = lax.rsqrt(jnp.mean(xv * xv, axis=-1, keepdims=True) + RMS_EPS)
        n = xv * r
        err = n * gv - t_ref[...]
        acc_ref[...] += jnp.sum(err * err, axis=0, keepdims=True)
        dy = err * (1.0 / D)
        dg_ref[...] += jnp.sum(dy * n, axis=0, keepdims=True)
        dn = dy * gv
        dx_ref[...] = r * (dn - n * jnp.mean(dn * n, axis=-1, keepdims=True))

        @pl.when(i == nt - 1)
        def _():
            tot = jnp.sum(acc_ref[...], axis=1, keepdims=True) * (0.5 / D)
            loss_ref[...] = jnp.broadcast_to(tot, loss_ref.shape)

    return pl.pallas_call(
        body, grid=(nt,), in_specs=[_row(D), _vec(D), _row(D)],
        out_specs=[_vec(128), _row(D), _vec(D)],
        out_shape=[SDS((1, 128), F32), SDS((t, D), F32), SDS((1, D), F32)],
        scratch_shapes=[pltpu.VMEM((1, D), F32)],
        compiler_params=_params(("arbitrary",)), name=name,
    )(x, fg, target)


def _halo_prev(width, col):
    per = TR // HALO
    return pl.BlockSpec((HALO, width), lambda i: (jnp.maximum(i * per - 1, 0), col))


def _halo_next(width, col, nt):
    per = TR // HALO
    return pl.BlockSpec((HALO, width), lambda i: (jnp.minimum((i + 1) * per, nt * per - 1), col))


def _pool_windows(ext, tile_index):
    rows = _iota((TR, PG), 0) + tile_index * TR + 1
    pooled, counts = [], []
    for gi in range(4):
        w = 2 << gi
        e = ext[:, gi * PG:(gi + 1) * PG]
        s = e
        step = 1
        while step < w:
            s = s + pltpu.roll(s, step, 0)
            step *= 2
        cnt = jnp.minimum(rows, w).astype(F32)
        pooled.append(s[HALO:] / cnt - e[HALO:])
        counts.append(cnt)
    return pooled, counts


def _pool_fwd(proj, pool_w, pool_scale, pool_proj, name):
    t = proj.shape[0]
    xcol = OFF_XP // PW

    def body(x_ref, h_ref, pw_ref, ps_ref, pp_ref, o_ref):
        i = pl.program_id(0)
        halo = jnp.where(i > 0, h_ref[...], 0.0)
        ext = jnp.concatenate([halo, x_ref[...]], axis=0)
        pooled, _ = _pool_windows(ext, i)
        mixed = [_dg(pooled[g].astype(BF16), pw_ref[g].astype(BF16), NN) for g in range(4)]
        ypre = jnp.concatenate(mixed, axis=1) * ps_ref[...]
        o_ref[...] = _dg(ypre.astype(BF16), pp_ref[...], NN)

    return pl.pallas_call(
        body, grid=(t // TR,),
        in_specs=[_row(PW, xcol), _halo_prev(PW, xcol),
                  pl.BlockSpec((4, PG, PG), lambda i: (0, 0, 0)), _vec(PW),
                  pl.BlockSpec((PW, D), lambda i: (0, 0))],
        out_specs=_row(D), out_shape=SDS((t, D), F32),
        compiler_params=_params(("parallel",)), name=name,
    )(proj, proj, pool_w, pool_scale, pool_proj)


def _pool_bwd_local(proj, pool_w, pool_scale, pool_proj, dya, name):
    t = proj.shape[0]
    xcol = OFF_XP // PW

    def body(x_ref, h_ref, pw_ref, ps_ref, pp_ref, dya_ref, dwin_ref, dpl_ref, dpw_ref, dps_ref, dpp_ref):
        i = pl.program_id(0)

        @pl.when(i == 0)
        def _():
            dpw_ref[...] = jnp.zeros_like(dpw_ref)
            dps_ref[...] = jnp.zeros_like(dps_ref)
            dpp_ref[...] = jnp.zeros_like(dpp_ref)

        halo = jnp.where(i > 0, h_ref[...], 0.0)
        ext = jnp.concatenate([halo, x_ref[...]], axis=0)
        pooled, counts = _pool_windows(ext, i)
        mixed = jnp.concatenate(
            [_dg(pooled[g].astype(BF16), pw_ref[g].astype(BF16), NN) for g in range(4)], axis=1)
        ps = ps_ref[...]
        ypre = mixed * ps
        dyab = dya_ref[...].astype(BF16)
        dypre = _dg(dyab, pp_ref[...], NT)
        dpp_ref[...] += _dg(ypre.astype(BF16), dyab, TN)
        dps_ref[...] += jnp.sum(dypre * mixed, axis=0, keepdims=True)
        dmixed = dypre * ps
        for g in range(4):
            dm = dmixed[:, g * PG:(g + 1) * PG].astype(BF16)
            dpw_ref[g] += _dg(pooled[g].astype(BF16), dm, TN)
            dpooled = _dg(dm, pw_ref[g].astype(BF16), NT)
            dwin_ref[:, g * PG:(g + 1) * PG] = dpooled / counts[g]
            dpl_ref[:, g * PG:(g + 1) * PG] = dpooled

    return pl.pallas_call(
        body, grid=(t // TR,),
        in_specs=[_row(PW, xcol), _halo_prev(PW, xcol),
                  pl.BlockSpec((4, PG, PG), lambda i: (0, 0, 0)), _vec(PW),
                  pl.BlockSpec((PW, D), lambda i: (0, 0)), _row(D)],
        out_specs=[_row(PW), _row(PW), pl.BlockSpec((4, PG, PG), lambda i: (0, 0, 0)), _vec(PW),
                   pl.BlockSpec((PW, D), lambda i: (0, 0))],
        out_shape=[SDS((t, PW), F32), SDS((t, PW), F32), SDS((4, PG, PG), F32), SDS((1, PW), F32),
                   SDS((PW, D), F32)],
        compiler_params=_params(("arbitrary",)), name=name,
    )(proj, proj, pool_w, pool_scale, pool_proj, dya)


def _pool_bwd_window(dwin, dpl, name):
    t = dwin.shape[0]
    nt = t // TR
    ext_rows = TR + HALO

    def body(dw_ref, h_ref, dp_ref, o_ref):
        i = pl.program_id(0)
        halo = jnp.where(i < nt - 1, h_ref[...], 0.0)
        ext = jnp.concatenate([dw_ref[...], halo], axis=0)
        for gi in range(4):
            w = 2 << gi
            s = ext[:, gi * PG:(gi + 1) * PG]
            step = 1
            while step < w:
                s = s + pltpu.roll(s, ext_rows - step, 0)
                step *= 2
            o_ref[:, gi * PG:(gi + 1) * PG] = (s[:TR] - dp_ref[:, gi * PG:(gi + 1) * PG]).astype(o_ref.dtype)

    return pl.pallas_call(
        body, grid=(nt,), in_specs=[_row(PW), _halo_next(PW, 0, nt), _row(PW)], out_specs=_row(PW),
        out_shape=SDS((t, PW), BF16), compiler_params=_params(("parallel",)), name=name,
    )(dwin, dwin, dpl)


def _conv_group(ext, cw_ref, cols):
    acc = cw_ref[3:4, cols] * ext
    for j in range(3):
        acc = acc + cw_ref[j:j + 1, cols] * pltpu.roll(ext, 3 - j, 0)
    return acc[HALO:]


def _gate_terms(raw, al, dt):
    beta = _sigmoid(raw)
    xg = raw + dt
    sp = jnp.maximum(xg, 0.0) + jnp.log(1.0 + jnp.exp(-jnp.abs(xg)))
    g = -jnp.exp(al) * sp
    return beta, g, _sigmoid(xg)


def _dn_pre_fwd(proj, conv_w, al_row, dt_row, name):
    t = proj.shape[0]

    def body(x_ref, h_ref, cw_ref, ba_ref, al_ref, dt_ref, q_ref, k_ref, v_ref, bg_ref):
        i = pl.program_id(0)
        keep = i > 0
        for grp in range(24):
            cols = slice(grp * HD, (grp + 1) * HD)
            ext = jnp.concatenate([jnp.where(keep, h_ref[:, cols], 0.0), x_ref[:, cols]], axis=0)
            s = _silu(_conv_group(ext, cw_ref, cols))
            seg, head = divmod(grp, NH)
            hc = slice(head * HD, (head + 1) * HD)
            if seg == 0:
                q_ref[:, hc] = s * lax.rsqrt(jnp.sum(s * s, axis=-1, keepdims=True) + L2_EPS) * (HD ** -0.5)
            elif seg == 1:
                k_ref[:, hc] = s * lax.rsqrt(jnp.sum(s * s, axis=-1, keepdims=True) + L2_EPS)
            else:
                v_ref[:, hc] = s
        lane = _iota((TR, 128), 1)
        rowc = _iota((TR, 128), 0) % CH
        beta, g, _ = _gate_terms(ba_ref[...], al_ref[...], dt_ref[...])
        step = 1
        while step < CH:
            g = g + jnp.where(rowc >= step, pltpu.roll(g, step, 0), 0.0)
            step *= 2
        bg_ref[...] = jnp.where(lane < NH, beta, jnp.where(lane < 2 * NH, g, 0.0))

    return pl.pallas_call(
        body, grid=(t // TR,),
        in_specs=[_row(3 * D, 0), _halo_prev(3 * D, 0), pl.BlockSpec((4, 3 * D), lambda i: (0, 0)),
                  _row(128, OFF_BA // 128), _vec(128), _vec(128)],
        out_specs=[_row(D), _row(D), _row(D), _row(128)],
        out_shape=[SDS((t, D), F32), SDS((t, D), F32), SDS((t, D), F32), SDS((t, 128), F32)],
        compiler_params=_params(("parallel",)), name=name,
    )(proj, proj, conv_w, proj, al_row, dt_row)


def _dn_pre_bwd_act(proj, conv_w, al_row, dt_row, dq, dk, dv, dbg, name):
    t = proj.shape[0]

    def body(x_ref, h_ref, cw_ref, ba_ref, al_ref, dt_ref, dq_ref, dk_ref, dv_ref, dbg_ref,
             dc_ref, draw_ref, dal_ref, ddt_ref):
        i = pl.program_id(0)

        @pl.when(i == 0)
        def _():
            dal_ref[...] = jnp.zeros_like(dal_ref)
            ddt_ref[...] = jnp.zeros_like(ddt_ref)

        keep = i > 0
        for grp in range(24):
            cols = slice(grp * HD, (grp + 1) * HD)
            ext = jnp.concatenate([jnp.where(keep, h_ref[:, cols], 0.0), x_ref[:, cols]], axis=0)
            cv = _conv_group(ext, cw_ref, cols)
            seg, head = divmod(grp, NH)
            hc = slice(head * HD, (head + 1) * HD)
            if seg == 2:
                ds = dv_ref[:, hc]
            else:
                s = _silu(cv)
                r = lax.rsqrt(jnp.sum(s * s, axis=-1, keepdims=True) + L2_EPS)
                dy = dq_ref[:, hc] if seg == 0 else dk_ref[:, hc]
                c = (HD ** -0.5) if seg == 0 else 1.0
                ds = (c * r) * (dy - s * ((r * r) * jnp.sum(dy * s, axis=-1, keepdims=True)))
            dc_ref[:, cols] = ds * _dsilu(cv)
        lane = _iota((TR, 128), 1)
        rowc = _iota((TR, 128), 0) % CH
        isb = lane < NH
        isg = jnp.logical_and(lane >= NH, lane < 2 * NH)
        beta, g, sg = _gate_terms(ba_ref[...], al_ref[...], dt_ref[...])
        dbgv = dbg_ref[...]
        dg = dbgv
        step = 1
        while step < CH:
            dg = dg + jnp.where(rowc < CH - step, pltpu.roll(dg, TR - step, 0), 0.0)
            step *= 2
        da_raw = dg * (-jnp.exp(al_ref[...])) * sg
        draw_ref[...] = jnp.where(isb, dbgv * beta * (1.0 - beta), jnp.where(isg, da_raw, 0.0)).astype(draw_ref.dtype)
        dal_ref[...] += jnp.sum(jnp.where(isg, dg * g, 0.0), axis=0, keepdims=True)
        ddt_ref[...] += jnp.sum(jnp.where(isg, da_raw, 0.0), axis=0, keepdims=True)

    return pl.pallas_call(
        body, grid=(t // TR,),
        in_specs=[_row(3 * D, 0), _halo_prev(3 * D, 0), pl.BlockSpec((4, 3 * D), lambda i: (0, 0)),
                  _row(128, OFF_BA // 128), _vec(128), _vec(128), _row(D), _row(D), _row(D), _row(128)],
        out_specs=[_row(3 * D), _row(128), _vec(128), _vec(128)],
        out_shape=[SDS((t, 3 * D), F32), SDS((t, 128), BF16), SDS((1, 128), F32), SDS((1, 128), F32)],
        compiler_params=_params(("arbitrary",)), name=name,
    )(proj, proj, conv_w, proj, al_row, dt_row, dq, dk, dv, dbg)


def _dn_pre_bwd_conv(proj, conv_w, dconv, name):
    t = proj.shape[0]
    nt = t // TR
    ext_rows = TR + HALO

    def body(x_ref, h_ref, cw_ref, dc_ref, dn_ref, dx_ref, dcw_ref):
        i = pl.program_id(0)

        @pl.when(i == 0)
        def _():
            dcw_ref[...] = jnp.zeros_like(dcw_ref)

        keep_prev = i > 0
        keep_next = i < nt - 1
        for grp in range(24):
            cols = slice(grp * HD, (grp + 1) * HD)
            dct = dc_ref[:, cols]
            dext = jnp.concatenate([dct, jnp.where(keep_next, dn_ref[:, cols], 0.0)], axis=0)
            acc = cw_ref[3:4, cols] * dext
            for j in range(3):
                acc = acc + cw_ref[j:j + 1, cols] * pltpu.roll(dext, ext_rows - (3 - j), 0)
            dx_ref[:, cols] = acc[:TR].astype(dx_ref.dtype)
            xext = jnp.concatenate([jnp.where(keep_prev, h_ref[:, cols], 0.0), x_ref[:, cols]], axis=0)
            for j in range(4):
                xs = xext if j == 3 else pltpu.roll(xext, 3 - j, 0)
                dcw_ref[j:j + 1, cols] += jnp.sum(xs[HALO:] * dct, axis=0, keepdims=True)

    return pl.pallas_call(
        body, grid=(nt,),
        in_specs=[_row(3 * D, 0), _halo_prev(3 * D, 0), pl.BlockSpec((4, 3 * D), lambda i: (0, 0)),
                  _row(3 * D), _halo_next(3 * D, 0, nt)],
        out_specs=[_row(3 * D), pl.BlockSpec((4, 3 * D), lambda i: (0, 0))],
        out_shape=[SDS((t, 3 * D), BF16), SDS((4, 3 * D), F32)],
        compiler_params=_params(("arbitrary",)), name=name,
    )(proj, proj, conv_w, dconv, dconv)


def _dn_post_fwd(o, proj, gn, name):
    t = o.shape[0]

    def body(o_ref, z_ref, g_ref, out_ref):
        gv = g_ref[...]
        for h in range(NH):
            hc = slice(h * HD, (h + 1) * HD)
            ov = o_ref[:, hc]
            r = lax.rsqrt(jnp.mean(ov * ov, axis=-1, keepdims=True) + RMS_EPS)
            out_ref[:, hc] = (((ov * r) * gv) * _silu(z_ref[:, hc])).astype(out_ref.dtype)

    return pl.pallas_call(
        body, grid=(t // TR,), in_specs=[_row(D), _row(D, OFF_Z // D), _vec(HD)], out_specs=_row(D),
        out_shape=SDS((t, D), BF16), compiler_params=_params(("parallel",)), name=name,
    )(o, proj, gn)


def _dn_post_bwd(o, proj, gn, dob, name):
    t = o.shape[0]

    def body(o_ref, z_ref, g_ref, d_ref, do_ref, dz_ref, dg_ref):
        @pl.when(pl.program_id(0) == 0)
        def _():
            dg_ref[...] = jnp.zeros_like(dg_ref)

        gv = g_ref[...]
        acc = jnp.zeros((1, HD), F32)
        for h in range(NH):
            hc = slice(h * HD, (h + 1) * HD)
            ov = o_ref[:, hc]
            zv = z_ref[:, hc]
            dv = d_ref[:, hc]
            r = lax.rsqrt(jnp.mean(ov * ov, axis=-1, keepdims=True) + RMS_EPS)
            n = ov * r
            dz_ref[:, hc] = (dv * (n * gv) * _dsilu(zv)).astype(dz_ref.dtype)
            dng = dv * _silu(zv)
            acc = acc + jnp.sum(dng * n, axis=0, keepdims=True)
            dn = dng * gv
            do_ref[:, hc] = r * (dn - n * jnp.mean(dn * n, axis=-1, keepdims=True))
        dg_ref[...] += acc

    return pl.pallas_call(
        body, grid=(t // TR,), in_specs=[_row(D), _row(D, OFF_Z // D), _vec(HD), _row(D)],
        out_specs=[_row(D), _row(D), _vec(HD)],
        out_shape=[SDS((t, D), F32), SDS((t, D), BF16), SDS((1, HD), F32)],
        compiler_params=_params(("arbitrary",)), name=name,
    )(o, proj, gn, dob)


def _merge_fwd(ya, yb, proj, name):
    t = ya.shape[0]

    def body(a_ref, b_ref, gp_ref, gd_ref, o_ref):
        o_ref[...] = (_sigmoid(gp_ref[...]) * a_ref[...] + _sigmoid(gd_ref[...]) * b_ref[...]).astype(o_ref.dtype)

    return pl.pallas_call(
        body, grid=(t // TR,), in_specs=[_row(D), _row(D), _row(D, OFF_GP // D), _row(D, OFF_GD // D)],
        out_specs=_row(D), out_shape=SDS((t, D), BF16),
        compiler_params=_params(("parallel",)), name=name,
    )(ya, yb, proj, proj)


def _merge_bwd(dm, ya, yb, proj, name):
    t = ya.shape[0]

    def body(d_ref, a_ref, b_ref, gp_ref, gd_ref, da_ref, db_ref, dgp_ref, dgd_ref):
        dv = d_ref[...]
        sp = _sigmoid(gp_ref[...])
        sd = _sigmoid(gd_ref[...])
        da_ref[...] = dv * sp
        db_ref[...] = (dv * sd).astype(db_ref.dtype)
        dgp_ref[...] = (dv * a_ref[...] * sp * (1.0 - sp)).astype(dgp_ref.dtype)
        dgd_ref[...] = (dv * b_ref[...] * sd * (1.0 - sd)).astype(dgd_ref.dtype)

    return pl.pallas_call(
        body, grid=(t // TR,),
        in_specs=[_row(D), _row(D), _row(D), _row(D, OFF_GP // D), _row(D, OFF_GD // D)],
        out_specs=[_row(D)] * 4,
        out_shape=[SDS((t, D), F32), SDS((t, D), BF16), SDS((t, D), BF16), SDS((t, D), BF16)],
        compiler_params=_params(("parallel",)), name=name,
    )(dm, ya, yb, proj, proj)


def _split2(x):
    hi = x.astype(BF16)
    return hi, (x - hi.astype(F32)).astype(BF16)


def _dot3(a, b, dims):
    ah, al = _split2(a)
    bh, bl = _split2(b)
    return _dg(ah, bh, dims) + (_dg(ah, bl, dims) + _dg(al, bh, dims))


def _neumann_inverses(mats):
    ri = _iota((CH, CH), 0)
    ci = _iota((CH, CH), 1)
    eye = jnp.where(ri == ci, 1.0, 0.0).astype(F32)
    xs = [-a for a in mats]
    ps = [eye + x for x in xs]
    for _ in range(5):
        xs = [_dot3(x, x, NN) for x in xs]
        ps = [p + _dot3(p, x, NN) for p, x in zip(ps, xs)]
    return ps


def _solve_with(inv):
    @jax.custom_vjp
    def solve(a, rhs):
        return _dot3(inv, rhs, NN)

    def fwd(a, rhs):
        sol = _dot3(inv, rhs, NN)
        return sol, sol

    def bwd(sol, d):
        drhs = _dot3(inv, d, TN)
        return -_dot3(drhs, sol, NT), drhs

    solve.defvjp(fwd, bwd)
    return solve


@jax.custom_vjp
def _rows_to_lanes(g64):
    ri = _iota((CH, CH), 0)
    ci = _iota((CH, CH), 1)
    diag = jnp.where(ri == ci, g64, 0.0)
    ones = jnp.ones((CH, CH), BF16)
    hi = diag.astype(BF16)
    rem = diag - hi.astype(F32)
    mid = rem.astype(BF16)
    lo = (rem - mid.astype(F32)).astype(BF16)
    return _dg(ones, hi, NN) + (_dg(ones, mid, NN) + _dg(ones, lo, NN))


def _rows_to_lanes_bwd(_, d):
    ri = _iota((CH, CH), 0)
    ci = _iota((CH, CH), 1)
    return (jnp.where(ri == ci, jnp.broadcast_to(jnp.sum(d, axis=0, keepdims=True), (CH, CH)), 0.0),)


_rows_to_lanes.defvjp(lambda g64: (_rows_to_lanes(g64), None), _rows_to_lanes_bwd)


def _chunk_local(solve_all, q, k, v, g128, g64, gl128, b128, b64):
    ri = _iota((CH, CH), 0)
    ci = _iota((CH, CH), 1)
    causal = ri >= ci
    strict = ri > ci
    gj = [_rows_to_lanes(g) for g in g64]
    decay = [jnp.where(causal, jnp.exp(jnp.where(causal, g - t, 0.0)), 0.0) for g, t in zip(g64, gj)]
    kk = [_nt(x, x) for x in k]
    a = [jnp.where(strict, b * m * dc, 0.0) for b, m, dc in zip(b64, kk, decay)]
    eg = [jnp.exp(g) for g in g128]
    rhs = [jnp.concatenate([b * x, (b * e) * y], axis=1) for b, x, e, y in zip(b128, v, eg, k)]
    sol = solve_all(a, rhs)
    qk = [jnp.where(causal, _nt(x, y) * dc, 0.0) for x, y, dc in zip(q, k, decay)]
    return ([s[:, :HD] for s in sol], [s[:, HD:] for s in sol], qk, [x * e for x, e in zip(q, eg)],
            [x * jnp.exp(gl - g) for x, gl, g in zip(k, gl128, g128)], [jnp.exp(gl) for gl in gl128])


def _all_head_gates(bgv):
    return tuple(list(z) for z in zip(*[_head_gates(bgv, h) for h in range(NH)]))


def _head_gates(bgv, h):
    lane = _iota((CH, 128), 1)
    row = _iota((CH, 128), 0)
    bcol = jnp.sum(jnp.where(lane == h, bgv, 0.0), axis=1, keepdims=True)
    gcol = jnp.sum(jnp.where(lane == NH + h, bgv, 0.0), axis=1, keepdims=True)
    g128 = jnp.broadcast_to(gcol, (CH, 128))
    gl128 = jnp.broadcast_to(jnp.sum(jnp.where(row == CH - 1, g128, 0.0), axis=0, keepdims=True), (CH, 128))
    return (g128, jnp.broadcast_to(gcol, (CH, CH)), gl128,
            jnp.broadcast_to(bcol, (CH, 128)), jnp.broadcast_to(bcol, (CH, CH)))


def _chunk_specs():
    row = pl.BlockSpec((CH, D), lambda i: (i, 0))
    small = pl.BlockSpec((CH, 128), lambda i: (i, 0))
    qk = pl.BlockSpec((NH, CH, CH), lambda i: (i, 0, 0))
    eg = pl.BlockSpec((1, NH, 128), lambda i: (i, 0, 0))
    return row, small, qk, eg


def _dn_local_fwd(q, k, v, bg, name):
    t = q.shape[0]
    n = t // CH

    def body(q_ref, k_ref, v_ref, bg_ref, u_ref, w_ref, qk_ref, qd_ref, kd_ref, eg_ref, inv_ref):
        cols = [slice(h * HD, (h + 1) * HD) for h in range(NH)]

        def solve_all(mats, rhs):
            invs = _neumann_inverses(mats)
            for h in range(NH):
                inv_ref[h] = invs[h]
            return [_dot3(m, r, NN) for m, r in zip(invs, rhs)]

        u, w, qk, qd, kd, egl = _chunk_local(
            solve_all, [q_ref[:, c] for c in cols], [k_ref[:, c] for c in cols], [v_ref[:, c] for c in cols],
            *_all_head_gates(bg_ref[...]))
        for h, hc in enumerate(cols):
            u_ref[:, hc] = u[h]
            w_ref[:, hc] = w[h].astype(w_ref.dtype)
            qd_ref[:, hc] = qd[h].astype(qd_ref.dtype)
            kd_ref[:, hc] = kd[h].astype(kd_ref.dtype)
            qk_ref[h] = qk[h].astype(qk_ref.dtype)
            eg_ref[0, h:h + 1, :] = egl[h][0:1, :]

    row, small, qkb, egb = _chunk_specs()
    return pl.pallas_call(
        body, grid=(n,), in_specs=[row, row, row, small], out_specs=[row, row, qkb, row, row, egb, qkb],
        out_shape=[SDS((t, D), F32), SDS((t, D), BF16), SDS((n * NH, CH, CH), BF16), SDS((t, D), BF16),
                   SDS((t, D), BF16), SDS((n, NH, 128), F32), SDS((n * NH, CH, CH), F32)],
        compiler_params=_params(("parallel",)), name=name,
    )(q, k, v, bg)


def _dn_local_bwd(q, k, v, bg, inv, du, dw, dqk, dqd, dkd, deg, name):
    t = q.shape[0]
    n = t // CH

    def body(q_ref, k_ref, v_ref, bg_ref, inv_ref, du_ref, dw_ref, dqk_ref, dqd_ref, dkd_ref, deg_ref,
             dq_ref, dk_ref, dv_ref, dbg_ref):
        bgv = bg_ref[...]
        lane = _iota((CH, 128), 1)
        row = _iota((CH, 128), 0)
        first = jnp.where(row == 0, 1.0, 0.0)
        acc = jnp.zeros((CH, 128), F32)
        cols = [slice(h * HD, (h + 1) * HD) for h in range(NH)]
        solves = [_solve_with(inv_ref[h]) for h in range(NH)]

        def solve_all(mats, rhs):
            return [f(m, r) for f, m, r in zip(solves, mats, rhs)]

        _, vjp = jax.vjp(functools.partial(_chunk_local, solve_all),
                         [q_ref[:, c] for c in cols], [k_ref[:, c] for c in cols], [v_ref[:, c] for c in cols],
                         *_all_head_gates(bgv))
        cts = ([du_ref[:, c] for c in cols], [dw_ref[:, c] for c in cols], [dqk_ref[h] for h in range(NH)],
               [dqd_ref[:, c] for c in cols], [dkd_ref[:, c] for c in cols],
               [jnp.broadcast_to(deg_ref[0, h:h + 1, :], (CH, 128)) * first for h in range(NH)])
        dq, dk, dv, dg128, dg64, dgl, db128, db64 = vjp(cts)
        for h, hc in enumerate(cols):
            dq_ref[:, hc] = dq[h]
            dk_ref[:, hc] = dk[h]
            dv_ref[:, hc] = dv[h]
            dg = jnp.sum(dg128[h], axis=1, keepdims=True) + jnp.sum(dg64[h], axis=1, keepdims=True)
            tot = jnp.sum(jnp.sum(dgl[h], axis=0, keepdims=True), axis=1, keepdims=True)
            dg = dg + jnp.where(row[:, 0:1] == CH - 1, tot, 0.0)
            db = jnp.sum(db128[h], axis=1, keepdims=True) + jnp.sum(db64[h], axis=1, keepdims=True)
            acc = acc + jnp.where(lane == h, db, 0.0) + jnp.where(lane == NH + h, dg, 0.0)
        dbg_ref[...] = acc

    row, small, qkb, egb = _chunk_specs()
    return pl.pallas_call(
        body, grid=(n,), in_specs=[row, row, row, small, qkb, row, row, qkb, row, row, egb],
        out_specs=[row, row, row, small],
        out_shape=[SDS((t, D), F32)] * 3 + [SDS((t, 128), F32)],
        compiler_params=_params(("parallel",)), name=name,
    )(q, k, v, bg, inv, du, dw, dqk, dqd, dkd, deg)


def _state_step(s, u, w, qk, qd, kd, egl):
    ws = [_nn(a, b) for a, b in zip(w, s)]
    v_new = [a - b for a, b in zip(u, ws)]
    qs = [_nn(a, b) for a, b in zip(qd, s)]
    intra = [_nn(a, b) for a, b in zip(qk, v_new)]
    upd = [_tn(a, b) for a, b in zip(kd, v_new)]
    return [a * e + b for a, e, b in zip(s, egl, upd)], [a + b for a, b in zip(qs, intra)]


def _dn_scan_fwd(u, w, qk, qd, kd, eg, name):
    t = u.shape[0]
    n = t // CH
    g = SCAN_CHUNKS

    def body(u_ref, w_ref, qk_ref, qd_ref, kd_ref, eg_ref, o_ref, save_ref, s_ref):
        @pl.when(pl.program_id(0) == 0)
        def _():
            s_ref[...] = jnp.zeros_like(s_ref)

        cols = [slice(h * HD, (h + 1) * HD) for h in range(NH)]
        s = [s_ref[h] for h in range(NH)]
        for c in range(g):
            rows = slice(c * CH, (c + 1) * CH)
            for h in range(NH):
                save_ref[c, h] = s[h].astype(save_ref.dtype)
            s, o = _state_step(
                s, [u_ref[rows, hc] for hc in cols], [w_ref[rows, hc].astype(F32) for hc in cols],
                [qk_ref[c * NH + h].astype(F32) for h in range(NH)], [qd_ref[rows, hc].astype(F32) for hc in cols],
                [kd_ref[rows, hc].astype(F32) for hc in cols], [eg_ref[c, h:h + 1, :] for h in range(NH)])
            for h, hc in enumerate(cols):
                o_ref[rows, hc] = o[h]
        for h in range(NH):
            s_ref[h] = s[h]

    row = pl.BlockSpec((g * CH, D), lambda i: (i, 0))
    qkb = pl.BlockSpec((g * NH, CH, CH), lambda i: (i, 0, 0))
    egb = pl.BlockSpec((g, NH, 128), lambda i: (i, 0, 0))
    return pl.pallas_call(
        body, grid=(n // g,), in_specs=[row, row, qkb, row, row, egb],
        out_specs=[row, pl.BlockSpec((g, NH, HD, HD), lambda i: (i, 0, 0, 0))],
        out_shape=[SDS((t, D), F32), SDS((n, NH, HD, HD), BF16)],
        scratch_shapes=[pltpu.VMEM((NH, HD, HD), F32)],
        compiler_params=_params(("arbitrary",)), name=name,
    )(u, w, qk, qd, kd, eg)


def _dn_scan_bwd(u, w, qk, qd, kd, eg, saved, do, name):
    t = u.shape[0]
    n = t // CH
    g = SCAN_CHUNKS
    last = n // g - 1

    def body(u_ref, w_ref, qk_ref, qd_ref, kd_ref, eg_ref, sv_ref, do_ref,
             du_ref, dw_ref, dqk_ref, dqd_ref, dkd_ref, deg_ref, ds_ref):
        @pl.when(pl.program_id(0) == 0)
        def _():
            ds_ref[...] = jnp.zeros_like(ds_ref)

        cols = [slice(h * HD, (h + 1) * HD) for h in range(NH)]
        ds = [ds_ref[h] for h in range(NH)]
        for c in reversed(range(g)):
            rows = slice(c * CH, (c + 1) * CH)
            _, vjp = jax.vjp(
                _state_step, [sv_ref[c, h].astype(F32) for h in range(NH)], [u_ref[rows, hc] for hc in cols],
                [w_ref[rows, hc].astype(F32) for hc in cols], [qk_ref[c * NH + h].astype(F32) for h in range(NH)],
                [qd_ref[rows, hc].astype(F32) for hc in cols], [kd_ref[rows, hc].astype(F32) for hc in cols],
                [eg_ref[c, h:h + 1, :] for h in range(NH)])
            ds, du, dw, dqk, dqd, dkd, deg = vjp((ds, [do_ref[rows, hc] for hc in cols]))
            for h, hc in enumerate(cols):
                du_ref[rows, hc] = du[h]
                dw_ref[rows, hc] = dw[h]
                dqk_ref[c * NH + h] = dqk[h]
                dqd_ref[rows, hc] = dqd[h]
                dkd_ref[rows, hc] = dkd[h]
                deg_ref[c, h:h + 1, :] = deg[h]
        for h in range(NH):
            ds_ref[h] = ds[h]

    row = pl.BlockSpec((g * CH, D), lambda i: (last - i, 0))
    qkb = pl.BlockSpec((g * NH, CH, CH), lambda i: (last - i, 0, 0))
    egb = pl.BlockSpec((g, NH, 128), lambda i: (last - i, 0, 0))
    return pl.pallas_call(
        body, grid=(n // g,),
        in_specs=[row, row, qkb, row, row, egb,
                  pl.BlockSpec((g, NH, HD, HD), lambda i: (last - i, 0, 0, 0)), row],
        out_specs=[row, row, qkb, row, row, egb],
        out_shape=[SDS((t, D), F32), SDS((t, D), F32), SDS((n * NH, CH, CH), F32), SDS((t, D), F32),
                   SDS((t, D), F32), SDS((n, NH, 128), F32)],
        scratch_shapes=[pltpu.VMEM((NH, HD, HD), F32)],
        compiler_params=_params(("arbitrary",)), name=name,
    )(u, w, qk, qd, kd, eg, saved, do)


def _ada_fwd(c_all, ada_w, ada_b, name):
    ncol = ada_w.shape[1]

    def body(c_ref, w_ref, b_ref, o_ref):
        o_ref[...] = _dg(_silu(c_ref[...]), w_ref[...], NN, HI) + b_ref[...]

    return pl.pallas_call(body, out_shape=SDS((NDEV, ncol), F32),
                          compiler_params=pltpu.CompilerParams(vmem_limit_bytes=VMEM_LIMIT), name=name,
                          )(c_all, ada_w, ada_b)


def _ada_bwd(c_all_t, dmod, name):
    ncol = dmod.shape[1]

    def body(c_ref, d_ref, o_ref):
        sc = _silu(c_ref[...])
        acc = sc[:, 0:1] * d_ref[0:1, :]
        for b in range(1, NDEV):
            acc = acc + sc[:, b:b + 1] * d_ref[b:b + 1, :]
        o_ref[...] = acc

    return pl.pallas_call(body, out_shape=SDS((D, ncol), F32),
                          compiler_params=pltpu.CompilerParams(vmem_limit_bytes=VMEM_LIMIT), name=name,
                          )(c_all_t, dmod)


def _sum_devices(parts, out_dtype, name):
    _, r, c = parts.shape
    tr = TR if r % TR == 0 else r

    def body(p_ref, o_ref):
        acc = p_ref[0].astype(F32)
        for i in range(1, NDEV):
            acc = acc + p_ref[i].astype(F32)
        o_ref[...] = acc.astype(o_ref.dtype)

    return pl.pallas_call(
        body, grid=(r // tr,), in_specs=[pl.BlockSpec((NDEV, tr, c), lambda i: (0, i, 0))],
        out_specs=pl.BlockSpec((tr, c), lambda i: (i, 0)), out_shape=SDS((r, c), out_dtype),
        compiler_params=_params(("parallel",)), name=name,
    )(parts)


def _adam_tiles(r, c):
    if r % 8 == 0:
        return _pick(r, (256, 352, 128, 8)), c
    return r, (256 if c % 256 == 0 else c)


def _adam_math(w, gv, m, v):
    m_new = ADAM_B1 * m + (1.0 - ADAM_B1) * gv
    v_new = ADAM_B2 * v + (1.0 - ADAM_B2) * (gv * gv)
    bc1 = 1.0 - ADAM_B1 ** ADAM_STEP
    bc2 = 1.0 - ADAM_B2 ** ADAM_STEP
    return -ADAM_LR * ((m_new / bc1) / (jnp.sqrt(v_new / bc2) + ADAM_EPS) + ADAM_WD * w), m_new, v_new


def _adamw(w, g, m, v, name):
    r, c = w.shape
    tr, tc = _adam_tiles(r, c)

    def body(w_ref, g_ref, m_ref, v_ref, d_ref, nm_ref, nv_ref):
        d_ref[...], nm_ref[...], nv_ref[...] = _adam_math(w_ref[...], g_ref[...], m_ref[...], v_ref[...])

    spec = pl.BlockSpec((tr, tc), lambda i, j: (i, j))
    return pl.pallas_call(
        body, grid=(r // tr, c // tc), in_specs=[spec] * 4, out_specs=[spec] * 3,
        out_shape=[SDS((r, c), F32)] * 3, compiler_params=_params(("parallel", "parallel")), name=name,
    )(w, g, m, v)


def _reduce_adamw(parts, w, m, v, name):
    r, c = w.shape
    tr, tc = _adam_tiles(r, c)

    def body(p_ref, w_ref, m_ref, v_ref, g_ref, d_ref, nm_ref, nv_ref):
        gv = p_ref[0].astype(F32)
        for i in range(1, NDEV):
            gv = gv + p_ref[i].astype(F32)
        g_ref[...] = gv
        d_ref[...], nm_ref[...], nv_ref[...] = _adam_math(w_ref[...], gv, m_ref[...], v_ref[...])

    spec = pl.BlockSpec((tr, tc), lambda i, j: (i, j))
    return pl.pallas_call(
        body, grid=(r // tr, c // tc),
        in_specs=[pl.BlockSpec((NDEV, tr, tc), lambda i, j: (0, i, j))] + [spec] * 3, out_specs=[spec] * 4,
        out_shape=[SDS((r, c), F32)] * 4, compiler_params=_params(("parallel", "parallel")), name=name,
    )(parts, w, m, v)


ANY = pl.BlockSpec(memory_space=pl.ANY)
MESH = pl.DeviceIdType.MESH


def _all_gather(xs, name, after=None):
    n = len(xs)
    extra = [] if after is None else [after]

    def body(*refs):
        x_refs, out_refs = refs[:n], refs[n + len(extra):2 * n + len(extra)]
        send_sems, recv_sems, local_sems = refs[-3:]
        mx, my, mc = lax.axis_index("x"), lax.axis_index("y"), lax.axis_index("c")
        me, sibling = (mx, my, mc), (mx, my, 1 - mc)
        chips = [(1 - mx, my), (mx, 1 - my), (1 - mx, 1 - my)]

        def rows(a, px, py, pc):
            return out_refs[a].at[4 * px + 2 * py + pc]

        def copy(a, k, block, to, src=None):
            return pltpu.make_async_remote_copy(
                src_ref=rows(a, *block) if src is None else src, dst_ref=rows(a, *block),
                send_sem=send_sems.at[a, k], recv_sem=recv_sems.at[a, k], device_id=to, device_id_type=MESH)

        mine = [pltpu.make_async_copy(x_refs[a], rows(a, *me), local_sems.at[a]) for a in range(n)]
        for cp in mine:
            cp.start()
        first = []
        for a in range(n):
            first.append(copy(a, 0, me, sibling, src=x_refs[a]))
            first += [copy(a, 1 + j, me, (*chip, mc), src=x_refs[a]) for j, chip in enumerate(chips)]
        for cp in first:
            cp.start()
        passed = []
        for a in range(n):
            for j, chip in enumerate(chips):
                copy(a, 1 + j, (*chip, mc), me).wait_recv()
                passed.append(copy(a, 4 + j, (*chip, mc), sibling))
                passed[-1].start()
        for a in range(n):
            copy(a, 0, sibling, me).wait_recv()
            for j, chip in enumerate(chips):
                copy(a, 4 + j, (*chip, 1 - mc), me).wait_recv()
        for cp in first + passed:
            cp.wait_send()
        for cp in mine:
            cp.wait()

    return pl.pallas_call(
        body, out_shape=[SDS((NDEV,) + x.shape, x.dtype) for x in xs], in_specs=[ANY] * (n + len(extra)),
        out_specs=[ANY] * n,
        scratch_shapes=[pltpu.SemaphoreType.DMA((n, 7)), pltpu.SemaphoreType.DMA((n, 7)),
                        pltpu.SemaphoreType.DMA((n,))],
        name=name,
    )(*xs, *extra)


HBM = pl.BlockSpec(memory_space=pltpu.HBM)
SEM = pl.BlockSpec(memory_space=pltpu.SEMAPHORE)
EFFECT = pltpu.SideEffectType.DATAFLOW_SIDE_EFFECTING


def _peers():
    mx, my, mc = lax.axis_index("x"), lax.axis_index("y"), lax.axis_index("c")
    out = []
    for k in range(1, NDEV):
        out.append((1 - mx if k & 4 else mx, 1 - my if k & 2 else my, 1 - mc if k & 1 else mc))
    return 4 * mx + 2 * my + mc, out


def _push_start(srcs, sliced, name, after=None):
    n = len(srcs)
    extra = [] if after is None else [after]
    me_idx = 4 * lax.axis_index("x") + 2 * lax.axis_index("y") + lax.axis_index("c")
    lands = []
    for s in srcs:
        blk = lax.dynamic_index_in_dim(s, me_idx, 0, keepdims=True) if sliced else s[None]
        shape = s.shape if sliced else (NDEV,) + s.shape
        lands.append(lax.dynamic_update_slice(lax.empty(shape, s.dtype), blk, (me_idx,) + (0,) * (len(shape) - 1)))

    def body(*refs):
        src_refs, land_refs = refs[:n], refs[n:2 * n]
        outs = refs[2 * n + len(extra):]
        send_sems, recv_sems = outs[:n], outs[n:2 * n]
        token = refs[-1]
        me, peers = _peers()
        for a in range(n):
            for k, (px, py, pc) in enumerate(peers):
                src = src_refs[a].at[4 * px + 2 * py + pc] if sliced else src_refs[a]
                pltpu.make_async_remote_copy(
                    src_ref=src, dst_ref=land_refs[a].at[me], send_sem=send_sems[a].at[k],
                    recv_sem=recv_sems[a].at[k], device_id=(px, py, pc), device_id_type=MESH).start()
        token[...] = jnp.zeros_like(token)

    outs = pl.pallas_call(
        body, name=name,
        out_shape=([pltpu.SemaphoreType.DMA((NDEV - 1,))] * (2 * n)
                   + [pltpu.HBM(s.shape, s.dtype) for s in srcs] + [pltpu.HBM(l.shape, l.dtype) for l in lands]
                   + [SDS((8, 128), F32)]),
        in_specs=[HBM] * (2 * n) + [pl.BlockSpec(memory_space=pl.ANY)] * len(extra),
        out_specs=[SEM] * (2 * n) + [HBM] * (2 * n) + [pl.BlockSpec(memory_space=pltpu.VMEM)],
        input_output_aliases={i: 2 * n + i for i in range(2 * n)},
        compiler_params=pltpu.CompilerParams(has_side_effects=EFFECT),
    )(*[pltpu.with_memory_space_constraint(s, pltpu.HBM) for s in srcs],
      *[pltpu.with_memory_space_constraint(l, pltpu.HBM) for l in lands], *extra)
    sends, recvs = outs[:n], outs[n:2 * n]
    src_thru, land_thru = outs[2 * n:3 * n], outs[3 * n:4 * n]
    return [(sends[a], recvs[a], src_thru[a], land_thru[a]) for a in range(n)], outs[-1]


def _push_wait(started, sliced, after, name):
    n = len(started)

    def body(*refs):
        src_refs, land_refs = refs[:n], refs[n:2 * n]
        send_sems, recv_sems = refs[2 * n:3 * n], refs[3 * n:4 * n]
        me, peers = _peers()
        for a in range(n):
            for k, (px, py, pc) in enumerate(peers):
                src = src_refs[a].at[4 * px + 2 * py + pc] if sliced else src_refs[a]
                cp = pltpu.make_async_remote_copy(
                    src_ref=src, dst_ref=land_refs[a].at[me], send_sem=send_sems[a].at[k],
                    recv_sem=recv_sems[a].at[k], device_id=(px, py, pc), device_id_type=MESH)
                cp.wait_send()
                cp.wait_recv()

    srcs = [s[2] for s in started]
    lands = [s[3] for s in started]
    outs = pl.pallas_call(
        body, name=name,
        out_shape=[pltpu.HBM(s.shape, s.dtype) for s in srcs] + [pltpu.HBM(l.shape, l.dtype) for l in lands],
        in_specs=[HBM] * (2 * n) + [SEM] * (2 * n) + [pl.BlockSpec(memory_space=pl.ANY)],
        out_specs=[HBM] * (2 * n),
        input_output_aliases={i: i for i in range(2 * n)},
        compiler_params=pltpu.CompilerParams(has_side_effects=EFFECT),
    )(*srcs, *lands, *[s[0] for s in started], *[s[1] for s in started], after)
    return outs[n:]


def _cols_from_blocks(blocks):
    _, rows, w = blocks.shape
    return blocks.transpose(1, 0, 2).reshape(rows, NDEV * w)


def _cols_to_blocks(full):
    rows, total = full.shape
    return full.reshape(rows, NDEV, total // NDEV).transpose(1, 0, 2)


def _mix_pad(wt):
    xp, q, k, v, z, ba, gp, gd = jnp.split(wt, (512, 1536, 2560, 3584, 4608, 4624, 5648), axis=0)
    pad = jnp.zeros((MIXP - OFF_BA - 16, wt.shape[1]), wt.dtype)
    return jnp.concatenate([q, k, v, z, gp, gd, xp, ba, pad], axis=0)


def _mix_unpad(wt):
    q, k, v, z, gp, gd, xp, ba = (wt[OFF_Q:OFF_K], wt[OFF_K:OFF_V], wt[OFF_V:OFF_Z], wt[OFF_Z:OFF_GP],
                                  wt[OFF_GP:OFF_GD], wt[OFF_GD:OFF_XP], wt[OFF_XP:OFF_BA], wt[OFF_BA:OFF_BA + 16])
    return jnp.concatenate([xp, q, k, v, z, ba, gp, gd], axis=0)


def _lane_row(vec8):
    return jnp.zeros((1, 128), F32).at[0, NH:2 * NH].set(vec8)


def _ffn_fwd(x, g, shift, scale, gate, w_in, w_out, tag, token=None, start_more=None):
    h = _norm_mod_fwd(x, g, shift, scale, f"{tag}_norm")
    if isinstance(w_in, tuple):
        w_in, = _push_wait([w_in], False, h, f"{tag}_gather_wait_in")
    w_in = w_in.reshape(2 * FH, D)
    u, a = _swiglu_up(h, w_in, f"{tag}_up", after=token)
    w_out, = _push_wait([w_out], False, a, f"{tag}_gather_wait_out")
    w_out = w_out.reshape(FH, D)
    y = _matmul(a, w_out, a_blk=True, out_dtype=F32, name=f"{tag}_down",
                after=None if start_more is None else start_more(h))
    return _resid_fwd(x, y, gate, 0.5, f"{tag}_res"), (h, u, a, y), w_in, w_out


def _ffn_bwd(dx_out, x, g, scale, gate, w_in, w_out, saved, tag):
    h, u, a, y = saved
    t = x.shape[0]
    dy, dgate = _resid_bwd(dx_out, y, gate, 0.5, f"{tag}_res_bwd")
    dw_out = _matmul(a, dy, ta=True, a_blk=True, out_dtype=BF16, name=f"{tag}_down_dw")
    sent_out, token = _push_start([dw_out.reshape(NDEV, FH // NDEV, D)], True, f"{tag}_grad_start_out")
    du = _swiglu_down_bwd(dy, w_out, u, f"{tag}_down_dx", after=token).reshape(NDEV, t, FB)
    dw_in = _matmul(du, h, ta=True, a_blk=True, out_dtype=BF16, name=f"{tag}_up_dw")
    sent_in, token = _push_start([dw_in.reshape(NDEV, FB, D)], True, f"{tag}_grad_start_in")
    dh = _matmul(du, w_in, a_blk=True, out_dtype=F32, name=f"{tag}_up_dx", after=token)
    dx, dshift, dscale, dg = _norm_mod_bwd(x, g, scale, dh, dx_out, f"{tag}_norm_bwd")
    return dx, (dshift, dscale, dgate), dg, sent_in + sent_out


def kernel(x, c, ada_w, ada_b, norm_g, ffn1_w_in, ffn1_w_out, ffn2_w_in, ffn2_w_out, mix_w_in, conv_w, a_log, dt_bias, dn_norm_g, pool_w, pool_scale, pool_proj, dn_proj, mix_w_out, final_g, loss_target, m_ada_w, m_ada_b, m_norm_g, m_ffn1_w_in, m_ffn1_w_out, m_ffn2_w_in, m_ffn2_w_out, m_mix_w_in, m_conv_w, m_a_log, m_dt_bias, m_dn_norm_g, m_pool_w, m_pool_scale, m_pool_proj, m_dn_proj, m_mix_w_out, m_final_g, v_ada_w, v_ada_b, v_norm_g, v_ffn1_w_in, v_ffn1_w_out, v_ffn2_w_in, v_ffn2_w_out, v_mix_w_in, v_conv_w, v_a_log, v_dt_bias, v_dn_norm_g, v_pool_w, v_pool_scale, v_pool_proj, v_dn_proj, v_mix_w_out, v_final_g):
    me = 4 * lax.axis_index("x") + 2 * lax.axis_index("y") + lax.axis_index("c")
    x0 = x[0]
    target = loss_target[0]
    t = x0.shape[0]

    big = [ffn1_w_in[0], ffn1_w_out[0], ffn2_w_in[0], ffn2_w_out[0], mix_w_in[0], pool_proj[0], dn_proj[0],
           mix_w_out[0]]
    small = jnp.concatenate([c.reshape(8, 128), conv_w[0].reshape(12, 128), norm_g[0].reshape(3, 128),
                             jnp.zeros((1, 128), F32)], axis=0)
    small_all, = _all_gather([small], "gather_small")
    c_all = small_all[:, 0:8, :].reshape(NDEV, D)
    conv_full = small_all[:, 8:20, :].reshape(NDEV, 4, 384).transpose(1, 0, 2).reshape(4, 3 * D)
    norm_full = small_all[:, 20:23, :].reshape(NDEV, 3, 128).transpose(1, 0, 2).reshape(3, D)

    ncol = ada_w.shape[2]
    ada_b_mine = lax.dynamic_slice(ada_b, (0, me * ncol), (1, ncol))
    mod_cols = _ada_fwd(c_all, ada_w[0], ada_b_mine, "ada_fwd")
    transposed = (0, 2, 4)
    payload = [(w.T if i in transposed else w).astype(BF16) for i, w in enumerate(big)]
    mod_all, w_in1 = _all_gather([mod_cols, payload[0]], "gather_mod_first_weight")
    started, token = _push_start([payload[1], payload[4]], False, "gather_start", after=mod_all)
    started = {1: started[0], 4: started[1]}

    def start_rest(h):
        more, token = _push_start([payload[i] for i in (5, 6, 7, 2, 3)], False, "gather_start_rest", after=h)
        started.update(zip((5, 6, 7, 2, 3), more))
        return token

    mod = lax.dynamic_index_in_dim(mod_all, me, axis=1, keepdims=False).reshape(9, D)
    shift = [mod[3 * s:3 * s + 1] for s in range(3)]
    scale = [mod[3 * s + 1:3 * s + 2] for s in range(3)]
    gate = [mod[3 * s + 2:3 * s + 3] for s in range(3)]
    ng = [norm_full[s:s + 1] for s in range(3)]
    fg = final_g.reshape(1, D)
    al_row = _lane_row(a_log[0])
    dt_row = _lane_row(dt_bias[0])
    gn = dn_norm_g
    pw = pool_w[0]
    ps = pool_scale

    x1, saved1, w_in1, w_out1 = _ffn_fwd(x0, ng[0], shift[0], scale[0], gate[0], w_in1, started[1], "ffn1", token,
                                         start_rest)

    h1 = _norm_mod_fwd(x1, ng[1], shift[1], scale[1], "mix_norm")
    seg, = _push_wait([started[4]], False, h1, "mix_gather_wait")
    w_mix = _mix_pad(seg.reshape(MIX_RAW, D))
    proj = _matmul(h1, w_mix, tb=True, out_dtype=F32, name="mix_in")
    qh, kh, vh, bg = _dn_pre_fwd(proj, conv_full, al_row, dt_row, "dn_pre")
    seg = _push_wait([started[i] for i in (5, 6, 7)], False, qh, "mix_gather_wait_rest")
    w_pp = _cols_from_blocks(seg[0])
    w_dn = seg[1].reshape(D, D)
    w_mo = seg[2].reshape(D, D)
    ya = _pool_fwd(proj, pw, ps, w_pp, "pool_fwd")
    u, w, qk, qd, kd, eg, inv = _dn_local_fwd(qh, kh, vh, bg, "dn_local")
    o, s_saved = _dn_scan_fwd(u, w, qk, qd, kd, eg, "dn_scan")
    ob = _dn_post_fwd(o, proj, gn, "dn_post")
    yb = _matmul(ob, w_dn, out_dtype=F32, name="dn_out")
    merged = _merge_fwd(ya, yb, proj, "merge")
    mix_y = _matmul(merged, w_mo, out_dtype=F32, name="mix_out")
    x2 = _resid_fwd(x1, mix_y, gate[1], 1.0, "mix_res")

    x3, saved2, w_in2, w_out2 = _ffn_fwd(x2, ng[2], shift[2], scale[2], gate[2], started[2], started[3], "ffn2")
    loss_row, dx3, dfg = _final_loss(x3, fg, target, "loss")

    dx2, dmod2, dng2, sent2 = _ffn_bwd(dx3, x2, ng[2], scale[2], gate[2], w_in2, w_out2, saved2, "ffn2")

    dmy, dgate1 = _resid_bwd(dx2, mix_y, gate[1], 1.0, "mix_res_bwd")
    dmerged = _matmul(dmy, w_mo, tb=True, out_dtype=F32, name="mix_out_dx")
    dw_mo = _matmul(merged, dmy, ta=True, out_dtype=BF16, name="mix_out_dw")
    dya, dyb, dgp, dgd = _merge_bwd(dmerged, ya, yb, proj, "merge_bwd")
    dob = _matmul(dyb, w_dn, tb=True, out_dtype=F32, name="dn_out_dx")
    dw_dn = _matmul(ob, dyb, ta=True, out_dtype=BF16, name="dn_out_dw")
    do, dz, dgn = _dn_post_bwd(o, proj, gn, dob, "dn_post_bwd")
    du, dw, dqk, dqd, dkd, deg = _dn_scan_bwd(u, w, qk, qd, kd, eg, s_saved, do, "dn_scan_bwd")
    dqh, dkh, dvh, dbg = _dn_local_bwd(qh, kh, vh, bg, inv, du, dw, dqk, dqd, dkd, deg, "dn_local_bwd")
    dconv, draw, dal, ddt = _dn_pre_bwd_act(proj, conv_full, al_row, dt_row, dqh, dkh, dvh, dbg, "dn_pre_bwd_act")
    dqkv, dcw = _dn_pre_bwd_conv(proj, conv_full, dconv, "dn_pre_bwd_conv")
    dwin, dpl, dpw, dps, dpp = _pool_bwd_local(proj, pw, ps, w_pp, dya, "pool_bwd_local")
    dxp = _pool_bwd_window(dwin, dpl, "pool_bwd_window")
    dproj = jnp.concatenate([dqkv, dz, dgp, dgd, dxp, draw, jnp.zeros((t, MIXP - OFF_BA - 128), BF16)], axis=1)
    dw_mix = _matmul(dproj, h1, ta=True, out_dtype=BF16, name="mix_in_dw")
    sent1, token = _push_start(
        [_mix_unpad(dw_mix).reshape(NDEV, MIX_RAW // NDEV, D), _cols_to_blocks(dpp.astype(BF16)),
         dw_dn.reshape(NDEV, -1, D), dw_mo.reshape(NDEV, -1, D)], True, "mix_grad_start")
    dh1 = _matmul(dproj, w_mix, out_dtype=F32, name="mix_in_dx", after=token)
    dx1, dsh1, dsc1, dng1 = _norm_mod_bwd(x1, ng[1], scale[1], dh1, dx2, "mix_norm_bwd")

    dx0, dmod0, dng0, sent0 = _ffn_bwd(dx1, x0, ng[0], scale[0], gate[0], w_in1, w_out1, saved1, "ffn1")

    dmod = jnp.concatenate([*dmod0, dsh1, dsc1, dgate1, *dmod2], axis=1).reshape(-1)
    flat = jnp.concatenate([
        dmod, dal[0, NH:2 * NH], ddt[0, NH:2 * NH], dgn.reshape(-1), dps.reshape(-1), dfg.reshape(-1),
        dpw.reshape(-1), jnp.concatenate([dng0, dng1, dng2], axis=0).reshape(-1), dcw.reshape(-1)])
    nflat = 90 * D
    flat = jnp.concatenate([flat, jnp.zeros((nflat - flat.shape[0],), F32)]).reshape(90, D)
    flat_all, = _all_gather([flat], "gather_small_grads")
    tot = _sum_devices(flat_all, F32, "sum_small_grads").reshape(-1)
    dmod_all = flat_all.reshape(NDEV, nflat)[:, :9 * D]
    dmod_cols = lax.dynamic_slice(dmod_all, (0, me * ncol), (NDEV, ncol))
    g_ada_w = _ada_bwd(c_all.T, dmod_cols, "ada_bwd")

    p = 0
    pieces = {}
    for nm, size in (("ada_b", 9 * D), ("a_log", NH), ("dt_bias", NH), ("dn_norm_g", HD), ("pool_scale", PW),
                     ("final_g", D), ("pool_w", 4 * PG * PG), ("norm_g", 3 * D), ("conv_w", 12 * D)):
        pieces[nm] = tot[p:p + size]
        p += size
    g_norm = lax.dynamic_slice(pieces["norm_g"].reshape(3, D), (0, me * 128), (3, 128))
    g_conv = lax.dynamic_slice(pieces["conv_w"].reshape(4, 3 * D), (0, me * 384), (4, 384))

    grads = {
        "ada_w": g_ada_w.reshape(ada_w.shape), "ada_b": pieces["ada_b"].reshape(ada_b.shape),
        "norm_g": g_norm.reshape(norm_g.shape), "conv_w": g_conv.reshape(conv_w.shape),
        "a_log": pieces["a_log"].reshape(a_log.shape), "dt_bias": pieces["dt_bias"].reshape(dt_bias.shape),
        "dn_norm_g": pieces["dn_norm_g"].reshape(dn_norm_g.shape), "pool_w": pieces["pool_w"].reshape(pool_w.shape),
        "pool_scale": pieces["pool_scale"].reshape(pool_scale.shape),
        "final_g": pieces["final_g"].reshape(final_g.shape),
    }
    weights = {"ada_w": ada_w, "ada_b": ada_b, "norm_g": norm_g, "ffn1_w_in": ffn1_w_in, "ffn1_w_out": ffn1_w_out,
               "ffn2_w_in": ffn2_w_in, "ffn2_w_out": ffn2_w_out, "mix_w_in": mix_w_in, "conv_w": conv_w,
               "a_log": a_log, "dt_bias": dt_bias, "dn_norm_g": dn_norm_g, "pool_w": pool_w,
               "pool_scale": pool_scale, "pool_proj": pool_proj, "dn_proj": dn_proj, "mix_w_out": mix_w_out,
               "final_g": final_g}
    m_in = {"ada_w": m_ada_w, "ada_b": m_ada_b, "norm_g": m_norm_g, "ffn1_w_in": m_ffn1_w_in,
            "ffn1_w_out": m_ffn1_w_out, "ffn2_w_in": m_ffn2_w_in, "ffn2_w_out": m_ffn2_w_out,
            "mix_w_in": m_mix_w_in, "conv_w": m_conv_w, "a_log": m_a_log, "dt_bias": m_dt_bias,
            "dn_norm_g": m_dn_norm_g, "pool_w": m_pool_w, "pool_scale": m_pool_scale, "pool_proj": m_pool_proj,
            "dn_proj": m_dn_proj, "mix_w_out": m_mix_w_out, "final_g": m_final_g}
    v_in = {"ada_w": v_ada_w, "ada_b": v_ada_b, "norm_g": v_norm_g, "ffn1_w_in": v_ffn1_w_in,
            "ffn1_w_out": v_ffn1_w_out, "ffn2_w_in": v_ffn2_w_in, "ffn2_w_out": v_ffn2_w_out,
            "mix_w_in": v_mix_w_in, "conv_w": v_conv_w, "a_log": v_a_log, "dt_bias": v_dt_bias,
            "dn_norm_g": v_dn_norm_g, "pool_w": v_pool_w, "pool_scale": v_pool_scale, "pool_proj": v_pool_proj,
            "dn_proj": v_dn_proj, "mix_w_out": v_mix_w_out, "final_g": v_final_g}

    names = list(weights)
    large = ("ada_w", "ffn1_w_in", "ffn1_w_out", "ffn2_w_in", "ffn2_w_out", "mix_w_in", "pool_proj", "dn_proj",
             "mix_w_out")
    delta, new_m, new_v = {}, {}, {}

    flipped = ("ffn1_w_in", "ffn2_w_in", "mix_w_in")

    def views(nm):
        shp = weights[nm].shape
        two_d = (shp[-2], shp[-1])
        if nm in flipped:
            return (lambda a: a.reshape(two_d).T), (lambda a: a.T.reshape(shp))
        return (lambda a: a.reshape(two_d)), (lambda a: a.reshape(shp))

    def reduce_update(sent, group, after, tag):
        for nm, r in zip(group, _push_wait(sent, True, after, f"{tag}_grad_wait")):
            view, back = views(nm)
            g_, d_, m_, v_ = _reduce_adamw(r, view(weights[nm]), view(m_in[nm]), view(v_in[nm]), f"adamw_{nm}")
            grads[nm], delta[nm], new_m[nm], new_v[nm] = back(g_), back(d_), back(m_), back(v_)
        return d_

    view, back = views("ada_w")
    done, m_, v_ = _adamw(view(ada_w), view(grads["ada_w"]), view(m_ada_w), view(v_ada_w), "adamw_ada_w")
    delta["ada_w"], new_m["ada_w"], new_v["ada_w"] = back(done), back(m_), back(v_)
    done = reduce_update(sent2, ("ffn2_w_in", "ffn2_w_out"), done, "ffn2")
    done = reduce_update(sent1, ("mix_w_in", "pool_proj", "dn_proj", "mix_w_out"), done, "mix")
    reduce_update(sent0, ("ffn1_w_in", "ffn1_w_out"), done, "ffn1")
    rest = [nm for nm in names if nm not in large]
    total = sum(weights[nm].size for nm in rest)
    padded = -(-total // D) * D

    def pack(tree, fill):
        flat_ = jnp.concatenate([tree[nm].reshape(-1) for nm in rest])
        return jnp.concatenate([flat_, jnp.full((padded - total,), fill, F32)]).reshape(-1, D)

    d_, m_, v_ = _adamw(pack(weights, 0.0), pack(grads, 0.0), pack(m_in, 0.0), pack(v_in, 1.0), "adamw_small")
    p = 0
    for nm in rest:
        size = weights[nm].size
        shp = weights[nm].shape
        delta[nm] = d_.reshape(-1)[p:p + size].reshape(shp)
        new_m[nm] = m_.reshape(-1)[p:p + size].reshape(shp)
        new_v[nm] = v_.reshape(-1)[p:p + size].reshape(shp)
        p += size

    loss = lax.psum(loss_row[0, 0], ("x", "y", "c"))
    grad_x = dx0.reshape(x.shape)
    return (loss, grad_x, *[grads[nm] for nm in names], *[delta[nm] for nm in names],
            *[new_m[nm] for nm in names], *[new_v[nm] for nm in names])
```

```python
import functools

import jax
import jax.numpy as jnp
from jax import lax
from jax.experimental import pallas as pl
from jax.experimental.pallas import tpu as pltpu

F32 = jnp.float32
BF16 = jnp.bfloat16
SDS = jax.ShapeDtypeStruct
HI = lax.Precision.HIGHEST

D = 1024
FH = 2816
FB = 704
NH = 8
HD = 128
CH = 64
SCAN_CHUNKS = 2
NDEV = 8
PW = 512
PG = 128
RMS_EPS = 1e-6
L2_EPS = 1e-6
TR = 512
HALO = 16
VMEM_LIMIT = 56 * 1024 * 1024

MIXP = 6912
OFF_Q, OFF_K, OFF_V, OFF_Z, OFF_GP, OFF_GD, OFF_XP, OFF_BA = 0, 1024, 2048, 3072, 4096, 5120, 6144, 6656
MIX_RAW = 6672

ADAM_LR = 0.001
ADAM_B1 = 0.9
ADAM_B2 = 0.999
ADAM_EPS = 1e-08
ADAM_WD = 0.01
ADAM_STEP = 10

NN = (((1,), (0,)), ((), ()))
NT = (((1,), (1,)), ((), ()))
TN = (((0,), (0,)), ((), ()))


def _dg(a, b, dims, prec=None):
    return lax.dot_general(a, b, dims, precision=prec, preferred_element_type=F32)


def _make_dots(prec):
    @jax.custom_vjp
    def nn(a, b):
        return _dg(a, b, NN, prec)

    @jax.custom_vjp
    def nt(a, b):
        return _dg(a, b, NT, prec)

    @jax.custom_vjp
    def tn(a, b):
        return _dg(a, b, TN, prec)

    nn.defvjp(lambda a, b: (nn(a, b), (a, b)), lambda r, d: (nt(d, r[1]), tn(r[0], d)))
    nt.defvjp(lambda a, b: (nt(a, b), (a, b)), lambda r, d: (nn(d, r[1]), tn(d, r[0])))
    tn.defvjp(lambda a, b: (tn(a, b), (a, b)), lambda r, d: (nt(r[1], d), nn(r[0], d)))
    return nn, nt, tn


_nn, _nt, _tn = _make_dots(None)


def _params(sem):
    return pltpu.CompilerParams(dimension_semantics=sem, vmem_limit_bytes=VMEM_LIMIT)


def _sigmoid(x):
    return 1.0 / (1.0 + jnp.exp(-x))


def _silu(x):
    return x * _sigmoid(x)


def _dsilu(x):
    s = _sigmoid(x)
    return s * (1.0 + x * (1.0 - s))


def _pick(n, cands):
    for c in cands:
        if n % c == 0:
            return c
    raise ValueError(f"no tile for {n}")


def _iota(shape, dim):
    return lax.broadcasted_iota(jnp.int32, shape, dim)


def _matmul(a, b, *, ta=False, tb=False, a_blk=False, b_blk=False, o_blk=False, tm=None, tn=None, tk=None,
            out_dtype, name, after=None):
    if a_blk:
        nb, r, cb = a.shape
        if ta:
            k_dim, m_dim, tm = r, nb * cb, cb
        else:
            m_dim, k_dim, tk = r, nb * cb, cb
    else:
        k_dim, m_dim = a.shape if ta else a.shape[::-1]
    if b_blk:
        nb, r, cb = b.shape
        if tb:
            n_dim, tk = r, cb
            assert nb * cb == k_dim
        else:
            n_dim, tn = nb * cb, cb
            assert r == k_dim
    else:
        n_dim = b.shape[0] if tb else b.shape[1]
    tm = tm or _pick(m_dim, (1024, 768, 512, 256, 128))
    tn = tn or _pick(n_dim, (1024, 768, 512, 256, 128))
    tk = tk or (k_dim if (k_dim <= 2816 and not ta) else _pick(k_dim, (2816, 2304, 1024, 512, 256)))
    nk = k_dim // tk
    dims = ((((0,) if ta else (1,)), ((1,) if tb else (0,))), ((), ()))

    def body(a_ref, b_ref, *rest):
        o_ref, acc_ref = rest[-2:]
        k = pl.program_id(2)

        @pl.when(k == 0)
        def _():
            acc_ref[...] = jnp.zeros_like(acc_ref)

        acc_ref[...] += lax.dot_general(a_ref[...].astype(BF16), b_ref[...].astype(BF16), dims,
                                        preferred_element_type=F32)

        @pl.when(k == nk - 1)
        def _():
            o_ref[...] = acc_ref[...].astype(o_ref.dtype)

    if a_blk:
        a_spec = (pl.BlockSpec((None, tk, tm), lambda i, j, k: (i, k, 0)) if ta
                  else pl.BlockSpec((None, tm, tk), lambda i, j, k: (k, i, 0)))
    else:
        a_spec = (pl.BlockSpec((tk, tm), lambda i, j, k: (k, i)) if ta
                  else pl.BlockSpec((tm, tk), lambda i, j, k: (i, k)))
    if b_blk:
        b_spec = (pl.BlockSpec((None, tn, tk), lambda i, j, k: (k, j, 0)) if tb
                  else pl.BlockSpec((None, tk, tn), lambda i, j, k: (j, k, 0)))
    else:
        b_spec = (pl.BlockSpec((tn, tk), lambda i, j, k: (j, k)) if tb
                  else pl.BlockSpec((tk, tn), lambda i, j, k: (k, j)))
    if o_blk:
        o_spec = pl.BlockSpec((None, tm, tn), lambda i, j, k: (j, i, 0))
        o_shape = SDS((n_dim // tn, m_dim, tn), out_dtype)
    else:
        o_spec = pl.BlockSpec((tm, tn), lambda i, j, k: (i, j))
        o_shape = SDS((m_dim, n_dim), out_dtype)
    return pl.pallas_call(
        body, grid=(m_dim // tm, n_dim // tn, nk),
        in_specs=[a_spec, b_spec] + ([] if after is None else [pl.BlockSpec(memory_space=pl.ANY)]),
        out_specs=o_spec,
        out_shape=o_shape,
        scratch_shapes=[pltpu.VMEM((tm, tn), F32)],
        compiler_params=_params(("parallel", "parallel", "arbitrary")),
        name=name,
    )(a, b, *([] if after is None else [after]))


def _row(width, col=0):
    return pl.BlockSpec((TR, width), lambda i: (i, col))


def _vec(width):
    return pl.BlockSpec((1, width), lambda i: (0, 0))


def _norm_mod_fwd(x, g, shift, scale, name):
    t = x.shape[0]

    def body(x_ref, g_ref, sh_ref, sc_ref, o_ref):
        xv = x_ref[...]
        r = lax.rsqrt(jnp.mean(xv * xv, axis=-1, keepdims=True) + RMS_EPS)
        o_ref[...] = (((xv * r) * g_ref[...]) * (1.0 + sc_ref[...]) + sh_ref[...]).astype(o_ref.dtype)

    return pl.pallas_call(
        body, grid=(t // TR,), in_specs=[_row(D), _vec(D), _vec(D), _vec(D)], out_specs=_row(D),
        out_shape=SDS((t, D), BF16), compiler_params=_params(("parallel",)), name=name,
    )(x, g, shift, scale)


def _norm_mod_bwd(x, g, scale, dh, dx_in, name):
    t = x.shape[0]

    def body(x_ref, g_ref, sc_ref, dh_ref, dxi_ref, dx_ref, dsh_ref, dsc_ref, dg_ref):
        @pl.when(pl.program_id(0) == 0)
        def _():
            dsh_ref[...] = jnp.zeros_like(dsh_ref)
            dsc_ref[...] = jnp.zeros_like(dsc_ref)
            dg_ref[...] = jnp.zeros_like(dg_ref)

        xv = x_ref[...]
        gv = g_ref[...]
        dh = dh_ref[...]
        r = lax.rsqrt(jnp.mean(xv * xv, axis=-1, keepdims=True) + RMS_EPS)
        n = xv * r
        dsh_ref[...] += jnp.sum(dh, axis=0, keepdims=True)
        dsc_ref[...] += jnp.sum(dh * (n * gv), axis=0, keepdims=True)
        tt = dh * (1.0 + sc_ref[...])
        dg_ref[...] += jnp.sum(tt * n, axis=0, keepdims=True)
        dn = tt * gv
        dx_ref[...] = dxi_ref[...] + r * (dn - n * jnp.mean(dn * n, axis=-1, keepdims=True))

    return pl.pallas_call(
        body, grid=(t // TR,), in_specs=[_row(D), _vec(D), _vec(D), _row(D), _row(D)],
        out_specs=[_row(D), _vec(D), _vec(D), _vec(D)],
        out_shape=[SDS((t, D), F32), SDS((1, D), F32), SDS((1, D), F32), SDS((1, D), F32)],
        compiler_params=_params(("arbitrary",)), name=name,
    )(x, g, scale, dh, dx_in)


def _swiglu_up(h, w_in, name, after=None):
    t = h.shape[0]
    tm = _pick(t, (1024, 512, 256))
    half = NDEV // 2
    extra = [] if after is None else [after]

    def body(h_ref, wg_ref, wu_ref, *rest):
        u_ref, a_ref = rest[-2:]
        hv = h_ref[...]
        gate = _dg(hv, wg_ref[...], NT)
        up = _dg(hv, wu_ref[...], NT)
        u_ref[0] = gate.astype(u_ref.dtype)
        u_ref[1] = up.astype(u_ref.dtype)
        a_ref[...] = (_silu(gate) * up).astype(a_ref.dtype)

    return pl.pallas_call(
        body, grid=(t // tm, half),
        in_specs=[pl.BlockSpec((tm, D), lambda i, j: (i, 0)),
                  pl.BlockSpec((FB, D), lambda i, j: (j, 0)),
                  pl.BlockSpec((FB, D), lambda i, j: (j + half, 0))]
        + [pl.BlockSpec(memory_space=pl.ANY)] * len(extra),
        out_specs=[pl.BlockSpec((2, None, tm, FB), lambda i, j: (0, j, i, 0)),
                   pl.BlockSpec((None, tm, FB), lambda i, j: (j, i, 0))],
        out_shape=[SDS((2, half, t, FB), BF16), SDS((half, t, FB), BF16)],
        compiler_params=_params(("parallel", "parallel")), name=name,
    )(h, w_in, w_in, *extra)


def _swiglu_down_bwd(dy, w_out, u, name, after=None):
    t = dy.shape[0]
    tm = _pick(t, (1024, 512, 256))
    half = NDEV // 2
    extra = [] if after is None else [after]
    pair = pl.BlockSpec((2, None, tm, FB), lambda i, j: (0, j, i, 0))

    def body(dy_ref, w_ref, u_ref, *rest):
        o_ref = rest[-1]
        da = _dg(dy_ref[...], w_ref[...], NT)
        gate = u_ref[0].astype(F32)
        o_ref[0] = (da * u_ref[1].astype(F32) * _dsilu(gate)).astype(o_ref.dtype)
        o_ref[1] = (da * _silu(gate)).astype(o_ref.dtype)

    return pl.pallas_call(
        body, grid=(t // tm, half),
        in_specs=[pl.BlockSpec((tm, D), lambda i, j: (i, 0)), pl.BlockSpec((FB, D), lambda i, j: (j, 0)), pair]
        + [pl.BlockSpec(memory_space=pl.ANY)] * len(extra),
        out_specs=pair, out_shape=SDS((2, half, t, FB), BF16),
        compiler_params=_params(("parallel", "parallel")), name=name,
    )(dy, w_out, u, *extra)


def _resid_fwd(x, y, gate, coef, name):
    t = x.shape[0]

    def body(x_ref, y_ref, g_ref, o_ref):
        o_ref[...] = x_ref[...] + (coef * g_ref[...]) * y_ref[...]

    return pl.pallas_call(
        body, grid=(t // TR,), in_specs=[_row(D), _row(D), _vec(D)], out_specs=_row(D),
        out_shape=SDS((t, D), F32), compiler_params=_params(("parallel",)), name=name,
    )(x, y, gate)


def _resid_bwd(dx, y, gate, coef, name):
    t = dx.shape[0]

    def body(dx_ref, y_ref, g_ref, dy_ref, dg_ref):
        @pl.when(pl.program_id(0) == 0)
        def _():
            dg_ref[...] = jnp.zeros_like(dg_ref)

        dxv = dx_ref[...]
        dy_ref[...] = ((coef * g_ref[...]) * dxv).astype(dy_ref.dtype)
        dg_ref[...] += jnp.sum((coef * dxv) * y_ref[...], axis=0, keepdims=True)

    return pl.pallas_call(
        body, grid=(t // TR,), in_specs=[_row(D), _row(D), _vec(D)], out_specs=[_row(D), _vec(D)],
        out_shape=[SDS((t, D), BF16), SDS((1, D), F32)],
        compiler_params=_params(("arbitrary",)), name=name,
    )(dx, y, gate)


def _final_loss(x, fg, target, name):
    t = x.shape[0]
    nt = t // TR

    def body(x_ref, g_ref, t_ref, loss_ref, dx_ref, dg_ref, acc_ref):
        i = pl.program_id(0)

        @pl.when(i == 0)
        def _():
            acc_ref[...] = jnp.zeros_like(acc_ref)
            dg_ref[...] = jnp.zeros_like(dg_ref)

        xv = x_ref[...]
        gv = g_ref[...]
        r = lax.rsqrt(jnp.mean(xv * xv, axis=-1, keepdims=True) + RMS_EPS)
        n = xv * r
        err = n * gv - t_ref[...]
        acc_ref[...] += jnp.sum(err * err, axis=0, keepdims=True)
        dy = err * (1.0 / D)
        dg_ref[...] += jnp.sum(dy * n, axis=0, keepdims=True)
        dn = dy * gv
        dx_ref[...] = r * (dn - n * jnp.mean(dn * n, axis=-1, keepdims=True))

        @pl.when(i == nt - 1)
        def _():
            tot = jnp.sum(acc_ref[...], axis=1, keepdims=True) * (0.5 / D)
            loss_ref[...] = jnp.broadcast_to(tot, loss_ref.shape)

    return pl.pallas_call(
        body, grid=(nt,), in_specs=[_row(D), _vec(D), _row(D)],
        out_specs=[_vec(128), _row(D), _vec(D)],
        out_shape=[SDS((1, 128), F32), SDS((t, D), F32), SDS((1, D), F32)],
        scratch_shapes=[pltpu.VMEM((1, D), F32)],
        compiler_params=_params(("arbitrary",)), name=name,
    )(x, fg, target)


def _halo_prev(width, col):
    per = TR // HALO
    return pl.BlockSpec((HALO, width), lambda i: (jnp.maximum(i * per - 1, 0), col))


def _halo_next(width, col, nt):
    per = TR // HALO
    return pl.BlockSpec((HALO, width), lambda i: (jnp.minimum((i + 1) * per, nt * per - 1), col))


def _pool_windows(ext, tile_index):
    rows = _iota((TR, PG), 0) + tile_index * TR + 1
    pooled, counts = [], []
    for gi in range(4):
        w = 2 << gi
        e = ext[:, gi * PG:(gi + 1) * PG]
        s = e
        step = 1
        while step < w:
            s = s + pltpu.roll(s, step, 0)
            step *= 2
        cnt = jnp.minimum(rows, w).astype(F32)
        pooled.append(s[HALO:] / cnt - e[HALO:])
        counts.append(cnt)
    return pooled, counts


def _pool_fwd(proj, pool_w, pool_scale, pool_proj, name):
    t = proj.shape[0]
    xcol = OFF_XP // PW

    def body(x_ref, h_ref, pw_ref, ps_ref, pp_ref, o_ref):
        i = pl.program_id(0)
        halo = jnp.where(i > 0, h_ref[...], 0.0)
        ext = jnp.concatenate([halo, x_ref[...]], axis=0)
        pooled, _ = _pool_windows(ext, i)
        mixed = [_dg(pooled[g].astype(BF16), pw_ref[g].astype(BF16), NN) for g in range(4)]
        ypre = jnp.concatenate(mixed, axis=1) * ps_ref[...]
        o_ref[...] = _dg(ypre.astype(BF16), pp_ref[...], NN)

    return pl.pallas_call(
        body, grid=(t // TR,),
        in_specs=[_row(PW, xcol), _halo_prev(PW, xcol),
                  pl.BlockSpec((4, PG, PG), lambda i: (0, 0, 0)), _vec(PW),
                  pl.BlockSpec((PW, D), lambda i: (0, 0))],
        out_specs=_row(D), out_shape=SDS((t, D), F32),
        compiler_params=_params(("parallel",)), name=name,
    )(proj, proj, pool_w, pool_scale, pool_proj)


def _pool_bwd_local(proj, pool_w, pool_scale, pool_proj, dya, name):
    t = proj.shape[0]
    xcol = OFF_XP // PW

    def body(x_ref, h_ref, pw_ref, ps_ref, pp_ref, dya_ref, dwin_ref, dpl_ref, dpw_ref, dps_ref, dpp_ref):
        i = pl.program_id(0)

        @pl.when(i == 0)
        def _():
            dpw_ref[...] = jnp.zeros_like(dpw_ref)
            dps_ref[...] = jnp.zeros_like(dps_ref)
            dpp_ref[...] = jnp.zeros_like(dpp_ref)

        halo = jnp.where(i > 0, h_ref[...], 0.0)
        ext = jnp.concatenate([halo, x_ref[...]], axis=0)
        pooled, counts = _pool_windows(ext, i)
        mixed = jnp.concatenate(
            [_dg(pooled[g].astype(BF16), pw_ref[g].astype(BF16), NN) for g in range(4)], axis=1)
        ps = ps_ref[...]
        ypre = mixed * ps
        dyab = dya_ref[...].astype(BF16)
        dypre = _dg(dyab, pp_ref[...], NT)
        dpp_ref[...] += _dg(ypre.astype(BF16), dyab, TN)
        dps_ref[...] += jnp.sum(dypre * mixed, axis=0, keepdims=True)
        dmixed = dypre * ps
        for g in range(4):
            dm = dmixed[:, g * PG:(g + 1) * PG].astype(BF16)
            dpw_ref[g] += _dg(pooled[g].astype(BF16), dm, TN)
            dpooled = _dg(dm, pw_ref[g].astype(BF16), NT)
            dwin_ref[:, g * PG:(g + 1) * PG] = dpooled / counts[g]
            dpl_ref[:, g * PG:(g + 1) * PG] = dpooled

    return pl.pallas_call(
        body, grid=(t // TR,),
        in_specs=[_row(PW, xcol), _halo_prev(PW, xcol),
                  pl.BlockSpec((4, PG, PG), lambda i: (0, 0, 0)), _vec(PW),
                  pl.BlockSpec((PW, D), lambda i: (0, 0)), _row(D)],
        out_specs=[_row(PW), _row(PW), pl.BlockSpec((4, PG, PG), lambda i: (0, 0, 0)), _vec(PW),
                   pl.BlockSpec((PW, D), lambda i: (0, 0))],
        out_shape=[SDS((t, PW), F32), SDS((t, PW), F32), SDS((4, PG, PG), F32), SDS((1, PW), F32),
                   SDS((PW, D), F32)],
        compiler_params=_params(("arbitrary",)), name=name,
    )(proj, proj, pool_w, pool_scale, pool_proj, dya)


def _pool_bwd_window(dwin, dpl, name):
    t = dwin.shape[0]
    nt = t // TR
    ext_rows = TR + HALO

    def body(dw_ref, h_ref, dp_ref, o_ref):
        i = pl.program_id(0)
        halo = jnp.where(i < nt - 1, h_ref[...], 0.0)
        ext = jnp.concatenate([dw_ref[...], halo], axis=0)
        for gi in range(4):
            w = 2 << gi
            s = ext[:, gi * PG:(gi + 1) * PG]
            step = 1
            while step < w:
                s = s + pltpu.roll(s, ext_rows - step, 0)
                step *= 2
            o_ref[:, gi * PG:(gi + 1) * PG] = (s[:TR] - dp_ref[:, gi * PG:(gi + 1) * PG]).astype(o_ref.dtype)

    return pl.pallas_call(
        body, grid=(nt,), in_specs=[_row(PW), _halo_next(PW, 0, nt), _row(PW)], out_specs=_row(PW),
        out_shape=SDS((t, PW), BF16), compiler_params=_params(("parallel",)), name=name,
    )(dwin, dwin, dpl)


def _conv_group(ext, cw_ref, cols):
    acc = cw_ref[3:4, cols] * ext
    for j in range(3):
        acc = acc + cw_ref[j:j + 1, cols] * pltpu.roll(ext, 3 - j, 0)
    return acc[HALO:]


def _gate_terms(raw, al, dt):
    beta = _sigmoid(raw)
    xg = raw + dt
    sp = jnp.maximum(xg, 0.0) + jnp.log(1.0 + jnp.exp(-jnp.abs(xg)))
    g = -jnp.exp(al) * sp
    return beta, g, _sigmoid(xg)


def _dn_pre_fwd(proj, conv_w, al_row, dt_row, name):
    t = proj.shape[0]

    def body(x_ref, h_ref, cw_ref, ba_ref, al_ref, dt_ref, q_ref, k_ref, v_ref, bg_ref):
        i = pl.program_id(0)
        keep = i > 0
        for grp in range(24):
            cols = slice(grp * HD, (grp + 1) * HD)
            ext = jnp.concatenate([jnp.where(keep, h_ref[:, cols], 0.0), x_ref[:, cols]], axis=0)
            s = _silu(_conv_group(ext, cw_ref, cols))
            seg, head = divmod(grp, NH)
            hc = slice(head * HD, (head + 1) * HD)
            if seg == 0:
                q_ref[:, hc] = s * lax.rsqrt(jnp.sum(s * s, axis=-1, keepdims=True) + L2_EPS) * (HD ** -0.5)
            elif seg == 1:
                k_ref[:, hc] = s * lax.rsqrt(jnp.sum(s * s, axis=-1, keepdims=True) + L2_EPS)
            else:
                v_ref[:, hc] = s
        lane = _iota((TR, 128), 1)
        rowc = _iota((TR, 128), 0) % CH
        beta, g, _ = _gate_terms(ba_ref[...], al_ref[...], dt_ref[...])
        step = 1
        while step < CH:
            g = g + jnp.where(rowc >= step, pltpu.roll(g, step, 0), 0.0)
            step *= 2
        bg_ref[...] = jnp.where(lane < NH, beta, jnp.where(lane < 2 * NH, g, 0.0))

    return pl.pallas_call(
        body, grid=(t // TR,),
        in_specs=[_row(3 * D, 0), _halo_prev(3 * D, 0), pl.BlockSpec((4, 3 * D), lambda i: (0, 0)),
                  _row(128, OFF_BA // 128), _vec(128), _vec(128)],
        out_specs=[_row(D), _row(D), _row(D), _row(128)],
        out_shape=[SDS((t, D), F32), SDS((t, D), F32), SDS((t, D), F32), SDS((t, 128), F32)],
        compiler_params=_params(("parallel",)), name=name,
    )(proj, proj, conv_w, proj, al_row, dt_row)


def _dn_pre_bwd_act(proj, conv_w, al_row, dt_row, dq, dk, dv, dbg, name):
    t = proj.shape[0]

    def body(x_ref, h_ref, cw_ref, ba_ref, al_ref, dt_ref, dq_ref, dk_ref, dv_ref, dbg_ref,
             dc_ref, draw_ref, dal_ref, ddt_ref):
        i = pl.program_id(0)

        @pl.when(i == 0)
        def _():
            dal_ref[...] = jnp.zeros_like(dal_ref)
            ddt_ref[...] = jnp.zeros_like(ddt_ref)

        keep = i > 0
        for grp in range(24):
            cols = slice(grp * HD, (grp + 1) * HD)
            ext = jnp.concatenate([jnp.where(keep, h_ref[:, cols], 0.0), x_ref[:, cols]], axis=0)
            cv = _conv_group(ext, cw_ref, cols)
            seg, head = divmod(grp, NH)
            hc = slice(head * HD, (head + 1) * HD)
            if seg == 2:
                ds = dv_ref[:, hc]
            else:
                s = _silu(cv)
                r = lax.rsqrt(jnp.sum(s * s, axis=-1, keepdims=True) + L2_EPS)
                dy = dq_ref[:, hc] if seg == 0 else dk_ref[:, hc]
                c = (HD ** -0.5) if seg == 0 else 1.0
                ds = (c * r) * (dy - s * ((r * r) * jnp.sum(dy * s, axis=-1, keepdims=True)))
            dc_ref[:, cols] = ds * _dsilu(cv)
        lane = _iota((TR, 128), 1)
        rowc = _iota((TR, 128), 0) % CH
        isb = lane < NH
        isg = jnp.logical_and(lane >= NH, lane < 2 * NH)
        beta, g, sg = _gate_terms(ba_ref[...], al_ref[...], dt_ref[...])
        dbgv = dbg_ref[...]
        dg = dbgv
        step = 1
        while step < CH:
            dg = dg + jnp.where(rowc < CH - step, pltpu.roll(dg, TR - step, 0), 0.0)
            step *= 2
        da_raw = dg * (-jnp.exp(al_ref[...])) * sg
        draw_ref[...] = jnp.where(isb, dbgv * beta * (1.0 - beta), jnp.where(isg, da_raw, 0.0)).astype(draw_ref.dtype)
        dal_ref[...] += jnp.sum(jnp.where(isg, dg * g, 0.0), axis=0, keepdims=True)
        ddt_ref[...] += jnp.sum(jnp.where(isg, da_raw, 0.0), axis=0, keepdims=True)

    return pl.pallas_call(
        body, grid=(t // TR,),
        in_specs=[_row(3 * D, 0), _halo_prev(3 * D, 0), pl.BlockSpec((4, 3 * D), lambda i: (0, 0)),
                  _row(128, OFF_BA // 128), _vec(128), _vec(128), _row(D), _row(D), _row(D), _row(128)],
        out_specs=[_row(3 * D), _row(128), _vec(128), _vec(128)],
        out_shape=[SDS((t, 3 * D), F32), SDS((t, 128), BF16), SDS((1, 128), F32), SDS((1, 128), F32)],
        compiler_params=_params(("arbitrary",)), name=name,
    )(proj, proj, conv_w, proj, al_row, dt_row, dq, dk, dv, dbg)


def _dn_pre_bwd_conv(proj, conv_w, dconv, name):
    t = proj.shape[0]
    nt = t // TR
    ext_rows = TR + HALO

    def body(x_ref, h_ref, cw_ref, dc_ref, dn_ref, dx_ref, dcw_ref):
        i = pl.program_id(0)

        @pl.when(i == 0)
        def _():
            dcw_ref[...] = jnp.zeros_like(dcw_ref)

        keep_prev = i > 0
        keep_next = i < nt - 1
        for grp in range(24):
            cols = slice(grp * HD, (grp + 1) * HD)
            dct = dc_ref[:, cols]
            dext = jnp.concatenate([dct, jnp.where(keep_next, dn_ref[:, cols], 0.0)], axis=0)
            acc = cw_ref[3:4, cols] * dext
            for j in range(3):
                acc = acc + cw_ref[j:j + 1, cols] * pltpu.roll(dext, ext_rows - (3 - j), 0)
            dx_ref[:, cols] = acc[:TR].astype(dx_ref.dtype)
            xext = jnp.concatenate([jnp.where(keep_prev, h_ref[:, cols], 0.0), x_ref[:, cols]], axis=0)
            for j in range(4):
                xs = xext if j == 3 else pltpu.roll(xext, 3 - j, 0)
                dcw_ref[j:j + 1, cols] += jnp.sum(xs[HALO:] * dct, axis=0, keepdims=True)

    return pl.pallas_call(
        body, grid=(nt,),
        in_specs=[_row(3 * D, 0), _halo_prev(3 * D, 0), pl.BlockSpec((4, 3 * D), lambda i: (0, 0)),
                  _row(3 * D), _halo_next(3 * D, 0, nt)],
        out_specs=[_row(3 * D), pl.BlockSpec((4, 3 * D), lambda i: (0, 0))],
        out_shape=[SDS((t, 3 * D), BF16), SDS((4, 3 * D), F32)],
        compiler_params=_params(("arbitrary",)), name=name,
    )(proj, proj, conv_w, dconv, dconv)


def _dn_post_fwd(o, proj, gn, name):
    t = o.shape[0]

    def body(o_ref, z_ref, g_ref, out_ref):
        gv = g_ref[...]
        for h in range(NH):
            hc = slice(h * HD, (h + 1) * HD)
            ov = o_ref[:, hc]
            r = lax.rsqrt(jnp.mean(ov * ov, axis=-1, keepdims=True) + RMS_EPS)
            out_ref[:, hc] = (((ov * r) * gv) * _silu(z_ref[:, hc])).astype(out_ref.dtype)

    return pl.pallas_call(
        body, grid=(t // TR,), in_specs=[_row(D), _row(D, OFF_Z // D), _vec(HD)], out_specs=_row(D),
        out_shape=SDS((t, D), BF16), compiler_params=_params(("parallel",)), name=name,
    )(o, proj, gn)


def _dn_post_bwd(o, proj, gn, dob, name):
    t = o.shape[0]

    def body(o_ref, z_ref, g_ref, d_ref, do_ref, dz_ref, dg_ref):
        @pl.when(pl.program_id(0) == 0)
        def _():
            dg_ref[...] = jnp.zeros_like(dg_ref)

        gv = g_ref[...]
        acc = jnp.zeros((1, HD), F32)
        for h in range(NH):
            hc = slice(h * HD, (h + 1) * HD)
            ov = o_ref[:, hc]
            zv = z_ref[:, hc]
            dv = d_ref[:, hc]
            r = lax.rsqrt(jnp.mean(ov * ov, axis=-1, keepdims=True) + RMS_EPS)
            n = ov * r
            dz_ref[:, hc] = (dv * (n * gv) * _dsilu(zv)).astype(dz_ref.dtype)
            dng = dv * _silu(zv)
            acc = acc + jnp.sum(dng * n, axis=0, keepdims=True)
            dn = dng * gv
            do_ref[:, hc] = r * (dn - n * jnp.mean(dn * n, axis=-1, keepdims=True))
        dg_ref[...] += acc

    return pl.pallas_call(
        body, grid=(t // TR,), in_specs=[_row(D), _row(D, OFF_Z // D), _vec(HD), _row(D)],
        out_specs=[_row(D), _row(D), _vec(HD)],
        out_shape=[SDS((t, D), F32), SDS((t, D), BF16), SDS((1, HD), F32)],
        compiler_params=_params(("arbitrary",)), name=name,
    )(o, proj, gn, dob)


def _merge_fwd(ya, yb, proj, name):
    t = ya.shape[0]

    def body(a_ref, b_ref, gp_ref, gd_ref, o_ref):
        o_ref[...] = (_sigmoid(gp_ref[...]) * a_ref[...] + _sigmoid(gd_ref[...]) * b_ref[...]).astype(o_ref.dtype)

    return pl.pallas_call(
        body, grid=(t // TR,), in_specs=[_row(D), _row(D), _row(D, OFF_GP // D), _row(D, OFF_GD // D)],
        out_specs=_row(D), out_shape=SDS((t, D), BF16),
        compiler_params=_params(("parallel",)), name=name,
    )(ya, yb, proj, proj)


def _merge_bwd(dm, ya, yb, proj, name):
    t = ya.shape[0]

    def body(d_ref, a_ref, b_ref, gp_ref, gd_ref, da_ref, db_ref, dgp_ref, dgd_ref):
        dv = d_ref[...]
        sp = _sigmoid(gp_ref[...])
        sd = _sigmoid(gd_ref[...])
        da_ref[...] = dv * sp
        db_ref[...] = (dv * sd).astype(db_ref.dtype)
        dgp_ref[...] = (dv * a_ref[...] * sp * (1.0 - sp)).astype(dgp_ref.dtype)
        dgd_ref[...] = (dv * b_ref[...] * sd * (1.0 - sd)).astype(dgd_ref.dtype)

    return pl.pallas_call(
        body, grid=(t // TR,),
        in_specs=[_row(D), _row(D), _row(D), _row(D, OFF_GP // D), _row(D, OFF_GD // D)],
        out_specs=[_row(D)] * 4,
        out_shape=[SDS((t, D), F32), SDS((t, D), BF16), SDS((t, D), BF16), SDS((t, D), BF16)],
        compiler_params=_params(("parallel",)), name=name,
    )(dm, ya, yb, proj, proj)


def _split2(x):
    hi = x.astype(BF16)
    return hi, (x - hi.astype(F32)).astype(BF16)


def _dot3(a, b, dims):
    ah, al = _split2(a)
    bh, bl = _split2(b)
    return _dg(ah, bh, dims) + (_dg(ah, bl, dims) + _dg(al, bh, dims))


def _neumann_inverses(mats):
    ri = _iota((CH, CH), 0)
    ci = _iota((CH, CH), 1)
    eye = jnp.where(ri == ci, 1.0, 0.0).astype(F32)
    xs = [-a for a in mats]
    ps = [eye + x for x in xs]
    for _ in range(5):
        xs = [_dot3(x, x, NN) for x in xs]
        ps = [p + _dot3(p, x, NN) for p, x in zip(ps, xs)]
    return ps


def _solve_with(inv):
    @jax.custom_vjp
    def solve(a, rhs):
        return _dot3(inv, rhs, NN)

    def fwd(a, rhs):
        sol = _dot3(inv, rhs, NN)
        return sol, sol

    def bwd(sol, d):
        drhs = _dot3(inv, d, TN)
        return -_dot3(drhs, sol, NT), drhs

    solve.defvjp(fwd, bwd)
    return solve


@jax.custom_vjp
def _rows_to_lanes(g64):
    ri = _iota((CH, CH), 0)
    ci = _iota((CH, CH), 1)
    diag = jnp.where(ri == ci, g64, 0.0)
    ones = jnp.ones((CH, CH), BF16)
    hi = diag.astype(BF16)
    rem = diag - hi.astype(F32)
    mid = rem.astype(BF16)
    lo = (rem - mid.astype(F32)).astype(BF16)
    return _dg(ones, hi, NN) + (_dg(ones, mid, NN) + _dg(ones, lo, NN))


def _rows_to_lanes_bwd(_, d):
    ri = _iota((CH, CH), 0)
    ci = _iota((CH, CH), 1)
    return (jnp.where(ri == ci, jnp.broadcast_to(jnp.sum(d, axis=0, keepdims=True), (CH, CH)), 0.0),)


_rows_to_lanes.defvjp(lambda g64: (_rows_to_lanes(g64), None), _rows_to_lanes_bwd)


def _chunk_local(solve_all, q, k, v, g128, g64, gl128, b128, b64):
    ri = _iota((CH, CH), 0)
    ci = _iota((CH, CH), 1)
    causal = ri >= ci
    strict = ri > ci
    gj = [_rows_to_lanes(g) for g in g64]
    decay = [jnp.where(causal, jnp.exp(jnp.where(causal, g - t, 0.0)), 0.0) for g, t in zip(g64, gj)]
    kk = [_nt(x, x) for x in k]
    a = [jnp.where(strict, b * m * dc, 0.0) for b, m, dc in zip(b64, kk, decay)]
    eg = [jnp.exp(g) for g in g128]
    rhs = [jnp.concatenate([b * x, (b * e) * y], axis=1) for b, x, e, y in zip(b128, v, eg, k)]
    sol = solve_all(a, rhs)
    qk = [jnp.where(causal, _nt(x, y) * dc, 0.0) for x, y, dc in zip(q, k, decay)]
    return ([s[:, :HD] for s in sol], [s[:, HD:] for s in sol], qk, [x * e for x, e in zip(q, eg)],
            [x * jnp.exp(gl - g) for x, gl, g in zip(k, gl128, g128)], [jnp.exp(gl) for gl in gl128])


def _all_head_gates(bgv):
    return tuple(list(z) for z in zip(*[_head_gates(bgv, h) for h in range(NH)]))


def _head_gates(bgv, h):
    lane = _iota((CH, 128), 1)
    row = _iota((CH, 128), 0)
    bcol = jnp.sum(jnp.where(lane == h, bgv, 0.0), axis=1, keepdims=True)
    gcol = jnp.sum(jnp.where(lane == NH + h, bgv, 0.0), axis=1, keepdims=True)
    g128 = jnp.broadcast_to(gcol, (CH, 128))
    gl128 = jnp.broadcast_to(jnp.sum(jnp.where(row == CH - 1, g128, 0.0), axis=0, keepdims=True), (CH, 128))
    return (g128, jnp.broadcast_to(gcol, (CH, CH)), gl128,
            jnp.broadcast_to(bcol, (CH, 128)), jnp.broadcast_to(bcol, (CH, CH)))


def _chunk_specs():
    row = pl.BlockSpec((CH, D), lambda i: (i, 0))
    small = pl.BlockSpec((CH, 128), lambda i: (i, 0))
    qk = pl.BlockSpec((NH, CH, CH), lambda i: (i, 0, 0))
    eg = pl.BlockSpec((1, NH, 128), lambda i: (i, 0, 0))
    return row, small, qk, eg


def _dn_local_fwd(q, k, v, bg, name):
    t = q.shape[0]
    n = t // CH

    def body(q_ref, k_ref, v_ref, bg_ref, u_ref, w_ref, qk_ref, qd_ref, kd_ref, eg_ref, inv_ref):
        cols = [slice(h * HD, (h + 1) * HD) for h in range(NH)]

        def solve_all(mats, rhs):
            invs = _neumann_inverses(mats)
            for h in range(NH):
                inv_ref[h] = invs[h]
            return [_dot3(m, r, NN) for m, r in zip(invs, rhs)]

        u, w, qk, qd, kd, egl = _chunk_local(
            solve_all, [q_ref[:, c] for c in cols], [k_ref[:, c] for c in cols], [v_ref[:, c] for c in cols],
            *_all_head_gates(bg_ref[...]))
        for h, hc in enumerate(cols):
            u_ref[:, hc] = u[h]
            w_ref[:, hc] = w[h].astype(w_ref.dtype)
            qd_ref[:, hc] = qd[h].astype(qd_ref.dtype)
            kd_ref[:, hc] = kd[h].astype(kd_ref.dtype)
            qk_ref[h] = qk[h].astype(qk_ref.dtype)
            eg_ref[0, h:h + 1, :] = egl[h][0:1, :]

    row, small, qkb, egb = _chunk_specs()
    return pl.pallas_call(
        body, grid=(n,), in_specs=[row, row, row, small], out_specs=[row, row, qkb, row, row, egb, qkb],
        out_shape=[SDS((t, D), F32), SDS((t, D), BF16), SDS((n * NH, CH, CH), BF16), SDS((t, D), BF16),
                   SDS((t, D), BF16), SDS((n, NH, 128), F32), SDS((n * NH, CH, CH), F32)],
        compiler_params=_params(("parallel",)), name=name,
    )(q, k, v, bg)


def _dn_local_bwd(q, k, v, bg, inv, du, dw, dqk, dqd, dkd, deg, name):
    t = q.shape[0]
    n = t // CH

    def body(q_ref, k_ref, v_ref, bg_ref, inv_ref, du_ref, dw_ref, dqk_ref, dqd_ref, dkd_ref, deg_ref,
             dq_ref, dk_ref, dv_ref, dbg_ref):
        bgv = bg_ref[...]
        lane = _iota((CH, 128), 1)
        row = _iota((CH, 128), 0)
        first = jnp.where(row == 0, 1.0, 0.0)
        acc = jnp.zeros((CH, 128), F32)
        cols = [slice(h * HD, (h + 1) * HD) for h in range(NH)]
        solves = [_solve_with(inv_ref[h]) for h in range(NH)]

        def solve_all(mats, rhs):
            return [f(m, r) for f, m, r in zip(solves, mats, rhs)]

        _, vjp = jax.vjp(functools.partial(_chunk_local, solve_all),
                         [q_ref[:, c] for c in cols], [k_ref[:, c] for c in cols], [v_ref[:, c] for c in cols],
                         *_all_head_gates(bgv))
        cts = ([du_ref[:, c] for c in cols], [dw_ref[:, c] for c in cols], [dqk_ref[h] for h in range(NH)],
               [dqd_ref[:, c] for c in cols], [dkd_ref[:, c] for c in cols],
               [jnp.broadcast_to(deg_ref[0, h:h + 1, :], (CH, 128)) * first for h in range(NH)])
        dq, dk, dv, dg128, dg64, dgl, db128, db64 = vjp(cts)
        for h, hc in enumerate(cols):
            dq_ref[:, hc] = dq[h]
            dk_ref[:, hc] = dk[h]
            dv_ref[:, hc] = dv[h]
            dg = jnp.sum(dg128[h], axis=1, keepdims=True) + jnp.sum(dg64[h], axis=1, keepdims=True)
            tot = jnp.sum(jnp.sum(dgl[h], axis=0, keepdims=True), axis=1, keepdims=True)
            dg = dg + jnp.where(row[:, 0:1] == CH - 1, tot, 0.0)
            db = jnp.sum(db128[h], axis=1, keepdims=True) + jnp.sum(db64[h], axis=1, keepdims=True)
            acc = acc + jnp.where(lane == h, db, 0.0) + jnp.where(lane == NH + h, dg, 0.0)
        dbg_ref[...] = acc

    row, small, qkb, egb = _chunk_specs()
    return pl.pallas_call(
        body, grid=(n,), in_specs=[row, row, row, small, qkb, row, row, qkb, row, row, egb],
        out_specs=[row, row, row, small],
        out_shape=[SDS((t, D), F32)] * 3 + [SDS((t, 128), F32)],
        compiler_params=_params(("parallel",)), name=name,
    )(q, k, v, bg, inv, du, dw, dqk, dqd, dkd, deg)


def _state_step(s, u, w, qk, qd, kd, egl):
    ws = [_nn(a, b) for a, b in zip(w, s)]
    v_new = [a - b for a, b in zip(u, ws)]
    qs = [_nn(a, b) for a, b in zip(qd, s)]
    intra = [_nn(a, b) for a, b in zip(qk, v_new)]
    upd = [_tn(a, b) for a, b in zip(kd, v_new)]
    return [a * e + b for a, e, b in zip(s, egl, upd)], [a + b for a, b in zip(qs, intra)]


def _dn_scan_fwd(u, w, qk, qd, kd, eg, name):
    t = u.shape[0]
    n = t // CH
    g = SCAN_CHUNKS

    def body(u_ref, w_ref, qk_ref, qd_ref, kd_ref, eg_ref, o_ref, save_ref, s_ref):
        @pl.when(pl.program_id(0) == 0)
        def _():
            s_ref[...] = jnp.zeros_like(s_ref)

        cols = [slice(h * HD, (h + 1) * HD) for h in range(NH)]
        s = [s_ref[h] for h in range(NH)]
        for c in range(g):
            rows = slice(c * CH, (c + 1) * CH)
            for h in range(NH):
                save_ref[c, h] = s[h].astype(save_ref.dtype)
            s, o = _state_step(
                s, [u_ref[rows, hc] for hc in cols], [w_ref[rows, hc].astype(F32) for hc in cols],
                [qk_ref[c * NH + h].astype(F32) for h in range(NH)], [qd_ref[rows, hc].astype(F32) for hc in cols],
                [kd_ref[rows, hc].astype(F32) for hc in cols], [eg_ref[c, h:h + 1, :] for h in range(NH)])
            for h, hc in enumerate(cols):
                o_ref[rows, hc] = o[h]
        for h in range(NH):
            s_ref[h] = s[h]

    row = pl.BlockSpec((g * CH, D), lambda i: (i, 0))
    qkb = pl.BlockSpec((g * NH, CH, CH), lambda i: (i, 0, 0))
    egb = pl.BlockSpec((g, NH, 128), lambda i: (i, 0, 0))
    return pl.pallas_call(
        body, grid=(n // g,), in_specs=[row, row, qkb, row, row, egb],
        out_specs=[row, pl.BlockSpec((g, NH, HD, HD), lambda i: (i, 0, 0, 0))],
        out_shape=[SDS((t, D), F32), SDS((n, NH, HD, HD), BF16)],
        scratch_shapes=[pltpu.VMEM((NH, HD, HD), F32)],
        compiler_params=_params(("arbitrary",)), name=name,
    )(u, w, qk, qd, kd, eg)


def _dn_scan_bwd(u, w, qk, qd, kd, eg, saved, do, name):
    t = u.shape[0]
    n = t // CH
    g = SCAN_CHUNKS
    last = n // g - 1

    def body(u_ref, w_ref, qk_ref, qd_ref, kd_ref, eg_ref, sv_ref, do_ref,
             du_ref, dw_ref, dqk_ref, dqd_ref, dkd_ref, deg_ref, ds_ref):
        @pl.when(pl.program_id(0) == 0)
        def _():
            ds_ref[...] = jnp.zeros_like(ds_ref)

        cols = [slice(h * HD, (h + 1) * HD) for h in range(NH)]
        ds = [ds_ref[h] for h in range(NH)]
        for c in reversed(range(g)):
            rows = slice(c * CH, (c + 1) * CH)
            _, vjp = jax.vjp(
                _state_step, [sv_ref[c, h].astype(F32) for h in range(NH)], [u_ref[rows, hc] for hc in cols],
                [w_ref[rows, hc].astype(F32) for hc in cols], [qk_ref[c * NH + h].astype(F32) for h in range(NH)],
                [qd_ref[rows, hc].astype(F32) for hc in cols], [kd_ref[rows, hc].astype(F32) for hc in cols],
                [eg_ref[c, h:h + 1, :] for h in range(NH)])
            ds, du, dw, dqk, dqd, dkd, deg = vjp((ds, [do_ref[rows, hc] for hc in cols]))
            for h, hc in enumerate(cols):
                du_ref[rows, hc] = du[h]
                dw_ref[rows, hc] = dw[h]
                dqk_ref[c * NH + h] = dqk[h]
                dqd_ref[rows, hc] = dqd[h]
                dkd_ref[rows, hc] = dkd[h]
                deg_ref[c, h:h + 1, :] = deg[h]
        for h in range(NH):
            ds_ref[h] = ds[h]

    row = pl.BlockSpec((g * CH, D), lambda i: (last - i, 0))
    qkb = pl.BlockSpec((g * NH, CH, CH), lambda i: (last - i, 0, 0))
    egb = pl.BlockSpec((g, NH, 128), lambda i: (last - i, 0, 0))
    return pl.pallas_call(
        body, grid=(n // g,),
        in_specs=[row, row, qkb, row, row, egb,
                  pl.BlockSpec((g, NH, HD, HD), lambda i: (last - i, 0, 0, 0)), row],
        out_specs=[row, row, qkb, row, row, egb],
        out_shape=[SDS((t, D), F32), SDS((t, D), F32), SDS((n * NH, CH, CH), F32), SDS((t, D), F32),
                   SDS((t, D), F32), SDS((n, NH, 128), F32)],
        scratch_shapes=[pltpu.VMEM((NH, HD, HD), F32)],
        compiler_params=_params(("arbitrary",)), name=name,
    )(u, w, qk, qd, kd, eg, saved, do)


def _ada_fwd(c_all, ada_w, ada_b, name):
    ncol = ada_w.shape[1]

    def body(c_ref, w_ref, b_ref, o_ref):
        o_ref[...] = _dg(_silu(c_ref[...]), w_ref[...], NN, HI) + b_ref[...]

    return pl.pallas_call(body, out_shape=SDS((NDEV, ncol), F32),
                          compiler_params=pltpu.CompilerParams(vmem_limit_bytes=VMEM_LIMIT), name=name,
                          )(c_all, ada_w, ada_b)


def _ada_bwd(c_all_t, dmod, name):
    ncol = dmod.shape[1]

    def body(c_ref, d_ref, o_ref):
        sc = _silu(c_ref[...])
        acc = sc[:, 0:1] * d_ref[0:1, :]
        for b in range(1, NDEV):
            acc = acc + sc[:, b:b + 1] * d_ref[b:b + 1, :]
        o_ref[...] = acc

    return pl.pallas_call(body, out_shape=SDS((D, ncol), F32),
                          compiler_params=pltpu.CompilerParams(vmem_limit_bytes=VMEM_LIMIT), name=name,
                          )(c_all_t, dmod)


def _sum_devices(parts, out_dtype, name):
    _, r, c = parts.shape
    tr = TR if r % TR == 0 else r

    def body(p_ref, o_ref):
        acc = p_ref[0].astype(F32)
        for i in range(1, NDEV):
            acc = acc + p_ref[i].astype(F32)
        o_ref[...] = acc.astype(o_ref.dtype)

    return pl.pallas_call(
        body, grid=(r // tr,), in_specs=[pl.BlockSpec((NDEV, tr, c), lambda i: (0, i, 0))],
        out_specs=pl.BlockSpec((tr, c), lambda i: (i, 0)), out_shape=SDS((r, c), out_dtype),
        compiler_params=_params(("parallel",)), name=name,
    )(parts)


def _adam_tiles(r, c):
    if r % 8 == 0:
        return _pick(r, (256, 352, 128, 8)), c
    return r, (256 if c % 256 == 0 else c)


def _adam_math(w, gv, m, v):
    m_new = ADAM_B1 * m + (1.0 - ADAM_B1) * gv
    v_new = ADAM_B2 * v + (1.0 - ADAM_B2) * (gv * gv)
    bc1 = 1.0 - ADAM_B1 ** ADAM_STEP
    bc2 = 1.0 - ADAM_B2 ** ADAM_STEP
    return -ADAM_LR * ((m_new / bc1) / (jnp.sqrt(v_new / bc2) + ADAM_EPS) + ADAM_WD * w), m_new, v_new


def _adamw(w, g, m, v, name):
    r, c = w.shape
    tr, tc = _adam_tiles(r, c)

    def body(w_ref, g_ref, m_ref, v_ref, d_ref, nm_ref, nv_ref):
        d_ref[...], nm_ref[...], nv_ref[...] = _adam_math(w_ref[...], g_ref[...], m_ref[...], v_ref[...])

    spec = pl.BlockSpec((tr, tc), lambda i, j: (i, j))
    return pl.pallas_call(
        body, grid=(r // tr, c // tc), in_specs=[spec] * 4, out_specs=[spec] * 3,
        out_shape=[SDS((r, c), F32)] * 3, compiler_params=_params(("parallel", "parallel")), name=name,
    )(w, g, m, v)


def _reduce_adamw(parts, w, m, v, name):
    r, c = w.shape
    tr, tc = _adam_tiles(r, c)

    def body(p_ref, w_ref, m_ref, v_ref, g_ref, d_ref, nm_ref, nv_ref):
        gv = p_ref[0].astype(F32)
        for i in range(1, NDEV):
            gv = gv + p_ref[i].astype(F32)
        g_ref[...] = gv
        d_ref[...], nm_ref[...], nv_ref[...] = _adam_math(w_ref[...], gv, m_ref[...], v_ref[...])

    spec = pl.BlockSpec((tr, tc), lambda i, j: (i, j))
    return pl.pallas_call(
        body, grid=(r // tr, c // tc),
        in_specs=[pl.BlockSpec((NDEV, tr, tc), lambda i, j: (0, i, j))] + [spec] * 3, out_specs=[spec] * 4,
        out_shape=[SDS((r, c), F32)] * 4, compiler_params=_params(("parallel", "parallel")), name=name,
    )(parts, w, m, v)


ANY = pl.BlockSpec(memory_space=pl.ANY)
MESH = pl.DeviceIdType.MESH


def _all_gather(xs, name, after=None):
    n = len(xs)
    extra = [] if after is None else [after]

    def body(*refs):
        x_refs, out_refs = refs[:n], refs[n + len(extra):2 * n + len(extra)]
        send_sems, recv_sems, local_sems = refs[-3:]
        mx, my, mc = lax.axis_index("x"), lax.axis_index("y"), lax.axis_index("c")
        me, sibling = (mx, my, mc), (mx, my, 1 - mc)
        chips = [(1 - mx, my), (mx, 1 - my), (1 - mx, 1 - my)]

        def rows(a, px, py, pc):
            return out_refs[a].at[4 * px + 2 * py + pc]

        def copy(a, k, block, to, src=None):
            return pltpu.make_async_remote_copy(
                src_ref=rows(a, *block) if src is None else src, dst_ref=rows(a, *block),
                send_sem=send_sems.at[a, k], recv_sem=recv_sems.at[a, k], device_id=to, device_id_type=MESH)

        mine = [pltpu.make_async_copy(x_refs[a], rows(a, *me), local_sems.at[a]) for a in range(n)]
        for cp in mine:
            cp.start()
        first = []
        for a in range(n):
            first.append(copy(a, 0, me, sibling, src=x_refs[a]))
            first += [copy(a, 1 + j, me, (*chip, mc), src=x_refs[a]) for j, chip in enumerate(chips)]
        for cp in first:
            cp.start()
        passed = []
        for a in range(n):
            for j, chip in enumerate(chips):
                copy(a, 1 + j, (*chip, mc), me).wait_recv()
                passed.append(copy(a, 4 + j, (*chip, mc), sibling))
                passed[-1].start()
        for a in range(n):
            copy(a, 0, sibling, me).wait_recv()
            for j, chip in enumerate(chips):
                copy(a, 4 + j, (*chip, 1 - mc), me).wait_recv()
        for cp in first + passed:
            cp.wait_send()
        for cp in mine:
            cp.wait()

    return pl.pallas_call(
        body, out_shape=[SDS((NDEV,) + x.shape, x.dtype) for x in xs], in_specs=[ANY] * (n + len(extra)),
        out_specs=[ANY] * n,
        scratch_shapes=[pltpu.SemaphoreType.DMA((n, 7)), pltpu.SemaphoreType.DMA((n, 7)),
                        pltpu.SemaphoreType.DMA((n,))],
        name=name,
    )(*xs, *extra)


HBM = pl.BlockSpec(memory_space=pltpu.HBM)
SEM = pl.BlockSpec(memory_space=pltpu.SEMAPHORE)
EFFECT = pltpu.SideEffectType.DATAFLOW_SIDE_EFFECTING


def _peers():
    mx, my, mc = lax.axis_index("x"), lax.axis_index("y"), lax.axis_index("c")
    out = []
    for k in range(1, NDEV):
        out.append((1 - mx if k & 4 else mx, 1 - my if k & 2 else my, 1 - mc if k & 1 else mc))
    return 4 * mx + 2 * my + mc, out


def _push_start(srcs, sliced, name, after=None):
    n = len(srcs)
    extra = [] if after is None else [after]
    lands = [lax.empty(s.shape if sliced else (NDEV,) + s.shape, s.dtype) for s in srcs]

    def body(*refs):
        src_refs, land_refs = refs[:n], refs[n:2 * n]
        outs = refs[2 * n + len(extra):]
        send_sems, recv_sems = outs[:n], outs[n:2 * n]
        token = refs[-1]
        me, peers = _peers()
        for a in range(n):
            for k, (px, py, pc) in enumerate(peers):
                src = src_refs[a].at[4 * px + 2 * py + pc] if sliced else src_refs[a]
                pltpu.make_async_remote_copy(
                    src_ref=src, dst_ref=land_refs[a].at[me], send_sem=send_sems[a].at[k],
                    recv_sem=recv_sems[a].at[k], device_id=(px, py, pc), device_id_type=MESH).start()
            pltpu.make_async_copy(src_refs[a].at[me] if sliced else src_refs[a], land_refs[a].at[me],
                                  send_sems[a].at[NDEV - 1]).start()
        token[...] = jnp.zeros_like(token)

    outs = pl.pallas_call(
        body, name=name,
        out_shape=([pltpu.SemaphoreType.DMA((NDEV,))] * n + [pltpu.SemaphoreType.DMA((NDEV - 1,))] * n
                   + [pltpu.HBM(s.shape, s.dtype) for s in srcs] + [pltpu.HBM(l.shape, l.dtype) for l in lands]
                   + [SDS((8, 128), F32)]),
        in_specs=[HBM] * (2 * n) + [pl.BlockSpec(memory_space=pl.ANY)] * len(extra),
        out_specs=[SEM] * (2 * n) + [HBM] * (2 * n) + [pl.BlockSpec(memory_space=pltpu.VMEM)],
        input_output_aliases={i: 2 * n + i for i in range(2 * n)},
        compiler_params=pltpu.CompilerParams(has_side_effects=EFFECT),
    )(*[pltpu.with_memory_space_constraint(s, pltpu.HBM) for s in srcs],
      *[pltpu.with_memory_space_constraint(l, pltpu.HBM) for l in lands], *extra)
    sends, recvs = outs[:n], outs[n:2 * n]
    src_thru, land_thru = outs[2 * n:3 * n], outs[3 * n:4 * n]
    return [(sends[a], recvs[a], src_thru[a], land_thru[a]) for a in range(n)], outs[-1]


def _push_wait(started, sliced, after, name):
    n = len(started)

    def body(*refs):
        src_refs, land_refs = refs[:n], refs[n:2 * n]
        send_sems, recv_sems = refs[2 * n:3 * n], refs[3 * n:4 * n]
        me, peers = _peers()
        for a in range(n):
            for k, (px, py, pc) in enumerate(peers):
                src = src_refs[a].at[4 * px + 2 * py + pc] if sliced else src_refs[a]
                cp = pltpu.make_async_remote_copy(
                    src_ref=src, dst_ref=land_refs[a].at[me], send_sem=send_sems[a].at[k],
                    recv_sem=recv_sems[a].at[k], device_id=(px, py, pc), device_id_type=MESH)
                cp.wait_send()
                cp.wait_recv()
            pltpu.make_async_copy(src_refs[a].at[me] if sliced else src_refs[a], land_refs[a].at[me],
                                  send_sems[a].at[NDEV - 1]).wait()

    srcs = [s[2] for s in started]
    lands = [s[3] for s in started]
    outs = pl.pallas_call(
        body, name=name,
        out_shape=[pltpu.HBM(s.shape, s.dtype) for s in srcs] + [pltpu.HBM(l.shape, l.dtype) for l in lands],
        in_specs=[HBM] * (2 * n) + [SEM] * (2 * n) + [pl.BlockSpec(memory_space=pl.ANY)],
        out_specs=[HBM] * (2 * n),
        input_output_aliases={i: i for i in range(2 * n)},
        compiler_params=pltpu.CompilerParams(has_side_effects=EFFECT),
    )(*srcs, *lands, *[s[0] for s in started], *[s[1] for s in started], after)
    return outs[n:]


def _cols_from_blocks(blocks):
    _, rows, w = blocks.shape
    return blocks.transpose(1, 0, 2).reshape(rows, NDEV * w)


def _cols_to_blocks(full):
    rows, total = full.shape
    return full.reshape(rows, NDEV, total // NDEV).transpose(1, 0, 2)


def _mix_pad(wt):
    xp, q, k, v, z, ba, gp, gd = jnp.split(wt, (512, 1536, 2560, 3584, 4608, 4624, 5648), axis=0)
    pad = jnp.zeros((MIXP - OFF_BA - 16, wt.shape[1]), wt.dtype)
    return jnp.concatenate([q, k, v, z, gp, gd, xp, ba, pad], axis=0)


def _mix_unpad(wt):
    q, k, v, z, gp, gd, xp, ba = (wt[OFF_Q:OFF_K], wt[OFF_K:OFF_V], wt[OFF_V:OFF_Z], wt[OFF_Z:OFF_GP],
                                  wt[OFF_GP:OFF_GD], wt[OFF_GD:OFF_XP], wt[OFF_XP:OFF_BA], wt[OFF_BA:OFF_BA + 16])
    return jnp.concatenate([xp, q, k, v, z, ba, gp, gd], axis=0)


def _lane_row(vec8):
    return jnp.zeros((1, 128), F32).at[0, NH:2 * NH].set(vec8)


def _ffn_fwd(x, g, shift, scale, gate, w_in, w_out, tag, token=None, start_more=None):
    h = _norm_mod_fwd(x, g, shift, scale, f"{tag}_norm")
    if isinstance(w_in, tuple):
        w_in, = _push_wait([w_in], False, h, f"{tag}_gather_wait_in")
    w_in = w_in.reshape(2 * FH, D)
    u, a = _swiglu_up(h, w_in, f"{tag}_up", after=token)
    w_out, = _push_wait([w_out], False, a, f"{tag}_gather_wait_out")
    w_out = w_out.reshape(FH, D)
    y = _matmul(a, w_out, a_blk=True, out_dtype=F32, name=f"{tag}_down",
                after=None if start_more is None else start_more(h))
    return _resid_fwd(x, y, gate, 0.5, f"{tag}_res"), (h, u, a, y), w_in, w_out


def _ffn_bwd(dx_out, x, g, scale, gate, w_in, w_out, saved, tag):
    h, u, a, y = saved
    t = x.shape[0]
    dy, dgate = _resid_bwd(dx_out, y, gate, 0.5, f"{tag}_res_bwd")
    dw_out = _matmul(a, dy, ta=True, a_blk=True, out_dtype=BF16, name=f"{tag}_down_dw")
    sent_out, token = _push_start([dw_out.reshape(NDEV, FH // NDEV, D)], True, f"{tag}_grad_start_out")
    du = _swiglu_down_bwd(dy, w_out, u, f"{tag}_down_dx", after=token).reshape(NDEV, t, FB)
    dw_in = _matmul(du, h, ta=True, a_blk=True, out_dtype=BF16, name=f"{tag}_up_dw")
    sent_in, token = _push_start([dw_in.reshape(NDEV, FB, D)], True, f"{tag}_grad_start_in")
    dh = _matmul(du, w_in, a_blk=True, out_dtype=F32, name=f"{tag}_up_dx", after=token)
    dx, dshift, dscale, dg = _norm_mod_bwd(x, g, scale, dh, dx_out, f"{tag}_norm_bwd")
    return dx, (dshift, dscale, dgate), dg, sent_in + sent_out


def kernel(x, c, ada_w, ada_b, norm_g, ffn1_w_in, ffn1_w_out, ffn2_w_in, ffn2_w_out, mix_w_in, conv_w, a_log, dt_bias, dn_norm_g, pool_w, pool_scale, pool_proj, dn_proj, mix_w_out, final_g, loss_target, m_ada_w, m_ada_b, m_norm_g, m_ffn1_w_in, m_ffn1_w_out, m_ffn2_w_in, m_ffn2_w_out, m_mix_w_in, m_conv_w, m_a_log, m_dt_bias, m_dn_norm_g, m_pool_w, m_pool_scale, m_pool_proj, m_dn_proj, m_mix_w_out, m_final_g, v_ada_w, v_ada_b, v_norm_g, v_ffn1_w_in, v_ffn1_w_out, v_ffn2_w_in, v_ffn2_w_out, v_mix_w_in, v_conv_w, v_a_log, v_dt_bias, v_dn_norm_g, v_pool_w, v_pool_scale, v_pool_proj, v_dn_proj, v_mix_w_out, v_final_g):
    me = 4 * lax.axis_index("x") + 2 * lax.axis_index("y") + lax.axis_index("c")
    x0 = x[0]
    target = loss_target[0]
    t = x0.shape[0]

    big = [ffn1_w_in[0], ffn1_w_out[0], ffn2_w_in[0], ffn2_w_out[0], mix_w_in[0], pool_proj[0], dn_proj[0],
           mix_w_out[0]]
    small = jnp.concatenate([c.reshape(8, 128), conv_w[0].reshape(12, 128), norm_g[0].reshape(3, 128),
                             jnp.zeros((1, 128), F32)], axis=0)
    small_all, = _all_gather([small], "gather_small")
    c_all = small_all[:, 0:8, :].reshape(NDEV, D)
    conv_full = small_all[:, 8:20, :].reshape(NDEV, 4, 384).transpose(1, 0, 2).reshape(4, 3 * D)
    norm_full = small_all[:, 20:23, :].reshape(NDEV, 3, 128).transpose(1, 0, 2).reshape(3, D)

    ncol = ada_w.shape[2]
    ada_b_mine = lax.dynamic_slice(ada_b, (0, me * ncol), (1, ncol))
    mod_cols = _ada_fwd(c_all, ada_w[0], ada_b_mine, "ada_fwd")
    transposed = (0, 2, 4)
    payload = [(w.T if i in transposed else w).astype(BF16) for i, w in enumerate(big)]
    mod_all, w_in1 = _all_gather([mod_cols, payload[0]], "gather_mod_first_weight")
    started, token = _push_start([payload[1], payload[4]], False, "gather_start", after=mod_all)
    started = {1: started[0], 4: started[1]}

    def start_rest(h):
        more, token = _push_start([payload[i] for i in (5, 6, 7, 2, 3)], False, "gather_start_rest", after=h)
        started.update(zip((5, 6, 7, 2, 3), more))
        return token

    mod = lax.dynamic_index_in_dim(mod_all, me, axis=1, keepdims=False).reshape(9, D)
    shift = [mod[3 * s:3 * s + 1] for s in range(3)]
    scale = [mod[3 * s + 1:3 * s + 2] for s in range(3)]
    gate = [mod[3 * s + 2:3 * s + 3] for s in range(3)]
    ng = [norm_full[s:s + 1] for s in range(3)]
    fg = final_g.reshape(1, D)
    al_row = _lane_row(a_log[0])
    dt_row = _lane_row(dt_bias[0])
    gn = dn_norm_g
    pw = pool_w[0]
    ps = pool_scale

    x1, saved1, w_in1, w_out1 = _ffn_fwd(x0, ng[0], shift[0], scale[0], gate[0], w_in1, started[1], "ffn1", token,
                                         start_rest)

    h1 = _norm_mod_fwd(x1, ng[1], shift[1], scale[1], "mix_norm")
    seg, = _push_wait([started[4]], False, h1, "mix_gather_wait")
    w_mix = _mix_pad(seg.reshape(MIX_RAW, D))
    proj = _matmul(h1, w_mix, tb=True, out_dtype=F32, name="mix_in")
    qh, kh, vh, bg = _dn_pre_fwd(proj, conv_full, al_row, dt_row, "dn_pre")
    seg = _push_wait([started[i] for i in (5, 6, 7)], False, qh, "mix_gather_wait_rest")
    w_pp = _cols_from_blocks(seg[0])
    w_dn = seg[1].reshape(D, D)
    w_mo = seg[2].reshape(D, D)
    ya = _pool_fwd(proj, pw, ps, w_pp, "pool_fwd")
    u, w, qk, qd, kd, eg, inv = _dn_local_fwd(qh, kh, vh, bg, "dn_local")
    o, s_saved = _dn_scan_fwd(u, w, qk, qd, kd, eg, "dn_scan")
    ob = _dn_post_fwd(o, proj, gn, "dn_post")
    yb = _matmul(ob, w_dn, out_dtype=F32, name="dn_out")
    merged = _merge_fwd(ya, yb, proj, "merge")
    mix_y = _matmul(merged, w_mo, out_dtype=F32, name="mix_out")
    x2 = _resid_fwd(x1, mix_y, gate[1], 1.0, "mix_res")

    x3, saved2, w_in2, w_out2 = _ffn_fwd(x2, ng[2], shift[2], scale[2], gate[2], started[2], started[3], "ffn2")
    loss_row, dx3, dfg = _final_loss(x3, fg, target, "loss")

    dx2, dmod2, dng2, sent2 = _ffn_bwd(dx3, x2, ng[2], scale[2], gate[2], w_in2, w_out2, saved2, "ffn2")

    dmy, dgate1 = _resid_bwd(dx2, mix_y, gate[1], 1.0, "mix_res_bwd")
    dmerged = _matmul(dmy, w_mo, tb=True, out_dtype=F32, name="mix_out_dx")
    dw_mo = _matmul(merged, dmy, ta=True, out_dtype=BF16, name="mix_out_dw")
    dya, dyb, dgp, dgd = _merge_bwd(dmerged, ya, yb, proj, "merge_bwd")
    dob = _matmul(dyb, w_dn, tb=True, out_dtype=F32, name="dn_out_dx")
    dw_dn = _matmul(ob, dyb, ta=True, out_dtype=BF16, name="dn_out_dw")
    do, dz, dgn = _dn_post_bwd(o, proj, gn, dob, "dn_post_bwd")
    du, dw, dqk, dqd, dkd, deg = _dn_scan_bwd(u, w, qk, qd, kd, eg, s_saved, do, "dn_scan_bwd")
    dqh, dkh, dvh, dbg = _dn_local_bwd(qh, kh, vh, bg, inv, du, dw, dqk, dqd, dkd, deg, "dn_local_bwd")
    dconv, draw, dal, ddt = _dn_pre_bwd_act(proj, conv_full, al_row, dt_row, dqh, dkh, dvh, dbg, "dn_pre_bwd_act")
    dqkv, dcw = _dn_pre_bwd_conv(proj, conv_full, dconv, "dn_pre_bwd_conv")
    dwin, dpl, dpw, dps, dpp = _pool_bwd_local(proj, pw, ps, w_pp, dya, "pool_bwd_local")
    dxp = _pool_bwd_window(dwin, dpl, "pool_bwd_window")
    dproj = jnp.concatenate([dqkv, dz, dgp, dgd, dxp, draw, jnp.zeros((t, MIXP - OFF_BA - 128), BF16)], axis=1)
    dw_mix = _matmul(dproj, h1, ta=True, out_dtype=BF16, name="mix_in_dw")
    sent1, token = _push_start(
        [_mix_unpad(dw_mix).reshape(NDEV, MIX_RAW // NDEV, D), _cols_to_blocks(dpp.astype(BF16)),
         dw_dn.reshape(NDEV, -1, D), dw_mo.reshape(NDEV, -1, D)], True, "mix_grad_start")
    dh1 = _matmul(dproj, w_mix, out_dtype=F32, name="mix_in_dx", after=token)
    dx1, dsh1, dsc1, dng1 = _norm_mod_bwd(x1, ng[1], scale[1], dh1, dx2, "mix_norm_bwd")

    dx0, dmod0, dng0, sent0 = _ffn_bwd(dx1, x0, ng[0], scale[0], gate[0], w_in1, w_out1, saved1, "ffn1")

    dmod = jnp.concatenate([*dmod0, dsh1, dsc1, dgate1, *dmod2], axis=1).reshape(-1)
    flat = jnp.concatenate([
        dmod, dal[0, NH:2 * NH], ddt[0, NH:2 * NH], dgn.reshape(-1), dps.reshape(-1), dfg.reshape(-1),
        dpw.reshape(-1), jnp.concatenate([dng0, dng1, dng2], axis=0).reshape(-1), dcw.reshape(-1),
        loss_row[0, 0:1]])
    nflat = 90 * D
    flat = jnp.concatenate([flat, jnp.zeros((nflat - flat.shape[0],), F32)]).reshape(90, D)
    flat_all, = _all_gather([flat], "gather_small_grads")
    tot = _sum_devices(flat_all, F32, "sum_small_grads").reshape(-1)
    dmod_all = flat_all.reshape(NDEV, nflat)[:, :9 * D]
    dmod_cols = lax.dynamic_slice(dmod_all, (0, me * ncol), (NDEV, ncol))
    g_ada_w = _ada_bwd(c_all.T, dmod_cols, "ada_bwd")

    p = 0
    pieces = {}
    for nm, size in (("ada_b", 9 * D), ("a_log", NH), ("dt_bias", NH), ("dn_norm_g", HD), ("pool_scale", PW),
                     ("final_g", D), ("pool_w", 4 * PG * PG), ("norm_g", 3 * D), ("conv_w", 12 * D),
                     ("loss", 1)):
        pieces[nm] = tot[p:p + size]
        p += size
    g_norm = lax.dynamic_slice(pieces["norm_g"].reshape(3, D), (0, me * 128), (3, 128))
    g_conv = lax.dynamic_slice(pieces["conv_w"].reshape(4, 3 * D), (0, me * 384), (4, 384))

    grads = {
        "ada_w": g_ada_w.reshape(ada_w.shape), "ada_b": pieces["ada_b"].reshape(ada_b.shape),
        "norm_g": g_norm.reshape(norm_g.shape), "conv_w": g_conv.reshape(conv_w.shape),
        "a_log": pieces["a_log"].reshape(a_log.shape), "dt_bias": pieces["dt_bias"].reshape(dt_bias.shape),
        "dn_norm_g": pieces["dn_norm_g"].reshape(dn_norm_g.shape), "pool_w": pieces["pool_w"].reshape(pool_w.shape),
        "pool_scale": pieces["pool_scale"].reshape(pool_scale.shape),
        "final_g": pieces["final_g"].reshape(final_g.shape),
    }
    weights = {"ada_w": ada_w, "ada_b": ada_b, "norm_g": norm_g, "ffn1_w_in": ffn1_w_in, "ffn1_w_out": ffn1_w_out,
               "ffn2_w_in": ffn2_w_in, "ffn2_w_out": ffn2_w_out, "mix_w_in": mix_w_in, "conv_w": conv_w,
               "a_log": a_log, "dt_bias": dt_bias, "dn_norm_g": dn_norm_g, "pool_w": pool_w,
               "pool_scale": pool_scale, "pool_proj": pool_proj, "dn_proj": dn_proj, "mix_w_out": mix_w_out,
               "final_g": final_g}
    m_in = {"ada_w": m_ada_w, "ada_b": m_ada_b, "norm_g": m_norm_g, "ffn1_w_in": m_ffn1_w_in,
            "ffn1_w_out": m_ffn1_w_out, "ffn2_w_in": m_ffn2_w_in, "ffn2_w_out": m_ffn2_w_out,
            "mix_w_in": m_mix_w_in, "conv_w": m_conv_w, "a_log": m_a_log, "dt_bias": m_dt_bias,
            "dn_norm_g": m_dn_norm_g, "pool_w": m_pool_w, "pool_scale": m_pool_scale, "pool_proj": m_pool_proj,
            "dn_proj": m_dn_proj, "mix_w_out": m_mix_w_out, "final_g": m_final_g}
    v_in = {"ada_w": v_ada_w, "ada_b": v_ada_b, "norm_g": v_norm_g, "ffn1_w_in": v_ffn1_w_in,
            "ffn1_w_out": v_ffn1_w_out, "ffn2_w_in": v_ffn2_w_in, "ffn2_w_out": v_ffn2_w_out,
            "mix_w_in": v_mix_w_in, "conv_w": v_conv_w, "a_log": v_a_log, "dt_bias": v_dt_bias,
            "dn_norm_g": v_dn_norm_g, "pool_w": v_pool_w, "pool_scale": v_pool_scale, "pool_proj": v_pool_proj,
            "dn_proj": v_dn_proj, "mix_w_out": v_mix_w_out, "final_g": v_final_g}

    names = list(weights)
    large = ("ada_w", "ffn1_w_in", "ffn1_w_out", "ffn2_w_in", "ffn2_w_out", "mix_w_in", "pool_proj", "dn_proj",
             "mix_w_out")
    delta, new_m, new_v = {}, {}, {}

    flipped = ("ffn1_w_in", "ffn2_w_in", "mix_w_in")

    def views(nm):
        shp = weights[nm].shape
        two_d = (shp[-2], shp[-1])
        if nm in flipped:
            return (lambda a: a.reshape(two_d).T), (lambda a: a.T.reshape(shp))
        return (lambda a: a.reshape(two_d)), (lambda a: a.reshape(shp))

    def reduce_update(sent, group, after, tag):
        for nm, r in zip(group, _push_wait(sent, True, after, f"{tag}_grad_wait")):
            view, back = views(nm)
            g_, d_, m_, v_ = _reduce_adamw(r, view(weights[nm]), view(m_in[nm]), view(v_in[nm]), f"adamw_{nm}")
            grads[nm], delta[nm], new_m[nm], new_v[nm] = back(g_), back(d_), back(m_), back(v_)
        return d_

    view, back = views("ada_w")
    done, m_, v_ = _adamw(view(ada_w), view(grads["ada_w"]), view(m_ada_w), view(v_ada_w), "adamw_ada_w")
    delta["ada_w"], new_m["ada_w"], new_v["ada_w"] = back(done), back(m_), back(v_)
    done = reduce_update(sent2, ("ffn2_w_in", "ffn2_w_out"), done, "ffn2")
    done = reduce_update(sent1, ("mix_w_in", "pool_proj", "dn_proj", "mix_w_out"), done, "mix")
    reduce_update(sent0, ("ffn1_w_in", "ffn1_w_out"), done, "ffn1")
    rest = [nm for nm in names if nm not in large]
    total = sum(weights[nm].size for nm in rest)
    padded = -(-total // D) * D

    def pack(tree, fill):
        flat_ = jnp.concatenate([tree[nm].reshape(-1) for nm in rest])
        return jnp.concatenate([flat_, jnp.full((padded - total,), fill, F32)]).reshape(-1, D)

    d_, m_, v_ = _adamw(pack(weights, 0.0), pack(grads, 0.0), pack(m_in, 0.0), pack(v_in, 1.0), "adamw_small")
    p = 0
    for nm in rest:
        size = weights[nm].size
        shp = weights[nm].shape
        delta[nm] = d_.reshape(-1)[p:p + size].reshape(shp)
        new_m[nm] = m_.reshape(-1)[p:p + size].reshape(shp)
        new_v[nm] = v_.reshape(-1)[p:p + size].reshape(shp)
        p += size

    loss = pieces["loss"][0]
    grad_x = dx0.reshape(x.shape)
    return (loss, grad_x, *[grads[nm] for nm in names], *[delta[nm] for nm in names],
            *[new_m[nm] for nm in names], *[new_v[nm] for nm in names])
```

```python
import functools

import jax
import jax.numpy as jnp
from jax import lax
from jax.experimental import pallas as pl
from jax.experimental.pallas import tpu as pltpu

F32 = jnp.float32
BF16 = jnp.bfloat16
SDS = jax.ShapeDtypeStruct
HI = lax.Precision.HIGHEST

D = 1024
FH = 2816
FB = 704
NH = 8
HD = 128
CH = 64
SCAN_CHUNKS = 2
NDEV = 8
PW = 512
PG = 128
RMS_EPS = 1e-6
L2_EPS = 1e-6
TR = 512
HALO = 16
VMEM_LIMIT = 56 * 1024 * 1024

MIXP = 6912
OFF_Q, OFF_K, OFF_V, OFF_Z, OFF_GP, OFF_GD, OFF_XP, OFF_BA = 0, 1024, 2048, 3072, 4096, 5120, 6144, 6656
MIX_RAW = 6672

ADAM_LR = 0.001
ADAM_B1 = 0.9
ADAM_B2 = 0.999
ADAM_EPS = 1e-08
ADAM_WD = 0.01
ADAM_STEP = 10

NN = (((1,), (0,)), ((), ()))
NT = (((1,), (1,)), ((), ()))
TN = (((0,), (0,)), ((), ()))


def _dg(a, b, dims, prec=None):
    return lax.dot_general(a, b, dims, precision=prec, preferred_element_type=F32)


def _make_dots(prec):
    @jax.custom_vjp
    def nn(a, b):
        return _dg(a, b, NN, prec)

    @jax.custom_vjp
    def nt(a, b):
        return _dg(a, b, NT, prec)

    @jax.custom_vjp
    def tn(a, b):
        return _dg(a, b, TN, prec)

    nn.defvjp(lambda a, b: (nn(a, b), (a, b)), lambda r, d: (nt(d, r[1]), tn(r[0], d)))
    nt.defvjp(lambda a, b: (nt(a, b), (a, b)), lambda r, d: (nn(d, r[1]), tn(d, r[0])))
    tn.defvjp(lambda a, b: (tn(a, b), (a, b)), lambda r, d: (nt(r[1], d), nn(r[0], d)))
    return nn, nt, tn


_nn, _nt, _tn = _make_dots(None)


def _params(sem):
    return pltpu.CompilerParams(dimension_semantics=sem, vmem_limit_bytes=VMEM_LIMIT)


def _sigmoid(x):
    return 1.0 / (1.0 + jnp.exp(-x))


def _silu(x):
    return x * _sigmoid(x)


def _dsilu(x):
    s = _sigmoid(x)
    return s * (1.0 + x * (1.0 - s))


def _pick(n, cands):
    for c in cands:
        if n % c == 0:
            return c
    raise ValueError(f"no tile for {n}")


def _iota(shape, dim):
    return lax.broadcasted_iota(jnp.int32, shape, dim)


def _matmul(a, b, *, ta=False, tb=False, a_blk=False, b_blk=False, o_blk=False, tm=None, tn=None, tk=None,
            out_dtype, name, after=None):
    if a_blk:
        nb, r, cb = a.shape
        if ta:
            k_dim, m_dim, tm = r, nb * cb, cb
        else:
            m_dim, k_dim, tk = r, nb * cb, cb
    else:
        k_dim, m_dim = a.shape if ta else a.shape[::-1]
    if b_blk:
        nb, r, cb = b.shape
        if tb:
            n_dim, tk = r, cb
            assert nb * cb == k_dim
        else:
            n_dim, tn = nb * cb, cb
            assert r == k_dim
    else:
        n_dim = b.shape[0] if tb else b.shape[1]
    tm = tm or _pick(m_dim, (1024, 768, 512, 256, 128))
    tn = tn or _pick(n_dim, (1024, 768, 512, 256, 128))
    tk = tk or (k_dim if (k_dim <= 2816 and not ta) else _pick(k_dim, (2816, 2304, 1024, 512, 256)))
    nk = k_dim // tk
    dims = ((((0,) if ta else (1,)), ((1,) if tb else (0,))), ((), ()))

    def body(a_ref, b_ref, *rest):
        o_ref, acc_ref = rest[-2:]
        k = pl.program_id(2)

        @pl.when(k == 0)
        def _():
            acc_ref[...] = jnp.zeros_like(acc_ref)

        acc_ref[...] += lax.dot_general(a_ref[...].astype(BF16), b_ref[...].astype(BF16), dims,
                                        preferred_element_type=F32)

        @pl.when(k == nk - 1)
        def _():
            o_ref[...] = acc_ref[...].astype(o_ref.dtype)

    if a_blk:
        a_spec = (pl.BlockSpec((None, tk, tm), lambda i, j, k: (i, k, 0)) if ta
                  else pl.BlockSpec((None, tm, tk), lambda i, j, k: (k, i, 0)))
    else:
        a_spec = (pl.BlockSpec((tk, tm), lambda i, j, k: (k, i)) if ta
                  else pl.BlockSpec((tm, tk), lambda i, j, k: (i, k)))
    if b_blk:
        b_spec = (pl.BlockSpec((None, tn, tk), lambda i, j, k: (k, j, 0)) if tb
                  else pl.BlockSpec((None, tk, tn), lambda i, j, k: (j, k, 0)))
    else:
        b_spec = (pl.BlockSpec((tn, tk), lambda i, j, k: (j, k)) if tb
                  else pl.BlockSpec((tk, tn), lambda i, j, k: (k, j)))
    if o_blk:
        o_spec = pl.BlockSpec((None, tm, tn), lambda i, j, k: (j, i, 0))
        o_shape = SDS((n_dim // tn, m_dim, tn), out_dtype)
    else:
        o_spec = pl.BlockSpec((tm, tn), lambda i, j, k: (i, j))
        o_shape = SDS((m_dim, n_dim), out_dtype)
    return pl.pallas_call(
        body, grid=(m_dim // tm, n_dim // tn, nk),
        in_specs=[a_spec, b_spec] + ([] if after is None else [pl.BlockSpec(memory_space=pl.ANY)]),
        out_specs=o_spec,
        out_shape=o_shape,
        scratch_shapes=[pltpu.VMEM((tm, tn), F32)],
        compiler_params=_params(("parallel", "parallel", "arbitrary")),
        name=name,
    )(a, b, *([] if after is None else [after]))


def _matmul_residual(a, b, x, gate, coef, *, a_blk=False, norm=None, name, after=None):
    if a_blk:
        nb, m_dim, tk = a.shape
        nk = nb
        a_spec = pl.BlockSpec((None, 512, tk), lambda i, k: (k, i, 0))
    else:
        m_dim, tk = a.shape
        nk = 1
        a_spec = pl.BlockSpec((512, tk), lambda i, k: (i, 0))
    tm = 512
    extra = [] if after is None else [after]
    vecs = [gate] + (list(norm) if norm else [])

    def body(a_ref, b_ref, x_ref, gate_ref, *rest):
        vec_refs = rest[:len(vecs) - 1]
        outs = rest[len(vecs) - 1 + len(extra):]
        acc_ref = outs[-1]
        k = pl.program_id(1)

        @pl.when(k == 0)
        def _():
            acc_ref[...] = jnp.zeros_like(acc_ref)

        acc_ref[...] += _dg(a_ref[...], b_ref[...], NN)

        @pl.when(k == nk - 1)
        def _():
            y = acc_ref[...]
            xn = x_ref[...] + (coef * gate_ref[...]) * y
            outs[0][...] = xn
            outs[1][...] = y.astype(outs[1].dtype)
            if norm:
                g_ref, sh_ref, sc_ref = vec_refs
                r = lax.rsqrt(jnp.mean(xn * xn, axis=-1, keepdims=True) + RMS_EPS)
                outs[2][...] = (((xn * r) * g_ref[...]) * (1.0 + sc_ref[...]) + sh_ref[...]).astype(outs[2].dtype)

    row = pl.BlockSpec((tm, D), lambda i, k: (i, 0))
    vec = pl.BlockSpec((1, D), lambda i, k: (0, 0))
    return pl.pallas_call(
        body, grid=(m_dim // tm, nk),
        in_specs=[a_spec, pl.BlockSpec((tk, D), lambda i, k: (k, 0)), row] + [vec] * len(vecs)
        + [pl.BlockSpec(memory_space=pl.ANY)] * len(extra),
        out_specs=[row] * (3 if norm else 2),
        out_shape=[SDS((m_dim, D), F32), SDS((m_dim, D), BF16)] + ([SDS((m_dim, D), BF16)] if norm else []),
        scratch_shapes=[pltpu.VMEM((tm, D), F32)],
        compiler_params=_params(("parallel", "arbitrary")), name=name,
    )(a, b, x, *vecs, *extra)


def _row(width, col=0):
    return pl.BlockSpec((TR, width), lambda i: (i, col))


def _vec(width):
    return pl.BlockSpec((1, width), lambda i: (0, 0))


def _norm_mod_fwd(x, g, shift, scale, name):
    t = x.shape[0]

    def body(x_ref, g_ref, sh_ref, sc_ref, o_ref):
        xv = x_ref[...]
        r = lax.rsqrt(jnp.mean(xv * xv, axis=-1, keepdims=True) + RMS_EPS)
        o_ref[...] = (((xv * r) * g_ref[...]) * (1.0 + sc_ref[...]) + sh_ref[...]).astype(o_ref.dtype)

    return pl.pallas_call(
        body, grid=(t // TR,), in_specs=[_row(D), _vec(D), _vec(D), _vec(D)], out_specs=_row(D),
        out_shape=SDS((t, D), BF16), compiler_params=_params(("parallel",)), name=name,
    )(x, g, shift, scale)


def _norm_mod_bwd(x, g, scale, dh, dx_in, name):
    t = x.shape[0]

    def body(x_ref, g_ref, sc_ref, dh_ref, dxi_ref, dx_ref, dsh_ref, dsc_ref, dg_ref):
        @pl.when(pl.program_id(0) == 0)
        def _():
            dsh_ref[...] = jnp.zeros_like(dsh_ref)
            dsc_ref[...] = jnp.zeros_like(dsc_ref)
            dg_ref[...] = jnp.zeros_like(dg_ref)

        xv = x_ref[...]
        gv = g_ref[...]
        dh = dh_ref[...]
        r = lax.rsqrt(jnp.mean(xv * xv, axis=-1, keepdims=True) + RMS_EPS)
        n = xv * r
        dsh_ref[...] += jnp.sum(dh, axis=0, keepdims=True)
        dsc_ref[...] += jnp.sum(dh * (n * gv), axis=0, keepdims=True)
        tt = dh * (1.0 + sc_ref[...])
        dg_ref[...] += jnp.sum(tt * n, axis=0, keepdims=True)
        dn = tt * gv
        dx_ref[...] = dxi_ref[...] + r * (dn - n * jnp.mean(dn * n, axis=-1, keepdims=True))

    return pl.pallas_call(
        body, grid=(t // TR,), in_specs=[_row(D), _vec(D), _vec(D), _row(D), _row(D)],
        out_specs=[_row(D), _vec(D), _vec(D), _vec(D)],
        out_shape=[SDS((t, D), F32), SDS((1, D), F32), SDS((1, D), F32), SDS((1, D), F32)],
        compiler_params=_params(("arbitrary",)), name=name,
    )(x, g, scale, dh, dx_in)


def _swiglu_up(h, w_in, name, after=None):
    t = h.shape[0]
    tm = _pick(t, (1024, 512, 256))
    half = NDEV // 2
    extra = [] if after is None else [after]

    def body(h_ref, wg_ref, wu_ref, *rest):
        u_ref, a_ref = rest[-2:]
        hv = h_ref[...]
        gate = _dg(hv, wg_ref[...], NT)
        up = _dg(hv, wu_ref[...], NT)
        u_ref[0] = gate.astype(u_ref.dtype)
        u_ref[1] = up.astype(u_ref.dtype)
        a_ref[...] = (_silu(gate) * up).astype(a_ref.dtype)

    return pl.pallas_call(
        body, grid=(t // tm, half),
        in_specs=[pl.BlockSpec((tm, D), lambda i, j: (i, 0)),
                  pl.BlockSpec((FB, D), lambda i, j: (j, 0)),
                  pl.BlockSpec((FB, D), lambda i, j: (j + half, 0))]
        + [pl.BlockSpec(memory_space=pl.ANY)] * len(extra),
        out_specs=[pl.BlockSpec((2, None, tm, FB), lambda i, j: (0, j, i, 0)),
                   pl.BlockSpec((None, tm, FB), lambda i, j: (j, i, 0))],
        out_shape=[SDS((2, half, t, FB), BF16), SDS((half, t, FB), BF16)],
        compiler_params=_params(("parallel", "parallel")), name=name,
    )(h, w_in, w_in, *extra)


def _swiglu_down_bwd(dy, w_out, u, name, after=None):
    t = dy.shape[0]
    tm = _pick(t, (1024, 512, 256))
    half = NDEV // 2
    extra = [] if after is None else [after]
    pair = pl.BlockSpec((2, None, tm, FB), lambda i, j: (0, j, i, 0))

    def body(dy_ref, w_ref, u_ref, *rest):
        o_ref = rest[-1]
        da = _dg(dy_ref[...], w_ref[...], NT)
        gate = u_ref[0].astype(F32)
        o_ref[0] = (da * u_ref[1].astype(F32) * _dsilu(gate)).astype(o_ref.dtype)
        o_ref[1] = (da * _silu(gate)).astype(o_ref.dtype)

    return pl.pallas_call(
        body, grid=(t // tm, half),
        in_specs=[pl.BlockSpec((tm, D), lambda i, j: (i, 0)), pl.BlockSpec((FB, D), lambda i, j: (j, 0)), pair]
        + [pl.BlockSpec(memory_space=pl.ANY)] * len(extra),
        out_specs=pair, out_shape=SDS((2, half, t, FB), BF16),
        compiler_params=_params(("parallel", "parallel")), name=name,
    )(dy, w_out, u, *extra)


def _resid_fwd(x, y, gate, coef, name):
    t = x.shape[0]

    def body(x_ref, y_ref, g_ref, o_ref):
        o_ref[...] = x_ref[...] + (coef * g_ref[...]) * y_ref[...]

    return pl.pallas_call(
        body, grid=(t // TR,), in_specs=[_row(D), _row(D), _vec(D)], out_specs=_row(D),
        out_shape=SDS((t, D), F32), compiler_params=_params(("parallel",)), name=name,
    )(x, y, gate)


def _resid_bwd(dx, y, gate, coef, name):
    t = dx.shape[0]

    def body(dx_ref, y_ref, g_ref, dy_ref, dg_ref):
        @pl.when(pl.program_id(0) == 0)
        def _():
            dg_ref[...] = jnp.zeros_like(dg_ref)

        dxv = dx_ref[...]
        dy_ref[...] = ((coef * g_ref[...]) * dxv).astype(dy_ref.dtype)
        dg_ref[...] += jnp.sum((coef * dxv) * y_ref[...], axis=0, keepdims=True)

    return pl.pallas_call(
        body, grid=(t // TR,), in_specs=[_row(D), _row(D), _vec(D)], out_specs=[_row(D), _vec(D)],
        out_shape=[SDS((t, D), BF16), SDS((1, D), F32)],
        compiler_params=_params(("arbitrary",)), name=name,
    )(dx, y, gate)


def _final_loss(x, fg, target, name):
    t = x.shape[0]
    nt = t // TR

    def body(x_ref, g_ref, t_ref, loss_ref, dx_ref, dg_ref, acc_ref):
        i = pl.program_id(0)

        @pl.when(i == 0)
        def _():
            acc_ref[...] = jnp.zeros_like(acc_ref)
            dg_ref[...] = jnp.zeros_like(dg_ref)

        xv = x_ref[...]
        gv = g_ref[...]
        r = lax.rsqrt(jnp.mean(xv * xv, axis=-1, keepdims=True) + RMS_EPS)
        n = xv * r
        err = n * gv - t_ref[...]
        acc_ref[...] += jnp.sum(err * err, axis=0, keepdims=True)
        dy = err * (1.0 / D)
        dg_ref[...] += jnp.sum(dy * n, axis=0, keepdims=True)
        dn = dy * gv
        dx_ref[...] = r * (dn - n * jnp.mean(dn * n, axis=-1, keepdims=True))

        @pl.when(i == nt - 1)
        def _():
            tot = jnp.sum(acc_ref[...], axis=1, keepdims=True) * (0.5 / D)
            loss_ref[...] = jnp.broadcast_to(tot, loss_ref.shape)

    return pl.pallas_call(
        body, grid=(nt,), in_specs=[_row(D), _vec(D), _row(D)],
        out_specs=[_vec(128), _row(D), _vec(D)],
        out_shape=[SDS((1, 128), F32), SDS((t, D), F32), SDS((1, D), F32)],
        scratch_shapes=[pltpu.VMEM((1, D), F32)],
        compiler_params=_params(("arbitrary",)), name=name,
    )(x, fg, target)


def _halo_prev(width, col):
    per = TR // HALO
    return pl.BlockSpec((HALO, width), lambda i: (jnp.maximum(i * per - 1, 0), col))


def _halo_next(width, col, nt):
    per = TR // HALO
    return pl.BlockSpec((HALO, width), lambda i: (jnp.minimum((i + 1) * per, nt * per - 1), col))


def _pool_windows(ext, tile_index):
    rows = _iota((TR, PG), 0) + tile_index * TR + 1
    pooled, counts = [], []
    for gi in range(4):
        w = 2 << gi
        e = ext[:, gi * PG:(gi + 1) * PG]
        s = e
        step = 1
        while step < w:
            s = s + pltpu.roll(s, step, 0)
            step *= 2
        cnt = jnp.minimum(rows, w).astype(F32)
        pooled.append(s[HALO:] / cnt - e[HALO:])
        counts.append(cnt)
    return pooled, counts


def _pool_fwd(proj, pool_w, pool_scale, pool_proj, name):
    t = proj.shape[0]
    xcol = OFF_XP // PW

    def body(x_ref, h_ref, pw_ref, ps_ref, pp_ref, o_ref):
        i = pl.program_id(0)
        halo = jnp.where(i > 0, h_ref[...], 0.0)
        ext = jnp.concatenate([halo, x_ref[...]], axis=0)
        pooled, _ = _pool_windows(ext, i)
        mixed = [_dg(pooled[g].astype(BF16), pw_ref[g].astype(BF16), NN) for g in range(4)]
        ypre = jnp.concatenate(mixed, axis=1) * ps_ref[...]
        o_ref[...] = _dg(ypre.astype(BF16), pp_ref[...], NN)

    return pl.pallas_call(
        body, grid=(t // TR,),
        in_specs=[_row(PW, xcol), _halo_prev(PW, xcol),
                  pl.BlockSpec((4, PG, PG), lambda i: (0, 0, 0)), _vec(PW),
                  pl.BlockSpec((PW, D), lambda i: (0, 0))],
        out_specs=_row(D), out_shape=SDS((t, D), F32),
        compiler_params=_params(("parallel",)), name=name,
    )(proj, proj, pool_w, pool_scale, pool_proj)


def _pool_bwd_local(proj, pool_w, pool_scale, pool_proj, dya, name):
    t = proj.shape[0]
    xcol = OFF_XP // PW

    def body(x_ref, h_ref, pw_ref, ps_ref, pp_ref, dya_ref, dwin_ref, dpl_ref, dpw_ref, dps_ref, dpp_ref):
        i = pl.program_id(0)

        @pl.when(i == 0)
        def _():
            dpw_ref[...] = jnp.zeros_like(dpw_ref)
            dps_ref[...] = jnp.zeros_like(dps_ref)
            dpp_ref[...] = jnp.zeros_like(dpp_ref)

        halo = jnp.where(i > 0, h_ref[...], 0.0)
        ext = jnp.concatenate([halo, x_ref[...]], axis=0)
        pooled, counts = _pool_windows(ext, i)
        mixed = jnp.concatenate(
            [_dg(pooled[g].astype(BF16), pw_ref[g].astype(BF16), NN) for g in range(4)], axis=1)
        ps = ps_ref[...]
        ypre = mixed * ps
        dyab = dya_ref[...].astype(BF16)
        dypre = _dg(dyab, pp_ref[...], NT)
        dpp_ref[...] += _dg(ypre.astype(BF16), dyab, TN)
        dps_ref[...] += jnp.sum(dypre * mixed, axis=0, keepdims=True)
        dmixed = dypre * ps
        for g in range(4):
            dm = dmixed[:, g * PG:(g + 1) * PG].astype(BF16)
            dpw_ref[g] += _dg(pooled[g].astype(BF16), dm, TN)
            dpooled = _dg(dm, pw_ref[g].astype(BF16), NT)
            dwin_ref[:, g * PG:(g + 1) * PG] = dpooled / counts[g]
            dpl_ref[:, g * PG:(g + 1) * PG] = dpooled

    return pl.pallas_call(
        body, grid=(t // TR,),
        in_specs=[_row(PW, xcol), _halo_prev(PW, xcol),
                  pl.BlockSpec((4, PG, PG), lambda i: (0, 0, 0)), _vec(PW),
                  pl.BlockSpec((PW, D), lambda i: (0, 0)), _row(D)],
        out_specs=[_row(PW), _row(PW), pl.BlockSpec((4, PG, PG), lambda i: (0, 0, 0)), _vec(PW),
                   pl.BlockSpec((PW, D), lambda i: (0, 0))],
        out_shape=[SDS((t, PW), F32), SDS((t, PW), F32), SDS((4, PG, PG), F32), SDS((1, PW), F32),
                   SDS((PW, D), F32)],
        compiler_params=_params(("arbitrary",)), name=name,
    )(proj, proj, pool_w, pool_scale, pool_proj, dya)


def _pool_bwd_window(dwin, dpl, name):
    t = dwin.shape[0]
    nt = t // TR
    ext_rows = TR + HALO

    def body(dw_ref, h_ref, dp_ref, o_ref):
        i = pl.program_id(0)
        halo = jnp.where(i < nt - 1, h_ref[...], 0.0)
        ext = jnp.concatenate([dw_ref[...], halo], axis=0)
        for gi in range(4):
            w = 2 << gi
            s = ext[:, gi * PG:(gi + 1) * PG]
            step = 1
            while step < w:
                s = s + pltpu.roll(s, ext_rows - step, 0)
                step *= 2
            o_ref[:, gi * PG:(gi + 1) * PG] = (s[:TR] - dp_ref[:, gi * PG:(gi + 1) * PG]).astype(o_ref.dtype)

    return pl.pallas_call(
        body, grid=(nt,), in_specs=[_row(PW), _halo_next(PW, 0, nt), _row(PW)], out_specs=_row(PW),
        out_shape=SDS((t, PW), BF16), compiler_params=_params(("parallel",)), name=name,
    )(dwin, dwin, dpl)


def _conv_group(ext, cw_ref, cols):
    acc = cw_ref[3:4, cols] * ext
    for j in range(3):
        acc = acc + cw_ref[j:j + 1, cols] * pltpu.roll(ext, 3 - j, 0)
    return acc[HALO:]


def _gate_terms(raw, al, dt):
    beta = _sigmoid(raw)
    xg = raw + dt
    sp = jnp.maximum(xg, 0.0) + jnp.log(1.0 + jnp.exp(-jnp.abs(xg)))
    g = -jnp.exp(al) * sp
    return beta, g, _sigmoid(xg)


def _dn_pre_fwd(proj, conv_w, al_row, dt_row, name):
    t = proj.shape[0]

    def body(x_ref, h_ref, cw_ref, ba_ref, al_ref, dt_ref, q_ref, k_ref, v_ref, bg_ref):
        i = pl.program_id(0)
        keep = i > 0
        for grp in range(24):
            cols = slice(grp * HD, (grp + 1) * HD)
            ext = jnp.concatenate([jnp.where(keep, h_ref[:, cols], 0.0), x_ref[:, cols]], axis=0)
            s = _silu(_conv_group(ext, cw_ref, cols))
            seg, head = divmod(grp, NH)
            hc = slice(head * HD, (head + 1) * HD)
            if seg == 0:
                q_ref[:, hc] = s * lax.rsqrt(jnp.sum(s * s, axis=-1, keepdims=True) + L2_EPS) * (HD ** -0.5)
            elif seg == 1:
                k_ref[:, hc] = s * lax.rsqrt(jnp.sum(s * s, axis=-1, keepdims=True) + L2_EPS)
            else:
                v_ref[:, hc] = s
        lane = _iota((TR, 128), 1)
        rowc = _iota((TR, 128), 0) % CH
        beta, g, _ = _gate_terms(ba_ref[...], al_ref[...], dt_ref[...])
        step = 1
        while step < CH:
            g = g + jnp.where(rowc >= step, pltpu.roll(g, step, 0), 0.0)
            step *= 2
        bg_ref[...] = jnp.where(lane < NH, beta, jnp.where(lane < 2 * NH, g, 0.0))

    return pl.pallas_call(
        body, grid=(t // TR,),
        in_specs=[_row(3 * D, 0), _halo_prev(3 * D, 0), pl.BlockSpec((4, 3 * D), lambda i: (0, 0)),
                  _row(128, OFF_BA // 128), _vec(128), _vec(128)],
        out_specs=[_row(D), _row(D), _row(D), _row(128)],
        out_shape=[SDS((t, D), F32), SDS((t, D), F32), SDS((t, D), F32), SDS((t, 128), F32)],
        compiler_params=_params(("parallel",)), name=name,
    )(proj, proj, conv_w, proj, al_row, dt_row)


def _dn_pre_bwd_act(proj, conv_w, al_row, dt_row, dq, dk, dv, dbg, name):
    t = proj.shape[0]

    def body(x_ref, h_ref, cw_ref, ba_ref, al_ref, dt_ref, dq_ref, dk_ref, dv_ref, dbg_ref,
             dc_ref, draw_ref, dal_ref, ddt_ref):
        i = pl.program_id(0)

        @pl.when(i == 0)
        def _():
            dal_ref[...] = jnp.zeros_like(dal_ref)
            ddt_ref[...] = jnp.zeros_like(ddt_ref)

        keep = i > 0
        for grp in range(24):
            cols = slice(grp * HD, (grp + 1) * HD)
            ext = jnp.concatenate([jnp.where(keep, h_ref[:, cols], 0.0), x_ref[:, cols]], axis=0)
            cv = _conv_group(ext, cw_ref, cols)
            seg, head = divmod(grp, NH)
            hc = slice(head * HD, (head + 1) * HD)
            if seg == 2:
                ds = dv_ref[:, hc]
            else:
                s = _silu(cv)
                r = lax.rsqrt(jnp.sum(s * s, axis=-1, keepdims=True) + L2_EPS)
                dy = dq_ref[:, hc] if seg == 0 else dk_ref[:, hc]
                c = (HD ** -0.5) if seg == 0 else 1.0
                ds = (c * r) * (dy - s * ((r * r) * jnp.sum(dy * s, axis=-1, keepdims=True)))
            dc_ref[:, cols] = ds * _dsilu(cv)
        lane = _iota((TR, 128), 1)
        rowc = _iota((TR, 128), 0) % CH
        isb = lane < NH
        isg = jnp.logical_and(lane >= NH, lane < 2 * NH)
        beta, g, sg = _gate_terms(ba_ref[...], al_ref[...], dt_ref[...])
        dbgv = dbg_ref[...]
        dg = dbgv
        step = 1
        while step < CH:
            dg = dg + jnp.where(rowc < CH - step, pltpu.roll(dg, TR - step, 0), 0.0)
            step *= 2
        da_raw = dg * (-jnp.exp(al_ref[...])) * sg
        draw_ref[...] = jnp.where(isb, dbgv * beta * (1.0 - beta), jnp.where(isg, da_raw, 0.0)).astype(draw_ref.dtype)
        dal_ref[...] += jnp.sum(jnp.where(isg, dg * g, 0.0), axis=0, keepdims=True)
        ddt_ref[...] += jnp.sum(jnp.where(isg, da_raw, 0.0), axis=0, keepdims=True)

    return pl.pallas_call(
        body, grid=(t // TR,),
        in_specs=[_row(3 * D, 0), _halo_prev(3 * D, 0), pl.BlockSpec((4, 3 * D), lambda i: (0, 0)),
                  _row(128, OFF_BA // 128), _vec(128), _vec(128), _row(D), _row(D), _row(D), _row(128)],
        out_specs=[_row(3 * D), _row(128), _vec(128), _vec(128)],
        out_shape=[SDS((t, 3 * D), F32), SDS((t, 128), BF16), SDS((1, 128), F32), SDS((1, 128), F32)],
        compiler_params=_params(("arbitrary",)), name=name,
    )(proj, proj, conv_w, proj, al_row, dt_row, dq, dk, dv, dbg)


def _dn_pre_bwd_conv(proj, conv_w, dconv, name):
    t = proj.shape[0]
    nt = t // TR
    ext_rows = TR + HALO

    def body(x_ref, h_ref, cw_ref, dc_ref, dn_ref, dx_ref, dcw_ref):
        i = pl.program_id(0)

        @pl.when(i == 0)
        def _():
            dcw_ref[...] = jnp.zeros_like(dcw_ref)

        keep_prev = i > 0
        keep_next = i < nt - 1
        for grp in range(24):
            cols = slice(grp * HD, (grp + 1) * HD)
            dct = dc_ref[:, cols]
            dext = jnp.concatenate([dct, jnp.where(keep_next, dn_ref[:, cols], 0.0)], axis=0)
            acc = cw_ref[3:4, cols] * dext
            for j in range(3):
                acc = acc + cw_ref[j:j + 1, cols] * pltpu.roll(dext, ext_rows - (3 - j), 0)
            dx_ref[:, cols] = acc[:TR].astype(dx_ref.dtype)
            xext = jnp.concatenate([jnp.where(keep_prev, h_ref[:, cols], 0.0), x_ref[:, cols]], axis=0)
            for j in range(4):
                xs = xext if j == 3 else pltpu.roll(xext, 3 - j, 0)
                dcw_ref[j:j + 1, cols] += jnp.sum(xs[HALO:] * dct, axis=0, keepdims=True)

    return pl.pallas_call(
        body, grid=(nt,),
        in_specs=[_row(3 * D, 0), _halo_prev(3 * D, 0), pl.BlockSpec((4, 3 * D), lambda i: (0, 0)),
                  _row(3 * D), _halo_next(3 * D, 0, nt)],
        out_specs=[_row(3 * D), pl.BlockSpec((4, 3 * D), lambda i: (0, 0))],
        out_shape=[SDS((t, 3 * D), BF16), SDS((4, 3 * D), F32)],
        compiler_params=_params(("arbitrary",)), name=name,
    )(proj, proj, conv_w, dconv, dconv)


def _dn_post_fwd(o, proj, gn, name):
    t = o.shape[0]

    def body(o_ref, z_ref, g_ref, out_ref):
        gv = g_ref[...]
        for h in range(NH):
            hc = slice(h * HD, (h + 1) * HD)
            ov = o_ref[:, hc]
            r = lax.rsqrt(jnp.mean(ov * ov, axis=-1, keepdims=True) + RMS_EPS)
            out_ref[:, hc] = (((ov * r) * gv) * _silu(z_ref[:, hc])).astype(out_ref.dtype)

    return pl.pallas_call(
        body, grid=(t // TR,), in_specs=[_row(D), _row(D, OFF_Z // D), _vec(HD)], out_specs=_row(D),
        out_shape=SDS((t, D), BF16), compiler_params=_params(("parallel",)), name=name,
    )(o, proj, gn)


def _dn_post_bwd(o, proj, gn, dob, name):
    t = o.shape[0]

    def body(o_ref, z_ref, g_ref, d_ref, do_ref, dz_ref, dg_ref):
        @pl.when(pl.program_id(0) == 0)
        def _():
            dg_ref[...] = jnp.zeros_like(dg_ref)

        gv = g_ref[...]
        acc = jnp.zeros((1, HD), F32)
        for h in range(NH):
            hc = slice(h * HD, (h + 1) * HD)
            ov = o_ref[:, hc]
            zv = z_ref[:, hc]
            dv = d_ref[:, hc]
            r = lax.rsqrt(jnp.mean(ov * ov, axis=-1, keepdims=True) + RMS_EPS)
            n = ov * r
            dz_ref[:, hc] = (dv * (n * gv) * _dsilu(zv)).astype(dz_ref.dtype)
            dng = dv * _silu(zv)
            acc = acc + jnp.sum(dng * n, axis=0, keepdims=True)
            dn = dng * gv
            do_ref[:, hc] = r * (dn - n * jnp.mean(dn * n, axis=-1, keepdims=True))
        dg_ref[...] += acc

    return pl.pallas_call(
        body, grid=(t // TR,), in_specs=[_row(D), _row(D, OFF_Z // D), _vec(HD), _row(D)],
        out_specs=[_row(D), _row(D), _vec(HD)],
        out_shape=[SDS((t, D), F32), SDS((t, D), BF16), SDS((1, HD), F32)],
        compiler_params=_params(("arbitrary",)), name=name,
    )(o, proj, gn, dob)


def _merge_fwd(ya, yb, proj, name):
    t = ya.shape[0]

    def body(a_ref, b_ref, gp_ref, gd_ref, o_ref):
        o_ref[...] = (_sigmoid(gp_ref[...]) * a_ref[...] + _sigmoid(gd_ref[...]) * b_ref[...]).astype(o_ref.dtype)

    return pl.pallas_call(
        body, grid=(t // TR,), in_specs=[_row(D), _row(D), _row(D, OFF_GP // D), _row(D, OFF_GD // D)],
        out_specs=_row(D), out_shape=SDS((t, D), BF16),
        compiler_params=_params(("parallel",)), name=name,
    )(ya, yb, proj, proj)


def _merge_bwd(dm, ya, yb, proj, name):
    t = ya.shape[0]

    def body(d_ref, a_ref, b_ref, gp_ref, gd_ref, da_ref, db_ref, dgp_ref, dgd_ref):
        dv = d_ref[...]
        sp = _sigmoid(gp_ref[...])
        sd = _sigmoid(gd_ref[...])
        da_ref[...] = dv * sp
        db_ref[...] = (dv * sd).astype(db_ref.dtype)
        dgp_ref[...] = (dv * a_ref[...] * sp * (1.0 - sp)).astype(dgp_ref.dtype)
        dgd_ref[...] = (dv * b_ref[...] * sd * (1.0 - sd)).astype(dgd_ref.dtype)

    return pl.pallas_call(
        body, grid=(t // TR,),
        in_specs=[_row(D), _row(D), _row(D), _row(D, OFF_GP // D), _row(D, OFF_GD // D)],
        out_specs=[_row(D)] * 4,
        out_shape=[SDS((t, D), F32), SDS((t, D), BF16), SDS((t, D), BF16), SDS((t, D), BF16)],
        compiler_params=_params(("parallel",)), name=name,
    )(dm, ya, yb, proj, proj)


def _split2(x):
    hi = x.astype(BF16)
    return hi, (x - hi.astype(F32)).astype(BF16)


def _dot3(a, b, dims):
    ah, al = _split2(a)
    bh, bl = _split2(b)
    return _dg(ah, bh, dims) + (_dg(ah, bl, dims) + _dg(al, bh, dims))


def _neumann_inverses(mats):
    ri = _iota((CH, CH), 0)
    ci = _iota((CH, CH), 1)
    eye = jnp.where(ri == ci, 1.0, 0.0).astype(F32)
    xs = [-a for a in mats]
    ps = [eye + x for x in xs]
    for _ in range(5):
        xs = [_dot3(x, x, NN) for x in xs]
        ps = [p + _dot3(p, x, NN) for p, x in zip(ps, xs)]
    return ps


def _solve_with(inv):
    @jax.custom_vjp
    def solve(a, rhs):
        return _dot3(inv, rhs, NN)

    def fwd(a, rhs):
        sol = _dot3(inv, rhs, NN)
        return sol, sol

    def bwd(sol, d):
        drhs = _dot3(inv, d, TN)
        return -_dot3(drhs, sol, NT), drhs

    solve.defvjp(fwd, bwd)
    return solve


@jax.custom_vjp
def _rows_to_lanes(g64):
    ri = _iota((CH, CH), 0)
    ci = _iota((CH, CH), 1)
    diag = jnp.where(ri == ci, g64, 0.0)
    ones = jnp.ones((CH, CH), BF16)
    hi = diag.astype(BF16)
    rem = diag - hi.astype(F32)
    mid = rem.astype(BF16)
    lo = (rem - mid.astype(F32)).astype(BF16)
    return _dg(ones, hi, NN) + (_dg(ones, mid, NN) + _dg(ones, lo, NN))


def _rows_to_lanes_bwd(_, d):
    ri = _iota((CH, CH), 0)
    ci = _iota((CH, CH), 1)
    return (jnp.where(ri == ci, jnp.broadcast_to(jnp.sum(d, axis=0, keepdims=True), (CH, CH)), 0.0),)


_rows_to_lanes.defvjp(lambda g64: (_rows_to_lanes(g64), None), _rows_to_lanes_bwd)


def _chunk_local(solve_all, q, k, v, g128, g64, gl128, b128, b64):
    ri = _iota((CH, CH), 0)
    ci = _iota((CH, CH), 1)
    causal = ri >= ci
    strict = ri > ci
    gj = [_rows_to_lanes(g) for g in g64]
    decay = [jnp.where(causal, jnp.exp(jnp.where(causal, g - t, 0.0)), 0.0) for g, t in zip(g64, gj)]
    kk = [_nt(x, x) for x in k]
    a = [jnp.where(strict, b * m * dc, 0.0) for b, m, dc in zip(b64, kk, decay)]
    eg = [jnp.exp(g) for g in g128]
    rhs = [jnp.concatenate([b * x, (b * e) * y], axis=1) for b, x, e, y in zip(b128, v, eg, k)]
    sol = solve_all(a, rhs)
    qk = [jnp.where(causal, _nt(x, y) * dc, 0.0) for x, y, dc in zip(q, k, decay)]
    return ([s[:, :HD] for s in sol], [s[:, HD:] for s in sol], qk, [x * e for x, e in zip(q, eg)],
            [x * jnp.exp(gl - g) for x, gl, g in zip(k, gl128, g128)], [jnp.exp(gl) for gl in gl128])


def _all_head_gates(bgv):
    return tuple(list(z) for z in zip(*[_head_gates(bgv, h) for h in range(NH)]))


def _head_gates(bgv, h):
    lane = _iota((CH, 128), 1)
    row = _iota((CH, 128), 0)
    bcol = jnp.sum(jnp.where(lane == h, bgv, 0.0), axis=1, keepdims=True)
    gcol = jnp.sum(jnp.where(lane == NH + h, bgv, 0.0), axis=1, keepdims=True)
    g128 = jnp.broadcast_to(gcol, (CH, 128))
    gl128 = jnp.broadcast_to(jnp.sum(jnp.where(row == CH - 1, g128, 0.0), axis=0, keepdims=True), (CH, 128))
    return (g128, jnp.broadcast_to(gcol, (CH, CH)), gl128,
            jnp.broadcast_to(bcol, (CH, 128)), jnp.broadcast_to(bcol, (CH, CH)))


def _chunk_specs():
    row = pl.BlockSpec((CH, D), lambda i: (i, 0))
    small = pl.BlockSpec((CH, 128), lambda i: (i, 0))
    qk = pl.BlockSpec((NH, CH, CH), lambda i: (i, 0, 0))
    eg = pl.BlockSpec((1, NH, 128), lambda i: (i, 0, 0))
    return row, small, qk, eg


def _dn_local_fwd(q, k, v, bg, name):
    t = q.shape[0]
    n = t // CH

    def body(q_ref, k_ref, v_ref, bg_ref, u_ref, w_ref, qk_ref, qd_ref, kd_ref, eg_ref, inv_ref):
        cols = [slice(h * HD, (h + 1) * HD) for h in range(NH)]

        def solve_all(mats, rhs):
            invs = _neumann_inverses(mats)
            for h in range(NH):
                inv_ref[h] = invs[h]
            return [_dot3(m, r, NN) for m, r in zip(invs, rhs)]

        u, w, qk, qd, kd, egl = _chunk_local(
            solve_all, [q_ref[:, c] for c in cols], [k_ref[:, c] for c in cols], [v_ref[:, c] for c in cols],
            *_all_head_gates(bg_ref[...]))
        for h, hc in enumerate(cols):
            u_ref[:, hc] = u[h]
            w_ref[:, hc] = w[h].astype(w_ref.dtype)
            qd_ref[:, hc] = qd[h].astype(qd_ref.dtype)
            kd_ref[:, hc] = kd[h].astype(kd_ref.dtype)
            qk_ref[h] = qk[h].astype(qk_ref.dtype)
            eg_ref[0, h:h + 1, :] = egl[h][0:1, :]

    row, small, qkb, egb = _chunk_specs()
    return pl.pallas_call(
        body, grid=(n,), in_specs=[row, row, row, small], out_specs=[row, row, qkb, row, row, egb, qkb],
        out_shape=[SDS((t, D), F32), SDS((t, D), BF16), SDS((n * NH, CH, CH), BF16), SDS((t, D), BF16),
                   SDS((t, D), BF16), SDS((n, NH, 128), F32), SDS((n * NH, CH, CH), F32)],
        compiler_params=_params(("parallel",)), name=name,
    )(q, k, v, bg)


def _dn_local_bwd(q, k, v, bg, inv, du, dw, dqk, dqd, dkd, deg, name):
    t = q.shape[0]
    n = t // CH

    def body(q_ref, k_ref, v_ref, bg_ref, inv_ref, du_ref, dw_ref, dqk_ref, dqd_ref, dkd_ref, deg_ref,
             dq_ref, dk_ref, dv_ref, dbg_ref):
        bgv = bg_ref[...]
        lane = _iota((CH, 128), 1)
        row = _iota((CH, 128), 0)
        first = jnp.where(row == 0, 1.0, 0.0)
        acc = jnp.zeros((CH, 128), F32)
        cols = [slice(h * HD, (h + 1) * HD) for h in range(NH)]
        solves = [_solve_with(inv_ref[h]) for h in range(NH)]

        def solve_all(mats, rhs):
            return [f(m, r) for f, m, r in zip(solves, mats, rhs)]

        _, vjp = jax.vjp(functools.partial(_chunk_local, solve_all),
                         [q_ref[:, c] for c in cols], [k_ref[:, c] for c in cols], [v_ref[:, c] for c in cols],
                         *_all_head_gates(bgv))
        cts = ([du_ref[:, c].astype(F32) for c in cols], [dw_ref[:, c].astype(F32) for c in cols],
               [dqk_ref[h] for h in range(NH)],
               [dqd_ref[:, c].astype(F32) for c in cols], [dkd_ref[:, c].astype(F32) for c in cols],
               [jnp.broadcast_to(deg_ref[0, h:h + 1, :], (CH, 128)) * first for h in range(NH)])
        dq, dk, dv, dg128, dg64, dgl, db128, db64 = vjp(cts)
        for h, hc in enumerate(cols):
            dq_ref[:, hc] = dq[h]
            dk_ref[:, hc] = dk[h]
            dv_ref[:, hc] = dv[h]
            dg = jnp.sum(dg128[h], axis=1, keepdims=True) + jnp.sum(dg64[h], axis=1, keepdims=True)
            tot = jnp.sum(jnp.sum(dgl[h], axis=0, keepdims=True), axis=1, keepdims=True)
            dg = dg + jnp.where(row[:, 0:1] == CH - 1, tot, 0.0)
            db = jnp.sum(db128[h], axis=1, keepdims=True) + jnp.sum(db64[h], axis=1, keepdims=True)
            acc = acc + jnp.where(lane == h, db, 0.0) + jnp.where(lane == NH + h, dg, 0.0)
        dbg_ref[...] = acc

    row, small, qkb, egb = _chunk_specs()
    return pl.pallas_call(
        body, grid=(n,), in_specs=[row, row, row, small, qkb, row, row, qkb, row, row, egb],
        out_specs=[row, row, row, small],
        out_shape=[SDS((t, D), F32)] * 3 + [SDS((t, 128), F32)],
        compiler_params=_params(("parallel",)), name=name,
    )(q, k, v, bg, inv, du, dw, dqk, dqd, dkd, deg)


def _state_step(s, u, w, qk, qd, kd, egl):
    ws = [_nn(a, b) for a, b in zip(w, s)]
    v_new = [a - b for a, b in zip(u, ws)]
    qs = [_nn(a, b) for a, b in zip(qd, s)]
    intra = [_nn(a, b) for a, b in zip(qk, v_new)]
    upd = [_tn(a, b) for a, b in zip(kd, v_new)]
    return [a * e + b for a, e, b in zip(s, egl, upd)], [a + b for a, b in zip(qs, intra)]


def _dn_scan_fwd(u, w, qk, qd, kd, eg, name):
    t = u.shape[0]
    n = t // CH
    g = SCAN_CHUNKS

    def body(u_ref, w_ref, qk_ref, qd_ref, kd_ref, eg_ref, o_ref, save_ref, s_ref):
        @pl.when(pl.program_id(0) == 0)
        def _():
            s_ref[...] = jnp.zeros_like(s_ref)

        cols = [slice(h * HD, (h + 1) * HD) for h in range(NH)]
        s = [s_ref[h] for h in range(NH)]
        for c in range(g):
            rows = slice(c * CH, (c + 1) * CH)
            for h in range(NH):
                save_ref[c, h] = s[h].astype(save_ref.dtype)
            s, o = _state_step(
                s, [u_ref[rows, hc] for hc in cols], [w_ref[rows, hc].astype(F32) for hc in cols],
                [qk_ref[c * NH + h].astype(F32) for h in range(NH)], [qd_ref[rows, hc].astype(F32) for hc in cols],
                [kd_ref[rows, hc].astype(F32) for hc in cols], [eg_ref[c, h:h + 1, :] for h in range(NH)])
            for h, hc in enumerate(cols):
                o_ref[rows, hc] = o[h]
        for h in range(NH):
            s_ref[h] = s[h]

    row = pl.BlockSpec((g * CH, D), lambda i: (i, 0))
    qkb = pl.BlockSpec((g * NH, CH, CH), lambda i: (i, 0, 0))
    egb = pl.BlockSpec((g, NH, 128), lambda i: (i, 0, 0))
    return pl.pallas_call(
        body, grid=(n // g,), in_specs=[row, row, qkb, row, row, egb],
        out_specs=[row, pl.BlockSpec((g, NH, HD, HD), lambda i: (i, 0, 0, 0))],
        out_shape=[SDS((t, D), F32), SDS((n, NH, HD, HD), BF16)],
        scratch_shapes=[pltpu.VMEM((NH, HD, HD), F32)],
        compiler_params=_params(("arbitrary",)), name=name,
    )(u, w, qk, qd, kd, eg)


def _dn_scan_bwd(u, w, qk, qd, kd, eg, saved, do, name):
    t = u.shape[0]
    n = t // CH
    g = SCAN_CHUNKS
    last = n // g - 1

    def body(u_ref, w_ref, qk_ref, qd_ref, kd_ref, eg_ref, sv_ref, do_ref,
             du_ref, dw_ref, dqk_ref, dqd_ref, dkd_ref, deg_ref, ds_ref):
        @pl.when(pl.program_id(0) == 0)
        def _():
            ds_ref[...] = jnp.zeros_like(ds_ref)

        cols = [slice(h * HD, (h + 1) * HD) for h in range(NH)]
        ds = [ds_ref[h] for h in range(NH)]
        for c in reversed(range(g)):
            rows = slice(c * CH, (c + 1) * CH)
            _, vjp = jax.vjp(
                _state_step, [sv_ref[c, h].astype(F32) for h in range(NH)], [u_ref[rows, hc] for hc in cols],
                [w_ref[rows, hc].astype(F32) for hc in cols], [qk_ref[c * NH + h].astype(F32) for h in range(NH)],
                [qd_ref[rows, hc].astype(F32) for hc in cols], [kd_ref[rows, hc].astype(F32) for hc in cols],
                [eg_ref[c, h:h + 1, :] for h in range(NH)])
            ds, du, dw, dqk, dqd, dkd, deg = vjp((ds, [do_ref[rows, hc] for hc in cols]))
            for h, hc in enumerate(cols):
                du_ref[rows, hc] = du[h].astype(du_ref.dtype)
                dw_ref[rows, hc] = dw[h].astype(dw_ref.dtype)
                dqk_ref[c * NH + h] = dqk[h]
                dqd_ref[rows, hc] = dqd[h].astype(dqd_ref.dtype)
                dkd_ref[rows, hc] = dkd[h].astype(dkd_ref.dtype)
                deg_ref[c, h:h + 1, :] = deg[h]
        for h in range(NH):
            ds_ref[h] = ds[h]

    row = pl.BlockSpec((g * CH, D), lambda i: (last - i, 0))
    qkb = pl.BlockSpec((g * NH, CH, CH), lambda i: (last - i, 0, 0))
    egb = pl.BlockSpec((g, NH, 128), lambda i: (last - i, 0, 0))
    return pl.pallas_call(
        body, grid=(n // g,),
        in_specs=[row, row, qkb, row, row, egb,
                  pl.BlockSpec((g, NH, HD, HD), lambda i: (last - i, 0, 0, 0)), row],
        out_specs=[row, row, qkb, row, row, egb],
        out_shape=[SDS((t, D), BF16), SDS((t, D), BF16), SDS((n * NH, CH, CH), F32), SDS((t, D), BF16),
                   SDS((t, D), BF16), SDS((n, NH, 128), F32)],
        scratch_shapes=[pltpu.VMEM((NH, HD, HD), F32)],
        compiler_params=_params(("arbitrary",)), name=name,
    )(u, w, qk, qd, kd, eg, saved, do)


def _ada_fwd(c_all, ada_w, ada_b, name):
    ncol = ada_w.shape[1]

    def body(c_ref, w_ref, b_ref, o_ref):
        o_ref[...] = _dg(_silu(c_ref[...]), w_ref[...], NN, HI) + b_ref[...]

    return pl.pallas_call(body, out_shape=SDS((NDEV, ncol), F32),
                          compiler_params=pltpu.CompilerParams(vmem_limit_bytes=VMEM_LIMIT), name=name,
                          )(c_all, ada_w, ada_b)


def _ada_bwd(c_all_t, dmod, name):
    ncol = dmod.shape[1]

    def body(c_ref, d_ref, o_ref):
        sc = _silu(c_ref[...])
        acc = sc[:, 0:1] * d_ref[0:1, :]
        for b in range(1, NDEV):
            acc = acc + sc[:, b:b + 1] * d_ref[b:b + 1, :]
        o_ref[...] = acc

    return pl.pallas_call(body, out_shape=SDS((D, ncol), F32),
                          compiler_params=pltpu.CompilerParams(vmem_limit_bytes=VMEM_LIMIT), name=name,
                          )(c_all_t, dmod)


def _sum_devices(parts, out_dtype, name):
    _, r, c = parts.shape
    tr = TR if r % TR == 0 else r

    def body(p_ref, o_ref):
        acc = p_ref[0].astype(F32)
        for i in range(1, NDEV):
            acc = acc + p_ref[i].astype(F32)
        o_ref[...] = acc.astype(o_ref.dtype)

    return pl.pallas_call(
        body, grid=(r // tr,), in_specs=[pl.BlockSpec((NDEV, tr, c), lambda i: (0, i, 0))],
        out_specs=pl.BlockSpec((tr, c), lambda i: (i, 0)), out_shape=SDS((r, c), out_dtype),
        compiler_params=_params(("parallel",)), name=name,
    )(parts)


def _adam_tiles(r, c):
    if r % 8 == 0:
        return _pick(r, (256, 352, 128, 8)), c
    return r, (256 if c % 256 == 0 else c)


def _adam_math(w, gv, m, v):
    m_new = ADAM_B1 * m + (1.0 - ADAM_B1) * gv
    v_new = ADAM_B2 * v + (1.0 - ADAM_B2) * (gv * gv)
    bc1 = 1.0 - ADAM_B1 ** ADAM_STEP
    bc2 = 1.0 - ADAM_B2 ** ADAM_STEP
    return -ADAM_LR * ((m_new / bc1) / (jnp.sqrt(v_new / bc2) + ADAM_EPS) + ADAM_WD * w), m_new, v_new


def _adamw(w, g, m, v, name):
    r, c = w.shape
    tr, tc = _adam_tiles(r, c)

    def body(w_ref, g_ref, m_ref, v_ref, d_ref, nm_ref, nv_ref):
        d_ref[...], nm_ref[...], nv_ref[...] = _adam_math(w_ref[...], g_ref[...], m_ref[...], v_ref[...])

    spec = pl.BlockSpec((tr, tc), lambda i, j: (i, j))
    return pl.pallas_call(
        body, grid=(r // tr, c // tc), in_specs=[spec] * 4, out_specs=[spec] * 3,
        out_shape=[SDS((r, c), F32)] * 3, compiler_params=_params(("parallel", "parallel")), name=name,
    )(w, g, m, v)


def _reduce_adamw(parts, w, m, v, name):
    r, c = w.shape
    tr, tc = _adam_tiles(r, c)

    def body(p_ref, w_ref, m_ref, v_ref, g_ref, d_ref, nm_ref, nv_ref):
        gv = p_ref[0].astype(F32)
        for i in range(1, NDEV):
            gv = gv + p_ref[i].astype(F32)
        g_ref[...] = gv
        d_ref[...], nm_ref[...], nv_ref[...] = _adam_math(w_ref[...], gv, m_ref[...], v_ref[...])

    spec = pl.BlockSpec((tr, tc), lambda i, j: (i, j))
    return pl.pallas_call(
        body, grid=(r // tr, c // tc),
        in_specs=[pl.BlockSpec((NDEV, tr, tc), lambda i, j: (0, i, j))] + [spec] * 3, out_specs=[spec] * 4,
        out_shape=[SDS((r, c), F32)] * 4, compiler_params=_params(("parallel", "parallel")), name=name,
    )(parts, w, m, v)


ANY = pl.BlockSpec(memory_space=pl.ANY)
MESH = pl.DeviceIdType.MESH


def _all_gather(xs, name, after=None):
    n = len(xs)
    extra = [] if after is None else [after]

    def body(*refs):
        x_refs, out_refs = refs[:n], refs[n + len(extra):2 * n + len(extra)]
        send_sems, recv_sems, local_sems = refs[-3:]
        mx, my, mc = lax.axis_index("x"), lax.axis_index("y"), lax.axis_index("c")
        me, sibling = (mx, my, mc), (mx, my, 1 - mc)
        chips = [(1 - mx, my), (mx, 1 - my), (1 - mx, 1 - my)]

        def rows(a, px, py, pc):
            return out_refs[a].at[4 * px + 2 * py + pc]

        def copy(a, k, block, to, src=None):
            return pltpu.make_async_remote_copy(
                src_ref=rows(a, *block) if src is None else src, dst_ref=rows(a, *block),
                send_sem=send_sems.at[a, k], recv_sem=recv_sems.at[a, k], device_id=to, device_id_type=MESH)

        mine = [pltpu.make_async_copy(x_refs[a], rows(a, *me), local_sems.at[a]) for a in range(n)]
        for cp in mine:
            cp.start()
        first = []
        for a in range(n):
            first.append(copy(a, 0, me, sibling, src=x_refs[a]))
            first += [copy(a, 1 + j, me, (*chip, mc), src=x_refs[a]) for j, chip in enumerate(chips)]
        for cp in first:
            cp.start()
        passed = []
        for a in range(n):
            for j, chip in enumerate(chips):
                copy(a, 1 + j, (*chip, mc), me).wait_recv()
                passed.append(copy(a, 4 + j, (*chip, mc), sibling))
                passed[-1].start()
        for a in range(n):
            copy(a, 0, sibling, me).wait_recv()
            for j, chip in enumerate(chips):
                copy(a, 4 + j, (*chip, 1 - mc), me).wait_recv()
        for cp in first + passed:
            cp.wait_send()
        for cp in mine:
            cp.wait()

    return pl.pallas_call(
        body, out_shape=[SDS((NDEV,) + x.shape, x.dtype) for x in xs], in_specs=[ANY] * (n + len(extra)),
        out_specs=[ANY] * n,
        scratch_shapes=[pltpu.SemaphoreType.DMA((n, 7)), pltpu.SemaphoreType.DMA((n, 7)),
                        pltpu.SemaphoreType.DMA((n,))],
        name=name,
    )(*xs, *extra)


HBM = pl.BlockSpec(memory_space=pltpu.HBM)
SEM = pl.BlockSpec(memory_space=pltpu.SEMAPHORE)
EFFECT = pltpu.SideEffectType.DATAFLOW_SIDE_EFFECTING


def _peers():
    mx, my, mc = lax.axis_index("x"), lax.axis_index("y"), lax.axis_index("c")
    out = []
    for k in range(1, NDEV):
        out.append((1 - mx if k & 4 else mx, 1 - my if k & 2 else my, 1 - mc if k & 1 else mc))
    return 4 * mx + 2 * my + mc, out


def _push_start(srcs, sliced, name, after=None):
    n = len(srcs)
    extra = [] if after is None else [after]
    lands = [lax.empty(s.shape if sliced else (NDEV,) + s.shape, s.dtype) for s in srcs]

    def body(*refs):
        src_refs, land_refs = refs[:n], refs[n:2 * n]
        outs = refs[2 * n + len(extra):]
        send_sems, recv_sems = outs[:n], outs[n:2 * n]
        token = refs[-1]
        me, peers = _peers()
        for a in range(n):
            for k, (px, py, pc) in enumerate(peers):
                src = src_refs[a].at[4 * px + 2 * py + pc] if sliced else src_refs[a]
                pltpu.make_async_remote_copy(
                    src_ref=src, dst_ref=land_refs[a].at[me], send_sem=send_sems[a].at[k],
                    recv_sem=recv_sems[a].at[k], device_id=(px, py, pc), device_id_type=MESH).start()
            pltpu.make_async_copy(src_refs[a].at[me] if sliced else src_refs[a], land_refs[a].at[me],
                                  send_sems[a].at[NDEV - 1]).start()
        token[...] = jnp.zeros_like(token)

    outs = pl.pallas_call(
        body, name=name,
        out_shape=([pltpu.SemaphoreType.DMA((NDEV,))] * n + [pltpu.SemaphoreType.DMA((NDEV - 1,))] * n
                   + [pltpu.HBM(s.shape, s.dtype) for s in srcs] + [pltpu.HBM(l.shape, l.dtype) for l in lands]
                   + [SDS((8, 128), F32)]),
        in_specs=[HBM] * (2 * n) + [pl.BlockSpec(memory_space=pl.ANY)] * len(extra),
        out_specs=[SEM] * (2 * n) + [HBM] * (2 * n) + [pl.BlockSpec(memory_space=pltpu.VMEM)],
        input_output_aliases={i: 2 * n + i for i in range(2 * n)},
        compiler_params=pltpu.CompilerParams(has_side_effects=EFFECT),
    )(*[pltpu.with_memory_space_constraint(s, pltpu.HBM) for s in srcs],
      *[pltpu.with_memory_space_constraint(l, pltpu.HBM) for l in lands], *extra)
    sends, recvs = outs[:n], outs[n:2 * n]
    src_thru, land_thru = outs[2 * n:3 * n], outs[3 * n:4 * n]
    return [(sends[a], recvs[a], src_thru[a], land_thru[a]) for a in range(n)], outs[-1]


def _push_wait(started, sliced, after, name):
    n = len(started)

    def body(*refs):
        src_refs, land_refs = refs[:n], refs[n:2 * n]
        send_sems, recv_sems = refs[2 * n:3 * n], refs[3 * n:4 * n]
        me, peers = _peers()
        for a in range(n):
            for k, (px, py, pc) in enumerate(peers):
                src = src_refs[a].at[4 * px + 2 * py + pc] if sliced else src_refs[a]
                cp = pltpu.make_async_remote_copy(
                    src_ref=src, dst_ref=land_refs[a].at[me], send_sem=send_sems[a].at[k],
                    recv_sem=recv_sems[a].at[k], device_id=(px, py, pc), device_id_type=MESH)
                cp.wait_send()
                cp.wait_recv()
            pltpu.make_async_copy(src_refs[a].at[me] if sliced else src_refs[a], land_refs[a].at[me],
                                  send_sems[a].at[NDEV - 1]).wait()

    srcs = [s[2] for s in started]
    lands = [s[3] for s in started]
    outs = pl.pallas_call(
        body, name=name,
        out_shape=[pltpu.HBM(s.shape, s.dtype) for s in srcs] + [pltpu.HBM(l.shape, l.dtype) for l in lands],
        in_specs=[HBM] * (2 * n) + [SEM] * (2 * n) + [pl.BlockSpec(memory_space=pl.ANY)],
        out_specs=[HBM] * (2 * n),
        input_output_aliases={i: i for i in range(2 * n)},
        compiler_params=pltpu.CompilerParams(has_side_effects=EFFECT),
    )(*srcs, *lands, *[s[0] for s in started], *[s[1] for s in started], after)
    return outs[n:]


def _cols_from_blocks(blocks):
    _, rows, w = blocks.shape
    return blocks.transpose(1, 0, 2).reshape(rows, NDEV * w)


def _cols_to_blocks(full):
    rows, total = full.shape
    return full.reshape(rows, NDEV, total // NDEV).transpose(1, 0, 2)


def _mix_pad(wt):
    xp, q, k, v, z, ba, gp, gd = jnp.split(wt, (512, 1536, 2560, 3584, 4608, 4624, 5648), axis=0)
    pad = jnp.zeros((MIXP - OFF_BA - 16, wt.shape[1]), wt.dtype)
    return jnp.concatenate([q, k, v, z, gp, gd, xp, ba, pad], axis=0)


def _mix_unpad(wt):
    q, k, v, z, gp, gd, xp, ba = (wt[OFF_Q:OFF_K], wt[OFF_K:OFF_V], wt[OFF_V:OFF_Z], wt[OFF_Z:OFF_GP],
                                  wt[OFF_GP:OFF_GD], wt[OFF_GD:OFF_XP], wt[OFF_XP:OFF_BA], wt[OFF_BA:OFF_BA + 16])
    return jnp.concatenate([xp, q, k, v, z, ba, gp, gd], axis=0)


def _lane_row(vec8):
    return jnp.zeros((1, 128), F32).at[0, NH:2 * NH].set(vec8)


def _ffn_fwd(x, h, gate, w_in, w_out, tag, next_norm=None, token=None, start_more=None):
    if isinstance(w_in, tuple):
        w_in, = _push_wait([w_in], False, h, f"{tag}_gather_wait_in")
    w_in = w_in.reshape(2 * FH, D)
    u, a = _swiglu_up(h, w_in, f"{tag}_up", after=token)
    w_out, = _push_wait([w_out], False, a, f"{tag}_gather_wait_out")
    w_out = w_out.reshape(FH, D)
    outs = _matmul_residual(a, w_out, x, gate, 0.5, a_blk=True, norm=next_norm, name=f"{tag}_down",
                            after=None if start_more is None else start_more(h))
    return outs[0], (h, u, a, outs[1]), w_in, w_out, (outs[2] if next_norm else None)


def _ffn_bwd(dx_out, x, g, scale, gate, w_in, w_out, saved, tag):
    h, u, a, y = saved
    t = x.shape[0]
    dy, dgate = _resid_bwd(dx_out, y, gate, 0.5, f"{tag}_res_bwd")
    dw_out = _matmul(a, dy, ta=True, a_blk=True, out_dtype=BF16, name=f"{tag}_down_dw")
    sent_out, token = _push_start([dw_out.reshape(NDEV, FH // NDEV, D)], True, f"{tag}_grad_start_out")
    du = _swiglu_down_bwd(dy, w_out, u, f"{tag}_down_dx", after=token).reshape(NDEV, t, FB)
    dw_in = _matmul(du, h, ta=True, a_blk=True, out_dtype=BF16, name=f"{tag}_up_dw")
    sent_in, token = _push_start([dw_in.reshape(NDEV, FB, D)], True, f"{tag}_grad_start_in")
    dh = _matmul(du, w_in, a_blk=True, out_dtype=F32, name=f"{tag}_up_dx", after=token)
    dx, dshift, dscale, dg = _norm_mod_bwd(x, g, scale, dh, dx_out, f"{tag}_norm_bwd")
    return dx, (dshift, dscale, dgate), dg, sent_in + sent_out


def kernel(x, c, ada_w, ada_b, norm_g, ffn1_w_in, ffn1_w_out, ffn2_w_in, ffn2_w_out, mix_w_in, conv_w, a_log, dt_bias, dn_norm_g, pool_w, pool_scale, pool_proj, dn_proj, mix_w_out, final_g, loss_target, m_ada_w, m_ada_b, m_norm_g, m_ffn1_w_in, m_ffn1_w_out, m_ffn2_w_in, m_ffn2_w_out, m_mix_w_in, m_conv_w, m_a_log, m_dt_bias, m_dn_norm_g, m_pool_w, m_pool_scale, m_pool_proj, m_dn_proj, m_mix_w_out, m_final_g, v_ada_w, v_ada_b, v_norm_g, v_ffn1_w_in, v_ffn1_w_out, v_ffn2_w_in, v_ffn2_w_out, v_mix_w_in, v_conv_w, v_a_log, v_dt_bias, v_dn_norm_g, v_pool_w, v_pool_scale, v_pool_proj, v_dn_proj, v_mix_w_out, v_final_g):
    me = 4 * lax.axis_index("x") + 2 * lax.axis_index("y") + lax.axis_index("c")
    x0 = x[0]
    target = loss_target[0]
    t = x0.shape[0]

    big = [ffn1_w_in[0], ffn1_w_out[0], ffn2_w_in[0], ffn2_w_out[0], mix_w_in[0], pool_proj[0], dn_proj[0],
           mix_w_out[0]]
    small = jnp.concatenate([c.reshape(8, 128), conv_w[0].reshape(12, 128), norm_g[0].reshape(3, 128),
                             jnp.zeros((1, 128), F32)], axis=0)
    small_all, = _all_gather([small], "gather_small")
    c_all = small_all[:, 0:8, :].reshape(NDEV, D)
    conv_full = small_all[:, 8:20, :].reshape(NDEV, 4, 384).transpose(1, 0, 2).reshape(4, 3 * D)
    norm_full = small_all[:, 20:23, :].reshape(NDEV, 3, 128).transpose(1, 0, 2).reshape(3, D)

    ncol = ada_w.shape[2]
    ada_b_mine = lax.dynamic_slice(ada_b, (0, me * ncol), (1, ncol))
    mod_cols = _ada_fwd(c_all, ada_w[0], ada_b_mine, "ada_fwd")
    transposed = (0, 2, 4)
    payload = [(w.T if i in transposed else w).astype(BF16) for i, w in enumerate(big)]
    mod_all, w_in1 = _all_gather([mod_cols, payload[0]], "gather_mod_first_weight")
    started, token = _push_start([payload[1], payload[4]], False, "gather_start", after=mod_all)
    started = {1: started[0], 4: started[1]}

    def start_rest(h):
        more, token = _push_start([payload[i] for i in (5, 6, 7, 2, 3)], False, "gather_start_rest", after=h)
        started.update(zip((5, 6, 7, 2, 3), more))
        return token

    mod = lax.dynamic_index_in_dim(mod_all, me, axis=1, keepdims=False).reshape(9, D)
    shift = [mod[3 * s:3 * s + 1] for s in range(3)]
    scale = [mod[3 * s + 1:3 * s + 2] for s in range(3)]
    gate = [mod[3 * s + 2:3 * s + 3] for s in range(3)]
    ng = [norm_full[s:s + 1] for s in range(3)]
    fg = final_g.reshape(1, D)
    al_row = _lane_row(a_log[0])
    dt_row = _lane_row(dt_bias[0])
    gn = dn_norm_g
    pw = pool_w[0]
    ps = pool_scale

    h0 = _norm_mod_fwd(x0, ng[0], shift[0], scale[0], "ffn1_norm")
    x1, saved1, w_in1, w_out1, h1 = _ffn_fwd(x0, h0, gate[0], w_in1, started[1], "ffn1",
                                             (ng[1], shift[1], scale[1]), token, start_rest)

    seg, = _push_wait([started[4]], False, h1, "mix_gather_wait")
    w_mix = _mix_pad(seg.reshape(MIX_RAW, D))
    proj = _matmul(h1, w_mix, tb=True, out_dtype=F32, name="mix_in")
    qh, kh, vh, bg = _dn_pre_fwd(proj, conv_full, al_row, dt_row, "dn_pre")
    seg = _push_wait([started[i] for i in (5, 6, 7)], False, qh, "mix_gather_wait_rest")
    w_pp = _cols_from_blocks(seg[0])
    w_dn = seg[1].reshape(D, D)
    w_mo = seg[2].reshape(D, D)
    ya = _pool_fwd(proj, pw, ps, w_pp, "pool_fwd")
    u, w, qk, qd, kd, eg, inv = _dn_local_fwd(qh, kh, vh, bg, "dn_local")
    o, s_saved = _dn_scan_fwd(u, w, qk, qd, kd, eg, "dn_scan")
    ob = _dn_post_fwd(o, proj, gn, "dn_post")
    yb = _matmul(ob, w_dn, out_dtype=F32, name="dn_out")
    merged = _merge_fwd(ya, yb, proj, "merge")
    x2, mix_y, h2 = _matmul_residual(merged, w_mo, x1, gate[1], 1.0, norm=(ng[2], shift[2], scale[2]),
                                     name="mix_out")

    x3, saved2, w_in2, w_out2, _ = _ffn_fwd(x2, h2, gate[2], started[2], started[3], "ffn2")
    loss_row, dx3, dfg = _final_loss(x3, fg, target, "loss")

    dx2, dmod2, dng2, sent2 = _ffn_bwd(dx3, x2, ng[2], scale[2], gate[2], w_in2, w_out2, saved2, "ffn2")

    dmy, dgate1 = _resid_bwd(dx2, mix_y, gate[1], 1.0, "mix_res_bwd")
    dmerged = _matmul(dmy, w_mo, tb=True, out_dtype=F32, name="mix_out_dx")
    dw_mo = _matmul(merged, dmy, ta=True, out_dtype=BF16, name="mix_out_dw")
    dya, dyb, dgp, dgd = _merge_bwd(dmerged, ya, yb, proj, "merge_bwd")
    dob = _matmul(dyb, w_dn, tb=True, out_dtype=F32, name="dn_out_dx")
    dw_dn = _matmul(ob, dyb, ta=True, out_dtype=BF16, name="dn_out_dw")
    do, dz, dgn = _dn_post_bwd(o, proj, gn, dob, "dn_post_bwd")
    du, dw, dqk, dqd, dkd, deg = _dn_scan_bwd(u, w, qk, qd, kd, eg, s_saved, do, "dn_scan_bwd")
    dqh, dkh, dvh, dbg = _dn_local_bwd(qh, kh, vh, bg, inv, du, dw, dqk, dqd, dkd, deg, "dn_local_bwd")
    dconv, draw, dal, ddt = _dn_pre_bwd_act(proj, conv_full, al_row, dt_row, dqh, dkh, dvh, dbg, "dn_pre_bwd_act")
    dqkv, dcw = _dn_pre_bwd_conv(proj, conv_full, dconv, "dn_pre_bwd_conv")
    dwin, dpl, dpw, dps, dpp = _pool_bwd_local(proj, pw, ps, w_pp, dya, "pool_bwd_local")
    dxp = _pool_bwd_window(dwin, dpl, "pool_bwd_window")
    dproj = jnp.concatenate([dqkv, dz, dgp, dgd, dxp, draw, jnp.zeros((t, MIXP - OFF_BA - 128), BF16)], axis=1)
    dw_mix = _matmul(dproj, h1, ta=True, out_dtype=BF16, name="mix_in_dw")
    sent1, token = _push_start(
        [_mix_unpad(dw_mix).reshape(NDEV, MIX_RAW // NDEV, D), _cols_to_blocks(dpp.astype(BF16)),
         dw_dn.reshape(NDEV, -1, D), dw_mo.reshape(NDEV, -1, D)], True, "mix_grad_start")
    dh1 = _matmul(dproj, w_mix, out_dtype=F32, name="mix_in_dx", after=token)
    dx1, dsh1, dsc1, dng1 = _norm_mod_bwd(x1, ng[1], scale[1], dh1, dx2, "mix_norm_bwd")

    dx0, dmod0, dng0, sent0 = _ffn_bwd(dx1, x0, ng[0], scale[0], gate[0], w_in1, w_out1, saved1, "ffn1")

    dmod = jnp.concatenate([*dmod0, dsh1, dsc1, dgate1, *dmod2], axis=1).reshape(-1)
    flat = jnp.concatenate([
        dmod, dal[0, NH:2 * NH], ddt[0, NH:2 * NH], dgn.reshape(-1), dps.reshape(-1), dfg.reshape(-1),
        dpw.reshape(-1), jnp.concatenate([dng0, dng1, dng2], axis=0).reshape(-1), dcw.reshape(-1),
        loss_row[0, 0:1]])
    nflat = 90 * D
    flat = jnp.concatenate([flat, jnp.zeros((nflat - flat.shape[0],), F32)]).reshape(90, D)
    flat_all, = _all_gather([flat], "gather_small_grads")
    tot = _sum_devices(flat_all, F32, "sum_small_grads").reshape(-1)
    dmod_all = flat_all.reshape(NDEV, nflat)[:, :9 * D]
    dmod_cols = lax.dynamic_slice(dmod_all, (0, me * ncol), (NDEV, ncol))
    g_ada_w = _ada_bwd(c_all.T, dmod_cols, "ada_bwd")

    p = 0
    pieces = {}
    for nm, size in (("ada_b", 9 * D), ("a_log", NH), ("dt_bias", NH), ("dn_norm_g", HD), ("pool_scale", PW),
                     ("final_g", D), ("pool_w", 4 * PG * PG), ("norm_g", 3 * D), ("conv_w", 12 * D),
                     ("loss", 1)):
        pieces[nm] = tot[p:p + size]
        p += size
    g_norm = lax.dynamic_slice(pieces["norm_g"].reshape(3, D), (0, me * 128), (3, 128))
    g_conv = lax.dynamic_slice(pieces["conv_w"].reshape(4, 3 * D), (0, me * 384), (4, 384))

    grads = {
        "ada_w": g_ada_w.reshape(ada_w.shape), "ada_b": pieces["ada_b"].reshape(ada_b.shape),
        "norm_g": g_norm.reshape(norm_g.shape), "conv_w": g_conv.reshape(conv_w.shape),
        "a_log": pieces["a_log"].reshape(a_log.shape), "dt_bias": pieces["dt_bias"].reshape(dt_bias.shape),
        "dn_norm_g": pieces["dn_norm_g"].reshape(dn_norm_g.shape), "pool_w": pieces["pool_w"].reshape(pool_w.shape),
        "pool_scale": pieces["pool_scale"].reshape(pool_scale.shape),
        "final_g": pieces["final_g"].reshape(final_g.shape),
    }
    weights = {"ada_w": ada_w, "ada_b": ada_b, "norm_g": norm_g, "ffn1_w_in": ffn1_w_in, "ffn1_w_out": ffn1_w_out,
               "ffn2_w_in": ffn2_w_in, "ffn2_w_out": ffn2_w_out, "mix_w_in": mix_w_in, "conv_w": conv_w,
               "a_log": a_log, "dt_bias": dt_bias, "dn_norm_g": dn_norm_g, "pool_w": pool_w,
               "pool_scale": pool_scale, "pool_proj": pool_proj, "dn_proj": dn_proj, "mix_w_out": mix_w_out,
               "final_g": final_g}
    m_in = {"ada_w": m_ada_w, "ada_b": m_ada_b, "norm_g": m_norm_g, "ffn1_w_in": m_ffn1_w_in,
            "ffn1_w_out": m_ffn1_w_out, "ffn2_w_in": m_ffn2_w_in, "ffn2_w_out": m_ffn2_w_out,
            "mix_w_in": m_mix_w_in, "conv_w": m_conv_w, "a_log": m_a_log, "dt_bias": m_dt_bias,
            "dn_norm_g": m_dn_norm_g, "pool_w": m_pool_w, "pool_scale": m_pool_scale, "pool_proj": m_pool_proj,
            "dn_proj": m_dn_proj, "mix_w_out": m_mix_w_out, "final_g": m_final_g}
    v_in = {"ada_w": v_ada_w, "ada_b": v_ada_b, "norm_g": v_norm_g, "ffn1_w_in": v_ffn1_w_in,
            "ffn1_w_out": v_ffn1_w_out, "ffn2_w_in": v_ffn2_w_in, "ffn2_w_out": v_ffn2_w_out,
            "mix_w_in": v_mix_w_in, "conv_w": v_conv_w, "a_log": v_a_log, "dt_bias": v_dt_bias,
            "dn_norm_g": v_dn_norm_g, "pool_w": v_pool_w, "pool_scale": v_pool_scale, "pool_proj": v_pool_proj,
            "dn_proj": v_dn_proj, "mix_w_out": v_mix_w_out, "final_g": v_final_g}

    names = list(weights)
    large = ("ada_w", "ffn1_w_in", "ffn1_w_out", "ffn2_w_in", "ffn2_w_out", "mix_w_in", "pool_proj", "dn_proj",
             "mix_w_out")
    delta, new_m, new_v = {}, {}, {}

    flipped = ("ffn1_w_in", "ffn2_w_in", "mix_w_in")

    def views(nm):
        shp = weights[nm].shape
        two_d = (shp[-2], shp[-1])
        if nm in flipped:
            return (lambda a: a.reshape(two_d).T), (lambda a: a.T.reshape(shp))
        return (lambda a: a.reshape(two_d)), (lambda a: a.reshape(shp))

    def reduce_update(sent, group, after, tag):
        for nm, r in zip(group, _push_wait(sent, True, after, f"{tag}_grad_wait")):
            view, back = views(nm)
            g_, d_, m_, v_ = _reduce_adamw(r, view(weights[nm]), view(m_in[nm]), view(v_in[nm]), f"adamw_{nm}")
            grads[nm], delta[nm], new_m[nm], new_v[nm] = back(g_), back(d_), back(m_), back(v_)
        return d_

    view, back = views("ada_w")
    done, m_, v_ = _adamw(view(ada_w), view(grads["ada_w"]), view(m_ada_w), view(v_ada_w), "adamw_ada_w")
    delta["ada_w"], new_m["ada_w"], new_v["ada_w"] = back(done), back(m_), back(v_)
    done = reduce_update(sent2, ("ffn2_w_in", "ffn2_w_out"), done, "ffn2")
    done = reduce_update(sent1, ("mix_w_in", "pool_proj", "dn_proj", "mix_w_out"), done, "mix")
    reduce_update(sent0, ("ffn1_w_in", "ffn1_w_out"), done, "ffn1")
    rest = [nm for nm in names if nm not in large]
    total = sum(weights[nm].size for nm in rest)
    padded = -(-total // D) * D

    def pack(tree, fill):
        flat_ = jnp.concatenate([tree[nm].reshape(-1) for nm in rest])
        return jnp.concatenate([flat_, jnp.full((padded - total,), fill, F32)]).reshape(-1, D)

    d_, m_, v_ = _adamw(pack(weights, 0.0), pack(grads, 0.0), pack(m_in, 0.0), pack(v_in, 1.0), "adamw_small")
    p = 0
    for nm in rest:
        size = weights[nm].size
        shp = weights[nm].shape
        delta[nm] = d_.reshape(-1)[p:p + size].reshape(shp)
        new_m[nm] = m_.reshape(-1)[p:p + size].reshape(shp)
        new_v[nm] = v_.reshape(-1)[p:p + size].reshape(shp)
        p += size

    loss = pieces["loss"][0]
    grad_x = dx0.reshape(x.shape)
    return (loss, grad_x, *[grads[nm] for nm in names], *[delta[nm] for nm in names],
            *[new_m[nm] for nm in names], *[new_v[nm] for nm in names])
```

```python
import functools

import jax
import jax.numpy as jnp
from jax import lax
from jax.experimental import pallas as pl
from jax.experimental.pallas import tpu as pltpu

F32 = jnp.float32
BF16 = jnp.bfloat16
SDS = jax.ShapeDtypeStruct
HI = lax.Precision.HIGHEST

D = 1024
FH = 2816
FB = 704
NH = 8
HD = 128
CH = 64
SCAN_CHUNKS = 2
NDEV = 8
PW = 512
PG = 128
RMS_EPS = 1e-6
L2_EPS = 1e-6
TR = 512
HALO = 16
VMEM_LIMIT = 56 * 1024 * 1024

MIXP = 6912
OFF_Q, OFF_K, OFF_V, OFF_Z, OFF_GP, OFF_GD, OFF_XP, OFF_BA = 0, 1024, 2048, 3072, 4096, 5120, 6144, 6656
MIX_RAW = 6672

ADAM_LR = 0.001
ADAM_B1 = 0.9
ADAM_B2 = 0.999
ADAM_EPS = 1e-08
ADAM_WD = 0.01
ADAM_STEP = 10

NN = (((1,), (0,)), ((), ()))
NT = (((1,), (1,)), ((), ()))
TN = (((0,), (0,)), ((), ()))


def _dg(a, b, dims, prec=None):
    return lax.dot_general(a, b, dims, precision=prec, preferred_element_type=F32)


def _make_dots(prec):
    @jax.custom_vjp
    def nn(a, b):
        return _dg(a, b, NN, prec)

    @jax.custom_vjp
    def nt(a, b):
        return _dg(a, b, NT, prec)

    @jax.custom_vjp
    def tn(a, b):
        return _dg(a, b, TN, prec)

    nn.defvjp(lambda a, b: (nn(a, b), (a, b)), lambda r, d: (nt(d, r[1]), tn(r[0], d)))
    nt.defvjp(lambda a, b: (nt(a, b), (a, b)), lambda r, d: (nn(d, r[1]), tn(d, r[0])))
    tn.defvjp(lambda a, b: (tn(a, b), (a, b)), lambda r, d: (nt(r[1], d), nn(r[0], d)))
    return nn, nt, tn


_nn, _nt, _tn = _make_dots(None)


def _params(sem):
    return pltpu.CompilerParams(dimension_semantics=sem, vmem_limit_bytes=VMEM_LIMIT)


def _sigmoid(x):
    return 1.0 / (1.0 + jnp.exp(-x))


def _silu(x):
    return x * _sigmoid(x)


def _dsilu(x):
    s = _sigmoid(x)
    return s * (1.0 + x * (1.0 - s))


def _pick(n, cands):
    for c in cands:
        if n % c == 0:
            return c
    raise ValueError(f"no tile for {n}")


def _iota(shape, dim):
    return lax.broadcasted_iota(jnp.int32, shape, dim)


def _matmul(a, b, *, ta=False, tb=False, a_blk=False, b_blk=False, o_blk=False, tm=None, tn=None, tk=None,
            out_dtype, name, after=None):
    if a_blk:
        nb, r, cb = a.shape
        if ta:
            k_dim, m_dim, tm = r, nb * cb, cb
        else:
            m_dim, k_dim, tk = r, nb * cb, cb
    else:
        k_dim, m_dim = a.shape if ta else a.shape[::-1]
    if b_blk:
        nb, r, cb = b.shape
        if tb:
            n_dim, tk = r, cb
            assert nb * cb == k_dim
        else:
            n_dim, tn = nb * cb, cb
            assert r == k_dim
    else:
        n_dim = b.shape[0] if tb else b.shape[1]
    tm = tm or _pick(m_dim, (1024, 768, 512, 256, 128))
    tn = tn or _pick(n_dim, (1024, 768, 512, 256, 128))
    tk = tk or (k_dim if (k_dim <= 2816 and not ta) else _pick(k_dim, (2816, 2304, 1024, 512, 256)))
    nk = k_dim // tk
    dims = ((((0,) if ta else (1,)), ((1,) if tb else (0,))), ((), ()))

    def body(a_ref, b_ref, *rest):
        o_ref, acc_ref = rest[-2:]
        k = pl.program_id(2)

        @pl.when(k == 0)
        def _():
            acc_ref[...] = jnp.zeros_like(acc_ref)

        acc_ref[...] += lax.dot_general(a_ref[...].astype(BF16), b_ref[...].astype(BF16), dims,
                                        preferred_element_type=F32)

        @pl.when(k == nk - 1)
        def _():
            o_ref[...] = acc_ref[...].astype(o_ref.dtype)

    if a_blk:
        a_spec = (pl.BlockSpec((None, tk, tm), lambda i, j, k: (i, k, 0)) if ta
                  else pl.BlockSpec((None, tm, tk), lambda i, j, k: (k, i, 0)))
    else:
        a_spec = (pl.BlockSpec((tk, tm), lambda i, j, k: (k, i)) if ta
                  else pl.BlockSpec((tm, tk), lambda i, j, k: (i, k)))
    if b_blk:
        b_spec = (pl.BlockSpec((None, tn, tk), lambda i, j, k: (k, j, 0)) if tb
                  else pl.BlockSpec((None, tk, tn), lambda i, j, k: (j, k, 0)))
    else:
        b_spec = (pl.BlockSpec((tn, tk), lambda i, j, k: (j, k)) if tb
                  else pl.BlockSpec((tk, tn), lambda i, j, k: (k, j)))
    if o_blk:
        o_spec = pl.BlockSpec((None, tm, tn), lambda i, j, k: (j, i, 0))
        o_shape = SDS((n_dim // tn, m_dim, tn), out_dtype)
    else:
        o_spec = pl.BlockSpec((tm, tn), lambda i, j, k: (i, j))
        o_shape = SDS((m_dim, n_dim), out_dtype)
    return pl.pallas_call(
        body, grid=(m_dim // tm, n_dim // tn, nk),
        in_specs=[a_spec, b_spec] + ([] if after is None else [pl.BlockSpec(memory_space=pl.ANY)]),
        out_specs=o_spec,
        out_shape=o_shape,
        scratch_shapes=[pltpu.VMEM((tm, tn), F32)],
        compiler_params=_params(("parallel", "parallel", "arbitrary")),
        name=name,
    )(a, b, *([] if after is None else [after]))


def _matmul_residual(a, b, x, gate, coef, *, a_blk=False, norm=None, name, after=None):
    if a_blk:
        nb, m_dim, tk = a.shape
        nk = nb
        a_spec = pl.BlockSpec((None, 512, tk), lambda i, k: (k, i, 0))
    else:
        m_dim, tk = a.shape
        nk = 1
        a_spec = pl.BlockSpec((512, tk), lambda i, k: (i, 0))
    tm = 512
    extra = [] if after is None else [after]
    vecs = [gate] + (list(norm) if norm else [])

    def body(a_ref, b_ref, x_ref, gate_ref, *rest):
        vec_refs = rest[:len(vecs) - 1]
        outs = rest[len(vecs) - 1 + len(extra):]
        acc_ref = outs[-1]
        k = pl.program_id(1)

        @pl.when(k == 0)
        def _():
            acc_ref[...] = jnp.zeros_like(acc_ref)

        acc_ref[...] += _dg(a_ref[...], b_ref[...], NN)

        @pl.when(k == nk - 1)
        def _():
            y = acc_ref[...]
            xn = x_ref[...] + (coef * gate_ref[...]) * y
            outs[0][...] = xn
            outs[1][...] = y.astype(outs[1].dtype)
            if norm:
                g_ref, sh_ref, sc_ref = vec_refs
                r = lax.rsqrt(jnp.mean(xn * xn, axis=-1, keepdims=True) + RMS_EPS)
                outs[2][...] = (((xn * r) * g_ref[...]) * (1.0 + sc_ref[...]) + sh_ref[...]).astype(outs[2].dtype)

    row = pl.BlockSpec((tm, D), lambda i, k: (i, 0))
    vec = pl.BlockSpec((1, D), lambda i, k: (0, 0))
    return pl.pallas_call(
        body, grid=(m_dim // tm, nk),
        in_specs=[a_spec, pl.BlockSpec((tk, D), lambda i, k: (k, 0)), row] + [vec] * len(vecs)
        + [pl.BlockSpec(memory_space=pl.ANY)] * len(extra),
        out_specs=[row] * (3 if norm else 2),
        out_shape=[SDS((m_dim, D), F32), SDS((m_dim, D), BF16)] + ([SDS((m_dim, D), BF16)] if norm else []),
        scratch_shapes=[pltpu.VMEM((tm, D), F32)],
        compiler_params=_params(("parallel", "arbitrary")), name=name,
    )(a, b, x, *vecs, *extra)


def _row(width, col=0):
    return pl.BlockSpec((TR, width), lambda i: (i, col))


def _vec(width):
    return pl.BlockSpec((1, width), lambda i: (0, 0))


def _norm_mod_fwd(x, g, shift, scale, name):
    t = x.shape[0]

    def body(x_ref, g_ref, sh_ref, sc_ref, o_ref):
        xv = x_ref[...]
        r = lax.rsqrt(jnp.mean(xv * xv, axis=-1, keepdims=True) + RMS_EPS)
        o_ref[...] = (((xv * r) * g_ref[...]) * (1.0 + sc_ref[...]) + sh_ref[...]).astype(o_ref.dtype)

    return pl.pallas_call(
        body, grid=(t // TR,), in_specs=[_row(D), _vec(D), _vec(D), _vec(D)], out_specs=_row(D),
        out_shape=SDS((t, D), BF16), compiler_params=_params(("parallel",)), name=name,
    )(x, g, shift, scale)


def _norm_mod_bwd(x, g, scale, dh, dx_in, name):
    t = x.shape[0]

    def body(x_ref, g_ref, sc_ref, dh_ref, dxi_ref, dx_ref, dsh_ref, dsc_ref, dg_ref):
        @pl.when(pl.program_id(0) == 0)
        def _():
            dsh_ref[...] = jnp.zeros_like(dsh_ref)
            dsc_ref[...] = jnp.zeros_like(dsc_ref)
            dg_ref[...] = jnp.zeros_like(dg_ref)

        xv = x_ref[...]
        gv = g_ref[...]
        dh = dh_ref[...]
        r = lax.rsqrt(jnp.mean(xv * xv, axis=-1, keepdims=True) + RMS_EPS)
        n = xv * r
        dsh_ref[...] += jnp.sum(dh, axis=0, keepdims=True)
        dsc_ref[...] += jnp.sum(dh * (n * gv), axis=0, keepdims=True)
        tt = dh * (1.0 + sc_ref[...])
        dg_ref[...] += jnp.sum(tt * n, axis=0, keepdims=True)
        dn = tt * gv
        dx_ref[...] = dxi_ref[...] + r * (dn - n * jnp.mean(dn * n, axis=-1, keepdims=True))

    return pl.pallas_call(
        body, grid=(t // TR,), in_specs=[_row(D), _vec(D), _vec(D), _row(D), _row(D)],
        out_specs=[_row(D), _vec(D), _vec(D), _vec(D)],
        out_shape=[SDS((t, D), F32), SDS((1, D), F32), SDS((1, D), F32), SDS((1, D), F32)],
        compiler_params=_params(("arbitrary",)), name=name,
    )(x, g, scale, dh, dx_in)


def _swiglu_up(h, w_in, name, after=None):
    t = h.shape[0]
    tm = _pick(t, (1024, 512, 256))
    half = NDEV // 2
    extra = [] if after is None else [after]

    def body(h_ref, wg_ref, wu_ref, *rest):
        u_ref, a_ref = rest[-2:]
        hv = h_ref[...]
        gate = _dg(hv, wg_ref[...], NT)
        up = _dg(hv, wu_ref[...], NT)
        u_ref[0] = gate.astype(u_ref.dtype)
        u_ref[1] = up.astype(u_ref.dtype)
        a_ref[...] = (_silu(gate) * up).astype(a_ref.dtype)

    return pl.pallas_call(
        body, grid=(t // tm, half),
        in_specs=[pl.BlockSpec((tm, D), lambda i, j: (i, 0)),
                  pl.BlockSpec((FB, D), lambda i, j: (j, 0)),
                  pl.BlockSpec((FB, D), lambda i, j: (j + half, 0))]
        + [pl.BlockSpec(memory_space=pl.ANY)] * len(extra),
        out_specs=[pl.BlockSpec((2, None, tm, FB), lambda i, j: (0, j, i, 0)),
                   pl.BlockSpec((None, tm, FB), lambda i, j: (j, i, 0))],
        out_shape=[SDS((2, half, t, FB), BF16), SDS((half, t, FB), BF16)],
        compiler_params=_params(("parallel", "parallel")), name=name,
    )(h, w_in, w_in, *extra)


def _swiglu_down_bwd(dy, w_out, u, name, after=None):
    t = dy.shape[0]
    tm = _pick(t, (1024, 512, 256))
    half = NDEV // 2
    extra = [] if after is None else [after]
    pair = pl.BlockSpec((2, None, tm, FB), lambda i, j: (0, j, i, 0))

    def body(dy_ref, w_ref, u_ref, *rest):
        o_ref = rest[-1]
        da = _dg(dy_ref[...], w_ref[...], NT)
        gate = u_ref[0].astype(F32)
        o_ref[0] = (da * u_ref[1].astype(F32) * _dsilu(gate)).astype(o_ref.dtype)
        o_ref[1] = (da * _silu(gate)).astype(o_ref.dtype)

    return pl.pallas_call(
        body, grid=(t // tm, half),
        in_specs=[pl.BlockSpec((tm, D), lambda i, j: (i, 0)), pl.BlockSpec((FB, D), lambda i, j: (j, 0)), pair]
        + [pl.BlockSpec(memory_space=pl.ANY)] * len(extra),
        out_specs=pair, out_shape=SDS((2, half, t, FB), BF16),
        compiler_params=_params(("parallel", "parallel")), name=name,
    )(dy, w_out, u, *extra)


def _resid_fwd(x, y, gate, coef, name):
    t = x.shape[0]

    def body(x_ref, y_ref, g_ref, o_ref):
        o_ref[...] = x_ref[...] + (coef * g_ref[...]) * y_ref[...]

    return pl.pallas_call(
        body, grid=(t // TR,), in_specs=[_row(D), _row(D), _vec(D)], out_specs=_row(D),
        out_shape=SDS((t, D), F32), compiler_params=_params(("parallel",)), name=name,
    )(x, y, gate)


def _resid_bwd(dx, y, gate, coef, name):
    t = dx.shape[0]

    def body(dx_ref, y_ref, g_ref, dy_ref, dg_ref):
        @pl.when(pl.program_id(0) == 0)
        def _():
            dg_ref[...] = jnp.zeros_like(dg_ref)

        dxv = dx_ref[...]
        dy_ref[...] = ((coef * g_ref[...]) * dxv).astype(dy_ref.dtype)
        dg_ref[...] += jnp.sum((coef * dxv) * y_ref[...], axis=0, keepdims=True)

    return pl.pallas_call(
        body, grid=(t // TR,), in_specs=[_row(D), _row(D), _vec(D)], out_specs=[_row(D), _vec(D)],
        out_shape=[SDS((t, D), BF16), SDS((1, D), F32)],
        compiler_params=_params(("arbitrary",)), name=name,
    )(dx, y, gate)


def _final_loss(x, fg, target, name):
    t = x.shape[0]
    nt = t // TR

    def body(x_ref, g_ref, t_ref, loss_ref, dx_ref, dg_ref, acc_ref):
        i = pl.program_id(0)

        @pl.when(i == 0)
        def _():
            acc_ref[...] = jnp.zeros_like(acc_ref)
            dg_ref[...] = jnp.zeros_like(dg_ref)

        xv = x_ref[...]
        gv = g_ref[...]
        r = lax.rsqrt(jnp.mean(xv * xv, axis=-1, keepdims=True) + RMS_EPS)
        n = xv * r
        err = n * gv - t_ref[...]
        acc_ref[...] += jnp.sum(err * err, axis=0, keepdims=True)
        dy = err * (1.0 / D)
        dg_ref[...] += jnp.sum(dy * n, axis=0, keepdims=True)
        dn = dy * gv
        dx_ref[...] = r * (dn - n * jnp.mean(dn * n, axis=-1, keepdims=True))

        @pl.when(i == nt - 1)
        def _():
            tot = jnp.sum(acc_ref[...], axis=1, keepdims=True) * (0.5 / D)
            loss_ref[...] = jnp.broadcast_to(tot, loss_ref.shape)

    return pl.pallas_call(
        body, grid=(nt,), in_specs=[_row(D), _vec(D), _row(D)],
        out_specs=[_vec(128), _row(D), _vec(D)],
        out_shape=[SDS((1, 128), F32), SDS((t, D), F32), SDS((1, D), F32)],
        scratch_shapes=[pltpu.VMEM((1, D), F32)],
        compiler_params=_params(("arbitrary",)), name=name,
    )(x, fg, target)


def _halo_prev(width, col):
    per = TR // HALO
    return pl.BlockSpec((HALO, width), lambda i: (jnp.maximum(i * per - 1, 0), col))


def _halo_next(width, col, nt):
    per = TR // HALO
    return pl.BlockSpec((HALO, width), lambda i: (jnp.minimum((i + 1) * per, nt * per - 1), col))


def _pool_windows(ext, tile_index):
    rows = _iota((TR, PG), 0) + tile_index * TR + 1
    pooled, counts = [], []
    for gi in range(4):
        w = 2 << gi
        e = ext[:, gi * PG:(gi + 1) * PG]
        s = e
        step = 1
        while step < w:
            s = s + pltpu.roll(s, step, 0)
            step *= 2
        cnt = jnp.minimum(rows, w).astype(F32)
        pooled.append(s[HALO:] / cnt - e[HALO:])
        counts.append(cnt)
    return pooled, counts


def _pool_fwd(proj, pool_w, pool_scale, pool_proj, name):
    t = proj.shape[0]
    xcol = OFF_XP // PW

    def body(x_ref, h_ref, pw_ref, ps_ref, pp_ref, o_ref):
        i = pl.program_id(0)
        halo = jnp.where(i > 0, h_ref[...], 0.0)
        ext = jnp.concatenate([halo, x_ref[...]], axis=0)
        pooled, _ = _pool_windows(ext, i)
        mixed = [_dg(pooled[g].astype(BF16), pw_ref[g].astype(BF16), NN) for g in range(4)]
        ypre = jnp.concatenate(mixed, axis=1) * ps_ref[...]
        o_ref[...] = _dg(ypre.astype(BF16), pp_ref[...], NN)

    return pl.pallas_call(
        body, grid=(t // TR,),
        in_specs=[_row(PW, xcol), _halo_prev(PW, xcol),
                  pl.BlockSpec((4, PG, PG), lambda i: (0, 0, 0)), _vec(PW),
                  pl.BlockSpec((PW, D), lambda i: (0, 0))],
        out_specs=_row(D), out_shape=SDS((t, D), F32),
        compiler_params=_params(("parallel",)), name=name,
    )(proj, proj, pool_w, pool_scale, pool_proj)


def _pool_bwd_local(proj, pool_w, pool_scale, pool_proj, dya, name):
    t = proj.shape[0]
    xcol = OFF_XP // PW

    def body(x_ref, h_ref, pw_ref, ps_ref, pp_ref, dya_ref, dwin_ref, dpl_ref, dpw_ref, dps_ref, dpp_ref):
        i = pl.program_id(0)

        @pl.when(i == 0)
        def _():
            dpw_ref[...] = jnp.zeros_like(dpw_ref)
            dps_ref[...] = jnp.zeros_like(dps_ref)
            dpp_ref[...] = jnp.zeros_like(dpp_ref)

        halo = jnp.where(i > 0, h_ref[...], 0.0)
        ext = jnp.concatenate([halo, x_ref[...]], axis=0)
        pooled, counts = _pool_windows(ext, i)
        mixed = jnp.concatenate(
            [_dg(pooled[g].astype(BF16), pw_ref[g].astype(BF16), NN) for g in range(4)], axis=1)
        ps = ps_ref[...]
        ypre = mixed * ps
        dyab = dya_ref[...].astype(BF16)
        dypre = _dg(dyab, pp_ref[...], NT)
        dpp_ref[...] += _dg(ypre.astype(BF16), dyab, TN)
        dps_ref[...] += jnp.sum(dypre * mixed, axis=0, keepdims=True)
        dmixed = dypre * ps
        for g in range(4):
            dm = dmixed[:, g * PG:(g + 1) * PG].astype(BF16)
            dpw_ref[g] += _dg(pooled[g].astype(BF16), dm, TN)
            dpooled = _dg(dm, pw_ref[g].astype(BF16), NT)
            dwin_ref[:, g * PG:(g + 1) * PG] = dpooled / counts[g]
            dpl_ref[:, g * PG:(g + 1) * PG] = dpooled

    return pl.pallas_call(
        body, grid=(t // TR,),
        in_specs=[_row(PW, xcol), _halo_prev(PW, xcol),
                  pl.BlockSpec((4, PG, PG), lambda i: (0, 0, 0)), _vec(PW),
                  pl.BlockSpec((PW, D), lambda i: (0, 0)), _row(D)],
        out_specs=[_row(PW), _row(PW), pl.BlockSpec((4, PG, PG), lambda i: (0, 0, 0)), _vec(PW),
                   pl.BlockSpec((PW, D), lambda i: (0, 0))],
        out_shape=[SDS((t, PW), F32), SDS((t, PW), F32), SDS((4, PG, PG), F32), SDS((1, PW), F32),
                   SDS((PW, D), F32)],
        compiler_params=_params(("arbitrary",)), name=name,
    )(proj, proj, pool_w, pool_scale, pool_proj, dya)


def _pool_bwd_window(dwin, dpl, dproj, name):
    t = dwin.shape[0]
    nt = t // TR
    ext_rows = TR + HALO

    def body(dw_ref, h_ref, dp_ref, _, o_ref):
        i = pl.program_id(0)
        halo = jnp.where(i < nt - 1, h_ref[...], 0.0)
        ext = jnp.concatenate([dw_ref[...], halo], axis=0)
        for gi in range(4):
            w = 2 << gi
            s = ext[:, gi * PG:(gi + 1) * PG]
            step = 1
            while step < w:
                s = s + pltpu.roll(s, ext_rows - step, 0)
                step *= 2
            o_ref[:, gi * PG:(gi + 1) * PG] = (s[:TR] - dp_ref[:, gi * PG:(gi + 1) * PG]).astype(o_ref.dtype)

    return pl.pallas_call(
        body, grid=(nt,),
        in_specs=[_row(PW), _halo_next(PW, 0, nt), _row(PW), pl.BlockSpec(memory_space=pl.ANY)],
        out_specs=_into(PW, OFF_XP), out_shape=SDS(dproj.shape, dproj.dtype), input_output_aliases={3: 0},
        compiler_params=_params(("parallel",)), name=name,
    )(dwin, dwin, dpl, dproj)


def _conv_group(ext, cw_ref, cols):
    acc = cw_ref[3:4, cols] * ext
    for j in range(3):
        acc = acc + cw_ref[j:j + 1, cols] * pltpu.roll(ext, 3 - j, 0)
    return acc[HALO:]


def _gate_terms(raw, al, dt):
    beta = _sigmoid(raw)
    xg = raw + dt
    sp = jnp.maximum(xg, 0.0) + jnp.log(1.0 + jnp.exp(-jnp.abs(xg)))
    g = -jnp.exp(al) * sp
    return beta, g, _sigmoid(xg)


def _dn_pre_fwd(proj, conv_w, al_row, dt_row, name):
    t = proj.shape[0]

    def body(x_ref, h_ref, cw_ref, ba_ref, al_ref, dt_ref, q_ref, k_ref, v_ref, bg_ref):
        i = pl.program_id(0)
        keep = i > 0
        for grp in range(24):
            cols = slice(grp * HD, (grp + 1) * HD)
            ext = jnp.concatenate([jnp.where(keep, h_ref[:, cols], 0.0), x_ref[:, cols]], axis=0)
            s = _silu(_conv_group(ext, cw_ref, cols))
            seg, head = divmod(grp, NH)
            hc = slice(head * HD, (head + 1) * HD)
            if seg == 0:
                q_ref[:, hc] = s * lax.rsqrt(jnp.sum(s * s, axis=-1, keepdims=True) + L2_EPS) * (HD ** -0.5)
            elif seg == 1:
                k_ref[:, hc] = s * lax.rsqrt(jnp.sum(s * s, axis=-1, keepdims=True) + L2_EPS)
            else:
                v_ref[:, hc] = s
        lane = _iota((TR, 128), 1)
        rowc = _iota((TR, 128), 0) % CH
        beta, g, _ = _gate_terms(ba_ref[...], al_ref[...], dt_ref[...])
        step = 1
        while step < CH:
            g = g + jnp.where(rowc >= step, pltpu.roll(g, step, 0), 0.0)
            step *= 2
        bg_ref[...] = jnp.where(lane < NH, beta, jnp.where(lane < 2 * NH, g, 0.0))

    return pl.pallas_call(
        body, grid=(t // TR,),
        in_specs=[_row(3 * D, 0), _halo_prev(3 * D, 0), pl.BlockSpec((4, 3 * D), lambda i: (0, 0)),
                  _row(128, OFF_BA // 128), _vec(128), _vec(128)],
        out_specs=[_row(D), _row(D), _row(D), _row(128)],
        out_shape=[SDS((t, D), F32), SDS((t, D), F32), SDS((t, D), F32), SDS((t, 128), F32)],
        compiler_params=_params(("parallel",)), name=name,
    )(proj, proj, conv_w, proj, al_row, dt_row)


def _dn_pre_bwd_act(proj, conv_w, al_row, dt_row, dq, dk, dv, dbg, dproj, name):
    t = proj.shape[0]

    def body(x_ref, h_ref, cw_ref, ba_ref, al_ref, dt_ref, dq_ref, dk_ref, dv_ref, dbg_ref, _,
             dc_ref, draw_ref, dal_ref, ddt_ref):
        i = pl.program_id(0)

        @pl.when(i == 0)
        def _():
            dal_ref[...] = jnp.zeros_like(dal_ref)
            ddt_ref[...] = jnp.zeros_like(ddt_ref)

        keep = i > 0
        for grp in range(24):
            cols = slice(grp * HD, (grp + 1) * HD)
            ext = jnp.concatenate([jnp.where(keep, h_ref[:, cols], 0.0), x_ref[:, cols]], axis=0)
            cv = _conv_group(ext, cw_ref, cols)
            seg, head = divmod(grp, NH)
            hc = slice(head * HD, (head + 1) * HD)
            if seg == 2:
                ds = dv_ref[:, hc]
            else:
                s = _silu(cv)
                r = lax.rsqrt(jnp.sum(s * s, axis=-1, keepdims=True) + L2_EPS)
                dy = dq_ref[:, hc] if seg == 0 else dk_ref[:, hc]
                c = (HD ** -0.5) if seg == 0 else 1.0
                ds = (c * r) * (dy - s * ((r * r) * jnp.sum(dy * s, axis=-1, keepdims=True)))
            dc_ref[:, cols] = ds * _dsilu(cv)
        lane = _iota((TR, 128), 1)
        rowc = _iota((TR, 128), 0) % CH
        isb = lane < NH
        isg = jnp.logical_and(lane >= NH, lane < 2 * NH)
        beta, g, sg = _gate_terms(ba_ref[...], al_ref[...], dt_ref[...])
        dbgv = dbg_ref[...]
        dg = dbgv
        step = 1
        while step < CH:
            dg = dg + jnp.where(rowc < CH - step, pltpu.roll(dg, TR - step, 0), 0.0)
            step *= 2
        da_raw = dg * (-jnp.exp(al_ref[...])) * sg
        draw = jnp.where(isb, dbgv * beta * (1.0 - beta), jnp.where(isg, da_raw, 0.0))
        draw_ref[:, :128] = draw.astype(draw_ref.dtype)
        draw_ref[:, 128:] = jnp.zeros((TR, MIXP - OFF_BA - 128), draw_ref.dtype)
        dal_ref[...] += jnp.sum(jnp.where(isg, dg * g, 0.0), axis=0, keepdims=True)
        ddt_ref[...] += jnp.sum(jnp.where(isg, da_raw, 0.0), axis=0, keepdims=True)

    return pl.pallas_call(
        body, grid=(t // TR,),
        in_specs=[_row(3 * D, 0), _halo_prev(3 * D, 0), pl.BlockSpec((4, 3 * D), lambda i: (0, 0)),
                  _row(128, OFF_BA // 128), _vec(128), _vec(128), _row(D), _row(D), _row(D), _row(128),
                  pl.BlockSpec(memory_space=pl.ANY)],
        out_specs=[_row(3 * D), _into(MIXP - OFF_BA, OFF_BA), _vec(128), _vec(128)],
        out_shape=[SDS((t, 3 * D), F32), SDS(dproj.shape, dproj.dtype), SDS((1, 128), F32), SDS((1, 128), F32)],
        input_output_aliases={10: 1},
        compiler_params=_params(("arbitrary",)), name=name,
    )(proj, proj, conv_w, proj, al_row, dt_row, dq, dk, dv, dbg, dproj)


def _dn_pre_bwd_conv(proj, conv_w, dconv, dproj, name):
    t = proj.shape[0]
    nt = t // TR
    ext_rows = TR + HALO

    def body(x_ref, h_ref, cw_ref, dc_ref, dn_ref, _, dx_ref, dcw_ref):
        i = pl.program_id(0)

        @pl.when(i == 0)
        def _():
            dcw_ref[...] = jnp.zeros_like(dcw_ref)

        keep_prev = i > 0
        keep_next = i < nt - 1
        for grp in range(24):
            cols = slice(grp * HD, (grp + 1) * HD)
            dct = dc_ref[:, cols]
            dext = jnp.concatenate([dct, jnp.where(keep_next, dn_ref[:, cols], 0.0)], axis=0)
            acc = cw_ref[3:4, cols] * dext
            for j in range(3):
                acc = acc + cw_ref[j:j + 1, cols] * pltpu.roll(dext, ext_rows - (3 - j), 0)
            dx_ref[:, cols] = acc[:TR].astype(dx_ref.dtype)
            xext = jnp.concatenate([jnp.where(keep_prev, h_ref[:, cols], 0.0), x_ref[:, cols]], axis=0)
            for j in range(4):
                xs = xext if j == 3 else pltpu.roll(xext, 3 - j, 0)
                dcw_ref[j:j + 1, cols] += jnp.sum(xs[HALO:] * dct, axis=0, keepdims=True)

    return pl.pallas_call(
        body, grid=(nt,),
        in_specs=[_row(3 * D, 0), _halo_prev(3 * D, 0), pl.BlockSpec((4, 3 * D), lambda i: (0, 0)),
                  _row(3 * D), _halo_next(3 * D, 0, nt), pl.BlockSpec(memory_space=pl.ANY)],
        out_specs=[_into(3 * D, OFF_Q), pl.BlockSpec((4, 3 * D), lambda i: (0, 0))],
        out_shape=[SDS(dproj.shape, dproj.dtype), SDS((4, 3 * D), F32)],
        input_output_aliases={5: 0},
        compiler_params=_params(("arbitrary",)), name=name,
    )(proj, proj, conv_w, dconv, dconv, dproj)


def _dn_post_fwd(o, proj, gn, name):
    t = o.shape[0]

    def body(o_ref, z_ref, g_ref, out_ref):
        gv = g_ref[...]
        for h in range(NH):
            hc = slice(h * HD, (h + 1) * HD)
            ov = o_ref[:, hc]
            r = lax.rsqrt(jnp.mean(ov * ov, axis=-1, keepdims=True) + RMS_EPS)
            out_ref[:, hc] = (((ov * r) * gv) * _silu(z_ref[:, hc])).astype(out_ref.dtype)

    return pl.pallas_call(
        body, grid=(t // TR,), in_specs=[_row(D), _row(D, OFF_Z // D), _vec(HD)], out_specs=_row(D),
        out_shape=SDS((t, D), BF16), compiler_params=_params(("parallel",)), name=name,
    )(o, proj, gn)


def _dn_post_bwd(o, proj, gn, dob, dproj, name):
    t = o.shape[0]

    def body(o_ref, z_ref, g_ref, d_ref, _, do_ref, dz_ref, dg_ref):
        @pl.when(pl.program_id(0) == 0)
        def _():
            dg_ref[...] = jnp.zeros_like(dg_ref)

        gv = g_ref[...]
        acc = jnp.zeros((1, HD), F32)
        for h in range(NH):
            hc = slice(h * HD, (h + 1) * HD)
            ov = o_ref[:, hc]
            zv = z_ref[:, hc]
            dv = d_ref[:, hc]
            r = lax.rsqrt(jnp.mean(ov * ov, axis=-1, keepdims=True) + RMS_EPS)
            n = ov * r
            dz_ref[:, hc] = (dv * (n * gv) * _dsilu(zv)).astype(dz_ref.dtype)
            dng = dv * _silu(zv)
            acc = acc + jnp.sum(dng * n, axis=0, keepdims=True)
            dn = dng * gv
            do_ref[:, hc] = r * (dn - n * jnp.mean(dn * n, axis=-1, keepdims=True))
        dg_ref[...] += acc

    return pl.pallas_call(
        body, grid=(t // TR,),
        in_specs=[_row(D), _row(D, OFF_Z // D), _vec(HD), _row(D), pl.BlockSpec(memory_space=pl.ANY)],
        out_specs=[_row(D), _into(D, OFF_Z), _vec(HD)],
        out_shape=[SDS((t, D), F32), SDS(dproj.shape, dproj.dtype), SDS((1, HD), F32)],
        input_output_aliases={4: 1},
        compiler_params=_params(("arbitrary",)), name=name,
    )(o, proj, gn, dob, dproj)


def _merge_fwd(ya, yb, proj, name):
    t = ya.shape[0]

    def body(a_ref, b_ref, gp_ref, gd_ref, o_ref):
        o_ref[...] = (_sigmoid(gp_ref[...]) * a_ref[...] + _sigmoid(gd_ref[...]) * b_ref[...]).astype(o_ref.dtype)

    return pl.pallas_call(
        body, grid=(t // TR,), in_specs=[_row(D), _row(D), _row(D, OFF_GP // D), _row(D, OFF_GD // D)],
        out_specs=_row(D), out_shape=SDS((t, D), BF16),
        compiler_params=_params(("parallel",)), name=name,
    )(ya, yb, proj, proj)


def _into(width, offset):
    assert offset % width == 0
    return pl.BlockSpec((TR, width), lambda i: (i, offset // width))


def _merge_bwd(dm, ya, yb, proj, dproj, name):
    t = ya.shape[0]

    def body(d_ref, a_ref, b_ref, gp_ref, gd_ref, _, da_ref, db_ref, dg_ref):
        dv = d_ref[...]
        sp = _sigmoid(gp_ref[...])
        sd = _sigmoid(gd_ref[...])
        da_ref[...] = dv * sp
        db_ref[...] = (dv * sd).astype(db_ref.dtype)
        dg_ref[:, :D] = (dv * a_ref[...] * sp * (1.0 - sp)).astype(dg_ref.dtype)
        dg_ref[:, D:] = (dv * b_ref[...] * sd * (1.0 - sd)).astype(dg_ref.dtype)

    return pl.pallas_call(
        body, grid=(t // TR,),
        in_specs=[_row(D), _row(D), _row(D), _row(D, OFF_GP // D), _row(D, OFF_GD // D),
                  pl.BlockSpec(memory_space=pl.ANY)],
        out_specs=[_row(D), _row(D), _into(2 * D, OFF_GP)],
        out_shape=[SDS((t, D), F32), SDS((t, D), BF16), SDS(dproj.shape, dproj.dtype)],
        input_output_aliases={5: 2},
        compiler_params=_params(("parallel",)), name=name,
    )(dm, ya, yb, proj, proj, dproj)


def _split2(x):
    hi = x.astype(BF16)
    return hi, (x - hi.astype(F32)).astype(BF16)


def _dot3(a, b, dims):
    ah, al = _split2(a)
    bh, bl = _split2(b)
    return _dg(ah, bh, dims) + (_dg(ah, bl, dims) + _dg(al, bh, dims))


def _neumann_inverses(mats):
    ri = _iota((CH, CH), 0)
    ci = _iota((CH, CH), 1)
    eye = jnp.where(ri == ci, 1.0, 0.0).astype(F32)
    xs = [-a for a in mats]
    ps = [eye + x for x in xs]
    for _ in range(5):
        xs = [_dot3(x, x, NN) for x in xs]
        ps = [p + _dot3(p, x, NN) for p, x in zip(ps, xs)]
    return ps


def _solve_with(inv):
    @jax.custom_vjp
    def solve(a, rhs):
        return _dot3(inv, rhs, NN)

    def fwd(a, rhs):
        sol = _dot3(inv, rhs, NN)
        return sol, sol

    def bwd(sol, d):
        drhs = _dot3(inv, d, TN)
        return -_dot3(drhs, sol, NT), drhs

    solve.defvjp(fwd, bwd)
    return solve


@jax.custom_vjp
def _rows_to_lanes(g64):
    ri = _iota((CH, CH), 0)
    ci = _iota((CH, CH), 1)
    diag = jnp.where(ri == ci, g64, 0.0)
    ones = jnp.ones((CH, CH), BF16)
    hi = diag.astype(BF16)
    rem = diag - hi.astype(F32)
    mid = rem.astype(BF16)
    lo = (rem - mid.astype(F32)).astype(BF16)
    return _dg(ones, hi, NN) + (_dg(ones, mid, NN) + _dg(ones, lo, NN))


def _rows_to_lanes_bwd(_, d):
    ri = _iota((CH, CH), 0)
    ci = _iota((CH, CH), 1)
    return (jnp.where(ri == ci, jnp.broadcast_to(jnp.sum(d, axis=0, keepdims=True), (CH, CH)), 0.0),)


_rows_to_lanes.defvjp(lambda g64: (_rows_to_lanes(g64), None), _rows_to_lanes_bwd)


def _chunk_local(solve_all, q, k, v, g128, g64, gl128, b128, b64):
    ri = _iota((CH, CH), 0)
    ci = _iota((CH, CH), 1)
    causal = ri >= ci
    strict = ri > ci
    gj = [_rows_to_lanes(g) for g in g64]
    decay = [jnp.where(causal, jnp.exp(jnp.where(causal, g - t, 0.0)), 0.0) for g, t in zip(g64, gj)]
    kk = [_nt(x, x) for x in k]
    a = [jnp.where(strict, b * m * dc, 0.0) for b, m, dc in zip(b64, kk, decay)]
    eg = [jnp.exp(g) for g in g128]
    rhs = [jnp.concatenate([b * x, (b * e) * y], axis=1) for b, x, e, y in zip(b128, v, eg, k)]
    sol = solve_all(a, rhs)
    qk = [jnp.where(causal, _nt(x, y) * dc, 0.0) for x, y, dc in zip(q, k, decay)]
    return ([s[:, :HD] for s in sol], [s[:, HD:] for s in sol], qk, [x * e for x, e in zip(q, eg)],
            [x * jnp.exp(gl - g) for x, gl, g in zip(k, gl128, g128)], [jnp.exp(gl) for gl in gl128])


def _all_head_gates(bgv):
    return tuple(list(z) for z in zip(*[_head_gates(bgv, h) for h in range(NH)]))


def _head_gates(bgv, h):
    lane = _iota((CH, 128), 1)
    row = _iota((CH, 128), 0)
    bcol = jnp.sum(jnp.where(lane == h, bgv, 0.0), axis=1, keepdims=True)
    gcol = jnp.sum(jnp.where(lane == NH + h, bgv, 0.0), axis=1, keepdims=True)
    g128 = jnp.broadcast_to(gcol, (CH, 128))
    gl128 = jnp.broadcast_to(jnp.sum(jnp.where(row == CH - 1, g128, 0.0), axis=0, keepdims=True), (CH, 128))
    return (g128, jnp.broadcast_to(gcol, (CH, CH)), gl128,
            jnp.broadcast_to(bcol, (CH, 128)), jnp.broadcast_to(bcol, (CH, CH)))


def _chunk_specs():
    row = pl.BlockSpec((CH, D), lambda i: (i, 0))
    small = pl.BlockSpec((CH, 128), lambda i: (i, 0))
    qk = pl.BlockSpec((NH, CH, CH), lambda i: (i, 0, 0))
    eg = pl.BlockSpec((1, NH, 128), lambda i: (i, 0, 0))
    return row, small, qk, eg


def _dn_local_fwd(q, k, v, bg, name):
    t = q.shape[0]
    n = t // CH

    def body(q_ref, k_ref, v_ref, bg_ref, u_ref, w_ref, qk_ref, qd_ref, kd_ref, eg_ref, inv_ref):
        cols = [slice(h * HD, (h + 1) * HD) for h in range(NH)]

        def solve_all(mats, rhs):
            invs = _neumann_inverses(mats)
            for h in range(NH):
                inv_ref[h] = invs[h]
            return [_dot3(m, r, NN) for m, r in zip(invs, rhs)]

        u, w, qk, qd, kd, egl = _chunk_local(
            solve_all, [q_ref[:, c] for c in cols], [k_ref[:, c] for c in cols], [v_ref[:, c] for c in cols],
            *_all_head_gates(bg_ref[...]))
        for h, hc in enumerate(cols):
            u_ref[:, hc] = u[h]
            w_ref[:, hc] = w[h].astype(w_ref.dtype)
            qd_ref[:, hc] = qd[h].astype(qd_ref.dtype)
            kd_ref[:, hc] = kd[h].astype(kd_ref.dtype)
            qk_ref[h] = qk[h].astype(qk_ref.dtype)
            eg_ref[0, h:h + 1, :] = egl[h][0:1, :]

    row, small, qkb, egb = _chunk_specs()
    return pl.pallas_call(
        body, grid=(n,), in_specs=[row, row, row, small], out_specs=[row, row, qkb, row, row, egb, qkb],
        out_shape=[SDS((t, D), F32), SDS((t, D), BF16), SDS((n * NH, CH, CH), BF16), SDS((t, D), BF16),
                   SDS((t, D), BF16), SDS((n, NH, 128), F32), SDS((n * NH, CH, CH), F32)],
        compiler_params=_params(("parallel",)), name=name,
    )(q, k, v, bg)


def _dn_local_bwd(q, k, v, bg, inv, du, dw, dqk, dqd, dkd, deg, name):
    t = q.shape[0]
    n = t // CH

    def body(q_ref, k_ref, v_ref, bg_ref, inv_ref, du_ref, dw_ref, dqk_ref, dqd_ref, dkd_ref, deg_ref,
             dq_ref, dk_ref, dv_ref, dbg_ref):
        bgv = bg_ref[...]
        lane = _iota((CH, 128), 1)
        row = _iota((CH, 128), 0)
        first = jnp.where(row == 0, 1.0, 0.0)
        acc = jnp.zeros((CH, 128), F32)
        cols = [slice(h * HD, (h + 1) * HD) for h in range(NH)]
        solves = [_solve_with(inv_ref[h]) for h in range(NH)]

        def solve_all(mats, rhs):
            return [f(m, r) for f, m, r in zip(solves, mats, rhs)]

        _, vjp = jax.vjp(functools.partial(_chunk_local, solve_all),
                         [q_ref[:, c] for c in cols], [k_ref[:, c] for c in cols], [v_ref[:, c] for c in cols],
                         *_all_head_gates(bgv))
        cts = ([du_ref[:, c].astype(F32) for c in cols], [dw_ref[:, c].astype(F32) for c in cols],
               [dqk_ref[h] for h in range(NH)],
               [dqd_ref[:, c].astype(F32) for c in cols], [dkd_ref[:, c].astype(F32) for c in cols],
               [jnp.broadcast_to(deg_ref[0, h:h + 1, :], (CH, 128)) * first for h in range(NH)])
        dq, dk, dv, dg128, dg64, dgl, db128, db64 = vjp(cts)
        for h, hc in enumerate(cols):
            dq_ref[:, hc] = dq[h]
            dk_ref[:, hc] = dk[h]
            dv_ref[:, hc] = dv[h]
            dg = jnp.sum(dg128[h], axis=1, keepdims=True) + jnp.sum(dg64[h], axis=1, keepdims=True)
            tot = jnp.sum(jnp.sum(dgl[h], axis=0, keepdims=True), axis=1, keepdims=True)
            dg = dg + jnp.where(row[:, 0:1] == CH - 1, tot, 0.0)
            db = jnp.sum(db128[h], axis=1, keepdims=True) + jnp.sum(db64[h], axis=1, keepdims=True)
            acc = acc + jnp.where(lane == h, db, 0.0) + jnp.where(lane == NH + h, dg, 0.0)
        dbg_ref[...] = acc

    row, small, qkb, egb = _chunk_specs()
    return pl.pallas_call(
        body, grid=(n,), in_specs=[row, row, row, small, qkb, row, row, qkb, row, row, egb],
        out_specs=[row, row, row, small],
        out_shape=[SDS((t, D), F32)] * 3 + [SDS((t, 128), F32)],
        compiler_params=_params(("parallel",)), name=name,
    )(q, k, v, bg, inv, du, dw, dqk, dqd, dkd, deg)


def _state_step(s, u, w, qk, qd, kd, egl):
    ws = [_nn(a, b) for a, b in zip(w, s)]
    v_new = [a - b for a, b in zip(u, ws)]
    qs = [_nn(a, b) for a, b in zip(qd, s)]
    intra = [_nn(a, b) for a, b in zip(qk, v_new)]
    upd = [_tn(a, b) for a, b in zip(kd, v_new)]
    return [a * e + b for a, e, b in zip(s, egl, upd)], [a + b for a, b in zip(qs, intra)]


def _dn_scan_fwd(u, w, qk, qd, kd, eg, name):
    t = u.shape[0]
    n = t // CH
    g = SCAN_CHUNKS

    def body(u_ref, w_ref, qk_ref, qd_ref, kd_ref, eg_ref, o_ref, save_ref, s_ref):
        @pl.when(pl.program_id(0) == 0)
        def _():
            s_ref[...] = jnp.zeros_like(s_ref)

        cols = [slice(h * HD, (h + 1) * HD) for h in range(NH)]
        s = [s_ref[h] for h in range(NH)]
        for c in range(g):
            rows = slice(c * CH, (c + 1) * CH)
            for h in range(NH):
                save_ref[c, h] = s[h].astype(save_ref.dtype)
            s, o = _state_step(
                s, [u_ref[rows, hc] for hc in cols], [w_ref[rows, hc].astype(F32) for hc in cols],
                [qk_ref[c * NH + h].astype(F32) for h in range(NH)], [qd_ref[rows, hc].astype(F32) for hc in cols],
                [kd_ref[rows, hc].astype(F32) for hc in cols], [eg_ref[c, h:h + 1, :] for h in range(NH)])
            for h, hc in enumerate(cols):
                o_ref[rows, hc] = o[h]
        for h in range(NH):
            s_ref[h] = s[h]

    row = pl.BlockSpec((g * CH, D), lambda i: (i, 0))
    qkb = pl.BlockSpec((g * NH, CH, CH), lambda i: (i, 0, 0))
    egb = pl.BlockSpec((g, NH, 128), lambda i: (i, 0, 0))
    return pl.pallas_call(
        body, grid=(n // g,), in_specs=[row, row, qkb, row, row, egb],
        out_specs=[row, pl.BlockSpec((g, NH, HD, HD), lambda i: (i, 0, 0, 0))],
        out_shape=[SDS((t, D), F32), SDS((n, NH, HD, HD), BF16)],
        scratch_shapes=[pltpu.VMEM((NH, HD, HD), F32)],
        compiler_params=_params(("arbitrary",)), name=name,
    )(u, w, qk, qd, kd, eg)


def _dn_scan_bwd(u, w, qk, qd, kd, eg, saved, do, name):
    t = u.shape[0]
    n = t // CH
    g = SCAN_CHUNKS
    last = n // g - 1

    def body(u_ref, w_ref, qk_ref, qd_ref, kd_ref, eg_ref, sv_ref, do_ref,
             du_ref, dw_ref, dqk_ref, dqd_ref, dkd_ref, deg_ref, ds_ref):
        @pl.when(pl.program_id(0) == 0)
        def _():
            ds_ref[...] = jnp.zeros_like(ds_ref)

        cols = [slice(h * HD, (h + 1) * HD) for h in range(NH)]
        ds = [ds_ref[h] for h in range(NH)]
        for c in reversed(range(g)):
            rows = slice(c * CH, (c + 1) * CH)
            _, vjp = jax.vjp(
                _state_step, [sv_ref[c, h].astype(F32) for h in range(NH)], [u_ref[rows, hc] for hc in cols],
                [w_ref[rows, hc].astype(F32) for hc in cols], [qk_ref[c * NH + h].astype(F32) for h in range(NH)],
                [qd_ref[rows, hc].astype(F32) for hc in cols], [kd_ref[rows, hc].astype(F32) for hc in cols],
                [eg_ref[c, h:h + 1, :] for h in range(NH)])
            ds, du, dw, dqk, dqd, dkd, deg = vjp((ds, [do_ref[rows, hc] for hc in cols]))
            for h, hc in enumerate(cols):
                du_ref[rows, hc] = du[h].astype(du_ref.dtype)
                dw_ref[rows, hc] = dw[h].astype(dw_ref.dtype)
                dqk_ref[c * NH + h] = dqk[h]
                dqd_ref[rows, hc] = dqd[h].astype(dqd_ref.dtype)
                dkd_ref[rows, hc] = dkd[h].astype(dkd_ref.dtype)
                deg_ref[c, h:h + 1, :] = deg[h]
        for h in range(NH):
            ds_ref[h] = ds[h]

    row = pl.BlockSpec((g * CH, D), lambda i: (last - i, 0))
    qkb = pl.BlockSpec((g * NH, CH, CH), lambda i: (last - i, 0, 0))
    egb = pl.BlockSpec((g, NH, 128), lambda i: (last - i, 0, 0))
    return pl.pallas_call(
        body, grid=(n // g,),
        in_specs=[row, row, qkb, row, row, egb,
                  pl.BlockSpec((g, NH, HD, HD), lambda i: (last - i, 0, 0, 0)), row],
        out_specs=[row, row, qkb, row, row, egb],
        out_shape=[SDS((t, D), BF16), SDS((t, D), BF16), SDS((n * NH, CH, CH), F32), SDS((t, D), BF16),
                   SDS((t, D), BF16), SDS((n, NH, 128), F32)],
        scratch_shapes=[pltpu.VMEM((NH, HD, HD), F32)],
        compiler_params=_params(("arbitrary",)), name=name,
    )(u, w, qk, qd, kd, eg, saved, do)


def _ada_fwd(c_all, ada_w, ada_b, name):
    ncol = ada_w.shape[1]

    def body(c_ref, w_ref, b_ref, o_ref):
        o_ref[...] = _dg(_silu(c_ref[...]), w_ref[...], NN, HI) + b_ref[...]

    return pl.pallas_call(body, out_shape=SDS((NDEV, ncol), F32),
                          compiler_params=pltpu.CompilerParams(vmem_limit_bytes=VMEM_LIMIT), name=name,
                          )(c_all, ada_w, ada_b)


def _ada_bwd(c_all_t, dmod, name):
    ncol = dmod.shape[1]

    def body(c_ref, d_ref, o_ref):
        sc = _silu(c_ref[...])
        acc = sc[:, 0:1] * d_ref[0:1, :]
        for b in range(1, NDEV):
            acc = acc + sc[:, b:b + 1] * d_ref[b:b + 1, :]
        o_ref[...] = acc

    return pl.pallas_call(body, out_shape=SDS((D, ncol), F32),
                          compiler_params=pltpu.CompilerParams(vmem_limit_bytes=VMEM_LIMIT), name=name,
                          )(c_all_t, dmod)


def _sum_devices(parts, out_dtype, name):
    _, r, c = parts.shape
    tr = TR if r % TR == 0 else r

    def body(p_ref, o_ref):
        acc = p_ref[0].astype(F32)
        for i in range(1, NDEV):
            acc = acc + p_ref[i].astype(F32)
        o_ref[...] = acc.astype(o_ref.dtype)

    return pl.pallas_call(
        body, grid=(r // tr,), in_specs=[pl.BlockSpec((NDEV, tr, c), lambda i: (0, i, 0))],
        out_specs=pl.BlockSpec((tr, c), lambda i: (i, 0)), out_shape=SDS((r, c), out_dtype),
        compiler_params=_params(("parallel",)), name=name,
    )(parts)


def _adam_tiles(r, c):
    if r % 8 == 0:
        return _pick(r, (256, 352, 128, 8)), c
    return r, (256 if c % 256 == 0 else c)


def _adam_math(w, gv, m, v):
    m_new = ADAM_B1 * m + (1.0 - ADAM_B1) * gv
    v_new = ADAM_B2 * v + (1.0 - ADAM_B2) * (gv * gv)
    bc1 = 1.0 - ADAM_B1 ** ADAM_STEP
    bc2 = 1.0 - ADAM_B2 ** ADAM_STEP
    return -ADAM_LR * ((m_new / bc1) / (jnp.sqrt(v_new / bc2) + ADAM_EPS) + ADAM_WD * w), m_new, v_new


def _adamw(w, g, m, v, name):
    r, c = w.shape
    tr, tc = _adam_tiles(r, c)

    def body(w_ref, g_ref, m_ref, v_ref, d_ref, nm_ref, nv_ref):
        d_ref[...], nm_ref[...], nv_ref[...] = _adam_math(w_ref[...], g_ref[...], m_ref[...], v_ref[...])

    spec = pl.BlockSpec((tr, tc), lambda i, j: (i, j))
    return pl.pallas_call(
        body, grid=(r // tr, c // tc), in_specs=[spec] * 4, out_specs=[spec] * 3,
        out_shape=[SDS((r, c), F32)] * 3, compiler_params=_params(("parallel", "parallel")), name=name,
    )(w, g, m, v)


def _reduce_adamw(parts, w, m, v, name):
    r, c = w.shape
    tr, tc = _adam_tiles(r, c)

    def body(p_ref, w_ref, m_ref, v_ref, g_ref, d_ref, nm_ref, nv_ref):
        gv = p_ref[0].astype(F32)
        for i in range(1, NDEV):
            gv = gv + p_ref[i].astype(F32)
        g_ref[...] = gv
        d_ref[...], nm_ref[...], nv_ref[...] = _adam_math(w_ref[...], gv, m_ref[...], v_ref[...])

    spec = pl.BlockSpec((tr, tc), lambda i, j: (i, j))
    return pl.pallas_call(
        body, grid=(r // tr, c // tc),
        in_specs=[pl.BlockSpec((NDEV, tr, tc), lambda i, j: (0, i, j))] + [spec] * 3, out_specs=[spec] * 4,
        out_shape=[SDS((r, c), F32)] * 4, compiler_params=_params(("parallel", "parallel")), name=name,
    )(parts, w, m, v)


ANY = pl.BlockSpec(memory_space=pl.ANY)
MESH = pl.DeviceIdType.MESH


def _all_gather(xs, name, after=None):
    n = len(xs)
    extra = [] if after is None else [after]

    def body(*refs):
        x_refs, out_refs = refs[:n], refs[n + len(extra):2 * n + len(extra)]
        send_sems, recv_sems, local_sems = refs[-3:]
        mx, my, mc = lax.axis_index("x"), lax.axis_index("y"), lax.axis_index("c")
        me, sibling = (mx, my, mc), (mx, my, 1 - mc)
        chips = [(1 - mx, my), (mx, 1 - my), (1 - mx, 1 - my)]

        def rows(a, px, py, pc):
            return out_refs[a].at[4 * px + 2 * py + pc]

        def copy(a, k, block, to, src=None):
            return pltpu.make_async_remote_copy(
                src_ref=rows(a, *block) if src is None else src, dst_ref=rows(a, *block),
                send_sem=send_sems.at[a, k], recv_sem=recv_sems.at[a, k], device_id=to, device_id_type=MESH)

        mine = [pltpu.make_async_copy(x_refs[a], rows(a, *me), local_sems.at[a]) for a in range(n)]
        for cp in mine:
            cp.start()
        first = []
        for a in range(n):
            first.append(copy(a, 0, me, sibling, src=x_refs[a]))
            first += [copy(a, 1 + j, me, (*chip, mc), src=x_refs[a]) for j, chip in enumerate(chips)]
        for cp in first:
            cp.start()
        passed = []
        for a in range(n):
            for j, chip in enumerate(chips):
                copy(a, 1 + j, (*chip, mc), me).wait_recv()
                passed.append(copy(a, 4 + j, (*chip, mc), sibling))
                passed[-1].start()
        for a in range(n):
            copy(a, 0, sibling, me).wait_recv()
            for j, chip in enumerate(chips):
                copy(a, 4 + j, (*chip, 1 - mc), me).wait_recv()
        for cp in first + passed:
            cp.wait_send()
        for cp in mine:
            cp.wait()

    return pl.pallas_call(
        body, out_shape=[SDS((NDEV,) + x.shape, x.dtype) for x in xs], in_specs=[ANY] * (n + len(extra)),
        out_specs=[ANY] * n,
        scratch_shapes=[pltpu.SemaphoreType.DMA((n, 7)), pltpu.SemaphoreType.DMA((n, 7)),
                        pltpu.SemaphoreType.DMA((n,))],
        name=name,
    )(*xs, *extra)


HBM = pl.BlockSpec(memory_space=pltpu.HBM)
SEM = pl.BlockSpec(memory_space=pltpu.SEMAPHORE)
EFFECT = pltpu.SideEffectType.DATAFLOW_SIDE_EFFECTING


def _peers():
    mx, my, mc = lax.axis_index("x"), lax.axis_index("y"), lax.axis_index("c")
    out = []
    for k in range(1, NDEV):
        out.append((1 - mx if k & 4 else mx, 1 - my if k & 2 else my, 1 - mc if k & 1 else mc))
    return 4 * mx + 2 * my + mc, out


def _push_start(srcs, sliced, name, after=None):
    n = len(srcs)
    extra = [] if after is None else [after]
    lands = [lax.empty(s.shape if sliced else (NDEV,) + s.shape, s.dtype) for s in srcs]

    def body(*refs):
        src_refs, land_refs = refs[:n], refs[n:2 * n]
        outs = refs[2 * n + len(extra):]
        send_sems, recv_sems = outs[:n], outs[n:2 * n]
        token = refs[-1]
        me, peers = _peers()
        for a in range(n):
            for k, (px, py, pc) in enumerate(peers):
                src = src_refs[a].at[4 * px + 2 * py + pc] if sliced else src_refs[a]
                pltpu.make_async_remote_copy(
                    src_ref=src, dst_ref=land_refs[a].at[me], send_sem=send_sems[a].at[k],
                    recv_sem=recv_sems[a].at[k], device_id=(px, py, pc), device_id_type=MESH).start()
            pltpu.make_async_copy(src_refs[a].at[me] if sliced else src_refs[a], land_refs[a].at[me],
                                  send_sems[a].at[NDEV - 1]).start()
        token[...] = jnp.zeros_like(token)

    outs = pl.pallas_call(
        body, name=name,
        out_shape=([pltpu.SemaphoreType.DMA((NDEV,))] * n + [pltpu.SemaphoreType.DMA((NDEV - 1,))] * n
                   + [pltpu.HBM(s.shape, s.dtype) for s in srcs] + [pltpu.HBM(l.shape, l.dtype) for l in lands]
                   + [SDS((8, 128), F32)]),
        in_specs=[HBM] * (2 * n) + [pl.BlockSpec(memory_space=pl.ANY)] * len(extra),
        out_specs=[SEM] * (2 * n) + [HBM] * (2 * n) + [pl.BlockSpec(memory_space=pltpu.VMEM)],
        input_output_aliases={i: 2 * n + i for i in range(2 * n)},
        compiler_params=pltpu.CompilerParams(has_side_effects=EFFECT),
    )(*[pltpu.with_memory_space_constraint(s, pltpu.HBM) for s in srcs],
      *[pltpu.with_memory_space_constraint(l, pltpu.HBM) for l in lands], *extra)
    sends, recvs = outs[:n], outs[n:2 * n]
    src_thru, land_thru = outs[2 * n:3 * n], outs[3 * n:4 * n]
    return [(sends[a], recvs[a], src_thru[a], land_thru[a]) for a in range(n)], outs[-1]


def _push_wait(started, sliced, after, name):
    n = len(started)

    def body(*refs):
        src_refs, land_refs = refs[:n], refs[n:2 * n]
        send_sems, recv_sems = refs[2 * n:3 * n], refs[3 * n:4 * n]
        me, peers = _peers()
        for a in range(n):
            for k, (px, py, pc) in enumerate(peers):
                src = src_refs[a].at[4 * px + 2 * py + pc] if sliced else src_refs[a]
                cp = pltpu.make_async_remote_copy(
                    src_ref=src, dst_ref=land_refs[a].at[me], send_sem=send_sems[a].at[k],
                    recv_sem=recv_sems[a].at[k], device_id=(px, py, pc), device_id_type=MESH)
                cp.wait_send()
                cp.wait_recv()
            pltpu.make_async_copy(src_refs[a].at[me] if sliced else src_refs[a], land_refs[a].at[me],
                                  send_sems[a].at[NDEV - 1]).wait()

    srcs = [s[2] for s in started]
    lands = [s[3] for s in started]
    outs = pl.pallas_call(
        body, name=name,
        out_shape=[pltpu.HBM(s.shape, s.dtype) for s in srcs] + [pltpu.HBM(l.shape, l.dtype) for l in lands],
        in_specs=[HBM] * (2 * n) + [SEM] * (2 * n) + [pl.BlockSpec(memory_space=pl.ANY)],
        out_specs=[HBM] * (2 * n),
        input_output_aliases={i: i for i in range(2 * n)},
        compiler_params=pltpu.CompilerParams(has_side_effects=EFFECT),
    )(*srcs, *lands, *[s[0] for s in started], *[s[1] for s in started], after)
    return outs[n:]


def _cols_from_blocks(blocks):
    _, rows, w = blocks.shape
    return blocks.transpose(1, 0, 2).reshape(rows, NDEV * w)


def _cols_to_blocks(full):
    rows, total = full.shape
    return full.reshape(rows, NDEV, total // NDEV).transpose(1, 0, 2)


def _mix_pad(wt):
    xp, q, k, v, z, ba, gp, gd = jnp.split(wt, (512, 1536, 2560, 3584, 4608, 4624, 5648), axis=0)
    pad = jnp.zeros((MIXP - OFF_BA - 16, wt.shape[1]), wt.dtype)
    return jnp.concatenate([q, k, v, z, gp, gd, xp, ba, pad], axis=0)


def _mix_unpad(wt):
    q, k, v, z, gp, gd, xp, ba = (wt[OFF_Q:OFF_K], wt[OFF_K:OFF_V], wt[OFF_V:OFF_Z], wt[OFF_Z:OFF_GP],
                                  wt[OFF_GP:OFF_GD], wt[OFF_GD:OFF_XP], wt[OFF_XP:OFF_BA], wt[OFF_BA:OFF_BA + 16])
    return jnp.concatenate([xp, q, k, v, z, ba, gp, gd], axis=0)


def _lane_row(vec8):
    return jnp.zeros((1, 128), F32).at[0, NH:2 * NH].set(vec8)


def _ffn_fwd(x, h, gate, w_in, w_out, tag, next_norm=None, token=None, start_more=None):
    if isinstance(w_in, tuple):
        w_in, = _push_wait([w_in], False, h, f"{tag}_gather_wait_in")
    w_in = w_in.reshape(2 * FH, D)
    u, a = _swiglu_up(h, w_in, f"{tag}_up", after=token)
    w_out, = _push_wait([w_out], False, a, f"{tag}_gather_wait_out")
    w_out = w_out.reshape(FH, D)
    outs = _matmul_residual(a, w_out, x, gate, 0.5, a_blk=True, norm=next_norm, name=f"{tag}_down",
                            after=None if start_more is None else start_more(h))
    return outs[0], (h, u, a, outs[1]), w_in, w_out, (outs[2] if next_norm else None)


def _ffn_bwd(dx_out, x, g, scale, gate, w_in, w_out, saved, tag):
    h, u, a, y = saved
    t = x.shape[0]
    dy, dgate = _resid_bwd(dx_out, y, gate, 0.5, f"{tag}_res_bwd")
    dw_out = _matmul(a, dy, ta=True, a_blk=True, out_dtype=BF16, name=f"{tag}_down_dw")
    sent_out, token = _push_start([dw_out.reshape(NDEV, FH // NDEV, D)], True, f"{tag}_grad_start_out")
    du = _swiglu_down_bwd(dy, w_out, u, f"{tag}_down_dx", after=token).reshape(NDEV, t, FB)
    dw_in = _matmul(du, h, ta=True, a_blk=True, out_dtype=BF16, name=f"{tag}_up_dw")
    sent_in, token = _push_start([dw_in.reshape(NDEV, FB, D)], True, f"{tag}_grad_start_in")
    dh = _matmul(du, w_in, a_blk=True, out_dtype=F32, name=f"{tag}_up_dx", after=token)
    dx, dshift, dscale, dg = _norm_mod_bwd(x, g, scale, dh, dx_out, f"{tag}_norm_bwd")
    return dx, (dshift, dscale, dgate), dg, sent_in + sent_out


def kernel(x, c, ada_w, ada_b, norm_g, ffn1_w_in, ffn1_w_out, ffn2_w_in, ffn2_w_out, mix_w_in, conv_w, a_log, dt_bias, dn_norm_g, pool_w, pool_scale, pool_proj, dn_proj, mix_w_out, final_g, loss_target, m_ada_w, m_ada_b, m_norm_g, m_ffn1_w_in, m_ffn1_w_out, m_ffn2_w_in, m_ffn2_w_out, m_mix_w_in, m_conv_w, m_a_log, m_dt_bias, m_dn_norm_g, m_pool_w, m_pool_scale, m_pool_proj, m_dn_proj, m_mix_w_out, m_final_g, v_ada_w, v_ada_b, v_norm_g, v_ffn1_w_in, v_ffn1_w_out, v_ffn2_w_in, v_ffn2_w_out, v_mix_w_in, v_conv_w, v_a_log, v_dt_bias, v_dn_norm_g, v_pool_w, v_pool_scale, v_pool_proj, v_dn_proj, v_mix_w_out, v_final_g):
    me = 4 * lax.axis_index("x") + 2 * lax.axis_index("y") + lax.axis_index("c")
    x0 = x[0]
    target = loss_target[0]
    t = x0.shape[0]

    big = [ffn1_w_in[0], ffn1_w_out[0], ffn2_w_in[0], ffn2_w_out[0], mix_w_in[0], pool_proj[0], dn_proj[0],
           mix_w_out[0]]
    small = jnp.concatenate([c.reshape(8, 128), conv_w[0].reshape(12, 128), norm_g[0].reshape(3, 128),
                             jnp.zeros((1, 128), F32)], axis=0)
    small_all, = _all_gather([small], "gather_small")
    c_all = small_all[:, 0:8, :].reshape(NDEV, D)
    conv_full = small_all[:, 8:20, :].reshape(NDEV, 4, 384).transpose(1, 0, 2).reshape(4, 3 * D)
    norm_full = small_all[:, 20:23, :].reshape(NDEV, 3, 128).transpose(1, 0, 2).reshape(3, D)

    ncol = ada_w.shape[2]
    ada_b_mine = lax.dynamic_slice(ada_b, (0, me * ncol), (1, ncol))
    mod_cols = _ada_fwd(c_all, ada_w[0], ada_b_mine, "ada_fwd")
    transposed = (0, 2, 4)
    payload = [(w.T if i in transposed else w).astype(BF16) for i, w in enumerate(big)]
    mod_all, w_in1 = _all_gather([mod_cols, payload[0]], "gather_mod_first_weight")
    started, token = _push_start([payload[1], payload[4]], False, "gather_start", after=mod_all)
    started = {1: started[0], 4: started[1]}

    def start_rest(h):
        more, token = _push_start([payload[i] for i in (5, 6, 7, 2, 3)], False, "gather_start_rest", after=h)
        started.update(zip((5, 6, 7, 2, 3), more))
        return token

    mod = lax.dynamic_index_in_dim(mod_all, me, axis=1, keepdims=False).reshape(9, D)
    shift = [mod[3 * s:3 * s + 1] for s in range(3)]
    scale = [mod[3 * s + 1:3 * s + 2] for s in range(3)]
    gate = [mod[3 * s + 2:3 * s + 3] for s in range(3)]
    ng = [norm_full[s:s + 1] for s in range(3)]
    fg = final_g.reshape(1, D)
    al_row = _lane_row(a_log[0])
    dt_row = _lane_row(dt_bias[0])
    gn = dn_norm_g
    pw = pool_w[0]
    ps = pool_scale

    h0 = _norm_mod_fwd(x0, ng[0], shift[0], scale[0], "ffn1_norm")
    x1, saved1, w_in1, w_out1, h1 = _ffn_fwd(x0, h0, gate[0], w_in1, started[1], "ffn1",
                                             (ng[1], shift[1], scale[1]), token, start_rest)

    seg, = _push_wait([started[4]], False, h1, "mix_gather_wait")
    w_mix = _mix_pad(seg.reshape(MIX_RAW, D))
    proj = _matmul(h1, w_mix, tb=True, out_dtype=F32, name="mix_in")
    qh, kh, vh, bg = _dn_pre_fwd(proj, conv_full, al_row, dt_row, "dn_pre")
    seg = _push_wait([started[i] for i in (5, 6, 7)], False, qh, "mix_gather_wait_rest")
    w_pp = _cols_from_blocks(seg[0])
    w_dn = seg[1].reshape(D, D)
    w_mo = seg[2].reshape(D, D)
    ya = _pool_fwd(proj, pw, ps, w_pp, "pool_fwd")
    u, w, qk, qd, kd, eg, inv = _dn_local_fwd(qh, kh, vh, bg, "dn_local")
    o, s_saved = _dn_scan_fwd(u, w, qk, qd, kd, eg, "dn_scan")
    ob = _dn_post_fwd(o, proj, gn, "dn_post")
    yb = _matmul(ob, w_dn, out_dtype=F32, name="dn_out")
    merged = _merge_fwd(ya, yb, proj, "merge")
    x2, mix_y, h2 = _matmul_residual(merged, w_mo, x1, gate[1], 1.0, norm=(ng[2], shift[2], scale[2]),
                                     name="mix_out")

    x3, saved2, w_in2, w_out2, _ = _ffn_fwd(x2, h2, gate[2], started[2], started[3], "ffn2")
    loss_row, dx3, dfg = _final_loss(x3, fg, target, "loss")

    dx2, dmod2, dng2, sent2 = _ffn_bwd(dx3, x2, ng[2], scale[2], gate[2], w_in2, w_out2, saved2, "ffn2")

    dmy, dgate1 = _resid_bwd(dx2, mix_y, gate[1], 1.0, "mix_res_bwd")
    dmerged = _matmul(dmy, w_mo, tb=True, out_dtype=F32, name="mix_out_dx")
    dw_mo = _matmul(merged, dmy, ta=True, out_dtype=BF16, name="mix_out_dw")
    dproj = lax.empty((t, MIXP), BF16)
    dya, dyb, dproj = _merge_bwd(dmerged, ya, yb, proj, dproj, "merge_bwd")
    dob = _matmul(dyb, w_dn, tb=True, out_dtype=F32, name="dn_out_dx")
    dw_dn = _matmul(ob, dyb, ta=True, out_dtype=BF16, name="dn_out_dw")
    do, dproj, dgn = _dn_post_bwd(o, proj, gn, dob, dproj, "dn_post_bwd")
    du, dw, dqk, dqd, dkd, deg = _dn_scan_bwd(u, w, qk, qd, kd, eg, s_saved, do, "dn_scan_bwd")
    dqh, dkh, dvh, dbg = _dn_local_bwd(qh, kh, vh, bg, inv, du, dw, dqk, dqd, dkd, deg, "dn_local_bwd")
    dconv, dproj, dal, ddt = _dn_pre_bwd_act(proj, conv_full, al_row, dt_row, dqh, dkh, dvh, dbg, dproj,
                                             "dn_pre_bwd_act")
    dproj, dcw = _dn_pre_bwd_conv(proj, conv_full, dconv, dproj, "dn_pre_bwd_conv")
    dwin, dpl, dpw, dps, dpp = _pool_bwd_local(proj, pw, ps, w_pp, dya, "pool_bwd_local")
    dproj = _pool_bwd_window(dwin, dpl, dproj, "pool_bwd_window")
    dw_mix = _matmul(dproj, h1, ta=True, out_dtype=BF16, name="mix_in_dw")
    sent1, token = _push_start(
        [_mix_unpad(dw_mix).reshape(NDEV, MIX_RAW // NDEV, D), _cols_to_blocks(dpp.astype(BF16)),
         dw_dn.reshape(NDEV, -1, D), dw_mo.reshape(NDEV, -1, D)], True, "mix_grad_start")
    dh1 = _matmul(dproj, w_mix, out_dtype=F32, name="mix_in_dx", after=token)
    dx1, dsh1, dsc1, dng1 = _norm_mod_bwd(x1, ng[1], scale[1], dh1, dx2, "mix_norm_bwd")

    dx0, dmod0, dng0, sent0 = _ffn_bwd(dx1, x0, ng[0], scale[0], gate[0], w_in1, w_out1, saved1, "ffn1")

    dmod = jnp.concatenate([*dmod0, dsh1, dsc1, dgate1, *dmod2], axis=1).reshape(-1)
    flat = jnp.concatenate([
        dmod, dal[0, NH:2 * NH], ddt[0, NH:2 * NH], dgn.reshape(-1), dps.reshape(-1), dfg.reshape(-1),
        dpw.reshape(-1), jnp.concatenate([dng0, dng1, dng2], axis=0).reshape(-1), dcw.reshape(-1),
        loss_row[0, 0:1]])
    nflat = 90 * D
    flat = jnp.concatenate([flat, jnp.zeros((nflat - flat.shape[0],), F32)]).reshape(90, D)
    sent_small, small_token = _push_start([flat], False, "small_grad_start")

    def small_grads(flat_all):
        tot = _sum_devices(flat_all, F32, "sum_small_grads").reshape(-1)
        dmod_all = flat_all.reshape(NDEV, nflat)[:, :9 * D]
        dmod_cols = lax.dynamic_slice(dmod_all, (0, me * ncol), (NDEV, ncol))
        g_ada_w = _ada_bwd(c_all.T, dmod_cols, "ada_bwd")
        p = 0
        pieces = {}
        for nm, size in (("ada_b", 9 * D), ("a_log", NH), ("dt_bias", NH), ("dn_norm_g", HD), ("pool_scale", PW),
                         ("final_g", D), ("pool_w", 4 * PG * PG), ("norm_g", 3 * D), ("conv_w", 12 * D),
                         ("loss", 1)):
            pieces[nm] = tot[p:p + size]
            p += size
        g_norm = lax.dynamic_slice(pieces["norm_g"].reshape(3, D), (0, me * 128), (3, 128))
        g_conv = lax.dynamic_slice(pieces["conv_w"].reshape(4, 3 * D), (0, me * 384), (4, 384))
        return pieces["loss"][0], {
            "ada_w": g_ada_w.reshape(ada_w.shape), "ada_b": pieces["ada_b"].reshape(ada_b.shape),
            "norm_g": g_norm.reshape(norm_g.shape), "conv_w": g_conv.reshape(conv_w.shape),
            "a_log": pieces["a_log"].reshape(a_log.shape), "dt_bias": pieces["dt_bias"].reshape(dt_bias.shape),
            "dn_norm_g": pieces["dn_norm_g"].reshape(dn_norm_g.shape),
            "pool_w": pieces["pool_w"].reshape(pool_w.shape),
            "pool_scale": pieces["pool_scale"].reshape(pool_scale.shape),
            "final_g": pieces["final_g"].reshape(final_g.shape),
        }

    grads = {}
    weights = {"ada_w": ada_w, "ada_b": ada_b, "norm_g": norm_g, "ffn1_w_in": ffn1_w_in, "ffn1_w_out": ffn1_w_out,
               "ffn2_w_in": ffn2_w_in, "ffn2_w_out": ffn2_w_out, "mix_w_in": mix_w_in, "conv_w": conv_w,
               "a_log": a_log, "dt_bias": dt_bias, "dn_norm_g": dn_norm_g, "pool_w": pool_w,
               "pool_scale": pool_scale, "pool_proj": pool_proj, "dn_proj": dn_proj, "mix_w_out": mix_w_out,
               "final_g": final_g}
    m_in = {"ada_w": m_ada_w, "ada_b": m_ada_b, "norm_g": m_norm_g, "ffn1_w_in": m_ffn1_w_in,
            "ffn1_w_out": m_ffn1_w_out, "ffn2_w_in": m_ffn2_w_in, "ffn2_w_out": m_ffn2_w_out,
            "mix_w_in": m_mix_w_in, "conv_w": m_conv_w, "a_log": m_a_log, "dt_bias": m_dt_bias,
            "dn_norm_g": m_dn_norm_g, "pool_w": m_pool_w, "pool_scale": m_pool_scale, "pool_proj": m_pool_proj,
            "dn_proj": m_dn_proj, "mix_w_out": m_mix_w_out, "final_g": m_final_g}
    v_in = {"ada_w": v_ada_w, "ada_b": v_ada_b, "norm_g": v_norm_g, "ffn1_w_in": v_ffn1_w_in,
            "ffn1_w_out": v_ffn1_w_out, "ffn2_w_in": v_ffn2_w_in, "ffn2_w_out": v_ffn2_w_out,
            "mix_w_in": v_mix_w_in, "conv_w": v_conv_w, "a_log": v_a_log, "dt_bias": v_dt_bias,
            "dn_norm_g": v_dn_norm_g, "pool_w": v_pool_w, "pool_scale": v_pool_scale, "pool_proj": v_pool_proj,
            "dn_proj": v_dn_proj, "mix_w_out": v_mix_w_out, "final_g": v_final_g}

    names = list(weights)
    large = ("ada_w", "ffn1_w_in", "ffn1_w_out", "ffn2_w_in", "ffn2_w_out", "mix_w_in", "pool_proj", "dn_proj",
             "mix_w_out")
    delta, new_m, new_v = {}, {}, {}

    flipped = ("ffn1_w_in", "ffn2_w_in", "mix_w_in")

    def views(nm):
        shp = weights[nm].shape
        two_d = (shp[-2], shp[-1])
        if nm in flipped:
            return (lambda a: a.reshape(two_d).T), (lambda a: a.T.reshape(shp))
        return (lambda a: a.reshape(two_d)), (lambda a: a.reshape(shp))

    def reduce_update(sent, group, after, tag):
        for nm, r in zip(group, _push_wait(sent, True, after, f"{tag}_grad_wait")):
            view, back = views(nm)
            g_, d_, m_, v_ = _reduce_adamw(r, view(weights[nm]), view(m_in[nm]), view(v_in[nm]), f"adamw_{nm}")
            grads[nm], delta[nm], new_m[nm], new_v[nm] = back(g_), back(d_), back(m_), back(v_)
        return d_

    done = reduce_update(sent2, ("ffn2_w_in", "ffn2_w_out"), small_token, "ffn2")
    done = reduce_update(sent1, ("mix_w_in", "pool_proj", "dn_proj", "mix_w_out"), done, "mix")
    flat_all, = _push_wait(sent_small, False, done, "small_grad_wait")
    loss, small = small_grads(flat_all)
    grads.update(small)
    view, back = views("ada_w")
    done, m_, v_ = _adamw(view(ada_w), view(grads["ada_w"]), view(m_ada_w), view(v_ada_w), "adamw_ada_w")
    delta["ada_w"], new_m["ada_w"], new_v["ada_w"] = back(done), back(m_), back(v_)
    reduce_update(sent0, ("ffn1_w_in", "ffn1_w_out"), done, "ffn1")
    rest = [nm for nm in names if nm not in large]
    total = sum(weights[nm].size for nm in rest)
    padded = -(-total // D) * D

    def pack(tree, fill):
        flat_ = jnp.concatenate([tree[nm].reshape(-1) for nm in rest])
        return jnp.concatenate([flat_, jnp.full((padded - total,), fill, F32)]).reshape(-1, D)

    d_, m_, v_ = _adamw(pack(weights, 0.0), pack(grads, 0.0), pack(m_in, 0.0), pack(v_in, 1.0), "adamw_small")
    p = 0
    for nm in rest:
        size = weights[nm].size
        shp = weights[nm].shape
        delta[nm] = d_.reshape(-1)[p:p + size].reshape(shp)
        new_m[nm] = m_.reshape(-1)[p:p + size].reshape(shp)
        new_v[nm] = v_.reshape(-1)[p:p + size].reshape(shp)
        p += size

    grad_x = dx0.reshape(x.shape)
    return (loss, grad_x, *[grads[nm] for nm in names], *[delta[nm] for nm in names],
            *[new_m[nm] for nm in names], *[new_v[nm] for nm in names])
```

```python
import functools

import jax
import jax.numpy as jnp
from jax import lax
from jax.experimental import pallas as pl
from jax.experimental.pallas import tpu as pltpu

F32 = jnp.float32
BF16 = jnp.bfloat16
SDS = jax.ShapeDtypeStruct
HI = lax.Precision.HIGHEST

D = 1024
FH = 2816
FB = 704
NH = 8
HD = 128
CH = 64
SCAN_CHUNKS = 2
NDEV = 8
PW = 512
PG = 128
RMS_EPS = 1e-6
L2_EPS = 1e-6
TR = 512
HALO = 16
VMEM_LIMIT = 56 * 1024 * 1024

MIXP = 6912
OFF_Q, OFF_K, OFF_V, OFF_Z, OFF_GP, OFF_GD, OFF_XP, OFF_BA = 0, 1024, 2048, 3072, 4096, 5120, 6144, 6656
MIX_RAW = 6672

ADAM_LR = 0.001
ADAM_B1 = 0.9
ADAM_B2 = 0.999
ADAM_EPS = 1e-08
ADAM_WD = 0.01
ADAM_STEP = 10

NN = (((1,), (0,)), ((), ()))
NT = (((1,), (1,)), ((), ()))
TN = (((0,), (0,)), ((), ()))


def _dg(a, b, dims, prec=None):
    return lax.dot_general(a, b, dims, precision=prec, preferred_element_type=F32)


def _make_dots(prec):
    @jax.custom_vjp
    def nn(a, b):
        return _dg(a, b, NN, prec)

    @jax.custom_vjp
    def nt(a, b):
        return _dg(a, b, NT, prec)

    @jax.custom_vjp
    def tn(a, b):
        return _dg(a, b, TN, prec)

    nn.defvjp(lambda a, b: (nn(a, b), (a, b)), lambda r, d: (nt(d, r[1]), tn(r[0], d)))
    nt.defvjp(lambda a, b: (nt(a, b), (a, b)), lambda r, d: (nn(d, r[1]), tn(d, r[0])))
    tn.defvjp(lambda a, b: (tn(a, b), (a, b)), lambda r, d: (nt(r[1], d), nn(r[0], d)))
    return nn, nt, tn


_nn, _nt, _tn = _make_dots(None)


def _params(sem):
    return pltpu.CompilerParams(dimension_semantics=sem, vmem_limit_bytes=VMEM_LIMIT)


def _sigmoid(x):
    return 1.0 / (1.0 + jnp.exp(-x))


def _silu(x):
    return x * _sigmoid(x)


def _dsilu(x):
    s = _sigmoid(x)
    return s * (1.0 + x * (1.0 - s))


def _pick(n, cands):
    for c in cands:
        if n % c == 0:
            return c
    raise ValueError(f"no tile for {n}")


def _iota(shape, dim):
    return lax.broadcasted_iota(jnp.int32, shape, dim)


def _matmul(a, b, *, ta=False, tb=False, a_blk=False, b_blk=False, o_blk=False, tm=None, tn=None, tk=None,
            out_dtype, name, after=None):
    if a_blk:
        nb, r, cb = a.shape
        if ta:
            k_dim, m_dim, tm = r, nb * cb, cb
        else:
            m_dim, k_dim, tk = r, nb * cb, cb
    else:
        k_dim, m_dim = a.shape if ta else a.shape[::-1]
    if b_blk:
        nb, r, cb = b.shape
        if tb:
            n_dim, tk = r, cb
            assert nb * cb == k_dim
        else:
            n_dim, tn = nb * cb, cb
            assert r == k_dim
    else:
        n_dim = b.shape[0] if tb else b.shape[1]
    tm = tm or _pick(m_dim, (1024, 768, 512, 256, 128))
    tn = tn or _pick(n_dim, (1024, 768, 512, 256, 128))
    tk = tk or (k_dim if (k_dim <= 2816 and not ta) else _pick(k_dim, (2816, 2304, 1024, 512, 256)))
    nk = k_dim // tk
    dims = ((((0,) if ta else (1,)), ((1,) if tb else (0,))), ((), ()))

    def body(a_ref, b_ref, *rest):
        o_ref, acc_ref = rest[-2:]
        k = pl.program_id(2)

        @pl.when(k == 0)
        def _():
            acc_ref[...] = jnp.zeros_like(acc_ref)

        acc_ref[...] += lax.dot_general(a_ref[...].astype(BF16), b_ref[...].astype(BF16), dims,
                                        preferred_element_type=F32)

        @pl.when(k == nk - 1)
        def _():
            o_ref[...] = acc_ref[...].astype(o_ref.dtype)

    if a_blk:
        a_spec = (pl.BlockSpec((None, tk, tm), lambda i, j, k: (i, k, 0)) if ta
                  else pl.BlockSpec((None, tm, tk), lambda i, j, k: (k, i, 0)))
    else:
        a_spec = (pl.BlockSpec((tk, tm), lambda i, j, k: (k, i)) if ta
                  else pl.BlockSpec((tm, tk), lambda i, j, k: (i, k)))
    if b_blk:
        b_spec = (pl.BlockSpec((None, tn, tk), lambda i, j, k: (k, j, 0)) if tb
                  else pl.BlockSpec((None, tk, tn), lambda i, j, k: (j, k, 0)))
    else:
        b_spec = (pl.BlockSpec((tn, tk), lambda i, j, k: (j, k)) if tb
                  else pl.BlockSpec((tk, tn), lambda i, j, k: (k, j)))
    if o_blk:
        o_spec = pl.BlockSpec((None, tm, tn), lambda i, j, k: (j, i, 0))
        o_shape = SDS((n_dim // tn, m_dim, tn), out_dtype)
    else:
        o_spec = pl.BlockSpec((tm, tn), lambda i, j, k: (i, j))
        o_shape = SDS((m_dim, n_dim), out_dtype)
    return pl.pallas_call(
        body, grid=(m_dim // tm, n_dim // tn, nk),
        in_specs=[a_spec, b_spec] + ([] if after is None else [pl.BlockSpec(memory_space=pl.ANY)]),
        out_specs=o_spec,
        out_shape=o_shape,
        scratch_shapes=[pltpu.VMEM((tm, tn), F32)],
        compiler_params=_params(("parallel", "parallel", "arbitrary")),
        name=name,
    )(a, b, *([] if after is None else [after]))


def _matmul_residual(a, b, x, gate, coef, *, a_blk=False, norm=None, name, after=None):
    if a_blk:
        nb, m_dim, tk = a.shape
        nk = nb
        a_spec = pl.BlockSpec((None, 512, tk), lambda i, k: (k, i, 0))
    else:
        m_dim, tk = a.shape
        nk = 1
        a_spec = pl.BlockSpec((512, tk), lambda i, k: (i, 0))
    tm = 512
    extra = [] if after is None else [after]
    vecs = [gate] + (list(norm) if norm else [])

    def body(a_ref, b_ref, x_ref, gate_ref, *rest):
        vec_refs = rest[:len(vecs) - 1]
        outs = rest[len(vecs) - 1 + len(extra):]
        acc_ref = outs[-1]
        k = pl.program_id(1)

        @pl.when(k == 0)
        def _():
            acc_ref[...] = jnp.zeros_like(acc_ref)

        acc_ref[...] += _dg(a_ref[...], b_ref[...], NN)

        @pl.when(k == nk - 1)
        def _():
            y = acc_ref[...]
            xn = x_ref[...] + (coef * gate_ref[...]) * y
            outs[0][...] = xn
            outs[1][...] = y.astype(outs[1].dtype)
            if norm:
                g_ref, sh_ref, sc_ref = vec_refs
                r = lax.rsqrt(jnp.mean(xn * xn, axis=-1, keepdims=True) + RMS_EPS)
                outs[2][...] = (((xn * r) * g_ref[...]) * (1.0 + sc_ref[...]) + sh_ref[...]).astype(outs[2].dtype)

    row = pl.BlockSpec((tm, D), lambda i, k: (i, 0))
    vec = pl.BlockSpec((1, D), lambda i, k: (0, 0))
    return pl.pallas_call(
        body, grid=(m_dim // tm, nk),
        in_specs=[a_spec, pl.BlockSpec((tk, D), lambda i, k: (k, 0)), row] + [vec] * len(vecs)
        + [pl.BlockSpec(memory_space=pl.ANY)] * len(extra),
        out_specs=[row] * (3 if norm else 2),
        out_shape=[SDS((m_dim, D), F32), SDS((m_dim, D), BF16)] + ([SDS((m_dim, D), BF16)] if norm else []),
        scratch_shapes=[pltpu.VMEM((tm, D), F32)],
        compiler_params=_params(("parallel", "arbitrary")), name=name,
    )(a, b, x, *vecs, *extra)


def _row(width, col=0):
    return pl.BlockSpec((TR, width), lambda i: (i, col))


def _vec(width):
    return pl.BlockSpec((1, width), lambda i: (0, 0))


def _norm_mod_fwd(x, g, shift, scale, name, after=None):
    t = x.shape[0]
    extra = [] if after is None else [after]

    def body(x_ref, g_ref, sh_ref, sc_ref, *rest):
        o_ref = rest[-1]
        xv = x_ref[...]
        r = lax.rsqrt(jnp.mean(xv * xv, axis=-1, keepdims=True) + RMS_EPS)
        o_ref[...] = (((xv * r) * g_ref[...]) * (1.0 + sc_ref[...]) + sh_ref[...]).astype(o_ref.dtype)

    return pl.pallas_call(
        body, grid=(t // TR,),
        in_specs=[_row(D), _vec(D), _vec(D), _vec(D)] + [pl.BlockSpec(memory_space=pl.ANY)] * len(extra),
        out_specs=_row(D),
        out_shape=SDS((t, D), BF16), compiler_params=_params(("parallel",)), name=name,
    )(x, g, shift, scale, *extra)


def _norm_mod_bwd(x, g, scale, dh, dx_in, name):
    t = x.shape[0]

    def body(x_ref, g_ref, sc_ref, dh_ref, dxi_ref, dx_ref, dsh_ref, dsc_ref, dg_ref):
        @pl.when(pl.program_id(0) == 0)
        def _():
            dsh_ref[...] = jnp.zeros_like(dsh_ref)
            dsc_ref[...] = jnp.zeros_like(dsc_ref)
            dg_ref[...] = jnp.zeros_like(dg_ref)

        xv = x_ref[...]
        gv = g_ref[...]
        dh = dh_ref[...]
        r = lax.rsqrt(jnp.mean(xv * xv, axis=-1, keepdims=True) + RMS_EPS)
        n = xv * r
        dsh_ref[...] += jnp.sum(dh, axis=0, keepdims=True)
        dsc_ref[...] += jnp.sum(dh * (n * gv), axis=0, keepdims=True)
        tt = dh * (1.0 + sc_ref[...])
        dg_ref[...] += jnp.sum(tt * n, axis=0, keepdims=True)
        dn = tt * gv
        dx_ref[...] = dxi_ref[...] + r * (dn - n * jnp.mean(dn * n, axis=-1, keepdims=True))

    return pl.pallas_call(
        body, grid=(t // TR,), in_specs=[_row(D), _vec(D), _vec(D), _row(D), _row(D)],
        out_specs=[_row(D), _vec(D), _vec(D), _vec(D)],
        out_shape=[SDS((t, D), F32), SDS((1, D), F32), SDS((1, D), F32), SDS((1, D), F32)],
        compiler_params=_params(("arbitrary",)), name=name,
    )(x, g, scale, dh, dx_in)


def _swiglu_up(h, w_in, name, after=None):
    t = h.shape[0]
    tm = _pick(t, (1024, 512, 256))
    half = NDEV // 2
    extra = [] if after is None else [after]

    def body(h_ref, wg_ref, wu_ref, *rest):
        u_ref, a_ref = rest[-2:]
        hv = h_ref[...]
        gate = _dg(hv, wg_ref[...], NT)
        up = _dg(hv, wu_ref[...], NT)
        u_ref[0] = gate.astype(u_ref.dtype)
        u_ref[1] = up.astype(u_ref.dtype)
        a_ref[...] = (_silu(gate) * up).astype(a_ref.dtype)

    return pl.pallas_call(
        body, grid=(t // tm, half),
        in_specs=[pl.BlockSpec((tm, D), lambda i, j: (i, 0)),
                  pl.BlockSpec((FB, D), lambda i, j: (j, 0)),
                  pl.BlockSpec((FB, D), lambda i, j: (j + half, 0))]
        + [pl.BlockSpec(memory_space=pl.ANY)] * len(extra),
        out_specs=[pl.BlockSpec((2, None, tm, FB), lambda i, j: (0, j, i, 0)),
                   pl.BlockSpec((None, tm, FB), lambda i, j: (j, i, 0))],
        out_shape=[SDS((2, half, t, FB), BF16), SDS((half, t, FB), BF16)],
        compiler_params=_params(("parallel", "parallel")), name=name,
    )(h, w_in, w_in, *extra)


def _swiglu_down_bwd(dy, w_out, u, name, after=None):
    t = dy.shape[0]
    tm = _pick(t, (1024, 512, 256))
    half = NDEV // 2
    extra = [] if after is None else [after]
    pair = pl.BlockSpec((2, None, tm, FB), lambda i, j: (0, j, i, 0))

    def body(dy_ref, w_ref, u_ref, *rest):
        o_ref = rest[-1]
        da = _dg(dy_ref[...], w_ref[...], NT)
        gate = u_ref[0].astype(F32)
        o_ref[0] = (da * u_ref[1].astype(F32) * _dsilu(gate)).astype(o_ref.dtype)
        o_ref[1] = (da * _silu(gate)).astype(o_ref.dtype)

    return pl.pallas_call(
        body, grid=(t // tm, half),
        in_specs=[pl.BlockSpec((tm, D), lambda i, j: (i, 0)), pl.BlockSpec((FB, D), lambda i, j: (j, 0)), pair]
        + [pl.BlockSpec(memory_space=pl.ANY)] * len(extra),
        out_specs=pair, out_shape=SDS((2, half, t, FB), BF16),
        compiler_params=_params(("parallel", "parallel")), name=name,
    )(dy, w_out, u, *extra)


def _resid_fwd(x, y, gate, coef, name):
    t = x.shape[0]

    def body(x_ref, y_ref, g_ref, o_ref):
        o_ref[...] = x_ref[...] + (coef * g_ref[...]) * y_ref[...]

    return pl.pallas_call(
        body, grid=(t // TR,), in_specs=[_row(D), _row(D), _vec(D)], out_specs=_row(D),
        out_shape=SDS((t, D), F32), compiler_params=_params(("parallel",)), name=name,
    )(x, y, gate)


def _resid_bwd(dx, y, gate, coef, name):
    t = dx.shape[0]

    def body(dx_ref, y_ref, g_ref, dy_ref, dg_ref):
        @pl.when(pl.program_id(0) == 0)
        def _():
            dg_ref[...] = jnp.zeros_like(dg_ref)

        dxv = dx_ref[...]
        dy_ref[...] = ((coef * g_ref[...]) * dxv).astype(dy_ref.dtype)
        dg_ref[...] += jnp.sum((coef * dxv) * y_ref[...], axis=0, keepdims=True)

    return pl.pallas_call(
        body, grid=(t // TR,), in_specs=[_row(D), _row(D), _vec(D)], out_specs=[_row(D), _vec(D)],
        out_shape=[SDS((t, D), BF16), SDS((1, D), F32)],
        compiler_params=_params(("arbitrary",)), name=name,
    )(dx, y, gate)


def _final_loss(x, fg, target, name):
    t = x.shape[0]
    nt = t // TR

    def body(x_ref, g_ref, t_ref, loss_ref, dx_ref, dg_ref, acc_ref):
        i = pl.program_id(0)

        @pl.when(i == 0)
        def _():
            acc_ref[...] = jnp.zeros_like(acc_ref)
            dg_ref[...] = jnp.zeros_like(dg_ref)

        xv = x_ref[...]
        gv = g_ref[...]
        r = lax.rsqrt(jnp.mean(xv * xv, axis=-1, keepdims=True) + RMS_EPS)
        n = xv * r
        err = n * gv - t_ref[...]
        acc_ref[...] += jnp.sum(err * err, axis=0, keepdims=True)
        dy = err * (1.0 / D)
        dg_ref[...] += jnp.sum(dy * n, axis=0, keepdims=True)
        dn = dy * gv
        dx_ref[...] = r * (dn - n * jnp.mean(dn * n, axis=-1, keepdims=True))

        @pl.when(i == nt - 1)
        def _():
            tot = jnp.sum(acc_ref[...], axis=1, keepdims=True) * (0.5 / D)
            loss_ref[...] = jnp.broadcast_to(tot, loss_ref.shape)

    return pl.pallas_call(
        body, grid=(nt,), in_specs=[_row(D), _vec(D), _row(D)],
        out_specs=[_vec(128), _row(D), _vec(D)],
        out_shape=[SDS((1, 128), F32), SDS((t, D), F32), SDS((1, D), F32)],
        scratch_shapes=[pltpu.VMEM((1, D), F32)],
        compiler_params=_params(("arbitrary",)), name=name,
    )(x, fg, target)


def _halo_prev(width, col):
    per = TR // HALO
    return pl.BlockSpec((HALO, width), lambda i: (jnp.maximum(i * per - 1, 0), col))


def _halo_next(width, col, nt):
    per = TR // HALO
    return pl.BlockSpec((HALO, width), lambda i: (jnp.minimum((i + 1) * per, nt * per - 1), col))


def _pool_windows(ext, tile_index):
    rows = _iota((TR, PG), 0) + tile_index * TR + 1
    pooled, counts = [], []
    for gi in range(4):
        w = 2 << gi
        e = ext[:, gi * PG:(gi + 1) * PG]
        s = e
        step = 1
        while step < w:
            s = s + pltpu.roll(s, step, 0)
            step *= 2
        cnt = jnp.minimum(rows, w).astype(F32)
        pooled.append(s[HALO:] / cnt - e[HALO:])
        counts.append(cnt)
    return pooled, counts


def _pool_fwd(proj, pool_w, pool_scale, pool_proj, name):
    t = proj.shape[0]
    xcol = OFF_XP // PW

    def body(x_ref, h_ref, pw_ref, ps_ref, pp_ref, o_ref):
        i = pl.program_id(0)
        halo = jnp.where(i > 0, h_ref[...], 0.0)
        ext = jnp.concatenate([halo, x_ref[...]], axis=0)
        pooled, _ = _pool_windows(ext, i)
        mixed = [_dg(pooled[g].astype(BF16), pw_ref[g].astype(BF16), NN) for g in range(4)]
        ypre = jnp.concatenate(mixed, axis=1) * ps_ref[...]
        o_ref[...] = _dg(ypre.astype(BF16), pp_ref[...], NN)

    return pl.pallas_call(
        body, grid=(t // TR,),
        in_specs=[_row(PW, xcol), _halo_prev(PW, xcol),
                  pl.BlockSpec((4, PG, PG), lambda i: (0, 0, 0)), _vec(PW),
                  pl.BlockSpec((PW, D), lambda i: (0, 0))],
        out_specs=_row(D), out_shape=SDS((t, D), F32),
        compiler_params=_params(("parallel",)), name=name,
    )(proj, proj, pool_w, pool_scale, pool_proj)


def _pool_bwd_local(proj, pool_w, pool_scale, pool_proj, dya, name):
    t = proj.shape[0]
    xcol = OFF_XP // PW

    def body(x_ref, h_ref, pw_ref, ps_ref, pp_ref, dya_ref, dwin_ref, dpl_ref, dpw_ref, dps_ref, dpp_ref):
        i = pl.program_id(0)

        @pl.when(i == 0)
        def _():
            dpw_ref[...] = jnp.zeros_like(dpw_ref)
            dps_ref[...] = jnp.zeros_like(dps_ref)
            dpp_ref[...] = jnp.zeros_like(dpp_ref)

        halo = jnp.where(i > 0, h_ref[...], 0.0)
        ext = jnp.concatenate([halo, x_ref[...]], axis=0)
        pooled, counts = _pool_windows(ext, i)
        mixed = jnp.concatenate(
            [_dg(pooled[g].astype(BF16), pw_ref[g].astype(BF16), NN) for g in range(4)], axis=1)
        ps = ps_ref[...]
        ypre = mixed * ps
        dyab = dya_ref[...].astype(BF16)
        dypre = _dg(dyab, pp_ref[...], NT)
        dpp_ref[...] += _dg(ypre.astype(BF16), dyab, TN)
        dps_ref[...] += jnp.sum(dypre * mixed, axis=0, keepdims=True)
        dmixed = dypre * ps
        for g in range(4):
            dm = dmixed[:, g * PG:(g + 1) * PG].astype(BF16)
            dpw_ref[g] += _dg(pooled[g].astype(BF16), dm, TN)
            dpooled = _dg(dm, pw_ref[g].astype(BF16), NT)
            dwin_ref[:, g * PG:(g + 1) * PG] = dpooled / counts[g]
            dpl_ref[:, g * PG:(g + 1) * PG] = dpooled

    return pl.pallas_call(
        body, grid=(t // TR,),
        in_specs=[_row(PW, xcol), _halo_prev(PW, xcol),
                  pl.BlockSpec((4, PG, PG), lambda i: (0, 0, 0)), _vec(PW),
                  pl.BlockSpec((PW, D), lambda i: (0, 0)), _row(D)],
        out_specs=[_row(PW), _row(PW), pl.BlockSpec((4, PG, PG), lambda i: (0, 0, 0)), _vec(PW),
                   pl.BlockSpec((PW, D), lambda i: (0, 0))],
        out_shape=[SDS((t, PW), F32), SDS((t, PW), F32), SDS((4, PG, PG), F32), SDS((1, PW), F32),
                   SDS((PW, D), F32)],
        compiler_params=_params(("arbitrary",)), name=name,
    )(proj, proj, pool_w, pool_scale, pool_proj, dya)


def _pool_bwd_window(dwin, dpl, dproj, name):
    t = dwin.shape[0]
    nt = t // TR
    ext_rows = TR + HALO

    def body(dw_ref, h_ref, dp_ref, _, o_ref):
        i = pl.program_id(0)
        halo = jnp.where(i < nt - 1, h_ref[...], 0.0)
        ext = jnp.concatenate([dw_ref[...], halo], axis=0)
        for gi in range(4):
            w = 2 << gi
            s = ext[:, gi * PG:(gi + 1) * PG]
            step = 1
            while step < w:
                s = s + pltpu.roll(s, ext_rows - step, 0)
                step *= 2
            o_ref[:, gi * PG:(gi + 1) * PG] = (s[:TR] - dp_ref[:, gi * PG:(gi + 1) * PG]).astype(o_ref.dtype)

    return pl.pallas_call(
        body, grid=(nt,),
        in_specs=[_row(PW), _halo_next(PW, 0, nt), _row(PW), pl.BlockSpec(memory_space=pl.ANY)],
        out_specs=_into(PW, OFF_XP), out_shape=SDS(dproj.shape, dproj.dtype), input_output_aliases={3: 0},
        compiler_params=_params(("parallel",)), name=name,
    )(dwin, dwin, dpl, dproj)


def _conv_group(ext, cw_ref, cols):
    acc = cw_ref[3:4, cols] * ext
    for j in range(3):
        acc = acc + cw_ref[j:j + 1, cols] * pltpu.roll(ext, 3 - j, 0)
    return acc[HALO:]


def _gate_terms(raw, al, dt):
    beta = _sigmoid(raw)
    xg = raw + dt
    sp = jnp.maximum(xg, 0.0) + jnp.log(1.0 + jnp.exp(-jnp.abs(xg)))
    g = -jnp.exp(al) * sp
    return beta, g, _sigmoid(xg)


def _dn_pre_fwd(proj, conv_w, al_row, dt_row, name):
    t = proj.shape[0]

    def body(x_ref, h_ref, cw_ref, ba_ref, al_ref, dt_ref, q_ref, k_ref, v_ref, bg_ref):
        i = pl.program_id(0)
        keep = i > 0
        for grp in range(24):
            cols = slice(grp * HD, (grp + 1) * HD)
            ext = jnp.concatenate([jnp.where(keep, h_ref[:, cols], 0.0), x_ref[:, cols]], axis=0)
            s = _silu(_conv_group(ext, cw_ref, cols))
            seg, head = divmod(grp, NH)
            hc = slice(head * HD, (head + 1) * HD)
            if seg == 0:
                q_ref[:, hc] = s * lax.rsqrt(jnp.sum(s * s, axis=-1, keepdims=True) + L2_EPS) * (HD ** -0.5)
            elif seg == 1:
                k_ref[:, hc] = s * lax.rsqrt(jnp.sum(s * s, axis=-1, keepdims=True) + L2_EPS)
            else:
                v_ref[:, hc] = s
        lane = _iota((TR, 128), 1)
        rowc = _iota((TR, 128), 0) % CH
        beta, g, _ = _gate_terms(ba_ref[...], al_ref[...], dt_ref[...])
        step = 1
        while step < CH:
            g = g + jnp.where(rowc >= step, pltpu.roll(g, step, 0), 0.0)
            step *= 2
        bg_ref[...] = jnp.where(lane < NH, beta, jnp.where(lane < 2 * NH, g, 0.0))

    return pl.pallas_call(
        body, grid=(t // TR,),
        in_specs=[_row(3 * D, 0), _halo_prev(3 * D, 0), pl.BlockSpec((4, 3 * D), lambda i: (0, 0)),
                  _row(128, OFF_BA // 128), _vec(128), _vec(128)],
        out_specs=[_row(D), _row(D), _row(D), _row(128)],
        out_shape=[SDS((t, D), F32), SDS((t, D), F32), SDS((t, D), F32), SDS((t, 128), F32)],
        compiler_params=_params(("parallel",)), name=name,
    )(proj, proj, conv_w, proj, al_row, dt_row)


def _dn_pre_bwd_act(proj, conv_w, al_row, dt_row, dq, dk, dv, dbg, dproj, name):
    t = proj.shape[0]

    def body(x_ref, h_ref, cw_ref, ba_ref, al_ref, dt_ref, dq_ref, dk_ref, dv_ref, dbg_ref, _,
             dc_ref, draw_ref, dal_ref, ddt_ref):
        i = pl.program_id(0)

        @pl.when(i == 0)
        def _():
            dal_ref[...] = jnp.zeros_like(dal_ref)
            ddt_ref[...] = jnp.zeros_like(ddt_ref)

        keep = i > 0
        for grp in range(24):
            cols = slice(grp * HD, (grp + 1) * HD)
            ext = jnp.concatenate([jnp.where(keep, h_ref[:, cols], 0.0), x_ref[:, cols]], axis=0)
            cv = _conv_group(ext, cw_ref, cols)
            seg, head = divmod(grp, NH)
            hc = slice(head * HD, (head + 1) * HD)
            if seg == 2:
                ds = dv_ref[:, hc]
            else:
                s = _silu(cv)
                r = lax.rsqrt(jnp.sum(s * s, axis=-1, keepdims=True) + L2_EPS)
                dy = dq_ref[:, hc] if seg == 0 else dk_ref[:, hc]
                c = (HD ** -0.5) if seg == 0 else 1.0
                ds = (c * r) * (dy - s * ((r * r) * jnp.sum(dy * s, axis=-1, keepdims=True)))
            dc_ref[:, cols] = ds * _dsilu(cv)
        lane = _iota((TR, 128), 1)
        rowc = _iota((TR, 128), 0) % CH
        isb = lane < NH
        isg = jnp.logical_and(lane >= NH, lane < 2 * NH)
        beta, g, sg = _gate_terms(ba_ref[...], al_ref[...], dt_ref[...])
        dbgv = dbg_ref[...]
        dg = dbgv
        step = 1
        while step < CH:
            dg = dg + jnp.where(rowc < CH - step, pltpu.roll(dg, TR - step, 0), 0.0)
            step *= 2
        da_raw = dg * (-jnp.exp(al_ref[...])) * sg
        draw = jnp.where(isb, dbgv * beta * (1.0 - beta), jnp.where(isg, da_raw, 0.0))
        draw_ref[:, :128] = draw.astype(draw_ref.dtype)
        draw_ref[:, 128:] = jnp.zeros((TR, MIXP - OFF_BA - 128), draw_ref.dtype)
        dal_ref[...] += jnp.sum(jnp.where(isg, dg * g, 0.0), axis=0, keepdims=True)
        ddt_ref[...] += jnp.sum(jnp.where(isg, da_raw, 0.0), axis=0, keepdims=True)

    return pl.pallas_call(
        body, grid=(t // TR,),
        in_specs=[_row(3 * D, 0), _halo_prev(3 * D, 0), pl.BlockSpec((4, 3 * D), lambda i: (0, 0)),
                  _row(128, OFF_BA // 128), _vec(128), _vec(128), _row(D), _row(D), _row(D), _row(128),
                  pl.BlockSpec(memory_space=pl.ANY)],
        out_specs=[_row(3 * D), _into(MIXP - OFF_BA, OFF_BA), _vec(128), _vec(128)],
        out_shape=[SDS((t, 3 * D), F32), SDS(dproj.shape, dproj.dtype), SDS((1, 128), F32), SDS((1, 128), F32)],
        input_output_aliases={10: 1},
        compiler_params=_params(("arbitrary",)), name=name,
    )(proj, proj, conv_w, proj, al_row, dt_row, dq, dk, dv, dbg, dproj)


def _dn_pre_bwd_conv(proj, conv_w, dconv, dproj, name):
    t = proj.shape[0]
    nt = t // TR
    ext_rows = TR + HALO

    def body(x_ref, h_ref, cw_ref, dc_ref, dn_ref, _, dx_ref, dcw_ref):
        i = pl.program_id(0)

        @pl.when(i == 0)
        def _():
            dcw_ref[...] = jnp.zeros_like(dcw_ref)

        keep_prev = i > 0
        keep_next = i < nt - 1
        for grp in range(24):
            cols = slice(grp * HD, (grp + 1) * HD)
            dct = dc_ref[:, cols]
            dext = jnp.concatenate([dct, jnp.where(keep_next, dn_ref[:, cols], 0.0)], axis=0)
            acc = cw_ref[3:4, cols] * dext
            for j in range(3):
                acc = acc + cw_ref[j:j + 1, cols] * pltpu.roll(dext, ext_rows - (3 - j), 0)
            dx_ref[:, cols] = acc[:TR].astype(dx_ref.dtype)
            xext = jnp.concatenate([jnp.where(keep_prev, h_ref[:, cols], 0.0), x_ref[:, cols]], axis=0)
            for j in range(4):
                xs = xext if j == 3 else pltpu.roll(xext, 3 - j, 0)
                dcw_ref[j:j + 1, cols] += jnp.sum(xs[HALO:] * dct, axis=0, keepdims=True)

    return pl.pallas_call(
        body, grid=(nt,),
        in_specs=[_row(3 * D, 0), _halo_prev(3 * D, 0), pl.BlockSpec((4, 3 * D), lambda i: (0, 0)),
                  _row(3 * D), _halo_next(3 * D, 0, nt), pl.BlockSpec(memory_space=pl.ANY)],
        out_specs=[_into(3 * D, OFF_Q), pl.BlockSpec((4, 3 * D), lambda i: (0, 0))],
        out_shape=[SDS(dproj.shape, dproj.dtype), SDS((4, 3 * D), F32)],
        input_output_aliases={5: 0},
        compiler_params=_params(("arbitrary",)), name=name,
    )(proj, proj, conv_w, dconv, dconv, dproj)


def _dn_post_fwd(o, proj, gn, name):
    t = o.shape[0]

    def body(o_ref, z_ref, g_ref, out_ref):
        gv = g_ref[...]
        for h in range(NH):
            hc = slice(h * HD, (h + 1) * HD)
            ov = o_ref[:, hc]
            r = lax.rsqrt(jnp.mean(ov * ov, axis=-1, keepdims=True) + RMS_EPS)
            out_ref[:, hc] = (((ov * r) * gv) * _silu(z_ref[:, hc])).astype(out_ref.dtype)

    return pl.pallas_call(
        body, grid=(t // TR,), in_specs=[_row(D), _row(D, OFF_Z // D), _vec(HD)], out_specs=_row(D),
        out_shape=SDS((t, D), BF16), compiler_params=_params(("parallel",)), name=name,
    )(o, proj, gn)


def _dn_post_bwd(o, proj, gn, dob, dproj, name):
    t = o.shape[0]

    def body(o_ref, z_ref, g_ref, d_ref, _, do_ref, dz_ref, dg_ref):
        @pl.when(pl.program_id(0) == 0)
        def _():
            dg_ref[...] = jnp.zeros_like(dg_ref)

        gv = g_ref[...]
        acc = jnp.zeros((1, HD), F32)
        for h in range(NH):
            hc = slice(h * HD, (h + 1) * HD)
            ov = o_ref[:, hc]
            zv = z_ref[:, hc]
            dv = d_ref[:, hc]
            r = lax.rsqrt(jnp.mean(ov * ov, axis=-1, keepdims=True) + RMS_EPS)
            n = ov * r
            dz_ref[:, hc] = (dv * (n * gv) * _dsilu(zv)).astype(dz_ref.dtype)
            dng = dv * _silu(zv)
            acc = acc + jnp.sum(dng * n, axis=0, keepdims=True)
            dn = dng * gv
            do_ref[:, hc] = r * (dn - n * jnp.mean(dn * n, axis=-1, keepdims=True))
        dg_ref[...] += acc

    return pl.pallas_call(
        body, grid=(t // TR,),
        in_specs=[_row(D), _row(D, OFF_Z // D), _vec(HD), _row(D), pl.BlockSpec(memory_space=pl.ANY)],
        out_specs=[_row(D), _into(D, OFF_Z), _vec(HD)],
        out_shape=[SDS((t, D), F32), SDS(dproj.shape, dproj.dtype), SDS((1, HD), F32)],
        input_output_aliases={4: 1},
        compiler_params=_params(("arbitrary",)), name=name,
    )(o, proj, gn, dob, dproj)


def _merge_fwd(ya, yb, proj, name):
    t = ya.shape[0]

    def body(a_ref, b_ref, gp_ref, gd_ref, o_ref):
        o_ref[...] = (_sigmoid(gp_ref[...]) * a_ref[...] + _sigmoid(gd_ref[...]) * b_ref[...]).astype(o_ref.dtype)

    return pl.pallas_call(
        body, grid=(t // TR,), in_specs=[_row(D), _row(D), _row(D, OFF_GP // D), _row(D, OFF_GD // D)],
        out_specs=_row(D), out_shape=SDS((t, D), BF16),
        compiler_params=_params(("parallel",)), name=name,
    )(ya, yb, proj, proj)


def _into(width, offset):
    assert offset % width == 0
    return pl.BlockSpec((TR, width), lambda i: (i, offset // width))


def _merge_bwd(dm, ya, yb, proj, dproj, name):
    t = ya.shape[0]

    def body(d_ref, a_ref, b_ref, gp_ref, gd_ref, _, da_ref, db_ref, dg_ref):
        dv = d_ref[...]
        sp = _sigmoid(gp_ref[...])
        sd = _sigmoid(gd_ref[...])
        da_ref[...] = dv * sp
        db_ref[...] = (dv * sd).astype(db_ref.dtype)
        dg_ref[:, :D] = (dv * a_ref[...] * sp * (1.0 - sp)).astype(dg_ref.dtype)
        dg_ref[:, D:] = (dv * b_ref[...] * sd * (1.0 - sd)).astype(dg_ref.dtype)

    return pl.pallas_call(
        body, grid=(t // TR,),
        in_specs=[_row(D), _row(D), _row(D), _row(D, OFF_GP // D), _row(D, OFF_GD // D),
                  pl.BlockSpec(memory_space=pl.ANY)],
        out_specs=[_row(D), _row(D), _into(2 * D, OFF_GP)],
        out_shape=[SDS((t, D), F32), SDS((t, D), BF16), SDS(dproj.shape, dproj.dtype)],
        input_output_aliases={5: 2},
        compiler_params=_params(("parallel",)), name=name,
    )(dm, ya, yb, proj, proj, dproj)


def _split2(x):
    hi = x.astype(BF16)
    return hi, (x - hi.astype(F32)).astype(BF16)


def _dot3(a, b, dims):
    ah, al = _split2(a)
    bh, bl = _split2(b)
    return _dg(ah, bh, dims) + (_dg(ah, bl, dims) + _dg(al, bh, dims))


def _neumann_inverses(mats):
    ri = _iota((CH, CH), 0)
    ci = _iota((CH, CH), 1)
    eye = jnp.where(ri == ci, 1.0, 0.0).astype(F32)
    xs = [-a for a in mats]
    ps = [eye + x for x in xs]
    for _ in range(5):
        xs = [_dot3(x, x, NN) for x in xs]
        ps = [p + _dot3(p, x, NN) for p, x in zip(ps, xs)]
    return ps


def _solve_with(inv):
    @jax.custom_vjp
    def solve(a, rhs):
        return _dot3(inv, rhs, NN)

    def fwd(a, rhs):
        sol = _dot3(inv, rhs, NN)
        return sol, sol

    def bwd(sol, d):
        drhs = _dot3(inv, d, TN)
        return -_dot3(drhs, sol, NT), drhs

    solve.defvjp(fwd, bwd)
    return solve


@jax.custom_vjp
def _rows_to_lanes(g64):
    ri = _iota((CH, CH), 0)
    ci = _iota((CH, CH), 1)
    diag = jnp.where(ri == ci, g64, 0.0)
    ones = jnp.ones((CH, CH), BF16)
    hi = diag.astype(BF16)
    rem = diag - hi.astype(F32)
    mid = rem.astype(BF16)
    lo = (rem - mid.astype(F32)).astype(BF16)
    return _dg(ones, hi, NN) + (_dg(ones, mid, NN) + _dg(ones, lo, NN))


def _rows_to_lanes_bwd(_, d):
    ri = _iota((CH, CH), 0)
    ci = _iota((CH, CH), 1)
    return (jnp.where(ri == ci, jnp.broadcast_to(jnp.sum(d, axis=0, keepdims=True), (CH, CH)), 0.0),)


_rows_to_lanes.defvjp(lambda g64: (_rows_to_lanes(g64), None), _rows_to_lanes_bwd)


def _chunk_local(solve_all, q, k, v, g128, g64, gl128, b128, b64):
    ri = _iota((CH, CH), 0)
    ci = _iota((CH, CH), 1)
    causal = ri >= ci
    strict = ri > ci
    gj = [_rows_to_lanes(g) for g in g64]
    decay = [jnp.where(causal, jnp.exp(jnp.where(causal, g - t, 0.0)), 0.0) for g, t in zip(g64, gj)]
    kk = [_nt(x, x) for x in k]
    a = [jnp.where(strict, b * m * dc, 0.0) for b, m, dc in zip(b64, kk, decay)]
    eg = [jnp.exp(g) for g in g128]
    rhs = [jnp.concatenate([b * x, (b * e) * y], axis=1) for b, x, e, y in zip(b128, v, eg, k)]
    sol = solve_all(a, rhs)
    qk = [jnp.where(causal, _nt(x, y) * dc, 0.0) for x, y, dc in zip(q, k, decay)]
    return ([s[:, :HD] for s in sol], [s[:, HD:] for s in sol], qk, [x * e for x, e in zip(q, eg)],
            [x * jnp.exp(gl - g) for x, gl, g in zip(k, gl128, g128)], [jnp.exp(gl) for gl in gl128])


def _all_head_gates(bgv):
    return tuple(list(z) for z in zip(*[_head_gates(bgv, h) for h in range(NH)]))


def _head_gates(bgv, h):
    lane = _iota((CH, 128), 1)
    row = _iota((CH, 128), 0)
    bcol = jnp.sum(jnp.where(lane == h, bgv, 0.0), axis=1, keepdims=True)
    gcol = jnp.sum(jnp.where(lane == NH + h, bgv, 0.0), axis=1, keepdims=True)
    g128 = jnp.broadcast_to(gcol, (CH, 128))
    gl128 = jnp.broadcast_to(jnp.sum(jnp.where(row == CH - 1, g128, 0.0), axis=0, keepdims=True), (CH, 128))
    return (g128, jnp.broadcast_to(gcol, (CH, CH)), gl128,
            jnp.broadcast_to(bcol, (CH, 128)), jnp.broadcast_to(bcol, (CH, CH)))


def _chunk_specs():
    row = pl.BlockSpec((CH, D), lambda i: (i, 0))
    small = pl.BlockSpec((CH, 128), lambda i: (i, 0))
    qk = pl.BlockSpec((NH, CH, CH), lambda i: (i, 0, 0))
    eg = pl.BlockSpec((1, NH, 128), lambda i: (i, 0, 0))
    return row, small, qk, eg


def _dn_local_fwd(q, k, v, bg, name):
    t = q.shape[0]
    n = t // CH

    def body(q_ref, k_ref, v_ref, bg_ref, u_ref, w_ref, qk_ref, qd_ref, kd_ref, eg_ref, inv_ref):
        cols = [slice(h * HD, (h + 1) * HD) for h in range(NH)]

        def solve_all(mats, rhs):
            invs = _neumann_inverses(mats)
            for h in range(NH):
                inv_ref[h] = invs[h]
            return [_dot3(m, r, NN) for m, r in zip(invs, rhs)]

        u, w, qk, qd, kd, egl = _chunk_local(
            solve_all, [q_ref[:, c] for c in cols], [k_ref[:, c] for c in cols], [v_ref[:, c] for c in cols],
            *_all_head_gates(bg_ref[...]))
        for h, hc in enumerate(cols):
            u_ref[:, hc] = u[h]
            w_ref[:, hc] = w[h].astype(w_ref.dtype)
            qd_ref[:, hc] = qd[h].astype(qd_ref.dtype)
            kd_ref[:, hc] = kd[h].astype(kd_ref.dtype)
            qk_ref[h] = qk[h].astype(qk_ref.dtype)
            eg_ref[0, h:h + 1, :] = egl[h][0:1, :]

    row, small, qkb, egb = _chunk_specs()
    return pl.pallas_call(
        body, grid=(n,), in_specs=[row, row, row, small], out_specs=[row, row, qkb, row, row, egb, qkb],
        out_shape=[SDS((t, D), F32), SDS((t, D), BF16), SDS((n * NH, CH, CH), BF16), SDS((t, D), BF16),
                   SDS((t, D), BF16), SDS((n, NH, 128), F32), SDS((n * NH, CH, CH), F32)],
        compiler_params=_params(("parallel",)), name=name,
    )(q, k, v, bg)


def _dn_local_bwd(q, k, v, bg, inv, du, dw, dqk, dqd, dkd, deg, name):
    t = q.shape[0]
    n = t // CH

    def body(q_ref, k_ref, v_ref, bg_ref, inv_ref, du_ref, dw_ref, dqk_ref, dqd_ref, dkd_ref, deg_ref,
             dq_ref, dk_ref, dv_ref, dbg_ref):
        bgv = bg_ref[...]
        lane = _iota((CH, 128), 1)
        row = _iota((CH, 128), 0)
        first = jnp.where(row == 0, 1.0, 0.0)
        acc = jnp.zeros((CH, 128), F32)
        cols = [slice(h * HD, (h + 1) * HD) for h in range(NH)]
        solves = [_solve_with(inv_ref[h]) for h in range(NH)]

        def solve_all(mats, rhs):
            return [f(m, r) for f, m, r in zip(solves, mats, rhs)]

        _, vjp = jax.vjp(functools.partial(_chunk_local, solve_all),
                         [q_ref[:, c] for c in cols], [k_ref[:, c] for c in cols], [v_ref[:, c] for c in cols],
                         *_all_head_gates(bgv))
        cts = ([du_ref[:, c].astype(F32) for c in cols], [dw_ref[:, c].astype(F32) for c in cols],
               [dqk_ref[h] for h in range(NH)],
               [dqd_ref[:, c].astype(F32) for c in cols], [dkd_ref[:, c].astype(F32) for c in cols],
               [jnp.broadcast_to(deg_ref[0, h:h + 1, :], (CH, 128)) * first for h in range(NH)])
        dq, dk, dv, dg128, dg64, dgl, db128, db64 = vjp(cts)
        for h, hc in enumerate(cols):
            dq_ref[:, hc] = dq[h]
            dk_ref[:, hc] = dk[h]
            dv_ref[:, hc] = dv[h]
            dg = jnp.sum(dg128[h], axis=1, keepdims=True) + jnp.sum(dg64[h], axis=1, keepdims=True)
            tot = jnp.sum(jnp.sum(dgl[h], axis=0, keepdims=True), axis=1, keepdims=True)
            dg = dg + jnp.where(row[:, 0:1] == CH - 1, tot, 0.0)
            db = jnp.sum(db128[h], axis=1, keepdims=True) + jnp.sum(db64[h], axis=1, keepdims=True)
            acc = acc + jnp.where(lane == h, db, 0.0) + jnp.where(lane == NH + h, dg, 0.0)
        dbg_ref[...] = acc

    row, small, qkb, egb = _chunk_specs()
    return pl.pallas_call(
        body, grid=(n,), in_specs=[row, row, row, small, qkb, row, row, qkb, row, row, egb],
        out_specs=[row, row, row, small],
        out_shape=[SDS((t, D), F32)] * 3 + [SDS((t, 128), F32)],
        compiler_params=_params(("parallel",)), name=name,
    )(q, k, v, bg, inv, du, dw, dqk, dqd, dkd, deg)


def _state_step(s, u, w, qk, qd, kd, egl):
    ws = [_nn(a, b) for a, b in zip(w, s)]
    v_new = [a - b for a, b in zip(u, ws)]
    qs = [_nn(a, b) for a, b in zip(qd, s)]
    intra = [_nn(a, b) for a, b in zip(qk, v_new)]
    upd = [_tn(a, b) for a, b in zip(kd, v_new)]
    return [a * e + b for a, e, b in zip(s, egl, upd)], [a + b for a, b in zip(qs, intra)]


def _dn_scan_fwd(u, w, qk, qd, kd, eg, name):
    t = u.shape[0]
    n = t // CH
    g = SCAN_CHUNKS

    def body(u_ref, w_ref, qk_ref, qd_ref, kd_ref, eg_ref, o_ref, save_ref, s_ref):
        @pl.when(pl.program_id(0) == 0)
        def _():
            s_ref[...] = jnp.zeros_like(s_ref)

        cols = [slice(h * HD, (h + 1) * HD) for h in range(NH)]
        s = [s_ref[h] for h in range(NH)]
        for c in range(g):
            rows = slice(c * CH, (c + 1) * CH)
            for h in range(NH):
                save_ref[c, h] = s[h].astype(save_ref.dtype)
            s, o = _state_step(
                s, [u_ref[rows, hc] for hc in cols], [w_ref[rows, hc].astype(F32) for hc in cols],
                [qk_ref[c * NH + h].astype(F32) for h in range(NH)], [qd_ref[rows, hc].astype(F32) for hc in cols],
                [kd_ref[rows, hc].astype(F32) for hc in cols], [eg_ref[c, h:h + 1, :] for h in range(NH)])
            for h, hc in enumerate(cols):
                o_ref[rows, hc] = o[h]
        for h in range(NH):
            s_ref[h] = s[h]

    row = pl.BlockSpec((g * CH, D), lambda i: (i, 0))
    qkb = pl.BlockSpec((g * NH, CH, CH), lambda i: (i, 0, 0))
    egb = pl.BlockSpec((g, NH, 128), lambda i: (i, 0, 0))
    return pl.pallas_call(
        body, grid=(n // g,), in_specs=[row, row, qkb, row, row, egb],
        out_specs=[row, pl.BlockSpec((g, NH, HD, HD), lambda i: (i, 0, 0, 0))],
        out_shape=[SDS((t, D), F32), SDS((n, NH, HD, HD), BF16)],
        scratch_shapes=[pltpu.VMEM((NH, HD, HD), F32)],
        compiler_params=_params(("arbitrary",)), name=name,
    )(u, w, qk, qd, kd, eg)


def _dn_scan_bwd(u, w, qk, qd, kd, eg, saved, do, name):
    t = u.shape[0]
    n = t // CH
    g = SCAN_CHUNKS
    last = n // g - 1

    def body(u_ref, w_ref, qk_ref, qd_ref, kd_ref, eg_ref, sv_ref, do_ref,
             du_ref, dw_ref, dqk_ref, dqd_ref, dkd_ref, deg_ref, ds_ref):
        @pl.when(pl.program_id(0) == 0)
        def _():
            ds_ref[...] = jnp.zeros_like(ds_ref)

        cols = [slice(h * HD, (h + 1) * HD) for h in range(NH)]
        ds = [ds_ref[h] for h in range(NH)]
        for c in reversed(range(g)):
            rows = slice(c * CH, (c + 1) * CH)
            _, vjp = jax.vjp(
                _state_step, [sv_ref[c, h].astype(F32) for h in range(NH)], [u_ref[rows, hc] for hc in cols],
                [w_ref[rows, hc].astype(F32) for hc in cols], [qk_ref[c * NH + h].astype(F32) for h in range(NH)],
                [qd_ref[rows, hc].astype(F32) for hc in cols], [kd_ref[rows, hc].astype(F32) for hc in cols],
                [eg_ref[c, h:h + 1, :] for h in range(NH)])
            ds, du, dw, dqk, dqd, dkd, deg = vjp((ds, [do_ref[rows, hc] for hc in cols]))
            for h, hc in enumerate(cols):
                du_ref[rows, hc] = du[h].astype(du_ref.dtype)
                dw_ref[rows, hc] = dw[h].astype(dw_ref.dtype)
                dqk_ref[c * NH + h] = dqk[h]
                dqd_ref[rows, hc] = dqd[h].astype(dqd_ref.dtype)
                dkd_ref[rows, hc] = dkd[h].astype(dkd_ref.dtype)
                deg_ref[c, h:h + 1, :] = deg[h]
        for h in range(NH):
            ds_ref[h] = ds[h]

    row = pl.BlockSpec((g * CH, D), lambda i: (last - i, 0))
    qkb = pl.BlockSpec((g * NH, CH, CH), lambda i: (last - i, 0, 0))
    egb = pl.BlockSpec((g, NH, 128), lambda i: (last - i, 0, 0))
    return pl.pallas_call(
        body, grid=(n // g,),
        in_specs=[row, row, qkb, row, row, egb,
                  pl.BlockSpec((g, NH, HD, HD), lambda i: (last - i, 0, 0, 0)), row],
        out_specs=[row, row, qkb, row, row, egb],
        out_shape=[SDS((t, D), BF16), SDS((t, D), BF16), SDS((n * NH, CH, CH), F32), SDS((t, D), BF16),
                   SDS((t, D), BF16), SDS((n, NH, 128), F32)],
        scratch_shapes=[pltpu.VMEM((NH, HD, HD), F32)],
        compiler_params=_params(("arbitrary",)), name=name,
    )(u, w, qk, qd, kd, eg, saved, do)


def _ada_fwd(c_all, ada_w, ada_b, name):
    ncol = ada_w.shape[1]

    def body(c_ref, w_ref, b_ref, o_ref):
        o_ref[...] = _dg(_silu(c_ref[...]), w_ref[...], NN, HI) + b_ref[...]

    return pl.pallas_call(body, out_shape=SDS((NDEV, ncol), F32),
                          compiler_params=pltpu.CompilerParams(vmem_limit_bytes=VMEM_LIMIT), name=name,
                          )(c_all, ada_w, ada_b)


def _ada_bwd(c_all_t, dmod, name):
    ncol = dmod.shape[1]

    def body(c_ref, d_ref, o_ref):
        sc = _silu(c_ref[...])
        acc = sc[:, 0:1] * d_ref[0:1, :]
        for b in range(1, NDEV):
            acc = acc + sc[:, b:b + 1] * d_ref[b:b + 1, :]
        o_ref[...] = acc

    return pl.pallas_call(body, out_shape=SDS((D, ncol), F32),
                          compiler_params=pltpu.CompilerParams(vmem_limit_bytes=VMEM_LIMIT), name=name,
                          )(c_all_t, dmod)


def _sum_devices(parts, out_dtype, name):
    _, r, c = parts.shape
    tr = TR if r % TR == 0 else r

    def body(p_ref, o_ref):
        acc = p_ref[0].astype(F32)
        for i in range(1, NDEV):
            acc = acc + p_ref[i].astype(F32)
        o_ref[...] = acc.astype(o_ref.dtype)

    return pl.pallas_call(
        body, grid=(r // tr,), in_specs=[pl.BlockSpec((NDEV, tr, c), lambda i: (0, i, 0))],
        out_specs=pl.BlockSpec((tr, c), lambda i: (i, 0)), out_shape=SDS((r, c), out_dtype),
        compiler_params=_params(("parallel",)), name=name,
    )(parts)


def _adam_tiles(r, c):
    if r % 8 == 0:
        return _pick(r, (256, 352, 128, 8)), c
    return r, (256 if c % 256 == 0 else c)


def _adam_math(w, gv, m, v):
    m_new = ADAM_B1 * m + (1.0 - ADAM_B1) * gv
    v_new = ADAM_B2 * v + (1.0 - ADAM_B2) * (gv * gv)
    bc1 = 1.0 - ADAM_B1 ** ADAM_STEP
    bc2 = 1.0 - ADAM_B2 ** ADAM_STEP
    return -ADAM_LR * ((m_new / bc1) / (jnp.sqrt(v_new / bc2) + ADAM_EPS) + ADAM_WD * w), m_new, v_new


def _adamw(w, g, m, v, name):
    r, c = w.shape
    tr, tc = _adam_tiles(r, c)

    def body(w_ref, g_ref, m_ref, v_ref, d_ref, nm_ref, nv_ref):
        d_ref[...], nm_ref[...], nv_ref[...] = _adam_math(w_ref[...], g_ref[...], m_ref[...], v_ref[...])

    spec = pl.BlockSpec((tr, tc), lambda i, j: (i, j))
    return pl.pallas_call(
        body, grid=(r // tr, c // tc), in_specs=[spec] * 4, out_specs=[spec] * 3,
        out_shape=[SDS((r, c), F32)] * 3, compiler_params=_params(("parallel", "parallel")), name=name,
    )(w, g, m, v)


def _reduce_adamw(parts, w, m, v, name):
    r, c = w.shape
    tr, tc = _adam_tiles(r, c)

    def body(p_ref, w_ref, m_ref, v_ref, g_ref, d_ref, nm_ref, nv_ref):
        gv = p_ref[0].astype(F32)
        for i in range(1, NDEV):
            gv = gv + p_ref[i].astype(F32)
        g_ref[...] = gv
        d_ref[...], nm_ref[...], nv_ref[...] = _adam_math(w_ref[...], gv, m_ref[...], v_ref[...])

    spec = pl.BlockSpec((tr, tc), lambda i, j: (i, j))
    return pl.pallas_call(
        body, grid=(r // tr, c // tc),
        in_specs=[pl.BlockSpec((NDEV, tr, tc), lambda i, j: (0, i, j))] + [spec] * 3, out_specs=[spec] * 4,
        out_shape=[SDS((r, c), F32)] * 4, compiler_params=_params(("parallel", "parallel")), name=name,
    )(parts, w, m, v)


ANY = pl.BlockSpec(memory_space=pl.ANY)
MESH = pl.DeviceIdType.MESH


def _all_gather(xs, name, after=None):
    n = len(xs)
    extra = [] if after is None else [after]

    def body(*refs):
        x_refs, out_refs = refs[:n], refs[n + len(extra):2 * n + len(extra)]
        send_sems, recv_sems, local_sems = refs[-3:]
        mx, my, mc = lax.axis_index("x"), lax.axis_index("y"), lax.axis_index("c")
        me, sibling = (mx, my, mc), (mx, my, 1 - mc)
        chips = [(1 - mx, my), (mx, 1 - my), (1 - mx, 1 - my)]

        def rows(a, px, py, pc):
            return out_refs[a].at[4 * px + 2 * py + pc]

        def copy(a, k, block, to, src=None):
            return pltpu.make_async_remote_copy(
                src_ref=rows(a, *block) if src is None else src, dst_ref=rows(a, *block),
                send_sem=send_sems.at[a, k], recv_sem=recv_sems.at[a, k], device_id=to, device_id_type=MESH)

        mine = [pltpu.make_async_copy(x_refs[a], rows(a, *me), local_sems.at[a]) for a in range(n)]
        for cp in mine:
            cp.start()
        first = []
        for a in range(n):
            first.append(copy(a, 0, me, sibling, src=x_refs[a]))
            first += [copy(a, 1 + j, me, (*chip, mc), src=x_refs[a]) for j, chip in enumerate(chips)]
        for cp in first:
            cp.start()
        passed = []
        for a in range(n):
            for j, chip in enumerate(chips):
                copy(a, 1 + j, (*chip, mc), me).wait_recv()
                passed.append(copy(a, 4 + j, (*chip, mc), sibling))
                passed[-1].start()
        for a in range(n):
            copy(a, 0, sibling, me).wait_recv()
            for j, chip in enumerate(chips):
                copy(a, 4 + j, (*chip, 1 - mc), me).wait_recv()
        for cp in first + passed:
            cp.wait_send()
        for cp in mine:
            cp.wait()

    return pl.pallas_call(
        body, out_shape=[SDS((NDEV,) + x.shape, x.dtype) for x in xs], in_specs=[ANY] * (n + len(extra)),
        out_specs=[ANY] * n,
        scratch_shapes=[pltpu.SemaphoreType.DMA((n, 7)), pltpu.SemaphoreType.DMA((n, 7)),
                        pltpu.SemaphoreType.DMA((n,))],
        name=name,
    )(*xs, *extra)


HBM = pl.BlockSpec(memory_space=pltpu.HBM)
SEM = pl.BlockSpec(memory_space=pltpu.SEMAPHORE)
EFFECT = pltpu.SideEffectType.DATAFLOW_SIDE_EFFECTING


def _peers():
    mx, my, mc = lax.axis_index("x"), lax.axis_index("y"), lax.axis_index("c")
    out = []
    for k in range(1, NDEV):
        out.append((1 - mx if k & 4 else mx, 1 - my if k & 2 else my, 1 - mc if k & 1 else mc))
    return 4 * mx + 2 * my + mc, out


def _push_start(srcs, sliced, name, after=None):
    n = len(srcs)
    extra = [] if after is None else [after]
    lands = [lax.empty(s.shape if sliced else (NDEV,) + s.shape, s.dtype) for s in srcs]

    def body(*refs):
        src_refs, land_refs = refs[:n], refs[n:2 * n]
        outs = refs[2 * n + len(extra):]
        send_sems, recv_sems = outs[:n], outs[n:2 * n]
        token = refs[-1]
        me, peers = _peers()
        for a in range(n):
            for k, (px, py, pc) in enumerate(peers):
                src = src_refs[a].at[4 * px + 2 * py + pc] if sliced else src_refs[a]
                pltpu.make_async_remote_copy(
                    src_ref=src, dst_ref=land_refs[a].at[me], send_sem=send_sems[a].at[k],
                    recv_sem=recv_sems[a].at[k], device_id=(px, py, pc), device_id_type=MESH).start()
            pltpu.make_async_copy(src_refs[a].at[me] if sliced else src_refs[a], land_refs[a].at[me],
                                  send_sems[a].at[NDEV - 1]).start()
        token[...] = jnp.zeros_like(token)

    outs = pl.pallas_call(
        body, name=name,
        out_shape=([pltpu.SemaphoreType.DMA((NDEV,))] * n + [pltpu.SemaphoreType.DMA((NDEV - 1,))] * n
                   + [pltpu.HBM(s.shape, s.dtype) for s in srcs] + [pltpu.HBM(l.shape, l.dtype) for l in lands]
                   + [SDS((8, 128), F32)]),
        in_specs=[HBM] * (2 * n) + [pl.BlockSpec(memory_space=pl.ANY)] * len(extra),
        out_specs=[SEM] * (2 * n) + [HBM] * (2 * n) + [pl.BlockSpec(memory_space=pltpu.VMEM)],
        input_output_aliases={i: 2 * n + i for i in range(2 * n)},
        compiler_params=pltpu.CompilerParams(has_side_effects=EFFECT),
    )(*[pltpu.with_memory_space_constraint(s, pltpu.HBM) for s in srcs],
      *[pltpu.with_memory_space_constraint(l, pltpu.HBM) for l in lands], *extra)
    sends, recvs = outs[:n], outs[n:2 * n]
    src_thru, land_thru = outs[2 * n:3 * n], outs[3 * n:4 * n]
    return [(sends[a], recvs[a], src_thru[a], land_thru[a]) for a in range(n)], outs[-1]


def _push_wait(started, sliced, after, name):
    n = len(started)
    afters = list(after) if isinstance(after, (list, tuple)) else [after]

    def body(*refs):
        src_refs, land_refs = refs[:n], refs[n:2 * n]
        send_sems, recv_sems = refs[2 * n:3 * n], refs[3 * n:4 * n]
        me, peers = _peers()
        for a in range(n):
            for k, (px, py, pc) in enumerate(peers):
                src = src_refs[a].at[4 * px + 2 * py + pc] if sliced else src_refs[a]
                cp = pltpu.make_async_remote_copy(
                    src_ref=src, dst_ref=land_refs[a].at[me], send_sem=send_sems[a].at[k],
                    recv_sem=recv_sems[a].at[k], device_id=(px, py, pc), device_id_type=MESH)
                cp.wait_send()
                cp.wait_recv()
            pltpu.make_async_copy(src_refs[a].at[me] if sliced else src_refs[a], land_refs[a].at[me],
                                  send_sems[a].at[NDEV - 1]).wait()

    srcs = [s[2] for s in started]
    lands = [s[3] for s in started]
    outs = pl.pallas_call(
        body, name=name,
        out_shape=[pltpu.HBM(s.shape, s.dtype) for s in srcs] + [pltpu.HBM(l.shape, l.dtype) for l in lands],
        in_specs=[HBM] * (2 * n) + [SEM] * (2 * n) + [pl.BlockSpec(memory_space=pl.ANY)] * len(afters),
        out_specs=[HBM] * (2 * n),
        input_output_aliases={i: i for i in range(2 * n)},
        compiler_params=pltpu.CompilerParams(has_side_effects=EFFECT),
    )(*srcs, *lands, *[s[0] for s in started], *[s[1] for s in started], *afters)
    return outs[n:]


def _cols_from_blocks(blocks):
    _, rows, w = blocks.shape
    return blocks.transpose(1, 0, 2).reshape(rows, NDEV * w)


def _cols_to_blocks(full):
    rows, total = full.shape
    return full.reshape(rows, NDEV, total // NDEV).transpose(1, 0, 2)


def _mix_pad(wt):
    xp, q, k, v, z, ba, gp, gd = jnp.split(wt, (512, 1536, 2560, 3584, 4608, 4624, 5648), axis=0)
    pad = jnp.zeros((MIXP - OFF_BA - 16, wt.shape[1]), wt.dtype)
    return jnp.concatenate([q, k, v, z, gp, gd, xp, ba, pad], axis=0)


def _mix_unpad(wt):
    q, k, v, z, gp, gd, xp, ba = (wt[OFF_Q:OFF_K], wt[OFF_K:OFF_V], wt[OFF_V:OFF_Z], wt[OFF_Z:OFF_GP],
                                  wt[OFF_GP:OFF_GD], wt[OFF_GD:OFF_XP], wt[OFF_XP:OFF_BA], wt[OFF_BA:OFF_BA + 16])
    return jnp.concatenate([xp, q, k, v, z, ba, gp, gd], axis=0)


def _lane_row(vec8):
    return jnp.zeros((1, 128), F32).at[0, NH:2 * NH].set(vec8)


def _ffn_fwd(x, h, gate, w_in, w_out, tag, next_norm=None, token=None, start_more=None):
    if isinstance(w_in, tuple):
        w_in, = _push_wait([w_in], False, h, f"{tag}_gather_wait_in")
    w_in = w_in.reshape(2 * FH, D)
    u, a = _swiglu_up(h, w_in, f"{tag}_up", after=token)
    w_out, = _push_wait([w_out], False, a, f"{tag}_gather_wait_out")
    w_out = w_out.reshape(FH, D)
    outs = _matmul_residual(a, w_out, x, gate, 0.5, a_blk=True, norm=next_norm, name=f"{tag}_down",
                            after=None if start_more is None else start_more(h))
    return outs[0], (h, u, a, outs[1]), w_in, w_out, (outs[2] if next_norm else None)


def _ffn_bwd(dx_out, x, g, scale, gate, w_in, w_out, saved, tag):
    h, u, a, y = saved
    t = x.shape[0]
    dy, dgate = _resid_bwd(dx_out, y, gate, 0.5, f"{tag}_res_bwd")
    dw_out = _matmul(a, dy, ta=True, a_blk=True, out_dtype=BF16, name=f"{tag}_down_dw")
    sent_out, token = _push_start([dw_out.reshape(NDEV, FH // NDEV, D)], True, f"{tag}_grad_start_out")
    du = _swiglu_down_bwd(dy, w_out, u, f"{tag}_down_dx", after=token).reshape(NDEV, t, FB)
    dw_in = _matmul(du, h, ta=True, a_blk=True, out_dtype=BF16, name=f"{tag}_up_dw")
    sent_in, token = _push_start([dw_in.reshape(NDEV, FB, D)], True, f"{tag}_grad_start_in")
    dh = _matmul(du, w_in, a_blk=True, out_dtype=F32, name=f"{tag}_up_dx", after=token)
    dx, dshift, dscale, dg = _norm_mod_bwd(x, g, scale, dh, dx_out, f"{tag}_norm_bwd")
    return dx, (dshift, dscale, dgate), dg, sent_in + sent_out


def kernel(x, c, ada_w, ada_b, norm_g, ffn1_w_in, ffn1_w_out, ffn2_w_in, ffn2_w_out, mix_w_in, conv_w, a_log, dt_bias, dn_norm_g, pool_w, pool_scale, pool_proj, dn_proj, mix_w_out, final_g, loss_target, m_ada_w, m_ada_b, m_norm_g, m_ffn1_w_in, m_ffn1_w_out, m_ffn2_w_in, m_ffn2_w_out, m_mix_w_in, m_conv_w, m_a_log, m_dt_bias, m_dn_norm_g, m_pool_w, m_pool_scale, m_pool_proj, m_dn_proj, m_mix_w_out, m_final_g, v_ada_w, v_ada_b, v_norm_g, v_ffn1_w_in, v_ffn1_w_out, v_ffn2_w_in, v_ffn2_w_out, v_mix_w_in, v_conv_w, v_a_log, v_dt_bias, v_dn_norm_g, v_pool_w, v_pool_scale, v_pool_proj, v_dn_proj, v_mix_w_out, v_final_g):
    me = 4 * lax.axis_index("x") + 2 * lax.axis_index("y") + lax.axis_index("c")
    x0 = x[0]
    target = loss_target[0]
    t = x0.shape[0]

    big = [ffn1_w_in[0], ffn1_w_out[0], ffn2_w_in[0], ffn2_w_out[0], mix_w_in[0], pool_proj[0], dn_proj[0],
           mix_w_out[0]]
    small = jnp.concatenate([c.reshape(8, 128), conv_w[0].reshape(12, 128), norm_g[0].reshape(3, 128),
                             jnp.zeros((1, 128), F32)], axis=0)
    small_all, = _all_gather([small], "gather_small")
    c_all = small_all[:, 0:8, :].reshape(NDEV, D)
    conv_full = small_all[:, 8:20, :].reshape(NDEV, 4, 384).transpose(1, 0, 2).reshape(4, 3 * D)
    norm_full = small_all[:, 20:23, :].reshape(NDEV, 3, 128).transpose(1, 0, 2).reshape(3, D)

    ncol = ada_w.shape[2]
    ada_b_mine = lax.dynamic_slice(ada_b, (0, me * ncol), (1, ncol))
    mod_cols = _ada_fwd(c_all, ada_w[0], ada_b_mine, "ada_fwd")
    transposed = (0, 2, 4)
    payload = [(w.T if i in transposed else w).astype(BF16) for i, w in enumerate(big)]
    mod_all, w_in1 = _all_gather([mod_cols, payload[0]], "gather_mod_first_weight")
    started, token = _push_start([payload[1], payload[4]], False, "gather_start", after=mod_all)
    started = {1: started[0], 4: started[1]}

    def start_rest(h):
        more, token = _push_start([payload[i] for i in (5, 6, 7, 2, 3)], False, "gather_start_rest", after=h)
        started.update(zip((5, 6, 7, 2, 3), more))
        return token

    mod = lax.dynamic_index_in_dim(mod_all, me, axis=1, keepdims=False).reshape(9, D)
    shift = [mod[3 * s:3 * s + 1] for s in range(3)]
    scale = [mod[3 * s + 1:3 * s + 2] for s in range(3)]
    gate = [mod[3 * s + 2:3 * s + 3] for s in range(3)]
    ng = [norm_full[s:s + 1] for s in range(3)]
    fg = final_g.reshape(1, D)
    al_row = _lane_row(a_log[0])
    dt_row = _lane_row(dt_bias[0])
    gn = dn_norm_g
    pw = pool_w[0]
    ps = pool_scale

    h0 = _norm_mod_fwd(x0, ng[0], shift[0], scale[0], "ffn1_norm", after=token)
    x1, saved1, w_in1, w_out1, h1 = _ffn_fwd(x0, h0, gate[0], w_in1, started[1], "ffn1",
                                             (ng[1], shift[1], scale[1]), token, start_rest)

    seg, = _push_wait([started[4]], False, h1, "mix_gather_wait")
    w_mix = _mix_pad(seg.reshape(MIX_RAW, D))
    proj = _matmul(h1, w_mix, tb=True, out_dtype=F32, name="mix_in")
    qh, kh, vh, bg = _dn_pre_fwd(proj, conv_full, al_row, dt_row, "dn_pre")
    seg = _push_wait([started[i] for i in (5, 6, 7)], False, qh, "mix_gather_wait_rest")
    w_pp = _cols_from_blocks(seg[0])
    w_dn = seg[1].reshape(D, D)
    w_mo = seg[2].reshape(D, D)
    ya = _pool_fwd(proj, pw, ps, w_pp, "pool_fwd")
    u, w, qk, qd, kd, eg, inv = _dn_local_fwd(qh, kh, vh, bg, "dn_local")
    o, s_saved = _dn_scan_fwd(u, w, qk, qd, kd, eg, "dn_scan")
    ob = _dn_post_fwd(o, proj, gn, "dn_post")
    yb = _matmul(ob, w_dn, out_dtype=F32, name="dn_out")
    merged = _merge_fwd(ya, yb, proj, "merge")
    x2, mix_y, h2 = _matmul_residual(merged, w_mo, x1, gate[1], 1.0, norm=(ng[2], shift[2], scale[2]),
                                     name="mix_out")

    x3, saved2, w_in2, w_out2, _ = _ffn_fwd(x2, h2, gate[2], started[2], started[3], "ffn2")
    loss_row, dx3, dfg = _final_loss(x3, fg, target, "loss")

    dx2, dmod2, dng2, sent2 = _ffn_bwd(dx3, x2, ng[2], scale[2], gate[2], w_in2, w_out2, saved2, "ffn2")

    dmy, dgate1 = _resid_bwd(dx2, mix_y, gate[1], 1.0, "mix_res_bwd")
    dmerged = _matmul(dmy, w_mo, tb=True, out_dtype=F32, name="mix_out_dx")
    dw_mo = _matmul(merged, dmy, ta=True, out_dtype=BF16, name="mix_out_dw")
    dproj = lax.empty((t, MIXP), BF16)
    dya, dyb, dproj = _merge_bwd(dmerged, ya, yb, proj, dproj, "merge_bwd")
    dob = _matmul(dyb, w_dn, tb=True, out_dtype=F32, name="dn_out_dx")
    dw_dn = _matmul(ob, dyb, ta=True, out_dtype=BF16, name="dn_out_dw")
    do, dproj, dgn = _dn_post_bwd(o, proj, gn, dob, dproj, "dn_post_bwd")
    du, dw, dqk, dqd, dkd, deg = _dn_scan_bwd(u, w, qk, qd, kd, eg, s_saved, do, "dn_scan_bwd")
    dqh, dkh, dvh, dbg = _dn_local_bwd(qh, kh, vh, bg, inv, du, dw, dqk, dqd, dkd, deg, "dn_local_bwd")
    dconv, dproj, dal, ddt = _dn_pre_bwd_act(proj, conv_full, al_row, dt_row, dqh, dkh, dvh, dbg, dproj,
                                             "dn_pre_bwd_act")
    dproj, dcw = _dn_pre_bwd_conv(proj, conv_full, dconv, dproj, "dn_pre_bwd_conv")
    dwin, dpl, dpw, dps, dpp = _pool_bwd_local(proj, pw, ps, w_pp, dya, "pool_bwd_local")
    dproj = _pool_bwd_window(dwin, dpl, dproj, "pool_bwd_window")
    dw_mix = _matmul(dproj, h1, ta=True, out_dtype=BF16, name="mix_in_dw")
    sent1, token = _push_start(
        [_mix_unpad(dw_mix).reshape(NDEV, MIX_RAW // NDEV, D), _cols_to_blocks(dpp.astype(BF16)),
         dw_dn.reshape(NDEV, -1, D), dw_mo.reshape(NDEV, -1, D)], True, "mix_grad_start")
    dh1 = _matmul(dproj, w_mix, out_dtype=F32, name="mix_in_dx", after=token)
    dx1, dsh1, dsc1, dng1 = _norm_mod_bwd(x1, ng[1], scale[1], dh1, dx2, "mix_norm_bwd")

    dx0, dmod0, dng0, sent0 = _ffn_bwd(dx1, x0, ng[0], scale[0], gate[0], w_in1, w_out1, saved1, "ffn1")

    dmod = jnp.concatenate([*dmod0, dsh1, dsc1, dgate1, *dmod2], axis=1).reshape(-1)
    flat = jnp.concatenate([
        dmod, dal[0, NH:2 * NH], ddt[0, NH:2 * NH], dgn.reshape(-1), dps.reshape(-1), dfg.reshape(-1),
        dpw.reshape(-1), jnp.concatenate([dng0, dng1, dng2], axis=0).reshape(-1), dcw.reshape(-1),
        loss_row[0, 0:1]])
    nflat = 90 * D
    flat = jnp.concatenate([flat, jnp.zeros((nflat - flat.shape[0],), F32)]).reshape(90, D)
    sent_small, small_token = _push_start([flat], False, "small_grad_start")

    def small_grads(flat_all):
        tot = _sum_devices(flat_all, F32, "sum_small_grads").reshape(-1)
        dmod_all = flat_all.reshape(NDEV, nflat)[:, :9 * D]
        dmod_cols = lax.dynamic_slice(dmod_all, (0, me * ncol), (NDEV, ncol))
        g_ada_w = _ada_bwd(c_all.T, dmod_cols, "ada_bwd")
        p = 0
        pieces = {}
        for nm, size in (("ada_b", 9 * D), ("a_log", NH), ("dt_bias", NH), ("dn_norm_g", HD), ("pool_scale", PW),
                         ("final_g", D), ("pool_w", 4 * PG * PG), ("norm_g", 3 * D), ("conv_w", 12 * D),
                         ("loss", 1)):
            pieces[nm] = tot[p:p + size]
            p += size
        g_norm = lax.dynamic_slice(pieces["norm_g"].reshape(3, D), (0, me * 128), (3, 128))
        g_conv = lax.dynamic_slice(pieces["conv_w"].reshape(4, 3 * D), (0, me * 384), (4, 384))
        return pieces["loss"][0], {
            "ada_w": g_ada_w.reshape(ada_w.shape), "ada_b": pieces["ada_b"].reshape(ada_b.shape),
            "norm_g": g_norm.reshape(norm_g.shape), "conv_w": g_conv.reshape(conv_w.shape),
            "a_log": pieces["a_log"].reshape(a_log.shape), "dt_bias": pieces["dt_bias"].reshape(dt_bias.shape),
            "dn_norm_g": pieces["dn_norm_g"].reshape(dn_norm_g.shape),
            "pool_w": pieces["pool_w"].reshape(pool_w.shape),
            "pool_scale": pieces["pool_scale"].reshape(pool_scale.shape),
            "final_g": pieces["final_g"].reshape(final_g.shape),
        }

    grads = {}
    weights = {"ada_w": ada_w, "ada_b": ada_b, "norm_g": norm_g, "ffn1_w_in": ffn1_w_in, "ffn1_w_out": ffn1_w_out,
               "ffn2_w_in": ffn2_w_in, "ffn2_w_out": ffn2_w_out, "mix_w_in": mix_w_in, "conv_w": conv_w,
               "a_log": a_log, "dt_bias": dt_bias, "dn_norm_g": dn_norm_g, "pool_w": pool_w,
               "pool_scale": pool_scale, "pool_proj": pool_proj, "dn_proj": dn_proj, "mix_w_out": mix_w_out,
               "final_g": final_g}
    m_in = {"ada_w": m_ada_w, "ada_b": m_ada_b, "norm_g": m_norm_g, "ffn1_w_in": m_ffn1_w_in,
            "ffn1_w_out": m_ffn1_w_out, "ffn2_w_in": m_ffn2_w_in, "ffn2_w_out": m_ffn2_w_out,
            "mix_w_in": m_mix_w_in, "conv_w": m_conv_w, "a_log": m_a_log, "dt_bias": m_dt_bias,
            "dn_norm_g": m_dn_norm_g, "pool_w": m_pool_w, "pool_scale": m_pool_scale, "pool_proj": m_pool_proj,
            "dn_proj": m_dn_proj, "mix_w_out": m_mix_w_out, "final_g": m_final_g}
    v_in = {"ada_w": v_ada_w, "ada_b": v_ada_b, "norm_g": v_norm_g, "ffn1_w_in": v_ffn1_w_in,
            "ffn1_w_out": v_ffn1_w_out, "ffn2_w_in": v_ffn2_w_in, "ffn2_w_out": v_ffn2_w_out,
            "mix_w_in": v_mix_w_in, "conv_w": v_conv_w, "a_log": v_a_log, "dt_bias": v_dt_bias,
            "dn_norm_g": v_dn_norm_g, "pool_w": v_pool_w, "pool_scale": v_pool_scale, "pool_proj": v_pool_proj,
            "dn_proj": v_dn_proj, "mix_w_out": v_mix_w_out, "final_g": v_final_g}

    names = list(weights)
    large = ("ada_w", "ffn1_w_in", "ffn1_w_out", "ffn2_w_in", "ffn2_w_out", "mix_w_in", "pool_proj", "dn_proj",
             "mix_w_out")
    delta, new_m, new_v = {}, {}, {}

    flipped = ("ffn1_w_in", "ffn2_w_in", "mix_w_in")

    def views(nm):
        shp = weights[nm].shape
        two_d = (shp[-2], shp[-1])
        if nm in flipped:
            return (lambda a: a.reshape(two_d).T), (lambda a: a.T.reshape(shp))
        return (lambda a: a.reshape(two_d)), (lambda a: a.reshape(shp))

    def reduce_update(sent, group, after, tag):
        done = []
        for nm, r in zip(group, _push_wait(sent, True, after, f"{tag}_grad_wait")):
            view, back = views(nm)
            g_, d_, m_, v_ = _reduce_adamw(r, view(weights[nm]), view(m_in[nm]), view(v_in[nm]), f"adamw_{nm}")
            grads[nm], delta[nm], new_m[nm], new_v[nm] = back(g_), back(d_), back(m_), back(v_)
            done.append(d_)
        return done

    done = reduce_update(sent2, ("ffn2_w_in", "ffn2_w_out"), small_token, "ffn2")
    done += reduce_update(sent1, ("mix_w_in", "pool_proj", "dn_proj", "mix_w_out"), done, "mix")
    flat_all, = _push_wait(sent_small, False, done, "small_grad_wait")
    loss, small = small_grads(flat_all)
    grads.update(small)
    view, back = views("ada_w")
    done, m_, v_ = _adamw(view(ada_w), view(grads["ada_w"]), view(m_ada_w), view(v_ada_w), "adamw_ada_w")
    delta["ada_w"], new_m["ada_w"], new_v["ada_w"] = back(done), back(m_), back(v_)
    reduce_update(sent0, ("ffn1_w_in", "ffn1_w_out"), done, "ffn1")
    rest = [nm for nm in names if nm not in large]
    total = sum(weights[nm].size for nm in rest)
    padded = -(-total // D) * D

    def pack(tree, fill):
        flat_ = jnp.concatenate([tree[nm].reshape(-1) for nm in rest])
        return jnp.concatenate([flat_, jnp.full((padded - total,), fill, F32)]).reshape(-1, D)

    d_, m_, v_ = _adamw(pack(weights, 0.0), pack(grads, 0.0), pack(m_in, 0.0), pack(v_in, 1.0), "adamw_small")
    p = 0
    for nm in rest:
        size = weights[nm].size
        shp = weights[nm].shape
        delta[nm] = d_.reshape(-1)[p:p + size].reshape(shp)
        new_m[nm] = m_.reshape(-1)[p:p + size].reshape(shp)
        new_v[nm] = v_.reshape(-1)[p:p + size].reshape(shp)
        p += size

    grad_x = dx0.reshape(x.shape)
    return (loss, grad_x, *[grads[nm] for nm in names], *[delta[nm] for nm in names],
            *[new_m[nm] for nm in names], *[new_v[nm] for nm in names])
```

```python
import functools

import jax
import jax.numpy as jnp
from jax import lax
from jax.experimental import pallas as pl
from jax.experimental.pallas import tpu as pltpu

F32 = jnp.float32
BF16 = jnp.bfloat16
SDS = jax.ShapeDtypeStruct
HI = lax.Precision.HIGHEST

D = 1024
FH = 2816
FB = 704
NH = 8
HD = 128
CH = 64
SCAN_CHUNKS = 2
NDEV = 8
PW = 512
PG = 128
RMS_EPS = 1e-6
L2_EPS = 1e-6
TR = 512
HALO = 16
VMEM_LIMIT = 56 * 1024 * 1024

MIXP = 6912
OFF_Q, OFF_K, OFF_V, OFF_Z, OFF_GP, OFF_GD, OFF_XP, OFF_BA = 0, 1024, 2048, 3072, 4096, 5120, 6144, 6656
MIX_RAW = 6672

ADAM_LR = 0.001
ADAM_B1 = 0.9
ADAM_B2 = 0.999
ADAM_EPS = 1e-08
ADAM_WD = 0.01
ADAM_STEP = 10

NN = (((1,), (0,)), ((), ()))
NT = (((1,), (1,)), ((), ()))
TN = (((0,), (0,)), ((), ()))


def _dg(a, b, dims, prec=None):
    return lax.dot_general(a, b, dims, precision=prec, preferred_element_type=F32)


def _make_dots(prec):
    @jax.custom_vjp
    def nn(a, b):
        return _dg(a, b, NN, prec)

    @jax.custom_vjp
    def nt(a, b):
        return _dg(a, b, NT, prec)

    @jax.custom_vjp
    def tn(a, b):
        return _dg(a, b, TN, prec)

    nn.defvjp(lambda a, b: (nn(a, b), (a, b)), lambda r, d: (nt(d, r[1]), tn(r[0], d)))
    nt.defvjp(lambda a, b: (nt(a, b), (a, b)), lambda r, d: (nn(d, r[1]), tn(d, r[0])))
    tn.defvjp(lambda a, b: (tn(a, b), (a, b)), lambda r, d: (nt(r[1], d), nn(r[0], d)))
    return nn, nt, tn


_nn, _nt, _tn = _make_dots(None)


def _params(sem):
    return pltpu.CompilerParams(dimension_semantics=sem, vmem_limit_bytes=VMEM_LIMIT)


def _sigmoid(x):
    return 1.0 / (1.0 + jnp.exp(-x))


def _silu(x):
    return x * _sigmoid(x)


def _dsilu(x):
    s = _sigmoid(x)
    return s * (1.0 + x * (1.0 - s))


def _pick(n, cands):
    for c in cands:
        if n % c == 0:
            return c
    raise ValueError(f"no tile for {n}")


def _iota(shape, dim):
    return lax.broadcasted_iota(jnp.int32, shape, dim)


def _matmul(a, b, *, ta=False, tb=False, a_blk=False, b_blk=False, o_blk=False, tm=None, tn=None, tk=None,
            out_dtype, name, after=None):
    if a_blk:
        nb, r, cb = a.shape
        if ta:
            k_dim, m_dim, tm = r, nb * cb, cb
        else:
            m_dim, k_dim, tk = r, nb * cb, cb
    else:
        k_dim, m_dim = a.shape if ta else a.shape[::-1]
    if b_blk:
        nb, r, cb = b.shape
        if tb:
            n_dim, tk = r, cb
            assert nb * cb == k_dim
        else:
            n_dim, tn = nb * cb, cb
            assert r == k_dim
    else:
        n_dim = b.shape[0] if tb else b.shape[1]
    tm = tm or _pick(m_dim, (1024, 768, 512, 256, 128))
    tn = tn or _pick(n_dim, (1024, 768, 512, 256, 128))
    tk = tk or (k_dim if (k_dim <= 2816 and not ta) else _pick(k_dim, (2816, 2304, 1024, 512, 256)))
    nk = k_dim // tk
    dims = ((((0,) if ta else (1,)), ((1,) if tb else (0,))), ((), ()))

    def body(a_ref, b_ref, *rest):
        o_ref, acc_ref = rest[-2:]
        k = pl.program_id(2)

        @pl.when(k == 0)
        def _():
            acc_ref[...] = jnp.zeros_like(acc_ref)

        acc_ref[...] += lax.dot_general(a_ref[...].astype(BF16), b_ref[...].astype(BF16), dims,
                                        preferred_element_type=F32)

        @pl.when(k == nk - 1)
        def _():
            o_ref[...] = acc_ref[...].astype(o_ref.dtype)

    if a_blk:
        a_spec = (pl.BlockSpec((None, tk, tm), lambda i, j, k: (i, k, 0)) if ta
                  else pl.BlockSpec((None, tm, tk), lambda i, j, k: (k, i, 0)))
    else:
        a_spec = (pl.BlockSpec((tk, tm), lambda i, j, k: (k, i)) if ta
                  else pl.BlockSpec((tm, tk), lambda i, j, k: (i, k)))
    if b_blk:
        b_spec = (pl.BlockSpec((None, tn, tk), lambda i, j, k: (k, j, 0)) if tb
                  else pl.BlockSpec((None, tk, tn), lambda i, j, k: (j, k, 0)))
    else:
        b_spec = (pl.BlockSpec((tn, tk), lambda i, j, k: (j, k)) if tb
                  else pl.BlockSpec((tk, tn), lambda i, j, k: (k, j)))
    if o_blk:
        o_spec = pl.BlockSpec((None, tm, tn), lambda i, j, k: (j, i, 0))
        o_shape = SDS((n_dim // tn, m_dim, tn), out_dtype)
    else:
        o_spec = pl.BlockSpec((tm, tn), lambda i, j, k: (i, j))
        o_shape = SDS((m_dim, n_dim), out_dtype)
    return pl.pallas_call(
        body, grid=(m_dim // tm, n_dim // tn, nk),
        in_specs=[a_spec, b_spec] + ([] if after is None else [pl.BlockSpec(memory_space=pl.ANY)]),
        out_specs=o_spec,
        out_shape=o_shape,
        scratch_shapes=[pltpu.VMEM((tm, tn), F32)],
        compiler_params=_params(("parallel", "parallel", "arbitrary")),
        name=name,
    )(a, b, *([] if after is None else [after]))


def _matmul_residual(a, b, x, gate, coef, *, a_blk=False, norm=None, name, after=None):
    if a_blk:
        nb, m_dim, tk = a.shape
        nk = nb
        a_spec = pl.BlockSpec((None, 512, tk), lambda i, k: (k, i, 0))
    else:
        m_dim, tk = a.shape
        nk = 1
        a_spec = pl.BlockSpec((512, tk), lambda i, k: (i, 0))
    tm = 512
    extra = [] if after is None else [after]
    vecs = [gate] + (list(norm) if norm else [])

    def body(a_ref, b_ref, x_ref, gate_ref, *rest):
        vec_refs = rest[:len(vecs) - 1]
        outs = rest[len(vecs) - 1 + len(extra):]
        acc_ref = outs[-1]
        k = pl.program_id(1)

        @pl.when(k == 0)
        def _():
            acc_ref[...] = jnp.zeros_like(acc_ref)

        acc_ref[...] += _dg(a_ref[...], b_ref[...], NN)

        @pl.when(k == nk - 1)
        def _():
            y = acc_ref[...]
            xn = x_ref[...] + (coef * gate_ref[...]) * y
            outs[0][...] = xn
            outs[1][...] = y.astype(outs[1].dtype)
            if norm:
                g_ref, sh_ref, sc_ref = vec_refs
                r = lax.rsqrt(jnp.mean(xn * xn, axis=-1, keepdims=True) + RMS_EPS)
                outs[2][...] = (((xn * r) * g_ref[...]) * (1.0 + sc_ref[...]) + sh_ref[...]).astype(outs[2].dtype)

    row = pl.BlockSpec((tm, D), lambda i, k: (i, 0))
    vec = pl.BlockSpec((1, D), lambda i, k: (0, 0))
    return pl.pallas_call(
        body, grid=(m_dim // tm, nk),
        in_specs=[a_spec, pl.BlockSpec((tk, D), lambda i, k: (k, 0)), row] + [vec] * len(vecs)
        + [pl.BlockSpec(memory_space=pl.ANY)] * len(extra),
        out_specs=[row] * (3 if norm else 2),
        out_shape=[SDS((m_dim, D), F32), SDS((m_dim, D), BF16)] + ([SDS((m_dim, D), BF16)] if norm else []),
        scratch_shapes=[pltpu.VMEM((tm, D), F32)],
        compiler_params=_params(("parallel", "arbitrary")), name=name,
    )(a, b, x, *vecs, *extra)


def _row(width, col=0):
    return pl.BlockSpec((TR, width), lambda i: (i, col))


def _vec(width):
    return pl.BlockSpec((1, width), lambda i: (0, 0))


def _norm_mod_fwd(x, g, shift, scale, name, after=None):
    t = x.shape[0]
    extra = [] if after is None else [after]

    def body(x_ref, g_ref, sh_ref, sc_ref, *rest):
        o_ref = rest[-1]
        xv = x_ref[...]
        r = lax.rsqrt(jnp.mean(xv * xv, axis=-1, keepdims=True) + RMS_EPS)
        o_ref[...] = (((xv * r) * g_ref[...]) * (1.0 + sc_ref[...]) + sh_ref[...]).astype(o_ref.dtype)

    return pl.pallas_call(
        body, grid=(t // TR,),
        in_specs=[_row(D), _vec(D), _vec(D), _vec(D)] + [pl.BlockSpec(memory_space=pl.ANY)] * len(extra),
        out_specs=_row(D),
        out_shape=SDS((t, D), BF16), compiler_params=_params(("parallel",)), name=name,
    )(x, g, shift, scale, *extra)


def _residual_branch_bwd(dxv, y_ref, gate_ref, coef, dy_ref, dgate_ref):
    dy_ref[...] = ((coef * gate_ref[...]) * dxv).astype(dy_ref.dtype)
    dgate_ref[...] += jnp.sum((coef * dxv) * y_ref[...], axis=0, keepdims=True)


def _norm_mod_bwd(x, g, scale, dh, dx_in, name, below=None):
    t = x.shape[0]
    lower = [] if below is None else list(below[:2])

    def body(x_ref, g_ref, sc_ref, dh_ref, dxi_ref, *rest):
        dx_ref, dsh_ref, dsc_ref, dg_ref = rest[len(lower):len(lower) + 4]

        @pl.when(pl.program_id(0) == 0)
        def _():
            for ref in rest[len(lower) + 1:]:
                if ref.shape[0] == 1:
                    ref[...] = jnp.zeros_like(ref)

        xv = x_ref[...]
        gv = g_ref[...]
        dh = dh_ref[...]
        r = lax.rsqrt(jnp.mean(xv * xv, axis=-1, keepdims=True) + RMS_EPS)
        n = xv * r
        dsh_ref[...] += jnp.sum(dh, axis=0, keepdims=True)
        dsc_ref[...] += jnp.sum(dh * (n * gv), axis=0, keepdims=True)
        tt = dh * (1.0 + sc_ref[...])
        dg_ref[...] += jnp.sum(tt * n, axis=0, keepdims=True)
        dn = tt * gv
        dxv = dxi_ref[...] + r * (dn - n * jnp.mean(dn * n, axis=-1, keepdims=True))
        dx_ref[...] = dxv
        if below is not None:
            _residual_branch_bwd(dxv, rest[0], rest[1], below[2], rest[-2], rest[-1])

    more_in = [] if below is None else [_row(D), _vec(D)]
    more_out = [] if below is None else [_row(D), _vec(D)]
    more_shape = [] if below is None else [SDS((t, D), BF16), SDS((1, D), F32)]
    return pl.pallas_call(
        body, grid=(t // TR,), in_specs=[_row(D), _vec(D), _vec(D), _row(D), _row(D)] + more_in,
        out_specs=[_row(D), _vec(D), _vec(D), _vec(D)] + more_out,
        out_shape=[SDS((t, D), F32), SDS((1, D), F32), SDS((1, D), F32), SDS((1, D), F32)] + more_shape,
        compiler_params=_params(("arbitrary",)), name=name,
    )(x, g, scale, dh, dx_in, *lower)


def _swiglu_up(h, w_in, name, after=None):
    t = h.shape[0]
    tm = _pick(t, (1024, 512, 256))
    half = NDEV // 2
    extra = [] if after is None else [after]

    def body(h_ref, wg_ref, wu_ref, *rest):
        u_ref, a_ref = rest[-2:]
        hv = h_ref[...]
        gate = _dg(hv, wg_ref[...], NT)
        up = _dg(hv, wu_ref[...], NT)
        u_ref[0] = gate.astype(u_ref.dtype)
        u_ref[1] = up.astype(u_ref.dtype)
        a_ref[...] = (_silu(gate) * up).astype(a_ref.dtype)

    return pl.pallas_call(
        body, grid=(t // tm, half),
        in_specs=[pl.BlockSpec((tm, D), lambda i, j: (i, 0)),
                  pl.BlockSpec((FB, D), lambda i, j: (j, 0)),
                  pl.BlockSpec((FB, D), lambda i, j: (j + half, 0))]
        + [pl.BlockSpec(memory_space=pl.ANY)] * len(extra),
        out_specs=[pl.BlockSpec((2, None, tm, FB), lambda i, j: (0, j, i, 0)),
                   pl.BlockSpec((None, tm, FB), lambda i, j: (j, i, 0))],
        out_shape=[SDS((2, half, t, FB), BF16), SDS((half, t, FB), BF16)],
        compiler_params=_params(("parallel", "parallel")), name=name,
    )(h, w_in, w_in, *extra)


def _swiglu_down_bwd(dy, w_out, u, name, after=None):
    t = dy.shape[0]
    tm = _pick(t, (1024, 512, 256))
    half = NDEV // 2
    extra = [] if after is None else [after]
    pair = pl.BlockSpec((2, None, tm, FB), lambda i, j: (0, j, i, 0))

    def body(dy_ref, w_ref, u_ref, *rest):
        o_ref = rest[-1]
        da = _dg(dy_ref[...], w_ref[...], NT)
        gate = u_ref[0].astype(F32)
        o_ref[0] = (da * u_ref[1].astype(F32) * _dsilu(gate)).astype(o_ref.dtype)
        o_ref[1] = (da * _silu(gate)).astype(o_ref.dtype)

    return pl.pallas_call(
        body, grid=(t // tm, half),
        in_specs=[pl.BlockSpec((tm, D), lambda i, j: (i, 0)), pl.BlockSpec((FB, D), lambda i, j: (j, 0)), pair]
        + [pl.BlockSpec(memory_space=pl.ANY)] * len(extra),
        out_specs=pair, out_shape=SDS((2, half, t, FB), BF16),
        compiler_params=_params(("parallel", "parallel")), name=name,
    )(dy, w_out, u, *extra)


def _final_loss(x, fg, target, below, name):
    t = x.shape[0]
    nt = t // TR

    def body(x_ref, g_ref, t_ref, y_ref, gate_ref, loss_ref, dx_ref, dg_ref, dy_ref, dgate_ref, acc_ref):
        i = pl.program_id(0)

        @pl.when(i == 0)
        def _():
            acc_ref[...] = jnp.zeros_like(acc_ref)
            dg_ref[...] = jnp.zeros_like(dg_ref)
            dgate_ref[...] = jnp.zeros_like(dgate_ref)

        xv = x_ref[...]
        gv = g_ref[...]
        r = lax.rsqrt(jnp.mean(xv * xv, axis=-1, keepdims=True) + RMS_EPS)
        n = xv * r
        err = n * gv - t_ref[...]
        acc_ref[...] += jnp.sum(err * err, axis=0, keepdims=True)
        dy = err * (1.0 / D)
        dg_ref[...] += jnp.sum(dy * n, axis=0, keepdims=True)
        dn = dy * gv
        dxv = r * (dn - n * jnp.mean(dn * n, axis=-1, keepdims=True))
        dx_ref[...] = dxv
        _residual_branch_bwd(dxv, y_ref, gate_ref, below[2], dy_ref, dgate_ref)

        @pl.when(i == nt - 1)
        def _():
            tot = jnp.sum(acc_ref[...], axis=1, keepdims=True) * (0.5 / D)
            loss_ref[...] = jnp.broadcast_to(tot, loss_ref.shape)

    return pl.pallas_call(
        body, grid=(nt,), in_specs=[_row(D), _vec(D), _row(D), _row(D), _vec(D)],
        out_specs=[_vec(128), _row(D), _vec(D), _row(D), _vec(D)],
        out_shape=[SDS((1, 128), F32), SDS((t, D), F32), SDS((1, D), F32), SDS((t, D), BF16), SDS((1, D), F32)],
        scratch_shapes=[pltpu.VMEM((1, D), F32)],
        compiler_params=_params(("arbitrary",)), name=name,
    )(x, fg, target, below[0], below[1])


def _halo_prev(width, col):
    per = TR // HALO
    return pl.BlockSpec((HALO, width), lambda i: (jnp.maximum(i * per - 1, 0), col))


def _halo_next(width, col, nt):
    per = TR // HALO
    return pl.BlockSpec((HALO, width), lambda i: (jnp.minimum((i + 1) * per, nt * per - 1), col))


def _pool_windows(ext, tile_index):
    rows = _iota((TR, PG), 0) + tile_index * TR + 1
    pooled, counts = [], []
    for gi in range(4):
        w = 2 << gi
        e = ext[:, gi * PG:(gi + 1) * PG]
        s = e
        step = 1
        while step < w:
            s = s + pltpu.roll(s, step, 0)
            step *= 2
        cnt = jnp.minimum(rows, w).astype(F32)
        pooled.append(s[HALO:] / cnt - e[HALO:])
        counts.append(cnt)
    return pooled, counts


def _pool_fwd(proj, pool_w, pool_scale, pool_proj, name):
    t = proj.shape[0]
    xcol = OFF_XP // PW

    def body(x_ref, h_ref, pw_ref, ps_ref, pp_ref, o_ref):
        i = pl.program_id(0)
        halo = jnp.where(i > 0, h_ref[...], 0.0)
        ext = jnp.concatenate([halo, x_ref[...]], axis=0)
        pooled, _ = _pool_windows(ext, i)
        mixed = [_dg(pooled[g].astype(BF16), pw_ref[g].astype(BF16), NN) for g in range(4)]
        ypre = jnp.concatenate(mixed, axis=1) * ps_ref[...]
        o_ref[...] = _dg(ypre.astype(BF16), pp_ref[...], NN)

    return pl.pallas_call(
        body, grid=(t // TR,),
        in_specs=[_row(PW, xcol), _halo_prev(PW, xcol),
                  pl.BlockSpec((4, PG, PG), lambda i: (0, 0, 0)), _vec(PW),
                  pl.BlockSpec((PW, D), lambda i: (0, 0))],
        out_specs=_row(D), out_shape=SDS((t, D), F32),
        compiler_params=_params(("parallel",)), name=name,
    )(proj, proj, pool_w, pool_scale, pool_proj)


def _pool_bwd_local(proj, pool_w, pool_scale, pool_proj, dya, name):
    t = proj.shape[0]
    xcol = OFF_XP // PW

    def body(x_ref, h_ref, pw_ref, ps_ref, pp_ref, dya_ref, dwin_ref, dpl_ref, dpw_ref, dps_ref, dpp_ref):
        i = pl.program_id(0)

        @pl.when(i == 0)
        def _():
            dpw_ref[...] = jnp.zeros_like(dpw_ref)
            dps_ref[...] = jnp.zeros_like(dps_ref)
            dpp_ref[...] = jnp.zeros_like(dpp_ref)

        halo = jnp.where(i > 0, h_ref[...], 0.0)
        ext = jnp.concatenate([halo, x_ref[...]], axis=0)
        pooled, counts = _pool_windows(ext, i)
        mixed = jnp.concatenate(
            [_dg(pooled[g].astype(BF16), pw_ref[g].astype(BF16), NN) for g in range(4)], axis=1)
        ps = ps_ref[...]
        ypre = mixed * ps
        dyab = dya_ref[...].astype(BF16)
        dypre = _dg(dyab, pp_ref[...], NT)
        dpp_ref[...] += _dg(ypre.astype(BF16), dyab, TN)
        dps_ref[...] += jnp.sum(dypre * mixed, axis=0, keepdims=True)
        dmixed = dypre * ps
        for g in range(4):
            dm = dmixed[:, g * PG:(g + 1) * PG].astype(BF16)
            dpw_ref[g] += _dg(pooled[g].astype(BF16), dm, TN)
            dpooled = _dg(dm, pw_ref[g].astype(BF16), NT)
            dwin_ref[:, g * PG:(g + 1) * PG] = dpooled / counts[g]
            dpl_ref[:, g * PG:(g + 1) * PG] = dpooled

    return pl.pallas_call(
        body, grid=(t // TR,),
        in_specs=[_row(PW, xcol), _halo_prev(PW, xcol),
                  pl.BlockSpec((4, PG, PG), lambda i: (0, 0, 0)), _vec(PW),
                  pl.BlockSpec((PW, D), lambda i: (0, 0)), _row(D)],
        out_specs=[_row(PW), _row(PW), pl.BlockSpec((4, PG, PG), lambda i: (0, 0, 0)), _vec(PW),
                   pl.BlockSpec((PW, D), lambda i: (0, 0))],
        out_shape=[SDS((t, PW), F32), SDS((t, PW), F32), SDS((4, PG, PG), F32), SDS((1, PW), F32),
                   SDS((PW, D), F32)],
        compiler_params=_params(("arbitrary",)), name=name,
    )(proj, proj, pool_w, pool_scale, pool_proj, dya)


def _pool_bwd_window(dwin, dpl, dproj, name):
    t = dwin.shape[0]
    nt = t // TR
    ext_rows = TR + HALO

    def body(dw_ref, h_ref, dp_ref, _, o_ref):
        i = pl.program_id(0)
        halo = jnp.where(i < nt - 1, h_ref[...], 0.0)
        ext = jnp.concatenate([dw_ref[...], halo], axis=0)
        for gi in range(4):
            w = 2 << gi
            s = ext[:, gi * PG:(gi + 1) * PG]
            step = 1
            while step < w:
                s = s + pltpu.roll(s, ext_rows - step, 0)
                step *= 2
            o_ref[:, gi * PG:(gi + 1) * PG] = (s[:TR] - dp_ref[:, gi * PG:(gi + 1) * PG]).astype(o_ref.dtype)

    return pl.pallas_call(
        body, grid=(nt,),
        in_specs=[_row(PW), _halo_next(PW, 0, nt), _row(PW), pl.BlockSpec(memory_space=pl.ANY)],
        out_specs=_into(PW, OFF_XP), out_shape=SDS(dproj.shape, dproj.dtype), input_output_aliases={3: 0},
        compiler_params=_params(("parallel",)), name=name,
    )(dwin, dwin, dpl, dproj)


def _conv_group(ext, cw_ref, cols):
    acc = cw_ref[3:4, cols] * ext
    for j in range(3):
        acc = acc + cw_ref[j:j + 1, cols] * pltpu.roll(ext, 3 - j, 0)
    return acc[HALO:]


def _gate_terms(raw, al, dt):
    beta = _sigmoid(raw)
    xg = raw + dt
    sp = jnp.maximum(xg, 0.0) + jnp.log(1.0 + jnp.exp(-jnp.abs(xg)))
    g = -jnp.exp(al) * sp
    return beta, g, _sigmoid(xg)


def _dn_pre_fwd(proj, conv_w, al_row, dt_row, name):
    t = proj.shape[0]

    def body(x_ref, h_ref, cw_ref, ba_ref, al_ref, dt_ref, q_ref, k_ref, v_ref, bg_ref):
        i = pl.program_id(0)
        keep = i > 0
        for grp in range(24):
            cols = slice(grp * HD, (grp + 1) * HD)
            ext = jnp.concatenate([jnp.where(keep, h_ref[:, cols], 0.0), x_ref[:, cols]], axis=0)
            s = _silu(_conv_group(ext, cw_ref, cols))
            seg, head = divmod(grp, NH)
            hc = slice(head * HD, (head + 1) * HD)
            if seg == 0:
                q_ref[:, hc] = s * lax.rsqrt(jnp.sum(s * s, axis=-1, keepdims=True) + L2_EPS) * (HD ** -0.5)
            elif seg == 1:
                k_ref[:, hc] = s * lax.rsqrt(jnp.sum(s * s, axis=-1, keepdims=True) + L2_EPS)
            else:
                v_ref[:, hc] = s
        lane = _iota((TR, 128), 1)
        rowc = _iota((TR, 128), 0) % CH
        beta, g, _ = _gate_terms(ba_ref[...], al_ref[...], dt_ref[...])
        step = 1
        while step < CH:
            g = g + jnp.where(rowc >= step, pltpu.roll(g, step, 0), 0.0)
            step *= 2
        bg_ref[...] = jnp.where(lane < NH, beta, jnp.where(lane < 2 * NH, g, 0.0))

    return pl.pallas_call(
        body, grid=(t // TR,),
        in_specs=[_row(3 * D, 0), _halo_prev(3 * D, 0), pl.BlockSpec((4, 3 * D), lambda i: (0, 0)),
                  _row(128, OFF_BA // 128), _vec(128), _vec(128)],
        out_specs=[_row(D), _row(D), _row(D), _row(128)],
        out_shape=[SDS((t, D), F32), SDS((t, D), F32), SDS((t, D), F32), SDS((t, 128), F32)],
        compiler_params=_params(("parallel",)), name=name,
    )(proj, proj, conv_w, proj, al_row, dt_row)


def _dn_pre_bwd_act(proj, conv_w, al_row, dt_row, dq, dk, dv, dbg, dproj, name):
    t = proj.shape[0]

    def body(x_ref, h_ref, cw_ref, ba_ref, al_ref, dt_ref, dq_ref, dk_ref, dv_ref, dbg_ref, _,
             dc_ref, draw_ref, dal_ref, ddt_ref):
        i = pl.program_id(0)

        @pl.when(i == 0)
        def _():
            dal_ref[...] = jnp.zeros_like(dal_ref)
            ddt_ref[...] = jnp.zeros_like(ddt_ref)

        keep = i > 0
        for grp in range(24):
            cols = slice(grp * HD, (grp + 1) * HD)
            ext = jnp.concatenate([jnp.where(keep, h_ref[:, cols], 0.0), x_ref[:, cols]], axis=0)
            cv = _conv_group(ext, cw_ref, cols)
            seg, head = divmod(grp, NH)
            hc = slice(head * HD, (head + 1) * HD)
            if seg == 2:
                ds = dv_ref[:, hc]
            else:
                s = _silu(cv)
                r = lax.rsqrt(jnp.sum(s * s, axis=-1, keepdims=True) + L2_EPS)
                dy = dq_ref[:, hc] if seg == 0 else dk_ref[:, hc]
                c = (HD ** -0.5) if seg == 0 else 1.0
                ds = (c * r) * (dy - s * ((r * r) * jnp.sum(dy * s, axis=-1, keepdims=True)))
            dc_ref[:, cols] = ds * _dsilu(cv)
        lane = _iota((TR, 128), 1)
        rowc = _iota((TR, 128), 0) % CH
        isb = lane < NH
        isg = jnp.logical_and(lane >= NH, lane < 2 * NH)
        beta, g, sg = _gate_terms(ba_ref[...], al_ref[...], dt_ref[...])
        dbgv = dbg_ref[...]
        dg = dbgv
        step = 1
        while step < CH:
            dg = dg + jnp.where(rowc < CH - step, pltpu.roll(dg, TR - step, 0), 0.0)
            step *= 2
        da_raw = dg * (-jnp.exp(al_ref[...])) * sg
        draw = jnp.where(isb, dbgv * beta * (1.0 - beta), jnp.where(isg, da_raw, 0.0))
        draw_ref[:, :128] = draw.astype(draw_ref.dtype)
        draw_ref[:, 128:] = jnp.zeros((TR, MIXP - OFF_BA - 128), draw_ref.dtype)
        dal_ref[...] += jnp.sum(jnp.where(isg, dg * g, 0.0), axis=0, keepdims=True)
        ddt_ref[...] += jnp.sum(jnp.where(isg, da_raw, 0.0), axis=0, keepdims=True)

    return pl.pallas_call(
        body, grid=(t // TR,),
        in_specs=[_row(3 * D, 0), _halo_prev(3 * D, 0), pl.BlockSpec((4, 3 * D), lambda i: (0, 0)),
                  _row(128, OFF_BA // 128), _vec(128), _vec(128), _row(D), _row(D), _row(D), _row(128),
                  pl.BlockSpec(memory_space=pl.ANY)],
        out_specs=[_row(3 * D), _into(MIXP - OFF_BA, OFF_BA), _vec(128), _vec(128)],
        out_shape=[SDS((t, 3 * D), F32), SDS(dproj.shape, dproj.dtype), SDS((1, 128), F32), SDS((1, 128), F32)],
        input_output_aliases={10: 1},
        compiler_params=_params(("arbitrary",)), name=name,
    )(proj, proj, conv_w, proj, al_row, dt_row, dq, dk, dv, dbg, dproj)


def _dn_pre_bwd_conv(proj, conv_w, dconv, dproj, name):
    t = proj.shape[0]
    nt = t // TR
    ext_rows = TR + HALO

    def body(x_ref, h_ref, cw_ref, dc_ref, dn_ref, _, dx_ref, dcw_ref):
        i = pl.program_id(0)

        @pl.when(i == 0)
        def _():
            dcw_ref[...] = jnp.zeros_like(dcw_ref)

        keep_prev = i > 0
        keep_next = i < nt - 1
        for grp in range(24):
            cols = slice(grp * HD, (grp + 1) * HD)
            dct = dc_ref[:, cols]
            dext = jnp.concatenate([dct, jnp.where(keep_next, dn_ref[:, cols], 0.0)], axis=0)
            acc = cw_ref[3:4, cols] * dext
            for j in range(3):
                acc = acc + cw_ref[j:j + 1, cols] * pltpu.roll(dext, ext_rows - (3 - j), 0)
            dx_ref[:, cols] = acc[:TR].astype(dx_ref.dtype)
            xext = jnp.concatenate([jnp.where(keep_prev, h_ref[:, cols], 0.0), x_ref[:, cols]], axis=0)
            for j in range(4):
                xs = xext if j == 3 else pltpu.roll(xext, 3 - j, 0)
                dcw_ref[j:j + 1, cols] += jnp.sum(xs[HALO:] * dct, axis=0, keepdims=True)

    return pl.pallas_call(
        body, grid=(nt,),
        in_specs=[_row(3 * D, 0), _halo_prev(3 * D, 0), pl.BlockSpec((4, 3 * D), lambda i: (0, 0)),
                  _row(3 * D), _halo_next(3 * D, 0, nt), pl.BlockSpec(memory_space=pl.ANY)],
        out_specs=[_into(3 * D, OFF_Q), pl.BlockSpec((4, 3 * D), lambda i: (0, 0))],
        out_shape=[SDS(dproj.shape, dproj.dtype), SDS((4, 3 * D), F32)],
        input_output_aliases={5: 0},
        compiler_params=_params(("arbitrary",)), name=name,
    )(proj, proj, conv_w, dconv, dconv, dproj)


def _dn_post_fwd(o, proj, gn, name):
    t = o.shape[0]

    def body(o_ref, z_ref, g_ref, out_ref):
        gv = g_ref[...]
        for h in range(NH):
            hc = slice(h * HD, (h + 1) * HD)
            ov = o_ref[:, hc]
            r = lax.rsqrt(jnp.mean(ov * ov, axis=-1, keepdims=True) + RMS_EPS)
            out_ref[:, hc] = (((ov * r) * gv) * _silu(z_ref[:, hc])).astype(out_ref.dtype)

    return pl.pallas_call(
        body, grid=(t // TR,), in_specs=[_row(D), _row(D, OFF_Z // D), _vec(HD)], out_specs=_row(D),
        out_shape=SDS((t, D), BF16), compiler_params=_params(("parallel",)), name=name,
    )(o, proj, gn)


def _dn_post_bwd(o, proj, gn, dob, dproj, name):
    t = o.shape[0]

    def body(o_ref, z_ref, g_ref, d_ref, _, do_ref, dz_ref, dg_ref):
        @pl.when(pl.program_id(0) == 0)
        def _():
            dg_ref[...] = jnp.zeros_like(dg_ref)

        gv = g_ref[...]
        acc = jnp.zeros((1, HD), F32)
        for h in range(NH):
            hc = slice(h * HD, (h + 1) * HD)
            ov = o_ref[:, hc]
            zv = z_ref[:, hc]
            dv = d_ref[:, hc]
            r = lax.rsqrt(jnp.mean(ov * ov, axis=-1, keepdims=True) + RMS_EPS)
            n = ov * r
            dz_ref[:, hc] = (dv * (n * gv) * _dsilu(zv)).astype(dz_ref.dtype)
            dng = dv * _silu(zv)
            acc = acc + jnp.sum(dng * n, axis=0, keepdims=True)
            dn = dng * gv
            do_ref[:, hc] = r * (dn - n * jnp.mean(dn * n, axis=-1, keepdims=True))
        dg_ref[...] += acc

    return pl.pallas_call(
        body, grid=(t // TR,),
        in_specs=[_row(D), _row(D, OFF_Z // D), _vec(HD), _row(D), pl.BlockSpec(memory_space=pl.ANY)],
        out_specs=[_row(D), _into(D, OFF_Z), _vec(HD)],
        out_shape=[SDS((t, D), F32), SDS(dproj.shape, dproj.dtype), SDS((1, HD), F32)],
        input_output_aliases={4: 1},
        compiler_params=_params(("arbitrary",)), name=name,
    )(o, proj, gn, dob, dproj)


def _merge_fwd(ya, yb, proj, name):
    t = ya.shape[0]

    def body(a_ref, b_ref, gp_ref, gd_ref, o_ref):
        o_ref[...] = (_sigmoid(gp_ref[...]) * a_ref[...] + _sigmoid(gd_ref[...]) * b_ref[...]).astype(o_ref.dtype)

    return pl.pallas_call(
        body, grid=(t // TR,), in_specs=[_row(D), _row(D), _row(D, OFF_GP // D), _row(D, OFF_GD // D)],
        out_specs=_row(D), out_shape=SDS((t, D), BF16),
        compiler_params=_params(("parallel",)), name=name,
    )(ya, yb, proj, proj)


def _into(width, offset):
    assert offset % width == 0
    return pl.BlockSpec((TR, width), lambda i: (i, offset // width))


def _merge_bwd(dm, ya, yb, proj, dproj, name):
    t = ya.shape[0]

    def body(d_ref, a_ref, b_ref, gp_ref, gd_ref, _, da_ref, db_ref, dg_ref):
        dv = d_ref[...]
        sp = _sigmoid(gp_ref[...])
        sd = _sigmoid(gd_ref[...])
        da_ref[...] = dv * sp
        db_ref[...] = (dv * sd).astype(db_ref.dtype)
        dg_ref[:, :D] = (dv * a_ref[...] * sp * (1.0 - sp)).astype(dg_ref.dtype)
        dg_ref[:, D:] = (dv * b_ref[...] * sd * (1.0 - sd)).astype(dg_ref.dtype)

    return pl.pallas_call(
        body, grid=(t // TR,),
        in_specs=[_row(D), _row(D), _row(D), _row(D, OFF_GP // D), _row(D, OFF_GD // D),
                  pl.BlockSpec(memory_space=pl.ANY)],
        out_specs=[_row(D), _row(D), _into(2 * D, OFF_GP)],
        out_shape=[SDS((t, D), F32), SDS((t, D), BF16), SDS(dproj.shape, dproj.dtype)],
        input_output_aliases={5: 2},
        compiler_params=_params(("parallel",)), name=name,
    )(dm, ya, yb, proj, proj, dproj)


def _split2(x):
    hi = x.astype(BF16)
    return hi, (x - hi.astype(F32)).astype(BF16)


def _dot3(a, b, dims):
    ah, al = _split2(a)
    bh, bl = _split2(b)
    return _dg(ah, bh, dims) + (_dg(ah, bl, dims) + _dg(al, bh, dims))


def _neumann_inverses(mats):
    ri = _iota((CH, CH), 0)
    ci = _iota((CH, CH), 1)
    eye = jnp.where(ri == ci, 1.0, 0.0).astype(F32)
    xs = [-a for a in mats]
    ps = [eye + x for x in xs]
    for _ in range(5):
        xs = [_dot3(x, x, NN) for x in xs]
        ps = [p + _dot3(p, x, NN) for p, x in zip(ps, xs)]
    return ps


def _solve_with(inv):
    @jax.custom_vjp
    def solve(a, rhs):
        return _dot3(inv, rhs, NN)

    def fwd(a, rhs):
        sol = _dot3(inv, rhs, NN)
        return sol, sol

    def bwd(sol, d):
        drhs = _dot3(inv, d, TN)
        return -_dot3(drhs, sol, NT), drhs

    solve.defvjp(fwd, bwd)
    return solve


@jax.custom_vjp
def _rows_to_lanes(g64):
    ri = _iota((CH, CH), 0)
    ci = _iota((CH, CH), 1)
    diag = jnp.where(ri == ci, g64, 0.0)
    ones = jnp.ones((CH, CH), BF16)
    hi = diag.astype(BF16)
    rem = diag - hi.astype(F32)
    mid = rem.astype(BF16)
    lo = (rem - mid.astype(F32)).astype(BF16)
    return _dg(ones, hi, NN) + (_dg(ones, mid, NN) + _dg(ones, lo, NN))


def _rows_to_lanes_bwd(_, d):
    ri = _iota((CH, CH), 0)
    ci = _iota((CH, CH), 1)
    return (jnp.where(ri == ci, jnp.broadcast_to(jnp.sum(d, axis=0, keepdims=True), (CH, CH)), 0.0),)


_rows_to_lanes.defvjp(lambda g64: (_rows_to_lanes(g64), None), _rows_to_lanes_bwd)


def _chunk_local(solve_all, q, k, v, g128, g64, gl128, b128, b64):
    ri = _iota((CH, CH), 0)
    ci = _iota((CH, CH), 1)
    causal = ri >= ci
    strict = ri > ci
    gj = [_rows_to_lanes(g) for g in g64]
    decay = [jnp.where(causal, jnp.exp(jnp.where(causal, g - t, 0.0)), 0.0) for g, t in zip(g64, gj)]
    kk = [_nt(x, x) for x in k]
    a = [jnp.where(strict, b * m * dc, 0.0) for b, m, dc in zip(b64, kk, decay)]
    eg = [jnp.exp(g) for g in g128]
    rhs = [jnp.concatenate([b * x, (b * e) * y], axis=1) for b, x, e, y in zip(b128, v, eg, k)]
    sol = solve_all(a, rhs)
    qk = [jnp.where(causal, _nt(x, y) * dc, 0.0) for x, y, dc in zip(q, k, decay)]
    return ([s[:, :HD] for s in sol], [s[:, HD:] for s in sol], qk, [x * e for x, e in zip(q, eg)],
            [x * jnp.exp(gl - g) for x, gl, g in zip(k, gl128, g128)], [jnp.exp(gl) for gl in gl128])


def _all_head_gates(bgv):
    return tuple(list(z) for z in zip(*[_head_gates(bgv, h) for h in range(NH)]))


def _head_gates(bgv, h):
    lane = _iota((CH, 128), 1)
    row = _iota((CH, 128), 0)
    bcol = jnp.sum(jnp.where(lane == h, bgv, 0.0), axis=1, keepdims=True)
    gcol = jnp.sum(jnp.where(lane == NH + h, bgv, 0.0), axis=1, keepdims=True)
    g128 = jnp.broadcast_to(gcol, (CH, 128))
    gl128 = jnp.broadcast_to(jnp.sum(jnp.where(row == CH - 1, g128, 0.0), axis=0, keepdims=True), (CH, 128))
    return (g128, jnp.broadcast_to(gcol, (CH, CH)), gl128,
            jnp.broadcast_to(bcol, (CH, 128)), jnp.broadcast_to(bcol, (CH, CH)))


def _chunk_specs():
    row = pl.BlockSpec((CH, D), lambda i: (i, 0))
    small = pl.BlockSpec((CH, 128), lambda i: (i, 0))
    qk = pl.BlockSpec((NH, CH, CH), lambda i: (i, 0, 0))
    eg = pl.BlockSpec((1, NH, 128), lambda i: (i, 0, 0))
    return row, small, qk, eg


def _dn_local_fwd(q, k, v, bg, name):
    t = q.shape[0]
    n = t // CH

    def body(q_ref, k_ref, v_ref, bg_ref, u_ref, w_ref, qk_ref, qd_ref, kd_ref, eg_ref, inv_ref):
        cols = [slice(h * HD, (h + 1) * HD) for h in range(NH)]

        def solve_all(mats, rhs):
            invs = _neumann_inverses(mats)
            for h in range(NH):
                inv_ref[h] = invs[h]
            return [_dot3(m, r, NN) for m, r in zip(invs, rhs)]

        u, w, qk, qd, kd, egl = _chunk_local(
            solve_all, [q_ref[:, c] for c in cols], [k_ref[:, c] for c in cols], [v_ref[:, c] for c in cols],
            *_all_head_gates(bg_ref[...]))
        for h, hc in enumerate(cols):
            u_ref[:, hc] = u[h]
            w_ref[:, hc] = w[h].astype(w_ref.dtype)
            qd_ref[:, hc] = qd[h].astype(qd_ref.dtype)
            kd_ref[:, hc] = kd[h].astype(kd_ref.dtype)
            qk_ref[h] = qk[h].astype(qk_ref.dtype)
            eg_ref[0, h:h + 1, :] = egl[h][0:1, :]

    row, small, qkb, egb = _chunk_specs()
    return pl.pallas_call(
        body, grid=(n,), in_specs=[row, row, row, small], out_specs=[row, row, qkb, row, row, egb, qkb],
        out_shape=[SDS((t, D), F32), SDS((t, D), BF16), SDS((n * NH, CH, CH), BF16), SDS((t, D), BF16),
                   SDS((t, D), BF16), SDS((n, NH, 128), F32), SDS((n * NH, CH, CH), F32)],
        compiler_params=_params(("parallel",)), name=name,
    )(q, k, v, bg)


def _dn_local_bwd(q, k, v, bg, inv, du, dw, dqk, dqd, dkd, deg, name):
    t = q.shape[0]
    n = t // CH

    def body(q_ref, k_ref, v_ref, bg_ref, inv_ref, du_ref, dw_ref, dqk_ref, dqd_ref, dkd_ref, deg_ref,
             dq_ref, dk_ref, dv_ref, dbg_ref):
        bgv = bg_ref[...]
        lane = _iota((CH, 128), 1)
        row = _iota((CH, 128), 0)
        first = jnp.where(row == 0, 1.0, 0.0)
        acc = jnp.zeros((CH, 128), F32)
        cols = [slice(h * HD, (h + 1) * HD) for h in range(NH)]
        solves = [_solve_with(inv_ref[h]) for h in range(NH)]

        def solve_all(mats, rhs):
            return [f(m, r) for f, m, r in zip(solves, mats, rhs)]

        _, vjp = jax.vjp(functools.partial(_chunk_local, solve_all),
                         [q_ref[:, c] for c in cols], [k_ref[:, c] for c in cols], [v_ref[:, c] for c in cols],
                         *_all_head_gates(bgv))
        cts = ([du_ref[:, c].astype(F32) for c in cols], [dw_ref[:, c].astype(F32) for c in cols],
               [dqk_ref[h] for h in range(NH)],
               [dqd_ref[:, c].astype(F32) for c in cols], [dkd_ref[:, c].astype(F32) for c in cols],
               [jnp.broadcast_to(deg_ref[0, h:h + 1, :], (CH, 128)) * first for h in range(NH)])
        dq, dk, dv, dg128, dg64, dgl, db128, db64 = vjp(cts)
        for h, hc in enumerate(cols):
            dq_ref[:, hc] = dq[h]
            dk_ref[:, hc] = dk[h]
            dv_ref[:, hc] = dv[h]
            dg = jnp.sum(dg128[h], axis=1, keepdims=True) + jnp.sum(dg64[h], axis=1, keepdims=True)
            tot = jnp.sum(jnp.sum(dgl[h], axis=0, keepdims=True), axis=1, keepdims=True)
            dg = dg + jnp.where(row[:, 0:1] == CH - 1, tot, 0.0)
            db = jnp.sum(db128[h], axis=1, keepdims=True) + jnp.sum(db64[h], axis=1, keepdims=True)
            acc = acc + jnp.where(lane == h, db, 0.0) + jnp.where(lane == NH + h, dg, 0.0)
        dbg_ref[...] = acc

    row, small, qkb, egb = _chunk_specs()
    return pl.pallas_call(
        body, grid=(n,), in_specs=[row, row, row, small, qkb, row, row, qkb, row, row, egb],
        out_specs=[row, row, row, small],
        out_shape=[SDS((t, D), F32)] * 3 + [SDS((t, 128), F32)],
        compiler_params=_params(("parallel",)), name=name,
    )(q, k, v, bg, inv, du, dw, dqk, dqd, dkd, deg)


def _state_step(s, u, w, qk, qd, kd, egl):
    ws = [_nn(a, b) for a, b in zip(w, s)]
    v_new = [a - b for a, b in zip(u, ws)]
    qs = [_nn(a, b) for a, b in zip(qd, s)]
    intra = [_nn(a, b) for a, b in zip(qk, v_new)]
    upd = [_tn(a, b) for a, b in zip(kd, v_new)]
    return [a * e + b for a, e, b in zip(s, egl, upd)], [a + b for a, b in zip(qs, intra)]


def _dn_scan_fwd(u, w, qk, qd, kd, eg, name):
    t = u.shape[0]
    n = t // CH
    g = SCAN_CHUNKS

    def body(u_ref, w_ref, qk_ref, qd_ref, kd_ref, eg_ref, o_ref, save_ref, s_ref):
        @pl.when(pl.program_id(0) == 0)
        def _():
            s_ref[...] = jnp.zeros_like(s_ref)

        cols = [slice(h * HD, (h + 1) * HD) for h in range(NH)]
        s = [s_ref[h] for h in range(NH)]
        for c in range(g):
            rows = slice(c * CH, (c + 1) * CH)
            for h in range(NH):
                save_ref[c, h] = s[h].astype(save_ref.dtype)
            s, o = _state_step(
                s, [u_ref[rows, hc] for hc in cols], [w_ref[rows, hc].astype(F32) for hc in cols],
                [qk_ref[c * NH + h].astype(F32) for h in range(NH)], [qd_ref[rows, hc].astype(F32) for hc in cols],
                [kd_ref[rows, hc].astype(F32) for hc in cols], [eg_ref[c, h:h + 1, :] for h in range(NH)])
            for h, hc in enumerate(cols):
                o_ref[rows, hc] = o[h]
        for h in range(NH):
            s_ref[h] = s[h]

    row = pl.BlockSpec((g * CH, D), lambda i: (i, 0))
    qkb = pl.BlockSpec((g * NH, CH, CH), lambda i: (i, 0, 0))
    egb = pl.BlockSpec((g, NH, 128), lambda i: (i, 0, 0))
    return pl.pallas_call(
        body, grid=(n // g,), in_specs=[row, row, qkb, row, row, egb],
        out_specs=[row, pl.BlockSpec((g, NH, HD, HD), lambda i: (i, 0, 0, 0))],
        out_shape=[SDS((t, D), F32), SDS((n, NH, HD, HD), BF16)],
        scratch_shapes=[pltpu.VMEM((NH, HD, HD), F32)],
        compiler_params=_params(("arbitrary",)), name=name,
    )(u, w, qk, qd, kd, eg)


def _dn_scan_bwd(u, w, qk, qd, kd, eg, saved, do, name):
    t = u.shape[0]
    n = t // CH
    g = SCAN_CHUNKS
    last = n // g - 1

    def body(u_ref, w_ref, qk_ref, qd_ref, kd_ref, eg_ref, sv_ref, do_ref,
             du_ref, dw_ref, dqk_ref, dqd_ref, dkd_ref, deg_ref, ds_ref):
        @pl.when(pl.program_id(0) == 0)
        def _():
            ds_ref[...] = jnp.zeros_like(ds_ref)

        cols = [slice(h * HD, (h + 1) * HD) for h in range(NH)]
        ds = [ds_ref[h] for h in range(NH)]
        for c in reversed(range(g)):
            rows = slice(c * CH, (c + 1) * CH)
            _, vjp = jax.vjp(
                _state_step, [sv_ref[c, h].astype(F32) for h in range(NH)], [u_ref[rows, hc] for hc in cols],
                [w_ref[rows, hc].astype(F32) for hc in cols], [qk_ref[c * NH + h].astype(F32) for h in range(NH)],
                [qd_ref[rows, hc].astype(F32) for hc in cols], [kd_ref[rows, hc].astype(F32) for hc in cols],
                [eg_ref[c, h:h + 1, :] for h in range(NH)])
            ds, du, dw, dqk, dqd, dkd, deg = vjp((ds, [do_ref[rows, hc] for hc in cols]))
            for h, hc in enumerate(cols):
                du_ref[rows, hc] = du[h].astype(du_ref.dtype)
                dw_ref[rows, hc] = dw[h].astype(dw_ref.dtype)
                dqk_ref[c * NH + h] = dqk[h]
                dqd_ref[rows, hc] = dqd[h].astype(dqd_ref.dtype)
                dkd_ref[rows, hc] = dkd[h].astype(dkd_ref.dtype)
                deg_ref[c, h:h + 1, :] = deg[h]
        for h in range(NH):
            ds_ref[h] = ds[h]

    row = pl.BlockSpec((g * CH, D), lambda i: (last - i, 0))
    qkb = pl.BlockSpec((g * NH, CH, CH), lambda i: (last - i, 0, 0))
    egb = pl.BlockSpec((g, NH, 128), lambda i: (last - i, 0, 0))
    return pl.pallas_call(
        body, grid=(n // g,),
        in_specs=[row, row, qkb, row, row, egb,
                  pl.BlockSpec((g, NH, HD, HD), lambda i: (last - i, 0, 0, 0)), row],
        out_specs=[row, row, qkb, row, row, egb],
        out_shape=[SDS((t, D), BF16), SDS((t, D), BF16), SDS((n * NH, CH, CH), F32), SDS((t, D), BF16),
                   SDS((t, D), BF16), SDS((n, NH, 128), F32)],
        scratch_shapes=[pltpu.VMEM((NH, HD, HD), F32)],
        compiler_params=_params(("arbitrary",)), name=name,
    )(u, w, qk, qd, kd, eg, saved, do)


def _ada_fwd(c_all, ada_w, ada_b, name):
    ncol = ada_w.shape[1]

    def body(c_ref, w_ref, b_ref, o_ref):
        o_ref[...] = _dg(_silu(c_ref[...]), w_ref[...], NN, HI) + b_ref[...]

    return pl.pallas_call(body, out_shape=SDS((NDEV, ncol), F32),
                          compiler_params=pltpu.CompilerParams(vmem_limit_bytes=VMEM_LIMIT), name=name,
                          )(c_all, ada_w, ada_b)


def _ada_bwd(c_all_t, dmod, name):
    ncol = dmod.shape[1]

    def body(c_ref, d_ref, o_ref):
        sc = _silu(c_ref[...])
        acc = sc[:, 0:1] * d_ref[0:1, :]
        for b in range(1, NDEV):
            acc = acc + sc[:, b:b + 1] * d_ref[b:b + 1, :]
        o_ref[...] = acc

    return pl.pallas_call(body, out_shape=SDS((D, ncol), F32),
                          compiler_params=pltpu.CompilerParams(vmem_limit_bytes=VMEM_LIMIT), name=name,
                          )(c_all_t, dmod)


def _sum_devices(parts, out_dtype, name):
    _, r, c = parts.shape
    tr = TR if r % TR == 0 else r

    def body(p_ref, o_ref):
        acc = p_ref[0].astype(F32)
        for i in range(1, NDEV):
            acc = acc + p_ref[i].astype(F32)
        o_ref[...] = acc.astype(o_ref.dtype)

    return pl.pallas_call(
        body, grid=(r // tr,), in_specs=[pl.BlockSpec((NDEV, tr, c), lambda i: (0, i, 0))],
        out_specs=pl.BlockSpec((tr, c), lambda i: (i, 0)), out_shape=SDS((r, c), out_dtype),
        compiler_params=_params(("parallel",)), name=name,
    )(parts)


def _adam_tiles(r, c):
    if r % 8 == 0:
        return _pick(r, (256, 352, 128, 8)), c
    return r, (256 if c % 256 == 0 else c)


def _adam_math(w, gv, m, v):
    m_new = ADAM_B1 * m + (1.0 - ADAM_B1) * gv
    v_new = ADAM_B2 * v + (1.0 - ADAM_B2) * (gv * gv)
    bc1 = 1.0 - ADAM_B1 ** ADAM_STEP
    bc2 = 1.0 - ADAM_B2 ** ADAM_STEP
    return -ADAM_LR * ((m_new / bc1) / (jnp.sqrt(v_new / bc2) + ADAM_EPS) + ADAM_WD * w), m_new, v_new


def _adamw(w, g, m, v, name):
    r, c = w.shape
    tr, tc = _adam_tiles(r, c)

    def body(w_ref, g_ref, m_ref, v_ref, d_ref, nm_ref, nv_ref):
        d_ref[...], nm_ref[...], nv_ref[...] = _adam_math(w_ref[...], g_ref[...], m_ref[...], v_ref[...])

    spec = pl.BlockSpec((tr, tc), lambda i, j: (i, j))
    return pl.pallas_call(
        body, grid=(r // tr, c // tc), in_specs=[spec] * 4, out_specs=[spec] * 3,
        out_shape=[SDS((r, c), F32)] * 3, compiler_params=_params(("parallel", "parallel")), name=name,
    )(w, g, m, v)


def _reduce_adamw(parts, w, m, v, name):
    r, c = w.shape
    tr, tc = _adam_tiles(r, c)

    def body(p_ref, w_ref, m_ref, v_ref, g_ref, d_ref, nm_ref, nv_ref):
        gv = p_ref[0].astype(F32)
        for i in range(1, NDEV):
            gv = gv + p_ref[i].astype(F32)
        g_ref[...] = gv
        d_ref[...], nm_ref[...], nv_ref[...] = _adam_math(w_ref[...], gv, m_ref[...], v_ref[...])

    spec = pl.BlockSpec((tr, tc), lambda i, j: (i, j))
    return pl.pallas_call(
        body, grid=(r // tr, c // tc),
        in_specs=[pl.BlockSpec((NDEV, tr, tc), lambda i, j: (0, i, j))] + [spec] * 3, out_specs=[spec] * 4,
        out_shape=[SDS((r, c), F32)] * 4, compiler_params=_params(("parallel", "parallel")), name=name,
    )(parts, w, m, v)


ANY = pl.BlockSpec(memory_space=pl.ANY)
MESH = pl.DeviceIdType.MESH


def _all_gather(xs, name, after=None):
    n = len(xs)
    extra = [] if after is None else [after]

    def body(*refs):
        x_refs, out_refs = refs[:n], refs[n + len(extra):2 * n + len(extra)]
        send_sems, recv_sems, local_sems = refs[-3:]
        mx, my, mc = lax.axis_index("x"), lax.axis_index("y"), lax.axis_index("c")
        me, sibling = (mx, my, mc), (mx, my, 1 - mc)
        chips = [(1 - mx, my), (mx, 1 - my), (1 - mx, 1 - my)]

        def rows(a, px, py, pc):
            return out_refs[a].at[4 * px + 2 * py + pc]

        def copy(a, k, block, to, src=None):
            return pltpu.make_async_remote_copy(
                src_ref=rows(a, *block) if src is None else src, dst_ref=rows(a, *block),
                send_sem=send_sems.at[a, k], recv_sem=recv_sems.at[a, k], device_id=to, device_id_type=MESH)

        mine = [pltpu.make_async_copy(x_refs[a], rows(a, *me), local_sems.at[a]) for a in range(n)]
        for cp in mine:
            cp.start()
        first = []
        for a in range(n):
            first.append(copy(a, 0, me, sibling, src=x_refs[a]))
            first += [copy(a, 1 + j, me, (*chip, mc), src=x_refs[a]) for j, chip in enumerate(chips)]
        for cp in first:
            cp.start()
        passed = []
        for a in range(n):
            for j, chip in enumerate(chips):
                copy(a, 1 + j, (*chip, mc), me).wait_recv()
                passed.append(copy(a, 4 + j, (*chip, mc), sibling))
                passed[-1].start()
        for a in range(n):
            copy(a, 0, sibling, me).wait_recv()
            for j, chip in enumerate(chips):
                copy(a, 4 + j, (*chip, 1 - mc), me).wait_recv()
        for cp in first + passed:
            cp.wait_send()
        for cp in mine:
            cp.wait()

    return pl.pallas_call(
        body, out_shape=[SDS((NDEV,) + x.shape, x.dtype) for x in xs], in_specs=[ANY] * (n + len(extra)),
        out_specs=[ANY] * n,
        scratch_shapes=[pltpu.SemaphoreType.DMA((n, 7)), pltpu.SemaphoreType.DMA((n, 7)),
                        pltpu.SemaphoreType.DMA((n,))],
        name=name,
    )(*xs, *extra)


HBM = pl.BlockSpec(memory_space=pltpu.HBM)
SEM = pl.BlockSpec(memory_space=pltpu.SEMAPHORE)
EFFECT = pltpu.SideEffectType.DATAFLOW_SIDE_EFFECTING


def _peers():
    mx, my, mc = lax.axis_index("x"), lax.axis_index("y"), lax.axis_index("c")
    out = []
    for k in range(1, NDEV):
        out.append((1 - mx if k & 4 else mx, 1 - my if k & 2 else my, 1 - mc if k & 1 else mc))
    return 4 * mx + 2 * my + mc, out


NEAR = (0, 1, 3, 5)


def _push_start(srcs, sliced, name, after=None, near=()):
    n = len(srcs)
    extra = [] if after is None else [after]
    lands = [lax.empty(s.shape if sliced else (NDEV,) + s.shape, s.dtype) for s in srcs]

    def body(*refs):
        src_refs, land_refs = refs[:n], refs[n:2 * n]
        outs = refs[2 * n + len(extra):]
        send_sems, recv_sems = outs[:n], outs[n:2 * n]
        token = refs[-1]
        me, peers = _peers()
        for a in range(n):
            for k, (px, py, pc) in enumerate(peers):
                if a in near and k not in NEAR:
                    continue
                src = src_refs[a].at[4 * px + 2 * py + pc] if sliced else src_refs[a]
                pltpu.make_async_remote_copy(
                    src_ref=src, dst_ref=land_refs[a].at[me], send_sem=send_sems[a].at[k],
                    recv_sem=recv_sems[a].at[k], device_id=(px, py, pc), device_id_type=MESH).start()
            pltpu.make_async_copy(src_refs[a].at[me] if sliced else src_refs[a], land_refs[a].at[me],
                                  send_sems[a].at[NDEV - 1]).start()
        token[...] = jnp.zeros_like(token)

    outs = pl.pallas_call(
        body, name=name,
        out_shape=([pltpu.SemaphoreType.DMA((NDEV,))] * n + [pltpu.SemaphoreType.DMA((NDEV - 1,))] * n
                   + [pltpu.HBM(s.shape, s.dtype) for s in srcs] + [pltpu.HBM(l.shape, l.dtype) for l in lands]
                   + [SDS((8, 128), F32)]),
        in_specs=[HBM] * (2 * n) + [pl.BlockSpec(memory_space=pl.ANY)] * len(extra),
        out_specs=[SEM] * (2 * n) + [HBM] * (2 * n) + [pl.BlockSpec(memory_space=pltpu.VMEM)],
        input_output_aliases={i: 2 * n + i for i in range(2 * n)},
        compiler_params=pltpu.CompilerParams(has_side_effects=EFFECT),
    )(*[pltpu.with_memory_space_constraint(s, pltpu.HBM) for s in srcs],
      *[pltpu.with_memory_space_constraint(l, pltpu.HBM) for l in lands], *extra)
    sends, recvs = outs[:n], outs[n:2 * n]
    src_thru, land_thru = outs[2 * n:3 * n], outs[3 * n:4 * n]
    return [(sends[a], recvs[a], src_thru[a], land_thru[a]) for a in range(n)], outs[-1]


def _push_wait(started, sliced, after, name, near=()):
    n = len(started)
    afters = list(after) if isinstance(after, (list, tuple)) else [after]

    def body(*refs):
        src_refs, land_refs = refs[:n], refs[n:2 * n]
        send_sems, recv_sems = refs[2 * n:3 * n], refs[3 * n:4 * n]
        me, peers = _peers()
        for a in range(n):
            for k, (px, py, pc) in enumerate(peers):
                if a in near and k not in NEAR:
                    continue
                src = src_refs[a].at[4 * px + 2 * py + pc] if sliced else src_refs[a]
                cp = pltpu.make_async_remote_copy(
                    src_ref=src, dst_ref=land_refs[a].at[me], send_sem=send_sems[a].at[k],
                    recv_sem=recv_sems[a].at[k], device_id=(px, py, pc), device_id_type=MESH)
                cp.wait_send()
                cp.wait_recv()
            pltpu.make_async_copy(src_refs[a].at[me] if sliced else src_refs[a], land_refs[a].at[me],
                                  send_sems[a].at[NDEV - 1]).wait()

    srcs = [s[2] for s in started]
    lands = [s[3] for s in started]
    outs = pl.pallas_call(
        body, name=name,
        out_shape=[pltpu.HBM(s.shape, s.dtype) for s in srcs] + [pltpu.HBM(l.shape, l.dtype) for l in lands],
        in_specs=[HBM] * (2 * n) + [SEM] * (2 * n) + [pl.BlockSpec(memory_space=pl.ANY)] * len(afters),
        out_specs=[HBM] * (2 * n),
        input_output_aliases={i: i for i in range(2 * n)},
        compiler_params=pltpu.CompilerParams(has_side_effects=EFFECT),
    )(*srcs, *lands, *[s[0] for s in started], *[s[1] for s in started], *afters)
    return outs[n:]


def _relay_to_sibling(land, name):
    def body(_, land_ref, send_sems, recv_sems):
        mx, my, mc = lax.axis_index("x"), lax.axis_index("y"), lax.axis_index("c")
        chips = [(1 - mx, my), (mx, 1 - my), (1 - mx, 1 - my)]

        def copy(j, core):
            slot = land_ref.at[4 * chips[j][0] + 2 * chips[j][1] + core]
            return pltpu.make_async_remote_copy(
                src_ref=slot, dst_ref=slot, send_sem=send_sems.at[j], recv_sem=recv_sems.at[j],
                device_id=(mx, my, 1 - mc), device_id_type=MESH)

        mine = [copy(j, mc) for j in range(3)]
        for cp in mine:
            cp.start()
        for j in range(3):
            copy(j, 1 - mc).wait_recv()
        for cp in mine:
            cp.wait_send()

    return pl.pallas_call(
        body, out_shape=SDS(land.shape, land.dtype), in_specs=[ANY], out_specs=ANY, input_output_aliases={0: 0},
        scratch_shapes=[pltpu.SemaphoreType.DMA((3,)), pltpu.SemaphoreType.DMA((3,))], name=name,
    )(land)


def _cols_from_blocks(blocks):
    _, rows, w = blocks.shape
    return blocks.transpose(1, 0, 2).reshape(rows, NDEV * w)


def _cols_to_blocks(full):
    rows, total = full.shape
    return full.reshape(rows, NDEV, total // NDEV).transpose(1, 0, 2)


def _mix_pad(wt):
    xp, q, k, v, z, ba, gp, gd = jnp.split(wt, (512, 1536, 2560, 3584, 4608, 4624, 5648), axis=0)
    pad = jnp.zeros((MIXP - OFF_BA - 16, wt.shape[1]), wt.dtype)
    return jnp.concatenate([q, k, v, z, gp, gd, xp, ba, pad], axis=0)


def _mix_unpad(wt):
    q, k, v, z, gp, gd, xp, ba = (wt[OFF_Q:OFF_K], wt[OFF_K:OFF_V], wt[OFF_V:OFF_Z], wt[OFF_Z:OFF_GP],
                                  wt[OFF_GP:OFF_GD], wt[OFF_GD:OFF_XP], wt[OFF_XP:OFF_BA], wt[OFF_BA:OFF_BA + 16])
    return jnp.concatenate([xp, q, k, v, z, ba, gp, gd], axis=0)


def _lane_row(vec8):
    return jnp.zeros((1, 128), F32).at[0, NH:2 * NH].set(vec8)


def _ffn_fwd(x, h, gate, w_in, w_out, tag, next_norm=None, token=None, start_more=None):
    if isinstance(w_in, tuple):
        w_in, = _push_wait([w_in], False, h, f"{tag}_gather_wait_in")
    w_in = w_in.reshape(2 * FH, D)
    u, a = _swiglu_up(h, w_in, f"{tag}_up", after=token)
    w_out, = _push_wait([w_out], False, a, f"{tag}_gather_wait_out")
    w_out = w_out.reshape(FH, D)
    outs = _matmul_residual(a, w_out, x, gate, 0.5, a_blk=True, norm=next_norm, name=f"{tag}_down",
                            after=None if start_more is None else start_more(h))
    return outs[0], (h, u, a, outs[1]), w_in, w_out, (outs[2] if next_norm else None)


def _ffn_bwd(dx_out, dy, x, g, scale, w_in, w_out, saved, tag, below=None):
    h, u, a, _ = saved
    t = x.shape[0]
    dw_out = _matmul(a, dy, ta=True, a_blk=True, out_dtype=BF16, name=f"{tag}_down_dw")
    sent_out, token = _push_start([dw_out.reshape(NDEV, FH // NDEV, D)], True, f"{tag}_grad_start_out")
    du = _swiglu_down_bwd(dy, w_out, u, f"{tag}_down_dx", after=token).reshape(NDEV, t, FB)
    dw_in = _matmul(du, h, ta=True, a_blk=True, out_dtype=BF16, name=f"{tag}_up_dw")
    sent_in, token = _push_start([dw_in.reshape(NDEV, FB, D)], True, f"{tag}_grad_start_in")
    dh = _matmul(du, w_in, a_blk=True, out_dtype=F32, name=f"{tag}_up_dx", after=token)
    return _norm_mod_bwd(x, g, scale, dh, dx_out, f"{tag}_norm_bwd", below), sent_in + sent_out


def kernel(x, c, ada_w, ada_b, norm_g, ffn1_w_in, ffn1_w_out, ffn2_w_in, ffn2_w_out, mix_w_in, conv_w, a_log, dt_bias, dn_norm_g, pool_w, pool_scale, pool_proj, dn_proj, mix_w_out, final_g, loss_target, m_ada_w, m_ada_b, m_norm_g, m_ffn1_w_in, m_ffn1_w_out, m_ffn2_w_in, m_ffn2_w_out, m_mix_w_in, m_conv_w, m_a_log, m_dt_bias, m_dn_norm_g, m_pool_w, m_pool_scale, m_pool_proj, m_dn_proj, m_mix_w_out, m_final_g, v_ada_w, v_ada_b, v_norm_g, v_ffn1_w_in, v_ffn1_w_out, v_ffn2_w_in, v_ffn2_w_out, v_mix_w_in, v_conv_w, v_a_log, v_dt_bias, v_dn_norm_g, v_pool_w, v_pool_scale, v_pool_proj, v_dn_proj, v_mix_w_out, v_final_g):
    me = 4 * lax.axis_index("x") + 2 * lax.axis_index("y") + lax.axis_index("c")
    x0 = x[0]
    target = loss_target[0]
    t = x0.shape[0]

    big = [ffn1_w_in[0], ffn1_w_out[0], ffn2_w_in[0], ffn2_w_out[0], mix_w_in[0], pool_proj[0], dn_proj[0],
           mix_w_out[0]]
    small = jnp.concatenate([c.reshape(8, 128), conv_w[0].reshape(12, 128), norm_g[0].reshape(3, 128),
                             jnp.zeros((1, 128), F32)], axis=0)
    small_all, = _all_gather([small], "gather_small")
    c_all = small_all[:, 0:8, :].reshape(NDEV, D)
    conv_full = small_all[:, 8:20, :].reshape(NDEV, 4, 384).transpose(1, 0, 2).reshape(4, 3 * D)
    norm_full = small_all[:, 20:23, :].reshape(NDEV, 3, 128).transpose(1, 0, 2).reshape(3, D)

    ncol = ada_w.shape[2]
    ada_b_mine = lax.dynamic_slice(ada_b, (0, me * ncol), (1, ncol))
    mod_cols = _ada_fwd(c_all, ada_w[0], ada_b_mine, "ada_fwd")
    transposed = (0, 2, 4)
    payload = [(w.T if i in transposed else w).astype(BF16) for i, w in enumerate(big)]
    mod_all, w_in1 = _all_gather([mod_cols, payload[0]], "gather_mod_first_weight")
    started, token = _push_start([payload[1], payload[4]], False, "gather_start", after=mod_all, near=(1,))
    started = {1: started[0], 4: started[1]}

    def start_rest(h):
        more, token = _push_start([payload[i] for i in (5, 6, 7, 2, 3)], False, "gather_start_rest", after=h)
        started.update(zip((5, 6, 7, 2, 3), more))
        return token

    mod = lax.dynamic_index_in_dim(mod_all, me, axis=1, keepdims=False).reshape(9, D)
    shift = [mod[3 * s:3 * s + 1] for s in range(3)]
    scale = [mod[3 * s + 1:3 * s + 2] for s in range(3)]
    gate = [mod[3 * s + 2:3 * s + 3] for s in range(3)]
    ng = [norm_full[s:s + 1] for s in range(3)]
    fg = final_g.reshape(1, D)
    al_row = _lane_row(a_log[0])
    dt_row = _lane_row(dt_bias[0])
    gn = dn_norm_g
    pw = pool_w[0]
    ps = pool_scale

    h0 = _norm_mod_fwd(x0, ng[0], shift[0], scale[0], "ffn1_norm", after=token)
    x1, saved1, w_in1, w_out1, h1 = _ffn_fwd(x0, h0, gate[0], w_in1, started[1], "ffn1",
                                             (ng[1], shift[1], scale[1]), token, start_rest)

    seg, = _push_wait([started[4]], False, h1, "mix_gather_wait", near=(0,))
    w_mix = _mix_pad(_relay_to_sibling(seg, "mix_gather_relay").reshape(MIX_RAW, D))
    proj = _matmul(h1, w_mix, tb=True, out_dtype=F32, name="mix_in")
    qh, kh, vh, bg = _dn_pre_fwd(proj, conv_full, al_row, dt_row, "dn_pre")
    seg = _push_wait([started[i] for i in (5, 6, 7)], False, qh, "mix_gather_wait_rest")
    w_pp = _cols_from_blocks(seg[0])
    w_dn = seg[1].reshape(D, D)
    w_mo = seg[2].reshape(D, D)
    ya = _pool_fwd(proj, pw, ps, w_pp, "pool_fwd")
    u, w, qk, qd, kd, eg, inv = _dn_local_fwd(qh, kh, vh, bg, "dn_local")
    o, s_saved = _dn_scan_fwd(u, w, qk, qd, kd, eg, "dn_scan")
    ob = _dn_post_fwd(o, proj, gn, "dn_post")
    yb = _matmul(ob, w_dn, out_dtype=F32, name="dn_out")
    merged = _merge_fwd(ya, yb, proj, "merge")
    x2, mix_y, h2 = _matmul_residual(merged, w_mo, x1, gate[1], 1.0, norm=(ng[2], shift[2], scale[2]),
                                     name="mix_out")

    x3, saved2, w_in2, w_out2, _ = _ffn_fwd(x2, h2, gate[2], started[2], started[3], "ffn2")
    loss_row, dx3, dfg, dy2, dgate2 = _final_loss(x3, fg, target, (saved2[3], gate[2], 0.5), "loss")

    (dx2, dsh2, dsc2, dng2, dmy, dgate1), sent2 = _ffn_bwd(dx3, dy2, x2, ng[2], scale[2], w_in2, w_out2, saved2,
                                                           "ffn2", (mix_y, gate[1], 1.0))

    dmerged = _matmul(dmy, w_mo, tb=True, out_dtype=F32, name="mix_out_dx")
    dw_mo = _matmul(merged, dmy, ta=True, out_dtype=BF16, name="mix_out_dw")
    dproj = lax.empty((t, MIXP), BF16)
    dya, dyb, dproj = _merge_bwd(dmerged, ya, yb, proj, dproj, "merge_bwd")
    dob = _matmul(dyb, w_dn, tb=True, out_dtype=F32, name="dn_out_dx")
    dw_dn = _matmul(ob, dyb, ta=True, out_dtype=BF16, name="dn_out_dw")
    do, dproj, dgn = _dn_post_bwd(o, proj, gn, dob, dproj, "dn_post_bwd")
    du, dw, dqk, dqd, dkd, deg = _dn_scan_bwd(u, w, qk, qd, kd, eg, s_saved, do, "dn_scan_bwd")
    dqh, dkh, dvh, dbg = _dn_local_bwd(qh, kh, vh, bg, inv, du, dw, dqk, dqd, dkd, deg, "dn_local_bwd")
    dconv, dproj, dal, ddt = _dn_pre_bwd_act(proj, conv_full, al_row, dt_row, dqh, dkh, dvh, dbg, dproj,
                                             "dn_pre_bwd_act")
    dproj, dcw = _dn_pre_bwd_conv(proj, conv_full, dconv, dproj, "dn_pre_bwd_conv")
    dwin, dpl, dpw, dps, dpp = _pool_bwd_local(proj, pw, ps, w_pp, dya, "pool_bwd_local")
    dproj = _pool_bwd_window(dwin, dpl, dproj, "pool_bwd_window")
    dw_mix = _matmul(dproj, h1, ta=True, out_dtype=BF16, name="mix_in_dw")
    sent1, token = _push_start(
        [_mix_unpad(dw_mix).reshape(NDEV, MIX_RAW // NDEV, D), _cols_to_blocks(dpp.astype(BF16)),
         dw_dn.reshape(NDEV, -1, D), dw_mo.reshape(NDEV, -1, D)], True, "mix_grad_start")
    dh1 = _matmul(dproj, w_mix, out_dtype=F32, name="mix_in_dx", after=token)
    dx1, dsh1, dsc1, dng1, dy0, dgate0 = _norm_mod_bwd(x1, ng[1], scale[1], dh1, dx2, "mix_norm_bwd",
                                                       (saved1[3], gate[0], 0.5))

    (dx0, dsh0, dsc0, dng0), sent0 = _ffn_bwd(dx1, dy0, x0, ng[0], scale[0], w_in1, w_out1, saved1, "ffn1")

    dmod = jnp.concatenate([dsh0, dsc0, dgate0, dsh1, dsc1, dgate1, dsh2, dsc2, dgate2], axis=1).reshape(-1)
    flat = jnp.concatenate([
        dmod, dal[0, NH:2 * NH], ddt[0, NH:2 * NH], dgn.reshape(-1), dps.reshape(-1), dfg.reshape(-1),
        dpw.reshape(-1), jnp.concatenate([dng0, dng1, dng2], axis=0).reshape(-1), dcw.reshape(-1),
        loss_row[0, 0:1]])
    nflat = 90 * D
    flat = jnp.concatenate([flat, jnp.zeros((nflat - flat.shape[0],), F32)]).reshape(90, D)
    sent_small, small_token = _push_start([flat], False, "small_grad_start")

    def small_grads(flat_all):
        tot = _sum_devices(flat_all, F32, "sum_small_grads").reshape(-1)
        dmod_all = flat_all.reshape(NDEV, nflat)[:, :9 * D]
        dmod_cols = lax.dynamic_slice(dmod_all, (0, me * ncol), (NDEV, ncol))
        g_ada_w = _ada_bwd(c_all.T, dmod_cols, "ada_bwd")
        p = 0
        pieces = {}
        for nm, size in (("ada_b", 9 * D), ("a_log", NH), ("dt_bias", NH), ("dn_norm_g", HD), ("pool_scale", PW),
                         ("final_g", D), ("pool_w", 4 * PG * PG), ("norm_g", 3 * D), ("conv_w", 12 * D),
                         ("loss", 1)):
            pieces[nm] = tot[p:p + size]
            p += size
        g_norm = lax.dynamic_slice(pieces["norm_g"].reshape(3, D), (0, me * 128), (3, 128))
        g_conv = lax.dynamic_slice(pieces["conv_w"].reshape(4, 3 * D), (0, me * 384), (4, 384))
        return pieces["loss"][0], {
            "ada_w": g_ada_w.reshape(ada_w.shape), "ada_b": pieces["ada_b"].reshape(ada_b.shape),
            "norm_g": g_norm.reshape(norm_g.shape), "conv_w": g_conv.reshape(conv_w.shape),
            "a_log": pieces["a_log"].reshape(a_log.shape), "dt_bias": pieces["dt_bias"].reshape(dt_bias.shape),
            "dn_norm_g": pieces["dn_norm_g"].reshape(dn_norm_g.shape),
            "pool_w": pieces["pool_w"].reshape(pool_w.shape),
            "pool_scale": pieces["pool_scale"].reshape(pool_scale.shape),
            "final_g": pieces["final_g"].reshape(final_g.shape),
        }

    grads = {}
    weights = {"ada_w": ada_w, "ada_b": ada_b, "norm_g": norm_g, "ffn1_w_in": ffn1_w_in, "ffn1_w_out": ffn1_w_out,
               "ffn2_w_in": ffn2_w_in, "ffn2_w_out": ffn2_w_out, "mix_w_in": mix_w_in, "conv_w": conv_w,
               "a_log": a_log, "dt_bias": dt_bias, "dn_norm_g": dn_norm_g, "pool_w": pool_w,
               "pool_scale": pool_scale, "pool_proj": pool_proj, "dn_proj": dn_proj, "mix_w_out": mix_w_out,
               "final_g": final_g}
    m_in = {"ada_w": m_ada_w, "ada_b": m_ada_b, "norm_g": m_norm_g, "ffn1_w_in": m_ffn1_w_in,
            "ffn1_w_out": m_ffn1_w_out, "ffn2_w_in": m_ffn2_w_in, "ffn2_w_out": m_ffn2_w_out,
            "mix_w_in": m_mix_w_in, "conv_w": m_conv_w, "a_log": m_a_log, "dt_bias": m_dt_bias,
            "dn_norm_g": m_dn_norm_g, "pool_w": m_pool_w, "pool_scale": m_pool_scale, "pool_proj": m_pool_proj,
            "dn_proj": m_dn_proj, "mix_w_out": m_mix_w_out, "final_g": m_final_g}
    v_in = {"ada_w": v_ada_w, "ada_b": v_ada_b, "norm_g": v_norm_g, "ffn1_w_in": v_ffn1_w_in,
            "ffn1_w_out": v_ffn1_w_out, "ffn2_w_in": v_ffn2_w_in, "ffn2_w_out": v_ffn2_w_out,
            "mix_w_in": v_mix_w_in, "conv_w": v_conv_w, "a_log": v_a_log, "dt_bias": v_dt_bias,
            "dn_norm_g": v_dn_norm_g, "pool_w": v_pool_w, "pool_scale": v_pool_scale, "pool_proj": v_pool_proj,
            "dn_proj": v_dn_proj, "mix_w_out": v_mix_w_out, "final_g": v_final_g}

    names = list(weights)
    large = ("ada_w", "ffn1_w_in", "ffn1_w_out", "ffn2_w_in", "ffn2_w_out", "mix_w_in", "pool_proj", "dn_proj",
             "mix_w_out")
    delta, new_m, new_v = {}, {}, {}

    flipped = ("ffn1_w_in", "ffn2_w_in", "mix_w_in")

    def views(nm):
        shp = weights[nm].shape
        two_d = (shp[-2], shp[-1])
        if nm in flipped:
            return (lambda a: a.reshape(two_d).T), (lambda a: a.T.reshape(shp))
        return (lambda a: a.reshape(two_d)), (lambda a: a.reshape(shp))

    def reduce_update(sent, group, after, tag):
        done = []
        for nm, r in zip(group, _push_wait(sent, True, after, f"{tag}_grad_wait")):
            view, back = views(nm)
            g_, d_, m_, v_ = _reduce_adamw(r, view(weights[nm]), view(m_in[nm]), view(v_in[nm]), f"adamw_{nm}")
            grads[nm], delta[nm], new_m[nm], new_v[nm] = back(g_), back(d_), back(m_), back(v_)
            done.append(d_)
        return done

    done = reduce_update(sent2, ("ffn2_w_in", "ffn2_w_out"), small_token, "ffn2")
    done += reduce_update(sent1, ("mix_w_in", "pool_proj", "dn_proj", "mix_w_out"), done, "mix")
    flat_all, = _push_wait(sent_small, False, done, "small_grad_wait")
    loss, small = small_grads(flat_all)
    grads.update(small)
    view, back = views("ada_w")
    done, m_, v_ = _adamw(view(ada_w), view(grads["ada_w"]), view(m_ada_w), view(v_ada_w), "adamw_ada_w")
    delta["ada_w"], new_m["ada_w"], new_v["ada_w"] = back(done), back(m_), back(v_)
    reduce_update(sent0, ("ffn1_w_in", "ffn1_w_out"), done, "ffn1")
    rest = [nm for nm in names if nm not in large]
    total = sum(weights[nm].size for nm in rest)
    padded = -(-total // D) * D

    def pack(tree, fill):
        flat_ = jnp.concatenate([tree[nm].reshape(-1) for nm in rest])
        return jnp.concatenate([flat_, jnp.full((padded - total,), fill, F32)]).reshape(-1, D)

    d_, m_, v_ = _adamw(pack(weights, 0.0), pack(grads, 0.0), pack(m_in, 0.0), pack(v_in, 1.0), "adamw_small")
    p = 0
    for nm in rest:
        size = weights[nm].size
        shp = weights[nm].shape
        delta[nm] = d_.reshape(-1)[p:p + size].reshape(shp)
        new_m[nm] = m_.reshape(-1)[p:p + size].reshape(shp)
        new_v[nm] = v_.reshape(-1)[p:p + size].reshape(shp)
        p += size

    grad_x = dx0.reshape(x.shape)
    return (loss, grad_x, *[grads[nm] for nm in names], *[delta[nm] for nm in names],
            *[new_m[nm] for nm in names], *[new_v[nm] for nm in names])
```

```python
import functools

import jax
import jax.numpy as jnp
from jax import lax
from jax.experimental import pallas as pl
from jax.experimental.pallas import tpu as pltpu

F32 = jnp.float32
BF16 = jnp.bfloat16
SDS = jax.ShapeDtypeStruct
HI = lax.Precision.HIGHEST

D = 1024
FH = 2816
FB = 704
NH = 8
HD = 128
CH = 64
SCAN_CHUNKS = 2
NDEV = 8
PW = 512
PG = 128
RMS_EPS = 1e-6
L2_EPS = 1e-6
TR = 512
HALO = 16
VMEM_LIMIT = 56 * 1024 * 1024

MIXP = 6912
OFF_Q, OFF_K, OFF_V, OFF_Z, OFF_GP, OFF_GD, OFF_XP, OFF_BA = 0, 1024, 2048, 3072, 4096, 5120, 6144, 6656
MIX_RAW = 6672

ADAM_LR = 0.001
ADAM_B1 = 0.9
ADAM_B2 = 0.999
ADAM_EPS = 1e-08
ADAM_WD = 0.01
ADAM_STEP = 10

NN = (((1,), (0,)), ((), ()))
NT = (((1,), (1,)), ((), ()))
TN = (((0,), (0,)), ((), ()))


def _dg(a, b, dims, prec=None):
    return lax.dot_general(a, b, dims, precision=prec, preferred_element_type=F32)


def _make_dots(prec):
    @jax.custom_vjp
    def nn(a, b):
        return _dg(a, b, NN, prec)

    @jax.custom_vjp
    def nt(a, b):
        return _dg(a, b, NT, prec)

    @jax.custom_vjp
    def tn(a, b):
        return _dg(a, b, TN, prec)

    nn.defvjp(lambda a, b: (nn(a, b), (a, b)), lambda r, d: (nt(d, r[1]), tn(r[0], d)))
    nt.defvjp(lambda a, b: (nt(a, b), (a, b)), lambda r, d: (nn(d, r[1]), tn(d, r[0])))
    tn.defvjp(lambda a, b: (tn(a, b), (a, b)), lambda r, d: (nt(r[1], d), nn(r[0], d)))
    return nn, nt, tn


_nn, _nt, _tn = _make_dots(None)


def _params(sem):
    return pltpu.CompilerParams(dimension_semantics=sem, vmem_limit_bytes=VMEM_LIMIT)


def _sigmoid(x):
    return 1.0 / (1.0 + jnp.exp(-x))


def _silu(x):
    return x * _sigmoid(x)


def _dsilu(x):
    s = _sigmoid(x)
    return s * (1.0 + x * (1.0 - s))


def _pick(n, cands):
    for c in cands:
        if n % c == 0:
            return c
    raise ValueError(f"no tile for {n}")


def _iota(shape, dim):
    return lax.broadcasted_iota(jnp.int32, shape, dim)


def _matmul(a, b, *, ta=False, tb=False, a_blk=False, b_blk=False, o_blk=False, tm=None, tn=None, tk=None,
            out_dtype, name, after=None):
    if a_blk:
        nb, r, cb = a.shape
        if ta:
            k_dim, m_dim, tm = r, nb * cb, cb
        else:
            m_dim, k_dim, tk = r, nb * cb, cb
    else:
        k_dim, m_dim = a.shape if ta else a.shape[::-1]
    if b_blk:
        nb, r, cb = b.shape
        if tb:
            n_dim, tk = r, cb
            assert nb * cb == k_dim
        else:
            n_dim, tn = nb * cb, cb
            assert r == k_dim
    else:
        n_dim = b.shape[0] if tb else b.shape[1]
    tm = tm or _pick(m_dim, (1024, 768, 512, 256, 128))
    tn = tn or _pick(n_dim, (1024, 768, 512, 256, 128))
    tk = tk or (k_dim if (k_dim <= 2816 and not ta) else _pick(k_dim, (2816, 2304, 1024, 512, 256)))
    nk = k_dim // tk
    dims = ((((0,) if ta else (1,)), ((1,) if tb else (0,))), ((), ()))

    def body(a_ref, b_ref, *rest):
        o_ref, acc_ref = rest[-2:]
        part = lax.dot_general(a_ref[...].astype(BF16), b_ref[...].astype(BF16), dims,
                               preferred_element_type=F32)
        if nk == 1:
            o_ref[...] = part.astype(o_ref.dtype)
            return
        k = pl.program_id(2)

        @pl.when(k == 0)
        def _():
            acc_ref[...] = part

        @pl.when(jnp.logical_and(k > 0, k < nk - 1))
        def _():
            acc_ref[...] += part

        @pl.when(k == nk - 1)
        def _():
            o_ref[...] = (acc_ref[...] + part).astype(o_ref.dtype)

    if a_blk:
        a_spec = (pl.BlockSpec((None, tk, tm), lambda i, j, k: (i, k, 0)) if ta
                  else pl.BlockSpec((None, tm, tk), lambda i, j, k: (k, i, 0)))
    else:
        a_spec = (pl.BlockSpec((tk, tm), lambda i, j, k: (k, i)) if ta
                  else pl.BlockSpec((tm, tk), lambda i, j, k: (i, k)))
    if b_blk:
        b_spec = (pl.BlockSpec((None, tn, tk), lambda i, j, k: (k, j, 0)) if tb
                  else pl.BlockSpec((None, tk, tn), lambda i, j, k: (j, k, 0)))
    else:
        b_spec = (pl.BlockSpec((tn, tk), lambda i, j, k: (j, k)) if tb
                  else pl.BlockSpec((tk, tn), lambda i, j, k: (k, j)))
    if o_blk:
        o_spec = pl.BlockSpec((None, tm, tn), lambda i, j, k: (j, i, 0))
        o_shape = SDS((n_dim // tn, m_dim, tn), out_dtype)
    else:
        o_spec = pl.BlockSpec((tm, tn), lambda i, j, k: (i, j))
        o_shape = SDS((m_dim, n_dim), out_dtype)
    return pl.pallas_call(
        body, grid=(m_dim // tm, n_dim // tn, nk),
        in_specs=[a_spec, b_spec] + ([] if after is None else [pl.BlockSpec(memory_space=pl.ANY)]),
        out_specs=o_spec,
        out_shape=o_shape,
        scratch_shapes=[pltpu.VMEM((tm, tn), F32)],
        compiler_params=_params(("parallel", "parallel", "arbitrary")),
        name=name,
    )(a, b, *([] if after is None else [after]))


def _matmul_residual(a, b, x, gate, coef, *, a_blk=False, norm=None, name, after=None):
    if a_blk:
        nb, m_dim, tk = a.shape
        nk = nb
        a_spec = pl.BlockSpec((None, 512, tk), lambda i, k: (k, i, 0))
    else:
        m_dim, tk = a.shape
        nk = 1
        a_spec = pl.BlockSpec((512, tk), lambda i, k: (i, 0))
    tm = 512
    extra = [] if after is None else [after]
    vecs = [gate] + (list(norm) if norm else [])

    def body(a_ref, b_ref, x_ref, gate_ref, *rest):
        vec_refs = rest[:len(vecs) - 1]
        outs = rest[len(vecs) - 1 + len(extra):]
        acc_ref = outs[-1]
        k = pl.program_id(1)
        part = _dg(a_ref[...], b_ref[...], NN)
        if nk > 1:
            @pl.when(k == 0)
            def _():
                acc_ref[...] = part

            @pl.when(jnp.logical_and(k > 0, k < nk - 1))
            def _():
                acc_ref[...] += part

        @pl.when(k == nk - 1)
        def _():
            y = part if nk == 1 else acc_ref[...] + part
            xn = x_ref[...] + (coef * gate_ref[...]) * y
            outs[0][...] = xn
            outs[1][...] = y.astype(outs[1].dtype)
            if norm:
                g_ref, sh_ref, sc_ref = vec_refs
                r = lax.rsqrt(jnp.mean(xn * xn, axis=-1, keepdims=True) + RMS_EPS)
                outs[2][...] = (((xn * r) * g_ref[...]) * (1.0 + sc_ref[...]) + sh_ref[...]).astype(outs[2].dtype)

    row = pl.BlockSpec((tm, D), lambda i, k: (i, 0))
    vec = pl.BlockSpec((1, D), lambda i, k: (0, 0))
    return pl.pallas_call(
        body, grid=(m_dim // tm, nk),
        in_specs=[a_spec, pl.BlockSpec((tk, D), lambda i, k: (k, 0)), row] + [vec] * len(vecs)
        + [pl.BlockSpec(memory_space=pl.ANY)] * len(extra),
        out_specs=[row] * (3 if norm else 2),
        out_shape=[SDS((m_dim, D), F32), SDS((m_dim, D), BF16)] + ([SDS((m_dim, D), BF16)] if norm else []),
        scratch_shapes=[pltpu.VMEM((tm, D), F32)],
        compiler_params=_params(("parallel", "arbitrary")), name=name,
    )(a, b, x, *vecs, *extra)


def _row(width, col=0):
    return pl.BlockSpec((TR, width), lambda i: (i, col))


def _vec(width):
    return pl.BlockSpec((1, width), lambda i: (0, 0))


def _norm_mod_fwd(x, g, shift, scale, name, after=None):
    t = x.shape[0]
    extra = [] if after is None else [after]

    def body(x_ref, g_ref, sh_ref, sc_ref, *rest):
        o_ref = rest[-1]
        xv = x_ref[...]
        r = lax.rsqrt(jnp.mean(xv * xv, axis=-1, keepdims=True) + RMS_EPS)
        o_ref[...] = (((xv * r) * g_ref[...]) * (1.0 + sc_ref[...]) + sh_ref[...]).astype(o_ref.dtype)

    return pl.pallas_call(
        body, grid=(t // TR,),
        in_specs=[_row(D), _vec(D), _vec(D), _vec(D)] + [pl.BlockSpec(memory_space=pl.ANY)] * len(extra),
        out_specs=_row(D),
        out_shape=SDS((t, D), BF16), compiler_params=_params(("parallel",)), name=name,
    )(x, g, shift, scale, *extra)


def _residual_branch_bwd(dxv, y_ref, gate_ref, coef, dy_ref, dgate_ref):
    dy_ref[...] = ((coef * gate_ref[...]) * dxv).astype(dy_ref.dtype)
    dgate_ref[...] += jnp.sum((coef * dxv) * y_ref[...], axis=0, keepdims=True)


def _norm_mod_bwd(x, g, scale, dh, dx_in, name, below=None):
    t = x.shape[0]
    lower = [] if below is None else list(below[:2])

    def body(x_ref, g_ref, sc_ref, dh_ref, dxi_ref, *rest):
        dx_ref, dsh_ref, dsc_ref, dg_ref = rest[len(lower):len(lower) + 4]

        @pl.when(pl.program_id(0) == 0)
        def _():
            for ref in rest[len(lower) + 1:]:
                if ref.shape[0] == 1:
                    ref[...] = jnp.zeros_like(ref)

        xv = x_ref[...]
        gv = g_ref[...]
        dh = dh_ref[...]
        r = lax.rsqrt(jnp.mean(xv * xv, axis=-1, keepdims=True) + RMS_EPS)
        n = xv * r
        dsh_ref[...] += jnp.sum(dh, axis=0, keepdims=True)
        dsc_ref[...] += jnp.sum(dh * (n * gv), axis=0, keepdims=True)
        tt = dh * (1.0 + sc_ref[...])
        dg_ref[...] += jnp.sum(tt * n, axis=0, keepdims=True)
        dn = tt * gv
        dxv = dxi_ref[...] + r * (dn - n * jnp.mean(dn * n, axis=-1, keepdims=True))
        dx_ref[...] = dxv
        if below is not None:
            _residual_branch_bwd(dxv, rest[0], rest[1], below[2], rest[-2], rest[-1])

    more_in = [] if below is None else [_row(D), _vec(D)]
    more_out = [] if below is None else [_row(D), _vec(D)]
    more_shape = [] if below is None else [SDS((t, D), BF16), SDS((1, D), F32)]
    return pl.pallas_call(
        body, grid=(t // TR,), in_specs=[_row(D), _vec(D), _vec(D), _row(D), _row(D)] + more_in,
        out_specs=[_row(D), _vec(D), _vec(D), _vec(D)] + more_out,
        out_shape=[SDS((t, D), F32), SDS((1, D), F32), SDS((1, D), F32), SDS((1, D), F32)] + more_shape,
        compiler_params=_params(("arbitrary",)), name=name,
    )(x, g, scale, dh, dx_in, *lower)


def _swiglu_up(h, w_in, name, after=None):
    t = h.shape[0]
    tm = _pick(t, (1024, 512, 256))
    half = NDEV // 2
    extra = [] if after is None else [after]

    def body(h_ref, wg_ref, wu_ref, *rest):
        u_ref, a_ref = rest[-2:]
        hv = h_ref[...]
        gate = _dg(hv, wg_ref[...], NT)
        up = _dg(hv, wu_ref[...], NT)
        u_ref[0] = gate.astype(u_ref.dtype)
        u_ref[1] = up.astype(u_ref.dtype)
        a_ref[...] = (_silu(gate) * up).astype(a_ref.dtype)

    return pl.pallas_call(
        body, grid=(t // tm, half),
        in_specs=[pl.BlockSpec((tm, D), lambda i, j: (i, 0)),
                  pl.BlockSpec((FB, D), lambda i, j: (j, 0)),
                  pl.BlockSpec((FB, D), lambda i, j: (j + half, 0))]
        + [pl.BlockSpec(memory_space=pl.ANY)] * len(extra),
        out_specs=[pl.BlockSpec((2, None, tm, FB), lambda i, j: (0, j, i, 0)),
                   pl.BlockSpec((None, tm, FB), lambda i, j: (j, i, 0))],
        out_shape=[SDS((2, half, t, FB), BF16), SDS((half, t, FB), BF16)],
        compiler_params=_params(("parallel", "parallel")), name=name,
    )(h, w_in, w_in, *extra)


def _swiglu_down_bwd(dy, w_out, u, name, after=None):
    t = dy.shape[0]
    tm = _pick(t, (1024, 512, 256))
    half = NDEV // 2
    extra = [] if after is None else [after]
    pair = pl.BlockSpec((2, None, tm, FB), lambda i, j: (0, j, i, 0))

    def body(dy_ref, w_ref, u_ref, *rest):
        o_ref = rest[-1]
        da = _dg(dy_ref[...], w_ref[...], NT)
        gate = u_ref[0].astype(F32)
        o_ref[0] = (da * u_ref[1].astype(F32) * _dsilu(gate)).astype(o_ref.dtype)
        o_ref[1] = (da * _silu(gate)).astype(o_ref.dtype)

    return pl.pallas_call(
        body, grid=(t // tm, half),
        in_specs=[pl.BlockSpec((tm, D), lambda i, j: (i, 0)), pl.BlockSpec((FB, D), lambda i, j: (j, 0)), pair]
        + [pl.BlockSpec(memory_space=pl.ANY)] * len(extra),
        out_specs=pair, out_shape=SDS((2, half, t, FB), BF16),
        compiler_params=_params(("parallel", "parallel")), name=name,
    )(dy, w_out, u, *extra)


def _final_loss(x, fg, target, below, name):
    t = x.shape[0]
    nt = t // TR

    def body(x_ref, g_ref, t_ref, y_ref, gate_ref, loss_ref, dx_ref, dg_ref, dy_ref, dgate_ref, acc_ref):
        i = pl.program_id(0)

        @pl.when(i == 0)
        def _():
            acc_ref[...] = jnp.zeros_like(acc_ref)
            dg_ref[...] = jnp.zeros_like(dg_ref)
            dgate_ref[...] = jnp.zeros_like(dgate_ref)

        xv = x_ref[...]
        gv = g_ref[...]
        r = lax.rsqrt(jnp.mean(xv * xv, axis=-1, keepdims=True) + RMS_EPS)
        n = xv * r
        err = n * gv - t_ref[...]
        acc_ref[...] += jnp.sum(err * err, axis=0, keepdims=True)
        dy = err * (1.0 / D)
        dg_ref[...] += jnp.sum(dy * n, axis=0, keepdims=True)
        dn = dy * gv
        dxv = r * (dn - n * jnp.mean(dn * n, axis=-1, keepdims=True))
        dx_ref[...] = dxv
        _residual_branch_bwd(dxv, y_ref, gate_ref, below[2], dy_ref, dgate_ref)

        @pl.when(i == nt - 1)
        def _():
            tot = jnp.sum(acc_ref[...], axis=1, keepdims=True) * (0.5 / D)
            loss_ref[...] = jnp.broadcast_to(tot, loss_ref.shape)

    return pl.pallas_call(
        body, grid=(nt,), in_specs=[_row(D), _vec(D), _row(D), _row(D), _vec(D)],
        out_specs=[_vec(128), _row(D), _vec(D), _row(D), _vec(D)],
        out_shape=[SDS((1, 128), F32), SDS((t, D), F32), SDS((1, D), F32), SDS((t, D), BF16), SDS((1, D), F32)],
        scratch_shapes=[pltpu.VMEM((1, D), F32)],
        compiler_params=_params(("arbitrary",)), name=name,
    )(x, fg, target, below[0], below[1])


def _halo_prev(width, col):
    per = TR // HALO
    return pl.BlockSpec((HALO, width), lambda i: (jnp.maximum(i * per - 1, 0), col))


def _halo_next(width, col, nt):
    per = TR // HALO
    return pl.BlockSpec((HALO, width), lambda i: (jnp.minimum((i + 1) * per, nt * per - 1), col))


def _pool_windows(ext, tile_index):
    rows = _iota((TR, PG), 0) + tile_index * TR + 1
    pooled, counts = [], []
    for gi in range(4):
        w = 2 << gi
        e = ext[:, gi * PG:(gi + 1) * PG]
        s = e
        step = 1
        while step < w:
            s = s + pltpu.roll(s, step, 0)
            step *= 2
        cnt = jnp.minimum(rows, w).astype(F32)
        pooled.append(s[HALO:] / cnt - e[HALO:])
        counts.append(cnt)
    return pooled, counts


def _pool_fwd(proj, pool_w, pool_scale, pool_proj, name):
    t = proj.shape[0]
    xcol = OFF_XP // PW

    def body(x_ref, h_ref, pw_ref, ps_ref, pp_ref, o_ref):
        i = pl.program_id(0)
        halo = jnp.where(i > 0, h_ref[...], 0.0)
        ext = jnp.concatenate([halo, x_ref[...]], axis=0)
        pooled, _ = _pool_windows(ext, i)
        mixed = [_dg(pooled[g].astype(BF16), pw_ref[g].astype(BF16), NN) for g in range(4)]
        ypre = jnp.concatenate(mixed, axis=1) * ps_ref[...]
        o_ref[...] = _dg(ypre.astype(BF16), pp_ref[...], NN)

    return pl.pallas_call(
        body, grid=(t // TR,),
        in_specs=[_row(PW, xcol), _halo_prev(PW, xcol),
                  pl.BlockSpec((4, PG, PG), lambda i: (0, 0, 0)), _vec(PW),
                  pl.BlockSpec((PW, D), lambda i: (0, 0))],
        out_specs=_row(D), out_shape=SDS((t, D), F32),
        compiler_params=_params(("parallel",)), name=name,
    )(proj, proj, pool_w, pool_scale, pool_proj)


def _pool_bwd_local(proj, pool_w, pool_scale, pool_proj, dya, name):
    t = proj.shape[0]
    xcol = OFF_XP // PW

    def body(x_ref, h_ref, pw_ref, ps_ref, pp_ref, dya_ref, dwin_ref, dpl_ref, dpw_ref, dps_ref, dpp_ref):
        i = pl.program_id(0)

        @pl.when(i == 0)
        def _():
            dpw_ref[...] = jnp.zeros_like(dpw_ref)
            dps_ref[...] = jnp.zeros_like(dps_ref)
            dpp_ref[...] = jnp.zeros_like(dpp_ref)

        halo = jnp.where(i > 0, h_ref[...], 0.0)
        ext = jnp.concatenate([halo, x_ref[...]], axis=0)
        pooled, counts = _pool_windows(ext, i)
        mixed = jnp.concatenate(
            [_dg(pooled[g].astype(BF16), pw_ref[g].astype(BF16), NN) for g in range(4)], axis=1)
        ps = ps_ref[...]
        ypre = mixed * ps
        dyab = dya_ref[...].astype(BF16)
        dypre = _dg(dyab, pp_ref[...], NT)
        dpp_ref[...] += _dg(ypre.astype(BF16), dyab, TN)
        dps_ref[...] += jnp.sum(dypre * mixed, axis=0, keepdims=True)
        dmixed = dypre * ps
        for g in range(4):
            dm = dmixed[:, g * PG:(g + 1) * PG].astype(BF16)
            dpw_ref[g] += _dg(pooled[g].astype(BF16), dm, TN)
            dpooled = _dg(dm, pw_ref[g].astype(BF16), NT)
            dwin_ref[:, g * PG:(g + 1) * PG] = dpooled / counts[g]
            dpl_ref[:, g * PG:(g + 1) * PG] = dpooled

    return pl.pallas_call(
        body, grid=(t // TR,),
        in_specs=[_row(PW, xcol), _halo_prev(PW, xcol),
                  pl.BlockSpec((4, PG, PG), lambda i: (0, 0, 0)), _vec(PW),
                  pl.BlockSpec((PW, D), lambda i: (0, 0)), _row(D)],
        out_specs=[_row(PW), _row(PW), pl.BlockSpec((4, PG, PG), lambda i: (0, 0, 0)), _vec(PW),
                   pl.BlockSpec((PW, D), lambda i: (0, 0))],
        out_shape=[SDS((t, PW), F32), SDS((t, PW), F32), SDS((4, PG, PG), F32), SDS((1, PW), F32),
                   SDS((PW, D), F32)],
        compiler_params=_params(("arbitrary",)), name=name,
    )(proj, proj, pool_w, pool_scale, pool_proj, dya)


def _pool_bwd_window(dwin, dpl, dproj, name):
    t = dwin.shape[0]
    nt = t // TR
    ext_rows = TR + HALO

    def body(dw_ref, h_ref, dp_ref, _, o_ref):
        i = pl.program_id(0)
        halo = jnp.where(i < nt - 1, h_ref[...], 0.0)
        ext = jnp.concatenate([dw_ref[...], halo], axis=0)
        for gi in range(4):
            w = 2 << gi
            s = ext[:, gi * PG:(gi + 1) * PG]
            step = 1
            while step < w:
                s = s + pltpu.roll(s, ext_rows - step, 0)
                step *= 2
            o_ref[:, gi * PG:(gi + 1) * PG] = (s[:TR] - dp_ref[:, gi * PG:(gi + 1) * PG]).astype(o_ref.dtype)

    return pl.pallas_call(
        body, grid=(nt,),
        in_specs=[_row(PW), _halo_next(PW, 0, nt), _row(PW), pl.BlockSpec(memory_space=pl.ANY)],
        out_specs=_into(PW, OFF_XP), out_shape=SDS(dproj.shape, dproj.dtype), input_output_aliases={3: 0},
        compiler_params=_params(("parallel",)), name=name,
    )(dwin, dwin, dpl, dproj)


def _conv_group(ext, cw_ref, cols):
    acc = cw_ref[3:4, cols] * ext
    for j in range(3):
        acc = acc + cw_ref[j:j + 1, cols] * pltpu.roll(ext, 3 - j, 0)
    return acc[HALO:]


def _gate_terms(raw, al, dt):
    beta = _sigmoid(raw)
    xg = raw + dt
    sp = jnp.maximum(xg, 0.0) + jnp.log(1.0 + jnp.exp(-jnp.abs(xg)))
    g = -jnp.exp(al) * sp
    return beta, g, _sigmoid(xg)


def _dn_pre_fwd(proj, conv_w, al_row, dt_row, name):
    t = proj.shape[0]

    def body(x_ref, h_ref, cw_ref, ba_ref, al_ref, dt_ref, q_ref, k_ref, v_ref, bg_ref):
        i = pl.program_id(0)
        keep = i > 0
        for grp in range(24):
            cols = slice(grp * HD, (grp + 1) * HD)
            ext = jnp.concatenate([jnp.where(keep, h_ref[:, cols], 0.0), x_ref[:, cols]], axis=0)
            s = _silu(_conv_group(ext, cw_ref, cols))
            seg, head = divmod(grp, NH)
            hc = slice(head * HD, (head + 1) * HD)
            if seg == 0:
                q_ref[:, hc] = s * lax.rsqrt(jnp.sum(s * s, axis=-1, keepdims=True) + L2_EPS) * (HD ** -0.5)
            elif seg == 1:
                k_ref[:, hc] = s * lax.rsqrt(jnp.sum(s * s, axis=-1, keepdims=True) + L2_EPS)
            else:
                v_ref[:, hc] = s
        lane = _iota((TR, 128), 1)
        rowc = _iota((TR, 128), 0) % CH
        beta, g, _ = _gate_terms(ba_ref[...], al_ref[...], dt_ref[...])
        step = 1
        while step < CH:
            g = g + jnp.where(rowc >= step, pltpu.roll(g, step, 0), 0.0)
            step *= 2
        bg_ref[...] = jnp.where(lane < NH, beta, jnp.where(lane < 2 * NH, g, 0.0))

    return pl.pallas_call(
        body, grid=(t // TR,),
        in_specs=[_row(3 * D, 0), _halo_prev(3 * D, 0), pl.BlockSpec((4, 3 * D), lambda i: (0, 0)),
                  _row(128, OFF_BA // 128), _vec(128), _vec(128)],
        out_specs=[_row(D), _row(D), _row(D), _row(128)],
        out_shape=[SDS((t, D), F32), SDS((t, D), F32), SDS((t, D), F32), SDS((t, 128), F32)],
        compiler_params=_params(("parallel",)), name=name,
    )(proj, proj, conv_w, proj, al_row, dt_row)


def _dn_pre_bwd_act(proj, conv_w, al_row, dt_row, dq, dk, dv, dbg, dproj, name):
    t = proj.shape[0]

    def body(x_ref, h_ref, cw_ref, ba_ref, al_ref, dt_ref, dq_ref, dk_ref, dv_ref, dbg_ref, _,
             dc_ref, draw_ref, dal_ref, ddt_ref):
        i = pl.program_id(0)

        @pl.when(i == 0)
        def _():
            dal_ref[...] = jnp.zeros_like(dal_ref)
            ddt_ref[...] = jnp.zeros_like(ddt_ref)

        keep = i > 0
        for grp in range(24):
            cols = slice(grp * HD, (grp + 1) * HD)
            ext = jnp.concatenate([jnp.where(keep, h_ref[:, cols], 0.0), x_ref[:, cols]], axis=0)
            cv = _conv_group(ext, cw_ref, cols)
            seg, head = divmod(grp, NH)
            hc = slice(head * HD, (head + 1) * HD)
            if seg == 2:
                ds = dv_ref[:, hc]
            else:
                s = _silu(cv)
                r = lax.rsqrt(jnp.sum(s * s, axis=-1, keepdims=True) + L2_EPS)
                dy = dq_ref[:, hc] if seg == 0 else dk_ref[:, hc]
                c = (HD ** -0.5) if seg == 0 else 1.0
                ds = (c * r) * (dy - s * ((r * r) * jnp.sum(dy * s, axis=-1, keepdims=True)))
            dc_ref[:, cols] = ds * _dsilu(cv)
        lane = _iota((TR, 128), 1)
        rowc = _iota((TR, 128), 0) % CH
        isb = lane < NH
        isg = jnp.logical_and(lane >= NH, lane < 2 * NH)
        beta, g, sg = _gate_terms(ba_ref[...], al_ref[...], dt_ref[...])
        dbgv = dbg_ref[...]
        dg = dbgv
        step = 1
        while step < CH:
            dg = dg + jnp.where(rowc < CH - step, pltpu.roll(dg, TR - step, 0), 0.0)
            step *= 2
        da_raw = dg * (-jnp.exp(al_ref[...])) * sg
        draw = jnp.where(isb, dbgv * beta * (1.0 - beta), jnp.where(isg, da_raw, 0.0))
        draw_ref[:, :128] = draw.astype(draw_ref.dtype)
        draw_ref[:, 128:] = jnp.zeros((TR, MIXP - OFF_BA - 128), draw_ref.dtype)
        dal_ref[...] += jnp.sum(jnp.where(isg, dg * g, 0.0), axis=0, keepdims=True)
        ddt_ref[...] += jnp.sum(jnp.where(isg, da_raw, 0.0), axis=0, keepdims=True)

    return pl.pallas_call(
        body, grid=(t // TR,),
        in_specs=[_row(3 * D, 0), _halo_prev(3 * D, 0), pl.BlockSpec((4, 3 * D), lambda i: (0, 0)),
                  _row(128, OFF_BA // 128), _vec(128), _vec(128), _row(D), _row(D), _row(D), _row(128),
                  pl.BlockSpec(memory_space=pl.ANY)],
        out_specs=[_row(3 * D), _into(MIXP - OFF_BA, OFF_BA), _vec(128), _vec(128)],
        out_shape=[SDS((t, 3 * D), F32), SDS(dproj.shape, dproj.dtype), SDS((1, 128), F32), SDS((1, 128), F32)],
        input_output_aliases={10: 1},
        compiler_params=_params(("arbitrary",)), name=name,
    )(proj, proj, conv_w, proj, al_row, dt_row, dq, dk, dv, dbg, dproj)


def _dn_pre_bwd_conv(proj, conv_w, dconv, dproj, name):
    t = proj.shape[0]
    nt = t // TR
    ext_rows = TR + HALO

    def body(x_ref, h_ref, cw_ref, dc_ref, dn_ref, _, dx_ref, dcw_ref):
        i = pl.program_id(0)

        @pl.when(i == 0)
        def _():
            dcw_ref[...] = jnp.zeros_like(dcw_ref)

        keep_prev = i > 0
        keep_next = i < nt - 1
        for grp in range(24):
            cols = slice(grp * HD, (grp + 1) * HD)
            dct = dc_ref[:, cols]
            dext = jnp.concatenate([dct, jnp.where(keep_next, dn_ref[:, cols], 0.0)], axis=0)
            acc = cw_ref[3:4, cols] * dext
            for j in range(3):
                acc = acc + cw_ref[j:j + 1, cols] * pltpu.roll(dext, ext_rows - (3 - j), 0)
            dx_ref[:, cols] = acc[:TR].astype(dx_ref.dtype)
            xext = jnp.concatenate([jnp.where(keep_prev, h_ref[:, cols], 0.0), x_ref[:, cols]], axis=0)
            for j in range(4):
                xs = xext if j == 3 else pltpu.roll(xext, 3 - j, 0)
                dcw_ref[j:j + 1, cols] += jnp.sum(xs[HALO:] * dct, axis=0, keepdims=True)

    return pl.pallas_call(
        body, grid=(nt,),
        in_specs=[_row(3 * D, 0), _halo_prev(3 * D, 0), pl.BlockSpec((4, 3 * D), lambda i: (0, 0)),
                  _row(3 * D), _halo_next(3 * D, 0, nt), pl.BlockSpec(memory_space=pl.ANY)],
        out_specs=[_into(3 * D, OFF_Q), pl.BlockSpec((4, 3 * D), lambda i: (0, 0))],
        out_shape=[SDS(dproj.shape, dproj.dtype), SDS((4, 3 * D), F32)],
        input_output_aliases={5: 0},
        compiler_params=_params(("arbitrary",)), name=name,
    )(proj, proj, conv_w, dconv, dconv, dproj)


def _dn_post_fwd(o, proj, gn, name):
    t = o.shape[0]

    def body(o_ref, z_ref, g_ref, out_ref):
        gv = g_ref[...]
        for h in range(NH):
            hc = slice(h * HD, (h + 1) * HD)
            ov = o_ref[:, hc]
            r = lax.rsqrt(jnp.mean(ov * ov, axis=-1, keepdims=True) + RMS_EPS)
            out_ref[:, hc] = (((ov * r) * gv) * _silu(z_ref[:, hc])).astype(out_ref.dtype)

    return pl.pallas_call(
        body, grid=(t // TR,), in_specs=[_row(D), _row(D, OFF_Z // D), _vec(HD)], out_specs=_row(D),
        out_shape=SDS((t, D), BF16), compiler_params=_params(("parallel",)), name=name,
    )(o, proj, gn)


def _dn_post_bwd(o, proj, gn, dob, dproj, name):
    t = o.shape[0]

    def body(o_ref, z_ref, g_ref, d_ref, _, do_ref, dz_ref, dg_ref):
        @pl.when(pl.program_id(0) == 0)
        def _():
            dg_ref[...] = jnp.zeros_like(dg_ref)

        gv = g_ref[...]
        acc = jnp.zeros((1, HD), F32)
        for h in range(NH):
            hc = slice(h * HD, (h + 1) * HD)
            ov = o_ref[:, hc]
            zv = z_ref[:, hc]
            dv = d_ref[:, hc]
            r = lax.rsqrt(jnp.mean(ov * ov, axis=-1, keepdims=True) + RMS_EPS)
            n = ov * r
            dz_ref[:, hc] = (dv * (n * gv) * _dsilu(zv)).astype(dz_ref.dtype)
            dng = dv * _silu(zv)
            acc = acc + jnp.sum(dng * n, axis=0, keepdims=True)
            dn = dng * gv
            do_ref[:, hc] = r * (dn - n * jnp.mean(dn * n, axis=-1, keepdims=True))
        dg_ref[...] += acc

    return pl.pallas_call(
        body, grid=(t // TR,),
        in_specs=[_row(D), _row(D, OFF_Z // D), _vec(HD), _row(D), pl.BlockSpec(memory_space=pl.ANY)],
        out_specs=[_row(D), _into(D, OFF_Z), _vec(HD)],
        out_shape=[SDS((t, D), F32), SDS(dproj.shape, dproj.dtype), SDS((1, HD), F32)],
        input_output_aliases={4: 1},
        compiler_params=_params(("arbitrary",)), name=name,
    )(o, proj, gn, dob, dproj)


def _merge_fwd(ya, yb, proj, name):
    t = ya.shape[0]

    def body(a_ref, b_ref, gp_ref, gd_ref, o_ref):
        o_ref[...] = (_sigmoid(gp_ref[...]) * a_ref[...] + _sigmoid(gd_ref[...]) * b_ref[...]).astype(o_ref.dtype)

    return pl.pallas_call(
        body, grid=(t // TR,), in_specs=[_row(D), _row(D), _row(D, OFF_GP // D), _row(D, OFF_GD // D)],
        out_specs=_row(D), out_shape=SDS((t, D), BF16),
        compiler_params=_params(("parallel",)), name=name,
    )(ya, yb, proj, proj)


def _into(width, offset):
    assert offset % width == 0
    return pl.BlockSpec((TR, width), lambda i: (i, offset // width))


def _merge_bwd(dm, ya, yb, proj, dproj, name):
    t = ya.shape[0]

    def body(d_ref, a_ref, b_ref, gp_ref, gd_ref, _, da_ref, db_ref, dg_ref):
        dv = d_ref[...]
        sp = _sigmoid(gp_ref[...])
        sd = _sigmoid(gd_ref[...])
        da_ref[...] = dv * sp
        db_ref[...] = (dv * sd).astype(db_ref.dtype)
        dg_ref[:, :D] = (dv * a_ref[...] * sp * (1.0 - sp)).astype(dg_ref.dtype)
        dg_ref[:, D:] = (dv * b_ref[...] * sd * (1.0 - sd)).astype(dg_ref.dtype)

    return pl.pallas_call(
        body, grid=(t // TR,),
        in_specs=[_row(D), _row(D), _row(D), _row(D, OFF_GP // D), _row(D, OFF_GD // D),
                  pl.BlockSpec(memory_space=pl.ANY)],
        out_specs=[_row(D), _row(D), _into(2 * D, OFF_GP)],
        out_shape=[SDS((t, D), F32), SDS((t, D), BF16), SDS(dproj.shape, dproj.dtype)],
        input_output_aliases={5: 2},
        compiler_params=_params(("parallel",)), name=name,
    )(dm, ya, yb, proj, proj, dproj)


def _split2(x):
    hi = x.astype(BF16)
    return hi, (x - hi.astype(F32)).astype(BF16)


def _dot3(a, b, dims):
    ah, al = _split2(a)
    bh, bl = _split2(b)
    return _dg(ah, bh, dims) + (_dg(ah, bl, dims) + _dg(al, bh, dims))


def _neumann_inverses(mats):
    ri = _iota((CH, CH), 0)
    ci = _iota((CH, CH), 1)
    eye = jnp.where(ri == ci, 1.0, 0.0).astype(F32)
    xs = [-a for a in mats]
    ps = [eye + x for x in xs]
    for _ in range(5):
        xs = [_dot3(x, x, NN) for x in xs]
        ps = [p + _dot3(p, x, NN) for p, x in zip(ps, xs)]
    return ps


def _solve_with(inv):
    @jax.custom_vjp
    def solve(a, rhs):
        return _dot3(inv, rhs, NN)

    def fwd(a, rhs):
        sol = _dot3(inv, rhs, NN)
        return sol, sol

    def bwd(sol, d):
        drhs = _dot3(inv, d, TN)
        return -_dot3(drhs, sol, NT), drhs

    solve.defvjp(fwd, bwd)
    return solve


@jax.custom_vjp
def _rows_to_lanes(g64):
    ri = _iota((CH, CH), 0)
    ci = _iota((CH, CH), 1)
    diag = jnp.where(ri == ci, g64, 0.0)
    ones = jnp.ones((CH, CH), BF16)
    hi = diag.astype(BF16)
    rem = diag - hi.astype(F32)
    mid = rem.astype(BF16)
    lo = (rem - mid.astype(F32)).astype(BF16)
    return _dg(ones, hi, NN) + (_dg(ones, mid, NN) + _dg(ones, lo, NN))


def _rows_to_lanes_bwd(_, d):
    ri = _iota((CH, CH), 0)
    ci = _iota((CH, CH), 1)
    return (jnp.where(ri == ci, jnp.broadcast_to(jnp.sum(d, axis=0, keepdims=True), (CH, CH)), 0.0),)


_rows_to_lanes.defvjp(lambda g64: (_rows_to_lanes(g64), None), _rows_to_lanes_bwd)


def _chunk_local(solve_all, q, k, v, g128, g64, gl128, b128, b64):
    ri = _iota((CH, CH), 0)
    ci = _iota((CH, CH), 1)
    causal = ri >= ci
    strict = ri > ci
    gj = [_rows_to_lanes(g) for g in g64]
    decay = [jnp.where(causal, jnp.exp(jnp.where(causal, g - t, 0.0)), 0.0) for g, t in zip(g64, gj)]
    kk = [_nt(x, x) for x in k]
    a = [jnp.where(strict, b * m * dc, 0.0) for b, m, dc in zip(b64, kk, decay)]
    eg = [jnp.exp(g) for g in g128]
    rhs = [jnp.concatenate([b * x, (b * e) * y], axis=1) for b, x, e, y in zip(b128, v, eg, k)]
    sol = solve_all(a, rhs)
    qk = [jnp.where(causal, _nt(x, y) * dc, 0.0) for x, y, dc in zip(q, k, decay)]
    return ([s[:, :HD] for s in sol], [s[:, HD:] for s in sol], qk, [x * e for x, e in zip(q, eg)],
            [x * jnp.exp(gl - g) for x, gl, g in zip(k, gl128, g128)], [jnp.exp(gl) for gl in gl128])


def _all_head_gates(bgv):
    return tuple(list(z) for z in zip(*[_head_gates(bgv, h) for h in range(NH)]))


def _head_gates(bgv, h):
    lane = _iota((CH, 128), 1)
    row = _iota((CH, 128), 0)
    bcol = jnp.sum(jnp.where(lane == h, bgv, 0.0), axis=1, keepdims=True)
    gcol = jnp.sum(jnp.where(lane == NH + h, bgv, 0.0), axis=1, keepdims=True)
    g128 = jnp.broadcast_to(gcol, (CH, 128))
    gl128 = jnp.broadcast_to(jnp.sum(jnp.where(row == CH - 1, g128, 0.0), axis=0, keepdims=True), (CH, 128))
    return (g128, jnp.broadcast_to(gcol, (CH, CH)), gl128,
            jnp.broadcast_to(bcol, (CH, 128)), jnp.broadcast_to(bcol, (CH, CH)))


def _chunk_specs():
    row = pl.BlockSpec((CH, D), lambda i: (i, 0))
    small = pl.BlockSpec((CH, 128), lambda i: (i, 0))
    qk = pl.BlockSpec((NH, CH, CH), lambda i: (i, 0, 0))
    eg = pl.BlockSpec((1, NH, 128), lambda i: (i, 0, 0))
    return row, small, qk, eg


def _dn_local_fwd(q, k, v, bg, name):
    t = q.shape[0]
    n = t // CH

    def body(q_ref, k_ref, v_ref, bg_ref, u_ref, w_ref, qk_ref, qd_ref, kd_ref, eg_ref, inv_ref):
        cols = [slice(h * HD, (h + 1) * HD) for h in range(NH)]

        def solve_all(mats, rhs):
            invs = _neumann_inverses(mats)
            for h in range(NH):
                inv_ref[h] = invs[h]
            return [_dot3(m, r, NN) for m, r in zip(invs, rhs)]

        u, w, qk, qd, kd, egl = _chunk_local(
            solve_all, [q_ref[:, c] for c in cols], [k_ref[:, c] for c in cols], [v_ref[:, c] for c in cols],
            *_all_head_gates(bg_ref[...]))
        for h, hc in enumerate(cols):
            u_ref[:, hc] = u[h]
            w_ref[:, hc] = w[h].astype(w_ref.dtype)
            qd_ref[:, hc] = qd[h].astype(qd_ref.dtype)
            kd_ref[:, hc] = kd[h].astype(kd_ref.dtype)
            qk_ref[h] = qk[h].astype(qk_ref.dtype)
            eg_ref[0, h:h + 1, :] = egl[h][0:1, :]

    row, small, qkb, egb = _chunk_specs()
    return pl.pallas_call(
        body, grid=(n,), in_specs=[row, row, row, small], out_specs=[row, row, qkb, row, row, egb, qkb],
        out_shape=[SDS((t, D), F32), SDS((t, D), BF16), SDS((n * NH, CH, CH), BF16), SDS((t, D), BF16),
                   SDS((t, D), BF16), SDS((n, NH, 128), F32), SDS((n * NH, CH, CH), F32)],
        compiler_params=_params(("parallel",)), name=name,
    )(q, k, v, bg)


def _dn_local_bwd(q, k, v, bg, inv, du, dw, dqk, dqd, dkd, deg, name):
    t = q.shape[0]
    n = t // CH

    def body(q_ref, k_ref, v_ref, bg_ref, inv_ref, du_ref, dw_ref, dqk_ref, dqd_ref, dkd_ref, deg_ref,
             dq_ref, dk_ref, dv_ref, dbg_ref):
        bgv = bg_ref[...]
        lane = _iota((CH, 128), 1)
        row = _iota((CH, 128), 0)
        first = jnp.where(row == 0, 1.0, 0.0)
        acc = jnp.zeros((CH, 128), F32)
        cols = [slice(h * HD, (h + 1) * HD) for h in range(NH)]
        solves = [_solve_with(inv_ref[h]) for h in range(NH)]

        def solve_all(mats, rhs):
            return [f(m, r) for f, m, r in zip(solves, mats, rhs)]

        _, vjp = jax.vjp(functools.partial(_chunk_local, solve_all),
                         [q_ref[:, c] for c in cols], [k_ref[:, c] for c in cols], [v_ref[:, c] for c in cols],
                         *_all_head_gates(bgv))
        cts = ([du_ref[:, c].astype(F32) for c in cols], [dw_ref[:, c].astype(F32) for c in cols],
               [dqk_ref[h] for h in range(NH)],
               [dqd_ref[:, c].astype(F32) for c in cols], [dkd_ref[:, c].astype(F32) for c in cols],
               [jnp.broadcast_to(deg_ref[0, h:h + 1, :], (CH, 128)) * first for h in range(NH)])
        dq, dk, dv, dg128, dg64, dgl, db128, db64 = vjp(cts)
        for h, hc in enumerate(cols):
            dq_ref[:, hc] = dq[h]
            dk_ref[:, hc] = dk[h]
            dv_ref[:, hc] = dv[h]
            dg = jnp.sum(dg128[h], axis=1, keepdims=True) + jnp.sum(dg64[h], axis=1, keepdims=True)
            tot = jnp.sum(jnp.sum(dgl[h], axis=0, keepdims=True), axis=1, keepdims=True)
            dg = dg + jnp.where(row[:, 0:1] == CH - 1, tot, 0.0)
            db = jnp.sum(db128[h], axis=1, keepdims=True) + jnp.sum(db64[h], axis=1, keepdims=True)
            acc = acc + jnp.where(lane == h, db, 0.0) + jnp.where(lane == NH + h, dg, 0.0)
        dbg_ref[...] = acc

    row, small, qkb, egb = _chunk_specs()
    return pl.pallas_call(
        body, grid=(n,), in_specs=[row, row, row, small, qkb, row, row, qkb, row, row, egb],
        out_specs=[row, row, row, small],
        out_shape=[SDS((t, D), F32)] * 3 + [SDS((t, 128), F32)],
        compiler_params=_params(("parallel",)), name=name,
    )(q, k, v, bg, inv, du, dw, dqk, dqd, dkd, deg)


def _state_step(s, u, w, qk, qd, kd, egl):
    ws = [_nn(a, b) for a, b in zip(w, s)]
    v_new = [a - b for a, b in zip(u, ws)]
    qs = [_nn(a, b) for a, b in zip(qd, s)]
    intra = [_nn(a, b) for a, b in zip(qk, v_new)]
    upd = [_tn(a, b) for a, b in zip(kd, v_new)]
    return [a * e + b for a, e, b in zip(s, egl, upd)], [a + b for a, b in zip(qs, intra)]


def _dn_scan_fwd(u, w, qk, qd, kd, eg, name):
    t = u.shape[0]
    n = t // CH
    g = SCAN_CHUNKS

    def body(u_ref, w_ref, qk_ref, qd_ref, kd_ref, eg_ref, o_ref, save_ref, s_ref):
        @pl.when(pl.program_id(0) == 0)
        def _():
            s_ref[...] = jnp.zeros_like(s_ref)

        cols = [slice(h * HD, (h + 1) * HD) for h in range(NH)]
        s = [s_ref[h] for h in range(NH)]
        for c in range(g):
            rows = slice(c * CH, (c + 1) * CH)
            for h in range(NH):
                save_ref[c, h] = s[h].astype(save_ref.dtype)
            s, o = _state_step(
                s, [u_ref[rows, hc] for hc in cols], [w_ref[rows, hc].astype(F32) for hc in cols],
                [qk_ref[c * NH + h].astype(F32) for h in range(NH)], [qd_ref[rows, hc].astype(F32) for hc in cols],
                [kd_ref[rows, hc].astype(F32) for hc in cols], [eg_ref[c, h:h + 1, :] for h in range(NH)])
            for h, hc in enumerate(cols):
                o_ref[rows, hc] = o[h]
        for h in range(NH):
            s_ref[h] = s[h]

    row = pl.BlockSpec((g * CH, D), lambda i: (i, 0))
    qkb = pl.BlockSpec((g * NH, CH, CH), lambda i: (i, 0, 0))
    egb = pl.BlockSpec((g, NH, 128), lambda i: (i, 0, 0))
    return pl.pallas_call(
        body, grid=(n // g,), in_specs=[row, row, qkb, row, row, egb],
        out_specs=[row, pl.BlockSpec((g, NH, HD, HD), lambda i: (i, 0, 0, 0))],
        out_shape=[SDS((t, D), F32), SDS((n, NH, HD, HD), BF16)],
        scratch_shapes=[pltpu.VMEM((NH, HD, HD), F32)],
        compiler_params=_params(("arbitrary",)), name=name,
    )(u, w, qk, qd, kd, eg)


def _dn_scan_bwd(u, w, qk, qd, kd, eg, saved, do, name):
    t = u.shape[0]
    n = t // CH
    g = SCAN_CHUNKS
    last = n // g - 1

    def body(u_ref, w_ref, qk_ref, qd_ref, kd_ref, eg_ref, sv_ref, do_ref,
             du_ref, dw_ref, dqk_ref, dqd_ref, dkd_ref, deg_ref, ds_ref):
        @pl.when(pl.program_id(0) == 0)
        def _():
            ds_ref[...] = jnp.zeros_like(ds_ref)

        cols = [slice(h * HD, (h + 1) * HD) for h in range(NH)]
        ds = [ds_ref[h] for h in range(NH)]
        for c in reversed(range(g)):
            rows = slice(c * CH, (c + 1) * CH)
            _, vjp = jax.vjp(
                _state_step, [sv_ref[c, h].astype(F32) for h in range(NH)], [u_ref[rows, hc] for hc in cols],
                [w_ref[rows, hc].astype(F32) for hc in cols], [qk_ref[c * NH + h].astype(F32) for h in range(NH)],
                [qd_ref[rows, hc].astype(F32) for hc in cols], [kd_ref[rows, hc].astype(F32) for hc in cols],
                [eg_ref[c, h:h + 1, :] for h in range(NH)])
            ds, du, dw, dqk, dqd, dkd, deg = vjp((ds, [do_ref[rows, hc] for hc in cols]))
            for h, hc in enumerate(cols):
                du_ref[rows, hc] = du[h].astype(du_ref.dtype)
                dw_ref[rows, hc] = dw[h].astype(dw_ref.dtype)
                dqk_ref[c * NH + h] = dqk[h]
                dqd_ref[rows, hc] = dqd[h].astype(dqd_ref.dtype)
                dkd_ref[rows, hc] = dkd[h].astype(dkd_ref.dtype)
                deg_ref[c, h:h + 1, :] = deg[h]
        for h in range(NH):
            ds_ref[h] = ds[h]

    row = pl.BlockSpec((g * CH, D), lambda i: (last - i, 0))
    qkb = pl.BlockSpec((g * NH, CH, CH), lambda i: (last - i, 0, 0))
    egb = pl.BlockSpec((g, NH, 128), lambda i: (last - i, 0, 0))
    return pl.pallas_call(
        body, grid=(n // g,),
        in_specs=[row, row, qkb, row, row, egb,
                  pl.BlockSpec((g, NH, HD, HD), lambda i: (last - i, 0, 0, 0)), row],
        out_specs=[row, row, qkb, row, row, egb],
        out_shape=[SDS((t, D), BF16), SDS((t, D), BF16), SDS((n * NH, CH, CH), F32), SDS((t, D), BF16),
                   SDS((t, D), BF16), SDS((n, NH, 128), F32)],
        scratch_shapes=[pltpu.VMEM((NH, HD, HD), F32)],
        compiler_params=_params(("arbitrary",)), name=name,
    )(u, w, qk, qd, kd, eg, saved, do)


def _ada_fwd(c_all, ada_w, ada_b, name):
    ncol = ada_w.shape[1]

    def body(c_ref, w_ref, b_ref, o_ref):
        o_ref[...] = _dg(_silu(c_ref[...]), w_ref[...], NN, HI) + b_ref[...]

    return pl.pallas_call(body, out_shape=SDS((NDEV, ncol), F32),
                          compiler_params=pltpu.CompilerParams(vmem_limit_bytes=VMEM_LIMIT), name=name,
                          )(c_all, ada_w, ada_b)


def _ada_bwd(c_all_t, dmod, name):
    ncol = dmod.shape[1]

    def body(c_ref, d_ref, o_ref):
        sc = _silu(c_ref[...])
        acc = sc[:, 0:1] * d_ref[0:1, :]
        for b in range(1, NDEV):
            acc = acc + sc[:, b:b + 1] * d_ref[b:b + 1, :]
        o_ref[...] = acc

    return pl.pallas_call(body, out_shape=SDS((D, ncol), F32),
                          compiler_params=pltpu.CompilerParams(vmem_limit_bytes=VMEM_LIMIT), name=name,
                          )(c_all_t, dmod)


def _sum_devices(parts, out_dtype, name):
    _, r, c = parts.shape
    tr = TR if r % TR == 0 else r

    def body(p_ref, o_ref):
        acc = p_ref[0].astype(F32)
        for i in range(1, NDEV):
            acc = acc + p_ref[i].astype(F32)
        o_ref[...] = acc.astype(o_ref.dtype)

    return pl.pallas_call(
        body, grid=(r // tr,), in_specs=[pl.BlockSpec((NDEV, tr, c), lambda i: (0, i, 0))],
        out_specs=pl.BlockSpec((tr, c), lambda i: (i, 0)), out_shape=SDS((r, c), out_dtype),
        compiler_params=_params(("parallel",)), name=name,
    )(parts)


def _adam_tiles(r, c):
    if r % 8 == 0:
        return _pick(r, (256, 352, 128, 8)), c
    return r, (256 if c % 256 == 0 else c)


def _adam_math(w, gv, m, v):
    m_new = ADAM_B1 * m + (1.0 - ADAM_B1) * gv
    v_new = ADAM_B2 * v + (1.0 - ADAM_B2) * (gv * gv)
    bc1 = 1.0 - ADAM_B1 ** ADAM_STEP
    bc2 = 1.0 - ADAM_B2 ** ADAM_STEP
    return -ADAM_LR * ((m_new / bc1) / (jnp.sqrt(v_new / bc2) + ADAM_EPS) + ADAM_WD * w), m_new, v_new


def _adamw(w, g, m, v, name):
    r, c = w.shape
    tr, tc = _adam_tiles(r, c)

    def body(w_ref, g_ref, m_ref, v_ref, d_ref, nm_ref, nv_ref):
        d_ref[...], nm_ref[...], nv_ref[...] = _adam_math(w_ref[...], g_ref[...], m_ref[...], v_ref[...])

    spec = pl.BlockSpec((tr, tc), lambda i, j: (i, j))
    return pl.pallas_call(
        body, grid=(r // tr, c // tc), in_specs=[spec] * 4, out_specs=[spec] * 3,
        out_shape=[SDS((r, c), F32)] * 3, compiler_params=_params(("parallel", "parallel")), name=name,
    )(w, g, m, v)


def _reduce_adamw(parts, w, m, v, name):
    r, c = w.shape
    tr, tc = _adam_tiles(r, c)

    def body(p_ref, w_ref, m_ref, v_ref, g_ref, d_ref, nm_ref, nv_ref):
        gv = p_ref[0].astype(F32)
        for i in range(1, NDEV):
            gv = gv + p_ref[i].astype(F32)
        g_ref[...] = gv
        d_ref[...], nm_ref[...], nv_ref[...] = _adam_math(w_ref[...], gv, m_ref[...], v_ref[...])

    spec = pl.BlockSpec((tr, tc), lambda i, j: (i, j))
    return pl.pallas_call(
        body, grid=(r // tr, c // tc),
        in_specs=[pl.BlockSpec((NDEV, tr, tc), lambda i, j: (0, i, j))] + [spec] * 3, out_specs=[spec] * 4,
        out_shape=[SDS((r, c), F32)] * 4, compiler_params=_params(("parallel", "parallel")), name=name,
    )(parts, w, m, v)


ANY = pl.BlockSpec(memory_space=pl.ANY)
MESH = pl.DeviceIdType.MESH


def _all_gather(xs, name, after=None):
    n = len(xs)
    extra = [] if after is None else [after]

    def body(*refs):
        x_refs, out_refs = refs[:n], refs[n + len(extra):2 * n + len(extra)]
        send_sems, recv_sems, local_sems = refs[-3:]
        mx, my, mc = lax.axis_index("x"), lax.axis_index("y"), lax.axis_index("c")
        me, sibling = (mx, my, mc), (mx, my, 1 - mc)
        chips = [(1 - mx, my), (mx, 1 - my), (1 - mx, 1 - my)]

        def rows(a, px, py, pc):
            return out_refs[a].at[4 * px + 2 * py + pc]

        def copy(a, k, block, to, src=None):
            return pltpu.make_async_remote_copy(
                src_ref=rows(a, *block) if src is None else src, dst_ref=rows(a, *block),
                send_sem=send_sems.at[a, k], recv_sem=recv_sems.at[a, k], device_id=to, device_id_type=MESH)

        mine = [pltpu.make_async_copy(x_refs[a], rows(a, *me), local_sems.at[a]) for a in range(n)]
        for cp in mine:
            cp.start()
        first = []
        for a in range(n):
            first.append(copy(a, 0, me, sibling, src=x_refs[a]))
            first += [copy(a, 1 + j, me, (*chip, mc), src=x_refs[a]) for j, chip in enumerate(chips)]
        for cp in first:
            cp.start()
        passed = []
        for a in range(n):
            for j, chip in enumerate(chips):
                copy(a, 1 + j, (*chip, mc), me).wait_recv()
                passed.append(copy(a, 4 + j, (*chip, mc), sibling))
                passed[-1].start()
        for a in range(n):
            copy(a, 0, sibling, me).wait_recv()
            for j, chip in enumerate(chips):
                copy(a, 4 + j, (*chip, 1 - mc), me).wait_recv()
        for cp in first + passed:
            cp.wait_send()
        for cp in mine:
            cp.wait()

    return pl.pallas_call(
        body, out_shape=[SDS((NDEV,) + x.shape, x.dtype) for x in xs], in_specs=[ANY] * (n + len(extra)),
        out_specs=[ANY] * n,
        scratch_shapes=[pltpu.SemaphoreType.DMA((n, 7)), pltpu.SemaphoreType.DMA((n, 7)),
                        pltpu.SemaphoreType.DMA((n,))],
        name=name,
    )(*xs, *extra)


HBM = pl.BlockSpec(memory_space=pltpu.HBM)
SEM = pl.BlockSpec(memory_space=pltpu.SEMAPHORE)
EFFECT = pltpu.SideEffectType.DATAFLOW_SIDE_EFFECTING


def _peers():
    mx, my, mc = lax.axis_index("x"), lax.axis_index("y"), lax.axis_index("c")
    out = []
    for k in range(1, NDEV):
        out.append((1 - mx if k & 4 else mx, 1 - my if k & 2 else my, 1 - mc if k & 1 else mc))
    return 4 * mx + 2 * my + mc, out


NEAR = (0, 1, 3, 5)


def _push_start(srcs, sliced, name, after=None, near=()):
    n = len(srcs)
    extra = [] if after is None else [after]
    lands = [lax.empty(s.shape if sliced else (NDEV,) + s.shape, s.dtype) for s in srcs]

    def body(*refs):
        src_refs, land_refs = refs[:n], refs[n:2 * n]
        outs = refs[2 * n + len(extra):]
        send_sems, recv_sems = outs[:n], outs[n:2 * n]
        token = refs[-1]
        me, peers = _peers()
        for a in range(n):
            for k, (px, py, pc) in enumerate(peers):
                if a in near and k not in NEAR:
                    continue
                src = src_refs[a].at[4 * px + 2 * py + pc] if sliced else src_refs[a]
                pltpu.make_async_remote_copy(
                    src_ref=src, dst_ref=land_refs[a].at[me], send_sem=send_sems[a].at[k],
                    recv_sem=recv_sems[a].at[k], device_id=(px, py, pc), device_id_type=MESH).start()
            pltpu.make_async_copy(src_refs[a].at[me] if sliced else src_refs[a], land_refs[a].at[me],
                                  send_sems[a].at[NDEV - 1]).start()
        token[...] = jnp.zeros_like(token)

    outs = pl.pallas_call(
        body, name=name,
        out_shape=([pltpu.SemaphoreType.DMA((NDEV,))] * n + [pltpu.SemaphoreType.DMA((NDEV - 1,))] * n
                   + [pltpu.HBM(s.shape, s.dtype) for s in srcs] + [pltpu.HBM(l.shape, l.dtype) for l in lands]
                   + [SDS((8, 128), F32)]),
        in_specs=[HBM] * (2 * n) + [pl.BlockSpec(memory_space=pl.ANY)] * len(extra),
        out_specs=[SEM] * (2 * n) + [HBM] * (2 * n) + [pl.BlockSpec(memory_space=pltpu.VMEM)],
        input_output_aliases={i: 2 * n + i for i in range(2 * n)},
        compiler_params=pltpu.CompilerParams(has_side_effects=EFFECT),
    )(*[pltpu.with_memory_space_constraint(s, pltpu.HBM) for s in srcs],
      *[pltpu.with_memory_space_constraint(l, pltpu.HBM) for l in lands], *extra)
    sends, recvs = outs[:n], outs[n:2 * n]
    src_thru, land_thru = outs[2 * n:3 * n], outs[3 * n:4 * n]
    return [(sends[a], recvs[a], src_thru[a], land_thru[a]) for a in range(n)], outs[-1]


def _push_wait(started, sliced, after, name, near=()):
    n = len(started)
    afters = list(after) if isinstance(after, (list, tuple)) else [after]

    def body(*refs):
        src_refs, land_refs = refs[:n], refs[n:2 * n]
        send_sems, recv_sems = refs[2 * n:3 * n], refs[3 * n:4 * n]
        me, peers = _peers()
        for a in range(n):
            for k, (px, py, pc) in enumerate(peers):
                if a in near and k not in NEAR:
                    continue
                src = src_refs[a].at[4 * px + 2 * py + pc] if sliced else src_refs[a]
                cp = pltpu.make_async_remote_copy(
                    src_ref=src, dst_ref=land_refs[a].at[me], send_sem=send_sems[a].at[k],
                    recv_sem=recv_sems[a].at[k], device_id=(px, py, pc), device_id_type=MESH)
                cp.wait_send()
                cp.wait_recv()
            pltpu.make_async_copy(src_refs[a].at[me] if sliced else src_refs[a], land_refs[a].at[me],
                                  send_sems[a].at[NDEV - 1]).wait()

    srcs = [s[2] for s in started]
    lands = [s[3] for s in started]
    outs = pl.pallas_call(
        body, name=name,
        out_shape=[pltpu.HBM(s.shape, s.dtype) for s in srcs] + [pltpu.HBM(l.shape, l.dtype) for l in lands],
        in_specs=[HBM] * (2 * n) + [SEM] * (2 * n) + [pl.BlockSpec(memory_space=pl.ANY)] * len(afters),
        out_specs=[HBM] * (2 * n),
        input_output_aliases={i: i for i in range(2 * n)},
        compiler_params=pltpu.CompilerParams(has_side_effects=EFFECT),
    )(*srcs, *lands, *[s[0] for s in started], *[s[1] for s in started], *afters)
    return outs[n:]


def _relay_to_sibling(land, name):
    def body(_, land_ref, send_sems, recv_sems):
        mx, my, mc = lax.axis_index("x"), lax.axis_index("y"), lax.axis_index("c")
        chips = [(1 - mx, my), (mx, 1 - my), (1 - mx, 1 - my)]

        def copy(j, core):
            slot = land_ref.at[4 * chips[j][0] + 2 * chips[j][1] + core]
            return pltpu.make_async_remote_copy(
                src_ref=slot, dst_ref=slot, send_sem=send_sems.at[j], recv_sem=recv_sems.at[j],
                device_id=(mx, my, 1 - mc), device_id_type=MESH)

        mine = [copy(j, mc) for j in range(3)]
        for cp in mine:
            cp.start()
        for j in range(3):
            copy(j, 1 - mc).wait_recv()
        for cp in mine:
            cp.wait_send()

    return pl.pallas_call(
        body, out_shape=SDS(land.shape, land.dtype), in_specs=[ANY], out_specs=ANY, input_output_aliases={0: 0},
        scratch_shapes=[pltpu.SemaphoreType.DMA((3,)), pltpu.SemaphoreType.DMA((3,))], name=name,
    )(land)


def _cols_from_blocks(blocks):
    _, rows, w = blocks.shape
    return blocks.transpose(1, 0, 2).reshape(rows, NDEV * w)


def _cols_to_blocks(full):
    rows, total = full.shape
    return full.reshape(rows, NDEV, total // NDEV).transpose(1, 0, 2)


def _mix_pad(wt):
    xp, q, k, v, z, ba, gp, gd = jnp.split(wt, (512, 1536, 2560, 3584, 4608, 4624, 5648), axis=0)
    pad = jnp.zeros((MIXP - OFF_BA - 16, wt.shape[1]), wt.dtype)
    return jnp.concatenate([q, k, v, z, gp, gd, xp, ba, pad], axis=0)


def _mix_unpad(wt):
    q, k, v, z, gp, gd, xp, ba = (wt[OFF_Q:OFF_K], wt[OFF_K:OFF_V], wt[OFF_V:OFF_Z], wt[OFF_Z:OFF_GP],
                                  wt[OFF_GP:OFF_GD], wt[OFF_GD:OFF_XP], wt[OFF_XP:OFF_BA], wt[OFF_BA:OFF_BA + 16])
    return jnp.concatenate([xp, q, k, v, z, ba, gp, gd], axis=0)


def _lane_row(vec8):
    return jnp.zeros((1, 128), F32).at[0, NH:2 * NH].set(vec8)


def _ffn_fwd(x, h, gate, w_in, w_out, tag, next_norm=None, token=None, start_more=None):
    if isinstance(w_in, tuple):
        w_in, = _push_wait([w_in], False, h, f"{tag}_gather_wait_in")
    w_in = w_in.reshape(2 * FH, D)
    u, a = _swiglu_up(h, w_in, f"{tag}_up", after=token)
    w_out, = _push_wait([w_out], False, a, f"{tag}_gather_wait_out")
    w_out = w_out.reshape(FH, D)
    outs = _matmul_residual(a, w_out, x, gate, 0.5, a_blk=True, norm=next_norm, name=f"{tag}_down",
                            after=None if start_more is None else start_more(h))
    return outs[0], (h, u, a, outs[1]), w_in, w_out, (outs[2] if next_norm else None)


def _ffn_bwd(dx_out, dy, x, g, scale, w_in, w_out, saved, tag, below=None):
    h, u, a, _ = saved
    t = x.shape[0]
    dw_out = _matmul(a, dy, ta=True, a_blk=True, out_dtype=BF16, name=f"{tag}_down_dw")
    sent_out, token = _push_start([dw_out.reshape(NDEV, FH // NDEV, D)], True, f"{tag}_grad_start_out")
    du = _swiglu_down_bwd(dy, w_out, u, f"{tag}_down_dx", after=token).reshape(NDEV, t, FB)
    dw_in = _matmul(du, h, ta=True, a_blk=True, out_dtype=BF16, name=f"{tag}_up_dw")
    sent_in, token = _push_start([dw_in.reshape(NDEV, FB, D)], True, f"{tag}_grad_start_in")
    dh = _matmul(du, w_in, a_blk=True, out_dtype=F32, name=f"{tag}_up_dx", after=token)
    return _norm_mod_bwd(x, g, scale, dh, dx_out, f"{tag}_norm_bwd", below), sent_in + sent_out


def kernel(x, c, ada_w, ada_b, norm_g, ffn1_w_in, ffn1_w_out, ffn2_w_in, ffn2_w_out, mix_w_in, conv_w, a_log, dt_bias, dn_norm_g, pool_w, pool_scale, pool_proj, dn_proj, mix_w_out, final_g, loss_target, m_ada_w, m_ada_b, m_norm_g, m_ffn1_w_in, m_ffn1_w_out, m_ffn2_w_in, m_ffn2_w_out, m_mix_w_in, m_conv_w, m_a_log, m_dt_bias, m_dn_norm_g, m_pool_w, m_pool_scale, m_pool_proj, m_dn_proj, m_mix_w_out, m_final_g, v_ada_w, v_ada_b, v_norm_g, v_ffn1_w_in, v_ffn1_w_out, v_ffn2_w_in, v_ffn2_w_out, v_mix_w_in, v_conv_w, v_a_log, v_dt_bias, v_dn_norm_g, v_pool_w, v_pool_scale, v_pool_proj, v_dn_proj, v_mix_w_out, v_final_g):
    me = 4 * lax.axis_index("x") + 2 * lax.axis_index("y") + lax.axis_index("c")
    x0 = x[0]
    target = loss_target[0]
    t = x0.shape[0]

    big = [ffn1_w_in[0], ffn1_w_out[0], ffn2_w_in[0], ffn2_w_out[0], mix_w_in[0], pool_proj[0], dn_proj[0],
           mix_w_out[0]]
    small = jnp.concatenate([c.reshape(8, 128), conv_w[0].reshape(12, 128), norm_g[0].reshape(3, 128),
                             jnp.zeros((1, 128), F32)], axis=0)
    small_all, = _all_gather([small], "gather_small")
    c_all = small_all[:, 0:8, :].reshape(NDEV, D)
    conv_full = small_all[:, 8:20, :].reshape(NDEV, 4, 384).transpose(1, 0, 2).reshape(4, 3 * D)
    norm_full = small_all[:, 20:23, :].reshape(NDEV, 3, 128).transpose(1, 0, 2).reshape(3, D)

    ncol = ada_w.shape[2]
    ada_b_mine = lax.dynamic_slice(ada_b, (0, me * ncol), (1, ncol))
    mod_cols = _ada_fwd(c_all, ada_w[0], ada_b_mine, "ada_fwd")
    transposed = (0, 2, 4)
    payload = [(w.T if i in transposed else w).astype(BF16) for i, w in enumerate(big)]
    mod_all, w_in1 = _all_gather([mod_cols, payload[0]], "gather_mod_first_weight")
    started, token = _push_start([payload[1], payload[4]], False, "gather_start", after=mod_all, near=(1,))
    started = {1: started[0], 4: started[1]}

    def start_rest(h):
        more, token = _push_start([payload[i] for i in (5, 6, 7, 2, 3)], False, "gather_start_rest", after=h)
        started.update(zip((5, 6, 7, 2, 3), more))
        return token

    mod = lax.dynamic_index_in_dim(mod_all, me, axis=1, keepdims=False).reshape(9, D)
    shift = [mod[3 * s:3 * s + 1] for s in range(3)]
    scale = [mod[3 * s + 1:3 * s + 2] for s in range(3)]
    gate = [mod[3 * s + 2:3 * s + 3] for s in range(3)]
    ng = [norm_full[s:s + 1] for s in range(3)]
    fg = final_g.reshape(1, D)
    al_row = _lane_row(a_log[0])
    dt_row = _lane_row(dt_bias[0])
    gn = dn_norm_g
    pw = pool_w[0]
    ps = pool_scale

    h0 = _norm_mod_fwd(x0, ng[0], shift[0], scale[0], "ffn1_norm", after=token)
    x1, saved1, w_in1, w_out1, h1 = _ffn_fwd(x0, h0, gate[0], w_in1, started[1], "ffn1",
                                             (ng[1], shift[1], scale[1]), token, start_rest)

    seg, = _push_wait([started[4]], False, h1, "mix_gather_wait", near=(0,))
    w_mix = _mix_pad(_relay_to_sibling(seg, "mix_gather_relay").reshape(MIX_RAW, D))
    proj = _matmul(h1, w_mix, tb=True, out_dtype=F32, name="mix_in")
    qh, kh, vh, bg = _dn_pre_fwd(proj, conv_full, al_row, dt_row, "dn_pre")
    seg = _push_wait([started[i] for i in (5, 6, 7)], False, qh, "mix_gather_wait_rest")
    w_pp = _cols_from_blocks(seg[0])
    w_dn = seg[1].reshape(D, D)
    w_mo = seg[2].reshape(D, D)
    ya = _pool_fwd(proj, pw, ps, w_pp, "pool_fwd")
    u, w, qk, qd, kd, eg, inv = _dn_local_fwd(qh, kh, vh, bg, "dn_local")
    o, s_saved = _dn_scan_fwd(u, w, qk, qd, kd, eg, "dn_scan")
    ob = _dn_post_fwd(o, proj, gn, "dn_post")
    yb = _matmul(ob, w_dn, out_dtype=F32, name="dn_out")
    merged = _merge_fwd(ya, yb, proj, "merge")
    x2, mix_y, h2 = _matmul_residual(merged, w_mo, x1, gate[1], 1.0, norm=(ng[2], shift[2], scale[2]),
                                     name="mix_out")

    x3, saved2, w_in2, w_out2, _ = _ffn_fwd(x2, h2, gate[2], started[2], started[3], "ffn2")
    loss_row, dx3, dfg, dy2, dgate2 = _final_loss(x3, fg, target, (saved2[3], gate[2], 0.5), "loss")

    (dx2, dsh2, dsc2, dng2, dmy, dgate1), sent2 = _ffn_bwd(dx3, dy2, x2, ng[2], scale[2], w_in2, w_out2, saved2,
                                                           "ffn2", (mix_y, gate[1], 1.0))

    dmerged = _matmul(dmy, w_mo, tb=True, out_dtype=F32, name="mix_out_dx")
    dw_mo = _matmul(merged, dmy, ta=True, out_dtype=BF16, name="mix_out_dw")
    dproj = lax.empty((t, MIXP), BF16)
    dya, dyb, dproj = _merge_bwd(dmerged, ya, yb, proj, dproj, "merge_bwd")
    dob = _matmul(dyb, w_dn, tb=True, out_dtype=F32, name="dn_out_dx")
    dw_dn = _matmul(ob, dyb, ta=True, out_dtype=BF16, name="dn_out_dw")
    do, dproj, dgn = _dn_post_bwd(o, proj, gn, dob, dproj, "dn_post_bwd")
    du, dw, dqk, dqd, dkd, deg = _dn_scan_bwd(u, w, qk, qd, kd, eg, s_saved, do, "dn_scan_bwd")
    dqh, dkh, dvh, dbg = _dn_local_bwd(qh, kh, vh, bg, inv, du, dw, dqk, dqd, dkd, deg, "dn_local_bwd")
    dconv, dproj, dal, ddt = _dn_pre_bwd_act(proj, conv_full, al_row, dt_row, dqh, dkh, dvh, dbg, dproj,
                                             "dn_pre_bwd_act")
    dproj, dcw = _dn_pre_bwd_conv(proj, conv_full, dconv, dproj, "dn_pre_bwd_conv")
    dwin, dpl, dpw, dps, dpp = _pool_bwd_local(proj, pw, ps, w_pp, dya, "pool_bwd_local")
    dproj = _pool_bwd_window(dwin, dpl, dproj, "pool_bwd_window")
    dw_mix = _matmul(dproj, h1, ta=True, out_dtype=BF16, name="mix_in_dw")
    sent1, token = _push_start(
        [_mix_unpad(dw_mix).reshape(NDEV, MIX_RAW // NDEV, D), _cols_to_blocks(dpp.astype(BF16)),
         dw_dn.reshape(NDEV, -1, D), dw_mo.reshape(NDEV, -1, D)], True, "mix_grad_start")
    dh1 = _matmul(dproj, w_mix, out_dtype=F32, name="mix_in_dx", after=token)
    dx1, dsh1, dsc1, dng1, dy0, dgate0 = _norm_mod_bwd(x1, ng[1], scale[1], dh1, dx2, "mix_norm_bwd",
                                                       (saved1[3], gate[0], 0.5))

    (dx0, dsh0, dsc0, dng0), sent0 = _ffn_bwd(dx1, dy0, x0, ng[0], scale[0], w_in1, w_out1, saved1, "ffn1")

    dmod = jnp.concatenate([dsh0, dsc0, dgate0, dsh1, dsc1, dgate1, dsh2, dsc2, dgate2], axis=1).reshape(-1)
    flat = jnp.concatenate([
        dmod, dal[0, NH:2 * NH], ddt[0, NH:2 * NH], dgn.reshape(-1), dps.reshape(-1), dfg.reshape(-1),
        dpw.reshape(-1), jnp.concatenate([dng0, dng1, dng2], axis=0).reshape(-1), dcw.reshape(-1),
        loss_row[0, 0:1]])
    nflat = 90 * D
    flat = jnp.concatenate([flat, jnp.zeros((nflat - flat.shape[0],), F32)]).reshape(90, D)
    sent_small, small_token = _push_start([flat], False, "small_grad_start")

    def small_grads(flat_all):
        tot = _sum_devices(flat_all, F32, "sum_small_grads").reshape(-1)
        dmod_all = flat_all.reshape(NDEV, nflat)[:, :9 * D]
        dmod_cols = lax.dynamic_slice(dmod_all, (0, me * ncol), (NDEV, ncol))
        g_ada_w = _ada_bwd(c_all.T, dmod_cols, "ada_bwd")
        p = 0
        pieces = {}
        for nm, size in (("ada_b", 9 * D), ("a_log", NH), ("dt_bias", NH), ("dn_norm_g", HD), ("pool_scale", PW),
                         ("final_g", D), ("pool_w", 4 * PG * PG), ("norm_g", 3 * D), ("conv_w", 12 * D),
                         ("loss", 1)):
            pieces[nm] = tot[p:p + size]
            p += size
        g_norm = lax.dynamic_slice(pieces["norm_g"].reshape(3, D), (0, me * 128), (3, 128))
        g_conv = lax.dynamic_slice(pieces["conv_w"].reshape(4, 3 * D), (0, me * 384), (4, 384))
        return pieces["loss"][0], {
            "ada_w": g_ada_w.reshape(ada_w.shape), "ada_b": pieces["ada_b"].reshape(ada_b.shape),
            "norm_g": g_norm.reshape(norm_g.shape), "conv_w": g_conv.reshape(conv_w.shape),
            "a_log": pieces["a_log"].reshape(a_log.shape), "dt_bias": pieces["dt_bias"].reshape(dt_bias.shape),
            "dn_norm_g": pieces["dn_norm_g"].reshape(dn_norm_g.shape),
            "pool_w": pieces["pool_w"].reshape(pool_w.shape),
            "pool_scale": pieces["pool_scale"].reshape(pool_scale.shape),
            "final_g": pieces["final_g"].reshape(final_g.shape),
        }

    grads = {}
    weights = {"ada_w": ada_w, "ada_b": ada_b, "norm_g": norm_g, "ffn1_w_in": ffn1_w_in, "ffn1_w_out": ffn1_w_out,
               "ffn2_w_in": ffn2_w_in, "ffn2_w_out": ffn2_w_out, "mix_w_in": mix_w_in, "conv_w": conv_w,
               "a_log": a_log, "dt_bias": dt_bias, "dn_norm_g": dn_norm_g, "pool_w": pool_w,
               "pool_scale": pool_scale, "pool_proj": pool_proj, "dn_proj": dn_proj, "mix_w_out": mix_w_out,
               "final_g": final_g}
    m_in = {"ada_w": m_ada_w, "ada_b": m_ada_b, "norm_g": m_norm_g, "ffn1_w_in": m_ffn1_w_in,
            "ffn1_w_out": m_ffn1_w_out, "ffn2_w_in": m_ffn2_w_in, "ffn2_w_out": m_ffn2_w_out,
            "mix_w_in": m_mix_w_in, "conv_w": m_conv_w, "a_log": m_a_log, "dt_bias": m_dt_bias,
            "dn_norm_g": m_dn_norm_g, "pool_w": m_pool_w, "pool_scale": m_pool_scale, "pool_proj": m_pool_proj,
            "dn_proj": m_dn_proj, "mix_w_out": m_mix_w_out, "final_g": m_final_g}
    v_in = {"ada_w": v_ada_w, "ada_b": v_ada_b, "norm_g": v_norm_g, "ffn1_w_in": v_ffn1_w_in,
            "ffn1_w_out": v_ffn1_w_out, "ffn2_w_in": v_ffn2_w_in, "ffn2_w_out": v_ffn2_w_out,
            "mix_w_in": v_mix_w_in, "conv_w": v_conv_w, "a_log": v_a_log, "dt_bias": v_dt_bias,
            "dn_norm_g": v_dn_norm_g, "pool_w": v_pool_w, "pool_scale": v_pool_scale, "pool_proj": v_pool_proj,
            "dn_proj": v_dn_proj, "mix_w_out": v_mix_w_out, "final_g": v_final_g}

    names = list(weights)
    large = ("ada_w", "ffn1_w_in", "ffn1_w_out", "ffn2_w_in", "ffn2_w_out", "mix_w_in", "pool_proj", "dn_proj",
             "mix_w_out")
    delta, new_m, new_v = {}, {}, {}

    flipped = ("ffn1_w_in", "ffn2_w_in", "mix_w_in")

    def views(nm):
        shp = weights[nm].shape
        two_d = (shp[-2], shp[-1])
        if nm in flipped:
            return (lambda a: a.reshape(two_d).T), (lambda a: a.T.reshape(shp))
        return (lambda a: a.reshape(two_d)), (lambda a: a.reshape(shp))

    def reduce_update(sent, group, after, tag):
        done = []
        for nm, r in zip(group, _push_wait(sent, True, after, f"{tag}_grad_wait")):
            view, back = views(nm)
            g_, d_, m_, v_ = _reduce_adamw(r, view(weights[nm]), view(m_in[nm]), view(v_in[nm]), f"adamw_{nm}")
            grads[nm], delta[nm], new_m[nm], new_v[nm] = back(g_), back(d_), back(m_), back(v_)
            done.append(d_)
        return done

    done = reduce_update(sent2, ("ffn2_w_in", "ffn2_w_out"), small_token, "ffn2")
    done += reduce_update(sent1, ("mix_w_in", "pool_proj", "dn_proj", "mix_w_out"), done, "mix")
    flat_all, = _push_wait(sent_small, False, done, "small_grad_wait")
    loss, small = small_grads(flat_all)
    grads.update(small)
    view, back = views("ada_w")
    done, m_, v_ = _adamw(view(ada_w), view(grads["ada_w"]), view(m_ada_w), view(v_ada_w), "adamw_ada_w")
    delta["ada_w"], new_m["ada_w"], new_v["ada_w"] = back(done), back(m_), back(v_)
    reduce_update(sent0, ("ffn1_w_in", "ffn1_w_out"), done, "ffn1")
    rest = [nm for nm in names if nm not in large]
    total = sum(weights[nm].size for nm in rest)
    padded = -(-total // D) * D

    def pack(tree, fill):
        flat_ = jnp.concatenate([tree[nm].reshape(-1) for nm in rest])
        return jnp.concatenate([flat_, jnp.full((padded - total,), fill, F32)]).reshape(-1, D)

    d_, m_, v_ = _adamw(pack(weights, 0.0), pack(grads, 0.0), pack(m_in, 0.0), pack(v_in, 1.0), "adamw_small")
    p = 0
    for nm in rest:
        size = weights[nm].size
        shp = weights[nm].shape
        delta[nm] = d_.reshape(-1)[p:p + size].reshape(shp)
        new_m[nm] = m_.reshape(-1)[p:p + size].reshape(shp)
        new_v[nm] = v_.reshape(-1)[p:p + size].reshape(shp)
        p += size

    grad_x = dx0.reshape(x.shape)
    return (loss, grad_x, *[grads[nm] for nm in names], *[delta[nm] for nm in names],
            *[new_m[nm] for nm in names], *[new_v[nm] for nm in names])
```

```python
import functools

import jax
import jax.numpy as jnp
from jax import lax
from jax.experimental import pallas as pl
from jax.experimental.pallas import tpu as pltpu

F32 = jnp.float32
BF16 = jnp.bfloat16
SDS = jax.ShapeDtypeStruct
HI = lax.Precision.HIGHEST

D = 1024
FH = 2816
FB = 704
NH = 8
HD = 128
CH = 64
SCAN_CHUNKS = 4
NDEV = 8
PW = 512
PG = 128
RMS_EPS = 1e-6
L2_EPS = 1e-6
TR = 512
HALO = 16
VMEM_LIMIT = 56 * 1024 * 1024

MIXP = 6912
OFF_Q, OFF_K, OFF_V, OFF_Z, OFF_GP, OFF_GD, OFF_XP, OFF_BA = 0, 1024, 2048, 3072, 4096, 5120, 6144, 6656
MIX_RAW = 6672

ADAM_LR = 0.001
ADAM_B1 = 0.9
ADAM_B2 = 0.999
ADAM_EPS = 1e-08
ADAM_WD = 0.01
ADAM_STEP = 10

NN = (((1,), (0,)), ((), ()))
NT = (((1,), (1,)), ((), ()))
TN = (((0,), (0,)), ((), ()))


def _dg(a, b, dims, prec=None):
    return lax.dot_general(a, b, dims, precision=prec, preferred_element_type=F32)


def _make_dots(prec):
    @jax.custom_vjp
    def nn(a, b):
        return _dg(a, b, NN, prec)

    @jax.custom_vjp
    def nt(a, b):
        return _dg(a, b, NT, prec)

    @jax.custom_vjp
    def tn(a, b):
        return _dg(a, b, TN, prec)

    nn.defvjp(lambda a, b: (nn(a, b), (a, b)), lambda r, d: (nt(d, r[1]), tn(r[0], d)))
    nt.defvjp(lambda a, b: (nt(a, b), (a, b)), lambda r, d: (nn(d, r[1]), tn(d, r[0])))
    tn.defvjp(lambda a, b: (tn(a, b), (a, b)), lambda r, d: (nt(r[1], d), nn(r[0], d)))
    return nn, nt, tn


_nn, _nt, _tn = _make_dots(None)


def _params(sem):
    return pltpu.CompilerParams(dimension_semantics=sem, vmem_limit_bytes=VMEM_LIMIT)


def _sigmoid(x):
    return 1.0 / (1.0 + jnp.exp(-x))


def _silu(x):
    return x * _sigmoid(x)


def _dsilu(x):
    s = _sigmoid(x)
    return s * (1.0 + x * (1.0 - s))


def _pick(n, cands):
    for c in cands:
        if n % c == 0:
            return c
    raise ValueError(f"no tile for {n}")


def _iota(shape, dim):
    return lax.broadcasted_iota(jnp.int32, shape, dim)


def _matmul(a, b, *, ta=False, tb=False, a_blk=False, b_blk=False, o_blk=False, tm=None, tn=None, tk=None,
            out_dtype, name, after=None):
    if a_blk:
        nb, r, cb = a.shape
        if ta:
            k_dim, m_dim, tm = r, nb * cb, cb
        else:
            m_dim, k_dim, tk = r, nb * cb, cb
    else:
        k_dim, m_dim = a.shape if ta else a.shape[::-1]
    if b_blk:
        nb, r, cb = b.shape
        if tb:
            n_dim, tk = r, cb
            assert nb * cb == k_dim
        else:
            n_dim, tn = nb * cb, cb
            assert r == k_dim
    else:
        n_dim = b.shape[0] if tb else b.shape[1]
    tm = tm or _pick(m_dim, (1024, 768, 512, 256, 128))
    tn = tn or _pick(n_dim, (1024, 768, 512, 256, 128))
    tk = tk or (k_dim if (k_dim <= 2816 and not ta) else _pick(k_dim, (2816, 2304, 2048, 1024, 512, 256)))
    nk = k_dim // tk
    dims = ((((0,) if ta else (1,)), ((1,) if tb else (0,))), ((), ()))

    def body(a_ref, b_ref, *rest):
        o_ref, acc_ref = rest[-2:]
        k = pl.program_id(2)

        @pl.when(k == 0)
        def _():
            acc_ref[...] = jnp.zeros_like(acc_ref)

        acc_ref[...] += lax.dot_general(a_ref[...].astype(BF16), b_ref[...].astype(BF16), dims,
                                        preferred_element_type=F32)

        @pl.when(k == nk - 1)
        def _():
            o_ref[...] = acc_ref[...].astype(o_ref.dtype)

    if a_blk:
        a_spec = (pl.BlockSpec((None, tk, tm), lambda i, j, k: (i, k, 0)) if ta
                  else pl.BlockSpec((None, tm, tk), lambda i, j, k: (k, i, 0)))
    else:
        a_spec = (pl.BlockSpec((tk, tm), lambda i, j, k: (k, i)) if ta
                  else pl.BlockSpec((tm, tk), lambda i, j, k: (i, k)))
    if b_blk:
        b_spec = (pl.BlockSpec((None, tn, tk), lambda i, j, k: (k, j, 0)) if tb
                  else pl.BlockSpec((None, tk, tn), lambda i, j, k: (j, k, 0)))
    else:
        b_spec = (pl.BlockSpec((tn, tk), lambda i, j, k: (j, k)) if tb
                  else pl.BlockSpec((tk, tn), lambda i, j, k: (k, j)))
    if o_blk:
        o_spec = pl.BlockSpec((None, tm, tn), lambda i, j, k: (j, i, 0))
        o_shape = SDS((n_dim // tn, m_dim, tn), out_dtype)
    else:
        o_spec = pl.BlockSpec((tm, tn), lambda i, j, k: (i, j))
        o_shape = SDS((m_dim, n_dim), out_dtype)
    return pl.pallas_call(
        body, grid=(m_dim // tm, n_dim // tn, nk),
        in_specs=[a_spec, b_spec] + ([] if after is None else [pl.BlockSpec(memory_space=pl.ANY)]),
        out_specs=o_spec,
        out_shape=o_shape,
        scratch_shapes=[pltpu.VMEM((tm, tn), F32)],
        compiler_params=_params(("parallel", "parallel", "arbitrary")),
        name=name,
    )(a, b, *([] if after is None else [after]))


def _matmul_residual(a, b, x, gate, coef, *, a_blk=False, norm=None, name, after=None):
    if a_blk:
        nb, m_dim, tk = a.shape
        nk = nb
        a_spec = pl.BlockSpec((None, 512, tk), lambda i, k: (k, i, 0))
    else:
        m_dim, tk = a.shape
        nk = 1
        a_spec = pl.BlockSpec((512, tk), lambda i, k: (i, 0))
    tm = 512
    extra = [] if after is None else [after]
    vecs = [gate] + (list(norm) if norm else [])

    def body(a_ref, b_ref, x_ref, gate_ref, *rest):
        vec_refs = rest[:len(vecs) - 1]
        outs = rest[len(vecs) - 1 + len(extra):]
        acc_ref = outs[-1]
        k = pl.program_id(1)

        @pl.when(k == 0)
        def _():
            acc_ref[...] = jnp.zeros_like(acc_ref)

        acc_ref[...] += _dg(a_ref[...], b_ref[...], NN)

        @pl.when(k == nk - 1)
        def _():
            y = acc_ref[...]
            xn = x_ref[...] + (coef * gate_ref[...]) * y
            outs[0][...] = xn
            outs[1][...] = y.astype(outs[1].dtype)
            if norm:
                g_ref, sh_ref, sc_ref = vec_refs
                r = lax.rsqrt(jnp.mean(xn * xn, axis=-1, keepdims=True) + RMS_EPS)
                outs[2][...] = (((xn * r) * g_ref[...]) * (1.0 + sc_ref[...]) + sh_ref[...]).astype(outs[2].dtype)

    row = pl.BlockSpec((tm, D), lambda i, k: (i, 0))
    vec = pl.BlockSpec((1, D), lambda i, k: (0, 0))
    return pl.pallas_call(
        body, grid=(m_dim // tm, nk),
        in_specs=[a_spec, pl.BlockSpec((tk, D), lambda i, k: (k, 0)), row] + [vec] * len(vecs)
        + [pl.BlockSpec(memory_space=pl.ANY)] * len(extra),
        out_specs=[row] * (3 if norm else 2),
        out_shape=[SDS((m_dim, D), F32), SDS((m_dim, D), BF16)] + ([SDS((m_dim, D), BF16)] if norm else []),
        scratch_shapes=[pltpu.VMEM((tm, D), F32)],
        compiler_params=_params(("parallel", "arbitrary")), name=name,
    )(a, b, x, *vecs, *extra)


def _row(width, col=0):
    return pl.BlockSpec((TR, width), lambda i: (i, col))


def _vec(width):
    return pl.BlockSpec((1, width), lambda i: (0, 0))


def _norm_mod_fwd(x, g, shift, scale, name, after=None):
    t = x.shape[0]
    extra = [] if after is None else [after]

    def body(x_ref, g_ref, sh_ref, sc_ref, *rest):
        o_ref = rest[-1]
        xv = x_ref[...]
        r = lax.rsqrt(jnp.mean(xv * xv, axis=-1, keepdims=True) + RMS_EPS)
        o_ref[...] = (((xv * r) * g_ref[...]) * (1.0 + sc_ref[...]) + sh_ref[...]).astype(o_ref.dtype)

    return pl.pallas_call(
        body, grid=(t // TR,),
        in_specs=[_row(D), _vec(D), _vec(D), _vec(D)] + [pl.BlockSpec(memory_space=pl.ANY)] * len(extra),
        out_specs=_row(D),
        out_shape=SDS((t, D), BF16), compiler_params=_params(("parallel",)), name=name,
    )(x, g, shift, scale, *extra)


def _residual_branch_bwd(dxv, y_ref, gate_ref, coef, dy_ref, dgate_ref):
    dy_ref[...] = ((coef * gate_ref[...]) * dxv).astype(dy_ref.dtype)
    dgate_ref[...] += jnp.sum((coef * dxv) * y_ref[...], axis=0, keepdims=True)


def _norm_mod_bwd(x, g, scale, dh, dx_in, name, below=None):
    t = x.shape[0]
    lower = [] if below is None else list(below[:2])

    def body(x_ref, g_ref, sc_ref, dh_ref, dxi_ref, *rest):
        dx_ref, dsh_ref, dsc_ref, dg_ref = rest[len(lower):len(lower) + 4]

        @pl.when(pl.program_id(0) == 0)
        def _():
            for ref in rest[len(lower) + 1:]:
                if ref.shape[0] == 1:
                    ref[...] = jnp.zeros_like(ref)

        xv = x_ref[...]
        gv = g_ref[...]
        dh = dh_ref[...]
        r = lax.rsqrt(jnp.mean(xv * xv, axis=-1, keepdims=True) + RMS_EPS)
        n = xv * r
        dsh_ref[...] += jnp.sum(dh, axis=0, keepdims=True)
        dsc_ref[...] += jnp.sum(dh * (n * gv), axis=0, keepdims=True)
        tt = dh * (1.0 + sc_ref[...])
        dg_ref[...] += jnp.sum(tt * n, axis=0, keepdims=True)
        dn = tt * gv
        dxv = dxi_ref[...] + r * (dn - n * jnp.mean(dn * n, axis=-1, keepdims=True))
        dx_ref[...] = dxv
        if below is not None:
            _residual_branch_bwd(dxv, rest[0], rest[1], below[2], rest[-2], rest[-1])

    more_in = [] if below is None else [_row(D), _vec(D)]
    more_out = [] if below is None else [_row(D), _vec(D)]
    more_shape = [] if below is None else [SDS((t, D), BF16), SDS((1, D), F32)]
    return pl.pallas_call(
        body, grid=(t // TR,), in_specs=[_row(D), _vec(D), _vec(D), _row(D), _row(D)] + more_in,
        out_specs=[_row(D), _vec(D), _vec(D), _vec(D)] + more_out,
        out_shape=[SDS((t, D), F32), SDS((1, D), F32), SDS((1, D), F32), SDS((1, D), F32)] + more_shape,
        compiler_params=_params(("arbitrary",)), name=name,
    )(x, g, scale, dh, dx_in, *lower)


def _swiglu_up(h, w_in, name, after=None):
    t = h.shape[0]
    tm = _pick(t, (1024, 512, 256))
    half = NDEV // 2
    extra = [] if after is None else [after]

    def body(h_ref, wg_ref, wu_ref, *rest):
        u_ref, a_ref = rest[-2:]
        hv = h_ref[...]
        gate = _dg(hv, wg_ref[...], NT)
        up = _dg(hv, wu_ref[...], NT)
        u_ref[0] = gate.astype(u_ref.dtype)
        u_ref[1] = up.astype(u_ref.dtype)
        a_ref[...] = (_silu(gate) * up).astype(a_ref.dtype)

    return pl.pallas_call(
        body, grid=(t // tm, half),
        in_specs=[pl.BlockSpec((tm, D), lambda i, j: (i, 0)),
                  pl.BlockSpec((FB, D), lambda i, j: (j, 0)),
                  pl.BlockSpec((FB, D), lambda i, j: (j + half, 0))]
        + [pl.BlockSpec(memory_space=pl.ANY)] * len(extra),
        out_specs=[pl.BlockSpec((2, None, tm, FB), lambda i, j: (0, j, i, 0)),
                   pl.BlockSpec((None, tm, FB), lambda i, j: (j, i, 0))],
        out_shape=[SDS((2, half, t, FB), BF16), SDS((half, t, FB), BF16)],
        compiler_params=_params(("parallel", "parallel")), name=name,
    )(h, w_in, w_in, *extra)


def _swiglu_down_bwd(dy, w_out, u, name, after=None):
    t = dy.shape[0]
    tm = _pick(t, (1024, 512, 256))
    half = NDEV // 2
    extra = [] if after is None else [after]
    pair = pl.BlockSpec((2, None, tm, FB), lambda i, j: (0, j, i, 0))

    def body(dy_ref, w_ref, u_ref, *rest):
        o_ref = rest[-1]
        da = _dg(dy_ref[...], w_ref[...], NT)
        gate = u_ref[0].astype(F32)
        o_ref[0] = (da * u_ref[1].astype(F32) * _dsilu(gate)).astype(o_ref.dtype)
        o_ref[1] = (da * _silu(gate)).astype(o_ref.dtype)

    return pl.pallas_call(
        body, grid=(t // tm, half),
        in_specs=[pl.BlockSpec((tm, D), lambda i, j: (i, 0)), pl.BlockSpec((FB, D), lambda i, j: (j, 0)), pair]
        + [pl.BlockSpec(memory_space=pl.ANY)] * len(extra),
        out_specs=pair, out_shape=SDS((2, half, t, FB), BF16),
        compiler_params=_params(("parallel", "parallel")), name=name,
    )(dy, w_out, u, *extra)


def _final_loss(x, fg, target, below, name):
    t = x.shape[0]
    nt = t // TR

    def body(x_ref, g_ref, t_ref, y_ref, gate_ref, loss_ref, dx_ref, dg_ref, dy_ref, dgate_ref, acc_ref):
        i = pl.program_id(0)

        @pl.when(i == 0)
        def _():
            acc_ref[...] = jnp.zeros_like(acc_ref)
            dg_ref[...] = jnp.zeros_like(dg_ref)
            dgate_ref[...] = jnp.zeros_like(dgate_ref)

        xv = x_ref[...]
        gv = g_ref[...]
        r = lax.rsqrt(jnp.mean(xv * xv, axis=-1, keepdims=True) + RMS_EPS)
        n = xv * r
        err = n * gv - t_ref[...]
        acc_ref[...] += jnp.sum(err * err, axis=0, keepdims=True)
        dy = err * (1.0 / D)
        dg_ref[...] += jnp.sum(dy * n, axis=0, keepdims=True)
        dn = dy * gv
        dxv = r * (dn - n * jnp.mean(dn * n, axis=-1, keepdims=True))
        dx_ref[...] = dxv
        _residual_branch_bwd(dxv, y_ref, gate_ref, below[2], dy_ref, dgate_ref)

        @pl.when(i == nt - 1)
        def _():
            tot = jnp.sum(acc_ref[...], axis=1, keepdims=True) * (0.5 / D)
            loss_ref[...] = jnp.broadcast_to(tot, loss_ref.shape)

    return pl.pallas_call(
        body, grid=(nt,), in_specs=[_row(D), _vec(D), _row(D), _row(D), _vec(D)],
        out_specs=[_vec(128), _row(D), _vec(D), _row(D), _vec(D)],
        out_shape=[SDS((1, 128), F32), SDS((t, D), F32), SDS((1, D), F32), SDS((t, D), BF16), SDS((1, D), F32)],
        scratch_shapes=[pltpu.VMEM((1, D), F32)],
        compiler_params=_params(("arbitrary",)), name=name,
    )(x, fg, target, below[0], below[1])


def _halo_prev(width, col):
    per = TR // HALO
    return pl.BlockSpec((HALO, width), lambda i: (jnp.maximum(i * per - 1, 0), col))


def _halo_next(width, col, nt):
    per = TR // HALO
    return pl.BlockSpec((HALO, width), lambda i: (jnp.minimum((i + 1) * per, nt * per - 1), col))


def _pool_windows(ext, tile_index):
    rows = _iota((TR, PG), 0) + tile_index * TR + 1
    pooled, counts = [], []
    for gi in range(4):
        w = 2 << gi
        e = ext[:, gi * PG:(gi + 1) * PG]
        s = e
        step = 1
        while step < w:
            s = s + pltpu.roll(s, step, 0)
            step *= 2
        cnt = jnp.minimum(rows, w).astype(F32)
        pooled.append(s[HALO:] / cnt - e[HALO:])
        counts.append(cnt)
    return pooled, counts


def _pool_fwd(proj, pool_w, pool_scale, pool_proj, name):
    t = proj.shape[0]
    xcol = OFF_XP // PW

    def body(x_ref, h_ref, pw_ref, ps_ref, pp_ref, o_ref):
        i = pl.program_id(0)
        halo = jnp.where(i > 0, h_ref[...], 0.0)
        ext = jnp.concatenate([halo, x_ref[...]], axis=0)
        pooled, _ = _pool_windows(ext, i)
        mixed = [_dg(pooled[g].astype(BF16), pw_ref[g].astype(BF16), NN) for g in range(4)]
        ypre = jnp.concatenate(mixed, axis=1) * ps_ref[...]
        o_ref[...] = _dg(ypre.astype(BF16), pp_ref[...], NN)

    return pl.pallas_call(
        body, grid=(t // TR,),
        in_specs=[_row(PW, xcol), _halo_prev(PW, xcol),
                  pl.BlockSpec((4, PG, PG), lambda i: (0, 0, 0)), _vec(PW),
                  pl.BlockSpec((PW, D), lambda i: (0, 0))],
        out_specs=_row(D), out_shape=SDS((t, D), F32),
        compiler_params=_params(("parallel",)), name=name,
    )(proj, proj, pool_w, pool_scale, pool_proj)


def _pool_bwd_local(proj, pool_w, pool_scale, pool_proj, dya, name):
    t = proj.shape[0]
    xcol = OFF_XP // PW

    def body(x_ref, h_ref, pw_ref, ps_ref, pp_ref, dya_ref, dwin_ref, dpl_ref, dpw_ref, dps_ref, dpp_ref):
        i = pl.program_id(0)

        @pl.when(i == 0)
        def _():
            dpw_ref[...] = jnp.zeros_like(dpw_ref)
            dps_ref[...] = jnp.zeros_like(dps_ref)
            dpp_ref[...] = jnp.zeros_like(dpp_ref)

        halo = jnp.where(i > 0, h_ref[...], 0.0)
        ext = jnp.concatenate([halo, x_ref[...]], axis=0)
        pooled, counts = _pool_windows(ext, i)
        mixed = jnp.concatenate(
            [_dg(pooled[g].astype(BF16), pw_ref[g].astype(BF16), NN) for g in range(4)], axis=1)
        ps = ps_ref[...]
        ypre = mixed * ps
        dyab = dya_ref[...].astype(BF16)
        dypre = _dg(dyab, pp_ref[...], NT)
        dpp_ref[...] += _dg(ypre.astype(BF16), dyab, TN)
        dps_ref[...] += jnp.sum(dypre * mixed, axis=0, keepdims=True)
        dmixed = dypre * ps
        for g in range(4):
            dm = dmixed[:, g * PG:(g + 1) * PG].astype(BF16)
            dpw_ref[g] += _dg(pooled[g].astype(BF16), dm, TN)
            dpooled = _dg(dm, pw_ref[g].astype(BF16), NT)
            dwin_ref[:, g * PG:(g + 1) * PG] = dpooled / counts[g]
            dpl_ref[:, g * PG:(g + 1) * PG] = dpooled

    return pl.pallas_call(
        body, grid=(t // TR,),
        in_specs=[_row(PW, xcol), _halo_prev(PW, xcol),
                  pl.BlockSpec((4, PG, PG), lambda i: (0, 0, 0)), _vec(PW),
                  pl.BlockSpec((PW, D), lambda i: (0, 0)), _row(D)],
        out_specs=[_row(PW), _row(PW), pl.BlockSpec((4, PG, PG), lambda i: (0, 0, 0)), _vec(PW),
                   pl.BlockSpec((PW, D), lambda i: (0, 0))],
        out_shape=[SDS((t, PW), F32), SDS((t, PW), F32), SDS((4, PG, PG), F32), SDS((1, PW), F32),
                   SDS((PW, D), F32)],
        compiler_params=_params(("arbitrary",)), name=name,
    )(proj, proj, pool_w, pool_scale, pool_proj, dya)


def _pool_bwd_window(dwin, dpl, dproj, name):
    t = dwin.shape[0]
    nt = t // TR
    ext_rows = TR + HALO

    def body(dw_ref, h_ref, dp_ref, _, o_ref):
        i = pl.program_id(0)
        halo = jnp.where(i < nt - 1, h_ref[...], 0.0)
        ext = jnp.concatenate([dw_ref[...], halo], axis=0)
        for gi in range(4):
            w = 2 << gi
            s = ext[:, gi * PG:(gi + 1) * PG]
            step = 1
            while step < w:
                s = s + pltpu.roll(s, ext_rows - step, 0)
                step *= 2
            o_ref[:, gi * PG:(gi + 1) * PG] = (s[:TR] - dp_ref[:, gi * PG:(gi + 1) * PG]).astype(o_ref.dtype)

    return pl.pallas_call(
        body, grid=(nt,),
        in_specs=[_row(PW), _halo_next(PW, 0, nt), _row(PW), pl.BlockSpec(memory_space=pl.ANY)],
        out_specs=_into(PW, OFF_XP), out_shape=SDS(dproj.shape, dproj.dtype), input_output_aliases={3: 0},
        compiler_params=_params(("parallel",)), name=name,
    )(dwin, dwin, dpl, dproj)


def _conv_group(ext, cw_ref, cols):
    acc = cw_ref[3:4, cols] * ext
    for j in range(3):
        acc = acc + cw_ref[j:j + 1, cols] * pltpu.roll(ext, 3 - j, 0)
    return acc[HALO:]


def _gate_terms(raw, al, dt):
    beta = _sigmoid(raw)
    xg = raw + dt
    sp = jnp.maximum(xg, 0.0) + jnp.log(1.0 + jnp.exp(-jnp.abs(xg)))
    g = -jnp.exp(al) * sp
    return beta, g, _sigmoid(xg)


def _dn_pre_fwd(proj, conv_w, al_row, dt_row, name):
    t = proj.shape[0]

    def body(x_ref, h_ref, cw_ref, ba_ref, al_ref, dt_ref, q_ref, k_ref, v_ref, bg_ref):
        i = pl.program_id(0)
        keep = i > 0
        for grp in range(24):
            cols = slice(grp * HD, (grp + 1) * HD)
            ext = jnp.concatenate([jnp.where(keep, h_ref[:, cols], 0.0), x_ref[:, cols]], axis=0)
            s = _silu(_conv_group(ext, cw_ref, cols))
            seg, head = divmod(grp, NH)
            hc = slice(head * HD, (head + 1) * HD)
            if seg == 0:
                q_ref[:, hc] = s * lax.rsqrt(jnp.sum(s * s, axis=-1, keepdims=True) + L2_EPS) * (HD ** -0.5)
            elif seg == 1:
                k_ref[:, hc] = s * lax.rsqrt(jnp.sum(s * s, axis=-1, keepdims=True) + L2_EPS)
            else:
                v_ref[:, hc] = s
        lane = _iota((TR, 128), 1)
        rowc = _iota((TR, 128), 0) % CH
        beta, g, _ = _gate_terms(ba_ref[...], al_ref[...], dt_ref[...])
        step = 1
        while step < CH:
            g = g + jnp.where(rowc >= step, pltpu.roll(g, step, 0), 0.0)
            step *= 2
        bg_ref[...] = jnp.where(lane < NH, beta, jnp.where(lane < 2 * NH, g, 0.0))

    return pl.pallas_call(
        body, grid=(t // TR,),
        in_specs=[_row(3 * D, 0), _halo_prev(3 * D, 0), pl.BlockSpec((4, 3 * D), lambda i: (0, 0)),
                  _row(128, OFF_BA // 128), _vec(128), _vec(128)],
        out_specs=[_row(D), _row(D), _row(D), _row(128)],
        out_shape=[SDS((t, D), F32), SDS((t, D), F32), SDS((t, D), F32), SDS((t, 128), F32)],
        compiler_params=_params(("parallel",)), name=name,
    )(proj, proj, conv_w, proj, al_row, dt_row)


def _dn_pre_bwd_act(proj, conv_w, al_row, dt_row, dq, dk, dv, dbg, dproj, name):
    t = proj.shape[0]

    def body(x_ref, h_ref, cw_ref, ba_ref, al_ref, dt_ref, dq_ref, dk_ref, dv_ref, dbg_ref, _,
             dc_ref, draw_ref, dal_ref, ddt_ref):
        i = pl.program_id(0)

        @pl.when(i == 0)
        def _():
            dal_ref[...] = jnp.zeros_like(dal_ref)
            ddt_ref[...] = jnp.zeros_like(ddt_ref)

        keep = i > 0
        for grp in range(24):
            cols = slice(grp * HD, (grp + 1) * HD)
            ext = jnp.concatenate([jnp.where(keep, h_ref[:, cols], 0.0), x_ref[:, cols]], axis=0)
            cv = _conv_group(ext, cw_ref, cols)
            seg, head = divmod(grp, NH)
            hc = slice(head * HD, (head + 1) * HD)
            if seg == 2:
                ds = dv_ref[:, hc]
            else:
                s = _silu(cv)
                r = lax.rsqrt(jnp.sum(s * s, axis=-1, keepdims=True) + L2_EPS)
                dy = dq_ref[:, hc] if seg == 0 else dk_ref[:, hc]
                c = (HD ** -0.5) if seg == 0 else 1.0
                ds = (c * r) * (dy - s * ((r * r) * jnp.sum(dy * s, axis=-1, keepdims=True)))
            dc_ref[:, cols] = ds * _dsilu(cv)
        lane = _iota((TR, 128), 1)
        rowc = _iota((TR, 128), 0) % CH
        isb = lane < NH
        isg = jnp.logical_and(lane >= NH, lane < 2 * NH)
        beta, g, sg = _gate_terms(ba_ref[...], al_ref[...], dt_ref[...])
        dbgv = dbg_ref[...]
        dg = dbgv
        step = 1
        while step < CH:
            dg = dg + jnp.where(rowc < CH - step, pltpu.roll(dg, TR - step, 0), 0.0)
            step *= 2
        da_raw = dg * (-jnp.exp(al_ref[...])) * sg
        draw = jnp.where(isb, dbgv * beta * (1.0 - beta), jnp.where(isg, da_raw, 0.0))
        draw_ref[:, :128] = draw.astype(draw_ref.dtype)
        draw_ref[:, 128:] = jnp.zeros((TR, MIXP - OFF_BA - 128), draw_ref.dtype)
        dal_ref[...] += jnp.sum(jnp.where(isg, dg * g, 0.0), axis=0, keepdims=True)
        ddt_ref[...] += jnp.sum(jnp.where(isg, da_raw, 0.0), axis=0, keepdims=True)

    return pl.pallas_call(
        body, grid=(t // TR,),
        in_specs=[_row(3 * D, 0), _halo_prev(3 * D, 0), pl.BlockSpec((4, 3 * D), lambda i: (0, 0)),
                  _row(128, OFF_BA // 128), _vec(128), _vec(128), _row(D), _row(D), _row(D), _row(128),
                  pl.BlockSpec(memory_space=pl.ANY)],
        out_specs=[_row(3 * D), _into(MIXP - OFF_BA, OFF_BA), _vec(128), _vec(128)],
        out_shape=[SDS((t, 3 * D), F32), SDS(dproj.shape, dproj.dtype), SDS((1, 128), F32), SDS((1, 128), F32)],
        input_output_aliases={10: 1},
        compiler_params=_params(("arbitrary",)), name=name,
    )(proj, proj, conv_w, proj, al_row, dt_row, dq, dk, dv, dbg, dproj)


def _dn_pre_bwd_conv(proj, conv_w, dconv, dproj, name):
    t = proj.shape[0]
    nt = t // TR
    ext_rows = TR + HALO

    def body(x_ref, h_ref, cw_ref, dc_ref, dn_ref, _, dx_ref, dcw_ref):
        i = pl.program_id(0)

        @pl.when(i == 0)
        def _():
            dcw_ref[...] = jnp.zeros_like(dcw_ref)

        keep_prev = i > 0
        keep_next = i < nt - 1
        for grp in range(24):
            cols = slice(grp * HD, (grp + 1) * HD)
            dct = dc_ref[:, cols]
            dext = jnp.concatenate([dct, jnp.where(keep_next, dn_ref[:, cols], 0.0)], axis=0)
            acc = cw_ref[3:4, cols] * dext
            for j in range(3):
                acc = acc + cw_ref[j:j + 1, cols] * pltpu.roll(dext, ext_rows - (3 - j), 0)
            dx_ref[:, cols] = acc[:TR].astype(dx_ref.dtype)
            xext = jnp.concatenate([jnp.where(keep_prev, h_ref[:, cols], 0.0), x_ref[:, cols]], axis=0)
            for j in range(4):
                xs = xext if j == 3 else pltpu.roll(xext, 3 - j, 0)
                dcw_ref[j:j + 1, cols] += jnp.sum(xs[HALO:] * dct, axis=0, keepdims=True)

    return pl.pallas_call(
        body, grid=(nt,),
        in_specs=[_row(3 * D, 0), _halo_prev(3 * D, 0), pl.BlockSpec((4, 3 * D), lambda i: (0, 0)),
                  _row(3 * D), _halo_next(3 * D, 0, nt), pl.BlockSpec(memory_space=pl.ANY)],
        out_specs=[_into(3 * D, OFF_Q), pl.BlockSpec((4, 3 * D), lambda i: (0, 0))],
        out_shape=[SDS(dproj.shape, dproj.dtype), SDS((4, 3 * D), F32)],
        input_output_aliases={5: 0},
        compiler_params=_params(("arbitrary",)), name=name,
    )(proj, proj, conv_w, dconv, dconv, dproj)


def _dn_post_fwd(o, proj, gn, name):
    t = o.shape[0]

    def body(o_ref, z_ref, g_ref, out_ref):
        gv = g_ref[...]
        for h in range(NH):
            hc = slice(h * HD, (h + 1) * HD)
            ov = o_ref[:, hc]
            r = lax.rsqrt(jnp.mean(ov * ov, axis=-1, keepdims=True) + RMS_EPS)
            out_ref[:, hc] = (((ov * r) * gv) * _silu(z_ref[:, hc])).astype(out_ref.dtype)

    return pl.pallas_call(
        body, grid=(t // TR,), in_specs=[_row(D), _row(D, OFF_Z // D), _vec(HD)], out_specs=_row(D),
        out_shape=SDS((t, D), BF16), compiler_params=_params(("parallel",)), name=name,
    )(o, proj, gn)


def _dn_post_bwd(o, proj, gn, dob, dproj, name):
    t = o.shape[0]

    def body(o_ref, z_ref, g_ref, d_ref, _, do_ref, dz_ref, dg_ref):
        @pl.when(pl.program_id(0) == 0)
        def _():
            dg_ref[...] = jnp.zeros_like(dg_ref)

        gv = g_ref[...]
        acc = jnp.zeros((1, HD), F32)
        for h in range(NH):
            hc = slice(h * HD, (h + 1) * HD)
            ov = o_ref[:, hc]
            zv = z_ref[:, hc]
            dv = d_ref[:, hc]
            r = lax.rsqrt(jnp.mean(ov * ov, axis=-1, keepdims=True) + RMS_EPS)
            n = ov * r
            dz_ref[:, hc] = (dv * (n * gv) * _dsilu(zv)).astype(dz_ref.dtype)
            dng = dv * _silu(zv)
            acc = acc + jnp.sum(dng * n, axis=0, keepdims=True)
            dn = dng * gv
            do_ref[:, hc] = r * (dn - n * jnp.mean(dn * n, axis=-1, keepdims=True))
        dg_ref[...] += acc

    return pl.pallas_call(
        body, grid=(t // TR,),
        in_specs=[_row(D), _row(D, OFF_Z // D), _vec(HD), _row(D), pl.BlockSpec(memory_space=pl.ANY)],
        out_specs=[_row(D), _into(D, OFF_Z), _vec(HD)],
        out_shape=[SDS((t, D), F32), SDS(dproj.shape, dproj.dtype), SDS((1, HD), F32)],
        input_output_aliases={4: 1},
        compiler_params=_params(("arbitrary",)), name=name,
    )(o, proj, gn, dob, dproj)


def _merge_fwd(ya, yb, proj, name):
    t = ya.shape[0]

    def body(a_ref, b_ref, gp_ref, gd_ref, o_ref):
        o_ref[...] = (_sigmoid(gp_ref[...]) * a_ref[...] + _sigmoid(gd_ref[...]) * b_ref[...]).astype(o_ref.dtype)

    return pl.pallas_call(
        body, grid=(t // TR,), in_specs=[_row(D), _row(D), _row(D, OFF_GP // D), _row(D, OFF_GD // D)],
        out_specs=_row(D), out_shape=SDS((t, D), BF16),
        compiler_params=_params(("parallel",)), name=name,
    )(ya, yb, proj, proj)


def _into(width, offset):
    assert offset % width == 0
    return pl.BlockSpec((TR, width), lambda i: (i, offset // width))


def _merge_bwd(dm, ya, yb, proj, dproj, name):
    t = ya.shape[0]

    def body(d_ref, a_ref, b_ref, gp_ref, gd_ref, _, da_ref, db_ref, dg_ref):
        dv = d_ref[...]
        sp = _sigmoid(gp_ref[...])
        sd = _sigmoid(gd_ref[...])
        da_ref[...] = (dv * sp).astype(da_ref.dtype)
        db_ref[...] = (dv * sd).astype(db_ref.dtype)
        dg_ref[:, :D] = (dv * a_ref[...] * sp * (1.0 - sp)).astype(dg_ref.dtype)
        dg_ref[:, D:] = (dv * b_ref[...] * sd * (1.0 - sd)).astype(dg_ref.dtype)

    return pl.pallas_call(
        body, grid=(t // TR,),
        in_specs=[_row(D), _row(D), _row(D), _row(D, OFF_GP // D), _row(D, OFF_GD // D),
                  pl.BlockSpec(memory_space=pl.ANY)],
        out_specs=[_row(D), _row(D), _into(2 * D, OFF_GP)],
        out_shape=[SDS((t, D), BF16), SDS((t, D), BF16), SDS(dproj.shape, dproj.dtype)],
        input_output_aliases={5: 2},
        compiler_params=_params(("parallel",)), name=name,
    )(dm, ya, yb, proj, proj, dproj)


def _split2(x):
    hi = x.astype(BF16)
    return hi, (x - hi.astype(F32)).astype(BF16)


def _dot3(a, b, dims):
    ah, al = _split2(a)
    bh, bl = _split2(b)
    return _dg(ah, bh, dims) + (_dg(ah, bl, dims) + _dg(al, bh, dims))


def _neumann_inverses(mats):
    ri = _iota((CH, CH), 0)
    ci = _iota((CH, CH), 1)
    eye = jnp.where(ri == ci, 1.0, 0.0).astype(F32)
    xs = [-a for a in mats]
    ps = [eye + x for x in xs]
    for _ in range(5):
        xs = [_dot3(x, x, NN) for x in xs]
        ps = [p + _dot3(p, x, NN) for p, x in zip(ps, xs)]
    return ps


def _solve_with(inv):
    @jax.custom_vjp
    def solve(a, rhs):
        return _dot3(inv, rhs, NN)

    def fwd(a, rhs):
        sol = _dot3(inv, rhs, NN)
        return sol, sol

    def bwd(sol, d):
        drhs = _dot3(inv, d, TN)
        return -_dot3(drhs, sol, NT), drhs

    solve.defvjp(fwd, bwd)
    return solve


@jax.custom_vjp
def _rows_to_lanes(g64):
    ri = _iota((CH, CH), 0)
    ci = _iota((CH, CH), 1)
    diag = jnp.where(ri == ci, g64, 0.0)
    ones = jnp.ones((CH, CH), BF16)
    hi = diag.astype(BF16)
    rem = diag - hi.astype(F32)
    mid = rem.astype(BF16)
    lo = (rem - mid.astype(F32)).astype(BF16)
    return _dg(ones, hi, NN) + (_dg(ones, mid, NN) + _dg(ones, lo, NN))


def _rows_to_lanes_bwd(_, d):
    ri = _iota((CH, CH), 0)
    ci = _iota((CH, CH), 1)
    return (jnp.where(ri == ci, jnp.broadcast_to(jnp.sum(d, axis=0, keepdims=True), (CH, CH)), 0.0),)


_rows_to_lanes.defvjp(lambda g64: (_rows_to_lanes(g64), None), _rows_to_lanes_bwd)


def _chunk_local(solve_all, q, k, v, g128, g64, gl128, b128, b64):
    ri = _iota((CH, CH), 0)
    ci = _iota((CH, CH), 1)
    causal = ri >= ci
    strict = ri > ci
    gj = [_rows_to_lanes(g) for g in g64]
    decay = [jnp.where(causal, jnp.exp(jnp.where(causal, g - t, 0.0)), 0.0) for g, t in zip(g64, gj)]
    kk = [_nt(x, x) for x in k]
    a = [jnp.where(strict, b * m * dc, 0.0) for b, m, dc in zip(b64, kk, decay)]
    eg = [jnp.exp(g) for g in g128]
    rhs = [jnp.concatenate([b * x, (b * e) * y], axis=1) for b, x, e, y in zip(b128, v, eg, k)]
    sol = solve_all(a, rhs)
    qk = [jnp.where(causal, _nt(x, y) * dc, 0.0) for x, y, dc in zip(q, k, decay)]
    return ([s[:, :HD] for s in sol], [s[:, HD:] for s in sol], qk, [x * e for x, e in zip(q, eg)],
            [x * jnp.exp(gl - g) for x, gl, g in zip(k, gl128, g128)], [jnp.exp(gl) for gl in gl128])


def _all_head_gates(bgv):
    return tuple(list(z) for z in zip(*[_head_gates(bgv, h) for h in range(NH)]))


def _head_gates(bgv, h):
    lane = _iota((CH, 128), 1)
    row = _iota((CH, 128), 0)
    bcol = jnp.sum(jnp.where(lane == h, bgv, 0.0), axis=1, keepdims=True)
    gcol = jnp.sum(jnp.where(lane == NH + h, bgv, 0.0), axis=1, keepdims=True)
    g128 = jnp.broadcast_to(gcol, (CH, 128))
    gl128 = jnp.broadcast_to(jnp.sum(jnp.where(row == CH - 1, g128, 0.0), axis=0, keepdims=True), (CH, 128))
    return (g128, jnp.broadcast_to(gcol, (CH, CH)), gl128,
            jnp.broadcast_to(bcol, (CH, 128)), jnp.broadcast_to(bcol, (CH, CH)))


def _chunk_specs():
    row = pl.BlockSpec((CH, D), lambda i: (i, 0))
    small = pl.BlockSpec((CH, 128), lambda i: (i, 0))
    qk = pl.BlockSpec((NH, CH, CH), lambda i: (i, 0, 0))
    eg = pl.BlockSpec((1, NH, 128), lambda i: (i, 0, 0))
    return row, small, qk, eg


def _dn_local_fwd(q, k, v, bg, name):
    t = q.shape[0]
    n = t // CH

    def body(q_ref, k_ref, v_ref, bg_ref, u_ref, w_ref, qk_ref, qd_ref, kd_ref, eg_ref, inv_ref):
        cols = [slice(h * HD, (h + 1) * HD) for h in range(NH)]

        def solve_all(mats, rhs):
            invs = _neumann_inverses(mats)
            for h in range(NH):
                inv_ref[h] = invs[h]
            return [_dot3(m, r, NN) for m, r in zip(invs, rhs)]

        u, w, qk, qd, kd, egl = _chunk_local(
            solve_all, [q_ref[:, c] for c in cols], [k_ref[:, c] for c in cols], [v_ref[:, c] for c in cols],
            *_all_head_gates(bg_ref[...]))
        for h, hc in enumerate(cols):
            u_ref[:, hc] = u[h]
            w_ref[:, hc] = w[h].astype(w_ref.dtype)
            qd_ref[:, hc] = qd[h].astype(qd_ref.dtype)
            kd_ref[:, hc] = kd[h].astype(kd_ref.dtype)
            qk_ref[h] = qk[h].astype(qk_ref.dtype)
            eg_ref[0, h:h + 1, :] = egl[h][0:1, :]

    row, small, qkb, egb = _chunk_specs()
    return pl.pallas_call(
        body, grid=(n,), in_specs=[row, row, row, small], out_specs=[row, row, qkb, row, row, egb, qkb],
        out_shape=[SDS((t, D), F32), SDS((t, D), BF16), SDS((n * NH, CH, CH), BF16), SDS((t, D), BF16),
                   SDS((t, D), BF16), SDS((n, NH, 128), F32), SDS((n * NH, CH, CH), F32)],
        compiler_params=_params(("parallel",)), name=name,
    )(q, k, v, bg)


def _dn_local_bwd(q, k, v, bg, inv, du, dw, dqk, dqd, dkd, deg, name):
    t = q.shape[0]
    n = t // CH

    def body(q_ref, k_ref, v_ref, bg_ref, inv_ref, du_ref, dw_ref, dqk_ref, dqd_ref, dkd_ref, deg_ref,
             dq_ref, dk_ref, dv_ref, dbg_ref):
        bgv = bg_ref[...]
        lane = _iota((CH, 128), 1)
        row = _iota((CH, 128), 0)
        first = jnp.where(row == 0, 1.0, 0.0)
        acc = jnp.zeros((CH, 128), F32)
        cols = [slice(h * HD, (h + 1) * HD) for h in range(NH)]
        solves = [_solve_with(inv_ref[h]) for h in range(NH)]

        def solve_all(mats, rhs):
            return [f(m, r) for f, m, r in zip(solves, mats, rhs)]

        _, vjp = jax.vjp(functools.partial(_chunk_local, solve_all),
                         [q_ref[:, c] for c in cols], [k_ref[:, c] for c in cols], [v_ref[:, c] for c in cols],
                         *_all_head_gates(bgv))
        cts = ([du_ref[:, c].astype(F32) for c in cols], [dw_ref[:, c].astype(F32) for c in cols],
               [dqk_ref[h] for h in range(NH)],
               [dqd_ref[:, c].astype(F32) for c in cols], [dkd_ref[:, c].astype(F32) for c in cols],
               [jnp.broadcast_to(deg_ref[0, h:h + 1, :], (CH, 128)) * first for h in range(NH)])
        dq, dk, dv, dg128, dg64, dgl, db128, db64 = vjp(cts)
        for h, hc in enumerate(cols):
            dq_ref[:, hc] = dq[h]
            dk_ref[:, hc] = dk[h]
            dv_ref[:, hc] = dv[h]
            dg = jnp.sum(dg128[h], axis=1, keepdims=True) + jnp.sum(dg64[h], axis=1, keepdims=True)
            tot = jnp.sum(jnp.sum(dgl[h], axis=0, keepdims=True), axis=1, keepdims=True)
            dg = dg + jnp.where(row[:, 0:1] == CH - 1, tot, 0.0)
            db = jnp.sum(db128[h], axis=1, keepdims=True) + jnp.sum(db64[h], axis=1, keepdims=True)
            acc = acc + jnp.where(lane == h, db, 0.0) + jnp.where(lane == NH + h, dg, 0.0)
        dbg_ref[...] = acc

    row, small, qkb, egb = _chunk_specs()
    return pl.pallas_call(
        body, grid=(n,), in_specs=[row, row, row, small, qkb, row, row, qkb, row, row, egb],
        out_specs=[row, row, row, small],
        out_shape=[SDS((t, D), F32)] * 3 + [SDS((t, 128), F32)],
        compiler_params=_params(("parallel",)), name=name,
    )(q, k, v, bg, inv, du, dw, dqk, dqd, dkd, deg)


def _state_step(s, u, w, qk, qd, kd, egl):
    ws = [_nn(a, b) for a, b in zip(w, s)]
    v_new = [a - b for a, b in zip(u, ws)]
    qs = [_nn(a, b) for a, b in zip(qd, s)]
    intra = [_nn(a, b) for a, b in zip(qk, v_new)]
    upd = [_tn(a, b) for a, b in zip(kd, v_new)]
    return [a * e + b for a, e, b in zip(s, egl, upd)], [a + b for a, b in zip(qs, intra)]


def _dn_scan_fwd(u, w, qk, qd, kd, eg, name):
    t = u.shape[0]
    n = t // CH
    g = SCAN_CHUNKS

    def body(u_ref, w_ref, qk_ref, qd_ref, kd_ref, eg_ref, o_ref, save_ref, s_ref):
        @pl.when(pl.program_id(0) == 0)
        def _():
            s_ref[...] = jnp.zeros_like(s_ref)

        cols = [slice(h * HD, (h + 1) * HD) for h in range(NH)]
        s = [s_ref[h] for h in range(NH)]
        for c in range(g):
            rows = slice(c * CH, (c + 1) * CH)
            for h in range(NH):
                save_ref[c, h] = s[h].astype(save_ref.dtype)
            s, o = _state_step(
                s, [u_ref[rows, hc] for hc in cols], [w_ref[rows, hc].astype(F32) for hc in cols],
                [qk_ref[c * NH + h].astype(F32) for h in range(NH)], [qd_ref[rows, hc].astype(F32) for hc in cols],
                [kd_ref[rows, hc].astype(F32) for hc in cols], [eg_ref[c, h:h + 1, :] for h in range(NH)])
            for h, hc in enumerate(cols):
                o_ref[rows, hc] = o[h]
        for h in range(NH):
            s_ref[h] = s[h]

    row = pl.BlockSpec((g * CH, D), lambda i: (i, 0))
    qkb = pl.BlockSpec((g * NH, CH, CH), lambda i: (i, 0, 0))
    egb = pl.BlockSpec((g, NH, 128), lambda i: (i, 0, 0))
    return pl.pallas_call(
        body, grid=(n // g,), in_specs=[row, row, qkb, row, row, egb],
        out_specs=[row, pl.BlockSpec((g, NH, HD, HD), lambda i: (i, 0, 0, 0))],
        out_shape=[SDS((t, D), F32), SDS((n, NH, HD, HD), BF16)],
        scratch_shapes=[pltpu.VMEM((NH, HD, HD), F32)],
        compiler_params=_params(("arbitrary",)), name=name,
    )(u, w, qk, qd, kd, eg)


def _dn_scan_bwd(u, w, qk, qd, kd, eg, saved, do, name):
    t = u.shape[0]
    n = t // CH
    g = SCAN_CHUNKS
    last = n // g - 1

    def body(u_ref, w_ref, qk_ref, qd_ref, kd_ref, eg_ref, sv_ref, do_ref,
             du_ref, dw_ref, dqk_ref, dqd_ref, dkd_ref, deg_ref, ds_ref):
        @pl.when(pl.program_id(0) == 0)
        def _():
            ds_ref[...] = jnp.zeros_like(ds_ref)

        cols = [slice(h * HD, (h + 1) * HD) for h in range(NH)]
        ds = [ds_ref[h] for h in range(NH)]
        for c in reversed(range(g)):
            rows = slice(c * CH, (c + 1) * CH)
            _, vjp = jax.vjp(
                _state_step, [sv_ref[c, h].astype(F32) for h in range(NH)], [u_ref[rows, hc] for hc in cols],
                [w_ref[rows, hc].astype(F32) for hc in cols], [qk_ref[c * NH + h].astype(F32) for h in range(NH)],
                [qd_ref[rows, hc].astype(F32) for hc in cols], [kd_ref[rows, hc].astype(F32) for hc in cols],
                [eg_ref[c, h:h + 1, :] for h in range(NH)])
            ds, du, dw, dqk, dqd, dkd, deg = vjp((ds, [do_ref[rows, hc] for hc in cols]))
            for h, hc in enumerate(cols):
                du_ref[rows, hc] = du[h].astype(du_ref.dtype)
                dw_ref[rows, hc] = dw[h].astype(dw_ref.dtype)
                dqk_ref[c * NH + h] = dqk[h]
                dqd_ref[rows, hc] = dqd[h].astype(dqd_ref.dtype)
                dkd_ref[rows, hc] = dkd[h].astype(dkd_ref.dtype)
                deg_ref[c, h:h + 1, :] = deg[h]
        for h in range(NH):
            ds_ref[h] = ds[h]

    row = pl.BlockSpec((g * CH, D), lambda i: (last - i, 0))
    qkb = pl.BlockSpec((g * NH, CH, CH), lambda i: (last - i, 0, 0))
    egb = pl.BlockSpec((g, NH, 128), lambda i: (last - i, 0, 0))
    return pl.pallas_call(
        body, grid=(n // g,),
        in_specs=[row, row, qkb, row, row, egb,
                  pl.BlockSpec((g, NH, HD, HD), lambda i: (last - i, 0, 0, 0)), row],
        out_specs=[row, row, qkb, row, row, egb],
        out_shape=[SDS((t, D), BF16), SDS((t, D), BF16), SDS((n * NH, CH, CH), F32), SDS((t, D), BF16),
                   SDS((t, D), BF16), SDS((n, NH, 128), F32)],
        scratch_shapes=[pltpu.VMEM((NH, HD, HD), F32)],
        compiler_params=_params(("arbitrary",)), name=name,
    )(u, w, qk, qd, kd, eg, saved, do)


def _ada_fwd(c_all, ada_w, ada_b, name):
    ncol = ada_w.shape[1]

    def body(c_ref, w_ref, b_ref, o_ref):
        o_ref[...] = _dg(_silu(c_ref[...]), w_ref[...], NN, HI) + b_ref[...]

    return pl.pallas_call(body, out_shape=SDS((NDEV, ncol), F32),
                          compiler_params=pltpu.CompilerParams(vmem_limit_bytes=VMEM_LIMIT), name=name,
                          )(c_all, ada_w, ada_b)


def _ada_bwd(c_all_t, dmod, name):
    ncol = dmod.shape[1]

    def body(c_ref, d_ref, o_ref):
        sc = _silu(c_ref[...])
        acc = sc[:, 0:1] * d_ref[0:1, :]
        for b in range(1, NDEV):
            acc = acc + sc[:, b:b + 1] * d_ref[b:b + 1, :]
        o_ref[...] = acc

    return pl.pallas_call(body, out_shape=SDS((D, ncol), F32),
                          compiler_params=pltpu.CompilerParams(vmem_limit_bytes=VMEM_LIMIT), name=name,
                          )(c_all_t, dmod)


def _sum_devices(parts, out_dtype, name):
    _, r, c = parts.shape
    tr = TR if r % TR == 0 else r

    def body(p_ref, o_ref):
        acc = p_ref[0].astype(F32)
        for i in range(1, NDEV):
            acc = acc + p_ref[i].astype(F32)
        o_ref[...] = acc.astype(o_ref.dtype)

    return pl.pallas_call(
        body, grid=(r // tr,), in_specs=[pl.BlockSpec((NDEV, tr, c), lambda i: (0, i, 0))],
        out_specs=pl.BlockSpec((tr, c), lambda i: (i, 0)), out_shape=SDS((r, c), out_dtype),
        compiler_params=_params(("parallel",)), name=name,
    )(parts)


def _adam_tiles(r, c):
    if r % 8 == 0:
        return _pick(r, (256, 352, 128, 8)), c
    return r, (256 if c % 256 == 0 else c)


def _adam_math(w, gv, m, v):
    m_new = ADAM_B1 * m + (1.0 - ADAM_B1) * gv
    v_new = ADAM_B2 * v + (1.0 - ADAM_B2) * (gv * gv)
    bc1 = 1.0 - ADAM_B1 ** ADAM_STEP
    bc2 = 1.0 - ADAM_B2 ** ADAM_STEP
    return -ADAM_LR * ((m_new / bc1) / (jnp.sqrt(v_new / bc2) + ADAM_EPS) + ADAM_WD * w), m_new, v_new


def _adamw(w, g, m, v, name):
    r, c = w.shape
    tr, tc = _adam_tiles(r, c)

    def body(w_ref, g_ref, m_ref, v_ref, d_ref, nm_ref, nv_ref):
        d_ref[...], nm_ref[...], nv_ref[...] = _adam_math(w_ref[...], g_ref[...], m_ref[...], v_ref[...])

    spec = pl.BlockSpec((tr, tc), lambda i, j: (i, j))
    return pl.pallas_call(
        body, grid=(r // tr, c // tc), in_specs=[spec] * 4, out_specs=[spec] * 3,
        out_shape=[SDS((r, c), F32)] * 3, compiler_params=_params(("parallel", "parallel")), name=name,
    )(w, g, m, v)


def _reduce_adamw(parts, w, m, v, name):
    r, c = w.shape
    tr, tc = _adam_tiles(r, c)

    def body(p_ref, w_ref, m_ref, v_ref, g_ref, d_ref, nm_ref, nv_ref):
        gv = p_ref[0].astype(F32)
        for i in range(1, NDEV):
            gv = gv + p_ref[i].astype(F32)
        g_ref[...] = gv
        d_ref[...], nm_ref[...], nv_ref[...] = _adam_math(w_ref[...], gv, m_ref[...], v_ref[...])

    spec = pl.BlockSpec((tr, tc), lambda i, j: (i, j))
    return pl.pallas_call(
        body, grid=(r // tr, c // tc),
        in_specs=[pl.BlockSpec((NDEV, tr, tc), lambda i, j: (0, i, j))] + [spec] * 3, out_specs=[spec] * 4,
        out_shape=[SDS((r, c), F32)] * 4, compiler_params=_params(("parallel", "parallel")), name=name,
    )(parts, w, m, v)


ANY = pl.BlockSpec(memory_space=pl.ANY)
MESH = pl.DeviceIdType.MESH


def _all_gather(xs, name, after=None):
    n = len(xs)
    extra = [] if after is None else [after]

    def body(*refs):
        x_refs, out_refs = refs[:n], refs[n + len(extra):2 * n + len(extra)]
        send_sems, recv_sems, local_sems = refs[-3:]
        mx, my, mc = lax.axis_index("x"), lax.axis_index("y"), lax.axis_index("c")
        me, sibling = (mx, my, mc), (mx, my, 1 - mc)
        chips = [(1 - mx, my), (mx, 1 - my), (1 - mx, 1 - my)]

        def rows(a, px, py, pc):
            return out_refs[a].at[4 * px + 2 * py + pc]

        def copy(a, k, block, to, src=None):
            return pltpu.make_async_remote_copy(
                src_ref=rows(a, *block) if src is None else src, dst_ref=rows(a, *block),
                send_sem=send_sems.at[a, k], recv_sem=recv_sems.at[a, k], device_id=to, device_id_type=MESH)

        mine = [pltpu.make_async_copy(x_refs[a], rows(a, *me), local_sems.at[a]) for a in range(n)]
        for cp in mine:
            cp.start()
        first = []
        for a in range(n):
            first.append(copy(a, 0, me, sibling, src=x_refs[a]))
            first += [copy(a, 1 + j, me, (*chip, mc), src=x_refs[a]) for j, chip in enumerate(chips)]
        for cp in first:
            cp.start()
        passed = []
        for a in range(n):
            for j, chip in enumerate(chips):
                copy(a, 1 + j, (*chip, mc), me).wait_recv()
                passed.append(copy(a, 4 + j, (*chip, mc), sibling))
                passed[-1].start()
        for a in range(n):
            copy(a, 0, sibling, me).wait_recv()
            for j, chip in enumerate(chips):
                copy(a, 4 + j, (*chip, 1 - mc), me).wait_recv()
        for cp in first + passed:
            cp.wait_send()
        for cp in mine:
            cp.wait()

    return pl.pallas_call(
        body, out_shape=[SDS((NDEV,) + x.shape, x.dtype) for x in xs], in_specs=[ANY] * (n + len(extra)),
        out_specs=[ANY] * n,
        scratch_shapes=[pltpu.SemaphoreType.DMA((n, 7)), pltpu.SemaphoreType.DMA((n, 7)),
                        pltpu.SemaphoreType.DMA((n,))],
        name=name,
    )(*xs, *extra)


HBM = pl.BlockSpec(memory_space=pltpu.HBM)
SEM = pl.BlockSpec(memory_space=pltpu.SEMAPHORE)
EFFECT = pltpu.SideEffectType.DATAFLOW_SIDE_EFFECTING


def _peers():
    mx, my, mc = lax.axis_index("x"), lax.axis_index("y"), lax.axis_index("c")
    out = []
    for k in range(1, NDEV):
        out.append((1 - mx if k & 4 else mx, 1 - my if k & 2 else my, 1 - mc if k & 1 else mc))
    return 4 * mx + 2 * my + mc, out


NEAR = (0, 1, 3, 5)


def _push_start(srcs, sliced, name, after=None, near=()):
    n = len(srcs)
    extra = [] if after is None else [after]
    lands = [lax.empty(s.shape if sliced else (NDEV,) + s.shape, s.dtype) for s in srcs]

    def body(*refs):
        src_refs, land_refs = refs[:n], refs[n:2 * n]
        outs = refs[2 * n + len(extra):]
        send_sems, recv_sems = outs[:n], outs[n:2 * n]
        token = refs[-1]
        me, peers = _peers()
        for a in range(n):
            for k, (px, py, pc) in enumerate(peers):
                if a in near and k not in NEAR:
                    continue
                src = src_refs[a].at[4 * px + 2 * py + pc] if sliced else src_refs[a]
                pltpu.make_async_remote_copy(
                    src_ref=src, dst_ref=land_refs[a].at[me], send_sem=send_sems[a].at[k],
                    recv_sem=recv_sems[a].at[k], device_id=(px, py, pc), device_id_type=MESH).start()
            pltpu.make_async_copy(src_refs[a].at[me] if sliced else src_refs[a], land_refs[a].at[me],
                                  send_sems[a].at[NDEV - 1]).start()
        token[...] = jnp.zeros_like(token)

    outs = pl.pallas_call(
        body, name=name,
        out_shape=([pltpu.SemaphoreType.DMA((NDEV,))] * n + [pltpu.SemaphoreType.DMA((NDEV - 1,))] * n
                   + [pltpu.HBM(s.shape, s.dtype) for s in srcs] + [pltpu.HBM(l.shape, l.dtype) for l in lands]
                   + [SDS((8, 128), F32)]),
        in_specs=[HBM] * (2 * n) + [pl.BlockSpec(memory_space=pl.ANY)] * len(extra),
        out_specs=[SEM] * (2 * n) + [HBM] * (2 * n) + [pl.BlockSpec(memory_space=pltpu.VMEM)],
        input_output_aliases={i: 2 * n + i for i in range(2 * n)},
        compiler_params=pltpu.CompilerParams(has_side_effects=EFFECT),
    )(*[pltpu.with_memory_space_constraint(s, pltpu.HBM) for s in srcs],
      *[pltpu.with_memory_space_constraint(l, pltpu.HBM) for l in lands], *extra)
    sends, recvs = outs[:n], outs[n:2 * n]
    src_thru, land_thru = outs[2 * n:3 * n], outs[3 * n:4 * n]
    return [(sends[a], recvs[a], src_thru[a], land_thru[a]) for a in range(n)], outs[-1]


def _push_wait(started, sliced, after, name, near=()):
    n = len(started)
    afters = list(after) if isinstance(after, (list, tuple)) else [after]

    def body(*refs):
        src_refs, land_refs = refs[:n], refs[n:2 * n]
        send_sems, recv_sems = refs[2 * n:3 * n], refs[3 * n:4 * n]
        me, peers = _peers()
        for a in range(n):
            for k, (px, py, pc) in enumerate(peers):
                if a in near and k not in NEAR:
                    continue
                src = src_refs[a].at[4 * px + 2 * py + pc] if sliced else src_refs[a]
                cp = pltpu.make_async_remote_copy(
                    src_ref=src, dst_ref=land_refs[a].at[me], send_sem=send_sems[a].at[k],
                    recv_sem=recv_sems[a].at[k], device_id=(px, py, pc), device_id_type=MESH)
                cp.wait_send()
                cp.wait_recv()
            pltpu.make_async_copy(src_refs[a].at[me] if sliced else src_refs[a], land_refs[a].at[me],
                                  send_sems[a].at[NDEV - 1]).wait()

    srcs = [s[2] for s in started]
    lands = [s[3] for s in started]
    outs = pl.pallas_call(
        body, name=name,
        out_shape=[pltpu.HBM(s.shape, s.dtype) for s in srcs] + [pltpu.HBM(l.shape, l.dtype) for l in lands],
        in_specs=[HBM] * (2 * n) + [SEM] * (2 * n) + [pl.BlockSpec(memory_space=pl.ANY)] * len(afters),
        out_specs=[HBM] * (2 * n),
        input_output_aliases={i: i for i in range(2 * n)},
        compiler_params=pltpu.CompilerParams(has_side_effects=EFFECT),
    )(*srcs, *lands, *[s[0] for s in started], *[s[1] for s in started], *afters)
    return outs[n:]


def _relay_to_sibling(land, name):
    def body(_, land_ref, send_sems, recv_sems):
        mx, my, mc = lax.axis_index("x"), lax.axis_index("y"), lax.axis_index("c")
        chips = [(1 - mx, my), (mx, 1 - my), (1 - mx, 1 - my)]

        def copy(j, core):
            slot = land_ref.at[4 * chips[j][0] + 2 * chips[j][1] + core]
            return pltpu.make_async_remote_copy(
                src_ref=slot, dst_ref=slot, send_sem=send_sems.at[j], recv_sem=recv_sems.at[j],
                device_id=(mx, my, 1 - mc), device_id_type=MESH)

        mine = [copy(j, mc) for j in range(3)]
        for cp in mine:
            cp.start()
        for j in range(3):
            copy(j, 1 - mc).wait_recv()
        for cp in mine:
            cp.wait_send()

    return pl.pallas_call(
        body, out_shape=SDS(land.shape, land.dtype), in_specs=[ANY], out_specs=ANY, input_output_aliases={0: 0},
        scratch_shapes=[pltpu.SemaphoreType.DMA((3,)), pltpu.SemaphoreType.DMA((3,))], name=name,
    )(land)


def _cols_from_blocks(blocks):
    _, rows, w = blocks.shape
    return blocks.transpose(1, 0, 2).reshape(rows, NDEV * w)


def _cols_to_blocks(full):
    rows, total = full.shape
    return full.reshape(rows, NDEV, total // NDEV).transpose(1, 0, 2)


def _mix_pad(wt):
    xp, q, k, v, z, ba, gp, gd = jnp.split(wt, (512, 1536, 2560, 3584, 4608, 4624, 5648), axis=0)
    pad = jnp.zeros((MIXP - OFF_BA - 16, wt.shape[1]), wt.dtype)
    return jnp.concatenate([q, k, v, z, gp, gd, xp, ba, pad], axis=0)


def _mix_unpad(wt):
    q, k, v, z, gp, gd, xp, ba = (wt[OFF_Q:OFF_K], wt[OFF_K:OFF_V], wt[OFF_V:OFF_Z], wt[OFF_Z:OFF_GP],
                                  wt[OFF_GP:OFF_GD], wt[OFF_GD:OFF_XP], wt[OFF_XP:OFF_BA], wt[OFF_BA:OFF_BA + 16])
    return jnp.concatenate([xp, q, k, v, z, ba, gp, gd], axis=0)


def _lane_row(vec8):
    return jnp.zeros((1, 128), F32).at[0, NH:2 * NH].set(vec8)


def _ffn_fwd(x, h, gate, w_in, w_out, tag, next_norm=None, token=None, start_more=None):
    if isinstance(w_in, tuple):
        w_in, = _push_wait([w_in], False, h, f"{tag}_gather_wait_in")
    w_in = w_in.reshape(2 * FH, D)
    u, a = _swiglu_up(h, w_in, f"{tag}_up", after=token)
    w_out, = _push_wait([w_out], False, a, f"{tag}_gather_wait_out")
    w_out = w_out.reshape(FH, D)
    outs = _matmul_residual(a, w_out, x, gate, 0.5, a_blk=True, norm=next_norm, name=f"{tag}_down",
                            after=None if start_more is None else start_more(h))
    return outs[0], (h, u, a, outs[1]), w_in, w_out, (outs[2] if next_norm else None)


def _ffn_bwd(dx_out, dy, x, g, scale, w_in, w_out, saved, tag, below=None):
    h, u, a, _ = saved
    t = x.shape[0]
    dw_out = _matmul(a, dy, ta=True, a_blk=True, out_dtype=BF16, name=f"{tag}_down_dw")
    sent_out, token = _push_start([dw_out.reshape(NDEV, FH // NDEV, D)], True, f"{tag}_grad_start_out")
    du = _swiglu_down_bwd(dy, w_out, u, f"{tag}_down_dx", after=token).reshape(NDEV, t, FB)
    dw_in = _matmul(du, h, ta=True, a_blk=True, out_dtype=BF16, name=f"{tag}_up_dw")
    sent_in, token = _push_start([dw_in.reshape(NDEV, FB, D)], True, f"{tag}_grad_start_in")
    dh = _matmul(du, w_in, a_blk=True, out_dtype=F32, name=f"{tag}_up_dx", after=token)
    return _norm_mod_bwd(x, g, scale, dh, dx_out, f"{tag}_norm_bwd", below), sent_in + sent_out


def kernel(x, c, ada_w, ada_b, norm_g, ffn1_w_in, ffn1_w_out, ffn2_w_in, ffn2_w_out, mix_w_in, conv_w, a_log, dt_bias, dn_norm_g, pool_w, pool_scale, pool_proj, dn_proj, mix_w_out, final_g, loss_target, m_ada_w, m_ada_b, m_norm_g, m_ffn1_w_in, m_ffn1_w_out, m_ffn2_w_in, m_ffn2_w_out, m_mix_w_in, m_conv_w, m_a_log, m_dt_bias, m_dn_norm_g, m_pool_w, m_pool_scale, m_pool_proj, m_dn_proj, m_mix_w_out, m_final_g, v_ada_w, v_ada_b, v_norm_g, v_ffn1_w_in, v_ffn1_w_out, v_ffn2_w_in, v_ffn2_w_out, v_mix_w_in, v_conv_w, v_a_log, v_dt_bias, v_dn_norm_g, v_pool_w, v_pool_scale, v_pool_proj, v_dn_proj, v_mix_w_out, v_final_g):
    me = 4 * lax.axis_index("x") + 2 * lax.axis_index("y") + lax.axis_index("c")
    x0 = x[0]
    target = loss_target[0]
    t = x0.shape[0]

    big = [ffn1_w_in[0], ffn1_w_out[0], ffn2_w_in[0], ffn2_w_out[0], mix_w_in[0], pool_proj[0], dn_proj[0],
           mix_w_out[0]]
    small = jnp.concatenate([c.reshape(8, 128), conv_w[0].reshape(12, 128), norm_g[0].reshape(3, 128),
                             jnp.zeros((1, 128), F32)], axis=0)
    small_all, = _all_gather([small], "gather_small")
    c_all = small_all[:, 0:8, :].reshape(NDEV, D)
    conv_full = small_all[:, 8:20, :].reshape(NDEV, 4, 384).transpose(1, 0, 2).reshape(4, 3 * D)
    norm_full = small_all[:, 20:23, :].reshape(NDEV, 3, 128).transpose(1, 0, 2).reshape(3, D)

    ncol = ada_w.shape[2]
    ada_b_mine = lax.dynamic_slice(ada_b, (0, me * ncol), (1, ncol))
    mod_cols = _ada_fwd(c_all, ada_w[0], ada_b_mine, "ada_fwd")
    transposed = (0, 2, 4)
    payload = [(w.T if i in transposed else w).astype(BF16) for i, w in enumerate(big)]
    mod_all, w_in1 = _all_gather([mod_cols, payload[0]], "gather_mod_first_weight")
    started, token = _push_start([payload[1], payload[4]], False, "gather_start", after=mod_all, near=(1,))
    started = {1: started[0], 4: started[1]}

    def start_rest(h):
        more, token = _push_start([payload[i] for i in (5, 6, 7, 2, 3)], False, "gather_start_rest", after=h)
        started.update(zip((5, 6, 7, 2, 3), more))
        return token

    mod = lax.dynamic_index_in_dim(mod_all, me, axis=1, keepdims=False).reshape(9, D)
    shift = [mod[3 * s:3 * s + 1] for s in range(3)]
    scale = [mod[3 * s + 1:3 * s + 2] for s in range(3)]
    gate = [mod[3 * s + 2:3 * s + 3] for s in range(3)]
    ng = [norm_full[s:s + 1] for s in range(3)]
    fg = final_g.reshape(1, D)
    al_row = _lane_row(a_log[0])
    dt_row = _lane_row(dt_bias[0])
    gn = dn_norm_g
    pw = pool_w[0]
    ps = pool_scale

    h0 = _norm_mod_fwd(x0, ng[0], shift[0], scale[0], "ffn1_norm", after=token)
    x1, saved1, w_in1, w_out1, h1 = _ffn_fwd(x0, h0, gate[0], w_in1, started[1], "ffn1",
                                             (ng[1], shift[1], scale[1]), token, start_rest)

    seg, = _push_wait([started[4]], False, h1, "mix_gather_wait", near=(0,))
    w_mix = _mix_pad(_relay_to_sibling(seg, "mix_gather_relay").reshape(MIX_RAW, D))
    proj = _matmul(h1, w_mix, tb=True, out_dtype=F32, name="mix_in")
    qh, kh, vh, bg = _dn_pre_fwd(proj, conv_full, al_row, dt_row, "dn_pre")
    seg = _push_wait([started[i] for i in (5, 6, 7)], False, qh, "mix_gather_wait_rest")
    w_pp = _cols_from_blocks(seg[0])
    w_dn = seg[1].reshape(D, D)
    w_mo = seg[2].reshape(D, D)
    ya = _pool_fwd(proj, pw, ps, w_pp, "pool_fwd")
    u, w, qk, qd, kd, eg, inv = _dn_local_fwd(qh, kh, vh, bg, "dn_local")
    o, s_saved = _dn_scan_fwd(u, w, qk, qd, kd, eg, "dn_scan")
    ob = _dn_post_fwd(o, proj, gn, "dn_post")
    yb = _matmul(ob, w_dn, out_dtype=F32, name="dn_out")
    merged = _merge_fwd(ya, yb, proj, "merge")
    x2, mix_y, h2 = _matmul_residual(merged, w_mo, x1, gate[1], 1.0, norm=(ng[2], shift[2], scale[2]),
                                     name="mix_out")

    x3, saved2, w_in2, w_out2, _ = _ffn_fwd(x2, h2, gate[2], started[2], started[3], "ffn2")
    loss_row, dx3, dfg, dy2, dgate2 = _final_loss(x3, fg, target, (saved2[3], gate[2], 0.5), "loss")

    (dx2, dsh2, dsc2, dng2, dmy, dgate1), sent2 = _ffn_bwd(dx3, dy2, x2, ng[2], scale[2], w_in2, w_out2, saved2,
                                                           "ffn2", (mix_y, gate[1], 1.0))

    dmerged = _matmul(dmy, w_mo, tb=True, out_dtype=BF16, name="mix_out_dx")
    dw_mo = _matmul(merged, dmy, ta=True, out_dtype=BF16, name="mix_out_dw")
    dproj = lax.empty((t, MIXP), BF16)
    dya, dyb, dproj = _merge_bwd(dmerged, ya, yb, proj, dproj, "merge_bwd")
    dob = _matmul(dyb, w_dn, tb=True, out_dtype=F32, name="dn_out_dx")
    dw_dn = _matmul(ob, dyb, ta=True, out_dtype=BF16, name="dn_out_dw")
    do, dproj, dgn = _dn_post_bwd(o, proj, gn, dob, dproj, "dn_post_bwd")
    du, dw, dqk, dqd, dkd, deg = _dn_scan_bwd(u, w, qk, qd, kd, eg, s_saved, do, "dn_scan_bwd")
    dqh, dkh, dvh, dbg = _dn_local_bwd(qh, kh, vh, bg, inv, du, dw, dqk, dqd, dkd, deg, "dn_local_bwd")
    dconv, dproj, dal, ddt = _dn_pre_bwd_act(proj, conv_full, al_row, dt_row, dqh, dkh, dvh, dbg, dproj,
                                             "dn_pre_bwd_act")
    dproj, dcw = _dn_pre_bwd_conv(proj, conv_full, dconv, dproj, "dn_pre_bwd_conv")
    dwin, dpl, dpw, dps, dpp = _pool_bwd_local(proj, pw, ps, w_pp, dya, "pool_bwd_local")
    dproj = _pool_bwd_window(dwin, dpl, dproj, "pool_bwd_window")
    dw_mix = _matmul(dproj, h1, ta=True, out_dtype=BF16, name="mix_in_dw")
    sent1, token = _push_start(
        [_mix_unpad(dw_mix).reshape(NDEV, MIX_RAW // NDEV, D), _cols_to_blocks(dpp.astype(BF16)),
         dw_dn.reshape(NDEV, -1, D), dw_mo.reshape(NDEV, -1, D)], True, "mix_grad_start")
    dh1 = _matmul(dproj, w_mix, out_dtype=F32, name="mix_in_dx", after=token)
    dx1, dsh1, dsc1, dng1, dy0, dgate0 = _norm_mod_bwd(x1, ng[1], scale[1], dh1, dx2, "mix_norm_bwd",
                                                       (saved1[3], gate[0], 0.5))

    (dx0, dsh0, dsc0, dng0), sent0 = _ffn_bwd(dx1, dy0, x0, ng[0], scale[0], w_in1, w_out1, saved1, "ffn1")

    dmod = jnp.concatenate([dsh0, dsc0, dgate0, dsh1, dsc1, dgate1, dsh2, dsc2, dgate2], axis=1).reshape(-1)
    flat = jnp.concatenate([
        dmod, dal[0, NH:2 * NH], ddt[0, NH:2 * NH], dgn.reshape(-1), dps.reshape(-1), dfg.reshape(-1),
        dpw.reshape(-1), jnp.concatenate([dng0, dng1, dng2], axis=0).reshape(-1), dcw.reshape(-1),
        loss_row[0, 0:1]])
    nflat = 90 * D
    flat = jnp.concatenate([flat, jnp.zeros((nflat - flat.shape[0],), F32)]).reshape(90, D)
    sent_small, small_token = _push_start([flat], False, "small_grad_start")

    def small_grads(flat_all):
        tot = _sum_devices(flat_all, F32, "sum_small_grads").reshape(-1)
        dmod_all = flat_all.reshape(NDEV, nflat)[:, :9 * D]
        dmod_cols = lax.dynamic_slice(dmod_all, (0, me * ncol), (NDEV, ncol))
        g_ada_w = _ada_bwd(c_all.T, dmod_cols, "ada_bwd")
        p = 0
        pieces = {}
        for nm, size in (("ada_b", 9 * D), ("a_log", NH), ("dt_bias", NH), ("dn_norm_g", HD), ("pool_scale", PW),
                         ("final_g", D), ("pool_w", 4 * PG * PG), ("norm_g", 3 * D), ("conv_w", 12 * D),
                         ("loss", 1)):
            pieces[nm] = tot[p:p + size]
            p += size
        g_norm = lax.dynamic_slice(pieces["norm_g"].reshape(3, D), (0, me * 128), (3, 128))
        g_conv = lax.dynamic_slice(pieces["conv_w"].reshape(4, 3 * D), (0, me * 384), (4, 384))
        return pieces["loss"][0], {
            "ada_w": g_ada_w.reshape(ada_w.shape), "ada_b": pieces["ada_b"].reshape(ada_b.shape),
            "norm_g": g_norm.reshape(norm_g.shape), "conv_w": g_conv.reshape(conv_w.shape),
            "a_log": pieces["a_log"].reshape(a_log.shape), "dt_bias": pieces["dt_bias"].reshape(dt_bias.shape),
            "dn_norm_g": pieces["dn_norm_g"].reshape(dn_norm_g.shape),
            "pool_w": pieces["pool_w"].reshape(pool_w.shape),
            "pool_scale": pieces["pool_scale"].reshape(pool_scale.shape),
            "final_g": pieces["final_g"].reshape(final_g.shape),
        }

    grads = {}
    weights = {"ada_w": ada_w, "ada_b": ada_b, "norm_g": norm_g, "ffn1_w_in": ffn1_w_in, "ffn1_w_out": ffn1_w_out,
               "ffn2_w_in": ffn2_w_in, "ffn2_w_out": ffn2_w_out, "mix_w_in": mix_w_in, "conv_w": conv_w,
               "a_log": a_log, "dt_bias": dt_bias, "dn_norm_g": dn_norm_g, "pool_w": pool_w,
               "pool_scale": pool_scale, "pool_proj": pool_proj, "dn_proj": dn_proj, "mix_w_out": mix_w_out,
               "final_g": final_g}
    m_in = {"ada_w": m_ada_w, "ada_b": m_ada_b, "norm_g": m_norm_g, "ffn1_w_in": m_ffn1_w_in,
            "ffn1_w_out": m_ffn1_w_out, "ffn2_w_in": m_ffn2_w_in, "ffn2_w_out": m_ffn2_w_out,
            "mix_w_in": m_mix_w_in, "conv_w": m_conv_w, "a_log": m_a_log, "dt_bias": m_dt_bias,
            "dn_norm_g": m_dn_norm_g, "pool_w": m_pool_w, "pool_scale": m_pool_scale, "pool_proj": m_pool_proj,
            "dn_proj": m_dn_proj, "mix_w_out": m_mix_w_out, "final_g": m_final_g}
    v_in = {"ada_w": v_ada_w, "ada_b": v_ada_b, "norm_g": v_norm_g, "ffn1_w_in": v_ffn1_w_in,
            "ffn1_w_out": v_ffn1_w_out, "ffn2_w_in": v_ffn2_w_in, "ffn2_w_out": v_ffn2_w_out,
            "mix_w_in": v_mix_w_in, "conv_w": v_conv_w, "a_log": v_a_log, "dt_bias": v_dt_bias,
            "dn_norm_g": v_dn_norm_g, "pool_w": v_pool_w, "pool_scale": v_pool_scale, "pool_proj": v_pool_proj,
            "dn_proj": v_dn_proj, "mix_w_out": v_mix_w_out, "final_g": v_final_g}

    names = list(weights)
    large = ("ada_w", "ffn1_w_in", "ffn1_w_out", "ffn2_w_in", "ffn2_w_out", "mix_w_in", "pool_proj", "dn_proj",
             "mix_w_out")
    delta, new_m, new_v = {}, {}, {}

    flipped = ("ffn1_w_in", "ffn2_w_in", "mix_w_in")

    def views(nm):
        shp = weights[nm].shape
        two_d = (shp[-2], shp[-1])
        if nm in flipped:
            return (lambda a: a.reshape(two_d).T), (lambda a: a.T.reshape(shp))
        return (lambda a: a.reshape(two_d)), (lambda a: a.reshape(shp))

    def reduce_update(sent, group, after, tag):
        done = []
        for nm, r in zip(group, _push_wait(sent, True, after, f"{tag}_grad_wait")):
            view, back = views(nm)
            g_, d_, m_, v_ = _reduce_adamw(r, view(weights[nm]), view(m_in[nm]), view(v_in[nm]), f"adamw_{nm}")
            grads[nm], delta[nm], new_m[nm], new_v[nm] = back(g_), back(d_), back(m_), back(v_)
            done.append(d_)
        return done

    done = reduce_update(sent2, ("ffn2_w_in", "ffn2_w_out"), small_token, "ffn2")
    done += reduce_update(sent1, ("mix_w_in", "pool_proj", "dn_proj", "mix_w_out"), done, "mix")
    flat_all, = _push_wait(sent_small, False, done, "small_grad_wait")
    loss, small = small_grads(flat_all)
    grads.update(small)
    view, back = views("ada_w")
    done, m_, v_ = _adamw(view(ada_w), view(grads["ada_w"]), view(m_ada_w), view(v_ada_w), "adamw_ada_w")
    delta["ada_w"], new_m["ada_w"], new_v["ada_w"] = back(done), back(m_), back(v_)
    reduce_update(sent0, ("ffn1_w_in", "ffn1_w_out"), done, "ffn1")
    rest = [nm for nm in names if nm not in large]
    total = sum(weights[nm].size for nm in rest)
    padded = -(-total // D) * D

    def pack(tree, fill):
        flat_ = jnp.concatenate([tree[nm].reshape(-1) for nm in rest])
        return jnp.concatenate([flat_, jnp.full((padded - total,), fill, F32)]).reshape(-1, D)

    d_, m_, v_ = _adamw(pack(weights, 0.0), pack(grads, 0.0), pack(m_in, 0.0), pack(v_in, 1.0), "adamw_small")
    p = 0
    for nm in rest:
        size = weights[nm].size
        shp = weights[nm].shape
        delta[nm] = d_.reshape(-1)[p:p + size].reshape(shp)
        new_m[nm] = m_.reshape(-1)[p:p + size].reshape(shp)
        new_v[nm] = v_.reshape(-1)[p:p + size].reshape(shp)
        p += size

    grad_x = dx0.reshape(x.shape)
    return (loss, grad_x, *[grads[nm] for nm in names], *[delta[nm] for nm in names],
            *[new_m[nm] for nm in names], *[new_v[nm] for nm in names])
```

```python
import functools

import jax
import jax.numpy as jnp
from jax import lax
from jax.experimental import pallas as pl
from jax.experimental.pallas import tpu as pltpu

F32 = jnp.float32
BF16 = jnp.bfloat16
SDS = jax.ShapeDtypeStruct
HI = lax.Precision.HIGHEST

D = 1024
FH = 2816
FB = 704
NH = 8
HD = 128
CH = 64
SCAN_CHUNKS = 4
NDEV = 8
PW = 512
PG = 128
RMS_EPS = 1e-6
L2_EPS = 1e-6
TR = 512
HALO = 16
VMEM_LIMIT = 56 * 1024 * 1024
MATMUL_VMEM = 40 * 1024 * 1024

MIXP = 6912
OFF_Q, OFF_K, OFF_V, OFF_Z, OFF_GP, OFF_GD, OFF_XP, OFF_BA = 0, 1024, 2048, 3072, 4096, 5120, 6144, 6656
MIX_RAW = 6672

ADAM_LR = 0.001
ADAM_B1 = 0.9
ADAM_B2 = 0.999
ADAM_EPS = 1e-08
ADAM_WD = 0.01
ADAM_STEP = 10

NN = (((1,), (0,)), ((), ()))
NT = (((1,), (1,)), ((), ()))
TN = (((0,), (0,)), ((), ()))


def _dg(a, b, dims, prec=None):
    return lax.dot_general(a, b, dims, precision=prec, preferred_element_type=F32)


def _make_dots(prec):
    @jax.custom_vjp
    def nn(a, b):
        return _dg(a, b, NN, prec)

    @jax.custom_vjp
    def nt(a, b):
        return _dg(a, b, NT, prec)

    @jax.custom_vjp
    def tn(a, b):
        return _dg(a, b, TN, prec)

    nn.defvjp(lambda a, b: (nn(a, b), (a, b)), lambda r, d: (nt(d, r[1]), tn(r[0], d)))
    nt.defvjp(lambda a, b: (nt(a, b), (a, b)), lambda r, d: (nn(d, r[1]), tn(d, r[0])))
    tn.defvjp(lambda a, b: (tn(a, b), (a, b)), lambda r, d: (nt(r[1], d), nn(r[0], d)))
    return nn, nt, tn


_nn, _nt, _tn = _make_dots(None)


def _params(sem):
    return pltpu.CompilerParams(dimension_semantics=sem, vmem_limit_bytes=VMEM_LIMIT)


def _sigmoid(x):
    return 1.0 / (1.0 + jnp.exp(-x))


def _silu(x):
    return x * _sigmoid(x)


def _dsilu(x):
    s = _sigmoid(x)
    return s * (1.0 + x * (1.0 - s))


def _pick(n, cands):
    for c in cands:
        if n % c == 0:
            return c
    raise ValueError(f"no tile for {n}")


def _iota(shape, dim):
    return lax.broadcasted_iota(jnp.int32, shape, dim)


def _matmul(a, b, *, ta=False, tb=False, a_blk=False, b_blk=False, o_blk=False, tm=None, tn=None, tk=None,
            out_dtype, name, after=None):
    if a_blk:
        nb, r, cb = a.shape
        if ta:
            k_dim, m_dim, tm = r, nb * cb, cb
        else:
            m_dim, k_dim, tk = r, nb * cb, cb
    else:
        k_dim, m_dim = a.shape if ta else a.shape[::-1]
    if b_blk:
        nb, r, cb = b.shape
        if tb:
            n_dim, tk = r, cb
            assert nb * cb == k_dim
        else:
            n_dim, tn = nb * cb, cb
            assert r == k_dim
    else:
        n_dim = b.shape[0] if tb else b.shape[1]
    tn = tn or _pick(n_dim, (1024, 768, 512, 256, 128))
    out_bytes = jnp.dtype(out_dtype).itemsize

    def vmem(tm_, tk_):
        return 4 * tk_ * (tm_ + tn) + tm_ * tn * (4 + 2 * out_bytes)

    k_cands = [tk] if tk else [c for c in (k_dim, 4096, 3456, 2816, 2304, 2048, 1024, 512, 256)
                               if c <= k_dim and k_dim % c == 0]
    m_cands = [tm] if tm else [c for c in (2048, 1024, 768, 512, 256, 128) if m_dim % c == 0]
    base = next((c for c in m_cands if c <= 1024), m_cands[-1])
    tk = next((c for c in k_cands if vmem(base, c) <= MATMUL_VMEM), k_cands[-1])
    tm = next((c for c in m_cands if vmem(c, tk) <= MATMUL_VMEM), m_cands[-1])
    nk = k_dim // tk
    dims = ((((0,) if ta else (1,)), ((1,) if tb else (0,))), ((), ()))

    def body(a_ref, b_ref, *rest):
        o_ref, acc_ref = rest[-2:]
        k = pl.program_id(2)

        @pl.when(k == 0)
        def _():
            acc_ref[...] = jnp.zeros_like(acc_ref)

        acc_ref[...] += lax.dot_general(a_ref[...].astype(BF16), b_ref[...].astype(BF16), dims,
                                        preferred_element_type=F32)

        @pl.when(k == nk - 1)
        def _():
            o_ref[...] = acc_ref[...].astype(o_ref.dtype)

    if a_blk:
        a_spec = (pl.BlockSpec((None, tk, tm), lambda i, j, k: (i, k, 0)) if ta
                  else pl.BlockSpec((None, tm, tk), lambda i, j, k: (k, i, 0)))
    else:
        a_spec = (pl.BlockSpec((tk, tm), lambda i, j, k: (k, i)) if ta
                  else pl.BlockSpec((tm, tk), lambda i, j, k: (i, k)))
    if b_blk:
        b_spec = (pl.BlockSpec((None, tn, tk), lambda i, j, k: (k, j, 0)) if tb
                  else pl.BlockSpec((None, tk, tn), lambda i, j, k: (j, k, 0)))
    else:
        b_spec = (pl.BlockSpec((tn, tk), lambda i, j, k: (j, k)) if tb
                  else pl.BlockSpec((tk, tn), lambda i, j, k: (k, j)))
    if o_blk:
        o_spec = pl.BlockSpec((None, tm, tn), lambda i, j, k: (j, i, 0))
        o_shape = SDS((n_dim // tn, m_dim, tn), out_dtype)
    else:
        o_spec = pl.BlockSpec((tm, tn), lambda i, j, k: (i, j))
        o_shape = SDS((m_dim, n_dim), out_dtype)
    return pl.pallas_call(
        body, grid=(m_dim // tm, n_dim // tn, nk),
        in_specs=[a_spec, b_spec] + ([] if after is None else [pl.BlockSpec(memory_space=pl.ANY)]),
        out_specs=o_spec,
        out_shape=o_shape,
        scratch_shapes=[pltpu.VMEM((tm, tn), F32)],
        compiler_params=_params(("parallel", "parallel", "arbitrary")),
        name=name,
    )(a, b, *([] if after is None else [after]))


def _matmul_residual(a, b, x, gate, coef, *, a_blk=False, norm=None, name, after=None):
    if a_blk:
        nb, m_dim, tk = a.shape
        nk = nb
        a_spec = pl.BlockSpec((None, 512, tk), lambda i, k: (k, i, 0))
    else:
        m_dim, tk = a.shape
        nk = 1
        a_spec = pl.BlockSpec((512, tk), lambda i, k: (i, 0))
    tm = 512
    extra = [] if after is None else [after]
    vecs = [gate] + (list(norm) if norm else [])

    def body(a_ref, b_ref, x_ref, gate_ref, *rest):
        vec_refs = rest[:len(vecs) - 1]
        outs = rest[len(vecs) - 1 + len(extra):]
        acc_ref = outs[-1]
        k = pl.program_id(1)

        @pl.when(k == 0)
        def _():
            acc_ref[...] = jnp.zeros_like(acc_ref)

        acc_ref[...] += _dg(a_ref[...], b_ref[...], NN)

        @pl.when(k == nk - 1)
        def _():
            y = acc_ref[...]
            xn = x_ref[...] + (coef * gate_ref[...]) * y
            outs[0][...] = xn
            outs[1][...] = y.astype(outs[1].dtype)
            if norm:
                g_ref, sh_ref, sc_ref = vec_refs
                r = lax.rsqrt(jnp.mean(xn * xn, axis=-1, keepdims=True) + RMS_EPS)
                outs[2][...] = (((xn * r) * g_ref[...]) * (1.0 + sc_ref[...]) + sh_ref[...]).astype(outs[2].dtype)

    row = pl.BlockSpec((tm, D), lambda i, k: (i, 0))
    vec = pl.BlockSpec((1, D), lambda i, k: (0, 0))
    return pl.pallas_call(
        body, grid=(m_dim // tm, nk),
        in_specs=[a_spec, pl.BlockSpec((tk, D), lambda i, k: (k, 0)), row] + [vec] * len(vecs)
        + [pl.BlockSpec(memory_space=pl.ANY)] * len(extra),
        out_specs=[row] * (3 if norm else 2),
        out_shape=[SDS((m_dim, D), F32), SDS((m_dim, D), BF16)] + ([SDS((m_dim, D), BF16)] if norm else []),
        scratch_shapes=[pltpu.VMEM((tm, D), F32)],
        compiler_params=_params(("parallel", "arbitrary")), name=name,
    )(a, b, x, *vecs, *extra)


def _row(width, col=0):
    return pl.BlockSpec((TR, width), lambda i: (i, col))


def _vec(width):
    return pl.BlockSpec((1, width), lambda i: (0, 0))


def _norm_mod_fwd(x, g, shift, scale, name, after=None):
    t = x.shape[0]
    extra = [] if after is None else [after]

    def body(x_ref, g_ref, sh_ref, sc_ref, *rest):
        o_ref = rest[-1]
        xv = x_ref[...]
        r = lax.rsqrt(jnp.mean(xv * xv, axis=-1, keepdims=True) + RMS_EPS)
        o_ref[...] = (((xv * r) * g_ref[...]) * (1.0 + sc_ref[...]) + sh_ref[...]).astype(o_ref.dtype)

    return pl.pallas_call(
        body, grid=(t // TR,),
        in_specs=[_row(D), _vec(D), _vec(D), _vec(D)] + [pl.BlockSpec(memory_space=pl.ANY)] * len(extra),
        out_specs=_row(D),
        out_shape=SDS((t, D), BF16), compiler_params=_params(("parallel",)), name=name,
    )(x, g, shift, scale, *extra)


def _residual_branch_bwd(dxv, y_ref, gate_ref, coef, dy_ref, dgate_ref):
    dy_ref[...] = ((coef * gate_ref[...]) * dxv).astype(dy_ref.dtype)
    dgate_ref[...] += jnp.sum((coef * dxv) * y_ref[...], axis=0, keepdims=True)


def _norm_mod_bwd(x, g, scale, dh, dx_in, name, below=None):
    t = x.shape[0]
    lower = [] if below is None else list(below[:2])

    def body(x_ref, g_ref, sc_ref, dh_ref, dxi_ref, *rest):
        dx_ref, dsh_ref, dsc_ref, dg_ref = rest[len(lower):len(lower) + 4]

        @pl.when(pl.program_id(0) == 0)
        def _():
            for ref in rest[len(lower) + 1:]:
                if ref.shape[0] == 1:
                    ref[...] = jnp.zeros_like(ref)

        xv = x_ref[...]
        gv = g_ref[...]
        dh = dh_ref[...]
        r = lax.rsqrt(jnp.mean(xv * xv, axis=-1, keepdims=True) + RMS_EPS)
        n = xv * r
        dsh_ref[...] += jnp.sum(dh, axis=0, keepdims=True)
        dsc_ref[...] += jnp.sum(dh * (n * gv), axis=0, keepdims=True)
        tt = dh * (1.0 + sc_ref[...])
        dg_ref[...] += jnp.sum(tt * n, axis=0, keepdims=True)
        dn = tt * gv
        dxv = dxi_ref[...] + r * (dn - n * jnp.mean(dn * n, axis=-1, keepdims=True))
        dx_ref[...] = dxv
        if below is not None:
            _residual_branch_bwd(dxv, rest[0], rest[1], below[2], rest[-2], rest[-1])

    more_in = [] if below is None else [_row(D), _vec(D)]
    more_out = [] if below is None else [_row(D), _vec(D)]
    more_shape = [] if below is None else [SDS((t, D), BF16), SDS((1, D), F32)]
    return pl.pallas_call(
        body, grid=(t // TR,), in_specs=[_row(D), _vec(D), _vec(D), _row(D), _row(D)] + more_in,
        out_specs=[_row(D), _vec(D), _vec(D), _vec(D)] + more_out,
        out_shape=[SDS((t, D), F32), SDS((1, D), F32), SDS((1, D), F32), SDS((1, D), F32)] + more_shape,
        compiler_params=_params(("arbitrary",)), name=name,
    )(x, g, scale, dh, dx_in, *lower)


def _swiglu_up(h, w_in, name, after=None):
    t = h.shape[0]
    tm = _pick(t, (1024, 512, 256))
    half = NDEV // 2
    extra = [] if after is None else [after]

    def body(h_ref, wg_ref, wu_ref, *rest):
        u_ref, a_ref = rest[-2:]
        hv = h_ref[...]
        gate = _dg(hv, wg_ref[...], NT)
        up = _dg(hv, wu_ref[...], NT)
        u_ref[0] = gate.astype(u_ref.dtype)
        u_ref[1] = up.astype(u_ref.dtype)
        a_ref[...] = (_silu(gate) * up).astype(a_ref.dtype)

    return pl.pallas_call(
        body, grid=(t // tm, half),
        in_specs=[pl.BlockSpec((tm, D), lambda i, j: (i, 0)),
                  pl.BlockSpec((FB, D), lambda i, j: (j, 0)),
                  pl.BlockSpec((FB, D), lambda i, j: (j + half, 0))]
        + [pl.BlockSpec(memory_space=pl.ANY)] * len(extra),
        out_specs=[pl.BlockSpec((2, None, tm, FB), lambda i, j: (0, j, i, 0)),
                   pl.BlockSpec((None, tm, FB), lambda i, j: (j, i, 0))],
        out_shape=[SDS((2, half, t, FB), BF16), SDS((half, t, FB), BF16)],
        compiler_params=_params(("parallel", "parallel")), name=name,
    )(h, w_in, w_in, *extra)


def _swiglu_down_bwd(dy, w_out, u, name, after=None):
    t = dy.shape[0]
    tm = _pick(t, (1024, 512, 256))
    half = NDEV // 2
    extra = [] if after is None else [after]
    pair = pl.BlockSpec((2, None, tm, FB), lambda i, j: (0, j, i, 0))

    def body(dy_ref, w_ref, u_ref, *rest):
        o_ref = rest[-1]
        da = _dg(dy_ref[...], w_ref[...], NT)
        gate = u_ref[0].astype(F32)
        o_ref[0] = (da * u_ref[1].astype(F32) * _dsilu(gate)).astype(o_ref.dtype)
        o_ref[1] = (da * _silu(gate)).astype(o_ref.dtype)

    return pl.pallas_call(
        body, grid=(t // tm, half),
        in_specs=[pl.BlockSpec((tm, D), lambda i, j: (i, 0)), pl.BlockSpec((FB, D), lambda i, j: (j, 0)), pair]
        + [pl.BlockSpec(memory_space=pl.ANY)] * len(extra),
        out_specs=pair, out_shape=SDS((2, half, t, FB), BF16),
        compiler_params=_params(("parallel", "parallel")), name=name,
    )(dy, w_out, u, *extra)


def _final_loss(x, fg, target, below, name):
    t = x.shape[0]
    nt = t // TR

    def body(x_ref, g_ref, t_ref, y_ref, gate_ref, loss_ref, dx_ref, dg_ref, dy_ref, dgate_ref, acc_ref):
        i = pl.program_id(0)

        @pl.when(i == 0)
        def _():
            acc_ref[...] = jnp.zeros_like(acc_ref)
            dg_ref[...] = jnp.zeros_like(dg_ref)
            dgate_ref[...] = jnp.zeros_like(dgate_ref)

        xv = x_ref[...]
        gv = g_ref[...]
        r = lax.rsqrt(jnp.mean(xv * xv, axis=-1, keepdims=True) + RMS_EPS)
        n = xv * r
        err = n * gv - t_ref[...]
        acc_ref[...] += jnp.sum(err * err, axis=0, keepdims=True)
        dy = err * (1.0 / D)
        dg_ref[...] += jnp.sum(dy * n, axis=0, keepdims=True)
        dn = dy * gv
        dxv = r * (dn - n * jnp.mean(dn * n, axis=-1, keepdims=True))
        dx_ref[...] = dxv
        _residual_branch_bwd(dxv, y_ref, gate_ref, below[2], dy_ref, dgate_ref)

        @pl.when(i == nt - 1)
        def _():
            tot = jnp.sum(acc_ref[...], axis=1, keepdims=True) * (0.5 / D)
            loss_ref[...] = jnp.broadcast_to(tot, loss_ref.shape)

    return pl.pallas_call(
        body, grid=(nt,), in_specs=[_row(D), _vec(D), _row(D), _row(D), _vec(D)],
        out_specs=[_vec(128), _row(D), _vec(D), _row(D), _vec(D)],
        out_shape=[SDS((1, 128), F32), SDS((t, D), F32), SDS((1, D), F32), SDS((t, D), BF16), SDS((1, D), F32)],
        scratch_shapes=[pltpu.VMEM((1, D), F32)],
        compiler_params=_params(("arbitrary",)), name=name,
    )(x, fg, target, below[0], below[1])


def _halo_prev(width, col):
    per = TR // HALO
    return pl.BlockSpec((HALO, width), lambda i: (jnp.maximum(i * per - 1, 0), col))


def _halo_next(width, col, nt):
    per = TR // HALO
    return pl.BlockSpec((HALO, width), lambda i: (jnp.minimum((i + 1) * per, nt * per - 1), col))


def _pool_windows(ext, tile_index):
    rows = _iota((TR, PG), 0) + tile_index * TR + 1
    pooled, counts = [], []
    for gi in range(4):
        w = 2 << gi
        e = ext[:, gi * PG:(gi + 1) * PG]
        s = e
        step = 1
        while step < w:
            s = s + pltpu.roll(s, step, 0)
            step *= 2
        cnt = jnp.minimum(rows, w).astype(F32)
        pooled.append(s[HALO:] / cnt - e[HALO:])
        counts.append(cnt)
    return pooled, counts


def _pool_fwd(proj, pool_w, pool_scale, pool_proj, name):
    t = proj.shape[0]
    xcol = OFF_XP // PW

    def body(x_ref, h_ref, pw_ref, ps_ref, pp_ref, o_ref):
        i = pl.program_id(0)
        halo = jnp.where(i > 0, h_ref[...], 0.0)
        ext = jnp.concatenate([halo, x_ref[...]], axis=0)
        pooled, _ = _pool_windows(ext, i)
        mixed = [_dg(pooled[g].astype(BF16), pw_ref[g].astype(BF16), NN) for g in range(4)]
        ypre = jnp.concatenate(mixed, axis=1) * ps_ref[...]
        o_ref[...] = _dg(ypre.astype(BF16), pp_ref[...], NN)

    return pl.pallas_call(
        body, grid=(t // TR,),
        in_specs=[_row(PW, xcol), _halo_prev(PW, xcol),
                  pl.BlockSpec((4, PG, PG), lambda i: (0, 0, 0)), _vec(PW),
                  pl.BlockSpec((PW, D), lambda i: (0, 0))],
        out_specs=_row(D), out_shape=SDS((t, D), F32),
        compiler_params=_params(("parallel",)), name=name,
    )(proj, proj, pool_w, pool_scale, pool_proj)


def _pool_bwd_local(proj, pool_w, pool_scale, pool_proj, dya, name):
    t = proj.shape[0]
    xcol = OFF_XP // PW

    def body(x_ref, h_ref, pw_ref, ps_ref, pp_ref, dya_ref, dwin_ref, dpl_ref, dpw_ref, dps_ref, dpp_ref):
        i = pl.program_id(0)

        @pl.when(i == 0)
        def _():
            dpw_ref[...] = jnp.zeros_like(dpw_ref)
            dps_ref[...] = jnp.zeros_like(dps_ref)
            dpp_ref[...] = jnp.zeros_like(dpp_ref)

        halo = jnp.where(i > 0, h_ref[...], 0.0)
        ext = jnp.concatenate([halo, x_ref[...]], axis=0)
        pooled, counts = _pool_windows(ext, i)
        mixed = jnp.concatenate(
            [_dg(pooled[g].astype(BF16), pw_ref[g].astype(BF16), NN) for g in range(4)], axis=1)
        ps = ps_ref[...]
        ypre = mixed * ps
        dyab = dya_ref[...].astype(BF16)
        dypre = _dg(dyab, pp_ref[...], NT)
        dpp_ref[...] += _dg(ypre.astype(BF16), dyab, TN)
        dps_ref[...] += jnp.sum(dypre * mixed, axis=0, keepdims=True)
        dmixed = dypre * ps
        for g in range(4):
            dm = dmixed[:, g * PG:(g + 1) * PG].astype(BF16)
            dpw_ref[g] += _dg(pooled[g].astype(BF16), dm, TN)
            dpooled = _dg(dm, pw_ref[g].astype(BF16), NT)
            dwin_ref[:, g * PG:(g + 1) * PG] = dpooled / counts[g]
            dpl_ref[:, g * PG:(g + 1) * PG] = dpooled

    return pl.pallas_call(
        body, grid=(t // TR,),
        in_specs=[_row(PW, xcol), _halo_prev(PW, xcol),
                  pl.BlockSpec((4, PG, PG), lambda i: (0, 0, 0)), _vec(PW),
                  pl.BlockSpec((PW, D), lambda i: (0, 0)), _row(D)],
        out_specs=[_row(PW), _row(PW), pl.BlockSpec((4, PG, PG), lambda i: (0, 0, 0)), _vec(PW),
                   pl.BlockSpec((PW, D), lambda i: (0, 0))],
        out_shape=[SDS((t, PW), F32), SDS((t, PW), F32), SDS((4, PG, PG), F32), SDS((1, PW), F32),
                   SDS((PW, D), F32)],
        compiler_params=_params(("arbitrary",)), name=name,
    )(proj, proj, pool_w, pool_scale, pool_proj, dya)


def _pool_bwd_window(dwin, dpl, dproj, name):
    t = dwin.shape[0]
    nt = t // TR
    ext_rows = TR + HALO

    def body(dw_ref, h_ref, dp_ref, _, o_ref):
        i = pl.program_id(0)
        halo = jnp.where(i < nt - 1, h_ref[...], 0.0)
        ext = jnp.concatenate([dw_ref[...], halo], axis=0)
        for gi in range(4):
            w = 2 << gi
            s = ext[:, gi * PG:(gi + 1) * PG]
            step = 1
            while step < w:
                s = s + pltpu.roll(s, ext_rows - step, 0)
                step *= 2
            o_ref[:, gi * PG:(gi + 1) * PG] = (s[:TR] - dp_ref[:, gi * PG:(gi + 1) * PG]).astype(o_ref.dtype)

    return pl.pallas_call(
        body, grid=(nt,),
        in_specs=[_row(PW), _halo_next(PW, 0, nt), _row(PW), pl.BlockSpec(memory_space=pl.ANY)],
        out_specs=_into(PW, OFF_XP), out_shape=SDS(dproj.shape, dproj.dtype), input_output_aliases={3: 0},
        compiler_params=_params(("parallel",)), name=name,
    )(dwin, dwin, dpl, dproj)


def _conv_group(ext, cw_ref, cols):
    acc = cw_ref[3:4, cols] * ext
    for j in range(3):
        acc = acc + cw_ref[j:j + 1, cols] * pltpu.roll(ext, 3 - j, 0)
    return acc[HALO:]


def _gate_terms(raw, al, dt):
    beta = _sigmoid(raw)
    xg = raw + dt
    sp = jnp.maximum(xg, 0.0) + jnp.log(1.0 + jnp.exp(-jnp.abs(xg)))
    g = -jnp.exp(al) * sp
    return beta, g, _sigmoid(xg)


def _dn_pre_fwd(proj, conv_w, al_row, dt_row, name):
    t = proj.shape[0]

    def body(x_ref, h_ref, cw_ref, ba_ref, al_ref, dt_ref, q_ref, k_ref, v_ref, bg_ref):
        i = pl.program_id(0)
        keep = i > 0
        for grp in range(24):
            cols = slice(grp * HD, (grp + 1) * HD)
            ext = jnp.concatenate([jnp.where(keep, h_ref[:, cols], 0.0), x_ref[:, cols]], axis=0)
            s = _silu(_conv_group(ext, cw_ref, cols))
            seg, head = divmod(grp, NH)
            hc = slice(head * HD, (head + 1) * HD)
            if seg == 0:
                q_ref[:, hc] = s * lax.rsqrt(jnp.sum(s * s, axis=-1, keepdims=True) + L2_EPS) * (HD ** -0.5)
            elif seg == 1:
                k_ref[:, hc] = s * lax.rsqrt(jnp.sum(s * s, axis=-1, keepdims=True) + L2_EPS)
            else:
                v_ref[:, hc] = s
        lane = _iota((TR, 128), 1)
        rowc = _iota((TR, 128), 0) % CH
        beta, g, _ = _gate_terms(ba_ref[...], al_ref[...], dt_ref[...])
        step = 1
        while step < CH:
            g = g + jnp.where(rowc >= step, pltpu.roll(g, step, 0), 0.0)
            step *= 2
        bg_ref[...] = jnp.where(lane < NH, beta, jnp.where(lane < 2 * NH, g, 0.0))

    return pl.pallas_call(
        body, grid=(t // TR,),
        in_specs=[_row(3 * D, 0), _halo_prev(3 * D, 0), pl.BlockSpec((4, 3 * D), lambda i: (0, 0)),
                  _row(128, OFF_BA // 128), _vec(128), _vec(128)],
        out_specs=[_row(D), _row(D), _row(D), _row(128)],
        out_shape=[SDS((t, D), F32), SDS((t, D), F32), SDS((t, D), F32), SDS((t, 128), F32)],
        compiler_params=_params(("parallel",)), name=name,
    )(proj, proj, conv_w, proj, al_row, dt_row)


def _dn_pre_bwd_act(proj, conv_w, al_row, dt_row, dq, dk, dv, dbg, dproj, name):
    t = proj.shape[0]

    def body(x_ref, h_ref, cw_ref, ba_ref, al_ref, dt_ref, dq_ref, dk_ref, dv_ref, dbg_ref, _,
             dc_ref, draw_ref, dal_ref, ddt_ref):
        i = pl.program_id(0)

        @pl.when(i == 0)
        def _():
            dal_ref[...] = jnp.zeros_like(dal_ref)
            ddt_ref[...] = jnp.zeros_like(ddt_ref)

        keep = i > 0
        for grp in range(24):
            cols = slice(grp * HD, (grp + 1) * HD)
            ext = jnp.concatenate([jnp.where(keep, h_ref[:, cols], 0.0), x_ref[:, cols]], axis=0)
            cv = _conv_group(ext, cw_ref, cols)
            seg, head = divmod(grp, NH)
            hc = slice(head * HD, (head + 1) * HD)
            if seg == 2:
                ds = dv_ref[:, hc]
            else:
                s = _silu(cv)
                r = lax.rsqrt(jnp.sum(s * s, axis=-1, keepdims=True) + L2_EPS)
                dy = dq_ref[:, hc] if seg == 0 else dk_ref[:, hc]
                c = (HD ** -0.5) if seg == 0 else 1.0
                ds = (c * r) * (dy - s * ((r * r) * jnp.sum(dy * s, axis=-1, keepdims=True)))
            dc_ref[:, cols] = ds * _dsilu(cv)
        lane = _iota((TR, 128), 1)
        rowc = _iota((TR, 128), 0) % CH
        isb = lane < NH
        isg = jnp.logical_and(lane >= NH, lane < 2 * NH)
        beta, g, sg = _gate_terms(ba_ref[...], al_ref[...], dt_ref[...])
        dbgv = dbg_ref[...]
        dg = dbgv
        step = 1
        while step < CH:
            dg = dg + jnp.where(rowc < CH - step, pltpu.roll(dg, TR - step, 0), 0.0)
            step *= 2
        da_raw = dg * (-jnp.exp(al_ref[...])) * sg
        draw = jnp.where(isb, dbgv * beta * (1.0 - beta), jnp.where(isg, da_raw, 0.0))
        draw_ref[:, :128] = draw.astype(draw_ref.dtype)
        draw_ref[:, 128:] = jnp.zeros((TR, MIXP - OFF_BA - 128), draw_ref.dtype)
        dal_ref[...] += jnp.sum(jnp.where(isg, dg * g, 0.0), axis=0, keepdims=True)
        ddt_ref[...] += jnp.sum(jnp.where(isg, da_raw, 0.0), axis=0, keepdims=True)

    return pl.pallas_call(
        body, grid=(t // TR,),
        in_specs=[_row(3 * D, 0), _halo_prev(3 * D, 0), pl.BlockSpec((4, 3 * D), lambda i: (0, 0)),
                  _row(128, OFF_BA // 128), _vec(128), _vec(128), _row(D), _row(D), _row(D), _row(128),
                  pl.BlockSpec(memory_space=pl.ANY)],
        out_specs=[_row(3 * D), _into(MIXP - OFF_BA, OFF_BA), _vec(128), _vec(128)],
        out_shape=[SDS((t, 3 * D), F32), SDS(dproj.shape, dproj.dtype), SDS((1, 128), F32), SDS((1, 128), F32)],
        input_output_aliases={10: 1},
        compiler_params=_params(("arbitrary",)), name=name,
    )(proj, proj, conv_w, proj, al_row, dt_row, dq, dk, dv, dbg, dproj)


def _dn_pre_bwd_conv(proj, conv_w, dconv, dproj, name):
    t = proj.shape[0]
    nt = t // TR
    ext_rows = TR + HALO

    def body(x_ref, h_ref, cw_ref, dc_ref, dn_ref, _, dx_ref, dcw_ref):
        i = pl.program_id(0)

        @pl.when(i == 0)
        def _():
            dcw_ref[...] = jnp.zeros_like(dcw_ref)

        keep_prev = i > 0
        keep_next = i < nt - 1
        for grp in range(24):
            cols = slice(grp * HD, (grp + 1) * HD)
            dct = dc_ref[:, cols]
            dext = jnp.concatenate([dct, jnp.where(keep_next, dn_ref[:, cols], 0.0)], axis=0)
            acc = cw_ref[3:4, cols] * dext
            for j in range(3):
                acc = acc + cw_ref[j:j + 1, cols] * pltpu.roll(dext, ext_rows - (3 - j), 0)
            dx_ref[:, cols] = acc[:TR].astype(dx_ref.dtype)
            xext = jnp.concatenate([jnp.where(keep_prev, h_ref[:, cols], 0.0), x_ref[:, cols]], axis=0)
            for j in range(4):
                xs = xext if j == 3 else pltpu.roll(xext, 3 - j, 0)
                dcw_ref[j:j + 1, cols] += jnp.sum(xs[HALO:] * dct, axis=0, keepdims=True)

    return pl.pallas_call(
        body, grid=(nt,),
        in_specs=[_row(3 * D, 0), _halo_prev(3 * D, 0), pl.BlockSpec((4, 3 * D), lambda i: (0, 0)),
                  _row(3 * D), _halo_next(3 * D, 0, nt), pl.BlockSpec(memory_space=pl.ANY)],
        out_specs=[_into(3 * D, OFF_Q), pl.BlockSpec((4, 3 * D), lambda i: (0, 0))],
        out_shape=[SDS(dproj.shape, dproj.dtype), SDS((4, 3 * D), F32)],
        input_output_aliases={5: 0},
        compiler_params=_params(("arbitrary",)), name=name,
    )(proj, proj, conv_w, dconv, dconv, dproj)


def _dn_post_fwd(o, proj, gn, name):
    t = o.shape[0]

    def body(o_ref, z_ref, g_ref, out_ref):
        gv = g_ref[...]
        for h in range(NH):
            hc = slice(h * HD, (h + 1) * HD)
            ov = o_ref[:, hc]
            r = lax.rsqrt(jnp.mean(ov * ov, axis=-1, keepdims=True) + RMS_EPS)
            out_ref[:, hc] = (((ov * r) * gv) * _silu(z_ref[:, hc])).astype(out_ref.dtype)

    return pl.pallas_call(
        body, grid=(t // TR,), in_specs=[_row(D), _row(D, OFF_Z // D), _vec(HD)], out_specs=_row(D),
        out_shape=SDS((t, D), BF16), compiler_params=_params(("parallel",)), name=name,
    )(o, proj, gn)


def _dn_post_bwd(o, proj, gn, dob, dproj, name):
    t = o.shape[0]

    def body(o_ref, z_ref, g_ref, d_ref, _, do_ref, dz_ref, dg_ref):
        @pl.when(pl.program_id(0) == 0)
        def _():
            dg_ref[...] = jnp.zeros_like(dg_ref)

        gv = g_ref[...]
        acc = jnp.zeros((1, HD), F32)
        for h in range(NH):
            hc = slice(h * HD, (h + 1) * HD)
            ov = o_ref[:, hc]
            zv = z_ref[:, hc]
            dv = d_ref[:, hc]
            r = lax.rsqrt(jnp.mean(ov * ov, axis=-1, keepdims=True) + RMS_EPS)
            n = ov * r
            dz_ref[:, hc] = (dv * (n * gv) * _dsilu(zv)).astype(dz_ref.dtype)
            dng = dv * _silu(zv)
            acc = acc + jnp.sum(dng * n, axis=0, keepdims=True)
            dn = dng * gv
            do_ref[:, hc] = r * (dn - n * jnp.mean(dn * n, axis=-1, keepdims=True))
        dg_ref[...] += acc

    return pl.pallas_call(
        body, grid=(t // TR,),
        in_specs=[_row(D), _row(D, OFF_Z // D), _vec(HD), _row(D), pl.BlockSpec(memory_space=pl.ANY)],
        out_specs=[_row(D), _into(D, OFF_Z), _vec(HD)],
        out_shape=[SDS((t, D), F32), SDS(dproj.shape, dproj.dtype), SDS((1, HD), F32)],
        input_output_aliases={4: 1},
        compiler_params=_params(("arbitrary",)), name=name,
    )(o, proj, gn, dob, dproj)


def _merge_fwd(ya, yb, proj, name):
    t = ya.shape[0]

    def body(a_ref, b_ref, gp_ref, gd_ref, o_ref):
        o_ref[...] = (_sigmoid(gp_ref[...]) * a_ref[...] + _sigmoid(gd_ref[...]) * b_ref[...]).astype(o_ref.dtype)

    return pl.pallas_call(
        body, grid=(t // TR,), in_specs=[_row(D), _row(D), _row(D, OFF_GP // D), _row(D, OFF_GD // D)],
        out_specs=_row(D), out_shape=SDS((t, D), BF16),
        compiler_params=_params(("parallel",)), name=name,
    )(ya, yb, proj, proj)


def _into(width, offset):
    assert offset % width == 0
    return pl.BlockSpec((TR, width), lambda i: (i, offset // width))


def _merge_bwd(dm, ya, yb, proj, dproj, name):
    t = ya.shape[0]

    def body(d_ref, a_ref, b_ref, gp_ref, gd_ref, _, da_ref, db_ref, dg_ref):
        dv = d_ref[...]
        sp = _sigmoid(gp_ref[...])
        sd = _sigmoid(gd_ref[...])
        da_ref[...] = (dv * sp).astype(da_ref.dtype)
        db_ref[...] = (dv * sd).astype(db_ref.dtype)
        dg_ref[:, :D] = (dv * a_ref[...] * sp * (1.0 - sp)).astype(dg_ref.dtype)
        dg_ref[:, D:] = (dv * b_ref[...] * sd * (1.0 - sd)).astype(dg_ref.dtype)

    return pl.pallas_call(
        body, grid=(t // TR,),
        in_specs=[_row(D), _row(D), _row(D), _row(D, OFF_GP // D), _row(D, OFF_GD // D),
                  pl.BlockSpec(memory_space=pl.ANY)],
        out_specs=[_row(D), _row(D), _into(2 * D, OFF_GP)],
        out_shape=[SDS((t, D), BF16), SDS((t, D), BF16), SDS(dproj.shape, dproj.dtype)],
        input_output_aliases={5: 2},
        compiler_params=_params(("parallel",)), name=name,
    )(dm, ya, yb, proj, proj, dproj)


def _split2(x):
    hi = x.astype(BF16)
    return hi, (x - hi.astype(F32)).astype(BF16)


def _dot3(a, b, dims):
    ah, al = _split2(a)
    bh, bl = _split2(b)
    return _dg(ah, bh, dims) + (_dg(ah, bl, dims) + _dg(al, bh, dims))


def _neumann_inverses(mats):
    ri = _iota((CH, CH), 0)
    ci = _iota((CH, CH), 1)
    eye = jnp.where(ri == ci, 1.0, 0.0).astype(F32)
    xs = [-a for a in mats]
    ps = [eye + x for x in xs]
    for _ in range(5):
        xs = [_dot3(x, x, NN) for x in xs]
        ps = [p + _dot3(p, x, NN) for p, x in zip(ps, xs)]
    return ps


def _solve_with(inv):
    @jax.custom_vjp
    def solve(a, rhs):
        return _dot3(inv, rhs, NN)

    def fwd(a, rhs):
        sol = _dot3(inv, rhs, NN)
        return sol, sol

    def bwd(sol, d):
        drhs = _dot3(inv, d, TN)
        return -_dot3(drhs, sol, NT), drhs

    solve.defvjp(fwd, bwd)
    return solve


@jax.custom_vjp
def _rows_to_lanes(g64):
    ri = _iota((CH, CH), 0)
    ci = _iota((CH, CH), 1)
    diag = jnp.where(ri == ci, g64, 0.0)
    ones = jnp.ones((CH, CH), BF16)
    hi = diag.astype(BF16)
    rem = diag - hi.astype(F32)
    mid = rem.astype(BF16)
    lo = (rem - mid.astype(F32)).astype(BF16)
    return _dg(ones, hi, NN) + (_dg(ones, mid, NN) + _dg(ones, lo, NN))


def _rows_to_lanes_bwd(_, d):
    ri = _iota((CH, CH), 0)
    ci = _iota((CH, CH), 1)
    return (jnp.where(ri == ci, jnp.broadcast_to(jnp.sum(d, axis=0, keepdims=True), (CH, CH)), 0.0),)


_rows_to_lanes.defvjp(lambda g64: (_rows_to_lanes(g64), None), _rows_to_lanes_bwd)


def _chunk_local(solve_all, q, k, v, g128, g64, gl128, b128, b64):
    ri = _iota((CH, CH), 0)
    ci = _iota((CH, CH), 1)
    causal = ri >= ci
    strict = ri > ci
    gj = [_rows_to_lanes(g) for g in g64]
    decay = [jnp.where(causal, jnp.exp(jnp.where(causal, g - t, 0.0)), 0.0) for g, t in zip(g64, gj)]
    kk = [_nt(x, x) for x in k]
    a = [jnp.where(strict, b * m * dc, 0.0) for b, m, dc in zip(b64, kk, decay)]
    eg = [jnp.exp(g) for g in g128]
    rhs = [jnp.concatenate([b * x, (b * e) * y], axis=1) for b, x, e, y in zip(b128, v, eg, k)]
    sol = solve_all(a, rhs)
    qk = [jnp.where(causal, _nt(x, y) * dc, 0.0) for x, y, dc in zip(q, k, decay)]
    return ([s[:, :HD] for s in sol], [s[:, HD:] for s in sol], qk, [x * e for x, e in zip(q, eg)],
            [x * jnp.exp(gl - g) for x, gl, g in zip(k, gl128, g128)], [jnp.exp(gl) for gl in gl128])


def _all_head_gates(bgv):
    return tuple(list(z) for z in zip(*[_head_gates(bgv, h) for h in range(NH)]))


def _head_gates(bgv, h):
    lane = _iota((CH, 128), 1)
    row = _iota((CH, 128), 0)
    bcol = jnp.sum(jnp.where(lane == h, bgv, 0.0), axis=1, keepdims=True)
    gcol = jnp.sum(jnp.where(lane == NH + h, bgv, 0.0), axis=1, keepdims=True)
    g128 = jnp.broadcast_to(gcol, (CH, 128))
    gl128 = jnp.broadcast_to(jnp.sum(jnp.where(row == CH - 1, g128, 0.0), axis=0, keepdims=True), (CH, 128))
    return (g128, jnp.broadcast_to(gcol, (CH, CH)), gl128,
            jnp.broadcast_to(bcol, (CH, 128)), jnp.broadcast_to(bcol, (CH, CH)))


def _chunk_specs():
    row = pl.BlockSpec((CH, D), lambda i: (i, 0))
    small = pl.BlockSpec((CH, 128), lambda i: (i, 0))
    qk = pl.BlockSpec((NH, CH, CH), lambda i: (i, 0, 0))
    eg = pl.BlockSpec((1, NH, 128), lambda i: (i, 0, 0))
    return row, small, qk, eg


def _dn_local_fwd(q, k, v, bg, name):
    t = q.shape[0]
    n = t // CH

    def body(q_ref, k_ref, v_ref, bg_ref, u_ref, w_ref, qk_ref, qd_ref, kd_ref, eg_ref, inv_ref):
        cols = [slice(h * HD, (h + 1) * HD) for h in range(NH)]

        def solve_all(mats, rhs):
            invs = _neumann_inverses(mats)
            for h in range(NH):
                inv_ref[h] = invs[h]
            return [_dot3(m, r, NN) for m, r in zip(invs, rhs)]

        u, w, qk, qd, kd, egl = _chunk_local(
            solve_all, [q_ref[:, c] for c in cols], [k_ref[:, c] for c in cols], [v_ref[:, c] for c in cols],
            *_all_head_gates(bg_ref[...]))
        for h, hc in enumerate(cols):
            u_ref[:, hc] = u[h]
            w_ref[:, hc] = w[h].astype(w_ref.dtype)
            qd_ref[:, hc] = qd[h].astype(qd_ref.dtype)
            kd_ref[:, hc] = kd[h].astype(kd_ref.dtype)
            qk_ref[h] = qk[h].astype(qk_ref.dtype)
            eg_ref[0, h:h + 1, :] = egl[h][0:1, :]

    row, small, qkb, egb = _chunk_specs()
    return pl.pallas_call(
        body, grid=(n,), in_specs=[row, row, row, small], out_specs=[row, row, qkb, row, row, egb, qkb],
        out_shape=[SDS((t, D), F32), SDS((t, D), BF16), SDS((n * NH, CH, CH), BF16), SDS((t, D), BF16),
                   SDS((t, D), BF16), SDS((n, NH, 128), F32), SDS((n * NH, CH, CH), F32)],
        compiler_params=_params(("parallel",)), name=name,
    )(q, k, v, bg)


def _dn_local_bwd(q, k, v, bg, inv, du, dw, dqk, dqd, dkd, deg, name):
    t = q.shape[0]
    n = t // CH

    def body(q_ref, k_ref, v_ref, bg_ref, inv_ref, du_ref, dw_ref, dqk_ref, dqd_ref, dkd_ref, deg_ref,
             dq_ref, dk_ref, dv_ref, dbg_ref):
        bgv = bg_ref[...]
        lane = _iota((CH, 128), 1)
        row = _iota((CH, 128), 0)
        first = jnp.where(row == 0, 1.0, 0.0)
        acc = jnp.zeros((CH, 128), F32)
        cols = [slice(h * HD, (h + 1) * HD) for h in range(NH)]
        solves = [_solve_with(inv_ref[h]) for h in range(NH)]

        def solve_all(mats, rhs):
            return [f(m, r) for f, m, r in zip(solves, mats, rhs)]

        _, vjp = jax.vjp(functools.partial(_chunk_local, solve_all),
                         [q_ref[:, c] for c in cols], [k_ref[:, c] for c in cols], [v_ref[:, c] for c in cols],
                         *_all_head_gates(bgv))
        cts = ([du_ref[:, c].astype(F32) for c in cols], [dw_ref[:, c].astype(F32) for c in cols],
               [dqk_ref[h] for h in range(NH)],
               [dqd_ref[:, c].astype(F32) for c in cols], [dkd_ref[:, c].astype(F32) for c in cols],
               [jnp.broadcast_to(deg_ref[0, h:h + 1, :], (CH, 128)) * first for h in range(NH)])
        dq, dk, dv, dg128, dg64, dgl, db128, db64 = vjp(cts)
        for h, hc in enumerate(cols):
            dq_ref[:, hc] = dq[h]
            dk_ref[:, hc] = dk[h]
            dv_ref[:, hc] = dv[h]
            dg = jnp.sum(dg128[h], axis=1, keepdims=True) + jnp.sum(dg64[h], axis=1, keepdims=True)
            tot = jnp.sum(jnp.sum(dgl[h], axis=0, keepdims=True), axis=1, keepdims=True)
            dg = dg + jnp.where(row[:, 0:1] == CH - 1, tot, 0.0)
            db = jnp.sum(db128[h], axis=1, keepdims=True) + jnp.sum(db64[h], axis=1, keepdims=True)
            acc = acc + jnp.where(lane == h, db, 0.0) + jnp.where(lane == NH + h, dg, 0.0)
        dbg_ref[...] = acc

    row, small, qkb, egb = _chunk_specs()
    return pl.pallas_call(
        body, grid=(n,), in_specs=[row, row, row, small, qkb, row, row, qkb, row, row, egb],
        out_specs=[row, row, row, small],
        out_shape=[SDS((t, D), F32)] * 3 + [SDS((t, 128), F32)],
        compiler_params=_params(("parallel",)), name=name,
    )(q, k, v, bg, inv, du, dw, dqk, dqd, dkd, deg)


def _state_step(s, u, w, qk, qd, kd, egl):
    ws = [_nn(a, b) for a, b in zip(w, s)]
    v_new = [a - b for a, b in zip(u, ws)]
    qs = [_nn(a, b) for a, b in zip(qd, s)]
    intra = [_nn(a, b) for a, b in zip(qk, v_new)]
    upd = [_tn(a, b) for a, b in zip(kd, v_new)]
    return [a * e + b for a, e, b in zip(s, egl, upd)], [a + b for a, b in zip(qs, intra)]


def _dn_scan_fwd(u, w, qk, qd, kd, eg, name):
    t = u.shape[0]
    n = t // CH
    g = SCAN_CHUNKS

    def body(u_ref, w_ref, qk_ref, qd_ref, kd_ref, eg_ref, o_ref, save_ref, s_ref):
        @pl.when(pl.program_id(0) == 0)
        def _():
            s_ref[...] = jnp.zeros_like(s_ref)

        cols = [slice(h * HD, (h + 1) * HD) for h in range(NH)]
        s = [s_ref[h] for h in range(NH)]
        for c in range(g):
            rows = slice(c * CH, (c + 1) * CH)
            for h in range(NH):
                save_ref[c, h] = s[h].astype(save_ref.dtype)
            s, o = _state_step(
                s, [u_ref[rows, hc] for hc in cols], [w_ref[rows, hc].astype(F32) for hc in cols],
                [qk_ref[c * NH + h].astype(F32) for h in range(NH)], [qd_ref[rows, hc].astype(F32) for hc in cols],
                [kd_ref[rows, hc].astype(F32) for hc in cols], [eg_ref[c, h:h + 1, :] for h in range(NH)])
            for h, hc in enumerate(cols):
                o_ref[rows, hc] = o[h]
        for h in range(NH):
            s_ref[h] = s[h]

    row = pl.BlockSpec((g * CH, D), lambda i: (i, 0))
    qkb = pl.BlockSpec((g * NH, CH, CH), lambda i: (i, 0, 0))
    egb = pl.BlockSpec((g, NH, 128), lambda i: (i, 0, 0))
    return pl.pallas_call(
        body, grid=(n // g,), in_specs=[row, row, qkb, row, row, egb],
        out_specs=[row, pl.BlockSpec((g, NH, HD, HD), lambda i: (i, 0, 0, 0))],
        out_shape=[SDS((t, D), F32), SDS((n, NH, HD, HD), BF16)],
        scratch_shapes=[pltpu.VMEM((NH, HD, HD), F32)],
        compiler_params=_params(("arbitrary",)), name=name,
    )(u, w, qk, qd, kd, eg)


def _dn_scan_bwd(u, w, qk, qd, kd, eg, saved, do, name):
    t = u.shape[0]
    n = t // CH
    g = SCAN_CHUNKS
    last = n // g - 1

    def body(u_ref, w_ref, qk_ref, qd_ref, kd_ref, eg_ref, sv_ref, do_ref,
             du_ref, dw_ref, dqk_ref, dqd_ref, dkd_ref, deg_ref, ds_ref):
        @pl.when(pl.program_id(0) == 0)
        def _():
            ds_ref[...] = jnp.zeros_like(ds_ref)

        cols = [slice(h * HD, (h + 1) * HD) for h in range(NH)]
        ds = [ds_ref[h] for h in range(NH)]
        for c in reversed(range(g)):
            rows = slice(c * CH, (c + 1) * CH)
            _, vjp = jax.vjp(
                _state_step, [sv_ref[c, h].astype(F32) for h in range(NH)], [u_ref[rows, hc] for hc in cols],
                [w_ref[rows, hc].astype(F32) for hc in cols], [qk_ref[c * NH + h].astype(F32) for h in range(NH)],
                [qd_ref[rows, hc].astype(F32) for hc in cols], [kd_ref[rows, hc].astype(F32) for hc in cols],
                [eg_ref[c, h:h + 1, :] for h in range(NH)])
            ds, du, dw, dqk, dqd, dkd, deg = vjp((ds, [do_ref[rows, hc] for hc in cols]))
            for h, hc in enumerate(cols):
                du_ref[rows, hc] = du[h].astype(du_ref.dtype)
                dw_ref[rows, hc] = dw[h].astype(dw_ref.dtype)
                dqk_ref[c * NH + h] = dqk[h]
                dqd_ref[rows, hc] = dqd[h].astype(dqd_ref.dtype)
                dkd_ref[rows, hc] = dkd[h].astype(dkd_ref.dtype)
                deg_ref[c, h:h + 1, :] = deg[h]
        for h in range(NH):
            ds_ref[h] = ds[h]

    row = pl.BlockSpec((g * CH, D), lambda i: (last - i, 0))
    qkb = pl.BlockSpec((g * NH, CH, CH), lambda i: (last - i, 0, 0))
    egb = pl.BlockSpec((g, NH, 128), lambda i: (last - i, 0, 0))
    return pl.pallas_call(
        body, grid=(n // g,),
        in_specs=[row, row, qkb, row, row, egb,
                  pl.BlockSpec((g, NH, HD, HD), lambda i: (last - i, 0, 0, 0)), row],
        out_specs=[row, row, qkb, row, row, egb],
        out_shape=[SDS((t, D), BF16), SDS((t, D), BF16), SDS((n * NH, CH, CH), F32), SDS((t, D), BF16),
                   SDS((t, D), BF16), SDS((n, NH, 128), F32)],
        scratch_shapes=[pltpu.VMEM((NH, HD, HD), F32)],
        compiler_params=_params(("arbitrary",)), name=name,
    )(u, w, qk, qd, kd, eg, saved, do)


def _ada_fwd(c_all, ada_w, ada_b, name):
    ncol = ada_w.shape[1]

    def body(c_ref, w_ref, b_ref, o_ref):
        o_ref[...] = _dg(_silu(c_ref[...]), w_ref[...], NN, HI) + b_ref[...]

    return pl.pallas_call(body, out_shape=SDS((NDEV, ncol), F32),
                          compiler_params=pltpu.CompilerParams(vmem_limit_bytes=VMEM_LIMIT), name=name,
                          )(c_all, ada_w, ada_b)


def _ada_bwd(c_all_t, dmod, name):
    ncol = dmod.shape[1]

    def body(c_ref, d_ref, o_ref):
        sc = _silu(c_ref[...])
        acc = sc[:, 0:1] * d_ref[0:1, :]
        for b in range(1, NDEV):
            acc = acc + sc[:, b:b + 1] * d_ref[b:b + 1, :]
        o_ref[...] = acc

    return pl.pallas_call(body, out_shape=SDS((D, ncol), F32),
                          compiler_params=pltpu.CompilerParams(vmem_limit_bytes=VMEM_LIMIT), name=name,
                          )(c_all_t, dmod)


def _sum_devices(parts, out_dtype, name):
    _, r, c = parts.shape
    tr = TR if r % TR == 0 else r

    def body(p_ref, o_ref):
        acc = p_ref[0].astype(F32)
        for i in range(1, NDEV):
            acc = acc + p_ref[i].astype(F32)
        o_ref[...] = acc.astype(o_ref.dtype)

    return pl.pallas_call(
        body, grid=(r // tr,), in_specs=[pl.BlockSpec((NDEV, tr, c), lambda i: (0, i, 0))],
        out_specs=pl.BlockSpec((tr, c), lambda i: (i, 0)), out_shape=SDS((r, c), out_dtype),
        compiler_params=_params(("parallel",)), name=name,
    )(parts)


def _adam_tiles(r, c):
    if r % 8 == 0:
        return _pick(r, (256, 352, 128, 8)), c
    return r, (256 if c % 256 == 0 else c)


def _adam_math(w, gv, m, v):
    m_new = ADAM_B1 * m + (1.0 - ADAM_B1) * gv
    v_new = ADAM_B2 * v + (1.0 - ADAM_B2) * (gv * gv)
    bc1 = 1.0 - ADAM_B1 ** ADAM_STEP
    bc2 = 1.0 - ADAM_B2 ** ADAM_STEP
    return -ADAM_LR * ((m_new / bc1) / (jnp.sqrt(v_new / bc2) + ADAM_EPS) + ADAM_WD * w), m_new, v_new


def _adamw(w, g, m, v, name):
    r, c = w.shape
    tr, tc = _adam_tiles(r, c)

    def body(w_ref, g_ref, m_ref, v_ref, d_ref, nm_ref, nv_ref):
        d_ref[...], nm_ref[...], nv_ref[...] = _adam_math(w_ref[...], g_ref[...], m_ref[...], v_ref[...])

    spec = pl.BlockSpec((tr, tc), lambda i, j: (i, j))
    return pl.pallas_call(
        body, grid=(r // tr, c // tc), in_specs=[spec] * 4, out_specs=[spec] * 3,
        out_shape=[SDS((r, c), F32)] * 3, compiler_params=_params(("parallel", "parallel")), name=name,
    )(w, g, m, v)


def _reduce_adamw(parts, w, m, v, name):
    r, c = w.shape
    tr, tc = _adam_tiles(r, c)

    def body(p_ref, w_ref, m_ref, v_ref, g_ref, d_ref, nm_ref, nv_ref):
        gv = p_ref[0].astype(F32)
        for i in range(1, NDEV):
            gv = gv + p_ref[i].astype(F32)
        g_ref[...] = gv
        d_ref[...], nm_ref[...], nv_ref[...] = _adam_math(w_ref[...], gv, m_ref[...], v_ref[...])

    spec = pl.BlockSpec((tr, tc), lambda i, j: (i, j))
    return pl.pallas_call(
        body, grid=(r // tr, c // tc),
        in_specs=[pl.BlockSpec((NDEV, tr, tc), lambda i, j: (0, i, j))] + [spec] * 3, out_specs=[spec] * 4,
        out_shape=[SDS((r, c), F32)] * 4, compiler_params=_params(("parallel", "parallel")), name=name,
    )(parts, w, m, v)


ANY = pl.BlockSpec(memory_space=pl.ANY)
MESH = pl.DeviceIdType.MESH


def _all_gather(xs, name, after=None):
    n = len(xs)
    extra = [] if after is None else [after]

    def body(*refs):
        x_refs, out_refs = refs[:n], refs[n + len(extra):2 * n + len(extra)]
        send_sems, recv_sems, local_sems = refs[-3:]
        mx, my, mc = lax.axis_index("x"), lax.axis_index("y"), lax.axis_index("c")
        me, sibling = (mx, my, mc), (mx, my, 1 - mc)
        chips = [(1 - mx, my), (mx, 1 - my), (1 - mx, 1 - my)]

        def rows(a, px, py, pc):
            return out_refs[a].at[4 * px + 2 * py + pc]

        def copy(a, k, block, to, src=None):
            return pltpu.make_async_remote_copy(
                src_ref=rows(a, *block) if src is None else src, dst_ref=rows(a, *block),
                send_sem=send_sems.at[a, k], recv_sem=recv_sems.at[a, k], device_id=to, device_id_type=MESH)

        mine = [pltpu.make_async_copy(x_refs[a], rows(a, *me), local_sems.at[a]) for a in range(n)]
        for cp in mine:
            cp.start()
        first = []
        for a in range(n):
            first.append(copy(a, 0, me, sibling, src=x_refs[a]))
            first += [copy(a, 1 + j, me, (*chip, mc), src=x_refs[a]) for j, chip in enumerate(chips)]
        for cp in first:
            cp.start()
        passed = []
        for a in range(n):
            for j, chip in enumerate(chips):
                copy(a, 1 + j, (*chip, mc), me).wait_recv()
                passed.append(copy(a, 4 + j, (*chip, mc), sibling))
                passed[-1].start()
        for a in range(n):
            copy(a, 0, sibling, me).wait_recv()
            for j, chip in enumerate(chips):
                copy(a, 4 + j, (*chip, 1 - mc), me).wait_recv()
        for cp in first + passed:
            cp.wait_send()
        for cp in mine:
            cp.wait()

    return pl.pallas_call(
        body, out_shape=[SDS((NDEV,) + x.shape, x.dtype) for x in xs], in_specs=[ANY] * (n + len(extra)),
        out_specs=[ANY] * n,
        scratch_shapes=[pltpu.SemaphoreType.DMA((n, 7)), pltpu.SemaphoreType.DMA((n, 7)),
                        pltpu.SemaphoreType.DMA((n,))],
        name=name,
    )(*xs, *extra)


HBM = pl.BlockSpec(memory_space=pltpu.HBM)
SEM = pl.BlockSpec(memory_space=pltpu.SEMAPHORE)
EFFECT = pltpu.SideEffectType.DATAFLOW_SIDE_EFFECTING


def _peers():
    mx, my, mc = lax.axis_index("x"), lax.axis_index("y"), lax.axis_index("c")
    out = []
    for k in range(1, NDEV):
        out.append((1 - mx if k & 4 else mx, 1 - my if k & 2 else my, 1 - mc if k & 1 else mc))
    return 4 * mx + 2 * my + mc, out


NEAR = (0, 1, 3, 5)


def _push_start(srcs, sliced, name, after=None, near=()):
    n = len(srcs)
    extra = [] if after is None else [after]
    lands = [lax.empty(s.shape if sliced else (NDEV,) + s.shape, s.dtype) for s in srcs]

    def body(*refs):
        src_refs, land_refs = refs[:n], refs[n:2 * n]
        outs = refs[2 * n + len(extra):]
        send_sems, recv_sems = outs[:n], outs[n:2 * n]
        token = refs[-1]
        me, peers = _peers()
        for a in range(n):
            for k, (px, py, pc) in enumerate(peers):
                if a in near and k not in NEAR:
                    continue
                src = src_refs[a].at[4 * px + 2 * py + pc] if sliced else src_refs[a]
                pltpu.make_async_remote_copy(
                    src_ref=src, dst_ref=land_refs[a].at[me], send_sem=send_sems[a].at[k],
                    recv_sem=recv_sems[a].at[k], device_id=(px, py, pc), device_id_type=MESH).start()
            pltpu.make_async_copy(src_refs[a].at[me] if sliced else src_refs[a], land_refs[a].at[me],
                                  send_sems[a].at[NDEV - 1]).start()
        token[...] = jnp.zeros_like(token)

    outs = pl.pallas_call(
        body, name=name,
        out_shape=([pltpu.SemaphoreType.DMA((NDEV,))] * n + [pltpu.SemaphoreType.DMA((NDEV - 1,))] * n
                   + [pltpu.HBM(s.shape, s.dtype) for s in srcs] + [pltpu.HBM(l.shape, l.dtype) for l in lands]
                   + [SDS((8, 128), F32)]),
        in_specs=[HBM] * (2 * n) + [pl.BlockSpec(memory_space=pl.ANY)] * len(extra),
        out_specs=[SEM] * (2 * n) + [HBM] * (2 * n) + [pl.BlockSpec(memory_space=pltpu.VMEM)],
        input_output_aliases={i: 2 * n + i for i in range(2 * n)},
        compiler_params=pltpu.CompilerParams(has_side_effects=EFFECT),
    )(*[pltpu.with_memory_space_constraint(s, pltpu.HBM) for s in srcs],
      *[pltpu.with_memory_space_constraint(l, pltpu.HBM) for l in lands], *extra)
    sends, recvs = outs[:n], outs[n:2 * n]
    src_thru, land_thru = outs[2 * n:3 * n], outs[3 * n:4 * n]
    return [(sends[a], recvs[a], src_thru[a], land_thru[a]) for a in range(n)], outs[-1]


def _push_wait(started, sliced, after, name, near=()):
    n = len(started)
    afters = list(after) if isinstance(after, (list, tuple)) else [after]

    def body(*refs):
        src_refs, land_refs = refs[:n], refs[n:2 * n]
        send_sems, recv_sems = refs[2 * n:3 * n], refs[3 * n:4 * n]
        me, peers = _peers()
        for a in range(n):
            for k, (px, py, pc) in enumerate(peers):
                if a in near and k not in NEAR:
                    continue
                src = src_refs[a].at[4 * px + 2 * py + pc] if sliced else src_refs[a]
                cp = pltpu.make_async_remote_copy(
                    src_ref=src, dst_ref=land_refs[a].at[me], send_sem=send_sems[a].at[k],
                    recv_sem=recv_sems[a].at[k], device_id=(px, py, pc), device_id_type=MESH)
                cp.wait_send()
                cp.wait_recv()
            pltpu.make_async_copy(src_refs[a].at[me] if sliced else src_refs[a], land_refs[a].at[me],
                                  send_sems[a].at[NDEV - 1]).wait()

    srcs = [s[2] for s in started]
    lands = [s[3] for s in started]
    outs = pl.pallas_call(
        body, name=name,
        out_shape=[pltpu.HBM(s.shape, s.dtype) for s in srcs] + [pltpu.HBM(l.shape, l.dtype) for l in lands],
        in_specs=[HBM] * (2 * n) + [SEM] * (2 * n) + [pl.BlockSpec(memory_space=pl.ANY)] * len(afters),
        out_specs=[HBM] * (2 * n),
        input_output_aliases={i: i for i in range(2 * n)},
        compiler_params=pltpu.CompilerParams(has_side_effects=EFFECT),
    )(*srcs, *lands, *[s[0] for s in started], *[s[1] for s in started], *afters)
    return outs[n:]


def _relay_to_sibling(land, name):
    def body(_, land_ref, send_sems, recv_sems):
        mx, my, mc = lax.axis_index("x"), lax.axis_index("y"), lax.axis_index("c")
        chips = [(1 - mx, my), (mx, 1 - my), (1 - mx, 1 - my)]

        def copy(j, core):
            slot = land_ref.at[4 * chips[j][0] + 2 * chips[j][1] + core]
            return pltpu.make_async_remote_copy(
                src_ref=slot, dst_ref=slot, send_sem=send_sems.at[j], recv_sem=recv_sems.at[j],
                device_id=(mx, my, 1 - mc), device_id_type=MESH)

        mine = [copy(j, mc) for j in range(3)]
        for cp in mine:
            cp.start()
        for j in range(3):
            copy(j, 1 - mc).wait_recv()
        for cp in mine:
            cp.wait_send()

    return pl.pallas_call(
        body, out_shape=SDS(land.shape, land.dtype), in_specs=[ANY], out_specs=ANY, input_output_aliases={0: 0},
        scratch_shapes=[pltpu.SemaphoreType.DMA((3,)), pltpu.SemaphoreType.DMA((3,))], name=name,
    )(land)


def _cols_from_blocks(blocks):
    _, rows, w = blocks.shape
    return blocks.transpose(1, 0, 2).reshape(rows, NDEV * w)


def _cols_to_blocks(full):
    rows, total = full.shape
    return full.reshape(rows, NDEV, total // NDEV).transpose(1, 0, 2)


def _mix_pad(wt):
    xp, q, k, v, z, ba, gp, gd = jnp.split(wt, (512, 1536, 2560, 3584, 4608, 4624, 5648), axis=0)
    pad = jnp.zeros((MIXP - OFF_BA - 16, wt.shape[1]), wt.dtype)
    return jnp.concatenate([q, k, v, z, gp, gd, xp, ba, pad], axis=0)


def _mix_unpad(wt):
    q, k, v, z, gp, gd, xp, ba = (wt[OFF_Q:OFF_K], wt[OFF_K:OFF_V], wt[OFF_V:OFF_Z], wt[OFF_Z:OFF_GP],
                                  wt[OFF_GP:OFF_GD], wt[OFF_GD:OFF_XP], wt[OFF_XP:OFF_BA], wt[OFF_BA:OFF_BA + 16])
    return jnp.concatenate([xp, q, k, v, z, ba, gp, gd], axis=0)


def _lane_row(vec8):
    return jnp.zeros((1, 128), F32).at[0, NH:2 * NH].set(vec8)


def _ffn_fwd(x, h, gate, w_in, w_out, tag, next_norm=None, token=None, start_more=None):
    if isinstance(w_in, tuple):
        w_in, = _push_wait([w_in], False, h, f"{tag}_gather_wait_in")
    w_in = w_in.reshape(2 * FH, D)
    u, a = _swiglu_up(h, w_in, f"{tag}_up", after=token)
    w_out, = _push_wait([w_out], False, a, f"{tag}_gather_wait_out")
    w_out = w_out.reshape(FH, D)
    outs = _matmul_residual(a, w_out, x, gate, 0.5, a_blk=True, norm=next_norm, name=f"{tag}_down",
                            after=None if start_more is None else start_more(h))
    return outs[0], (h, u, a, outs[1]), w_in, w_out, (outs[2] if next_norm else None)


def _ffn_bwd(dx_out, dy, x, g, scale, w_in, w_out, saved, tag, below=None):
    h, u, a, _ = saved
    t = x.shape[0]
    dw_out = _matmul(a, dy, ta=True, a_blk=True, out_dtype=BF16, name=f"{tag}_down_dw")
    sent_out, token = _push_start([dw_out.reshape(NDEV, FH // NDEV, D)], True, f"{tag}_grad_start_out")
    du = _swiglu_down_bwd(dy, w_out, u, f"{tag}_down_dx", after=token).reshape(NDEV, t, FB)
    dw_in = _matmul(du, h, ta=True, a_blk=True, out_dtype=BF16, name=f"{tag}_up_dw")
    sent_in, token = _push_start([dw_in.reshape(NDEV, FB, D)], True, f"{tag}_grad_start_in")
    dh = _matmul(du, w_in, a_blk=True, out_dtype=F32, name=f"{tag}_up_dx", after=token)
    return _norm_mod_bwd(x, g, scale, dh, dx_out, f"{tag}_norm_bwd", below), sent_in + sent_out


def kernel(x, c, ada_w, ada_b, norm_g, ffn1_w_in, ffn1_w_out, ffn2_w_in, ffn2_w_out, mix_w_in, conv_w, a_log, dt_bias, dn_norm_g, pool_w, pool_scale, pool_proj, dn_proj, mix_w_out, final_g, loss_target, m_ada_w, m_ada_b, m_norm_g, m_ffn1_w_in, m_ffn1_w_out, m_ffn2_w_in, m_ffn2_w_out, m_mix_w_in, m_conv_w, m_a_log, m_dt_bias, m_dn_norm_g, m_pool_w, m_pool_scale, m_pool_proj, m_dn_proj, m_mix_w_out, m_final_g, v_ada_w, v_ada_b, v_norm_g, v_ffn1_w_in, v_ffn1_w_out, v_ffn2_w_in, v_ffn2_w_out, v_mix_w_in, v_conv_w, v_a_log, v_dt_bias, v_dn_norm_g, v_pool_w, v_pool_scale, v_pool_proj, v_dn_proj, v_mix_w_out, v_final_g):
    me = 4 * lax.axis_index("x") + 2 * lax.axis_index("y") + lax.axis_index("c")
    x0 = x[0]
    target = loss_target[0]
    t = x0.shape[0]

    big = [ffn1_w_in[0], ffn1_w_out[0], ffn2_w_in[0], ffn2_w_out[0], mix_w_in[0], pool_proj[0], dn_proj[0],
           mix_w_out[0]]
    small = jnp.concatenate([c.reshape(8, 128), conv_w[0].reshape(12, 128), norm_g[0].reshape(3, 128),
                             jnp.zeros((1, 128), F32)], axis=0)
    small_all, = _all_gather([small], "gather_small")
    c_all = small_all[:, 0:8, :].reshape(NDEV, D)
    conv_full = small_all[:, 8:20, :].reshape(NDEV, 4, 384).transpose(1, 0, 2).reshape(4, 3 * D)
    norm_full = small_all[:, 20:23, :].reshape(NDEV, 3, 128).transpose(1, 0, 2).reshape(3, D)

    ncol = ada_w.shape[2]
    ada_b_mine = lax.dynamic_slice(ada_b, (0, me * ncol), (1, ncol))
    mod_cols = _ada_fwd(c_all, ada_w[0], ada_b_mine, "ada_fwd")
    transposed = (0, 2, 4)
    payload = [(w.T if i in transposed else w).astype(BF16) for i, w in enumerate(big)]
    mod_all, w_in1 = _all_gather([mod_cols, payload[0]], "gather_mod_first_weight")
    started, token = _push_start([payload[1], payload[4]], False, "gather_start", after=mod_all, near=(1,))
    started = {1: started[0], 4: started[1]}

    def start_rest(h):
        more, token = _push_start([payload[i] for i in (5, 6, 7, 2, 3)], False, "gather_start_rest", after=h)
        started.update(zip((5, 6, 7, 2, 3), more))
        return token

    mod = lax.dynamic_index_in_dim(mod_all, me, axis=1, keepdims=False).reshape(9, D)
    shift = [mod[3 * s:3 * s + 1] for s in range(3)]
    scale = [mod[3 * s + 1:3 * s + 2] for s in range(3)]
    gate = [mod[3 * s + 2:3 * s + 3] for s in range(3)]
    ng = [norm_full[s:s + 1] for s in range(3)]
    fg = final_g.reshape(1, D)
    al_row = _lane_row(a_log[0])
    dt_row = _lane_row(dt_bias[0])
    gn = dn_norm_g
    pw = pool_w[0]
    ps = pool_scale

    h0 = _norm_mod_fwd(x0, ng[0], shift[0], scale[0], "ffn1_norm", after=token)
    x1, saved1, w_in1, w_out1, h1 = _ffn_fwd(x0, h0, gate[0], w_in1, started[1], "ffn1",
                                             (ng[1], shift[1], scale[1]), token, start_rest)

    seg, = _push_wait([started[4]], False, h1, "mix_gather_wait", near=(0,))
    w_mix = _mix_pad(_relay_to_sibling(seg, "mix_gather_relay").reshape(MIX_RAW, D))
    proj = _matmul(h1, w_mix, tb=True, out_dtype=F32, name="mix_in")
    qh, kh, vh, bg = _dn_pre_fwd(proj, conv_full, al_row, dt_row, "dn_pre")
    seg = _push_wait([started[i] for i in (5, 6, 7)], False, qh, "mix_gather_wait_rest")
    w_pp = _cols_from_blocks(seg[0])
    w_dn = seg[1].reshape(D, D)
    w_mo = seg[2].reshape(D, D)
    ya = _pool_fwd(proj, pw, ps, w_pp, "pool_fwd")
    u, w, qk, qd, kd, eg, inv = _dn_local_fwd(qh, kh, vh, bg, "dn_local")
    o, s_saved = _dn_scan_fwd(u, w, qk, qd, kd, eg, "dn_scan")
    ob = _dn_post_fwd(o, proj, gn, "dn_post")
    yb = _matmul(ob, w_dn, out_dtype=F32, name="dn_out")
    merged = _merge_fwd(ya, yb, proj, "merge")
    x2, mix_y, h2 = _matmul_residual(merged, w_mo, x1, gate[1], 1.0, norm=(ng[2], shift[2], scale[2]),
                                     name="mix_out")

    x3, saved2, w_in2, w_out2, _ = _ffn_fwd(x2, h2, gate[2], started[2], started[3], "ffn2")
    loss_row, dx3, dfg, dy2, dgate2 = _final_loss(x3, fg, target, (saved2[3], gate[2], 0.5), "loss")

    (dx2, dsh2, dsc2, dng2, dmy, dgate1), sent2 = _ffn_bwd(dx3, dy2, x2, ng[2], scale[2], w_in2, w_out2, saved2,
                                                           "ffn2", (mix_y, gate[1], 1.0))

    dmerged = _matmul(dmy, w_mo, tb=True, out_dtype=BF16, name="mix_out_dx")
    dw_mo = _matmul(merged, dmy, ta=True, out_dtype=BF16, name="mix_out_dw")
    dproj = lax.empty((t, MIXP), BF16)
    dya, dyb, dproj = _merge_bwd(dmerged, ya, yb, proj, dproj, "merge_bwd")
    dob = _matmul(dyb, w_dn, tb=True, out_dtype=F32, name="dn_out_dx")
    dw_dn = _matmul(ob, dyb, ta=True, out_dtype=BF16, name="dn_out_dw")
    do, dproj, dgn = _dn_post_bwd(o, proj, gn, dob, dproj, "dn_post_bwd")
    du, dw, dqk, dqd, dkd, deg = _dn_scan_bwd(u, w, qk, qd, kd, eg, s_saved, do, "dn_scan_bwd")
    dqh, dkh, dvh, dbg = _dn_local_bwd(qh, kh, vh, bg, inv, du, dw, dqk, dqd, dkd, deg, "dn_local_bwd")
    dconv, dproj, dal, ddt = _dn_pre_bwd_act(proj, conv_full, al_row, dt_row, dqh, dkh, dvh, dbg, dproj,
                                             "dn_pre_bwd_act")
    dproj, dcw = _dn_pre_bwd_conv(proj, conv_full, dconv, dproj, "dn_pre_bwd_conv")
    dwin, dpl, dpw, dps, dpp = _pool_bwd_local(proj, pw, ps, w_pp, dya, "pool_bwd_local")
    dproj = _pool_bwd_window(dwin, dpl, dproj, "pool_bwd_window")
    dw_mix = _matmul(dproj, h1, ta=True, out_dtype=BF16, name="mix_in_dw")
    sent1, token = _push_start(
        [_mix_unpad(dw_mix).reshape(NDEV, MIX_RAW // NDEV, D), _cols_to_blocks(dpp.astype(BF16)),
         dw_dn.reshape(NDEV, -1, D), dw_mo.reshape(NDEV, -1, D)], True, "mix_grad_start")
    dh1 = _matmul(dproj, w_mix, out_dtype=F32, name="mix_in_dx", after=token)
    dx1, dsh1, dsc1, dng1, dy0, dgate0 = _norm_mod_bwd(x1, ng[1], scale[1], dh1, dx2, "mix_norm_bwd",
                                                       (saved1[3], gate[0], 0.5))

    (dx0, dsh0, dsc0, dng0), sent0 = _ffn_bwd(dx1, dy0, x0, ng[0], scale[0], w_in1, w_out1, saved1, "ffn1")

    dmod = jnp.concatenate([dsh0, dsc0, dgate0, dsh1, dsc1, dgate1, dsh2, dsc2, dgate2], axis=1).reshape(-1)
    flat = jnp.concatenate([
        dmod, dal[0, NH:2 * NH], ddt[0, NH:2 * NH], dgn.reshape(-1), dps.reshape(-1), dfg.reshape(-1),
        dpw.reshape(-1), jnp.concatenate([dng0, dng1, dng2], axis=0).reshape(-1), dcw.reshape(-1),
        loss_row[0, 0:1]])
    nflat = 90 * D
    flat = jnp.concatenate([flat, jnp.zeros((nflat - flat.shape[0],), F32)]).reshape(90, D)
    sent_small, small_token = _push_start([flat], False, "small_grad_start")

    def small_grads(flat_all):
        tot = _sum_devices(flat_all, F32, "sum_small_grads").reshape(-1)
        dmod_all = flat_all.reshape(NDEV, nflat)[:, :9 * D]
        dmod_cols = lax.dynamic_slice(dmod_all, (0, me * ncol), (NDEV, ncol))
        g_ada_w = _ada_bwd(c_all.T, dmod_cols, "ada_bwd")
        p = 0
        pieces = {}
        for nm, size in (("ada_b", 9 * D), ("a_log", NH), ("dt_bias", NH), ("dn_norm_g", HD), ("pool_scale", PW),
                         ("final_g", D), ("pool_w", 4 * PG * PG), ("norm_g", 3 * D), ("conv_w", 12 * D),
                         ("loss", 1)):
            pieces[nm] = tot[p:p + size]
            p += size
        g_norm = lax.dynamic_slice(pieces["norm_g"].reshape(3, D), (0, me * 128), (3, 128))
        g_conv = lax.dynamic_slice(pieces["conv_w"].reshape(4, 3 * D), (0, me * 384), (4, 384))
        return pieces["loss"][0], {
            "ada_w": g_ada_w.reshape(ada_w.shape), "ada_b": pieces["ada_b"].reshape(ada_b.shape),
            "norm_g": g_norm.reshape(norm_g.shape), "conv_w": g_conv.reshape(conv_w.shape),
            "a_log": pieces["a_log"].reshape(a_log.shape), "dt_bias": pieces["dt_bias"].reshape(dt_bias.shape),
            "dn_norm_g": pieces["dn_norm_g"].reshape(dn_norm_g.shape),
            "pool_w": pieces["pool_w"].reshape(pool_w.shape),
            "pool_scale": pieces["pool_scale"].reshape(pool_scale.shape),
            "final_g": pieces["final_g"].reshape(final_g.shape),
        }

    grads = {}
    weights = {"ada_w": ada_w, "ada_b": ada_b, "norm_g": norm_g, "ffn1_w_in": ffn1_w_in, "ffn1_w_out": ffn1_w_out,
               "ffn2_w_in": ffn2_w_in, "ffn2_w_out": ffn2_w_out, "mix_w_in": mix_w_in, "conv_w": conv_w,
               "a_log": a_log, "dt_bias": dt_bias, "dn_norm_g": dn_norm_g, "pool_w": pool_w,
               "pool_scale": pool_scale, "pool_proj": pool_proj, "dn_proj": dn_proj, "mix_w_out": mix_w_out,
               "final_g": final_g}
    m_in = {"ada_w": m_ada_w, "ada_b": m_ada_b, "norm_g": m_norm_g, "ffn1_w_in": m_ffn1_w_in,
            "ffn1_w_out": m_ffn1_w_out, "ffn2_w_in": m_ffn2_w_in, "ffn2_w_out": m_ffn2_w_out,
            "mix_w_in": m_mix_w_in, "conv_w": m_conv_w, "a_log": m_a_log, "dt_bias": m_dt_bias,
            "dn_norm_g": m_dn_norm_g, "pool_w": m_pool_w, "pool_scale": m_pool_scale, "pool_proj": m_pool_proj,
            "dn_proj": m_dn_proj, "mix_w_out": m_mix_w_out, "final_g": m_final_g}
    v_in = {"ada_w": v_ada_w, "ada_b": v_ada_b, "norm_g": v_norm_g, "ffn1_w_in": v_ffn1_w_in,
            "ffn1_w_out": v_ffn1_w_out, "ffn2_w_in": v_ffn2_w_in, "ffn2_w_out": v_ffn2_w_out,
            "mix_w_in": v_mix_w_in, "conv_w": v_conv_w, "a_log": v_a_log, "dt_bias": v_dt_bias,
            "dn_norm_g": v_dn_norm_g, "pool_w": v_pool_w, "pool_scale": v_pool_scale, "pool_proj": v_pool_proj,
            "dn_proj": v_dn_proj, "mix_w_out": v_mix_w_out, "final_g": v_final_g}

    names = list(weights)
    large = ("ada_w", "ffn1_w_in", "ffn1_w_out", "ffn2_w_in", "ffn2_w_out", "mix_w_in", "pool_proj", "dn_proj",
             "mix_w_out")
    delta, new_m, new_v = {}, {}, {}

    flipped = ("ffn1_w_in", "ffn2_w_in", "mix_w_in")

    def views(nm):
        shp = weights[nm].shape
        two_d = (shp[-2], shp[-1])
        if nm in flipped:
            return (lambda a: a.reshape(two_d).T), (lambda a: a.T.reshape(shp))
        return (lambda a: a.reshape(two_d)), (lambda a: a.reshape(shp))

    def reduce_update(sent, group, after, tag):
        done = []
        for nm, r in zip(group, _push_wait(sent, True, after, f"{tag}_grad_wait")):
            view, back = views(nm)
            g_, d_, m_, v_ = _reduce_adamw(r, view(weights[nm]), view(m_in[nm]), view(v_in[nm]), f"adamw_{nm}")
            grads[nm], delta[nm], new_m[nm], new_v[nm] = back(g_), back(d_), back(m_), back(v_)
            done.append(d_)
        return done

    done = reduce_update(sent2, ("ffn2_w_in", "ffn2_w_out"), small_token, "ffn2")
    done += reduce_update(sent1, ("mix_w_in", "pool_proj", "dn_proj", "mix_w_out"), done, "mix")
    flat_all, = _push_wait(sent_small, False, done, "small_grad_wait")
    loss, small = small_grads(flat_all)
    grads.update(small)
    view, back = views("ada_w")
    done, m_, v_ = _adamw(view(ada_w), view(grads["ada_w"]), view(m_ada_w), view(v_ada_w), "adamw_ada_w")
    delta["ada_w"], new_m["ada_w"], new_v["ada_w"] = back(done), back(m_), back(v_)
    reduce_update(sent0, ("ffn1_w_in", "ffn1_w_out"), done, "ffn1")
    rest = [nm for nm in names if nm not in large]
    total = sum(weights[nm].size for nm in rest)
    padded = -(-total // D) * D

    def pack(tree, fill):
        flat_ = jnp.concatenate([tree[nm].reshape(-1) for nm in rest])
        return jnp.concatenate([flat_, jnp.full((padded - total,), fill, F32)]).reshape(-1, D)

    d_, m_, v_ = _adamw(pack(weights, 0.0), pack(grads, 0.0), pack(m_in, 0.0), pack(v_in, 1.0), "adamw_small")
    p = 0
    for nm in rest:
        size = weights[nm].size
        shp = weights[nm].shape
        delta[nm] = d_.reshape(-1)[p:p + size].reshape(shp)
        new_m[nm] = m_.reshape(-1)[p:p + size].reshape(shp)
        new_v[nm] = v_.reshape(-1)[p:p + size].reshape(shp)
        p += size

    grad_x = dx0.reshape(x.shape)
    return (loss, grad_x, *[grads[nm] for nm in names], *[delta[nm] for nm in names],
            *[new_m[nm] for nm in names], *[new_v[nm] for nm in names])
```

```python
import functools

import jax
import jax.numpy as jnp
from jax import lax
from jax.experimental import pallas as pl
from jax.experimental.pallas import tpu as pltpu

F32 = jnp.float32
BF16 = jnp.bfloat16
SDS = jax.ShapeDtypeStruct
HI = lax.Precision.HIGHEST

D = 1024
FH = 2816
FB = 704
NH = 8
HD = 128
CH = 64
SCAN_CHUNKS = 8
NDEV = 8
PW = 512
PG = 128
RMS_EPS = 1e-6
L2_EPS = 1e-6
TR = 512
HALO = 16
VMEM_LIMIT = 56 * 1024 * 1024
MATMUL_VMEM = 40 * 1024 * 1024

MIXP = 6912
OFF_Q, OFF_K, OFF_V, OFF_Z, OFF_GP, OFF_GD, OFF_XP, OFF_BA = 0, 1024, 2048, 3072, 4096, 5120, 6144, 6656
MIX_RAW = 6672

ADAM_LR = 0.001
ADAM_B1 = 0.9
ADAM_B2 = 0.999
ADAM_EPS = 1e-08
ADAM_WD = 0.01
ADAM_STEP = 10

NN = (((1,), (0,)), ((), ()))
NT = (((1,), (1,)), ((), ()))
TN = (((0,), (0,)), ((), ()))


def _dg(a, b, dims, prec=None):
    return lax.dot_general(a, b, dims, precision=prec, preferred_element_type=F32)


def _make_dots(prec):
    @jax.custom_vjp
    def nn(a, b):
        return _dg(a, b, NN, prec)

    @jax.custom_vjp
    def nt(a, b):
        return _dg(a, b, NT, prec)

    @jax.custom_vjp
    def tn(a, b):
        return _dg(a, b, TN, prec)

    nn.defvjp(lambda a, b: (nn(a, b), (a, b)), lambda r, d: (nt(d, r[1]), tn(r[0], d)))
    nt.defvjp(lambda a, b: (nt(a, b), (a, b)), lambda r, d: (nn(d, r[1]), tn(d, r[0])))
    tn.defvjp(lambda a, b: (tn(a, b), (a, b)), lambda r, d: (nt(r[1], d), nn(r[0], d)))
    return nn, nt, tn


_nn, _nt, _tn = _make_dots(None)


def _params(sem):
    return pltpu.CompilerParams(dimension_semantics=sem, vmem_limit_bytes=VMEM_LIMIT)


def _sigmoid(x):
    return 1.0 / (1.0 + jnp.exp(-x))


def _silu(x):
    return x * _sigmoid(x)


def _dsilu(x):
    s = _sigmoid(x)
    return s * (1.0 + x * (1.0 - s))


def _pick(n, cands):
    for c in cands:
        if n % c == 0:
            return c
    raise ValueError(f"no tile for {n}")


def _iota(shape, dim):
    return lax.broadcasted_iota(jnp.int32, shape, dim)


def _matmul(a, b, *, ta=False, tb=False, a_blk=False, b_blk=False, o_blk=False, tm=None, tn=None, tk=None,
            out_dtype, name, after=None):
    if a_blk:
        nb, r, cb = a.shape
        if ta:
            k_dim, m_dim, tm = r, nb * cb, cb
        else:
            m_dim, k_dim, tk = r, nb * cb, cb
    else:
        k_dim, m_dim = a.shape if ta else a.shape[::-1]
    if b_blk:
        nb, r, cb = b.shape
        if tb:
            n_dim, tk = r, cb
            assert nb * cb == k_dim
        else:
            n_dim, tn = nb * cb, cb
            assert r == k_dim
    else:
        n_dim = b.shape[0] if tb else b.shape[1]
    tn = tn or _pick(n_dim, (1024, 768, 512, 256, 128))
    out_bytes = jnp.dtype(out_dtype).itemsize

    def vmem(tm_, tk_):
        return 4 * tk_ * (tm_ + tn) + tm_ * tn * (4 + 2 * out_bytes)

    k_cands = [tk] if tk else [c for c in (k_dim, 4096, 3456, 2816, 2304, 2048, 1024, 512, 256)
                               if c <= k_dim and k_dim % c == 0]
    m_cands = [tm] if tm else [c for c in (2048, 1024, 768, 512, 256, 128) if m_dim % c == 0]
    base = next((c for c in m_cands if c <= 1024), m_cands[-1])
    tk = next((c for c in k_cands if vmem(base, c) <= MATMUL_VMEM), k_cands[-1])
    tm = next((c for c in m_cands if vmem(c, tk) <= MATMUL_VMEM), m_cands[-1])
    nk = k_dim // tk
    dims = ((((0,) if ta else (1,)), ((1,) if tb else (0,))), ((), ()))

    def body(a_ref, b_ref, *rest):
        o_ref, acc_ref = rest[-2:]
        k = pl.program_id(2)

        @pl.when(k == 0)
        def _():
            acc_ref[...] = jnp.zeros_like(acc_ref)

        acc_ref[...] += lax.dot_general(a_ref[...].astype(BF16), b_ref[...].astype(BF16), dims,
                                        preferred_element_type=F32)

        @pl.when(k == nk - 1)
        def _():
            o_ref[...] = acc_ref[...].astype(o_ref.dtype)

    if a_blk:
        a_spec = (pl.BlockSpec((None, tk, tm), lambda i, j, k: (i, k, 0)) if ta
                  else pl.BlockSpec((None, tm, tk), lambda i, j, k: (k, i, 0)))
    else:
        a_spec = (pl.BlockSpec((tk, tm), lambda i, j, k: (k, i)) if ta
                  else pl.BlockSpec((tm, tk), lambda i, j, k: (i, k)))
    if b_blk:
        b_spec = (pl.BlockSpec((None, tn, tk), lambda i, j, k: (k, j, 0)) if tb
                  else pl.BlockSpec((None, tk, tn), lambda i, j, k: (j, k, 0)))
    else:
        b_spec = (pl.BlockSpec((tn, tk), lambda i, j, k: (j, k)) if tb
                  else pl.BlockSpec((tk, tn), lambda i, j, k: (k, j)))
    if o_blk:
        o_spec = pl.BlockSpec((None, tm, tn), lambda i, j, k: (j, i, 0))
        o_shape = SDS((n_dim // tn, m_dim, tn), out_dtype)
    else:
        o_spec = pl.BlockSpec((tm, tn), lambda i, j, k: (i, j))
        o_shape = SDS((m_dim, n_dim), out_dtype)
    return pl.pallas_call(
        body, grid=(m_dim // tm, n_dim // tn, nk),
        in_specs=[a_spec, b_spec] + ([] if after is None else [pl.BlockSpec(memory_space=pl.ANY)]),
        out_specs=o_spec,
        out_shape=o_shape,
        scratch_shapes=[pltpu.VMEM((tm, tn), F32)],
        compiler_params=_params(("parallel", "parallel", "arbitrary")),
        name=name,
    )(a, b, *([] if after is None else [after]))


def _matmul_residual(a, b, x, gate, coef, *, a_blk=False, norm=None, name, after=None):
    m_dim = a.shape[-2]
    tm = _pick(m_dim, (1024, 512))
    if a_blk:
        nk, _, tk = a.shape
        a_spec = pl.BlockSpec((None, tm, tk), lambda i, k: (k, i, 0))
    else:
        tk = a.shape[1]
        nk = 1
        a_spec = pl.BlockSpec((tm, tk), lambda i, k: (i, 0))
    extra = [] if after is None else [after]
    vecs = [gate] + (list(norm) if norm else [])

    def body(a_ref, b_ref, x_ref, gate_ref, *rest):
        vec_refs = rest[:len(vecs) - 1]
        outs = rest[len(vecs) - 1 + len(extra):]
        acc_ref = outs[-1]
        k = pl.program_id(1)

        @pl.when(k == 0)
        def _():
            acc_ref[...] = jnp.zeros_like(acc_ref)

        acc_ref[...] += _dg(a_ref[...], b_ref[...], NN)

        @pl.when(k == nk - 1)
        def _():
            y = acc_ref[...]
            xn = x_ref[...] + (coef * gate_ref[...]) * y
            outs[0][...] = xn
            outs[1][...] = y.astype(outs[1].dtype)
            if norm:
                g_ref, sh_ref, sc_ref = vec_refs
                r = lax.rsqrt(jnp.mean(xn * xn, axis=-1, keepdims=True) + RMS_EPS)
                outs[2][...] = (((xn * r) * g_ref[...]) * (1.0 + sc_ref[...]) + sh_ref[...]).astype(outs[2].dtype)

    row = pl.BlockSpec((tm, D), lambda i, k: (i, 0))
    vec = pl.BlockSpec((1, D), lambda i, k: (0, 0))
    return pl.pallas_call(
        body, grid=(m_dim // tm, nk),
        in_specs=[a_spec, pl.BlockSpec((tk, D), lambda i, k: (k, 0)), row] + [vec] * len(vecs)
        + [pl.BlockSpec(memory_space=pl.ANY)] * len(extra),
        out_specs=[row] * (3 if norm else 2),
        out_shape=[SDS((m_dim, D), F32), SDS((m_dim, D), BF16)] + ([SDS((m_dim, D), BF16)] if norm else []),
        scratch_shapes=[pltpu.VMEM((tm, D), F32)],
        compiler_params=_params(("parallel", "arbitrary")), name=name,
    )(a, b, x, *vecs, *extra)


def _row(width, col=0):
    return pl.BlockSpec((TR, width), lambda i: (i, col))


def _vec(width):
    return pl.BlockSpec((1, width), lambda i: (0, 0))


def _norm_mod_fwd(x, g, shift, scale, name, after=None):
    t = x.shape[0]
    extra = [] if after is None else [after]

    def body(x_ref, g_ref, sh_ref, sc_ref, *rest):
        o_ref = rest[-1]
        xv = x_ref[...]
        r = lax.rsqrt(jnp.mean(xv * xv, axis=-1, keepdims=True) + RMS_EPS)
        o_ref[...] = (((xv * r) * g_ref[...]) * (1.0 + sc_ref[...]) + sh_ref[...]).astype(o_ref.dtype)

    return pl.pallas_call(
        body, grid=(t // TR,),
        in_specs=[_row(D), _vec(D), _vec(D), _vec(D)] + [pl.BlockSpec(memory_space=pl.ANY)] * len(extra),
        out_specs=_row(D),
        out_shape=SDS((t, D), BF16), compiler_params=_params(("parallel",)), name=name,
    )(x, g, shift, scale, *extra)


def _residual_branch_bwd(dxv, y_ref, gate_ref, coef, dy_ref, dgate_ref):
    dy_ref[...] = ((coef * gate_ref[...]) * dxv).astype(dy_ref.dtype)
    dgate_ref[...] += jnp.sum((coef * dxv) * y_ref[...], axis=0, keepdims=True)


def _norm_mod_bwd(x, g, scale, dh, dx_in, name, below=None):
    t = x.shape[0]
    lower = [] if below is None else list(below[:2])

    def body(x_ref, g_ref, sc_ref, dh_ref, dxi_ref, *rest):
        dx_ref, dsh_ref, dsc_ref, dg_ref = rest[len(lower):len(lower) + 4]

        @pl.when(pl.program_id(0) == 0)
        def _():
            for ref in rest[len(lower) + 1:]:
                if ref.shape[0] == 1:
                    ref[...] = jnp.zeros_like(ref)

        xv = x_ref[...]
        gv = g_ref[...]
        dh = dh_ref[...]
        r = lax.rsqrt(jnp.mean(xv * xv, axis=-1, keepdims=True) + RMS_EPS)
        n = xv * r
        dsh_ref[...] += jnp.sum(dh, axis=0, keepdims=True)
        dsc_ref[...] += jnp.sum(dh * (n * gv), axis=0, keepdims=True)
        tt = dh * (1.0 + sc_ref[...])
        dg_ref[...] += jnp.sum(tt * n, axis=0, keepdims=True)
        dn = tt * gv
        dxv = dxi_ref[...] + r * (dn - n * jnp.mean(dn * n, axis=-1, keepdims=True))
        dx_ref[...] = dxv
        if below is not None:
            _residual_branch_bwd(dxv, rest[0], rest[1], below[2], rest[-2], rest[-1])

    more_in = [] if below is None else [_row(D), _vec(D)]
    more_out = [] if below is None else [_row(D), _vec(D)]
    more_shape = [] if below is None else [SDS((t, D), BF16), SDS((1, D), F32)]
    return pl.pallas_call(
        body, grid=(t // TR,), in_specs=[_row(D), _vec(D), _vec(D), _row(D), _row(D)] + more_in,
        out_specs=[_row(D), _vec(D), _vec(D), _vec(D)] + more_out,
        out_shape=[SDS((t, D), F32), SDS((1, D), F32), SDS((1, D), F32), SDS((1, D), F32)] + more_shape,
        compiler_params=_params(("arbitrary",)), name=name,
    )(x, g, scale, dh, dx_in, *lower)


def _swiglu_up(h, w_in, name, after=None):
    t = h.shape[0]
    tm = _pick(t, (1024, 512, 256))
    half = NDEV // 2
    extra = [] if after is None else [after]

    def body(h_ref, wg_ref, wu_ref, *rest):
        u_ref, a_ref = rest[-2:]
        hv = h_ref[...]
        gate = _dg(hv, wg_ref[...], NT)
        up = _dg(hv, wu_ref[...], NT)
        u_ref[0] = gate.astype(u_ref.dtype)
        u_ref[1] = up.astype(u_ref.dtype)
        a_ref[...] = (_silu(gate) * up).astype(a_ref.dtype)

    return pl.pallas_call(
        body, grid=(t // tm, half),
        in_specs=[pl.BlockSpec((tm, D), lambda i, j: (i, 0)),
                  pl.BlockSpec((FB, D), lambda i, j: (j, 0)),
                  pl.BlockSpec((FB, D), lambda i, j: (j + half, 0))]
        + [pl.BlockSpec(memory_space=pl.ANY)] * len(extra),
        out_specs=[pl.BlockSpec((2, None, tm, FB), lambda i, j: (0, j, i, 0)),
                   pl.BlockSpec((None, tm, FB), lambda i, j: (j, i, 0))],
        out_shape=[SDS((2, half, t, FB), BF16), SDS((half, t, FB), BF16)],
        compiler_params=_params(("parallel", "parallel")), name=name,
    )(h, w_in, w_in, *extra)


def _swiglu_down_bwd(dy, w_out, u, name, after=None):
    t = dy.shape[0]
    tm = _pick(t, (1024, 512, 256))
    half = NDEV // 2
    extra = [] if after is None else [after]
    pair = pl.BlockSpec((2, None, tm, FB), lambda i, j: (0, j, i, 0))

    def body(dy_ref, w_ref, u_ref, *rest):
        o_ref = rest[-1]
        da = _dg(dy_ref[...], w_ref[...], NT)
        gate = u_ref[0].astype(F32)
        o_ref[0] = (da * u_ref[1].astype(F32) * _dsilu(gate)).astype(o_ref.dtype)
        o_ref[1] = (da * _silu(gate)).astype(o_ref.dtype)

    return pl.pallas_call(
        body, grid=(t // tm, half),
        in_specs=[pl.BlockSpec((tm, D), lambda i, j: (i, 0)), pl.BlockSpec((FB, D), lambda i, j: (j, 0)), pair]
        + [pl.BlockSpec(memory_space=pl.ANY)] * len(extra),
        out_specs=pair, out_shape=SDS((2, half, t, FB), BF16),
        compiler_params=_params(("parallel", "parallel")), name=name,
    )(dy, w_out, u, *extra)


def _final_loss(x, fg, target, below, name):
    t = x.shape[0]
    nt = t // TR

    def body(x_ref, g_ref, t_ref, y_ref, gate_ref, loss_ref, dx_ref, dg_ref, dy_ref, dgate_ref, acc_ref):
        i = pl.program_id(0)

        @pl.when(i == 0)
        def _():
            acc_ref[...] = jnp.zeros_like(acc_ref)
            dg_ref[...] = jnp.zeros_like(dg_ref)
            dgate_ref[...] = jnp.zeros_like(dgate_ref)

        xv = x_ref[...]
        gv = g_ref[...]
        r = lax.rsqrt(jnp.mean(xv * xv, axis=-1, keepdims=True) + RMS_EPS)
        n = xv * r
        err = n * gv - t_ref[...]
        acc_ref[...] += jnp.sum(err * err, axis=0, keepdims=True)
        dy = err * (1.0 / D)
        dg_ref[...] += jnp.sum(dy * n, axis=0, keepdims=True)
        dn = dy * gv
        dxv = r * (dn - n * jnp.mean(dn * n, axis=-1, keepdims=True))
        dx_ref[...] = dxv
        _residual_branch_bwd(dxv, y_ref, gate_ref, below[2], dy_ref, dgate_ref)

        @pl.when(i == nt - 1)
        def _():
            tot = jnp.sum(acc_ref[...], axis=1, keepdims=True) * (0.5 / D)
            loss_ref[...] = jnp.broadcast_to(tot, loss_ref.shape)

    return pl.pallas_call(
        body, grid=(nt,), in_specs=[_row(D), _vec(D), _row(D), _row(D), _vec(D)],
        out_specs=[_vec(128), _row(D), _vec(D), _row(D), _vec(D)],
        out_shape=[SDS((1, 128), F32), SDS((t, D), F32), SDS((1, D), F32), SDS((t, D), BF16), SDS((1, D), F32)],
        scratch_shapes=[pltpu.VMEM((1, D), F32)],
        compiler_params=_params(("arbitrary",)), name=name,
    )(x, fg, target, below[0], below[1])


def _halo_prev(width, col):
    per = TR // HALO
    return pl.BlockSpec((HALO, width), lambda i: (jnp.maximum(i * per - 1, 0), col))


def _halo_next(width, col, nt):
    per = TR // HALO
    return pl.BlockSpec((HALO, width), lambda i: (jnp.minimum((i + 1) * per, nt * per - 1), col))


def _pool_windows(ext, tile_index):
    rows = _iota((TR, PG), 0) + tile_index * TR + 1
    pooled, counts = [], []
    for gi in range(4):
        w = 2 << gi
        e = ext[:, gi * PG:(gi + 1) * PG]
        s = e
        step = 1
        while step < w:
            s = s + pltpu.roll(s, step, 0)
            step *= 2
        cnt = jnp.minimum(rows, w).astype(F32)
        pooled.append(s[HALO:] / cnt - e[HALO:])
        counts.append(cnt)
    return pooled, counts


def _pool_fwd(proj, pool_w, pool_scale, pool_proj, name):
    t = proj.shape[0]
    xcol = OFF_XP // PW

    def body(x_ref, h_ref, pw_ref, ps_ref, pp_ref, o_ref):
        i = pl.program_id(0)
        halo = jnp.where(i > 0, h_ref[...], 0.0)
        ext = jnp.concatenate([halo, x_ref[...]], axis=0)
        pooled, _ = _pool_windows(ext, i)
        mixed = [_dg(pooled[g].astype(BF16), pw_ref[g].astype(BF16), NN) for g in range(4)]
        ypre = jnp.concatenate(mixed, axis=1) * ps_ref[...]
        o_ref[...] = _dg(ypre.astype(BF16), pp_ref[...], NN)

    return pl.pallas_call(
        body, grid=(t // TR,),
        in_specs=[_row(PW, xcol), _halo_prev(PW, xcol),
                  pl.BlockSpec((4, PG, PG), lambda i: (0, 0, 0)), _vec(PW),
                  pl.BlockSpec((PW, D), lambda i: (0, 0))],
        out_specs=_row(D), out_shape=SDS((t, D), F32),
        compiler_params=_params(("parallel",)), name=name,
    )(proj, proj, pool_w, pool_scale, pool_proj)


def _pool_bwd_local(proj, pool_w, pool_scale, pool_proj, dya, name):
    t = proj.shape[0]
    xcol = OFF_XP // PW

    def body(x_ref, h_ref, pw_ref, ps_ref, pp_ref, dya_ref, dwin_ref, dpl_ref, dpw_ref, dps_ref, dpp_ref):
        i = pl.program_id(0)

        @pl.when(i == 0)
        def _():
            dpw_ref[...] = jnp.zeros_like(dpw_ref)
            dps_ref[...] = jnp.zeros_like(dps_ref)
            dpp_ref[...] = jnp.zeros_like(dpp_ref)

        halo = jnp.where(i > 0, h_ref[...], 0.0)
        ext = jnp.concatenate([halo, x_ref[...]], axis=0)
        pooled, counts = _pool_windows(ext, i)
        mixed = jnp.concatenate(
            [_dg(pooled[g].astype(BF16), pw_ref[g].astype(BF16), NN) for g in range(4)], axis=1)
        ps = ps_ref[...]
        ypre = mixed * ps
        dyab = dya_ref[...].astype(BF16)
        dypre = _dg(dyab, pp_ref[...], NT)
        dpp_ref[...] += _dg(ypre.astype(BF16), dyab, TN)
        dps_ref[...] += jnp.sum(dypre * mixed, axis=0, keepdims=True)
        dmixed = dypre * ps
        for g in range(4):
            dm = dmixed[:, g * PG:(g + 1) * PG].astype(BF16)
            dpw_ref[g] += _dg(pooled[g].astype(BF16), dm, TN)
            dpooled = _dg(dm, pw_ref[g].astype(BF16), NT)
            dwin_ref[:, g * PG:(g + 1) * PG] = dpooled / counts[g]
            dpl_ref[:, g * PG:(g + 1) * PG] = dpooled

    return pl.pallas_call(
        body, grid=(t // TR,),
        in_specs=[_row(PW, xcol), _halo_prev(PW, xcol),
                  pl.BlockSpec((4, PG, PG), lambda i: (0, 0, 0)), _vec(PW),
                  pl.BlockSpec((PW, D), lambda i: (0, 0)), _row(D)],
        out_specs=[_row(PW), _row(PW), pl.BlockSpec((4, PG, PG), lambda i: (0, 0, 0)), _vec(PW),
                   pl.BlockSpec((PW, D), lambda i: (0, 0))],
        out_shape=[SDS((t, PW), F32), SDS((t, PW), F32), SDS((4, PG, PG), F32), SDS((1, PW), F32),
                   SDS((PW, D), F32)],
        compiler_params=_params(("arbitrary",)), name=name,
    )(proj, proj, pool_w, pool_scale, pool_proj, dya)


def _pool_bwd_window(dwin, dpl, dproj, name):
    t = dwin.shape[0]
    nt = t // TR
    ext_rows = TR + HALO

    def body(dw_ref, h_ref, dp_ref, _, o_ref):
        i = pl.program_id(0)
        halo = jnp.where(i < nt - 1, h_ref[...], 0.0)
        ext = jnp.concatenate([dw_ref[...], halo], axis=0)
        for gi in range(4):
            w = 2 << gi
            s = ext[:, gi * PG:(gi + 1) * PG]
            step = 1
            while step < w:
                s = s + pltpu.roll(s, ext_rows - step, 0)
                step *= 2
            o_ref[:, gi * PG:(gi + 1) * PG] = (s[:TR] - dp_ref[:, gi * PG:(gi + 1) * PG]).astype(o_ref.dtype)

    return pl.pallas_call(
        body, grid=(nt,),
        in_specs=[_row(PW), _halo_next(PW, 0, nt), _row(PW), pl.BlockSpec(memory_space=pl.ANY)],
        out_specs=_into(PW, OFF_XP), out_shape=SDS(dproj.shape, dproj.dtype), input_output_aliases={3: 0},
        compiler_params=_params(("parallel",)), name=name,
    )(dwin, dwin, dpl, dproj)


def _conv_group(ext, cw_ref, cols):
    acc = cw_ref[3:4, cols] * ext
    for j in range(3):
        acc = acc + cw_ref[j:j + 1, cols] * pltpu.roll(ext, 3 - j, 0)
    return acc[HALO:]


def _gate_terms(raw, al, dt):
    beta = _sigmoid(raw)
    xg = raw + dt
    sp = jnp.maximum(xg, 0.0) + jnp.log(1.0 + jnp.exp(-jnp.abs(xg)))
    g = -jnp.exp(al) * sp
    return beta, g, _sigmoid(xg)


def _dn_pre_fwd(proj, conv_w, al_row, dt_row, name):
    t = proj.shape[0]

    def body(x_ref, h_ref, cw_ref, ba_ref, al_ref, dt_ref, q_ref, k_ref, v_ref, bg_ref):
        i = pl.program_id(0)
        keep = i > 0
        for grp in range(24):
            cols = slice(grp * HD, (grp + 1) * HD)
            ext = jnp.concatenate([jnp.where(keep, h_ref[:, cols], 0.0), x_ref[:, cols]], axis=0)
            s = _silu(_conv_group(ext, cw_ref, cols))
            seg, head = divmod(grp, NH)
            hc = slice(head * HD, (head + 1) * HD)
            if seg == 0:
                q_ref[:, hc] = s * lax.rsqrt(jnp.sum(s * s, axis=-1, keepdims=True) + L2_EPS) * (HD ** -0.5)
            elif seg == 1:
                k_ref[:, hc] = s * lax.rsqrt(jnp.sum(s * s, axis=-1, keepdims=True) + L2_EPS)
            else:
                v_ref[:, hc] = s
        lane = _iota((TR, 128), 1)
        rowc = _iota((TR, 128), 0) % CH
        beta, g, _ = _gate_terms(ba_ref[...], al_ref[...], dt_ref[...])
        step = 1
        while step < CH:
            g = g + jnp.where(rowc >= step, pltpu.roll(g, step, 0), 0.0)
            step *= 2
        bg_ref[...] = jnp.where(lane < NH, beta, jnp.where(lane < 2 * NH, g, 0.0))

    return pl.pallas_call(
        body, grid=(t // TR,),
        in_specs=[_row(3 * D, 0), _halo_prev(3 * D, 0), pl.BlockSpec((4, 3 * D), lambda i: (0, 0)),
                  _row(128, OFF_BA // 128), _vec(128), _vec(128)],
        out_specs=[_row(D), _row(D), _row(D), _row(128)],
        out_shape=[SDS((t, D), F32), SDS((t, D), F32), SDS((t, D), F32), SDS((t, 128), F32)],
        compiler_params=_params(("parallel",)), name=name,
    )(proj, proj, conv_w, proj, al_row, dt_row)


def _dn_pre_bwd_act(proj, conv_w, al_row, dt_row, dq, dk, dv, dbg, dproj, name):
    t = proj.shape[0]

    def body(x_ref, h_ref, cw_ref, ba_ref, al_ref, dt_ref, dq_ref, dk_ref, dv_ref, dbg_ref, _,
             dc_ref, draw_ref, dal_ref, ddt_ref):
        i = pl.program_id(0)

        @pl.when(i == 0)
        def _():
            dal_ref[...] = jnp.zeros_like(dal_ref)
            ddt_ref[...] = jnp.zeros_like(ddt_ref)

        keep = i > 0
        for grp in range(24):
            cols = slice(grp * HD, (grp + 1) * HD)
            ext = jnp.concatenate([jnp.where(keep, h_ref[:, cols], 0.0), x_ref[:, cols]], axis=0)
            cv = _conv_group(ext, cw_ref, cols)
            seg, head = divmod(grp, NH)
            hc = slice(head * HD, (head + 1) * HD)
            if seg == 2:
                ds = dv_ref[:, hc]
            else:
                s = _silu(cv)
                r = lax.rsqrt(jnp.sum(s * s, axis=-1, keepdims=True) + L2_EPS)
                dy = dq_ref[:, hc] if seg == 0 else dk_ref[:, hc]
                c = (HD ** -0.5) if seg == 0 else 1.0
                ds = (c * r) * (dy - s * ((r * r) * jnp.sum(dy * s, axis=-1, keepdims=True)))
            dc_ref[:, cols] = ds * _dsilu(cv)
        lane = _iota((TR, 128), 1)
        rowc = _iota((TR, 128), 0) % CH
        isb = lane < NH
        isg = jnp.logical_and(lane >= NH, lane < 2 * NH)
        beta, g, sg = _gate_terms(ba_ref[...], al_ref[...], dt_ref[...])
        dbgv = dbg_ref[...]
        dg = dbgv
        step = 1
        while step < CH:
            dg = dg + jnp.where(rowc < CH - step, pltpu.roll(dg, TR - step, 0), 0.0)
            step *= 2
        da_raw = dg * (-jnp.exp(al_ref[...])) * sg
        draw = jnp.where(isb, dbgv * beta * (1.0 - beta), jnp.where(isg, da_raw, 0.0))
        draw_ref[:, :128] = draw.astype(draw_ref.dtype)
        draw_ref[:, 128:] = jnp.zeros((TR, MIXP - OFF_BA - 128), draw_ref.dtype)
        dal_ref[...] += jnp.sum(jnp.where(isg, dg * g, 0.0), axis=0, keepdims=True)
        ddt_ref[...] += jnp.sum(jnp.where(isg, da_raw, 0.0), axis=0, keepdims=True)

    return pl.pallas_call(
        body, grid=(t // TR,),
        in_specs=[_row(3 * D, 0), _halo_prev(3 * D, 0), pl.BlockSpec((4, 3 * D), lambda i: (0, 0)),
                  _row(128, OFF_BA // 128), _vec(128), _vec(128), _row(D), _row(D), _row(D), _row(128),
                  pl.BlockSpec(memory_space=pl.ANY)],
        out_specs=[_row(3 * D), _into(MIXP - OFF_BA, OFF_BA), _vec(128), _vec(128)],
        out_shape=[SDS((t, 3 * D), F32), SDS(dproj.shape, dproj.dtype), SDS((1, 128), F32), SDS((1, 128), F32)],
        input_output_aliases={10: 1},
        compiler_params=_params(("arbitrary",)), name=name,
    )(proj, proj, conv_w, proj, al_row, dt_row, dq, dk, dv, dbg, dproj)


def _dn_pre_bwd_conv(proj, conv_w, dconv, dproj, name):
    t = proj.shape[0]
    nt = t // TR
    ext_rows = TR + HALO

    def body(x_ref, h_ref, cw_ref, dc_ref, dn_ref, _, dx_ref, dcw_ref):
        i = pl.program_id(0)

        @pl.when(i == 0)
        def _():
            dcw_ref[...] = jnp.zeros_like(dcw_ref)

        keep_prev = i > 0
        keep_next = i < nt - 1
        for grp in range(24):
            cols = slice(grp * HD, (grp + 1) * HD)
            dct = dc_ref[:, cols]
            dext = jnp.concatenate([dct, jnp.where(keep_next, dn_ref[:, cols], 0.0)], axis=0)
            acc = cw_ref[3:4, cols] * dext
            for j in range(3):
                acc = acc + cw_ref[j:j + 1, cols] * pltpu.roll(dext, ext_rows - (3 - j), 0)
            dx_ref[:, cols] = acc[:TR].astype(dx_ref.dtype)
            xext = jnp.concatenate([jnp.where(keep_prev, h_ref[:, cols], 0.0), x_ref[:, cols]], axis=0)
            for j in range(4):
                xs = xext if j == 3 else pltpu.roll(xext, 3 - j, 0)
                dcw_ref[j:j + 1, cols] += jnp.sum(xs[HALO:] * dct, axis=0, keepdims=True)

    return pl.pallas_call(
        body, grid=(nt,),
        in_specs=[_row(3 * D, 0), _halo_prev(3 * D, 0), pl.BlockSpec((4, 3 * D), lambda i: (0, 0)),
                  _row(3 * D), _halo_next(3 * D, 0, nt), pl.BlockSpec(memory_space=pl.ANY)],
        out_specs=[_into(3 * D, OFF_Q), pl.BlockSpec((4, 3 * D), lambda i: (0, 0))],
        out_shape=[SDS(dproj.shape, dproj.dtype), SDS((4, 3 * D), F32)],
        input_output_aliases={5: 0},
        compiler_params=_params(("arbitrary",)), name=name,
    )(proj, proj, conv_w, dconv, dconv, dproj)


def _dn_post_fwd(o, proj, gn, name):
    t = o.shape[0]

    def body(o_ref, z_ref, g_ref, out_ref):
        gv = g_ref[...]
        for h in range(NH):
            hc = slice(h * HD, (h + 1) * HD)
            ov = o_ref[:, hc]
            r = lax.rsqrt(jnp.mean(ov * ov, axis=-1, keepdims=True) + RMS_EPS)
            out_ref[:, hc] = (((ov * r) * gv) * _silu(z_ref[:, hc])).astype(out_ref.dtype)

    return pl.pallas_call(
        body, grid=(t // TR,), in_specs=[_row(D), _row(D, OFF_Z // D), _vec(HD)], out_specs=_row(D),
        out_shape=SDS((t, D), BF16), compiler_params=_params(("parallel",)), name=name,
    )(o, proj, gn)


def _dn_post_bwd(o, proj, gn, dob, dproj, name):
    t = o.shape[0]

    def body(o_ref, z_ref, g_ref, d_ref, _, do_ref, dz_ref, dg_ref):
        @pl.when(pl.program_id(0) == 0)
        def _():
            dg_ref[...] = jnp.zeros_like(dg_ref)

        gv = g_ref[...]
        acc = jnp.zeros((1, HD), F32)
        for h in range(NH):
            hc = slice(h * HD, (h + 1) * HD)
            ov = o_ref[:, hc]
            zv = z_ref[:, hc]
            dv = d_ref[:, hc]
            r = lax.rsqrt(jnp.mean(ov * ov, axis=-1, keepdims=True) + RMS_EPS)
            n = ov * r
            dz_ref[:, hc] = (dv * (n * gv) * _dsilu(zv)).astype(dz_ref.dtype)
            dng = dv * _silu(zv)
            acc = acc + jnp.sum(dng * n, axis=0, keepdims=True)
            dn = dng * gv
            do_ref[:, hc] = r * (dn - n * jnp.mean(dn * n, axis=-1, keepdims=True))
        dg_ref[...] += acc

    return pl.pallas_call(
        body, grid=(t // TR,),
        in_specs=[_row(D), _row(D, OFF_Z // D), _vec(HD), _row(D), pl.BlockSpec(memory_space=pl.ANY)],
        out_specs=[_row(D), _into(D, OFF_Z), _vec(HD)],
        out_shape=[SDS((t, D), F32), SDS(dproj.shape, dproj.dtype), SDS((1, HD), F32)],
        input_output_aliases={4: 1},
        compiler_params=_params(("arbitrary",)), name=name,
    )(o, proj, gn, dob, dproj)


def _merge_fwd(ya, yb, proj, name):
    t = ya.shape[0]

    def body(a_ref, b_ref, gp_ref, gd_ref, o_ref):
        o_ref[...] = (_sigmoid(gp_ref[...]) * a_ref[...] + _sigmoid(gd_ref[...]) * b_ref[...]).astype(o_ref.dtype)

    return pl.pallas_call(
        body, grid=(t // TR,), in_specs=[_row(D), _row(D), _row(D, OFF_GP // D), _row(D, OFF_GD // D)],
        out_specs=_row(D), out_shape=SDS((t, D), BF16),
        compiler_params=_params(("parallel",)), name=name,
    )(ya, yb, proj, proj)


def _into(width, offset):
    assert offset % width == 0
    return pl.BlockSpec((TR, width), lambda i: (i, offset // width))


def _merge_bwd(dm, ya, yb, proj, dproj, name):
    t = ya.shape[0]

    def body(d_ref, a_ref, b_ref, gp_ref, gd_ref, _, da_ref, db_ref, dg_ref):
        dv = d_ref[...]
        sp = _sigmoid(gp_ref[...])
        sd = _sigmoid(gd_ref[...])
        da_ref[...] = (dv * sp).astype(da_ref.dtype)
        db_ref[...] = (dv * sd).astype(db_ref.dtype)
        dg_ref[:, :D] = (dv * a_ref[...] * sp * (1.0 - sp)).astype(dg_ref.dtype)
        dg_ref[:, D:] = (dv * b_ref[...] * sd * (1.0 - sd)).astype(dg_ref.dtype)

    return pl.pallas_call(
        body, grid=(t // TR,),
        in_specs=[_row(D), _row(D), _row(D), _row(D, OFF_GP // D), _row(D, OFF_GD // D),
                  pl.BlockSpec(memory_space=pl.ANY)],
        out_specs=[_row(D), _row(D), _into(2 * D, OFF_GP)],
        out_shape=[SDS((t, D), BF16), SDS((t, D), BF16), SDS(dproj.shape, dproj.dtype)],
        input_output_aliases={5: 2},
        compiler_params=_params(("parallel",)), name=name,
    )(dm, ya, yb, proj, proj, dproj)


def _split2(x):
    hi = x.astype(BF16)
    return hi, (x - hi.astype(F32)).astype(BF16)


def _dot3(a, b, dims):
    ah, al = _split2(a)
    bh, bl = _split2(b)
    return _dg(ah, bh, dims) + (_dg(ah, bl, dims) + _dg(al, bh, dims))


def _neumann_inverses(mats):
    ri = _iota((CH, CH), 0)
    ci = _iota((CH, CH), 1)
    eye = jnp.where(ri == ci, 1.0, 0.0).astype(F32)
    xs = [-a for a in mats]
    ps = [eye + x for x in xs]
    for _ in range(5):
        xs = [_dot3(x, x, NN) for x in xs]
        ps = [p + _dot3(p, x, NN) for p, x in zip(ps, xs)]
    return ps


def _solve_with(inv):
    @jax.custom_vjp
    def solve(a, rhs):
        return _dot3(inv, rhs, NN)

    def fwd(a, rhs):
        sol = _dot3(inv, rhs, NN)
        return sol, sol

    def bwd(sol, d):
        drhs = _dot3(inv, d, TN)
        return -_dot3(drhs, sol, NT), drhs

    solve.defvjp(fwd, bwd)
    return solve


@jax.custom_vjp
def _rows_to_lanes(g64):
    ri = _iota((CH, CH), 0)
    ci = _iota((CH, CH), 1)
    diag = jnp.where(ri == ci, g64, 0.0)
    ones = jnp.ones((CH, CH), BF16)
    hi = diag.astype(BF16)
    rem = diag - hi.astype(F32)
    mid = rem.astype(BF16)
    lo = (rem - mid.astype(F32)).astype(BF16)
    return _dg(ones, hi, NN) + (_dg(ones, mid, NN) + _dg(ones, lo, NN))


def _rows_to_lanes_bwd(_, d):
    ri = _iota((CH, CH), 0)
    ci = _iota((CH, CH), 1)
    return (jnp.where(ri == ci, jnp.broadcast_to(jnp.sum(d, axis=0, keepdims=True), (CH, CH)), 0.0),)


_rows_to_lanes.defvjp(lambda g64: (_rows_to_lanes(g64), None), _rows_to_lanes_bwd)


def _chunk_local(solve_all, q, k, v, g128, g64, gl128, b128, b64):
    ri = _iota((CH, CH), 0)
    ci = _iota((CH, CH), 1)
    causal = ri >= ci
    strict = ri > ci
    gj = [_rows_to_lanes(g) for g in g64]
    decay = [jnp.where(causal, jnp.exp(jnp.where(causal, g - t, 0.0)), 0.0) for g, t in zip(g64, gj)]
    kk = [_nt(x, x) for x in k]
    a = [jnp.where(strict, b * m * dc, 0.0) for b, m, dc in zip(b64, kk, decay)]
    eg = [jnp.exp(g) for g in g128]
    rhs = [jnp.concatenate([b * x, (b * e) * y], axis=1) for b, x, e, y in zip(b128, v, eg, k)]
    sol = solve_all(a, rhs)
    qk = [jnp.where(causal, _nt(x, y) * dc, 0.0) for x, y, dc in zip(q, k, decay)]
    return ([s[:, :HD] for s in sol], [s[:, HD:] for s in sol], qk, [x * e for x, e in zip(q, eg)],
            [x * jnp.exp(gl - g) for x, gl, g in zip(k, gl128, g128)], [jnp.exp(gl) for gl in gl128])


def _all_head_gates(bgv):
    return tuple(list(z) for z in zip(*[_head_gates(bgv, h) for h in range(NH)]))


def _head_gates(bgv, h):
    lane = _iota((CH, 128), 1)
    row = _iota((CH, 128), 0)
    bcol = jnp.sum(jnp.where(lane == h, bgv, 0.0), axis=1, keepdims=True)
    gcol = jnp.sum(jnp.where(lane == NH + h, bgv, 0.0), axis=1, keepdims=True)
    g128 = jnp.broadcast_to(gcol, (CH, 128))
    gl128 = jnp.broadcast_to(jnp.sum(jnp.where(row == CH - 1, g128, 0.0), axis=0, keepdims=True), (CH, 128))
    return (g128, jnp.broadcast_to(gcol, (CH, CH)), gl128,
            jnp.broadcast_to(bcol, (CH, 128)), jnp.broadcast_to(bcol, (CH, CH)))


def _chunk_specs():
    row = pl.BlockSpec((CH, D), lambda i: (i, 0))
    small = pl.BlockSpec((CH, 128), lambda i: (i, 0))
    qk = pl.BlockSpec((NH, CH, CH), lambda i: (i, 0, 0))
    eg = pl.BlockSpec((1, NH, 128), lambda i: (i, 0, 0))
    return row, small, qk, eg


def _dn_local_fwd(q, k, v, bg, name):
    t = q.shape[0]
    n = t // CH

    def body(q_ref, k_ref, v_ref, bg_ref, u_ref, w_ref, qk_ref, qd_ref, kd_ref, eg_ref, inv_ref):
        cols = [slice(h * HD, (h + 1) * HD) for h in range(NH)]

        def solve_all(mats, rhs):
            invs = _neumann_inverses(mats)
            for h in range(NH):
                inv_ref[h] = invs[h]
            return [_dot3(m, r, NN) for m, r in zip(invs, rhs)]

        u, w, qk, qd, kd, egl = _chunk_local(
            solve_all, [q_ref[:, c] for c in cols], [k_ref[:, c] for c in cols], [v_ref[:, c] for c in cols],
            *_all_head_gates(bg_ref[...]))
        for h, hc in enumerate(cols):
            u_ref[:, hc] = u[h]
            w_ref[:, hc] = w[h].astype(w_ref.dtype)
            qd_ref[:, hc] = qd[h].astype(qd_ref.dtype)
            kd_ref[:, hc] = kd[h].astype(kd_ref.dtype)
            qk_ref[h] = qk[h].astype(qk_ref.dtype)
            eg_ref[0, h:h + 1, :] = egl[h][0:1, :]

    row, small, qkb, egb = _chunk_specs()
    return pl.pallas_call(
        body, grid=(n,), in_specs=[row, row, row, small], out_specs=[row, row, qkb, row, row, egb, qkb],
        out_shape=[SDS((t, D), F32), SDS((t, D), BF16), SDS((n * NH, CH, CH), BF16), SDS((t, D), BF16),
                   SDS((t, D), BF16), SDS((n, NH, 128), F32), SDS((n * NH, CH, CH), F32)],
        compiler_params=_params(("parallel",)), name=name,
    )(q, k, v, bg)


def _dn_local_bwd(q, k, v, bg, inv, du, dw, dqk, dqd, dkd, deg, name):
    t = q.shape[0]
    n = t // CH

    def body(q_ref, k_ref, v_ref, bg_ref, inv_ref, du_ref, dw_ref, dqk_ref, dqd_ref, dkd_ref, deg_ref,
             dq_ref, dk_ref, dv_ref, dbg_ref):
        bgv = bg_ref[...]
        lane = _iota((CH, 128), 1)
        row = _iota((CH, 128), 0)
        first = jnp.where(row == 0, 1.0, 0.0)
        acc = jnp.zeros((CH, 128), F32)
        cols = [slice(h * HD, (h + 1) * HD) for h in range(NH)]
        solves = [_solve_with(inv_ref[h]) for h in range(NH)]

        def solve_all(mats, rhs):
            return [f(m, r) for f, m, r in zip(solves, mats, rhs)]

        _, vjp = jax.vjp(functools.partial(_chunk_local, solve_all),
                         [q_ref[:, c] for c in cols], [k_ref[:, c] for c in cols], [v_ref[:, c] for c in cols],
                         *_all_head_gates(bgv))
        cts = ([du_ref[:, c].astype(F32) for c in cols], [dw_ref[:, c].astype(F32) for c in cols],
               [dqk_ref[h] for h in range(NH)],
               [dqd_ref[:, c].astype(F32) for c in cols], [dkd_ref[:, c].astype(F32) for c in cols],
               [jnp.broadcast_to(deg_ref[0, h:h + 1, :], (CH, 128)) * first for h in range(NH)])
        dq, dk, dv, dg128, dg64, dgl, db128, db64 = vjp(cts)
        for h, hc in enumerate(cols):
            dq_ref[:, hc] = dq[h]
            dk_ref[:, hc] = dk[h]
            dv_ref[:, hc] = dv[h]
            dg = jnp.sum(dg128[h], axis=1, keepdims=True) + jnp.sum(dg64[h], axis=1, keepdims=True)
            tot = jnp.sum(jnp.sum(dgl[h], axis=0, keepdims=True), axis=1, keepdims=True)
            dg = dg + jnp.where(row[:, 0:1] == CH - 1, tot, 0.0)
            db = jnp.sum(db128[h], axis=1, keepdims=True) + jnp.sum(db64[h], axis=1, keepdims=True)
            acc = acc + jnp.where(lane == h, db, 0.0) + jnp.where(lane == NH + h, dg, 0.0)
        dbg_ref[...] = acc

    row, small, qkb, egb = _chunk_specs()
    return pl.pallas_call(
        body, grid=(n,), in_specs=[row, row, row, small, qkb, row, row, qkb, row, row, egb],
        out_specs=[row, row, row, small],
        out_shape=[SDS((t, D), F32)] * 3 + [SDS((t, 128), F32)],
        compiler_params=_params(("parallel",)), name=name,
    )(q, k, v, bg, inv, du, dw, dqk, dqd, dkd, deg)


def _state_step(s, u, w, qk, qd, kd, egl):
    ws = [_nn(a, b) for a, b in zip(w, s)]
    v_new = [a - b for a, b in zip(u, ws)]
    qs = [_nn(a, b) for a, b in zip(qd, s)]
    intra = [_nn(a, b) for a, b in zip(qk, v_new)]
    upd = [_tn(a, b) for a, b in zip(kd, v_new)]
    return [a * e + b for a, e, b in zip(s, egl, upd)], [a + b for a, b in zip(qs, intra)]


def _dn_scan_fwd(u, w, qk, qd, kd, eg, name):
    t = u.shape[0]
    n = t // CH
    g = SCAN_CHUNKS

    def body(u_ref, w_ref, qk_ref, qd_ref, kd_ref, eg_ref, o_ref, save_ref, s_ref):
        @pl.when(pl.program_id(0) == 0)
        def _():
            s_ref[...] = jnp.zeros_like(s_ref)

        cols = [slice(h * HD, (h + 1) * HD) for h in range(NH)]
        s = [s_ref[h] for h in range(NH)]
        for c in range(g):
            rows = slice(c * CH, (c + 1) * CH)
            for h in range(NH):
                save_ref[c, h] = s[h].astype(save_ref.dtype)
            s, o = _state_step(
                s, [u_ref[rows, hc] for hc in cols], [w_ref[rows, hc].astype(F32) for hc in cols],
                [qk_ref[c * NH + h].astype(F32) for h in range(NH)], [qd_ref[rows, hc].astype(F32) for hc in cols],
                [kd_ref[rows, hc].astype(F32) for hc in cols], [eg_ref[c, h:h + 1, :] for h in range(NH)])
            for h, hc in enumerate(cols):
                o_ref[rows, hc] = o[h]
        for h in range(NH):
            s_ref[h] = s[h]

    row = pl.BlockSpec((g * CH, D), lambda i: (i, 0))
    qkb = pl.BlockSpec((g * NH, CH, CH), lambda i: (i, 0, 0))
    egb = pl.BlockSpec((g, NH, 128), lambda i: (i, 0, 0))
    return pl.pallas_call(
        body, grid=(n // g,), in_specs=[row, row, qkb, row, row, egb],
        out_specs=[row, pl.BlockSpec((g, NH, HD, HD), lambda i: (i, 0, 0, 0))],
        out_shape=[SDS((t, D), F32), SDS((n, NH, HD, HD), BF16)],
        scratch_shapes=[pltpu.VMEM((NH, HD, HD), F32)],
        compiler_params=_params(("arbitrary",)), name=name,
    )(u, w, qk, qd, kd, eg)


def _dn_scan_bwd(u, w, qk, qd, kd, eg, saved, do, name):
    t = u.shape[0]
    n = t // CH
    g = SCAN_CHUNKS
    last = n // g - 1

    def body(u_ref, w_ref, qk_ref, qd_ref, kd_ref, eg_ref, sv_ref, do_ref,
             du_ref, dw_ref, dqk_ref, dqd_ref, dkd_ref, deg_ref, ds_ref):
        @pl.when(pl.program_id(0) == 0)
        def _():
            ds_ref[...] = jnp.zeros_like(ds_ref)

        cols = [slice(h * HD, (h + 1) * HD) for h in range(NH)]
        ds = [ds_ref[h] for h in range(NH)]
        for c in reversed(range(g)):
            rows = slice(c * CH, (c + 1) * CH)
            _, vjp = jax.vjp(
                _state_step, [sv_ref[c, h].astype(F32) for h in range(NH)], [u_ref[rows, hc] for hc in cols],
                [w_ref[rows, hc].astype(F32) for hc in cols], [qk_ref[c * NH + h].astype(F32) for h in range(NH)],
                [qd_ref[rows, hc].astype(F32) for hc in cols], [kd_ref[rows, hc].astype(F32) for hc in cols],
                [eg_ref[c, h:h + 1, :] for h in range(NH)])
            ds, du, dw, dqk, dqd, dkd, deg = vjp((ds, [do_ref[rows, hc] for hc in cols]))
            for h, hc in enumerate(cols):
                du_ref[rows, hc] = du[h].astype(du_ref.dtype)
                dw_ref[rows, hc] = dw[h].astype(dw_ref.dtype)
                dqk_ref[c * NH + h] = dqk[h]
                dqd_ref[rows, hc] = dqd[h].astype(dqd_ref.dtype)
                dkd_ref[rows, hc] = dkd[h].astype(dkd_ref.dtype)
                deg_ref[c, h:h + 1, :] = deg[h]
        for h in range(NH):
            ds_ref[h] = ds[h]

    row = pl.BlockSpec((g * CH, D), lambda i: (last - i, 0))
    qkb = pl.BlockSpec((g * NH, CH, CH), lambda i: (last - i, 0, 0))
    egb = pl.BlockSpec((g, NH, 128), lambda i: (last - i, 0, 0))
    return pl.pallas_call(
        body, grid=(n // g,),
        in_specs=[row, row, qkb, row, row, egb,
                  pl.BlockSpec((g, NH, HD, HD), lambda i: (last - i, 0, 0, 0)), row],
        out_specs=[row, row, qkb, row, row, egb],
        out_shape=[SDS((t, D), BF16), SDS((t, D), BF16), SDS((n * NH, CH, CH), F32), SDS((t, D), BF16),
                   SDS((t, D), BF16), SDS((n, NH, 128), F32)],
        scratch_shapes=[pltpu.VMEM((NH, HD, HD), F32)],
        compiler_params=_params(("arbitrary",)), name=name,
    )(u, w, qk, qd, kd, eg, saved, do)


def _ada_fwd(c_all, ada_w, ada_b, name):
    ncol = ada_w.shape[1]

    def body(c_ref, w_ref, b_ref, o_ref):
        o_ref[...] = _dg(_silu(c_ref[...]), w_ref[...], NN, HI) + b_ref[...]

    return pl.pallas_call(body, out_shape=SDS((NDEV, ncol), F32),
                          compiler_params=pltpu.CompilerParams(vmem_limit_bytes=VMEM_LIMIT), name=name,
                          )(c_all, ada_w, ada_b)


def _ada_bwd(c_all_t, dmod, name):
    ncol = dmod.shape[1]

    def body(c_ref, d_ref, o_ref):
        sc = _silu(c_ref[...])
        acc = sc[:, 0:1] * d_ref[0:1, :]
        for b in range(1, NDEV):
            acc = acc + sc[:, b:b + 1] * d_ref[b:b + 1, :]
        o_ref[...] = acc

    return pl.pallas_call(body, out_shape=SDS((D, ncol), F32),
                          compiler_params=pltpu.CompilerParams(vmem_limit_bytes=VMEM_LIMIT), name=name,
                          )(c_all_t, dmod)


def _sum_devices(parts, out_dtype, name):
    _, r, c = parts.shape
    tr = TR if r % TR == 0 else r

    def body(p_ref, o_ref):
        acc = p_ref[0].astype(F32)
        for i in range(1, NDEV):
            acc = acc + p_ref[i].astype(F32)
        o_ref[...] = acc.astype(o_ref.dtype)

    return pl.pallas_call(
        body, grid=(r // tr,), in_specs=[pl.BlockSpec((NDEV, tr, c), lambda i: (0, i, 0))],
        out_specs=pl.BlockSpec((tr, c), lambda i: (i, 0)), out_shape=SDS((r, c), out_dtype),
        compiler_params=_params(("parallel",)), name=name,
    )(parts)


def _adam_tiles(r, c):
    if r % 8 == 0:
        return _pick(r, (256, 352, 128, 8)), c
    return r, (256 if c % 256 == 0 else c)


def _adam_math(w, gv, m, v):
    m_new = ADAM_B1 * m + (1.0 - ADAM_B1) * gv
    v_new = ADAM_B2 * v + (1.0 - ADAM_B2) * (gv * gv)
    bc1 = 1.0 - ADAM_B1 ** ADAM_STEP
    bc2 = 1.0 - ADAM_B2 ** ADAM_STEP
    return -ADAM_LR * ((m_new / bc1) / (jnp.sqrt(v_new / bc2) + ADAM_EPS) + ADAM_WD * w), m_new, v_new


def _adamw(w, g, m, v, name):
    r, c = w.shape
    tr, tc = _adam_tiles(r, c)

    def body(w_ref, g_ref, m_ref, v_ref, d_ref, nm_ref, nv_ref):
        d_ref[...], nm_ref[...], nv_ref[...] = _adam_math(w_ref[...], g_ref[...], m_ref[...], v_ref[...])

    spec = pl.BlockSpec((tr, tc), lambda i, j: (i, j))
    return pl.pallas_call(
        body, grid=(r // tr, c // tc), in_specs=[spec] * 4, out_specs=[spec] * 3,
        out_shape=[SDS((r, c), F32)] * 3, compiler_params=_params(("parallel", "parallel")), name=name,
    )(w, g, m, v)


def _reduce_adamw(parts, w, m, v, name):
    r, c = w.shape
    tr, tc = _adam_tiles(r, c)

    def body(p_ref, w_ref, m_ref, v_ref, g_ref, d_ref, nm_ref, nv_ref):
        gv = p_ref[0].astype(F32)
        for i in range(1, NDEV):
            gv = gv + p_ref[i].astype(F32)
        g_ref[...] = gv
        d_ref[...], nm_ref[...], nv_ref[...] = _adam_math(w_ref[...], gv, m_ref[...], v_ref[...])

    spec = pl.BlockSpec((tr, tc), lambda i, j: (i, j))
    return pl.pallas_call(
        body, grid=(r // tr, c // tc),
        in_specs=[pl.BlockSpec((NDEV, tr, tc), lambda i, j: (0, i, j))] + [spec] * 3, out_specs=[spec] * 4,
        out_shape=[SDS((r, c), F32)] * 4, compiler_params=_params(("parallel", "parallel")), name=name,
    )(parts, w, m, v)


ANY = pl.BlockSpec(memory_space=pl.ANY)
MESH = pl.DeviceIdType.MESH


def _all_gather(xs, name, after=None):
    n = len(xs)
    extra = [] if after is None else [after]

    def body(*refs):
        x_refs, out_refs = refs[:n], refs[n + len(extra):2 * n + len(extra)]
        send_sems, recv_sems, local_sems = refs[-3:]
        mx, my, mc = lax.axis_index("x"), lax.axis_index("y"), lax.axis_index("c")
        me, sibling = (mx, my, mc), (mx, my, 1 - mc)
        chips = [(1 - mx, my), (mx, 1 - my), (1 - mx, 1 - my)]

        def rows(a, px, py, pc):
            return out_refs[a].at[4 * px + 2 * py + pc]

        def copy(a, k, block, to, src=None):
            return pltpu.make_async_remote_copy(
                src_ref=rows(a, *block) if src is None else src, dst_ref=rows(a, *block),
                send_sem=send_sems.at[a, k], recv_sem=recv_sems.at[a, k], device_id=to, device_id_type=MESH)

        mine = [pltpu.make_async_copy(x_refs[a], rows(a, *me), local_sems.at[a]) for a in range(n)]
        for cp in mine:
            cp.start()
        first = []
        for a in range(n):
            first.append(copy(a, 0, me, sibling, src=x_refs[a]))
            first += [copy(a, 1 + j, me, (*chip, mc), src=x_refs[a]) for j, chip in enumerate(chips)]
        for cp in first:
            cp.start()
        passed = []
        for a in range(n):
            for j, chip in enumerate(chips):
                copy(a, 1 + j, (*chip, mc), me).wait_recv()
                passed.append(copy(a, 4 + j, (*chip, mc), sibling))
                passed[-1].start()
        for a in range(n):
            copy(a, 0, sibling, me).wait_recv()
            for j, chip in enumerate(chips):
                copy(a, 4 + j, (*chip, 1 - mc), me).wait_recv()
        for cp in first + passed:
            cp.wait_send()
        for cp in mine:
            cp.wait()

    return pl.pallas_call(
        body, out_shape=[SDS((NDEV,) + x.shape, x.dtype) for x in xs], in_specs=[ANY] * (n + len(extra)),
        out_specs=[ANY] * n,
        scratch_shapes=[pltpu.SemaphoreType.DMA((n, 7)), pltpu.SemaphoreType.DMA((n, 7)),
                        pltpu.SemaphoreType.DMA((n,))],
        name=name,
    )(*xs, *extra)


HBM = pl.BlockSpec(memory_space=pltpu.HBM)
SEM = pl.BlockSpec(memory_space=pltpu.SEMAPHORE)
EFFECT = pltpu.SideEffectType.DATAFLOW_SIDE_EFFECTING


def _peers():
    mx, my, mc = lax.axis_index("x"), lax.axis_index("y"), lax.axis_index("c")
    out = []
    for k in range(1, NDEV):
        out.append((1 - mx if k & 4 else mx, 1 - my if k & 2 else my, 1 - mc if k & 1 else mc))
    return 4 * mx + 2 * my + mc, out


NEAR = (0, 1, 3, 5)


def _push_start(srcs, sliced, name, after=None, near=()):
    n = len(srcs)
    extra = [] if after is None else [after]
    lands = [lax.empty(s.shape if sliced else (NDEV,) + s.shape, s.dtype) for s in srcs]

    def body(*refs):
        src_refs, land_refs = refs[:n], refs[n:2 * n]
        outs = refs[2 * n + len(extra):]
        send_sems, recv_sems = outs[:n], outs[n:2 * n]
        token = refs[-1]
        me, peers = _peers()
        for a in range(n):
            for k, (px, py, pc) in enumerate(peers):
                if a in near and k not in NEAR:
                    continue
                src = src_refs[a].at[4 * px + 2 * py + pc] if sliced else src_refs[a]
                pltpu.make_async_remote_copy(
                    src_ref=src, dst_ref=land_refs[a].at[me], send_sem=send_sems[a].at[k],
                    recv_sem=recv_sems[a].at[k], device_id=(px, py, pc), device_id_type=MESH).start()
            pltpu.make_async_copy(src_refs[a].at[me] if sliced else src_refs[a], land_refs[a].at[me],
                                  send_sems[a].at[NDEV - 1]).start()
        token[...] = jnp.zeros_like(token)

    outs = pl.pallas_call(
        body, name=name,
        out_shape=([pltpu.SemaphoreType.DMA((NDEV,))] * n + [pltpu.SemaphoreType.DMA((NDEV - 1,))] * n
                   + [pltpu.HBM(s.shape, s.dtype) for s in srcs] + [pltpu.HBM(l.shape, l.dtype) for l in lands]
                   + [SDS((8, 128), F32)]),
        in_specs=[HBM] * (2 * n) + [pl.BlockSpec(memory_space=pl.ANY)] * len(extra),
        out_specs=[SEM] * (2 * n) + [HBM] * (2 * n) + [pl.BlockSpec(memory_space=pltpu.VMEM)],
        input_output_aliases={i: 2 * n + i for i in range(2 * n)},
        compiler_params=pltpu.CompilerParams(has_side_effects=EFFECT),
    )(*[pltpu.with_memory_space_constraint(s, pltpu.HBM) for s in srcs],
      *[pltpu.with_memory_space_constraint(l, pltpu.HBM) for l in lands], *extra)
    sends, recvs = outs[:n], outs[n:2 * n]
    src_thru, land_thru = outs[2 * n:3 * n], outs[3 * n:4 * n]
    return [(sends[a], recvs[a], src_thru[a], land_thru[a]) for a in range(n)], outs[-1]


def _push_wait(started, sliced, after, name, near=()):
    n = len(started)
    afters = list(after) if isinstance(after, (list, tuple)) else [after]

    def body(*refs):
        src_refs, land_refs = refs[:n], refs[n:2 * n]
        send_sems, recv_sems = refs[2 * n:3 * n], refs[3 * n:4 * n]
        me, peers = _peers()
        for a in range(n):
            for k, (px, py, pc) in enumerate(peers):
                if a in near and k not in NEAR:
                    continue
                src = src_refs[a].at[4 * px + 2 * py + pc] if sliced else src_refs[a]
                cp = pltpu.make_async_remote_copy(
                    src_ref=src, dst_ref=land_refs[a].at[me], send_sem=send_sems[a].at[k],
                    recv_sem=recv_sems[a].at[k], device_id=(px, py, pc), device_id_type=MESH)
                cp.wait_send()
                cp.wait_recv()
            pltpu.make_async_copy(src_refs[a].at[me] if sliced else src_refs[a], land_refs[a].at[me],
                                  send_sems[a].at[NDEV - 1]).wait()

    srcs = [s[2] for s in started]
    lands = [s[3] for s in started]
    outs = pl.pallas_call(
        body, name=name,
        out_shape=[pltpu.HBM(s.shape, s.dtype) for s in srcs] + [pltpu.HBM(l.shape, l.dtype) for l in lands],
        in_specs=[HBM] * (2 * n) + [SEM] * (2 * n) + [pl.BlockSpec(memory_space=pl.ANY)] * len(afters),
        out_specs=[HBM] * (2 * n),
        input_output_aliases={i: i for i in range(2 * n)},
        compiler_params=pltpu.CompilerParams(has_side_effects=EFFECT),
    )(*srcs, *lands, *[s[0] for s in started], *[s[1] for s in started], *afters)
    return outs[n:]


def _relay_to_sibling(land, name):
    def body(_, land_ref, send_sems, recv_sems):
        mx, my, mc = lax.axis_index("x"), lax.axis_index("y"), lax.axis_index("c")
        chips = [(1 - mx, my), (mx, 1 - my), (1 - mx, 1 - my)]

        def copy(j, core):
            slot = land_ref.at[4 * chips[j][0] + 2 * chips[j][1] + core]
            return pltpu.make_async_remote_copy(
                src_ref=slot, dst_ref=slot, send_sem=send_sems.at[j], recv_sem=recv_sems.at[j],
                device_id=(mx, my, 1 - mc), device_id_type=MESH)

        mine = [copy(j, mc) for j in range(3)]
        for cp in mine:
            cp.start()
        for j in range(3):
            copy(j, 1 - mc).wait_recv()
        for cp in mine:
            cp.wait_send()

    return pl.pallas_call(
        body, out_shape=SDS(land.shape, land.dtype), in_specs=[ANY], out_specs=ANY, input_output_aliases={0: 0},
        scratch_shapes=[pltpu.SemaphoreType.DMA((3,)), pltpu.SemaphoreType.DMA((3,))], name=name,
    )(land)


def _cols_from_blocks(blocks):
    _, rows, w = blocks.shape
    return blocks.transpose(1, 0, 2).reshape(rows, NDEV * w)


def _cols_to_blocks(full):
    rows, total = full.shape
    return full.reshape(rows, NDEV, total // NDEV).transpose(1, 0, 2)


def _mix_pad(wt):
    xp, q, k, v, z, ba, gp, gd = jnp.split(wt, (512, 1536, 2560, 3584, 4608, 4624, 5648), axis=0)
    pad = jnp.zeros((MIXP - OFF_BA - 16, wt.shape[1]), wt.dtype)
    return jnp.concatenate([q, k, v, z, gp, gd, xp, ba, pad], axis=0)


def _mix_unpad(wt):
    q, k, v, z, gp, gd, xp, ba = (wt[OFF_Q:OFF_K], wt[OFF_K:OFF_V], wt[OFF_V:OFF_Z], wt[OFF_Z:OFF_GP],
                                  wt[OFF_GP:OFF_GD], wt[OFF_GD:OFF_XP], wt[OFF_XP:OFF_BA], wt[OFF_BA:OFF_BA + 16])
    return jnp.concatenate([xp, q, k, v, z, ba, gp, gd], axis=0)


def _lane_row(vec8):
    return jnp.zeros((1, 128), F32).at[0, NH:2 * NH].set(vec8)


def _ffn_fwd(x, h, gate, w_in, w_out, tag, next_norm=None, token=None, start_more=None):
    if isinstance(w_in, tuple):
        w_in, = _push_wait([w_in], False, h, f"{tag}_gather_wait_in")
    w_in = w_in.reshape(2 * FH, D)
    u, a = _swiglu_up(h, w_in, f"{tag}_up", after=token)
    w_out, = _push_wait([w_out], False, a, f"{tag}_gather_wait_out")
    w_out = w_out.reshape(FH, D)
    outs = _matmul_residual(a, w_out, x, gate, 0.5, a_blk=True, norm=next_norm, name=f"{tag}_down",
                            after=None if start_more is None else start_more(h))
    return outs[0], (h, u, a, outs[1]), w_in, w_out, (outs[2] if next_norm else None)


def _ffn_bwd(dx_out, dy, x, g, scale, w_in, w_out, saved, tag, below=None):
    h, u, a, _ = saved
    t = x.shape[0]
    dw_out = _matmul(a, dy, ta=True, a_blk=True, out_dtype=BF16, name=f"{tag}_down_dw")
    sent_out, token = _push_start([dw_out.reshape(NDEV, FH // NDEV, D)], True, f"{tag}_grad_start_out")
    du = _swiglu_down_bwd(dy, w_out, u, f"{tag}_down_dx", after=token).reshape(NDEV, t, FB)
    dw_in = _matmul(du, h, ta=True, a_blk=True, out_dtype=BF16, name=f"{tag}_up_dw")
    sent_in, token = _push_start([dw_in.reshape(NDEV, FB, D)], True, f"{tag}_grad_start_in")
    dh = _matmul(du, w_in, a_blk=True, out_dtype=F32, name=f"{tag}_up_dx", after=token)
    return _norm_mod_bwd(x, g, scale, dh, dx_out, f"{tag}_norm_bwd", below), sent_in + sent_out


def kernel(x, c, ada_w, ada_b, norm_g, ffn1_w_in, ffn1_w_out, ffn2_w_in, ffn2_w_out, mix_w_in, conv_w, a_log, dt_bias, dn_norm_g, pool_w, pool_scale, pool_proj, dn_proj, mix_w_out, final_g, loss_target, m_ada_w, m_ada_b, m_norm_g, m_ffn1_w_in, m_ffn1_w_out, m_ffn2_w_in, m_ffn2_w_out, m_mix_w_in, m_conv_w, m_a_log, m_dt_bias, m_dn_norm_g, m_pool_w, m_pool_scale, m_pool_proj, m_dn_proj, m_mix_w_out, m_final_g, v_ada_w, v_ada_b, v_norm_g, v_ffn1_w_in, v_ffn1_w_out, v_ffn2_w_in, v_ffn2_w_out, v_mix_w_in, v_conv_w, v_a_log, v_dt_bias, v_dn_norm_g, v_pool_w, v_pool_scale, v_pool_proj, v_dn_proj, v_mix_w_out, v_final_g):
    me = 4 * lax.axis_index("x") + 2 * lax.axis_index("y") + lax.axis_index("c")
    x0 = x[0]
    target = loss_target[0]
    t = x0.shape[0]

    big = [ffn1_w_in[0], ffn1_w_out[0], ffn2_w_in[0], ffn2_w_out[0], mix_w_in[0], pool_proj[0], dn_proj[0],
           mix_w_out[0]]
    small = jnp.concatenate([c.reshape(8, 128), conv_w[0].reshape(12, 128), norm_g[0].reshape(3, 128),
                             jnp.zeros((1, 128), F32)], axis=0)
    small_all, = _all_gather([small], "gather_small")
    c_all = small_all[:, 0:8, :].reshape(NDEV, D)
    conv_full = small_all[:, 8:20, :].reshape(NDEV, 4, 384).transpose(1, 0, 2).reshape(4, 3 * D)
    norm_full = small_all[:, 20:23, :].reshape(NDEV, 3, 128).transpose(1, 0, 2).reshape(3, D)

    ncol = ada_w.shape[2]
    ada_b_mine = lax.dynamic_slice(ada_b, (0, me * ncol), (1, ncol))
    mod_cols = _ada_fwd(c_all, ada_w[0], ada_b_mine, "ada_fwd")
    transposed = (0, 2, 4)
    payload = [(w.T if i in transposed else w).astype(BF16) for i, w in enumerate(big)]
    mod_all, w_in1 = _all_gather([mod_cols, payload[0]], "gather_mod_first_weight")
    started, token = _push_start([payload[1], payload[4]], False, "gather_start", after=mod_all, near=(1,))
    started = {1: started[0], 4: started[1]}

    def start_rest(h):
        more, token = _push_start([payload[i] for i in (5, 6, 7, 2, 3)], False, "gather_start_rest", after=h)
        started.update(zip((5, 6, 7, 2, 3), more))
        return token

    mod = lax.dynamic_index_in_dim(mod_all, me, axis=1, keepdims=False).reshape(9, D)
    shift = [mod[3 * s:3 * s + 1] for s in range(3)]
    scale = [mod[3 * s + 1:3 * s + 2] for s in range(3)]
    gate = [mod[3 * s + 2:3 * s + 3] for s in range(3)]
    ng = [norm_full[s:s + 1] for s in range(3)]
    fg = final_g.reshape(1, D)
    al_row = _lane_row(a_log[0])
    dt_row = _lane_row(dt_bias[0])
    gn = dn_norm_g
    pw = pool_w[0]
    ps = pool_scale

    h0 = _norm_mod_fwd(x0, ng[0], shift[0], scale[0], "ffn1_norm", after=token)
    x1, saved1, w_in1, w_out1, h1 = _ffn_fwd(x0, h0, gate[0], w_in1, started[1], "ffn1",
                                             (ng[1], shift[1], scale[1]), token, start_rest)

    seg, = _push_wait([started[4]], False, h1, "mix_gather_wait", near=(0,))
    w_mix = _mix_pad(_relay_to_sibling(seg, "mix_gather_relay").reshape(MIX_RAW, D))
    proj = _matmul(h1, w_mix, tb=True, out_dtype=F32, name="mix_in")
    qh, kh, vh, bg = _dn_pre_fwd(proj, conv_full, al_row, dt_row, "dn_pre")
    seg = _push_wait([started[i] for i in (5, 6, 7)], False, qh, "mix_gather_wait_rest")
    w_pp = _cols_from_blocks(seg[0])
    w_dn = seg[1].reshape(D, D)
    w_mo = seg[2].reshape(D, D)
    ya = _pool_fwd(proj, pw, ps, w_pp, "pool_fwd")
    u, w, qk, qd, kd, eg, inv = _dn_local_fwd(qh, kh, vh, bg, "dn_local")
    o, s_saved = _dn_scan_fwd(u, w, qk, qd, kd, eg, "dn_scan")
    ob = _dn_post_fwd(o, proj, gn, "dn_post")
    yb = _matmul(ob, w_dn, out_dtype=F32, name="dn_out")
    merged = _merge_fwd(ya, yb, proj, "merge")
    x2, mix_y, h2 = _matmul_residual(merged, w_mo, x1, gate[1], 1.0, norm=(ng[2], shift[2], scale[2]),
                                     name="mix_out")

    x3, saved2, w_in2, w_out2, _ = _ffn_fwd(x2, h2, gate[2], started[2], started[3], "ffn2")
    loss_row, dx3, dfg, dy2, dgate2 = _final_loss(x3, fg, target, (saved2[3], gate[2], 0.5), "loss")

    (dx2, dsh2, dsc2, dng2, dmy, dgate1), sent2 = _ffn_bwd(dx3, dy2, x2, ng[2], scale[2], w_in2, w_out2, saved2,
                                                           "ffn2", (mix_y, gate[1], 1.0))

    dmerged = _matmul(dmy, w_mo, tb=True, out_dtype=BF16, name="mix_out_dx")
    dw_mo = _matmul(merged, dmy, ta=True, out_dtype=BF16, name="mix_out_dw")
    dproj = lax.empty((t, MIXP), BF16)
    dya, dyb, dproj = _merge_bwd(dmerged, ya, yb, proj, dproj, "merge_bwd")
    dob = _matmul(dyb, w_dn, tb=True, out_dtype=F32, name="dn_out_dx")
    dw_dn = _matmul(ob, dyb, ta=True, out_dtype=BF16, name="dn_out_dw")
    do, dproj, dgn = _dn_post_bwd(o, proj, gn, dob, dproj, "dn_post_bwd")
    du, dw, dqk, dqd, dkd, deg = _dn_scan_bwd(u, w, qk, qd, kd, eg, s_saved, do, "dn_scan_bwd")
    dqh, dkh, dvh, dbg = _dn_local_bwd(qh, kh, vh, bg, inv, du, dw, dqk, dqd, dkd, deg, "dn_local_bwd")
    dconv, dproj, dal, ddt = _dn_pre_bwd_act(proj, conv_full, al_row, dt_row, dqh, dkh, dvh, dbg, dproj,
                                             "dn_pre_bwd_act")
    dproj, dcw = _dn_pre_bwd_conv(proj, conv_full, dconv, dproj, "dn_pre_bwd_conv")
    dwin, dpl, dpw, dps, dpp = _pool_bwd_local(proj, pw, ps, w_pp, dya, "pool_bwd_local")
    dproj = _pool_bwd_window(dwin, dpl, dproj, "pool_bwd_window")
    dw_mix = _matmul(dproj, h1, ta=True, out_dtype=BF16, name="mix_in_dw")
    sent1, token = _push_start(
        [_mix_unpad(dw_mix).reshape(NDEV, MIX_RAW // NDEV, D), _cols_to_blocks(dpp.astype(BF16)),
         dw_dn.reshape(NDEV, -1, D), dw_mo.reshape(NDEV, -1, D)], True, "mix_grad_start")
    dh1 = _matmul(dproj, w_mix, out_dtype=F32, name="mix_in_dx", after=token)
    dx1, dsh1, dsc1, dng1, dy0, dgate0 = _norm_mod_bwd(x1, ng[1], scale[1], dh1, dx2, "mix_norm_bwd",
                                                       (saved1[3], gate[0], 0.5))

    (dx0, dsh0, dsc0, dng0), sent0 = _ffn_bwd(dx1, dy0, x0, ng[0], scale[0], w_in1, w_out1, saved1, "ffn1")

    dmod = jnp.concatenate([dsh0, dsc0, dgate0, dsh1, dsc1, dgate1, dsh2, dsc2, dgate2], axis=1).reshape(-1)
    flat = jnp.concatenate([
        dmod, dal[0, NH:2 * NH], ddt[0, NH:2 * NH], dgn.reshape(-1), dps.reshape(-1), dfg.reshape(-1),
        dpw.reshape(-1), jnp.concatenate([dng0, dng1, dng2], axis=0).reshape(-1), dcw.reshape(-1),
        loss_row[0, 0:1]])
    nflat = 90 * D
    flat = jnp.concatenate([flat, jnp.zeros((nflat - flat.shape[0],), F32)]).reshape(90, D)
    sent_small, small_token = _push_start([flat], False, "small_grad_start")

    def small_grads(flat_all):
        tot = _sum_devices(flat_all, F32, "sum_small_grads").reshape(-1)
        dmod_all = flat_all.reshape(NDEV, nflat)[:, :9 * D]
        dmod_cols = lax.dynamic_slice(dmod_all, (0, me * ncol), (NDEV, ncol))
        g_ada_w = _ada_bwd(c_all.T, dmod_cols, "ada_bwd")
        p = 0
        pieces = {}
        for nm, size in (("ada_b", 9 * D), ("a_log", NH), ("dt_bias", NH), ("dn_norm_g", HD), ("pool_scale", PW),
                         ("final_g", D), ("pool_w", 4 * PG * PG), ("norm_g", 3 * D), ("conv_w", 12 * D),
                         ("loss", 1)):
            pieces[nm] = tot[p:p + size]
            p += size
        g_norm = lax.dynamic_slice(pieces["norm_g"].reshape(3, D), (0, me * 128), (3, 128))
        g_conv = lax.dynamic_slice(pieces["conv_w"].reshape(4, 3 * D), (0, me * 384), (4, 384))
        return pieces["loss"][0], {
            "ada_w": g_ada_w.reshape(ada_w.shape), "ada_b": pieces["ada_b"].reshape(ada_b.shape),
            "norm_g": g_norm.reshape(norm_g.shape), "conv_w": g_conv.reshape(conv_w.shape),
            "a_log": pieces["a_log"].reshape(a_log.shape), "dt_bias": pieces["dt_bias"].reshape(dt_bias.shape),
            "dn_norm_g": pieces["dn_norm_g"].reshape(dn_norm_g.shape),
            "pool_w": pieces["pool_w"].reshape(pool_w.shape),
            "pool_scale": pieces["pool_scale"].reshape(pool_scale.shape),
            "final_g": pieces["final_g"].reshape(final_g.shape),
        }

    grads = {}
    weights = {"ada_w": ada_w, "ada_b": ada_b, "norm_g": norm_g, "ffn1_w_in": ffn1_w_in, "ffn1_w_out": ffn1_w_out,
               "ffn2_w_in": ffn2_w_in, "ffn2_w_out": ffn2_w_out, "mix_w_in": mix_w_in, "conv_w": conv_w,
               "a_log": a_log, "dt_bias": dt_bias, "dn_norm_g": dn_norm_g, "pool_w": pool_w,
               "pool_scale": pool_scale, "pool_proj": pool_proj, "dn_proj": dn_proj, "mix_w_out": mix_w_out,
               "final_g": final_g}
    m_in = {"ada_w": m_ada_w, "ada_b": m_ada_b, "norm_g": m_norm_g, "ffn1_w_in": m_ffn1_w_in,
            "ffn1_w_out": m_ffn1_w_out, "ffn2_w_in": m_ffn2_w_in, "ffn2_w_out": m_ffn2_w_out,
            "mix_w_in": m_mix_w_in, "conv_w": m_conv_w, "a_log": m_a_log, "dt_bias": m_dt_bias,
            "dn_norm_g": m_dn_norm_g, "pool_w": m_pool_w, "pool_scale": m_pool_scale, "pool_proj": m_pool_proj,
            "dn_proj": m_dn_proj, "mix_w_out": m_mix_w_out, "final_g": m_final_g}
    v_in = {"ada_w": v_ada_w, "ada_b": v_ada_b, "norm_g": v_norm_g, "ffn1_w_in": v_ffn1_w_in,
            "ffn1_w_out": v_ffn1_w_out, "ffn2_w_in": v_ffn2_w_in, "ffn2_w_out": v_ffn2_w_out,
            "mix_w_in": v_mix_w_in, "conv_w": v_conv_w, "a_log": v_a_log, "dt_bias": v_dt_bias,
            "dn_norm_g": v_dn_norm_g, "pool_w": v_pool_w, "pool_scale": v_pool_scale, "pool_proj": v_pool_proj,
            "dn_proj": v_dn_proj, "mix_w_out": v_mix_w_out, "final_g": v_final_g}

    names = list(weights)
    large = ("ada_w", "ffn1_w_in", "ffn1_w_out", "ffn2_w_in", "ffn2_w_out", "mix_w_in", "pool_proj", "dn_proj",
             "mix_w_out")
    delta, new_m, new_v = {}, {}, {}

    flipped = ("ffn1_w_in", "ffn2_w_in", "mix_w_in")

    def views(nm):
        shp = weights[nm].shape
        two_d = (shp[-2], shp[-1])
        if nm in flipped:
            return (lambda a: a.reshape(two_d).T), (lambda a: a.T.reshape(shp))
        return (lambda a: a.reshape(two_d)), (lambda a: a.reshape(shp))

    def reduce_update(sent, group, after, tag):
        done = []
        for nm, r in zip(group, _push_wait(sent, True, after, f"{tag}_grad_wait")):
            view, back = views(nm)
            g_, d_, m_, v_ = _reduce_adamw(r, view(weights[nm]), view(m_in[nm]), view(v_in[nm]), f"adamw_{nm}")
            grads[nm], delta[nm], new_m[nm], new_v[nm] = back(g_), back(d_), back(m_), back(v_)
            done.append(d_)
        return done

    done = reduce_update(sent2, ("ffn2_w_in", "ffn2_w_out"), small_token, "ffn2")
    done += reduce_update(sent1, ("mix_w_in", "pool_proj", "dn_proj", "mix_w_out"), done, "mix")
    flat_all, = _push_wait(sent_small, False, done, "small_grad_wait")
    loss, small = small_grads(flat_all)
    grads.update(small)
    view, back = views("ada_w")
    done, m_, v_ = _adamw(view(ada_w), view(grads["ada_w"]), view(m_ada_w), view(v_ada_w), "adamw_ada_w")
    delta["ada_w"], new_m["ada_w"], new_v["ada_w"] = back(done), back(m_), back(v_)
    reduce_update(sent0, ("ffn1_w_in", "ffn1_w_out"), done, "ffn1")
    rest = [nm for nm in names if nm not in large]
    total = sum(weights[nm].size for nm in rest)
    padded = -(-total // D) * D

    def pack(tree, fill):
        flat_ = jnp.concatenate([tree[nm].reshape(-1) for nm in rest])
        return jnp.concatenate([flat_, jnp.full((padded - total,), fill, F32)]).reshape(-1, D)

    d_, m_, v_ = _adamw(pack(weights, 0.0), pack(grads, 0.0), pack(m_in, 0.0), pack(v_in, 1.0), "adamw_small")
    p = 0
    for nm in rest:
        size = weights[nm].size
        shp = weights[nm].shape
        delta[nm] = d_.reshape(-1)[p:p + size].reshape(shp)
        new_m[nm] = m_.reshape(-1)[p:p + size].reshape(shp)
        new_v[nm] = v_.reshape(-1)[p:p + size].reshape(shp)
        p += size

    grad_x = dx0.reshape(x.shape)
    return (loss, grad_x, *[grads[nm] for nm in names], *[delta[nm] for nm in names],
            *[new_m[nm] for nm in names], *[new_v[nm] for nm in names])
```

```python
import functools

import jax
import jax.numpy as jnp
from jax import lax
from jax.experimental import pallas as pl
from jax.experimental.pallas import tpu as pltpu

F32 = jnp.float32
BF16 = jnp.bfloat16
SDS = jax.ShapeDtypeStruct
HI = lax.Precision.HIGHEST

D = 1024
FH = 2816
FB = 704
NH = 8
HD = 128
CH = 64
SCAN_CHUNKS = 8
LOCAL_CHUNKS = 2
NDEV = 8
PW = 512
PG = 128
RMS_EPS = 1e-6
L2_EPS = 1e-6
TR = 512
HALO = 16
VMEM_LIMIT = 56 * 1024 * 1024
MATMUL_VMEM = 40 * 1024 * 1024

MIXP = 6912
OFF_Q, OFF_K, OFF_V, OFF_Z, OFF_GP, OFF_GD, OFF_XP, OFF_BA = 0, 1024, 2048, 3072, 4096, 5120, 6144, 6656
MIX_RAW = 6672

ADAM_LR = 0.001
ADAM_B1 = 0.9
ADAM_B2 = 0.999
ADAM_EPS = 1e-08
ADAM_WD = 0.01
ADAM_STEP = 10

NN = (((1,), (0,)), ((), ()))
NT = (((1,), (1,)), ((), ()))
TN = (((0,), (0,)), ((), ()))


def _dg(a, b, dims, prec=None):
    return lax.dot_general(a, b, dims, precision=prec, preferred_element_type=F32)


def _make_dots(prec):
    @jax.custom_vjp
    def nn(a, b):
        return _dg(a, b, NN, prec)

    @jax.custom_vjp
    def nt(a, b):
        return _dg(a, b, NT, prec)

    @jax.custom_vjp
    def tn(a, b):
        return _dg(a, b, TN, prec)

    nn.defvjp(lambda a, b: (nn(a, b), (a, b)), lambda r, d: (nt(d, r[1]), tn(r[0], d)))
    nt.defvjp(lambda a, b: (nt(a, b), (a, b)), lambda r, d: (nn(d, r[1]), tn(d, r[0])))
    tn.defvjp(lambda a, b: (tn(a, b), (a, b)), lambda r, d: (nt(r[1], d), nn(r[0], d)))
    return nn, nt, tn


_nn, _nt, _tn = _make_dots(None)


def _params(sem):
    return pltpu.CompilerParams(dimension_semantics=sem, vmem_limit_bytes=VMEM_LIMIT)


def _sigmoid(x):
    return 1.0 / (1.0 + jnp.exp(-x))


def _silu(x):
    return x * _sigmoid(x)


def _dsilu(x):
    s = _sigmoid(x)
    return s * (1.0 + x * (1.0 - s))


def _pick(n, cands):
    for c in cands:
        if n % c == 0:
            return c
    raise ValueError(f"no tile for {n}")


def _iota(shape, dim):
    return lax.broadcasted_iota(jnp.int32, shape, dim)


def _matmul(a, b, *, ta=False, tb=False, a_blk=False, b_blk=False, o_blk=False, tm=None, tn=None, tk=None,
            out_dtype, name, after=None):
    if a_blk:
        nb, r, cb = a.shape
        if ta:
            k_dim, m_dim, tm = r, nb * cb, cb
        else:
            m_dim, k_dim, tk = r, nb * cb, cb
    else:
        k_dim, m_dim = a.shape if ta else a.shape[::-1]
    if b_blk:
        nb, r, cb = b.shape
        if tb:
            n_dim, tk = r, cb
            assert nb * cb == k_dim
        else:
            n_dim, tn = nb * cb, cb
            assert r == k_dim
    else:
        n_dim = b.shape[0] if tb else b.shape[1]
    tn = tn or _pick(n_dim, (1024, 768, 512, 256, 128))
    out_bytes = jnp.dtype(out_dtype).itemsize

    def vmem(tm_, tk_):
        return 4 * tk_ * (tm_ + tn) + tm_ * tn * (4 + 2 * out_bytes)

    k_cands = [tk] if tk else [c for c in (k_dim, 4096, 3456, 2816, 2304, 2048, 1024, 512, 256)
                               if c <= k_dim and k_dim % c == 0]
    m_cands = [tm] if tm else [c for c in (2048, 1024, 768, 512, 256, 128) if m_dim % c == 0]
    base = next((c for c in m_cands if c <= 1024), m_cands[-1])
    tk = next((c for c in k_cands if vmem(base, c) <= MATMUL_VMEM), k_cands[-1])
    tm = next((c for c in m_cands if vmem(c, tk) <= MATMUL_VMEM), m_cands[-1])
    nk = k_dim // tk
    dims = ((((0,) if ta else (1,)), ((1,) if tb else (0,))), ((), ()))

    def body(a_ref, b_ref, *rest):
        o_ref, acc_ref = rest[-2:]
        k = pl.program_id(2)

        @pl.when(k == 0)
        def _():
            acc_ref[...] = jnp.zeros_like(acc_ref)

        acc_ref[...] += lax.dot_general(a_ref[...].astype(BF16), b_ref[...].astype(BF16), dims,
                                        preferred_element_type=F32)

        @pl.when(k == nk - 1)
        def _():
            o_ref[...] = acc_ref[...].astype(o_ref.dtype)

    if a_blk:
        a_spec = (pl.BlockSpec((None, tk, tm), lambda i, j, k: (i, k, 0)) if ta
                  else pl.BlockSpec((None, tm, tk), lambda i, j, k: (k, i, 0)))
    else:
        a_spec = (pl.BlockSpec((tk, tm), lambda i, j, k: (k, i)) if ta
                  else pl.BlockSpec((tm, tk), lambda i, j, k: (i, k)))
    if b_blk:
        b_spec = (pl.BlockSpec((None, tn, tk), lambda i, j, k: (k, j, 0)) if tb
                  else pl.BlockSpec((None, tk, tn), lambda i, j, k: (j, k, 0)))
    else:
        b_spec = (pl.BlockSpec((tn, tk), lambda i, j, k: (j, k)) if tb
                  else pl.BlockSpec((tk, tn), lambda i, j, k: (k, j)))
    if o_blk:
        o_spec = pl.BlockSpec((None, tm, tn), lambda i, j, k: (j, i, 0))
        o_shape = SDS((n_dim // tn, m_dim, tn), out_dtype)
    else:
        o_spec = pl.BlockSpec((tm, tn), lambda i, j, k: (i, j))
        o_shape = SDS((m_dim, n_dim), out_dtype)
    return pl.pallas_call(
        body, grid=(m_dim // tm, n_dim // tn, nk),
        in_specs=[a_spec, b_spec] + ([] if after is None else [pl.BlockSpec(memory_space=pl.ANY)]),
        out_specs=o_spec,
        out_shape=o_shape,
        scratch_shapes=[pltpu.VMEM((tm, tn), F32)],
        compiler_params=_params(("parallel", "parallel", "arbitrary")),
        name=name,
    )(a, b, *([] if after is None else [after]))


def _matmul_residual(a, b, x, gate, coef, *, a_blk=False, norm=None, name, after=None):
    m_dim = a.shape[-2]
    tm = _pick(m_dim, (1024, 512))
    if a_blk:
        nk, _, tk = a.shape
        a_spec = pl.BlockSpec((None, tm, tk), lambda i, k: (k, i, 0))
    else:
        tk = a.shape[1]
        nk = 1
        a_spec = pl.BlockSpec((tm, tk), lambda i, k: (i, 0))
    extra = [] if after is None else [after]
    vecs = [gate] + (list(norm) if norm else [])

    def body(a_ref, b_ref, x_ref, gate_ref, *rest):
        vec_refs = rest[:len(vecs) - 1]
        outs = rest[len(vecs) - 1 + len(extra):]
        acc_ref = outs[-1]
        k = pl.program_id(1)

        @pl.when(k == 0)
        def _():
            acc_ref[...] = jnp.zeros_like(acc_ref)

        acc_ref[...] += _dg(a_ref[...], b_ref[...], NN)

        @pl.when(k == nk - 1)
        def _():
            y = acc_ref[...]
            xn = x_ref[...] + (coef * gate_ref[...]) * y
            outs[0][...] = xn
            outs[1][...] = y.astype(outs[1].dtype)
            if norm:
                g_ref, sh_ref, sc_ref = vec_refs
                r = lax.rsqrt(jnp.mean(xn * xn, axis=-1, keepdims=True) + RMS_EPS)
                outs[2][...] = (((xn * r) * g_ref[...]) * (1.0 + sc_ref[...]) + sh_ref[...]).astype(outs[2].dtype)

    row = pl.BlockSpec((tm, D), lambda i, k: (i, 0))
    vec = pl.BlockSpec((1, D), lambda i, k: (0, 0))
    return pl.pallas_call(
        body, grid=(m_dim // tm, nk),
        in_specs=[a_spec, pl.BlockSpec((tk, D), lambda i, k: (k, 0)), row] + [vec] * len(vecs)
        + [pl.BlockSpec(memory_space=pl.ANY)] * len(extra),
        out_specs=[row] * (3 if norm else 2),
        out_shape=[SDS((m_dim, D), F32), SDS((m_dim, D), BF16)] + ([SDS((m_dim, D), BF16)] if norm else []),
        scratch_shapes=[pltpu.VMEM((tm, D), F32)],
        compiler_params=_params(("parallel", "arbitrary")), name=name,
    )(a, b, x, *vecs, *extra)


def _row(width, col=0):
    return pl.BlockSpec((TR, width), lambda i: (i, col))


def _vec(width):
    return pl.BlockSpec((1, width), lambda i: (0, 0))


def _norm_mod_fwd(x, g, shift, scale, name, after=None):
    t = x.shape[0]
    extra = [] if after is None else [after]

    def body(x_ref, g_ref, sh_ref, sc_ref, *rest):
        o_ref = rest[-1]
        xv = x_ref[...]
        r = lax.rsqrt(jnp.mean(xv * xv, axis=-1, keepdims=True) + RMS_EPS)
        o_ref[...] = (((xv * r) * g_ref[...]) * (1.0 + sc_ref[...]) + sh_ref[...]).astype(o_ref.dtype)

    return pl.pallas_call(
        body, grid=(t // TR,),
        in_specs=[_row(D), _vec(D), _vec(D), _vec(D)] + [pl.BlockSpec(memory_space=pl.ANY)] * len(extra),
        out_specs=_row(D),
        out_shape=SDS((t, D), BF16), compiler_params=_params(("parallel",)), name=name,
    )(x, g, shift, scale, *extra)


def _residual_branch_bwd(dxv, y_ref, gate_ref, coef, dy_ref, dgate_ref):
    dy_ref[...] = ((coef * gate_ref[...]) * dxv).astype(dy_ref.dtype)
    dgate_ref[...] += jnp.sum((coef * dxv) * y_ref[...], axis=0, keepdims=True)


def _norm_mod_bwd(x, g, scale, dh, dx_in, name, below=None):
    t = x.shape[0]
    lower = [] if below is None else list(below[:2])

    def body(x_ref, g_ref, sc_ref, dh_ref, dxi_ref, *rest):
        dx_ref, dsh_ref, dsc_ref, dg_ref = rest[len(lower):len(lower) + 4]

        @pl.when(pl.program_id(0) == 0)
        def _():
            for ref in rest[len(lower) + 1:]:
                if ref.shape[0] == 1:
                    ref[...] = jnp.zeros_like(ref)

        xv = x_ref[...]
        gv = g_ref[...]
        dh = dh_ref[...]
        r = lax.rsqrt(jnp.mean(xv * xv, axis=-1, keepdims=True) + RMS_EPS)
        n = xv * r
        dsh_ref[...] += jnp.sum(dh, axis=0, keepdims=True)
        dsc_ref[...] += jnp.sum(dh * (n * gv), axis=0, keepdims=True)
        tt = dh * (1.0 + sc_ref[...])
        dg_ref[...] += jnp.sum(tt * n, axis=0, keepdims=True)
        dn = tt * gv
        dxv = dxi_ref[...] + r * (dn - n * jnp.mean(dn * n, axis=-1, keepdims=True))
        dx_ref[...] = dxv
        if below is not None:
            _residual_branch_bwd(dxv, rest[0], rest[1], below[2], rest[-2], rest[-1])

    more_in = [] if below is None else [_row(D), _vec(D)]
    more_out = [] if below is None else [_row(D), _vec(D)]
    more_shape = [] if below is None else [SDS((t, D), BF16), SDS((1, D), F32)]
    return pl.pallas_call(
        body, grid=(t // TR,), in_specs=[_row(D), _vec(D), _vec(D), _row(D), _row(D)] + more_in,
        out_specs=[_row(D), _vec(D), _vec(D), _vec(D)] + more_out,
        out_shape=[SDS((t, D), F32), SDS((1, D), F32), SDS((1, D), F32), SDS((1, D), F32)] + more_shape,
        compiler_params=_params(("arbitrary",)), name=name,
    )(x, g, scale, dh, dx_in, *lower)


def _swiglu_up(h, w_in, name, after=None):
    t = h.shape[0]
    tm = _pick(t, (1024, 512, 256))
    half = NDEV // 2
    extra = [] if after is None else [after]

    def body(h_ref, wg_ref, wu_ref, *rest):
        u_ref, a_ref = rest[-2:]
        hv = h_ref[...]
        gate = _dg(hv, wg_ref[...], NT)
        up = _dg(hv, wu_ref[...], NT)
        u_ref[0] = gate.astype(u_ref.dtype)
        u_ref[1] = up.astype(u_ref.dtype)
        a_ref[...] = (_silu(gate) * up).astype(a_ref.dtype)

    return pl.pallas_call(
        body, grid=(t // tm, half),
        in_specs=[pl.BlockSpec((tm, D), lambda i, j: (i, 0)),
                  pl.BlockSpec((FB, D), lambda i, j: (j, 0)),
                  pl.BlockSpec((FB, D), lambda i, j: (j + half, 0))]
        + [pl.BlockSpec(memory_space=pl.ANY)] * len(extra),
        out_specs=[pl.BlockSpec((2, None, tm, FB), lambda i, j: (0, j, i, 0)),
                   pl.BlockSpec((None, tm, FB), lambda i, j: (j, i, 0))],
        out_shape=[SDS((2, half, t, FB), BF16), SDS((half, t, FB), BF16)],
        compiler_params=_params(("parallel", "parallel")), name=name,
    )(h, w_in, w_in, *extra)


def _swiglu_down_bwd(dy, w_out, u, name, after=None):
    t = dy.shape[0]
    tm = _pick(t, (1024, 512, 256))
    half = NDEV // 2
    extra = [] if after is None else [after]
    pair = pl.BlockSpec((2, None, tm, FB), lambda i, j: (0, j, i, 0))

    def body(dy_ref, w_ref, u_ref, *rest):
        o_ref = rest[-1]
        da = _dg(dy_ref[...], w_ref[...], NT)
        gate = u_ref[0].astype(F32)
        o_ref[0] = (da * u_ref[1].astype(F32) * _dsilu(gate)).astype(o_ref.dtype)
        o_ref[1] = (da * _silu(gate)).astype(o_ref.dtype)

    return pl.pallas_call(
        body, grid=(t // tm, half),
        in_specs=[pl.BlockSpec((tm, D), lambda i, j: (i, 0)), pl.BlockSpec((FB, D), lambda i, j: (j, 0)), pair]
        + [pl.BlockSpec(memory_space=pl.ANY)] * len(extra),
        out_specs=pair, out_shape=SDS((2, half, t, FB), BF16),
        compiler_params=_params(("parallel", "parallel")), name=name,
    )(dy, w_out, u, *extra)


def _final_loss(x, fg, target, below, name):
    t = x.shape[0]
    nt = t // TR

    def body(x_ref, g_ref, t_ref, y_ref, gate_ref, loss_ref, dx_ref, dg_ref, dy_ref, dgate_ref, acc_ref):
        i = pl.program_id(0)

        @pl.when(i == 0)
        def _():
            acc_ref[...] = jnp.zeros_like(acc_ref)
            dg_ref[...] = jnp.zeros_like(dg_ref)
            dgate_ref[...] = jnp.zeros_like(dgate_ref)

        xv = x_ref[...]
        gv = g_ref[...]
        r = lax.rsqrt(jnp.mean(xv * xv, axis=-1, keepdims=True) + RMS_EPS)
        n = xv * r
        err = n * gv - t_ref[...]
        acc_ref[...] += jnp.sum(err * err, axis=0, keepdims=True)
        dy = err * (1.0 / D)
        dg_ref[...] += jnp.sum(dy * n, axis=0, keepdims=True)
        dn = dy * gv
        dxv = r * (dn - n * jnp.mean(dn * n, axis=-1, keepdims=True))
        dx_ref[...] = dxv
        _residual_branch_bwd(dxv, y_ref, gate_ref, below[2], dy_ref, dgate_ref)

        @pl.when(i == nt - 1)
        def _():
            tot = jnp.sum(acc_ref[...], axis=1, keepdims=True) * (0.5 / D)
            loss_ref[...] = jnp.broadcast_to(tot, loss_ref.shape)

    return pl.pallas_call(
        body, grid=(nt,), in_specs=[_row(D), _vec(D), _row(D), _row(D), _vec(D)],
        out_specs=[_vec(128), _row(D), _vec(D), _row(D), _vec(D)],
        out_shape=[SDS((1, 128), F32), SDS((t, D), F32), SDS((1, D), F32), SDS((t, D), BF16), SDS((1, D), F32)],
        scratch_shapes=[pltpu.VMEM((1, D), F32)],
        compiler_params=_params(("arbitrary",)), name=name,
    )(x, fg, target, below[0], below[1])


def _halo_prev(width, col):
    per = TR // HALO
    return pl.BlockSpec((HALO, width), lambda i: (jnp.maximum(i * per - 1, 0), col))


def _halo_next(width, col, nt):
    per = TR // HALO
    return pl.BlockSpec((HALO, width), lambda i: (jnp.minimum((i + 1) * per, nt * per - 1), col))


def _pool_windows(ext, tile_index):
    rows = _iota((TR, PG), 0) + tile_index * TR + 1
    pooled, counts = [], []
    for gi in range(4):
        w = 2 << gi
        e = ext[:, gi * PG:(gi + 1) * PG]
        s = e
        step = 1
        while step < w:
            s = s + pltpu.roll(s, step, 0)
            step *= 2
        cnt = jnp.minimum(rows, w).astype(F32)
        pooled.append(s[HALO:] / cnt - e[HALO:])
        counts.append(cnt)
    return pooled, counts


def _pool_fwd(proj, pool_w, pool_scale, pool_proj, name):
    t = proj.shape[0]
    xcol = OFF_XP // PW

    def body(x_ref, h_ref, pw_ref, ps_ref, pp_ref, o_ref):
        i = pl.program_id(0)
        halo = jnp.where(i > 0, h_ref[...], 0.0)
        ext = jnp.concatenate([halo, x_ref[...]], axis=0)
        pooled, _ = _pool_windows(ext, i)
        mixed = [_dg(pooled[g].astype(BF16), pw_ref[g].astype(BF16), NN) for g in range(4)]
        ypre = jnp.concatenate(mixed, axis=1) * ps_ref[...]
        o_ref[...] = _dg(ypre.astype(BF16), pp_ref[...], NN)

    return pl.pallas_call(
        body, grid=(t // TR,),
        in_specs=[_row(PW, xcol), _halo_prev(PW, xcol),
                  pl.BlockSpec((4, PG, PG), lambda i: (0, 0, 0)), _vec(PW),
                  pl.BlockSpec((PW, D), lambda i: (0, 0))],
        out_specs=_row(D), out_shape=SDS((t, D), F32),
        compiler_params=_params(("parallel",)), name=name,
    )(proj, proj, pool_w, pool_scale, pool_proj)


def _pool_bwd_local(proj, pool_w, pool_scale, pool_proj, dya, name):
    t = proj.shape[0]
    xcol = OFF_XP // PW

    def body(x_ref, h_ref, pw_ref, ps_ref, pp_ref, dya_ref, dwin_ref, dpl_ref, dpw_ref, dps_ref, dpp_ref):
        i = pl.program_id(0)

        @pl.when(i == 0)
        def _():
            dpw_ref[...] = jnp.zeros_like(dpw_ref)
            dps_ref[...] = jnp.zeros_like(dps_ref)
            dpp_ref[...] = jnp.zeros_like(dpp_ref)

        halo = jnp.where(i > 0, h_ref[...], 0.0)
        ext = jnp.concatenate([halo, x_ref[...]], axis=0)
        pooled, counts = _pool_windows(ext, i)
        mixed = jnp.concatenate(
            [_dg(pooled[g].astype(BF16), pw_ref[g].astype(BF16), NN) for g in range(4)], axis=1)
        ps = ps_ref[...]
        ypre = mixed * ps
        dyab = dya_ref[...].astype(BF16)
        dypre = _dg(dyab, pp_ref[...], NT)
        dpp_ref[...] += _dg(ypre.astype(BF16), dyab, TN)
        dps_ref[...] += jnp.sum(dypre * mixed, axis=0, keepdims=True)
        dmixed = dypre * ps
        for g in range(4):
            dm = dmixed[:, g * PG:(g + 1) * PG].astype(BF16)
            dpw_ref[g] += _dg(pooled[g].astype(BF16), dm, TN)
            dpooled = _dg(dm, pw_ref[g].astype(BF16), NT)
            dwin_ref[:, g * PG:(g + 1) * PG] = dpooled / counts[g]
            dpl_ref[:, g * PG:(g + 1) * PG] = dpooled

    return pl.pallas_call(
        body, grid=(t // TR,),
        in_specs=[_row(PW, xcol), _halo_prev(PW, xcol),
                  pl.BlockSpec((4, PG, PG), lambda i: (0, 0, 0)), _vec(PW),
                  pl.BlockSpec((PW, D), lambda i: (0, 0)), _row(D)],
        out_specs=[_row(PW), _row(PW), pl.BlockSpec((4, PG, PG), lambda i: (0, 0, 0)), _vec(PW),
                   pl.BlockSpec((PW, D), lambda i: (0, 0))],
        out_shape=[SDS((t, PW), F32), SDS((t, PW), F32), SDS((4, PG, PG), F32), SDS((1, PW), F32),
                   SDS((PW, D), F32)],
        compiler_params=_params(("arbitrary",)), name=name,
    )(proj, proj, pool_w, pool_scale, pool_proj, dya)


def _pool_bwd_window(dwin, dpl, dproj, name):
    t = dwin.shape[0]
    nt = t // TR
    ext_rows = TR + HALO

    def body(dw_ref, h_ref, dp_ref, _, o_ref):
        i = pl.program_id(0)
        halo = jnp.where(i < nt - 1, h_ref[...], 0.0)
        ext = jnp.concatenate([dw_ref[...], halo], axis=0)
        for gi in range(4):
            w = 2 << gi
            s = ext[:, gi * PG:(gi + 1) * PG]
            step = 1
            while step < w:
                s = s + pltpu.roll(s, ext_rows - step, 0)
                step *= 2
            o_ref[:, gi * PG:(gi + 1) * PG] = (s[:TR] - dp_ref[:, gi * PG:(gi + 1) * PG]).astype(o_ref.dtype)

    return pl.pallas_call(
        body, grid=(nt,),
        in_specs=[_row(PW), _halo_next(PW, 0, nt), _row(PW), pl.BlockSpec(memory_space=pl.ANY)],
        out_specs=_into(PW, OFF_XP), out_shape=SDS(dproj.shape, dproj.dtype), input_output_aliases={3: 0},
        compiler_params=_params(("parallel",)), name=name,
    )(dwin, dwin, dpl, dproj)


def _conv_group(ext, cw_ref, cols):
    acc = cw_ref[3:4, cols] * ext
    for j in range(3):
        acc = acc + cw_ref[j:j + 1, cols] * pltpu.roll(ext, 3 - j, 0)
    return acc[HALO:]


def _gate_terms(raw, al, dt):
    beta = _sigmoid(raw)
    xg = raw + dt
    sp = jnp.maximum(xg, 0.0) + jnp.log(1.0 + jnp.exp(-jnp.abs(xg)))
    g = -jnp.exp(al) * sp
    return beta, g, _sigmoid(xg)


def _dn_pre_fwd(proj, conv_w, al_row, dt_row, name):
    t = proj.shape[0]

    def body(x_ref, h_ref, cw_ref, ba_ref, al_ref, dt_ref, q_ref, k_ref, v_ref, bg_ref):
        i = pl.program_id(0)
        keep = i > 0
        for grp in range(24):
            cols = slice(grp * HD, (grp + 1) * HD)
            ext = jnp.concatenate([jnp.where(keep, h_ref[:, cols], 0.0), x_ref[:, cols]], axis=0)
            s = _silu(_conv_group(ext, cw_ref, cols))
            seg, head = divmod(grp, NH)
            hc = slice(head * HD, (head + 1) * HD)
            if seg == 0:
                q_ref[:, hc] = s * lax.rsqrt(jnp.sum(s * s, axis=-1, keepdims=True) + L2_EPS) * (HD ** -0.5)
            elif seg == 1:
                k_ref[:, hc] = s * lax.rsqrt(jnp.sum(s * s, axis=-1, keepdims=True) + L2_EPS)
            else:
                v_ref[:, hc] = s
        lane = _iota((TR, 128), 1)
        rowc = _iota((TR, 128), 0) % CH
        beta, g, _ = _gate_terms(ba_ref[...], al_ref[...], dt_ref[...])
        step = 1
        while step < CH:
            g = g + jnp.where(rowc >= step, pltpu.roll(g, step, 0), 0.0)
            step *= 2
        bg_ref[...] = jnp.where(lane < NH, beta, jnp.where(lane < 2 * NH, g, 0.0))

    return pl.pallas_call(
        body, grid=(t // TR,),
        in_specs=[_row(3 * D, 0), _halo_prev(3 * D, 0), pl.BlockSpec((4, 3 * D), lambda i: (0, 0)),
                  _row(128, OFF_BA // 128), _vec(128), _vec(128)],
        out_specs=[_row(D), _row(D), _row(D), _row(128)],
        out_shape=[SDS((t, D), F32), SDS((t, D), F32), SDS((t, D), F32), SDS((t, 128), F32)],
        compiler_params=_params(("parallel",)), name=name,
    )(proj, proj, conv_w, proj, al_row, dt_row)


def _dn_pre_bwd_act(proj, conv_w, al_row, dt_row, dq, dk, dv, dbg, dproj, name):
    t = proj.shape[0]

    def body(x_ref, h_ref, cw_ref, ba_ref, al_ref, dt_ref, dq_ref, dk_ref, dv_ref, dbg_ref, _,
             dc_ref, draw_ref, dal_ref, ddt_ref):
        i = pl.program_id(0)

        @pl.when(i == 0)
        def _():
            dal_ref[...] = jnp.zeros_like(dal_ref)
            ddt_ref[...] = jnp.zeros_like(ddt_ref)

        keep = i > 0
        for grp in range(24):
            cols = slice(grp * HD, (grp + 1) * HD)
            ext = jnp.concatenate([jnp.where(keep, h_ref[:, cols], 0.0), x_ref[:, cols]], axis=0)
            cv = _conv_group(ext, cw_ref, cols)
            seg, head = divmod(grp, NH)
            hc = slice(head * HD, (head + 1) * HD)
            if seg == 2:
                ds = dv_ref[:, hc]
            else:
                s = _silu(cv)
                r = lax.rsqrt(jnp.sum(s * s, axis=-1, keepdims=True) + L2_EPS)
                dy = dq_ref[:, hc] if seg == 0 else dk_ref[:, hc]
                c = (HD ** -0.5) if seg == 0 else 1.0
                ds = (c * r) * (dy - s * ((r * r) * jnp.sum(dy * s, axis=-1, keepdims=True)))
            dc_ref[:, cols] = ds * _dsilu(cv)
        lane = _iota((TR, 128), 1)
        rowc = _iota((TR, 128), 0) % CH
        isb = lane < NH
        isg = jnp.logical_and(lane >= NH, lane < 2 * NH)
        beta, g, sg = _gate_terms(ba_ref[...], al_ref[...], dt_ref[...])
        dbgv = dbg_ref[...]
        dg = dbgv
        step = 1
        while step < CH:
            dg = dg + jnp.where(rowc < CH - step, pltpu.roll(dg, TR - step, 0), 0.0)
            step *= 2
        da_raw = dg * (-jnp.exp(al_ref[...])) * sg
        draw = jnp.where(isb, dbgv * beta * (1.0 - beta), jnp.where(isg, da_raw, 0.0))
        draw_ref[:, :128] = draw.astype(draw_ref.dtype)
        draw_ref[:, 128:] = jnp.zeros((TR, MIXP - OFF_BA - 128), draw_ref.dtype)
        dal_ref[...] += jnp.sum(jnp.where(isg, dg * g, 0.0), axis=0, keepdims=True)
        ddt_ref[...] += jnp.sum(jnp.where(isg, da_raw, 0.0), axis=0, keepdims=True)

    return pl.pallas_call(
        body, grid=(t // TR,),
        in_specs=[_row(3 * D, 0), _halo_prev(3 * D, 0), pl.BlockSpec((4, 3 * D), lambda i: (0, 0)),
                  _row(128, OFF_BA // 128), _vec(128), _vec(128), _row(D), _row(D), _row(D), _row(128),
                  pl.BlockSpec(memory_space=pl.ANY)],
        out_specs=[_row(3 * D), _into(MIXP - OFF_BA, OFF_BA), _vec(128), _vec(128)],
        out_shape=[SDS((t, 3 * D), F32), SDS(dproj.shape, dproj.dtype), SDS((1, 128), F32), SDS((1, 128), F32)],
        input_output_aliases={10: 1},
        compiler_params=_params(("arbitrary",)), name=name,
    )(proj, proj, conv_w, proj, al_row, dt_row, dq, dk, dv, dbg, dproj)


def _dn_pre_bwd_conv(proj, conv_w, dconv, dproj, name):
    t = proj.shape[0]
    nt = t // TR
    ext_rows = TR + HALO

    def body(x_ref, h_ref, cw_ref, dc_ref, dn_ref, _, dx_ref, dcw_ref):
        i = pl.program_id(0)

        @pl.when(i == 0)
        def _():
            dcw_ref[...] = jnp.zeros_like(dcw_ref)

        keep_prev = i > 0
        keep_next = i < nt - 1
        for grp in range(24):
            cols = slice(grp * HD, (grp + 1) * HD)
            dct = dc_ref[:, cols]
            dext = jnp.concatenate([dct, jnp.where(keep_next, dn_ref[:, cols], 0.0)], axis=0)
            acc = cw_ref[3:4, cols] * dext
            for j in range(3):
                acc = acc + cw_ref[j:j + 1, cols] * pltpu.roll(dext, ext_rows - (3 - j), 0)
            dx_ref[:, cols] = acc[:TR].astype(dx_ref.dtype)
            xext = jnp.concatenate([jnp.where(keep_prev, h_ref[:, cols], 0.0), x_ref[:, cols]], axis=0)
            for j in range(4):
                xs = xext if j == 3 else pltpu.roll(xext, 3 - j, 0)
                dcw_ref[j:j + 1, cols] += jnp.sum(xs[HALO:] * dct, axis=0, keepdims=True)

    return pl.pallas_call(
        body, grid=(nt,),
        in_specs=[_row(3 * D, 0), _halo_prev(3 * D, 0), pl.BlockSpec((4, 3 * D), lambda i: (0, 0)),
                  _row(3 * D), _halo_next(3 * D, 0, nt), pl.BlockSpec(memory_space=pl.ANY)],
        out_specs=[_into(3 * D, OFF_Q), pl.BlockSpec((4, 3 * D), lambda i: (0, 0))],
        out_shape=[SDS(dproj.shape, dproj.dtype), SDS((4, 3 * D), F32)],
        input_output_aliases={5: 0},
        compiler_params=_params(("arbitrary",)), name=name,
    )(proj, proj, conv_w, dconv, dconv, dproj)


def _dn_post_fwd(o, proj, gn, name):
    t = o.shape[0]

    def body(o_ref, z_ref, g_ref, out_ref):
        gv = g_ref[...]
        for h in range(NH):
            hc = slice(h * HD, (h + 1) * HD)
            ov = o_ref[:, hc]
            r = lax.rsqrt(jnp.mean(ov * ov, axis=-1, keepdims=True) + RMS_EPS)
            out_ref[:, hc] = (((ov * r) * gv) * _silu(z_ref[:, hc])).astype(out_ref.dtype)

    return pl.pallas_call(
        body, grid=(t // TR,), in_specs=[_row(D), _row(D, OFF_Z // D), _vec(HD)], out_specs=_row(D),
        out_shape=SDS((t, D), BF16), compiler_params=_params(("parallel",)), name=name,
    )(o, proj, gn)


def _dn_post_bwd(o, proj, gn, dob, dproj, name):
    t = o.shape[0]

    def body(o_ref, z_ref, g_ref, d_ref, _, do_ref, dz_ref, dg_ref):
        @pl.when(pl.program_id(0) == 0)
        def _():
            dg_ref[...] = jnp.zeros_like(dg_ref)

        gv = g_ref[...]
        acc = jnp.zeros((1, HD), F32)
        for h in range(NH):
            hc = slice(h * HD, (h + 1) * HD)
            ov = o_ref[:, hc]
            zv = z_ref[:, hc]
            dv = d_ref[:, hc]
            r = lax.rsqrt(jnp.mean(ov * ov, axis=-1, keepdims=True) + RMS_EPS)
            n = ov * r
            dz_ref[:, hc] = (dv * (n * gv) * _dsilu(zv)).astype(dz_ref.dtype)
            dng = dv * _silu(zv)
            acc = acc + jnp.sum(dng * n, axis=0, keepdims=True)
            dn = dng * gv
            do_ref[:, hc] = r * (dn - n * jnp.mean(dn * n, axis=-1, keepdims=True))
        dg_ref[...] += acc

    return pl.pallas_call(
        body, grid=(t // TR,),
        in_specs=[_row(D), _row(D, OFF_Z // D), _vec(HD), _row(D), pl.BlockSpec(memory_space=pl.ANY)],
        out_specs=[_row(D), _into(D, OFF_Z), _vec(HD)],
        out_shape=[SDS((t, D), F32), SDS(dproj.shape, dproj.dtype), SDS((1, HD), F32)],
        input_output_aliases={4: 1},
        compiler_params=_params(("arbitrary",)), name=name,
    )(o, proj, gn, dob, dproj)


def _merge_fwd(ya, yb, proj, name):
    t = ya.shape[0]

    def body(a_ref, b_ref, gp_ref, gd_ref, o_ref):
        o_ref[...] = (_sigmoid(gp_ref[...]) * a_ref[...] + _sigmoid(gd_ref[...]) * b_ref[...]).astype(o_ref.dtype)

    return pl.pallas_call(
        body, grid=(t // TR,), in_specs=[_row(D), _row(D), _row(D, OFF_GP // D), _row(D, OFF_GD // D)],
        out_specs=_row(D), out_shape=SDS((t, D), BF16),
        compiler_params=_params(("parallel",)), name=name,
    )(ya, yb, proj, proj)


def _into(width, offset):
    assert offset % width == 0
    return pl.BlockSpec((TR, width), lambda i: (i, offset // width))


def _merge_bwd(dm, ya, yb, proj, dproj, name):
    t = ya.shape[0]

    def body(d_ref, a_ref, b_ref, gp_ref, gd_ref, _, da_ref, db_ref, dg_ref):
        dv = d_ref[...]
        sp = _sigmoid(gp_ref[...])
        sd = _sigmoid(gd_ref[...])
        da_ref[...] = (dv * sp).astype(da_ref.dtype)
        db_ref[...] = (dv * sd).astype(db_ref.dtype)
        dg_ref[:, :D] = (dv * a_ref[...] * sp * (1.0 - sp)).astype(dg_ref.dtype)
        dg_ref[:, D:] = (dv * b_ref[...] * sd * (1.0 - sd)).astype(dg_ref.dtype)

    return pl.pallas_call(
        body, grid=(t // TR,),
        in_specs=[_row(D), _row(D), _row(D), _row(D, OFF_GP // D), _row(D, OFF_GD // D),
                  pl.BlockSpec(memory_space=pl.ANY)],
        out_specs=[_row(D), _row(D), _into(2 * D, OFF_GP)],
        out_shape=[SDS((t, D), BF16), SDS((t, D), BF16), SDS(dproj.shape, dproj.dtype)],
        input_output_aliases={5: 2},
        compiler_params=_params(("parallel",)), name=name,
    )(dm, ya, yb, proj, proj, dproj)


def _split2(x):
    hi = x.astype(BF16)
    return hi, (x - hi.astype(F32)).astype(BF16)


def _dot3(a, b, dims):
    ah, al = _split2(a)
    bh, bl = _split2(b)
    return _dg(ah, bh, dims) + (_dg(ah, bl, dims) + _dg(al, bh, dims))


def _neumann_inverses(mats):
    ri = _iota((CH, CH), 0)
    ci = _iota((CH, CH), 1)
    eye = jnp.where(ri == ci, 1.0, 0.0).astype(F32)
    xs = [-a for a in mats]
    ps = [eye + x for x in xs]
    for _ in range(5):
        xs = [_dot3(x, x, NN) for x in xs]
        ps = [p + _dot3(p, x, NN) for p, x in zip(ps, xs)]
    return ps


def _solve_with(inv):
    @jax.custom_vjp
    def solve(a, rhs):
        return _dot3(inv, rhs, NN)

    def fwd(a, rhs):
        sol = _dot3(inv, rhs, NN)
        return sol, sol

    def bwd(sol, d):
        drhs = _dot3(inv, d, TN)
        return -_dot3(drhs, sol, NT), drhs

    solve.defvjp(fwd, bwd)
    return solve


@jax.custom_vjp
def _rows_to_lanes(g64):
    ri = _iota((CH, CH), 0)
    ci = _iota((CH, CH), 1)
    diag = jnp.where(ri == ci, g64, 0.0)
    ones = jnp.ones((CH, CH), BF16)
    hi = diag.astype(BF16)
    rem = diag - hi.astype(F32)
    mid = rem.astype(BF16)
    lo = (rem - mid.astype(F32)).astype(BF16)
    return _dg(ones, hi, NN) + (_dg(ones, mid, NN) + _dg(ones, lo, NN))


def _rows_to_lanes_bwd(_, d):
    ri = _iota((CH, CH), 0)
    ci = _iota((CH, CH), 1)
    return (jnp.where(ri == ci, jnp.broadcast_to(jnp.sum(d, axis=0, keepdims=True), (CH, CH)), 0.0),)


_rows_to_lanes.defvjp(lambda g64: (_rows_to_lanes(g64), None), _rows_to_lanes_bwd)


def _chunk_local(solve_all, q, k, v, g128, g64, gl128, b128, b64):
    ri = _iota((CH, CH), 0)
    ci = _iota((CH, CH), 1)
    causal = ri >= ci
    strict = ri > ci
    gj = [_rows_to_lanes(g) for g in g64]
    decay = [jnp.where(causal, jnp.exp(jnp.where(causal, g - t, 0.0)), 0.0) for g, t in zip(g64, gj)]
    kk = [_nt(x, x) for x in k]
    a = [jnp.where(strict, b * m * dc, 0.0) for b, m, dc in zip(b64, kk, decay)]
    eg = [jnp.exp(g) for g in g128]
    rhs = [jnp.concatenate([b * x, (b * e) * y], axis=1) for b, x, e, y in zip(b128, v, eg, k)]
    sol = solve_all(a, rhs)
    qk = [jnp.where(causal, _nt(x, y) * dc, 0.0) for x, y, dc in zip(q, k, decay)]
    return ([s[:, :HD] for s in sol], [s[:, HD:] for s in sol], qk, [x * e for x, e in zip(q, eg)],
            [x * jnp.exp(gl - g) for x, gl, g in zip(k, gl128, g128)], [jnp.exp(gl) for gl in gl128])


def _all_head_gates(bgv):
    return tuple(list(z) for z in zip(*[_head_gates(bgv, h) for h in range(NH)]))


def _head_gates(bgv, h):
    lane = _iota((CH, 128), 1)
    row = _iota((CH, 128), 0)
    bcol = jnp.sum(jnp.where(lane == h, bgv, 0.0), axis=1, keepdims=True)
    gcol = jnp.sum(jnp.where(lane == NH + h, bgv, 0.0), axis=1, keepdims=True)
    g128 = jnp.broadcast_to(gcol, (CH, 128))
    gl128 = jnp.broadcast_to(jnp.sum(jnp.where(row == CH - 1, g128, 0.0), axis=0, keepdims=True), (CH, 128))
    return (g128, jnp.broadcast_to(gcol, (CH, CH)), gl128,
            jnp.broadcast_to(bcol, (CH, 128)), jnp.broadcast_to(bcol, (CH, CH)))


def _chunk_specs():
    g = LOCAL_CHUNKS
    row = pl.BlockSpec((g * CH, D), lambda i: (i, 0))
    small = pl.BlockSpec((g * CH, 128), lambda i: (i, 0))
    qk = pl.BlockSpec((g * NH, CH, CH), lambda i: (i, 0, 0))
    eg = pl.BlockSpec((g, NH, 128), lambda i: (i, 0, 0))
    return row, small, qk, eg


def _chunk_heads():
    return [(slice(c * CH, (c + 1) * CH), slice(h * HD, (h + 1) * HD), c, h)
            for c in range(LOCAL_CHUNKS) for h in range(NH)]


def _all_gates(bg_ref):
    per_chunk = [_all_head_gates(bg_ref[c * CH:(c + 1) * CH, :]) for c in range(LOCAL_CHUNKS)]
    return tuple(sum((list(pc[j]) for pc in per_chunk), []) for j in range(5))


def _dn_local_fwd(q, k, v, bg, name):
    t = q.shape[0]
    n = t // CH
    pairs = _chunk_heads()

    def body(q_ref, k_ref, v_ref, bg_ref, u_ref, w_ref, qk_ref, qd_ref, kd_ref, eg_ref, inv_ref):
        def solve_all(mats, rhs):
            invs = _neumann_inverses(mats)
            for p in range(len(pairs)):
                inv_ref[p] = invs[p]
            return [_dot3(m, r, NN) for m, r in zip(invs, rhs)]

        u, w, qk, qd, kd, egl = _chunk_local(
            solve_all, [q_ref[r, hc] for r, hc, _, _ in pairs], [k_ref[r, hc] for r, hc, _, _ in pairs],
            [v_ref[r, hc] for r, hc, _, _ in pairs], *_all_gates(bg_ref))
        for p, (r, hc, c, h) in enumerate(pairs):
            u_ref[r, hc] = u[p]
            w_ref[r, hc] = w[p].astype(w_ref.dtype)
            qd_ref[r, hc] = qd[p].astype(qd_ref.dtype)
            kd_ref[r, hc] = kd[p].astype(kd_ref.dtype)
            qk_ref[p] = qk[p].astype(qk_ref.dtype)
            eg_ref[c, h:h + 1, :] = egl[p][0:1, :]

    row, small, qkb, egb = _chunk_specs()
    return pl.pallas_call(
        body, grid=(n // LOCAL_CHUNKS,), in_specs=[row, row, row, small],
        out_specs=[row, row, qkb, row, row, egb, qkb],
        out_shape=[SDS((t, D), F32), SDS((t, D), BF16), SDS((n * NH, CH, CH), BF16), SDS((t, D), BF16),
                   SDS((t, D), BF16), SDS((n, NH, 128), F32), SDS((n * NH, CH, CH), F32)],
        compiler_params=_params(("parallel",)), name=name,
    )(q, k, v, bg)


def _dn_local_bwd(q, k, v, bg, inv, du, dw, dqk, dqd, dkd, deg, name):
    t = q.shape[0]
    n = t // CH
    pairs = _chunk_heads()

    def body(q_ref, k_ref, v_ref, bg_ref, inv_ref, du_ref, dw_ref, dqk_ref, dqd_ref, dkd_ref, deg_ref,
             dq_ref, dk_ref, dv_ref, dbg_ref):
        lane = _iota((CH, 128), 1)
        row = _iota((CH, 128), 0)
        first = jnp.where(row == 0, 1.0, 0.0)
        solves = [_solve_with(inv_ref[p]) for p in range(len(pairs))]

        def solve_all(mats, rhs):
            return [f(m, r) for f, m, r in zip(solves, mats, rhs)]

        _, vjp = jax.vjp(functools.partial(_chunk_local, solve_all),
                         [q_ref[r, hc] for r, hc, _, _ in pairs], [k_ref[r, hc] for r, hc, _, _ in pairs],
                         [v_ref[r, hc] for r, hc, _, _ in pairs], *_all_gates(bg_ref))
        cts = ([du_ref[r, hc].astype(F32) for r, hc, _, _ in pairs],
               [dw_ref[r, hc].astype(F32) for r, hc, _, _ in pairs],
               [dqk_ref[p] for p in range(len(pairs))],
               [dqd_ref[r, hc].astype(F32) for r, hc, _, _ in pairs],
               [dkd_ref[r, hc].astype(F32) for r, hc, _, _ in pairs],
               [jnp.broadcast_to(deg_ref[c, h:h + 1, :], (CH, 128)) * first for _, _, c, h in pairs])
        dq, dk, dv, dg128, dg64, dgl, db128, db64 = vjp(cts)
        acc = [jnp.zeros((CH, 128), F32) for _ in range(LOCAL_CHUNKS)]
        for p, (r, hc, c, h) in enumerate(pairs):
            dq_ref[r, hc] = dq[p]
            dk_ref[r, hc] = dk[p]
            dv_ref[r, hc] = dv[p]
            dg = jnp.sum(dg128[p], axis=1, keepdims=True) + jnp.sum(dg64[p], axis=1, keepdims=True)
            tot = jnp.sum(jnp.sum(dgl[p], axis=0, keepdims=True), axis=1, keepdims=True)
            dg = dg + jnp.where(row[:, 0:1] == CH - 1, tot, 0.0)
            db = jnp.sum(db128[p], axis=1, keepdims=True) + jnp.sum(db64[p], axis=1, keepdims=True)
            acc[c] = acc[c] + jnp.where(lane == h, db, 0.0) + jnp.where(lane == NH + h, dg, 0.0)
        for c in range(LOCAL_CHUNKS):
            dbg_ref[c * CH:(c + 1) * CH, :] = acc[c]

    row, small, qkb, egb = _chunk_specs()
    return pl.pallas_call(
        body, grid=(n // LOCAL_CHUNKS,), in_specs=[row, row, row, small, qkb, row, row, qkb, row, row, egb],
        out_specs=[row, row, row, small],
        out_shape=[SDS((t, D), F32)] * 3 + [SDS((t, 128), F32)],
        compiler_params=_params(("parallel",)), name=name,
    )(q, k, v, bg, inv, du, dw, dqk, dqd, dkd, deg)


def _state_step(s, u, w, qk, qd, kd, egl):
    ws = [_nn(a, b) for a, b in zip(w, s)]
    v_new = [a - b for a, b in zip(u, ws)]
    qs = [_nn(a, b) for a, b in zip(qd, s)]
    intra = [_nn(a, b) for a, b in zip(qk, v_new)]
    upd = [_tn(a, b) for a, b in zip(kd, v_new)]
    return [a * e + b for a, e, b in zip(s, egl, upd)], [a + b for a, b in zip(qs, intra)]


def _dn_scan_fwd(u, w, qk, qd, kd, eg, name):
    t = u.shape[0]
    n = t // CH
    g = SCAN_CHUNKS

    def body(u_ref, w_ref, qk_ref, qd_ref, kd_ref, eg_ref, o_ref, save_ref, s_ref):
        @pl.when(pl.program_id(0) == 0)
        def _():
            s_ref[...] = jnp.zeros_like(s_ref)

        cols = [slice(h * HD, (h + 1) * HD) for h in range(NH)]
        s = [s_ref[h] for h in range(NH)]
        for c in range(g):
            rows = slice(c * CH, (c + 1) * CH)
            for h in range(NH):
                save_ref[c, h] = s[h].astype(save_ref.dtype)
            s, o = _state_step(
                s, [u_ref[rows, hc] for hc in cols], [w_ref[rows, hc].astype(F32) for hc in cols],
                [qk_ref[c * NH + h].astype(F32) for h in range(NH)], [qd_ref[rows, hc].astype(F32) for hc in cols],
                [kd_ref[rows, hc].astype(F32) for hc in cols], [eg_ref[c, h:h + 1, :] for h in range(NH)])
            for h, hc in enumerate(cols):
                o_ref[rows, hc] = o[h]
        for h in range(NH):
            s_ref[h] = s[h]

    row = pl.BlockSpec((g * CH, D), lambda i: (i, 0))
    qkb = pl.BlockSpec((g * NH, CH, CH), lambda i: (i, 0, 0))
    egb = pl.BlockSpec((g, NH, 128), lambda i: (i, 0, 0))
    return pl.pallas_call(
        body, grid=(n // g,), in_specs=[row, row, qkb, row, row, egb],
        out_specs=[row, pl.BlockSpec((g, NH, HD, HD), lambda i: (i, 0, 0, 0))],
        out_shape=[SDS((t, D), F32), SDS((n, NH, HD, HD), BF16)],
        scratch_shapes=[pltpu.VMEM((NH, HD, HD), F32)],
        compiler_params=_params(("arbitrary",)), name=name,
    )(u, w, qk, qd, kd, eg)


def _dn_scan_bwd(u, w, qk, qd, kd, eg, saved, do, name):
    t = u.shape[0]
    n = t // CH
    g = SCAN_CHUNKS
    last = n // g - 1

    def body(u_ref, w_ref, qk_ref, qd_ref, kd_ref, eg_ref, sv_ref, do_ref,
             du_ref, dw_ref, dqk_ref, dqd_ref, dkd_ref, deg_ref, ds_ref):
        @pl.when(pl.program_id(0) == 0)
        def _():
            ds_ref[...] = jnp.zeros_like(ds_ref)

        cols = [slice(h * HD, (h + 1) * HD) for h in range(NH)]
        ds = [ds_ref[h] for h in range(NH)]
        for c in reversed(range(g)):
            rows = slice(c * CH, (c + 1) * CH)
            _, vjp = jax.vjp(
                _state_step, [sv_ref[c, h].astype(F32) for h in range(NH)], [u_ref[rows, hc] for hc in cols],
                [w_ref[rows, hc].astype(F32) for hc in cols], [qk_ref[c * NH + h].astype(F32) for h in range(NH)],
                [qd_ref[rows, hc].astype(F32) for hc in cols], [kd_ref[rows, hc].astype(F32) for hc in cols],
                [eg_ref[c, h:h + 1, :] for h in range(NH)])
            ds, du, dw, dqk, dqd, dkd, deg = vjp((ds, [do_ref[rows, hc] for hc in cols]))
            for h, hc in enumerate(cols):
                du_ref[rows, hc] = du[h].astype(du_ref.dtype)
                dw_ref[rows, hc] = dw[h].astype(dw_ref.dtype)
                dqk_ref[c * NH + h] = dqk[h]
                dqd_ref[rows, hc] = dqd[h].astype(dqd_ref.dtype)
                dkd_ref[rows, hc] = dkd[h].astype(dkd_ref.dtype)
                deg_ref[c, h:h + 1, :] = deg[h]
        for h in range(NH):
            ds_ref[h] = ds[h]

    row = pl.BlockSpec((g * CH, D), lambda i: (last - i, 0))
    qkb = pl.BlockSpec((g * NH, CH, CH), lambda i: (last - i, 0, 0))
    egb = pl.BlockSpec((g, NH, 128), lambda i: (last - i, 0, 0))
    return pl.pallas_call(
        body, grid=(n // g,),
        in_specs=[row, row, qkb, row, row, egb,
                  pl.BlockSpec((g, NH, HD, HD), lambda i: (last - i, 0, 0, 0)), row],
        out_specs=[row, row, qkb, row, row, egb],
        out_shape=[SDS((t, D), BF16), SDS((t, D), BF16), SDS((n * NH, CH, CH), F32), SDS((t, D), BF16),
                   SDS((t, D), BF16), SDS((n, NH, 128), F32)],
        scratch_shapes=[pltpu.VMEM((NH, HD, HD), F32)],
        compiler_params=_params(("arbitrary",)), name=name,
    )(u, w, qk, qd, kd, eg, saved, do)


def _ada_fwd(c_all, ada_w, ada_b, name):
    ncol = ada_w.shape[1]

    def body(c_ref, w_ref, b_ref, o_ref):
        o_ref[...] = _dg(_silu(c_ref[...]), w_ref[...], NN, HI) + b_ref[...]

    return pl.pallas_call(body, out_shape=SDS((NDEV, ncol), F32),
                          compiler_params=pltpu.CompilerParams(vmem_limit_bytes=VMEM_LIMIT), name=name,
                          )(c_all, ada_w, ada_b)


def _ada_bwd(c_all_t, dmod, name):
    ncol = dmod.shape[1]

    def body(c_ref, d_ref, o_ref):
        sc = _silu(c_ref[...])
        acc = sc[:, 0:1] * d_ref[0:1, :]
        for b in range(1, NDEV):
            acc = acc + sc[:, b:b + 1] * d_ref[b:b + 1, :]
        o_ref[...] = acc

    return pl.pallas_call(body, out_shape=SDS((D, ncol), F32),
                          compiler_params=pltpu.CompilerParams(vmem_limit_bytes=VMEM_LIMIT), name=name,
                          )(c_all_t, dmod)


def _sum_devices(parts, out_dtype, name):
    _, r, c = parts.shape
    tr = TR if r % TR == 0 else r

    def body(p_ref, o_ref):
        acc = p_ref[0].astype(F32)
        for i in range(1, NDEV):
            acc = acc + p_ref[i].astype(F32)
        o_ref[...] = acc.astype(o_ref.dtype)

    return pl.pallas_call(
        body, grid=(r // tr,), in_specs=[pl.BlockSpec((NDEV, tr, c), lambda i: (0, i, 0))],
        out_specs=pl.BlockSpec((tr, c), lambda i: (i, 0)), out_shape=SDS((r, c), out_dtype),
        compiler_params=_params(("parallel",)), name=name,
    )(parts)


def _adam_tiles(r, c):
    if r % 8 == 0:
        return _pick(r, (256, 352, 128, 8)), c
    return r, (256 if c % 256 == 0 else c)


def _adam_math(w, gv, m, v):
    m_new = ADAM_B1 * m + (1.0 - ADAM_B1) * gv
    v_new = ADAM_B2 * v + (1.0 - ADAM_B2) * (gv * gv)
    bc1 = 1.0 - ADAM_B1 ** ADAM_STEP
    bc2 = 1.0 - ADAM_B2 ** ADAM_STEP
    return -ADAM_LR * ((m_new / bc1) / (jnp.sqrt(v_new / bc2) + ADAM_EPS) + ADAM_WD * w), m_new, v_new


def _adamw(w, g, m, v, name):
    r, c = w.shape
    tr, tc = _adam_tiles(r, c)

    def body(w_ref, g_ref, m_ref, v_ref, d_ref, nm_ref, nv_ref):
        d_ref[...], nm_ref[...], nv_ref[...] = _adam_math(w_ref[...], g_ref[...], m_ref[...], v_ref[...])

    spec = pl.BlockSpec((tr, tc), lambda i, j: (i, j))
    return pl.pallas_call(
        body, grid=(r // tr, c // tc), in_specs=[spec] * 4, out_specs=[spec] * 3,
        out_shape=[SDS((r, c), F32)] * 3, compiler_params=_params(("parallel", "parallel")), name=name,
    )(w, g, m, v)


def _reduce_adamw(parts, w, m, v, name):
    r, c = w.shape
    tr, tc = _adam_tiles(r, c)

    def body(p_ref, w_ref, m_ref, v_ref, g_ref, d_ref, nm_ref, nv_ref):
        gv = p_ref[0].astype(F32)
        for i in range(1, NDEV):
            gv = gv + p_ref[i].astype(F32)
        g_ref[...] = gv
        d_ref[...], nm_ref[...], nv_ref[...] = _adam_math(w_ref[...], gv, m_ref[...], v_ref[...])

    spec = pl.BlockSpec((tr, tc), lambda i, j: (i, j))
    return pl.pallas_call(
        body, grid=(r // tr, c // tc),
        in_specs=[pl.BlockSpec((NDEV, tr, tc), lambda i, j: (0, i, j))] + [spec] * 3, out_specs=[spec] * 4,
        out_shape=[SDS((r, c), F32)] * 4, compiler_params=_params(("parallel", "parallel")), name=name,
    )(parts, w, m, v)


ANY = pl.BlockSpec(memory_space=pl.ANY)
MESH = pl.DeviceIdType.MESH


def _all_gather(xs, name, after=None):
    n = len(xs)
    extra = [] if after is None else [after]

    def body(*refs):
        x_refs, out_refs = refs[:n], refs[n + len(extra):2 * n + len(extra)]
        send_sems, recv_sems, local_sems = refs[-3:]
        mx, my, mc = lax.axis_index("x"), lax.axis_index("y"), lax.axis_index("c")
        me, sibling = (mx, my, mc), (mx, my, 1 - mc)
        chips = [(1 - mx, my), (mx, 1 - my), (1 - mx, 1 - my)]

        def rows(a, px, py, pc):
            return out_refs[a].at[4 * px + 2 * py + pc]

        def copy(a, k, block, to, src=None):
            return pltpu.make_async_remote_copy(
                src_ref=rows(a, *block) if src is None else src, dst_ref=rows(a, *block),
                send_sem=send_sems.at[a, k], recv_sem=recv_sems.at[a, k], device_id=to, device_id_type=MESH)

        mine = [pltpu.make_async_copy(x_refs[a], rows(a, *me), local_sems.at[a]) for a in range(n)]
        for cp in mine:
            cp.start()
        first = []
        for a in range(n):
            first.append(copy(a, 0, me, sibling, src=x_refs[a]))
            first += [copy(a, 1 + j, me, (*chip, mc), src=x_refs[a]) for j, chip in enumerate(chips)]
        for cp in first:
            cp.start()
        passed = []
        for a in range(n):
            for j, chip in enumerate(chips):
                copy(a, 1 + j, (*chip, mc), me).wait_recv()
                passed.append(copy(a, 4 + j, (*chip, mc), sibling))
                passed[-1].start()
        for a in range(n):
            copy(a, 0, sibling, me).wait_recv()
            for j, chip in enumerate(chips):
                copy(a, 4 + j, (*chip, 1 - mc), me).wait_recv()
        for cp in first + passed:
            cp.wait_send()
        for cp in mine:
            cp.wait()

    return pl.pallas_call(
        body, out_shape=[SDS((NDEV,) + x.shape, x.dtype) for x in xs], in_specs=[ANY] * (n + len(extra)),
        out_specs=[ANY] * n,
        scratch_shapes=[pltpu.SemaphoreType.DMA((n, 7)), pltpu.SemaphoreType.DMA((n, 7)),
                        pltpu.SemaphoreType.DMA((n,))],
        name=name,
    )(*xs, *extra)


HBM = pl.BlockSpec(memory_space=pltpu.HBM)
SEM = pl.BlockSpec(memory_space=pltpu.SEMAPHORE)
EFFECT = pltpu.SideEffectType.DATAFLOW_SIDE_EFFECTING


def _peers():
    mx, my, mc = lax.axis_index("x"), lax.axis_index("y"), lax.axis_index("c")
    out = []
    for k in range(1, NDEV):
        out.append((1 - mx if k & 4 else mx, 1 - my if k & 2 else my, 1 - mc if k & 1 else mc))
    return 4 * mx + 2 * my + mc, out


NEAR = (0, 1, 3, 5)


def _push_start(srcs, sliced, name, after=None, near=()):
    n = len(srcs)
    extra = [] if after is None else [after]
    lands = [lax.empty(s.shape if sliced else (NDEV,) + s.shape, s.dtype) for s in srcs]

    def body(*refs):
        src_refs, land_refs = refs[:n], refs[n:2 * n]
        outs = refs[2 * n + len(extra):]
        send_sems, recv_sems = outs[:n], outs[n:2 * n]
        token = refs[-1]
        me, peers = _peers()
        for a in range(n):
            for k, (px, py, pc) in enumerate(peers):
                if a in near and k not in NEAR:
                    continue
                src = src_refs[a].at[4 * px + 2 * py + pc] if sliced else src_refs[a]
                pltpu.make_async_remote_copy(
                    src_ref=src, dst_ref=land_refs[a].at[me], send_sem=send_sems[a].at[k],
                    recv_sem=recv_sems[a].at[k], device_id=(px, py, pc), device_id_type=MESH).start()
            pltpu.make_async_copy(src_refs[a].at[me] if sliced else src_refs[a], land_refs[a].at[me],
                                  send_sems[a].at[NDEV - 1]).start()
        token[...] = jnp.zeros_like(token)

    outs = pl.pallas_call(
        body, name=name,
        out_shape=([pltpu.SemaphoreType.DMA((NDEV,))] * n + [pltpu.SemaphoreType.DMA((NDEV - 1,))] * n
                   + [pltpu.HBM(s.shape, s.dtype) for s in srcs] + [pltpu.HBM(l.shape, l.dtype) for l in lands]
                   + [SDS((8, 128), F32)]),
        in_specs=[HBM] * (2 * n) + [pl.BlockSpec(memory_space=pl.ANY)] * len(extra),
        out_specs=[SEM] * (2 * n) + [HBM] * (2 * n) + [pl.BlockSpec(memory_space=pltpu.VMEM)],
        input_output_aliases={i: 2 * n + i for i in range(2 * n)},
        compiler_params=pltpu.CompilerParams(has_side_effects=EFFECT),
    )(*[pltpu.with_memory_space_constraint(s, pltpu.HBM) for s in srcs],
      *[pltpu.with_memory_space_constraint(l, pltpu.HBM) for l in lands], *extra)
    sends, recvs = outs[:n], outs[n:2 * n]
    src_thru, land_thru = outs[2 * n:3 * n], outs[3 * n:4 * n]
    return [(sends[a], recvs[a], src_thru[a], land_thru[a]) for a in range(n)], outs[-1]


def _push_wait(started, sliced, after, name, near=()):
    n = len(started)
    afters = list(after) if isinstance(after, (list, tuple)) else [after]

    def body(*refs):
        src_refs, land_refs = refs[:n], refs[n:2 * n]
        send_sems, recv_sems = refs[2 * n:3 * n], refs[3 * n:4 * n]
        me, peers = _peers()
        for a in range(n):
            for k, (px, py, pc) in enumerate(peers):
                if a in near and k not in NEAR:
                    continue
                src = src_refs[a].at[4 * px + 2 * py + pc] if sliced else src_refs[a]
                cp = pltpu.make_async_remote_copy(
                    src_ref=src, dst_ref=land_refs[a].at[me], send_sem=send_sems[a].at[k],
                    recv_sem=recv_sems[a].at[k], device_id=(px, py, pc), device_id_type=MESH)
                cp.wait_send()
                cp.wait_recv()
            pltpu.make_async_copy(src_refs[a].at[me] if sliced else src_refs[a], land_refs[a].at[me],
                                  send_sems[a].at[NDEV - 1]).wait()

    srcs = [s[2] for s in started]
    lands = [s[3] for s in started]
    outs = pl.pallas_call(
        body, name=name,
        out_shape=[pltpu.HBM(s.shape, s.dtype) for s in srcs] + [pltpu.HBM(l.shape, l.dtype) for l in lands],
        in_specs=[HBM] * (2 * n) + [SEM] * (2 * n) + [pl.BlockSpec(memory_space=pl.ANY)] * len(afters),
        out_specs=[HBM] * (2 * n),
        input_output_aliases={i: i for i in range(2 * n)},
        compiler_params=pltpu.CompilerParams(has_side_effects=EFFECT),
    )(*srcs, *lands, *[s[0] for s in started], *[s[1] for s in started], *afters)
    return outs[n:]


def _relay_to_sibling(land, name):
    def body(_, land_ref, send_sems, recv_sems):
        mx, my, mc = lax.axis_index("x"), lax.axis_index("y"), lax.axis_index("c")
        chips = [(1 - mx, my), (mx, 1 - my), (1 - mx, 1 - my)]

        def copy(j, core):
            slot = land_ref.at[4 * chips[j][0] + 2 * chips[j][1] + core]
            return pltpu.make_async_remote_copy(
                src_ref=slot, dst_ref=slot, send_sem=send_sems.at[j], recv_sem=recv_sems.at[j],
                device_id=(mx, my, 1 - mc), device_id_type=MESH)

        mine = [copy(j, mc) for j in range(3)]
        for cp in mine:
            cp.start()
        for j in range(3):
            copy(j, 1 - mc).wait_recv()
        for cp in mine:
            cp.wait_send()

    return pl.pallas_call(
        body, out_shape=SDS(land.shape, land.dtype), in_specs=[ANY], out_specs=ANY, input_output_aliases={0: 0},
        scratch_shapes=[pltpu.SemaphoreType.DMA((3,)), pltpu.SemaphoreType.DMA((3,))], name=name,
    )(land)


def _cols_from_blocks(blocks):
    _, rows, w = blocks.shape
    return blocks.transpose(1, 0, 2).reshape(rows, NDEV * w)


def _cols_to_blocks(full):
    rows, total = full.shape
    return full.reshape(rows, NDEV, total // NDEV).transpose(1, 0, 2)


def _mix_pad(wt):
    xp, q, k, v, z, ba, gp, gd = jnp.split(wt, (512, 1536, 2560, 3584, 4608, 4624, 5648), axis=0)
    pad = jnp.zeros((MIXP - OFF_BA - 16, wt.shape[1]), wt.dtype)
    return jnp.concatenate([q, k, v, z, gp, gd, xp, ba, pad], axis=0)


def _mix_unpad(wt, name):
    pieces = [(OFF_XP, 0, PW), (OFF_Q, PW, 4 * D), (OFF_BA, PW + 4 * D, 16), (OFF_GP, PW + 4 * D + 16, 2 * D)]

    def body(src_ref, out_ref, sems):
        copies = [pltpu.make_async_copy(src_ref.at[pl.ds(s, n)], out_ref.at[pl.ds(d, n)], sems.at[j])
                  for j, (s, d, n) in enumerate(pieces)]
        for cp in copies:
            cp.start()
        for cp in copies:
            cp.wait()

    return pl.pallas_call(
        body, out_shape=SDS((MIX_RAW, wt.shape[1]), wt.dtype), in_specs=[ANY], out_specs=ANY,
        scratch_shapes=[pltpu.SemaphoreType.DMA((len(pieces),))], name=name,
    )(wt)


def _lane_row(vec8):
    return jnp.zeros((1, 128), F32).at[0, NH:2 * NH].set(vec8)


def _ffn_fwd(x, h, gate, w_in, w_out, tag, next_norm=None, token=None, start_more=None):
    if isinstance(w_in, tuple):
        w_in, = _push_wait([w_in], False, h, f"{tag}_gather_wait_in")
    w_in = w_in.reshape(2 * FH, D)
    u, a = _swiglu_up(h, w_in, f"{tag}_up", after=token)
    w_out, = _push_wait([w_out], False, a, f"{tag}_gather_wait_out")
    w_out = w_out.reshape(FH, D)
    outs = _matmul_residual(a, w_out, x, gate, 0.5, a_blk=True, norm=next_norm, name=f"{tag}_down",
                            after=None if start_more is None else start_more(h))
    return outs[0], (h, u, a, outs[1]), w_in, w_out, (outs[2] if next_norm else None)


def _ffn_bwd(dx_out, dy, x, g, scale, w_in, w_out, saved, tag, below=None):
    h, u, a, _ = saved
    t = x.shape[0]
    dw_out = _matmul(a, dy, ta=True, a_blk=True, out_dtype=BF16, name=f"{tag}_down_dw")
    sent_out, token = _push_start([dw_out.reshape(NDEV, FH // NDEV, D)], True, f"{tag}_grad_start_out")
    du = _swiglu_down_bwd(dy, w_out, u, f"{tag}_down_dx", after=token).reshape(NDEV, t, FB)
    dw_in = _matmul(du, h, ta=True, a_blk=True, out_dtype=BF16, name=f"{tag}_up_dw")
    sent_in, token = _push_start([dw_in.reshape(NDEV, FB, D)], True, f"{tag}_grad_start_in")
    dh = _matmul(du, w_in, a_blk=True, out_dtype=F32, name=f"{tag}_up_dx", after=token)
    return _norm_mod_bwd(x, g, scale, dh, dx_out, f"{tag}_norm_bwd", below), sent_in + sent_out


def kernel(x, c, ada_w, ada_b, norm_g, ffn1_w_in, ffn1_w_out, ffn2_w_in, ffn2_w_out, mix_w_in, conv_w, a_log, dt_bias, dn_norm_g, pool_w, pool_scale, pool_proj, dn_proj, mix_w_out, final_g, loss_target, m_ada_w, m_ada_b, m_norm_g, m_ffn1_w_in, m_ffn1_w_out, m_ffn2_w_in, m_ffn2_w_out, m_mix_w_in, m_conv_w, m_a_log, m_dt_bias, m_dn_norm_g, m_pool_w, m_pool_scale, m_pool_proj, m_dn_proj, m_mix_w_out, m_final_g, v_ada_w, v_ada_b, v_norm_g, v_ffn1_w_in, v_ffn1_w_out, v_ffn2_w_in, v_ffn2_w_out, v_mix_w_in, v_conv_w, v_a_log, v_dt_bias, v_dn_norm_g, v_pool_w, v_pool_scale, v_pool_proj, v_dn_proj, v_mix_w_out, v_final_g):
    me = 4 * lax.axis_index("x") + 2 * lax.axis_index("y") + lax.axis_index("c")
    x0 = x[0]
    target = loss_target[0]
    t = x0.shape[0]

    big = [ffn1_w_in[0], ffn1_w_out[0], ffn2_w_in[0], ffn2_w_out[0], mix_w_in[0], pool_proj[0], dn_proj[0],
           mix_w_out[0]]
    small = jnp.concatenate([c.reshape(8, 128), conv_w[0].reshape(12, 128), norm_g[0].reshape(3, 128),
                             jnp.zeros((1, 128), F32)], axis=0)
    small_all, = _all_gather([small], "gather_small")
    c_all = small_all[:, 0:8, :].reshape(NDEV, D)
    conv_full = small_all[:, 8:20, :].reshape(NDEV, 4, 384).transpose(1, 0, 2).reshape(4, 3 * D)
    norm_full = small_all[:, 20:23, :].reshape(NDEV, 3, 128).transpose(1, 0, 2).reshape(3, D)

    ncol = ada_w.shape[2]
    ada_b_mine = lax.dynamic_slice(ada_b, (0, me * ncol), (1, ncol))
    mod_cols = _ada_fwd(c_all, ada_w[0], ada_b_mine, "ada_fwd")
    transposed = (0, 2, 4)
    payload = [(w.T if i in transposed else w).astype(BF16) for i, w in enumerate(big)]
    mod_all, w_in1 = _all_gather([mod_cols, payload[0]], "gather_mod_first_weight")
    started, token = _push_start([payload[1], payload[4]], False, "gather_start", after=mod_all, near=(1,))
    started = {1: started[0], 4: started[1]}

    def start_rest(h):
        more, token = _push_start([payload[i] for i in (5, 6, 7, 2, 3)], False, "gather_start_rest", after=h)
        started.update(zip((5, 6, 7, 2, 3), more))
        return token

    mod = lax.dynamic_index_in_dim(mod_all, me, axis=1, keepdims=False).reshape(9, D)
    shift = [mod[3 * s:3 * s + 1] for s in range(3)]
    scale = [mod[3 * s + 1:3 * s + 2] for s in range(3)]
    gate = [mod[3 * s + 2:3 * s + 3] for s in range(3)]
    ng = [norm_full[s:s + 1] for s in range(3)]
    fg = final_g.reshape(1, D)
    al_row = _lane_row(a_log[0])
    dt_row = _lane_row(dt_bias[0])
    gn = dn_norm_g
    pw = pool_w[0]
    ps = pool_scale

    h0 = _norm_mod_fwd(x0, ng[0], shift[0], scale[0], "ffn1_norm", after=token)
    x1, saved1, w_in1, w_out1, h1 = _ffn_fwd(x0, h0, gate[0], w_in1, started[1], "ffn1",
                                             (ng[1], shift[1], scale[1]), token, start_rest)

    seg, = _push_wait([started[4]], False, h1, "mix_gather_wait", near=(0,))
    w_mix = _mix_pad(_relay_to_sibling(seg, "mix_gather_relay").reshape(MIX_RAW, D))
    proj = _matmul(h1, w_mix, tb=True, out_dtype=F32, name="mix_in")
    qh, kh, vh, bg = _dn_pre_fwd(proj, conv_full, al_row, dt_row, "dn_pre")
    seg = _push_wait([started[i] for i in (5, 6, 7)], False, qh, "mix_gather_wait_rest")
    w_pp = _cols_from_blocks(seg[0])
    w_dn = seg[1].reshape(D, D)
    w_mo = seg[2].reshape(D, D)
    ya = _pool_fwd(proj, pw, ps, w_pp, "pool_fwd")
    u, w, qk, qd, kd, eg, inv = _dn_local_fwd(qh, kh, vh, bg, "dn_local")
    o, s_saved = _dn_scan_fwd(u, w, qk, qd, kd, eg, "dn_scan")
    ob = _dn_post_fwd(o, proj, gn, "dn_post")
    yb = _matmul(ob, w_dn, out_dtype=F32, name="dn_out")
    merged = _merge_fwd(ya, yb, proj, "merge")
    x2, mix_y, h2 = _matmul_residual(merged, w_mo, x1, gate[1], 1.0, norm=(ng[2], shift[2], scale[2]),
                                     name="mix_out")

    x3, saved2, w_in2, w_out2, _ = _ffn_fwd(x2, h2, gate[2], started[2], started[3], "ffn2")
    loss_row, dx3, dfg, dy2, dgate2 = _final_loss(x3, fg, target, (saved2[3], gate[2], 0.5), "loss")

    (dx2, dsh2, dsc2, dng2, dmy, dgate1), sent2 = _ffn_bwd(dx3, dy2, x2, ng[2], scale[2], w_in2, w_out2, saved2,
                                                           "ffn2", (mix_y, gate[1], 1.0))

    dmerged = _matmul(dmy, w_mo, tb=True, out_dtype=BF16, name="mix_out_dx")
    dw_mo = _matmul(merged, dmy, ta=True, out_dtype=BF16, name="mix_out_dw")
    dproj = lax.empty((t, MIXP), BF16)
    dya, dyb, dproj = _merge_bwd(dmerged, ya, yb, proj, dproj, "merge_bwd")
    dob = _matmul(dyb, w_dn, tb=True, out_dtype=F32, name="dn_out_dx")
    dw_dn = _matmul(ob, dyb, ta=True, out_dtype=BF16, name="dn_out_dw")
    do, dproj, dgn = _dn_post_bwd(o, proj, gn, dob, dproj, "dn_post_bwd")
    du, dw, dqk, dqd, dkd, deg = _dn_scan_bwd(u, w, qk, qd, kd, eg, s_saved, do, "dn_scan_bwd")
    dqh, dkh, dvh, dbg = _dn_local_bwd(qh, kh, vh, bg, inv, du, dw, dqk, dqd, dkd, deg, "dn_local_bwd")
    dconv, dproj, dal, ddt = _dn_pre_bwd_act(proj, conv_full, al_row, dt_row, dqh, dkh, dvh, dbg, dproj,
                                             "dn_pre_bwd_act")
    dproj, dcw = _dn_pre_bwd_conv(proj, conv_full, dconv, dproj, "dn_pre_bwd_conv")
    dwin, dpl, dpw, dps, dpp = _pool_bwd_local(proj, pw, ps, w_pp, dya, "pool_bwd_local")
    dproj = _pool_bwd_window(dwin, dpl, dproj, "pool_bwd_window")
    dw_mix = _matmul(dproj, h1, ta=True, out_dtype=BF16, name="mix_in_dw")
    sent1, token = _push_start(
        [_mix_unpad(dw_mix, "mix_in_dw_unpad").reshape(NDEV, MIX_RAW // NDEV, D), _cols_to_blocks(dpp.astype(BF16)),
         dw_dn.reshape(NDEV, -1, D), dw_mo.reshape(NDEV, -1, D)], True, "mix_grad_start")
    dh1 = _matmul(dproj, w_mix, out_dtype=F32, name="mix_in_dx", after=token)
    dx1, dsh1, dsc1, dng1, dy0, dgate0 = _norm_mod_bwd(x1, ng[1], scale[1], dh1, dx2, "mix_norm_bwd",
                                                       (saved1[3], gate[0], 0.5))

    (dx0, dsh0, dsc0, dng0), sent0 = _ffn_bwd(dx1, dy0, x0, ng[0], scale[0], w_in1, w_out1, saved1, "ffn1")

    dmod = jnp.concatenate([dsh0, dsc0, dgate0, dsh1, dsc1, dgate1, dsh2, dsc2, dgate2], axis=1).reshape(-1)
    flat = jnp.concatenate([
        dmod, dal[0, NH:2 * NH], ddt[0, NH:2 * NH], dgn.reshape(-1), dps.reshape(-1), dfg.reshape(-1),
        dpw.reshape(-1), jnp.concatenate([dng0, dng1, dng2], axis=0).reshape(-1), dcw.reshape(-1),
        loss_row[0, 0:1]])
    nflat = 90 * D
    flat = jnp.concatenate([flat, jnp.zeros((nflat - flat.shape[0],), F32)]).reshape(90, D)
    sent_small, small_token = _push_start([flat], False, "small_grad_start")

    def small_grads(flat_all):
        tot = _sum_devices(flat_all, F32, "sum_small_grads").reshape(-1)
        dmod_all = flat_all.reshape(NDEV, nflat)[:, :9 * D]
        dmod_cols = lax.dynamic_slice(dmod_all, (0, me * ncol), (NDEV, ncol))
        g_ada_w = _ada_bwd(c_all.T, dmod_cols, "ada_bwd")
        p = 0
        pieces = {}
        for nm, size in (("ada_b", 9 * D), ("a_log", NH), ("dt_bias", NH), ("dn_norm_g", HD), ("pool_scale", PW),
                         ("final_g", D), ("pool_w", 4 * PG * PG), ("norm_g", 3 * D), ("conv_w", 12 * D),
                         ("loss", 1)):
            pieces[nm] = tot[p:p + size]
            p += size
        g_norm = lax.dynamic_slice(pieces["norm_g"].reshape(3, D), (0, me * 128), (3, 128))
        g_conv = lax.dynamic_slice(pieces["conv_w"].reshape(4, 3 * D), (0, me * 384), (4, 384))
        return pieces["loss"][0], {
            "ada_w": g_ada_w.reshape(ada_w.shape), "ada_b": pieces["ada_b"].reshape(ada_b.shape),
            "norm_g": g_norm.reshape(norm_g.shape), "conv_w": g_conv.reshape(conv_w.shape),
            "a_log": pieces["a_log"].reshape(a_log.shape), "dt_bias": pieces["dt_bias"].reshape(dt_bias.shape),
            "dn_norm_g": pieces["dn_norm_g"].reshape(dn_norm_g.shape),
            "pool_w": pieces["pool_w"].reshape(pool_w.shape),
            "pool_scale": pieces["pool_scale"].reshape(pool_scale.shape),
            "final_g": pieces["final_g"].reshape(final_g.shape),
        }

    grads = {}
    weights = {"ada_w": ada_w, "ada_b": ada_b, "norm_g": norm_g, "ffn1_w_in": ffn1_w_in, "ffn1_w_out": ffn1_w_out,
               "ffn2_w_in": ffn2_w_in, "ffn2_w_out": ffn2_w_out, "mix_w_in": mix_w_in, "conv_w": conv_w,
               "a_log": a_log, "dt_bias": dt_bias, "dn_norm_g": dn_norm_g, "pool_w": pool_w,
               "pool_scale": pool_scale, "pool_proj": pool_proj, "dn_proj": dn_proj, "mix_w_out": mix_w_out,
               "final_g": final_g}
    m_in = {"ada_w": m_ada_w, "ada_b": m_ada_b, "norm_g": m_norm_g, "ffn1_w_in": m_ffn1_w_in,
            "ffn1_w_out": m_ffn1_w_out, "ffn2_w_in": m_ffn2_w_in, "ffn2_w_out": m_ffn2_w_out,
            "mix_w_in": m_mix_w_in, "conv_w": m_conv_w, "a_log": m_a_log, "dt_bias": m_dt_bias,
            "dn_norm_g": m_dn_norm_g, "pool_w": m_pool_w, "pool_scale": m_pool_scale, "pool_proj": m_pool_proj,
            "dn_proj": m_dn_proj, "mix_w_out": m_mix_w_out, "final_g": m_final_g}
    v_in = {"ada_w": v_ada_w, "ada_b": v_ada_b, "norm_g": v_norm_g, "ffn1_w_in": v_ffn1_w_in,
            "ffn1_w_out": v_ffn1_w_out, "ffn2_w_in": v_ffn2_w_in, "ffn2_w_out": v_ffn2_w_out,
            "mix_w_in": v_mix_w_in, "conv_w": v_conv_w, "a_log": v_a_log, "dt_bias": v_dt_bias,
            "dn_norm_g": v_dn_norm_g, "pool_w": v_pool_w, "pool_scale": v_pool_scale, "pool_proj": v_pool_proj,
            "dn_proj": v_dn_proj, "mix_w_out": v_mix_w_out, "final_g": v_final_g}

    names = list(weights)
    large = ("ada_w", "ffn1_w_in", "ffn1_w_out", "ffn2_w_in", "ffn2_w_out", "mix_w_in", "pool_proj", "dn_proj",
             "mix_w_out")
    delta, new_m, new_v = {}, {}, {}

    flipped = ("ffn1_w_in", "ffn2_w_in", "mix_w_in")

    def views(nm):
        shp = weights[nm].shape
        two_d = (shp[-2], shp[-1])
        if nm in flipped:
            return (lambda a: a.reshape(two_d).T), (lambda a: a.T.reshape(shp))
        return (lambda a: a.reshape(two_d)), (lambda a: a.reshape(shp))

    def reduce_update(sent, group, after, tag):
        done = []
        for nm, r in zip(group, _push_wait(sent, True, after, f"{tag}_grad_wait")):
            view, back = views(nm)
            g_, d_, m_, v_ = _reduce_adamw(r, view(weights[nm]), view(m_in[nm]), view(v_in[nm]), f"adamw_{nm}")
            grads[nm], delta[nm], new_m[nm], new_v[nm] = back(g_), back(d_), back(m_), back(v_)
            done.append(d_)
        return done

    done = reduce_update(sent2, ("ffn2_w_in", "ffn2_w_out"), small_token, "ffn2")
    done += reduce_update(sent1, ("mix_w_in", "pool_proj", "dn_proj", "mix_w_out"), done, "mix")
    flat_all, = _push_wait(sent_small, False, done, "small_grad_wait")
    loss, small = small_grads(flat_all)
    grads.update(small)
    view, back = views("ada_w")
    done, m_, v_ = _adamw(view(ada_w), view(grads["ada_w"]), view(m_ada_w), view(v_ada_w), "adamw_ada_w")
    delta["ada_w"], new_m["ada_w"], new_v["ada_w"] = back(done), back(m_), back(v_)
    reduce_update(sent0, ("ffn1_w_in", "ffn1_w_out"), done, "ffn1")
    rest = [nm for nm in names if nm not in large]
    total = sum(weights[nm].size for nm in rest)
    padded = -(-total // D) * D

    def pack(tree, fill):
        flat_ = jnp.concatenate([tree[nm].reshape(-1) for nm in rest])
        return jnp.concatenate([flat_, jnp.full((padded - total,), fill, F32)]).reshape(-1, D)

    d_, m_, v_ = _adamw(pack(weights, 0.0), pack(grads, 0.0), pack(m_in, 0.0), pack(v_in, 1.0), "adamw_small")
    p = 0
    for nm in rest:
        size = weights[nm].size
        shp = weights[nm].shape
        delta[nm] = d_.reshape(-1)[p:p + size].reshape(shp)
        new_m[nm] = m_.reshape(-1)[p:p + size].reshape(shp)
        new_v[nm] = v_.reshape(-1)[p:p + size].reshape(shp)
        p += size

    grad_x = dx0.reshape(x.shape)
    return (loss, grad_x, *[grads[nm] for nm in names], *[delta[nm] for nm in names],
            *[new_m[nm] for nm in names], *[new_v[nm] for nm in names])
```

```python
import functools

import jax
import jax.numpy as jnp
from jax import lax
from jax.experimental import pallas as pl
from jax.experimental.pallas import tpu as pltpu

F32 = jnp.float32
BF16 = jnp.bfloat16
SDS = jax.ShapeDtypeStruct
HI = lax.Precision.HIGHEST

D = 1024
FH = 2816
FB = 704
NH = 8
HD = 128
CH = 64
SCAN_CHUNKS = 8
LOCAL_CHUNKS = 2
NDEV = 8
PW = 512
PG = 128
RMS_EPS = 1e-6
L2_EPS = 1e-6
TR = 512
HALO = 16
VMEM_LIMIT = 56 * 1024 * 1024
MATMUL_VMEM = 40 * 1024 * 1024

MIXP = 6912
OFF_Q, OFF_K, OFF_V, OFF_Z, OFF_GP, OFF_GD, OFF_XP, OFF_BA = 0, 1024, 2048, 3072, 4096, 5120, 6144, 6656
MIX_RAW = 6672

ADAM_LR = 0.001
ADAM_B1 = 0.9
ADAM_B2 = 0.999
ADAM_EPS = 1e-08
ADAM_WD = 0.01
ADAM_STEP = 10

NN = (((1,), (0,)), ((), ()))
NT = (((1,), (1,)), ((), ()))
TN = (((0,), (0,)), ((), ()))


def _dg(a, b, dims, prec=None):
    return lax.dot_general(a, b, dims, precision=prec, preferred_element_type=F32)


def _make_dots(prec):
    @jax.custom_vjp
    def nn(a, b):
        return _dg(a, b, NN, prec)

    @jax.custom_vjp
    def nt(a, b):
        return _dg(a, b, NT, prec)

    @jax.custom_vjp
    def tn(a, b):
        return _dg(a, b, TN, prec)

    nn.defvjp(lambda a, b: (nn(a, b), (a, b)), lambda r, d: (nt(d, r[1]), tn(r[0], d)))
    nt.defvjp(lambda a, b: (nt(a, b), (a, b)), lambda r, d: (nn(d, r[1]), tn(d, r[0])))
    tn.defvjp(lambda a, b: (tn(a, b), (a, b)), lambda r, d: (nt(r[1], d), nn(r[0], d)))
    return nn, nt, tn


_nn, _nt, _tn = _make_dots(None)


def _params(sem):
    return pltpu.CompilerParams(dimension_semantics=sem, vmem_limit_bytes=VMEM_LIMIT)


def _sigmoid(x):
    return 1.0 / (1.0 + jnp.exp(-x))


def _silu(x):
    return x * _sigmoid(x)


def _dsilu(x):
    s = _sigmoid(x)
    return s * (1.0 + x * (1.0 - s))


def _pick(n, cands):
    for c in cands:
        if n % c == 0:
            return c
    raise ValueError(f"no tile for {n}")


def _iota(shape, dim):
    return lax.broadcasted_iota(jnp.int32, shape, dim)


def _matmul(a, b, *, ta=False, tb=False, a_blk=False, b_blk=False, o_blk=False, tm=None, tn=None, tk=None,
            out_dtype, name, after=None):
    if a_blk:
        nb, r, cb = a.shape
        if ta:
            k_dim, m_dim, tm = r, nb * cb, cb
        else:
            m_dim, k_dim, tk = r, nb * cb, cb
    else:
        k_dim, m_dim = a.shape if ta else a.shape[::-1]
    if b_blk:
        nb, r, cb = b.shape
        if tb:
            n_dim, tk = r, cb
            assert nb * cb == k_dim
        else:
            n_dim, tn = nb * cb, cb
            assert r == k_dim
    else:
        n_dim = b.shape[0] if tb else b.shape[1]
    tn = tn or _pick(n_dim, (1024, 768, 512, 256, 128))
    out_bytes = jnp.dtype(out_dtype).itemsize

    def vmem(tm_, tk_):
        return 4 * tk_ * (tm_ + tn) + tm_ * tn * (4 + 2 * out_bytes)

    k_cands = [tk] if tk else [c for c in (k_dim, 4096, 3456, 2816, 2304, 2048, 1024, 512, 256)
                               if c <= k_dim and k_dim % c == 0]
    m_cands = [tm] if tm else [c for c in (2048, 1024, 768, 512, 256, 128) if m_dim % c == 0]
    base = next((c for c in m_cands if c <= 1024), m_cands[-1])
    tk = next((c for c in k_cands if vmem(base, c) <= MATMUL_VMEM), k_cands[-1])
    tm = next((c for c in m_cands if vmem(c, tk) <= MATMUL_VMEM), m_cands[-1])
    nk = k_dim // tk
    dims = ((((0,) if ta else (1,)), ((1,) if tb else (0,))), ((), ()))

    def body(a_ref, b_ref, *rest):
        o_ref, acc_ref = rest[-2:]
        k = pl.program_id(2)

        @pl.when(k == 0)
        def _():
            acc_ref[...] = jnp.zeros_like(acc_ref)

        acc_ref[...] += lax.dot_general(a_ref[...].astype(BF16), b_ref[...].astype(BF16), dims,
                                        preferred_element_type=F32)

        @pl.when(k == nk - 1)
        def _():
            o_ref[...] = acc_ref[...].astype(o_ref.dtype)

    if a_blk:
        a_spec = (pl.BlockSpec((None, tk, tm), lambda i, j, k: (i, k, 0)) if ta
                  else pl.BlockSpec((None, tm, tk), lambda i, j, k: (k, i, 0)))
    else:
        a_spec = (pl.BlockSpec((tk, tm), lambda i, j, k: (k, i)) if ta
                  else pl.BlockSpec((tm, tk), lambda i, j, k: (i, k)))
    if b_blk:
        b_spec = (pl.BlockSpec((None, tn, tk), lambda i, j, k: (k, j, 0)) if tb
                  else pl.BlockSpec((None, tk, tn), lambda i, j, k: (j, k, 0)))
    else:
        b_spec = (pl.BlockSpec((tn, tk), lambda i, j, k: (j, k)) if tb
                  else pl.BlockSpec((tk, tn), lambda i, j, k: (k, j)))
    if o_blk:
        o_spec = pl.BlockSpec((None, tm, tn), lambda i, j, k: (j, i, 0))
        o_shape = SDS((n_dim // tn, m_dim, tn), out_dtype)
    else:
        o_spec = pl.BlockSpec((tm, tn), lambda i, j, k: (i, j))
        o_shape = SDS((m_dim, n_dim), out_dtype)
    return pl.pallas_call(
        body, grid=(m_dim // tm, n_dim // tn, nk),
        in_specs=[a_spec, b_spec] + ([] if after is None else [pl.BlockSpec(memory_space=pl.ANY)]),
        out_specs=o_spec,
        out_shape=o_shape,
        scratch_shapes=[pltpu.VMEM((tm, tn), F32)],
        compiler_params=_params(("parallel", "parallel", "arbitrary")),
        name=name,
    )(a, b, *([] if after is None else [after]))


def _matmul_residual(a, b, x, gate, coef, *, a_blk=False, norm=None, name, after=None):
    m_dim = a.shape[-2]
    tm = _pick(m_dim, (1024, 512))
    if a_blk:
        nk, _, tk = a.shape
        a_spec = pl.BlockSpec((None, tm, tk), lambda i, k: (k, i, 0))
    else:
        tk = a.shape[1]
        nk = 1
        a_spec = pl.BlockSpec((tm, tk), lambda i, k: (i, 0))
    extra = [] if after is None else [after]
    vecs = [gate] + (list(norm) if norm else [])

    def body(a_ref, b_ref, x_ref, gate_ref, *rest):
        vec_refs = rest[:len(vecs) - 1]
        outs = rest[len(vecs) - 1 + len(extra):]
        acc_ref = outs[-1]
        k = pl.program_id(1)

        @pl.when(k == 0)
        def _():
            acc_ref[...] = jnp.zeros_like(acc_ref)

        acc_ref[...] += _dg(a_ref[...], b_ref[...], NN)

        @pl.when(k == nk - 1)
        def _():
            y = acc_ref[...]
            xn = x_ref[...] + (coef * gate_ref[...]) * y
            outs[0][...] = xn
            outs[1][...] = y.astype(outs[1].dtype)
            if norm:
                g_ref, sh_ref, sc_ref = vec_refs
                r = lax.rsqrt(jnp.mean(xn * xn, axis=-1, keepdims=True) + RMS_EPS)
                outs[2][...] = (((xn * r) * g_ref[...]) * (1.0 + sc_ref[...]) + sh_ref[...]).astype(outs[2].dtype)

    row = pl.BlockSpec((tm, D), lambda i, k: (i, 0))
    vec = pl.BlockSpec((1, D), lambda i, k: (0, 0))
    return pl.pallas_call(
        body, grid=(m_dim // tm, nk),
        in_specs=[a_spec, pl.BlockSpec((tk, D), lambda i, k: (k, 0)), row] + [vec] * len(vecs)
        + [pl.BlockSpec(memory_space=pl.ANY)] * len(extra),
        out_specs=[row] * (3 if norm else 2),
        out_shape=[SDS((m_dim, D), F32), SDS((m_dim, D), BF16)] + ([SDS((m_dim, D), BF16)] if norm else []),
        scratch_shapes=[pltpu.VMEM((tm, D), F32)],
        compiler_params=_params(("parallel", "arbitrary")), name=name,
    )(a, b, x, *vecs, *extra)


def _row(width, col=0):
    return pl.BlockSpec((TR, width), lambda i: (i, col))


def _vec(width):
    return pl.BlockSpec((1, width), lambda i: (0, 0))


def _norm_mod_fwd(x, g, shift, scale, name, after=None):
    t = x.shape[0]
    extra = [] if after is None else [after]

    def body(x_ref, g_ref, sh_ref, sc_ref, *rest):
        o_ref = rest[-1]
        xv = x_ref[...]
        r = lax.rsqrt(jnp.mean(xv * xv, axis=-1, keepdims=True) + RMS_EPS)
        o_ref[...] = (((xv * r) * g_ref[...]) * (1.0 + sc_ref[...]) + sh_ref[...]).astype(o_ref.dtype)

    return pl.pallas_call(
        body, grid=(t // TR,),
        in_specs=[_row(D), _vec(D), _vec(D), _vec(D)] + [pl.BlockSpec(memory_space=pl.ANY)] * len(extra),
        out_specs=_row(D),
        out_shape=SDS((t, D), BF16), compiler_params=_params(("parallel",)), name=name,
    )(x, g, shift, scale, *extra)


def _residual_branch_bwd(dxv, y_ref, gate_ref, coef, dy_ref, dgate_ref):
    dy_ref[...] = ((coef * gate_ref[...]) * dxv).astype(dy_ref.dtype)
    dgate_ref[...] += jnp.sum((coef * dxv) * y_ref[...], axis=0, keepdims=True)


def _norm_mod_bwd(x, g, scale, dh, dx_in, name, below=None):
    t = x.shape[0]
    lower = [] if below is None else list(below[:2])

    def body(x_ref, g_ref, sc_ref, dh_ref, dxi_ref, *rest):
        dx_ref, dsh_ref, dsc_ref, dg_ref = rest[len(lower):len(lower) + 4]

        @pl.when(pl.program_id(0) == 0)
        def _():
            for ref in rest[len(lower) + 1:]:
                if ref.shape[0] == 1:
                    ref[...] = jnp.zeros_like(ref)

        xv = x_ref[...]
        gv = g_ref[...]
        dh = dh_ref[...]
        r = lax.rsqrt(jnp.mean(xv * xv, axis=-1, keepdims=True) + RMS_EPS)
        n = xv * r
        dsh_ref[...] += jnp.sum(dh, axis=0, keepdims=True)
        dsc_ref[...] += jnp.sum(dh * (n * gv), axis=0, keepdims=True)
        tt = dh * (1.0 + sc_ref[...])
        dg_ref[...] += jnp.sum(tt * n, axis=0, keepdims=True)
        dn = tt * gv
        dxv = dxi_ref[...] + r * (dn - n * jnp.mean(dn * n, axis=-1, keepdims=True))
        dx_ref[...] = dxv
        if below is not None:
            _residual_branch_bwd(dxv, rest[0], rest[1], below[2], rest[-2], rest[-1])

    more_in = [] if below is None else [_row(D), _vec(D)]
    more_out = [] if below is None else [_row(D), _vec(D)]
    more_shape = [] if below is None else [SDS((t, D), BF16), SDS((1, D), F32)]
    return pl.pallas_call(
        body, grid=(t // TR,), in_specs=[_row(D), _vec(D), _vec(D), _row(D), _row(D)] + more_in,
        out_specs=[_row(D), _vec(D), _vec(D), _vec(D)] + more_out,
        out_shape=[SDS((t, D), F32), SDS((1, D), F32), SDS((1, D), F32), SDS((1, D), F32)] + more_shape,
        compiler_params=_params(("arbitrary",)), name=name,
    )(x, g, scale, dh, dx_in, *lower)


def _swiglu_up(h, w_in, name, after=None):
    t = h.shape[0]
    tm = _pick(t, (1024, 512, 256))
    half = NDEV // 2
    extra = [] if after is None else [after]

    def body(h_ref, wg_ref, wu_ref, *rest):
        u_ref, a_ref = rest[-2:]
        hv = h_ref[...]
        gate = _dg(hv, wg_ref[...], NT)
        up = _dg(hv, wu_ref[...], NT)
        u_ref[0] = gate.astype(u_ref.dtype)
        u_ref[1] = up.astype(u_ref.dtype)
        a_ref[...] = (_silu(gate) * up).astype(a_ref.dtype)

    return pl.pallas_call(
        body, grid=(t // tm, half),
        in_specs=[pl.BlockSpec((tm, D), lambda i, j: (i, 0)),
                  pl.BlockSpec((FB, D), lambda i, j: (j, 0)),
                  pl.BlockSpec((FB, D), lambda i, j: (j + half, 0))]
        + [pl.BlockSpec(memory_space=pl.ANY)] * len(extra),
        out_specs=[pl.BlockSpec((2, None, tm, FB), lambda i, j: (0, j, i, 0)),
                   pl.BlockSpec((None, tm, FB), lambda i, j: (j, i, 0))],
        out_shape=[SDS((2, half, t, FB), BF16), SDS((half, t, FB), BF16)],
        compiler_params=_params(("parallel", "parallel")), name=name,
    )(h, w_in, w_in, *extra)


def _swiglu_down_bwd(dy, w_out, u, name, after=None):
    t = dy.shape[0]
    tm = _pick(t, (1024, 512, 256))
    half = NDEV // 2
    extra = [] if after is None else [after]
    pair = pl.BlockSpec((2, None, tm, FB), lambda i, j: (0, j, i, 0))

    def body(dy_ref, w_ref, u_ref, *rest):
        o_ref = rest[-1]
        da = _dg(dy_ref[...], w_ref[...], NT)
        gate = u_ref[0].astype(F32)
        o_ref[0] = (da * u_ref[1].astype(F32) * _dsilu(gate)).astype(o_ref.dtype)
        o_ref[1] = (da * _silu(gate)).astype(o_ref.dtype)

    return pl.pallas_call(
        body, grid=(t // tm, half),
        in_specs=[pl.BlockSpec((tm, D), lambda i, j: (i, 0)), pl.BlockSpec((FB, D), lambda i, j: (j, 0)), pair]
        + [pl.BlockSpec(memory_space=pl.ANY)] * len(extra),
        out_specs=pair, out_shape=SDS((2, half, t, FB), BF16),
        compiler_params=_params(("parallel", "parallel")), name=name,
    )(dy, w_out, u, *extra)


def _final_loss(x, fg, target, below, name):
    t = x.shape[0]
    nt = t // TR

    def body(x_ref, g_ref, t_ref, y_ref, gate_ref, loss_ref, dx_ref, dg_ref, dy_ref, dgate_ref, acc_ref):
        i = pl.program_id(0)

        @pl.when(i == 0)
        def _():
            acc_ref[...] = jnp.zeros_like(acc_ref)
            dg_ref[...] = jnp.zeros_like(dg_ref)
            dgate_ref[...] = jnp.zeros_like(dgate_ref)

        xv = x_ref[...]
        gv = g_ref[...]
        r = lax.rsqrt(jnp.mean(xv * xv, axis=-1, keepdims=True) + RMS_EPS)
        n = xv * r
        err = n * gv - t_ref[...]
        acc_ref[...] += jnp.sum(err * err, axis=0, keepdims=True)
        dy = err * (1.0 / D)
        dg_ref[...] += jnp.sum(dy * n, axis=0, keepdims=True)
        dn = dy * gv
        dxv = r * (dn - n * jnp.mean(dn * n, axis=-1, keepdims=True))
        dx_ref[...] = dxv
        _residual_branch_bwd(dxv, y_ref, gate_ref, below[2], dy_ref, dgate_ref)

        @pl.when(i == nt - 1)
        def _():
            tot = jnp.sum(acc_ref[...], axis=1, keepdims=True) * (0.5 / D)
            loss_ref[...] = jnp.broadcast_to(tot, loss_ref.shape)

    return pl.pallas_call(
        body, grid=(nt,), in_specs=[_row(D), _vec(D), _row(D), _row(D), _vec(D)],
        out_specs=[_vec(128), _row(D), _vec(D), _row(D), _vec(D)],
        out_shape=[SDS((1, 128), F32), SDS((t, D), F32), SDS((1, D), F32), SDS((t, D), BF16), SDS((1, D), F32)],
        scratch_shapes=[pltpu.VMEM((1, D), F32)],
        compiler_params=_params(("arbitrary",)), name=name,
    )(x, fg, target, below[0], below[1])


def _halo_prev(width, col):
    per = TR // HALO
    return pl.BlockSpec((HALO, width), lambda i: (jnp.maximum(i * per - 1, 0), col))


def _halo_next(width, col, nt):
    per = TR // HALO
    return pl.BlockSpec((HALO, width), lambda i: (jnp.minimum((i + 1) * per, nt * per - 1), col))


def _pool_windows(ext, tile_index):
    rows = _iota((TR, PG), 0) + tile_index * TR + 1
    pooled, counts = [], []
    for gi in range(4):
        w = 2 << gi
        e = ext[:, gi * PG:(gi + 1) * PG]
        s = e
        step = 1
        while step < w:
            s = s + pltpu.roll(s, step, 0)
            step *= 2
        cnt = jnp.minimum(rows, w).astype(F32)
        pooled.append(s[HALO:] / cnt - e[HALO:])
        counts.append(cnt)
    return pooled, counts


def _pool_fwd(proj, pool_w, pool_scale, pool_proj, name):
    t = proj.shape[0]
    xcol = OFF_XP // PW

    def body(x_ref, h_ref, pw_ref, ps_ref, pp_ref, o_ref):
        i = pl.program_id(0)
        halo = jnp.where(i > 0, h_ref[...], 0.0)
        ext = jnp.concatenate([halo, x_ref[...]], axis=0)
        pooled, _ = _pool_windows(ext, i)
        mixed = [_dg(pooled[g].astype(BF16), pw_ref[g].astype(BF16), NN) for g in range(4)]
        ypre = jnp.concatenate(mixed, axis=1) * ps_ref[...]
        o_ref[...] = _dg(ypre.astype(BF16), pp_ref[...], NN)

    return pl.pallas_call(
        body, grid=(t // TR,),
        in_specs=[_row(PW, xcol), _halo_prev(PW, xcol),
                  pl.BlockSpec((4, PG, PG), lambda i: (0, 0, 0)), _vec(PW),
                  pl.BlockSpec((PW, D), lambda i: (0, 0))],
        out_specs=_row(D), out_shape=SDS((t, D), F32),
        compiler_params=_params(("parallel",)), name=name,
    )(proj, proj, pool_w, pool_scale, pool_proj)


def _pool_bwd_local(proj, pool_w, pool_scale, pool_proj, dya, name):
    t = proj.shape[0]
    xcol = OFF_XP // PW

    def body(x_ref, h_ref, pw_ref, ps_ref, pp_ref, dya_ref, dwin_ref, dpl_ref, dpw_ref, dps_ref, dpp_ref):
        i = pl.program_id(0)

        @pl.when(i == 0)
        def _():
            dpw_ref[...] = jnp.zeros_like(dpw_ref)
            dps_ref[...] = jnp.zeros_like(dps_ref)
            dpp_ref[...] = jnp.zeros_like(dpp_ref)

        halo = jnp.where(i > 0, h_ref[...], 0.0)
        ext = jnp.concatenate([halo, x_ref[...]], axis=0)
        pooled, counts = _pool_windows(ext, i)
        mixed = jnp.concatenate(
            [_dg(pooled[g].astype(BF16), pw_ref[g].astype(BF16), NN) for g in range(4)], axis=1)
        ps = ps_ref[...]
        ypre = mixed * ps
        dyab = dya_ref[...].astype(BF16)
        dypre = _dg(dyab, pp_ref[...], NT)
        dpp_ref[...] += _dg(ypre.astype(BF16), dyab, TN)
        dps_ref[...] += jnp.sum(dypre * mixed, axis=0, keepdims=True)
        dmixed = dypre * ps
        for g in range(4):
            dm = dmixed[:, g * PG:(g + 1) * PG].astype(BF16)
            dpw_ref[g] += _dg(pooled[g].astype(BF16), dm, TN)
            dpooled = _dg(dm, pw_ref[g].astype(BF16), NT)
            dwin_ref[:, g * PG:(g + 1) * PG] = dpooled / counts[g]
            dpl_ref[:, g * PG:(g + 1) * PG] = dpooled

    return pl.pallas_call(
        body, grid=(t // TR,),
        in_specs=[_row(PW, xcol), _halo_prev(PW, xcol),
                  pl.BlockSpec((4, PG, PG), lambda i: (0, 0, 0)), _vec(PW),
                  pl.BlockSpec((PW, D), lambda i: (0, 0)), _row(D)],
        out_specs=[_row(PW), _row(PW), pl.BlockSpec((4, PG, PG), lambda i: (0, 0, 0)), _vec(PW),
                   pl.BlockSpec((PW, D), lambda i: (0, 0))],
        out_shape=[SDS((t, PW), F32), SDS((t, PW), F32), SDS((4, PG, PG), F32), SDS((1, PW), F32),
                   SDS((PW, D), F32)],
        compiler_params=_params(("arbitrary",)), name=name,
    )(proj, proj, pool_w, pool_scale, pool_proj, dya)


def _pool_bwd_window(dwin, dpl, dproj, name):
    t = dwin.shape[0]
    nt = t // TR
    ext_rows = TR + HALO

    def body(dw_ref, h_ref, dp_ref, _, o_ref):
        i = pl.program_id(0)
        halo = jnp.where(i < nt - 1, h_ref[...], 0.0)
        ext = jnp.concatenate([dw_ref[...], halo], axis=0)
        for gi in range(4):
            w = 2 << gi
            s = ext[:, gi * PG:(gi + 1) * PG]
            step = 1
            while step < w:
                s = s + pltpu.roll(s, ext_rows - step, 0)
                step *= 2
            o_ref[:, gi * PG:(gi + 1) * PG] = (s[:TR] - dp_ref[:, gi * PG:(gi + 1) * PG]).astype(o_ref.dtype)

    return pl.pallas_call(
        body, grid=(nt,),
        in_specs=[_row(PW), _halo_next(PW, 0, nt), _row(PW), pl.BlockSpec(memory_space=pl.ANY)],
        out_specs=_into(PW, OFF_XP), out_shape=SDS(dproj.shape, dproj.dtype), input_output_aliases={3: 0},
        compiler_params=_params(("parallel",)), name=name,
    )(dwin, dwin, dpl, dproj)


def _conv_group(ext, cw_ref, cols):
    acc = cw_ref[3:4, cols] * ext
    for j in range(3):
        acc = acc + cw_ref[j:j + 1, cols] * pltpu.roll(ext, 3 - j, 0)
    return acc[HALO:]


def _gate_terms(raw, al, dt):
    beta = _sigmoid(raw)
    xg = raw + dt
    sp = jnp.maximum(xg, 0.0) + jnp.log(1.0 + jnp.exp(-jnp.abs(xg)))
    g = -jnp.exp(al) * sp
    return beta, g, _sigmoid(xg)


def _dn_pre_fwd(proj, conv_w, al_row, dt_row, name):
    t = proj.shape[0]

    def body(x_ref, h_ref, cw_ref, ba_ref, al_ref, dt_ref, q_ref, k_ref, v_ref, bg_ref):
        i = pl.program_id(0)
        keep = i > 0
        for grp in range(24):
            cols = slice(grp * HD, (grp + 1) * HD)
            ext = jnp.concatenate([jnp.where(keep, h_ref[:, cols], 0.0), x_ref[:, cols]], axis=0)
            s = _silu(_conv_group(ext, cw_ref, cols))
            seg, head = divmod(grp, NH)
            hc = slice(head * HD, (head + 1) * HD)
            if seg == 0:
                q_ref[:, hc] = s * lax.rsqrt(jnp.sum(s * s, axis=-1, keepdims=True) + L2_EPS) * (HD ** -0.5)
            elif seg == 1:
                k_ref[:, hc] = s * lax.rsqrt(jnp.sum(s * s, axis=-1, keepdims=True) + L2_EPS)
            else:
                v_ref[:, hc] = s
        lane = _iota((TR, 128), 1)
        rowc = _iota((TR, 128), 0) % CH
        beta, g, _ = _gate_terms(ba_ref[...], al_ref[...], dt_ref[...])
        step = 1
        while step < CH:
            g = g + jnp.where(rowc >= step, pltpu.roll(g, step, 0), 0.0)
            step *= 2
        bg_ref[...] = jnp.where(lane < NH, beta, jnp.where(lane < 2 * NH, g, 0.0))

    return pl.pallas_call(
        body, grid=(t // TR,),
        in_specs=[_row(3 * D, 0), _halo_prev(3 * D, 0), pl.BlockSpec((4, 3 * D), lambda i: (0, 0)),
                  _row(128, OFF_BA // 128), _vec(128), _vec(128)],
        out_specs=[_row(D), _row(D), _row(D), _row(128)],
        out_shape=[SDS((t, D), F32), SDS((t, D), F32), SDS((t, D), F32), SDS((t, 128), F32)],
        compiler_params=_params(("parallel",)), name=name,
    )(proj, proj, conv_w, proj, al_row, dt_row)


def _dn_pre_bwd_act(proj, conv_w, al_row, dt_row, dq, dk, dv, dbg, dproj, name):
    t = proj.shape[0]

    def body(x_ref, h_ref, cw_ref, ba_ref, al_ref, dt_ref, dq_ref, dk_ref, dv_ref, dbg_ref, _,
             dc_ref, draw_ref, dal_ref, ddt_ref):
        i = pl.program_id(0)

        @pl.when(i == 0)
        def _():
            dal_ref[...] = jnp.zeros_like(dal_ref)
            ddt_ref[...] = jnp.zeros_like(ddt_ref)

        keep = i > 0
        for grp in range(24):
            cols = slice(grp * HD, (grp + 1) * HD)
            ext = jnp.concatenate([jnp.where(keep, h_ref[:, cols], 0.0), x_ref[:, cols]], axis=0)
            cv = _conv_group(ext, cw_ref, cols)
            seg, head = divmod(grp, NH)
            hc = slice(head * HD, (head + 1) * HD)
            if seg == 2:
                ds = dv_ref[:, hc]
            else:
                s = _silu(cv)
                r = lax.rsqrt(jnp.sum(s * s, axis=-1, keepdims=True) + L2_EPS)
                dy = dq_ref[:, hc] if seg == 0 else dk_ref[:, hc]
                c = (HD ** -0.5) if seg == 0 else 1.0
                ds = (c * r) * (dy - s * ((r * r) * jnp.sum(dy * s, axis=-1, keepdims=True)))
            dc_ref[:, cols] = ds * _dsilu(cv)
        lane = _iota((TR, 128), 1)
        rowc = _iota((TR, 128), 0) % CH
        isb = lane < NH
        isg = jnp.logical_and(lane >= NH, lane < 2 * NH)
        beta, g, sg = _gate_terms(ba_ref[...], al_ref[...], dt_ref[...])
        dbgv = dbg_ref[...]
        dg = dbgv
        step = 1
        while step < CH:
            dg = dg + jnp.where(rowc < CH - step, pltpu.roll(dg, TR - step, 0), 0.0)
            step *= 2
        da_raw = dg * (-jnp.exp(al_ref[...])) * sg
        draw = jnp.where(isb, dbgv * beta * (1.0 - beta), jnp.where(isg, da_raw, 0.0))
        draw_ref[:, :128] = draw.astype(draw_ref.dtype)
        draw_ref[:, 128:] = jnp.zeros((TR, MIXP - OFF_BA - 128), draw_ref.dtype)
        dal_ref[...] += jnp.sum(jnp.where(isg, dg * g, 0.0), axis=0, keepdims=True)
        ddt_ref[...] += jnp.sum(jnp.where(isg, da_raw, 0.0), axis=0, keepdims=True)

    return pl.pallas_call(
        body, grid=(t // TR,),
        in_specs=[_row(3 * D, 0), _halo_prev(3 * D, 0), pl.BlockSpec((4, 3 * D), lambda i: (0, 0)),
                  _row(128, OFF_BA // 128), _vec(128), _vec(128), _row(D), _row(D), _row(D), _row(128),
                  pl.BlockSpec(memory_space=pl.ANY)],
        out_specs=[_row(3 * D), _into(MIXP - OFF_BA, OFF_BA), _vec(128), _vec(128)],
        out_shape=[SDS((t, 3 * D), F32), SDS(dproj.shape, dproj.dtype), SDS((1, 128), F32), SDS((1, 128), F32)],
        input_output_aliases={10: 1},
        compiler_params=_params(("arbitrary",)), name=name,
    )(proj, proj, conv_w, proj, al_row, dt_row, dq, dk, dv, dbg, dproj)


def _dn_pre_bwd_conv(proj, conv_w, dconv, dproj, name):
    t = proj.shape[0]
    nt = t // TR
    ext_rows = TR + HALO

    def body(x_ref, h_ref, cw_ref, dc_ref, dn_ref, _, dx_ref, dcw_ref):
        i = pl.program_id(0)

        @pl.when(i == 0)
        def _():
            dcw_ref[...] = jnp.zeros_like(dcw_ref)

        keep_prev = i > 0
        keep_next = i < nt - 1
        for grp in range(24):
            cols = slice(grp * HD, (grp + 1) * HD)
            dct = dc_ref[:, cols]
            dext = jnp.concatenate([dct, jnp.where(keep_next, dn_ref[:, cols], 0.0)], axis=0)
            acc = cw_ref[3:4, cols] * dext
            for j in range(3):
                acc = acc + cw_ref[j:j + 1, cols] * pltpu.roll(dext, ext_rows - (3 - j), 0)
            dx_ref[:, cols] = acc[:TR].astype(dx_ref.dtype)
            xext = jnp.concatenate([jnp.where(keep_prev, h_ref[:, cols], 0.0), x_ref[:, cols]], axis=0)
            for j in range(4):
                xs = xext if j == 3 else pltpu.roll(xext, 3 - j, 0)
                dcw_ref[j:j + 1, cols] += jnp.sum(xs[HALO:] * dct, axis=0, keepdims=True)

    return pl.pallas_call(
        body, grid=(nt,),
        in_specs=[_row(3 * D, 0), _halo_prev(3 * D, 0), pl.BlockSpec((4, 3 * D), lambda i: (0, 0)),
                  _row(3 * D), _halo_next(3 * D, 0, nt), pl.BlockSpec(memory_space=pl.ANY)],
        out_specs=[_into(3 * D, OFF_Q), pl.BlockSpec((4, 3 * D), lambda i: (0, 0))],
        out_shape=[SDS(dproj.shape, dproj.dtype), SDS((4, 3 * D), F32)],
        input_output_aliases={5: 0},
        compiler_params=_params(("arbitrary",)), name=name,
    )(proj, proj, conv_w, dconv, dconv, dproj)


def _dn_post_fwd(o, proj, gn, name):
    t = o.shape[0]

    def body(o_ref, z_ref, g_ref, out_ref):
        gv = g_ref[...]
        for h in range(NH):
            hc = slice(h * HD, (h + 1) * HD)
            ov = o_ref[:, hc]
            r = lax.rsqrt(jnp.mean(ov * ov, axis=-1, keepdims=True) + RMS_EPS)
            out_ref[:, hc] = (((ov * r) * gv) * _silu(z_ref[:, hc])).astype(out_ref.dtype)

    return pl.pallas_call(
        body, grid=(t // TR,), in_specs=[_row(D), _row(D, OFF_Z // D), _vec(HD)], out_specs=_row(D),
        out_shape=SDS((t, D), BF16), compiler_params=_params(("parallel",)), name=name,
    )(o, proj, gn)


def _dn_post_bwd(o, proj, gn, dob, dproj, name):
    t = o.shape[0]

    def body(o_ref, z_ref, g_ref, d_ref, _, do_ref, dz_ref, dg_ref):
        @pl.when(pl.program_id(0) == 0)
        def _():
            dg_ref[...] = jnp.zeros_like(dg_ref)

        gv = g_ref[...]
        acc = jnp.zeros((1, HD), F32)
        for h in range(NH):
            hc = slice(h * HD, (h + 1) * HD)
            ov = o_ref[:, hc]
            zv = z_ref[:, hc]
            dv = d_ref[:, hc]
            r = lax.rsqrt(jnp.mean(ov * ov, axis=-1, keepdims=True) + RMS_EPS)
            n = ov * r
            dz_ref[:, hc] = (dv * (n * gv) * _dsilu(zv)).astype(dz_ref.dtype)
            dng = dv * _silu(zv)
            acc = acc + jnp.sum(dng * n, axis=0, keepdims=True)
            dn = dng * gv
            do_ref[:, hc] = r * (dn - n * jnp.mean(dn * n, axis=-1, keepdims=True))
        dg_ref[...] += acc

    return pl.pallas_call(
        body, grid=(t // TR,),
        in_specs=[_row(D), _row(D, OFF_Z // D), _vec(HD), _row(D), pl.BlockSpec(memory_space=pl.ANY)],
        out_specs=[_row(D), _into(D, OFF_Z), _vec(HD)],
        out_shape=[SDS((t, D), F32), SDS(dproj.shape, dproj.dtype), SDS((1, HD), F32)],
        input_output_aliases={4: 1},
        compiler_params=_params(("arbitrary",)), name=name,
    )(o, proj, gn, dob, dproj)


def _merge_fwd(ya, yb, proj, name):
    t = ya.shape[0]

    def body(a_ref, b_ref, gp_ref, gd_ref, o_ref):
        o_ref[...] = (_sigmoid(gp_ref[...]) * a_ref[...] + _sigmoid(gd_ref[...]) * b_ref[...]).astype(o_ref.dtype)

    return pl.pallas_call(
        body, grid=(t // TR,), in_specs=[_row(D), _row(D), _row(D, OFF_GP // D), _row(D, OFF_GD // D)],
        out_specs=_row(D), out_shape=SDS((t, D), BF16),
        compiler_params=_params(("parallel",)), name=name,
    )(ya, yb, proj, proj)


def _into(width, offset):
    assert offset % width == 0
    return pl.BlockSpec((TR, width), lambda i: (i, offset // width))


def _merge_bwd(dm, ya, yb, proj, dproj, name):
    t = ya.shape[0]

    def body(d_ref, a_ref, b_ref, gp_ref, gd_ref, _, da_ref, db_ref, dg_ref):
        dv = d_ref[...]
        sp = _sigmoid(gp_ref[...])
        sd = _sigmoid(gd_ref[...])
        da_ref[...] = (dv * sp).astype(da_ref.dtype)
        db_ref[...] = (dv * sd).astype(db_ref.dtype)
        dg_ref[:, :D] = (dv * a_ref[...] * sp * (1.0 - sp)).astype(dg_ref.dtype)
        dg_ref[:, D:] = (dv * b_ref[...] * sd * (1.0 - sd)).astype(dg_ref.dtype)

    return pl.pallas_call(
        body, grid=(t // TR,),
        in_specs=[_row(D), _row(D), _row(D), _row(D, OFF_GP // D), _row(D, OFF_GD // D),
                  pl.BlockSpec(memory_space=pl.ANY)],
        out_specs=[_row(D), _row(D), _into(2 * D, OFF_GP)],
        out_shape=[SDS((t, D), BF16), SDS((t, D), BF16), SDS(dproj.shape, dproj.dtype)],
        input_output_aliases={5: 2},
        compiler_params=_params(("parallel",)), name=name,
    )(dm, ya, yb, proj, proj, dproj)


def _split2(x):
    hi = x.astype(BF16)
    return hi, (x - hi.astype(F32)).astype(BF16)


def _dot3(a, b, dims):
    ah, al = _split2(a)
    bh, bl = _split2(b)
    return _dg(ah, bh, dims) + (_dg(ah, bl, dims) + _dg(al, bh, dims))


def _neumann_inverses(mats):
    ri = _iota((CH, CH), 0)
    ci = _iota((CH, CH), 1)
    eye = jnp.where(ri == ci, 1.0, 0.0).astype(F32)
    xs = [-a for a in mats]
    ps = [eye + x for x in xs]
    for _ in range(5):
        xs = [_dot3(x, x, NN) for x in xs]
        ps = [p + _dot3(p, x, NN) for p, x in zip(ps, xs)]
    return ps


def _solve_with(inv):
    @jax.custom_vjp
    def solve(a, rhs):
        return _dot3(inv, rhs, NN)

    def fwd(a, rhs):
        sol = _dot3(inv, rhs, NN)
        return sol, sol

    def bwd(sol, d):
        drhs = _dot3(inv, d, TN)
        return -_dot3(drhs, sol, NT), drhs

    solve.defvjp(fwd, bwd)
    return solve


@jax.custom_vjp
def _rows_to_lanes(g64):
    ri = _iota((CH, CH), 0)
    ci = _iota((CH, CH), 1)
    diag = jnp.where(ri == ci, g64, 0.0)
    ones = jnp.ones((CH, CH), BF16)
    hi = diag.astype(BF16)
    rem = diag - hi.astype(F32)
    mid = rem.astype(BF16)
    lo = (rem - mid.astype(F32)).astype(BF16)
    return _dg(ones, hi, NN) + (_dg(ones, mid, NN) + _dg(ones, lo, NN))


def _rows_to_lanes_bwd(_, d):
    ri = _iota((CH, CH), 0)
    ci = _iota((CH, CH), 1)
    return (jnp.where(ri == ci, jnp.broadcast_to(jnp.sum(d, axis=0, keepdims=True), (CH, CH)), 0.0),)


_rows_to_lanes.defvjp(lambda g64: (_rows_to_lanes(g64), None), _rows_to_lanes_bwd)


def _chunk_local(solve_all, q, k, v, g128, g64, gl128, b128, b64):
    ri = _iota((CH, CH), 0)
    ci = _iota((CH, CH), 1)
    causal = ri >= ci
    strict = ri > ci
    gj = [_rows_to_lanes(g) for g in g64]
    decay = [jnp.where(causal, jnp.exp(jnp.where(causal, g - t, 0.0)), 0.0) for g, t in zip(g64, gj)]
    kk = [_nt(x, x) for x in k]
    a = [jnp.where(strict, b * m * dc, 0.0) for b, m, dc in zip(b64, kk, decay)]
    eg = [jnp.exp(g) for g in g128]
    rhs = [jnp.concatenate([b * x, (b * e) * y], axis=1) for b, x, e, y in zip(b128, v, eg, k)]
    sol = solve_all(a, rhs)
    qk = [jnp.where(causal, _nt(x, y) * dc, 0.0) for x, y, dc in zip(q, k, decay)]
    return ([s[:, :HD] for s in sol], [s[:, HD:] for s in sol], qk, [x * e for x, e in zip(q, eg)],
            [x * jnp.exp(gl - g) for x, gl, g in zip(k, gl128, g128)], [jnp.exp(gl) for gl in gl128])


def _all_head_gates(bgv):
    return tuple(list(z) for z in zip(*[_head_gates(bgv, h) for h in range(NH)]))


def _head_gates(bgv, h):
    lane = _iota((CH, 128), 1)
    row = _iota((CH, 128), 0)
    bcol = jnp.sum(jnp.where(lane == h, bgv, 0.0), axis=1, keepdims=True)
    gcol = jnp.sum(jnp.where(lane == NH + h, bgv, 0.0), axis=1, keepdims=True)
    g128 = jnp.broadcast_to(gcol, (CH, 128))
    gl128 = jnp.broadcast_to(jnp.sum(jnp.where(row == CH - 1, g128, 0.0), axis=0, keepdims=True), (CH, 128))
    return (g128, jnp.broadcast_to(gcol, (CH, CH)), gl128,
            jnp.broadcast_to(bcol, (CH, 128)), jnp.broadcast_to(bcol, (CH, CH)))


def _chunk_specs():
    g = LOCAL_CHUNKS
    row = pl.BlockSpec((g * CH, D), lambda i: (i, 0))
    small = pl.BlockSpec((g * CH, 128), lambda i: (i, 0))
    qk = pl.BlockSpec((g * NH, CH, CH), lambda i: (i, 0, 0))
    eg = pl.BlockSpec((g, NH, 128), lambda i: (i, 0, 0))
    return row, small, qk, eg


def _chunk_heads():
    return [(slice(c * CH, (c + 1) * CH), slice(h * HD, (h + 1) * HD), c, h)
            for c in range(LOCAL_CHUNKS) for h in range(NH)]


def _all_gates(bg_ref):
    per_chunk = [_all_head_gates(bg_ref[c * CH:(c + 1) * CH, :]) for c in range(LOCAL_CHUNKS)]
    return tuple(sum((list(pc[j]) for pc in per_chunk), []) for j in range(5))


def _dn_local_fwd(q, k, v, bg, name):
    t = q.shape[0]
    n = t // CH
    pairs = _chunk_heads()

    def body(q_ref, k_ref, v_ref, bg_ref, u_ref, w_ref, qk_ref, qd_ref, kd_ref, eg_ref, inv_ref):
        def solve_all(mats, rhs):
            invs = _neumann_inverses(mats)
            for p in range(len(pairs)):
                inv_ref[p] = invs[p]
            return [_dot3(m, r, NN) for m, r in zip(invs, rhs)]

        u, w, qk, qd, kd, egl = _chunk_local(
            solve_all, [q_ref[r, hc] for r, hc, _, _ in pairs], [k_ref[r, hc] for r, hc, _, _ in pairs],
            [v_ref[r, hc] for r, hc, _, _ in pairs], *_all_gates(bg_ref))
        for p, (r, hc, c, h) in enumerate(pairs):
            u_ref[r, hc] = u[p]
            w_ref[r, hc] = w[p].astype(w_ref.dtype)
            qd_ref[r, hc] = qd[p].astype(qd_ref.dtype)
            kd_ref[r, hc] = kd[p].astype(kd_ref.dtype)
            qk_ref[p] = qk[p].astype(qk_ref.dtype)
            eg_ref[c, h:h + 1, :] = egl[p][0:1, :]

    row, small, qkb, egb = _chunk_specs()
    return pl.pallas_call(
        body, grid=(n // LOCAL_CHUNKS,), in_specs=[row, row, row, small],
        out_specs=[row, row, qkb, row, row, egb, qkb],
        out_shape=[SDS((t, D), F32), SDS((t, D), BF16), SDS((n * NH, CH, CH), BF16), SDS((t, D), BF16),
                   SDS((t, D), BF16), SDS((n, NH, 128), F32), SDS((n * NH, CH, CH), F32)],
        compiler_params=_params(("parallel",)), name=name,
    )(q, k, v, bg)


def _dn_local_bwd(q, k, v, bg, inv, du, dw, dqk, dqd, dkd, deg, name):
    t = q.shape[0]
    n = t // CH
    pairs = _chunk_heads()

    def body(q_ref, k_ref, v_ref, bg_ref, inv_ref, du_ref, dw_ref, dqk_ref, dqd_ref, dkd_ref, deg_ref,
             dq_ref, dk_ref, dv_ref, dbg_ref):
        lane = _iota((CH, 128), 1)
        row = _iota((CH, 128), 0)
        first = jnp.where(row == 0, 1.0, 0.0)
        solves = [_solve_with(inv_ref[p]) for p in range(len(pairs))]

        def solve_all(mats, rhs):
            return [f(m, r) for f, m, r in zip(solves, mats, rhs)]

        _, vjp = jax.vjp(functools.partial(_chunk_local, solve_all),
                         [q_ref[r, hc] for r, hc, _, _ in pairs], [k_ref[r, hc] for r, hc, _, _ in pairs],
                         [v_ref[r, hc] for r, hc, _, _ in pairs], *_all_gates(bg_ref))
        cts = ([du_ref[r, hc].astype(F32) for r, hc, _, _ in pairs],
               [dw_ref[r, hc].astype(F32) for r, hc, _, _ in pairs],
               [dqk_ref[p] for p in range(len(pairs))],
               [dqd_ref[r, hc].astype(F32) for r, hc, _, _ in pairs],
               [dkd_ref[r, hc].astype(F32) for r, hc, _, _ in pairs],
               [jnp.broadcast_to(deg_ref[c, h:h + 1, :], (CH, 128)) * first for _, _, c, h in pairs])
        dq, dk, dv, dg128, dg64, dgl, db128, db64 = vjp(cts)
        acc = [jnp.zeros((CH, 128), F32) for _ in range(LOCAL_CHUNKS)]
        for p, (r, hc, c, h) in enumerate(pairs):
            dq_ref[r, hc] = dq[p]
            dk_ref[r, hc] = dk[p]
            dv_ref[r, hc] = dv[p]
            dg = jnp.sum(dg128[p], axis=1, keepdims=True) + jnp.sum(dg64[p], axis=1, keepdims=True)
            tot = jnp.sum(jnp.sum(dgl[p], axis=0, keepdims=True), axis=1, keepdims=True)
            dg = dg + jnp.where(row[:, 0:1] == CH - 1, tot, 0.0)
            db = jnp.sum(db128[p], axis=1, keepdims=True) + jnp.sum(db64[p], axis=1, keepdims=True)
            acc[c] = acc[c] + jnp.where(lane == h, db, 0.0) + jnp.where(lane == NH + h, dg, 0.0)
        for c in range(LOCAL_CHUNKS):
            dbg_ref[c * CH:(c + 1) * CH, :] = acc[c]

    row, small, qkb, egb = _chunk_specs()
    return pl.pallas_call(
        body, grid=(n // LOCAL_CHUNKS,), in_specs=[row, row, row, small, qkb, row, row, qkb, row, row, egb],
        out_specs=[row, row, row, small],
        out_shape=[SDS((t, D), F32)] * 3 + [SDS((t, 128), F32)],
        compiler_params=_params(("parallel",)), name=name,
    )(q, k, v, bg, inv, du, dw, dqk, dqd, dkd, deg)


def _state_step(s, u, w, qk, qd, kd, egl):
    ws = [_nn(a, b) for a, b in zip(w, s)]
    v_new = [a - b for a, b in zip(u, ws)]
    qs = [_nn(a, b) for a, b in zip(qd, s)]
    intra = [_nn(a, b) for a, b in zip(qk, v_new)]
    upd = [_tn(a, b) for a, b in zip(kd, v_new)]
    return [a * e + b for a, e, b in zip(s, egl, upd)], [a + b for a, b in zip(qs, intra)]


def _dn_scan_fwd(u, w, qk, qd, kd, eg, name):
    t = u.shape[0]
    n = t // CH
    g = SCAN_CHUNKS

    def body(u_ref, w_ref, qk_ref, qd_ref, kd_ref, eg_ref, o_ref, save_ref, s_ref):
        @pl.when(pl.program_id(0) == 0)
        def _():
            s_ref[...] = jnp.zeros_like(s_ref)

        cols = [slice(h * HD, (h + 1) * HD) for h in range(NH)]
        s = [s_ref[h] for h in range(NH)]
        for c in range(g):
            rows = slice(c * CH, (c + 1) * CH)
            for h in range(NH):
                save_ref[c, h] = s[h].astype(save_ref.dtype)
            s, o = _state_step(
                s, [u_ref[rows, hc] for hc in cols], [w_ref[rows, hc].astype(F32) for hc in cols],
                [qk_ref[c * NH + h].astype(F32) for h in range(NH)], [qd_ref[rows, hc].astype(F32) for hc in cols],
                [kd_ref[rows, hc].astype(F32) for hc in cols], [eg_ref[c, h:h + 1, :] for h in range(NH)])
            for h, hc in enumerate(cols):
                o_ref[rows, hc] = o[h]
        for h in range(NH):
            s_ref[h] = s[h]

    row = pl.BlockSpec((g * CH, D), lambda i: (i, 0))
    qkb = pl.BlockSpec((g * NH, CH, CH), lambda i: (i, 0, 0))
    egb = pl.BlockSpec((g, NH, 128), lambda i: (i, 0, 0))
    return pl.pallas_call(
        body, grid=(n // g,), in_specs=[row, row, qkb, row, row, egb],
        out_specs=[row, pl.BlockSpec((g, NH, HD, HD), lambda i: (i, 0, 0, 0))],
        out_shape=[SDS((t, D), F32), SDS((n, NH, HD, HD), BF16)],
        scratch_shapes=[pltpu.VMEM((NH, HD, HD), F32)],
        compiler_params=_params(("arbitrary",)), name=name,
    )(u, w, qk, qd, kd, eg)


def _dn_scan_bwd(u, w, qk, qd, kd, eg, saved, do, name):
    t = u.shape[0]
    n = t // CH
    g = SCAN_CHUNKS
    last = n // g - 1

    def body(u_ref, w_ref, qk_ref, qd_ref, kd_ref, eg_ref, sv_ref, do_ref,
             du_ref, dw_ref, dqk_ref, dqd_ref, dkd_ref, deg_ref, ds_ref):
        @pl.when(pl.program_id(0) == 0)
        def _():
            ds_ref[...] = jnp.zeros_like(ds_ref)

        cols = [slice(h * HD, (h + 1) * HD) for h in range(NH)]
        ds = [ds_ref[h] for h in range(NH)]
        for c in reversed(range(g)):
            rows = slice(c * CH, (c + 1) * CH)
            _, vjp = jax.vjp(
                _state_step, [sv_ref[c, h].astype(F32) for h in range(NH)], [u_ref[rows, hc] for hc in cols],
                [w_ref[rows, hc].astype(F32) for hc in cols], [qk_ref[c * NH + h].astype(F32) for h in range(NH)],
                [qd_ref[rows, hc].astype(F32) for hc in cols], [kd_ref[rows, hc].astype(F32) for hc in cols],
                [eg_ref[c, h:h + 1, :] for h in range(NH)])
            ds, du, dw, dqk, dqd, dkd, deg = vjp((ds, [do_ref[rows, hc] for hc in cols]))
            for h, hc in enumerate(cols):
                du_ref[rows, hc] = du[h].astype(du_ref.dtype)
                dw_ref[rows, hc] = dw[h].astype(dw_ref.dtype)
                dqk_ref[c * NH + h] = dqk[h]
                dqd_ref[rows, hc] = dqd[h].astype(dqd_ref.dtype)
                dkd_ref[rows, hc] = dkd[h].astype(dkd_ref.dtype)
                deg_ref[c, h:h + 1, :] = deg[h]
        for h in range(NH):
            ds_ref[h] = ds[h]

    row = pl.BlockSpec((g * CH, D), lambda i: (last - i, 0))
    qkb = pl.BlockSpec((g * NH, CH, CH), lambda i: (last - i, 0, 0))
    egb = pl.BlockSpec((g, NH, 128), lambda i: (last - i, 0, 0))
    return pl.pallas_call(
        body, grid=(n // g,),
        in_specs=[row, row, qkb, row, row, egb,
                  pl.BlockSpec((g, NH, HD, HD), lambda i: (last - i, 0, 0, 0)), row],
        out_specs=[row, row, qkb, row, row, egb],
        out_shape=[SDS((t, D), BF16), SDS((t, D), BF16), SDS((n * NH, CH, CH), F32), SDS((t, D), BF16),
                   SDS((t, D), BF16), SDS((n, NH, 128), F32)],
        scratch_shapes=[pltpu.VMEM((NH, HD, HD), F32)],
        compiler_params=_params(("arbitrary",)), name=name,
    )(u, w, qk, qd, kd, eg, saved, do)


def _ada_fwd(c_all, ada_w, ada_b, name):
    ncol = ada_w.shape[1]

    def body(c_ref, w_ref, b_ref, o_ref):
        o_ref[...] = _dg(_silu(c_ref[...]), w_ref[...], NN, HI) + b_ref[...]

    return pl.pallas_call(body, out_shape=SDS((NDEV, ncol), F32),
                          compiler_params=pltpu.CompilerParams(vmem_limit_bytes=VMEM_LIMIT), name=name,
                          )(c_all, ada_w, ada_b)


def _ada_bwd(c_all_t, dmod, name):
    ncol = dmod.shape[1]

    def body(c_ref, d_ref, o_ref):
        sc = _silu(c_ref[...])
        acc = sc[:, 0:1] * d_ref[0:1, :]
        for b in range(1, NDEV):
            acc = acc + sc[:, b:b + 1] * d_ref[b:b + 1, :]
        o_ref[...] = acc

    return pl.pallas_call(body, out_shape=SDS((D, ncol), F32),
                          compiler_params=pltpu.CompilerParams(vmem_limit_bytes=VMEM_LIMIT), name=name,
                          )(c_all_t, dmod)


def _sum_devices(parts, out_dtype, name):
    _, r, c = parts.shape
    tr = TR if r % TR == 0 else r

    def body(p_ref, o_ref):
        acc = p_ref[0].astype(F32)
        for i in range(1, NDEV):
            acc = acc + p_ref[i].astype(F32)
        o_ref[...] = acc.astype(o_ref.dtype)

    return pl.pallas_call(
        body, grid=(r // tr,), in_specs=[pl.BlockSpec((NDEV, tr, c), lambda i: (0, i, 0))],
        out_specs=pl.BlockSpec((tr, c), lambda i: (i, 0)), out_shape=SDS((r, c), out_dtype),
        compiler_params=_params(("parallel",)), name=name,
    )(parts)


def _adam_tiles(r, c):
    if r % 8 == 0:
        return _pick(r, (256, 352, 128, 8)), c
    return r, (256 if c % 256 == 0 else c)


def _adam_math(w, gv, m, v):
    m_new = ADAM_B1 * m + (1.0 - ADAM_B1) * gv
    v_new = ADAM_B2 * v + (1.0 - ADAM_B2) * (gv * gv)
    bc1 = 1.0 - ADAM_B1 ** ADAM_STEP
    bc2 = 1.0 - ADAM_B2 ** ADAM_STEP
    return -ADAM_LR * ((m_new / bc1) / (jnp.sqrt(v_new / bc2) + ADAM_EPS) + ADAM_WD * w), m_new, v_new


def _adamw(w, g, m, v, name):
    r, c = w.shape
    tr, tc = _adam_tiles(r, c)

    def body(w_ref, g_ref, m_ref, v_ref, d_ref, nm_ref, nv_ref):
        d_ref[...], nm_ref[...], nv_ref[...] = _adam_math(w_ref[...], g_ref[...], m_ref[...], v_ref[...])

    spec = pl.BlockSpec((tr, tc), lambda i, j: (i, j))
    return pl.pallas_call(
        body, grid=(r // tr, c // tc), in_specs=[spec] * 4, out_specs=[spec] * 3,
        out_shape=[SDS((r, c), F32)] * 3, compiler_params=_params(("parallel", "parallel")), name=name,
    )(w, g, m, v)


def _reduce_adamw(parts, w, m, v, name):
    r, c = w.shape
    tr, tc = _adam_tiles(r, c)

    def body(p_ref, w_ref, m_ref, v_ref, g_ref, d_ref, nm_ref, nv_ref):
        gv = p_ref[0].astype(F32)
        for i in range(1, NDEV):
            gv = gv + p_ref[i].astype(F32)
        g_ref[...] = gv
        d_ref[...], nm_ref[...], nv_ref[...] = _adam_math(w_ref[...], gv, m_ref[...], v_ref[...])

    spec = pl.BlockSpec((tr, tc), lambda i, j: (i, j))
    return pl.pallas_call(
        body, grid=(r // tr, c // tc),
        in_specs=[pl.BlockSpec((NDEV, tr, tc), lambda i, j: (0, i, j))] + [spec] * 3, out_specs=[spec] * 4,
        out_shape=[SDS((r, c), F32)] * 4, compiler_params=_params(("parallel", "parallel")), name=name,
    )(parts, w, m, v)


ANY = pl.BlockSpec(memory_space=pl.ANY)
MESH = pl.DeviceIdType.MESH


def _all_gather(xs, name, after=None):
    n = len(xs)
    extra = [] if after is None else [after]

    def body(*refs):
        x_refs, out_refs = refs[:n], refs[n + len(extra):2 * n + len(extra)]
        send_sems, recv_sems, local_sems = refs[-3:]
        mx, my, mc = lax.axis_index("x"), lax.axis_index("y"), lax.axis_index("c")
        me, sibling = (mx, my, mc), (mx, my, 1 - mc)
        chips = [(1 - mx, my), (mx, 1 - my), (1 - mx, 1 - my)]

        def rows(a, px, py, pc):
            return out_refs[a].at[4 * px + 2 * py + pc]

        def copy(a, k, block, to, src=None):
            return pltpu.make_async_remote_copy(
                src_ref=rows(a, *block) if src is None else src, dst_ref=rows(a, *block),
                send_sem=send_sems.at[a, k], recv_sem=recv_sems.at[a, k], device_id=to, device_id_type=MESH)

        mine = [pltpu.make_async_copy(x_refs[a], rows(a, *me), local_sems.at[a]) for a in range(n)]
        for cp in mine:
            cp.start()
        first = []
        for a in range(n):
            first.append(copy(a, 0, me, sibling, src=x_refs[a]))
            first += [copy(a, 1 + j, me, (*chip, mc), src=x_refs[a]) for j, chip in enumerate(chips)]
        for cp in first:
            cp.start()
        passed = []
        for a in range(n):
            for j, chip in enumerate(chips):
                copy(a, 1 + j, (*chip, mc), me).wait_recv()
                passed.append(copy(a, 4 + j, (*chip, mc), sibling))
                passed[-1].start()
        for a in range(n):
            copy(a, 0, sibling, me).wait_recv()
            for j, chip in enumerate(chips):
                copy(a, 4 + j, (*chip, 1 - mc), me).wait_recv()
        for cp in first + passed:
            cp.wait_send()
        for cp in mine:
            cp.wait()

    return pl.pallas_call(
        body, out_shape=[SDS((NDEV,) + x.shape, x.dtype) for x in xs], in_specs=[ANY] * (n + len(extra)),
        out_specs=[ANY] * n,
        scratch_shapes=[pltpu.SemaphoreType.DMA((n, 7)), pltpu.SemaphoreType.DMA((n, 7)),
                        pltpu.SemaphoreType.DMA((n,))],
        name=name,
    )(*xs, *extra)


HBM = pl.BlockSpec(memory_space=pltpu.HBM)
SEM = pl.BlockSpec(memory_space=pltpu.SEMAPHORE)
EFFECT = pltpu.SideEffectType.DATAFLOW_SIDE_EFFECTING


def _peers():
    mx, my, mc = lax.axis_index("x"), lax.axis_index("y"), lax.axis_index("c")
    out = []
    for k in range(1, NDEV):
        out.append((1 - mx if k & 4 else mx, 1 - my if k & 2 else my, 1 - mc if k & 1 else mc))
    return 4 * mx + 2 * my + mc, out


NEAR = (0, 1, 3, 5)


def _push_start(srcs, sliced, name, after=None, near=()):
    n = len(srcs)
    extra = [] if after is None else [after]
    lands = [lax.empty(s.shape if sliced else (NDEV,) + s.shape, s.dtype) for s in srcs]

    def body(*refs):
        src_refs, land_refs = refs[:n], refs[n:2 * n]
        outs = refs[2 * n + len(extra):]
        send_sems, recv_sems = outs[:n], outs[n:2 * n]
        token = refs[-1]
        me, peers = _peers()
        for a in range(n):
            for k, (px, py, pc) in enumerate(peers):
                if a in near and k not in NEAR:
                    continue
                src = src_refs[a].at[4 * px + 2 * py + pc] if sliced else src_refs[a]
                pltpu.make_async_remote_copy(
                    src_ref=src, dst_ref=land_refs[a].at[me], send_sem=send_sems[a].at[k],
                    recv_sem=recv_sems[a].at[k], device_id=(px, py, pc), device_id_type=MESH).start()
            pltpu.make_async_copy(src_refs[a].at[me] if sliced else src_refs[a], land_refs[a].at[me],
                                  send_sems[a].at[NDEV - 1]).start()
        token[...] = jnp.zeros_like(token)

    outs = pl.pallas_call(
        body, name=name,
        out_shape=([pltpu.SemaphoreType.DMA((NDEV,))] * n + [pltpu.SemaphoreType.DMA((NDEV - 1,))] * n
                   + [pltpu.HBM(s.shape, s.dtype) for s in srcs] + [pltpu.HBM(l.shape, l.dtype) for l in lands]
                   + [SDS((8, 128), F32)]),
        in_specs=[HBM] * (2 * n) + [pl.BlockSpec(memory_space=pl.ANY)] * len(extra),
        out_specs=[SEM] * (2 * n) + [HBM] * (2 * n) + [pl.BlockSpec(memory_space=pltpu.VMEM)],
        input_output_aliases={i: 2 * n + i for i in range(2 * n)},
        compiler_params=pltpu.CompilerParams(has_side_effects=EFFECT),
    )(*[pltpu.with_memory_space_constraint(s, pltpu.HBM) for s in srcs],
      *[pltpu.with_memory_space_constraint(l, pltpu.HBM) for l in lands], *extra)
    sends, recvs = outs[:n], outs[n:2 * n]
    src_thru, land_thru = outs[2 * n:3 * n], outs[3 * n:4 * n]
    return [(sends[a], recvs[a], src_thru[a], land_thru[a]) for a in range(n)], outs[-1]


def _push_wait(started, sliced, after, name, near=()):
    n = len(started)
    afters = list(after) if isinstance(after, (list, tuple)) else [after]

    def body(*refs):
        src_refs, land_refs = refs[:n], refs[n:2 * n]
        send_sems, recv_sems = refs[2 * n:3 * n], refs[3 * n:4 * n]
        me, peers = _peers()
        for a in range(n):
            for k, (px, py, pc) in enumerate(peers):
                if a in near and k not in NEAR:
                    continue
                src = src_refs[a].at[4 * px + 2 * py + pc] if sliced else src_refs[a]
                cp = pltpu.make_async_remote_copy(
                    src_ref=src, dst_ref=land_refs[a].at[me], send_sem=send_sems[a].at[k],
                    recv_sem=recv_sems[a].at[k], device_id=(px, py, pc), device_id_type=MESH)
                cp.wait_send()
                cp.wait_recv()
            pltpu.make_async_copy(src_refs[a].at[me] if sliced else src_refs[a], land_refs[a].at[me],
                                  send_sems[a].at[NDEV - 1]).wait()

    srcs = [s[2] for s in started]
    lands = [s[3] for s in started]
    outs = pl.pallas_call(
        body, name=name,
        out_shape=[pltpu.HBM(s.shape, s.dtype) for s in srcs] + [pltpu.HBM(l.shape, l.dtype) for l in lands],
        in_specs=[HBM] * (2 * n) + [SEM] * (2 * n) + [pl.BlockSpec(memory_space=pl.ANY)] * len(afters),
        out_specs=[HBM] * (2 * n),
        input_output_aliases={i: i for i in range(2 * n)},
        compiler_params=pltpu.CompilerParams(has_side_effects=EFFECT),
    )(*srcs, *lands, *[s[0] for s in started], *[s[1] for s in started], *afters)
    return outs[n:]


def _relay_to_sibling(land, name):
    def body(_, land_ref, send_sems, recv_sems):
        mx, my, mc = lax.axis_index("x"), lax.axis_index("y"), lax.axis_index("c")
        chips = [(1 - mx, my), (mx, 1 - my), (1 - mx, 1 - my)]

        def copy(j, core):
            slot = land_ref.at[4 * chips[j][0] + 2 * chips[j][1] + core]
            return pltpu.make_async_remote_copy(
                src_ref=slot, dst_ref=slot, send_sem=send_sems.at[j], recv_sem=recv_sems.at[j],
                device_id=(mx, my, 1 - mc), device_id_type=MESH)

        mine = [copy(j, mc) for j in range(3)]
        for cp in mine:
            cp.start()
        for j in range(3):
            copy(j, 1 - mc).wait_recv()
        for cp in mine:
            cp.wait_send()

    return pl.pallas_call(
        body, out_shape=SDS(land.shape, land.dtype), in_specs=[ANY], out_specs=ANY, input_output_aliases={0: 0},
        scratch_shapes=[pltpu.SemaphoreType.DMA((3,)), pltpu.SemaphoreType.DMA((3,))], name=name,
    )(land)


def _cols_from_blocks(blocks):
    _, rows, w = blocks.shape
    return blocks.transpose(1, 0, 2).reshape(rows, NDEV * w)


def _cols_to_blocks(full):
    rows, total = full.shape
    return full.reshape(rows, NDEV, total // NDEV).transpose(1, 0, 2)


def _mix_pad(wt):
    xp, q, k, v, z, ba, gp, gd = jnp.split(wt, (512, 1536, 2560, 3584, 4608, 4624, 5648), axis=0)
    pad = jnp.zeros((MIXP - OFF_BA - 16, wt.shape[1]), wt.dtype)
    return jnp.concatenate([q, k, v, z, gp, gd, xp, ba, pad], axis=0)


def _mix_unpad(wt):
    q, k, v, z, gp, gd, xp, ba = (wt[OFF_Q:OFF_K], wt[OFF_K:OFF_V], wt[OFF_V:OFF_Z], wt[OFF_Z:OFF_GP],
                                  wt[OFF_GP:OFF_GD], wt[OFF_GD:OFF_XP], wt[OFF_XP:OFF_BA], wt[OFF_BA:OFF_BA + 16])
    return jnp.concatenate([xp, q, k, v, z, ba, gp, gd], axis=0)


def _lane_row(vec8):
    return jnp.zeros((1, 128), F32).at[0, NH:2 * NH].set(vec8)


def _ffn_fwd(x, h, gate, w_in, w_out, tag, next_norm=None, token=None, start_more=None):
    if isinstance(w_in, tuple):
        w_in, = _push_wait([w_in], False, h, f"{tag}_gather_wait_in")
    w_in = w_in.reshape(2 * FH, D)
    u, a = _swiglu_up(h, w_in, f"{tag}_up", after=token)
    w_out, = _push_wait([w_out], False, a, f"{tag}_gather_wait_out")
    w_out = w_out.reshape(FH, D)
    outs = _matmul_residual(a, w_out, x, gate, 0.5, a_blk=True, norm=next_norm, name=f"{tag}_down",
                            after=None if start_more is None else start_more(h))
    return outs[0], (h, u, a, outs[1]), w_in, w_out, (outs[2] if next_norm else None)


def _ffn_bwd(dx_out, dy, x, g, scale, w_in, w_out, saved, tag, below=None):
    h, u, a, _ = saved
    t = x.shape[0]
    dw_out = _matmul(a, dy, ta=True, a_blk=True, out_dtype=BF16, name=f"{tag}_down_dw")
    sent_out, token = _push_start([dw_out.reshape(NDEV, FH // NDEV, D)], True, f"{tag}_grad_start_out")
    du = _swiglu_down_bwd(dy, w_out, u, f"{tag}_down_dx", after=token).reshape(NDEV, t, FB)
    dw_in = _matmul(du, h, ta=True, a_blk=True, out_dtype=BF16, name=f"{tag}_up_dw")
    sent_in, token = _push_start([dw_in.reshape(NDEV, FB, D)], True, f"{tag}_grad_start_in")
    dh = _matmul(du, w_in, a_blk=True, out_dtype=F32, name=f"{tag}_up_dx", after=token)
    return _norm_mod_bwd(x, g, scale, dh, dx_out, f"{tag}_norm_bwd", below), sent_in + sent_out


def kernel(x, c, ada_w, ada_b, norm_g, ffn1_w_in, ffn1_w_out, ffn2_w_in, ffn2_w_out, mix_w_in, conv_w, a_log, dt_bias, dn_norm_g, pool_w, pool_scale, pool_proj, dn_proj, mix_w_out, final_g, loss_target, m_ada_w, m_ada_b, m_norm_g, m_ffn1_w_in, m_ffn1_w_out, m_ffn2_w_in, m_ffn2_w_out, m_mix_w_in, m_conv_w, m_a_log, m_dt_bias, m_dn_norm_g, m_pool_w, m_pool_scale, m_pool_proj, m_dn_proj, m_mix_w_out, m_final_g, v_ada_w, v_ada_b, v_norm_g, v_ffn1_w_in, v_ffn1_w_out, v_ffn2_w_in, v_ffn2_w_out, v_mix_w_in, v_conv_w, v_a_log, v_dt_bias, v_dn_norm_g, v_pool_w, v_pool_scale, v_pool_proj, v_dn_proj, v_mix_w_out, v_final_g):
    me = 4 * lax.axis_index("x") + 2 * lax.axis_index("y") + lax.axis_index("c")
    x0 = x[0]
    target = loss_target[0]
    t = x0.shape[0]

    big = [ffn1_w_in[0], ffn1_w_out[0], ffn2_w_in[0], ffn2_w_out[0], mix_w_in[0], pool_proj[0], dn_proj[0],
           mix_w_out[0]]
    small = jnp.concatenate([c.reshape(8, 128), conv_w[0].reshape(12, 128), norm_g[0].reshape(3, 128),
                             jnp.zeros((1, 128), F32)], axis=0)
    small_all, = _all_gather([small], "gather_small")
    c_all = small_all[:, 0:8, :].reshape(NDEV, D)
    conv_full = small_all[:, 8:20, :].reshape(NDEV, 4, 384).transpose(1, 0, 2).reshape(4, 3 * D)
    norm_full = small_all[:, 20:23, :].reshape(NDEV, 3, 128).transpose(1, 0, 2).reshape(3, D)

    ncol = ada_w.shape[2]
    ada_b_mine = lax.dynamic_slice(ada_b, (0, me * ncol), (1, ncol))
    mod_cols = _ada_fwd(c_all, ada_w[0], ada_b_mine, "ada_fwd")
    transposed = (0, 2, 4)
    payload = [(w.T if i in transposed else w).astype(BF16) for i, w in enumerate(big)]
    mod_all, w_in1 = _all_gather([mod_cols, payload[0]], "gather_mod_first_weight")
    started, token = _push_start([payload[1], payload[4]], False, "gather_start", after=mod_all, near=(1,))
    started = {1: started[0], 4: started[1]}

    def start_rest(h):
        more, token = _push_start([payload[i] for i in (5, 6, 7, 2, 3)], False, "gather_start_rest", after=h)
        started.update(zip((5, 6, 7, 2, 3), more))
        return token

    mod = lax.dynamic_index_in_dim(mod_all, me, axis=1, keepdims=False).reshape(9, D)
    shift = [mod[3 * s:3 * s + 1] for s in range(3)]
    scale = [mod[3 * s + 1:3 * s + 2] for s in range(3)]
    gate = [mod[3 * s + 2:3 * s + 3] for s in range(3)]
    ng = [norm_full[s:s + 1] for s in range(3)]
    fg = final_g.reshape(1, D)
    al_row = _lane_row(a_log[0])
    dt_row = _lane_row(dt_bias[0])
    gn = dn_norm_g
    pw = pool_w[0]
    ps = pool_scale

    h0 = _norm_mod_fwd(x0, ng[0], shift[0], scale[0], "ffn1_norm", after=token)
    x1, saved1, w_in1, w_out1, h1 = _ffn_fwd(x0, h0, gate[0], w_in1, started[1], "ffn1",
                                             (ng[1], shift[1], scale[1]), token, start_rest)

    seg, = _push_wait([started[4]], False, h1, "mix_gather_wait", near=(0,))
    w_mix = _mix_pad(_relay_to_sibling(seg, "mix_gather_relay").reshape(MIX_RAW, D))
    proj = _matmul(h1, w_mix, tb=True, out_dtype=F32, name="mix_in")
    qh, kh, vh, bg = _dn_pre_fwd(proj, conv_full, al_row, dt_row, "dn_pre")
    seg = _push_wait([started[i] for i in (5, 6, 7)], False, qh, "mix_gather_wait_rest")
    w_pp = _cols_from_blocks(seg[0])
    w_dn = seg[1].reshape(D, D)
    w_mo = seg[2].reshape(D, D)
    ya = _pool_fwd(proj, pw, ps, w_pp, "pool_fwd")
    u, w, qk, qd, kd, eg, inv = _dn_local_fwd(qh, kh, vh, bg, "dn_local")
    o, s_saved = _dn_scan_fwd(u, w, qk, qd, kd, eg, "dn_scan")
    ob = _dn_post_fwd(o, proj, gn, "dn_post")
    yb = _matmul(ob, w_dn, out_dtype=F32, name="dn_out")
    merged = _merge_fwd(ya, yb, proj, "merge")
    x2, mix_y, h2 = _matmul_residual(merged, w_mo, x1, gate[1], 1.0, norm=(ng[2], shift[2], scale[2]),
                                     name="mix_out")

    x3, saved2, w_in2, w_out2, _ = _ffn_fwd(x2, h2, gate[2], started[2], started[3], "ffn2")
    loss_row, dx3, dfg, dy2, dgate2 = _final_loss(x3, fg, target, (saved2[3], gate[2], 0.5), "loss")

    (dx2, dsh2, dsc2, dng2, dmy, dgate1), sent2 = _ffn_bwd(dx3, dy2, x2, ng[2], scale[2], w_in2, w_out2, saved2,
                                                           "ffn2", (mix_y, gate[1], 1.0))

    dmerged = _matmul(dmy, w_mo, tb=True, out_dtype=BF16, name="mix_out_dx")
    dw_mo = _matmul(merged, dmy, ta=True, out_dtype=BF16, name="mix_out_dw")
    dproj = lax.empty((t, MIXP), BF16)
    dya, dyb, dproj = _merge_bwd(dmerged, ya, yb, proj, dproj, "merge_bwd")
    dob = _matmul(dyb, w_dn, tb=True, out_dtype=F32, name="dn_out_dx")
    dw_dn = _matmul(ob, dyb, ta=True, out_dtype=BF16, name="dn_out_dw")
    do, dproj, dgn = _dn_post_bwd(o, proj, gn, dob, dproj, "dn_post_bwd")
    du, dw, dqk, dqd, dkd, deg = _dn_scan_bwd(u, w, qk, qd, kd, eg, s_saved, do, "dn_scan_bwd")
    dqh, dkh, dvh, dbg = _dn_local_bwd(qh, kh, vh, bg, inv, du, dw, dqk, dqd, dkd, deg, "dn_local_bwd")
    dconv, dproj, dal, ddt = _dn_pre_bwd_act(proj, conv_full, al_row, dt_row, dqh, dkh, dvh, dbg, dproj,
                                             "dn_pre_bwd_act")
    dproj, dcw = _dn_pre_bwd_conv(proj, conv_full, dconv, dproj, "dn_pre_bwd_conv")
    dwin, dpl, dpw, dps, dpp = _pool_bwd_local(proj, pw, ps, w_pp, dya, "pool_bwd_local")
    dproj = _pool_bwd_window(dwin, dpl, dproj, "pool_bwd_window")
    dw_mix = _matmul(dproj, h1, ta=True, out_dtype=BF16, name="mix_in_dw")
    sent1, token = _push_start(
        [_mix_unpad(dw_mix).reshape(NDEV, MIX_RAW // NDEV, D), _cols_to_blocks(dpp.astype(BF16)),
         dw_dn.reshape(NDEV, -1, D), dw_mo.reshape(NDEV, -1, D)], True, "mix_grad_start")
    dh1 = _matmul(dproj, w_mix, out_dtype=F32, name="mix_in_dx", after=token)
    dx1, dsh1, dsc1, dng1, dy0, dgate0 = _norm_mod_bwd(x1, ng[1], scale[1], dh1, dx2, "mix_norm_bwd",
                                                       (saved1[3], gate[0], 0.5))

    (dx0, dsh0, dsc0, dng0), sent0 = _ffn_bwd(dx1, dy0, x0, ng[0], scale[0], w_in1, w_out1, saved1, "ffn1")

    dmod = jnp.concatenate([dsh0, dsc0, dgate0, dsh1, dsc1, dgate1, dsh2, dsc2, dgate2], axis=1).reshape(-1)
    flat = jnp.concatenate([
        dmod, dal[0, NH:2 * NH], ddt[0, NH:2 * NH], dgn.reshape(-1), dps.reshape(-1), dfg.reshape(-1),
        dpw.reshape(-1), jnp.concatenate([dng0, dng1, dng2], axis=0).reshape(-1), dcw.reshape(-1),
        loss_row[0, 0:1]])
    nflat = 90 * D
    flat = jnp.concatenate([flat, jnp.zeros((nflat - flat.shape[0],), F32)]).reshape(90, D)
    sent_small, small_token = _push_start([flat], False, "small_grad_start")

    def small_grads(flat_all):
        tot = _sum_devices(flat_all, F32, "sum_small_grads").reshape(-1)
        dmod_all = flat_all.reshape(NDEV, nflat)[:, :9 * D]
        dmod_cols = lax.dynamic_slice(dmod_all, (0, me * ncol), (NDEV, ncol))
        g_ada_w = _ada_bwd(c_all.T, dmod_cols, "ada_bwd")
        p = 0
        pieces = {}
        for nm, size in (("ada_b", 9 * D), ("a_log", NH), ("dt_bias", NH), ("dn_norm_g", HD), ("pool_scale", PW),
                         ("final_g", D), ("pool_w", 4 * PG * PG), ("norm_g", 3 * D), ("conv_w", 12 * D),
                         ("loss", 1)):
            pieces[nm] = tot[p:p + size]
            p += size
        g_norm = lax.dynamic_slice(pieces["norm_g"].reshape(3, D), (0, me * 128), (3, 128))
        g_conv = lax.dynamic_slice(pieces["conv_w"].reshape(4, 3 * D), (0, me * 384), (4, 384))
        return pieces["loss"][0], {
            "ada_w": g_ada_w.reshape(ada_w.shape), "ada_b": pieces["ada_b"].reshape(ada_b.shape),
            "norm_g": g_norm.reshape(norm_g.shape), "conv_w": g_conv.reshape(conv_w.shape),
            "a_log": pieces["a_log"].reshape(a_log.shape), "dt_bias": pieces["dt_bias"].reshape(dt_bias.shape),
            "dn_norm_g": pieces["dn_norm_g"].reshape(dn_norm_g.shape),
            "pool_w": pieces["pool_w"].reshape(pool_w.shape),
            "pool_scale": pieces["pool_scale"].reshape(pool_scale.shape),
            "final_g": pieces["final_g"].reshape(final_g.shape),
        }

    grads = {}
    weights = {"ada_w": ada_w, "ada_b": ada_b, "norm_g": norm_g, "ffn1_w_in": ffn1_w_in, "ffn1_w_out": ffn1_w_out,
               "ffn2_w_in": ffn2_w_in, "ffn2_w_out": ffn2_w_out, "mix_w_in": mix_w_in, "conv_w": conv_w,
               "a_log": a_log, "dt_bias": dt_bias, "dn_norm_g": dn_norm_g, "pool_w": pool_w,
               "pool_scale": pool_scale, "pool_proj": pool_proj, "dn_proj": dn_proj, "mix_w_out": mix_w_out,
               "final_g": final_g}
    m_in = {"ada_w": m_ada_w, "ada_b": m_ada_b, "norm_g": m_norm_g, "ffn1_w_in": m_ffn1_w_in,
            "ffn1_w_out": m_ffn1_w_out, "ffn2_w_in": m_ffn2_w_in, "ffn2_w_out": m_ffn2_w_out,
            "mix_w_in": m_mix_w_in, "conv_w": m_conv_w, "a_log": m_a_log, "dt_bias": m_dt_bias,
            "dn_norm_g": m_dn_norm_g, "pool_w": m_pool_w, "pool_scale": m_pool_scale, "pool_proj": m_pool_proj,
            "dn_proj": m_dn_proj, "mix_w_out": m_mix_w_out, "final_g": m_final_g}
    v_in = {"ada_w": v_ada_w, "ada_b": v_ada_b, "norm_g": v_norm_g, "ffn1_w_in": v_ffn1_w_in,
            "ffn1_w_out": v_ffn1_w_out, "ffn2_w_in": v_ffn2_w_in, "ffn2_w_out": v_ffn2_w_out,
            "mix_w_in": v_mix_w_in, "conv_w": v_conv_w, "a_log": v_a_log, "dt_bias": v_dt_bias,
            "dn_norm_g": v_dn_norm_g, "pool_w": v_pool_w, "pool_scale": v_pool_scale, "pool_proj": v_pool_proj,
            "dn_proj": v_dn_proj, "mix_w_out": v_mix_w_out, "final_g": v_final_g}

    names = list(weights)
    large = ("ada_w", "ffn1_w_in", "ffn1_w_out", "ffn2_w_in", "ffn2_w_out", "mix_w_in", "pool_proj", "dn_proj",
             "mix_w_out")
    delta, new_m, new_v = {}, {}, {}

    flipped = ("ffn1_w_in", "ffn2_w_in", "mix_w_in")

    def views(nm):
        shp = weights[nm].shape
        two_d = (shp[-2], shp[-1])
        if nm in flipped:
            return (lambda a: a.reshape(two_d).T), (lambda a: a.T.reshape(shp))
        return (lambda a: a.reshape(two_d)), (lambda a: a.reshape(shp))

    def reduce_update(sent, group, after, tag):
        done = []
        for nm, r in zip(group, _push_wait(sent, True, after, f"{tag}_grad_wait")):
            view, back = views(nm)
            g_, d_, m_, v_ = _reduce_adamw(r, view(weights[nm]), view(m_in[nm]), view(v_in[nm]), f"adamw_{nm}")
            grads[nm], delta[nm], new_m[nm], new_v[nm] = back(g_), back(d_), back(m_), back(v_)
            done.append(d_)
        return done

    done = reduce_update(sent2, ("ffn2_w_in", "ffn2_w_out"), small_token, "ffn2")
    done += reduce_update(sent1, ("mix_w_in", "pool_proj", "dn_proj", "mix_w_out"), done, "mix")
    flat_all, = _push_wait(sent_small, False, done, "small_grad_wait")
    loss, small = small_grads(flat_all)
    grads.update(small)
    view, back = views("ada_w")
    done, m_, v_ = _adamw(view(ada_w), view(grads["ada_w"]), view(m_ada_w), view(v_ada_w), "adamw_ada_w")
    delta["ada_w"], new_m["ada_w"], new_v["ada_w"] = back(done), back(m_), back(v_)
    reduce_update(sent0, ("ffn1_w_in", "ffn1_w_out"), done, "ffn1")
    rest = [nm for nm in names if nm not in large]
    total = sum(weights[nm].size for nm in rest)
    padded = -(-total // D) * D

    def pack(tree, fill):
        flat_ = jnp.concatenate([tree[nm].reshape(-1) for nm in rest])
        return jnp.concatenate([flat_, jnp.full((padded - total,), fill, F32)]).reshape(-1, D)

    d_, m_, v_ = _adamw(pack(weights, 0.0), pack(grads, 0.0), pack(m_in, 0.0), pack(v_in, 1.0), "adamw_small")
    p = 0
    for nm in rest:
        size = weights[nm].size
        shp = weights[nm].shape
        delta[nm] = d_.reshape(-1)[p:p + size].reshape(shp)
        new_m[nm] = m_.reshape(-1)[p:p + size].reshape(shp)
        new_v[nm] = v_.reshape(-1)[p:p + size].reshape(shp)
        p += size

    grad_x = dx0.reshape(x.shape)
    return (loss, grad_x, *[grads[nm] for nm in names], *[delta[nm] for nm in names],
            *[new_m[nm] for nm in names], *[new_v[nm] for nm in names])
```

```python
import functools

import jax
import jax.numpy as jnp
from jax import lax
from jax.experimental import pallas as pl
from jax.experimental.pallas import tpu as pltpu

F32 = jnp.float32
BF16 = jnp.bfloat16
SDS = jax.ShapeDtypeStruct
HI = lax.Precision.HIGHEST

D = 1024
FH = 2816
FB = 704
NH = 8
HD = 128
CH = 64
SCAN_CHUNKS = 8
LOCAL_CHUNKS = 4
NDEV = 8
PW = 512
PG = 128
RMS_EPS = 1e-6
L2_EPS = 1e-6
TR = 512
HALO = 16
VMEM_LIMIT = 56 * 1024 * 1024
MATMUL_VMEM = 40 * 1024 * 1024

MIXP = 6912
OFF_Q, OFF_K, OFF_V, OFF_Z, OFF_GP, OFF_GD, OFF_XP, OFF_BA = 0, 1024, 2048, 3072, 4096, 5120, 6144, 6656
MIX_RAW = 6672

ADAM_LR = 0.001
ADAM_B1 = 0.9
ADAM_B2 = 0.999
ADAM_EPS = 1e-08
ADAM_WD = 0.01
ADAM_STEP = 10

NN = (((1,), (0,)), ((), ()))
NT = (((1,), (1,)), ((), ()))
TN = (((0,), (0,)), ((), ()))


def _dg(a, b, dims, prec=None):
    return lax.dot_general(a, b, dims, precision=prec, preferred_element_type=F32)


def _make_dots(prec):
    @jax.custom_vjp
    def nn(a, b):
        return _dg(a, b, NN, prec)

    @jax.custom_vjp
    def nt(a, b):
        return _dg(a, b, NT, prec)

    @jax.custom_vjp
    def tn(a, b):
        return _dg(a, b, TN, prec)

    nn.defvjp(lambda a, b: (nn(a, b), (a, b)), lambda r, d: (nt(d, r[1]), tn(r[0], d)))
    nt.defvjp(lambda a, b: (nt(a, b), (a, b)), lambda r, d: (nn(d, r[1]), tn(d, r[0])))
    tn.defvjp(lambda a, b: (tn(a, b), (a, b)), lambda r, d: (nt(r[1], d), nn(r[0], d)))
    return nn, nt, tn


_nn, _nt, _tn = _make_dots(None)


def _params(sem):
    return pltpu.CompilerParams(dimension_semantics=sem, vmem_limit_bytes=VMEM_LIMIT)


def _sigmoid(x):
    return 1.0 / (1.0 + jnp.exp(-x))


def _silu(x):
    return x * _sigmoid(x)


def _silu_pair(x):
    s = _sigmoid(x)
    return x * s, s * (1.0 + x * (1.0 - s))


def _pick(n, cands):
    for c in cands:
        if n % c == 0:
            return c
    raise ValueError(f"no tile for {n}")


def _iota(shape, dim):
    return lax.broadcasted_iota(jnp.int32, shape, dim)


def _matmul(a, b, *, ta=False, tb=False, a_blk=False, b_blk=False, o_blk=False, tm=None, tn=None, tk=None,
            out_dtype, name, after=None):
    if a_blk:
        nb, r, cb = a.shape
        if ta:
            k_dim, m_dim, tm = r, nb * cb, cb
        else:
            m_dim, k_dim, tk = r, nb * cb, cb
    else:
        k_dim, m_dim = a.shape if ta else a.shape[::-1]
    if b_blk:
        nb, r, cb = b.shape
        if tb:
            n_dim, tk = r, cb
            assert nb * cb == k_dim
        else:
            n_dim, tn = nb * cb, cb
            assert r == k_dim
    else:
        n_dim = b.shape[0] if tb else b.shape[1]
    tn = tn or _pick(n_dim, (1024, 768, 512, 256, 128))
    out_bytes = jnp.dtype(out_dtype).itemsize

    def vmem(tm_, tk_):
        return 4 * tk_ * (tm_ + tn) + tm_ * tn * (4 + 2 * out_bytes)

    k_cands = [tk] if tk else [c for c in (k_dim, 4096, 3456, 2816, 2304, 2048, 1024, 512, 256)
                               if c <= k_dim and k_dim % c == 0]
    m_cands = [tm] if tm else [c for c in (2048, 1024, 768, 512, 256, 128) if m_dim % c == 0]
    base = next((c for c in m_cands if c <= 1024), m_cands[-1])
    tk = next((c for c in k_cands if vmem(base, c) <= MATMUL_VMEM), k_cands[-1])
    tm = next((c for c in m_cands if vmem(c, tk) <= MATMUL_VMEM), m_cands[-1])
    nk = k_dim // tk
    dims = ((((0,) if ta else (1,)), ((1,) if tb else (0,))), ((), ()))

    def body(a_ref, b_ref, *rest):
        o_ref, acc_ref = rest[-2:]
        k = pl.program_id(2)

        @pl.when(k == 0)
        def _():
            acc_ref[...] = jnp.zeros_like(acc_ref)

        acc_ref[...] += lax.dot_general(a_ref[...].astype(BF16), b_ref[...].astype(BF16), dims,
                                        preferred_element_type=F32)

        @pl.when(k == nk - 1)
        def _():
            o_ref[...] = acc_ref[...].astype(o_ref.dtype)

    if a_blk:
        a_spec = (pl.BlockSpec((None, tk, tm), lambda i, j, k: (i, k, 0)) if ta
                  else pl.BlockSpec((None, tm, tk), lambda i, j, k: (k, i, 0)))
    else:
        a_spec = (pl.BlockSpec((tk, tm), lambda i, j, k: (k, i)) if ta
                  else pl.BlockSpec((tm, tk), lambda i, j, k: (i, k)))
    if b_blk:
        b_spec = (pl.BlockSpec((None, tn, tk), lambda i, j, k: (k, j, 0)) if tb
                  else pl.BlockSpec((None, tk, tn), lambda i, j, k: (j, k, 0)))
    else:
        b_spec = (pl.BlockSpec((tn, tk), lambda i, j, k: (j, k)) if tb
                  else pl.BlockSpec((tk, tn), lambda i, j, k: (k, j)))
    if o_blk:
        o_spec = pl.BlockSpec((None, tm, tn), lambda i, j, k: (j, i, 0))
        o_shape = SDS((n_dim // tn, m_dim, tn), out_dtype)
    else:
        o_spec = pl.BlockSpec((tm, tn), lambda i, j, k: (i, j))
        o_shape = SDS((m_dim, n_dim), out_dtype)
    return pl.pallas_call(
        body, grid=(m_dim // tm, n_dim // tn, nk),
        in_specs=[a_spec, b_spec] + ([] if after is None else [pl.BlockSpec(memory_space=pl.ANY)]),
        out_specs=o_spec,
        out_shape=o_shape,
        scratch_shapes=[pltpu.VMEM((tm, tn), F32)],
        compiler_params=_params(("parallel", "parallel", "arbitrary")),
        name=name,
    )(a, b, *([] if after is None else [after]))


def _matmul_residual(a, b, x, gate, coef, *, a_blk=False, norm=None, name, after=None):
    m_dim = a.shape[-2]
    tm = _pick(m_dim, (1024, 512))
    if a_blk:
        nk, _, tk = a.shape
        a_spec = pl.BlockSpec((None, tm, tk), lambda i, k: (k, i, 0))
    else:
        tk = a.shape[1]
        nk = 1
        a_spec = pl.BlockSpec((tm, tk), lambda i, k: (i, 0))
    extra = [] if after is None else [after]
    vecs = [gate] + (list(norm) if norm else [])

    def body(a_ref, b_ref, x_ref, gate_ref, *rest):
        vec_refs = rest[:len(vecs) - 1]
        outs = rest[len(vecs) - 1 + len(extra):]
        acc_ref = outs[-1]
        k = pl.program_id(1)

        @pl.when(k == 0)
        def _():
            acc_ref[...] = jnp.zeros_like(acc_ref)

        acc_ref[...] += _dg(a_ref[...], b_ref[...], NN)

        @pl.when(k == nk - 1)
        def _():
            y = acc_ref[...]
            xn = x_ref[...] + (coef * gate_ref[...]) * y
            outs[0][...] = xn
            outs[1][...] = y.astype(outs[1].dtype)
            if norm:
                g_ref, sh_ref, sc_ref = vec_refs
                r = lax.rsqrt(jnp.mean(xn * xn, axis=-1, keepdims=True) + RMS_EPS)
                outs[2][...] = (((xn * r) * g_ref[...]) * (1.0 + sc_ref[...]) + sh_ref[...]).astype(outs[2].dtype)

    row = pl.BlockSpec((tm, D), lambda i, k: (i, 0))
    vec = pl.BlockSpec((1, D), lambda i, k: (0, 0))
    return pl.pallas_call(
        body, grid=(m_dim // tm, nk),
        in_specs=[a_spec, pl.BlockSpec((tk, D), lambda i, k: (k, 0)), row] + [vec] * len(vecs)
        + [pl.BlockSpec(memory_space=pl.ANY)] * len(extra),
        out_specs=[row] * (3 if norm else 2),
        out_shape=[SDS((m_dim, D), F32), SDS((m_dim, D), BF16)] + ([SDS((m_dim, D), BF16)] if norm else []),
        scratch_shapes=[pltpu.VMEM((tm, D), F32)],
        compiler_params=_params(("parallel", "arbitrary")), name=name,
    )(a, b, x, *vecs, *extra)


def _row(width, col=0):
    return pl.BlockSpec((TR, width), lambda i: (i, col))


def _vec(width):
    return pl.BlockSpec((1, width), lambda i: (0, 0))


def _norm_mod_fwd(x, g, shift, scale, name, after=None):
    t = x.shape[0]
    extra = [] if after is None else [after]

    def body(x_ref, g_ref, sh_ref, sc_ref, *rest):
        o_ref = rest[-1]
        xv = x_ref[...]
        r = lax.rsqrt(jnp.mean(xv * xv, axis=-1, keepdims=True) + RMS_EPS)
        o_ref[...] = (((xv * r) * g_ref[...]) * (1.0 + sc_ref[...]) + sh_ref[...]).astype(o_ref.dtype)

    return pl.pallas_call(
        body, grid=(t // TR,),
        in_specs=[_row(D), _vec(D), _vec(D), _vec(D)] + [pl.BlockSpec(memory_space=pl.ANY)] * len(extra),
        out_specs=_row(D),
        out_shape=SDS((t, D), BF16), compiler_params=_params(("parallel",)), name=name,
    )(x, g, shift, scale, *extra)


def _residual_branch_bwd(dxv, y_ref, gate_ref, coef, dy_ref, dgate_ref):
    dy_ref[...] = ((coef * gate_ref[...]) * dxv).astype(dy_ref.dtype)
    dgate_ref[...] += jnp.sum((coef * dxv) * y_ref[...], axis=0, keepdims=True)


def _norm_mod_bwd(x, g, scale, dh, dx_in, name, below=None):
    t = x.shape[0]
    lower = [] if below is None else list(below[:2])

    def body(x_ref, g_ref, sc_ref, dh_ref, dxi_ref, *rest):
        dx_ref, dsh_ref, dsc_ref, dg_ref = rest[len(lower):len(lower) + 4]

        @pl.when(pl.program_id(0) == 0)
        def _():
            for ref in rest[len(lower) + 1:]:
                if ref.shape[0] == 1:
                    ref[...] = jnp.zeros_like(ref)

        xv = x_ref[...]
        gv = g_ref[...]
        dh = dh_ref[...]
        r = lax.rsqrt(jnp.mean(xv * xv, axis=-1, keepdims=True) + RMS_EPS)
        n = xv * r
        dsh_ref[...] += jnp.sum(dh, axis=0, keepdims=True)
        dsc_ref[...] += jnp.sum(dh * (n * gv), axis=0, keepdims=True)
        tt = dh * (1.0 + sc_ref[...])
        dg_ref[...] += jnp.sum(tt * n, axis=0, keepdims=True)
        dn = tt * gv
        dxv = dxi_ref[...] + r * (dn - n * jnp.mean(dn * n, axis=-1, keepdims=True))
        dx_ref[...] = dxv
        if below is not None:
            _residual_branch_bwd(dxv, rest[0], rest[1], below[2], rest[-2], rest[-1])

    more_in = [] if below is None else [_row(D), _vec(D)]
    more_out = [] if below is None else [_row(D), _vec(D)]
    more_shape = [] if below is None else [SDS((t, D), BF16), SDS((1, D), F32)]
    return pl.pallas_call(
        body, grid=(t // TR,), in_specs=[_row(D), _vec(D), _vec(D), _row(D), _row(D)] + more_in,
        out_specs=[_row(D), _vec(D), _vec(D), _vec(D)] + more_out,
        out_shape=[SDS((t, D), F32), SDS((1, D), F32), SDS((1, D), F32), SDS((1, D), F32)] + more_shape,
        compiler_params=_params(("arbitrary",)), name=name,
    )(x, g, scale, dh, dx_in, *lower)


def _swiglu_up(h, w_in, name, after=None):
    t = h.shape[0]
    tm = _pick(t, (1024, 512, 256))
    half = NDEV // 2
    extra = [] if after is None else [after]

    def body(h_ref, wg_ref, wu_ref, *rest):
        u_ref, a_ref = rest[-2:]
        hv = h_ref[...]
        gate = _dg(hv, wg_ref[...], NT)
        up = _dg(hv, wu_ref[...], NT)
        u_ref[0] = gate.astype(u_ref.dtype)
        u_ref[1] = up.astype(u_ref.dtype)
        a_ref[...] = (_silu(gate) * up).astype(a_ref.dtype)

    return pl.pallas_call(
        body, grid=(t // tm, half),
        in_specs=[pl.BlockSpec((tm, D), lambda i, j: (i, 0)),
                  pl.BlockSpec((FB, D), lambda i, j: (j, 0)),
                  pl.BlockSpec((FB, D), lambda i, j: (j + half, 0))]
        + [pl.BlockSpec(memory_space=pl.ANY)] * len(extra),
        out_specs=[pl.BlockSpec((2, None, tm, FB), lambda i, j: (0, j, i, 0)),
                   pl.BlockSpec((None, tm, FB), lambda i, j: (j, i, 0))],
        out_shape=[SDS((2, half, t, FB), BF16), SDS((half, t, FB), BF16)],
        compiler_params=_params(("parallel", "parallel")), name=name,
    )(h, w_in, w_in, *extra)


def _swiglu_down_bwd(dy, w_out, u, name, after=None):
    t = dy.shape[0]
    tm = _pick(t, (1024, 512, 256))
    half = NDEV // 2
    extra = [] if after is None else [after]
    pair = pl.BlockSpec((2, None, tm, FB), lambda i, j: (0, j, i, 0))

    def body(dy_ref, w_ref, u_ref, *rest):
        o_ref = rest[-1]
        da = _dg(dy_ref[...], w_ref[...], NT)
        gate = u_ref[0].astype(F32)
        act, dact = _silu_pair(gate)
        o_ref[0] = (da * u_ref[1].astype(F32) * dact).astype(o_ref.dtype)
        o_ref[1] = (da * act).astype(o_ref.dtype)

    return pl.pallas_call(
        body, grid=(t // tm, half),
        in_specs=[pl.BlockSpec((tm, D), lambda i, j: (i, 0)), pl.BlockSpec((FB, D), lambda i, j: (j, 0)), pair]
        + [pl.BlockSpec(memory_space=pl.ANY)] * len(extra),
        out_specs=pair, out_shape=SDS((2, half, t, FB), BF16),
        compiler_params=_params(("parallel", "parallel")), name=name,
    )(dy, w_out, u, *extra)


def _final_loss(x, fg, target, below, name):
    t = x.shape[0]
    nt = t // TR

    def body(x_ref, g_ref, t_ref, y_ref, gate_ref, loss_ref, dx_ref, dg_ref, dy_ref, dgate_ref, acc_ref):
        i = pl.program_id(0)

        @pl.when(i == 0)
        def _():
            acc_ref[...] = jnp.zeros_like(acc_ref)
            dg_ref[...] = jnp.zeros_like(dg_ref)
            dgate_ref[...] = jnp.zeros_like(dgate_ref)

        xv = x_ref[...]
        gv = g_ref[...]
        r = lax.rsqrt(jnp.mean(xv * xv, axis=-1, keepdims=True) + RMS_EPS)
        n = xv * r
        err = n * gv - t_ref[...]
        acc_ref[...] += jnp.sum(err * err, axis=0, keepdims=True)
        dy = err * (1.0 / D)
        dg_ref[...] += jnp.sum(dy * n, axis=0, keepdims=True)
        dn = dy * gv
        dxv = r * (dn - n * jnp.mean(dn * n, axis=-1, keepdims=True))
        dx_ref[...] = dxv
        _residual_branch_bwd(dxv, y_ref, gate_ref, below[2], dy_ref, dgate_ref)

        @pl.when(i == nt - 1)
        def _():
            tot = jnp.sum(acc_ref[...], axis=1, keepdims=True) * (0.5 / D)
            loss_ref[...] = jnp.broadcast_to(tot, loss_ref.shape)

    return pl.pallas_call(
        body, grid=(nt,), in_specs=[_row(D), _vec(D), _row(D), _row(D), _vec(D)],
        out_specs=[_vec(128), _row(D), _vec(D), _row(D), _vec(D)],
        out_shape=[SDS((1, 128), F32), SDS((t, D), F32), SDS((1, D), F32), SDS((t, D), BF16), SDS((1, D), F32)],
        scratch_shapes=[pltpu.VMEM((1, D), F32)],
        compiler_params=_params(("arbitrary",)), name=name,
    )(x, fg, target, below[0], below[1])


def _halo_prev(width, col):
    per = TR // HALO
    return pl.BlockSpec((HALO, width), lambda i: (jnp.maximum(i * per - 1, 0), col))


def _halo_next(width, col, nt):
    per = TR // HALO
    return pl.BlockSpec((HALO, width), lambda i: (jnp.minimum((i + 1) * per, nt * per - 1), col))


def _pool_windows(ext, tile_index):
    rows = _iota((TR, PG), 0) + tile_index * TR + 1
    pooled, counts = [], []
    for gi in range(4):
        w = 2 << gi
        e = ext[:, gi * PG:(gi + 1) * PG]
        s = e
        step = 1
        while step < w:
            s = s + pltpu.roll(s, step, 0)
            step *= 2
        cnt = jnp.minimum(rows, w).astype(F32)
        pooled.append(s[HALO:] / cnt - e[HALO:])
        counts.append(cnt)
    return pooled, counts


def _pool_fwd(proj, pool_w, pool_scale, pool_proj, name):
    t = proj.shape[0]
    xcol = OFF_XP // PW

    def body(x_ref, h_ref, pw_ref, ps_ref, pp_ref, o_ref):
        i = pl.program_id(0)
        halo = jnp.where(i > 0, h_ref[...], 0.0)
        ext = jnp.concatenate([halo, x_ref[...]], axis=0)
        pooled, _ = _pool_windows(ext, i)
        mixed = [_dg(pooled[g].astype(BF16), pw_ref[g].astype(BF16), NN) for g in range(4)]
        ypre = jnp.concatenate(mixed, axis=1) * ps_ref[...]
        o_ref[...] = _dg(ypre.astype(BF16), pp_ref[...], NN)

    return pl.pallas_call(
        body, grid=(t // TR,),
        in_specs=[_row(PW, xcol), _halo_prev(PW, xcol),
                  pl.BlockSpec((4, PG, PG), lambda i: (0, 0, 0)), _vec(PW),
                  pl.BlockSpec((PW, D), lambda i: (0, 0))],
        out_specs=_row(D), out_shape=SDS((t, D), F32),
        compiler_params=_params(("parallel",)), name=name,
    )(proj, proj, pool_w, pool_scale, pool_proj)


def _pool_bwd_local(proj, pool_w, pool_scale, pool_proj, dya, name):
    t = proj.shape[0]
    xcol = OFF_XP // PW

    def body(x_ref, h_ref, pw_ref, ps_ref, pp_ref, dya_ref, dwin_ref, dpl_ref, dpw_ref, dps_ref, dpp_ref):
        i = pl.program_id(0)

        @pl.when(i == 0)
        def _():
            dpw_ref[...] = jnp.zeros_like(dpw_ref)
            dps_ref[...] = jnp.zeros_like(dps_ref)
            dpp_ref[...] = jnp.zeros_like(dpp_ref)

        halo = jnp.where(i > 0, h_ref[...], 0.0)
        ext = jnp.concatenate([halo, x_ref[...]], axis=0)
        pooled, counts = _pool_windows(ext, i)
        mixed = jnp.concatenate(
            [_dg(pooled[g].astype(BF16), pw_ref[g].astype(BF16), NN) for g in range(4)], axis=1)
        ps = ps_ref[...]
        ypre = mixed * ps
        dyab = dya_ref[...].astype(BF16)
        dypre = _dg(dyab, pp_ref[...], NT)
        dpp_ref[...] += _dg(ypre.astype(BF16), dyab, TN)
        dps_ref[...] += jnp.sum(dypre * mixed, axis=0, keepdims=True)
        dmixed = dypre * ps
        for g in range(4):
            dm = dmixed[:, g * PG:(g + 1) * PG].astype(BF16)
            dpw_ref[g] += _dg(pooled[g].astype(BF16), dm, TN)
            dpooled = _dg(dm, pw_ref[g].astype(BF16), NT)
            dwin_ref[:, g * PG:(g + 1) * PG] = dpooled / counts[g]
            dpl_ref[:, g * PG:(g + 1) * PG] = dpooled

    return pl.pallas_call(
        body, grid=(t // TR,),
        in_specs=[_row(PW, xcol), _halo_prev(PW, xcol),
                  pl.BlockSpec((4, PG, PG), lambda i: (0, 0, 0)), _vec(PW),
                  pl.BlockSpec((PW, D), lambda i: (0, 0)), _row(D)],
        out_specs=[_row(PW), _row(PW), pl.BlockSpec((4, PG, PG), lambda i: (0, 0, 0)), _vec(PW),
                   pl.BlockSpec((PW, D), lambda i: (0, 0))],
        out_shape=[SDS((t, PW), F32), SDS((t, PW), F32), SDS((4, PG, PG), F32), SDS((1, PW), F32),
                   SDS((PW, D), F32)],
        compiler_params=_params(("arbitrary",)), name=name,
    )(proj, proj, pool_w, pool_scale, pool_proj, dya)


def _pool_bwd_window(dwin, dpl, dproj, name):
    t = dwin.shape[0]
    nt = t // TR
    ext_rows = TR + HALO

    def body(dw_ref, h_ref, dp_ref, _, o_ref):
        i = pl.program_id(0)
        halo = jnp.where(i < nt - 1, h_ref[...], 0.0)
        ext = jnp.concatenate([dw_ref[...], halo], axis=0)
        for gi in range(4):
            w = 2 << gi
            s = ext[:, gi * PG:(gi + 1) * PG]
            step = 1
            while step < w:
                s = s + pltpu.roll(s, ext_rows - step, 0)
                step *= 2
            o_ref[:, gi * PG:(gi + 1) * PG] = (s[:TR] - dp_ref[:, gi * PG:(gi + 1) * PG]).astype(o_ref.dtype)

    return pl.pallas_call(
        body, grid=(nt,),
        in_specs=[_row(PW), _halo_next(PW, 0, nt), _row(PW), pl.BlockSpec(memory_space=pl.ANY)],
        out_specs=_into(PW, OFF_XP), out_shape=SDS(dproj.shape, dproj.dtype), input_output_aliases={3: 0},
        compiler_params=_params(("parallel",)), name=name,
    )(dwin, dwin, dpl, dproj)


def _conv_group(ext, cw_ref, cols):
    acc = cw_ref[3:4, cols] * ext
    for j in range(3):
        acc = acc + cw_ref[j:j + 1, cols] * pltpu.roll(ext, 3 - j, 0)
    return acc[HALO:]


def _gate_terms(raw, al, dt):
    beta = _sigmoid(raw)
    xg = raw + dt
    sp = jnp.maximum(xg, 0.0) + jnp.log(1.0 + jnp.exp(-jnp.abs(xg)))
    g = -jnp.exp(al) * sp
    return beta, g, _sigmoid(xg)


def _dn_pre_fwd(proj, conv_w, al_row, dt_row, name):
    t = proj.shape[0]

    def body(x_ref, h_ref, cw_ref, ba_ref, al_ref, dt_ref, q_ref, k_ref, v_ref, bg_ref):
        i = pl.program_id(0)
        keep = i > 0
        for grp in range(24):
            cols = slice(grp * HD, (grp + 1) * HD)
            ext = jnp.concatenate([jnp.where(keep, h_ref[:, cols], 0.0), x_ref[:, cols]], axis=0)
            s = _silu(_conv_group(ext, cw_ref, cols))
            seg, head = divmod(grp, NH)
            hc = slice(head * HD, (head + 1) * HD)
            if seg == 0:
                q_ref[:, hc] = s * lax.rsqrt(jnp.sum(s * s, axis=-1, keepdims=True) + L2_EPS) * (HD ** -0.5)
            elif seg == 1:
                k_ref[:, hc] = s * lax.rsqrt(jnp.sum(s * s, axis=-1, keepdims=True) + L2_EPS)
            else:
                v_ref[:, hc] = s
        lane = _iota((TR, 128), 1)
        rowc = _iota((TR, 128), 0) % CH
        beta, g, _ = _gate_terms(ba_ref[...], al_ref[...], dt_ref[...])
        step = 1
        while step < CH:
            g = g + jnp.where(rowc >= step, pltpu.roll(g, step, 0), 0.0)
            step *= 2
        bg_ref[...] = jnp.where(lane < NH, beta, jnp.where(lane < 2 * NH, g, 0.0))

    return pl.pallas_call(
        body, grid=(t // TR,),
        in_specs=[_row(3 * D, 0), _halo_prev(3 * D, 0), pl.BlockSpec((4, 3 * D), lambda i: (0, 0)),
                  _row(128, OFF_BA // 128), _vec(128), _vec(128)],
        out_specs=[_row(D), _row(D), _row(D), _row(128)],
        out_shape=[SDS((t, D), F32), SDS((t, D), F32), SDS((t, D), F32), SDS((t, 128), F32)],
        compiler_params=_params(("parallel",)), name=name,
    )(proj, proj, conv_w, proj, al_row, dt_row)


def _dn_pre_bwd_act(proj, conv_w, al_row, dt_row, dq, dk, dv, dbg, dproj, name):
    t = proj.shape[0]

    def body(x_ref, h_ref, cw_ref, ba_ref, al_ref, dt_ref, dq_ref, dk_ref, dv_ref, dbg_ref, _,
             dc_ref, draw_ref, dal_ref, ddt_ref):
        i = pl.program_id(0)

        @pl.when(i == 0)
        def _():
            dal_ref[...] = jnp.zeros_like(dal_ref)
            ddt_ref[...] = jnp.zeros_like(ddt_ref)

        keep = i > 0
        for grp in range(24):
            cols = slice(grp * HD, (grp + 1) * HD)
            ext = jnp.concatenate([jnp.where(keep, h_ref[:, cols], 0.0), x_ref[:, cols]], axis=0)
            cv = _conv_group(ext, cw_ref, cols)
            seg, head = divmod(grp, NH)
            hc = slice(head * HD, (head + 1) * HD)
            s, dact = _silu_pair(cv)
            if seg == 2:
                ds = dv_ref[:, hc]
            else:
                r = lax.rsqrt(jnp.sum(s * s, axis=-1, keepdims=True) + L2_EPS)
                dy = dq_ref[:, hc] if seg == 0 else dk_ref[:, hc]
                c = (HD ** -0.5) if seg == 0 else 1.0
                ds = (c * r) * (dy - s * ((r * r) * jnp.sum(dy * s, axis=-1, keepdims=True)))
            dc_ref[:, cols] = ds * dact
        lane = _iota((TR, 128), 1)
        rowc = _iota((TR, 128), 0) % CH
        isb = lane < NH
        isg = jnp.logical_and(lane >= NH, lane < 2 * NH)
        beta, g, sg = _gate_terms(ba_ref[...], al_ref[...], dt_ref[...])
        dbgv = dbg_ref[...]
        dg = dbgv
        step = 1
        while step < CH:
            dg = dg + jnp.where(rowc < CH - step, pltpu.roll(dg, TR - step, 0), 0.0)
            step *= 2
        da_raw = dg * (-jnp.exp(al_ref[...])) * sg
        draw = jnp.where(isb, dbgv * beta * (1.0 - beta), jnp.where(isg, da_raw, 0.0))
        draw_ref[:, :128] = draw.astype(draw_ref.dtype)
        draw_ref[:, 128:] = jnp.zeros((TR, MIXP - OFF_BA - 128), draw_ref.dtype)
        dal_ref[...] += jnp.sum(jnp.where(isg, dg * g, 0.0), axis=0, keepdims=True)
        ddt_ref[...] += jnp.sum(jnp.where(isg, da_raw, 0.0), axis=0, keepdims=True)

    return pl.pallas_call(
        body, grid=(t // TR,),
        in_specs=[_row(3 * D, 0), _halo_prev(3 * D, 0), pl.BlockSpec((4, 3 * D), lambda i: (0, 0)),
                  _row(128, OFF_BA // 128), _vec(128), _vec(128), _row(D), _row(D), _row(D), _row(128),
                  pl.BlockSpec(memory_space=pl.ANY)],
        out_specs=[_row(3 * D), _into(MIXP - OFF_BA, OFF_BA), _vec(128), _vec(128)],
        out_shape=[SDS((t, 3 * D), F32), SDS(dproj.shape, dproj.dtype), SDS((1, 128), F32), SDS((1, 128), F32)],
        input_output_aliases={10: 1},
        compiler_params=_params(("arbitrary",)), name=name,
    )(proj, proj, conv_w, proj, al_row, dt_row, dq, dk, dv, dbg, dproj)


def _dn_pre_bwd_conv(proj, conv_w, dconv, dproj, name):
    t = proj.shape[0]
    nt = t // TR
    ext_rows = TR + HALO

    def body(x_ref, h_ref, cw_ref, dc_ref, dn_ref, _, dx_ref, dcw_ref):
        i = pl.program_id(0)

        @pl.when(i == 0)
        def _():
            dcw_ref[...] = jnp.zeros_like(dcw_ref)

        keep_prev = i > 0
        keep_next = i < nt - 1
        for grp in range(24):
            cols = slice(grp * HD, (grp + 1) * HD)
            dct = dc_ref[:, cols]
            dext = jnp.concatenate([dct, jnp.where(keep_next, dn_ref[:, cols], 0.0)], axis=0)
            acc = cw_ref[3:4, cols] * dext
            for j in range(3):
                acc = acc + cw_ref[j:j + 1, cols] * pltpu.roll(dext, ext_rows - (3 - j), 0)
            dx_ref[:, cols] = acc[:TR].astype(dx_ref.dtype)
            xext = jnp.concatenate([jnp.where(keep_prev, h_ref[:, cols], 0.0), x_ref[:, cols]], axis=0)
            for j in range(4):
                xs = xext if j == 3 else pltpu.roll(xext, 3 - j, 0)
                dcw_ref[j:j + 1, cols] += jnp.sum(xs[HALO:] * dct, axis=0, keepdims=True)

    return pl.pallas_call(
        body, grid=(nt,),
        in_specs=[_row(3 * D, 0), _halo_prev(3 * D, 0), pl.BlockSpec((4, 3 * D), lambda i: (0, 0)),
                  _row(3 * D), _halo_next(3 * D, 0, nt), pl.BlockSpec(memory_space=pl.ANY)],
        out_specs=[_into(3 * D, OFF_Q), pl.BlockSpec((4, 3 * D), lambda i: (0, 0))],
        out_shape=[SDS(dproj.shape, dproj.dtype), SDS((4, 3 * D), F32)],
        input_output_aliases={5: 0},
        compiler_params=_params(("arbitrary",)), name=name,
    )(proj, proj, conv_w, dconv, dconv, dproj)


def _dn_post_fwd(o, proj, gn, name):
    t = o.shape[0]

    def body(o_ref, z_ref, g_ref, out_ref):
        gv = g_ref[...]
        for h in range(NH):
            hc = slice(h * HD, (h + 1) * HD)
            ov = o_ref[:, hc]
            r = lax.rsqrt(jnp.mean(ov * ov, axis=-1, keepdims=True) + RMS_EPS)
            out_ref[:, hc] = (((ov * r) * gv) * _silu(z_ref[:, hc])).astype(out_ref.dtype)

    return pl.pallas_call(
        body, grid=(t // TR,), in_specs=[_row(D), _row(D, OFF_Z // D), _vec(HD)], out_specs=_row(D),
        out_shape=SDS((t, D), BF16), compiler_params=_params(("parallel",)), name=name,
    )(o, proj, gn)


def _dn_post_bwd(o, proj, gn, dob, dproj, name):
    t = o.shape[0]

    def body(o_ref, z_ref, g_ref, d_ref, _, do_ref, dz_ref, dg_ref):
        @pl.when(pl.program_id(0) == 0)
        def _():
            dg_ref[...] = jnp.zeros_like(dg_ref)

        gv = g_ref[...]
        acc = jnp.zeros((1, HD), F32)
        for h in range(NH):
            hc = slice(h * HD, (h + 1) * HD)
            ov = o_ref[:, hc]
            zv = z_ref[:, hc]
            dv = d_ref[:, hc]
            r = lax.rsqrt(jnp.mean(ov * ov, axis=-1, keepdims=True) + RMS_EPS)
            n = ov * r
            act, dact = _silu_pair(zv)
            dz_ref[:, hc] = (dv * (n * gv) * dact).astype(dz_ref.dtype)
            dng = dv * act
            acc = acc + jnp.sum(dng * n, axis=0, keepdims=True)
            dn = dng * gv
            do_ref[:, hc] = r * (dn - n * jnp.mean(dn * n, axis=-1, keepdims=True))
        dg_ref[...] += acc

    return pl.pallas_call(
        body, grid=(t // TR,),
        in_specs=[_row(D), _row(D, OFF_Z // D), _vec(HD), _row(D), pl.BlockSpec(memory_space=pl.ANY)],
        out_specs=[_row(D), _into(D, OFF_Z), _vec(HD)],
        out_shape=[SDS((t, D), F32), SDS(dproj.shape, dproj.dtype), SDS((1, HD), F32)],
        input_output_aliases={4: 1},
        compiler_params=_params(("arbitrary",)), name=name,
    )(o, proj, gn, dob, dproj)


def _merge_fwd(ya, yb, proj, name):
    t = ya.shape[0]

    def body(a_ref, b_ref, gp_ref, gd_ref, o_ref):
        o_ref[...] = (_sigmoid(gp_ref[...]) * a_ref[...] + _sigmoid(gd_ref[...]) * b_ref[...]).astype(o_ref.dtype)

    return pl.pallas_call(
        body, grid=(t // TR,), in_specs=[_row(D), _row(D), _row(D, OFF_GP // D), _row(D, OFF_GD // D)],
        out_specs=_row(D), out_shape=SDS((t, D), BF16),
        compiler_params=_params(("parallel",)), name=name,
    )(ya, yb, proj, proj)


def _into(width, offset):
    assert offset % width == 0
    return pl.BlockSpec((TR, width), lambda i: (i, offset // width))


def _merge_bwd(dm, ya, yb, proj, dproj, name):
    t = ya.shape[0]

    def body(d_ref, a_ref, b_ref, gp_ref, gd_ref, _, da_ref, db_ref, dg_ref):
        dv = d_ref[...]
        sp = _sigmoid(gp_ref[...])
        sd = _sigmoid(gd_ref[...])
        da_ref[...] = (dv * sp).astype(da_ref.dtype)
        db_ref[...] = (dv * sd).astype(db_ref.dtype)
        dg_ref[:, :D] = (dv * a_ref[...] * sp * (1.0 - sp)).astype(dg_ref.dtype)
        dg_ref[:, D:] = (dv * b_ref[...] * sd * (1.0 - sd)).astype(dg_ref.dtype)

    return pl.pallas_call(
        body, grid=(t // TR,),
        in_specs=[_row(D), _row(D), _row(D), _row(D, OFF_GP // D), _row(D, OFF_GD // D),
                  pl.BlockSpec(memory_space=pl.ANY)],
        out_specs=[_row(D), _row(D), _into(2 * D, OFF_GP)],
        out_shape=[SDS((t, D), BF16), SDS((t, D), BF16), SDS(dproj.shape, dproj.dtype)],
        input_output_aliases={5: 2},
        compiler_params=_params(("parallel",)), name=name,
    )(dm, ya, yb, proj, proj, dproj)


def _split2(x):
    hi = x.astype(BF16)
    return hi, (x - hi.astype(F32)).astype(BF16)


def _dot3(a, b, dims):
    ah, al = _split2(a)
    bh, bl = _split2(b)
    return _dg(ah, bh, dims) + (_dg(ah, bl, dims) + _dg(al, bh, dims))


def _neumann_inverses(mats):
    ri = _iota((CH, CH), 0)
    ci = _iota((CH, CH), 1)
    eye = jnp.where(ri == ci, 1.0, 0.0).astype(F32)
    xs = [-a for a in mats]
    ps = [eye + x for x in xs]
    for _ in range(5):
        xs = [_dot3(x, x, NN) for x in xs]
        ps = [p + _dot3(p, x, NN) for p, x in zip(ps, xs)]
    return ps


def _solve_with(inv):
    @jax.custom_vjp
    def solve(a, rhs):
        return _dot3(inv, rhs, NN)

    def fwd(a, rhs):
        sol = _dot3(inv, rhs, NN)
        return sol, sol

    def bwd(sol, d):
        drhs = _dot3(inv, d, TN)
        return -_dot3(drhs, sol, NT), drhs

    solve.defvjp(fwd, bwd)
    return solve


@jax.custom_vjp
def _rows_to_lanes(g64):
    ri = _iota((CH, CH), 0)
    ci = _iota((CH, CH), 1)
    diag = jnp.where(ri == ci, g64, 0.0)
    ones = jnp.ones((CH, CH), BF16)
    hi = diag.astype(BF16)
    rem = diag - hi.astype(F32)
    mid = rem.astype(BF16)
    lo = (rem - mid.astype(F32)).astype(BF16)
    return _dg(ones, hi, NN) + (_dg(ones, mid, NN) + _dg(ones, lo, NN))


def _rows_to_lanes_bwd(_, d):
    ri = _iota((CH, CH), 0)
    ci = _iota((CH, CH), 1)
    return (jnp.where(ri == ci, jnp.broadcast_to(jnp.sum(d, axis=0, keepdims=True), (CH, CH)), 0.0),)


_rows_to_lanes.defvjp(lambda g64: (_rows_to_lanes(g64), None), _rows_to_lanes_bwd)


def _chunk_local(solve_all, q, k, v, g128, g64, gl128, b128, b64):
    ri = _iota((CH, CH), 0)
    ci = _iota((CH, CH), 1)
    causal = ri >= ci
    strict = ri > ci
    gj = [_rows_to_lanes(g) for g in g64]
    decay = [jnp.where(causal, jnp.exp(jnp.where(causal, g - t, 0.0)), 0.0) for g, t in zip(g64, gj)]
    kk = [_nt(x, x) for x in k]
    a = [jnp.where(strict, b * m * dc, 0.0) for b, m, dc in zip(b64, kk, decay)]
    eg = [jnp.exp(g) for g in g128]
    rhs = [jnp.concatenate([b * x, (b * e) * y], axis=1) for b, x, e, y in zip(b128, v, eg, k)]
    sol = solve_all(a, rhs)
    qk = [jnp.where(causal, _nt(x, y) * dc, 0.0) for x, y, dc in zip(q, k, decay)]
    return ([s[:, :HD] for s in sol], [s[:, HD:] for s in sol], qk, [x * e for x, e in zip(q, eg)],
            [x * jnp.exp(gl - g) for x, gl, g in zip(k, gl128, g128)], [jnp.exp(gl) for gl in gl128])


def _all_head_gates(bgv):
    return tuple(list(z) for z in zip(*[_head_gates(bgv, h) for h in range(NH)]))


def _head_gates(bgv, h):
    lane = _iota((CH, 128), 1)
    row = _iota((CH, 128), 0)
    bcol = jnp.sum(jnp.where(lane == h, bgv, 0.0), axis=1, keepdims=True)
    gcol = jnp.sum(jnp.where(lane == NH + h, bgv, 0.0), axis=1, keepdims=True)
    g128 = jnp.broadcast_to(gcol, (CH, 128))
    gl128 = jnp.broadcast_to(jnp.sum(jnp.where(row == CH - 1, g128, 0.0), axis=0, keepdims=True), (CH, 128))
    return (g128, jnp.broadcast_to(gcol, (CH, CH)), gl128,
            jnp.broadcast_to(bcol, (CH, 128)), jnp.broadcast_to(bcol, (CH, CH)))


def _chunk_specs():
    g = LOCAL_CHUNKS
    row = pl.BlockSpec((g * CH, D), lambda i: (i, 0))
    small = pl.BlockSpec((g * CH, 128), lambda i: (i, 0))
    qk = pl.BlockSpec((g * NH, CH, CH), lambda i: (i, 0, 0))
    eg = pl.BlockSpec((g, NH, 128), lambda i: (i, 0, 0))
    return row, small, qk, eg


def _chunk_heads():
    return [(slice(c * CH, (c + 1) * CH), slice(h * HD, (h + 1) * HD), c, h)
            for c in range(LOCAL_CHUNKS) for h in range(NH)]


def _all_gates(bg_ref):
    per_chunk = [_all_head_gates(bg_ref[c * CH:(c + 1) * CH, :]) for c in range(LOCAL_CHUNKS)]
    return tuple(sum((list(pc[j]) for pc in per_chunk), []) for j in range(5))


def _dn_local_fwd(q, k, v, bg, name):
    t = q.shape[0]
    n = t // CH
    pairs = _chunk_heads()

    def body(q_ref, k_ref, v_ref, bg_ref, u_ref, w_ref, qk_ref, qd_ref, kd_ref, eg_ref, inv_ref):
        def solve_all(mats, rhs):
            invs = _neumann_inverses(mats)
            for p in range(len(pairs)):
                inv_ref[p] = invs[p]
            return [_dot3(m, r, NN) for m, r in zip(invs, rhs)]

        u, w, qk, qd, kd, egl = _chunk_local(
            solve_all, [q_ref[r, hc] for r, hc, _, _ in pairs], [k_ref[r, hc] for r, hc, _, _ in pairs],
            [v_ref[r, hc] for r, hc, _, _ in pairs], *_all_gates(bg_ref))
        for p, (r, hc, c, h) in enumerate(pairs):
            u_ref[r, hc] = u[p]
            w_ref[r, hc] = w[p].astype(w_ref.dtype)
            qd_ref[r, hc] = qd[p].astype(qd_ref.dtype)
            kd_ref[r, hc] = kd[p].astype(kd_ref.dtype)
            qk_ref[p] = qk[p].astype(qk_ref.dtype)
            eg_ref[c, h:h + 1, :] = egl[p][0:1, :]

    row, small, qkb, egb = _chunk_specs()
    return pl.pallas_call(
        body, grid=(n // LOCAL_CHUNKS,), in_specs=[row, row, row, small],
        out_specs=[row, row, qkb, row, row, egb, qkb],
        out_shape=[SDS((t, D), F32), SDS((t, D), BF16), SDS((n * NH, CH, CH), BF16), SDS((t, D), BF16),
                   SDS((t, D), BF16), SDS((n, NH, 128), F32), SDS((n * NH, CH, CH), F32)],
        compiler_params=_params(("parallel",)), name=name,
    )(q, k, v, bg)


def _dn_local_bwd(q, k, v, bg, inv, du, dw, dqk, dqd, dkd, deg, name):
    t = q.shape[0]
    n = t // CH
    pairs = _chunk_heads()

    def body(q_ref, k_ref, v_ref, bg_ref, inv_ref, du_ref, dw_ref, dqk_ref, dqd_ref, dkd_ref, deg_ref,
             dq_ref, dk_ref, dv_ref, dbg_ref):
        lane = _iota((CH, 128), 1)
        row = _iota((CH, 128), 0)
        first = jnp.where(row == 0, 1.0, 0.0)
        solves = [_solve_with(inv_ref[p]) for p in range(len(pairs))]

        def solve_all(mats, rhs):
            return [f(m, r) for f, m, r in zip(solves, mats, rhs)]

        _, vjp = jax.vjp(functools.partial(_chunk_local, solve_all),
                         [q_ref[r, hc] for r, hc, _, _ in pairs], [k_ref[r, hc] for r, hc, _, _ in pairs],
                         [v_ref[r, hc] for r, hc, _, _ in pairs], *_all_gates(bg_ref))
        cts = ([du_ref[r, hc].astype(F32) for r, hc, _, _ in pairs],
               [dw_ref[r, hc].astype(F32) for r, hc, _, _ in pairs],
               [dqk_ref[p] for p in range(len(pairs))],
               [dqd_ref[r, hc].astype(F32) for r, hc, _, _ in pairs],
               [dkd_ref[r, hc].astype(F32) for r, hc, _, _ in pairs],
               [jnp.broadcast_to(deg_ref[c, h:h + 1, :], (CH, 128)) * first for _, _, c, h in pairs])
        dq, dk, dv, dg128, dg64, dgl, db128, db64 = vjp(cts)
        acc = [jnp.zeros((CH, 128), F32) for _ in range(LOCAL_CHUNKS)]
        for p, (r, hc, c, h) in enumerate(pairs):
            dq_ref[r, hc] = dq[p]
            dk_ref[r, hc] = dk[p]
            dv_ref[r, hc] = dv[p]
            dg = jnp.sum(dg128[p], axis=1, keepdims=True) + jnp.sum(dg64[p], axis=1, keepdims=True)
            tot = jnp.sum(jnp.sum(dgl[p], axis=0, keepdims=True), axis=1, keepdims=True)
            dg = dg + jnp.where(row[:, 0:1] == CH - 1, tot, 0.0)
            db = jnp.sum(db128[p], axis=1, keepdims=True) + jnp.sum(db64[p], axis=1, keepdims=True)
            acc[c] = acc[c] + jnp.where(lane == h, db, 0.0) + jnp.where(lane == NH + h, dg, 0.0)
        for c in range(LOCAL_CHUNKS):
            dbg_ref[c * CH:(c + 1) * CH, :] = acc[c]

    row, small, qkb, egb = _chunk_specs()
    return pl.pallas_call(
        body, grid=(n // LOCAL_CHUNKS,), in_specs=[row, row, row, small, qkb, row, row, qkb, row, row, egb],
        out_specs=[row, row, row, small],
        out_shape=[SDS((t, D), F32)] * 3 + [SDS((t, 128), F32)],
        compiler_params=_params(("parallel",)), name=name,
    )(q, k, v, bg, inv, du, dw, dqk, dqd, dkd, deg)


def _state_step(s, u, w, qk, qd, kd, egl):
    ws = [_nn(a, b) for a, b in zip(w, s)]
    v_new = [a - b for a, b in zip(u, ws)]
    qs = [_nn(a, b) for a, b in zip(qd, s)]
    intra = [_nn(a, b) for a, b in zip(qk, v_new)]
    upd = [_tn(a, b) for a, b in zip(kd, v_new)]
    return [a * e + b for a, e, b in zip(s, egl, upd)], [a + b for a, b in zip(qs, intra)]


def _dn_scan_fwd(u, w, qk, qd, kd, eg, name):
    t = u.shape[0]
    n = t // CH
    g = SCAN_CHUNKS

    def body(u_ref, w_ref, qk_ref, qd_ref, kd_ref, eg_ref, o_ref, save_ref, s_ref):
        @pl.when(pl.program_id(0) == 0)
        def _():
            s_ref[...] = jnp.zeros_like(s_ref)

        cols = [slice(h * HD, (h + 1) * HD) for h in range(NH)]
        s = [s_ref[h] for h in range(NH)]
        for c in range(g):
            rows = slice(c * CH, (c + 1) * CH)
            for h in range(NH):
                save_ref[c, h] = s[h].astype(save_ref.dtype)
            s, o = _state_step(
                s, [u_ref[rows, hc] for hc in cols], [w_ref[rows, hc].astype(F32) for hc in cols],
                [qk_ref[c * NH + h].astype(F32) for h in range(NH)], [qd_ref[rows, hc].astype(F32) for hc in cols],
                [kd_ref[rows, hc].astype(F32) for hc in cols], [eg_ref[c, h:h + 1, :] for h in range(NH)])
            for h, hc in enumerate(cols):
                o_ref[rows, hc] = o[h]
        for h in range(NH):
            s_ref[h] = s[h]

    row = pl.BlockSpec((g * CH, D), lambda i: (i, 0))
    qkb = pl.BlockSpec((g * NH, CH, CH), lambda i: (i, 0, 0))
    egb = pl.BlockSpec((g, NH, 128), lambda i: (i, 0, 0))
    return pl.pallas_call(
        body, grid=(n // g,), in_specs=[row, row, qkb, row, row, egb],
        out_specs=[row, pl.BlockSpec((g, NH, HD, HD), lambda i: (i, 0, 0, 0))],
        out_shape=[SDS((t, D), F32), SDS((n, NH, HD, HD), BF16)],
        scratch_shapes=[pltpu.VMEM((NH, HD, HD), F32)],
        compiler_params=_params(("arbitrary",)), name=name,
    )(u, w, qk, qd, kd, eg)


def _dn_scan_bwd(u, w, qk, qd, kd, eg, saved, do, name):
    t = u.shape[0]
    n = t // CH
    g = SCAN_CHUNKS
    last = n // g - 1

    def body(u_ref, w_ref, qk_ref, qd_ref, kd_ref, eg_ref, sv_ref, do_ref,
             du_ref, dw_ref, dqk_ref, dqd_ref, dkd_ref, deg_ref, ds_ref):
        @pl.when(pl.program_id(0) == 0)
        def _():
            ds_ref[...] = jnp.zeros_like(ds_ref)

        cols = [slice(h * HD, (h + 1) * HD) for h in range(NH)]
        ds = [ds_ref[h] for h in range(NH)]
        for c in reversed(range(g)):
            rows = slice(c * CH, (c + 1) * CH)
            _, vjp = jax.vjp(
                _state_step, [sv_ref[c, h].astype(F32) for h in range(NH)], [u_ref[rows, hc] for hc in cols],
                [w_ref[rows, hc].astype(F32) for hc in cols], [qk_ref[c * NH + h].astype(F32) for h in range(NH)],
                [qd_ref[rows, hc].astype(F32) for hc in cols], [kd_ref[rows, hc].astype(F32) for hc in cols],
                [eg_ref[c, h:h + 1, :] for h in range(NH)])
            ds, du, dw, dqk, dqd, dkd, deg = vjp((ds, [do_ref[rows, hc] for hc in cols]))
            for h, hc in enumerate(cols):
                du_ref[rows, hc] = du[h].astype(du_ref.dtype)
                dw_ref[rows, hc] = dw[h].astype(dw_ref.dtype)
                dqk_ref[c * NH + h] = dqk[h]
                dqd_ref[rows, hc] = dqd[h].astype(dqd_ref.dtype)
                dkd_ref[rows, hc] = dkd[h].astype(dkd_ref.dtype)
                deg_ref[c, h:h + 1, :] = deg[h]
        for h in range(NH):
            ds_ref[h] = ds[h]

    row = pl.BlockSpec((g * CH, D), lambda i: (last - i, 0))
    qkb = pl.BlockSpec((g * NH, CH, CH), lambda i: (last - i, 0, 0))
    egb = pl.BlockSpec((g, NH, 128), lambda i: (last - i, 0, 0))
    return pl.pallas_call(
        body, grid=(n // g,),
        in_specs=[row, row, qkb, row, row, egb,
                  pl.BlockSpec((g, NH, HD, HD), lambda i: (last - i, 0, 0, 0)), row],
        out_specs=[row, row, qkb, row, row, egb],
        out_shape=[SDS((t, D), BF16), SDS((t, D), BF16), SDS((n * NH, CH, CH), F32), SDS((t, D), BF16),
                   SDS((t, D), BF16), SDS((n, NH, 128), F32)],
        scratch_shapes=[pltpu.VMEM((NH, HD, HD), F32)],
        compiler_params=_params(("arbitrary",)), name=name,
    )(u, w, qk, qd, kd, eg, saved, do)


def _ada_fwd(c_all, ada_w, ada_b, name):
    ncol = ada_w.shape[1]

    def body(c_ref, w_ref, b_ref, o_ref):
        o_ref[...] = _dg(_silu(c_ref[...]), w_ref[...], NN, HI) + b_ref[...]

    return pl.pallas_call(body, out_shape=SDS((NDEV, ncol), F32),
                          compiler_params=pltpu.CompilerParams(vmem_limit_bytes=VMEM_LIMIT), name=name,
                          )(c_all, ada_w, ada_b)


def _ada_bwd(c_all_t, dmod, name):
    ncol = dmod.shape[1]

    def body(c_ref, d_ref, o_ref):
        sc = _silu(c_ref[...])
        acc = sc[:, 0:1] * d_ref[0:1, :]
        for b in range(1, NDEV):
            acc = acc + sc[:, b:b + 1] * d_ref[b:b + 1, :]
        o_ref[...] = acc

    return pl.pallas_call(body, out_shape=SDS((D, ncol), F32),
                          compiler_params=pltpu.CompilerParams(vmem_limit_bytes=VMEM_LIMIT), name=name,
                          )(c_all_t, dmod)


def _sum_devices(parts, out_dtype, name):
    _, r, c = parts.shape
    tr = TR if r % TR == 0 else r

    def body(p_ref, o_ref):
        acc = p_ref[0].astype(F32)
        for i in range(1, NDEV):
            acc = acc + p_ref[i].astype(F32)
        o_ref[...] = acc.astype(o_ref.dtype)

    return pl.pallas_call(
        body, grid=(r // tr,), in_specs=[pl.BlockSpec((NDEV, tr, c), lambda i: (0, i, 0))],
        out_specs=pl.BlockSpec((tr, c), lambda i: (i, 0)), out_shape=SDS((r, c), out_dtype),
        compiler_params=_params(("parallel",)), name=name,
    )(parts)


def _adam_tiles(r, c):
    if r % 8 == 0:
        return _pick(r, (256, 352, 128, 8)), c
    return r, (256 if c % 256 == 0 else c)


def _adam_math(w, gv, m, v):
    m_new = ADAM_B1 * m + (1.0 - ADAM_B1) * gv
    v_new = ADAM_B2 * v + (1.0 - ADAM_B2) * (gv * gv)
    bc1 = 1.0 - ADAM_B1 ** ADAM_STEP
    bc2 = 1.0 - ADAM_B2 ** ADAM_STEP
    return -ADAM_LR * ((m_new / bc1) / (jnp.sqrt(v_new / bc2) + ADAM_EPS) + ADAM_WD * w), m_new, v_new


def _adamw(w, g, m, v, name):
    r, c = w.shape
    tr, tc = _adam_tiles(r, c)

    def body(w_ref, g_ref, m_ref, v_ref, d_ref, nm_ref, nv_ref):
        d_ref[...], nm_ref[...], nv_ref[...] = _adam_math(w_ref[...], g_ref[...], m_ref[...], v_ref[...])

    spec = pl.BlockSpec((tr, tc), lambda i, j: (i, j))
    return pl.pallas_call(
        body, grid=(r // tr, c // tc), in_specs=[spec] * 4, out_specs=[spec] * 3,
        out_shape=[SDS((r, c), F32)] * 3, compiler_params=_params(("parallel", "parallel")), name=name,
    )(w, g, m, v)


def _reduce_adamw(parts, w, m, v, name):
    r, c = w.shape
    tr, tc = _adam_tiles(r, c)

    def body(p_ref, w_ref, m_ref, v_ref, g_ref, d_ref, nm_ref, nv_ref):
        gv = p_ref[0].astype(F32)
        for i in range(1, NDEV):
            gv = gv + p_ref[i].astype(F32)
        g_ref[...] = gv
        d_ref[...], nm_ref[...], nv_ref[...] = _adam_math(w_ref[...], gv, m_ref[...], v_ref[...])

    spec = pl.BlockSpec((tr, tc), lambda i, j: (i, j))
    return pl.pallas_call(
        body, grid=(r // tr, c // tc),
        in_specs=[pl.BlockSpec((NDEV, tr, tc), lambda i, j: (0, i, j))] + [spec] * 3, out_specs=[spec] * 4,
        out_shape=[SDS((r, c), F32)] * 4, compiler_params=_params(("parallel", "parallel")), name=name,
    )(parts, w, m, v)


ANY = pl.BlockSpec(memory_space=pl.ANY)
MESH = pl.DeviceIdType.MESH


def _all_gather(xs, name, after=None):
    n = len(xs)
    extra = [] if after is None else [after]

    def body(*refs):
        x_refs, out_refs = refs[:n], refs[n + len(extra):2 * n + len(extra)]
        send_sems, recv_sems, local_sems = refs[-3:]
        mx, my, mc = lax.axis_index("x"), lax.axis_index("y"), lax.axis_index("c")
        me, sibling = (mx, my, mc), (mx, my, 1 - mc)
        chips = [(1 - mx, my), (mx, 1 - my), (1 - mx, 1 - my)]

        def rows(a, px, py, pc):
            return out_refs[a].at[4 * px + 2 * py + pc]

        def copy(a, k, block, to, src=None):
            return pltpu.make_async_remote_copy(
                src_ref=rows(a, *block) if src is None else src, dst_ref=rows(a, *block),
                send_sem=send_sems.at[a, k], recv_sem=recv_sems.at[a, k], device_id=to, device_id_type=MESH)

        mine = [pltpu.make_async_copy(x_refs[a], rows(a, *me), local_sems.at[a]) for a in range(n)]
        for cp in mine:
            cp.start()
        first = []
        for a in range(n):
            first.append(copy(a, 0, me, sibling, src=x_refs[a]))
            first += [copy(a, 1 + j, me, (*chip, mc), src=x_refs[a]) for j, chip in enumerate(chips)]
        for cp in first:
            cp.start()
        passed = []
        for a in range(n):
            for j, chip in enumerate(chips):
                copy(a, 1 + j, (*chip, mc), me).wait_recv()
                passed.append(copy(a, 4 + j, (*chip, mc), sibling))
                passed[-1].start()
        for a in range(n):
            copy(a, 0, sibling, me).wait_recv()
            for j, chip in enumerate(chips):
                copy(a, 4 + j, (*chip, 1 - mc), me).wait_recv()
        for cp in first + passed:
            cp.wait_send()
        for cp in mine:
            cp.wait()

    return pl.pallas_call(
        body, out_shape=[SDS((NDEV,) + x.shape, x.dtype) for x in xs], in_specs=[ANY] * (n + len(extra)),
        out_specs=[ANY] * n,
        scratch_shapes=[pltpu.SemaphoreType.DMA((n, 7)), pltpu.SemaphoreType.DMA((n, 7)),
                        pltpu.SemaphoreType.DMA((n,))],
        name=name,
    )(*xs, *extra)


HBM = pl.BlockSpec(memory_space=pltpu.HBM)
SEM = pl.BlockSpec(memory_space=pltpu.SEMAPHORE)
EFFECT = pltpu.SideEffectType.DATAFLOW_SIDE_EFFECTING


def _peers():
    mx, my, mc = lax.axis_index("x"), lax.axis_index("y"), lax.axis_index("c")
    out = []
    for k in range(1, NDEV):
        out.append((1 - mx if k & 4 else mx, 1 - my if k & 2 else my, 1 - mc if k & 1 else mc))
    return 4 * mx + 2 * my + mc, out


NEAR = (0, 1, 3, 5)


def _push_start(srcs, sliced, name, after=None, near=()):
    n = len(srcs)
    extra = [] if after is None else [after]
    lands = [lax.empty(s.shape if sliced else (NDEV,) + s.shape, s.dtype) for s in srcs]

    def body(*refs):
        src_refs, land_refs = refs[:n], refs[n:2 * n]
        outs = refs[2 * n + len(extra):]
        send_sems, recv_sems = outs[:n], outs[n:2 * n]
        token = refs[-1]
        me, peers = _peers()
        for a in range(n):
            for k, (px, py, pc) in enumerate(peers):
                if a in near and k not in NEAR:
                    continue
                src = src_refs[a].at[4 * px + 2 * py + pc] if sliced else src_refs[a]
                pltpu.make_async_remote_copy(
                    src_ref=src, dst_ref=land_refs[a].at[me], send_sem=send_sems[a].at[k],
                    recv_sem=recv_sems[a].at[k], device_id=(px, py, pc), device_id_type=MESH).start()
            pltpu.make_async_copy(src_refs[a].at[me] if sliced else src_refs[a], land_refs[a].at[me],
                                  send_sems[a].at[NDEV - 1]).start()
        token[...] = jnp.zeros_like(token)

    outs = pl.pallas_call(
        body, name=name,
        out_shape=([pltpu.SemaphoreType.DMA((NDEV,))] * n + [pltpu.SemaphoreType.DMA((NDEV - 1,))] * n
                   + [pltpu.HBM(s.shape, s.dtype) for s in srcs] + [pltpu.HBM(l.shape, l.dtype) for l in lands]
                   + [SDS((8, 128), F32)]),
        in_specs=[HBM] * (2 * n) + [pl.BlockSpec(memory_space=pl.ANY)] * len(extra),
        out_specs=[SEM] * (2 * n) + [HBM] * (2 * n) + [pl.BlockSpec(memory_space=pltpu.VMEM)],
        input_output_aliases={i: 2 * n + i for i in range(2 * n)},
        compiler_params=pltpu.CompilerParams(has_side_effects=EFFECT),
    )(*[pltpu.with_memory_space_constraint(s, pltpu.HBM) for s in srcs],
      *[pltpu.with_memory_space_constraint(l, pltpu.HBM) for l in lands], *extra)
    sends, recvs = outs[:n], outs[n:2 * n]
    src_thru, land_thru = outs[2 * n:3 * n], outs[3 * n:4 * n]
    return [(sends[a], recvs[a], src_thru[a], land_thru[a]) for a in range(n)], outs[-1]


def _push_wait(started, sliced, after, name, near=()):
    n = len(started)
    afters = list(after) if isinstance(after, (list, tuple)) else [after]

    def body(*refs):
        src_refs, land_refs = refs[:n], refs[n:2 * n]
        send_sems, recv_sems = refs[2 * n:3 * n], refs[3 * n:4 * n]
        me, peers = _peers()
        for a in range(n):
            for k, (px, py, pc) in enumerate(peers):
                if a in near and k not in NEAR:
                    continue
                src = src_refs[a].at[4 * px + 2 * py + pc] if sliced else src_refs[a]
                cp = pltpu.make_async_remote_copy(
                    src_ref=src, dst_ref=land_refs[a].at[me], send_sem=send_sems[a].at[k],
                    recv_sem=recv_sems[a].at[k], device_id=(px, py, pc), device_id_type=MESH)
                cp.wait_send()
                cp.wait_recv()
            pltpu.make_async_copy(src_refs[a].at[me] if sliced else src_refs[a], land_refs[a].at[me],
                                  send_sems[a].at[NDEV - 1]).wait()

    srcs = [s[2] for s in started]
    lands = [s[3] for s in started]
    outs = pl.pallas_call(
        body, name=name,
        out_shape=[pltpu.HBM(s.shape, s.dtype) for s in srcs] + [pltpu.HBM(l.shape, l.dtype) for l in lands],
        in_specs=[HBM] * (2 * n) + [SEM] * (2 * n) + [pl.BlockSpec(memory_space=pl.ANY)] * len(afters),
        out_specs=[HBM] * (2 * n),
        input_output_aliases={i: i for i in range(2 * n)},
        compiler_params=pltpu.CompilerParams(has_side_effects=EFFECT),
    )(*srcs, *lands, *[s[0] for s in started], *[s[1] for s in started], *afters)
    return outs[n:]


def _relay_to_sibling(land, name):
    def body(_, land_ref, send_sems, recv_sems):
        mx, my, mc = lax.axis_index("x"), lax.axis_index("y"), lax.axis_index("c")
        chips = [(1 - mx, my), (mx, 1 - my), (1 - mx, 1 - my)]

        def copy(j, core):
            slot = land_ref.at[4 * chips[j][0] + 2 * chips[j][1] + core]
            return pltpu.make_async_remote_copy(
                src_ref=slot, dst_ref=slot, send_sem=send_sems.at[j], recv_sem=recv_sems.at[j],
                device_id=(mx, my, 1 - mc), device_id_type=MESH)

        mine = [copy(j, mc) for j in range(3)]
        for cp in mine:
            cp.start()
        for j in range(3):
            copy(j, 1 - mc).wait_recv()
        for cp in mine:
            cp.wait_send()

    return pl.pallas_call(
        body, out_shape=SDS(land.shape, land.dtype), in_specs=[ANY], out_specs=ANY, input_output_aliases={0: 0},
        scratch_shapes=[pltpu.SemaphoreType.DMA((3,)), pltpu.SemaphoreType.DMA((3,))], name=name,
    )(land)


def _cols_from_blocks(blocks):
    _, rows, w = blocks.shape
    return blocks.transpose(1, 0, 2).reshape(rows, NDEV * w)


def _cols_to_blocks(full):
    rows, total = full.shape
    return full.reshape(rows, NDEV, total // NDEV).transpose(1, 0, 2)


def _mix_pad(wt):
    xp, qkvz, ba, gates = jnp.split(wt, (PW, PW + 4 * D, PW + 4 * D + 16), axis=0)
    pad = jnp.zeros((MIXP - OFF_BA - 16, wt.shape[1]), wt.dtype)
    return jnp.concatenate([qkvz, gates, xp, ba, pad], axis=0)


def _mix_unpad(wt):
    return jnp.concatenate([wt[OFF_XP:OFF_BA], wt[OFF_Q:OFF_GP], wt[OFF_BA:OFF_BA + 16], wt[OFF_GP:OFF_XP]], axis=0)


def _lane_row(vec8):
    return jnp.zeros((1, 128), F32).at[0, NH:2 * NH].set(vec8)


def _ffn_fwd(x, h, gate, w_in, w_out, tag, next_norm=None, token=None, start_more=None):
    if isinstance(w_in, tuple):
        w_in, = _push_wait([w_in], False, h, f"{tag}_gather_wait_in")
    w_in = w_in.reshape(2 * FH, D)
    u, a = _swiglu_up(h, w_in, f"{tag}_up", after=token)
    w_out, = _push_wait([w_out], False, a, f"{tag}_gather_wait_out")
    w_out = w_out.reshape(FH, D)
    outs = _matmul_residual(a, w_out, x, gate, 0.5, a_blk=True, norm=next_norm, name=f"{tag}_down",
                            after=None if start_more is None else start_more(h))
    return outs[0], (h, u, a, outs[1]), w_in, w_out, (outs[2] if next_norm else None)


def _ffn_bwd(dx_out, dy, x, g, scale, w_in, w_out, saved, tag, below=None):
    h, u, a, _ = saved
    t = x.shape[0]
    dw_out = _matmul(a, dy, ta=True, a_blk=True, out_dtype=BF16, name=f"{tag}_down_dw")
    sent_out, token = _push_start([dw_out.reshape(NDEV, FH // NDEV, D)], True, f"{tag}_grad_start_out")
    du = _swiglu_down_bwd(dy, w_out, u, f"{tag}_down_dx", after=token).reshape(NDEV, t, FB)
    dw_in = _matmul(du, h, ta=True, a_blk=True, out_dtype=BF16, name=f"{tag}_up_dw")
    sent_in, token = _push_start([dw_in.reshape(NDEV, FB, D)], True, f"{tag}_grad_start_in")
    dh = _matmul(du, w_in, a_blk=True, out_dtype=F32, name=f"{tag}_up_dx", after=token)
    return _norm_mod_bwd(x, g, scale, dh, dx_out, f"{tag}_norm_bwd", below), sent_in + sent_out


def kernel(x, c, ada_w, ada_b, norm_g, ffn1_w_in, ffn1_w_out, ffn2_w_in, ffn2_w_out, mix_w_in, conv_w, a_log, dt_bias, dn_norm_g, pool_w, pool_scale, pool_proj, dn_proj, mix_w_out, final_g, loss_target, m_ada_w, m_ada_b, m_norm_g, m_ffn1_w_in, m_ffn1_w_out, m_ffn2_w_in, m_ffn2_w_out, m_mix_w_in, m_conv_w, m_a_log, m_dt_bias, m_dn_norm_g, m_pool_w, m_pool_scale, m_pool_proj, m_dn_proj, m_mix_w_out, m_final_g, v_ada_w, v_ada_b, v_norm_g, v_ffn1_w_in, v_ffn1_w_out, v_ffn2_w_in, v_ffn2_w_out, v_mix_w_in, v_conv_w, v_a_log, v_dt_bias, v_dn_norm_g, v_pool_w, v_pool_scale, v_pool_proj, v_dn_proj, v_mix_w_out, v_final_g):
    me = 4 * lax.axis_index("x") + 2 * lax.axis_index("y") + lax.axis_index("c")
    x0 = x[0]
    target = loss_target[0]
    t = x0.shape[0]

    big = [ffn1_w_in[0], ffn1_w_out[0], ffn2_w_in[0], ffn2_w_out[0], mix_w_in[0], pool_proj[0], dn_proj[0],
           mix_w_out[0]]
    small = jnp.concatenate([c.reshape(8, 128), conv_w[0].reshape(12, 128), norm_g[0].reshape(3, 128),
                             jnp.zeros((1, 128), F32)], axis=0)
    small_all, = _all_gather([small], "gather_small")
    c_all = small_all[:, 0:8, :].reshape(NDEV, D)
    conv_full = small_all[:, 8:20, :].reshape(NDEV, 4, 384).transpose(1, 0, 2).reshape(4, 3 * D)
    norm_full = small_all[:, 20:23, :].reshape(NDEV, 3, 128).transpose(1, 0, 2).reshape(3, D)

    ncol = ada_w.shape[2]
    ada_b_mine = lax.dynamic_slice(ada_b, (0, me * ncol), (1, ncol))
    mod_cols = _ada_fwd(c_all, ada_w[0], ada_b_mine, "ada_fwd")
    transposed = (0, 2, 4)
    payload = [(w.T if i in transposed else w).astype(BF16) for i, w in enumerate(big)]
    mod_all, w_in1 = _all_gather([mod_cols, payload[0]], "gather_mod_first_weight")
    started, token = _push_start([payload[1], payload[4]], False, "gather_start", after=mod_all, near=(1,))
    started = {1: started[0], 4: started[1]}

    def start_rest(h):
        more, token = _push_start([payload[i] for i in (5, 6, 7, 2, 3)], False, "gather_start_rest", after=h)
        started.update(zip((5, 6, 7, 2, 3), more))
        return token

    mod = lax.dynamic_index_in_dim(mod_all, me, axis=1, keepdims=False).reshape(9, D)
    shift = [mod[3 * s:3 * s + 1] for s in range(3)]
    scale = [mod[3 * s + 1:3 * s + 2] for s in range(3)]
    gate = [mod[3 * s + 2:3 * s + 3] for s in range(3)]
    ng = [norm_full[s:s + 1] for s in range(3)]
    fg = final_g.reshape(1, D)
    al_row = _lane_row(a_log[0])
    dt_row = _lane_row(dt_bias[0])
    gn = dn_norm_g
    pw = pool_w[0]
    ps = pool_scale

    h0 = _norm_mod_fwd(x0, ng[0], shift[0], scale[0], "ffn1_norm", after=token)
    x1, saved1, w_in1, w_out1, h1 = _ffn_fwd(x0, h0, gate[0], w_in1, started[1], "ffn1",
                                             (ng[1], shift[1], scale[1]), token, start_rest)

    seg, = _push_wait([started[4]], False, h1, "mix_gather_wait", near=(0,))
    w_mix = _mix_pad(_relay_to_sibling(seg, "mix_gather_relay").reshape(MIX_RAW, D))
    proj = _matmul(h1, w_mix, tb=True, out_dtype=F32, name="mix_in")
    qh, kh, vh, bg = _dn_pre_fwd(proj, conv_full, al_row, dt_row, "dn_pre")
    seg = _push_wait([started[i] for i in (5, 6, 7)], False, qh, "mix_gather_wait_rest")
    w_pp = _cols_from_blocks(seg[0])
    w_dn = seg[1].reshape(D, D)
    w_mo = seg[2].reshape(D, D)
    ya = _pool_fwd(proj, pw, ps, w_pp, "pool_fwd")
    u, w, qk, qd, kd, eg, inv = _dn_local_fwd(qh, kh, vh, bg, "dn_local")
    o, s_saved = _dn_scan_fwd(u, w, qk, qd, kd, eg, "dn_scan")
    ob = _dn_post_fwd(o, proj, gn, "dn_post")
    yb = _matmul(ob, w_dn, out_dtype=F32, name="dn_out")
    merged = _merge_fwd(ya, yb, proj, "merge")
    x2, mix_y, h2 = _matmul_residual(merged, w_mo, x1, gate[1], 1.0, norm=(ng[2], shift[2], scale[2]),
                                     name="mix_out")

    x3, saved2, w_in2, w_out2, _ = _ffn_fwd(x2, h2, gate[2], started[2], started[3], "ffn2")
    loss_row, dx3, dfg, dy2, dgate2 = _final_loss(x3, fg, target, (saved2[3], gate[2], 0.5), "loss")

    (dx2, dsh2, dsc2, dng2, dmy, dgate1), sent2 = _ffn_bwd(dx3, dy2, x2, ng[2], scale[2], w_in2, w_out2, saved2,
                                                           "ffn2", (mix_y, gate[1], 1.0))

    dmerged = _matmul(dmy, w_mo, tb=True, out_dtype=BF16, name="mix_out_dx")
    dw_mo = _matmul(merged, dmy, ta=True, out_dtype=BF16, name="mix_out_dw")
    dproj = lax.empty((t, MIXP), BF16)
    dya, dyb, dproj = _merge_bwd(dmerged, ya, yb, proj, dproj, "merge_bwd")
    dob = _matmul(dyb, w_dn, tb=True, out_dtype=F32, name="dn_out_dx")
    dw_dn = _matmul(ob, dyb, ta=True, out_dtype=BF16, name="dn_out_dw")
    do, dproj, dgn = _dn_post_bwd(o, proj, gn, dob, dproj, "dn_post_bwd")
    du, dw, dqk, dqd, dkd, deg = _dn_scan_bwd(u, w, qk, qd, kd, eg, s_saved, do, "dn_scan_bwd")
    dqh, dkh, dvh, dbg = _dn_local_bwd(qh, kh, vh, bg, inv, du, dw, dqk, dqd, dkd, deg, "dn_local_bwd")
    dconv, dproj, dal, ddt = _dn_pre_bwd_act(proj, conv_full, al_row, dt_row, dqh, dkh, dvh, dbg, dproj,
                                             "dn_pre_bwd_act")
    dproj, dcw = _dn_pre_bwd_conv(proj, conv_full, dconv, dproj, "dn_pre_bwd_conv")
    dwin, dpl, dpw, dps, dpp = _pool_bwd_local(proj, pw, ps, w_pp, dya, "pool_bwd_local")
    dproj = _pool_bwd_window(dwin, dpl, dproj, "pool_bwd_window")
    dw_mix = _matmul(dproj, h1, ta=True, out_dtype=BF16, name="mix_in_dw")
    sent1, token = _push_start(
        [_mix_unpad(dw_mix).reshape(NDEV, MIX_RAW // NDEV, D), _cols_to_blocks(dpp.astype(BF16)),
         dw_dn.reshape(NDEV, -1, D), dw_mo.reshape(NDEV, -1, D)], True, "mix_grad_start")
    dh1 = _matmul(dproj, w_mix, out_dtype=F32, name="mix_in_dx", after=token)
    dx1, dsh1, dsc1, dng1, dy0, dgate0 = _norm_mod_bwd(x1, ng[1], scale[1], dh1, dx2, "mix_norm_bwd",
                                                       (saved1[3], gate[0], 0.5))

    (dx0, dsh0, dsc0, dng0), sent0 = _ffn_bwd(dx1, dy0, x0, ng[0], scale[0], w_in1, w_out1, saved1, "ffn1")

    dmod = jnp.concatenate([dsh0, dsc0, dgate0, dsh1, dsc1, dgate1, dsh2, dsc2, dgate2], axis=1).reshape(-1)
    flat = jnp.concatenate([
        dmod, dal[0, NH:2 * NH], ddt[0, NH:2 * NH], dgn.reshape(-1), dps.reshape(-1), dfg.reshape(-1),
        dpw.reshape(-1), jnp.concatenate([dng0, dng1, dng2], axis=0).reshape(-1), dcw.reshape(-1),
        loss_row[0, 0:1]])
    nflat = 90 * D
    flat = jnp.concatenate([flat, jnp.zeros((nflat - flat.shape[0],), F32)]).reshape(90, D)
    sent_small, small_token = _push_start([flat], False, "small_grad_start")

    def small_grads(flat_all):
        tot = _sum_devices(flat_all, F32, "sum_small_grads").reshape(-1)
        dmod_all = flat_all.reshape(NDEV, nflat)[:, :9 * D]
        dmod_cols = lax.dynamic_slice(dmod_all, (0, me * ncol), (NDEV, ncol))
        g_ada_w = _ada_bwd(c_all.T, dmod_cols, "ada_bwd")
        p = 0
        pieces = {}
        for nm, size in (("ada_b", 9 * D), ("a_log", NH), ("dt_bias", NH), ("dn_norm_g", HD), ("pool_scale", PW),
                         ("final_g", D), ("pool_w", 4 * PG * PG), ("norm_g", 3 * D), ("conv_w", 12 * D),
                         ("loss", 1)):
            pieces[nm] = tot[p:p + size]
            p += size
        g_norm = lax.dynamic_slice(pieces["norm_g"].reshape(3, D), (0, me * 128), (3, 128))
        g_conv = lax.dynamic_slice(pieces["conv_w"].reshape(4, 3 * D), (0, me * 384), (4, 384))
        return pieces["loss"][0], {
            "ada_w": g_ada_w.reshape(ada_w.shape), "ada_b": pieces["ada_b"].reshape(ada_b.shape),
            "norm_g": g_norm.reshape(norm_g.shape), "conv_w": g_conv.reshape(conv_w.shape),
            "a_log": pieces["a_log"].reshape(a_log.shape), "dt_bias": pieces["dt_bias"].reshape(dt_bias.shape),
            "dn_norm_g": pieces["dn_norm_g"].reshape(dn_norm_g.shape),
            "pool_w": pieces["pool_w"].reshape(pool_w.shape),
            "pool_scale": pieces["pool_scale"].reshape(pool_scale.shape),
            "final_g": pieces["final_g"].reshape(final_g.shape),
        }

    grads = {}
    weights = {"ada_w": ada_w, "ada_b": ada_b, "norm_g": norm_g, "ffn1_w_in": ffn1_w_in, "ffn1_w_out": ffn1_w_out,
               "ffn2_w_in": ffn2_w_in, "ffn2_w_out": ffn2_w_out, "mix_w_in": mix_w_in, "conv_w": conv_w,
               "a_log": a_log, "dt_bias": dt_bias, "dn_norm_g": dn_norm_g, "pool_w": pool_w,
               "pool_scale": pool_scale, "pool_proj": pool_proj, "dn_proj": dn_proj, "mix_w_out": mix_w_out,
               "final_g": final_g}
    m_in = {"ada_w": m_ada_w, "ada_b": m_ada_b, "norm_g": m_norm_g, "ffn1_w_in": m_ffn1_w_in,
            "ffn1_w_out": m_ffn1_w_out, "ffn2_w_in": m_ffn2_w_in, "ffn2_w_out": m_ffn2_w_out,
            "mix_w_in": m_mix_w_in, "conv_w": m_conv_w, "a_log": m_a_log, "dt_bias": m_dt_bias,
            "dn_norm_g": m_dn_norm_g, "pool_w": m_pool_w, "pool_scale": m_pool_scale, "pool_proj": m_pool_proj,
            "dn_proj": m_dn_proj, "mix_w_out": m_mix_w_out, "final_g": m_final_g}
    v_in = {"ada_w": v_ada_w, "ada_b": v_ada_b, "norm_g": v_norm_g, "ffn1_w_in": v_ffn1_w_in,
            "ffn1_w_out": v_ffn1_w_out, "ffn2_w_in": v_ffn2_w_in, "ffn2_w_out": v_ffn2_w_out,
            "mix_w_in": v_mix_w_in, "conv_w": v_conv_w, "a_log": v_a_log, "dt_bias": v_dt_bias,
            "dn_norm_g": v_dn_norm_g, "pool_w": v_pool_w, "pool_scale": v_pool_scale, "pool_proj": v_pool_proj,
            "dn_proj": v_dn_proj, "mix_w_out": v_mix_w_out, "final_g": v_final_g}

    names = list(weights)
    large = ("ada_w", "ffn1_w_in", "ffn1_w_out", "ffn2_w_in", "ffn2_w_out", "mix_w_in", "pool_proj", "dn_proj",
             "mix_w_out")
    delta, new_m, new_v = {}, {}, {}

    flipped = ("ffn1_w_in", "ffn2_w_in", "mix_w_in")

    def views(nm):
        shp = weights[nm].shape
        two_d = (shp[-2], shp[-1])
        if nm in flipped:
            return (lambda a: a.reshape(two_d).T), (lambda a: a.T.reshape(shp))
        return (lambda a: a.reshape(two_d)), (lambda a: a.reshape(shp))

    def reduce_update(sent, group, after, tag):
        done = []
        for nm, r in zip(group, _push_wait(sent, True, after, f"{tag}_grad_wait")):
            view, back = views(nm)
            g_, d_, m_, v_ = _reduce_adamw(r, view(weights[nm]), view(m_in[nm]), view(v_in[nm]), f"adamw_{nm}")
            grads[nm], delta[nm], new_m[nm], new_v[nm] = back(g_), back(d_), back(m_), back(v_)
            done.append(d_)
        return done

    done = reduce_update(sent2, ("ffn2_w_in", "ffn2_w_out"), small_token, "ffn2")
    done += reduce_update(sent1, ("mix_w_in", "pool_proj", "dn_proj", "mix_w_out"), done, "mix")
    flat_all, = _push_wait(sent_small, False, done, "small_grad_wait")
    loss, small = small_grads(flat_all)
    grads.update(small)
    view, back = views("ada_w")
    done, m_, v_ = _adamw(view(ada_w), view(grads["ada_w"]), view(m_ada_w), view(v_ada_w), "adamw_ada_w")
    delta["ada_w"], new_m["ada_w"], new_v["ada_w"] = back(done), back(m_), back(v_)
    reduce_update(sent0, ("ffn1_w_in", "ffn1_w_out"), done, "ffn1")
    rest = [nm for nm in names if nm not in large]
    total = sum(weights[nm].size for nm in rest)
    padded = -(-total // D) * D

    def pack(tree, fill):
        flat_ = jnp.concatenate([tree[nm].reshape(-1) for nm in rest])
        return jnp.concatenate([flat_, jnp.full((padded - total,), fill, F32)]).reshape(-1, D)

    d_, m_, v_ = _adamw(pack(weights, 0.0), pack(grads, 0.0), pack(m_in, 0.0), pack(v_in, 1.0), "adamw_small")
    p = 0
    for nm in rest:
        size = weights[nm].size
        shp = weights[nm].shape
        delta[nm] = d_.reshape(-1)[p:p + size].reshape(shp)
        new_m[nm] = m_.reshape(-1)[p:p + size].reshape(shp)
        new_v[nm] = v_.reshape(-1)[p:p + size].reshape(shp)
        p += size

    grad_x = dx0.reshape(x.shape)
    return (loss, grad_x, *[grads[nm] for nm in names], *[delta[nm] for nm in names],
            *[new_m[nm] for nm in names], *[new_v[nm] for nm in names])
```

```python
import functools

import jax
import jax.numpy as jnp
from jax import lax
from jax.experimental import pallas as pl
from jax.experimental.pallas import tpu as pltpu

F32 = jnp.float32
BF16 = jnp.bfloat16
SDS = jax.ShapeDtypeStruct
HI = lax.Precision.HIGHEST

D = 1024
FH = 2816
FB = 704
NH = 8
HD = 128
CH = 64
SCAN_CHUNKS = 8
LOCAL_CHUNKS = 2
NDEV = 8
PW = 512
PG = 128
RMS_EPS = 1e-6
L2_EPS = 1e-6
TR = 512
HALO = 16
VMEM_LIMIT = 56 * 1024 * 1024
MATMUL_VMEM = 40 * 1024 * 1024

MIXP = 6912
OFF_Q, OFF_K, OFF_V, OFF_Z, OFF_GP, OFF_GD, OFF_XP, OFF_BA = 0, 1024, 2048, 3072, 4096, 5120, 6144, 6656
MIX_RAW = 6672

ADAM_LR = 0.001
ADAM_B1 = 0.9
ADAM_B2 = 0.999
ADAM_EPS = 1e-08
ADAM_WD = 0.01
ADAM_STEP = 10

NN = (((1,), (0,)), ((), ()))
NT = (((1,), (1,)), ((), ()))
TN = (((0,), (0,)), ((), ()))


def _dg(a, b, dims, prec=None):
    return lax.dot_general(a, b, dims, precision=prec, preferred_element_type=F32)


def _make_dots(prec):
    @jax.custom_vjp
    def nn(a, b):
        return _dg(a, b, NN, prec)

    @jax.custom_vjp
    def nt(a, b):
        return _dg(a, b, NT, prec)

    @jax.custom_vjp
    def tn(a, b):
        return _dg(a, b, TN, prec)

    nn.defvjp(lambda a, b: (nn(a, b), (a, b)), lambda r, d: (nt(d, r[1]), tn(r[0], d)))
    nt.defvjp(lambda a, b: (nt(a, b), (a, b)), lambda r, d: (nn(d, r[1]), tn(d, r[0])))
    tn.defvjp(lambda a, b: (tn(a, b), (a, b)), lambda r, d: (nt(r[1], d), nn(r[0], d)))
    return nn, nt, tn


_nn, _nt, _tn = _make_dots(None)


def _params(sem):
    return pltpu.CompilerParams(dimension_semantics=sem, vmem_limit_bytes=VMEM_LIMIT)


def _sigmoid(x):
    return 1.0 / (1.0 + jnp.exp(-x))


def _silu(x):
    return x * _sigmoid(x)


def _dsilu(x):
    s = _sigmoid(x)
    return s * (1.0 + x * (1.0 - s))


def _pick(n, cands):
    for c in cands:
        if n % c == 0:
            return c
    raise ValueError(f"no tile for {n}")


def _iota(shape, dim):
    return lax.broadcasted_iota(jnp.int32, shape, dim)


def _matmul(a, b, *, ta=False, tb=False, a_blk=False, b_blk=False, o_blk=False, tm=None, tn=None, tk=None,
            out_dtype, name, after=None):
    if a_blk:
        nb, r, cb = a.shape
        if ta:
            k_dim, m_dim, tm = r, nb * cb, cb
        else:
            m_dim, k_dim, tk = r, nb * cb, cb
    else:
        k_dim, m_dim = a.shape if ta else a.shape[::-1]
    if b_blk:
        nb, r, cb = b.shape
        if tb:
            n_dim, tk = r, cb
            assert nb * cb == k_dim
        else:
            n_dim, tn = nb * cb, cb
            assert r == k_dim
    else:
        n_dim = b.shape[0] if tb else b.shape[1]
    tn = tn or _pick(n_dim, (1024, 768, 512, 256, 128))
    out_bytes = jnp.dtype(out_dtype).itemsize

    def vmem(tm_, tk_):
        return 4 * tk_ * (tm_ + tn) + tm_ * tn * (4 + 2 * out_bytes)

    k_cands = [tk] if tk else [c for c in (k_dim, 4096, 3456, 2816, 2304, 2048, 1024, 512, 256)
                               if c <= k_dim and k_dim % c == 0]
    m_cands = [tm] if tm else [c for c in (2048, 1024, 768, 512, 256, 128) if m_dim % c == 0]
    base = next((c for c in m_cands if c <= 1024), m_cands[-1])
    tk = next((c for c in k_cands if vmem(base, c) <= MATMUL_VMEM), k_cands[-1])
    tm = next((c for c in m_cands if vmem(c, tk) <= MATMUL_VMEM), m_cands[-1])
    nk = k_dim // tk
    dims = ((((0,) if ta else (1,)), ((1,) if tb else (0,))), ((), ()))

    def body(a_ref, b_ref, *rest):
        o_ref, acc_ref = rest[-2:]
        k = pl.program_id(2)

        @pl.when(k == 0)
        def _():
            acc_ref[...] = jnp.zeros_like(acc_ref)

        acc_ref[...] += lax.dot_general(a_ref[...].astype(BF16), b_ref[...].astype(BF16), dims,
                                        preferred_element_type=F32)

        @pl.when(k == nk - 1)
        def _():
            o_ref[...] = acc_ref[...].astype(o_ref.dtype)

    if a_blk:
        a_spec = (pl.BlockSpec((None, tk, tm), lambda i, j, k: (i, k, 0)) if ta
                  else pl.BlockSpec((None, tm, tk), lambda i, j, k: (k, i, 0)))
    else:
        a_spec = (pl.BlockSpec((tk, tm), lambda i, j, k: (k, i)) if ta
                  else pl.BlockSpec((tm, tk), lambda i, j, k: (i, k)))
    if b_blk:
        b_spec = (pl.BlockSpec((None, tn, tk), lambda i, j, k: (k, j, 0)) if tb
                  else pl.BlockSpec((None, tk, tn), lambda i, j, k: (j, k, 0)))
    else:
        b_spec = (pl.BlockSpec((tn, tk), lambda i, j, k: (j, k)) if tb
                  else pl.BlockSpec((tk, tn), lambda i, j, k: (k, j)))
    if o_blk:
        o_spec = pl.BlockSpec((None, tm, tn), lambda i, j, k: (j, i, 0))
        o_shape = SDS((n_dim // tn, m_dim, tn), out_dtype)
    else:
        o_spec = pl.BlockSpec((tm, tn), lambda i, j, k: (i, j))
        o_shape = SDS((m_dim, n_dim), out_dtype)
    return pl.pallas_call(
        body, grid=(m_dim // tm, n_dim // tn, nk),
        in_specs=[a_spec, b_spec] + ([] if after is None else [pl.BlockSpec(memory_space=pl.ANY)]),
        out_specs=o_spec,
        out_shape=o_shape,
        scratch_shapes=[pltpu.VMEM((tm, tn), F32)],
        compiler_params=_params(("parallel", "parallel", "arbitrary")),
        name=name,
    )(a, b, *([] if after is None else [after]))


def _matmul_residual(a, b, x, gate, coef, *, a_blk=False, norm=None, name, after=None):
    m_dim = a.shape[-2]
    tm = _pick(m_dim, (1024, 512))
    if a_blk:
        nk, _, tk = a.shape
        a_spec = pl.BlockSpec((None, tm, tk), lambda i, k: (k, i, 0))
    else:
        tk = a.shape[1]
        nk = 1
        a_spec = pl.BlockSpec((tm, tk), lambda i, k: (i, 0))
    extra = [] if after is None else [after]
    vecs = [gate] + (list(norm) if norm else [])

    def body(a_ref, b_ref, x_ref, gate_ref, *rest):
        vec_refs = rest[:len(vecs) - 1]
        outs = rest[len(vecs) - 1 + len(extra):]
        acc_ref = outs[-1]
        k = pl.program_id(1)

        @pl.when(k == 0)
        def _():
            acc_ref[...] = jnp.zeros_like(acc_ref)

        acc_ref[...] += _dg(a_ref[...], b_ref[...], NN)

        @pl.when(k == nk - 1)
        def _():
            y = acc_ref[...]
            xn = x_ref[...] + (coef * gate_ref[...]) * y
            outs[0][...] = xn
            outs[1][...] = y.astype(outs[1].dtype)
            if norm:
                g_ref, sh_ref, sc_ref = vec_refs
                r = lax.rsqrt(jnp.mean(xn * xn, axis=-1, keepdims=True) + RMS_EPS)
                outs[2][...] = (((xn * r) * g_ref[...]) * (1.0 + sc_ref[...]) + sh_ref[...]).astype(outs[2].dtype)

    row = pl.BlockSpec((tm, D), lambda i, k: (i, 0))
    vec = pl.BlockSpec((1, D), lambda i, k: (0, 0))
    return pl.pallas_call(
        body, grid=(m_dim // tm, nk),
        in_specs=[a_spec, pl.BlockSpec((tk, D), lambda i, k: (k, 0)), row] + [vec] * len(vecs)
        + [pl.BlockSpec(memory_space=pl.ANY)] * len(extra),
        out_specs=[row] * (3 if norm else 2),
        out_shape=[SDS((m_dim, D), F32), SDS((m_dim, D), BF16)] + ([SDS((m_dim, D), BF16)] if norm else []),
        scratch_shapes=[pltpu.VMEM((tm, D), F32)],
        compiler_params=_params(("parallel", "arbitrary")), name=name,
    )(a, b, x, *vecs, *extra)


def _matmul_residual_loss(a, b, x, gate, coef, fg, target, name):
    nk, m_dim, tk = a.shape
    tm = _pick(m_dim, (1024, 512))
    nt = m_dim // tm

    def body(a_ref, b_ref, x_ref, gate_ref, g_ref, t_ref, loss_ref, dx_ref, dg_ref, dy_ref, dgate_ref,
             acc_ref, sq_ref):
        i, k = pl.program_id(0), pl.program_id(1)

        @pl.when(k == 0)
        def _():
            acc_ref[...] = jnp.zeros_like(acc_ref)

        @pl.when(jnp.logical_and(i == 0, k == 0))
        def _():
            sq_ref[...] = jnp.zeros_like(sq_ref)
            dg_ref[...] = jnp.zeros_like(dg_ref)
            dgate_ref[...] = jnp.zeros_like(dgate_ref)

        acc_ref[...] += _dg(a_ref[...], b_ref[...], NN)

        @pl.when(k == nk - 1)
        def _():
            y = acc_ref[...]
            scaled_gate = coef * gate_ref[...]
            xn = x_ref[...] + scaled_gate * y
            gv = g_ref[...]
            r = lax.rsqrt(jnp.mean(xn * xn, axis=-1, keepdims=True) + RMS_EPS)
            n = xn * r
            err = n * gv - t_ref[...]
            sq_ref[...] += jnp.sum(err * err, axis=0, keepdims=True)
            dout = err * (1.0 / D)
            dg_ref[...] += jnp.sum(dout * n, axis=0, keepdims=True)
            dn = dout * gv
            dxv = r * (dn - n * jnp.mean(dn * n, axis=-1, keepdims=True))
            dx_ref[...] = dxv
            dy_ref[...] = (scaled_gate * dxv).astype(dy_ref.dtype)
            dgate_ref[...] += jnp.sum((coef * dxv) * y, axis=0, keepdims=True)

        @pl.when(jnp.logical_and(i == nt - 1, k == nk - 1))
        def _():
            tot = jnp.sum(sq_ref[...], axis=1, keepdims=True) * (0.5 / D)
            loss_ref[...] = jnp.broadcast_to(tot, loss_ref.shape)

    row = pl.BlockSpec((tm, D), lambda i, k: (i, 0))
    vec = pl.BlockSpec((1, D), lambda i, k: (0, 0))
    return pl.pallas_call(
        body, grid=(nt, nk),
        in_specs=[pl.BlockSpec((None, tm, tk), lambda i, k: (k, i, 0)), pl.BlockSpec((tk, D), lambda i, k: (k, 0)),
                  row, vec, vec, row],
        out_specs=[pl.BlockSpec((1, 128), lambda i, k: (0, 0)), row, vec, row, vec],
        out_shape=[SDS((1, 128), F32), SDS((m_dim, D), F32), SDS((1, D), F32), SDS((m_dim, D), BF16),
                   SDS((1, D), F32)],
        scratch_shapes=[pltpu.VMEM((tm, D), F32), pltpu.VMEM((1, D), F32)],
        compiler_params=_params(("arbitrary", "arbitrary")), name=name,
    )(a, b, x, gate, fg, target)


def _row(width, col=0):
    return pl.BlockSpec((TR, width), lambda i: (i, col))


def _vec(width):
    return pl.BlockSpec((1, width), lambda i: (0, 0))


def _norm_mod_fwd(x, g, shift, scale, name, after=None):
    t = x.shape[0]
    extra = [] if after is None else [after]

    def body(x_ref, g_ref, sh_ref, sc_ref, *rest):
        o_ref = rest[-1]
        xv = x_ref[...]
        r = lax.rsqrt(jnp.mean(xv * xv, axis=-1, keepdims=True) + RMS_EPS)
        o_ref[...] = (((xv * r) * g_ref[...]) * (1.0 + sc_ref[...]) + sh_ref[...]).astype(o_ref.dtype)

    return pl.pallas_call(
        body, grid=(t // TR,),
        in_specs=[_row(D), _vec(D), _vec(D), _vec(D)] + [pl.BlockSpec(memory_space=pl.ANY)] * len(extra),
        out_specs=_row(D),
        out_shape=SDS((t, D), BF16), compiler_params=_params(("parallel",)), name=name,
    )(x, g, shift, scale, *extra)


def _residual_branch_bwd(dxv, y_ref, gate_ref, coef, dy_ref, dgate_ref):
    dy_ref[...] = ((coef * gate_ref[...]) * dxv).astype(dy_ref.dtype)
    dgate_ref[...] += jnp.sum((coef * dxv) * y_ref[...], axis=0, keepdims=True)


def _norm_mod_bwd(x, g, scale, dh, dx_in, name, below=None):
    t = x.shape[0]
    lower = [] if below is None else list(below[:2])

    def body(x_ref, g_ref, sc_ref, dh_ref, dxi_ref, *rest):
        dx_ref, dsh_ref, dsc_ref, dg_ref = rest[len(lower):len(lower) + 4]

        @pl.when(pl.program_id(0) == 0)
        def _():
            for ref in rest[len(lower) + 1:]:
                if ref.shape[0] == 1:
                    ref[...] = jnp.zeros_like(ref)

        xv = x_ref[...]
        gv = g_ref[...]
        dh = dh_ref[...]
        r = lax.rsqrt(jnp.mean(xv * xv, axis=-1, keepdims=True) + RMS_EPS)
        n = xv * r
        dsh_ref[...] += jnp.sum(dh, axis=0, keepdims=True)
        dsc_ref[...] += jnp.sum(dh * (n * gv), axis=0, keepdims=True)
        tt = dh * (1.0 + sc_ref[...])
        dg_ref[...] += jnp.sum(tt * n, axis=0, keepdims=True)
        dn = tt * gv
        dxv = dxi_ref[...] + r * (dn - n * jnp.mean(dn * n, axis=-1, keepdims=True))
        dx_ref[...] = dxv
        if below is not None:
            _residual_branch_bwd(dxv, rest[0], rest[1], below[2], rest[-2], rest[-1])

    more_in = [] if below is None else [_row(D), _vec(D)]
    more_out = [] if below is None else [_row(D), _vec(D)]
    more_shape = [] if below is None else [SDS((t, D), BF16), SDS((1, D), F32)]
    return pl.pallas_call(
        body, grid=(t // TR,), in_specs=[_row(D), _vec(D), _vec(D), _row(D), _row(D)] + more_in,
        out_specs=[_row(D), _vec(D), _vec(D), _vec(D)] + more_out,
        out_shape=[SDS((t, D), F32), SDS((1, D), F32), SDS((1, D), F32), SDS((1, D), F32)] + more_shape,
        compiler_params=_params(("arbitrary",)), name=name,
    )(x, g, scale, dh, dx_in, *lower)


def _swiglu_up(h, w_in, name, after=None):
    t = h.shape[0]
    tm = _pick(t, (1024, 512, 256))
    half = NDEV // 2
    extra = [] if after is None else [after]

    def body(h_ref, wg_ref, wu_ref, *rest):
        u_ref, a_ref = rest[-2:]
        hv = h_ref[...]
        gate = _dg(hv, wg_ref[...], NT)
        up = _dg(hv, wu_ref[...], NT)
        u_ref[0] = gate.astype(u_ref.dtype)
        u_ref[1] = up.astype(u_ref.dtype)
        a_ref[...] = (_silu(gate) * up).astype(a_ref.dtype)

    return pl.pallas_call(
        body, grid=(t // tm, half),
        in_specs=[pl.BlockSpec((tm, D), lambda i, j: (i, 0)),
                  pl.BlockSpec((FB, D), lambda i, j: (j, 0)),
                  pl.BlockSpec((FB, D), lambda i, j: (j + half, 0))]
        + [pl.BlockSpec(memory_space=pl.ANY)] * len(extra),
        out_specs=[pl.BlockSpec((2, None, tm, FB), lambda i, j: (0, j, i, 0)),
                   pl.BlockSpec((None, tm, FB), lambda i, j: (j, i, 0))],
        out_shape=[SDS((2, half, t, FB), BF16), SDS((half, t, FB), BF16)],
        compiler_params=_params(("parallel", "parallel")), name=name,
    )(h, w_in, w_in, *extra)


def _swiglu_down_bwd(dy, w_out, u, name, after=None):
    t = dy.shape[0]
    tm = _pick(t, (1024, 512, 256))
    half = NDEV // 2
    extra = [] if after is None else [after]
    pair = pl.BlockSpec((2, None, tm, FB), lambda i, j: (0, j, i, 0))

    def body(dy_ref, w_ref, u_ref, *rest):
        o_ref = rest[-1]
        da = _dg(dy_ref[...], w_ref[...], NT)
        gate = u_ref[0].astype(F32)
        o_ref[0] = (da * u_ref[1].astype(F32) * _dsilu(gate)).astype(o_ref.dtype)
        o_ref[1] = (da * _silu(gate)).astype(o_ref.dtype)

    return pl.pallas_call(
        body, grid=(t // tm, half),
        in_specs=[pl.BlockSpec((tm, D), lambda i, j: (i, 0)), pl.BlockSpec((FB, D), lambda i, j: (j, 0)), pair]
        + [pl.BlockSpec(memory_space=pl.ANY)] * len(extra),
        out_specs=pair, out_shape=SDS((2, half, t, FB), BF16),
        compiler_params=_params(("parallel", "parallel")), name=name,
    )(dy, w_out, u, *extra)


def _final_loss(x, fg, target, below, name):
    t = x.shape[0]
    nt = t // TR

    def body(x_ref, g_ref, t_ref, y_ref, gate_ref, loss_ref, dx_ref, dg_ref, dy_ref, dgate_ref, acc_ref):
        i = pl.program_id(0)

        @pl.when(i == 0)
        def _():
            acc_ref[...] = jnp.zeros_like(acc_ref)
            dg_ref[...] = jnp.zeros_like(dg_ref)
            dgate_ref[...] = jnp.zeros_like(dgate_ref)

        xv = x_ref[...]
        gv = g_ref[...]
        r = lax.rsqrt(jnp.mean(xv * xv, axis=-1, keepdims=True) + RMS_EPS)
        n = xv * r
        err = n * gv - t_ref[...]
        acc_ref[...] += jnp.sum(err * err, axis=0, keepdims=True)
        dy = err * (1.0 / D)
        dg_ref[...] += jnp.sum(dy * n, axis=0, keepdims=True)
        dn = dy * gv
        dxv = r * (dn - n * jnp.mean(dn * n, axis=-1, keepdims=True))
        dx_ref[...] = dxv
        _residual_branch_bwd(dxv, y_ref, gate_ref, below[2], dy_ref, dgate_ref)

        @pl.when(i == nt - 1)
        def _():
            tot = jnp.sum(acc_ref[...], axis=1, keepdims=True) * (0.5 / D)
            loss_ref[...] = jnp.broadcast_to(tot, loss_ref.shape)

    return pl.pallas_call(
        body, grid=(nt,), in_specs=[_row(D), _vec(D), _row(D), _row(D), _vec(D)],
        out_specs=[_vec(128), _row(D), _vec(D), _row(D), _vec(D)],
        out_shape=[SDS((1, 128), F32), SDS((t, D), F32), SDS((1, D), F32), SDS((t, D), BF16), SDS((1, D), F32)],
        scratch_shapes=[pltpu.VMEM((1, D), F32)],
        compiler_params=_params(("arbitrary",)), name=name,
    )(x, fg, target, below[0], below[1])


def _halo_prev(width, col):
    per = TR // HALO
    return pl.BlockSpec((HALO, width), lambda i: (jnp.maximum(i * per - 1, 0), col))


def _halo_next(width, col, nt):
    per = TR // HALO
    return pl.BlockSpec((HALO, width), lambda i: (jnp.minimum((i + 1) * per, nt * per - 1), col))


def _pool_windows(ext, tile_index):
    rows = _iota((TR, PG), 0) + tile_index * TR + 1
    pooled, counts = [], []
    for gi in range(4):
        w = 2 << gi
        e = ext[:, gi * PG:(gi + 1) * PG]
        s = e
        step = 1
        while step < w:
            s = s + pltpu.roll(s, step, 0)
            step *= 2
        cnt = jnp.minimum(rows, w).astype(F32)
        pooled.append(s[HALO:] / cnt - e[HALO:])
        counts.append(cnt)
    return pooled, counts


def _pool_fwd(proj, pool_w, pool_scale, pool_proj, name):
    t = proj.shape[0]
    xcol = OFF_XP // PW

    def body(x_ref, h_ref, pw_ref, ps_ref, pp_ref, o_ref):
        i = pl.program_id(0)
        halo = jnp.where(i > 0, h_ref[...], 0.0)
        ext = jnp.concatenate([halo, x_ref[...]], axis=0)
        pooled, _ = _pool_windows(ext, i)
        mixed = [_dg(pooled[g].astype(BF16), pw_ref[g].astype(BF16), NN) for g in range(4)]
        ypre = jnp.concatenate(mixed, axis=1) * ps_ref[...]
        o_ref[...] = _dg(ypre.astype(BF16), pp_ref[...], NN)

    return pl.pallas_call(
        body, grid=(t // TR,),
        in_specs=[_row(PW, xcol), _halo_prev(PW, xcol),
                  pl.BlockSpec((4, PG, PG), lambda i: (0, 0, 0)), _vec(PW),
                  pl.BlockSpec((PW, D), lambda i: (0, 0))],
        out_specs=_row(D), out_shape=SDS((t, D), F32),
        compiler_params=_params(("parallel",)), name=name,
    )(proj, proj, pool_w, pool_scale, pool_proj)


def _pool_bwd_local(proj, pool_w, pool_scale, pool_proj, dya, name):
    t = proj.shape[0]
    xcol = OFF_XP // PW

    def body(x_ref, h_ref, pw_ref, ps_ref, pp_ref, dya_ref, dwin_ref, dpl_ref, dpw_ref, dps_ref, dpp_ref):
        i = pl.program_id(0)

        @pl.when(i == 0)
        def _():
            dpw_ref[...] = jnp.zeros_like(dpw_ref)
            dps_ref[...] = jnp.zeros_like(dps_ref)
            dpp_ref[...] = jnp.zeros_like(dpp_ref)

        halo = jnp.where(i > 0, h_ref[...], 0.0)
        ext = jnp.concatenate([halo, x_ref[...]], axis=0)
        pooled, counts = _pool_windows(ext, i)
        mixed = jnp.concatenate(
            [_dg(pooled[g].astype(BF16), pw_ref[g].astype(BF16), NN) for g in range(4)], axis=1)
        ps = ps_ref[...]
        ypre = mixed * ps
        dyab = dya_ref[...].astype(BF16)
        dypre = _dg(dyab, pp_ref[...], NT)
        dpp_ref[...] += _dg(ypre.astype(BF16), dyab, TN)
        dps_ref[...] += jnp.sum(dypre * mixed, axis=0, keepdims=True)
        dmixed = dypre * ps
        for g in range(4):
            dm = dmixed[:, g * PG:(g + 1) * PG].astype(BF16)
            dpw_ref[g] += _dg(pooled[g].astype(BF16), dm, TN)
            dpooled = _dg(dm, pw_ref[g].astype(BF16), NT)
            dwin_ref[:, g * PG:(g + 1) * PG] = dpooled / counts[g]
            dpl_ref[:, g * PG:(g + 1) * PG] = dpooled

    return pl.pallas_call(
        body, grid=(t // TR,),
        in_specs=[_row(PW, xcol), _halo_prev(PW, xcol),
                  pl.BlockSpec((4, PG, PG), lambda i: (0, 0, 0)), _vec(PW),
                  pl.BlockSpec((PW, D), lambda i: (0, 0)), _row(D)],
        out_specs=[_row(PW), _row(PW), pl.BlockSpec((4, PG, PG), lambda i: (0, 0, 0)), _vec(PW),
                   pl.BlockSpec((PW, D), lambda i: (0, 0))],
        out_shape=[SDS((t, PW), F32), SDS((t, PW), F32), SDS((4, PG, PG), F32), SDS((1, PW), F32),
                   SDS((PW, D), F32)],
        compiler_params=_params(("arbitrary",)), name=name,
    )(proj, proj, pool_w, pool_scale, pool_proj, dya)


def _pool_bwd_window(dwin, dpl, dproj, name):
    t = dwin.shape[0]
    nt = t // TR
    ext_rows = TR + HALO

    def body(dw_ref, h_ref, dp_ref, _, o_ref):
        i = pl.program_id(0)
        halo = jnp.where(i < nt - 1, h_ref[...], 0.0)
        ext = jnp.concatenate([dw_ref[...], halo], axis=0)
        for gi in range(4):
            w = 2 << gi
            s = ext[:, gi * PG:(gi + 1) * PG]
            step = 1
            while step < w:
                s = s + pltpu.roll(s, ext_rows - step, 0)
                step *= 2
            o_ref[:, gi * PG:(gi + 1) * PG] = (s[:TR] - dp_ref[:, gi * PG:(gi + 1) * PG]).astype(o_ref.dtype)

    return pl.pallas_call(
        body, grid=(nt,),
        in_specs=[_row(PW), _halo_next(PW, 0, nt), _row(PW), pl.BlockSpec(memory_space=pl.ANY)],
        out_specs=_into(PW, OFF_XP), out_shape=SDS(dproj.shape, dproj.dtype), input_output_aliases={3: 0},
        compiler_params=_params(("parallel",)), name=name,
    )(dwin, dwin, dpl, dproj)


def _conv_group(ext, cw_ref, cols):
    acc = cw_ref[3:4, cols] * ext
    for j in range(3):
        acc = acc + cw_ref[j:j + 1, cols] * pltpu.roll(ext, 3 - j, 0)
    return acc[HALO:]


def _gate_terms(raw, al, dt):
    beta = _sigmoid(raw)
    xg = raw + dt
    sp = jnp.maximum(xg, 0.0) + jnp.log(1.0 + jnp.exp(-jnp.abs(xg)))
    g = -jnp.exp(al) * sp
    return beta, g, _sigmoid(xg)


def _dn_pre_fwd(proj, conv_w, al_row, dt_row, name):
    t = proj.shape[0]

    def body(x_ref, h_ref, cw_ref, ba_ref, al_ref, dt_ref, q_ref, k_ref, v_ref, bg_ref):
        i = pl.program_id(0)
        keep = i > 0
        for grp in range(24):
            cols = slice(grp * HD, (grp + 1) * HD)
            ext = jnp.concatenate([jnp.where(keep, h_ref[:, cols], 0.0), x_ref[:, cols]], axis=0)
            s = _silu(_conv_group(ext, cw_ref, cols))
            seg, head = divmod(grp, NH)
            hc = slice(head * HD, (head + 1) * HD)
            if seg == 0:
                q_ref[:, hc] = s * lax.rsqrt(jnp.sum(s * s, axis=-1, keepdims=True) + L2_EPS) * (HD ** -0.5)
            elif seg == 1:
                k_ref[:, hc] = s * lax.rsqrt(jnp.sum(s * s, axis=-1, keepdims=True) + L2_EPS)
            else:
                v_ref[:, hc] = s
        lane = _iota((TR, 128), 1)
        rowc = _iota((TR, 128), 0) % CH
        beta, g, _ = _gate_terms(ba_ref[...], al_ref[...], dt_ref[...])
        step = 1
        while step < CH:
            g = g + jnp.where(rowc >= step, pltpu.roll(g, step, 0), 0.0)
            step *= 2
        bg_ref[...] = jnp.where(lane < NH, beta, jnp.where(lane < 2 * NH, g, 0.0))

    return pl.pallas_call(
        body, grid=(t // TR,),
        in_specs=[_row(3 * D, 0), _halo_prev(3 * D, 0), pl.BlockSpec((4, 3 * D), lambda i: (0, 0)),
                  _row(128, OFF_BA // 128), _vec(128), _vec(128)],
        out_specs=[_row(D), _row(D), _row(D), _row(128)],
        out_shape=[SDS((t, D), F32), SDS((t, D), F32), SDS((t, D), F32), SDS((t, 128), F32)],
        compiler_params=_params(("parallel",)), name=name,
    )(proj, proj, conv_w, proj, al_row, dt_row)


def _dn_pre_bwd_act(proj, conv_w, al_row, dt_row, dq, dk, dv, dbg, dproj, name):
    t = proj.shape[0]

    def body(x_ref, h_ref, cw_ref, ba_ref, al_ref, dt_ref, dq_ref, dk_ref, dv_ref, dbg_ref, _,
             dc_ref, draw_ref, dal_ref, ddt_ref):
        i = pl.program_id(0)

        @pl.when(i == 0)
        def _():
            dal_ref[...] = jnp.zeros_like(dal_ref)
            ddt_ref[...] = jnp.zeros_like(ddt_ref)

        keep = i > 0
        for grp in range(24):
            cols = slice(grp * HD, (grp + 1) * HD)
            ext = jnp.concatenate([jnp.where(keep, h_ref[:, cols], 0.0), x_ref[:, cols]], axis=0)
            cv = _conv_group(ext, cw_ref, cols)
            seg, head = divmod(grp, NH)
            hc = slice(head * HD, (head + 1) * HD)
            if seg == 2:
                ds = dv_ref[:, hc]
            else:
                s = _silu(cv)
                r = lax.rsqrt(jnp.sum(s * s, axis=-1, keepdims=True) + L2_EPS)
                dy = dq_ref[:, hc] if seg == 0 else dk_ref[:, hc]
                c = (HD ** -0.5) if seg == 0 else 1.0
                ds = (c * r) * (dy - s * ((r * r) * jnp.sum(dy * s, axis=-1, keepdims=True)))
            dc_ref[:, cols] = ds * _dsilu(cv)
        lane = _iota((TR, 128), 1)
        rowc = _iota((TR, 128), 0) % CH
        isb = lane < NH
        isg = jnp.logical_and(lane >= NH, lane < 2 * NH)
        beta, g, sg = _gate_terms(ba_ref[...], al_ref[...], dt_ref[...])
        dbgv = dbg_ref[...]
        dg = dbgv
        step = 1
        while step < CH:
            dg = dg + jnp.where(rowc < CH - step, pltpu.roll(dg, TR - step, 0), 0.0)
            step *= 2
        da_raw = dg * (-jnp.exp(al_ref[...])) * sg
        draw = jnp.where(isb, dbgv * beta * (1.0 - beta), jnp.where(isg, da_raw, 0.0))
        draw_ref[:, :128] = draw.astype(draw_ref.dtype)
        draw_ref[:, 128:] = jnp.zeros((TR, MIXP - OFF_BA - 128), draw_ref.dtype)
        dal_ref[...] += jnp.sum(jnp.where(isg, dg * g, 0.0), axis=0, keepdims=True)
        ddt_ref[...] += jnp.sum(jnp.where(isg, da_raw, 0.0), axis=0, keepdims=True)

    return pl.pallas_call(
        body, grid=(t // TR,),
        in_specs=[_row(3 * D, 0), _halo_prev(3 * D, 0), pl.BlockSpec((4, 3 * D), lambda i: (0, 0)),
                  _row(128, OFF_BA // 128), _vec(128), _vec(128), _row(D), _row(D), _row(D), _row(128),
                  pl.BlockSpec(memory_space=pl.ANY)],
        out_specs=[_row(3 * D), _into(MIXP - OFF_BA, OFF_BA), _vec(128), _vec(128)],
        out_shape=[SDS((t, 3 * D), F32), SDS(dproj.shape, dproj.dtype), SDS((1, 128), F32), SDS((1, 128), F32)],
        input_output_aliases={10: 1},
        compiler_params=_params(("arbitrary",)), name=name,
    )(proj, proj, conv_w, proj, al_row, dt_row, dq, dk, dv, dbg, dproj)


def _dn_pre_bwd_conv(proj, conv_w, dconv, dproj, name):
    t = proj.shape[0]
    nt = t // TR
    ext_rows = TR + HALO

    def body(x_ref, h_ref, cw_ref, dc_ref, dn_ref, _, dx_ref, dcw_ref):
        i = pl.program_id(0)

        @pl.when(i == 0)
        def _():
            dcw_ref[...] = jnp.zeros_like(dcw_ref)

        keep_prev = i > 0
        keep_next = i < nt - 1
        for grp in range(24):
            cols = slice(grp * HD, (grp + 1) * HD)
            dct = dc_ref[:, cols]
            dext = jnp.concatenate([dct, jnp.where(keep_next, dn_ref[:, cols], 0.0)], axis=0)
            acc = cw_ref[3:4, cols] * dext
            for j in range(3):
                acc = acc + cw_ref[j:j + 1, cols] * pltpu.roll(dext, ext_rows - (3 - j), 0)
            dx_ref[:, cols] = acc[:TR].astype(dx_ref.dtype)
            xext = jnp.concatenate([jnp.where(keep_prev, h_ref[:, cols], 0.0), x_ref[:, cols]], axis=0)
            for j in range(4):
                xs = xext if j == 3 else pltpu.roll(xext, 3 - j, 0)
                dcw_ref[j:j + 1, cols] += jnp.sum(xs[HALO:] * dct, axis=0, keepdims=True)

    return pl.pallas_call(
        body, grid=(nt,),
        in_specs=[_row(3 * D, 0), _halo_prev(3 * D, 0), pl.BlockSpec((4, 3 * D), lambda i: (0, 0)),
                  _row(3 * D), _halo_next(3 * D, 0, nt), pl.BlockSpec(memory_space=pl.ANY)],
        out_specs=[_into(3 * D, OFF_Q), pl.BlockSpec((4, 3 * D), lambda i: (0, 0))],
        out_shape=[SDS(dproj.shape, dproj.dtype), SDS((4, 3 * D), F32)],
        input_output_aliases={5: 0},
        compiler_params=_params(("arbitrary",)), name=name,
    )(proj, proj, conv_w, dconv, dconv, dproj)


def _dn_post_fwd(o, proj, gn, name):
    t = o.shape[0]

    def body(o_ref, z_ref, g_ref, out_ref):
        gv = g_ref[...]
        for h in range(NH):
            hc = slice(h * HD, (h + 1) * HD)
            ov = o_ref[:, hc]
            r = lax.rsqrt(jnp.mean(ov * ov, axis=-1, keepdims=True) + RMS_EPS)
            out_ref[:, hc] = (((ov * r) * gv) * _silu(z_ref[:, hc])).astype(out_ref.dtype)

    return pl.pallas_call(
        body, grid=(t // TR,), in_specs=[_row(D), _row(D, OFF_Z // D), _vec(HD)], out_specs=_row(D),
        out_shape=SDS((t, D), BF16), compiler_params=_params(("parallel",)), name=name,
    )(o, proj, gn)


def _dn_post_bwd(o, proj, gn, dob, dproj, name):
    t = o.shape[0]

    def body(o_ref, z_ref, g_ref, d_ref, _, do_ref, dz_ref, dg_ref):
        @pl.when(pl.program_id(0) == 0)
        def _():
            dg_ref[...] = jnp.zeros_like(dg_ref)

        gv = g_ref[...]
        acc = jnp.zeros((1, HD), F32)
        for h in range(NH):
            hc = slice(h * HD, (h + 1) * HD)
            ov = o_ref[:, hc]
            zv = z_ref[:, hc]
            dv = d_ref[:, hc]
            r = lax.rsqrt(jnp.mean(ov * ov, axis=-1, keepdims=True) + RMS_EPS)
            n = ov * r
            dz_ref[:, hc] = (dv * (n * gv) * _dsilu(zv)).astype(dz_ref.dtype)
            dng = dv * _silu(zv)
            acc = acc + jnp.sum(dng * n, axis=0, keepdims=True)
            dn = dng * gv
            do_ref[:, hc] = r * (dn - n * jnp.mean(dn * n, axis=-1, keepdims=True))
        dg_ref[...] += acc

    return pl.pallas_call(
        body, grid=(t // TR,),
        in_specs=[_row(D), _row(D, OFF_Z // D), _vec(HD), _row(D), pl.BlockSpec(memory_space=pl.ANY)],
        out_specs=[_row(D), _into(D, OFF_Z), _vec(HD)],
        out_shape=[SDS((t, D), F32), SDS(dproj.shape, dproj.dtype), SDS((1, HD), F32)],
        input_output_aliases={4: 1},
        compiler_params=_params(("arbitrary",)), name=name,
    )(o, proj, gn, dob, dproj)


def _merge_fwd(ya, yb, proj, name):
    t = ya.shape[0]

    def body(a_ref, b_ref, gp_ref, gd_ref, o_ref):
        o_ref[...] = (_sigmoid(gp_ref[...]) * a_ref[...] + _sigmoid(gd_ref[...]) * b_ref[...]).astype(o_ref.dtype)

    return pl.pallas_call(
        body, grid=(t // TR,), in_specs=[_row(D), _row(D), _row(D, OFF_GP // D), _row(D, OFF_GD // D)],
        out_specs=_row(D), out_shape=SDS((t, D), BF16),
        compiler_params=_params(("parallel",)), name=name,
    )(ya, yb, proj, proj)


def _into(width, offset):
    assert offset % width == 0
    return pl.BlockSpec((TR, width), lambda i: (i, offset // width))


def _merge_bwd(dm, ya, yb, proj, dproj, name):
    t = ya.shape[0]

    def body(d_ref, a_ref, b_ref, gp_ref, gd_ref, _, da_ref, db_ref, dg_ref):
        dv = d_ref[...]
        sp = _sigmoid(gp_ref[...])
        sd = _sigmoid(gd_ref[...])
        da_ref[...] = (dv * sp).astype(da_ref.dtype)
        db_ref[...] = (dv * sd).astype(db_ref.dtype)
        dg_ref[:, :D] = (dv * a_ref[...] * sp * (1.0 - sp)).astype(dg_ref.dtype)
        dg_ref[:, D:] = (dv * b_ref[...] * sd * (1.0 - sd)).astype(dg_ref.dtype)

    return pl.pallas_call(
        body, grid=(t // TR,),
        in_specs=[_row(D), _row(D), _row(D), _row(D, OFF_GP // D), _row(D, OFF_GD // D),
                  pl.BlockSpec(memory_space=pl.ANY)],
        out_specs=[_row(D), _row(D), _into(2 * D, OFF_GP)],
        out_shape=[SDS((t, D), BF16), SDS((t, D), BF16), SDS(dproj.shape, dproj.dtype)],
        input_output_aliases={5: 2},
        compiler_params=_params(("parallel",)), name=name,
    )(dm, ya, yb, proj, proj, dproj)


def _split2(x):
    hi = x.astype(BF16)
    return hi, (x - hi.astype(F32)).astype(BF16)


def _dot3(a, b, dims):
    ah, al = _split2(a)
    bh, bl = _split2(b)
    return _dg(ah, bh, dims) + (_dg(ah, bl, dims) + _dg(al, bh, dims))


def _neumann_inverses(mats):
    ri = _iota((CH, CH), 0)
    ci = _iota((CH, CH), 1)
    eye = jnp.where(ri == ci, 1.0, 0.0).astype(F32)
    xs = [-a for a in mats]
    ps = [eye + x for x in xs]
    for _ in range(5):
        xs = [_dot3(x, x, NN) for x in xs]
        ps = [p + _dot3(p, x, NN) for p, x in zip(ps, xs)]
    return ps


def _solve_with(inv):
    @jax.custom_vjp
    def solve(a, rhs):
        return _dot3(inv, rhs, NN)

    def fwd(a, rhs):
        sol = _dot3(inv, rhs, NN)
        return sol, sol

    def bwd(sol, d):
        drhs = _dot3(inv, d, TN)
        return -_dot3(drhs, sol, NT), drhs

    solve.defvjp(fwd, bwd)
    return solve


@jax.custom_vjp
def _rows_to_lanes(g64):
    ri = _iota((CH, CH), 0)
    ci = _iota((CH, CH), 1)
    diag = jnp.where(ri == ci, g64, 0.0)
    ones = jnp.ones((CH, CH), BF16)
    hi = diag.astype(BF16)
    rem = diag - hi.astype(F32)
    mid = rem.astype(BF16)
    lo = (rem - mid.astype(F32)).astype(BF16)
    return _dg(ones, hi, NN) + (_dg(ones, mid, NN) + _dg(ones, lo, NN))


def _rows_to_lanes_bwd(_, d):
    ri = _iota((CH, CH), 0)
    ci = _iota((CH, CH), 1)
    return (jnp.where(ri == ci, jnp.broadcast_to(jnp.sum(d, axis=0, keepdims=True), (CH, CH)), 0.0),)


_rows_to_lanes.defvjp(lambda g64: (_rows_to_lanes(g64), None), _rows_to_lanes_bwd)


def _chunk_local(solve_all, q, k, v, g128, g64, gl128, b128, b64):
    ri = _iota((CH, CH), 0)
    ci = _iota((CH, CH), 1)
    causal = ri >= ci
    strict = ri > ci
    gj = [_rows_to_lanes(g) for g in g64]
    decay = [jnp.where(causal, jnp.exp(jnp.where(causal, g - t, 0.0)), 0.0) for g, t in zip(g64, gj)]
    kk = [_nt(x, x) for x in k]
    a = [jnp.where(strict, b * m * dc, 0.0) for b, m, dc in zip(b64, kk, decay)]
    eg = [jnp.exp(g) for g in g128]
    rhs = [jnp.concatenate([b * x, (b * e) * y], axis=1) for b, x, e, y in zip(b128, v, eg, k)]
    sol = solve_all(a, rhs)
    qk = [jnp.where(causal, _nt(x, y) * dc, 0.0) for x, y, dc in zip(q, k, decay)]
    return ([s[:, :HD] for s in sol], [s[:, HD:] for s in sol], qk, [x * e for x, e in zip(q, eg)],
            [x * jnp.exp(gl - g) for x, gl, g in zip(k, gl128, g128)], [jnp.exp(gl) for gl in gl128])


def _all_head_gates(bgv):
    return tuple(list(z) for z in zip(*[_head_gates(bgv, h) for h in range(NH)]))


def _head_gates(bgv, h):
    lane = _iota((CH, 128), 1)
    row = _iota((CH, 128), 0)
    bcol = jnp.sum(jnp.where(lane == h, bgv, 0.0), axis=1, keepdims=True)
    gcol = jnp.sum(jnp.where(lane == NH + h, bgv, 0.0), axis=1, keepdims=True)
    g128 = jnp.broadcast_to(gcol, (CH, 128))
    gl128 = jnp.broadcast_to(jnp.sum(jnp.where(row == CH - 1, g128, 0.0), axis=0, keepdims=True), (CH, 128))
    return (g128, jnp.broadcast_to(gcol, (CH, CH)), gl128,
            jnp.broadcast_to(bcol, (CH, 128)), jnp.broadcast_to(bcol, (CH, CH)))


def _chunk_specs():
    g = LOCAL_CHUNKS
    row = pl.BlockSpec((g * CH, D), lambda i: (i, 0))
    small = pl.BlockSpec((g * CH, 128), lambda i: (i, 0))
    qk = pl.BlockSpec((g * NH, CH, CH), lambda i: (i, 0, 0))
    eg = pl.BlockSpec((g, NH, 128), lambda i: (i, 0, 0))
    return row, small, qk, eg


def _chunk_heads():
    return [(slice(c * CH, (c + 1) * CH), slice(h * HD, (h + 1) * HD), c, h)
            for c in range(LOCAL_CHUNKS) for h in range(NH)]


def _all_gates(bg_ref):
    per_chunk = [_all_head_gates(bg_ref[c * CH:(c + 1) * CH, :]) for c in range(LOCAL_CHUNKS)]
    return tuple(sum((list(pc[j]) for pc in per_chunk), []) for j in range(5))


def _dn_local_fwd(q, k, v, bg, name):
    t = q.shape[0]
    n = t // CH
    pairs = _chunk_heads()

    def body(q_ref, k_ref, v_ref, bg_ref, u_ref, w_ref, qk_ref, qd_ref, kd_ref, eg_ref, inv_ref):
        def solve_all(mats, rhs):
            invs = _neumann_inverses(mats)
            for p in range(len(pairs)):
                inv_ref[p] = invs[p]
            return [_dot3(m, r, NN) for m, r in zip(invs, rhs)]

        u, w, qk, qd, kd, egl = _chunk_local(
            solve_all, [q_ref[r, hc] for r, hc, _, _ in pairs], [k_ref[r, hc] for r, hc, _, _ in pairs],
            [v_ref[r, hc] for r, hc, _, _ in pairs], *_all_gates(bg_ref))
        for p, (r, hc, c, h) in enumerate(pairs):
            u_ref[r, hc] = u[p]
            w_ref[r, hc] = w[p].astype(w_ref.dtype)
            qd_ref[r, hc] = qd[p].astype(qd_ref.dtype)
            kd_ref[r, hc] = kd[p].astype(kd_ref.dtype)
            qk_ref[p] = qk[p].astype(qk_ref.dtype)
            eg_ref[c, h:h + 1, :] = egl[p][0:1, :]

    row, small, qkb, egb = _chunk_specs()
    return pl.pallas_call(
        body, grid=(n // LOCAL_CHUNKS,), in_specs=[row, row, row, small],
        out_specs=[row, row, qkb, row, row, egb, qkb],
        out_shape=[SDS((t, D), F32), SDS((t, D), BF16), SDS((n * NH, CH, CH), BF16), SDS((t, D), BF16),
                   SDS((t, D), BF16), SDS((n, NH, 128), F32), SDS((n * NH, CH, CH), F32)],
        compiler_params=_params(("parallel",)), name=name,
    )(q, k, v, bg)


def _dn_local_bwd(q, k, v, bg, inv, du, dw, dqk, dqd, dkd, deg, name):
    t = q.shape[0]
    n = t // CH
    pairs = _chunk_heads()

    def body(q_ref, k_ref, v_ref, bg_ref, inv_ref, du_ref, dw_ref, dqk_ref, dqd_ref, dkd_ref, deg_ref,
             dq_ref, dk_ref, dv_ref, dbg_ref):
        lane = _iota((CH, 128), 1)
        row = _iota((CH, 128), 0)
        first = jnp.where(row == 0, 1.0, 0.0)
        solves = [_solve_with(inv_ref[p]) for p in range(len(pairs))]

        def solve_all(mats, rhs):
            return [f(m, r) for f, m, r in zip(solves, mats, rhs)]

        _, vjp = jax.vjp(functools.partial(_chunk_local, solve_all),
                         [q_ref[r, hc] for r, hc, _, _ in pairs], [k_ref[r, hc] for r, hc, _, _ in pairs],
                         [v_ref[r, hc] for r, hc, _, _ in pairs], *_all_gates(bg_ref))
        cts = ([du_ref[r, hc].astype(F32) for r, hc, _, _ in pairs],
               [dw_ref[r, hc].astype(F32) for r, hc, _, _ in pairs],
               [dqk_ref[p] for p in range(len(pairs))],
               [dqd_ref[r, hc].astype(F32) for r, hc, _, _ in pairs],
               [dkd_ref[r, hc].astype(F32) for r, hc, _, _ in pairs],
               [jnp.broadcast_to(deg_ref[c, h:h + 1, :], (CH, 128)) * first for _, _, c, h in pairs])
        dq, dk, dv, dg128, dg64, dgl, db128, db64 = vjp(cts)
        acc = [jnp.zeros((CH, 128), F32) for _ in range(LOCAL_CHUNKS)]
        for p, (r, hc, c, h) in enumerate(pairs):
            dq_ref[r, hc] = dq[p]
            dk_ref[r, hc] = dk[p]
            dv_ref[r, hc] = dv[p]
            dg = jnp.sum(dg128[p], axis=1, keepdims=True) + jnp.sum(dg64[p], axis=1, keepdims=True)
            tot = jnp.sum(jnp.sum(dgl[p], axis=0, keepdims=True), axis=1, keepdims=True)
            dg = dg + jnp.where(row[:, 0:1] == CH - 1, tot, 0.0)
            db = jnp.sum(db128[p], axis=1, keepdims=True) + jnp.sum(db64[p], axis=1, keepdims=True)
            acc[c] = acc[c] + jnp.where(lane == h, db, 0.0) + jnp.where(lane == NH + h, dg, 0.0)
        for c in range(LOCAL_CHUNKS):
            dbg_ref[c * CH:(c + 1) * CH, :] = acc[c]

    row, small, qkb, egb = _chunk_specs()
    return pl.pallas_call(
        body, grid=(n // LOCAL_CHUNKS,), in_specs=[row, row, row, small, qkb, row, row, qkb, row, row, egb],
        out_specs=[row, row, row, small],
        out_shape=[SDS((t, D), F32)] * 3 + [SDS((t, 128), F32)],
        compiler_params=_params(("parallel",)), name=name,
    )(q, k, v, bg, inv, du, dw, dqk, dqd, dkd, deg)


def _state_step(s, u, w, qk, qd, kd, egl):
    ws = [_nn(a, b) for a, b in zip(w, s)]
    v_new = [a - b for a, b in zip(u, ws)]
    qs = [_nn(a, b) for a, b in zip(qd, s)]
    intra = [_nn(a, b) for a, b in zip(qk, v_new)]
    upd = [_tn(a, b) for a, b in zip(kd, v_new)]
    return [a * e + b for a, e, b in zip(s, egl, upd)], [a + b for a, b in zip(qs, intra)]


def _dn_scan_fwd(u, w, qk, qd, kd, eg, name):
    t = u.shape[0]
    n = t // CH
    g = SCAN_CHUNKS

    def body(u_ref, w_ref, qk_ref, qd_ref, kd_ref, eg_ref, o_ref, save_ref, s_ref):
        @pl.when(pl.program_id(0) == 0)
        def _():
            s_ref[...] = jnp.zeros_like(s_ref)

        cols = [slice(h * HD, (h + 1) * HD) for h in range(NH)]
        s = [s_ref[h] for h in range(NH)]
        for c in range(g):
            rows = slice(c * CH, (c + 1) * CH)
            for h in range(NH):
                save_ref[c, h] = s[h].astype(save_ref.dtype)
            s, o = _state_step(
                s, [u_ref[rows, hc] for hc in cols], [w_ref[rows, hc].astype(F32) for hc in cols],
                [qk_ref[c * NH + h].astype(F32) for h in range(NH)], [qd_ref[rows, hc].astype(F32) for hc in cols],
                [kd_ref[rows, hc].astype(F32) for hc in cols], [eg_ref[c, h:h + 1, :] for h in range(NH)])
            for h, hc in enumerate(cols):
                o_ref[rows, hc] = o[h]
        for h in range(NH):
            s_ref[h] = s[h]

    row = pl.BlockSpec((g * CH, D), lambda i: (i, 0))
    qkb = pl.BlockSpec((g * NH, CH, CH), lambda i: (i, 0, 0))
    egb = pl.BlockSpec((g, NH, 128), lambda i: (i, 0, 0))
    return pl.pallas_call(
        body, grid=(n // g,), in_specs=[row, row, qkb, row, row, egb],
        out_specs=[row, pl.BlockSpec((g, NH, HD, HD), lambda i: (i, 0, 0, 0))],
        out_shape=[SDS((t, D), F32), SDS((n, NH, HD, HD), BF16)],
        scratch_shapes=[pltpu.VMEM((NH, HD, HD), F32)],
        compiler_params=_params(("arbitrary",)), name=name,
    )(u, w, qk, qd, kd, eg)


def _dn_scan_bwd(u, w, qk, qd, kd, eg, saved, do, name):
    t = u.shape[0]
    n = t // CH
    g = SCAN_CHUNKS
    last = n // g - 1

    def body(u_ref, w_ref, qk_ref, qd_ref, kd_ref, eg_ref, sv_ref, do_ref,
             du_ref, dw_ref, dqk_ref, dqd_ref, dkd_ref, deg_ref, ds_ref):
        @pl.when(pl.program_id(0) == 0)
        def _():
            ds_ref[...] = jnp.zeros_like(ds_ref)

        cols = [slice(h * HD, (h + 1) * HD) for h in range(NH)]
        ds = [ds_ref[h] for h in range(NH)]
        for c in reversed(range(g)):
            rows = slice(c * CH, (c + 1) * CH)
            _, vjp = jax.vjp(
                _state_step, [sv_ref[c, h].astype(F32) for h in range(NH)], [u_ref[rows, hc] for hc in cols],
                [w_ref[rows, hc].astype(F32) for hc in cols], [qk_ref[c * NH + h].astype(F32) for h in range(NH)],
                [qd_ref[rows, hc].astype(F32) for hc in cols], [kd_ref[rows, hc].astype(F32) for hc in cols],
                [eg_ref[c, h:h + 1, :] for h in range(NH)])
            ds, du, dw, dqk, dqd, dkd, deg = vjp((ds, [do_ref[rows, hc] for hc in cols]))
            for h, hc in enumerate(cols):
                du_ref[rows, hc] = du[h].astype(du_ref.dtype)
                dw_ref[rows, hc] = dw[h].astype(dw_ref.dtype)
                dqk_ref[c * NH + h] = dqk[h]
                dqd_ref[rows, hc] = dqd[h].astype(dqd_ref.dtype)
                dkd_ref[rows, hc] = dkd[h].astype(dkd_ref.dtype)
                deg_ref[c, h:h + 1, :] = deg[h]
        for h in range(NH):
            ds_ref[h] = ds[h]

    row = pl.BlockSpec((g * CH, D), lambda i: (last - i, 0))
    qkb = pl.BlockSpec((g * NH, CH, CH), lambda i: (last - i, 0, 0))
    egb = pl.BlockSpec((g, NH, 128), lambda i: (last - i, 0, 0))
    return pl.pallas_call(
        body, grid=(n // g,),
        in_specs=[row, row, qkb, row, row, egb,
                  pl.BlockSpec((g, NH, HD, HD), lambda i: (last - i, 0, 0, 0)), row],
        out_specs=[row, row, qkb, row, row, egb],
        out_shape=[SDS((t, D), BF16), SDS((t, D), BF16), SDS((n * NH, CH, CH), F32), SDS((t, D), BF16),
                   SDS((t, D), BF16), SDS((n, NH, 128), F32)],
        scratch_shapes=[pltpu.VMEM((NH, HD, HD), F32)],
        compiler_params=_params(("arbitrary",)), name=name,
    )(u, w, qk, qd, kd, eg, saved, do)


def _ada_fwd(c_all, ada_w, ada_b, name):
    ncol = ada_w.shape[1]

    def body(c_ref, w_ref, b_ref, o_ref):
        o_ref[...] = _dg(_silu(c_ref[...]), w_ref[...], NN, HI) + b_ref[...]

    return pl.pallas_call(body, out_shape=SDS((NDEV, ncol), F32),
                          compiler_params=pltpu.CompilerParams(vmem_limit_bytes=VMEM_LIMIT), name=name,
                          )(c_all, ada_w, ada_b)


def _ada_bwd(c_all_t, dmod, name):
    ncol = dmod.shape[1]

    def body(c_ref, d_ref, o_ref):
        sc = _silu(c_ref[...])
        acc = sc[:, 0:1] * d_ref[0:1, :]
        for b in range(1, NDEV):
            acc = acc + sc[:, b:b + 1] * d_ref[b:b + 1, :]
        o_ref[...] = acc

    return pl.pallas_call(body, out_shape=SDS((D, ncol), F32),
                          compiler_params=pltpu.CompilerParams(vmem_limit_bytes=VMEM_LIMIT), name=name,
                          )(c_all_t, dmod)


def _sum_devices(parts, out_dtype, name):
    _, r, c = parts.shape
    tr = TR if r % TR == 0 else r

    def body(p_ref, o_ref):
        acc = p_ref[0].astype(F32)
        for i in range(1, NDEV):
            acc = acc + p_ref[i].astype(F32)
        o_ref[...] = acc.astype(o_ref.dtype)

    return pl.pallas_call(
        body, grid=(r // tr,), in_specs=[pl.BlockSpec((NDEV, tr, c), lambda i: (0, i, 0))],
        out_specs=pl.BlockSpec((tr, c), lambda i: (i, 0)), out_shape=SDS((r, c), out_dtype),
        compiler_params=_params(("parallel",)), name=name,
    )(parts)


def _adam_tiles(r, c):
    if r % 8 == 0:
        return _pick(r, (256, 352, 128, 8)), c
    return r, (256 if c % 256 == 0 else c)


def _adam_math(w, gv, m, v):
    m_new = ADAM_B1 * m + (1.0 - ADAM_B1) * gv
    v_new = ADAM_B2 * v + (1.0 - ADAM_B2) * (gv * gv)
    bc1 = 1.0 - ADAM_B1 ** ADAM_STEP
    bc2 = 1.0 - ADAM_B2 ** ADAM_STEP
    return -ADAM_LR * ((m_new / bc1) / (jnp.sqrt(v_new / bc2) + ADAM_EPS) + ADAM_WD * w), m_new, v_new


def _adamw(w, g, m, v, name):
    r, c = w.shape
    tr, tc = _adam_tiles(r, c)

    def body(w_ref, g_ref, m_ref, v_ref, d_ref, nm_ref, nv_ref):
        d_ref[...], nm_ref[...], nv_ref[...] = _adam_math(w_ref[...], g_ref[...], m_ref[...], v_ref[...])

    spec = pl.BlockSpec((tr, tc), lambda i, j: (i, j))
    return pl.pallas_call(
        body, grid=(r // tr, c // tc), in_specs=[spec] * 4, out_specs=[spec] * 3,
        out_shape=[SDS((r, c), F32)] * 3, compiler_params=_params(("parallel", "parallel")), name=name,
    )(w, g, m, v)


def _reduce_adamw(parts, w, m, v, name):
    r, c = w.shape
    tr, tc = _adam_tiles(r, c)

    def body(p_ref, w_ref, m_ref, v_ref, g_ref, d_ref, nm_ref, nv_ref):
        gv = p_ref[0].astype(F32)
        for i in range(1, NDEV):
            gv = gv + p_ref[i].astype(F32)
        g_ref[...] = gv
        d_ref[...], nm_ref[...], nv_ref[...] = _adam_math(w_ref[...], gv, m_ref[...], v_ref[...])

    spec = pl.BlockSpec((tr, tc), lambda i, j: (i, j))
    return pl.pallas_call(
        body, grid=(r // tr, c // tc),
        in_specs=[pl.BlockSpec((NDEV, tr, tc), lambda i, j: (0, i, j))] + [spec] * 3, out_specs=[spec] * 4,
        out_shape=[SDS((r, c), F32)] * 4, compiler_params=_params(("parallel", "parallel")), name=name,
    )(parts, w, m, v)


ANY = pl.BlockSpec(memory_space=pl.ANY)
MESH = pl.DeviceIdType.MESH


def _all_gather(xs, name, after=None):
    n = len(xs)
    extra = [] if after is None else [after]

    def body(*refs):
        x_refs, out_refs = refs[:n], refs[n + len(extra):2 * n + len(extra)]
        send_sems, recv_sems, local_sems = refs[-3:]
        mx, my, mc = lax.axis_index("x"), lax.axis_index("y"), lax.axis_index("c")
        me, sibling = (mx, my, mc), (mx, my, 1 - mc)
        chips = [(1 - mx, my), (mx, 1 - my), (1 - mx, 1 - my)]

        def rows(a, px, py, pc):
            return out_refs[a].at[4 * px + 2 * py + pc]

        def copy(a, k, block, to, src=None):
            return pltpu.make_async_remote_copy(
                src_ref=rows(a, *block) if src is None else src, dst_ref=rows(a, *block),
                send_sem=send_sems.at[a, k], recv_sem=recv_sems.at[a, k], device_id=to, device_id_type=MESH)

        mine = [pltpu.make_async_copy(x_refs[a], rows(a, *me), local_sems.at[a]) for a in range(n)]
        for cp in mine:
            cp.start()
        first = []
        for a in range(n):
            first.append(copy(a, 0, me, sibling, src=x_refs[a]))
            first += [copy(a, 1 + j, me, (*chip, mc), src=x_refs[a]) for j, chip in enumerate(chips)]
        for cp in first:
            cp.start()
        passed = []
        for a in range(n):
            for j, chip in enumerate(chips):
                copy(a, 1 + j, (*chip, mc), me).wait_recv()
                passed.append(copy(a, 4 + j, (*chip, mc), sibling))
                passed[-1].start()
        for a in range(n):
            copy(a, 0, sibling, me).wait_recv()
            for j, chip in enumerate(chips):
                copy(a, 4 + j, (*chip, 1 - mc), me).wait_recv()
        for cp in first + passed:
            cp.wait_send()
        for cp in mine:
            cp.wait()

    return pl.pallas_call(
        body, out_shape=[SDS((NDEV,) + x.shape, x.dtype) for x in xs], in_specs=[ANY] * (n + len(extra)),
        out_specs=[ANY] * n,
        scratch_shapes=[pltpu.SemaphoreType.DMA((n, 7)), pltpu.SemaphoreType.DMA((n, 7)),
                        pltpu.SemaphoreType.DMA((n,))],
        name=name,
    )(*xs, *extra)


HBM = pl.BlockSpec(memory_space=pltpu.HBM)
SEM = pl.BlockSpec(memory_space=pltpu.SEMAPHORE)
EFFECT = pltpu.SideEffectType.DATAFLOW_SIDE_EFFECTING


def _peers():
    mx, my, mc = lax.axis_index("x"), lax.axis_index("y"), lax.axis_index("c")
    out = []
    for k in range(1, NDEV):
        out.append((1 - mx if k & 4 else mx, 1 - my if k & 2 else my, 1 - mc if k & 1 else mc))
    return 4 * mx + 2 * my + mc, out


NEAR = (0, 1, 3, 5)


def _push_start(srcs, sliced, name, after=None, near=()):
    n = len(srcs)
    extra = [] if after is None else [after]
    lands = [lax.empty(s.shape if sliced else (NDEV,) + s.shape, s.dtype) for s in srcs]

    def body(*refs):
        src_refs, land_refs = refs[:n], refs[n:2 * n]
        outs = refs[2 * n + len(extra):]
        send_sems, recv_sems = outs[:n], outs[n:2 * n]
        token = refs[-1]
        me, peers = _peers()
        for a in range(n):
            for k, (px, py, pc) in enumerate(peers):
                if a in near and k not in NEAR:
                    continue
                src = src_refs[a].at[4 * px + 2 * py + pc] if sliced else src_refs[a]
                pltpu.make_async_remote_copy(
                    src_ref=src, dst_ref=land_refs[a].at[me], send_sem=send_sems[a].at[k],
                    recv_sem=recv_sems[a].at[k], device_id=(px, py, pc), device_id_type=MESH).start()
            pltpu.make_async_copy(src_refs[a].at[me] if sliced else src_refs[a], land_refs[a].at[me],
                                  send_sems[a].at[NDEV - 1]).start()
        token[...] = jnp.zeros_like(token)

    outs = pl.pallas_call(
        body, name=name,
        out_shape=([pltpu.SemaphoreType.DMA((NDEV,))] * n + [pltpu.SemaphoreType.DMA((NDEV - 1,))] * n
                   + [pltpu.HBM(s.shape, s.dtype) for s in srcs] + [pltpu.HBM(l.shape, l.dtype) for l in lands]
                   + [SDS((8, 128), F32)]),
        in_specs=[HBM] * (2 * n) + [pl.BlockSpec(memory_space=pl.ANY)] * len(extra),
        out_specs=[SEM] * (2 * n) + [HBM] * (2 * n) + [pl.BlockSpec(memory_space=pltpu.VMEM)],
        input_output_aliases={i: 2 * n + i for i in range(2 * n)},
        compiler_params=pltpu.CompilerParams(has_side_effects=EFFECT),
    )(*[pltpu.with_memory_space_constraint(s, pltpu.HBM) for s in srcs],
      *[pltpu.with_memory_space_constraint(l, pltpu.HBM) for l in lands], *extra)
    sends, recvs = outs[:n], outs[n:2 * n]
    src_thru, land_thru = outs[2 * n:3 * n], outs[3 * n:4 * n]
    return [(sends[a], recvs[a], src_thru[a], land_thru[a]) for a in range(n)], outs[-1]


def _push_wait(started, sliced, after, name, near=()):
    n = len(started)
    afters = list(after) if isinstance(after, (list, tuple)) else [after]

    def body(*refs):
        src_refs, land_refs = refs[:n], refs[n:2 * n]
        send_sems, recv_sems = refs[2 * n:3 * n], refs[3 * n:4 * n]
        me, peers = _peers()
        for a in range(n):
            for k, (px, py, pc) in enumerate(peers):
                if a in near and k not in NEAR:
                    continue
                src = src_refs[a].at[4 * px + 2 * py + pc] if sliced else src_refs[a]
                cp = pltpu.make_async_remote_copy(
                    src_ref=src, dst_ref=land_refs[a].at[me], send_sem=send_sems[a].at[k],
                    recv_sem=recv_sems[a].at[k], device_id=(px, py, pc), device_id_type=MESH)
                cp.wait_send()
                cp.wait_recv()
            pltpu.make_async_copy(src_refs[a].at[me] if sliced else src_refs[a], land_refs[a].at[me],
                                  send_sems[a].at[NDEV - 1]).wait()

    srcs = [s[2] for s in started]
    lands = [s[3] for s in started]
    outs = pl.pallas_call(
        body, name=name,
        out_shape=[pltpu.HBM(s.shape, s.dtype) for s in srcs] + [pltpu.HBM(l.shape, l.dtype) for l in lands],
        in_specs=[HBM] * (2 * n) + [SEM] * (2 * n) + [pl.BlockSpec(memory_space=pl.ANY)] * len(afters),
        out_specs=[HBM] * (2 * n),
        input_output_aliases={i: i for i in range(2 * n)},
        compiler_params=pltpu.CompilerParams(has_side_effects=EFFECT),
    )(*srcs, *lands, *[s[0] for s in started], *[s[1] for s in started], *afters)
    return outs[n:]


def _relay_to_sibling(land, name):
    def body(_, land_ref, send_sems, recv_sems):
        mx, my, mc = lax.axis_index("x"), lax.axis_index("y"), lax.axis_index("c")
        chips = [(1 - mx, my), (mx, 1 - my), (1 - mx, 1 - my)]

        def copy(j, core):
            slot = land_ref.at[4 * chips[j][0] + 2 * chips[j][1] + core]
            return pltpu.make_async_remote_copy(
                src_ref=slot, dst_ref=slot, send_sem=send_sems.at[j], recv_sem=recv_sems.at[j],
                device_id=(mx, my, 1 - mc), device_id_type=MESH)

        mine = [copy(j, mc) for j in range(3)]
        for cp in mine:
            cp.start()
        for j in range(3):
            copy(j, 1 - mc).wait_recv()
        for cp in mine:
            cp.wait_send()

    return pl.pallas_call(
        body, out_shape=SDS(land.shape, land.dtype), in_specs=[ANY], out_specs=ANY, input_output_aliases={0: 0},
        scratch_shapes=[pltpu.SemaphoreType.DMA((3,)), pltpu.SemaphoreType.DMA((3,))], name=name,
    )(land)


def _cols_from_blocks(blocks):
    _, rows, w = blocks.shape
    return blocks.transpose(1, 0, 2).reshape(rows, NDEV * w)


def _cols_to_blocks(full):
    rows, total = full.shape
    return full.reshape(rows, NDEV, total // NDEV).transpose(1, 0, 2)


def _mix_pad(wt):
    xp, q, k, v, z, ba, gp, gd = jnp.split(wt, (512, 1536, 2560, 3584, 4608, 4624, 5648), axis=0)
    pad = jnp.zeros((MIXP - OFF_BA - 16, wt.shape[1]), wt.dtype)
    return jnp.concatenate([q, k, v, z, gp, gd, xp, ba, pad], axis=0)


def _mix_unpad(wt):
    q, k, v, z, gp, gd, xp, ba = (wt[OFF_Q:OFF_K], wt[OFF_K:OFF_V], wt[OFF_V:OFF_Z], wt[OFF_Z:OFF_GP],
                                  wt[OFF_GP:OFF_GD], wt[OFF_GD:OFF_XP], wt[OFF_XP:OFF_BA], wt[OFF_BA:OFF_BA + 16])
    return jnp.concatenate([xp, q, k, v, z, ba, gp, gd], axis=0)


def _lane_row(vec8):
    return jnp.zeros((1, 128), F32).at[0, NH:2 * NH].set(vec8)


def _ffn_fwd(x, h, gate, w_in, w_out, tag, next_norm=None, token=None, start_more=None, final=None):
    if isinstance(w_in, tuple):
        w_in, = _push_wait([w_in], False, h, f"{tag}_gather_wait_in")
    w_in = w_in.reshape(2 * FH, D)
    u, a = _swiglu_up(h, w_in, f"{tag}_up", after=token)
    w_out, = _push_wait([w_out], False, a, f"{tag}_gather_wait_out")
    w_out = w_out.reshape(FH, D)
    if final is not None:
        return _matmul_residual_loss(a, w_out, x, gate, 0.5, *final, f"{tag}_down_loss"), (h, u, a, None), w_in, w_out
    outs = _matmul_residual(a, w_out, x, gate, 0.5, a_blk=True, norm=next_norm, name=f"{tag}_down",
                            after=None if start_more is None else start_more(h))
    return outs[0], (h, u, a, outs[1]), w_in, w_out, (outs[2] if next_norm else None)


def _ffn_bwd(dx_out, dy, x, g, scale, w_in, w_out, saved, tag, below=None):
    h, u, a, _ = saved
    t = x.shape[0]
    dw_out = _matmul(a, dy, ta=True, a_blk=True, out_dtype=BF16, name=f"{tag}_down_dw")
    sent_out, token = _push_start([dw_out.reshape(NDEV, FH // NDEV, D)], True, f"{tag}_grad_start_out")
    du = _swiglu_down_bwd(dy, w_out, u, f"{tag}_down_dx", after=token).reshape(NDEV, t, FB)
    dw_in = _matmul(du, h, ta=True, a_blk=True, out_dtype=BF16, name=f"{tag}_up_dw")
    sent_in, token = _push_start([dw_in.reshape(NDEV, FB, D)], True, f"{tag}_grad_start_in")
    dh = _matmul(du, w_in, a_blk=True, out_dtype=F32, name=f"{tag}_up_dx", after=token)
    return _norm_mod_bwd(x, g, scale, dh, dx_out, f"{tag}_norm_bwd", below), sent_in + sent_out


def kernel(x, c, ada_w, ada_b, norm_g, ffn1_w_in, ffn1_w_out, ffn2_w_in, ffn2_w_out, mix_w_in, conv_w, a_log, dt_bias, dn_norm_g, pool_w, pool_scale, pool_proj, dn_proj, mix_w_out, final_g, loss_target, m_ada_w, m_ada_b, m_norm_g, m_ffn1_w_in, m_ffn1_w_out, m_ffn2_w_in, m_ffn2_w_out, m_mix_w_in, m_conv_w, m_a_log, m_dt_bias, m_dn_norm_g, m_pool_w, m_pool_scale, m_pool_proj, m_dn_proj, m_mix_w_out, m_final_g, v_ada_w, v_ada_b, v_norm_g, v_ffn1_w_in, v_ffn1_w_out, v_ffn2_w_in, v_ffn2_w_out, v_mix_w_in, v_conv_w, v_a_log, v_dt_bias, v_dn_norm_g, v_pool_w, v_pool_scale, v_pool_proj, v_dn_proj, v_mix_w_out, v_final_g):
    me = 4 * lax.axis_index("x") + 2 * lax.axis_index("y") + lax.axis_index("c")
    x0 = x[0]
    target = loss_target[0]
    t = x0.shape[0]

    big = [ffn1_w_in[0], ffn1_w_out[0], ffn2_w_in[0], ffn2_w_out[0], mix_w_in[0], pool_proj[0], dn_proj[0],
           mix_w_out[0]]
    small = jnp.concatenate([c.reshape(8, 128), conv_w[0].reshape(12, 128), norm_g[0].reshape(3, 128),
                             jnp.zeros((1, 128), F32)], axis=0)
    small_all, = _all_gather([small], "gather_small")
    c_all = small_all[:, 0:8, :].reshape(NDEV, D)
    conv_full = small_all[:, 8:20, :].reshape(NDEV, 4, 384).transpose(1, 0, 2).reshape(4, 3 * D)
    norm_full = small_all[:, 20:23, :].reshape(NDEV, 3, 128).transpose(1, 0, 2).reshape(3, D)

    ncol = ada_w.shape[2]
    ada_b_mine = lax.dynamic_slice(ada_b, (0, me * ncol), (1, ncol))
    mod_cols = _ada_fwd(c_all, ada_w[0], ada_b_mine, "ada_fwd")
    transposed = (0, 2, 4)
    payload = [(w.T if i in transposed else w).astype(BF16) for i, w in enumerate(big)]
    mod_all, w_in1 = _all_gather([mod_cols, payload[0]], "gather_mod_first_weight")
    started, token = _push_start([payload[1], payload[4]], False, "gather_start", after=mod_all, near=(1,))
    started = {1: started[0], 4: started[1]}

    def start_rest(h):
        more, token = _push_start([payload[i] for i in (5, 6, 7, 2, 3)], False, "gather_start_rest", after=h)
        started.update(zip((5, 6, 7, 2, 3), more))
        return token

    mod = lax.dynamic_index_in_dim(mod_all, me, axis=1, keepdims=False).reshape(9, D)
    shift = [mod[3 * s:3 * s + 1] for s in range(3)]
    scale = [mod[3 * s + 1:3 * s + 2] for s in range(3)]
    gate = [mod[3 * s + 2:3 * s + 3] for s in range(3)]
    ng = [norm_full[s:s + 1] for s in range(3)]
    fg = final_g.reshape(1, D)
    al_row = _lane_row(a_log[0])
    dt_row = _lane_row(dt_bias[0])
    gn = dn_norm_g
    pw = pool_w[0]
    ps = pool_scale

    h0 = _norm_mod_fwd(x0, ng[0], shift[0], scale[0], "ffn1_norm", after=token)
    x1, saved1, w_in1, w_out1, h1 = _ffn_fwd(x0, h0, gate[0], w_in1, started[1], "ffn1",
                                             (ng[1], shift[1], scale[1]), token, start_rest)

    seg, = _push_wait([started[4]], False, h1, "mix_gather_wait", near=(0,))
    w_mix = _mix_pad(_relay_to_sibling(seg, "mix_gather_relay").reshape(MIX_RAW, D))
    proj = _matmul(h1, w_mix, tb=True, out_dtype=F32, name="mix_in")
    qh, kh, vh, bg = _dn_pre_fwd(proj, conv_full, al_row, dt_row, "dn_pre")
    seg = _push_wait([started[i] for i in (5, 6, 7)], False, qh, "mix_gather_wait_rest")
    w_pp = _cols_from_blocks(seg[0])
    w_dn = seg[1].reshape(D, D)
    w_mo = seg[2].reshape(D, D)
    ya = _pool_fwd(proj, pw, ps, w_pp, "pool_fwd")
    u, w, qk, qd, kd, eg, inv = _dn_local_fwd(qh, kh, vh, bg, "dn_local")
    o, s_saved = _dn_scan_fwd(u, w, qk, qd, kd, eg, "dn_scan")
    ob = _dn_post_fwd(o, proj, gn, "dn_post")
    yb = _matmul(ob, w_dn, out_dtype=F32, name="dn_out")
    merged = _merge_fwd(ya, yb, proj, "merge")
    x2, mix_y, h2 = _matmul_residual(merged, w_mo, x1, gate[1], 1.0, norm=(ng[2], shift[2], scale[2]),
                                     name="mix_out")

    (loss_row, dx3, dfg, dy2, dgate2), saved2, w_in2, w_out2 = _ffn_fwd(
        x2, h2, gate[2], started[2], started[3], "ffn2", final=(fg, target))

    (dx2, dsh2, dsc2, dng2, dmy, dgate1), sent2 = _ffn_bwd(dx3, dy2, x2, ng[2], scale[2], w_in2, w_out2, saved2,
                                                           "ffn2", (mix_y, gate[1], 1.0))

    dmerged = _matmul(dmy, w_mo, tb=True, out_dtype=BF16, name="mix_out_dx")
    dw_mo = _matmul(merged, dmy, ta=True, out_dtype=BF16, name="mix_out_dw")
    dproj = lax.empty((t, MIXP), BF16)
    dya, dyb, dproj = _merge_bwd(dmerged, ya, yb, proj, dproj, "merge_bwd")
    dob = _matmul(dyb, w_dn, tb=True, out_dtype=F32, name="dn_out_dx")
    dw_dn = _matmul(ob, dyb, ta=True, out_dtype=BF16, name="dn_out_dw")
    do, dproj, dgn = _dn_post_bwd(o, proj, gn, dob, dproj, "dn_post_bwd")
    du, dw, dqk, dqd, dkd, deg = _dn_scan_bwd(u, w, qk, qd, kd, eg, s_saved, do, "dn_scan_bwd")
    dqh, dkh, dvh, dbg = _dn_local_bwd(qh, kh, vh, bg, inv, du, dw, dqk, dqd, dkd, deg, "dn_local_bwd")
    dconv, dproj, dal, ddt = _dn_pre_bwd_act(proj, conv_full, al_row, dt_row, dqh, dkh, dvh, dbg, dproj,
                                             "dn_pre_bwd_act")
    dproj, dcw = _dn_pre_bwd_conv(proj, conv_full, dconv, dproj, "dn_pre_bwd_conv")
    dwin, dpl, dpw, dps, dpp = _pool_bwd_local(proj, pw, ps, w_pp, dya, "pool_bwd_local")
    dproj = _pool_bwd_window(dwin, dpl, dproj, "pool_bwd_window")
    dw_mix = _matmul(dproj, h1, ta=True, out_dtype=BF16, name="mix_in_dw")
    sent1, token = _push_start(
        [_mix_unpad(dw_mix).reshape(NDEV, MIX_RAW // NDEV, D), _cols_to_blocks(dpp.astype(BF16)),
         dw_dn.reshape(NDEV, -1, D), dw_mo.reshape(NDEV, -1, D)], True, "mix_grad_start")
    dh1 = _matmul(dproj, w_mix, out_dtype=F32, name="mix_in_dx", after=token)
    dx1, dsh1, dsc1, dng1, dy0, dgate0 = _norm_mod_bwd(x1, ng[1], scale[1], dh1, dx2, "mix_norm_bwd",
                                                       (saved1[3], gate[0], 0.5))

    (dx0, dsh0, dsc0, dng0), sent0 = _ffn_bwd(dx1, dy0, x0, ng[0], scale[0], w_in1, w_out1, saved1, "ffn1")

    dmod = jnp.concatenate([dsh0, dsc0, dgate0, dsh1, dsc1, dgate1, dsh2, dsc2, dgate2], axis=1).reshape(-1)
    flat = jnp.concatenate([
        dmod, dal[0, NH:2 * NH], ddt[0, NH:2 * NH], dgn.reshape(-1), dps.reshape(-1), dfg.reshape(-1),
        dpw.reshape(-1), jnp.concatenate([dng0, dng1, dng2], axis=0).reshape(-1), dcw.reshape(-1),
        loss_row[0, 0:1]])
    nflat = 90 * D
    flat = jnp.concatenate([flat, jnp.zeros((nflat - flat.shape[0],), F32)]).reshape(90, D)
    sent_small, small_token = _push_start([flat], False, "small_grad_start")

    def small_grads(flat_all):
        tot = _sum_devices(flat_all, F32, "sum_small_grads").reshape(-1)
        dmod_all = flat_all.reshape(NDEV, nflat)[:, :9 * D]
        dmod_cols = lax.dynamic_slice(dmod_all, (0, me * ncol), (NDEV, ncol))
        g_ada_w = _ada_bwd(c_all.T, dmod_cols, "ada_bwd")
        p = 0
        pieces = {}
        for nm, size in (("ada_b", 9 * D), ("a_log", NH), ("dt_bias", NH), ("dn_norm_g", HD), ("pool_scale", PW),
                         ("final_g", D), ("pool_w", 4 * PG * PG), ("norm_g", 3 * D), ("conv_w", 12 * D),
                         ("loss", 1)):
            pieces[nm] = tot[p:p + size]
            p += size
        g_norm = lax.dynamic_slice(pieces["norm_g"].reshape(3, D), (0, me * 128), (3, 128))
        g_conv = lax.dynamic_slice(pieces["conv_w"].reshape(4, 3 * D), (0, me * 384), (4, 384))
        return pieces["loss"][0], {
            "ada_w": g_ada_w.reshape(ada_w.shape), "ada_b": pieces["ada_b"].reshape(ada_b.shape),
            "norm_g": g_norm.reshape(norm_g.shape), "conv_w": g_conv.reshape(conv_w.shape),
            "a_log": pieces["a_log"].reshape(a_log.shape), "dt_bias": pieces["dt_bias"].reshape(dt_bias.shape),
            "dn_norm_g": pieces["dn_norm_g"].reshape(dn_norm_g.shape),
            "pool_w": pieces["pool_w"].reshape(pool_w.shape),
            "pool_scale": pieces["pool_scale"].reshape(pool_scale.shape),
            "final_g": pieces["final_g"].reshape(final_g.shape),
        }

    grads = {}
    weights = {"ada_w": ada_w, "ada_b": ada_b, "norm_g": norm_g, "ffn1_w_in": ffn1_w_in, "ffn1_w_out": ffn1_w_out,
               "ffn2_w_in": ffn2_w_in, "ffn2_w_out": ffn2_w_out, "mix_w_in": mix_w_in, "conv_w": conv_w,
               "a_log": a_log, "dt_bias": dt_bias, "dn_norm_g": dn_norm_g, "pool_w": pool_w,
               "pool_scale": pool_scale, "pool_proj": pool_proj, "dn_proj": dn_proj, "mix_w_out": mix_w_out,
               "final_g": final_g}
    m_in = {"ada_w": m_ada_w, "ada_b": m_ada_b, "norm_g": m_norm_g, "ffn1_w_in": m_ffn1_w_in,
            "ffn1_w_out": m_ffn1_w_out, "ffn2_w_in": m_ffn2_w_in, "ffn2_w_out": m_ffn2_w_out,
            "mix_w_in": m_mix_w_in, "conv_w": m_conv_w, "a_log": m_a_log, "dt_bias": m_dt_bias,
            "dn_norm_g": m_dn_norm_g, "pool_w": m_pool_w, "pool_scale": m_pool_scale, "pool_proj": m_pool_proj,
            "dn_proj": m_dn_proj, "mix_w_out": m_mix_w_out, "final_g": m_final_g}
    v_in = {"ada_w": v_ada_w, "ada_b": v_ada_b, "norm_g": v_norm_g, "ffn1_w_in": v_ffn1_w_in,
            "ffn1_w_out": v_ffn1_w_out, "ffn2_w_in": v_ffn2_w_in, "ffn2_w_out": v_ffn2_w_out,
            "mix_w_in": v_mix_w_in, "conv_w": v_conv_w, "a_log": v_a_log, "dt_bias": v_dt_bias,
            "dn_norm_g": v_dn_norm_g, "pool_w": v_pool_w, "pool_scale": v_pool_scale, "pool_proj": v_pool_proj,
            "dn_proj": v_dn_proj, "mix_w_out": v_mix_w_out, "final_g": v_final_g}

    names = list(weights)
    large = ("ada_w", "ffn1_w_in", "ffn1_w_out", "ffn2_w_in", "ffn2_w_out", "mix_w_in", "pool_proj", "dn_proj",
             "mix_w_out")
    delta, new_m, new_v = {}, {}, {}

    flipped = ("ffn1_w_in", "ffn2_w_in", "mix_w_in")

    def views(nm):
        shp = weights[nm].shape
        two_d = (shp[-2], shp[-1])
        if nm in flipped:
            return (lambda a: a.reshape(two_d).T), (lambda a: a.T.reshape(shp))
        return (lambda a: a.reshape(two_d)), (lambda a: a.reshape(shp))

    def reduce_update(sent, group, after, tag):
        done = []
        for nm, r in zip(group, _push_wait(sent, True, after, f"{tag}_grad_wait")):
            view, back = views(nm)
            g_, d_, m_, v_ = _reduce_adamw(r, view(weights[nm]), view(m_in[nm]), view(v_in[nm]), f"adamw_{nm}")
            grads[nm], delta[nm], new_m[nm], new_v[nm] = back(g_), back(d_), back(m_), back(v_)
            done.append(d_)
        return done

    done = reduce_update(sent2, ("ffn2_w_in", "ffn2_w_out"), small_token, "ffn2")
    done += reduce_update(sent1, ("mix_w_in", "pool_proj", "dn_proj", "mix_w_out"), done, "mix")
    flat_all, = _push_wait(sent_small, False, done, "small_grad_wait")
    loss, small = small_grads(flat_all)
    grads.update(small)
    view, back = views("ada_w")
    done, m_, v_ = _adamw(view(ada_w), view(grads["ada_w"]), view(m_ada_w), view(v_ada_w), "adamw_ada_w")
    delta["ada_w"], new_m["ada_w"], new_v["ada_w"] = back(done), back(m_), back(v_)
    reduce_update(sent0, ("ffn1_w_in", "ffn1_w_out"), done, "ffn1")
    rest = [nm for nm in names if nm not in large]
    total = sum(weights[nm].size for nm in rest)
    padded = -(-total // D) * D

    def pack(tree, fill):
        flat_ = jnp.concatenate([tree[nm].reshape(-1) for nm in rest])
        return jnp.concatenate([flat_, jnp.full((padded - total,), fill, F32)]).reshape(-1, D)

    d_, m_, v_ = _adamw(pack(weights, 0.0), pack(grads, 0.0), pack(m_in, 0.0), pack(v_in, 1.0), "adamw_small")
    p = 0
    for nm in rest:
        size = weights[nm].size
        shp = weights[nm].shape
        delta[nm] = d_.reshape(-1)[p:p + size].reshape(shp)
        new_m[nm] = m_.reshape(-1)[p:p + size].reshape(shp)
        new_v[nm] = v_.reshape(-1)[p:p + size].reshape(shp)
        p += size

    grad_x = dx0.reshape(x.shape)
    return (loss, grad_x, *[grads[nm] for nm in names], *[delta[nm] for nm in names],
            *[new_m[nm] for nm in names], *[new_v[nm] for nm in names])
```

```python
import functools

import jax
import jax.numpy as jnp
from jax import lax
from jax.experimental import pallas as pl
from jax.experimental.pallas import tpu as pltpu

F32 = jnp.float32
BF16 = jnp.bfloat16
SDS = jax.ShapeDtypeStruct
HI = lax.Precision.HIGHEST

D = 1024
FH = 2816
FB = 704
NH = 8
HD = 128
CH = 64
SCAN_CHUNKS = 8
LOCAL_CHUNKS = 2
NDEV = 8
PW = 512
PG = 128
RMS_EPS = 1e-6
L2_EPS = 1e-6
TR = 512
HALO = 16
VMEM_LIMIT = 56 * 1024 * 1024
MATMUL_VMEM = 40 * 1024 * 1024

MIXP = 6912
OFF_Q, OFF_K, OFF_V, OFF_Z, OFF_GP, OFF_GD, OFF_XP, OFF_BA = 0, 1024, 2048, 3072, 4096, 5120, 6144, 6656
MIX_RAW = 6672

ADAM_LR = 0.001
ADAM_B1 = 0.9
ADAM_B2 = 0.999
ADAM_EPS = 1e-08
ADAM_WD = 0.01
ADAM_STEP = 10

NN = (((1,), (0,)), ((), ()))
NT = (((1,), (1,)), ((), ()))
TN = (((0,), (0,)), ((), ()))


def _dg(a, b, dims, prec=None):
    return lax.dot_general(a, b, dims, precision=prec, preferred_element_type=F32)


def _make_dots(prec):
    @jax.custom_vjp
    def nn(a, b):
        return _dg(a, b, NN, prec)

    @jax.custom_vjp
    def nt(a, b):
        return _dg(a, b, NT, prec)

    @jax.custom_vjp
    def tn(a, b):
        return _dg(a, b, TN, prec)

    nn.defvjp(lambda a, b: (nn(a, b), (a, b)), lambda r, d: (nt(d, r[1]), tn(r[0], d)))
    nt.defvjp(lambda a, b: (nt(a, b), (a, b)), lambda r, d: (nn(d, r[1]), tn(d, r[0])))
    tn.defvjp(lambda a, b: (tn(a, b), (a, b)), lambda r, d: (nt(r[1], d), nn(r[0], d)))
    return nn, nt, tn


_nn, _nt, _tn = _make_dots(None)


def _params(sem):
    return pltpu.CompilerParams(dimension_semantics=sem, vmem_limit_bytes=VMEM_LIMIT)


def _sigmoid(x):
    return 1.0 / (1.0 + jnp.exp(-x))


def _silu(x):
    return x * _sigmoid(x)


def _dsilu(x):
    s = _sigmoid(x)
    return s * (1.0 + x * (1.0 - s))


def _pick(n, cands):
    for c in cands:
        if n % c == 0:
            return c
    raise ValueError(f"no tile for {n}")


def _iota(shape, dim):
    return lax.broadcasted_iota(jnp.int32, shape, dim)


def _matmul(a, b, *, ta=False, tb=False, a_blk=False, b_blk=False, o_blk=False, tm=None, tn=None, tk=None,
            out_dtype, name, after=None):
    if a_blk:
        nb, r, cb = a.shape
        if ta:
            k_dim, m_dim, tm = r, nb * cb, cb
        else:
            m_dim, k_dim, tk = r, nb * cb, cb
    else:
        k_dim, m_dim = a.shape if ta else a.shape[::-1]
    if b_blk:
        nb, r, cb = b.shape
        if tb:
            n_dim, tk = r, cb
            assert nb * cb == k_dim
        else:
            n_dim, tn = nb * cb, cb
            assert r == k_dim
    else:
        n_dim = b.shape[0] if tb else b.shape[1]
    tn = tn or _pick(n_dim, (1024, 768, 512, 256, 128))
    out_bytes = jnp.dtype(out_dtype).itemsize

    def vmem(tm_, tk_):
        return 4 * tk_ * (tm_ + tn) + tm_ * tn * (4 + 2 * out_bytes)

    k_cands = [tk] if tk else [c for c in (k_dim, 4096, 3456, 2816, 2304, 2048, 1024, 512, 256)
                               if c <= k_dim and k_dim % c == 0]
    m_cands = [tm] if tm else [c for c in (2048, 1024, 768, 512, 256, 128) if m_dim % c == 0]
    base = next((c for c in m_cands if c <= 1024), m_cands[-1])
    tk = next((c for c in k_cands if vmem(base, c) <= MATMUL_VMEM), k_cands[-1])
    tm = next((c for c in m_cands if vmem(c, tk) <= MATMUL_VMEM), m_cands[-1])
    nk = k_dim // tk
    dims = ((((0,) if ta else (1,)), ((1,) if tb else (0,))), ((), ()))

    def body(a_ref, b_ref, *rest):
        o_ref, acc_ref = rest[-2:]
        k = pl.program_id(2)

        @pl.when(k == 0)
        def _():
            acc_ref[...] = jnp.zeros_like(acc_ref)

        acc_ref[...] += lax.dot_general(a_ref[...].astype(BF16), b_ref[...].astype(BF16), dims,
                                        preferred_element_type=F32)

        @pl.when(k == nk - 1)
        def _():
            o_ref[...] = acc_ref[...].astype(o_ref.dtype)

    if a_blk:
        a_spec = (pl.BlockSpec((None, tk, tm), lambda i, j, k: (i, k, 0)) if ta
                  else pl.BlockSpec((None, tm, tk), lambda i, j, k: (k, i, 0)))
    else:
        a_spec = (pl.BlockSpec((tk, tm), lambda i, j, k: (k, i)) if ta
                  else pl.BlockSpec((tm, tk), lambda i, j, k: (i, k)))
    if b_blk:
        b_spec = (pl.BlockSpec((None, tn, tk), lambda i, j, k: (k, j, 0)) if tb
                  else pl.BlockSpec((None, tk, tn), lambda i, j, k: (j, k, 0)))
    else:
        b_spec = (pl.BlockSpec((tn, tk), lambda i, j, k: (j, k)) if tb
                  else pl.BlockSpec((tk, tn), lambda i, j, k: (k, j)))
    if o_blk:
        o_spec = pl.BlockSpec((None, tm, tn), lambda i, j, k: (j, i, 0))
        o_shape = SDS((n_dim // tn, m_dim, tn), out_dtype)
    else:
        o_spec = pl.BlockSpec((tm, tn), lambda i, j, k: (i, j))
        o_shape = SDS((m_dim, n_dim), out_dtype)
    return pl.pallas_call(
        body, grid=(m_dim // tm, n_dim // tn, nk),
        in_specs=[a_spec, b_spec] + ([] if after is None else [pl.BlockSpec(memory_space=pl.ANY)]),
        out_specs=o_spec,
        out_shape=o_shape,
        scratch_shapes=[pltpu.VMEM((tm, tn), F32)],
        compiler_params=_params(("parallel", "parallel", "arbitrary")),
        name=name,
    )(a, b, *([] if after is None else [after]))


def _matmul_residual(a, b, x, gate, coef, *, a_blk=False, norm=None, name, after=None):
    m_dim = a.shape[-2]
    tm = _pick(m_dim, (1024, 512))
    if a_blk:
        nk, _, tk = a.shape
        a_spec = pl.BlockSpec((None, tm, tk), lambda i, k: (k, i, 0))
    else:
        tk = a.shape[1]
        nk = 1
        a_spec = pl.BlockSpec((tm, tk), lambda i, k: (i, 0))
    extra = [] if after is None else [after]
    vecs = [gate] + (list(norm) if norm else [])

    def body(a_ref, b_ref, x_ref, gate_ref, *rest):
        vec_refs = rest[:len(vecs) - 1]
        outs = rest[len(vecs) - 1 + len(extra):]
        acc_ref = outs[-1]
        k = pl.program_id(1)

        @pl.when(k == 0)
        def _():
            acc_ref[...] = jnp.zeros_like(acc_ref)

        acc_ref[...] += _dg(a_ref[...], b_ref[...], NN)

        @pl.when(k == nk - 1)
        def _():
            y = acc_ref[...]
            xn = x_ref[...] + (coef * gate_ref[...]) * y
            outs[0][...] = xn
            outs[1][...] = y.astype(outs[1].dtype)
            if norm:
                g_ref, sh_ref, sc_ref = vec_refs
                r = lax.rsqrt(jnp.mean(xn * xn, axis=-1, keepdims=True) + RMS_EPS)
                outs[2][...] = (((xn * r) * g_ref[...]) * (1.0 + sc_ref[...]) + sh_ref[...]).astype(outs[2].dtype)

    row = pl.BlockSpec((tm, D), lambda i, k: (i, 0))
    vec = pl.BlockSpec((1, D), lambda i, k: (0, 0))
    return pl.pallas_call(
        body, grid=(m_dim // tm, nk),
        in_specs=[a_spec, pl.BlockSpec((tk, D), lambda i, k: (k, 0)), row] + [vec] * len(vecs)
        + [pl.BlockSpec(memory_space=pl.ANY)] * len(extra),
        out_specs=[row] * (3 if norm else 2),
        out_shape=[SDS((m_dim, D), F32), SDS((m_dim, D), BF16)] + ([SDS((m_dim, D), BF16)] if norm else []),
        scratch_shapes=[pltpu.VMEM((tm, D), F32)],
        compiler_params=_params(("parallel", "arbitrary")), name=name,
    )(a, b, x, *vecs, *extra)


def _matmul_residual_loss(a, b, x, gate, coef, fg, target, name):
    nk, m_dim, tk = a.shape
    tm = _pick(m_dim, (1024, 512))
    nt = m_dim // tm

    def body(a_ref, b_ref, x_ref, gate_ref, g_ref, t_ref, loss_ref, dx_ref, dg_ref, dy_ref, dgate_ref,
             acc_ref, sq_ref):
        i, k = pl.program_id(0), pl.program_id(1)

        @pl.when(k == 0)
        def _():
            acc_ref[...] = jnp.zeros_like(acc_ref)

        @pl.when(jnp.logical_and(i == 0, k == 0))
        def _():
            sq_ref[...] = jnp.zeros_like(sq_ref)
            dg_ref[...] = jnp.zeros_like(dg_ref)
            dgate_ref[...] = jnp.zeros_like(dgate_ref)

        acc_ref[...] += _dg(a_ref[...], b_ref[...], NN)

        @pl.when(k == nk - 1)
        def _():
            y = acc_ref[...]
            scaled_gate = coef * gate_ref[...]
            xn = x_ref[...] + scaled_gate * y
            gv = g_ref[...]
            r = lax.rsqrt(jnp.mean(xn * xn, axis=-1, keepdims=True) + RMS_EPS)
            n = xn * r
            err = n * gv - t_ref[...]
            sq_ref[...] += jnp.sum(err * err, axis=0, keepdims=True)
            dout = err * (1.0 / D)
            dg_ref[...] += jnp.sum(dout * n, axis=0, keepdims=True)
            dn = dout * gv
            dxv = r * (dn - n * jnp.mean(dn * n, axis=-1, keepdims=True))
            dx_ref[...] = dxv
            dy_ref[...] = (scaled_gate * dxv).astype(dy_ref.dtype)
            dgate_ref[...] += jnp.sum((coef * dxv) * y, axis=0, keepdims=True)

        @pl.when(jnp.logical_and(i == nt - 1, k == nk - 1))
        def _():
            tot = jnp.sum(sq_ref[...], axis=1, keepdims=True) * (0.5 / D)
            loss_ref[...] = jnp.broadcast_to(tot, loss_ref.shape)

    row = pl.BlockSpec((tm, D), lambda i, k: (i, 0))
    vec = pl.BlockSpec((1, D), lambda i, k: (0, 0))
    return pl.pallas_call(
        body, grid=(nt, nk),
        in_specs=[pl.BlockSpec((None, tm, tk), lambda i, k: (k, i, 0)), pl.BlockSpec((tk, D), lambda i, k: (k, 0)),
                  row, vec, vec, row],
        out_specs=[pl.BlockSpec((1, 128), lambda i, k: (0, 0)), row, vec, row, vec],
        out_shape=[SDS((1, 128), F32), SDS((m_dim, D), F32), SDS((1, D), F32), SDS((m_dim, D), BF16),
                   SDS((1, D), F32)],
        scratch_shapes=[pltpu.VMEM((tm, D), F32), pltpu.VMEM((1, D), F32)],
        compiler_params=_params(("arbitrary", "arbitrary")), name=name,
    )(a, b, x, gate, fg, target)


def _row(width, col=0):
    return pl.BlockSpec((TR, width), lambda i: (i, col))


def _vec(width):
    return pl.BlockSpec((1, width), lambda i: (0, 0))


def _norm_mod_fwd(x, g, shift, scale, name, after=None):
    t = x.shape[0]
    extra = [] if after is None else [after]

    def body(x_ref, g_ref, sh_ref, sc_ref, *rest):
        o_ref = rest[-1]
        xv = x_ref[...]
        r = lax.rsqrt(jnp.mean(xv * xv, axis=-1, keepdims=True) + RMS_EPS)
        o_ref[...] = (((xv * r) * g_ref[...]) * (1.0 + sc_ref[...]) + sh_ref[...]).astype(o_ref.dtype)

    return pl.pallas_call(
        body, grid=(t // TR,),
        in_specs=[_row(D), _vec(D), _vec(D), _vec(D)] + [pl.BlockSpec(memory_space=pl.ANY)] * len(extra),
        out_specs=_row(D),
        out_shape=SDS((t, D), BF16), compiler_params=_params(("parallel",)), name=name,
    )(x, g, shift, scale, *extra)


def _residual_branch_bwd(dxv, y_ref, gate_ref, coef, dy_ref, dgate_ref):
    dy_ref[...] = ((coef * gate_ref[...]) * dxv).astype(dy_ref.dtype)
    dgate_ref[...] += jnp.sum((coef * dxv) * y_ref[...], axis=0, keepdims=True)


def _norm_mod_bwd(x, g, scale, dh, dx_in, name, below=None):
    t = x.shape[0]
    lower = [] if below is None else list(below[:2])

    def body(x_ref, g_ref, sc_ref, dh_ref, dxi_ref, *rest):
        dx_ref, dsh_ref, dsc_ref, dg_ref = rest[len(lower):len(lower) + 4]

        @pl.when(pl.program_id(0) == 0)
        def _():
            for ref in rest[len(lower) + 1:]:
                if ref.shape[0] == 1:
                    ref[...] = jnp.zeros_like(ref)

        xv = x_ref[...]
        gv = g_ref[...]
        dh = dh_ref[...]
        r = lax.rsqrt(jnp.mean(xv * xv, axis=-1, keepdims=True) + RMS_EPS)
        n = xv * r
        dsh_ref[...] += jnp.sum(dh, axis=0, keepdims=True)
        dsc_ref[...] += jnp.sum(dh * (n * gv), axis=0, keepdims=True)
        tt = dh * (1.0 + sc_ref[...])
        dg_ref[...] += jnp.sum(tt * n, axis=0, keepdims=True)
        dn = tt * gv
        dxv = dxi_ref[...] + r * (dn - n * jnp.mean(dn * n, axis=-1, keepdims=True))
        dx_ref[...] = dxv
        if below is not None:
            _residual_branch_bwd(dxv, rest[0], rest[1], below[2], rest[-2], rest[-1])

    more_in = [] if below is None else [_row(D), _vec(D)]
    more_out = [] if below is None else [_row(D), _vec(D)]
    more_shape = [] if below is None else [SDS((t, D), BF16), SDS((1, D), F32)]
    return pl.pallas_call(
        body, grid=(t // TR,), in_specs=[_row(D), _vec(D), _vec(D), _row(D), _row(D)] + more_in,
        out_specs=[_row(D), _vec(D), _vec(D), _vec(D)] + more_out,
        out_shape=[SDS((t, D), F32), SDS((1, D), F32), SDS((1, D), F32), SDS((1, D), F32)] + more_shape,
        compiler_params=_params(("arbitrary",)), name=name,
    )(x, g, scale, dh, dx_in, *lower)


def _swiglu_up(h, w_in, name, after=None):
    t = h.shape[0]
    tm = _pick(t, (1024, 512, 256))
    half = NDEV // 2
    extra = [] if after is None else [after]

    def body(h_ref, wg_ref, wu_ref, *rest):
        u_ref, a_ref = rest[-2:]
        hv = h_ref[...]
        gate = _dg(hv, wg_ref[...], NT)
        up = _dg(hv, wu_ref[...], NT)
        u_ref[0] = gate.astype(u_ref.dtype)
        u_ref[1] = up.astype(u_ref.dtype)
        a_ref[...] = (_silu(gate) * up).astype(a_ref.dtype)

    return pl.pallas_call(
        body, grid=(t // tm, half),
        in_specs=[pl.BlockSpec((tm, D), lambda i, j: (i, 0)),
                  pl.BlockSpec((FB, D), lambda i, j: (j, 0)),
                  pl.BlockSpec((FB, D), lambda i, j: (j + half, 0))]
        + [pl.BlockSpec(memory_space=pl.ANY)] * len(extra),
        out_specs=[pl.BlockSpec((2, None, tm, FB), lambda i, j: (0, j, i, 0)),
                   pl.BlockSpec((None, tm, FB), lambda i, j: (j, i, 0))],
        out_shape=[SDS((2, half, t, FB), BF16), SDS((half, t, FB), BF16)],
        compiler_params=_params(("parallel", "parallel")), name=name,
    )(h, w_in, w_in, *extra)


def _swiglu_down_bwd(dy, w_out, u, name, after=None):
    t = dy.shape[0]
    tm = _pick(t, (1024, 512, 256))
    half = NDEV // 2
    extra = [] if after is None else [after]
    pair = pl.BlockSpec((2, None, tm, FB), lambda i, j: (0, j, i, 0))

    def body(dy_ref, w_ref, u_ref, *rest):
        o_ref = rest[-1]
        da = _dg(dy_ref[...], w_ref[...], NT)
        gate = u_ref[0].astype(F32)
        o_ref[0] = (da * u_ref[1].astype(F32) * _dsilu(gate)).astype(o_ref.dtype)
        o_ref[1] = (da * _silu(gate)).astype(o_ref.dtype)

    return pl.pallas_call(
        body, grid=(t // tm, half),
        in_specs=[pl.BlockSpec((tm, D), lambda i, j: (i, 0)), pl.BlockSpec((FB, D), lambda i, j: (j, 0)), pair]
        + [pl.BlockSpec(memory_space=pl.ANY)] * len(extra),
        out_specs=pair, out_shape=SDS((2, half, t, FB), BF16),
        compiler_params=_params(("parallel", "parallel")), name=name,
    )(dy, w_out, u, *extra)


def _halo_prev(width, col):
    per = TR // HALO
    return pl.BlockSpec((HALO, width), lambda i: (jnp.maximum(i * per - 1, 0), col))


def _halo_next(width, col, nt):
    per = TR // HALO
    return pl.BlockSpec((HALO, width), lambda i: (jnp.minimum((i + 1) * per, nt * per - 1), col))


def _pool_windows(ext, tile_index):
    rows = _iota((TR, PG), 0) + tile_index * TR + 1
    pooled, counts = [], []
    for gi in range(4):
        w = 2 << gi
        e = ext[:, gi * PG:(gi + 1) * PG]
        s = e
        step = 1
        while step < w:
            s = s + pltpu.roll(s, step, 0)
            step *= 2
        cnt = jnp.minimum(rows, w).astype(F32)
        pooled.append(s[HALO:] / cnt - e[HALO:])
        counts.append(cnt)
    return pooled, counts


def _pool_fwd(proj, pool_w, pool_scale, pool_proj, name):
    t = proj.shape[0]
    xcol = OFF_XP // PW

    def body(x_ref, h_ref, pw_ref, ps_ref, pp_ref, o_ref):
        i = pl.program_id(0)
        halo = jnp.where(i > 0, h_ref[...], 0.0)
        ext = jnp.concatenate([halo, x_ref[...]], axis=0)
        pooled, _ = _pool_windows(ext, i)
        mixed = [_dg(pooled[g].astype(BF16), pw_ref[g].astype(BF16), NN) for g in range(4)]
        ypre = jnp.concatenate(mixed, axis=1) * ps_ref[...]
        o_ref[...] = _dg(ypre.astype(BF16), pp_ref[...], NN)

    return pl.pallas_call(
        body, grid=(t // TR,),
        in_specs=[_row(PW, xcol), _halo_prev(PW, xcol),
                  pl.BlockSpec((4, PG, PG), lambda i: (0, 0, 0)), _vec(PW),
                  pl.BlockSpec((PW, D), lambda i: (0, 0))],
        out_specs=_row(D), out_shape=SDS((t, D), F32),
        compiler_params=_params(("parallel",)), name=name,
    )(proj, proj, pool_w, pool_scale, pool_proj)


def _pool_bwd_local(proj, pool_w, pool_scale, pool_proj, dya, name):
    t = proj.shape[0]
    xcol = OFF_XP // PW

    def body(x_ref, h_ref, pw_ref, ps_ref, pp_ref, dya_ref, dwin_ref, dpl_ref, dpw_ref, dps_ref, dpp_ref):
        i = pl.program_id(0)

        @pl.when(i == 0)
        def _():
            dpw_ref[...] = jnp.zeros_like(dpw_ref)
            dps_ref[...] = jnp.zeros_like(dps_ref)
            dpp_ref[...] = jnp.zeros_like(dpp_ref)

        halo = jnp.where(i > 0, h_ref[...], 0.0)
        ext = jnp.concatenate([halo, x_ref[...]], axis=0)
        pooled, counts = _pool_windows(ext, i)
        mixed = jnp.concatenate(
            [_dg(pooled[g].astype(BF16), pw_ref[g].astype(BF16), NN) for g in range(4)], axis=1)
        ps = ps_ref[...]
        ypre = mixed * ps
        dyab = dya_ref[...].astype(BF16)
        dypre = _dg(dyab, pp_ref[...], NT)
        dpp_ref[...] += _dg(ypre.astype(BF16), dyab, TN)
        dps_ref[...] += jnp.sum(dypre * mixed, axis=0, keepdims=True)
        dmixed = dypre * ps
        for g in range(4):
            dm = dmixed[:, g * PG:(g + 1) * PG].astype(BF16)
            dpw_ref[g] += _dg(pooled[g].astype(BF16), dm, TN)
            dpooled = _dg(dm, pw_ref[g].astype(BF16), NT)
            dwin_ref[:, g * PG:(g + 1) * PG] = dpooled / counts[g]
            dpl_ref[:, g * PG:(g + 1) * PG] = dpooled

    return pl.pallas_call(
        body, grid=(t // TR,),
        in_specs=[_row(PW, xcol), _halo_prev(PW, xcol),
                  pl.BlockSpec((4, PG, PG), lambda i: (0, 0, 0)), _vec(PW),
                  pl.BlockSpec((PW, D), lambda i: (0, 0)), _row(D)],
        out_specs=[_row(PW), _row(PW), pl.BlockSpec((4, PG, PG), lambda i: (0, 0, 0)), _vec(PW),
                   pl.BlockSpec((PW, D), lambda i: (0, 0))],
        out_shape=[SDS((t, PW), F32), SDS((t, PW), F32), SDS((4, PG, PG), F32), SDS((1, PW), F32),
                   SDS((PW, D), F32)],
        compiler_params=_params(("arbitrary",)), name=name,
    )(proj, proj, pool_w, pool_scale, pool_proj, dya)


def _pool_bwd_window(dwin, dpl, dproj, name):
    t = dwin.shape[0]
    nt = t // TR
    ext_rows = TR + HALO

    def body(dw_ref, h_ref, dp_ref, _, o_ref):
        i = pl.program_id(0)
        halo = jnp.where(i < nt - 1, h_ref[...], 0.0)
        ext = jnp.concatenate([dw_ref[...], halo], axis=0)
        for gi in range(4):
            w = 2 << gi
            s = ext[:, gi * PG:(gi + 1) * PG]
            step = 1
            while step < w:
                s = s + pltpu.roll(s, ext_rows - step, 0)
                step *= 2
            o_ref[:, gi * PG:(gi + 1) * PG] = (s[:TR] - dp_ref[:, gi * PG:(gi + 1) * PG]).astype(o_ref.dtype)

    return pl.pallas_call(
        body, grid=(nt,),
        in_specs=[_row(PW), _halo_next(PW, 0, nt), _row(PW), pl.BlockSpec(memory_space=pl.ANY)],
        out_specs=_into(PW, OFF_XP), out_shape=SDS(dproj.shape, dproj.dtype), input_output_aliases={3: 0},
        compiler_params=_params(("parallel",)), name=name,
    )(dwin, dwin, dpl, dproj)


def _conv_group(ext, cw_ref, cols):
    acc = cw_ref[3:4, cols] * ext
    for j in range(3):
        acc = acc + cw_ref[j:j + 1, cols] * pltpu.roll(ext, 3 - j, 0)
    return acc[HALO:]


def _gate_terms(raw, al, dt):
    beta = _sigmoid(raw)
    xg = raw + dt
    sp = jnp.maximum(xg, 0.0) + jnp.log(1.0 + jnp.exp(-jnp.abs(xg)))
    g = -jnp.exp(al) * sp
    return beta, g, _sigmoid(xg)


def _dn_pre_fwd(proj, conv_w, al_row, dt_row, name):
    t = proj.shape[0]

    def body(x_ref, h_ref, cw_ref, ba_ref, al_ref, dt_ref, q_ref, k_ref, v_ref, bg_ref):
        i = pl.program_id(0)
        keep = i > 0
        for grp in range(24):
            cols = slice(grp * HD, (grp + 1) * HD)
            ext = jnp.concatenate([jnp.where(keep, h_ref[:, cols], 0.0), x_ref[:, cols]], axis=0)
            s = _silu(_conv_group(ext, cw_ref, cols))
            seg, head = divmod(grp, NH)
            hc = slice(head * HD, (head + 1) * HD)
            if seg == 0:
                q_ref[:, hc] = s * lax.rsqrt(jnp.sum(s * s, axis=-1, keepdims=True) + L2_EPS) * (HD ** -0.5)
            elif seg == 1:
                k_ref[:, hc] = s * lax.rsqrt(jnp.sum(s * s, axis=-1, keepdims=True) + L2_EPS)
            else:
                v_ref[:, hc] = s
        lane = _iota((TR, 128), 1)
        rowc = _iota((TR, 128), 0) % CH
        beta, g, _ = _gate_terms(ba_ref[...], al_ref[...], dt_ref[...])
        step = 1
        while step < CH:
            g = g + jnp.where(rowc >= step, pltpu.roll(g, step, 0), 0.0)
            step *= 2
        bg_ref[...] = jnp.where(lane < NH, beta, jnp.where(lane < 2 * NH, g, 0.0))

    return pl.pallas_call(
        body, grid=(t // TR,),
        in_specs=[_row(3 * D, 0), _halo_prev(3 * D, 0), pl.BlockSpec((4, 3 * D), lambda i: (0, 0)),
                  _row(128, OFF_BA // 128), _vec(128), _vec(128)],
        out_specs=[_row(D), _row(D), _row(D), _row(128)],
        out_shape=[SDS((t, D), F32), SDS((t, D), F32), SDS((t, D), F32), SDS((t, 128), F32)],
        compiler_params=_params(("parallel",)), name=name,
    )(proj, proj, conv_w, proj, al_row, dt_row)


def _dn_pre_bwd_act(proj, conv_w, al_row, dt_row, dq, dk, dv, dbg, dproj, name):
    t = proj.shape[0]

    def body(x_ref, h_ref, cw_ref, ba_ref, al_ref, dt_ref, dq_ref, dk_ref, dv_ref, dbg_ref, _,
             dc_ref, draw_ref, dal_ref, ddt_ref):
        i = pl.program_id(0)

        @pl.when(i == 0)
        def _():
            dal_ref[...] = jnp.zeros_like(dal_ref)
            ddt_ref[...] = jnp.zeros_like(ddt_ref)

        keep = i > 0
        for grp in range(24):
            cols = slice(grp * HD, (grp + 1) * HD)
            ext = jnp.concatenate([jnp.where(keep, h_ref[:, cols], 0.0), x_ref[:, cols]], axis=0)
            cv = _conv_group(ext, cw_ref, cols)
            seg, head = divmod(grp, NH)
            hc = slice(head * HD, (head + 1) * HD)
            if seg == 2:
                ds = dv_ref[:, hc]
            else:
                s = _silu(cv)
                r = lax.rsqrt(jnp.sum(s * s, axis=-1, keepdims=True) + L2_EPS)
                dy = dq_ref[:, hc] if seg == 0 else dk_ref[:, hc]
                c = (HD ** -0.5) if seg == 0 else 1.0
                ds = (c * r) * (dy - s * ((r * r) * jnp.sum(dy * s, axis=-1, keepdims=True)))
            dc_ref[:, cols] = ds * _dsilu(cv)
        lane = _iota((TR, 128), 1)
        rowc = _iota((TR, 128), 0) % CH
        isb = lane < NH
        isg = jnp.logical_and(lane >= NH, lane < 2 * NH)
        beta, g, sg = _gate_terms(ba_ref[...], al_ref[...], dt_ref[...])
        dbgv = dbg_ref[...]
        dg = dbgv
        step = 1
        while step < CH:
            dg = dg + jnp.where(rowc < CH - step, pltpu.roll(dg, TR - step, 0), 0.0)
            step *= 2
        da_raw = dg * (-jnp.exp(al_ref[...])) * sg
        draw = jnp.where(isb, dbgv * beta * (1.0 - beta), jnp.where(isg, da_raw, 0.0))
        draw_ref[:, :128] = draw.astype(draw_ref.dtype)
        draw_ref[:, 128:] = jnp.zeros((TR, MIXP - OFF_BA - 128), draw_ref.dtype)
        dal_ref[...] += jnp.sum(jnp.where(isg, dg * g, 0.0), axis=0, keepdims=True)
        ddt_ref[...] += jnp.sum(jnp.where(isg, da_raw, 0.0), axis=0, keepdims=True)

    return pl.pallas_call(
        body, grid=(t // TR,),
        in_specs=[_row(3 * D, 0), _halo_prev(3 * D, 0), pl.BlockSpec((4, 3 * D), lambda i: (0, 0)),
                  _row(128, OFF_BA // 128), _vec(128), _vec(128), _row(D), _row(D), _row(D), _row(128),
                  pl.BlockSpec(memory_space=pl.ANY)],
        out_specs=[_row(3 * D), _into(MIXP - OFF_BA, OFF_BA), _vec(128), _vec(128)],
        out_shape=[SDS((t, 3 * D), F32), SDS(dproj.shape, dproj.dtype), SDS((1, 128), F32), SDS((1, 128), F32)],
        input_output_aliases={10: 1},
        compiler_params=_params(("arbitrary",)), name=name,
    )(proj, proj, conv_w, proj, al_row, dt_row, dq, dk, dv, dbg, dproj)


def _dn_pre_bwd_conv(proj, conv_w, dconv, dproj, name):
    t = proj.shape[0]
    nt = t // TR
    ext_rows = TR + HALO

    def body(x_ref, h_ref, cw_ref, dc_ref, dn_ref, _, dx_ref, dcw_ref):
        i = pl.program_id(0)

        @pl.when(i == 0)
        def _():
            dcw_ref[...] = jnp.zeros_like(dcw_ref)

        keep_prev = i > 0
        keep_next = i < nt - 1
        for grp in range(24):
            cols = slice(grp * HD, (grp + 1) * HD)
            dct = dc_ref[:, cols]
            dext = jnp.concatenate([dct, jnp.where(keep_next, dn_ref[:, cols], 0.0)], axis=0)
            acc = cw_ref[3:4, cols] * dext
            for j in range(3):
                acc = acc + cw_ref[j:j + 1, cols] * pltpu.roll(dext, ext_rows - (3 - j), 0)
            dx_ref[:, cols] = acc[:TR].astype(dx_ref.dtype)
            xext = jnp.concatenate([jnp.where(keep_prev, h_ref[:, cols], 0.0), x_ref[:, cols]], axis=0)
            for j in range(4):
                xs = xext if j == 3 else pltpu.roll(xext, 3 - j, 0)
                dcw_ref[j:j + 1, cols] += jnp.sum(xs[HALO:] * dct, axis=0, keepdims=True)

    return pl.pallas_call(
        body, grid=(nt,),
        in_specs=[_row(3 * D, 0), _halo_prev(3 * D, 0), pl.BlockSpec((4, 3 * D), lambda i: (0, 0)),
                  _row(3 * D), _halo_next(3 * D, 0, nt), pl.BlockSpec(memory_space=pl.ANY)],
        out_specs=[_into(3 * D, OFF_Q), pl.BlockSpec((4, 3 * D), lambda i: (0, 0))],
        out_shape=[SDS(dproj.shape, dproj.dtype), SDS((4, 3 * D), F32)],
        input_output_aliases={5: 0},
        compiler_params=_params(("arbitrary",)), name=name,
    )(proj, proj, conv_w, dconv, dconv, dproj)


def _dn_post_fwd(o, proj, gn, name):
    t = o.shape[0]

    def body(o_ref, z_ref, g_ref, out_ref):
        gv = g_ref[...]
        for h in range(NH):
            hc = slice(h * HD, (h + 1) * HD)
            ov = o_ref[:, hc]
            r = lax.rsqrt(jnp.mean(ov * ov, axis=-1, keepdims=True) + RMS_EPS)
            out_ref[:, hc] = (((ov * r) * gv) * _silu(z_ref[:, hc])).astype(out_ref.dtype)

    return pl.pallas_call(
        body, grid=(t // TR,), in_specs=[_row(D), _row(D, OFF_Z // D), _vec(HD)], out_specs=_row(D),
        out_shape=SDS((t, D), BF16), compiler_params=_params(("parallel",)), name=name,
    )(o, proj, gn)


def _dn_post_bwd(o, proj, gn, dob, dproj, name):
    t = o.shape[0]

    def body(o_ref, z_ref, g_ref, d_ref, _, do_ref, dz_ref, dg_ref):
        @pl.when(pl.program_id(0) == 0)
        def _():
            dg_ref[...] = jnp.zeros_like(dg_ref)

        gv = g_ref[...]
        acc = jnp.zeros((1, HD), F32)
        for h in range(NH):
            hc = slice(h * HD, (h + 1) * HD)
            ov = o_ref[:, hc]
            zv = z_ref[:, hc]
            dv = d_ref[:, hc]
            r = lax.rsqrt(jnp.mean(ov * ov, axis=-1, keepdims=True) + RMS_EPS)
            n = ov * r
            dz_ref[:, hc] = (dv * (n * gv) * _dsilu(zv)).astype(dz_ref.dtype)
            dng = dv * _silu(zv)
            acc = acc + jnp.sum(dng * n, axis=0, keepdims=True)
            dn = dng * gv
            do_ref[:, hc] = r * (dn - n * jnp.mean(dn * n, axis=-1, keepdims=True))
        dg_ref[...] += acc

    return pl.pallas_call(
        body, grid=(t // TR,),
        in_specs=[_row(D), _row(D, OFF_Z // D), _vec(HD), _row(D), pl.BlockSpec(memory_space=pl.ANY)],
        out_specs=[_row(D), _into(D, OFF_Z), _vec(HD)],
        out_shape=[SDS((t, D), F32), SDS(dproj.shape, dproj.dtype), SDS((1, HD), F32)],
        input_output_aliases={4: 1},
        compiler_params=_params(("arbitrary",)), name=name,
    )(o, proj, gn, dob, dproj)


def _merge_out(ya, yb, proj, w_out, x, gate, norm, name):
    t = ya.shape[0]

    def body(a_ref, b_ref, gp_ref, gd_ref, w_ref, x_ref, gate_ref, g_ref, sh_ref, sc_ref,
             m_ref, xn_ref, y_ref, h_ref):
        merged = (_sigmoid(gp_ref[...]) * a_ref[...] + _sigmoid(gd_ref[...]) * b_ref[...]).astype(m_ref.dtype)
        m_ref[...] = merged
        y = _dg(merged, w_ref[...], NN)
        xn = x_ref[...] + gate_ref[...] * y
        xn_ref[...] = xn
        y_ref[...] = y.astype(y_ref.dtype)
        r = lax.rsqrt(jnp.mean(xn * xn, axis=-1, keepdims=True) + RMS_EPS)
        h_ref[...] = (((xn * r) * g_ref[...]) * (1.0 + sc_ref[...]) + sh_ref[...]).astype(h_ref.dtype)

    return pl.pallas_call(
        body, grid=(t // TR,),
        in_specs=[_row(D), _row(D), _row(D, OFF_GP // D), _row(D, OFF_GD // D),
                  pl.BlockSpec((D, D), lambda i: (0, 0)), _row(D), _vec(D), _vec(D), _vec(D), _vec(D)],
        out_specs=[_row(D)] * 4,
        out_shape=[SDS((t, D), BF16), SDS((t, D), F32), SDS((t, D), BF16), SDS((t, D), BF16)],
        compiler_params=_params(("parallel",)), name=name,
    )(ya, yb, proj, proj, w_out, x, gate, *norm)


def _into(width, offset):
    assert offset % width == 0
    return pl.BlockSpec((TR, width), lambda i: (i, offset // width))


def _merge_bwd(dm, ya, yb, proj, dproj, name):
    t = ya.shape[0]

    def body(d_ref, a_ref, b_ref, gp_ref, gd_ref, _, da_ref, db_ref, dg_ref):
        dv = d_ref[...]
        sp = _sigmoid(gp_ref[...])
        sd = _sigmoid(gd_ref[...])
        da_ref[...] = (dv * sp).astype(da_ref.dtype)
        db_ref[...] = (dv * sd).astype(db_ref.dtype)
        dg_ref[:, :D] = (dv * a_ref[...] * sp * (1.0 - sp)).astype(dg_ref.dtype)
        dg_ref[:, D:] = (dv * b_ref[...] * sd * (1.0 - sd)).astype(dg_ref.dtype)

    return pl.pallas_call(
        body, grid=(t // TR,),
        in_specs=[_row(D), _row(D), _row(D), _row(D, OFF_GP // D), _row(D, OFF_GD // D),
                  pl.BlockSpec(memory_space=pl.ANY)],
        out_specs=[_row(D), _row(D), _into(2 * D, OFF_GP)],
        out_shape=[SDS((t, D), BF16), SDS((t, D), BF16), SDS(dproj.shape, dproj.dtype)],
        input_output_aliases={5: 2},
        compiler_params=_params(("parallel",)), name=name,
    )(dm, ya, yb, proj, proj, dproj)


def _split2(x):
    hi = x.astype(BF16)
    return hi, (x - hi.astype(F32)).astype(BF16)


def _dot3(a, b, dims):
    ah, al = _split2(a)
    bh, bl = _split2(b)
    return _dg(ah, bh, dims) + (_dg(ah, bl, dims) + _dg(al, bh, dims))


def _neumann_inverses(mats):
    ri = _iota((CH, CH), 0)
    ci = _iota((CH, CH), 1)
    eye = jnp.where(ri == ci, 1.0, 0.0).astype(F32)
    xs = [-a for a in mats]
    ps = [eye + x for x in xs]
    for _ in range(5):
        xs = [_dot3(x, x, NN) for x in xs]
        ps = [p + _dot3(p, x, NN) for p, x in zip(ps, xs)]
    return ps


def _solve_with(inv):
    @jax.custom_vjp
    def solve(a, rhs):
        return _dot3(inv, rhs, NN)

    def fwd(a, rhs):
        sol = _dot3(inv, rhs, NN)
        return sol, sol

    def bwd(sol, d):
        drhs = _dot3(inv, d, TN)
        return -_dot3(drhs, sol, NT), drhs

    solve.defvjp(fwd, bwd)
    return solve


@jax.custom_vjp
def _rows_to_lanes(g64):
    ri = _iota((CH, CH), 0)
    ci = _iota((CH, CH), 1)
    diag = jnp.where(ri == ci, g64, 0.0)
    ones = jnp.ones((CH, CH), BF16)
    hi = diag.astype(BF16)
    rem = diag - hi.astype(F32)
    mid = rem.astype(BF16)
    lo = (rem - mid.astype(F32)).astype(BF16)
    return _dg(ones, hi, NN) + (_dg(ones, mid, NN) + _dg(ones, lo, NN))


def _rows_to_lanes_bwd(_, d):
    ri = _iota((CH, CH), 0)
    ci = _iota((CH, CH), 1)
    return (jnp.where(ri == ci, jnp.broadcast_to(jnp.sum(d, axis=0, keepdims=True), (CH, CH)), 0.0),)


_rows_to_lanes.defvjp(lambda g64: (_rows_to_lanes(g64), None), _rows_to_lanes_bwd)


def _chunk_local(solve_all, q, k, v, g128, g64, gl128, b128, b64):
    ri = _iota((CH, CH), 0)
    ci = _iota((CH, CH), 1)
    causal = ri >= ci
    strict = ri > ci
    gj = [_rows_to_lanes(g) for g in g64]
    decay = [jnp.where(causal, jnp.exp(jnp.where(causal, g - t, 0.0)), 0.0) for g, t in zip(g64, gj)]
    kk = [_nt(x, x) for x in k]
    a = [jnp.where(strict, b * m * dc, 0.0) for b, m, dc in zip(b64, kk, decay)]
    eg = [jnp.exp(g) for g in g128]
    rhs = [jnp.concatenate([b * x, (b * e) * y], axis=1) for b, x, e, y in zip(b128, v, eg, k)]
    sol = solve_all(a, rhs)
    qk = [jnp.where(causal, _nt(x, y) * dc, 0.0) for x, y, dc in zip(q, k, decay)]
    return ([s[:, :HD] for s in sol], [s[:, HD:] for s in sol], qk, [x * e for x, e in zip(q, eg)],
            [x * jnp.exp(gl - g) for x, gl, g in zip(k, gl128, g128)], [jnp.exp(gl) for gl in gl128])


def _all_head_gates(bgv):
    return tuple(list(z) for z in zip(*[_head_gates(bgv, h) for h in range(NH)]))


def _head_gates(bgv, h):
    lane = _iota((CH, 128), 1)
    row = _iota((CH, 128), 0)
    bcol = jnp.sum(jnp.where(lane == h, bgv, 0.0), axis=1, keepdims=True)
    gcol = jnp.sum(jnp.where(lane == NH + h, bgv, 0.0), axis=1, keepdims=True)
    g128 = jnp.broadcast_to(gcol, (CH, 128))
    gl128 = jnp.broadcast_to(jnp.sum(jnp.where(row == CH - 1, g128, 0.0), axis=0, keepdims=True), (CH, 128))
    return (g128, jnp.broadcast_to(gcol, (CH, CH)), gl128,
            jnp.broadcast_to(bcol, (CH, 128)), jnp.broadcast_to(bcol, (CH, CH)))


def _chunk_specs():
    g = LOCAL_CHUNKS
    row = pl.BlockSpec((g * CH, D), lambda i: (i, 0))
    small = pl.BlockSpec((g * CH, 128), lambda i: (i, 0))
    qk = pl.BlockSpec((g * NH, CH, CH), lambda i: (i, 0, 0))
    eg = pl.BlockSpec((g, NH, 128), lambda i: (i, 0, 0))
    return row, small, qk, eg


def _chunk_heads():
    return [(slice(c * CH, (c + 1) * CH), slice(h * HD, (h + 1) * HD), c, h)
            for c in range(LOCAL_CHUNKS) for h in range(NH)]


def _all_gates(bg_ref):
    per_chunk = [_all_head_gates(bg_ref[c * CH:(c + 1) * CH, :]) for c in range(LOCAL_CHUNKS)]
    return tuple(sum((list(pc[j]) for pc in per_chunk), []) for j in range(5))


def _dn_local_fwd(q, k, v, bg, name):
    t = q.shape[0]
    n = t // CH
    pairs = _chunk_heads()

    def body(q_ref, k_ref, v_ref, bg_ref, u_ref, w_ref, qk_ref, qd_ref, kd_ref, eg_ref, inv_ref):
        def solve_all(mats, rhs):
            invs = _neumann_inverses(mats)
            for p in range(len(pairs)):
                inv_ref[p] = invs[p]
            return [_dot3(m, r, NN) for m, r in zip(invs, rhs)]

        u, w, qk, qd, kd, egl = _chunk_local(
            solve_all, [q_ref[r, hc] for r, hc, _, _ in pairs], [k_ref[r, hc] for r, hc, _, _ in pairs],
            [v_ref[r, hc] for r, hc, _, _ in pairs], *_all_gates(bg_ref))
        for p, (r, hc, c, h) in enumerate(pairs):
            u_ref[r, hc] = u[p]
            w_ref[r, hc] = w[p].astype(w_ref.dtype)
            qd_ref[r, hc] = qd[p].astype(qd_ref.dtype)
            kd_ref[r, hc] = kd[p].astype(kd_ref.dtype)
            qk_ref[p] = qk[p].astype(qk_ref.dtype)
            eg_ref[c, h:h + 1, :] = egl[p][0:1, :]

    row, small, qkb, egb = _chunk_specs()
    return pl.pallas_call(
        body, grid=(n // LOCAL_CHUNKS,), in_specs=[row, row, row, small],
        out_specs=[row, row, qkb, row, row, egb, qkb],
        out_shape=[SDS((t, D), F32), SDS((t, D), BF16), SDS((n * NH, CH, CH), BF16), SDS((t, D), BF16),
                   SDS((t, D), BF16), SDS((n, NH, 128), F32), SDS((n * NH, CH, CH), F32)],
        compiler_params=_params(("parallel",)), name=name,
    )(q, k, v, bg)


def _dn_local_bwd(q, k, v, bg, inv, du, dw, dqk, dqd, dkd, deg, name):
    t = q.shape[0]
    n = t // CH
    pairs = _chunk_heads()

    def body(q_ref, k_ref, v_ref, bg_ref, inv_ref, du_ref, dw_ref, dqk_ref, dqd_ref, dkd_ref, deg_ref,
             dq_ref, dk_ref, dv_ref, dbg_ref):
        lane = _iota((CH, 128), 1)
        row = _iota((CH, 128), 0)
        first = jnp.where(row == 0, 1.0, 0.0)
        solves = [_solve_with(inv_ref[p]) for p in range(len(pairs))]

        def solve_all(mats, rhs):
            return [f(m, r) for f, m, r in zip(solves, mats, rhs)]

        _, vjp = jax.vjp(functools.partial(_chunk_local, solve_all),
                         [q_ref[r, hc] for r, hc, _, _ in pairs], [k_ref[r, hc] for r, hc, _, _ in pairs],
                         [v_ref[r, hc] for r, hc, _, _ in pairs], *_all_gates(bg_ref))
        cts = ([du_ref[r, hc].astype(F32) for r, hc, _, _ in pairs],
               [dw_ref[r, hc].astype(F32) for r, hc, _, _ in pairs],
               [dqk_ref[p] for p in range(len(pairs))],
               [dqd_ref[r, hc].astype(F32) for r, hc, _, _ in pairs],
               [dkd_ref[r, hc].astype(F32) for r, hc, _, _ in pairs],
               [jnp.broadcast_to(deg_ref[c, h:h + 1, :], (CH, 128)) * first for _, _, c, h in pairs])
        dq, dk, dv, dg128, dg64, dgl, db128, db64 = vjp(cts)
        acc = [jnp.zeros((CH, 128), F32) for _ in range(LOCAL_CHUNKS)]
        for p, (r, hc, c, h) in enumerate(pairs):
            dq_ref[r, hc] = dq[p]
            dk_ref[r, hc] = dk[p]
            dv_ref[r, hc] = dv[p]
            dg = jnp.sum(dg128[p], axis=1, keepdims=True) + jnp.sum(dg64[p], axis=1, keepdims=True)
            tot = jnp.sum(jnp.sum(dgl[p], axis=0, keepdims=True), axis=1, keepdims=True)
            dg = dg + jnp.where(row[:, 0:1] == CH - 1, tot, 0.0)
            db = jnp.sum(db128[p], axis=1, keepdims=True) + jnp.sum(db64[p], axis=1, keepdims=True)
            acc[c] = acc[c] + jnp.where(lane == h, db, 0.0) + jnp.where(lane == NH + h, dg, 0.0)
        for c in range(LOCAL_CHUNKS):
            dbg_ref[c * CH:(c + 1) * CH, :] = acc[c]

    row, small, qkb, egb = _chunk_specs()
    return pl.pallas_call(
        body, grid=(n // LOCAL_CHUNKS,), in_specs=[row, row, row, small, qkb, row, row, qkb, row, row, egb],
        out_specs=[row, row, row, small],
        out_shape=[SDS((t, D), F32)] * 3 + [SDS((t, 128), F32)],
        compiler_params=_params(("parallel",)), name=name,
    )(q, k, v, bg, inv, du, dw, dqk, dqd, dkd, deg)


def _state_step(s, u, w, qk, qd, kd, egl):
    ws = [_nn(a, b) for a, b in zip(w, s)]
    v_new = [a - b for a, b in zip(u, ws)]
    qs = [_nn(a, b) for a, b in zip(qd, s)]
    intra = [_nn(a, b) for a, b in zip(qk, v_new)]
    upd = [_tn(a, b) for a, b in zip(kd, v_new)]
    return [a * e + b for a, e, b in zip(s, egl, upd)], [a + b for a, b in zip(qs, intra)]


def _dn_scan_fwd(u, w, qk, qd, kd, eg, name):
    t = u.shape[0]
    n = t // CH
    g = SCAN_CHUNKS

    def body(u_ref, w_ref, qk_ref, qd_ref, kd_ref, eg_ref, o_ref, save_ref, s_ref):
        @pl.when(pl.program_id(0) == 0)
        def _():
            s_ref[...] = jnp.zeros_like(s_ref)

        cols = [slice(h * HD, (h + 1) * HD) for h in range(NH)]
        s = [s_ref[h] for h in range(NH)]
        for c in range(g):
            rows = slice(c * CH, (c + 1) * CH)
            for h in range(NH):
                save_ref[c, h] = s[h].astype(save_ref.dtype)
            s, o = _state_step(
                s, [u_ref[rows, hc] for hc in cols], [w_ref[rows, hc].astype(F32) for hc in cols],
                [qk_ref[c * NH + h].astype(F32) for h in range(NH)], [qd_ref[rows, hc].astype(F32) for hc in cols],
                [kd_ref[rows, hc].astype(F32) for hc in cols], [eg_ref[c, h:h + 1, :] for h in range(NH)])
            for h, hc in enumerate(cols):
                o_ref[rows, hc] = o[h]
        for h in range(NH):
            s_ref[h] = s[h]

    row = pl.BlockSpec((g * CH, D), lambda i: (i, 0))
    qkb = pl.BlockSpec((g * NH, CH, CH), lambda i: (i, 0, 0))
    egb = pl.BlockSpec((g, NH, 128), lambda i: (i, 0, 0))
    return pl.pallas_call(
        body, grid=(n // g,), in_specs=[row, row, qkb, row, row, egb],
        out_specs=[row, pl.BlockSpec((g, NH, HD, HD), lambda i: (i, 0, 0, 0))],
        out_shape=[SDS((t, D), F32), SDS((n, NH, HD, HD), BF16)],
        scratch_shapes=[pltpu.VMEM((NH, HD, HD), F32)],
        compiler_params=_params(("arbitrary",)), name=name,
    )(u, w, qk, qd, kd, eg)


def _dn_scan_bwd(u, w, qk, qd, kd, eg, saved, do, name):
    t = u.shape[0]
    n = t // CH
    g = SCAN_CHUNKS
    last = n // g - 1

    def body(u_ref, w_ref, qk_ref, qd_ref, kd_ref, eg_ref, sv_ref, do_ref,
             du_ref, dw_ref, dqk_ref, dqd_ref, dkd_ref, deg_ref, ds_ref):
        @pl.when(pl.program_id(0) == 0)
        def _():
            ds_ref[...] = jnp.zeros_like(ds_ref)

        cols = [slice(h * HD, (h + 1) * HD) for h in range(NH)]
        ds = [ds_ref[h] for h in range(NH)]
        for c in reversed(range(g)):
            rows = slice(c * CH, (c + 1) * CH)
            _, vjp = jax.vjp(
                _state_step, [sv_ref[c, h].astype(F32) for h in range(NH)], [u_ref[rows, hc] for hc in cols],
                [w_ref[rows, hc].astype(F32) for hc in cols], [qk_ref[c * NH + h].astype(F32) for h in range(NH)],
                [qd_ref[rows, hc].astype(F32) for hc in cols], [kd_ref[rows, hc].astype(F32) for hc in cols],
                [eg_ref[c, h:h + 1, :] for h in range(NH)])
            ds, du, dw, dqk, dqd, dkd, deg = vjp((ds, [do_ref[rows, hc] for hc in cols]))
            for h, hc in enumerate(cols):
                du_ref[rows, hc] = du[h].astype(du_ref.dtype)
                dw_ref[rows, hc] = dw[h].astype(dw_ref.dtype)
                dqk_ref[c * NH + h] = dqk[h]
                dqd_ref[rows, hc] = dqd[h].astype(dqd_ref.dtype)
                dkd_ref[rows, hc] = dkd[h].astype(dkd_ref.dtype)
                deg_ref[c, h:h + 1, :] = deg[h]
        for h in range(NH):
            ds_ref[h] = ds[h]

    row = pl.BlockSpec((g * CH, D), lambda i: (last - i, 0))
    qkb = pl.BlockSpec((g * NH, CH, CH), lambda i: (last - i, 0, 0))
    egb = pl.BlockSpec((g, NH, 128), lambda i: (last - i, 0, 0))
    return pl.pallas_call(
        body, grid=(n // g,),
        in_specs=[row, row, qkb, row, row, egb,
                  pl.BlockSpec((g, NH, HD, HD), lambda i: (last - i, 0, 0, 0)), row],
        out_specs=[row, row, qkb, row, row, egb],
        out_shape=[SDS((t, D), BF16), SDS((t, D), BF16), SDS((n * NH, CH, CH), F32), SDS((t, D), BF16),
                   SDS((t, D), BF16), SDS((n, NH, 128), F32)],
        scratch_shapes=[pltpu.VMEM((NH, HD, HD), F32)],
        compiler_params=_params(("arbitrary",)), name=name,
    )(u, w, qk, qd, kd, eg, saved, do)


def _ada_fwd(c_all, ada_w, ada_b, name):
    ncol = ada_w.shape[1]

    def body(c_ref, w_ref, b_ref, o_ref):
        o_ref[...] = _dg(_silu(c_ref[...]), w_ref[...], NN, HI) + b_ref[...]

    return pl.pallas_call(body, out_shape=SDS((NDEV, ncol), F32),
                          compiler_params=pltpu.CompilerParams(vmem_limit_bytes=VMEM_LIMIT), name=name,
                          )(c_all, ada_w, ada_b)


def _ada_bwd(c_all_t, dmod, name):
    ncol = dmod.shape[1]

    def body(c_ref, d_ref, o_ref):
        sc = _silu(c_ref[...])
        acc = sc[:, 0:1] * d_ref[0:1, :]
        for b in range(1, NDEV):
            acc = acc + sc[:, b:b + 1] * d_ref[b:b + 1, :]
        o_ref[...] = acc

    return pl.pallas_call(body, out_shape=SDS((D, ncol), F32),
                          compiler_params=pltpu.CompilerParams(vmem_limit_bytes=VMEM_LIMIT), name=name,
                          )(c_all_t, dmod)


def _sum_devices(parts, out_dtype, name):
    _, r, c = parts.shape
    tr = TR if r % TR == 0 else r

    def body(p_ref, o_ref):
        acc = p_ref[0].astype(F32)
        for i in range(1, NDEV):
            acc = acc + p_ref[i].astype(F32)
        o_ref[...] = acc.astype(o_ref.dtype)

    return pl.pallas_call(
        body, grid=(r // tr,), in_specs=[pl.BlockSpec((NDEV, tr, c), lambda i: (0, i, 0))],
        out_specs=pl.BlockSpec((tr, c), lambda i: (i, 0)), out_shape=SDS((r, c), out_dtype),
        compiler_params=_params(("parallel",)), name=name,
    )(parts)


def _adam_tiles(r, c):
    if r % 8 == 0:
        return _pick(r, (256, 352, 128, 8)), c
    return r, (256 if c % 256 == 0 else c)


def _adam_math(w, gv, m, v):
    m_new = ADAM_B1 * m + (1.0 - ADAM_B1) * gv
    v_new = ADAM_B2 * v + (1.0 - ADAM_B2) * (gv * gv)
    bc1 = 1.0 - ADAM_B1 ** ADAM_STEP
    bc2 = 1.0 - ADAM_B2 ** ADAM_STEP
    return -ADAM_LR * ((m_new / bc1) / (jnp.sqrt(v_new / bc2) + ADAM_EPS) + ADAM_WD * w), m_new, v_new


def _adamw(w, g, m, v, name):
    r, c = w.shape
    tr, tc = _adam_tiles(r, c)

    def body(w_ref, g_ref, m_ref, v_ref, d_ref, nm_ref, nv_ref):
        d_ref[...], nm_ref[...], nv_ref[...] = _adam_math(w_ref[...], g_ref[...], m_ref[...], v_ref[...])

    spec = pl.BlockSpec((tr, tc), lambda i, j: (i, j))
    return pl.pallas_call(
        body, grid=(r // tr, c // tc), in_specs=[spec] * 4, out_specs=[spec] * 3,
        out_shape=[SDS((r, c), F32)] * 3, compiler_params=_params(("parallel", "parallel")), name=name,
    )(w, g, m, v)


def _reduce_adamw(parts, w, m, v, name):
    r, c = w.shape
    tr, tc = _adam_tiles(r, c)

    def body(p_ref, w_ref, m_ref, v_ref, g_ref, d_ref, nm_ref, nv_ref):
        gv = p_ref[0].astype(F32)
        for i in range(1, NDEV):
            gv = gv + p_ref[i].astype(F32)
        g_ref[...] = gv
        d_ref[...], nm_ref[...], nv_ref[...] = _adam_math(w_ref[...], gv, m_ref[...], v_ref[...])

    spec = pl.BlockSpec((tr, tc), lambda i, j: (i, j))
    return pl.pallas_call(
        body, grid=(r // tr, c // tc),
        in_specs=[pl.BlockSpec((NDEV, tr, tc), lambda i, j: (0, i, j))] + [spec] * 3, out_specs=[spec] * 4,
        out_shape=[SDS((r, c), F32)] * 4, compiler_params=_params(("parallel", "parallel")), name=name,
    )(parts, w, m, v)


ANY = pl.BlockSpec(memory_space=pl.ANY)
MESH = pl.DeviceIdType.MESH


def _all_gather(xs, name, after=None):
    n = len(xs)
    extra = [] if after is None else [after]

    def body(*refs):
        x_refs, out_refs = refs[:n], refs[n + len(extra):2 * n + len(extra)]
        send_sems, recv_sems, local_sems = refs[-3:]
        mx, my, mc = lax.axis_index("x"), lax.axis_index("y"), lax.axis_index("c")
        me, sibling = (mx, my, mc), (mx, my, 1 - mc)
        chips = [(1 - mx, my), (mx, 1 - my), (1 - mx, 1 - my)]

        def rows(a, px, py, pc):
            return out_refs[a].at[4 * px + 2 * py + pc]

        def copy(a, k, block, to, src=None):
            return pltpu.make_async_remote_copy(
                src_ref=rows(a, *block) if src is None else src, dst_ref=rows(a, *block),
                send_sem=send_sems.at[a, k], recv_sem=recv_sems.at[a, k], device_id=to, device_id_type=MESH)

        mine = [pltpu.make_async_copy(x_refs[a], rows(a, *me), local_sems.at[a]) for a in range(n)]
        for cp in mine:
            cp.start()
        first = []
        for a in range(n):
            first.append(copy(a, 0, me, sibling, src=x_refs[a]))
            first += [copy(a, 1 + j, me, (*chip, mc), src=x_refs[a]) for j, chip in enumerate(chips)]
        for cp in first:
            cp.start()
        passed = []
        for a in range(n):
            for j, chip in enumerate(chips):
                copy(a, 1 + j, (*chip, mc), me).wait_recv()
                passed.append(copy(a, 4 + j, (*chip, mc), sibling))
                passed[-1].start()
        for a in range(n):
            copy(a, 0, sibling, me).wait_recv()
            for j, chip in enumerate(chips):
                copy(a, 4 + j, (*chip, 1 - mc), me).wait_recv()
        for cp in first + passed:
            cp.wait_send()
        for cp in mine:
            cp.wait()

    return pl.pallas_call(
        body, out_shape=[SDS((NDEV,) + x.shape, x.dtype) for x in xs], in_specs=[ANY] * (n + len(extra)),
        out_specs=[ANY] * n,
        scratch_shapes=[pltpu.SemaphoreType.DMA((n, 7)), pltpu.SemaphoreType.DMA((n, 7)),
                        pltpu.SemaphoreType.DMA((n,))],
        name=name,
    )(*xs, *extra)


HBM = pl.BlockSpec(memory_space=pltpu.HBM)
SEM = pl.BlockSpec(memory_space=pltpu.SEMAPHORE)
EFFECT = pltpu.SideEffectType.DATAFLOW_SIDE_EFFECTING


def _peers():
    mx, my, mc = lax.axis_index("x"), lax.axis_index("y"), lax.axis_index("c")
    out = []
    for k in range(1, NDEV):
        out.append((1 - mx if k & 4 else mx, 1 - my if k & 2 else my, 1 - mc if k & 1 else mc))
    return 4 * mx + 2 * my + mc, out


NEAR = (0, 1, 3, 5)


def _push_start(srcs, sliced, name, after=None, near=()):
    n = len(srcs)
    extra = [] if after is None else [after]
    lands = [lax.empty(s.shape if sliced else (NDEV,) + s.shape, s.dtype) for s in srcs]

    def body(*refs):
        src_refs, land_refs = refs[:n], refs[n:2 * n]
        outs = refs[2 * n + len(extra):]
        send_sems, recv_sems = outs[:n], outs[n:2 * n]
        token = refs[-1]
        me, peers = _peers()
        for a in range(n):
            for k, (px, py, pc) in enumerate(peers):
                if a in near and k not in NEAR:
                    continue
                src = src_refs[a].at[4 * px + 2 * py + pc] if sliced else src_refs[a]
                pltpu.make_async_remote_copy(
                    src_ref=src, dst_ref=land_refs[a].at[me], send_sem=send_sems[a].at[k],
                    recv_sem=recv_sems[a].at[k], device_id=(px, py, pc), device_id_type=MESH).start()
            pltpu.make_async_copy(src_refs[a].at[me] if sliced else src_refs[a], land_refs[a].at[me],
                                  send_sems[a].at[NDEV - 1]).start()
        token[...] = jnp.zeros_like(token)

    outs = pl.pallas_call(
        body, name=name,
        out_shape=([pltpu.SemaphoreType.DMA((NDEV,))] * n + [pltpu.SemaphoreType.DMA((NDEV - 1,))] * n
                   + [pltpu.HBM(s.shape, s.dtype) for s in srcs] + [pltpu.HBM(l.shape, l.dtype) for l in lands]
                   + [SDS((8, 128), F32)]),
        in_specs=[HBM] * (2 * n) + [pl.BlockSpec(memory_space=pl.ANY)] * len(extra),
        out_specs=[SEM] * (2 * n) + [HBM] * (2 * n) + [pl.BlockSpec(memory_space=pltpu.VMEM)],
        input_output_aliases={i: 2 * n + i for i in range(2 * n)},
        compiler_params=pltpu.CompilerParams(has_side_effects=EFFECT),
    )(*[pltpu.with_memory_space_constraint(s, pltpu.HBM) for s in srcs],
      *[pltpu.with_memory_space_constraint(l, pltpu.HBM) for l in lands], *extra)
    sends, recvs = outs[:n], outs[n:2 * n]
    src_thru, land_thru = outs[2 * n:3 * n], outs[3 * n:4 * n]
    return [(sends[a], recvs[a], src_thru[a], land_thru[a]) for a in range(n)], outs[-1]


def _push_wait(started, sliced, after, name, near=()):
    n = len(started)
    afters = list(after) if isinstance(after, (list, tuple)) else [after]

    def body(*refs):
        src_refs, land_refs = refs[:n], refs[n:2 * n]
        send_sems, recv_sems = refs[2 * n:3 * n], refs[3 * n:4 * n]
        me, peers = _peers()
        for a in range(n):
            for k, (px, py, pc) in enumerate(peers):
                if a in near and k not in NEAR:
                    continue
                src = src_refs[a].at[4 * px + 2 * py + pc] if sliced else src_refs[a]
                cp = pltpu.make_async_remote_copy(
                    src_ref=src, dst_ref=land_refs[a].at[me], send_sem=send_sems[a].at[k],
                    recv_sem=recv_sems[a].at[k], device_id=(px, py, pc), device_id_type=MESH)
                cp.wait_send()
                cp.wait_recv()
            pltpu.make_async_copy(src_refs[a].at[me] if sliced else src_refs[a], land_refs[a].at[me],
                                  send_sems[a].at[NDEV - 1]).wait()

    srcs = [s[2] for s in started]
    lands = [s[3] for s in started]
    outs = pl.pallas_call(
        body, name=name,
        out_shape=[pltpu.HBM(s.shape, s.dtype) for s in srcs] + [pltpu.HBM(l.shape, l.dtype) for l in lands],
        in_specs=[HBM] * (2 * n) + [SEM] * (2 * n) + [pl.BlockSpec(memory_space=pl.ANY)] * len(afters),
        out_specs=[HBM] * (2 * n),
        input_output_aliases={i: i for i in range(2 * n)},
        compiler_params=pltpu.CompilerParams(has_side_effects=EFFECT),
    )(*srcs, *lands, *[s[0] for s in started], *[s[1] for s in started], *afters)
    return outs[n:]


def _relay_to_sibling(land, name):
    def body(_, land_ref, send_sems, recv_sems):
        mx, my, mc = lax.axis_index("x"), lax.axis_index("y"), lax.axis_index("c")
        chips = [(1 - mx, my), (mx, 1 - my), (1 - mx, 1 - my)]

        def copy(j, core):
            slot = land_ref.at[4 * chips[j][0] + 2 * chips[j][1] + core]
            return pltpu.make_async_remote_copy(
                src_ref=slot, dst_ref=slot, send_sem=send_sems.at[j], recv_sem=recv_sems.at[j],
                device_id=(mx, my, 1 - mc), device_id_type=MESH)

        mine = [copy(j, mc) for j in range(3)]
        for cp in mine:
            cp.start()
        for j in range(3):
            copy(j, 1 - mc).wait_recv()
        for cp in mine:
            cp.wait_send()

    return pl.pallas_call(
        body, out_shape=SDS(land.shape, land.dtype), in_specs=[ANY], out_specs=ANY, input_output_aliases={0: 0},
        scratch_shapes=[pltpu.SemaphoreType.DMA((3,)), pltpu.SemaphoreType.DMA((3,))], name=name,
    )(land)


def _cols_from_blocks(blocks):
    _, rows, w = blocks.shape
    return blocks.transpose(1, 0, 2).reshape(rows, NDEV * w)


def _cols_to_blocks(full):
    rows, total = full.shape
    return full.reshape(rows, NDEV, total // NDEV).transpose(1, 0, 2)


def _mix_pad(wt):
    xp, q, k, v, z, ba, gp, gd = jnp.split(wt, (512, 1536, 2560, 3584, 4608, 4624, 5648), axis=0)
    pad = jnp.zeros((MIXP - OFF_BA - 16, wt.shape[1]), wt.dtype)
    return jnp.concatenate([q, k, v, z, gp, gd, xp, ba, pad], axis=0)


def _mix_unpad(wt):
    q, k, v, z, gp, gd, xp, ba = (wt[OFF_Q:OFF_K], wt[OFF_K:OFF_V], wt[OFF_V:OFF_Z], wt[OFF_Z:OFF_GP],
                                  wt[OFF_GP:OFF_GD], wt[OFF_GD:OFF_XP], wt[OFF_XP:OFF_BA], wt[OFF_BA:OFF_BA + 16])
    return jnp.concatenate([xp, q, k, v, z, ba, gp, gd], axis=0)


def _lane_row(vec8):
    return jnp.zeros((1, 128), F32).at[0, NH:2 * NH].set(vec8)


def _ffn_fwd(x, h, gate, w_in, w_out, tag, next_norm=None, token=None, start_more=None, final=None):
    if isinstance(w_in, tuple):
        w_in, = _push_wait([w_in], False, h, f"{tag}_gather_wait_in")
    w_in = w_in.reshape(2 * FH, D)
    u, a = _swiglu_up(h, w_in, f"{tag}_up", after=token)
    w_out, = _push_wait([w_out], False, a, f"{tag}_gather_wait_out")
    w_out = w_out.reshape(FH, D)
    if final is not None:
        return _matmul_residual_loss(a, w_out, x, gate, 0.5, *final, f"{tag}_down_loss"), (h, u, a, None), w_in, w_out
    outs = _matmul_residual(a, w_out, x, gate, 0.5, a_blk=True, norm=next_norm, name=f"{tag}_down",
                            after=None if start_more is None else start_more(h))
    return outs[0], (h, u, a, outs[1]), w_in, w_out, (outs[2] if next_norm else None)


def _ffn_bwd(dx_out, dy, x, g, scale, w_in, w_out, saved, tag, below=None):
    h, u, a, _ = saved
    t = x.shape[0]
    dw_out = _matmul(a, dy, ta=True, a_blk=True, out_dtype=BF16, name=f"{tag}_down_dw")
    sent_out, token = _push_start([dw_out.reshape(NDEV, FH // NDEV, D)], True, f"{tag}_grad_start_out")
    du = _swiglu_down_bwd(dy, w_out, u, f"{tag}_down_dx", after=token).reshape(NDEV, t, FB)
    dw_in = _matmul(du, h, ta=True, a_blk=True, out_dtype=BF16, name=f"{tag}_up_dw")
    sent_in, token = _push_start([dw_in.reshape(NDEV, FB, D)], True, f"{tag}_grad_start_in")
    dh = _matmul(du, w_in, a_blk=True, out_dtype=F32, name=f"{tag}_up_dx", after=token)
    return _norm_mod_bwd(x, g, scale, dh, dx_out, f"{tag}_norm_bwd", below), sent_in + sent_out


def kernel(x, c, ada_w, ada_b, norm_g, ffn1_w_in, ffn1_w_out, ffn2_w_in, ffn2_w_out, mix_w_in, conv_w, a_log, dt_bias, dn_norm_g, pool_w, pool_scale, pool_proj, dn_proj, mix_w_out, final_g, loss_target, m_ada_w, m_ada_b, m_norm_g, m_ffn1_w_in, m_ffn1_w_out, m_ffn2_w_in, m_ffn2_w_out, m_mix_w_in, m_conv_w, m_a_log, m_dt_bias, m_dn_norm_g, m_pool_w, m_pool_scale, m_pool_proj, m_dn_proj, m_mix_w_out, m_final_g, v_ada_w, v_ada_b, v_norm_g, v_ffn1_w_in, v_ffn1_w_out, v_ffn2_w_in, v_ffn2_w_out, v_mix_w_in, v_conv_w, v_a_log, v_dt_bias, v_dn_norm_g, v_pool_w, v_pool_scale, v_pool_proj, v_dn_proj, v_mix_w_out, v_final_g):
    me = 4 * lax.axis_index("x") + 2 * lax.axis_index("y") + lax.axis_index("c")
    x0 = x[0]
    target = loss_target[0]
    t = x0.shape[0]

    big = [ffn1_w_in[0], ffn1_w_out[0], ffn2_w_in[0], ffn2_w_out[0], mix_w_in[0], pool_proj[0], dn_proj[0],
           mix_w_out[0]]
    small = jnp.concatenate([c.reshape(8, 128), conv_w[0].reshape(12, 128), norm_g[0].reshape(3, 128),
                             jnp.zeros((1, 128), F32)], axis=0)
    small_all, = _all_gather([small], "gather_small")
    c_all = small_all[:, 0:8, :].reshape(NDEV, D)
    conv_full = small_all[:, 8:20, :].reshape(NDEV, 4, 384).transpose(1, 0, 2).reshape(4, 3 * D)
    norm_full = small_all[:, 20:23, :].reshape(NDEV, 3, 128).transpose(1, 0, 2).reshape(3, D)

    ncol = ada_w.shape[2]
    ada_b_mine = lax.dynamic_slice(ada_b, (0, me * ncol), (1, ncol))
    mod_cols = _ada_fwd(c_all, ada_w[0], ada_b_mine, "ada_fwd")
    transposed = (0, 2, 4)
    payload = [(w.T if i in transposed else w).astype(BF16) for i, w in enumerate(big)]
    mod_all, w_in1 = _all_gather([mod_cols, payload[0]], "gather_mod_first_weight")
    started, token = _push_start([payload[1], payload[4]], False, "gather_start", after=mod_all, near=(1,))
    started = {1: started[0], 4: started[1]}

    def start_rest(h):
        more, token = _push_start([payload[i] for i in (5, 6, 7, 2, 3)], False, "gather_start_rest", after=h)
        started.update(zip((5, 6, 7, 2, 3), more))
        return token

    mod = lax.dynamic_index_in_dim(mod_all, me, axis=1, keepdims=False).reshape(9, D)
    shift = [mod[3 * s:3 * s + 1] for s in range(3)]
    scale = [mod[3 * s + 1:3 * s + 2] for s in range(3)]
    gate = [mod[3 * s + 2:3 * s + 3] for s in range(3)]
    ng = [norm_full[s:s + 1] for s in range(3)]
    fg = final_g.reshape(1, D)
    al_row = _lane_row(a_log[0])
    dt_row = _lane_row(dt_bias[0])
    gn = dn_norm_g
    pw = pool_w[0]
    ps = pool_scale

    h0 = _norm_mod_fwd(x0, ng[0], shift[0], scale[0], "ffn1_norm", after=token)
    x1, saved1, w_in1, w_out1, h1 = _ffn_fwd(x0, h0, gate[0], w_in1, started[1], "ffn1",
                                             (ng[1], shift[1], scale[1]), token, start_rest)

    seg, = _push_wait([started[4]], False, h1, "mix_gather_wait", near=(0,))
    w_mix = _mix_pad(_relay_to_sibling(seg, "mix_gather_relay").reshape(MIX_RAW, D))
    proj = _matmul(h1, w_mix, tb=True, out_dtype=F32, name="mix_in")
    qh, kh, vh, bg = _dn_pre_fwd(proj, conv_full, al_row, dt_row, "dn_pre")
    seg = _push_wait([started[i] for i in (5, 6, 7)], False, qh, "mix_gather_wait_rest")
    w_pp = _cols_from_blocks(seg[0])
    w_dn = seg[1].reshape(D, D)
    w_mo = seg[2].reshape(D, D)
    ya = _pool_fwd(proj, pw, ps, w_pp, "pool_fwd")
    u, w, qk, qd, kd, eg, inv = _dn_local_fwd(qh, kh, vh, bg, "dn_local")
    o, s_saved = _dn_scan_fwd(u, w, qk, qd, kd, eg, "dn_scan")
    ob = _dn_post_fwd(o, proj, gn, "dn_post")
    yb = _matmul(ob, w_dn, out_dtype=F32, name="dn_out")
    merged, x2, mix_y, h2 = _merge_out(ya, yb, proj, w_mo, x1, gate[1], (ng[2], shift[2], scale[2]), "mix_out")

    (loss_row, dx3, dfg, dy2, dgate2), saved2, w_in2, w_out2 = _ffn_fwd(
        x2, h2, gate[2], started[2], started[3], "ffn2", final=(fg, target))

    (dx2, dsh2, dsc2, dng2, dmy, dgate1), sent2 = _ffn_bwd(dx3, dy2, x2, ng[2], scale[2], w_in2, w_out2, saved2,
                                                           "ffn2", (mix_y, gate[1], 1.0))

    dmerged = _matmul(dmy, w_mo, tb=True, out_dtype=BF16, name="mix_out_dx")
    dw_mo = _matmul(merged, dmy, ta=True, out_dtype=BF16, name="mix_out_dw")
    dproj = lax.empty((t, MIXP), BF16)
    dya, dyb, dproj = _merge_bwd(dmerged, ya, yb, proj, dproj, "merge_bwd")
    dob = _matmul(dyb, w_dn, tb=True, out_dtype=F32, name="dn_out_dx")
    dw_dn = _matmul(ob, dyb, ta=True, out_dtype=BF16, name="dn_out_dw")
    do, dproj, dgn = _dn_post_bwd(o, proj, gn, dob, dproj, "dn_post_bwd")
    du, dw, dqk, dqd, dkd, deg = _dn_scan_bwd(u, w, qk, qd, kd, eg, s_saved, do, "dn_scan_bwd")
    dqh, dkh, dvh, dbg = _dn_local_bwd(qh, kh, vh, bg, inv, du, dw, dqk, dqd, dkd, deg, "dn_local_bwd")
    dconv, dproj, dal, ddt = _dn_pre_bwd_act(proj, conv_full, al_row, dt_row, dqh, dkh, dvh, dbg, dproj,
                                             "dn_pre_bwd_act")
    dproj, dcw = _dn_pre_bwd_conv(proj, conv_full, dconv, dproj, "dn_pre_bwd_conv")
    dwin, dpl, dpw, dps, dpp = _pool_bwd_local(proj, pw, ps, w_pp, dya, "pool_bwd_local")
    dproj = _pool_bwd_window(dwin, dpl, dproj, "pool_bwd_window")
    dw_mix = _matmul(dproj, h1, ta=True, out_dtype=BF16, name="mix_in_dw")
    sent1, token = _push_start(
        [_mix_unpad(dw_mix).reshape(NDEV, MIX_RAW // NDEV, D), _cols_to_blocks(dpp.astype(BF16)),
         dw_dn.reshape(NDEV, -1, D), dw_mo.reshape(NDEV, -1, D)], True, "mix_grad_start")
    dh1 = _matmul(dproj, w_mix, out_dtype=F32, name="mix_in_dx", after=token)
    dx1, dsh1, dsc1, dng1, dy0, dgate0 = _norm_mod_bwd(x1, ng[1], scale[1], dh1, dx2, "mix_norm_bwd",
                                                       (saved1[3], gate[0], 0.5))

    (dx0, dsh0, dsc0, dng0), sent0 = _ffn_bwd(dx1, dy0, x0, ng[0], scale[0], w_in1, w_out1, saved1, "ffn1")

    dmod = jnp.concatenate([dsh0, dsc0, dgate0, dsh1, dsc1, dgate1, dsh2, dsc2, dgate2], axis=1).reshape(-1)
    flat = jnp.concatenate([
        dmod, dal[0, NH:2 * NH], ddt[0, NH:2 * NH], dgn.reshape(-1), dps.reshape(-1), dfg.reshape(-1),
        dpw.reshape(-1), jnp.concatenate([dng0, dng1, dng2], axis=0).reshape(-1), dcw.reshape(-1),
        loss_row[0, 0:1]])
    nflat = 90 * D
    flat = jnp.concatenate([flat, jnp.zeros((nflat - flat.shape[0],), F32)]).reshape(90, D)
    sent_small, small_token = _push_start([flat], False, "small_grad_start")

    def small_grads(flat_all):
        tot = _sum_devices(flat_all, F32, "sum_small_grads").reshape(-1)
        dmod_all = flat_all.reshape(NDEV, nflat)[:, :9 * D]
        dmod_cols = lax.dynamic_slice(dmod_all, (0, me * ncol), (NDEV, ncol))
        g_ada_w = _ada_bwd(c_all.T, dmod_cols, "ada_bwd")
        p = 0
        pieces = {}
        for nm, size in (("ada_b", 9 * D), ("a_log", NH), ("dt_bias", NH), ("dn_norm_g", HD), ("pool_scale", PW),
                         ("final_g", D), ("pool_w", 4 * PG * PG), ("norm_g", 3 * D), ("conv_w", 12 * D),
                         ("loss", 1)):
            pieces[nm] = tot[p:p + size]
            p += size
        g_norm = lax.dynamic_slice(pieces["norm_g"].reshape(3, D), (0, me * 128), (3, 128))
        g_conv = lax.dynamic_slice(pieces["conv_w"].reshape(4, 3 * D), (0, me * 384), (4, 384))
        return pieces["loss"][0], {
            "ada_w": g_ada_w.reshape(ada_w.shape), "ada_b": pieces["ada_b"].reshape(ada_b.shape),
            "norm_g": g_norm.reshape(norm_g.shape), "conv_w": g_conv.reshape(conv_w.shape),
            "a_log": pieces["a_log"].reshape(a_log.shape), "dt_bias": pieces["dt_bias"].reshape(dt_bias.shape),
            "dn_norm_g": pieces["dn_norm_g"].reshape(dn_norm_g.shape),
            "pool_w": pieces["pool_w"].reshape(pool_w.shape),
            "pool_scale": pieces["pool_scale"].reshape(pool_scale.shape),
            "final_g": pieces["final_g"].reshape(final_g.shape),
        }

    grads = {}
    weights = {"ada_w": ada_w, "ada_b": ada_b, "norm_g": norm_g, "ffn1_w_in": ffn1_w_in, "ffn1_w_out": ffn1_w_out,
               "ffn2_w_in": ffn2_w_in, "ffn2_w_out": ffn2_w_out, "mix_w_in": mix_w_in, "conv_w": conv_w,
               "a_log": a_log, "dt_bias": dt_bias, "dn_norm_g": dn_norm_g, "pool_w": pool_w,
               "pool_scale": pool_scale, "pool_proj": pool_proj, "dn_proj": dn_proj, "mix_w_out": mix_w_out,
               "final_g": final_g}
    m_in = {"ada_w": m_ada_w, "ada_b": m_ada_b, "norm_g": m_norm_g, "ffn1_w_in": m_ffn1_w_in,
            "ffn1_w_out": m_ffn1_w_out, "ffn2_w_in": m_ffn2_w_in, "ffn2_w_out": m_ffn2_w_out,
            "mix_w_in": m_mix_w_in, "conv_w": m_conv_w, "a_log": m_a_log, "dt_bias": m_dt_bias,
            "dn_norm_g": m_dn_norm_g, "pool_w": m_pool_w, "pool_scale": m_pool_scale, "pool_proj": m_pool_proj,
            "dn_proj": m_dn_proj, "mix_w_out": m_mix_w_out, "final_g": m_final_g}
    v_in = {"ada_w": v_ada_w, "ada_b": v_ada_b, "norm_g": v_norm_g, "ffn1_w_in": v_ffn1_w_in,
            "ffn1_w_out": v_ffn1_w_out, "ffn2_w_in": v_ffn2_w_in, "ffn2_w_out": v_ffn2_w_out,
            "mix_w_in": v_mix_w_in, "conv_w": v_conv_w, "a_log": v_a_log, "dt_bias": v_dt_bias,
            "dn_norm_g": v_dn_norm_g, "pool_w": v_pool_w, "pool_scale": v_pool_scale, "pool_proj": v_pool_proj,
            "dn_proj": v_dn_proj, "mix_w_out": v_mix_w_out, "final_g": v_final_g}

    names = list(weights)
    large = ("ada_w", "ffn1_w_in", "ffn1_w_out", "ffn2_w_in", "ffn2_w_out", "mix_w_in", "pool_proj", "dn_proj",
             "mix_w_out")
    delta, new_m, new_v = {}, {}, {}

    flipped = ("ffn1_w_in", "ffn2_w_in", "mix_w_in")

    def views(nm):
        shp = weights[nm].shape
        two_d = (shp[-2], shp[-1])
        if nm in flipped:
            return (lambda a: a.reshape(two_d).T), (lambda a: a.T.reshape(shp))
        return (lambda a: a.reshape(two_d)), (lambda a: a.reshape(shp))

    def reduce_update(sent, group, after, tag):
        done = []
        for nm, r in zip(group, _push_wait(sent, True, after, f"{tag}_grad_wait")):
            view, back = views(nm)
            g_, d_, m_, v_ = _reduce_adamw(r, view(weights[nm]), view(m_in[nm]), view(v_in[nm]), f"adamw_{nm}")
            grads[nm], delta[nm], new_m[nm], new_v[nm] = back(g_), back(d_), back(m_), back(v_)
            done.append(d_)
        return done

    done = reduce_update(sent2, ("ffn2_w_in", "ffn2_w_out"), small_token, "ffn2")
    done += reduce_update(sent1, ("mix_w_in", "pool_proj", "dn_proj", "mix_w_out"), done, "mix")
    flat_all, = _push_wait(sent_small, False, done, "small_grad_wait")
    loss, small = small_grads(flat_all)
    grads.update(small)
    view, back = views("ada_w")
    done, m_, v_ = _adamw(view(ada_w), view(grads["ada_w"]), view(m_ada_w), view(v_ada_w), "adamw_ada_w")
    delta["ada_w"], new_m["ada_w"], new_v["ada_w"] = back(done), back(m_), back(v_)
    reduce_update(sent0, ("ffn1_w_in", "ffn1_w_out"), done, "ffn1")
    rest = [nm for nm in names if nm not in large]
    total = sum(weights[nm].size for nm in rest)
    padded = -(-total // D) * D

    def pack(tree, fill):
        flat_ = jnp.concatenate([tree[nm].reshape(-1) for nm in rest])
        return jnp.concatenate([flat_, jnp.full((padded - total,), fill, F32)]).reshape(-1, D)

    d_, m_, v_ = _adamw(pack(weights, 0.0), pack(grads, 0.0), pack(m_in, 0.0), pack(v_in, 1.0), "adamw_small")
    p = 0
    for nm in rest:
        size = weights[nm].size
        shp = weights[nm].shape
        delta[nm] = d_.reshape(-1)[p:p + size].reshape(shp)
        new_m[nm] = m_.reshape(-1)[p:p + size].reshape(shp)
        new_v[nm] = v_.reshape(-1)[p:p + size].reshape(shp)
        p += size

    grad_x = dx0.reshape(x.shape)
    return (loss, grad_x, *[grads[nm] for nm in names], *[delta[nm] for nm in names],
            *[new_m[nm] for nm in names], *[new_v[nm] for nm in names])
```

```python
import functools

import jax
import jax.numpy as jnp
from jax import lax
from jax.experimental import pallas as pl
from jax.experimental.pallas import tpu as pltpu

F32 = jnp.float32
BF16 = jnp.bfloat16
SDS = jax.ShapeDtypeStruct
HI = lax.Precision.HIGHEST

D = 1024
FH = 2816
FB = 704
NH = 8
HD = 128
CH = 64
SCAN_CHUNKS = 8
LOCAL_CHUNKS = 2
NDEV = 8
PW = 512
PG = 128
RMS_EPS = 1e-6
L2_EPS = 1e-6
TR = 512
HALO = 16
VMEM_LIMIT = 56 * 1024 * 1024
MATMUL_VMEM = 40 * 1024 * 1024

MIXP = 6912
OFF_Q, OFF_K, OFF_V, OFF_Z, OFF_GP, OFF_GD, OFF_XP, OFF_BA = 0, 1024, 2048, 3072, 4096, 5120, 6144, 6656
MIX_RAW = 6672

ADAM_LR = 0.001
ADAM_B1 = 0.9
ADAM_B2 = 0.999
ADAM_EPS = 1e-08
ADAM_WD = 0.01
ADAM_STEP = 10

NN = (((1,), (0,)), ((), ()))
NT = (((1,), (1,)), ((), ()))
TN = (((0,), (0,)), ((), ()))


def _dg(a, b, dims, prec=None):
    return lax.dot_general(a, b, dims, precision=prec, preferred_element_type=F32)


def _make_dots(prec):
    @jax.custom_vjp
    def nn(a, b):
        return _dg(a, b, NN, prec)

    @jax.custom_vjp
    def nt(a, b):
        return _dg(a, b, NT, prec)

    @jax.custom_vjp
    def tn(a, b):
        return _dg(a, b, TN, prec)

    nn.defvjp(lambda a, b: (nn(a, b), (a, b)), lambda r, d: (nt(d, r[1]), tn(r[0], d)))
    nt.defvjp(lambda a, b: (nt(a, b), (a, b)), lambda r, d: (nn(d, r[1]), tn(d, r[0])))
    tn.defvjp(lambda a, b: (tn(a, b), (a, b)), lambda r, d: (nt(r[1], d), nn(r[0], d)))
    return nn, nt, tn


_nn, _nt, _tn = _make_dots(None)


def _params(sem):
    return pltpu.CompilerParams(dimension_semantics=sem, vmem_limit_bytes=VMEM_LIMIT)


def _sigmoid(x):
    return 1.0 / (1.0 + jnp.exp(-x))


def _silu(x):
    return x * _sigmoid(x)


def _dsilu(x):
    s = _sigmoid(x)
    return s * (1.0 + x * (1.0 - s))


def _pick(n, cands):
    for c in cands:
        if n % c == 0:
            return c
    raise ValueError(f"no tile for {n}")


def _iota(shape, dim):
    return lax.broadcasted_iota(jnp.int32, shape, dim)


def _matmul(a, b, *, ta=False, tb=False, a_blk=False, b_blk=False, o_blk=False, tm=None, tn=None, tk=None,
            out_dtype, name, after=None):
    if a_blk:
        nb, r, cb = a.shape
        if ta:
            k_dim, m_dim, tm = r, nb * cb, cb
        else:
            m_dim, k_dim, tk = r, nb * cb, cb
    else:
        k_dim, m_dim = a.shape if ta else a.shape[::-1]
    if b_blk:
        nb, r, cb = b.shape
        if tb:
            n_dim, tk = r, cb
            assert nb * cb == k_dim
        else:
            n_dim, tn = nb * cb, cb
            assert r == k_dim
    else:
        n_dim = b.shape[0] if tb else b.shape[1]
    tn = tn or _pick(n_dim, (1024, 768, 512, 256, 128))
    out_bytes = jnp.dtype(out_dtype).itemsize

    def vmem(tm_, tk_):
        return 4 * tk_ * (tm_ + tn) + tm_ * tn * (4 + 2 * out_bytes)

    k_cands = [tk] if tk else [c for c in (k_dim, 4096, 3456, 2816, 2304, 2048, 1024, 512, 256)
                               if c <= k_dim and k_dim % c == 0]
    m_cands = [tm] if tm else [c for c in (2048, 1024, 768, 512, 256, 128) if m_dim % c == 0]
    base = next((c for c in m_cands if c <= 1024), m_cands[-1])
    tk = next((c for c in k_cands if vmem(base, c) <= MATMUL_VMEM), k_cands[-1])
    tm = next((c for c in m_cands if vmem(c, tk) <= MATMUL_VMEM), m_cands[-1])
    nk = k_dim // tk
    dims = ((((0,) if ta else (1,)), ((1,) if tb else (0,))), ((), ()))

    def body(a_ref, b_ref, *rest):
        o_ref, acc_ref = rest[-2:]
        k = pl.program_id(2)

        @pl.when(k == 0)
        def _():
            acc_ref[...] = jnp.zeros_like(acc_ref)

        acc_ref[...] += lax.dot_general(a_ref[...].astype(BF16), b_ref[...].astype(BF16), dims,
                                        preferred_element_type=F32)

        @pl.when(k == nk - 1)
        def _():
            o_ref[...] = acc_ref[...].astype(o_ref.dtype)

    if a_blk:
        a_spec = (pl.BlockSpec((None, tk, tm), lambda i, j, k: (i, k, 0)) if ta
                  else pl.BlockSpec((None, tm, tk), lambda i, j, k: (k, i, 0)))
    else:
        a_spec = (pl.BlockSpec((tk, tm), lambda i, j, k: (k, i)) if ta
                  else pl.BlockSpec((tm, tk), lambda i, j, k: (i, k)))
    if b_blk:
        b_spec = (pl.BlockSpec((None, tn, tk), lambda i, j, k: (k, j, 0)) if tb
                  else pl.BlockSpec((None, tk, tn), lambda i, j, k: (j, k, 0)))
    else:
        b_spec = (pl.BlockSpec((tn, tk), lambda i, j, k: (j, k)) if tb
                  else pl.BlockSpec((tk, tn), lambda i, j, k: (k, j)))
    if o_blk:
        o_spec = pl.BlockSpec((None, tm, tn), lambda i, j, k: (j, i, 0))
        o_shape = SDS((n_dim // tn, m_dim, tn), out_dtype)
    else:
        o_spec = pl.BlockSpec((tm, tn), lambda i, j, k: (i, j))
        o_shape = SDS((m_dim, n_dim), out_dtype)
    return pl.pallas_call(
        body, grid=(m_dim // tm, n_dim // tn, nk),
        in_specs=[a_spec, b_spec] + ([] if after is None else [pl.BlockSpec(memory_space=pl.ANY)]),
        out_specs=o_spec,
        out_shape=o_shape,
        scratch_shapes=[pltpu.VMEM((tm, tn), F32)],
        compiler_params=_params(("parallel", "parallel", "arbitrary")),
        name=name,
    )(a, b, *([] if after is None else [after]))


def _matmul_residual(a, b, x, gate, coef, *, a_blk=False, norm=None, name, after=None):
    m_dim = a.shape[-2]
    tm = _pick(m_dim, (1024, 512))
    if a_blk:
        nk, _, tk = a.shape
        a_spec = pl.BlockSpec((None, tm, tk), lambda i, k: (k, i, 0))
    else:
        tk = a.shape[1]
        nk = 1
        a_spec = pl.BlockSpec((tm, tk), lambda i, k: (i, 0))
    extra = [] if after is None else [after]
    vecs = [gate] + (list(norm) if norm else [])

    def body(a_ref, b_ref, x_ref, gate_ref, *rest):
        vec_refs = rest[:len(vecs) - 1]
        outs = rest[len(vecs) - 1 + len(extra):]
        acc_ref = outs[-1]
        k = pl.program_id(1)

        @pl.when(k == 0)
        def _():
            acc_ref[...] = jnp.zeros_like(acc_ref)

        acc_ref[...] += _dg(a_ref[...], b_ref[...], NN)

        @pl.when(k == nk - 1)
        def _():
            y = acc_ref[...]
            xn = x_ref[...] + (coef * gate_ref[...]) * y
            outs[0][...] = xn
            outs[1][...] = y.astype(outs[1].dtype)
            if norm:
                g_ref, sh_ref, sc_ref = vec_refs
                r = lax.rsqrt(jnp.mean(xn * xn, axis=-1, keepdims=True) + RMS_EPS)
                outs[2][...] = (((xn * r) * g_ref[...]) * (1.0 + sc_ref[...]) + sh_ref[...]).astype(outs[2].dtype)

    row = pl.BlockSpec((tm, D), lambda i, k: (i, 0))
    vec = pl.BlockSpec((1, D), lambda i, k: (0, 0))
    return pl.pallas_call(
        body, grid=(m_dim // tm, nk),
        in_specs=[a_spec, pl.BlockSpec((tk, D), lambda i, k: (k, 0)), row] + [vec] * len(vecs)
        + [pl.BlockSpec(memory_space=pl.ANY)] * len(extra),
        out_specs=[row] * (3 if norm else 2),
        out_shape=[SDS((m_dim, D), F32), SDS((m_dim, D), BF16)] + ([SDS((m_dim, D), BF16)] if norm else []),
        scratch_shapes=[pltpu.VMEM((tm, D), F32)],
        compiler_params=_params(("parallel", "arbitrary")), name=name,
    )(a, b, x, *vecs, *extra)


def _matmul_residual_loss(a, b, x, gate, coef, fg, target, name):
    nk, m_dim, tk = a.shape
    tm = _pick(m_dim, (1024, 512))
    nt = m_dim // tm

    def body(a_ref, b_ref, x_ref, gate_ref, g_ref, t_ref, loss_ref, dx_ref, dg_ref, dy_ref, dgate_ref,
             acc_ref, sq_ref):
        i, k = pl.program_id(0), pl.program_id(1)

        @pl.when(k == 0)
        def _():
            acc_ref[...] = jnp.zeros_like(acc_ref)

        @pl.when(jnp.logical_and(i == 0, k == 0))
        def _():
            sq_ref[...] = jnp.zeros_like(sq_ref)
            dg_ref[...] = jnp.zeros_like(dg_ref)
            dgate_ref[...] = jnp.zeros_like(dgate_ref)

        acc_ref[...] += _dg(a_ref[...], b_ref[...], NN)

        @pl.when(k == nk - 1)
        def _():
            y = acc_ref[...]
            scaled_gate = coef * gate_ref[...]
            xn = x_ref[...] + scaled_gate * y
            gv = g_ref[...]
            r = lax.rsqrt(jnp.mean(xn * xn, axis=-1, keepdims=True) + RMS_EPS)
            n = xn * r
            err = n * gv - t_ref[...]
            sq_ref[...] += jnp.sum(err * err, axis=0, keepdims=True)
            dout = err * (1.0 / D)
            dg_ref[...] += jnp.sum(dout * n, axis=0, keepdims=True)
            dn = dout * gv
            dxv = r * (dn - n * jnp.mean(dn * n, axis=-1, keepdims=True))
            dx_ref[...] = dxv
            dy_ref[...] = (scaled_gate * dxv).astype(dy_ref.dtype)
            dgate_ref[...] += jnp.sum((coef * dxv) * y, axis=0, keepdims=True)

        @pl.when(jnp.logical_and(i == nt - 1, k == nk - 1))
        def _():
            tot = jnp.sum(sq_ref[...], axis=1, keepdims=True) * (0.5 / D)
            loss_ref[...] = jnp.broadcast_to(tot, loss_ref.shape)

    row = pl.BlockSpec((tm, D), lambda i, k: (i, 0))
    vec = pl.BlockSpec((1, D), lambda i, k: (0, 0))
    return pl.pallas_call(
        body, grid=(nt, nk),
        in_specs=[pl.BlockSpec((None, tm, tk), lambda i, k: (k, i, 0)), pl.BlockSpec((tk, D), lambda i, k: (k, 0)),
                  row, vec, vec, row],
        out_specs=[pl.BlockSpec((1, 128), lambda i, k: (0, 0)), row, vec, row, vec],
        out_shape=[SDS((1, 128), F32), SDS((m_dim, D), F32), SDS((1, D), F32), SDS((m_dim, D), BF16),
                   SDS((1, D), F32)],
        scratch_shapes=[pltpu.VMEM((tm, D), F32), pltpu.VMEM((1, D), F32)],
        compiler_params=_params(("arbitrary", "arbitrary")), name=name,
    )(a, b, x, gate, fg, target)


def _row(width, col=0):
    return pl.BlockSpec((TR, width), lambda i: (i, col))


def _vec(width):
    return pl.BlockSpec((1, width), lambda i: (0, 0))


def _norm_mod_fwd(x, g, shift, scale, name, after=None):
    t = x.shape[0]
    extra = [] if after is None else [after]

    def body(x_ref, g_ref, sh_ref, sc_ref, *rest):
        o_ref = rest[-1]
        xv = x_ref[...]
        r = lax.rsqrt(jnp.mean(xv * xv, axis=-1, keepdims=True) + RMS_EPS)
        o_ref[...] = (((xv * r) * g_ref[...]) * (1.0 + sc_ref[...]) + sh_ref[...]).astype(o_ref.dtype)

    return pl.pallas_call(
        body, grid=(t // TR,),
        in_specs=[_row(D), _vec(D), _vec(D), _vec(D)] + [pl.BlockSpec(memory_space=pl.ANY)] * len(extra),
        out_specs=_row(D),
        out_shape=SDS((t, D), BF16), compiler_params=_params(("parallel",)), name=name,
    )(x, g, shift, scale, *extra)


def _residual_branch_bwd(dxv, y_ref, gate_ref, coef, dy_ref, dgate_ref):
    dy_ref[...] = ((coef * gate_ref[...]) * dxv).astype(dy_ref.dtype)
    dgate_ref[...] += jnp.sum((coef * dxv) * y_ref[...], axis=0, keepdims=True)


def _norm_mod_bwd(x, g, scale, dh, dx_in, name, below=None):
    t = x.shape[0]
    lower = [] if below is None else list(below[:2])

    def body(x_ref, g_ref, sc_ref, dh_ref, dxi_ref, *rest):
        dx_ref, dsh_ref, dsc_ref, dg_ref = rest[len(lower):len(lower) + 4]

        @pl.when(pl.program_id(0) == 0)
        def _():
            for ref in rest[len(lower) + 1:]:
                if ref.shape[0] == 1:
                    ref[...] = jnp.zeros_like(ref)

        xv = x_ref[...]
        gv = g_ref[...]
        dh = dh_ref[...]
        r = lax.rsqrt(jnp.mean(xv * xv, axis=-1, keepdims=True) + RMS_EPS)
        n = xv * r
        dsh_ref[...] += jnp.sum(dh, axis=0, keepdims=True)
        dsc_ref[...] += jnp.sum(dh * (n * gv), axis=0, keepdims=True)
        tt = dh * (1.0 + sc_ref[...])
        dg_ref[...] += jnp.sum(tt * n, axis=0, keepdims=True)
        dn = tt * gv
        dxv = dxi_ref[...] + r * (dn - n * jnp.mean(dn * n, axis=-1, keepdims=True))
        dx_ref[...] = dxv
        if below is not None:
            _residual_branch_bwd(dxv, rest[0], rest[1], below[2], rest[-2], rest[-1])

    more_in = [] if below is None else [_row(D), _vec(D)]
    more_out = [] if below is None else [_row(D), _vec(D)]
    more_shape = [] if below is None else [SDS((t, D), BF16), SDS((1, D), F32)]
    return pl.pallas_call(
        body, grid=(t // TR,), in_specs=[_row(D), _vec(D), _vec(D), _row(D), _row(D)] + more_in,
        out_specs=[_row(D), _vec(D), _vec(D), _vec(D)] + more_out,
        out_shape=[SDS((t, D), F32), SDS((1, D), F32), SDS((1, D), F32), SDS((1, D), F32)] + more_shape,
        compiler_params=_params(("arbitrary",)), name=name,
    )(x, g, scale, dh, dx_in, *lower)


def _swiglu_up(h, w_in, name, after=None):
    t = h.shape[0]
    tm = _pick(t, (1024, 512, 256))
    half = NDEV // 2
    extra = [] if after is None else [after]

    def body(h_ref, wg_ref, wu_ref, *rest):
        u_ref, a_ref = rest[-2:]
        hv = h_ref[...]
        gate = _dg(hv, wg_ref[...], NT)
        up = _dg(hv, wu_ref[...], NT)
        u_ref[0] = gate.astype(u_ref.dtype)
        u_ref[1] = up.astype(u_ref.dtype)
        a_ref[...] = (_silu(gate) * up).astype(a_ref.dtype)

    return pl.pallas_call(
        body, grid=(t // tm, half),
        in_specs=[pl.BlockSpec((tm, D), lambda i, j: (i, 0)),
                  pl.BlockSpec((FB, D), lambda i, j: (j, 0)),
                  pl.BlockSpec((FB, D), lambda i, j: (j + half, 0))]
        + [pl.BlockSpec(memory_space=pl.ANY)] * len(extra),
        out_specs=[pl.BlockSpec((2, None, tm, FB), lambda i, j: (0, j, i, 0)),
                   pl.BlockSpec((None, tm, FB), lambda i, j: (j, i, 0))],
        out_shape=[SDS((2, half, t, FB), BF16), SDS((half, t, FB), BF16)],
        compiler_params=_params(("parallel", "parallel")), name=name,
    )(h, w_in, w_in, *extra)


def _swiglu_down_bwd(dy, w_out, u, name, after=None):
    t = dy.shape[0]
    tm = _pick(t, (1024, 512, 256))
    half = NDEV // 2
    extra = [] if after is None else [after]
    pair = pl.BlockSpec((2, None, tm, FB), lambda i, j: (0, j, i, 0))

    def body(dy_ref, w_ref, u_ref, *rest):
        o_ref = rest[-1]
        da = _dg(dy_ref[...], w_ref[...], NT)
        gate = u_ref[0].astype(F32)
        o_ref[0] = (da * u_ref[1].astype(F32) * _dsilu(gate)).astype(o_ref.dtype)
        o_ref[1] = (da * _silu(gate)).astype(o_ref.dtype)

    return pl.pallas_call(
        body, grid=(t // tm, half),
        in_specs=[pl.BlockSpec((tm, D), lambda i, j: (i, 0)), pl.BlockSpec((FB, D), lambda i, j: (j, 0)), pair]
        + [pl.BlockSpec(memory_space=pl.ANY)] * len(extra),
        out_specs=pair, out_shape=SDS((2, half, t, FB), BF16),
        compiler_params=_params(("parallel", "parallel")), name=name,
    )(dy, w_out, u, *extra)


def _halo_prev(width, col):
    per = TR // HALO
    return pl.BlockSpec((HALO, width), lambda i: (jnp.maximum(i * per - 1, 0), col))


def _halo_next(width, col, nt):
    per = TR // HALO
    return pl.BlockSpec((HALO, width), lambda i: (jnp.minimum((i + 1) * per, nt * per - 1), col))


def _pool_windows(ext, tile_index):
    rows = _iota((TR, PG), 0) + tile_index * TR + 1
    pooled, counts = [], []
    for gi in range(4):
        w = 2 << gi
        e = ext[:, gi * PG:(gi + 1) * PG]
        s = e
        step = 1
        while step < w:
            s = s + pltpu.roll(s, step, 0)
            step *= 2
        cnt = jnp.minimum(rows, w).astype(F32)
        pooled.append(s[HALO:] / cnt - e[HALO:])
        counts.append(cnt)
    return pooled, counts


def _pool_fwd(proj, pool_w, pool_scale, pool_proj, name):
    t = proj.shape[0]
    xcol = OFF_XP // PW

    def body(x_ref, h_ref, pw_ref, ps_ref, pp_ref, o_ref):
        i = pl.program_id(0)
        halo = jnp.where(i > 0, h_ref[...], 0.0)
        ext = jnp.concatenate([halo, x_ref[...]], axis=0)
        pooled, _ = _pool_windows(ext, i)
        mixed = [_dg(pooled[g].astype(BF16), pw_ref[g].astype(BF16), NN) for g in range(4)]
        ypre = jnp.concatenate(mixed, axis=1) * ps_ref[...]
        o_ref[...] = _dg(ypre.astype(BF16), pp_ref[...], NN)

    return pl.pallas_call(
        body, grid=(t // TR,),
        in_specs=[_row(PW, xcol), _halo_prev(PW, xcol),
                  pl.BlockSpec((4, PG, PG), lambda i: (0, 0, 0)), _vec(PW),
                  pl.BlockSpec((PW, D), lambda i: (0, 0))],
        out_specs=_row(D), out_shape=SDS((t, D), F32),
        compiler_params=_params(("parallel",)), name=name,
    )(proj, proj, pool_w, pool_scale, pool_proj)


def _pool_bwd_local(proj, pool_w, pool_scale, pool_proj, dya, name):
    t = proj.shape[0]
    xcol = OFF_XP // PW

    def body(x_ref, h_ref, pw_ref, ps_ref, pp_ref, dya_ref, dwin_ref, dpl_ref, dpw_ref, dps_ref, dpp_ref):
        i = pl.program_id(0)

        @pl.when(i == 0)
        def _():
            dpw_ref[...] = jnp.zeros_like(dpw_ref)
            dps_ref[...] = jnp.zeros_like(dps_ref)
            dpp_ref[...] = jnp.zeros_like(dpp_ref)

        halo = jnp.where(i > 0, h_ref[...], 0.0)
        ext = jnp.concatenate([halo, x_ref[...]], axis=0)
        pooled, counts = _pool_windows(ext, i)
        mixed = jnp.concatenate(
            [_dg(pooled[g].astype(BF16), pw_ref[g].astype(BF16), NN) for g in range(4)], axis=1)
        ps = ps_ref[...]
        ypre = mixed * ps
        dyab = dya_ref[...].astype(BF16)
        dypre = _dg(dyab, pp_ref[...], NT)
        dpp_ref[...] += _dg(ypre.astype(BF16), dyab, TN)
        dps_ref[...] += jnp.sum(dypre * mixed, axis=0, keepdims=True)
        dmixed = dypre * ps
        for g in range(4):
            dm = dmixed[:, g * PG:(g + 1) * PG].astype(BF16)
            dpw_ref[g] += _dg(pooled[g].astype(BF16), dm, TN)
            dpooled = _dg(dm, pw_ref[g].astype(BF16), NT)
            dwin_ref[:, g * PG:(g + 1) * PG] = dpooled / counts[g]
            dpl_ref[:, g * PG:(g + 1) * PG] = dpooled

    return pl.pallas_call(
        body, grid=(t // TR,),
        in_specs=[_row(PW, xcol), _halo_prev(PW, xcol),
                  pl.BlockSpec((4, PG, PG), lambda i: (0, 0, 0)), _vec(PW),
                  pl.BlockSpec((PW, D), lambda i: (0, 0)), _row(D)],
        out_specs=[_row(PW), _row(PW), pl.BlockSpec((4, PG, PG), lambda i: (0, 0, 0)), _vec(PW),
                   pl.BlockSpec((PW, D), lambda i: (0, 0))],
        out_shape=[SDS((t, PW), F32), SDS((t, PW), F32), SDS((4, PG, PG), F32), SDS((1, PW), F32),
                   SDS((PW, D), F32)],
        compiler_params=_params(("arbitrary",)), name=name,
    )(proj, proj, pool_w, pool_scale, pool_proj, dya)


def _pool_bwd_window(dwin, dpl, dproj, name):
    t = dwin.shape[0]
    nt = t // TR
    ext_rows = TR + HALO

    def body(dw_ref, h_ref, dp_ref, _, o_ref):
        i = pl.program_id(0)
        halo = jnp.where(i < nt - 1, h_ref[...], 0.0)
        ext = jnp.concatenate([dw_ref[...], halo], axis=0)
        for gi in range(4):
            w = 2 << gi
            s = ext[:, gi * PG:(gi + 1) * PG]
            step = 1
            while step < w:
                s = s + pltpu.roll(s, ext_rows - step, 0)
                step *= 2
            o_ref[:, gi * PG:(gi + 1) * PG] = (s[:TR] - dp_ref[:, gi * PG:(gi + 1) * PG]).astype(o_ref.dtype)

    return pl.pallas_call(
        body, grid=(nt,),
        in_specs=[_row(PW), _halo_next(PW, 0, nt), _row(PW), pl.BlockSpec(memory_space=pl.ANY)],
        out_specs=_into(PW, OFF_XP), out_shape=SDS(dproj.shape, dproj.dtype), input_output_aliases={3: 0},
        compiler_params=_params(("parallel",)), name=name,
    )(dwin, dwin, dpl, dproj)


def _conv_group(ext, cw_ref, cols):
    acc = cw_ref[3:4, cols] * ext
    for j in range(3):
        acc = acc + cw_ref[j:j + 1, cols] * pltpu.roll(ext, 3 - j, 0)
    return acc[HALO:]


def _gate_terms(raw, al, dt):
    beta = _sigmoid(raw)
    xg = raw + dt
    sp = jnp.maximum(xg, 0.0) + jnp.log(1.0 + jnp.exp(-jnp.abs(xg)))
    g = -jnp.exp(al) * sp
    return beta, g, _sigmoid(xg)


def _dn_pre_fwd(proj, conv_w, al_row, dt_row, name):
    t = proj.shape[0]

    def body(x_ref, h_ref, cw_ref, ba_ref, al_ref, dt_ref, q_ref, k_ref, v_ref, bg_ref):
        i = pl.program_id(0)
        keep = i > 0
        for grp in range(24):
            cols = slice(grp * HD, (grp + 1) * HD)
            ext = jnp.concatenate([jnp.where(keep, h_ref[:, cols], 0.0), x_ref[:, cols]], axis=0)
            s = _silu(_conv_group(ext, cw_ref, cols))
            seg, head = divmod(grp, NH)
            hc = slice(head * HD, (head + 1) * HD)
            if seg == 0:
                q_ref[:, hc] = s * lax.rsqrt(jnp.sum(s * s, axis=-1, keepdims=True) + L2_EPS) * (HD ** -0.5)
            elif seg == 1:
                k_ref[:, hc] = s * lax.rsqrt(jnp.sum(s * s, axis=-1, keepdims=True) + L2_EPS)
            else:
                v_ref[:, hc] = s
        lane = _iota((TR, 128), 1)
        rowc = _iota((TR, 128), 0) % CH
        beta, g, _ = _gate_terms(ba_ref[...], al_ref[...], dt_ref[...])
        step = 1
        while step < CH:
            g = g + jnp.where(rowc >= step, pltpu.roll(g, step, 0), 0.0)
            step *= 2
        bg_ref[...] = jnp.where(lane < NH, beta, jnp.where(lane < 2 * NH, g, 0.0))

    return pl.pallas_call(
        body, grid=(t // TR,),
        in_specs=[_row(3 * D, 0), _halo_prev(3 * D, 0), pl.BlockSpec((4, 3 * D), lambda i: (0, 0)),
                  _row(128, OFF_BA // 128), _vec(128), _vec(128)],
        out_specs=[_row(D), _row(D), _row(D), _row(128)],
        out_shape=[SDS((t, D), F32), SDS((t, D), F32), SDS((t, D), F32), SDS((t, 128), F32)],
        compiler_params=_params(("parallel",)), name=name,
    )(proj, proj, conv_w, proj, al_row, dt_row)


def _dn_pre_bwd_act(proj, conv_w, al_row, dt_row, dq, dk, dv, dbg, dproj, name):
    t = proj.shape[0]

    def body(x_ref, h_ref, cw_ref, ba_ref, al_ref, dt_ref, dq_ref, dk_ref, dv_ref, dbg_ref, _,
             dc_ref, draw_ref, dal_ref, ddt_ref):
        i = pl.program_id(0)

        @pl.when(i == 0)
        def _():
            dal_ref[...] = jnp.zeros_like(dal_ref)
            ddt_ref[...] = jnp.zeros_like(ddt_ref)

        keep = i > 0
        for grp in range(24):
            cols = slice(grp * HD, (grp + 1) * HD)
            ext = jnp.concatenate([jnp.where(keep, h_ref[:, cols], 0.0), x_ref[:, cols]], axis=0)
            cv = _conv_group(ext, cw_ref, cols)
            seg, head = divmod(grp, NH)
            hc = slice(head * HD, (head + 1) * HD)
            if seg == 2:
                ds = dv_ref[:, hc]
            else:
                s = _silu(cv)
                r = lax.rsqrt(jnp.sum(s * s, axis=-1, keepdims=True) + L2_EPS)
                dy = dq_ref[:, hc] if seg == 0 else dk_ref[:, hc]
                c = (HD ** -0.5) if seg == 0 else 1.0
                ds = (c * r) * (dy - s * ((r * r) * jnp.sum(dy * s, axis=-1, keepdims=True)))
            dc_ref[:, cols] = ds * _dsilu(cv)
        lane = _iota((TR, 128), 1)
        rowc = _iota((TR, 128), 0) % CH
        isb = lane < NH
        isg = jnp.logical_and(lane >= NH, lane < 2 * NH)
        beta, g, sg = _gate_terms(ba_ref[...], al_ref[...], dt_ref[...])
        dbgv = dbg_ref[...]
        dg = dbgv
        step = 1
        while step < CH:
            dg = dg + jnp.where(rowc < CH - step, pltpu.roll(dg, TR - step, 0), 0.0)
            step *= 2
        da_raw = dg * (-jnp.exp(al_ref[...])) * sg
        draw = jnp.where(isb, dbgv * beta * (1.0 - beta), jnp.where(isg, da_raw, 0.0))
        draw_ref[:, :128] = draw.astype(draw_ref.dtype)
        draw_ref[:, 128:] = jnp.zeros((TR, MIXP - OFF_BA - 128), draw_ref.dtype)
        dal_ref[...] += jnp.sum(jnp.where(isg, dg * g, 0.0), axis=0, keepdims=True)
        ddt_ref[...] += jnp.sum(jnp.where(isg, da_raw, 0.0), axis=0, keepdims=True)

    return pl.pallas_call(
        body, grid=(t // TR,),
        in_specs=[_row(3 * D, 0), _halo_prev(3 * D, 0), pl.BlockSpec((4, 3 * D), lambda i: (0, 0)),
                  _row(128, OFF_BA // 128), _vec(128), _vec(128), _row(D), _row(D), _row(D), _row(128),
                  pl.BlockSpec(memory_space=pl.ANY)],
        out_specs=[_row(3 * D), _into(MIXP - OFF_BA, OFF_BA), _vec(128), _vec(128)],
        out_shape=[SDS((t, 3 * D), F32), SDS(dproj.shape, dproj.dtype), SDS((1, 128), F32), SDS((1, 128), F32)],
        input_output_aliases={10: 1},
        compiler_params=_params(("arbitrary",)), name=name,
    )(proj, proj, conv_w, proj, al_row, dt_row, dq, dk, dv, dbg, dproj)


def _dn_pre_bwd_conv(proj, conv_w, dconv, dproj, name):
    t = proj.shape[0]
    nt = t // TR
    ext_rows = TR + HALO

    def body(x_ref, h_ref, cw_ref, dc_ref, dn_ref, _, dx_ref, dcw_ref):
        i = pl.program_id(0)

        @pl.when(i == 0)
        def _():
            dcw_ref[...] = jnp.zeros_like(dcw_ref)

        keep_prev = i > 0
        keep_next = i < nt - 1
        for grp in range(24):
            cols = slice(grp * HD, (grp + 1) * HD)
            dct = dc_ref[:, cols]
            dext = jnp.concatenate([dct, jnp.where(keep_next, dn_ref[:, cols], 0.0)], axis=0)
            acc = cw_ref[3:4, cols] * dext
            for j in range(3):
                acc = acc + cw_ref[j:j + 1, cols] * pltpu.roll(dext, ext_rows - (3 - j), 0)
            dx_ref[:, cols] = acc[:TR].astype(dx_ref.dtype)
            xext = jnp.concatenate([jnp.where(keep_prev, h_ref[:, cols], 0.0), x_ref[:, cols]], axis=0)
            for j in range(4):
                xs = xext if j == 3 else pltpu.roll(xext, 3 - j, 0)
                dcw_ref[j:j + 1, cols] += jnp.sum(xs[HALO:] * dct, axis=0, keepdims=True)

    return pl.pallas_call(
        body, grid=(nt,),
        in_specs=[_row(3 * D, 0), _halo_prev(3 * D, 0), pl.BlockSpec((4, 3 * D), lambda i: (0, 0)),
                  _row(3 * D), _halo_next(3 * D, 0, nt), pl.BlockSpec(memory_space=pl.ANY)],
        out_specs=[_into(3 * D, OFF_Q), pl.BlockSpec((4, 3 * D), lambda i: (0, 0))],
        out_shape=[SDS(dproj.shape, dproj.dtype), SDS((4, 3 * D), F32)],
        input_output_aliases={5: 0},
        compiler_params=_params(("arbitrary",)), name=name,
    )(proj, proj, conv_w, dconv, dconv, dproj)


def _dn_post_fwd(o, proj, gn, w_out, name):
    t = o.shape[0]

    def body(o_ref, z_ref, g_ref, w_ref, out_ref, y_ref):
        gv = g_ref[...]
        for h in range(NH):
            hc = slice(h * HD, (h + 1) * HD)
            ov = o_ref[:, hc]
            r = lax.rsqrt(jnp.mean(ov * ov, axis=-1, keepdims=True) + RMS_EPS)
            out_ref[:, hc] = (((ov * r) * gv) * _silu(z_ref[:, hc])).astype(out_ref.dtype)
        y_ref[...] = _dg(out_ref[...], w_ref[...], NN)

    return pl.pallas_call(
        body, grid=(t // TR,),
        in_specs=[_row(D), _row(D, OFF_Z // D), _vec(HD), pl.BlockSpec((D, D), lambda i: (0, 0))],
        out_specs=[_row(D), _row(D)], out_shape=[SDS((t, D), BF16), SDS((t, D), F32)],
        compiler_params=_params(("parallel",)), name=name,
    )(o, proj, gn, w_out)


def _dn_post_bwd(o, proj, gn, dob, dproj, name):
    t = o.shape[0]

    def body(o_ref, z_ref, g_ref, d_ref, _, do_ref, dz_ref, dg_ref):
        @pl.when(pl.program_id(0) == 0)
        def _():
            dg_ref[...] = jnp.zeros_like(dg_ref)

        gv = g_ref[...]
        acc = jnp.zeros((1, HD), F32)
        for h in range(NH):
            hc = slice(h * HD, (h + 1) * HD)
            ov = o_ref[:, hc]
            zv = z_ref[:, hc]
            dv = d_ref[:, hc]
            r = lax.rsqrt(jnp.mean(ov * ov, axis=-1, keepdims=True) + RMS_EPS)
            n = ov * r
            dz_ref[:, hc] = (dv * (n * gv) * _dsilu(zv)).astype(dz_ref.dtype)
            dng = dv * _silu(zv)
            acc = acc + jnp.sum(dng * n, axis=0, keepdims=True)
            dn = dng * gv
            do_ref[:, hc] = r * (dn - n * jnp.mean(dn * n, axis=-1, keepdims=True))
        dg_ref[...] += acc

    return pl.pallas_call(
        body, grid=(t // TR,),
        in_specs=[_row(D), _row(D, OFF_Z // D), _vec(HD), _row(D), pl.BlockSpec(memory_space=pl.ANY)],
        out_specs=[_row(D), _into(D, OFF_Z), _vec(HD)],
        out_shape=[SDS((t, D), F32), SDS(dproj.shape, dproj.dtype), SDS((1, HD), F32)],
        input_output_aliases={4: 1},
        compiler_params=_params(("arbitrary",)), name=name,
    )(o, proj, gn, dob, dproj)


def _merge_out(ya, yb, proj, w_out, x, gate, norm, name):
    t = ya.shape[0]

    def body(a_ref, b_ref, gp_ref, gd_ref, w_ref, x_ref, gate_ref, g_ref, sh_ref, sc_ref,
             m_ref, xn_ref, y_ref, h_ref):
        merged = (_sigmoid(gp_ref[...]) * a_ref[...] + _sigmoid(gd_ref[...]) * b_ref[...]).astype(m_ref.dtype)
        m_ref[...] = merged
        y = _dg(merged, w_ref[...], NN)
        xn = x_ref[...] + gate_ref[...] * y
        xn_ref[...] = xn
        y_ref[...] = y.astype(y_ref.dtype)
        r = lax.rsqrt(jnp.mean(xn * xn, axis=-1, keepdims=True) + RMS_EPS)
        h_ref[...] = (((xn * r) * g_ref[...]) * (1.0 + sc_ref[...]) + sh_ref[...]).astype(h_ref.dtype)

    return pl.pallas_call(
        body, grid=(t // TR,),
        in_specs=[_row(D), _row(D), _row(D, OFF_GP // D), _row(D, OFF_GD // D),
                  pl.BlockSpec((D, D), lambda i: (0, 0)), _row(D), _vec(D), _vec(D), _vec(D), _vec(D)],
        out_specs=[_row(D)] * 4,
        out_shape=[SDS((t, D), BF16), SDS((t, D), F32), SDS((t, D), BF16), SDS((t, D), BF16)],
        compiler_params=_params(("parallel",)), name=name,
    )(ya, yb, proj, proj, w_out, x, gate, *norm)


def _into(width, offset):
    assert offset % width == 0
    return pl.BlockSpec((TR, width), lambda i: (i, offset // width))


def _merge_bwd(dm, ya, yb, proj, dproj, name):
    t = ya.shape[0]

    def body(d_ref, a_ref, b_ref, gp_ref, gd_ref, _, da_ref, db_ref, dg_ref):
        dv = d_ref[...]
        sp = _sigmoid(gp_ref[...])
        sd = _sigmoid(gd_ref[...])
        da_ref[...] = (dv * sp).astype(da_ref.dtype)
        db_ref[...] = (dv * sd).astype(db_ref.dtype)
        dg_ref[:, :D] = (dv * a_ref[...] * sp * (1.0 - sp)).astype(dg_ref.dtype)
        dg_ref[:, D:] = (dv * b_ref[...] * sd * (1.0 - sd)).astype(dg_ref.dtype)

    return pl.pallas_call(
        body, grid=(t // TR,),
        in_specs=[_row(D), _row(D), _row(D), _row(D, OFF_GP // D), _row(D, OFF_GD // D),
                  pl.BlockSpec(memory_space=pl.ANY)],
        out_specs=[_row(D), _row(D), _into(2 * D, OFF_GP)],
        out_shape=[SDS((t, D), BF16), SDS((t, D), BF16), SDS(dproj.shape, dproj.dtype)],
        input_output_aliases={5: 2},
        compiler_params=_params(("parallel",)), name=name,
    )(dm, ya, yb, proj, proj, dproj)


def _split2(x):
    hi = x.astype(BF16)
    return hi, (x - hi.astype(F32)).astype(BF16)


def _dot3(a, b, dims):
    ah, al = _split2(a)
    bh, bl = _split2(b)
    return _dg(ah, bh, dims) + (_dg(ah, bl, dims) + _dg(al, bh, dims))


def _neumann_inverses(mats):
    ri = _iota((CH, CH), 0)
    ci = _iota((CH, CH), 1)
    eye = jnp.where(ri == ci, 1.0, 0.0).astype(F32)
    xs = [-a for a in mats]
    ps = [eye + x for x in xs]
    for _ in range(5):
        xs = [_dot3(x, x, NN) for x in xs]
        ps = [p + _dot3(p, x, NN) for p, x in zip(ps, xs)]
    return ps


def _solve_with(inv):
    @jax.custom_vjp
    def solve(a, rhs):
        return _dot3(inv, rhs, NN)

    def fwd(a, rhs):
        sol = _dot3(inv, rhs, NN)
        return sol, sol

    def bwd(sol, d):
        drhs = _dot3(inv, d, TN)
        return -_dot3(drhs, sol, NT), drhs

    solve.defvjp(fwd, bwd)
    return solve


@jax.custom_vjp
def _rows_to_lanes(g64):
    ri = _iota((CH, CH), 0)
    ci = _iota((CH, CH), 1)
    diag = jnp.where(ri == ci, g64, 0.0)
    ones = jnp.ones((CH, CH), BF16)
    hi = diag.astype(BF16)
    rem = diag - hi.astype(F32)
    mid = rem.astype(BF16)
    lo = (rem - mid.astype(F32)).astype(BF16)
    return _dg(ones, hi, NN) + (_dg(ones, mid, NN) + _dg(ones, lo, NN))


def _rows_to_lanes_bwd(_, d):
    ri = _iota((CH, CH), 0)
    ci = _iota((CH, CH), 1)
    return (jnp.where(ri == ci, jnp.broadcast_to(jnp.sum(d, axis=0, keepdims=True), (CH, CH)), 0.0),)


_rows_to_lanes.defvjp(lambda g64: (_rows_to_lanes(g64), None), _rows_to_lanes_bwd)


def _chunk_local(solve_all, q, k, v, g128, g64, gl128, b128, b64):
    ri = _iota((CH, CH), 0)
    ci = _iota((CH, CH), 1)
    causal = ri >= ci
    strict = ri > ci
    gj = [_rows_to_lanes(g) for g in g64]
    decay = [jnp.where(causal, jnp.exp(jnp.where(causal, g - t, 0.0)), 0.0) for g, t in zip(g64, gj)]
    kk = [_nt(x, x) for x in k]
    a = [jnp.where(strict, b * m * dc, 0.0) for b, m, dc in zip(b64, kk, decay)]
    eg = [jnp.exp(g) for g in g128]
    rhs = [jnp.concatenate([b * x, (b * e) * y], axis=1) for b, x, e, y in zip(b128, v, eg, k)]
    sol = solve_all(a, rhs)
    qk = [jnp.where(causal, _nt(x, y) * dc, 0.0) for x, y, dc in zip(q, k, decay)]
    return ([s[:, :HD] for s in sol], [s[:, HD:] for s in sol], qk, [x * e for x, e in zip(q, eg)],
            [x * jnp.exp(gl - g) for x, gl, g in zip(k, gl128, g128)], [jnp.exp(gl) for gl in gl128])


def _all_head_gates(bgv):
    return tuple(list(z) for z in zip(*[_head_gates(bgv, h) for h in range(NH)]))


def _head_gates(bgv, h):
    lane = _iota((CH, 128), 1)
    row = _iota((CH, 128), 0)
    bcol = jnp.sum(jnp.where(lane == h, bgv, 0.0), axis=1, keepdims=True)
    gcol = jnp.sum(jnp.where(lane == NH + h, bgv, 0.0), axis=1, keepdims=True)
    g128 = jnp.broadcast_to(gcol, (CH, 128))
    gl128 = jnp.broadcast_to(jnp.sum(jnp.where(row == CH - 1, g128, 0.0), axis=0, keepdims=True), (CH, 128))
    return (g128, jnp.broadcast_to(gcol, (CH, CH)), gl128,
            jnp.broadcast_to(bcol, (CH, 128)), jnp.broadcast_to(bcol, (CH, CH)))


def _chunk_specs():
    g = LOCAL_CHUNKS
    row = pl.BlockSpec((g * CH, D), lambda i: (i, 0))
    small = pl.BlockSpec((g * CH, 128), lambda i: (i, 0))
    qk = pl.BlockSpec((g * NH, CH, CH), lambda i: (i, 0, 0))
    eg = pl.BlockSpec((g, NH, 128), lambda i: (i, 0, 0))
    return row, small, qk, eg


def _chunk_heads():
    return [(slice(c * CH, (c + 1) * CH), slice(h * HD, (h + 1) * HD), c, h)
            for c in range(LOCAL_CHUNKS) for h in range(NH)]


def _all_gates(bg_ref):
    per_chunk = [_all_head_gates(bg_ref[c * CH:(c + 1) * CH, :]) for c in range(LOCAL_CHUNKS)]
    return tuple(sum((list(pc[j]) for pc in per_chunk), []) for j in range(5))


def _dn_local_fwd(q, k, v, bg, name):
    t = q.shape[0]
    n = t // CH
    pairs = _chunk_heads()

    def body(q_ref, k_ref, v_ref, bg_ref, u_ref, w_ref, qk_ref, qd_ref, kd_ref, eg_ref, inv_ref):
        def solve_all(mats, rhs):
            invs = _neumann_inverses(mats)
            for p in range(len(pairs)):
                inv_ref[p] = invs[p]
            return [_dot3(m, r, NN) for m, r in zip(invs, rhs)]

        u, w, qk, qd, kd, egl = _chunk_local(
            solve_all, [q_ref[r, hc] for r, hc, _, _ in pairs], [k_ref[r, hc] for r, hc, _, _ in pairs],
            [v_ref[r, hc] for r, hc, _, _ in pairs], *_all_gates(bg_ref))
        for p, (r, hc, c, h) in enumerate(pairs):
            u_ref[r, hc] = u[p]
            w_ref[r, hc] = w[p].astype(w_ref.dtype)
            qd_ref[r, hc] = qd[p].astype(qd_ref.dtype)
            kd_ref[r, hc] = kd[p].astype(kd_ref.dtype)
            qk_ref[p] = qk[p].astype(qk_ref.dtype)
            eg_ref[c, h:h + 1, :] = egl[p][0:1, :]

    row, small, qkb, egb = _chunk_specs()
    return pl.pallas_call(
        body, grid=(n // LOCAL_CHUNKS,), in_specs=[row, row, row, small],
        out_specs=[row, row, qkb, row, row, egb, qkb],
        out_shape=[SDS((t, D), F32), SDS((t, D), BF16), SDS((n * NH, CH, CH), BF16), SDS((t, D), BF16),
                   SDS((t, D), BF16), SDS((n, NH, 128), F32), SDS((n * NH, CH, CH), F32)],
        compiler_params=_params(("parallel",)), name=name,
    )(q, k, v, bg)


def _dn_local_bwd(q, k, v, bg, inv, du, dw, dqk, dqd, dkd, deg, name):
    t = q.shape[0]
    n = t // CH
    pairs = _chunk_heads()

    def body(q_ref, k_ref, v_ref, bg_ref, inv_ref, du_ref, dw_ref, dqk_ref, dqd_ref, dkd_ref, deg_ref,
             dq_ref, dk_ref, dv_ref, dbg_ref):
        lane = _iota((CH, 128), 1)
        row = _iota((CH, 128), 0)
        first = jnp.where(row == 0, 1.0, 0.0)
        solves = [_solve_with(inv_ref[p]) for p in range(len(pairs))]

        def solve_all(mats, rhs):
            return [f(m, r) for f, m, r in zip(solves, mats, rhs)]

        _, vjp = jax.vjp(functools.partial(_chunk_local, solve_all),
                         [q_ref[r, hc] for r, hc, _, _ in pairs], [k_ref[r, hc] for r, hc, _, _ in pairs],
                         [v_ref[r, hc] for r, hc, _, _ in pairs], *_all_gates(bg_ref))
        cts = ([du_ref[r, hc].astype(F32) for r, hc, _, _ in pairs],
               [dw_ref[r, hc].astype(F32) for r, hc, _, _ in pairs],
               [dqk_ref[p] for p in range(len(pairs))],
               [dqd_ref[r, hc].astype(F32) for r, hc, _, _ in pairs],
               [dkd_ref[r, hc].astype(F32) for r, hc, _, _ in pairs],
               [jnp.broadcast_to(deg_ref[c, h:h + 1, :], (CH, 128)) * first for _, _, c, h in pairs])
        dq, dk, dv, dg128, dg64, dgl, db128, db64 = vjp(cts)
        acc = [jnp.zeros((CH, 128), F32) for _ in range(LOCAL_CHUNKS)]
        for p, (r, hc, c, h) in enumerate(pairs):
            dq_ref[r, hc] = dq[p]
            dk_ref[r, hc] = dk[p]
            dv_ref[r, hc] = dv[p]
            dg = jnp.sum(dg128[p], axis=1, keepdims=True) + jnp.sum(dg64[p], axis=1, keepdims=True)
            tot = jnp.sum(jnp.sum(dgl[p], axis=0, keepdims=True), axis=1, keepdims=True)
            dg = dg + jnp.where(row[:, 0:1] == CH - 1, tot, 0.0)
            db = jnp.sum(db128[p], axis=1, keepdims=True) + jnp.sum(db64[p], axis=1, keepdims=True)
            acc[c] = acc[c] + jnp.where(lane == h, db, 0.0) + jnp.where(lane == NH + h, dg, 0.0)
        for c in range(LOCAL_CHUNKS):
            dbg_ref[c * CH:(c + 1) * CH, :] = acc[c]

    row, small, qkb, egb = _chunk_specs()
    return pl.pallas_call(
        body, grid=(n // LOCAL_CHUNKS,), in_specs=[row, row, row, small, qkb, row, row, qkb, row, row, egb],
        out_specs=[row, row, row, small],
        out_shape=[SDS((t, D), F32)] * 3 + [SDS((t, 128), F32)],
        compiler_params=_params(("parallel",)), name=name,
    )(q, k, v, bg, inv, du, dw, dqk, dqd, dkd, deg)


def _state_step(s, u, w, qk, qd, kd, egl):
    ws = [_nn(a, b) for a, b in zip(w, s)]
    v_new = [a - b for a, b in zip(u, ws)]
    qs = [_nn(a, b) for a, b in zip(qd, s)]
    intra = [_nn(a, b) for a, b in zip(qk, v_new)]
    upd = [_tn(a, b) for a, b in zip(kd, v_new)]
    return [a * e + b for a, e, b in zip(s, egl, upd)], [a + b for a, b in zip(qs, intra)]


def _dn_scan_fwd(u, w, qk, qd, kd, eg, name):
    t = u.shape[0]
    n = t // CH
    g = SCAN_CHUNKS

    def body(u_ref, w_ref, qk_ref, qd_ref, kd_ref, eg_ref, o_ref, save_ref, s_ref):
        @pl.when(pl.program_id(0) == 0)
        def _():
            s_ref[...] = jnp.zeros_like(s_ref)

        cols = [slice(h * HD, (h + 1) * HD) for h in range(NH)]
        s = [s_ref[h] for h in range(NH)]
        for c in range(g):
            rows = slice(c * CH, (c + 1) * CH)
            for h in range(NH):
                save_ref[c, h] = s[h].astype(save_ref.dtype)
            s, o = _state_step(
                s, [u_ref[rows, hc] for hc in cols], [w_ref[rows, hc].astype(F32) for hc in cols],
                [qk_ref[c * NH + h].astype(F32) for h in range(NH)], [qd_ref[rows, hc].astype(F32) for hc in cols],
                [kd_ref[rows, hc].astype(F32) for hc in cols], [eg_ref[c, h:h + 1, :] for h in range(NH)])
            for h, hc in enumerate(cols):
                o_ref[rows, hc] = o[h]
        for h in range(NH):
            s_ref[h] = s[h]

    row = pl.BlockSpec((g * CH, D), lambda i: (i, 0))
    qkb = pl.BlockSpec((g * NH, CH, CH), lambda i: (i, 0, 0))
    egb = pl.BlockSpec((g, NH, 128), lambda i: (i, 0, 0))
    return pl.pallas_call(
        body, grid=(n // g,), in_specs=[row, row, qkb, row, row, egb],
        out_specs=[row, pl.BlockSpec((g, NH, HD, HD), lambda i: (i, 0, 0, 0))],
        out_shape=[SDS((t, D), F32), SDS((n, NH, HD, HD), BF16)],
        scratch_shapes=[pltpu.VMEM((NH, HD, HD), F32)],
        compiler_params=_params(("arbitrary",)), name=name,
    )(u, w, qk, qd, kd, eg)


def _dn_scan_bwd(u, w, qk, qd, kd, eg, saved, do, name):
    t = u.shape[0]
    n = t // CH
    g = SCAN_CHUNKS
    last = n // g - 1

    def body(u_ref, w_ref, qk_ref, qd_ref, kd_ref, eg_ref, sv_ref, do_ref,
             du_ref, dw_ref, dqk_ref, dqd_ref, dkd_ref, deg_ref, ds_ref):
        @pl.when(pl.program_id(0) == 0)
        def _():
            ds_ref[...] = jnp.zeros_like(ds_ref)

        cols = [slice(h * HD, (h + 1) * HD) for h in range(NH)]
        ds = [ds_ref[h] for h in range(NH)]
        for c in reversed(range(g)):
            rows = slice(c * CH, (c + 1) * CH)
            _, vjp = jax.vjp(
                _state_step, [sv_ref[c, h].astype(F32) for h in range(NH)], [u_ref[rows, hc] for hc in cols],
                [w_ref[rows, hc].astype(F32) for hc in cols], [qk_ref[c * NH + h].astype(F32) for h in range(NH)],
                [qd_ref[rows, hc].astype(F32) for hc in cols], [kd_ref[rows, hc].astype(F32) for hc in cols],
                [eg_ref[c, h:h + 1, :] for h in range(NH)])
            ds, du, dw, dqk, dqd, dkd, deg = vjp((ds, [do_ref[rows, hc] for hc in cols]))
            for h, hc in enumerate(cols):
                du_ref[rows, hc] = du[h].astype(du_ref.dtype)
                dw_ref[rows, hc] = dw[h].astype(dw_ref.dtype)
                dqk_ref[c * NH + h] = dqk[h]
                dqd_ref[rows, hc] = dqd[h].astype(dqd_ref.dtype)
                dkd_ref[rows, hc] = dkd[h].astype(dkd_ref.dtype)
                deg_ref[c, h:h + 1, :] = deg[h]
        for h in range(NH):
            ds_ref[h] = ds[h]

    row = pl.BlockSpec((g * CH, D), lambda i: (last - i, 0))
    qkb = pl.BlockSpec((g * NH, CH, CH), lambda i: (last - i, 0, 0))
    egb = pl.BlockSpec((g, NH, 128), lambda i: (last - i, 0, 0))
    return pl.pallas_call(
        body, grid=(n // g,),
        in_specs=[row, row, qkb, row, row, egb,
                  pl.BlockSpec((g, NH, HD, HD), lambda i: (last - i, 0, 0, 0)), row],
        out_specs=[row, row, qkb, row, row, egb],
        out_shape=[SDS((t, D), BF16), SDS((t, D), BF16), SDS((n * NH, CH, CH), F32), SDS((t, D), BF16),
                   SDS((t, D), BF16), SDS((n, NH, 128), F32)],
        scratch_shapes=[pltpu.VMEM((NH, HD, HD), F32)],
        compiler_params=_params(("arbitrary",)), name=name,
    )(u, w, qk, qd, kd, eg, saved, do)


def _ada_fwd(c_all, ada_w, ada_b, name):
    ncol = ada_w.shape[1]

    def body(c_ref, w_ref, b_ref, o_ref):
        o_ref[...] = _dg(_silu(c_ref[...]), w_ref[...], NN, HI) + b_ref[...]

    return pl.pallas_call(body, out_shape=SDS((NDEV, ncol), F32),
                          compiler_params=pltpu.CompilerParams(vmem_limit_bytes=VMEM_LIMIT), name=name,
                          )(c_all, ada_w, ada_b)


def _ada_bwd(c_all_t, dmod, name):
    ncol = dmod.shape[1]

    def body(c_ref, d_ref, o_ref):
        sc = _silu(c_ref[...])
        acc = sc[:, 0:1] * d_ref[0:1, :]
        for b in range(1, NDEV):
            acc = acc + sc[:, b:b + 1] * d_ref[b:b + 1, :]
        o_ref[...] = acc

    return pl.pallas_call(body, out_shape=SDS((D, ncol), F32),
                          compiler_params=pltpu.CompilerParams(vmem_limit_bytes=VMEM_LIMIT), name=name,
                          )(c_all_t, dmod)


def _sum_devices(parts, out_dtype, name):
    _, r, c = parts.shape
    tr = TR if r % TR == 0 else r

    def body(p_ref, o_ref):
        acc = p_ref[0].astype(F32)
        for i in range(1, NDEV):
            acc = acc + p_ref[i].astype(F32)
        o_ref[...] = acc.astype(o_ref.dtype)

    return pl.pallas_call(
        body, grid=(r // tr,), in_specs=[pl.BlockSpec((NDEV, tr, c), lambda i: (0, i, 0))],
        out_specs=pl.BlockSpec((tr, c), lambda i: (i, 0)), out_shape=SDS((r, c), out_dtype),
        compiler_params=_params(("parallel",)), name=name,
    )(parts)


def _adam_tiles(r, c):
    if r % 8 == 0:
        return _pick(r, (256, 352, 128, 8)), c
    return r, (256 if c % 256 == 0 else c)


def _adam_math(w, gv, m, v):
    m_new = ADAM_B1 * m + (1.0 - ADAM_B1) * gv
    v_new = ADAM_B2 * v + (1.0 - ADAM_B2) * (gv * gv)
    bc1 = 1.0 - ADAM_B1 ** ADAM_STEP
    bc2 = 1.0 - ADAM_B2 ** ADAM_STEP
    return -ADAM_LR * ((m_new / bc1) / (jnp.sqrt(v_new / bc2) + ADAM_EPS) + ADAM_WD * w), m_new, v_new


def _adamw(w, g, m, v, name):
    r, c = w.shape
    tr, tc = _adam_tiles(r, c)

    def body(w_ref, g_ref, m_ref, v_ref, d_ref, nm_ref, nv_ref):
        d_ref[...], nm_ref[...], nv_ref[...] = _adam_math(w_ref[...], g_ref[...], m_ref[...], v_ref[...])

    spec = pl.BlockSpec((tr, tc), lambda i, j: (i, j))
    return pl.pallas_call(
        body, grid=(r // tr, c // tc), in_specs=[spec] * 4, out_specs=[spec] * 3,
        out_shape=[SDS((r, c), F32)] * 3, compiler_params=_params(("parallel", "parallel")), name=name,
    )(w, g, m, v)


def _reduce_adamw(parts, w, m, v, name):
    r, c = w.shape
    tr, tc = _adam_tiles(r, c)

    def body(p_ref, w_ref, m_ref, v_ref, g_ref, d_ref, nm_ref, nv_ref):
        gv = p_ref[0].astype(F32)
        for i in range(1, NDEV):
            gv = gv + p_ref[i].astype(F32)
        g_ref[...] = gv
        d_ref[...], nm_ref[...], nv_ref[...] = _adam_math(w_ref[...], gv, m_ref[...], v_ref[...])

    spec = pl.BlockSpec((tr, tc), lambda i, j: (i, j))
    return pl.pallas_call(
        body, grid=(r // tr, c // tc),
        in_specs=[pl.BlockSpec((NDEV, tr, tc), lambda i, j: (0, i, j))] + [spec] * 3, out_specs=[spec] * 4,
        out_shape=[SDS((r, c), F32)] * 4, compiler_params=_params(("parallel", "parallel")), name=name,
    )(parts, w, m, v)


ANY = pl.BlockSpec(memory_space=pl.ANY)
MESH = pl.DeviceIdType.MESH


def _all_gather(xs, name, after=None):
    n = len(xs)
    extra = [] if after is None else [after]

    def body(*refs):
        x_refs, out_refs = refs[:n], refs[n + len(extra):2 * n + len(extra)]
        send_sems, recv_sems, local_sems = refs[-3:]
        mx, my, mc = lax.axis_index("x"), lax.axis_index("y"), lax.axis_index("c")
        me, sibling = (mx, my, mc), (mx, my, 1 - mc)
        chips = [(1 - mx, my), (mx, 1 - my), (1 - mx, 1 - my)]

        def rows(a, px, py, pc):
            return out_refs[a].at[4 * px + 2 * py + pc]

        def copy(a, k, block, to, src=None):
            return pltpu.make_async_remote_copy(
                src_ref=rows(a, *block) if src is None else src, dst_ref=rows(a, *block),
                send_sem=send_sems.at[a, k], recv_sem=recv_sems.at[a, k], device_id=to, device_id_type=MESH)

        mine = [pltpu.make_async_copy(x_refs[a], rows(a, *me), local_sems.at[a]) for a in range(n)]
        for cp in mine:
            cp.start()
        first = []
        for a in range(n):
            first.append(copy(a, 0, me, sibling, src=x_refs[a]))
            first += [copy(a, 1 + j, me, (*chip, mc), src=x_refs[a]) for j, chip in enumerate(chips)]
        for cp in first:
            cp.start()
        passed = []
        for a in range(n):
            for j, chip in enumerate(chips):
                copy(a, 1 + j, (*chip, mc), me).wait_recv()
                passed.append(copy(a, 4 + j, (*chip, mc), sibling))
                passed[-1].start()
        for a in range(n):
            copy(a, 0, sibling, me).wait_recv()
            for j, chip in enumerate(chips):
                copy(a, 4 + j, (*chip, 1 - mc), me).wait_recv()
        for cp in first + passed:
            cp.wait_send()
        for cp in mine:
            cp.wait()

    return pl.pallas_call(
        body, out_shape=[SDS((NDEV,) + x.shape, x.dtype) for x in xs], in_specs=[ANY] * (n + len(extra)),
        out_specs=[ANY] * n,
        scratch_shapes=[pltpu.SemaphoreType.DMA((n, 7)), pltpu.SemaphoreType.DMA((n, 7)),
                        pltpu.SemaphoreType.DMA((n,))],
        name=name,
    )(*xs, *extra)


HBM = pl.BlockSpec(memory_space=pltpu.HBM)
SEM = pl.BlockSpec(memory_space=pltpu.SEMAPHORE)
EFFECT = pltpu.SideEffectType.DATAFLOW_SIDE_EFFECTING


def _peers():
    mx, my, mc = lax.axis_index("x"), lax.axis_index("y"), lax.axis_index("c")
    out = []
    for k in range(1, NDEV):
        out.append((1 - mx if k & 4 else mx, 1 - my if k & 2 else my, 1 - mc if k & 1 else mc))
    return 4 * mx + 2 * my + mc, out


NEAR = (0, 1, 3, 5)


def _push_start(srcs, sliced, name, after=None, near=()):
    n = len(srcs)
    extra = [] if after is None else [after]
    lands = [lax.empty(s.shape if sliced else (NDEV,) + s.shape, s.dtype) for s in srcs]

    def body(*refs):
        src_refs, land_refs = refs[:n], refs[n:2 * n]
        outs = refs[2 * n + len(extra):]
        send_sems, recv_sems = outs[:n], outs[n:2 * n]
        token = refs[-1]
        me, peers = _peers()
        for a in range(n):
            for k, (px, py, pc) in enumerate(peers):
                if a in near and k not in NEAR:
                    continue
                src = src_refs[a].at[4 * px + 2 * py + pc] if sliced else src_refs[a]
                pltpu.make_async_remote_copy(
                    src_ref=src, dst_ref=land_refs[a].at[me], send_sem=send_sems[a].at[k],
                    recv_sem=recv_sems[a].at[k], device_id=(px, py, pc), device_id_type=MESH).start()
            pltpu.make_async_copy(src_refs[a].at[me] if sliced else src_refs[a], land_refs[a].at[me],
                                  send_sems[a].at[NDEV - 1]).start()
        token[...] = jnp.zeros_like(token)

    outs = pl.pallas_call(
        body, name=name,
        out_shape=([pltpu.SemaphoreType.DMA((NDEV,))] * n + [pltpu.SemaphoreType.DMA((NDEV - 1,))] * n
                   + [pltpu.HBM(s.shape, s.dtype) for s in srcs] + [pltpu.HBM(l.shape, l.dtype) for l in lands]
                   + [SDS((8, 128), F32)]),
        in_specs=[HBM] * (2 * n) + [pl.BlockSpec(memory_space=pl.ANY)] * len(extra),
        out_specs=[SEM] * (2 * n) + [HBM] * (2 * n) + [pl.BlockSpec(memory_space=pltpu.VMEM)],
        input_output_aliases={i: 2 * n + i for i in range(2 * n)},
        compiler_params=pltpu.CompilerParams(has_side_effects=EFFECT),
    )(*[pltpu.with_memory_space_constraint(s, pltpu.HBM) for s in srcs],
      *[pltpu.with_memory_space_constraint(l, pltpu.HBM) for l in lands], *extra)
    sends, recvs = outs[:n], outs[n:2 * n]
    src_thru, land_thru = outs[2 * n:3 * n], outs[3 * n:4 * n]
    return [(sends[a], recvs[a], src_thru[a], land_thru[a]) for a in range(n)], outs[-1]


def _push_wait(started, sliced, after, name, near=()):
    n = len(started)
    afters = list(after) if isinstance(after, (list, tuple)) else [after]

    def body(*refs):
        src_refs, land_refs = refs[:n], refs[n:2 * n]
        send_sems, recv_sems = refs[2 * n:3 * n], refs[3 * n:4 * n]
        me, peers = _peers()
        for a in range(n):
            for k, (px, py, pc) in enumerate(peers):
                if a in near and k not in NEAR:
                    continue
                src = src_refs[a].at[4 * px + 2 * py + pc] if sliced else src_refs[a]
                cp = pltpu.make_async_remote_copy(
                    src_ref=src, dst_ref=land_refs[a].at[me], send_sem=send_sems[a].at[k],
                    recv_sem=recv_sems[a].at[k], device_id=(px, py, pc), device_id_type=MESH)
                cp.wait_send()
                cp.wait_recv()
            pltpu.make_async_copy(src_refs[a].at[me] if sliced else src_refs[a], land_refs[a].at[me],
                                  send_sems[a].at[NDEV - 1]).wait()

    srcs = [s[2] for s in started]
    lands = [s[3] for s in started]
    outs = pl.pallas_call(
        body, name=name,
        out_shape=[pltpu.HBM(s.shape, s.dtype) for s in srcs] + [pltpu.HBM(l.shape, l.dtype) for l in lands],
        in_specs=[HBM] * (2 * n) + [SEM] * (2 * n) + [pl.BlockSpec(memory_space=pl.ANY)] * len(afters),
        out_specs=[HBM] * (2 * n),
        input_output_aliases={i: i for i in range(2 * n)},
        compiler_params=pltpu.CompilerParams(has_side_effects=EFFECT),
    )(*srcs, *lands, *[s[0] for s in started], *[s[1] for s in started], *afters)
    return outs[n:]


def _relay_to_sibling(land, name):
    def body(_, land_ref, send_sems, recv_sems):
        mx, my, mc = lax.axis_index("x"), lax.axis_index("y"), lax.axis_index("c")
        chips = [(1 - mx, my), (mx, 1 - my), (1 - mx, 1 - my)]

        def copy(j, core):
            slot = land_ref.at[4 * chips[j][0] + 2 * chips[j][1] + core]
            return pltpu.make_async_remote_copy(
                src_ref=slot, dst_ref=slot, send_sem=send_sems.at[j], recv_sem=recv_sems.at[j],
                device_id=(mx, my, 1 - mc), device_id_type=MESH)

        mine = [copy(j, mc) for j in range(3)]
        for cp in mine:
            cp.start()
        for j in range(3):
            copy(j, 1 - mc).wait_recv()
        for cp in mine:
            cp.wait_send()

    return pl.pallas_call(
        body, out_shape=SDS(land.shape, land.dtype), in_specs=[ANY], out_specs=ANY, input_output_aliases={0: 0},
        scratch_shapes=[pltpu.SemaphoreType.DMA((3,)), pltpu.SemaphoreType.DMA((3,))], name=name,
    )(land)


def _cols_from_blocks(blocks):
    _, rows, w = blocks.shape
    return blocks.transpose(1, 0, 2).reshape(rows, NDEV * w)


def _cols_to_blocks(full):
    rows, total = full.shape
    return full.reshape(rows, NDEV, total // NDEV).transpose(1, 0, 2)


def _mix_pad(wt):
    xp, q, k, v, z, ba, gp, gd = jnp.split(wt, (512, 1536, 2560, 3584, 4608, 4624, 5648), axis=0)
    pad = jnp.zeros((MIXP - OFF_BA - 16, wt.shape[1]), wt.dtype)
    return jnp.concatenate([q, k, v, z, gp, gd, xp, ba, pad], axis=0)


def _mix_unpad(wt):
    q, k, v, z, gp, gd, xp, ba = (wt[OFF_Q:OFF_K], wt[OFF_K:OFF_V], wt[OFF_V:OFF_Z], wt[OFF_Z:OFF_GP],
                                  wt[OFF_GP:OFF_GD], wt[OFF_GD:OFF_XP], wt[OFF_XP:OFF_BA], wt[OFF_BA:OFF_BA + 16])
    return jnp.concatenate([xp, q, k, v, z, ba, gp, gd], axis=0)


def _lane_row(vec8):
    return jnp.zeros((1, 128), F32).at[0, NH:2 * NH].set(vec8)


def _ffn_fwd(x, h, gate, w_in, w_out, tag, next_norm=None, token=None, start_more=None, final=None):
    if isinstance(w_in, tuple):
        w_in, = _push_wait([w_in], False, h, f"{tag}_gather_wait_in")
    w_in = w_in.reshape(2 * FH, D)
    u, a = _swiglu_up(h, w_in, f"{tag}_up", after=token)
    w_out, = _push_wait([w_out], False, a, f"{tag}_gather_wait_out")
    w_out = w_out.reshape(FH, D)
    if final is not None:
        return _matmul_residual_loss(a, w_out, x, gate, 0.5, *final, f"{tag}_down_loss"), (h, u, a, None), w_in, w_out
    outs = _matmul_residual(a, w_out, x, gate, 0.5, a_blk=True, norm=next_norm, name=f"{tag}_down",
                            after=None if start_more is None else start_more(h))
    return outs[0], (h, u, a, outs[1]), w_in, w_out, (outs[2] if next_norm else None)


def _ffn_bwd(dx_out, dy, x, g, scale, w_in, w_out, saved, tag, below=None):
    h, u, a, _ = saved
    t = x.shape[0]
    dw_out = _matmul(a, dy, ta=True, a_blk=True, out_dtype=BF16, name=f"{tag}_down_dw")
    sent_out, token = _push_start([dw_out.reshape(NDEV, FH // NDEV, D)], True, f"{tag}_grad_start_out")
    du = _swiglu_down_bwd(dy, w_out, u, f"{tag}_down_dx", after=token).reshape(NDEV, t, FB)
    dw_in = _matmul(du, h, ta=True, a_blk=True, out_dtype=BF16, name=f"{tag}_up_dw")
    sent_in, token = _push_start([dw_in.reshape(NDEV, FB, D)], True, f"{tag}_grad_start_in")
    dh = _matmul(du, w_in, a_blk=True, out_dtype=F32, name=f"{tag}_up_dx", after=token)
    return _norm_mod_bwd(x, g, scale, dh, dx_out, f"{tag}_norm_bwd", below), sent_in + sent_out


def kernel(x, c, ada_w, ada_b, norm_g, ffn1_w_in, ffn1_w_out, ffn2_w_in, ffn2_w_out, mix_w_in, conv_w, a_log, dt_bias, dn_norm_g, pool_w, pool_scale, pool_proj, dn_proj, mix_w_out, final_g, loss_target, m_ada_w, m_ada_b, m_norm_g, m_ffn1_w_in, m_ffn1_w_out, m_ffn2_w_in, m_ffn2_w_out, m_mix_w_in, m_conv_w, m_a_log, m_dt_bias, m_dn_norm_g, m_pool_w, m_pool_scale, m_pool_proj, m_dn_proj, m_mix_w_out, m_final_g, v_ada_w, v_ada_b, v_norm_g, v_ffn1_w_in, v_ffn1_w_out, v_ffn2_w_in, v_ffn2_w_out, v_mix_w_in, v_conv_w, v_a_log, v_dt_bias, v_dn_norm_g, v_pool_w, v_pool_scale, v_pool_proj, v_dn_proj, v_mix_w_out, v_final_g):
    me = 4 * lax.axis_index("x") + 2 * lax.axis_index("y") + lax.axis_index("c")
    x0 = x[0]
    target = loss_target[0]
    t = x0.shape[0]

    big = [ffn1_w_in[0], ffn1_w_out[0], ffn2_w_in[0], ffn2_w_out[0], mix_w_in[0], pool_proj[0], dn_proj[0],
           mix_w_out[0]]
    small = jnp.concatenate([c.reshape(8, 128), conv_w[0].reshape(12, 128), norm_g[0].reshape(3, 128),
                             jnp.zeros((1, 128), F32)], axis=0)
    small_all, = _all_gather([small], "gather_small")
    c_all = small_all[:, 0:8, :].reshape(NDEV, D)
    conv_full = small_all[:, 8:20, :].reshape(NDEV, 4, 384).transpose(1, 0, 2).reshape(4, 3 * D)
    norm_full = small_all[:, 20:23, :].reshape(NDEV, 3, 128).transpose(1, 0, 2).reshape(3, D)

    ncol = ada_w.shape[2]
    ada_b_mine = lax.dynamic_slice(ada_b, (0, me * ncol), (1, ncol))
    mod_cols = _ada_fwd(c_all, ada_w[0], ada_b_mine, "ada_fwd")
    transposed = (0, 2, 4)
    payload = [(w.T if i in transposed else w).astype(BF16) for i, w in enumerate(big)]
    mod_all, w_in1 = _all_gather([mod_cols, payload[0]], "gather_mod_first_weight")
    started, token = _push_start([payload[1], payload[4]], False, "gather_start", after=mod_all, near=(1,))
    started = {1: started[0], 4: started[1]}

    def start_rest(h):
        more, token = _push_start([payload[i] for i in (5, 6, 7, 2, 3)], False, "gather_start_rest", after=h)
        started.update(zip((5, 6, 7, 2, 3), more))
        return token

    mod = lax.dynamic_index_in_dim(mod_all, me, axis=1, keepdims=False).reshape(9, D)
    shift = [mod[3 * s:3 * s + 1] for s in range(3)]
    scale = [mod[3 * s + 1:3 * s + 2] for s in range(3)]
    gate = [mod[3 * s + 2:3 * s + 3] for s in range(3)]
    ng = [norm_full[s:s + 1] for s in range(3)]
    fg = final_g.reshape(1, D)
    al_row = _lane_row(a_log[0])
    dt_row = _lane_row(dt_bias[0])
    gn = dn_norm_g
    pw = pool_w[0]
    ps = pool_scale

    h0 = _norm_mod_fwd(x0, ng[0], shift[0], scale[0], "ffn1_norm", after=token)
    x1, saved1, w_in1, w_out1, h1 = _ffn_fwd(x0, h0, gate[0], w_in1, started[1], "ffn1",
                                             (ng[1], shift[1], scale[1]), token, start_rest)

    seg, = _push_wait([started[4]], False, h1, "mix_gather_wait", near=(0,))
    w_mix = _mix_pad(_relay_to_sibling(seg, "mix_gather_relay").reshape(MIX_RAW, D))
    proj = _matmul(h1, w_mix, tb=True, out_dtype=F32, name="mix_in")
    qh, kh, vh, bg = _dn_pre_fwd(proj, conv_full, al_row, dt_row, "dn_pre")
    seg = _push_wait([started[i] for i in (5, 6, 7)], False, qh, "mix_gather_wait_rest")
    w_pp = _cols_from_blocks(seg[0])
    w_dn = seg[1].reshape(D, D)
    w_mo = seg[2].reshape(D, D)
    ya = _pool_fwd(proj, pw, ps, w_pp, "pool_fwd")
    u, w, qk, qd, kd, eg, inv = _dn_local_fwd(qh, kh, vh, bg, "dn_local")
    o, s_saved = _dn_scan_fwd(u, w, qk, qd, kd, eg, "dn_scan")
    ob, yb = _dn_post_fwd(o, proj, gn, w_dn, "dn_post_out")
    merged, x2, mix_y, h2 = _merge_out(ya, yb, proj, w_mo, x1, gate[1], (ng[2], shift[2], scale[2]), "mix_out")

    (loss_row, dx3, dfg, dy2, dgate2), saved2, w_in2, w_out2 = _ffn_fwd(
        x2, h2, gate[2], started[2], started[3], "ffn2", final=(fg, target))

    (dx2, dsh2, dsc2, dng2, dmy, dgate1), sent2 = _ffn_bwd(dx3, dy2, x2, ng[2], scale[2], w_in2, w_out2, saved2,
                                                           "ffn2", (mix_y, gate[1], 1.0))

    dmerged = _matmul(dmy, w_mo, tb=True, out_dtype=BF16, name="mix_out_dx")
    dw_mo = _matmul(merged, dmy, ta=True, out_dtype=BF16, name="mix_out_dw")
    dproj = lax.empty((t, MIXP), BF16)
    dya, dyb, dproj = _merge_bwd(dmerged, ya, yb, proj, dproj, "merge_bwd")
    dob = _matmul(dyb, w_dn, tb=True, out_dtype=F32, name="dn_out_dx")
    dw_dn = _matmul(ob, dyb, ta=True, out_dtype=BF16, name="dn_out_dw")
    do, dproj, dgn = _dn_post_bwd(o, proj, gn, dob, dproj, "dn_post_bwd")
    du, dw, dqk, dqd, dkd, deg = _dn_scan_bwd(u, w, qk, qd, kd, eg, s_saved, do, "dn_scan_bwd")
    dqh, dkh, dvh, dbg = _dn_local_bwd(qh, kh, vh, bg, inv, du, dw, dqk, dqd, dkd, deg, "dn_local_bwd")
    dconv, dproj, dal, ddt = _dn_pre_bwd_act(proj, conv_full, al_row, dt_row, dqh, dkh, dvh, dbg, dproj,
                                             "dn_pre_bwd_act")
    dproj, dcw = _dn_pre_bwd_conv(proj, conv_full, dconv, dproj, "dn_pre_bwd_conv")
    dwin, dpl, dpw, dps, dpp = _pool_bwd_local(proj, pw, ps, w_pp, dya, "pool_bwd_local")
    dproj = _pool_bwd_window(dwin, dpl, dproj, "pool_bwd_window")
    dw_mix = _matmul(dproj, h1, ta=True, out_dtype=BF16, name="mix_in_dw")
    sent1, token = _push_start(
        [_mix_unpad(dw_mix).reshape(NDEV, MIX_RAW // NDEV, D), _cols_to_blocks(dpp.astype(BF16)),
         dw_dn.reshape(NDEV, -1, D), dw_mo.reshape(NDEV, -1, D)], True, "mix_grad_start")
    dh1 = _matmul(dproj, w_mix, out_dtype=F32, name="mix_in_dx", after=token)
    dx1, dsh1, dsc1, dng1, dy0, dgate0 = _norm_mod_bwd(x1, ng[1], scale[1], dh1, dx2, "mix_norm_bwd",
                                                       (saved1[3], gate[0], 0.5))

    (dx0, dsh0, dsc0, dng0), sent0 = _ffn_bwd(dx1, dy0, x0, ng[0], scale[0], w_in1, w_out1, saved1, "ffn1")

    dmod = jnp.concatenate([dsh0, dsc0, dgate0, dsh1, dsc1, dgate1, dsh2, dsc2, dgate2], axis=1).reshape(-1)
    flat = jnp.concatenate([
        dmod, dal[0, NH:2 * NH], ddt[0, NH:2 * NH], dgn.reshape(-1), dps.reshape(-1), dfg.reshape(-1),
        dpw.reshape(-1), jnp.concatenate([dng0, dng1, dng2], axis=0).reshape(-1), dcw.reshape(-1),
        loss_row[0, 0:1]])
    nflat = 90 * D
    flat = jnp.concatenate([flat, jnp.zeros((nflat - flat.shape[0],), F32)]).reshape(90, D)
    sent_small, small_token = _push_start([flat], False, "small_grad_start")

    def small_grads(flat_all):
        tot = _sum_devices(flat_all, F32, "sum_small_grads").reshape(-1)
        dmod_all = flat_all.reshape(NDEV, nflat)[:, :9 * D]
        dmod_cols = lax.dynamic_slice(dmod_all, (0, me * ncol), (NDEV, ncol))
        g_ada_w = _ada_bwd(c_all.T, dmod_cols, "ada_bwd")
        p = 0
        pieces = {}
        for nm, size in (("ada_b", 9 * D), ("a_log", NH), ("dt_bias", NH), ("dn_norm_g", HD), ("pool_scale", PW),
                         ("final_g", D), ("pool_w", 4 * PG * PG), ("norm_g", 3 * D), ("conv_w", 12 * D),
                         ("loss", 1)):
            pieces[nm] = tot[p:p + size]
            p += size
        g_norm = lax.dynamic_slice(pieces["norm_g"].reshape(3, D), (0, me * 128), (3, 128))
        g_conv = lax.dynamic_slice(pieces["conv_w"].reshape(4, 3 * D), (0, me * 384), (4, 384))
        return pieces["loss"][0], {
            "ada_w": g_ada_w.reshape(ada_w.shape), "ada_b": pieces["ada_b"].reshape(ada_b.shape),
            "norm_g": g_norm.reshape(norm_g.shape), "conv_w": g_conv.reshape(conv_w.shape),
            "a_log": pieces["a_log"].reshape(a_log.shape), "dt_bias": pieces["dt_bias"].reshape(dt_bias.shape),
            "dn_norm_g": pieces["dn_norm_g"].reshape(dn_norm_g.shape),
            "pool_w": pieces["pool_w"].reshape(pool_w.shape),
            "pool_scale": pieces["pool_scale"].reshape(pool_scale.shape),
            "final_g": pieces["final_g"].reshape(final_g.shape),
        }

    grads = {}
    weights = {"ada_w": ada_w, "ada_b": ada_b, "norm_g": norm_g, "ffn1_w_in": ffn1_w_in, "ffn1_w_out": ffn1_w_out,
               "ffn2_w_in": ffn2_w_in, "ffn2_w_out": ffn2_w_out, "mix_w_in": mix_w_in, "conv_w": conv_w,
               "a_log": a_log, "dt_bias": dt_bias, "dn_norm_g": dn_norm_g, "pool_w": pool_w,
               "pool_scale": pool_scale, "pool_proj": pool_proj, "dn_proj": dn_proj, "mix_w_out": mix_w_out,
               "final_g": final_g}
    m_in = {"ada_w": m_ada_w, "ada_b": m_ada_b, "norm_g": m_norm_g, "ffn1_w_in": m_ffn1_w_in,
            "ffn1_w_out": m_ffn1_w_out, "ffn2_w_in": m_ffn2_w_in, "ffn2_w_out": m_ffn2_w_out,
            "mix_w_in": m_mix_w_in, "conv_w": m_conv_w, "a_log": m_a_log, "dt_bias": m_dt_bias,
            "dn_norm_g": m_dn_norm_g, "pool_w": m_pool_w, "pool_scale": m_pool_scale, "pool_proj": m_pool_proj,
            "dn_proj": m_dn_proj, "mix_w_out": m_mix_w_out, "final_g": m_final_g}
    v_in = {"ada_w": v_ada_w, "ada_b": v_ada_b, "norm_g": v_norm_g, "ffn1_w_in": v_ffn1_w_in,
            "ffn1_w_out": v_ffn1_w_out, "ffn2_w_in": v_ffn2_w_in, "ffn2_w_out": v_ffn2_w_out,
            "mix_w_in": v_mix_w_in, "conv_w": v_conv_w, "a_log": v_a_log, "dt_bias": v_dt_bias,
            "dn_norm_g": v_dn_norm_g, "pool_w": v_pool_w, "pool_scale": v_pool_scale, "pool_proj": v_pool_proj,
            "dn_proj": v_dn_proj, "mix_w_out": v_mix_w_out, "final_g": v_final_g}

    names = list(weights)
    large = ("ada_w", "ffn1_w_in", "ffn1_w_out", "ffn2_w_in", "ffn2_w_out", "mix_w_in", "pool_proj", "dn_proj",
             "mix_w_out")
    delta, new_m, new_v = {}, {}, {}

    flipped = ("ffn1_w_in", "ffn2_w_in", "mix_w_in")

    def views(nm):
        shp = weights[nm].shape
        two_d = (shp[-2], shp[-1])
        if nm in flipped:
            return (lambda a: a.reshape(two_d).T), (lambda a: a.T.reshape(shp))
        return (lambda a: a.reshape(two_d)), (lambda a: a.reshape(shp))

    def reduce_update(sent, group, after, tag):
        done = []
        for nm, r in zip(group, _push_wait(sent, True, after, f"{tag}_grad_wait")):
            view, back = views(nm)
            g_, d_, m_, v_ = _reduce_adamw(r, view(weights[nm]), view(m_in[nm]), view(v_in[nm]), f"adamw_{nm}")
            grads[nm], delta[nm], new_m[nm], new_v[nm] = back(g_), back(d_), back(m_), back(v_)
            done.append(d_)
        return done

    done = reduce_update(sent2, ("ffn2_w_in", "ffn2_w_out"), small_token, "ffn2")
    done += reduce_update(sent1, ("mix_w_in", "pool_proj", "dn_proj", "mix_w_out"), done, "mix")
    flat_all, = _push_wait(sent_small, False, done, "small_grad_wait")
    loss, small = small_grads(flat_all)
    grads.update(small)
    view, back = views("ada_w")
    done, m_, v_ = _adamw(view(ada_w), view(grads["ada_w"]), view(m_ada_w), view(v_ada_w), "adamw_ada_w")
    delta["ada_w"], new_m["ada_w"], new_v["ada_w"] = back(done), back(m_), back(v_)
    reduce_update(sent0, ("ffn1_w_in", "ffn1_w_out"), done, "ffn1")
    rest = [nm for nm in names if nm not in large]
    total = sum(weights[nm].size for nm in rest)
    padded = -(-total // D) * D

    def pack(tree, fill):
        flat_ = jnp.concatenate([tree[nm].reshape(-1) for nm in rest])
        return jnp.concatenate([flat_, jnp.full((padded - total,), fill, F32)]).reshape(-1, D)

    d_, m_, v_ = _adamw(pack(weights, 0.0), pack(grads, 0.0), pack(m_in, 0.0), pack(v_in, 1.0), "adamw_small")
    p = 0
    for nm in rest:
        size = weights[nm].size
        shp = weights[nm].shape
        delta[nm] = d_.reshape(-1)[p:p + size].reshape(shp)
        new_m[nm] = m_.reshape(-1)[p:p + size].reshape(shp)
        new_v[nm] = v_.reshape(-1)[p:p + size].reshape(shp)
        p += size

    grad_x = dx0.reshape(x.shape)
    return (loss, grad_x, *[grads[nm] for nm in names], *[delta[nm] for nm in names],
            *[new_m[nm] for nm in names], *[new_v[nm] for nm in names])
```

```python
import functools

import jax
import jax.numpy as jnp
from jax import lax
from jax.experimental import pallas as pl
from jax.experimental.pallas import tpu as pltpu

F32 = jnp.float32
BF16 = jnp.bfloat16
SDS = jax.ShapeDtypeStruct
HI = lax.Precision.HIGHEST

D = 1024
FH = 2816
FB = 704
NH = 8
HD = 128
CH = 64
SCAN_CHUNKS = 8
LOCAL_CHUNKS = 2
NDEV = 8
PW = 512
PG = 128
RMS_EPS = 1e-6
L2_EPS = 1e-6
TR = 512
HALO = 16
VMEM_LIMIT = 56 * 1024 * 1024
MATMUL_VMEM = 40 * 1024 * 1024

MIXP = 6912
OFF_Q, OFF_K, OFF_V, OFF_Z, OFF_GP, OFF_GD, OFF_XP, OFF_BA = 0, 1024, 2048, 3072, 4096, 5120, 6144, 6656
MIX_RAW = 6672

ADAM_LR = 0.001
ADAM_B1 = 0.9
ADAM_B2 = 0.999
ADAM_EPS = 1e-08
ADAM_WD = 0.01
ADAM_STEP = 10

NN = (((1,), (0,)), ((), ()))
NT = (((1,), (1,)), ((), ()))
TN = (((0,), (0,)), ((), ()))


def _dg(a, b, dims, prec=None):
    return lax.dot_general(a, b, dims, precision=prec, preferred_element_type=F32)


def _make_dots(prec):
    @jax.custom_vjp
    def nn(a, b):
        return _dg(a, b, NN, prec)

    @jax.custom_vjp
    def nt(a, b):
        return _dg(a, b, NT, prec)

    @jax.custom_vjp
    def tn(a, b):
        return _dg(a, b, TN, prec)

    nn.defvjp(lambda a, b: (nn(a, b), (a, b)), lambda r, d: (nt(d, r[1]), tn(r[0], d)))
    nt.defvjp(lambda a, b: (nt(a, b), (a, b)), lambda r, d: (nn(d, r[1]), tn(d, r[0])))
    tn.defvjp(lambda a, b: (tn(a, b), (a, b)), lambda r, d: (nt(r[1], d), nn(r[0], d)))
    return nn, nt, tn


_nn, _nt, _tn = _make_dots(None)


def _params(sem):
    return pltpu.CompilerParams(dimension_semantics=sem, vmem_limit_bytes=VMEM_LIMIT)


def _sigmoid(x):
    return 1.0 / (1.0 + jnp.exp(-x))


def _silu(x):
    return x * _sigmoid(x)


def _dsilu(x):
    s = _sigmoid(x)
    return s * (1.0 + x * (1.0 - s))


def _pick(n, cands):
    for c in cands:
        if n % c == 0:
            return c
    raise ValueError(f"no tile for {n}")


def _iota(shape, dim):
    return lax.broadcasted_iota(jnp.int32, shape, dim)


def _matmul(a, b, *, ta=False, tb=False, a_blk=False, b_blk=False, o_blk=False, tm=None, tn=None, tk=None,
            out_dtype, name, after=None):
    if a_blk:
        nb, r, cb = a.shape
        if ta:
            k_dim, m_dim, tm = r, nb * cb, cb
        else:
            m_dim, k_dim, tk = r, nb * cb, cb
    else:
        k_dim, m_dim = a.shape if ta else a.shape[::-1]
    if b_blk:
        nb, r, cb = b.shape
        if tb:
            n_dim, tk = r, cb
            assert nb * cb == k_dim
        else:
            n_dim, tn = nb * cb, cb
            assert r == k_dim
    else:
        n_dim = b.shape[0] if tb else b.shape[1]
    tn = tn or _pick(n_dim, (1024, 768, 512, 256, 128))
    out_bytes = jnp.dtype(out_dtype).itemsize

    def vmem(tm_, tk_):
        return 4 * tk_ * (tm_ + tn) + tm_ * tn * (4 + 2 * out_bytes)

    k_cands = [tk] if tk else [c for c in (k_dim, 4096, 3456, 2816, 2304, 2048, 1024, 512, 256)
                               if c <= k_dim and k_dim % c == 0]
    m_cands = [tm] if tm else [c for c in (2048, 1024, 768, 512, 256, 128) if m_dim % c == 0]
    base = next((c for c in m_cands if c <= 1024), m_cands[-1])
    tk = next((c for c in k_cands if vmem(base, c) <= MATMUL_VMEM), k_cands[-1])
    tm = next((c for c in m_cands if vmem(c, tk) <= MATMUL_VMEM), m_cands[-1])
    nk = k_dim // tk
    dims = ((((0,) if ta else (1,)), ((1,) if tb else (0,))), ((), ()))

    def body(a_ref, b_ref, *rest):
        o_ref, acc_ref = rest[-2:]
        k = pl.program_id(2)

        @pl.when(k == 0)
        def _():
            acc_ref[...] = jnp.zeros_like(acc_ref)

        acc_ref[...] += lax.dot_general(a_ref[...].astype(BF16), b_ref[...].astype(BF16), dims,
                                        preferred_element_type=F32)

        @pl.when(k == nk - 1)
        def _():
            o_ref[...] = acc_ref[...].astype(o_ref.dtype)

    if a_blk:
        a_spec = (pl.BlockSpec((None, tk, tm), lambda i, j, k: (i, k, 0)) if ta
                  else pl.BlockSpec((None, tm, tk), lambda i, j, k: (k, i, 0)))
    else:
        a_spec = (pl.BlockSpec((tk, tm), lambda i, j, k: (k, i)) if ta
                  else pl.BlockSpec((tm, tk), lambda i, j, k: (i, k)))
    if b_blk:
        b_spec = (pl.BlockSpec((None, tn, tk), lambda i, j, k: (k, j, 0)) if tb
                  else pl.BlockSpec((None, tk, tn), lambda i, j, k: (j, k, 0)))
    else:
        b_spec = (pl.BlockSpec((tn, tk), lambda i, j, k: (j, k)) if tb
                  else pl.BlockSpec((tk, tn), lambda i, j, k: (k, j)))
    if o_blk:
        o_spec = pl.BlockSpec((None, tm, tn), lambda i, j, k: (j, i, 0))
        o_shape = SDS((n_dim // tn, m_dim, tn), out_dtype)
    else:
        o_spec = pl.BlockSpec((tm, tn), lambda i, j, k: (i, j))
        o_shape = SDS((m_dim, n_dim), out_dtype)
    return pl.pallas_call(
        body, grid=(m_dim // tm, n_dim // tn, nk),
        in_specs=[a_spec, b_spec] + ([] if after is None else [pl.BlockSpec(memory_space=pl.ANY)]),
        out_specs=o_spec,
        out_shape=o_shape,
        scratch_shapes=[pltpu.VMEM((tm, tn), F32)],
        compiler_params=_params(("parallel", "parallel", "arbitrary")),
        name=name,
    )(a, b, *([] if after is None else [after]))


def _matmul_residual(a, b, x, gate, coef, *, a_blk=False, norm=None, name, after=None):
    m_dim = a.shape[-2]
    tm = _pick(m_dim, (1024, 512))
    if a_blk:
        nk, _, tk = a.shape
        a_spec = pl.BlockSpec((None, tm, tk), lambda i, k: (k, i, 0))
    else:
        tk = a.shape[1]
        nk = 1
        a_spec = pl.BlockSpec((tm, tk), lambda i, k: (i, 0))
    extra = [] if after is None else [after]
    vecs = [gate] + (list(norm) if norm else [])

    def body(a_ref, b_ref, x_ref, gate_ref, *rest):
        vec_refs = rest[:len(vecs) - 1]
        outs = rest[len(vecs) - 1 + len(extra):]
        acc_ref = outs[-1]
        k = pl.program_id(1)

        @pl.when(k == 0)
        def _():
            acc_ref[...] = jnp.zeros_like(acc_ref)

        acc_ref[...] += _dg(a_ref[...], b_ref[...], NN)

        @pl.when(k == nk - 1)
        def _():
            y = acc_ref[...]
            xn = x_ref[...] + (coef * gate_ref[...]) * y
            outs[0][...] = xn
            outs[1][...] = y.astype(outs[1].dtype)
            if norm:
                g_ref, sh_ref, sc_ref = vec_refs
                r = lax.rsqrt(jnp.mean(xn * xn, axis=-1, keepdims=True) + RMS_EPS)
                outs[2][...] = (((xn * r) * g_ref[...]) * (1.0 + sc_ref[...]) + sh_ref[...]).astype(outs[2].dtype)

    row = pl.BlockSpec((tm, D), lambda i, k: (i, 0))
    vec = pl.BlockSpec((1, D), lambda i, k: (0, 0))
    return pl.pallas_call(
        body, grid=(m_dim // tm, nk),
        in_specs=[a_spec, pl.BlockSpec((tk, D), lambda i, k: (k, 0)), row] + [vec] * len(vecs)
        + [pl.BlockSpec(memory_space=pl.ANY)] * len(extra),
        out_specs=[row] * (3 if norm else 2),
        out_shape=[SDS((m_dim, D), F32), SDS((m_dim, D), BF16)] + ([SDS((m_dim, D), BF16)] if norm else []),
        scratch_shapes=[pltpu.VMEM((tm, D), F32)],
        compiler_params=_params(("parallel", "arbitrary")), name=name,
    )(a, b, x, *vecs, *extra)


def _matmul_residual_loss(a, b, x, gate, coef, fg, target, name):
    nk, m_dim, tk = a.shape
    tm = _pick(m_dim, (1024, 512))
    nt = m_dim // tm

    def body(a_ref, b_ref, x_ref, gate_ref, g_ref, t_ref, loss_ref, dx_ref, dg_ref, dy_ref, dgate_ref,
             acc_ref, sq_ref):
        i, k = pl.program_id(0), pl.program_id(1)

        @pl.when(k == 0)
        def _():
            acc_ref[...] = jnp.zeros_like(acc_ref)

        @pl.when(jnp.logical_and(i == 0, k == 0))
        def _():
            sq_ref[...] = jnp.zeros_like(sq_ref)
            dg_ref[...] = jnp.zeros_like(dg_ref)
            dgate_ref[...] = jnp.zeros_like(dgate_ref)

        acc_ref[...] += _dg(a_ref[...], b_ref[...], NN)

        @pl.when(k == nk - 1)
        def _():
            y = acc_ref[...]
            scaled_gate = coef * gate_ref[...]
            xn = x_ref[...] + scaled_gate * y
            gv = g_ref[...]
            r = lax.rsqrt(jnp.mean(xn * xn, axis=-1, keepdims=True) + RMS_EPS)
            n = xn * r
            err = n * gv - t_ref[...]
            sq_ref[...] += jnp.sum(err * err, axis=0, keepdims=True)
            dout = err * (1.0 / D)
            dg_ref[...] += jnp.sum(dout * n, axis=0, keepdims=True)
            dn = dout * gv
            dxv = r * (dn - n * jnp.mean(dn * n, axis=-1, keepdims=True))
            dx_ref[...] = dxv
            dy_ref[...] = (scaled_gate * dxv).astype(dy_ref.dtype)
            dgate_ref[...] += jnp.sum((coef * dxv) * y, axis=0, keepdims=True)

        @pl.when(jnp.logical_and(i == nt - 1, k == nk - 1))
        def _():
            tot = jnp.sum(sq_ref[...], axis=1, keepdims=True) * (0.5 / D)
            loss_ref[...] = jnp.broadcast_to(tot, loss_ref.shape)

    row = pl.BlockSpec((tm, D), lambda i, k: (i, 0))
    vec = pl.BlockSpec((1, D), lambda i, k: (0, 0))
    return pl.pallas_call(
        body, grid=(nt, nk),
        in_specs=[pl.BlockSpec((None, tm, tk), lambda i, k: (k, i, 0)), pl.BlockSpec((tk, D), lambda i, k: (k, 0)),
                  row, vec, vec, row],
        out_specs=[pl.BlockSpec((1, 128), lambda i, k: (0, 0)), row, vec, row, vec],
        out_shape=[SDS((1, 128), F32), SDS((m_dim, D), F32), SDS((1, D), F32), SDS((m_dim, D), BF16),
                   SDS((1, D), F32)],
        scratch_shapes=[pltpu.VMEM((tm, D), F32), pltpu.VMEM((1, D), F32)],
        compiler_params=_params(("arbitrary", "arbitrary")), name=name,
    )(a, b, x, gate, fg, target)


def _row(width, col=0):
    return pl.BlockSpec((TR, width), lambda i: (i, col))


def _vec(width):
    return pl.BlockSpec((1, width), lambda i: (0, 0))


def _norm_mod_fwd(x, g, shift, scale, name, after=None):
    t = x.shape[0]
    extra = [] if after is None else [after]

    def body(x_ref, g_ref, sh_ref, sc_ref, *rest):
        o_ref = rest[-1]
        xv = x_ref[...]
        r = lax.rsqrt(jnp.mean(xv * xv, axis=-1, keepdims=True) + RMS_EPS)
        o_ref[...] = (((xv * r) * g_ref[...]) * (1.0 + sc_ref[...]) + sh_ref[...]).astype(o_ref.dtype)

    return pl.pallas_call(
        body, grid=(t // TR,),
        in_specs=[_row(D), _vec(D), _vec(D), _vec(D)] + [pl.BlockSpec(memory_space=pl.ANY)] * len(extra),
        out_specs=_row(D),
        out_shape=SDS((t, D), BF16), compiler_params=_params(("parallel",)), name=name,
    )(x, g, shift, scale, *extra)


def _residual_branch_bwd(dxv, y_ref, gate_ref, coef, dy_ref, dgate_ref):
    dy_ref[...] = ((coef * gate_ref[...]) * dxv).astype(dy_ref.dtype)
    dgate_ref[...] += jnp.sum((coef * dxv) * y_ref[...], axis=0, keepdims=True)


def _norm_mod_bwd(x, g, scale, dh, dx_in, name, below=None):
    t = x.shape[0]
    lower = [] if below is None else list(below[:2])

    def body(x_ref, g_ref, sc_ref, dh_ref, dxi_ref, *rest):
        dx_ref, dsh_ref, dsc_ref, dg_ref = rest[len(lower):len(lower) + 4]

        @pl.when(pl.program_id(0) == 0)
        def _():
            for ref in rest[len(lower) + 1:]:
                if ref.shape[0] == 1:
                    ref[...] = jnp.zeros_like(ref)

        xv = x_ref[...]
        gv = g_ref[...]
        dh = dh_ref[...]
        r = lax.rsqrt(jnp.mean(xv * xv, axis=-1, keepdims=True) + RMS_EPS)
        n = xv * r
        dsh_ref[...] += jnp.sum(dh, axis=0, keepdims=True)
        dsc_ref[...] += jnp.sum(dh * (n * gv), axis=0, keepdims=True)
        tt = dh * (1.0 + sc_ref[...])
        dg_ref[...] += jnp.sum(tt * n, axis=0, keepdims=True)
        dn = tt * gv
        dxv = dxi_ref[...] + r * (dn - n * jnp.mean(dn * n, axis=-1, keepdims=True))
        dx_ref[...] = dxv
        if below is not None:
            _residual_branch_bwd(dxv, rest[0], rest[1], below[2], rest[-2], rest[-1])

    more_in = [] if below is None else [_row(D), _vec(D)]
    more_out = [] if below is None else [_row(D), _vec(D)]
    more_shape = [] if below is None else [SDS((t, D), BF16), SDS((1, D), F32)]
    return pl.pallas_call(
        body, grid=(t // TR,), in_specs=[_row(D), _vec(D), _vec(D), _row(D), _row(D)] + more_in,
        out_specs=[_row(D), _vec(D), _vec(D), _vec(D)] + more_out,
        out_shape=[SDS((t, D), F32), SDS((1, D), F32), SDS((1, D), F32), SDS((1, D), F32)] + more_shape,
        compiler_params=_params(("arbitrary",)), name=name,
    )(x, g, scale, dh, dx_in, *lower)


def _swiglu_up(h, w_in, name, after=None):
    t = h.shape[0]
    tm = _pick(t, (1024, 512, 256))
    half = NDEV // 2
    extra = [] if after is None else [after]

    def body(h_ref, wg_ref, wu_ref, *rest):
        u_ref, a_ref = rest[-2:]
        hv = h_ref[...]
        gate = _dg(hv, wg_ref[...], NT)
        up = _dg(hv, wu_ref[...], NT)
        u_ref[0] = gate.astype(u_ref.dtype)
        u_ref[1] = up.astype(u_ref.dtype)
        a_ref[...] = (_silu(gate) * up).astype(a_ref.dtype)

    return pl.pallas_call(
        body, grid=(t // tm, half),
        in_specs=[pl.BlockSpec((tm, D), lambda i, j: (i, 0)),
                  pl.BlockSpec((FB, D), lambda i, j: (j, 0)),
                  pl.BlockSpec((FB, D), lambda i, j: (j + half, 0))]
        + [pl.BlockSpec(memory_space=pl.ANY)] * len(extra),
        out_specs=[pl.BlockSpec((2, None, tm, FB), lambda i, j: (0, j, i, 0)),
                   pl.BlockSpec((None, tm, FB), lambda i, j: (j, i, 0))],
        out_shape=[SDS((2, half, t, FB), BF16), SDS((half, t, FB), BF16)],
        compiler_params=_params(("parallel", "parallel")), name=name,
    )(h, w_in, w_in, *extra)


def _swiglu_down_bwd(dy, w_out, u, name, after=None):
    t = dy.shape[0]
    tm = _pick(t, (1024, 512, 256))
    half = NDEV // 2
    extra = [] if after is None else [after]
    pair = pl.BlockSpec((2, None, tm, FB), lambda i, j: (0, j, i, 0))

    def body(dy_ref, w_ref, u_ref, *rest):
        o_ref = rest[-1]
        da = _dg(dy_ref[...], w_ref[...], NT)
        gate = u_ref[0].astype(F32)
        o_ref[0] = (da * u_ref[1].astype(F32) * _dsilu(gate)).astype(o_ref.dtype)
        o_ref[1] = (da * _silu(gate)).astype(o_ref.dtype)

    return pl.pallas_call(
        body, grid=(t // tm, half),
        in_specs=[pl.BlockSpec((tm, D), lambda i, j: (i, 0)), pl.BlockSpec((FB, D), lambda i, j: (j, 0)), pair]
        + [pl.BlockSpec(memory_space=pl.ANY)] * len(extra),
        out_specs=pair, out_shape=SDS((2, half, t, FB), BF16),
        compiler_params=_params(("parallel", "parallel")), name=name,
    )(dy, w_out, u, *extra)


def _halo_prev(width, col):
    per = TR // HALO
    return pl.BlockSpec((HALO, width), lambda i: (jnp.maximum(i * per - 1, 0), col))


def _halo_next(width, col, nt):
    per = TR // HALO
    return pl.BlockSpec((HALO, width), lambda i: (jnp.minimum((i + 1) * per, nt * per - 1), col))


def _pool_windows(ext, tile_index):
    rows = _iota((TR, PG), 0) + tile_index * TR + 1
    pooled, counts = [], []
    for gi in range(4):
        w = 2 << gi
        e = ext[:, gi * PG:(gi + 1) * PG]
        s = e
        step = 1
        while step < w:
            s = s + pltpu.roll(s, step, 0)
            step *= 2
        cnt = jnp.minimum(rows, w).astype(F32)
        pooled.append(s[HALO:] / cnt - e[HALO:])
        counts.append(cnt)
    return pooled, counts


def _pool_fwd(proj, pool_w, pool_scale, pool_proj, name):
    t = proj.shape[0]
    xcol = OFF_XP // PW

    def body(x_ref, h_ref, pw_ref, ps_ref, pp_ref, o_ref):
        i = pl.program_id(0)
        halo = jnp.where(i > 0, h_ref[...], 0.0)
        ext = jnp.concatenate([halo, x_ref[...]], axis=0)
        pooled, _ = _pool_windows(ext, i)
        mixed = [_dg(pooled[g].astype(BF16), pw_ref[g].astype(BF16), NN) for g in range(4)]
        ypre = jnp.concatenate(mixed, axis=1) * ps_ref[...]
        o_ref[...] = _dg(ypre.astype(BF16), pp_ref[...], NN)

    return pl.pallas_call(
        body, grid=(t // TR,),
        in_specs=[_row(PW, xcol), _halo_prev(PW, xcol),
                  pl.BlockSpec((4, PG, PG), lambda i: (0, 0, 0)), _vec(PW),
                  pl.BlockSpec((PW, D), lambda i: (0, 0))],
        out_specs=_row(D), out_shape=SDS((t, D), F32),
        compiler_params=_params(("parallel",)), name=name,
    )(proj, proj, pool_w, pool_scale, pool_proj)


def _pool_bwd_local(proj, pool_w, pool_scale, pool_proj, dya, name):
    t = proj.shape[0]
    xcol = OFF_XP // PW

    def body(x_ref, h_ref, pw_ref, ps_ref, pp_ref, dya_ref, dwin_ref, dpl_ref, dpw_ref, dps_ref, dpp_ref):
        i = pl.program_id(0)

        @pl.when(i == 0)
        def _():
            dpw_ref[...] = jnp.zeros_like(dpw_ref)
            dps_ref[...] = jnp.zeros_like(dps_ref)
            dpp_ref[...] = jnp.zeros_like(dpp_ref)

        halo = jnp.where(i > 0, h_ref[...], 0.0)
        ext = jnp.concatenate([halo, x_ref[...]], axis=0)
        pooled, counts = _pool_windows(ext, i)
        mixed = jnp.concatenate(
            [_dg(pooled[g].astype(BF16), pw_ref[g].astype(BF16), NN) for g in range(4)], axis=1)
        ps = ps_ref[...]
        ypre = mixed * ps
        dyab = dya_ref[...].astype(BF16)
        dypre = _dg(dyab, pp_ref[...], NT)
        dpp_ref[...] += _dg(ypre.astype(BF16), dyab, TN)
        dps_ref[...] += jnp.sum(dypre * mixed, axis=0, keepdims=True)
        dmixed = dypre * ps
        for g in range(4):
            dm = dmixed[:, g * PG:(g + 1) * PG].astype(BF16)
            dpw_ref[g] += _dg(pooled[g].astype(BF16), dm, TN)
            dpooled = _dg(dm, pw_ref[g].astype(BF16), NT)
            dwin_ref[:, g * PG:(g + 1) * PG] = dpooled / counts[g]
            dpl_ref[:, g * PG:(g + 1) * PG] = dpooled

    return pl.pallas_call(
        body, grid=(t // TR,),
        in_specs=[_row(PW, xcol), _halo_prev(PW, xcol),
                  pl.BlockSpec((4, PG, PG), lambda i: (0, 0, 0)), _vec(PW),
                  pl.BlockSpec((PW, D), lambda i: (0, 0)), _row(D)],
        out_specs=[_row(PW), _row(PW), pl.BlockSpec((4, PG, PG), lambda i: (0, 0, 0)), _vec(PW),
                   pl.BlockSpec((PW, D), lambda i: (0, 0))],
        out_shape=[SDS((t, PW), F32), SDS((t, PW), F32), SDS((4, PG, PG), F32), SDS((1, PW), F32),
                   SDS((PW, D), F32)],
        compiler_params=_params(("arbitrary",)), name=name,
    )(proj, proj, pool_w, pool_scale, pool_proj, dya)


def _pool_bwd_window(dwin, dpl, dproj, name):
    t = dwin.shape[0]
    nt = t // TR
    ext_rows = TR + HALO

    def body(dw_ref, h_ref, dp_ref, _, o_ref):
        i = pl.program_id(0)
        halo = jnp.where(i < nt - 1, h_ref[...], 0.0)
        ext = jnp.concatenate([dw_ref[...], halo], axis=0)
        for gi in range(4):
            w = 2 << gi
            s = ext[:, gi * PG:(gi + 1) * PG]
            step = 1
            while step < w:
                s = s + pltpu.roll(s, ext_rows - step, 0)
                step *= 2
            o_ref[:, gi * PG:(gi + 1) * PG] = (s[:TR] - dp_ref[:, gi * PG:(gi + 1) * PG]).astype(o_ref.dtype)

    return pl.pallas_call(
        body, grid=(nt,),
        in_specs=[_row(PW), _halo_next(PW, 0, nt), _row(PW), pl.BlockSpec(memory_space=pl.ANY)],
        out_specs=_into(PW, OFF_XP), out_shape=SDS(dproj.shape, dproj.dtype), input_output_aliases={3: 0},
        compiler_params=_params(("parallel",)), name=name,
    )(dwin, dwin, dpl, dproj)


def _conv_group(ext, cw_ref, cols):
    acc = cw_ref[3:4, cols] * ext
    for j in range(3):
        acc = acc + cw_ref[j:j + 1, cols] * pltpu.roll(ext, 3 - j, 0)
    return acc[HALO:]


def _gate_terms(raw, al, dt):
    beta = _sigmoid(raw)
    xg = raw + dt
    sp = jnp.maximum(xg, 0.0) + jnp.log(1.0 + jnp.exp(-jnp.abs(xg)))
    g = -jnp.exp(al) * sp
    return beta, g, _sigmoid(xg)


def _dn_pre_fwd(proj, conv_w, al_row, dt_row, name):
    t = proj.shape[0]

    def body(x_ref, h_ref, cw_ref, ba_ref, al_ref, dt_ref, q_ref, k_ref, v_ref, bg_ref):
        i = pl.program_id(0)
        keep = i > 0
        for grp in range(24):
            cols = slice(grp * HD, (grp + 1) * HD)
            ext = jnp.concatenate([jnp.where(keep, h_ref[:, cols], 0.0), x_ref[:, cols]], axis=0)
            s = _silu(_conv_group(ext, cw_ref, cols))
            seg, head = divmod(grp, NH)
            hc = slice(head * HD, (head + 1) * HD)
            if seg == 0:
                q_ref[:, hc] = s * lax.rsqrt(jnp.sum(s * s, axis=-1, keepdims=True) + L2_EPS) * (HD ** -0.5)
            elif seg == 1:
                k_ref[:, hc] = s * lax.rsqrt(jnp.sum(s * s, axis=-1, keepdims=True) + L2_EPS)
            else:
                v_ref[:, hc] = s
        lane = _iota((TR, 128), 1)
        rowc = _iota((TR, 128), 0) % CH
        beta, g, _ = _gate_terms(ba_ref[...], al_ref[...], dt_ref[...])
        step = 1
        while step < CH:
            g = g + jnp.where(rowc >= step, pltpu.roll(g, step, 0), 0.0)
            step *= 2
        bg_ref[...] = jnp.where(lane < NH, beta, jnp.where(lane < 2 * NH, g, 0.0))

    return pl.pallas_call(
        body, grid=(t // TR,),
        in_specs=[_row(3 * D, 0), _halo_prev(3 * D, 0), pl.BlockSpec((4, 3 * D), lambda i: (0, 0)),
                  _row(128, OFF_BA // 128), _vec(128), _vec(128)],
        out_specs=[_row(D), _row(D), _row(D), _row(128)],
        out_shape=[SDS((t, D), F32), SDS((t, D), F32), SDS((t, D), F32), SDS((t, 128), F32)],
        compiler_params=_params(("parallel",)), name=name,
    )(proj, proj, conv_w, proj, al_row, dt_row)


def _dn_pre_bwd_act(proj, conv_w, al_row, dt_row, dq, dk, dv, dbg, dproj, name):
    t = proj.shape[0]

    def body(x_ref, h_ref, cw_ref, ba_ref, al_ref, dt_ref, dq_ref, dk_ref, dv_ref, dbg_ref, _,
             dc_ref, draw_ref, dal_ref, ddt_ref):
        i = pl.program_id(0)

        @pl.when(i == 0)
        def _():
            dal_ref[...] = jnp.zeros_like(dal_ref)
            ddt_ref[...] = jnp.zeros_like(ddt_ref)

        keep = i > 0
        for grp in range(24):
            cols = slice(grp * HD, (grp + 1) * HD)
            ext = jnp.concatenate([jnp.where(keep, h_ref[:, cols], 0.0), x_ref[:, cols]], axis=0)
            cv = _conv_group(ext, cw_ref, cols)
            seg, head = divmod(grp, NH)
            hc = slice(head * HD, (head + 1) * HD)
            if seg == 2:
                ds = dv_ref[:, hc]
            else:
                s = _silu(cv)
                r = lax.rsqrt(jnp.sum(s * s, axis=-1, keepdims=True) + L2_EPS)
                dy = dq_ref[:, hc] if seg == 0 else dk_ref[:, hc]
                c = (HD ** -0.5) if seg == 0 else 1.0
                ds = (c * r) * (dy - s * ((r * r) * jnp.sum(dy * s, axis=-1, keepdims=True)))
            dc_ref[:, cols] = ds * _dsilu(cv)
        lane = _iota((TR, 128), 1)
        rowc = _iota((TR, 128), 0) % CH
        isb = lane < NH
        isg = jnp.logical_and(lane >= NH, lane < 2 * NH)
        beta, g, sg = _gate_terms(ba_ref[...], al_ref[...], dt_ref[...])
        dbgv = dbg_ref[...]
        dg = dbgv
        step = 1
        while step < CH:
            dg = dg + jnp.where(rowc < CH - step, pltpu.roll(dg, TR - step, 0), 0.0)
            step *= 2
        da_raw = dg * (-jnp.exp(al_ref[...])) * sg
        draw = jnp.where(isb, dbgv * beta * (1.0 - beta), jnp.where(isg, da_raw, 0.0))
        draw_ref[:, :128] = draw.astype(draw_ref.dtype)
        draw_ref[:, 128:] = jnp.zeros((TR, MIXP - OFF_BA - 128), draw_ref.dtype)
        dal_ref[...] += jnp.sum(jnp.where(isg, dg * g, 0.0), axis=0, keepdims=True)
        ddt_ref[...] += jnp.sum(jnp.where(isg, da_raw, 0.0), axis=0, keepdims=True)

    return pl.pallas_call(
        body, grid=(t // TR,),
        in_specs=[_row(3 * D, 0), _halo_prev(3 * D, 0), pl.BlockSpec((4, 3 * D), lambda i: (0, 0)),
                  _row(128, OFF_BA // 128), _vec(128), _vec(128), _row(D), _row(D), _row(D), _row(128),
                  pl.BlockSpec(memory_space=pl.ANY)],
        out_specs=[_row(3 * D), _into(MIXP - OFF_BA, OFF_BA), _vec(128), _vec(128)],
        out_shape=[SDS((t, 3 * D), F32), SDS(dproj.shape, dproj.dtype), SDS((1, 128), F32), SDS((1, 128), F32)],
        input_output_aliases={10: 1},
        compiler_params=_params(("arbitrary",)), name=name,
    )(proj, proj, conv_w, proj, al_row, dt_row, dq, dk, dv, dbg, dproj)


def _dn_pre_bwd_conv(proj, conv_w, dconv, dproj, name):
    t = proj.shape[0]
    nt = t // TR
    ext_rows = TR + HALO

    def body(x_ref, h_ref, cw_ref, dc_ref, dn_ref, _, dx_ref, dcw_ref):
        i = pl.program_id(0)

        @pl.when(i == 0)
        def _():
            dcw_ref[...] = jnp.zeros_like(dcw_ref)

        keep_prev = i > 0
        keep_next = i < nt - 1
        for grp in range(24):
            cols = slice(grp * HD, (grp + 1) * HD)
            dct = dc_ref[:, cols]
            dext = jnp.concatenate([dct, jnp.where(keep_next, dn_ref[:, cols], 0.0)], axis=0)
            acc = cw_ref[3:4, cols] * dext
            for j in range(3):
                acc = acc + cw_ref[j:j + 1, cols] * pltpu.roll(dext, ext_rows - (3 - j), 0)
            dx_ref[:, cols] = acc[:TR].astype(dx_ref.dtype)
            xext = jnp.concatenate([jnp.where(keep_prev, h_ref[:, cols], 0.0), x_ref[:, cols]], axis=0)
            for j in range(4):
                xs = xext if j == 3 else pltpu.roll(xext, 3 - j, 0)
                dcw_ref[j:j + 1, cols] += jnp.sum(xs[HALO:] * dct, axis=0, keepdims=True)

    return pl.pallas_call(
        body, grid=(nt,),
        in_specs=[_row(3 * D, 0), _halo_prev(3 * D, 0), pl.BlockSpec((4, 3 * D), lambda i: (0, 0)),
                  _row(3 * D), _halo_next(3 * D, 0, nt), pl.BlockSpec(memory_space=pl.ANY)],
        out_specs=[_into(3 * D, OFF_Q), pl.BlockSpec((4, 3 * D), lambda i: (0, 0))],
        out_shape=[SDS(dproj.shape, dproj.dtype), SDS((4, 3 * D), F32)],
        input_output_aliases={5: 0},
        compiler_params=_params(("arbitrary",)), name=name,
    )(proj, proj, conv_w, dconv, dconv, dproj)


def _dn_post_fwd(o, proj, gn, w_out, name):
    t = o.shape[0]

    def body(o_ref, z_ref, g_ref, w_ref, out_ref, y_ref):
        gv = g_ref[...]
        for h in range(NH):
            hc = slice(h * HD, (h + 1) * HD)
            ov = o_ref[:, hc]
            r = lax.rsqrt(jnp.mean(ov * ov, axis=-1, keepdims=True) + RMS_EPS)
            out_ref[:, hc] = (((ov * r) * gv) * _silu(z_ref[:, hc])).astype(out_ref.dtype)
        y_ref[...] = _dg(out_ref[...], w_ref[...], NN)

    return pl.pallas_call(
        body, grid=(t // TR,),
        in_specs=[_row(D), _row(D, OFF_Z // D), _vec(HD), pl.BlockSpec((D, D), lambda i: (0, 0))],
        out_specs=[_row(D), _row(D)], out_shape=[SDS((t, D), BF16), SDS((t, D), F32)],
        compiler_params=_params(("parallel",)), name=name,
    )(o, proj, gn, w_out)


def _dn_post_bwd(o, proj, gn, dyb, w_out, dproj, name):
    t = o.shape[0]

    def body(o_ref, z_ref, g_ref, d_ref, w_ref, _, do_ref, dz_ref, dg_ref):
        @pl.when(pl.program_id(0) == 0)
        def _():
            dg_ref[...] = jnp.zeros_like(dg_ref)

        gv = g_ref[...]
        dob = _dg(d_ref[...], w_ref[...], NT)
        acc = jnp.zeros((1, HD), F32)
        for h in range(NH):
            hc = slice(h * HD, (h + 1) * HD)
            ov = o_ref[:, hc]
            zv = z_ref[:, hc]
            dv = dob[:, hc]
            r = lax.rsqrt(jnp.mean(ov * ov, axis=-1, keepdims=True) + RMS_EPS)
            n = ov * r
            dz_ref[:, hc] = (dv * (n * gv) * _dsilu(zv)).astype(dz_ref.dtype)
            dng = dv * _silu(zv)
            acc = acc + jnp.sum(dng * n, axis=0, keepdims=True)
            dn = dng * gv
            do_ref[:, hc] = r * (dn - n * jnp.mean(dn * n, axis=-1, keepdims=True))
        dg_ref[...] += acc

    return pl.pallas_call(
        body, grid=(t // TR,),
        in_specs=[_row(D), _row(D, OFF_Z // D), _vec(HD), _row(D), pl.BlockSpec((D, D), lambda i: (0, 0)),
                  pl.BlockSpec(memory_space=pl.ANY)],
        out_specs=[_row(D), _into(D, OFF_Z), _vec(HD)],
        out_shape=[SDS((t, D), F32), SDS(dproj.shape, dproj.dtype), SDS((1, HD), F32)],
        input_output_aliases={5: 1},
        compiler_params=_params(("arbitrary",)), name=name,
    )(o, proj, gn, dyb, w_out, dproj)


def _merge_out(ya, yb, proj, w_out, x, gate, norm, name):
    t = ya.shape[0]

    def body(a_ref, b_ref, gp_ref, gd_ref, w_ref, x_ref, gate_ref, g_ref, sh_ref, sc_ref,
             m_ref, xn_ref, y_ref, h_ref):
        merged = (_sigmoid(gp_ref[...]) * a_ref[...] + _sigmoid(gd_ref[...]) * b_ref[...]).astype(m_ref.dtype)
        m_ref[...] = merged
        y = _dg(merged, w_ref[...], NN)
        xn = x_ref[...] + gate_ref[...] * y
        xn_ref[...] = xn
        y_ref[...] = y.astype(y_ref.dtype)
        r = lax.rsqrt(jnp.mean(xn * xn, axis=-1, keepdims=True) + RMS_EPS)
        h_ref[...] = (((xn * r) * g_ref[...]) * (1.0 + sc_ref[...]) + sh_ref[...]).astype(h_ref.dtype)

    return pl.pallas_call(
        body, grid=(t // TR,),
        in_specs=[_row(D), _row(D), _row(D, OFF_GP // D), _row(D, OFF_GD // D),
                  pl.BlockSpec((D, D), lambda i: (0, 0)), _row(D), _vec(D), _vec(D), _vec(D), _vec(D)],
        out_specs=[_row(D)] * 4,
        out_shape=[SDS((t, D), BF16), SDS((t, D), F32), SDS((t, D), BF16), SDS((t, D), BF16)],
        compiler_params=_params(("parallel",)), name=name,
    )(ya, yb, proj, proj, w_out, x, gate, *norm)


def _into(width, offset):
    assert offset % width == 0
    return pl.BlockSpec((TR, width), lambda i: (i, offset // width))


def _merge_bwd(dm, ya, yb, proj, dproj, name):
    t = ya.shape[0]

    def body(d_ref, a_ref, b_ref, gp_ref, gd_ref, _, da_ref, db_ref, dg_ref):
        dv = d_ref[...]
        sp = _sigmoid(gp_ref[...])
        sd = _sigmoid(gd_ref[...])
        da_ref[...] = (dv * sp).astype(da_ref.dtype)
        db_ref[...] = (dv * sd).astype(db_ref.dtype)
        dg_ref[:, :D] = (dv * a_ref[...] * sp * (1.0 - sp)).astype(dg_ref.dtype)
        dg_ref[:, D:] = (dv * b_ref[...] * sd * (1.0 - sd)).astype(dg_ref.dtype)

    return pl.pallas_call(
        body, grid=(t // TR,),
        in_specs=[_row(D), _row(D), _row(D), _row(D, OFF_GP // D), _row(D, OFF_GD // D),
                  pl.BlockSpec(memory_space=pl.ANY)],
        out_specs=[_row(D), _row(D), _into(2 * D, OFF_GP)],
        out_shape=[SDS((t, D), BF16), SDS((t, D), BF16), SDS(dproj.shape, dproj.dtype)],
        input_output_aliases={5: 2},
        compiler_params=_params(("parallel",)), name=name,
    )(dm, ya, yb, proj, proj, dproj)


def _split2(x):
    hi = x.astype(BF16)
    return hi, (x - hi.astype(F32)).astype(BF16)


def _dot3(a, b, dims):
    ah, al = _split2(a)
    bh, bl = _split2(b)
    return _dg(ah, bh, dims) + (_dg(ah, bl, dims) + _dg(al, bh, dims))


def _neumann_inverses(mats):
    ri = _iota((CH, CH), 0)
    ci = _iota((CH, CH), 1)
    eye = jnp.where(ri == ci, 1.0, 0.0).astype(F32)
    xs = [-a for a in mats]
    ps = [eye + x for x in xs]
    for _ in range(5):
        xs = [_dot3(x, x, NN) for x in xs]
        ps = [p + _dot3(p, x, NN) for p, x in zip(ps, xs)]
    return ps


def _solve_with(inv):
    @jax.custom_vjp
    def solve(a, rhs):
        return _dot3(inv, rhs, NN)

    def fwd(a, rhs):
        sol = _dot3(inv, rhs, NN)
        return sol, sol

    def bwd(sol, d):
        drhs = _dot3(inv, d, TN)
        return -_dot3(drhs, sol, NT), drhs

    solve.defvjp(fwd, bwd)
    return solve


@jax.custom_vjp
def _rows_to_lanes(g64):
    ri = _iota((CH, CH), 0)
    ci = _iota((CH, CH), 1)
    diag = jnp.where(ri == ci, g64, 0.0)
    ones = jnp.ones((CH, CH), BF16)
    hi = diag.astype(BF16)
    rem = diag - hi.astype(F32)
    mid = rem.astype(BF16)
    lo = (rem - mid.astype(F32)).astype(BF16)
    return _dg(ones, hi, NN) + (_dg(ones, mid, NN) + _dg(ones, lo, NN))


def _rows_to_lanes_bwd(_, d):
    ri = _iota((CH, CH), 0)
    ci = _iota((CH, CH), 1)
    return (jnp.where(ri == ci, jnp.broadcast_to(jnp.sum(d, axis=0, keepdims=True), (CH, CH)), 0.0),)


_rows_to_lanes.defvjp(lambda g64: (_rows_to_lanes(g64), None), _rows_to_lanes_bwd)


def _chunk_local(solve_all, q, k, v, g128, g64, gl128, b128, b64):
    ri = _iota((CH, CH), 0)
    ci = _iota((CH, CH), 1)
    causal = ri >= ci
    strict = ri > ci
    gj = [_rows_to_lanes(g) for g in g64]
    decay = [jnp.where(causal, jnp.exp(jnp.where(causal, g - t, 0.0)), 0.0) for g, t in zip(g64, gj)]
    kk = [_nt(x, x) for x in k]
    a = [jnp.where(strict, b * m * dc, 0.0) for b, m, dc in zip(b64, kk, decay)]
    eg = [jnp.exp(g) for g in g128]
    rhs = [jnp.concatenate([b * x, (b * e) * y], axis=1) for b, x, e, y in zip(b128, v, eg, k)]
    sol = solve_all(a, rhs)
    qk = [jnp.where(causal, _nt(x, y) * dc, 0.0) for x, y, dc in zip(q, k, decay)]
    return ([s[:, :HD] for s in sol], [s[:, HD:] for s in sol], qk, [x * e for x, e in zip(q, eg)],
            [x * jnp.exp(gl - g) for x, gl, g in zip(k, gl128, g128)], [jnp.exp(gl) for gl in gl128])


def _all_head_gates(bgv):
    return tuple(list(z) for z in zip(*[_head_gates(bgv, h) for h in range(NH)]))


def _head_gates(bgv, h):
    lane = _iota((CH, 128), 1)
    row = _iota((CH, 128), 0)
    bcol = jnp.sum(jnp.where(lane == h, bgv, 0.0), axis=1, keepdims=True)
    gcol = jnp.sum(jnp.where(lane == NH + h, bgv, 0.0), axis=1, keepdims=True)
    g128 = jnp.broadcast_to(gcol, (CH, 128))
    gl128 = jnp.broadcast_to(jnp.sum(jnp.where(row == CH - 1, g128, 0.0), axis=0, keepdims=True), (CH, 128))
    return (g128, jnp.broadcast_to(gcol, (CH, CH)), gl128,
            jnp.broadcast_to(bcol, (CH, 128)), jnp.broadcast_to(bcol, (CH, CH)))


def _chunk_specs():
    g = LOCAL_CHUNKS
    row = pl.BlockSpec((g * CH, D), lambda i: (i, 0))
    small = pl.BlockSpec((g * CH, 128), lambda i: (i, 0))
    qk = pl.BlockSpec((g * NH, CH, CH), lambda i: (i, 0, 0))
    eg = pl.BlockSpec((g, NH, 128), lambda i: (i, 0, 0))
    return row, small, qk, eg


def _chunk_heads():
    return [(slice(c * CH, (c + 1) * CH), slice(h * HD, (h + 1) * HD), c, h)
            for c in range(LOCAL_CHUNKS) for h in range(NH)]


def _all_gates(bg_ref):
    per_chunk = [_all_head_gates(bg_ref[c * CH:(c + 1) * CH, :]) for c in range(LOCAL_CHUNKS)]
    return tuple(sum((list(pc[j]) for pc in per_chunk), []) for j in range(5))


def _dn_local_fwd(q, k, v, bg, name):
    t = q.shape[0]
    n = t // CH
    pairs = _chunk_heads()

    def body(q_ref, k_ref, v_ref, bg_ref, u_ref, w_ref, qk_ref, qd_ref, kd_ref, eg_ref, inv_ref):
        def solve_all(mats, rhs):
            invs = _neumann_inverses(mats)
            for p in range(len(pairs)):
                inv_ref[p] = invs[p]
            return [_dot3(m, r, NN) for m, r in zip(invs, rhs)]

        u, w, qk, qd, kd, egl = _chunk_local(
            solve_all, [q_ref[r, hc] for r, hc, _, _ in pairs], [k_ref[r, hc] for r, hc, _, _ in pairs],
            [v_ref[r, hc] for r, hc, _, _ in pairs], *_all_gates(bg_ref))
        for p, (r, hc, c, h) in enumerate(pairs):
            u_ref[r, hc] = u[p]
            w_ref[r, hc] = w[p].astype(w_ref.dtype)
            qd_ref[r, hc] = qd[p].astype(qd_ref.dtype)
            kd_ref[r, hc] = kd[p].astype(kd_ref.dtype)
            qk_ref[p] = qk[p].astype(qk_ref.dtype)
            eg_ref[c, h:h + 1, :] = egl[p][0:1, :]

    row, small, qkb, egb = _chunk_specs()
    return pl.pallas_call(
        body, grid=(n // LOCAL_CHUNKS,), in_specs=[row, row, row, small],
        out_specs=[row, row, qkb, row, row, egb, qkb],
        out_shape=[SDS((t, D), F32), SDS((t, D), BF16), SDS((n * NH, CH, CH), BF16), SDS((t, D), BF16),
                   SDS((t, D), BF16), SDS((n, NH, 128), F32), SDS((n * NH, CH, CH), F32)],
        compiler_params=_params(("parallel",)), name=name,
    )(q, k, v, bg)


def _dn_local_bwd(q, k, v, bg, inv, du, dw, dqk, dqd, dkd, deg, name):
    t = q.shape[0]
    n = t // CH
    pairs = _chunk_heads()

    def body(q_ref, k_ref, v_ref, bg_ref, inv_ref, du_ref, dw_ref, dqk_ref, dqd_ref, dkd_ref, deg_ref,
             dq_ref, dk_ref, dv_ref, dbg_ref):
        lane = _iota((CH, 128), 1)
        row = _iota((CH, 128), 0)
        first = jnp.where(row == 0, 1.0, 0.0)
        solves = [_solve_with(inv_ref[p]) for p in range(len(pairs))]

        def solve_all(mats, rhs):
            return [f(m, r) for f, m, r in zip(solves, mats, rhs)]

        _, vjp = jax.vjp(functools.partial(_chunk_local, solve_all),
                         [q_ref[r, hc] for r, hc, _, _ in pairs], [k_ref[r, hc] for r, hc, _, _ in pairs],
                         [v_ref[r, hc] for r, hc, _, _ in pairs], *_all_gates(bg_ref))
        cts = ([du_ref[r, hc].astype(F32) for r, hc, _, _ in pairs],
               [dw_ref[r, hc].astype(F32) for r, hc, _, _ in pairs],
               [dqk_ref[p] for p in range(len(pairs))],
               [dqd_ref[r, hc].astype(F32) for r, hc, _, _ in pairs],
               [dkd_ref[r, hc].astype(F32) for r, hc, _, _ in pairs],
               [jnp.broadcast_to(deg_ref[c, h:h + 1, :], (CH, 128)) * first for _, _, c, h in pairs])
        dq, dk, dv, dg128, dg64, dgl, db128, db64 = vjp(cts)
        acc = [jnp.zeros((CH, 128), F32) for _ in range(LOCAL_CHUNKS)]
        for p, (r, hc, c, h) in enumerate(pairs):
            dq_ref[r, hc] = dq[p]
            dk_ref[r, hc] = dk[p]
            dv_ref[r, hc] = dv[p]
            dg = jnp.sum(dg128[p], axis=1, keepdims=True) + jnp.sum(dg64[p], axis=1, keepdims=True)
            tot = jnp.sum(jnp.sum(dgl[p], axis=0, keepdims=True), axis=1, keepdims=True)
            dg = dg + jnp.where(row[:, 0:1] == CH - 1, tot, 0.0)
            db = jnp.sum(db128[p], axis=1, keepdims=True) + jnp.sum(db64[p], axis=1, keepdims=True)
            acc[c] = acc[c] + jnp.where(lane == h, db, 0.0) + jnp.where(lane == NH + h, dg, 0.0)
        for c in range(LOCAL_CHUNKS):
            dbg_ref[c * CH:(c + 1) * CH, :] = acc[c]

    row, small, qkb, egb = _chunk_specs()
    return pl.pallas_call(
        body, grid=(n // LOCAL_CHUNKS,), in_specs=[row, row, row, small, qkb, row, row, qkb, row, row, egb],
        out_specs=[row, row, row, small],
        out_shape=[SDS((t, D), F32)] * 3 + [SDS((t, 128), F32)],
        compiler_params=_params(("parallel",)), name=name,
    )(q, k, v, bg, inv, du, dw, dqk, dqd, dkd, deg)


def _state_step(s, u, w, qk, qd, kd, egl):
    ws = [_nn(a, b) for a, b in zip(w, s)]
    v_new = [a - b for a, b in zip(u, ws)]
    qs = [_nn(a, b) for a, b in zip(qd, s)]
    intra = [_nn(a, b) for a, b in zip(qk, v_new)]
    upd = [_tn(a, b) for a, b in zip(kd, v_new)]
    return [a * e + b for a, e, b in zip(s, egl, upd)], [a + b for a, b in zip(qs, intra)]


def _dn_scan_fwd(u, w, qk, qd, kd, eg, name):
    t = u.shape[0]
    n = t // CH
    g = SCAN_CHUNKS

    def body(u_ref, w_ref, qk_ref, qd_ref, kd_ref, eg_ref, o_ref, save_ref, s_ref):
        @pl.when(pl.program_id(0) == 0)
        def _():
            s_ref[...] = jnp.zeros_like(s_ref)

        cols = [slice(h * HD, (h + 1) * HD) for h in range(NH)]
        s = [s_ref[h] for h in range(NH)]
        for c in range(g):
            rows = slice(c * CH, (c + 1) * CH)
            for h in range(NH):
                save_ref[c, h] = s[h].astype(save_ref.dtype)
            s, o = _state_step(
                s, [u_ref[rows, hc] for hc in cols], [w_ref[rows, hc].astype(F32) for hc in cols],
                [qk_ref[c * NH + h].astype(F32) for h in range(NH)], [qd_ref[rows, hc].astype(F32) for hc in cols],
                [kd_ref[rows, hc].astype(F32) for hc in cols], [eg_ref[c, h:h + 1, :] for h in range(NH)])
            for h, hc in enumerate(cols):
                o_ref[rows, hc] = o[h]
        for h in range(NH):
            s_ref[h] = s[h]

    row = pl.BlockSpec((g * CH, D), lambda i: (i, 0))
    qkb = pl.BlockSpec((g * NH, CH, CH), lambda i: (i, 0, 0))
    egb = pl.BlockSpec((g, NH, 128), lambda i: (i, 0, 0))
    return pl.pallas_call(
        body, grid=(n // g,), in_specs=[row, row, qkb, row, row, egb],
        out_specs=[row, pl.BlockSpec((g, NH, HD, HD), lambda i: (i, 0, 0, 0))],
        out_shape=[SDS((t, D), F32), SDS((n, NH, HD, HD), BF16)],
        scratch_shapes=[pltpu.VMEM((NH, HD, HD), F32)],
        compiler_params=_params(("arbitrary",)), name=name,
    )(u, w, qk, qd, kd, eg)


def _dn_scan_bwd(u, w, qk, qd, kd, eg, saved, do, name):
    t = u.shape[0]
    n = t // CH
    g = SCAN_CHUNKS
    last = n // g - 1

    def body(u_ref, w_ref, qk_ref, qd_ref, kd_ref, eg_ref, sv_ref, do_ref,
             du_ref, dw_ref, dqk_ref, dqd_ref, dkd_ref, deg_ref, ds_ref):
        @pl.when(pl.program_id(0) == 0)
        def _():
            ds_ref[...] = jnp.zeros_like(ds_ref)

        cols = [slice(h * HD, (h + 1) * HD) for h in range(NH)]
        ds = [ds_ref[h] for h in range(NH)]
        for c in reversed(range(g)):
            rows = slice(c * CH, (c + 1) * CH)
            _, vjp = jax.vjp(
                _state_step, [sv_ref[c, h].astype(F32) for h in range(NH)], [u_ref[rows, hc] for hc in cols],
                [w_ref[rows, hc].astype(F32) for hc in cols], [qk_ref[c * NH + h].astype(F32) for h in range(NH)],
                [qd_ref[rows, hc].astype(F32) for hc in cols], [kd_ref[rows, hc].astype(F32) for hc in cols],
                [eg_ref[c, h:h + 1, :] for h in range(NH)])
            ds, du, dw, dqk, dqd, dkd, deg = vjp((ds, [do_ref[rows, hc] for hc in cols]))
            for h, hc in enumerate(cols):
                du_ref[rows, hc] = du[h].astype(du_ref.dtype)
                dw_ref[rows, hc] = dw[h].astype(dw_ref.dtype)
                dqk_ref[c * NH + h] = dqk[h]
                dqd_ref[rows, hc] = dqd[h].astype(dqd_ref.dtype)
                dkd_ref[rows, hc] = dkd[h].astype(dkd_ref.dtype)
                deg_ref[c, h:h + 1, :] = deg[h]
        for h in range(NH):
            ds_ref[h] = ds[h]

    row = pl.BlockSpec((g * CH, D), lambda i: (last - i, 0))
    qkb = pl.BlockSpec((g * NH, CH, CH), lambda i: (last - i, 0, 0))
    egb = pl.BlockSpec((g, NH, 128), lambda i: (last - i, 0, 0))
    return pl.pallas_call(
        body, grid=(n // g,),
        in_specs=[row, row, qkb, row, row, egb,
                  pl.BlockSpec((g, NH, HD, HD), lambda i: (last - i, 0, 0, 0)), row],
        out_specs=[row, row, qkb, row, row, egb],
        out_shape=[SDS((t, D), BF16), SDS((t, D), BF16), SDS((n * NH, CH, CH), F32), SDS((t, D), BF16),
                   SDS((t, D), BF16), SDS((n, NH, 128), F32)],
        scratch_shapes=[pltpu.VMEM((NH, HD, HD), F32)],
        compiler_params=_params(("arbitrary",)), name=name,
    )(u, w, qk, qd, kd, eg, saved, do)


def _ada_fwd(c_all, ada_w, ada_b, name):
    ncol = ada_w.shape[1]

    def body(c_ref, w_ref, b_ref, o_ref):
        o_ref[...] = _dg(_silu(c_ref[...]), w_ref[...], NN, HI) + b_ref[...]

    return pl.pallas_call(body, out_shape=SDS((NDEV, ncol), F32),
                          compiler_params=pltpu.CompilerParams(vmem_limit_bytes=VMEM_LIMIT), name=name,
                          )(c_all, ada_w, ada_b)


def _ada_bwd(c_all_t, dmod, name):
    ncol = dmod.shape[1]

    def body(c_ref, d_ref, o_ref):
        sc = _silu(c_ref[...])
        acc = sc[:, 0:1] * d_ref[0:1, :]
        for b in range(1, NDEV):
            acc = acc + sc[:, b:b + 1] * d_ref[b:b + 1, :]
        o_ref[...] = acc

    return pl.pallas_call(body, out_shape=SDS((D, ncol), F32),
                          compiler_params=pltpu.CompilerParams(vmem_limit_bytes=VMEM_LIMIT), name=name,
                          )(c_all_t, dmod)


def _sum_devices(parts, out_dtype, name):
    _, r, c = parts.shape
    tr = TR if r % TR == 0 else r

    def body(p_ref, o_ref):
        acc = p_ref[0].astype(F32)
        for i in range(1, NDEV):
            acc = acc + p_ref[i].astype(F32)
        o_ref[...] = acc.astype(o_ref.dtype)

    return pl.pallas_call(
        body, grid=(r // tr,), in_specs=[pl.BlockSpec((NDEV, tr, c), lambda i: (0, i, 0))],
        out_specs=pl.BlockSpec((tr, c), lambda i: (i, 0)), out_shape=SDS((r, c), out_dtype),
        compiler_params=_params(("parallel",)), name=name,
    )(parts)


def _adam_tiles(r, c):
    if r % 8 == 0:
        return _pick(r, (256, 352, 128, 8)), c
    return r, (256 if c % 256 == 0 else c)


def _adam_math(w, gv, m, v):
    m_new = ADAM_B1 * m + (1.0 - ADAM_B1) * gv
    v_new = ADAM_B2 * v + (1.0 - ADAM_B2) * (gv * gv)
    bc1 = 1.0 - ADAM_B1 ** ADAM_STEP
    bc2 = 1.0 - ADAM_B2 ** ADAM_STEP
    return -ADAM_LR * ((m_new / bc1) / (jnp.sqrt(v_new / bc2) + ADAM_EPS) + ADAM_WD * w), m_new, v_new


def _adamw(w, g, m, v, name):
    r, c = w.shape
    tr, tc = _adam_tiles(r, c)

    def body(w_ref, g_ref, m_ref, v_ref, d_ref, nm_ref, nv_ref):
        d_ref[...], nm_ref[...], nv_ref[...] = _adam_math(w_ref[...], g_ref[...], m_ref[...], v_ref[...])

    spec = pl.BlockSpec((tr, tc), lambda i, j: (i, j))
    return pl.pallas_call(
        body, grid=(r // tr, c // tc), in_specs=[spec] * 4, out_specs=[spec] * 3,
        out_shape=[SDS((r, c), F32)] * 3, compiler_params=_params(("parallel", "parallel")), name=name,
    )(w, g, m, v)


def _reduce_adamw(parts, w, m, v, name):
    r, c = w.shape
    tr, tc = _adam_tiles(r, c)

    def body(p_ref, w_ref, m_ref, v_ref, g_ref, d_ref, nm_ref, nv_ref):
        gv = p_ref[0].astype(F32)
        for i in range(1, NDEV):
            gv = gv + p_ref[i].astype(F32)
        g_ref[...] = gv
        d_ref[...], nm_ref[...], nv_ref[...] = _adam_math(w_ref[...], gv, m_ref[...], v_ref[...])

    spec = pl.BlockSpec((tr, tc), lambda i, j: (i, j))
    return pl.pallas_call(
        body, grid=(r // tr, c // tc),
        in_specs=[pl.BlockSpec((NDEV, tr, tc), lambda i, j: (0, i, j))] + [spec] * 3, out_specs=[spec] * 4,
        out_shape=[SDS((r, c), F32)] * 4, compiler_params=_params(("parallel", "parallel")), name=name,
    )(parts, w, m, v)


ANY = pl.BlockSpec(memory_space=pl.ANY)
MESH = pl.DeviceIdType.MESH


def _all_gather(xs, name, after=None):
    n = len(xs)
    extra = [] if after is None else [after]

    def body(*refs):
        x_refs, out_refs = refs[:n], refs[n + len(extra):2 * n + len(extra)]
        send_sems, recv_sems, local_sems = refs[-3:]
        mx, my, mc = lax.axis_index("x"), lax.axis_index("y"), lax.axis_index("c")
        me, sibling = (mx, my, mc), (mx, my, 1 - mc)
        chips = [(1 - mx, my), (mx, 1 - my), (1 - mx, 1 - my)]

        def rows(a, px, py, pc):
            return out_refs[a].at[4 * px + 2 * py + pc]

        def copy(a, k, block, to, src=None):
            return pltpu.make_async_remote_copy(
                src_ref=rows(a, *block) if src is None else src, dst_ref=rows(a, *block),
                send_sem=send_sems.at[a, k], recv_sem=recv_sems.at[a, k], device_id=to, device_id_type=MESH)

        mine = [pltpu.make_async_copy(x_refs[a], rows(a, *me), local_sems.at[a]) for a in range(n)]
        for cp in mine:
            cp.start()
        first = []
        for a in range(n):
            first.append(copy(a, 0, me, sibling, src=x_refs[a]))
            first += [copy(a, 1 + j, me, (*chip, mc), src=x_refs[a]) for j, chip in enumerate(chips)]
        for cp in first:
            cp.start()
        passed = []
        for a in range(n):
            for j, chip in enumerate(chips):
                copy(a, 1 + j, (*chip, mc), me).wait_recv()
                passed.append(copy(a, 4 + j, (*chip, mc), sibling))
                passed[-1].start()
        for a in range(n):
            copy(a, 0, sibling, me).wait_recv()
            for j, chip in enumerate(chips):
                copy(a, 4 + j, (*chip, 1 - mc), me).wait_recv()
        for cp in first + passed:
            cp.wait_send()
        for cp in mine:
            cp.wait()

    return pl.pallas_call(
        body, out_shape=[SDS((NDEV,) + x.shape, x.dtype) for x in xs], in_specs=[ANY] * (n + len(extra)),
        out_specs=[ANY] * n,
        scratch_shapes=[pltpu.SemaphoreType.DMA((n, 7)), pltpu.SemaphoreType.DMA((n, 7)),
                        pltpu.SemaphoreType.DMA((n,))],
        name=name,
    )(*xs, *extra)


HBM = pl.BlockSpec(memory_space=pltpu.HBM)
SEM = pl.BlockSpec(memory_space=pltpu.SEMAPHORE)
EFFECT = pltpu.SideEffectType.DATAFLOW_SIDE_EFFECTING


def _peers():
    mx, my, mc = lax.axis_index("x"), lax.axis_index("y"), lax.axis_index("c")
    out = []
    for k in range(1, NDEV):
        out.append((1 - mx if k & 4 else mx, 1 - my if k & 2 else my, 1 - mc if k & 1 else mc))
    return 4 * mx + 2 * my + mc, out


NEAR = (0, 1, 3, 5)


def _push_start(srcs, sliced, name, after=None, near=()):
    n = len(srcs)
    extra = [] if after is None else [after]
    lands = [lax.empty(s.shape if sliced else (NDEV,) + s.shape, s.dtype) for s in srcs]

    def body(*refs):
        src_refs, land_refs = refs[:n], refs[n:2 * n]
        outs = refs[2 * n + len(extra):]
        send_sems, recv_sems = outs[:n], outs[n:2 * n]
        token = refs[-1]
        me, peers = _peers()
        for a in range(n):
            for k, (px, py, pc) in enumerate(peers):
                if a in near and k not in NEAR:
                    continue
                src = src_refs[a].at[4 * px + 2 * py + pc] if sliced else src_refs[a]
                pltpu.make_async_remote_copy(
                    src_ref=src, dst_ref=land_refs[a].at[me], send_sem=send_sems[a].at[k],
                    recv_sem=recv_sems[a].at[k], device_id=(px, py, pc), device_id_type=MESH).start()
            pltpu.make_async_copy(src_refs[a].at[me] if sliced else src_refs[a], land_refs[a].at[me],
                                  send_sems[a].at[NDEV - 1]).start()
        token[...] = jnp.zeros_like(token)

    outs = pl.pallas_call(
        body, name=name,
        out_shape=([pltpu.SemaphoreType.DMA((NDEV,))] * n + [pltpu.SemaphoreType.DMA((NDEV - 1,))] * n
                   + [pltpu.HBM(s.shape, s.dtype) for s in srcs] + [pltpu.HBM(l.shape, l.dtype) for l in lands]
                   + [SDS((8, 128), F32)]),
        in_specs=[HBM] * (2 * n) + [pl.BlockSpec(memory_space=pl.ANY)] * len(extra),
        out_specs=[SEM] * (2 * n) + [HBM] * (2 * n) + [pl.BlockSpec(memory_space=pltpu.VMEM)],
        input_output_aliases={i: 2 * n + i for i in range(2 * n)},
        compiler_params=pltpu.CompilerParams(has_side_effects=EFFECT),
    )(*[pltpu.with_memory_space_constraint(s, pltpu.HBM) for s in srcs],
      *[pltpu.with_memory_space_constraint(l, pltpu.HBM) for l in lands], *extra)
    sends, recvs = outs[:n], outs[n:2 * n]
    src_thru, land_thru = outs[2 * n:3 * n], outs[3 * n:4 * n]
    return [(sends[a], recvs[a], src_thru[a], land_thru[a]) for a in range(n)], outs[-1]


def _push_wait(started, sliced, after, name, near=()):
    n = len(started)
    afters = list(after) if isinstance(after, (list, tuple)) else [after]

    def body(*refs):
        src_refs, land_refs = refs[:n], refs[n:2 * n]
        send_sems, recv_sems = refs[2 * n:3 * n], refs[3 * n:4 * n]
        me, peers = _peers()
        for a in range(n):
            for k, (px, py, pc) in enumerate(peers):
                if a in near and k not in NEAR:
                    continue
                src = src_refs[a].at[4 * px + 2 * py + pc] if sliced else src_refs[a]
                cp = pltpu.make_async_remote_copy(
                    src_ref=src, dst_ref=land_refs[a].at[me], send_sem=send_sems[a].at[k],
                    recv_sem=recv_sems[a].at[k], device_id=(px, py, pc), device_id_type=MESH)
                cp.wait_send()
                cp.wait_recv()
            pltpu.make_async_copy(src_refs[a].at[me] if sliced else src_refs[a], land_refs[a].at[me],
                                  send_sems[a].at[NDEV - 1]).wait()

    srcs = [s[2] for s in started]
    lands = [s[3] for s in started]
    outs = pl.pallas_call(
        body, name=name,
        out_shape=[pltpu.HBM(s.shape, s.dtype) for s in srcs] + [pltpu.HBM(l.shape, l.dtype) for l in lands],
        in_specs=[HBM] * (2 * n) + [SEM] * (2 * n) + [pl.BlockSpec(memory_space=pl.ANY)] * len(afters),
        out_specs=[HBM] * (2 * n),
        input_output_aliases={i: i for i in range(2 * n)},
        compiler_params=pltpu.CompilerParams(has_side_effects=EFFECT),
    )(*srcs, *lands, *[s[0] for s in started], *[s[1] for s in started], *afters)
    return outs[n:]


def _relay_to_sibling(land, name):
    def body(_, land_ref, send_sems, recv_sems):
        mx, my, mc = lax.axis_index("x"), lax.axis_index("y"), lax.axis_index("c")
        chips = [(1 - mx, my), (mx, 1 - my), (1 - mx, 1 - my)]

        def copy(j, core):
            slot = land_ref.at[4 * chips[j][0] + 2 * chips[j][1] + core]
            return pltpu.make_async_remote_copy(
                src_ref=slot, dst_ref=slot, send_sem=send_sems.at[j], recv_sem=recv_sems.at[j],
                device_id=(mx, my, 1 - mc), device_id_type=MESH)

        mine = [copy(j, mc) for j in range(3)]
        for cp in mine:
            cp.start()
        for j in range(3):
            copy(j, 1 - mc).wait_recv()
        for cp in mine:
            cp.wait_send()

    return pl.pallas_call(
        body, out_shape=SDS(land.shape, land.dtype), in_specs=[ANY], out_specs=ANY, input_output_aliases={0: 0},
        scratch_shapes=[pltpu.SemaphoreType.DMA((3,)), pltpu.SemaphoreType.DMA((3,))], name=name,
    )(land)


def _cols_from_blocks(blocks):
    _, rows, w = blocks.shape
    return blocks.transpose(1, 0, 2).reshape(rows, NDEV * w)


def _cols_to_blocks(full):
    rows, total = full.shape
    return full.reshape(rows, NDEV, total // NDEV).transpose(1, 0, 2)


def _mix_pad(wt):
    xp, q, k, v, z, ba, gp, gd = jnp.split(wt, (512, 1536, 2560, 3584, 4608, 4624, 5648), axis=0)
    pad = jnp.zeros((MIXP - OFF_BA - 16, wt.shape[1]), wt.dtype)
    return jnp.concatenate([q, k, v, z, gp, gd, xp, ba, pad], axis=0)


def _mix_unpad(wt):
    q, k, v, z, gp, gd, xp, ba = (wt[OFF_Q:OFF_K], wt[OFF_K:OFF_V], wt[OFF_V:OFF_Z], wt[OFF_Z:OFF_GP],
                                  wt[OFF_GP:OFF_GD], wt[OFF_GD:OFF_XP], wt[OFF_XP:OFF_BA], wt[OFF_BA:OFF_BA + 16])
    return jnp.concatenate([xp, q, k, v, z, ba, gp, gd], axis=0)


def _lane_row(vec8):
    return jnp.zeros((1, 128), F32).at[0, NH:2 * NH].set(vec8)


def _ffn_fwd(x, h, gate, w_in, w_out, tag, next_norm=None, token=None, start_more=None, final=None):
    if isinstance(w_in, tuple):
        w_in, = _push_wait([w_in], False, h, f"{tag}_gather_wait_in")
    w_in = w_in.reshape(2 * FH, D)
    u, a = _swiglu_up(h, w_in, f"{tag}_up", after=token)
    w_out, = _push_wait([w_out], False, a, f"{tag}_gather_wait_out")
    w_out = w_out.reshape(FH, D)
    if final is not None:
        return _matmul_residual_loss(a, w_out, x, gate, 0.5, *final, f"{tag}_down_loss"), (h, u, a, None), w_in, w_out
    outs = _matmul_residual(a, w_out, x, gate, 0.5, a_blk=True, norm=next_norm, name=f"{tag}_down",
                            after=None if start_more is None else start_more(h))
    return outs[0], (h, u, a, outs[1]), w_in, w_out, (outs[2] if next_norm else None)


def _ffn_bwd(dx_out, dy, x, g, scale, w_in, w_out, saved, tag, below=None):
    h, u, a, _ = saved
    t = x.shape[0]
    dw_out = _matmul(a, dy, ta=True, a_blk=True, out_dtype=BF16, name=f"{tag}_down_dw")
    sent_out, token = _push_start([dw_out.reshape(NDEV, FH // NDEV, D)], True, f"{tag}_grad_start_out")
    du = _swiglu_down_bwd(dy, w_out, u, f"{tag}_down_dx", after=token).reshape(NDEV, t, FB)
    dw_in = _matmul(du, h, ta=True, a_blk=True, out_dtype=BF16, name=f"{tag}_up_dw")
    sent_in, token = _push_start([dw_in.reshape(NDEV, FB, D)], True, f"{tag}_grad_start_in")
    dh = _matmul(du, w_in, a_blk=True, out_dtype=F32, name=f"{tag}_up_dx", after=token)
    return _norm_mod_bwd(x, g, scale, dh, dx_out, f"{tag}_norm_bwd", below), sent_in + sent_out


def kernel(x, c, ada_w, ada_b, norm_g, ffn1_w_in, ffn1_w_out, ffn2_w_in, ffn2_w_out, mix_w_in, conv_w, a_log, dt_bias, dn_norm_g, pool_w, pool_scale, pool_proj, dn_proj, mix_w_out, final_g, loss_target, m_ada_w, m_ada_b, m_norm_g, m_ffn1_w_in, m_ffn1_w_out, m_ffn2_w_in, m_ffn2_w_out, m_mix_w_in, m_conv_w, m_a_log, m_dt_bias, m_dn_norm_g, m_pool_w, m_pool_scale, m_pool_proj, m_dn_proj, m_mix_w_out, m_final_g, v_ada_w, v_ada_b, v_norm_g, v_ffn1_w_in, v_ffn1_w_out, v_ffn2_w_in, v_ffn2_w_out, v_mix_w_in, v_conv_w, v_a_log, v_dt_bias, v_dn_norm_g, v_pool_w, v_pool_scale, v_pool_proj, v_dn_proj, v_mix_w_out, v_final_g):
    me = 4 * lax.axis_index("x") + 2 * lax.axis_index("y") + lax.axis_index("c")
    x0 = x[0]
    target = loss_target[0]
    t = x0.shape[0]

    big = [ffn1_w_in[0], ffn1_w_out[0], ffn2_w_in[0], ffn2_w_out[0], mix_w_in[0], pool_proj[0], dn_proj[0],
           mix_w_out[0]]
    small = jnp.concatenate([c.reshape(8, 128), conv_w[0].reshape(12, 128), norm_g[0].reshape(3, 128),
                             jnp.zeros((1, 128), F32)], axis=0)
    small_all, = _all_gather([small], "gather_small")
    c_all = small_all[:, 0:8, :].reshape(NDEV, D)
    conv_full = small_all[:, 8:20, :].reshape(NDEV, 4, 384).transpose(1, 0, 2).reshape(4, 3 * D)
    norm_full = small_all[:, 20:23, :].reshape(NDEV, 3, 128).transpose(1, 0, 2).reshape(3, D)

    ncol = ada_w.shape[2]
    ada_b_mine = lax.dynamic_slice(ada_b, (0, me * ncol), (1, ncol))
    mod_cols = _ada_fwd(c_all, ada_w[0], ada_b_mine, "ada_fwd")
    transposed = (0, 2, 4)
    payload = [(w.T if i in transposed else w).astype(BF16) for i, w in enumerate(big)]
    mod_all, w_in1 = _all_gather([mod_cols, payload[0]], "gather_mod_first_weight")
    started, token = _push_start([payload[1], payload[4]], False, "gather_start", after=mod_all, near=(1,))
    started = {1: started[0], 4: started[1]}

    def start_rest(h):
        more, token = _push_start([payload[i] for i in (5, 6, 7, 2, 3)], False, "gather_start_rest", after=h)
        started.update(zip((5, 6, 7, 2, 3), more))
        return token

    mod = lax.dynamic_index_in_dim(mod_all, me, axis=1, keepdims=False).reshape(9, D)
    shift = [mod[3 * s:3 * s + 1] for s in range(3)]
    scale = [mod[3 * s + 1:3 * s + 2] for s in range(3)]
    gate = [mod[3 * s + 2:3 * s + 3] for s in range(3)]
    ng = [norm_full[s:s + 1] for s in range(3)]
    fg = final_g.reshape(1, D)
    al_row = _lane_row(a_log[0])
    dt_row = _lane_row(dt_bias[0])
    gn = dn_norm_g
    pw = pool_w[0]
    ps = pool_scale

    h0 = _norm_mod_fwd(x0, ng[0], shift[0], scale[0], "ffn1_norm", after=token)
    x1, saved1, w_in1, w_out1, h1 = _ffn_fwd(x0, h0, gate[0], w_in1, started[1], "ffn1",
                                             (ng[1], shift[1], scale[1]), token, start_rest)

    seg, = _push_wait([started[4]], False, h1, "mix_gather_wait", near=(0,))
    w_mix = _mix_pad(_relay_to_sibling(seg, "mix_gather_relay").reshape(MIX_RAW, D))
    proj = _matmul(h1, w_mix, tb=True, out_dtype=F32, name="mix_in")
    qh, kh, vh, bg = _dn_pre_fwd(proj, conv_full, al_row, dt_row, "dn_pre")
    seg = _push_wait([started[i] for i in (5, 6, 7)], False, qh, "mix_gather_wait_rest")
    w_pp = _cols_from_blocks(seg[0])
    w_dn = seg[1].reshape(D, D)
    w_mo = seg[2].reshape(D, D)
    ya = _pool_fwd(proj, pw, ps, w_pp, "pool_fwd")
    u, w, qk, qd, kd, eg, inv = _dn_local_fwd(qh, kh, vh, bg, "dn_local")
    o, s_saved = _dn_scan_fwd(u, w, qk, qd, kd, eg, "dn_scan")
    ob, yb = _dn_post_fwd(o, proj, gn, w_dn, "dn_post_out")
    merged, x2, mix_y, h2 = _merge_out(ya, yb, proj, w_mo, x1, gate[1], (ng[2], shift[2], scale[2]), "mix_out")

    (loss_row, dx3, dfg, dy2, dgate2), saved2, w_in2, w_out2 = _ffn_fwd(
        x2, h2, gate[2], started[2], started[3], "ffn2", final=(fg, target))

    (dx2, dsh2, dsc2, dng2, dmy, dgate1), sent2 = _ffn_bwd(dx3, dy2, x2, ng[2], scale[2], w_in2, w_out2, saved2,
                                                           "ffn2", (mix_y, gate[1], 1.0))

    dmerged = _matmul(dmy, w_mo, tb=True, out_dtype=BF16, name="mix_out_dx")
    dw_mo = _matmul(merged, dmy, ta=True, out_dtype=BF16, name="mix_out_dw")
    dproj = lax.empty((t, MIXP), BF16)
    dya, dyb, dproj = _merge_bwd(dmerged, ya, yb, proj, dproj, "merge_bwd")
    dw_dn = _matmul(ob, dyb, ta=True, out_dtype=BF16, name="dn_out_dw")
    do, dproj, dgn = _dn_post_bwd(o, proj, gn, dyb, w_dn, dproj, "dn_post_bwd")
    du, dw, dqk, dqd, dkd, deg = _dn_scan_bwd(u, w, qk, qd, kd, eg, s_saved, do, "dn_scan_bwd")
    dqh, dkh, dvh, dbg = _dn_local_bwd(qh, kh, vh, bg, inv, du, dw, dqk, dqd, dkd, deg, "dn_local_bwd")
    dconv, dproj, dal, ddt = _dn_pre_bwd_act(proj, conv_full, al_row, dt_row, dqh, dkh, dvh, dbg, dproj,
                                             "dn_pre_bwd_act")
    dproj, dcw = _dn_pre_bwd_conv(proj, conv_full, dconv, dproj, "dn_pre_bwd_conv")
    dwin, dpl, dpw, dps, dpp = _pool_bwd_local(proj, pw, ps, w_pp, dya, "pool_bwd_local")
    dproj = _pool_bwd_window(dwin, dpl, dproj, "pool_bwd_window")
    dw_mix = _matmul(dproj, h1, ta=True, out_dtype=BF16, name="mix_in_dw")
    sent1, token = _push_start(
        [_mix_unpad(dw_mix).reshape(NDEV, MIX_RAW // NDEV, D), _cols_to_blocks(dpp.astype(BF16)),
         dw_dn.reshape(NDEV, -1, D), dw_mo.reshape(NDEV, -1, D)], True, "mix_grad_start")
    dh1 = _matmul(dproj, w_mix, out_dtype=F32, name="mix_in_dx", after=token)
    dx1, dsh1, dsc1, dng1, dy0, dgate0 = _norm_mod_bwd(x1, ng[1], scale[1], dh1, dx2, "mix_norm_bwd",
                                                       (saved1[3], gate[0], 0.5))

    (dx0, dsh0, dsc0, dng0), sent0 = _ffn_bwd(dx1, dy0, x0, ng[0], scale[0], w_in1, w_out1, saved1, "ffn1")

    dmod = jnp.concatenate([dsh0, dsc0, dgate0, dsh1, dsc1, dgate1, dsh2, dsc2, dgate2], axis=1).reshape(-1)
    flat = jnp.concatenate([
        dmod, dal[0, NH:2 * NH], ddt[0, NH:2 * NH], dgn.reshape(-1), dps.reshape(-1), dfg.reshape(-1),
        dpw.reshape(-1), jnp.concatenate([dng0, dng1, dng2], axis=0).reshape(-1), dcw.reshape(-1),
        loss_row[0, 0:1]])
    nflat = 90 * D
    flat = jnp.concatenate([flat, jnp.zeros((nflat - flat.shape[0],), F32)]).reshape(90, D)
    sent_small, small_token = _push_start([flat], False, "small_grad_start")

    def small_grads(flat_all):
        tot = _sum_devices(flat_all, F32, "sum_small_grads").reshape(-1)
        dmod_all = flat_all.reshape(NDEV, nflat)[:, :9 * D]
        dmod_cols = lax.dynamic_slice(dmod_all, (0, me * ncol), (NDEV, ncol))
        g_ada_w = _ada_bwd(c_all.T, dmod_cols, "ada_bwd")
        p = 0
        pieces = {}
        for nm, size in (("ada_b", 9 * D), ("a_log", NH), ("dt_bias", NH), ("dn_norm_g", HD), ("pool_scale", PW),
                         ("final_g", D), ("pool_w", 4 * PG * PG), ("norm_g", 3 * D), ("conv_w", 12 * D),
                         ("loss", 1)):
            pieces[nm] = tot[p:p + size]
            p += size
        g_norm = lax.dynamic_slice(pieces["norm_g"].reshape(3, D), (0, me * 128), (3, 128))
        g_conv = lax.dynamic_slice(pieces["conv_w"].reshape(4, 3 * D), (0, me * 384), (4, 384))
        return pieces["loss"][0], {
            "ada_w": g_ada_w.reshape(ada_w.shape), "ada_b": pieces["ada_b"].reshape(ada_b.shape),
            "norm_g": g_norm.reshape(norm_g.shape), "conv_w": g_conv.reshape(conv_w.shape),
            "a_log": pieces["a_log"].reshape(a_log.shape), "dt_bias": pieces["dt_bias"].reshape(dt_bias.shape),
            "dn_norm_g": pieces["dn_norm_g"].reshape(dn_norm_g.shape),
            "pool_w": pieces["pool_w"].reshape(pool_w.shape),
            "pool_scale": pieces["pool_scale"].reshape(pool_scale.shape),
            "final_g": pieces["final_g"].reshape(final_g.shape),
        }

    grads = {}
    weights = {"ada_w": ada_w, "ada_b": ada_b, "norm_g": norm_g, "ffn1_w_in": ffn1_w_in, "ffn1_w_out": ffn1_w_out,
               "ffn2_w_in": ffn2_w_in, "ffn2_w_out": ffn2_w_out, "mix_w_in": mix_w_in, "conv_w": conv_w,
               "a_log": a_log, "dt_bias": dt_bias, "dn_norm_g": dn_norm_g, "pool_w": pool_w,
               "pool_scale": pool_scale, "pool_proj": pool_proj, "dn_proj": dn_proj, "mix_w_out": mix_w_out,
               "final_g": final_g}
    m_in = {"ada_w": m_ada_w, "ada_b": m_ada_b, "norm_g": m_norm_g, "ffn1_w_in": m_ffn1_w_in,
            "ffn1_w_out": m_ffn1_w_out, "ffn2_w_in": m_ffn2_w_in, "ffn2_w_out": m_ffn2_w_out,
            "mix_w_in": m_mix_w_in, "conv_w": m_conv_w, "a_log": m_a_log, "dt_bias": m_dt_bias,
            "dn_norm_g": m_dn_norm_g, "pool_w": m_pool_w, "pool_scale": m_pool_scale, "pool_proj": m_pool_proj,
            "dn_proj": m_dn_proj, "mix_w_out": m_mix_w_out, "final_g": m_final_g}
    v_in = {"ada_w": v_ada_w, "ada_b": v_ada_b, "norm_g": v_norm_g, "ffn1_w_in": v_ffn1_w_in,
            "ffn1_w_out": v_ffn1_w_out, "ffn2_w_in": v_ffn2_w_in, "ffn2_w_out": v_ffn2_w_out,
            "mix_w_in": v_mix_w_in, "conv_w": v_conv_w, "a_log": v_a_log, "dt_bias": v_dt_bias,
            "dn_norm_g": v_dn_norm_g, "pool_w": v_pool_w, "pool_scale": v_pool_scale, "pool_proj": v_pool_proj,
            "dn_proj": v_dn_proj, "mix_w_out": v_mix_w_out, "final_g": v_final_g}

    names = list(weights)
    large = ("ada_w", "ffn1_w_in", "ffn1_w_out", "ffn2_w_in", "ffn2_w_out", "mix_w_in", "pool_proj", "dn_proj",
             "mix_w_out")
    delta, new_m, new_v = {}, {}, {}

    flipped = ("ffn1_w_in", "ffn2_w_in", "mix_w_in")

    def views(nm):
        shp = weights[nm].shape
        two_d = (shp[-2], shp[-1])
        if nm in flipped:
            return (lambda a: a.reshape(two_d).T), (lambda a: a.T.reshape(shp))
        return (lambda a: a.reshape(two_d)), (lambda a: a.reshape(shp))

    def reduce_update(sent, group, after, tag):
        done = []
        for nm, r in zip(group, _push_wait(sent, True, after, f"{tag}_grad_wait")):
            view, back = views(nm)
            g_, d_, m_, v_ = _reduce_adamw(r, view(weights[nm]), view(m_in[nm]), view(v_in[nm]), f"adamw_{nm}")
            grads[nm], delta[nm], new_m[nm], new_v[nm] = back(g_), back(d_), back(m_), back(v_)
            done.append(d_)
        return done

    done = reduce_update(sent2, ("ffn2_w_in", "ffn2_w_out"), small_token, "ffn2")
    done += reduce_update(sent1, ("mix_w_in", "pool_proj", "dn_proj", "mix_w_out"), done, "mix")
    flat_all, = _push_wait(sent_small, False, done, "small_grad_wait")
    loss, small = small_grads(flat_all)
    grads.update(small)
    view, back = views("ada_w")
    done, m_, v_ = _adamw(view(ada_w), view(grads["ada_w"]), view(m_ada_w), view(v_ada_w), "adamw_ada_w")
    delta["ada_w"], new_m["ada_w"], new_v["ada_w"] = back(done), back(m_), back(v_)
    reduce_update(sent0, ("ffn1_w_in", "ffn1_w_out"), done, "ffn1")
    rest = [nm for nm in names if nm not in large]
    total = sum(weights[nm].size for nm in rest)
    padded = -(-total // D) * D

    def pack(tree, fill):
        flat_ = jnp.concatenate([tree[nm].reshape(-1) for nm in rest])
        return jnp.concatenate([flat_, jnp.full((padded - total,), fill, F32)]).reshape(-1, D)

    d_, m_, v_ = _adamw(pack(weights, 0.0), pack(grads, 0.0), pack(m_in, 0.0), pack(v_in, 1.0), "adamw_small")
    p = 0
    for nm in rest:
        size = weights[nm].size
        shp = weights[nm].shape
        delta[nm] = d_.reshape(-1)[p:p + size].reshape(shp)
        new_m[nm] = m_.reshape(-1)[p:p + size].reshape(shp)
        new_v[nm] = v_.reshape(-1)[p:p + size].reshape(shp)
        p += size

    grad_x = dx0.reshape(x.shape)
    return (loss, grad_x, *[grads[nm] for nm in names], *[delta[nm] for nm in names],
            *[new_m[nm] for nm in names], *[new_v[nm] for nm in names])
```

```python
import functools

import jax
import jax.numpy as jnp
from jax import lax
from jax.experimental import pallas as pl
from jax.experimental.pallas import tpu as pltpu

F32 = jnp.float32
BF16 = jnp.bfloat16
SDS = jax.ShapeDtypeStruct
HI = lax.Precision.HIGHEST

D = 1024
FH = 2816
FB = 704
NH = 8
HD = 128
CH = 64
SCAN_CHUNKS = 8
LOCAL_CHUNKS = 2
NDEV = 8
PW = 512
PG = 128
RMS_EPS = 1e-6
L2_EPS = 1e-6
TR = 512
HALO = 16
VMEM_LIMIT = 56 * 1024 * 1024
MATMUL_VMEM = 40 * 1024 * 1024

MIXP = 6912
OFF_Q, OFF_K, OFF_V, OFF_Z, OFF_GP, OFF_GD, OFF_XP, OFF_BA = 0, 1024, 2048, 3072, 4096, 5120, 6144, 6656
MIX_RAW = 6672

ADAM_LR = 0.001
ADAM_B1 = 0.9
ADAM_B2 = 0.999
ADAM_EPS = 1e-08
ADAM_WD = 0.01
ADAM_STEP = 10

NN = (((1,), (0,)), ((), ()))
NT = (((1,), (1,)), ((), ()))
TN = (((0,), (0,)), ((), ()))


def _dg(a, b, dims, prec=None):
    return lax.dot_general(a, b, dims, precision=prec, preferred_element_type=F32)


def _make_dots(prec):
    @jax.custom_vjp
    def nn(a, b):
        return _dg(a, b, NN, prec)

    @jax.custom_vjp
    def nt(a, b):
        return _dg(a, b, NT, prec)

    @jax.custom_vjp
    def tn(a, b):
        return _dg(a, b, TN, prec)

    nn.defvjp(lambda a, b: (nn(a, b), (a, b)), lambda r, d: (nt(d, r[1]), tn(r[0], d)))
    nt.defvjp(lambda a, b: (nt(a, b), (a, b)), lambda r, d: (nn(d, r[1]), tn(d, r[0])))
    tn.defvjp(lambda a, b: (tn(a, b), (a, b)), lambda r, d: (nt(r[1], d), nn(r[0], d)))
    return nn, nt, tn


_nn, _nt, _tn = _make_dots(None)


def _params(sem):
    return pltpu.CompilerParams(dimension_semantics=sem, vmem_limit_bytes=VMEM_LIMIT)


def _sigmoid(x):
    return 1.0 / (1.0 + jnp.exp(-x))


def _silu(x):
    return x * _sigmoid(x)


def _dsilu(x):
    s = _sigmoid(x)
    return s * (1.0 + x * (1.0 - s))


def _pick(n, cands):
    for c in cands:
        if n % c == 0:
            return c
    raise ValueError(f"no tile for {n}")


def _iota(shape, dim):
    return lax.broadcasted_iota(jnp.int32, shape, dim)


def _matmul(a, b, *, ta=False, tb=False, a_blk=False, b_blk=False, o_blk=False, tm=None, tn=None, tk=None,
            out_dtype, name, after=None):
    if a_blk:
        nb, r, cb = a.shape
        if ta:
            k_dim, m_dim, tm = r, nb * cb, cb
        else:
            m_dim, k_dim, tk = r, nb * cb, cb
    else:
        k_dim, m_dim = a.shape if ta else a.shape[::-1]
    if b_blk:
        nb, r, cb = b.shape
        if tb:
            n_dim, tk = r, cb
            assert nb * cb == k_dim
        else:
            n_dim, tn = nb * cb, cb
            assert r == k_dim
    else:
        n_dim = b.shape[0] if tb else b.shape[1]
    tn = tn or _pick(n_dim, (1024, 768, 512, 256, 128))
    out_bytes = jnp.dtype(out_dtype).itemsize

    def vmem(tm_, tk_):
        return 4 * tk_ * (tm_ + tn) + tm_ * tn * (4 + 2 * out_bytes)

    k_cands = [tk] if tk else [c for c in (k_dim, 4096, 3456, 2816, 2304, 2048, 1024, 512, 256)
                               if c <= k_dim and k_dim % c == 0]
    m_cands = [tm] if tm else [c for c in (2048, 1024, 768, 512, 256, 128) if m_dim % c == 0]
    base = next((c for c in m_cands if c <= 1024), m_cands[-1])
    tk = next((c for c in k_cands if vmem(base, c) <= MATMUL_VMEM), k_cands[-1])
    tm = next((c for c in m_cands if vmem(c, tk) <= MATMUL_VMEM), m_cands[-1])
    nk = k_dim // tk
    dims = ((((0,) if ta else (1,)), ((1,) if tb else (0,))), ((), ()))

    def body(a_ref, b_ref, *rest):
        o_ref, acc_ref = rest[-2:]
        k = pl.program_id(2)

        @pl.when(k == 0)
        def _():
            acc_ref[...] = jnp.zeros_like(acc_ref)

        acc_ref[...] += lax.dot_general(a_ref[...].astype(BF16), b_ref[...].astype(BF16), dims,
                                        preferred_element_type=F32)

        @pl.when(k == nk - 1)
        def _():
            o_ref[...] = acc_ref[...].astype(o_ref.dtype)

    if a_blk:
        a_spec = (pl.BlockSpec((None, tk, tm), lambda i, j, k: (i, k, 0)) if ta
                  else pl.BlockSpec((None, tm, tk), lambda i, j, k: (k, i, 0)))
    else:
        a_spec = (pl.BlockSpec((tk, tm), lambda i, j, k: (k, i)) if ta
                  else pl.BlockSpec((tm, tk), lambda i, j, k: (i, k)))
    if b_blk:
        b_spec = (pl.BlockSpec((None, tn, tk), lambda i, j, k: (k, j, 0)) if tb
                  else pl.BlockSpec((None, tk, tn), lambda i, j, k: (j, k, 0)))
    else:
        b_spec = (pl.BlockSpec((tn, tk), lambda i, j, k: (j, k)) if tb
                  else pl.BlockSpec((tk, tn), lambda i, j, k: (k, j)))
    if o_blk:
        o_spec = pl.BlockSpec((None, tm, tn), lambda i, j, k: (j, i, 0))
        o_shape = SDS((n_dim // tn, m_dim, tn), out_dtype)
    else:
        o_spec = pl.BlockSpec((tm, tn), lambda i, j, k: (i, j))
        o_shape = SDS((m_dim, n_dim), out_dtype)
    return pl.pallas_call(
        body, grid=(m_dim // tm, n_dim // tn, nk),
        in_specs=[a_spec, b_spec] + ([] if after is None else [pl.BlockSpec(memory_space=pl.ANY)]),
        out_specs=o_spec,
        out_shape=o_shape,
        scratch_shapes=[pltpu.VMEM((tm, tn), F32)],
        compiler_params=_params(("parallel", "parallel", "arbitrary")),
        name=name,
    )(a, b, *([] if after is None else [after]))


def _matmul_residual(a, b, x, gate, coef, *, a_blk=False, norm=None, name, after=None):
    m_dim = a.shape[-2]
    tm = _pick(m_dim, (1024, 512))
    if a_blk:
        nk, _, tk = a.shape
        a_spec = pl.BlockSpec((None, tm, tk), lambda i, k: (k, i, 0))
    else:
        tk = a.shape[1]
        nk = 1
        a_spec = pl.BlockSpec((tm, tk), lambda i, k: (i, 0))
    extra = [] if after is None else [after]
    vecs = [gate] + (list(norm) if norm else [])

    def body(a_ref, b_ref, x_ref, gate_ref, *rest):
        vec_refs = rest[:len(vecs) - 1]
        outs = rest[len(vecs) - 1 + len(extra):]
        acc_ref = outs[-1]
        k = pl.program_id(1)

        @pl.when(k == 0)
        def _():
            acc_ref[...] = jnp.zeros_like(acc_ref)

        acc_ref[...] += _dg(a_ref[...], b_ref[...], NN)

        @pl.when(k == nk - 1)
        def _():
            y = acc_ref[...]
            xn = x_ref[...] + (coef * gate_ref[...]) * y
            outs[0][...] = xn
            outs[1][...] = y.astype(outs[1].dtype)
            if norm:
                g_ref, sh_ref, sc_ref = vec_refs
                r = lax.rsqrt(jnp.mean(xn * xn, axis=-1, keepdims=True) + RMS_EPS)
                outs[2][...] = (((xn * r) * g_ref[...]) * (1.0 + sc_ref[...]) + sh_ref[...]).astype(outs[2].dtype)

    row = pl.BlockSpec((tm, D), lambda i, k: (i, 0))
    vec = pl.BlockSpec((1, D), lambda i, k: (0, 0))
    return pl.pallas_call(
        body, grid=(m_dim // tm, nk),
        in_specs=[a_spec, pl.BlockSpec((tk, D), lambda i, k: (k, 0)), row] + [vec] * len(vecs)
        + [pl.BlockSpec(memory_space=pl.ANY)] * len(extra),
        out_specs=[row] * (3 if norm else 2),
        out_shape=[SDS((m_dim, D), F32), SDS((m_dim, D), BF16)] + ([SDS((m_dim, D), BF16)] if norm else []),
        scratch_shapes=[pltpu.VMEM((tm, D), F32)],
        compiler_params=_params(("parallel", "arbitrary")), name=name,
    )(a, b, x, *vecs, *extra)


def _matmul_residual_loss(a, b, x, gate, coef, fg, target, name):
    nk, m_dim, tk = a.shape
    tm = _pick(m_dim, (1024, 512))
    nt = m_dim // tm

    def body(a_ref, b_ref, x_ref, gate_ref, g_ref, t_ref, loss_ref, dx_ref, dg_ref, dy_ref, dgate_ref,
             acc_ref, sq_ref):
        i, k = pl.program_id(0), pl.program_id(1)

        @pl.when(k == 0)
        def _():
            acc_ref[...] = jnp.zeros_like(acc_ref)

        @pl.when(jnp.logical_and(i == 0, k == 0))
        def _():
            sq_ref[...] = jnp.zeros_like(sq_ref)
            dg_ref[...] = jnp.zeros_like(dg_ref)
            dgate_ref[...] = jnp.zeros_like(dgate_ref)

        acc_ref[...] += _dg(a_ref[...], b_ref[...], NN)

        @pl.when(k == nk - 1)
        def _():
            y = acc_ref[...]
            scaled_gate = coef * gate_ref[...]
            xn = x_ref[...] + scaled_gate * y
            gv = g_ref[...]
            r = lax.rsqrt(jnp.mean(xn * xn, axis=-1, keepdims=True) + RMS_EPS)
            n = xn * r
            err = n * gv - t_ref[...]
            sq_ref[...] += jnp.sum(err * err, axis=0, keepdims=True)
            dout = err * (1.0 / D)
            dg_ref[...] += jnp.sum(dout * n, axis=0, keepdims=True)
            dn = dout * gv
            dxv = r * (dn - n * jnp.mean(dn * n, axis=-1, keepdims=True))
            dx_ref[...] = dxv
            dy_ref[...] = (scaled_gate * dxv).astype(dy_ref.dtype)
            dgate_ref[...] += jnp.sum((coef * dxv) * y, axis=0, keepdims=True)

        @pl.when(jnp.logical_and(i == nt - 1, k == nk - 1))
        def _():
            tot = jnp.sum(sq_ref[...], axis=1, keepdims=True) * (0.5 / D)
            loss_ref[...] = jnp.broadcast_to(tot, loss_ref.shape)

    row = pl.BlockSpec((tm, D), lambda i, k: (i, 0))
    vec = pl.BlockSpec((1, D), lambda i, k: (0, 0))
    return pl.pallas_call(
        body, grid=(nt, nk),
        in_specs=[pl.BlockSpec((None, tm, tk), lambda i, k: (k, i, 0)), pl.BlockSpec((tk, D), lambda i, k: (k, 0)),
                  row, vec, vec, row],
        out_specs=[pl.BlockSpec((1, 128), lambda i, k: (0, 0)), row, vec, row, vec],
        out_shape=[SDS((1, 128), F32), SDS((m_dim, D), F32), SDS((1, D), F32), SDS((m_dim, D), BF16),
                   SDS((1, D), F32)],
        scratch_shapes=[pltpu.VMEM((tm, D), F32), pltpu.VMEM((1, D), F32)],
        compiler_params=_params(("arbitrary", "arbitrary")), name=name,
    )(a, b, x, gate, fg, target)


def _row(width, col=0):
    return pl.BlockSpec((TR, width), lambda i: (i, col))


def _vec(width):
    return pl.BlockSpec((1, width), lambda i: (0, 0))


def _norm_mod_fwd(x, g, shift, scale, name, after=None):
    t = x.shape[0]
    extra = [] if after is None else [after]

    def body(x_ref, g_ref, sh_ref, sc_ref, *rest):
        o_ref = rest[-1]
        xv = x_ref[...]
        r = lax.rsqrt(jnp.mean(xv * xv, axis=-1, keepdims=True) + RMS_EPS)
        o_ref[...] = (((xv * r) * g_ref[...]) * (1.0 + sc_ref[...]) + sh_ref[...]).astype(o_ref.dtype)

    return pl.pallas_call(
        body, grid=(t // TR,),
        in_specs=[_row(D), _vec(D), _vec(D), _vec(D)] + [pl.BlockSpec(memory_space=pl.ANY)] * len(extra),
        out_specs=_row(D),
        out_shape=SDS((t, D), BF16), compiler_params=_params(("parallel",)), name=name,
    )(x, g, shift, scale, *extra)


def _residual_branch_bwd(dxv, y_ref, gate_ref, coef, dy_ref, dgate_ref):
    dy_ref[...] = ((coef * gate_ref[...]) * dxv).astype(dy_ref.dtype)
    dgate_ref[...] += jnp.sum((coef * dxv) * y_ref[...], axis=0, keepdims=True)


def _norm_mod_bwd(x, g, scale, dh, dx_in, name, below=None):
    t = x.shape[0]
    lower = [] if below is None else list(below[:2])

    def body(x_ref, g_ref, sc_ref, dh_ref, dxi_ref, *rest):
        dx_ref, dsh_ref, dsc_ref, dg_ref = rest[len(lower):len(lower) + 4]

        @pl.when(pl.program_id(0) == 0)
        def _():
            for ref in rest[len(lower) + 1:]:
                if ref.shape[0] == 1:
                    ref[...] = jnp.zeros_like(ref)

        xv = x_ref[...]
        gv = g_ref[...]
        dh = dh_ref[...]
        r = lax.rsqrt(jnp.mean(xv * xv, axis=-1, keepdims=True) + RMS_EPS)
        n = xv * r
        dsh_ref[...] += jnp.sum(dh, axis=0, keepdims=True)
        dsc_ref[...] += jnp.sum(dh * (n * gv), axis=0, keepdims=True)
        tt = dh * (1.0 + sc_ref[...])
        dg_ref[...] += jnp.sum(tt * n, axis=0, keepdims=True)
        dn = tt * gv
        dxv = dxi_ref[...] + r * (dn - n * jnp.mean(dn * n, axis=-1, keepdims=True))
        dx_ref[...] = dxv
        if below is not None:
            _residual_branch_bwd(dxv, rest[0], rest[1], below[2], rest[-2], rest[-1])

    more_in = [] if below is None else [_row(D), _vec(D)]
    more_out = [] if below is None else [_row(D), _vec(D)]
    more_shape = [] if below is None else [SDS((t, D), BF16), SDS((1, D), F32)]
    return pl.pallas_call(
        body, grid=(t // TR,), in_specs=[_row(D), _vec(D), _vec(D), _row(D), _row(D)] + more_in,
        out_specs=[_row(D), _vec(D), _vec(D), _vec(D)] + more_out,
        out_shape=[SDS((t, D), F32), SDS((1, D), F32), SDS((1, D), F32), SDS((1, D), F32)] + more_shape,
        compiler_params=_params(("arbitrary",)), name=name,
    )(x, g, scale, dh, dx_in, *lower)


def _swiglu_up(h, w_in, name, after=None):
    t = h.shape[0]
    tm = _pick(t, (1024, 512, 256))
    half = NDEV // 2
    extra = [] if after is None else [after]

    def body(h_ref, wg_ref, wu_ref, *rest):
        u_ref, a_ref = rest[-2:]
        hv = h_ref[...]
        gate = _dg(hv, wg_ref[...], NT)
        up = _dg(hv, wu_ref[...], NT)
        u_ref[0] = gate.astype(u_ref.dtype)
        u_ref[1] = up.astype(u_ref.dtype)
        a_ref[...] = (_silu(gate) * up).astype(a_ref.dtype)

    return pl.pallas_call(
        body, grid=(t // tm, half),
        in_specs=[pl.BlockSpec((tm, D), lambda i, j: (i, 0)),
                  pl.BlockSpec((FB, D), lambda i, j: (j, 0)),
                  pl.BlockSpec((FB, D), lambda i, j: (j + half, 0))]
        + [pl.BlockSpec(memory_space=pl.ANY)] * len(extra),
        out_specs=[pl.BlockSpec((2, None, tm, FB), lambda i, j: (0, j, i, 0)),
                   pl.BlockSpec((None, tm, FB), lambda i, j: (j, i, 0))],
        out_shape=[SDS((2, half, t, FB), BF16), SDS((half, t, FB), BF16)],
        compiler_params=_params(("parallel", "parallel")), name=name,
    )(h, w_in, w_in, *extra)


def _swiglu_down_bwd(dy, w_out, u, name, after=None):
    t = dy.shape[0]
    tm = _pick(t, (1024, 512, 256))
    half = NDEV // 2
    extra = [] if after is None else [after]
    pair = pl.BlockSpec((2, None, tm, FB), lambda i, j: (0, j, i, 0))

    def body(dy_ref, w_ref, u_ref, *rest):
        o_ref = rest[-1]
        da = _dg(dy_ref[...], w_ref[...], NT)
        gate = u_ref[0].astype(F32)
        o_ref[0] = (da * u_ref[1].astype(F32) * _dsilu(gate)).astype(o_ref.dtype)
        o_ref[1] = (da * _silu(gate)).astype(o_ref.dtype)

    return pl.pallas_call(
        body, grid=(t // tm, half),
        in_specs=[pl.BlockSpec((tm, D), lambda i, j: (i, 0)), pl.BlockSpec((FB, D), lambda i, j: (j, 0)), pair]
        + [pl.BlockSpec(memory_space=pl.ANY)] * len(extra),
        out_specs=pair, out_shape=SDS((2, half, t, FB), BF16),
        compiler_params=_params(("parallel", "parallel")), name=name,
    )(dy, w_out, u, *extra)


def _halo_prev(width, col):
    per = TR // HALO
    return pl.BlockSpec((HALO, width), lambda i: (jnp.maximum(i * per - 1, 0), col))


def _halo_next(width, col, nt):
    per = TR // HALO
    return pl.BlockSpec((HALO, width), lambda i: (jnp.minimum((i + 1) * per, nt * per - 1), col))


def _pool_windows(ext, tile_index):
    rows = _iota((TR, PG), 0) + tile_index * TR + 1
    pooled, counts = [], []
    for gi in range(4):
        w = 2 << gi
        e = ext[:, gi * PG:(gi + 1) * PG]
        s = e
        step = 1
        while step < w:
            s = s + pltpu.roll(s, step, 0)
            step *= 2
        cnt = jnp.minimum(rows, w).astype(F32)
        pooled.append(s[HALO:] / cnt - e[HALO:])
        counts.append(cnt)
    return pooled, counts


def _pool_fwd(proj, pool_w, pool_scale, pool_proj, name):
    t = proj.shape[0]
    xcol = OFF_XP // PW

    def body(x_ref, h_ref, pw_ref, ps_ref, pp_ref, o_ref):
        i = pl.program_id(0)
        halo = jnp.where(i > 0, h_ref[...], 0.0)
        ext = jnp.concatenate([halo, x_ref[...]], axis=0)
        pooled, _ = _pool_windows(ext, i)
        mixed = [_dg(pooled[g].astype(BF16), pw_ref[g].astype(BF16), NN) for g in range(4)]
        ypre = jnp.concatenate(mixed, axis=1) * ps_ref[...]
        o_ref[...] = _dg(ypre.astype(BF16), pp_ref[...], NN)

    return pl.pallas_call(
        body, grid=(t // TR,),
        in_specs=[_row(PW, xcol), _halo_prev(PW, xcol),
                  pl.BlockSpec((4, PG, PG), lambda i: (0, 0, 0)), _vec(PW),
                  pl.BlockSpec((PW, D), lambda i: (0, 0))],
        out_specs=_row(D), out_shape=SDS((t, D), F32),
        compiler_params=_params(("parallel",)), name=name,
    )(proj, proj, pool_w, pool_scale, pool_proj)


def _pool_bwd_local(proj, pool_w, pool_scale, pool_proj, dya, name):
    t = proj.shape[0]
    xcol = OFF_XP // PW

    def body(x_ref, h_ref, pw_ref, ps_ref, pp_ref, dya_ref, dwin_ref, dpl_ref, dpw_ref, dps_ref, dpp_ref):
        i = pl.program_id(0)

        @pl.when(i == 0)
        def _():
            dpw_ref[...] = jnp.zeros_like(dpw_ref)
            dps_ref[...] = jnp.zeros_like(dps_ref)
            dpp_ref[...] = jnp.zeros_like(dpp_ref)

        halo = jnp.where(i > 0, h_ref[...], 0.0)
        ext = jnp.concatenate([halo, x_ref[...]], axis=0)
        pooled, counts = _pool_windows(ext, i)
        mixed = jnp.concatenate(
            [_dg(pooled[g].astype(BF16), pw_ref[g].astype(BF16), NN) for g in range(4)], axis=1)
        ps = ps_ref[...]
        ypre = mixed * ps
        dyab = dya_ref[...].astype(BF16)
        dypre = _dg(dyab, pp_ref[...], NT)
        dpp_ref[...] += _dg(ypre.astype(BF16), dyab, TN)
        dps_ref[...] += jnp.sum(dypre * mixed, axis=0, keepdims=True)
        dmixed = dypre * ps
        for g in range(4):
            dm = dmixed[:, g * PG:(g + 1) * PG].astype(BF16)
            dpw_ref[g] += _dg(pooled[g].astype(BF16), dm, TN)
            dpooled = _dg(dm, pw_ref[g].astype(BF16), NT)
            dwin_ref[:, g * PG:(g + 1) * PG] = dpooled / counts[g]
            dpl_ref[:, g * PG:(g + 1) * PG] = dpooled

    return pl.pallas_call(
        body, grid=(t // TR,),
        in_specs=[_row(PW, xcol), _halo_prev(PW, xcol),
                  pl.BlockSpec((4, PG, PG), lambda i: (0, 0, 0)), _vec(PW),
                  pl.BlockSpec((PW, D), lambda i: (0, 0)), _row(D)],
        out_specs=[_row(PW), _row(PW), pl.BlockSpec((4, PG, PG), lambda i: (0, 0, 0)), _vec(PW),
                   pl.BlockSpec((PW, D), lambda i: (0, 0))],
        out_shape=[SDS((t, PW), F32), SDS((t, PW), F32), SDS((4, PG, PG), F32), SDS((1, PW), F32),
                   SDS((PW, D), F32)],
        compiler_params=_params(("arbitrary",)), name=name,
    )(proj, proj, pool_w, pool_scale, pool_proj, dya)


def _pool_bwd_window(dwin, dpl, dproj, name):
    t = dwin.shape[0]
    nt = t // TR
    ext_rows = TR + HALO

    def body(dw_ref, h_ref, dp_ref, _, o_ref):
        i = pl.program_id(0)
        halo = jnp.where(i < nt - 1, h_ref[...], 0.0)
        ext = jnp.concatenate([dw_ref[...], halo], axis=0)
        for gi in range(4):
            w = 2 << gi
            s = ext[:, gi * PG:(gi + 1) * PG]
            step = 1
            while step < w:
                s = s + pltpu.roll(s, ext_rows - step, 0)
                step *= 2
            o_ref[:, gi * PG:(gi + 1) * PG] = (s[:TR] - dp_ref[:, gi * PG:(gi + 1) * PG]).astype(o_ref.dtype)

    return pl.pallas_call(
        body, grid=(nt,),
        in_specs=[_row(PW), _halo_next(PW, 0, nt), _row(PW), pl.BlockSpec(memory_space=pl.ANY)],
        out_specs=_into(PW, OFF_XP), out_shape=SDS(dproj.shape, dproj.dtype), input_output_aliases={3: 0},
        compiler_params=_params(("parallel",)), name=name,
    )(dwin, dwin, dpl, dproj)


def _conv_group(ext, cw_ref, cols):
    acc = cw_ref[3:4, cols] * ext
    for j in range(3):
        acc = acc + cw_ref[j:j + 1, cols] * pltpu.roll(ext, 3 - j, 0)
    return acc[HALO:]


def _gate_terms(raw, al, dt):
    beta = _sigmoid(raw)
    xg = raw + dt
    sp = jnp.maximum(xg, 0.0) + jnp.log(1.0 + jnp.exp(-jnp.abs(xg)))
    g = -jnp.exp(al) * sp
    return beta, g, _sigmoid(xg)


def _dn_pre_fwd(proj, conv_w, al_row, dt_row, name):
    t = proj.shape[0]

    def body(x_ref, h_ref, cw_ref, ba_ref, al_ref, dt_ref, q_ref, k_ref, v_ref, bg_ref):
        i = pl.program_id(0)
        keep = i > 0
        for grp in range(24):
            cols = slice(grp * HD, (grp + 1) * HD)
            ext = jnp.concatenate([jnp.where(keep, h_ref[:, cols], 0.0), x_ref[:, cols]], axis=0)
            s = _silu(_conv_group(ext, cw_ref, cols))
            seg, head = divmod(grp, NH)
            hc = slice(head * HD, (head + 1) * HD)
            if seg == 0:
                q_ref[:, hc] = s * lax.rsqrt(jnp.sum(s * s, axis=-1, keepdims=True) + L2_EPS) * (HD ** -0.5)
            elif seg == 1:
                k_ref[:, hc] = s * lax.rsqrt(jnp.sum(s * s, axis=-1, keepdims=True) + L2_EPS)
            else:
                v_ref[:, hc] = s
        lane = _iota((TR, 128), 1)
        rowc = _iota((TR, 128), 0) % CH
        beta, g, _ = _gate_terms(ba_ref[...], al_ref[...], dt_ref[...])
        step = 1
        while step < CH:
            g = g + jnp.where(rowc >= step, pltpu.roll(g, step, 0), 0.0)
            step *= 2
        bg_ref[...] = jnp.where(lane < NH, beta, jnp.where(lane < 2 * NH, g, 0.0))

    return pl.pallas_call(
        body, grid=(t // TR,),
        in_specs=[_row(3 * D, 0), _halo_prev(3 * D, 0), pl.BlockSpec((4, 3 * D), lambda i: (0, 0)),
                  _row(128, OFF_BA // 128), _vec(128), _vec(128)],
        out_specs=[_row(D), _row(D), _row(D), _row(128)],
        out_shape=[SDS((t, D), F32), SDS((t, D), F32), SDS((t, D), F32), SDS((t, 128), F32)],
        compiler_params=_params(("parallel",)), name=name,
    )(proj, proj, conv_w, proj, al_row, dt_row)


def _dn_pre_bwd_act(proj, conv_w, al_row, dt_row, dq, dk, dv, dbg, dproj, name):
    t = proj.shape[0]

    def body(x_ref, h_ref, cw_ref, ba_ref, al_ref, dt_ref, dq_ref, dk_ref, dv_ref, dbg_ref, _,
             dc_ref, draw_ref, dal_ref, ddt_ref):
        i = pl.program_id(0)

        @pl.when(i == 0)
        def _():
            dal_ref[...] = jnp.zeros_like(dal_ref)
            ddt_ref[...] = jnp.zeros_like(ddt_ref)

        keep = i > 0
        for grp in range(24):
            cols = slice(grp * HD, (grp + 1) * HD)
            ext = jnp.concatenate([jnp.where(keep, h_ref[:, cols], 0.0), x_ref[:, cols]], axis=0)
            cv = _conv_group(ext, cw_ref, cols)
            seg, head = divmod(grp, NH)
            hc = slice(head * HD, (head + 1) * HD)
            if seg == 2:
                ds = dv_ref[:, hc]
            else:
                s = _silu(cv)
                r = lax.rsqrt(jnp.sum(s * s, axis=-1, keepdims=True) + L2_EPS)
                dy = dq_ref[:, hc] if seg == 0 else dk_ref[:, hc]
                c = (HD ** -0.5) if seg == 0 else 1.0
                ds = (c * r) * (dy - s * ((r * r) * jnp.sum(dy * s, axis=-1, keepdims=True)))
            dc_ref[:, cols] = ds * _dsilu(cv)
        lane = _iota((TR, 128), 1)
        rowc = _iota((TR, 128), 0) % CH
        isb = lane < NH
        isg = jnp.logical_and(lane >= NH, lane < 2 * NH)
        beta, g, sg = _gate_terms(ba_ref[...], al_ref[...], dt_ref[...])
        dbgv = dbg_ref[...]
        dg = dbgv
        step = 1
        while step < CH:
            dg = dg + jnp.where(rowc < CH - step, pltpu.roll(dg, TR - step, 0), 0.0)
            step *= 2
        da_raw = dg * (-jnp.exp(al_ref[...])) * sg
        draw = jnp.where(isb, dbgv * beta * (1.0 - beta), jnp.where(isg, da_raw, 0.0))
        draw_ref[:, :128] = draw.astype(draw_ref.dtype)
        draw_ref[:, 128:] = jnp.zeros((TR, MIXP - OFF_BA - 128), draw_ref.dtype)
        dal_ref[...] += jnp.sum(jnp.where(isg, dg * g, 0.0), axis=0, keepdims=True)
        ddt_ref[...] += jnp.sum(jnp.where(isg, da_raw, 0.0), axis=0, keepdims=True)

    return pl.pallas_call(
        body, grid=(t // TR,),
        in_specs=[_row(3 * D, 0), _halo_prev(3 * D, 0), pl.BlockSpec((4, 3 * D), lambda i: (0, 0)),
                  _row(128, OFF_BA // 128), _vec(128), _vec(128), _row(D), _row(D), _row(D), _row(128),
                  pl.BlockSpec(memory_space=pl.ANY)],
        out_specs=[_row(3 * D), _into(MIXP - OFF_BA, OFF_BA), _vec(128), _vec(128)],
        out_shape=[SDS((t, 3 * D), F32), SDS(dproj.shape, dproj.dtype), SDS((1, 128), F32), SDS((1, 128), F32)],
        input_output_aliases={10: 1},
        compiler_params=_params(("arbitrary",)), name=name,
    )(proj, proj, conv_w, proj, al_row, dt_row, dq, dk, dv, dbg, dproj)


def _dn_pre_bwd_conv(proj, conv_w, dconv, dproj, name):
    t = proj.shape[0]
    nt = t // TR
    ext_rows = TR + HALO

    def body(x_ref, h_ref, cw_ref, dc_ref, dn_ref, _, dx_ref, dcw_ref):
        i = pl.program_id(0)

        @pl.when(i == 0)
        def _():
            dcw_ref[...] = jnp.zeros_like(dcw_ref)

        keep_prev = i > 0
        keep_next = i < nt - 1
        for grp in range(24):
            cols = slice(grp * HD, (grp + 1) * HD)
            dct = dc_ref[:, cols]
            dext = jnp.concatenate([dct, jnp.where(keep_next, dn_ref[:, cols], 0.0)], axis=0)
            acc = cw_ref[3:4, cols] * dext
            for j in range(3):
                acc = acc + cw_ref[j:j + 1, cols] * pltpu.roll(dext, ext_rows - (3 - j), 0)
            dx_ref[:, cols] = acc[:TR].astype(dx_ref.dtype)
            xext = jnp.concatenate([jnp.where(keep_prev, h_ref[:, cols], 0.0), x_ref[:, cols]], axis=0)
            for j in range(4):
                xs = xext if j == 3 else pltpu.roll(xext, 3 - j, 0)
                dcw_ref[j:j + 1, cols] += jnp.sum(xs[HALO:] * dct, axis=0, keepdims=True)

    return pl.pallas_call(
        body, grid=(nt,),
        in_specs=[_row(3 * D, 0), _halo_prev(3 * D, 0), pl.BlockSpec((4, 3 * D), lambda i: (0, 0)),
                  _row(3 * D), _halo_next(3 * D, 0, nt), pl.BlockSpec(memory_space=pl.ANY)],
        out_specs=[_into(3 * D, OFF_Q), pl.BlockSpec((4, 3 * D), lambda i: (0, 0))],
        out_shape=[SDS(dproj.shape, dproj.dtype), SDS((4, 3 * D), F32)],
        input_output_aliases={5: 0},
        compiler_params=_params(("arbitrary",)), name=name,
    )(proj, proj, conv_w, dconv, dconv, dproj)


def _dn_post_fwd(o, proj, gn, w_out, name):
    t = o.shape[0]

    def body(o_ref, z_ref, g_ref, w_ref, out_ref, y_ref):
        gv = g_ref[...]
        for h in range(NH):
            hc = slice(h * HD, (h + 1) * HD)
            ov = o_ref[:, hc]
            r = lax.rsqrt(jnp.mean(ov * ov, axis=-1, keepdims=True) + RMS_EPS)
            out_ref[:, hc] = (((ov * r) * gv) * _silu(z_ref[:, hc])).astype(out_ref.dtype)
        y_ref[...] = _dg(out_ref[...], w_ref[...], NN)

    return pl.pallas_call(
        body, grid=(t // TR,),
        in_specs=[_row(D), _row(D, OFF_Z // D), _vec(HD), pl.BlockSpec((D, D), lambda i: (0, 0))],
        out_specs=[_row(D), _row(D)], out_shape=[SDS((t, D), BF16), SDS((t, D), F32)],
        compiler_params=_params(("parallel",)), name=name,
    )(o, proj, gn, w_out)


def _dn_post_bwd(o, proj, gn, dyb, w_out, dproj, name):
    t = o.shape[0]

    def body(o_ref, z_ref, g_ref, d_ref, w_ref, _, do_ref, dz_ref, dg_ref):
        @pl.when(pl.program_id(0) == 0)
        def _():
            dg_ref[...] = jnp.zeros_like(dg_ref)

        gv = g_ref[...]
        dob = _dg(d_ref[...], w_ref[...], NT)
        acc = jnp.zeros((1, HD), F32)
        for h in range(NH):
            hc = slice(h * HD, (h + 1) * HD)
            ov = o_ref[:, hc]
            zv = z_ref[:, hc]
            dv = dob[:, hc]
            r = lax.rsqrt(jnp.mean(ov * ov, axis=-1, keepdims=True) + RMS_EPS)
            n = ov * r
            dz_ref[:, hc] = (dv * (n * gv) * _dsilu(zv)).astype(dz_ref.dtype)
            dng = dv * _silu(zv)
            acc = acc + jnp.sum(dng * n, axis=0, keepdims=True)
            dn = dng * gv
            do_ref[:, hc] = r * (dn - n * jnp.mean(dn * n, axis=-1, keepdims=True))
        dg_ref[...] += acc

    return pl.pallas_call(
        body, grid=(t // TR,),
        in_specs=[_row(D), _row(D, OFF_Z // D), _vec(HD), _row(D), pl.BlockSpec((D, D), lambda i: (0, 0)),
                  pl.BlockSpec(memory_space=pl.ANY)],
        out_specs=[_row(D), _into(D, OFF_Z), _vec(HD)],
        out_shape=[SDS((t, D), F32), SDS(dproj.shape, dproj.dtype), SDS((1, HD), F32)],
        input_output_aliases={5: 1},
        compiler_params=_params(("arbitrary",)), name=name,
    )(o, proj, gn, dyb, w_out, dproj)


def _merge_out(ya, yb, proj, w_out, x, gate, norm, name):
    t = ya.shape[0]

    def body(a_ref, b_ref, gp_ref, gd_ref, w_ref, x_ref, gate_ref, g_ref, sh_ref, sc_ref,
             m_ref, xn_ref, y_ref, h_ref):
        merged = (_sigmoid(gp_ref[...]) * a_ref[...] + _sigmoid(gd_ref[...]) * b_ref[...]).astype(m_ref.dtype)
        m_ref[...] = merged
        y = _dg(merged, w_ref[...], NN)
        xn = x_ref[...] + gate_ref[...] * y
        xn_ref[...] = xn
        y_ref[...] = y.astype(y_ref.dtype)
        r = lax.rsqrt(jnp.mean(xn * xn, axis=-1, keepdims=True) + RMS_EPS)
        h_ref[...] = (((xn * r) * g_ref[...]) * (1.0 + sc_ref[...]) + sh_ref[...]).astype(h_ref.dtype)

    return pl.pallas_call(
        body, grid=(t // TR,),
        in_specs=[_row(D), _row(D), _row(D, OFF_GP // D), _row(D, OFF_GD // D),
                  pl.BlockSpec((D, D), lambda i: (0, 0)), _row(D), _vec(D), _vec(D), _vec(D), _vec(D)],
        out_specs=[_row(D)] * 4,
        out_shape=[SDS((t, D), BF16), SDS((t, D), F32), SDS((t, D), BF16), SDS((t, D), BF16)],
        compiler_params=_params(("parallel",)), name=name,
    )(ya, yb, proj, proj, w_out, x, gate, *norm)


def _into(width, offset):
    assert offset % width == 0
    return pl.BlockSpec((TR, width), lambda i: (i, offset // width))


def _merge_bwd(dmy, w_out, ya, yb, proj, dproj, name):
    t = ya.shape[0]

    def body(d_ref, w_ref, a_ref, b_ref, gp_ref, gd_ref, _, da_ref, db_ref, dg_ref):
        dv = _dg(d_ref[...], w_ref[...], NT)
        sp = _sigmoid(gp_ref[...])
        sd = _sigmoid(gd_ref[...])
        da_ref[...] = (dv * sp).astype(da_ref.dtype)
        db_ref[...] = (dv * sd).astype(db_ref.dtype)
        dg_ref[:, :D] = (dv * a_ref[...] * sp * (1.0 - sp)).astype(dg_ref.dtype)
        dg_ref[:, D:] = (dv * b_ref[...] * sd * (1.0 - sd)).astype(dg_ref.dtype)

    return pl.pallas_call(
        body, grid=(t // TR,),
        in_specs=[_row(D), pl.BlockSpec((D, D), lambda i: (0, 0)), _row(D), _row(D), _row(D, OFF_GP // D),
                  _row(D, OFF_GD // D), pl.BlockSpec(memory_space=pl.ANY)],
        out_specs=[_row(D), _row(D), _into(2 * D, OFF_GP)],
        out_shape=[SDS((t, D), BF16), SDS((t, D), BF16), SDS(dproj.shape, dproj.dtype)],
        input_output_aliases={6: 2},
        compiler_params=_params(("parallel",)), name=name,
    )(dmy, w_out, ya, yb, proj, proj, dproj)


def _split2(x):
    hi = x.astype(BF16)
    return hi, (x - hi.astype(F32)).astype(BF16)


def _dot3(a, b, dims):
    ah, al = _split2(a)
    bh, bl = _split2(b)
    return _dg(ah, bh, dims) + (_dg(ah, bl, dims) + _dg(al, bh, dims))


def _neumann_inverses(mats):
    ri = _iota((CH, CH), 0)
    ci = _iota((CH, CH), 1)
    eye = jnp.where(ri == ci, 1.0, 0.0).astype(F32)
    xs = [-a for a in mats]
    ps = [eye + x for x in xs]
    for _ in range(5):
        xs = [_dot3(x, x, NN) for x in xs]
        ps = [p + _dot3(p, x, NN) for p, x in zip(ps, xs)]
    return ps


def _solve_with(inv):
    @jax.custom_vjp
    def solve(a, rhs):
        return _dot3(inv, rhs, NN)

    def fwd(a, rhs):
        sol = _dot3(inv, rhs, NN)
        return sol, sol

    def bwd(sol, d):
        drhs = _dot3(inv, d, TN)
        return -_dot3(drhs, sol, NT), drhs

    solve.defvjp(fwd, bwd)
    return solve


@jax.custom_vjp
def _rows_to_lanes(g64):
    ri = _iota((CH, CH), 0)
    ci = _iota((CH, CH), 1)
    diag = jnp.where(ri == ci, g64, 0.0)
    ones = jnp.ones((CH, CH), BF16)
    hi = diag.astype(BF16)
    rem = diag - hi.astype(F32)
    mid = rem.astype(BF16)
    lo = (rem - mid.astype(F32)).astype(BF16)
    return _dg(ones, hi, NN) + (_dg(ones, mid, NN) + _dg(ones, lo, NN))


def _rows_to_lanes_bwd(_, d):
    ri = _iota((CH, CH), 0)
    ci = _iota((CH, CH), 1)
    return (jnp.where(ri == ci, jnp.broadcast_to(jnp.sum(d, axis=0, keepdims=True), (CH, CH)), 0.0),)


_rows_to_lanes.defvjp(lambda g64: (_rows_to_lanes(g64), None), _rows_to_lanes_bwd)


def _chunk_local(solve_all, q, k, v, g128, g64, gl128, b128, b64):
    ri = _iota((CH, CH), 0)
    ci = _iota((CH, CH), 1)
    causal = ri >= ci
    strict = ri > ci
    gj = [_rows_to_lanes(g) for g in g64]
    decay = [jnp.where(causal, jnp.exp(jnp.where(causal, g - t, 0.0)), 0.0) for g, t in zip(g64, gj)]
    kk = [_nt(x, x) for x in k]
    a = [jnp.where(strict, b * m * dc, 0.0) for b, m, dc in zip(b64, kk, decay)]
    eg = [jnp.exp(g) for g in g128]
    rhs = [jnp.concatenate([b * x, (b * e) * y], axis=1) for b, x, e, y in zip(b128, v, eg, k)]
    sol = solve_all(a, rhs)
    qk = [jnp.where(causal, _nt(x, y) * dc, 0.0) for x, y, dc in zip(q, k, decay)]
    return ([s[:, :HD] for s in sol], [s[:, HD:] for s in sol], qk, [x * e for x, e in zip(q, eg)],
            [x * jnp.exp(gl - g) for x, gl, g in zip(k, gl128, g128)], [jnp.exp(gl) for gl in gl128])


def _all_head_gates(bgv):
    return tuple(list(z) for z in zip(*[_head_gates(bgv, h) for h in range(NH)]))


def _head_gates(bgv, h):
    lane = _iota((CH, 128), 1)
    row = _iota((CH, 128), 0)
    bcol = jnp.sum(jnp.where(lane == h, bgv, 0.0), axis=1, keepdims=True)
    gcol = jnp.sum(jnp.where(lane == NH + h, bgv, 0.0), axis=1, keepdims=True)
    g128 = jnp.broadcast_to(gcol, (CH, 128))
    gl128 = jnp.broadcast_to(jnp.sum(jnp.where(row == CH - 1, g128, 0.0), axis=0, keepdims=True), (CH, 128))
    return (g128, jnp.broadcast_to(gcol, (CH, CH)), gl128,
            jnp.broadcast_to(bcol, (CH, 128)), jnp.broadcast_to(bcol, (CH, CH)))


def _chunk_specs():
    g = LOCAL_CHUNKS
    row = pl.BlockSpec((g * CH, D), lambda i: (i, 0))
    small = pl.BlockSpec((g * CH, 128), lambda i: (i, 0))
    qk = pl.BlockSpec((g * NH, CH, CH), lambda i: (i, 0, 0))
    eg = pl.BlockSpec((g, NH, 128), lambda i: (i, 0, 0))
    return row, small, qk, eg


def _chunk_heads():
    return [(slice(c * CH, (c + 1) * CH), slice(h * HD, (h + 1) * HD), c, h)
            for c in range(LOCAL_CHUNKS) for h in range(NH)]


def _all_gates(bg_ref):
    per_chunk = [_all_head_gates(bg_ref[c * CH:(c + 1) * CH, :]) for c in range(LOCAL_CHUNKS)]
    return tuple(sum((list(pc[j]) for pc in per_chunk), []) for j in range(5))


def _dn_local_fwd(q, k, v, bg, name):
    t = q.shape[0]
    n = t // CH
    pairs = _chunk_heads()

    def body(q_ref, k_ref, v_ref, bg_ref, u_ref, w_ref, qk_ref, qd_ref, kd_ref, eg_ref, inv_ref):
        def solve_all(mats, rhs):
            invs = _neumann_inverses(mats)
            for p in range(len(pairs)):
                inv_ref[p] = invs[p]
            return [_dot3(m, r, NN) for m, r in zip(invs, rhs)]

        u, w, qk, qd, kd, egl = _chunk_local(
            solve_all, [q_ref[r, hc] for r, hc, _, _ in pairs], [k_ref[r, hc] for r, hc, _, _ in pairs],
            [v_ref[r, hc] for r, hc, _, _ in pairs], *_all_gates(bg_ref))
        for p, (r, hc, c, h) in enumerate(pairs):
            u_ref[r, hc] = u[p]
            w_ref[r, hc] = w[p].astype(w_ref.dtype)
            qd_ref[r, hc] = qd[p].astype(qd_ref.dtype)
            kd_ref[r, hc] = kd[p].astype(kd_ref.dtype)
            qk_ref[p] = qk[p].astype(qk_ref.dtype)
            eg_ref[c, h:h + 1, :] = egl[p][0:1, :]

    row, small, qkb, egb = _chunk_specs()
    return pl.pallas_call(
        body, grid=(n // LOCAL_CHUNKS,), in_specs=[row, row, row, small],
        out_specs=[row, row, qkb, row, row, egb, qkb],
        out_shape=[SDS((t, D), F32), SDS((t, D), BF16), SDS((n * NH, CH, CH), BF16), SDS((t, D), BF16),
                   SDS((t, D), BF16), SDS((n, NH, 128), F32), SDS((n * NH, CH, CH), F32)],
        compiler_params=_params(("parallel",)), name=name,
    )(q, k, v, bg)


def _dn_local_bwd(q, k, v, bg, inv, du, dw, dqk, dqd, dkd, deg, name):
    t = q.shape[0]
    n = t // CH
    pairs = _chunk_heads()

    def body(q_ref, k_ref, v_ref, bg_ref, inv_ref, du_ref, dw_ref, dqk_ref, dqd_ref, dkd_ref, deg_ref,
             dq_ref, dk_ref, dv_ref, dbg_ref):
        lane = _iota((CH, 128), 1)
        row = _iota((CH, 128), 0)
        first = jnp.where(row == 0, 1.0, 0.0)
        solves = [_solve_with(inv_ref[p]) for p in range(len(pairs))]

        def solve_all(mats, rhs):
            return [f(m, r) for f, m, r in zip(solves, mats, rhs)]

        _, vjp = jax.vjp(functools.partial(_chunk_local, solve_all),
                         [q_ref[r, hc] for r, hc, _, _ in pairs], [k_ref[r, hc] for r, hc, _, _ in pairs],
                         [v_ref[r, hc] for r, hc, _, _ in pairs], *_all_gates(bg_ref))
        cts = ([du_ref[r, hc].astype(F32) for r, hc, _, _ in pairs],
               [dw_ref[r, hc].astype(F32) for r, hc, _, _ in pairs],
               [dqk_ref[p] for p in range(len(pairs))],
               [dqd_ref[r, hc].astype(F32) for r, hc, _, _ in pairs],
               [dkd_ref[r, hc].astype(F32) for r, hc, _, _ in pairs],
               [jnp.broadcast_to(deg_ref[c, h:h + 1, :], (CH, 128)) * first for _, _, c, h in pairs])
        dq, dk, dv, dg128, dg64, dgl, db128, db64 = vjp(cts)
        acc = [jnp.zeros((CH, 128), F32) for _ in range(LOCAL_CHUNKS)]
        for p, (r, hc, c, h) in enumerate(pairs):
            dq_ref[r, hc] = dq[p]
            dk_ref[r, hc] = dk[p]
            dv_ref[r, hc] = dv[p]
            dg = jnp.sum(dg128[p], axis=1, keepdims=True) + jnp.sum(dg64[p], axis=1, keepdims=True)
            tot = jnp.sum(jnp.sum(dgl[p], axis=0, keepdims=True), axis=1, keepdims=True)
            dg = dg + jnp.where(row[:, 0:1] == CH - 1, tot, 0.0)
            db = jnp.sum(db128[p], axis=1, keepdims=True) + jnp.sum(db64[p], axis=1, keepdims=True)
            acc[c] = acc[c] + jnp.where(lane == h, db, 0.0) + jnp.where(lane == NH + h, dg, 0.0)
        for c in range(LOCAL_CHUNKS):
            dbg_ref[c * CH:(c + 1) * CH, :] = acc[c]

    row, small, qkb, egb = _chunk_specs()
    return pl.pallas_call(
        body, grid=(n // LOCAL_CHUNKS,), in_specs=[row, row, row, small, qkb, row, row, qkb, row, row, egb],
        out_specs=[row, row, row, small],
        out_shape=[SDS((t, D), F32)] * 3 + [SDS((t, 128), F32)],
        compiler_params=_params(("parallel",)), name=name,
    )(q, k, v, bg, inv, du, dw, dqk, dqd, dkd, deg)


def _state_step(s, u, w, qk, qd, kd, egl):
    ws = [_nn(a, b) for a, b in zip(w, s)]
    v_new = [a - b for a, b in zip(u, ws)]
    qs = [_nn(a, b) for a, b in zip(qd, s)]
    intra = [_nn(a, b) for a, b in zip(qk, v_new)]
    upd = [_tn(a, b) for a, b in zip(kd, v_new)]
    return [a * e + b for a, e, b in zip(s, egl, upd)], [a + b for a, b in zip(qs, intra)]


def _dn_scan_fwd(u, w, qk, qd, kd, eg, name):
    t = u.shape[0]
    n = t // CH
    g = SCAN_CHUNKS

    def body(u_ref, w_ref, qk_ref, qd_ref, kd_ref, eg_ref, o_ref, save_ref, s_ref):
        @pl.when(pl.program_id(0) == 0)
        def _():
            s_ref[...] = jnp.zeros_like(s_ref)

        cols = [slice(h * HD, (h + 1) * HD) for h in range(NH)]
        s = [s_ref[h] for h in range(NH)]
        for c in range(g):
            rows = slice(c * CH, (c + 1) * CH)
            for h in range(NH):
                save_ref[c, h] = s[h].astype(save_ref.dtype)
            s, o = _state_step(
                s, [u_ref[rows, hc] for hc in cols], [w_ref[rows, hc].astype(F32) for hc in cols],
                [qk_ref[c * NH + h].astype(F32) for h in range(NH)], [qd_ref[rows, hc].astype(F32) for hc in cols],
                [kd_ref[rows, hc].astype(F32) for hc in cols], [eg_ref[c, h:h + 1, :] for h in range(NH)])
            for h, hc in enumerate(cols):
                o_ref[rows, hc] = o[h]
        for h in range(NH):
            s_ref[h] = s[h]

    row = pl.BlockSpec((g * CH, D), lambda i: (i, 0))
    qkb = pl.BlockSpec((g * NH, CH, CH), lambda i: (i, 0, 0))
    egb = pl.BlockSpec((g, NH, 128), lambda i: (i, 0, 0))
    return pl.pallas_call(
        body, grid=(n // g,), in_specs=[row, row, qkb, row, row, egb],
        out_specs=[row, pl.BlockSpec((g, NH, HD, HD), lambda i: (i, 0, 0, 0))],
        out_shape=[SDS((t, D), F32), SDS((n, NH, HD, HD), BF16)],
        scratch_shapes=[pltpu.VMEM((NH, HD, HD), F32)],
        compiler_params=_params(("arbitrary",)), name=name,
    )(u, w, qk, qd, kd, eg)


def _dn_scan_bwd(u, w, qk, qd, kd, eg, saved, do, name):
    t = u.shape[0]
    n = t // CH
    g = SCAN_CHUNKS
    last = n // g - 1

    def body(u_ref, w_ref, qk_ref, qd_ref, kd_ref, eg_ref, sv_ref, do_ref,
             du_ref, dw_ref, dqk_ref, dqd_ref, dkd_ref, deg_ref, ds_ref):
        @pl.when(pl.program_id(0) == 0)
        def _():
            ds_ref[...] = jnp.zeros_like(ds_ref)

        cols = [slice(h * HD, (h + 1) * HD) for h in range(NH)]
        ds = [ds_ref[h] for h in range(NH)]
        for c in reversed(range(g)):
            rows = slice(c * CH, (c + 1) * CH)
            _, vjp = jax.vjp(
                _state_step, [sv_ref[c, h].astype(F32) for h in range(NH)], [u_ref[rows, hc] for hc in cols],
                [w_ref[rows, hc].astype(F32) for hc in cols], [qk_ref[c * NH + h].astype(F32) for h in range(NH)],
                [qd_ref[rows, hc].astype(F32) for hc in cols], [kd_ref[rows, hc].astype(F32) for hc in cols],
                [eg_ref[c, h:h + 1, :] for h in range(NH)])
            ds, du, dw, dqk, dqd, dkd, deg = vjp((ds, [do_ref[rows, hc] for hc in cols]))
            for h, hc in enumerate(cols):
                du_ref[rows, hc] = du[h].astype(du_ref.dtype)
                dw_ref[rows, hc] = dw[h].astype(dw_ref.dtype)
                dqk_ref[c * NH + h] = dqk[h]
                dqd_ref[rows, hc] = dqd[h].astype(dqd_ref.dtype)
                dkd_ref[rows, hc] = dkd[h].astype(dkd_ref.dtype)
                deg_ref[c, h:h + 1, :] = deg[h]
        for h in range(NH):
            ds_ref[h] = ds[h]

    row = pl.BlockSpec((g * CH, D), lambda i: (last - i, 0))
    qkb = pl.BlockSpec((g * NH, CH, CH), lambda i: (last - i, 0, 0))
    egb = pl.BlockSpec((g, NH, 128), lambda i: (last - i, 0, 0))
    return pl.pallas_call(
        body, grid=(n // g,),
        in_specs=[row, row, qkb, row, row, egb,
                  pl.BlockSpec((g, NH, HD, HD), lambda i: (last - i, 0, 0, 0)), row],
        out_specs=[row, row, qkb, row, row, egb],
        out_shape=[SDS((t, D), BF16), SDS((t, D), BF16), SDS((n * NH, CH, CH), F32), SDS((t, D), BF16),
                   SDS((t, D), BF16), SDS((n, NH, 128), F32)],
        scratch_shapes=[pltpu.VMEM((NH, HD, HD), F32)],
        compiler_params=_params(("arbitrary",)), name=name,
    )(u, w, qk, qd, kd, eg, saved, do)


def _ada_fwd(c_all, ada_w, ada_b, name):
    ncol = ada_w.shape[1]

    def body(c_ref, w_ref, b_ref, o_ref):
        o_ref[...] = _dg(_silu(c_ref[...]), w_ref[...], NN, HI) + b_ref[...]

    return pl.pallas_call(body, out_shape=SDS((NDEV, ncol), F32),
                          compiler_params=pltpu.CompilerParams(vmem_limit_bytes=VMEM_LIMIT), name=name,
                          )(c_all, ada_w, ada_b)


def _ada_bwd(c_all_t, dmod, name):
    ncol = dmod.shape[1]

    def body(c_ref, d_ref, o_ref):
        sc = _silu(c_ref[...])
        acc = sc[:, 0:1] * d_ref[0:1, :]
        for b in range(1, NDEV):
            acc = acc + sc[:, b:b + 1] * d_ref[b:b + 1, :]
        o_ref[...] = acc

    return pl.pallas_call(body, out_shape=SDS((D, ncol), F32),
                          compiler_params=pltpu.CompilerParams(vmem_limit_bytes=VMEM_LIMIT), name=name,
                          )(c_all_t, dmod)


def _sum_devices(parts, out_dtype, name):
    _, r, c = parts.shape
    tr = TR if r % TR == 0 else r

    def body(p_ref, o_ref):
        acc = p_ref[0].astype(F32)
        for i in range(1, NDEV):
            acc = acc + p_ref[i].astype(F32)
        o_ref[...] = acc.astype(o_ref.dtype)

    return pl.pallas_call(
        body, grid=(r // tr,), in_specs=[pl.BlockSpec((NDEV, tr, c), lambda i: (0, i, 0))],
        out_specs=pl.BlockSpec((tr, c), lambda i: (i, 0)), out_shape=SDS((r, c), out_dtype),
        compiler_params=_params(("parallel",)), name=name,
    )(parts)


def _adam_tiles(r, c):
    if r % 8 == 0:
        return _pick(r, (256, 352, 128, 8)), c
    return r, (256 if c % 256 == 0 else c)


def _adam_math(w, gv, m, v):
    m_new = ADAM_B1 * m + (1.0 - ADAM_B1) * gv
    v_new = ADAM_B2 * v + (1.0 - ADAM_B2) * (gv * gv)
    bc1 = 1.0 - ADAM_B1 ** ADAM_STEP
    bc2 = 1.0 - ADAM_B2 ** ADAM_STEP
    return -ADAM_LR * ((m_new / bc1) / (jnp.sqrt(v_new / bc2) + ADAM_EPS) + ADAM_WD * w), m_new, v_new


def _adamw(w, g, m, v, name):
    r, c = w.shape
    tr, tc = _adam_tiles(r, c)

    def body(w_ref, g_ref, m_ref, v_ref, d_ref, nm_ref, nv_ref):
        d_ref[...], nm_ref[...], nv_ref[...] = _adam_math(w_ref[...], g_ref[...], m_ref[...], v_ref[...])

    spec = pl.BlockSpec((tr, tc), lambda i, j: (i, j))
    return pl.pallas_call(
        body, grid=(r // tr, c // tc), in_specs=[spec] * 4, out_specs=[spec] * 3,
        out_shape=[SDS((r, c), F32)] * 3, compiler_params=_params(("parallel", "parallel")), name=name,
    )(w, g, m, v)


def _reduce_adamw(parts, w, m, v, name):
    r, c = w.shape
    tr, tc = _adam_tiles(r, c)

    def body(p_ref, w_ref, m_ref, v_ref, g_ref, d_ref, nm_ref, nv_ref):
        gv = p_ref[0].astype(F32)
        for i in range(1, NDEV):
            gv = gv + p_ref[i].astype(F32)
        g_ref[...] = gv
        d_ref[...], nm_ref[...], nv_ref[...] = _adam_math(w_ref[...], gv, m_ref[...], v_ref[...])

    spec = pl.BlockSpec((tr, tc), lambda i, j: (i, j))
    return pl.pallas_call(
        body, grid=(r // tr, c // tc),
        in_specs=[pl.BlockSpec((NDEV, tr, tc), lambda i, j: (0, i, j))] + [spec] * 3, out_specs=[spec] * 4,
        out_shape=[SDS((r, c), F32)] * 4, compiler_params=_params(("parallel", "parallel")), name=name,
    )(parts, w, m, v)


ANY = pl.BlockSpec(memory_space=pl.ANY)
MESH = pl.DeviceIdType.MESH


def _all_gather(xs, name, after=None):
    n = len(xs)
    extra = [] if after is None else [after]

    def body(*refs):
        x_refs, out_refs = refs[:n], refs[n + len(extra):2 * n + len(extra)]
        send_sems, recv_sems, local_sems = refs[-3:]
        mx, my, mc = lax.axis_index("x"), lax.axis_index("y"), lax.axis_index("c")
        me, sibling = (mx, my, mc), (mx, my, 1 - mc)
        chips = [(1 - mx, my), (mx, 1 - my), (1 - mx, 1 - my)]

        def rows(a, px, py, pc):
            return out_refs[a].at[4 * px + 2 * py + pc]

        def copy(a, k, block, to, src=None):
            return pltpu.make_async_remote_copy(
                src_ref=rows(a, *block) if src is None else src, dst_ref=rows(a, *block),
                send_sem=send_sems.at[a, k], recv_sem=recv_sems.at[a, k], device_id=to, device_id_type=MESH)

        mine = [pltpu.make_async_copy(x_refs[a], rows(a, *me), local_sems.at[a]) for a in range(n)]
        for cp in mine:
            cp.start()
        first = []
        for a in range(n):
            first.append(copy(a, 0, me, sibling, src=x_refs[a]))
            first += [copy(a, 1 + j, me, (*chip, mc), src=x_refs[a]) for j, chip in enumerate(chips)]
        for cp in first:
            cp.start()
        passed = []
        for a in range(n):
            for j, chip in enumerate(chips):
                copy(a, 1 + j, (*chip, mc), me).wait_recv()
                passed.append(copy(a, 4 + j, (*chip, mc), sibling))
                passed[-1].start()
        for a in range(n):
            copy(a, 0, sibling, me).wait_recv()
            for j, chip in enumerate(chips):
                copy(a, 4 + j, (*chip, 1 - mc), me).wait_recv()
        for cp in first + passed:
            cp.wait_send()
        for cp in mine:
            cp.wait()

    return pl.pallas_call(
        body, out_shape=[SDS((NDEV,) + x.shape, x.dtype) for x in xs], in_specs=[ANY] * (n + len(extra)),
        out_specs=[ANY] * n,
        scratch_shapes=[pltpu.SemaphoreType.DMA((n, 7)), pltpu.SemaphoreType.DMA((n, 7)),
                        pltpu.SemaphoreType.DMA((n,))],
        name=name,
    )(*xs, *extra)


HBM = pl.BlockSpec(memory_space=pltpu.HBM)
SEM = pl.BlockSpec(memory_space=pltpu.SEMAPHORE)
EFFECT = pltpu.SideEffectType.DATAFLOW_SIDE_EFFECTING


def _peers():
    mx, my, mc = lax.axis_index("x"), lax.axis_index("y"), lax.axis_index("c")
    out = []
    for k in range(1, NDEV):
        out.append((1 - mx if k & 4 else mx, 1 - my if k & 2 else my, 1 - mc if k & 1 else mc))
    return 4 * mx + 2 * my + mc, out


NEAR = (0, 1, 3, 5)


def _push_start(srcs, sliced, name, after=None, near=()):
    n = len(srcs)
    extra = [] if after is None else [after]
    lands = [lax.empty(s.shape if sliced else (NDEV,) + s.shape, s.dtype) for s in srcs]

    def body(*refs):
        src_refs, land_refs = refs[:n], refs[n:2 * n]
        outs = refs[2 * n + len(extra):]
        send_sems, recv_sems = outs[:n], outs[n:2 * n]
        token = refs[-1]
        me, peers = _peers()
        for a in range(n):
            for k, (px, py, pc) in enumerate(peers):
                if a in near and k not in NEAR:
                    continue
                src = src_refs[a].at[4 * px + 2 * py + pc] if sliced else src_refs[a]
                pltpu.make_async_remote_copy(
                    src_ref=src, dst_ref=land_refs[a].at[me], send_sem=send_sems[a].at[k],
                    recv_sem=recv_sems[a].at[k], device_id=(px, py, pc), device_id_type=MESH).start()
            pltpu.make_async_copy(src_refs[a].at[me] if sliced else src_refs[a], land_refs[a].at[me],
                                  send_sems[a].at[NDEV - 1]).start()
        token[...] = jnp.zeros_like(token)

    outs = pl.pallas_call(
        body, name=name,
        out_shape=([pltpu.SemaphoreType.DMA((NDEV,))] * n + [pltpu.SemaphoreType.DMA((NDEV - 1,))] * n
                   + [pltpu.HBM(s.shape, s.dtype) for s in srcs] + [pltpu.HBM(l.shape, l.dtype) for l in lands]
                   + [SDS((8, 128), F32)]),
        in_specs=[HBM] * (2 * n) + [pl.BlockSpec(memory_space=pl.ANY)] * len(extra),
        out_specs=[SEM] * (2 * n) + [HBM] * (2 * n) + [pl.BlockSpec(memory_space=pltpu.VMEM)],
        input_output_aliases={i: 2 * n + i for i in range(2 * n)},
        compiler_params=pltpu.CompilerParams(has_side_effects=EFFECT),
    )(*[pltpu.with_memory_space_constraint(s, pltpu.HBM) for s in srcs],
      *[pltpu.with_memory_space_constraint(l, pltpu.HBM) for l in lands], *extra)
    sends, recvs = outs[:n], outs[n:2 * n]
    src_thru, land_thru = outs[2 * n:3 * n], outs[3 * n:4 * n]
    return [(sends[a], recvs[a], src_thru[a], land_thru[a]) for a in range(n)], outs[-1]


def _push_wait(started, sliced, after, name, near=()):
    n = len(started)
    afters = list(after) if isinstance(after, (list, tuple)) else [after]

    def body(*refs):
        src_refs, land_refs = refs[:n], refs[n:2 * n]
        send_sems, recv_sems = refs[2 * n:3 * n], refs[3 * n:4 * n]
        me, peers = _peers()
        for a in range(n):
            for k, (px, py, pc) in enumerate(peers):
                if a in near and k not in NEAR:
                    continue
                src = src_refs[a].at[4 * px + 2 * py + pc] if sliced else src_refs[a]
                cp = pltpu.make_async_remote_copy(
                    src_ref=src, dst_ref=land_refs[a].at[me], send_sem=send_sems[a].at[k],
                    recv_sem=recv_sems[a].at[k], device_id=(px, py, pc), device_id_type=MESH)
                cp.wait_send()
                cp.wait_recv()
            pltpu.make_async_copy(src_refs[a].at[me] if sliced else src_refs[a], land_refs[a].at[me],
                                  send_sems[a].at[NDEV - 1]).wait()

    srcs = [s[2] for s in started]
    lands = [s[3] for s in started]
    outs = pl.pallas_call(
        body, name=name,
        out_shape=[pltpu.HBM(s.shape, s.dtype) for s in srcs] + [pltpu.HBM(l.shape, l.dtype) for l in lands],
        in_specs=[HBM] * (2 * n) + [SEM] * (2 * n) + [pl.BlockSpec(memory_space=pl.ANY)] * len(afters),
        out_specs=[HBM] * (2 * n),
        input_output_aliases={i: i for i in range(2 * n)},
        compiler_params=pltpu.CompilerParams(has_side_effects=EFFECT),
    )(*srcs, *lands, *[s[0] for s in started], *[s[1] for s in started], *afters)
    return outs[n:]


def _relay_to_sibling(land, name):
    def body(_, land_ref, send_sems, recv_sems):
        mx, my, mc = lax.axis_index("x"), lax.axis_index("y"), lax.axis_index("c")
        chips = [(1 - mx, my), (mx, 1 - my), (1 - mx, 1 - my)]

        def copy(j, core):
            slot = land_ref.at[4 * chips[j][0] + 2 * chips[j][1] + core]
            return pltpu.make_async_remote_copy(
                src_ref=slot, dst_ref=slot, send_sem=send_sems.at[j], recv_sem=recv_sems.at[j],
                device_id=(mx, my, 1 - mc), device_id_type=MESH)

        mine = [copy(j, mc) for j in range(3)]
        for cp in mine:
            cp.start()
        for j in range(3):
            copy(j, 1 - mc).wait_recv()
        for cp in mine:
            cp.wait_send()

    return pl.pallas_call(
        body, out_shape=SDS(land.shape, land.dtype), in_specs=[ANY], out_specs=ANY, input_output_aliases={0: 0},
        scratch_shapes=[pltpu.SemaphoreType.DMA((3,)), pltpu.SemaphoreType.DMA((3,))], name=name,
    )(land)


def _cols_from_blocks(blocks):
    _, rows, w = blocks.shape
    return blocks.transpose(1, 0, 2).reshape(rows, NDEV * w)


def _cols_to_blocks(full):
    rows, total = full.shape
    return full.reshape(rows, NDEV, total // NDEV).transpose(1, 0, 2)


def _mix_pad(wt):
    xp, q, k, v, z, ba, gp, gd = jnp.split(wt, (512, 1536, 2560, 3584, 4608, 4624, 5648), axis=0)
    pad = jnp.zeros((MIXP - OFF_BA - 16, wt.shape[1]), wt.dtype)
    return jnp.concatenate([q, k, v, z, gp, gd, xp, ba, pad], axis=0)


def _mix_unpad(wt):
    q, k, v, z, gp, gd, xp, ba = (wt[OFF_Q:OFF_K], wt[OFF_K:OFF_V], wt[OFF_V:OFF_Z], wt[OFF_Z:OFF_GP],
                                  wt[OFF_GP:OFF_GD], wt[OFF_GD:OFF_XP], wt[OFF_XP:OFF_BA], wt[OFF_BA:OFF_BA + 16])
    return jnp.concatenate([xp, q, k, v, z, ba, gp, gd], axis=0)


def _lane_row(vec8):
    return jnp.zeros((1, 128), F32).at[0, NH:2 * NH].set(vec8)


def _ffn_fwd(x, h, gate, w_in, w_out, tag, next_norm=None, token=None, start_more=None, final=None):
    if isinstance(w_in, tuple):
        w_in, = _push_wait([w_in], False, h, f"{tag}_gather_wait_in")
    w_in = w_in.reshape(2 * FH, D)
    u, a = _swiglu_up(h, w_in, f"{tag}_up", after=token)
    w_out, = _push_wait([w_out], False, a, f"{tag}_gather_wait_out")
    w_out = w_out.reshape(FH, D)
    if final is not None:
        return _matmul_residual_loss(a, w_out, x, gate, 0.5, *final, f"{tag}_down_loss"), (h, u, a, None), w_in, w_out
    outs = _matmul_residual(a, w_out, x, gate, 0.5, a_blk=True, norm=next_norm, name=f"{tag}_down",
                            after=None if start_more is None else start_more(h))
    return outs[0], (h, u, a, outs[1]), w_in, w_out, (outs[2] if next_norm else None)


def _ffn_bwd(dx_out, dy, x, g, scale, w_in, w_out, saved, tag, below=None):
    h, u, a, _ = saved
    t = x.shape[0]
    dw_out = _matmul(a, dy, ta=True, a_blk=True, out_dtype=BF16, name=f"{tag}_down_dw")
    sent_out, token = _push_start([dw_out.reshape(NDEV, FH // NDEV, D)], True, f"{tag}_grad_start_out")
    du = _swiglu_down_bwd(dy, w_out, u, f"{tag}_down_dx", after=token).reshape(NDEV, t, FB)
    dw_in = _matmul(du, h, ta=True, a_blk=True, out_dtype=BF16, name=f"{tag}_up_dw")
    sent_in, token = _push_start([dw_in.reshape(NDEV, FB, D)], True, f"{tag}_grad_start_in")
    dh = _matmul(du, w_in, a_blk=True, out_dtype=F32, name=f"{tag}_up_dx", after=token)
    return _norm_mod_bwd(x, g, scale, dh, dx_out, f"{tag}_norm_bwd", below), sent_in + sent_out


def kernel(x, c, ada_w, ada_b, norm_g, ffn1_w_in, ffn1_w_out, ffn2_w_in, ffn2_w_out, mix_w_in, conv_w, a_log, dt_bias, dn_norm_g, pool_w, pool_scale, pool_proj, dn_proj, mix_w_out, final_g, loss_target, m_ada_w, m_ada_b, m_norm_g, m_ffn1_w_in, m_ffn1_w_out, m_ffn2_w_in, m_ffn2_w_out, m_mix_w_in, m_conv_w, m_a_log, m_dt_bias, m_dn_norm_g, m_pool_w, m_pool_scale, m_pool_proj, m_dn_proj, m_mix_w_out, m_final_g, v_ada_w, v_ada_b, v_norm_g, v_ffn1_w_in, v_ffn1_w_out, v_ffn2_w_in, v_ffn2_w_out, v_mix_w_in, v_conv_w, v_a_log, v_dt_bias, v_dn_norm_g, v_pool_w, v_pool_scale, v_pool_proj, v_dn_proj, v_mix_w_out, v_final_g):
    me = 4 * lax.axis_index("x") + 2 * lax.axis_index("y") + lax.axis_index("c")
    x0 = x[0]
    target = loss_target[0]
    t = x0.shape[0]

    big = [ffn1_w_in[0], ffn1_w_out[0], ffn2_w_in[0], ffn2_w_out[0], mix_w_in[0], pool_proj[0], dn_proj[0],
           mix_w_out[0]]
    small = jnp.concatenate([c.reshape(8, 128), conv_w[0].reshape(12, 128), norm_g[0].reshape(3, 128),
                             jnp.zeros((1, 128), F32)], axis=0)
    small_all, = _all_gather([small], "gather_small")
    c_all = small_all[:, 0:8, :].reshape(NDEV, D)
    conv_full = small_all[:, 8:20, :].reshape(NDEV, 4, 384).transpose(1, 0, 2).reshape(4, 3 * D)
    norm_full = small_all[:, 20:23, :].reshape(NDEV, 3, 128).transpose(1, 0, 2).reshape(3, D)

    ncol = ada_w.shape[2]
    ada_b_mine = lax.dynamic_slice(ada_b, (0, me * ncol), (1, ncol))
    mod_cols = _ada_fwd(c_all, ada_w[0], ada_b_mine, "ada_fwd")
    transposed = (0, 2, 4)
    payload = [(w.T if i in transposed else w).astype(BF16) for i, w in enumerate(big)]
    mod_all, w_in1 = _all_gather([mod_cols, payload[0]], "gather_mod_first_weight")
    started, token = _push_start([payload[1], payload[4]], False, "gather_start", after=mod_all, near=(1,))
    started = {1: started[0], 4: started[1]}

    def start_rest(h):
        more, token = _push_start([payload[i] for i in (5, 6, 7, 2, 3)], False, "gather_start_rest", after=h)
        started.update(zip((5, 6, 7, 2, 3), more))
        return token

    mod = lax.dynamic_index_in_dim(mod_all, me, axis=1, keepdims=False).reshape(9, D)
    shift = [mod[3 * s:3 * s + 1] for s in range(3)]
    scale = [mod[3 * s + 1:3 * s + 2] for s in range(3)]
    gate = [mod[3 * s + 2:3 * s + 3] for s in range(3)]
    ng = [norm_full[s:s + 1] for s in range(3)]
    fg = final_g.reshape(1, D)
    al_row = _lane_row(a_log[0])
    dt_row = _lane_row(dt_bias[0])
    gn = dn_norm_g
    pw = pool_w[0]
    ps = pool_scale

    h0 = _norm_mod_fwd(x0, ng[0], shift[0], scale[0], "ffn1_norm", after=token)
    x1, saved1, w_in1, w_out1, h1 = _ffn_fwd(x0, h0, gate[0], w_in1, started[1], "ffn1",
                                             (ng[1], shift[1], scale[1]), token, start_rest)

    seg, = _push_wait([started[4]], False, h1, "mix_gather_wait", near=(0,))
    w_mix = _mix_pad(_relay_to_sibling(seg, "mix_gather_relay").reshape(MIX_RAW, D))
    proj = _matmul(h1, w_mix, tb=True, out_dtype=F32, name="mix_in")
    qh, kh, vh, bg = _dn_pre_fwd(proj, conv_full, al_row, dt_row, "dn_pre")
    seg = _push_wait([started[i] for i in (5, 6, 7)], False, qh, "mix_gather_wait_rest")
    w_pp = _cols_from_blocks(seg[0])
    w_dn = seg[1].reshape(D, D)
    w_mo = seg[2].reshape(D, D)
    ya = _pool_fwd(proj, pw, ps, w_pp, "pool_fwd")
    u, w, qk, qd, kd, eg, inv = _dn_local_fwd(qh, kh, vh, bg, "dn_local")
    o, s_saved = _dn_scan_fwd(u, w, qk, qd, kd, eg, "dn_scan")
    ob, yb = _dn_post_fwd(o, proj, gn, w_dn, "dn_post_out")
    merged, x2, mix_y, h2 = _merge_out(ya, yb, proj, w_mo, x1, gate[1], (ng[2], shift[2], scale[2]), "mix_out")

    (loss_row, dx3, dfg, dy2, dgate2), saved2, w_in2, w_out2 = _ffn_fwd(
        x2, h2, gate[2], started[2], started[3], "ffn2", final=(fg, target))

    (dx2, dsh2, dsc2, dng2, dmy, dgate1), sent2 = _ffn_bwd(dx3, dy2, x2, ng[2], scale[2], w_in2, w_out2, saved2,
                                                           "ffn2", (mix_y, gate[1], 1.0))

    dw_mo = _matmul(merged, dmy, ta=True, out_dtype=BF16, name="mix_out_dw")
    dproj = lax.empty((t, MIXP), BF16)
    dya, dyb, dproj = _merge_bwd(dmy, w_mo, ya, yb, proj, dproj, "merge_bwd")
    dw_dn = _matmul(ob, dyb, ta=True, out_dtype=BF16, name="dn_out_dw")
    do, dproj, dgn = _dn_post_bwd(o, proj, gn, dyb, w_dn, dproj, "dn_post_bwd")
    du, dw, dqk, dqd, dkd, deg = _dn_scan_bwd(u, w, qk, qd, kd, eg, s_saved, do, "dn_scan_bwd")
    dqh, dkh, dvh, dbg = _dn_local_bwd(qh, kh, vh, bg, inv, du, dw, dqk, dqd, dkd, deg, "dn_local_bwd")
    dconv, dproj, dal, ddt = _dn_pre_bwd_act(proj, conv_full, al_row, dt_row, dqh, dkh, dvh, dbg, dproj,
                                             "dn_pre_bwd_act")
    dproj, dcw = _dn_pre_bwd_conv(proj, conv_full, dconv, dproj, "dn_pre_bwd_conv")
    dwin, dpl, dpw, dps, dpp = _pool_bwd_local(proj, pw, ps, w_pp, dya, "pool_bwd_local")
    dproj = _pool_bwd_window(dwin, dpl, dproj, "pool_bwd_window")
    dw_mix = _matmul(dproj, h1, ta=True, out_dtype=BF16, name="mix_in_dw")
    sent1, token = _push_start(
        [_mix_unpad(dw_mix).reshape(NDEV, MIX_RAW // NDEV, D), _cols_to_blocks(dpp.astype(BF16)),
         dw_dn.reshape(NDEV, -1, D), dw_mo.reshape(NDEV, -1, D)], True, "mix_grad_start")
    dh1 = _matmul(dproj, w_mix, out_dtype=F32, name="mix_in_dx", after=token)
    dx1, dsh1, dsc1, dng1, dy0, dgate0 = _norm_mod_bwd(x1, ng[1], scale[1], dh1, dx2, "mix_norm_bwd",
                                                       (saved1[3], gate[0], 0.5))

    (dx0, dsh0, dsc0, dng0), sent0 = _ffn_bwd(dx1, dy0, x0, ng[0], scale[0], w_in1, w_out1, saved1, "ffn1")

    dmod = jnp.concatenate([dsh0, dsc0, dgate0, dsh1, dsc1, dgate1, dsh2, dsc2, dgate2], axis=1).reshape(-1)
    flat = jnp.concatenate([
        dmod, dal[0, NH:2 * NH], ddt[0, NH:2 * NH], dgn.reshape(-1), dps.reshape(-1), dfg.reshape(-1),
        dpw.reshape(-1), jnp.concatenate([dng0, dng1, dng2], axis=0).reshape(-1), dcw.reshape(-1),
        loss_row[0, 0:1]])
    nflat = 90 * D
    flat = jnp.concatenate([flat, jnp.zeros((nflat - flat.shape[0],), F32)]).reshape(90, D)
    sent_small, small_token = _push_start([flat], False, "small_grad_start")

    def small_grads(flat_all):
        tot = _sum_devices(flat_all, F32, "sum_small_grads").reshape(-1)
        dmod_all = flat_all.reshape(NDEV, nflat)[:, :9 * D]
        dmod_cols = lax.dynamic_slice(dmod_all, (0, me * ncol), (NDEV, ncol))
        g_ada_w = _ada_bwd(c_all.T, dmod_cols, "ada_bwd")
        p = 0
        pieces = {}
        for nm, size in (("ada_b", 9 * D), ("a_log", NH), ("dt_bias", NH), ("dn_norm_g", HD), ("pool_scale", PW),
                         ("final_g", D), ("pool_w", 4 * PG * PG), ("norm_g", 3 * D), ("conv_w", 12 * D),
                         ("loss", 1)):
            pieces[nm] = tot[p:p + size]
            p += size
        g_norm = lax.dynamic_slice(pieces["norm_g"].reshape(3, D), (0, me * 128), (3, 128))
        g_conv = lax.dynamic_slice(pieces["conv_w"].reshape(4, 3 * D), (0, me * 384), (4, 384))
        return pieces["loss"][0], {
            "ada_w": g_ada_w.reshape(ada_w.shape), "ada_b": pieces["ada_b"].reshape(ada_b.shape),
            "norm_g": g_norm.reshape(norm_g.shape), "conv_w": g_conv.reshape(conv_w.shape),
            "a_log": pieces["a_log"].reshape(a_log.shape), "dt_bias": pieces["dt_bias"].reshape(dt_bias.shape),
            "dn_norm_g": pieces["dn_norm_g"].reshape(dn_norm_g.shape),
            "pool_w": pieces["pool_w"].reshape(pool_w.shape),
            "pool_scale": pieces["pool_scale"].reshape(pool_scale.shape),
            "final_g": pieces["final_g"].reshape(final_g.shape),
        }

    grads = {}
    weights = {"ada_w": ada_w, "ada_b": ada_b, "norm_g": norm_g, "ffn1_w_in": ffn1_w_in, "ffn1_w_out": ffn1_w_out,
               "ffn2_w_in": ffn2_w_in, "ffn2_w_out": ffn2_w_out, "mix_w_in": mix_w_in, "conv_w": conv_w,
               "a_log": a_log, "dt_bias": dt_bias, "dn_norm_g": dn_norm_g, "pool_w": pool_w,
               "pool_scale": pool_scale, "pool_proj": pool_proj, "dn_proj": dn_proj, "mix_w_out": mix_w_out,
               "final_g": final_g}
    m_in = {"ada_w": m_ada_w, "ada_b": m_ada_b, "norm_g": m_norm_g, "ffn1_w_in": m_ffn1_w_in,
            "ffn1_w_out": m_ffn1_w_out, "ffn2_w_in": m_ffn2_w_in, "ffn2_w_out": m_ffn2_w_out,
            "mix_w_in": m_mix_w_in, "conv_w": m_conv_w, "a_log": m_a_log, "dt_bias": m_dt_bias,
            "dn_norm_g": m_dn_norm_g, "pool_w": m_pool_w, "pool_scale": m_pool_scale, "pool_proj": m_pool_proj,
            "dn_proj": m_dn_proj, "mix_w_out": m_mix_w_out, "final_g": m_final_g}
    v_in = {"ada_w": v_ada_w, "ada_b": v_ada_b, "norm_g": v_norm_g, "ffn1_w_in": v_ffn1_w_in,
            "ffn1_w_out": v_ffn1_w_out, "ffn2_w_in": v_ffn2_w_in, "ffn2_w_out": v_ffn2_w_out,
            "mix_w_in": v_mix_w_in, "conv_w": v_conv_w, "a_log": v_a_log, "dt_bias": v_dt_bias,
            "dn_norm_g": v_dn_norm_g, "pool_w": v_pool_w, "pool_scale": v_pool_scale, "pool_proj": v_pool_proj,
            "dn_proj": v_dn_proj, "mix_w_out": v_mix_w_out, "final_g": v_final_g}

    names = list(weights)
    large = ("ada_w", "ffn1_w_in", "ffn1_w_out", "ffn2_w_in", "ffn2_w_out", "mix_w_in", "pool_proj", "dn_proj",
             "mix_w_out")
    delta, new_m, new_v = {}, {}, {}

    flipped = ("ffn1_w_in", "ffn2_w_in", "mix_w_in")

    def views(nm):
        shp = weights[nm].shape
        two_d = (shp[-2], shp[-1])
        if nm in flipped:
            return (lambda a: a.reshape(two_d).T), (lambda a: a.T.reshape(shp))
        return (lambda a: a.reshape(two_d)), (lambda a: a.reshape(shp))

    def reduce_update(sent, group, after, tag):
        done = []
        for nm, r in zip(group, _push_wait(sent, True, after, f"{tag}_grad_wait")):
            view, back = views(nm)
            g_, d_, m_, v_ = _reduce_adamw(r, view(weights[nm]), view(m_in[nm]), view(v_in[nm]), f"adamw_{nm}")
            grads[nm], delta[nm], new_m[nm], new_v[nm] = back(g_), back(d_), back(m_), back(v_)
            done.append(d_)
        return done

    done = reduce_update(sent2, ("ffn2_w_in", "ffn2_w_out"), small_token, "ffn2")
    done += reduce_update(sent1, ("mix_w_in", "pool_proj", "dn_proj", "mix_w_out"), done, "mix")
    flat_all, = _push_wait(sent_small, False, done, "small_grad_wait")
    loss, small = small_grads(flat_all)
    grads.update(small)
    view, back = views("ada_w")
    done, m_, v_ = _adamw(view(ada_w), view(grads["ada_w"]), view(m_ada_w), view(v_ada_w), "adamw_ada_w")
    delta["ada_w"], new_m["ada_w"], new_v["ada_w"] = back(done), back(m_), back(v_)
    reduce_update(sent0, ("ffn1_w_in", "ffn1_w_out"), done, "ffn1")
    rest = [nm for nm in names if nm not in large]
    total = sum(weights[nm].size for nm in rest)
    padded = -(-total // D) * D

    def pack(tree, fill):
        flat_ = jnp.concatenate([tree[nm].reshape(-1) for nm in rest])
        return jnp.concatenate([flat_, jnp.full((padded - total,), fill, F32)]).reshape(-1, D)

    d_, m_, v_ = _adamw(pack(weights, 0.0), pack(grads, 0.0), pack(m_in, 0.0), pack(v_in, 1.0), "adamw_small")
    p = 0
    for nm in rest:
        size = weights[nm].size
        shp = weights[nm].shape
        delta[nm] = d_.reshape(-1)[p:p + size].reshape(shp)
        new_m[nm] = m_.reshape(-1)[p:p + size].reshape(shp)
        new_v[nm] = v_.reshape(-1)[p:p + size].reshape(shp)
        p += size

    grad_x = dx0.reshape(x.shape)
    return (loss, grad_x, *[grads[nm] for nm in names], *[delta[nm] for nm in names],
            *[new_m[nm] for nm in names], *[new_v[nm] for nm in names])
```
